```python
import jax, jax.numpy as jnp
from jax import lax
import numpy as np

D_MODEL = 2048
BATCH = 8
SEQ = 2048
DEPTH = 1

CTX_LEN = 256
GRID_W = 64
Q_BLOCK = 128
ROPE_THETA = 10000.0
NORM_EPS = 1e-6

MLA_HEADS = 8
MLA_Q_LORA = 768
MLA_KV_LORA = 512
MLA_NOPE = 128
MLA_ROPE = 64
MLA_V = 128
GQA_HEADS = 8
GQA_KV_HEADS = 2
GQA_HEAD_DIM = 128
D_FF = 5632
CONV_W = 3
N_BRANCH = 2

KV_COLS = MLA_KV_LORA + MLA_ROPE + 2 * GQA_KV_HEADS * GQA_HEAD_DIM
Q_COLS = MLA_Q_LORA + GQA_HEADS * GQA_HEAD_DIM
GATE_COLS = N_BRANCH * D_MODEL
IN_COLS = KV_COLS + Q_COLS + GATE_COLS
KV_SPLITS = [MLA_KV_LORA, MLA_KV_LORA + MLA_ROPE, MLA_KV_LORA + MLA_ROPE + GQA_KV_HEADS * GQA_HEAD_DIM]

kernel_name = "hybrid_mla_gqa_convffn_dit_prefix"


def rms_norm(x, g):
    xf = x.astype(jnp.float32)
    y = xf * lax.rsqrt(jnp.mean(xf * xf, axis=-1, keepdims=True) + NORM_EPS)
    return (y * g.astype(jnp.float32)).astype(x.dtype)


def modulate(h, shift, scale):
    return h * (1 + scale) + shift


def ada_terms(cond, w_ada, b_ada):
    return jnp.split(jax.nn.silu(cond) @ w_ada + b_ada, 6, axis=-1)


def grid_rope_tables(n_rows, rot_dim):
    row = jnp.repeat(jnp.arange(n_rows, dtype=jnp.float32), GRID_W)
    col = jnp.tile(jnp.arange(GRID_W, dtype=jnp.float32), n_rows)
    half = rot_dim // 2
    inv_freq = ROPE_THETA ** (-jnp.arange(0, half, 2, dtype=jnp.float32) / half)
    ang = jnp.concatenate([row[:, None] * inv_freq, col[:, None] * inv_freq], axis=-1)
    return jnp.cos(ang), jnp.sin(ang)


def apply_grid_rope(x, cos, sin):
    b, t, h, r = x.shape
    q = r // 4
    xs = x.reshape(b, t, h, 2, 2, q)
    x1, x2 = xs[..., 0, :], xs[..., 1, :]
    c = cos.reshape(t, 1, 2, q).astype(x.dtype)
    s = sin.reshape(t, 1, 2, q).astype(x.dtype)
    out = jnp.stack([x1 * c - x2 * s, x1 * s + x2 * c], axis=-2)
    return out.reshape(b, t, h, r)


def block_attention(q, k, v):
    b, tq, hk, g, dk = q.shape
    dv = v.shape[-1]
    scale = dk ** -0.5
    kf = k.astype(jnp.float32)
    qb = jnp.moveaxis(q.reshape(b, tq // Q_BLOCK, Q_BLOCK, hk, g, dk), 1, 0)

    def one_block(q_blk):
        s = jnp.einsum("bqhgd,bkhd->bhgqk", q_blk.astype(jnp.float32), kf) * scale
        p = jax.nn.softmax(s, axis=-1)
        return jnp.einsum("bhgqk,bkhd->bqhgd", p.astype(v.dtype), v)

    o = lax.map(one_block, qb)
    return jnp.moveaxis(o, 0, 1).reshape(b, tq, hk * g * dv)


def mixer_keys(kv, p, rope):
    b, t, _ = kv.shape
    c_kv, k_pe, k_b, v_b = jnp.split(kv, KV_SPLITS, axis=-1)
    kv_up = (rms_norm(c_kv, p["mla_kv_norm_g"]) @ p["w_kv_up"]).reshape(b, t, MLA_HEADS, MLA_NOPE + MLA_V)
    k_nope, v_a = jnp.split(kv_up, [MLA_NOPE], axis=-1)
    k_pe = k_pe.reshape(b, t, 1, MLA_ROPE)
    k_b = rms_norm(k_b.reshape(b, t, GQA_KV_HEADS, GQA_HEAD_DIM), p["gqa_k_norm_g"])
    v_b = v_b.reshape(b, t, GQA_KV_HEADS, GQA_HEAD_DIM)
    if rope is not None:
        cos_a, sin_a, cos_b, sin_b = rope
        k_pe = apply_grid_rope(k_pe, cos_a, sin_a)
        k_b = apply_grid_rope(k_b, cos_b, sin_b)
    k_a = jnp.concatenate([k_nope, jnp.broadcast_to(k_pe, (b, t, MLA_HEADS, MLA_ROPE))], axis=-1)
    return (k_a, v_a, k_b, v_b)


def mixer_queries(qp, p, rope):
    b, t, _ = qp.shape
    c_q, q_b = jnp.split(qp, [MLA_Q_LORA], axis=-1)
    q_a = (rms_norm(c_q, p["mla_q_norm_g"]) @ p["w_q_up"]).reshape(b, t, MLA_HEADS, MLA_NOPE + MLA_ROPE)
    q_nope, q_pe = jnp.split(q_a, [MLA_NOPE], axis=-1)
    q_b = rms_norm(q_b.reshape(b, t, GQA_HEADS, GQA_HEAD_DIM), p["gqa_q_norm_g"])
    if rope is not None:
        cos_a, sin_a, cos_b, sin_b = rope
        q_pe = apply_grid_rope(q_pe, cos_a, sin_a)
        q_b = apply_grid_rope(q_b, cos_b, sin_b)
    q_a = jnp.concatenate([q_nope, q_pe], axis=-1)[:, :, :, None, :]
    q_b = q_b.reshape(b, t, GQA_KV_HEADS, GQA_HEADS // GQA_KV_HEADS, GQA_HEAD_DIM)
    return q_a, q_b


def attend_and_merge(proj, keys, p, rope):
    q_a, q_b = mixer_queries(proj[..., KV_COLS:KV_COLS + Q_COLS], p, rope)
    g_a, g_b = jnp.split(jax.nn.sigmoid(proj[..., KV_COLS + Q_COLS:]), N_BRANCH, axis=-1)
    k_a, v_a, k_b, v_b = keys
    o_a = block_attention(q_a, k_a, v_a)
    o_b = block_attention(q_b, k_b, v_b)
    merged = g_a * (o_a @ p["w_br_a"]) + g_b * (o_b @ p["w_br_b"])
    return merged @ p["w_out"]


def conv_ffn(z, p):
    t = z.shape[1]
    u = z @ p["w_up"]
    pad = CONV_W // 2
    up = jnp.pad(u, ((0, 0), (pad, pad), (0, 0)))
    uc = p["conv_b"] + sum(p["conv_w"][j] * up[:, j:j + t] for j in range(CONV_W))
    a, bb = jnp.split(uc, 2, axis=-1)
    return (jax.nn.silu(a) * bb) @ p["w_down"]


def _fwd_setup_inputs(seed: int = 0) -> dict:
    key = jax.random.key(seed)
    ks = jax.random.split(key, 24)
    f32 = jnp.float32

    def nrm(k, shape, scale):
        return jax.random.normal(k, shape, f32) * scale

    def gain(k, shape):
        return 1.0 + 0.01 * jax.random.normal(k, shape, f32)

    L, D = DEPTH, D_MODEL
    return {
        "x": nrm(ks[0], (BATCH, SEQ, D), 1.0),
        "c": nrm(ks[1], (BATCH, D), 1.0),
        "ctx": nrm(ks[2], (BATCH, CTX_LEN, D), 1.0),
        "c_ctx": nrm(ks[3], (D,), 0.5),
        "w_ada": nrm(ks[4], (L, D, 6 * D), 0.5 * D ** -0.5),
        "b_ada": nrm(ks[5], (L, 6 * D), 0.01),
        "norm1_g": gain(ks[6], (L, D)),
        "w_in": nrm(ks[7], (L, D, IN_COLS), D ** -0.5),
        "mla_q_norm_g": gain(ks[8], (L, MLA_Q_LORA)),
        "w_q_up": nrm(ks[9], (L, MLA_Q_LORA, MLA_HEADS * (MLA_NOPE + MLA_ROPE)), MLA_Q_LORA ** -0.5),
        "mla_kv_norm_g": gain(ks[10], (L, MLA_KV_LORA)),
        "w_kv_up": nrm(ks[11], (L, MLA_KV_LORA, MLA_HEADS * (MLA_NOPE + MLA_V)), MLA_KV_LORA ** -0.5),
        "gqa_q_norm_g": gain(ks[12], (L, GQA_HEAD_DIM)),
        "gqa_k_norm_g": gain(ks[13], (L, GQA_HEAD_DIM)),
        "w_br_a": nrm(ks[14], (L, MLA_HEADS * MLA_V, D), (MLA_HEADS * MLA_V) ** -0.5),
        "w_br_b": nrm(ks[15], (L, GQA_HEADS * GQA_HEAD_DIM, D), (GQA_HEADS * GQA_HEAD_DIM) ** -0.5),
        "w_out": nrm(ks[16], (L, D, D), D ** -0.5),
        "norm2_g": gain(ks[17], (L, D)),
        "w_up": nrm(ks[18], (L, D, 2 * D_FF), D ** -0.5),
        "conv_w": nrm(ks[19], (L, CONV_W, 2 * D_FF), CONV_W ** -0.5),
        "conv_b": nrm(ks[20], (L, 2 * D_FF), 0.01),
        "w_down": nrm(ks[21], (L, D_FF, D), D_FF ** -0.5),
        "final_norm_g": gain(ks[22], (D,)),
    }


def _fwd_reference(x, c, ctx, c_ctx, w_ada, b_ada, norm1_g, w_in, mla_q_norm_g, w_q_up, mla_kv_norm_g,
              w_kv_up, gqa_q_norm_g, gqa_k_norm_g, w_br_a, w_br_b, w_out, norm2_g, w_up, conv_w,
              conv_b, w_down, final_norm_g):
    n_lat = x.shape[1]
    ROWS = n_lat // GRID_W
    rope = (*grid_rope_tables(ROWS, MLA_ROPE), *grid_rope_tables(ROWS, GQA_HEAD_DIM))
    cond_lat = c[:, None, :]
    cond_ctx = c_ctx[None, None, :]

    for l in range(DEPTH):
        p = {
            "w_in": w_in[l], "mla_q_norm_g": mla_q_norm_g[l], "w_q_up": w_q_up[l],
            "mla_kv_norm_g": mla_kv_norm_g[l], "w_kv_up": w_kv_up[l],
            "gqa_q_norm_g": gqa_q_norm_g[l], "gqa_k_norm_g": gqa_k_norm_g[l],
            "w_br_a": w_br_a[l], "w_br_b": w_br_b[l], "w_out": w_out[l],
            "w_up": w_up[l], "conv_w": conv_w[l], "conv_b": conv_b[l], "w_down": w_down[l],
        }
        last = l == DEPTH - 1
        sh1, sc1, g1, sh2, sc2, g2 = ada_terms(cond_lat, w_ada[l], b_ada[l])
        ctx_terms = ada_terms(cond_ctx, w_ada[l], b_ada[l])

        z_ctx = modulate(rms_norm(ctx, norm1_g[l]), ctx_terms[0], ctx_terms[1])
        ctx_proj = z_ctx @ (p["w_in"][:, :KV_COLS] if last else p["w_in"])
        ctx_keys = mixer_keys(ctx_proj[..., :KV_COLS], p, None)

        z_lat = modulate(rms_norm(x, norm1_g[l]), sh1, sc1)
        lat_proj = z_lat @ p["w_in"]
        lat_keys = mixer_keys(lat_proj[..., :KV_COLS], p, rope)
        keys = tuple(jnp.concatenate([ck, lk], axis=1) for ck, lk in zip(ctx_keys, lat_keys))
        x = x + g1 * attend_and_merge(lat_proj, keys, p, rope)
        x = x + g2 * conv_ffn(modulate(rms_norm(x, norm2_g[l]), sh2, sc2), p)

        if not last:
            ctx = ctx + ctx_terms[2] * attend_and_merge(ctx_proj, ctx_keys, p, None)
            z2 = modulate(rms_norm(ctx, norm2_g[l]), ctx_terms[3], ctx_terms[4])
            ctx = ctx + ctx_terms[5] * conv_ffn(z2, p)

    return rms_norm(x, final_norm_g)


import jax as _jax
import jax.numpy as _jnp

TWIN_FORMAT = 'train_step'
FWD_PARAMS = ['x', 'c', 'ctx', 'c_ctx', 'w_ada', 'b_ada', 'norm1_g', 'w_in', 'mla_q_norm_g', 'w_q_up', 'mla_kv_norm_g', 'w_kv_up', 'gqa_q_norm_g', 'gqa_k_norm_g', 'w_br_a', 'w_br_b', 'w_out', 'norm2_g', 'w_up', 'conv_w', 'conv_b', 'w_down', 'final_norm_g']
TWIN_WEIGHTS = ['c_ctx', 'w_ada', 'b_ada', 'norm1_g', 'w_in', 'mla_q_norm_g', 'w_q_up', 'mla_kv_norm_g', 'w_kv_up', 'gqa_q_norm_g', 'gqa_k_norm_g', 'w_br_a', 'w_br_b', 'w_out', 'norm2_g', 'w_up', 'conv_w', 'conv_b', 'w_down', 'final_norm_g']
TWIN_DIFF_INPUT = 'x'
TWIN_INPUTS = ['x', 'c', 'ctx', 'c_ctx', 'w_ada', 'b_ada', 'norm1_g', 'w_in', 'mla_q_norm_g', 'w_q_up', 'mla_kv_norm_g', 'w_kv_up', 'gqa_q_norm_g', 'gqa_k_norm_g', 'w_br_a', 'w_br_b', 'w_out', 'norm2_g', 'w_up', 'conv_w', 'conv_b', 'w_down', 'final_norm_g', 'loss_target', 'm_c_ctx', 'm_w_ada', 'm_b_ada', 'm_norm1_g', 'm_w_in', 'm_mla_q_norm_g', 'm_w_q_up', 'm_mla_kv_norm_g', 'm_w_kv_up', 'm_gqa_q_norm_g', 'm_gqa_k_norm_g', 'm_w_br_a', 'm_w_br_b', 'm_w_out', 'm_norm2_g', 'm_w_up', 'm_conv_w', 'm_conv_b', 'm_w_down', 'm_final_norm_g', 'v_c_ctx', 'v_w_ada', 'v_b_ada', 'v_norm1_g', 'v_w_in', 'v_mla_q_norm_g', 'v_w_q_up', 'v_mla_kv_norm_g', 'v_w_kv_up', 'v_gqa_q_norm_g', 'v_gqa_k_norm_g', 'v_w_br_a', 'v_w_br_b', 'v_w_out', 'v_norm2_g', 'v_w_up', 'v_conv_w', 'v_conv_b', 'v_w_down', 'v_final_norm_g']
TWIN_OUTPUTS = ['loss', 'grad_x', 'grad_c_ctx', 'grad_w_ada', 'grad_b_ada', 'grad_norm1_g', 'grad_w_in', 'grad_mla_q_norm_g', 'grad_w_q_up', 'grad_mla_kv_norm_g', 'grad_w_kv_up', 'grad_gqa_q_norm_g', 'grad_gqa_k_norm_g', 'grad_w_br_a', 'grad_w_br_b', 'grad_w_out', 'grad_norm2_g', 'grad_w_up', 'grad_conv_w', 'grad_conv_b', 'grad_w_down', 'grad_final_norm_g', 'delta_c_ctx', 'delta_w_ada', 'delta_b_ada', 'delta_norm1_g', 'delta_w_in', 'delta_mla_q_norm_g', 'delta_w_q_up', 'delta_mla_kv_norm_g', 'delta_w_kv_up', 'delta_gqa_q_norm_g', 'delta_gqa_k_norm_g', 'delta_w_br_a', 'delta_w_br_b', 'delta_w_out', 'delta_norm2_g', 'delta_w_up', 'delta_conv_w', 'delta_conv_b', 'delta_w_down', 'delta_final_norm_g', 'new_m_c_ctx', 'new_m_w_ada', 'new_m_b_ada', 'new_m_norm1_g', 'new_m_w_in', 'new_m_mla_q_norm_g', 'new_m_w_q_up', 'new_m_mla_kv_norm_g', 'new_m_w_kv_up', 'new_m_gqa_q_norm_g', 'new_m_gqa_k_norm_g', 'new_m_w_br_a', 'new_m_w_br_b', 'new_m_w_out', 'new_m_norm2_g', 'new_m_w_up', 'new_m_conv_w', 'new_m_conv_b', 'new_m_w_down', 'new_m_final_norm_g', 'new_v_c_ctx', 'new_v_w_ada', 'new_v_b_ada', 'new_v_norm1_g', 'new_v_w_in', 'new_v_mla_q_norm_g', 'new_v_w_q_up', 'new_v_mla_kv_norm_g', 'new_v_w_kv_up', 'new_v_gqa_q_norm_g', 'new_v_gqa_k_norm_g', 'new_v_w_br_a', 'new_v_w_br_b', 'new_v_w_out', 'new_v_norm2_g', 'new_v_w_up', 'new_v_conv_w', 'new_v_conv_b', 'new_v_w_down', 'new_v_final_norm_g']
TWIN_LEAF_KINDS = {'loss': 'loss', 'grad_x': 'grad_x', 'grad_c_ctx': 'grad_w', 'grad_w_ada': 'grad_w', 'grad_b_ada': 'grad_w', 'grad_norm1_g': 'grad_w', 'grad_w_in': 'grad_w', 'grad_mla_q_norm_g': 'grad_w', 'grad_w_q_up': 'grad_w', 'grad_mla_kv_norm_g': 'grad_w', 'grad_w_kv_up': 'grad_w', 'grad_gqa_q_norm_g': 'grad_w', 'grad_gqa_k_norm_g': 'grad_w', 'grad_w_br_a': 'grad_w', 'grad_w_br_b': 'grad_w', 'grad_w_out': 'grad_w', 'grad_norm2_g': 'grad_w', 'grad_w_up': 'grad_w', 'grad_conv_w': 'grad_w', 'grad_conv_b': 'grad_w', 'grad_w_down': 'grad_w', 'grad_final_norm_g': 'grad_w', 'delta_c_ctx': 'delta_w', 'delta_w_ada': 'delta_w', 'delta_b_ada': 'delta_w', 'delta_norm1_g': 'delta_w', 'delta_w_in': 'delta_w', 'delta_mla_q_norm_g': 'delta_w', 'delta_w_q_up': 'delta_w', 'delta_mla_kv_norm_g': 'delta_w', 'delta_w_kv_up': 'delta_w', 'delta_gqa_q_norm_g': 'delta_w', 'delta_gqa_k_norm_g': 'delta_w', 'delta_w_br_a': 'delta_w', 'delta_w_br_b': 'delta_w', 'delta_w_out': 'delta_w', 'delta_norm2_g': 'delta_w', 'delta_w_up': 'delta_w', 'delta_conv_w': 'delta_w', 'delta_conv_b': 'delta_w', 'delta_w_down': 'delta_w', 'delta_final_norm_g': 'delta_w', 'new_m_c_ctx': 'new_m', 'new_m_w_ada': 'new_m', 'new_m_b_ada': 'new_m', 'new_m_norm1_g': 'new_m', 'new_m_w_in': 'new_m', 'new_m_mla_q_norm_g': 'new_m', 'new_m_w_q_up': 'new_m', 'new_m_mla_kv_norm_g': 'new_m', 'new_m_w_kv_up': 'new_m', 'new_m_gqa_q_norm_g': 'new_m', 'new_m_gqa_k_norm_g': 'new_m', 'new_m_w_br_a': 'new_m', 'new_m_w_br_b': 'new_m', 'new_m_w_out': 'new_m', 'new_m_norm2_g': 'new_m', 'new_m_w_up': 'new_m', 'new_m_conv_w': 'new_m', 'new_m_conv_b': 'new_m', 'new_m_w_down': 'new_m', 'new_m_final_norm_g': 'new_m', 'new_v_c_ctx': 'new_v', 'new_v_w_ada': 'new_v', 'new_v_b_ada': 'new_v', 'new_v_norm1_g': 'new_v', 'new_v_w_in': 'new_v', 'new_v_mla_q_norm_g': 'new_v', 'new_v_w_q_up': 'new_v', 'new_v_mla_kv_norm_g': 'new_v', 'new_v_w_kv_up': 'new_v', 'new_v_gqa_q_norm_g': 'new_v', 'new_v_gqa_k_norm_g': 'new_v', 'new_v_w_br_a': 'new_v', 'new_v_w_br_b': 'new_v', 'new_v_w_out': 'new_v', 'new_v_norm2_g': 'new_v', 'new_v_w_up': 'new_v', 'new_v_conv_w': 'new_v', 'new_v_conv_b': 'new_v', 'new_v_w_down': 'new_v', 'new_v_final_norm_g': 'new_v'}


def _forward(args):
    return _fwd_reference(*[args[k] for k in FWD_PARAMS])


def _output_shape():
    out = _jax.eval_shape(lambda: _forward(_fwd_setup_inputs(0)))
    return out.shape, out.dtype

N_MICROBATCH = 1
ADAM_LR = 0.001
ADAM_B1 = 0.9
ADAM_B2 = 0.999
ADAM_EPS = 1e-08
ADAM_WD = 0.01
ADAM_STEP = 10
PER_EXAMPLE_BATCH_AXIS = {'x': 0, 'c': 0, 'ctx': 0, 'loss_target': 0}
SHARED_INPUTS = []
_WEIGHT_DTYPES = {'c_ctx': _jnp.float32, 'w_ada': _jnp.float32, 'b_ada': _jnp.float32, 'norm1_g': _jnp.float32, 'w_in': _jnp.float32, 'mla_q_norm_g': _jnp.float32, 'w_q_up': _jnp.float32, 'mla_kv_norm_g': _jnp.float32, 'w_kv_up': _jnp.float32, 'gqa_q_norm_g': _jnp.float32, 'gqa_k_norm_g': _jnp.float32, 'w_br_a': _jnp.float32, 'w_br_b': _jnp.float32, 'w_out': _jnp.float32, 'norm2_g': _jnp.float32, 'w_up': _jnp.float32, 'conv_w': _jnp.float32, 'conv_b': _jnp.float32, 'w_down': _jnp.float32, 'final_norm_g': _jnp.float32}
MOMENT_SCALE = {'c_ctx': 2.177741e-03, 'w_ada': 1.369611e-02, 'b_ada': 2.289797e-02, 'norm1_g': 2.238279e-03, 'w_in': 2.696996e-03, 'mla_q_norm_g': 1.466088e-03, 'w_q_up': 1.038700e-03, 'mla_kv_norm_g': 7.298588e-03, 'w_kv_up': 3.170317e-03, 'gqa_q_norm_g': 3.438935e-03, 'gqa_k_norm_g': 3.414433e-03, 'w_br_a': 3.236398e-03, 'w_br_b': 3.403276e-03, 'w_out': 4.611253e-03, 'norm2_g': 1.868102e-02, 'w_up': 8.240908e-03, 'conv_w': 8.293807e-03, 'conv_b': 7.514312e-03, 'w_down': 1.344172e-02, 'final_norm_g': 7.997542e+00}


def _to_microbatches(a, axis):
    t = _jnp.moveaxis(a, axis, 0)
    t = t.reshape((N_MICROBATCH, t.shape[0] // N_MICROBATCH) + t.shape[1:])
    return _jnp.moveaxis(t, 1, axis + 1)


def setup_inputs(seed: int = 0) -> dict:
    inp = _fwd_setup_inputs(seed)
    key = _jax.random.fold_in(_jax.random.key(seed), 7919)
    shape, _ = _output_shape()
    out = dict(inp)
    out["loss_target"] = _jax.random.normal(_jax.random.fold_in(key, 0), shape, _jnp.float32)
    for i, name in enumerate(TWIN_WEIGHTS):
        w = inp[name].astype(_jnp.float32)
        if MOMENT_SCALE is None:
            s = _jnp.sqrt(_jnp.mean(_jnp.square(w)) + 1e-30)
        else:
            s = MOMENT_SCALE[name]
        km, kv = _jax.random.split(_jax.random.fold_in(key, i + 1))
        out[name] = w
        out["m_" + name] = s * _jax.random.normal(km, w.shape, _jnp.float32)
        out["v_" + name] = (s * s) * _jax.random.uniform(kv, w.shape, _jnp.float32, 0.5, 1.5)
    if N_MICROBATCH > 1:
        for name, axis in PER_EXAMPLE_BATCH_AXIS.items():
            out[name] = _to_microbatches(out[name], axis)
    return {'x': out['x'], 'c': out['c'], 'ctx': out['ctx'], 'c_ctx': out['c_ctx'], 'w_ada': out['w_ada'], 'b_ada': out['b_ada'], 'norm1_g': out['norm1_g'], 'w_in': out['w_in'], 'mla_q_norm_g': out['mla_q_norm_g'], 'w_q_up': out['w_q_up'], 'mla_kv_norm_g': out['mla_kv_norm_g'], 'w_kv_up': out['w_kv_up'], 'gqa_q_norm_g': out['gqa_q_norm_g'], 'gqa_k_norm_g': out['gqa_k_norm_g'], 'w_br_a': out['w_br_a'], 'w_br_b': out['w_br_b'], 'w_out': out['w_out'], 'norm2_g': out['norm2_g'], 'w_up': out['w_up'], 'conv_w': out['conv_w'], 'conv_b': out['conv_b'], 'w_down': out['w_down'], 'final_norm_g': out['final_norm_g'], 'loss_target': out['loss_target'], 'm_c_ctx': out['m_c_ctx'], 'm_w_ada': out['m_w_ada'], 'm_b_ada': out['m_b_ada'], 'm_norm1_g': out['m_norm1_g'], 'm_w_in': out['m_w_in'], 'm_mla_q_norm_g': out['m_mla_q_norm_g'], 'm_w_q_up': out['m_w_q_up'], 'm_mla_kv_norm_g': out['m_mla_kv_norm_g'], 'm_w_kv_up': out['m_w_kv_up'], 'm_gqa_q_norm_g': out['m_gqa_q_norm_g'], 'm_gqa_k_norm_g': out['m_gqa_k_norm_g'], 'm_w_br_a': out['m_w_br_a'], 'm_w_br_b': out['m_w_br_b'], 'm_w_out': out['m_w_out'], 'm_norm2_g': out['m_norm2_g'], 'm_w_up': out['m_w_up'], 'm_conv_w': out['m_conv_w'], 'm_conv_b': out['m_conv_b'], 'm_w_down': out['m_w_down'], 'm_final_norm_g': out['m_final_norm_g'], 'v_c_ctx': out['v_c_ctx'], 'v_w_ada': out['v_w_ada'], 'v_b_ada': out['v_b_ada'], 'v_norm1_g': out['v_norm1_g'], 'v_w_in': out['v_w_in'], 'v_mla_q_norm_g': out['v_mla_q_norm_g'], 'v_w_q_up': out['v_w_q_up'], 'v_mla_kv_norm_g': out['v_mla_kv_norm_g'], 'v_w_kv_up': out['v_w_kv_up'], 'v_gqa_q_norm_g': out['v_gqa_q_norm_g'], 'v_gqa_k_norm_g': out['v_gqa_k_norm_g'], 'v_w_br_a': out['v_w_br_a'], 'v_w_br_b': out['v_w_br_b'], 'v_w_out': out['v_w_out'], 'v_norm2_g': out['v_norm2_g'], 'v_w_up': out['v_w_up'], 'v_conv_w': out['v_conv_w'], 'v_conv_b': out['v_conv_b'], 'v_w_down': out['v_w_down'], 'v_final_norm_g': out['v_final_norm_g']}


def _loss(weights, diff, rest, loss_target):
    with _jax.named_scope("forward"):
        args = {**rest, TWIN_DIFF_INPUT: diff, **{k: w.astype(_WEIGHT_DTYPES[k]) for k, w in weights.items()}}
        y = _forward(args)
    with _jax.named_scope("loss_head"):
        err = _jnp.square(y.astype(_jnp.float32) - loss_target)
        return 0.5 * _jnp.sum(_jnp.mean(err, axis=-1)) if err.ndim else 0.5 * err


def _adamw(w, g, m, v):
    m = ADAM_B1 * m + (1.0 - ADAM_B1) * g
    v = ADAM_B2 * v + (1.0 - ADAM_B2) * _jnp.square(g)
    m_hat = m / (1.0 - ADAM_B1 ** ADAM_STEP)
    v_hat = v / (1.0 - ADAM_B2 ** ADAM_STEP)
    delta = -ADAM_LR * (m_hat / (_jnp.sqrt(v_hat) + ADAM_EPS) + ADAM_WD * w)
    return delta, m, v


def reference(x, c, ctx, c_ctx, w_ada, b_ada, norm1_g, w_in, mla_q_norm_g, w_q_up, mla_kv_norm_g, w_kv_up, gqa_q_norm_g, gqa_k_norm_g, w_br_a, w_br_b, w_out, norm2_g, w_up, conv_w, conv_b, w_down, final_norm_g, loss_target, m_c_ctx, m_w_ada, m_b_ada, m_norm1_g, m_w_in, m_mla_q_norm_g, m_w_q_up, m_mla_kv_norm_g, m_w_kv_up, m_gqa_q_norm_g, m_gqa_k_norm_g, m_w_br_a, m_w_br_b, m_w_out, m_norm2_g, m_w_up, m_conv_w, m_conv_b, m_w_down, m_final_norm_g, v_c_ctx, v_w_ada, v_b_ada, v_norm1_g, v_w_in, v_mla_q_norm_g, v_w_q_up, v_mla_kv_norm_g, v_w_kv_up, v_gqa_q_norm_g, v_gqa_k_norm_g, v_w_br_a, v_w_br_b, v_w_out, v_norm2_g, v_w_up, v_conv_w, v_conv_b, v_w_down, v_final_norm_g):
    given = dict(x=x, c=c, ctx=ctx, c_ctx=c_ctx, w_ada=w_ada, b_ada=b_ada, norm1_g=norm1_g, w_in=w_in, mla_q_norm_g=mla_q_norm_g, w_q_up=w_q_up, mla_kv_norm_g=mla_kv_norm_g, w_kv_up=w_kv_up, gqa_q_norm_g=gqa_q_norm_g, gqa_k_norm_g=gqa_k_norm_g, w_br_a=w_br_a, w_br_b=w_br_b, w_out=w_out, norm2_g=norm2_g, w_up=w_up, conv_w=conv_w, conv_b=conv_b, w_down=w_down, final_norm_g=final_norm_g, loss_target=loss_target, m_c_ctx=m_c_ctx, m_w_ada=m_w_ada, m_b_ada=m_b_ada, m_norm1_g=m_norm1_g, m_w_in=m_w_in, m_mla_q_norm_g=m_mla_q_norm_g, m_w_q_up=m_w_q_up, m_mla_kv_norm_g=m_mla_kv_norm_g, m_w_kv_up=m_w_kv_up, m_gqa_q_norm_g=m_gqa_q_norm_g, m_gqa_k_norm_g=m_gqa_k_norm_g, m_w_br_a=m_w_br_a, m_w_br_b=m_w_br_b, m_w_out=m_w_out, m_norm2_g=m_norm2_g, m_w_up=m_w_up, m_conv_w=m_conv_w, m_conv_b=m_conv_b, m_w_down=m_w_down, m_final_norm_g=m_final_norm_g, v_c_ctx=v_c_ctx, v_w_ada=v_w_ada, v_b_ada=v_b_ada, v_norm1_g=v_norm1_g, v_w_in=v_w_in, v_mla_q_norm_g=v_mla_q_norm_g, v_w_q_up=v_w_q_up, v_mla_kv_norm_g=v_mla_kv_norm_g, v_w_kv_up=v_w_kv_up, v_gqa_q_norm_g=v_gqa_q_norm_g, v_gqa_k_norm_g=v_gqa_k_norm_g, v_w_br_a=v_w_br_a, v_w_br_b=v_w_br_b, v_w_out=v_w_out, v_norm2_g=v_norm2_g, v_w_up=v_w_up, v_conv_w=v_conv_w, v_conv_b=v_conv_b, v_w_down=v_w_down, v_final_norm_g=v_final_norm_g)
    weights = {n: given[n] for n in TWIN_WEIGHTS}
    shared = {n: given[n] for n in SHARED_INPUTS}
    per_example = {n: given[n] for n in ['x', 'c', 'ctx']}
    grad_fn = _jax.value_and_grad(_loss, argnums=(0, 1))

    def one_microbatch(ex, loss_target):
        ex = dict(ex)
        diff = ex.pop(TWIN_DIFF_INPUT)
        return grad_fn(weights, diff, {**shared, **ex}, loss_target)

    if N_MICROBATCH == 1:
        loss, (grad_w, grad_x) = one_microbatch(per_example, given["loss_target"])
    else:
        def body(carry, xs):
            loss_sum, grad_sum = carry
            l_k, (gw_k, gx_k) = one_microbatch(xs[0], xs[1])
            with _jax.named_scope("update"):
                return (loss_sum + l_k, _jax.tree.map(_jnp.add, grad_sum, gw_k)), gx_k

        init = (_jnp.zeros((), _jnp.float32), _jax.tree.map(_jnp.zeros_like, weights))
        (loss, grad_w), grad_x = _jax.lax.scan(body, init, (per_example, given["loss_target"]))
    with _jax.named_scope("update"):
        delta_w, new_m, new_v = {}, {}, {}
        for n in TWIN_WEIGHTS:
            delta_w[n], new_m[n], new_v[n] = _adamw(weights[n], grad_w[n], given["m_" + n], given["v_" + n])
    return (loss, grad_x, *[grad_w[n] for n in TWIN_WEIGHTS], *[delta_w[n] for n in TWIN_WEIGHTS],
            *[new_m[n] for n in TWIN_WEIGHTS], *[new_v[n] for n in TWIN_WEIGHTS])
```

```python
import functools

import jax
import jax.numpy as jnp
from jax import lax
from jax.experimental import pallas as pl
from jax.experimental.pallas import tpu as pltpu

F32 = jnp.float32
BF16 = jnp.bfloat16

GRID_W = 64
ROPE_THETA = 10000.0
NORM_EPS = 1e-6
MLA_HEADS = 8
MLA_Q_LORA = 768
MLA_KV_LORA = 512
MLA_NOPE = 128
MLA_ROPE = 64
MLA_V = 128
GQA_HEADS = 8
GQA_KV_HEADS = 2
GQA_HEAD_DIM = 128
ADAM_LR = 0.001
ADAM_B1 = 0.9
ADAM_B2 = 0.999
ADAM_EPS = 1e-08
ADAM_WD = 0.01
ADAM_STEP = 10

N_DEV = 8
MESH_AXES = ("x", "y", "c")
LANE = 128
MLA_SLOT = 2 * LANE
VMEM_LIMIT = 56 * 1024 * 1024
ROW_BLOCK = 256
ATT_Q_BLOCK = 256
MESH_ID = pl.DeviceIdType.MESH


def _tile(n, pref, align=LANE):
    if n <= pref:
        return n
    best = None
    t = align
    while t <= pref:
        if n % t == 0:
            best = t
        t += align
    assert best is not None, (n, pref, align)
    return best


def _cparams(sem=None):
    return pltpu.CompilerParams(dimension_semantics=sem, vmem_limit_bytes=VMEM_LIMIT)


def _all_gather(arrs, name):
    n = len(arrs)

    def body(*refs):
        ins = refs[:n]
        outs = refs[n:2 * n]
        send_sems, recv_sems, local_sems = refs[2 * n:]
        x, y, c = lax.axis_index("x"), lax.axis_index("y"), lax.axis_index("c")
        me, sibling = (x, y, c), (x, y, 1 - c)
        chips = [(1 - x, y), (x, 1 - y), (1 - x, 1 - y)]

        def rows(a, dev):
            px, py, pc = dev
            return outs[a].at[4 * px + 2 * py + pc]

        def copy(a, k, block, to, src=None):
            return pltpu.make_async_remote_copy(
                src_ref=rows(a, block) if src is None else src,
                dst_ref=rows(a, block),
                send_sem=send_sems.at[7 * a + k],
                recv_sem=recv_sems.at[7 * a + k],
                device_id=to,
                device_id_type=MESH_ID,
            )

        mine = [pltpu.make_async_copy(ins[a], rows(a, me), local_sems.at[a]) for a in range(n)]
        for cp in mine:
            cp.start()
        first = []
        for a in range(n):
            first.append(copy(a, 0, me, sibling, src=ins[a]))
            first += [copy(a, 1 + j, me, (*chip, c), src=ins[a]) for j, chip in enumerate(chips)]
        for cp in first:
            cp.start()
        passed = []
        for j, chip in enumerate(chips):
            for a in range(n):
                copy(a, 1 + j, (*chip, c), me).wait_recv()
                fwd = copy(a, 4 + j, (*chip, c), sibling)
                fwd.start()
                passed.append(fwd)
        for a in range(n):
            copy(a, 0, sibling, me).wait_recv()
            for j, chip in enumerate(chips):
                copy(a, 4 + j, (*chip, 1 - c), me).wait_recv()
        for cp in first + passed:
            cp.wait_send()
        for cp in mine:
            cp.wait()

    any_spec = pl.BlockSpec(memory_space=pl.ANY)
    outs = pl.pallas_call(
        body,
        name=name,
        out_shape=[jax.ShapeDtypeStruct((N_DEV,) + a.shape, a.dtype) for a in arrs],
        in_specs=[any_spec] * n,
        out_specs=[any_spec] * n,
        scratch_shapes=[
            pltpu.SemaphoreType.DMA((7 * n,)),
            pltpu.SemaphoreType.DMA((7 * n,)),
            pltpu.SemaphoreType.DMA((n,)),
        ],
    )(*arrs)
    return list(outs)


def _all_to_all(arrs, name):
    n = len(arrs)

    def body(*refs):
        ins = refs[:n]
        outs = refs[n:2 * n]
        send_sems, recv_sems, local_sems = refs[2 * n:]
        x, y, c = lax.axis_index("x"), lax.axis_index("y"), lax.axis_index("c")
        my_idx = 4 * x + 2 * y + c

        def peer(k):
            fx, fy, fc = (k >> 2) & 1, (k >> 1) & 1, k & 1
            return (x ^ fx if fx else x, y ^ fy if fy else y, c ^ fc if fc else c)

        def copy(a, k):
            px, py, pc = peer(k)
            return pltpu.make_async_remote_copy(
                src_ref=ins[a].at[4 * px + 2 * py + pc],
                dst_ref=outs[a].at[my_idx],
                send_sem=send_sems.at[7 * a + k - 1],
                recv_sem=recv_sems.at[7 * a + k - 1],
                device_id=(px, py, pc),
                device_id_type=MESH_ID,
            )

        mine = [pltpu.make_async_copy(ins[a].at[my_idx], outs[a].at[my_idx], local_sems.at[a]) for a in range(n)]
        for cp in mine:
            cp.start()
        order = [1, 4, 2, 5, 3, 6, 7]
        cps = [copy(a, k) for k in order for a in range(n)]
        for cp in cps:
            cp.start()
        for cp in cps:
            cp.wait()
        for cp in mine:
            cp.wait()

    any_spec = pl.BlockSpec(memory_space=pl.ANY)
    outs = pl.pallas_call(
        body,
        name=name,
        out_shape=[jax.ShapeDtypeStruct(a.shape, a.dtype) for a in arrs],
        in_specs=[any_spec] * n,
        out_specs=[any_spec] * n,
        scratch_shapes=[
            pltpu.SemaphoreType.DMA((7 * n,)),
            pltpu.SemaphoreType.DMA((7 * n,)),
            pltpu.SemaphoreType.DMA((n,)),
        ],
    )(*arrs)
    return list(outs)


_DIMS = {
    "nn": (((1,), (0,)), ((), ())),
    "nt": (((1,), (1,)), ((), ())),
    "tn": (((0,), (0,)), ((), ())),
}


def _mm_call(a, b, *, mode, grid, a_spec, b_spec, o_spec, out_shape, acc_shape, name):
    nk = grid[2]
    out_dtype = out_shape.dtype

    def body(a_ref, b_ref, o_ref, *scratch):
        p = lax.dot_general(a_ref[...].astype(BF16), b_ref[...].astype(BF16), _DIMS[mode],
                            preferred_element_type=F32)
        if nk == 1:
            o_ref[...] = p.astype(out_dtype)
        else:
            acc = scratch[0]
            k = pl.program_id(2)

            @pl.when(k == 0)
            def _():
                acc[...] = p

            @pl.when(k > 0)
            def _():
                acc[...] += p

            @pl.when(k == nk - 1)
            def _():
                o_ref[...] = acc[...].astype(out_dtype)

    return pl.pallas_call(
        body,
        name=name,
        out_shape=out_shape,
        grid=grid,
        in_specs=[a_spec, b_spec],
        out_specs=o_spec,
        scratch_shapes=[pltpu.VMEM(acc_shape, F32)] if nk > 1 else [],
        compiler_params=_cparams(("parallel", "parallel", "arbitrary")),
    )(a, b)


def _mm(a, b, mode, out_dtype, name, tm=512, tn=512, tk=2432, a_row_off=0, rows=None):
    if mode == "nn":
        (m, k), (k2, n) = a.shape, b.shape
    elif mode == "nt":
        (m, k), (n, k2) = a.shape, b.shape
    else:
        (k, m), (k2, n) = a.shape, b.shape
        if rows is not None:
            k = k2 = rows
    assert k == k2, (a.shape, b.shape, mode)
    if mode != "tn":
        m = (m if rows is None else rows + a_row_off) - a_row_off
    tm, tn, tk = _tile(m, tm, 8), _tile(n, tn), _tile(k, tk, 8 if mode == "tn" else LANE)
    assert a_row_off % tm == 0
    ro = a_row_off // tm
    grid = (m // tm, n // tn, k // tk)
    if mode == "tn":
        a_spec = pl.BlockSpec((tk, tm), lambda i, j, kk: (kk, i))
    else:
        a_spec = pl.BlockSpec((tm, tk), lambda i, j, kk: (i + ro, kk))
    if mode == "nt":
        b_spec = pl.BlockSpec((tn, tk), lambda i, j, kk: (j, kk))
    else:
        b_spec = pl.BlockSpec((tk, tn), lambda i, j, kk: (kk, j))
    o_spec = pl.BlockSpec((tm, tn), lambda i, j, kk: (i, j))
    return _mm_call(a, b, mode=mode, grid=grid, a_spec=a_spec, b_spec=b_spec, o_spec=o_spec,
                    out_shape=jax.ShapeDtypeStruct((m, n), out_dtype), acc_shape=(tm, tn), name=name)


def _rms(x):
    r = lax.rsqrt(jnp.mean(x * x, axis=-1, keepdims=True) + NORM_EPS)
    return x * r, r


def _rms_bwd(dxh, xh, r):
    return r * (dxh - xh * jnp.mean(dxh * xh, axis=-1, keepdims=True))


def _colsum(v):
    return jnp.sum(v, axis=0, keepdims=True)


def _rope(v, c, s1, s2, q):
    w = v.shape[-1]
    return v * c + pltpu.roll(v, w - q, 1) * s1 + pltpu.roll(v, q, 1) * s2


def _rope_t(d, c, s1, s2, q):
    w = d.shape[-1]
    return d * c + pltpu.roll(d * s1, q, 1) + pltpu.roll(d * s2, w - q, 1)


def _norm_mod_fwd(ctx, x, gain, mods):
    tc, d = ctx.shape
    t = x.shape[0]
    rb = min(ROW_BLOCK, tc)
    nbl = t // rb

    def body(ctx_ref, x_ref, g_ref, mod_ref, z_ref):
        i = pl.program_id(0)

        def emit(src, sh, sc):
            xh, _ = _rms(src[...])
            z_ref[...] = ((xh * g_ref[...]) * (1.0 + sc) + sh).astype(BF16)

        @pl.when(i >= nbl)
        def _():
            emit(ctx_ref, mod_ref[2:3, :], mod_ref[3:4, :])

        @pl.when(i < nbl)
        def _():
            emit(x_ref, mod_ref[0:1, :], mod_ref[1:2, :])

    return pl.pallas_call(
        body,
        name="norm1_mod_fwd",
        out_shape=jax.ShapeDtypeStruct((tc + t, d), BF16),
        grid=((tc + t) // rb,),
        in_specs=[
            pl.BlockSpec((rb, d), lambda i: (jnp.maximum(i - nbl, 0), 0)),
            pl.BlockSpec((rb, d), lambda i: (jnp.minimum(i, nbl - 1), 0)),
            pl.BlockSpec((1, d), lambda i: (0, 0)),
            pl.BlockSpec((8, d), lambda i: (0, 0)),
        ],
        out_specs=pl.BlockSpec((rb, d), lambda i: (i, 0)),
        compiler_params=_cparams(("arbitrary",)),
    )(ctx, x, gain, mods)


def _norm1_bwd(ctx, x, gain, mods, dz_ctx, dz_lat, dx1):
    tc, d = ctx.shape
    t = x.shape[0]
    rb = min(ROW_BLOCK, tc)
    nbl = t // rb

    def body(ctx_ref, x_ref, g_ref, mod_ref, dzc_ref, dzl_ref, dx1_ref, gx_ref, st_ref):
        i = pl.program_id(0)

        @pl.when(i == 0)
        def _():
            st_ref[...] = jnp.zeros_like(st_ref)

        def common(src, dz, sc, row_sh, row_sc):
            xh, r = _rms(src[...])
            g = g_ref[...]
            dxn = dz * (1.0 + sc)
            st_ref[row_sh:row_sh + 1, :] += _colsum(dz)
            st_ref[row_sc:row_sc + 1, :] += _colsum(dz * (xh * g))
            st_ref[2:3, :] += _colsum(dxn * xh)
            return _rms_bwd(dxn * g, xh, r)

        @pl.when(i >= nbl)
        def _():
            common(ctx_ref, dzc_ref[...], mod_ref[3:4, :], 3, 4)

        @pl.when(i < nbl)
        def _():
            gx_ref[...] = dx1_ref[...] + common(x_ref, dzl_ref[...], mod_ref[1:2, :], 0, 1)

    lat = lambda i: (jnp.minimum(i, nbl - 1), 0)
    cix = lambda i: (jnp.maximum(i - nbl, 0), 0)
    return pl.pallas_call(
        body,
        name="norm1_mod_bwd",
        out_shape=[jax.ShapeDtypeStruct((t, d), F32), jax.ShapeDtypeStruct((8, d), F32)],
        grid=((tc + t) // rb,),
        in_specs=[
            pl.BlockSpec((rb, d), cix),
            pl.BlockSpec((rb, d), lat),
            pl.BlockSpec((1, d), lambda i: (0, 0)),
            pl.BlockSpec((8, d), lambda i: (0, 0)),
            pl.BlockSpec((rb, d), cix),
            pl.BlockSpec((rb, d), lat),
            pl.BlockSpec((rb, d), lat),
        ],
        out_specs=[pl.BlockSpec((rb, d), lat), pl.BlockSpec((8, d), lambda i: (0, 0))],
        compiler_params=_cparams(("arbitrary",)),
    )(ctx, x, gain, mods, dz_ctx, dz_lat, dx1)


def _key_prep_fwd(kv, kv_gain, kb_gain, tabs):
    ta, wkv = kv.shape
    kvl = MLA_KV_LORA
    nb = GQA_KV_HEADS * GQA_HEAD_DIM
    rb = ROW_BLOCK if ta % ROW_BLOCK == 0 else LANE
    hd = GQA_HEAD_DIM

    def body(kv_ref, g_ref, gb_ref, ca, s1a, s2a, cb, s1b, s2b, kin_ref, kb_ref, vb_ref):
        xh, _ = _rms(kv_ref[:, 0:kvl])
        kin_ref[:, 0:kvl] = (xh * g_ref[...]).astype(BF16)
        kpe = kv_ref[:, kvl + 2 * nb:kvl + 2 * nb + LANE]
        kin_ref[:, kvl:kvl + LANE] = _rope(kpe, ca[...], s1a[...], s2a[...], MLA_ROPE // 4).astype(BF16)
        for h in range(GQA_KV_HEADS):
            nh, _ = _rms(kv_ref[:, kvl + h * hd:kvl + (h + 1) * hd])
            kb_ref[:, h * hd:(h + 1) * hd] = _rope(nh * gb_ref[...], cb[...], s1b[...], s2b[...], hd // 4).astype(BF16)
        vb_ref[...] = kv_ref[:, kvl + nb:kvl + 2 * nb].astype(BF16)

    row = lambda w: pl.BlockSpec((rb, w), lambda i: (i, 0))
    fix = lambda w: pl.BlockSpec((1, w), lambda i: (0, 0))
    return pl.pallas_call(
        body,
        name="key_prep_fwd",
        out_shape=[jax.ShapeDtypeStruct((ta, kvl + LANE), BF16), jax.ShapeDtypeStruct((ta, nb), BF16),
                   jax.ShapeDtypeStruct((ta, nb), BF16)],
        grid=(ta // rb,),
        in_specs=[row(wkv), fix(kvl), fix(hd)] + [row(LANE)] * 3 + [row(hd)] * 3,
        out_specs=[row(kvl + LANE), row(nb), row(nb)],
        compiler_params=_cparams(("parallel",)),
    )(kv, kv_gain, kb_gain, *tabs)


def _key_prep_bwd(kv, kv_gain, kb_gain, tabs, dkin, dkb, dvb):
    ta, wkv = kv.shape
    kvl = MLA_KV_LORA
    nb = GQA_KV_HEADS * GQA_HEAD_DIM
    rb = ROW_BLOCK if ta % ROW_BLOCK == 0 else LANE
    hd = GQA_HEAD_DIM

    def body(kv_ref, g_ref, gb_ref, ca, s1a, s2a, cb, s1b, s2b, dkin_ref, dkb_ref, dvb_ref, dkv_ref, st_ref, stb_ref):
        @pl.when(pl.program_id(0) == 0)
        def _():
            st_ref[...] = jnp.zeros_like(st_ref)
            stb_ref[...] = jnp.zeros_like(stb_ref)

        xh, r = _rms(kv_ref[:, 0:kvl])
        dn = dkin_ref[:, 0:kvl]
        st_ref[0:1, :] += _colsum(dn * xh)
        dkv_ref[:, 0:kvl] = _rms_bwd(dn * g_ref[...], xh, r).astype(BF16)
        dpe = _rope_t(dkin_ref[:, kvl:kvl + LANE], ca[...], s1a[...], s2a[...], MLA_ROPE // 4)
        dkv_ref[:, kvl + 2 * nb:kvl + 2 * nb + LANE] = dpe.astype(BF16)
        for h in range(GQA_KV_HEADS):
            nh, rh = _rms(kv_ref[:, kvl + h * hd:kvl + (h + 1) * hd])
            dn_h = _rope_t(dkb_ref[:, h * hd:(h + 1) * hd], cb[...], s1b[...], s2b[...], hd // 4)
            stb_ref[0:1, :] += _colsum(dn_h * nh)
            dkv_ref[:, kvl + h * hd:kvl + (h + 1) * hd] = _rms_bwd(dn_h * gb_ref[...], nh, rh).astype(BF16)
        dkv_ref[:, kvl + nb:kvl + 2 * nb] = dvb_ref[...].astype(BF16)

    row = lambda w: pl.BlockSpec((rb, w), lambda i: (i, 0))
    fix = lambda w: pl.BlockSpec((1, w), lambda i: (0, 0))
    return pl.pallas_call(
        body,
        name="key_prep_bwd",
        out_shape=[jax.ShapeDtypeStruct((ta, wkv), BF16), jax.ShapeDtypeStruct((8, kvl), F32),
                   jax.ShapeDtypeStruct((8, hd), F32)],
        grid=(ta // rb,),
        in_specs=[row(wkv), fix(kvl), fix(hd)] + [row(LANE)] * 3 + [row(hd)] * 3 + [row(kvl + LANE), row(nb), row(nb)],
        out_specs=[row(wkv), pl.BlockSpec((8, kvl), lambda i: (0, 0)), pl.BlockSpec((8, hd), lambda i: (0, 0))],
        compiler_params=_cparams(("arbitrary",)),
    )(kv, kv_gain, kb_gain, *tabs, dkin, dkb, dvb)


def _q_prep_fwd(qg, q_gain, qb_gain, tabs):
    t = qg.shape[0]
    ql = MLA_Q_LORA
    hd = GQA_HEAD_DIM
    hb = GQA_HEADS * hd
    rb = min(ROW_BLOCK, t)

    def body(q_ref, g_ref, gb_ref, cb, s1b, s2b, cqn_ref, qb_ref):
        xh, _ = _rms(q_ref[:, 0:ql])
        cqn_ref[...] = (xh * g_ref[...]).astype(BF16)
        for h in range(GQA_HEADS):
            nh, _ = _rms(q_ref[:, ql + h * hd:ql + (h + 1) * hd])
            qb_ref[:, h * hd:(h + 1) * hd] = _rope(nh * gb_ref[...], cb[...], s1b[...], s2b[...], hd // 4).astype(BF16)

    row = lambda w: pl.BlockSpec((rb, w), lambda i: (i, 0))
    fix = lambda w: pl.BlockSpec((1, w), lambda i: (0, 0))
    return pl.pallas_call(
        body,
        name="q_prep_fwd",
        out_shape=[jax.ShapeDtypeStruct((t, ql), BF16), jax.ShapeDtypeStruct((t, hb), BF16)],
        grid=(t // rb,),
        in_specs=[row(ql + hb), fix(ql), fix(hd)] + [row(hd)] * 3,
        out_specs=[row(ql), row(hb)],
        compiler_params=_cparams(("parallel",)),
    )(qg, q_gain, qb_gain, *tabs)


def _q_prep_bwd(qg, q_gain, qb_gain, tabs, dcqn, dqb, wpad):
    t = qg.shape[0]
    ql = MLA_Q_LORA
    hd = GQA_HEAD_DIM
    hb = GQA_HEADS * hd
    rb = min(ROW_BLOCK, t)

    def body(q_ref, g_ref, gb_ref, cb, s1b, s2b, dcqn_ref, dqb_ref, dq_ref, st_ref, stb_ref):
        @pl.when(pl.program_id(0) == 0)
        def _():
            st_ref[...] = jnp.zeros_like(st_ref)
            stb_ref[...] = jnp.zeros_like(stb_ref)

        xh, r = _rms(q_ref[:, 0:ql])
        dn = dcqn_ref[...]
        st_ref[0:1, :] += _colsum(dn * xh)
        dq_ref[:, 0:ql] = _rms_bwd(dn * g_ref[...], xh, r).astype(BF16)
        for h in range(GQA_HEADS):
            nh, rh = _rms(q_ref[:, ql + h * hd:ql + (h + 1) * hd])
            dn_h = _rope_t(dqb_ref[:, h * hd:(h + 1) * hd], cb[...], s1b[...], s2b[...], hd // 4)
            stb_ref[0:1, :] += _colsum(dn_h * nh)
            dq_ref[:, ql + h * hd:ql + (h + 1) * hd] = _rms_bwd(dn_h * gb_ref[...], nh, rh).astype(BF16)
        if wpad:
            dq_ref[:, ql + hb:ql + hb + wpad] = jnp.zeros((rb, wpad), BF16)

    row = lambda w: pl.BlockSpec((rb, w), lambda i: (i, 0))
    fix = lambda w: pl.BlockSpec((1, w), lambda i: (0, 0))
    return pl.pallas_call(
        body,
        name="q_prep_bwd",
        out_shape=[jax.ShapeDtypeStruct((t, ql + hb + wpad), BF16), jax.ShapeDtypeStruct((8, ql), F32),
                   jax.ShapeDtypeStruct((8, hd), F32)],
        grid=(t // rb,),
        in_specs=[row(ql + hb), fix(ql), fix(hd)] + [row(hd)] * 3 + [row(ql), row(hb)],
        out_specs=[row(ql + hb + wpad), pl.BlockSpec((8, ql), lambda i: (0, 0)), pl.BlockSpec((8, hd), lambda i: (0, 0))],
        compiler_params=_cparams(("arbitrary",)),
    )(qg, q_gain, qb_gain, *tabs, dcqn, dqb)


def _rope_a(v, tabs, transpose, out_dtype, name):
    t, w = v.shape
    rb = min(ROW_BLOCK, t)
    fn = _rope_t if transpose else _rope

    def body(v_ref, c, s1, s2, o_ref):
        for h in range(w // MLA_SLOT):
            sl = slice(h * MLA_SLOT, (h + 1) * MLA_SLOT)
            o_ref[:, sl] = fn(v_ref[:, sl].astype(F32), c[...], s1[...], s2[...], MLA_ROPE // 4).astype(out_dtype)

    row = lambda ww: pl.BlockSpec((rb, ww), lambda i: (i, 0))
    return pl.pallas_call(
        body,
        name=name,
        out_shape=jax.ShapeDtypeStruct((t, w), out_dtype),
        grid=(t // rb,),
        in_specs=[row(w)] + [row(MLA_SLOT)] * 3,
        out_specs=row(w),
        compiler_params=_cparams(("parallel",)),
    )(v, *tabs)


def _merge_fwd(pa, pb, qg, gate_blk):
    t, d = pa.shape
    rb = min(ROW_BLOCK, t)

    def body(pa_ref, pb_ref, ga_ref, gb_ref, o_ref):
        o_ref[...] = (jax.nn.sigmoid(ga_ref[...]) * pa_ref[...] + jax.nn.sigmoid(gb_ref[...]) * pb_ref[...]).astype(BF16)

    row = pl.BlockSpec((rb, d), lambda i: (i, 0))
    return pl.pallas_call(
        body,
        name="merge_fwd",
        out_shape=jax.ShapeDtypeStruct((t, d), BF16),
        grid=(t // rb,),
        in_specs=[row, row, pl.BlockSpec((rb, d), lambda i: (i, gate_blk)), pl.BlockSpec((rb, d), lambda i: (i, gate_blk + 1))],
        out_specs=row,
        compiler_params=_cparams(("parallel",)),
    )(pa, pb, qg, qg)


def _merge_bwd(dm, pa, pb, qg, gate_blk):
    t, d = pa.shape
    rb = min(ROW_BLOCK, t)

    def body(dm_ref, pa_ref, pb_ref, ga_ref, gb_ref, dpa_ref, dpb_ref, dg_ref):
        dmv = dm_ref[...]
        sa = jax.nn.sigmoid(ga_ref[...])
        sb = jax.nn.sigmoid(gb_ref[...])
        dpa_ref[...] = (dmv * sa).astype(BF16)
        dpb_ref[...] = (dmv * sb).astype(BF16)
        dg_ref[:, 0:d] = (dmv * pa_ref[...] * (sa * (1.0 - sa))).astype(BF16)
        dg_ref[:, d:2 * d] = (dmv * pb_ref[...] * (sb * (1.0 - sb))).astype(BF16)

    row = pl.BlockSpec((rb, d), lambda i: (i, 0))
    return pl.pallas_call(
        body,
        name="merge_bwd",
        out_shape=[jax.ShapeDtypeStruct((t, d), BF16), jax.ShapeDtypeStruct((t, d), BF16),
                   jax.ShapeDtypeStruct((t, 2 * d), BF16)],
        grid=(t // rb,),
        in_specs=[row, row, row, pl.BlockSpec((rb, d), lambda i: (i, gate_blk)), pl.BlockSpec((rb, d), lambda i: (i, gate_blk + 1))],
        out_specs=[row, row, pl.BlockSpec((rb, 2 * d), lambda i: (i, 0))],
        compiler_params=_cparams(("parallel",)),
    )(dm, pa, pb, qg, qg)


def _resid_norm_mod(x, branch, gain, mods, name):
    t, d = x.shape
    rb = min(ROW_BLOCK, t)

    def body(x_ref, b_ref, g_ref, mod_ref, x1_ref, z_ref):
        x1 = x_ref[...] + mod_ref[0:1, :] * b_ref[...]
        x1_ref[...] = x1
        xh, _ = _rms(x1)
        z_ref[...] = ((xh * g_ref[...]) * (1.0 + mod_ref[2:3, :]) + mod_ref[1:2, :]).astype(BF16)

    row = pl.BlockSpec((rb, d), lambda i: (i, 0))
    return pl.pallas_call(
        body,
        name=name,
        out_shape=[jax.ShapeDtypeStruct((t, d), F32), jax.ShapeDtypeStruct((t, d), BF16)],
        grid=(t // rb,),
        in_specs=[row, row, pl.BlockSpec((1, d), lambda i: (0, 0)), pl.BlockSpec((8, d), lambda i: (0, 0))],
        out_specs=[row, row],
        compiler_params=_cparams(("parallel",)),
    )(x, branch, gain, mods)


def _norm2_bwd(x1, attn, gain, mods, dz2, dx2):
    t, d = x1.shape
    rb = min(ROW_BLOCK, t)

    def body(x1_ref, at_ref, g_ref, mod_ref, dz_ref, dx2_ref, dx1_ref, da_ref, st_ref):
        @pl.when(pl.program_id(0) == 0)
        def _():
            st_ref[...] = jnp.zeros_like(st_ref)

        xh, r = _rms(x1_ref[...])
        g = g_ref[...]
        dz = dz_ref[...]
        dxn = dz * (1.0 + mod_ref[1:2, :])
        st_ref[0:1, :] += _colsum(dz)
        st_ref[1:2, :] += _colsum(dz * (xh * g))
        st_ref[2:3, :] += _colsum(dxn * xh)
        dx1 = dx2_ref[...] + _rms_bwd(dxn * g, xh, r)
        dx1_ref[...] = dx1
        st_ref[3:4, :] += _colsum(dx1 * at_ref[...])
        da_ref[...] = (dx1 * mod_ref[0:1, :]).astype(BF16)

    row = pl.BlockSpec((rb, d), lambda i: (i, 0))
    return pl.pallas_call(
        body,
        name="norm2_mod_bwd",
        out_shape=[jax.ShapeDtypeStruct((t, d), F32), jax.ShapeDtypeStruct((t, d), BF16), jax.ShapeDtypeStruct((8, d), F32)],
        grid=(t // rb,),
        in_specs=[row, row, pl.BlockSpec((1, d), lambda i: (0, 0)), pl.BlockSpec((8, d), lambda i: (0, 0)), row, row],
        out_specs=[row, row, pl.BlockSpec((8, d), lambda i: (0, 0))],
        compiler_params=_cparams(("arbitrary",)),
    )(x1, attn, gain, mods, dz2, dx2)


def _final_loss(x1, ffn, gain, mods, target):
    t, d = x1.shape
    rb = min(ROW_BLOCK, t)
    nb = t // rb

    def body(x1_ref, f_ref, g_ref, mod_ref, tg_ref, dx2_ref, df_ref, st_ref):
        i = pl.program_id(0)

        @pl.when(i == 0)
        def _():
            st_ref[...] = jnp.zeros_like(st_ref)

        ffn_v = f_ref[...]
        g2 = mod_ref[0:1, :]
        x2 = x1_ref[...] + g2 * ffn_v
        xh, r = _rms(x2)
        g = g_ref[...]
        err = xh * g - tg_ref[...]
        st_ref[2:3, :] += _colsum(err * err) * (0.5 / d)
        dy = err * (1.0 / d)
        st_ref[0:1, :] += _colsum(dy * xh)
        dx2 = _rms_bwd(dy * g, xh, r)
        dx2_ref[...] = dx2
        st_ref[1:2, :] += _colsum(dx2 * ffn_v)
        df_ref[...] = (dx2 * g2).astype(BF16)

        @pl.when(i == nb - 1)
        def _():
            st_ref[3:4, :] = jnp.broadcast_to(jnp.sum(st_ref[2:3, :], axis=-1, keepdims=True), (1, d))

    row = pl.BlockSpec((rb, d), lambda i: (i, 0))
    return pl.pallas_call(
        body,
        name="final_norm_loss",
        out_shape=[jax.ShapeDtypeStruct((t, d), F32), jax.ShapeDtypeStruct((t, d), BF16), jax.ShapeDtypeStruct((8, d), F32)],
        grid=(nb,),
        in_specs=[row, row, pl.BlockSpec((1, d), lambda i: (0, 0)), pl.BlockSpec((8, d), lambda i: (0, 0)), row],
        out_specs=[row, row, pl.BlockSpec((8, d), lambda i: (0, 0))],
        compiler_params=_cparams(("arbitrary",)),
    )(x1, ffn, gain, mods, target)


def _shift_rows(v, down):
    n = v.shape[0]
    rows = lax.broadcasted_iota(jnp.int32, v.shape, 0)
    if down:
        return jnp.where(rows == 0, 0.0, pltpu.roll(v, 1, 0))
    return jnp.where(rows == n - 1, 0.0, pltpu.roll(v, n - 1, 0))


def _conv_act(ua, ub, cwa, cwb, cba, cbb):
    a = cba + cwa[0:1, :] * _shift_rows(ua, True) + cwa[1:2, :] * ua + cwa[2:3, :] * _shift_rows(ua, False)
    b = cbb + cwb[0:1, :] * _shift_rows(ub, True) + cwb[1:2, :] * ub + cwb[2:3, :] * _shift_rows(ub, False)
    return a, b


def _conv_fwd(u, cw, cb):
    t, f2 = u.shape
    f = f2 // 2
    cbk = _tile(f, 256)
    nf = f // cbk

    def body(ua_ref, ub_ref, cwa_ref, cwb_ref, cba_ref, cbb_ref, h_ref):
        a, b = _conv_act(ua_ref[...], ub_ref[...], cwa_ref[...], cwb_ref[...], cba_ref[...], cbb_ref[...])
        h_ref[...] = (a * jax.nn.sigmoid(a) * b).astype(BF16)

    ca = lambda r: pl.BlockSpec((r, cbk), lambda j: (0, j))
    cbs = lambda r: pl.BlockSpec((r, cbk), lambda j: (0, nf + j))
    return pl.pallas_call(
        body,
        name="conv_gate_fwd",
        out_shape=jax.ShapeDtypeStruct((t, f), BF16),
        grid=(nf,),
        in_specs=[ca(t), cbs(t), ca(3), cbs(3), ca(1), cbs(1)],
        out_specs=ca(t),
        compiler_params=_cparams(("parallel",)),
    )(u, u, cw, cw, cb, cb)


def _conv_bwd(u, cw, cb, dh):
    t, f2 = u.shape
    f = f2 // 2
    cbk = _tile(f, 256)
    nf = f // cbk

    def body(ua_ref, ub_ref, cwa_ref, cwb_ref, cba_ref, cbb_ref, dh_ref, du_ref, dcw_ref, dcb_ref):
        ua, ub = ua_ref[...], ub_ref[...]
        cwa, cwb = cwa_ref[...], cwb_ref[...]
        a, b = _conv_act(ua, ub, cwa, cwb, cba_ref[...], cbb_ref[...])
        dh_v = dh_ref[...]
        sg = jax.nn.sigmoid(a)
        db = dh_v * (a * sg)
        da = dh_v * b * (sg * (1.0 + a * (1.0 - sg)))
        for idx, (dv, uu, cwv) in enumerate(((da, ua, cwa), (db, ub, cwb))):
            dcb_ref[idx] = _colsum(dv)
            dcw_ref[idx, 0:1, :] = _colsum(dv * _shift_rows(uu, True))
            dcw_ref[idx, 1:2, :] = _colsum(dv * uu)
            dcw_ref[idx, 2:3, :] = _colsum(dv * _shift_rows(uu, False))
            du = cwv[0:1, :] * _shift_rows(dv, False) + cwv[1:2, :] * dv + cwv[2:3, :] * _shift_rows(dv, True)
            du_ref[idx] = du.astype(BF16)

    ca = lambda r: pl.BlockSpec((r, cbk), lambda j: (0, j))
    cbs = lambda r: pl.BlockSpec((r, cbk), lambda j: (0, nf + j))
    o3 = lambda r: pl.BlockSpec((2, r, cbk), lambda j: (0, 0, j))
    return pl.pallas_call(
        body,
        name="conv_gate_bwd",
        out_shape=[jax.ShapeDtypeStruct((2, t, f), BF16), jax.ShapeDtypeStruct((2, 3, f), F32),
                   jax.ShapeDtypeStruct((2, 1, f), F32)],
        grid=(nf,),
        in_specs=[ca(t), cbs(t), ca(3), cbs(3), ca(1), cbs(1), ca(t)],
        out_specs=[o3(t), o3(3), o3(1)],
        compiler_params=_cparams(("parallel",)),
    )(u, u, cw, cw, cb, cb, dh)


def _attention_fwd(q, kk, vv, scale, *, hq, hkv, dk, dv, k_blk0, v_blk0, name):
    t = q.shape[0]
    tk = kk.shape[0]
    g_sz = hq // hkv
    tq = min(ATT_Q_BLOCK, t)

    def body(q_ref, k_ref, v_ref, o_ref, lse_ref):
        k = k_ref[...]
        v = v_ref[...]
        for j in range(g_sz):
            s = lax.dot_general(q_ref[:, j * dk:(j + 1) * dk], k, _DIMS["nt"], preferred_element_type=F32) * scale
            m = jnp.max(s, axis=-1, keepdims=True)
            p = jnp.exp(s - m)
            l = jnp.sum(p, axis=-1, keepdims=True)
            o = jnp.dot(p.astype(BF16), v, preferred_element_type=F32) / l
            o_ref[:, j * dv:(j + 1) * dv] = o.astype(BF16)
            lse_ref[0, :, j:j + 1] = m + jnp.log(l)

    return pl.pallas_call(
        body,
        name=name,
        out_shape=[jax.ShapeDtypeStruct((t, hq * dv), BF16), jax.ShapeDtypeStruct((hkv, t, g_sz), F32)],
        grid=(hkv, t // tq),
        in_specs=[
            pl.BlockSpec((tq, g_sz * dk), lambda g, i: (i, g)),
            pl.BlockSpec((tk, dk), lambda g, i: (0, k_blk0 + g)),
            pl.BlockSpec((tk, dv), lambda g, i: (0, v_blk0 + g)),
        ],
        out_specs=[
            pl.BlockSpec((tq, g_sz * dv), lambda g, i: (i, g)),
            pl.BlockSpec((1, tq, g_sz), lambda g, i: (g, i, 0)),
        ],
        compiler_params=_cparams(("parallel", "parallel")),
    )(q, kk, vv)


def _attention_bwd(q, kk, vv, do, lse, scale, *, hq, hkv, dk, dv, k_blk0, v_blk0, name):
    t = q.shape[0]
    tk = kk.shape[0]
    g_sz = hq // hkv
    tq = min(ATT_Q_BLOCK, t)

    def body(q_ref, k_ref, v_ref, do_ref, lse_ref, dq_ref, dk_ref, dv_ref):
        @pl.when(pl.program_id(1) == 0)
        def _():
            dk_ref[...] = jnp.zeros_like(dk_ref)
            dv_ref[...] = jnp.zeros_like(dv_ref)

        k = k_ref[...]
        v = v_ref[...]
        for j in range(g_sz):
            qj = q_ref[:, j * dk:(j + 1) * dk]
            doj = do_ref[:, j * dv:(j + 1) * dv]
            s = lax.dot_general(qj, k, _DIMS["nt"], preferred_element_type=F32) * scale
            p = jnp.exp(s - lse_ref[0, :, j:j + 1])
            dp = lax.dot_general(doj, v, _DIMS["nt"], preferred_element_type=F32)
            ds = (p * (dp - jnp.sum(p * dp, axis=-1, keepdims=True)) * scale).astype(BF16)
            dv_ref[...] += lax.dot_general(p.astype(BF16), doj, _DIMS["tn"], preferred_element_type=F32)
            dk_ref[...] += lax.dot_general(ds, qj, _DIMS["tn"], preferred_element_type=F32)
            dq_ref[:, j * dk:(j + 1) * dk] = jnp.dot(ds, k, preferred_element_type=F32)

    return pl.pallas_call(
        body,
        name=name,
        out_shape=[jax.ShapeDtypeStruct((t, hq * dk), F32), jax.ShapeDtypeStruct((tk, hkv * dk), F32),
                   jax.ShapeDtypeStruct((tk, hkv * dv), F32)],
        grid=(hkv, t // tq),
        in_specs=[
            pl.BlockSpec((tq, g_sz * dk), lambda g, i: (i, g)),
            pl.BlockSpec((tk, dk), lambda g, i: (0, k_blk0 + g)),
            pl.BlockSpec((tk, dv), lambda g, i: (0, v_blk0 + g)),
            pl.BlockSpec((tq, g_sz * dv), lambda g, i: (i, g)),
            pl.BlockSpec((1, tq, g_sz), lambda g, i: (g, i, 0)),
        ],
        out_specs=[
            pl.BlockSpec((tq, g_sz * dk), lambda g, i: (i, g)),
            pl.BlockSpec((tk, dk), lambda g, i: (0, g)),
            pl.BlockSpec((tk, dv), lambda g, i: (0, g)),
        ],
        compiler_params=_cparams(("parallel", "arbitrary")),
    )(q, kk, vv, do, lse)


def _silu(v):
    return v * jax.nn.sigmoid(v)


def _ada_fwd(conds, w_ada, b_ada_shard):
    r, d = conds.shape
    n = w_ada.shape[1]
    tn = _tile(n, 512)

    def body(c_ref, w_ref, b_ref, o_ref):
        s = _silu(c_ref[...]).astype(BF16)
        o_ref[...] = jnp.dot(s, w_ref[...].astype(BF16), preferred_element_type=F32) + b_ref[...]

    return pl.pallas_call(
        body,
        name="ada_fwd",
        out_shape=jax.ShapeDtypeStruct((r, n), F32),
        grid=(n // tn,),
        in_specs=[pl.BlockSpec((r, d), lambda j: (0, 0)), pl.BlockSpec((d, tn), lambda j: (0, j)),
                  pl.BlockSpec((1, tn), lambda j: (0, j))],
        out_specs=pl.BlockSpec((r, tn), lambda j: (0, j)),
        compiler_params=_cparams(("parallel",)),
    )(conds, w_ada, b_ada_shard)


def _cctx_partial(da16_shard, w_ada, c_ctx_row):
    d, n = w_ada.shape
    td = _tile(d, 512)

    def body(g_ref, w_ref, c_ref, o_ref):
        ds = lax.dot_general(g_ref[8:16, :].astype(BF16), w_ref[...].astype(BF16), _DIMS["nt"],
                             preferred_element_type=F32)
        cv = c_ref[...]
        sg = jax.nn.sigmoid(cv)
        o_ref[...] = ds * (sg * (1.0 + cv * (1.0 - sg)))

    return pl.pallas_call(
        body,
        name="cctx_partial",
        out_shape=jax.ShapeDtypeStruct((8, d), F32),
        grid=(d // td,),
        in_specs=[pl.BlockSpec((16, n), lambda j: (0, 0)), pl.BlockSpec((td, n), lambda j: (j, 0)),
                  pl.BlockSpec((1, td), lambda j: (0, j))],
        out_specs=pl.BlockSpec((8, td), lambda j: (0, j)),
        compiler_params=_cparams(("parallel",)),
    )(da16_shard, w_ada, c_ctx_row)


def _sum_parts(parts):
    p, _, n = parts.shape

    def body(p_ref, o_ref):
        acc = p_ref[0]
        for s in range(1, p):
            acc = acc + p_ref[s]
        o_ref[...] = acc

    return pl.pallas_call(
        body,
        name="sum_parts",
        out_shape=jax.ShapeDtypeStruct((1, n), F32),
        in_specs=[pl.BlockSpec(memory_space=pltpu.VMEM)],
        out_specs=pl.BlockSpec(memory_space=pltpu.VMEM),
    )(parts)


def _adam_math(w, g, m, v):
    m2 = ADAM_B1 * m + (1.0 - ADAM_B1) * g
    v2 = ADAM_B2 * v + (1.0 - ADAM_B2) * jnp.square(g)
    m_hat = m2 / (1.0 - ADAM_B1 ** ADAM_STEP)
    v_hat = v2 / (1.0 - ADAM_B2 ** ADAM_STEP)
    delta = -ADAM_LR * (m_hat / (jnp.sqrt(v_hat) + ADAM_EPS) + ADAM_WD * w)
    return delta, m2, v2


def _adamw(parts, w, m, v, name):
    p, r, c = parts.shape
    rb = _tile(r, max(8, (1 << 20) // (4 * c) // 8 * 8), 8)

    def body(p_ref, w_ref, m_ref, v_ref, g_ref, d_ref, m2_ref, v2_ref):
        g = p_ref[0].astype(F32)
        for s in range(1, p):
            g = g + p_ref[s].astype(F32)
        g_ref[...] = g
        d_ref[...], m2_ref[...], v2_ref[...] = _adam_math(w_ref[...], g, m_ref[...], v_ref[...])

    row = pl.BlockSpec((rb, c), lambda i: (i, 0))
    return pl.pallas_call(
        body,
        name=name,
        out_shape=[jax.ShapeDtypeStruct((r, c), F32)] * 4,
        grid=(r // rb,),
        in_specs=[pl.BlockSpec((p, rb, c), lambda i: (0, i, 0)), row, row, row],
        out_specs=[row] * 4,
        compiler_params=_cparams(("parallel",)),
    )(parts, w, m, v)


def _adamw_ada(conds, da16, w, m, v):
    d, n = w.shape
    rb = _tile(d, 256, LANE)

    def body(s_ref, da_ref, w_ref, m_ref, v_ref, g_ref, d_ref, m2_ref, v2_ref):
        g = lax.dot_general(_silu(s_ref[...]).astype(BF16), da_ref[...].astype(BF16), _DIMS["tn"],
                            preferred_element_type=F32)
        g_ref[...] = g
        d_ref[...], m2_ref[...], v2_ref[...] = _adam_math(w_ref[...], g, m_ref[...], v_ref[...])

    row = pl.BlockSpec((rb, n), lambda i: (i, 0))
    return pl.pallas_call(
        body,
        name="adamw_w_ada",
        out_shape=[jax.ShapeDtypeStruct((d, n), F32)] * 4,
        grid=(d // rb,),
        in_specs=[pl.BlockSpec((16, rb), lambda i: (0, i)), pl.BlockSpec((16, n), lambda i: (0, 0)), row, row, row],
        out_specs=[row] * 4,
        compiler_params=_cparams(("parallel",)),
    )(conds, da16, w, m, v)


def _cast_bf16(a, name):
    r, c = a.shape
    rb = _tile(r, 512, 8)

    def body(a_ref, o_ref):
        o_ref[...] = a_ref[...].astype(BF16)

    row = pl.BlockSpec((rb, c), lambda i: (i, 0))
    return pl.pallas_call(body, name=name, out_shape=jax.ShapeDtypeStruct((r, c), BF16), grid=(r // rb,),
                          in_specs=[row], out_specs=row, compiler_params=_cparams(("parallel",)))(a)


def _rope_tabs(t, rot):
    half, q = rot // 2, rot // 4
    n_rows = t // GRID_W
    row = jnp.repeat(jnp.arange(n_rows, dtype=F32), GRID_W)
    col = jnp.tile(jnp.arange(GRID_W, dtype=F32), n_rows)
    inv_freq = ROPE_THETA ** (-jnp.arange(0, half, 2, dtype=F32) / half)
    ang = jnp.concatenate([row[:, None] * inv_freq, col[:, None] * inv_freq], axis=-1)
    cos, sin = jnp.cos(ang), jnp.sin(ang)
    c0, c1, s0, s1 = cos[:, :q], cos[:, q:], sin[:, :q], sin[:, q:]
    z = jnp.zeros_like(s0)
    return (jnp.concatenate([c0, c0, c1, c1], -1), jnp.concatenate([-s0, z, -s1, z], -1),
            jnp.concatenate([z, s0, z, s1], -1))


def _pad_cols(a, left, total, fill=0.0):
    return jnp.pad(a, ((0, 0), (left, total - left - a.shape[1])), constant_values=fill)


def _with_ctx_rows(tab, tc, fill):
    return jnp.concatenate([tab, jnp.full((tc, tab.shape[1]), fill, F32)], axis=0)


def kernel(x, c, ctx, c_ctx, w_ada, b_ada, norm1_g, w_in, mla_q_norm_g, w_q_up, mla_kv_norm_g, w_kv_up, gqa_q_norm_g, gqa_k_norm_g, w_br_a, w_br_b, w_out, norm2_g, w_up, conv_w, conv_b, w_down, final_norm_g, loss_target, m_c_ctx, m_w_ada, m_b_ada, m_norm1_g, m_w_in, m_mla_q_norm_g, m_w_q_up, m_mla_kv_norm_g, m_w_kv_up, m_gqa_q_norm_g, m_gqa_k_norm_g, m_w_br_a, m_w_br_b, m_w_out, m_norm2_g, m_w_up, m_conv_w, m_conv_b, m_w_down, m_final_norm_g, v_c_ctx, v_w_ada, v_b_ada, v_norm1_g, v_w_in, v_mla_q_norm_g, v_w_q_up, v_mla_kv_norm_g, v_w_kv_up, v_gqa_q_norm_g, v_gqa_k_norm_g, v_w_br_a, v_w_br_b, v_w_out, v_norm2_g, v_w_up, v_conv_w, v_conv_b, v_w_down, v_final_norm_g):
    weights = dict(c_ctx=c_ctx, w_ada=w_ada, b_ada=b_ada, norm1_g=norm1_g, w_in=w_in, mla_q_norm_g=mla_q_norm_g,
                   w_q_up=w_q_up, mla_kv_norm_g=mla_kv_norm_g, w_kv_up=w_kv_up, gqa_q_norm_g=gqa_q_norm_g,
                   gqa_k_norm_g=gqa_k_norm_g, w_br_a=w_br_a, w_br_b=w_br_b, w_out=w_out, norm2_g=norm2_g, w_up=w_up,
                   conv_w=conv_w, conv_b=conv_b, w_down=w_down, final_norm_g=final_norm_g)
    mom_m = dict(c_ctx=m_c_ctx, w_ada=m_w_ada, b_ada=m_b_ada, norm1_g=m_norm1_g, w_in=m_w_in, mla_q_norm_g=m_mla_q_norm_g,
                 w_q_up=m_w_q_up, mla_kv_norm_g=m_mla_kv_norm_g, w_kv_up=m_w_kv_up, gqa_q_norm_g=m_gqa_q_norm_g,
                 gqa_k_norm_g=m_gqa_k_norm_g, w_br_a=m_w_br_a, w_br_b=m_w_br_b, w_out=m_w_out, norm2_g=m_norm2_g,
                 w_up=m_w_up, conv_w=m_conv_w, conv_b=m_conv_b, w_down=m_w_down, final_norm_g=m_final_norm_g)
    mom_v = dict(c_ctx=v_c_ctx, w_ada=v_w_ada, b_ada=v_b_ada, norm1_g=v_norm1_g, w_in=v_w_in, mla_q_norm_g=v_mla_q_norm_g,
                 w_q_up=v_w_q_up, mla_kv_norm_g=v_mla_kv_norm_g, w_kv_up=v_w_kv_up, gqa_q_norm_g=v_gqa_q_norm_g,
                 gqa_k_norm_g=v_gqa_k_norm_g, w_br_a=v_w_br_a, w_br_b=v_w_br_b, w_out=v_w_out, norm2_g=v_norm2_g,
                 w_up=v_w_up, conv_w=v_conv_w, conv_b=v_conv_b, w_down=v_w_down, final_norm_g=v_final_norm_g)
    order = list(weights)

    my_idx = 4 * lax.axis_index("x") + 2 * lax.axis_index("y") + lax.axis_index("c")
    xs, cts, tgt = x[0], ctx[0], loss_target[0]
    t, d = xs.shape
    tc = cts.shape[0]
    ta = t + tc
    kvl, ql = MLA_KV_LORA, MLA_Q_LORA
    nb = GQA_KV_HEADS * GQA_HEAD_DIM
    hb = GQA_HEADS * GQA_HEAD_DIM
    ha = MLA_HEADS
    f2 = w_up.shape[2] * N_DEV
    ff = f2 // 2

    (c_all,) = _all_gather([jnp.pad(c, ((0, 7), (0, 0)))], "gather_cond")
    conds = jnp.concatenate([c_all[:, 0, :], c_ctx[None, :], jnp.zeros((7, d), F32)], axis=0)
    ncol = w_ada.shape[2]
    b_shard = lax.dynamic_slice_in_dim(b_ada, my_idx * ncol, ncol, axis=1)
    ada_shard = _ada_fwd(conds, w_ada[0], b_shard)
    (ada_all,) = _all_gather([ada_shard], "gather_ada")
    ada = jnp.transpose(ada_all, (1, 0, 2)).reshape(16, N_DEV * ncol)
    lat = lax.dynamic_slice_in_dim(ada, my_idx, 1, axis=0).reshape(6, d)
    cxt = ada[8].reshape(6, d)
    zero2 = jnp.zeros((2, d), F32)
    mods1 = jnp.concatenate([lat[0:2], cxt[0:2], jnp.zeros((4, d), F32)], axis=0)
    mods2 = jnp.concatenate([lat[2:3], lat[3:4], lat[4:5], jnp.zeros((5, d), F32)], axis=0)
    mods2b = jnp.concatenate([lat[2:3], lat[4:5], jnp.zeros((6, d), F32)], axis=0)
    mods3 = jnp.concatenate([lat[5:6], jnp.zeros((7, d), F32)], axis=0)
    del zero2

    big = ["w_in", "w_q_up", "w_kv_up", "w_br_a", "w_br_b", "w_out", "w_up", "w_down"]
    shards = [_cast_bf16(weights[n][0], "cast_" + n) for n in big]
    gathered = dict(zip(big, _all_gather(shards, "gather_weights")))

    def cols_full(g):
        return jnp.transpose(g, (1, 0, 2)).reshape(g.shape[1], N_DEV * g.shape[2])

    w_in_f = cols_full(gathered["w_in"])
    o_kpe, o_kb, o_vb = kvl, kvl + MLA_ROPE, kvl + MLA_ROPE + nb
    o_q = o_vb + nb
    o_g = o_q + ql + hb
    wkv_w = kvl + 2 * nb + LANE
    w_kv_p = jnp.concatenate([w_in_f[:, :kvl], w_in_f[:, o_kb:o_q], w_in_f[:, o_kpe:o_kb],
                              jnp.zeros((d, LANE - MLA_ROPE), BF16)], axis=1)
    q_w = ql + hb
    q_pad = (-q_w) % 512 if d >= 512 else (-q_w) % d
    gate_blk = (q_w + q_pad) // d
    assert (q_w + q_pad) % d == 0
    w_qg_p = jnp.concatenate([w_in_f[:, o_q:o_g], jnp.zeros((d, q_pad), BF16), w_in_f[:, o_g:]], axis=1)
    w_cat = jnp.concatenate([w_kv_p, w_qg_p], axis=1)

    wq_f = cols_full(gathered["w_q_up"]).reshape(ql, ha, MLA_NOPE + MLA_ROPE)
    wq_ext = jnp.pad(wq_f, ((0, 0), (0, 0), (0, MLA_SLOT - MLA_NOPE - MLA_ROPE))).reshape(ql, ha * MLA_SLOT)
    wkv_f = cols_full(gathered["w_kv_up"]).reshape(kvl, ha, MLA_NOPE + MLA_V)
    wk_slots = jnp.pad(wkv_f[:, :, :MLA_NOPE], ((0, 0), (0, 0), (0, MLA_SLOT - MLA_NOPE))).reshape(kvl, ha * MLA_SLOT)
    wv_cols = wkv_f[:, :, MLA_NOPE:].reshape(kvl, ha * MLA_V)
    e_slot = jnp.pad(jnp.eye(MLA_ROPE, dtype=BF16),
                     ((0, LANE - MLA_ROPE), (MLA_NOPE, MLA_SLOT - MLA_NOPE - MLA_ROPE)))
    e_rows = jnp.concatenate([jnp.tile(e_slot, (1, ha)), jnp.zeros((LANE, ha * MLA_V), BF16)], axis=1)
    wkv_ext = jnp.concatenate([jnp.concatenate([wk_slots, wv_cols], axis=1), e_rows], axis=0)
    w_bra = cols_full(gathered["w_br_a"])
    w_brb = cols_full(gathered["w_br_b"])
    w_out_f = gathered["w_out"].reshape(d, d)
    w_up_f = cols_full(gathered["w_up"])
    w_down_f = gathered["w_down"].reshape(ff, d)
    conv_w_f = None

    (cw_all,) = _all_gather([jnp.pad(conv_w[0], ((0, 5), (0, 0)))], "gather_conv_w")
    conv_w_f = jnp.transpose(cw_all[:, :3, :], (1, 0, 2)).reshape(3, f2)

    ca, s1a, s2a = _rope_tabs(t, MLA_ROPE)
    cb_, s1b, s2b = _rope_tabs(t, GQA_HEAD_DIM)
    q_tabs_a = (_pad_cols(jnp.concatenate([jnp.ones((t, MLA_NOPE), F32), ca], 1), 0, MLA_SLOT),
                _pad_cols(s1a, MLA_NOPE, MLA_SLOT), _pad_cols(s2a, MLA_NOPE, MLA_SLOT))
    q_tabs_b = (cb_, s1b, s2b)
    k_tabs = (_with_ctx_rows(_pad_cols(ca, 0, LANE), tc, 1.0), _with_ctx_rows(_pad_cols(s1a, 0, LANE), tc, 0.0),
              _with_ctx_rows(_pad_cols(s2a, 0, LANE), tc, 0.0),
              _with_ctx_rows(cb_, tc, 1.0), _with_ctx_rows(s1b, tc, 0.0), _with_ctx_rows(s2b, tc, 0.0))

    z_all = _norm_mod_fwd(cts, xs, norm1_g, mods1)
    kv_all = _mm(z_all, w_kv_p, "nn", F32, "proj_kv", tm=1152, tn=wkv_w)
    qg = _mm(z_all, w_qg_p, "nn", F32, "proj_qg", tm=1024, tn=1024, rows=t)
    kin, k_b, v_b = _key_prep_fwd(kv_all, mla_kv_norm_g, gqa_k_norm_g, k_tabs)
    kv_a = _mm(kin, wkv_ext, "nn", BF16, "kv_up", tm=1152, tn=1024)
    cqn, q_b = _q_prep_fwd(qg, mla_q_norm_g, gqa_q_norm_g, q_tabs_b)
    qa_raw = _mm(cqn, wq_ext, "nn", F32, "q_up", tm=1024, tn=1024)
    q_a = _rope_a(qa_raw, q_tabs_a, False, BF16, "rope_q_fwd")
    sc_a = float((MLA_NOPE + MLA_ROPE) ** -0.5)
    sc_b = float(GQA_HEAD_DIM ** -0.5)
    att_a = dict(hq=ha, hkv=ha, dk=MLA_SLOT, dv=MLA_V, k_blk0=0, v_blk0=ha * MLA_SLOT // MLA_V)
    att_b = dict(hq=GQA_HEADS, hkv=GQA_KV_HEADS, dk=GQA_HEAD_DIM, dv=GQA_HEAD_DIM, k_blk0=0, v_blk0=0)
    o_a, lse_a = _attention_fwd(q_a, kv_a, kv_a, sc_a, name="attn_a_fwd", **att_a)
    o_b, lse_b = _attention_fwd(q_b, k_b, v_b, sc_b, name="attn_b_fwd", **att_b)
    pa = _mm(o_a, w_bra, "nn", F32, "br_a", tm=1024, tn=1024)
    pb = _mm(o_b, w_brb, "nn", F32, "br_b", tm=1024, tn=1024)
    merged = _merge_fwd(pa, pb, qg, gate_blk)
    attn = _mm(merged, w_out_f, "nn", F32, "w_out", tm=1024, tn=1024)
    x1, z2 = _resid_norm_mod(xs, attn, norm2_g, mods2, "resid_norm2_fwd")
    u = _mm(z2, w_up_f, "nn", F32, "w_up", tm=1024, tn=1024)
    h = _conv_fwd(u, conv_w_f, conv_b)
    ffn = _mm(h, w_down_f, "nn", F32, "w_down", tm=1024, tn=1024, tk=2816)

    dx2, dffn, st_fin = _final_loss(x1, ffn, final_norm_g[None, :], mods3, tgt)
    loss = lax.psum(st_fin[3, 0], MESH_AXES)
    dh = _mm(dffn, w_down_f, "nt", F32, "d_h", tm=1024, tn=1024)
    g_w_down = _mm(h, dffn, "tn", BF16, "g_w_down", tm=512, tn=1024)
    du3, dcw, dcb = _conv_bwd(u, conv_w_f, conv_b, dh)
    du = jnp.concatenate([du3[0], du3[1]], axis=1)
    dz2 = _mm(du, w_up_f, "nt", F32, "d_z2", tm=1024, tn=1024, tk=2816)
    g_w_up = _mm(z2, du, "tn", BF16, "g_w_up", tm=1024, tn=1024)
    dx1, dattn, st_n2 = _norm2_bwd(x1, attn, norm2_g, mods2b, dz2, dx2)
    dmerged = _mm(dattn, w_out_f, "nt", F32, "d_merged", tm=1024, tn=1024)
    g_w_out = _mm(merged, dattn, "tn", BF16, "g_w_out", tm=1024, tn=1024)
    dpa, dpb, dgates = _merge_bwd(dmerged, pa, pb, qg, gate_blk)
    do_a = _mm(dpa, w_bra, "nt", BF16, "d_o_a", tm=1024, tn=1024)
    do_b = _mm(dpb, w_brb, "nt", BF16, "d_o_b", tm=1024, tn=1024)
    g_w_bra = _mm(o_a, dpa, "tn", BF16, "g_w_br_a", tm=1024, tn=1024)
    g_w_brb = _mm(o_b, dpb, "tn", BF16, "g_w_br_b", tm=1024, tn=1024)
    dq_a, dk_a, dv_a = _attention_bwd(q_a, kv_a, kv_a, do_a, lse_a, sc_a, name="attn_a_bwd", **att_a)
    dq_b, dk_b, dv_b = _attention_bwd(q_b, k_b, v_b, do_b, lse_b, sc_b, name="attn_b_bwd", **att_b)
    dqa_raw = _rope_a(dq_a, q_tabs_a, True, BF16, "rope_q_bwd")
    dcqn = _mm(dqa_raw, wq_ext, "nt", F32, "d_cqn", tm=1024, tn=ql)
    g_wq_ext = _mm(cqn, dqa_raw, "tn", BF16, "g_w_q_up", tm=ql, tn=1024)
    dq_p, st_q, st_qb = _q_prep_bwd(qg, mla_q_norm_g, gqa_q_norm_g, q_tabs_b, dcqn, dq_b, q_pad)
    dkv_a = jnp.concatenate([dk_a.astype(BF16), dv_a.astype(BF16)], axis=1)
    dkin = _mm(dkv_a, wkv_ext, "nt", F32, "d_kin", tm=1152, tn=kvl + LANE, tk=3072)
    g_wkv_ext = _mm(kin, dkv_a, "tn", BF16, "g_w_kv_up", tm=kvl + LANE, tn=1024)
    dkv_p, st_kv, st_kb = _key_prep_bwd(kv_all, mla_kv_norm_g, gqa_k_norm_g, k_tabs, dkin, dk_b, dv_b)
    dqg = jnp.concatenate([dq_p, dgates], axis=1)
    dcat = jnp.concatenate([dkv_p[:t], dqg], axis=1)
    dz_lat = _mm(dcat, w_cat, "nt", F32, "d_z_lat", tm=1024, tn=1024, tk=2432)
    dz_ctx = _mm(dkv_p, w_kv_p, "nt", F32, "d_z_ctx", tm=min(ROW_BLOCK, tc), tn=1024, a_row_off=t)
    g_wkv_p = _mm(z_all, dkv_p, "tn", BF16, "g_w_in_kv", tm=1024, tn=wkv_w)
    g_wqg_p = _mm(z_all, dqg, "tn", BF16, "g_w_in_qg", tm=1024, tn=1024, rows=t)
    grad_x, st_n1 = _norm1_bwd(cts, xs, norm1_g, mods1, dz_ctx, dz_lat, dx1)

    def to_shards(g):
        return jnp.transpose(g.reshape(g.shape[0], N_DEV, g.shape[1] // N_DEV), (1, 0, 2))

    g_w_in = jnp.concatenate([g_wkv_p[:, :kvl], g_wkv_p[:, kvl + 2 * nb:kvl + 2 * nb + MLA_ROPE],
                              g_wkv_p[:, kvl:kvl + 2 * nb], g_wqg_p[:, :q_w], g_wqg_p[:, q_w + q_pad:]], axis=1)
    g_wq = g_wq_ext.reshape(ql, ha, MLA_SLOT)[:, :, :MLA_NOPE + MLA_ROPE].reshape(ql, ha * (MLA_NOPE + MLA_ROPE))
    g_wkv = jnp.concatenate([g_wkv_ext[:kvl, :ha * MLA_SLOT].reshape(kvl, ha, MLA_SLOT)[:, :, :MLA_NOPE],
                             g_wkv_ext[:kvl, ha * MLA_SLOT:].reshape(kvl, ha, MLA_V)], axis=2).reshape(kvl, ha * (MLA_NOPE + MLA_V))
    g_conv_w = jnp.concatenate([dcw[0], dcw[1]], axis=1)
    send = [to_shards(g_w_in), to_shards(g_wq), to_shards(g_wkv), to_shards(g_w_bra), to_shards(g_w_brb),
            g_w_out.reshape(N_DEV, d // N_DEV, d), to_shards(g_w_up), g_w_down.reshape(N_DEV, ff // N_DEV, d),
            to_shards(jnp.pad(g_conv_w, ((0, 5), (0, 0))))]
    recv = _all_to_all(send, "exchange_grads")
    recv = dict(zip(["w_in", "w_q_up", "w_kv_up", "w_br_a", "w_br_b", "w_out", "w_up", "w_down", "conv_w"], recv))

    d_lat = jnp.concatenate([st_n1[0], st_n1[1], st_n2[3], st_n2[0], st_n2[1], st_fin[1]])
    d_cxt = jnp.concatenate([st_n1[3], st_n1[4], jnp.zeros((4 * d,), F32)])
    small = jnp.concatenate([d_lat, d_cxt, st_n1[2], st_q[0], st_kv[0], st_qb[0], st_kb[0], st_n2[2],
                             jnp.concatenate([dcb[0, 0], dcb[1, 0]]), st_fin[0]])
    n_small = small.shape[0]
    pad_small = (-n_small) % LANE
    (small_all,) = _all_gather([jnp.pad(small, (0, pad_small)).reshape(1, -1)], "gather_small")
    offs = {}
    o = 0
    for nm, ln in (("d_lat", 6 * d), ("d_cxt", 6 * d), ("norm1_g", d), ("mla_q_norm_g", ql), ("mla_kv_norm_g", kvl),
                   ("gqa_q_norm_g", GQA_HEAD_DIM), ("gqa_k_norm_g", GQA_HEAD_DIM), ("norm2_g", d), ("conv_b", f2),
                   ("final_norm_g", d)):
        offs[nm] = (o, ln)
        o += ln

    def part(nm):
        a, ln = offs[nm]
        return small_all[:, :, a:a + ln]

    d_lat_all = part("d_lat")[:, 0, :]
    d_cxt_sum = _sum_parts(part("d_cxt"))
    da16 = jnp.concatenate([d_lat_all, d_cxt_sum, jnp.zeros((7, 6 * d), F32)], axis=0)
    da16_shard = lax.dynamic_slice_in_dim(da16, my_idx * ncol, ncol, axis=1)
    cc_part = _cctx_partial(da16_shard, w_ada[0], c_ctx[None, :])
    (cc_all,) = _all_gather([cc_part], "gather_cctx")
    cc_parts = cc_all[:, 0:1, :]

    res = {}

    def upd(nm, parts, shape2):
        wv, mv, vv = (a.reshape(shape2) for a in (weights[nm], mom_m[nm], mom_v[nm]))
        outs = _adamw(parts, wv, mv, vv, "adamw_" + nm)
        res[nm] = [o_.reshape(weights[nm].shape) for o_ in outs]

    for nm in ("w_in", "w_q_up", "w_kv_up", "w_br_a", "w_br_b", "w_out", "w_up", "w_down"):
        upd(nm, recv[nm], weights[nm].shape[1:])
    upd("conv_w", recv["conv_w"][:, :3, :], conv_w.shape[1:])
    for nm in ("norm1_g", "mla_q_norm_g", "mla_kv_norm_g", "gqa_q_norm_g", "gqa_k_norm_g", "norm2_g", "conv_b",
               "final_norm_g"):
        upd(nm, part(nm), (1, offs[nm][1]))
    upd("c_ctx", cc_parts, (1, d))
    b_parts = jnp.concatenate([d_lat_all[:, None, :], d_cxt_sum[None]], axis=0)
    upd("b_ada", b_parts, (1, 6 * d))
    outs = _adamw_ada(conds, da16_shard, w_ada[0], m_w_ada[0], v_w_ada[0])
    res["w_ada"] = [o_[None] for o_ in outs]

    return (loss, grad_x[None], *[res[n][0] for n in order], *[res[n][1] for n in order],
            *[res[n][2] for n in order], *[res[n][3] for n in order])
```

```python
import functools

import jax
import jax.numpy as jnp
from jax import lax
from jax.experimental import pallas as pl
from jax.experimental.pallas import tpu as pltpu

F32 = jnp.float32
BF16 = jnp.bfloat16

GRID_W = 64
ROPE_THETA = 10000.0
NORM_EPS = 1e-6
MLA_HEADS = 8
MLA_Q_LORA = 768
MLA_KV_LORA = 512
MLA_NOPE = 128
MLA_ROPE = 64
MLA_V = 128
GQA_HEADS = 8
GQA_KV_HEADS = 2
GQA_HEAD_DIM = 128
ADAM_LR = 0.001
ADAM_B1 = 0.9
ADAM_B2 = 0.999
ADAM_EPS = 1e-08
ADAM_WD = 0.01
ADAM_STEP = 10

N_DEV = 8
MESH_AXES = ("x", "y", "c")
LANE = 128
MLA_SLOT = 2 * LANE
VMEM_LIMIT = 56 * 1024 * 1024
ROW_BLOCK = 256
ATT_Q_BLOCK = 256
MESH_ID = pl.DeviceIdType.MESH


def _tile(n, pref, align=LANE):
    if n <= pref:
        return n
    best = None
    t = align
    while t <= pref:
        if n % t == 0:
            best = t
        t += align
    assert best is not None, (n, pref, align)
    return best


def _cparams(sem=None):
    return pltpu.CompilerParams(dimension_semantics=sem, vmem_limit_bytes=VMEM_LIMIT)


_ORDER_AFTER = []


def _after(*arrays):
    _ORDER_AFTER.extend(arrays)


def _pcall(body, *, in_specs, **kw):
    deps = tuple(_ORDER_AFTER)
    _ORDER_AFTER.clear()
    if not deps:
        return pl.pallas_call(body, in_specs=in_specs, **kw)
    n_in, n_dep = len(in_specs), len(deps)

    def with_deps(*refs):
        body(*refs[:n_in], *refs[n_in + n_dep:])

    call = pl.pallas_call(with_deps, in_specs=list(in_specs) + [pl.BlockSpec(memory_space=pl.ANY)] * n_dep, **kw)
    return lambda *args: call(*args, *deps)


def _all_gather(arrs, name):
    n = len(arrs)

    def body(*refs):
        ins = refs[:n]
        outs = refs[n:2 * n]
        send_sems, recv_sems, local_sems = refs[2 * n:]
        x, y, c = lax.axis_index("x"), lax.axis_index("y"), lax.axis_index("c")
        me, sibling = (x, y, c), (x, y, 1 - c)
        chips = [(1 - x, y), (x, 1 - y), (1 - x, 1 - y)]

        def rows(a, dev):
            px, py, pc = dev
            return outs[a].at[4 * px + 2 * py + pc]

        def copy(a, k, block, to, src=None):
            return pltpu.make_async_remote_copy(
                src_ref=rows(a, block) if src is None else src,
                dst_ref=rows(a, block),
                send_sem=send_sems.at[7 * a + k],
                recv_sem=recv_sems.at[7 * a + k],
                device_id=to,
                device_id_type=MESH_ID,
            )

        mine = [pltpu.make_async_copy(ins[a], rows(a, me), local_sems.at[a]) for a in range(n)]
        for cp in mine:
            cp.start()
        first = []
        for a in range(n):
            first.append(copy(a, 0, me, sibling, src=ins[a]))
            first += [copy(a, 1 + j, me, (*chip, c), src=ins[a]) for j, chip in enumerate(chips)]
        for cp in first:
            cp.start()
        passed = []
        for j, chip in enumerate(chips):
            for a in range(n):
                copy(a, 1 + j, (*chip, c), me).wait_recv()
                fwd = copy(a, 4 + j, (*chip, c), sibling)
                fwd.start()
                passed.append(fwd)
        for a in range(n):
            copy(a, 0, sibling, me).wait_recv()
            for j, chip in enumerate(chips):
                copy(a, 4 + j, (*chip, 1 - c), me).wait_recv()
        for cp in first + passed:
            cp.wait_send()
        for cp in mine:
            cp.wait()

    any_spec = pl.BlockSpec(memory_space=pl.ANY)
    outs = _pcall(
        body,
        name=name,
        out_shape=[jax.ShapeDtypeStruct((N_DEV,) + a.shape, a.dtype) for a in arrs],
        in_specs=[any_spec] * n,
        out_specs=[any_spec] * n,
        scratch_shapes=[
            pltpu.SemaphoreType.DMA((7 * n,)),
            pltpu.SemaphoreType.DMA((7 * n,)),
            pltpu.SemaphoreType.DMA((n,)),
        ],
    )(*arrs)
    return list(outs)


def _all_to_all(arrs, name):
    n = len(arrs)

    def body(*refs):
        ins = refs[:n]
        outs = refs[n:2 * n]
        send_sems, recv_sems, local_sems = refs[2 * n:]
        x, y, c = lax.axis_index("x"), lax.axis_index("y"), lax.axis_index("c")
        my_idx = 4 * x + 2 * y + c

        def peer(k):
            fx, fy, fc = (k >> 2) & 1, (k >> 1) & 1, k & 1
            return (x ^ fx if fx else x, y ^ fy if fy else y, c ^ fc if fc else c)

        def copy(a, k):
            px, py, pc = peer(k)
            return pltpu.make_async_remote_copy(
                src_ref=ins[a].at[4 * px + 2 * py + pc],
                dst_ref=outs[a].at[my_idx],
                send_sem=send_sems.at[7 * a + k - 1],
                recv_sem=recv_sems.at[7 * a + k - 1],
                device_id=(px, py, pc),
                device_id_type=MESH_ID,
            )

        mine = [pltpu.make_async_copy(ins[a].at[my_idx], outs[a].at[my_idx], local_sems.at[a]) for a in range(n)]
        for cp in mine:
            cp.start()
        order = [1, 4, 2, 5, 3, 6, 7]
        cps = [copy(a, k) for k in order for a in range(n)]
        for cp in cps:
            cp.start()
        for cp in cps:
            cp.wait()
        for cp in mine:
            cp.wait()

    any_spec = pl.BlockSpec(memory_space=pl.ANY)
    outs = _pcall(
        body,
        name=name,
        out_shape=[jax.ShapeDtypeStruct(a.shape, a.dtype) for a in arrs],
        in_specs=[any_spec] * n,
        out_specs=[any_spec] * n,
        scratch_shapes=[
            pltpu.SemaphoreType.DMA((7 * n,)),
            pltpu.SemaphoreType.DMA((7 * n,)),
            pltpu.SemaphoreType.DMA((n,)),
        ],
    )(*arrs)
    return list(outs)


_HBM = pl.BlockSpec(memory_space=pltpu.HBM)
_SEM = pl.BlockSpec(memory_space=pltpu.SEMAPHORE)
_EFFECT = pltpu.SideEffectType.DATAFLOW_SIDE_EFFECTING


def _descriptors(copies, send_sems, recv_sems):
    descs = []
    for i, (src, dst, dev) in enumerate(copies):
        if dev is None:
            descs.append(pltpu.make_async_copy(src, dst, recv_sems.at[i]))
        else:
            descs.append(pltpu.make_async_remote_copy(src_ref=src, dst_ref=dst, send_sem=send_sems.at[i],
                                                      recv_sem=recv_sems.at[i], device_id=dev, device_id_type=MESH_ID))
    return descs


def _split_start(name, arrays, copies_fn, n_copies):
    n = len(arrays)

    def body(*refs):
        send_sems, recv_sems = refs[n], refs[n + 1]
        token = refs[2 * n + 2]
        for dsc in _descriptors(copies_fn(refs[:n]), send_sems, recv_sems):
            dsc.start()
        token[...] = jnp.zeros_like(token)

    outs = _pcall(
        body,
        name=name,
        out_shape=(pltpu.SemaphoreType.DMA((n_copies,)), pltpu.SemaphoreType.DMA((n_copies,)),
                   *[pltpu.HBM(a.shape, a.dtype) for a in arrays], jax.ShapeDtypeStruct((8, LANE), F32)),
        in_specs=[_HBM] * n,
        out_specs=(_SEM, _SEM, *[_HBM] * n, pl.BlockSpec(memory_space=pltpu.VMEM)),
        input_output_aliases={i: 2 + i for i in range(n)},
        compiler_params=pltpu.CompilerParams(has_side_effects=_EFFECT),
    )(*[pltpu.with_memory_space_constraint(a, pltpu.HBM) for a in arrays])
    return outs[0], outs[1], list(outs[2:2 + n]), outs[2 + n]


def _split_wait(name, send_sems, recv_sems, arrays, copies_fn, after):
    n = len(arrays)

    def body(*refs):
        for dsc, (_, _, dev) in zip(_descriptors(copies_fn(refs[:n]), refs[n], refs[n + 1]), copies_fn(refs[:n])):
            if dev is None:
                dsc.wait()
            else:
                dsc.wait_send()
                dsc.wait_recv()

    outs = _pcall(
        body,
        name=name,
        out_shape=tuple(pltpu.HBM(a.shape, a.dtype) for a in arrays),
        in_specs=[_HBM] * n + [_SEM, _SEM, pl.BlockSpec(memory_space=pl.ANY)],
        out_specs=tuple([_HBM] * n),
        input_output_aliases={i: i for i in range(n)},
        compiler_params=pltpu.CompilerParams(has_side_effects=_EFFECT),
    )(*arrays, send_sems, recv_sems, after)
    return list(outs)


def _mesh_pos():
    x, y, c = lax.axis_index("x"), lax.axis_index("y"), lax.axis_index("c")
    return x, y, c, [(1 - x, y), (x, 1 - y), (1 - x, 1 - y)]


def _gather_ici_copies(n):
    def copies(refs):
        x, y, c, chips = _mesh_pos()
        me = 4 * x + 2 * y + c
        out = []
        for a in range(n):
            src, buf = refs[a], refs[n + a]
            out.append((src, buf.at[me], None))
            out.append((src, buf.at[me], (x, y, 1 - c)))
            out += [(src, buf.at[me], (cx, cy, c)) for cx, cy in chips]
        return out
    return copies


def _gather_d2d_copies(n):
    def copies(refs):
        x, y, c, chips = _mesh_pos()
        out = []
        for a in range(n):
            for cx, cy in chips:
                rows = refs[a].at[4 * cx + 2 * cy + c]
                out.append((rows, rows, (x, y, 1 - c)))
        return out
    return copies


def _reduce_d2d_copies(n):
    def copies(refs):
        x, y, c, _ = _mesh_pos()
        out = []
        for a in range(n):
            for k in range(4):
                out.append((refs[a].at[2 * k + (1 - c)], refs[n + a].at[k], (x, y, 1 - c)))
        return out
    return copies


def _reduce_ici_copies(n):
    def copies(refs):
        x, y, c, chips = _mesh_pos()
        mine = 2 * x + y
        out = []
        for a in range(n):
            src, land = refs[a], refs[n + a]
            out.append((src.at[mine], land.at[mine], None))
            out += [(src.at[2 * cx + cy], land.at[mine], (cx, cy, c)) for cx, cy in chips]
        return out
    return copies


def _pair_sum(send, land, c_idx, name):
    _, r, cols = send.shape
    rb = _tile(r, max(8, (1 << 20) // (2 * cols) // 8 * 8), 8)
    dt = send.dtype

    def body(c_ref, s_ref, l_ref, o_ref):
        o_ref[...] = (s_ref[...].astype(F32) + l_ref[...].astype(F32)).astype(dt)

    return pl.pallas_call(
        body,
        name=name,
        out_shape=jax.ShapeDtypeStruct((4, r, cols), dt),
        grid_spec=pltpu.PrefetchScalarGridSpec(
            num_scalar_prefetch=1,
            grid=(4, r // rb),
            in_specs=[pl.BlockSpec((None, rb, cols), lambda k, i, c_ref: (2 * k + c_ref[0], i, 0)),
                      pl.BlockSpec((None, rb, cols), lambda k, i, c_ref: (k, i, 0))],
            out_specs=pl.BlockSpec((None, rb, cols), lambda k, i, c_ref: (k, i, 0)),
        ),
        compiler_params=_cparams(("parallel", "parallel")),
    )(c_idx, send, land)


_DIMS = {
    "nn": (((1,), (0,)), ((), ())),
    "nt": (((1,), (1,)), ((), ())),
    "tn": (((0,), (0,)), ((), ())),
}


def _mm_call(a, b, *, mode, grid, a_spec, b_spec, o_spec, out_shape, acc_shape, name):
    nk = grid[2]
    out_dtype = out_shape.dtype

    def body(a_ref, b_ref, o_ref, *scratch):
        p = lax.dot_general(a_ref[...].astype(BF16), b_ref[...].astype(BF16), _DIMS[mode],
                            preferred_element_type=F32)
        if nk == 1:
            o_ref[...] = p.astype(out_dtype)
        else:
            acc = scratch[0]
            k = pl.program_id(2)

            @pl.when(k == 0)
            def _():
                acc[...] = p

            @pl.when(k > 0)
            def _():
                acc[...] += p

            @pl.when(k == nk - 1)
            def _():
                o_ref[...] = acc[...].astype(out_dtype)

    return _pcall(
        body,
        name=name,
        out_shape=out_shape,
        grid=grid,
        in_specs=[a_spec, b_spec],
        out_specs=o_spec,
        scratch_shapes=[pltpu.VMEM(acc_shape, F32)] if nk > 1 else [],
        compiler_params=_cparams(("parallel", "parallel", "arbitrary")),
    )(a, b)


def _mm(a, b, mode, out_dtype, name, tm=512, tn=512, tk=2432, a_row_off=0, rows=None):
    if mode == "nn":
        (m, k), (k2, n) = a.shape, b.shape
    elif mode == "nt":
        (m, k), (n, k2) = a.shape, b.shape
    else:
        (k, m), (k2, n) = a.shape, b.shape
        if rows is not None:
            k = k2 = rows
    assert k == k2, (a.shape, b.shape, mode)
    if mode != "tn":
        m = (m if rows is None else rows + a_row_off) - a_row_off
    tm, tn, tk = _tile(m, tm, 8), _tile(n, tn), _tile(k, tk, 8 if mode == "tn" else LANE)
    assert a_row_off % tm == 0
    ro = a_row_off // tm
    grid = (m // tm, n // tn, k // tk)
    if mode == "tn":
        a_spec = pl.BlockSpec((tk, tm), lambda i, j, kk: (kk, i))
    else:
        a_spec = pl.BlockSpec((tm, tk), lambda i, j, kk: (i + ro, kk))
    if mode == "nt":
        b_spec = pl.BlockSpec((tn, tk), lambda i, j, kk: (j, kk))
    else:
        b_spec = pl.BlockSpec((tk, tn), lambda i, j, kk: (kk, j))
    o_spec = pl.BlockSpec((tm, tn), lambda i, j, kk: (i, j))
    return _mm_call(a, b, mode=mode, grid=grid, a_spec=a_spec, b_spec=b_spec, o_spec=o_spec,
                    out_shape=jax.ShapeDtypeStruct((m, n), out_dtype), acc_shape=(tm, tn), name=name)


def _mm_up_fwd(z2, w3, name, tm=1024):
    t, d = z2.shape
    nsh, _, c = w3.shape
    tm = _tile(t, tm, 8)
    return _mm_call(z2, w3, mode="nn", grid=(t // tm, nsh, 1),
                    a_spec=pl.BlockSpec((tm, d), lambda i, j, kk: (i, 0)),
                    b_spec=pl.BlockSpec((None, d, c), lambda i, j, kk: (j, 0, 0)),
                    o_spec=pl.BlockSpec((tm, c), lambda i, j, kk: (i, j)),
                    out_shape=jax.ShapeDtypeStruct((t, nsh * c), F32), acc_shape=(tm, c), name=name)


def _mm_up_dz(du3, w3, name, tm=1024, tn=1024):
    _, t, f = du3.shape
    nsh, d, c = w3.shape
    half = nsh // 2
    assert f == half * c
    tm, tn = _tile(t, tm, 8), _tile(d, tn)
    return _mm_call(du3, w3, mode="nt", grid=(t // tm, d // tn, nsh),
                    a_spec=pl.BlockSpec((None, tm, c), lambda i, j, kk: (kk // half, i, kk % half)),
                    b_spec=pl.BlockSpec((None, tn, c), lambda i, j, kk: (kk, j, 0)),
                    o_spec=pl.BlockSpec((tm, tn), lambda i, j, kk: (i, j)),
                    out_shape=jax.ShapeDtypeStruct((t, d), F32), acc_shape=(tm, tn), name=name)


def _mm_up_gw(z2, du3, nsh, name, tm=1024):
    t, d = z2.shape
    f = du3.shape[2]
    half = nsh // 2
    c = f // half
    tm = _tile(d, tm)
    return _mm_call(z2, du3, mode="tn", grid=(d // tm, nsh, 1),
                    a_spec=pl.BlockSpec((t, tm), lambda i, j, kk: (0, i)),
                    b_spec=pl.BlockSpec((None, t, c), lambda i, j, kk: (j // half, 0, j % half)),
                    o_spec=pl.BlockSpec((None, tm, c), lambda i, j, kk: (j, i, 0)),
                    out_shape=jax.ShapeDtypeStruct((nsh, d, c), BF16), acc_shape=(tm, c), name=name)


def _rms(x):
    r = lax.rsqrt(jnp.mean(x * x, axis=-1, keepdims=True) + NORM_EPS)
    return x * r, r


def _rms_bwd(dxh, xh, r):
    return r * (dxh - xh * jnp.mean(dxh * xh, axis=-1, keepdims=True))


def _colsum(v):
    return jnp.sum(v, axis=0, keepdims=True)


def _rope(v, c, s1, s2, q):
    w = v.shape[-1]
    return v * c + pltpu.roll(v, w - q, 1) * s1 + pltpu.roll(v, q, 1) * s2


def _rope_t(d, c, s1, s2, q):
    w = d.shape[-1]
    return d * c + pltpu.roll(d * s1, q, 1) + pltpu.roll(d * s2, w - q, 1)


def _norm_mod_fwd(ctx, x, gain, mods):
    tc, d = ctx.shape
    t = x.shape[0]
    rb = min(ROW_BLOCK, tc)
    nbl = t // rb

    def body(ctx_ref, x_ref, g_ref, mod_ref, z_ref):
        i = pl.program_id(0)

        def emit(src, sh, sc):
            xh, _ = _rms(src[...])
            z_ref[...] = ((xh * g_ref[...]) * (1.0 + sc) + sh).astype(BF16)

        @pl.when(i >= nbl)
        def _():
            emit(ctx_ref, mod_ref[2:3, :], mod_ref[3:4, :])

        @pl.when(i < nbl)
        def _():
            emit(x_ref, mod_ref[0:1, :], mod_ref[1:2, :])

    return _pcall(
        body,
        name="norm1_mod_fwd",
        out_shape=jax.ShapeDtypeStruct((tc + t, d), BF16),
        grid=((tc + t) // rb,),
        in_specs=[
            pl.BlockSpec((rb, d), lambda i: (jnp.maximum(i - nbl, 0), 0)),
            pl.BlockSpec((rb, d), lambda i: (jnp.minimum(i, nbl - 1), 0)),
            pl.BlockSpec((1, d), lambda i: (0, 0)),
            pl.BlockSpec((8, d), lambda i: (0, 0)),
        ],
        out_specs=pl.BlockSpec((rb, d), lambda i: (i, 0)),
        compiler_params=_cparams(("arbitrary",)),
    )(ctx, x, gain, mods)


def _norm1_bwd(ctx, x, gain, mods, dz_ctx, dz_lat, dx1):
    tc, d = ctx.shape
    t = x.shape[0]
    rb = min(ROW_BLOCK, tc)
    nbl = t // rb

    def body(ctx_ref, x_ref, g_ref, mod_ref, dzc_ref, dzl_ref, dx1_ref, gx_ref, st_ref):
        i = pl.program_id(0)

        @pl.when(i == 0)
        def _():
            st_ref[...] = jnp.zeros_like(st_ref)

        def common(src, dz, sc, row_sh, row_sc):
            xh, r = _rms(src[...])
            g = g_ref[...]
            dxn = dz * (1.0 + sc)
            st_ref[row_sh:row_sh + 1, :] += _colsum(dz)
            st_ref[row_sc:row_sc + 1, :] += _colsum(dz * (xh * g))
            st_ref[2:3, :] += _colsum(dxn * xh)
            return _rms_bwd(dxn * g, xh, r)

        @pl.when(i >= nbl)
        def _():
            common(ctx_ref, dzc_ref[...], mod_ref[3:4, :], 3, 4)

        @pl.when(i < nbl)
        def _():
            gx_ref[...] = dx1_ref[...] + common(x_ref, dzl_ref[...], mod_ref[1:2, :], 0, 1)

    lat = lambda i: (jnp.minimum(i, nbl - 1), 0)
    cix = lambda i: (jnp.maximum(i - nbl, 0), 0)
    return _pcall(
        body,
        name="norm1_mod_bwd",
        out_shape=[jax.ShapeDtypeStruct((t, d), F32), jax.ShapeDtypeStruct((8, d), F32)],
        grid=((tc + t) // rb,),
        in_specs=[
            pl.BlockSpec((rb, d), cix),
            pl.BlockSpec((rb, d), lat),
            pl.BlockSpec((1, d), lambda i: (0, 0)),
            pl.BlockSpec((8, d), lambda i: (0, 0)),
            pl.BlockSpec((rb, d), cix),
            pl.BlockSpec((rb, d), lat),
            pl.BlockSpec((rb, d), lat),
        ],
        out_specs=[pl.BlockSpec((rb, d), lat), pl.BlockSpec((8, d), lambda i: (0, 0))],
        compiler_params=_cparams(("arbitrary",)),
    )(ctx, x, gain, mods, dz_ctx, dz_lat, dx1)


def _key_prep_fwd(kv, kv_gain, kb_gain, tabs):
    ta, wkv = kv.shape
    kvl = MLA_KV_LORA
    nb = GQA_KV_HEADS * GQA_HEAD_DIM
    rb = ROW_BLOCK if ta % ROW_BLOCK == 0 else LANE
    hd = GQA_HEAD_DIM

    def body(kv_ref, g_ref, gb_ref, ca, s1a, s2a, cb, s1b, s2b, kin_ref, kb_ref, vb_ref):
        xh, _ = _rms(kv_ref[:, 0:kvl])
        kin_ref[:, 0:kvl] = (xh * g_ref[...]).astype(BF16)
        kpe = kv_ref[:, kvl + 2 * nb:kvl + 2 * nb + LANE]
        kin_ref[:, kvl:kvl + LANE] = _rope(kpe, ca[...], s1a[...], s2a[...], MLA_ROPE // 4).astype(BF16)
        for h in range(GQA_KV_HEADS):
            nh, _ = _rms(kv_ref[:, kvl + h * hd:kvl + (h + 1) * hd])
            kb_ref[:, h * hd:(h + 1) * hd] = _rope(nh * gb_ref[...], cb[...], s1b[...], s2b[...], hd // 4).astype(BF16)
        vb_ref[...] = kv_ref[:, kvl + nb:kvl + 2 * nb].astype(BF16)

    row = lambda w: pl.BlockSpec((rb, w), lambda i: (i, 0))
    fix = lambda w: pl.BlockSpec((1, w), lambda i: (0, 0))
    return _pcall(
        body,
        name="key_prep_fwd",
        out_shape=[jax.ShapeDtypeStruct((ta, kvl + LANE), BF16), jax.ShapeDtypeStruct((ta, nb), BF16),
                   jax.ShapeDtypeStruct((ta, nb), BF16)],
        grid=(ta // rb,),
        in_specs=[row(wkv), fix(kvl), fix(hd)] + [row(LANE)] * 3 + [row(hd)] * 3,
        out_specs=[row(kvl + LANE), row(nb), row(nb)],
        compiler_params=_cparams(("parallel",)),
    )(kv, kv_gain, kb_gain, *tabs)


def _key_prep_bwd(kv, kv_gain, kb_gain, tabs, dkin, dkb, dvb):
    ta, wkv = kv.shape
    kvl = MLA_KV_LORA
    nb = GQA_KV_HEADS * GQA_HEAD_DIM
    rb = ROW_BLOCK if ta % ROW_BLOCK == 0 else LANE
    hd = GQA_HEAD_DIM

    def body(kv_ref, g_ref, gb_ref, ca, s1a, s2a, cb, s1b, s2b, dkin_ref, dkb_ref, dvb_ref, dkv_ref, st_ref, stb_ref):
        @pl.when(pl.program_id(0) == 0)
        def _():
            st_ref[...] = jnp.zeros_like(st_ref)
            stb_ref[...] = jnp.zeros_like(stb_ref)

        xh, r = _rms(kv_ref[:, 0:kvl])
        dn = dkin_ref[:, 0:kvl]
        st_ref[0:1, :] += _colsum(dn * xh)
        dkv_ref[:, 0:kvl] = _rms_bwd(dn * g_ref[...], xh, r).astype(BF16)
        dpe = _rope_t(dkin_ref[:, kvl:kvl + LANE], ca[...], s1a[...], s2a[...], MLA_ROPE // 4)
        dkv_ref[:, kvl + 2 * nb:kvl + 2 * nb + LANE] = dpe.astype(BF16)
        for h in range(GQA_KV_HEADS):
            nh, rh = _rms(kv_ref[:, kvl + h * hd:kvl + (h + 1) * hd])
            dn_h = _rope_t(dkb_ref[:, h * hd:(h + 1) * hd], cb[...], s1b[...], s2b[...], hd // 4)
            stb_ref[0:1, :] += _colsum(dn_h * nh)
            dkv_ref[:, kvl + h * hd:kvl + (h + 1) * hd] = _rms_bwd(dn_h * gb_ref[...], nh, rh).astype(BF16)
        dkv_ref[:, kvl + nb:kvl + 2 * nb] = dvb_ref[...].astype(BF16)

    row = lambda w: pl.BlockSpec((rb, w), lambda i: (i, 0))
    fix = lambda w: pl.BlockSpec((1, w), lambda i: (0, 0))
    return _pcall(
        body,
        name="key_prep_bwd",
        out_shape=[jax.ShapeDtypeStruct((ta, wkv), BF16), jax.ShapeDtypeStruct((8, kvl), F32),
                   jax.ShapeDtypeStruct((8, hd), F32)],
        grid=(ta // rb,),
        in_specs=[row(wkv), fix(kvl), fix(hd)] + [row(LANE)] * 3 + [row(hd)] * 3 + [row(kvl + LANE), row(nb), row(nb)],
        out_specs=[row(wkv), pl.BlockSpec((8, kvl), lambda i: (0, 0)), pl.BlockSpec((8, hd), lambda i: (0, 0))],
        compiler_params=_cparams(("arbitrary",)),
    )(kv, kv_gain, kb_gain, *tabs, dkin, dkb, dvb)


def _q_prep_fwd(qg, q_gain, qb_gain, tabs):
    t = qg.shape[0]
    ql = MLA_Q_LORA
    hd = GQA_HEAD_DIM
    hb = GQA_HEADS * hd
    rb = min(ROW_BLOCK, t)

    def body(q_ref, g_ref, gb_ref, cb, s1b, s2b, cqn_ref, qb_ref):
        xh, _ = _rms(q_ref[:, 0:ql])
        cqn_ref[...] = (xh * g_ref[...]).astype(BF16)
        for h in range(GQA_HEADS):
            nh, _ = _rms(q_ref[:, ql + h * hd:ql + (h + 1) * hd])
            qb_ref[:, h * hd:(h + 1) * hd] = _rope(nh * gb_ref[...], cb[...], s1b[...], s2b[...], hd // 4).astype(BF16)

    row = lambda w: pl.BlockSpec((rb, w), lambda i: (i, 0))
    fix = lambda w: pl.BlockSpec((1, w), lambda i: (0, 0))
    return _pcall(
        body,
        name="q_prep_fwd",
        out_shape=[jax.ShapeDtypeStruct((t, ql), BF16), jax.ShapeDtypeStruct((t, hb), BF16)],
        grid=(t // rb,),
        in_specs=[row(ql + hb), fix(ql), fix(hd)] + [row(hd)] * 3,
        out_specs=[row(ql), row(hb)],
        compiler_params=_cparams(("parallel",)),
    )(qg, q_gain, qb_gain, *tabs)


def _q_prep_bwd(qg, q_gain, qb_gain, tabs, dcqn, dqb, wpad):
    t = qg.shape[0]
    ql = MLA_Q_LORA
    hd = GQA_HEAD_DIM
    hb = GQA_HEADS * hd
    rb = min(ROW_BLOCK, t)

    def body(q_ref, g_ref, gb_ref, cb, s1b, s2b, dcqn_ref, dqb_ref, dq_ref, st_ref, stb_ref):
        @pl.when(pl.program_id(0) == 0)
        def _():
            st_ref[...] = jnp.zeros_like(st_ref)
            stb_ref[...] = jnp.zeros_like(stb_ref)

        xh, r = _rms(q_ref[:, 0:ql])
        dn = dcqn_ref[...]
        st_ref[0:1, :] += _colsum(dn * xh)
        dq_ref[:, 0:ql] = _rms_bwd(dn * g_ref[...], xh, r).astype(BF16)
        for h in range(GQA_HEADS):
            nh, rh = _rms(q_ref[:, ql + h * hd:ql + (h + 1) * hd])
            dn_h = _rope_t(dqb_ref[:, h * hd:(h + 1) * hd], cb[...], s1b[...], s2b[...], hd // 4)
            stb_ref[0:1, :] += _colsum(dn_h * nh)
            dq_ref[:, ql + h * hd:ql + (h + 1) * hd] = _rms_bwd(dn_h * gb_ref[...], nh, rh).astype(BF16)
        if wpad:
            dq_ref[:, ql + hb:ql + hb + wpad] = jnp.zeros((rb, wpad), BF16)

    row = lambda w: pl.BlockSpec((rb, w), lambda i: (i, 0))
    fix = lambda w: pl.BlockSpec((1, w), lambda i: (0, 0))
    return _pcall(
        body,
        name="q_prep_bwd",
        out_shape=[jax.ShapeDtypeStruct((t, ql + hb + wpad), BF16), jax.ShapeDtypeStruct((8, ql), F32),
                   jax.ShapeDtypeStruct((8, hd), F32)],
        grid=(t // rb,),
        in_specs=[row(ql + hb), fix(ql), fix(hd)] + [row(hd)] * 3 + [row(ql), row(hb)],
        out_specs=[row(ql + hb + wpad), pl.BlockSpec((8, ql), lambda i: (0, 0)), pl.BlockSpec((8, hd), lambda i: (0, 0))],
        compiler_params=_cparams(("arbitrary",)),
    )(qg, q_gain, qb_gain, *tabs, dcqn, dqb)


def _rope_a(v, tabs, transpose, out_dtype, name):
    t, w = v.shape
    rb = min(ROW_BLOCK, t)
    fn = _rope_t if transpose else _rope

    def body(v_ref, c, s1, s2, o_ref):
        for h in range(w // MLA_SLOT):
            sl = slice(h * MLA_SLOT, (h + 1) * MLA_SLOT)
            o_ref[:, sl] = fn(v_ref[:, sl].astype(F32), c[...], s1[...], s2[...], MLA_ROPE // 4).astype(out_dtype)

    row = lambda ww: pl.BlockSpec((rb, ww), lambda i: (i, 0))
    return _pcall(
        body,
        name=name,
        out_shape=jax.ShapeDtypeStruct((t, w), out_dtype),
        grid=(t // rb,),
        in_specs=[row(w)] + [row(MLA_SLOT)] * 3,
        out_specs=row(w),
        compiler_params=_cparams(("parallel",)),
    )(v, *tabs)


def _merge_fwd(pa, pb, qg, gate_blk):
    t, d = pa.shape
    rb = min(ROW_BLOCK, t)

    def body(pa_ref, pb_ref, ga_ref, gb_ref, o_ref):
        o_ref[...] = (jax.nn.sigmoid(ga_ref[...]) * pa_ref[...] + jax.nn.sigmoid(gb_ref[...]) * pb_ref[...]).astype(BF16)

    row = pl.BlockSpec((rb, d), lambda i: (i, 0))
    return _pcall(
        body,
        name="merge_fwd",
        out_shape=jax.ShapeDtypeStruct((t, d), BF16),
        grid=(t // rb,),
        in_specs=[row, row, pl.BlockSpec((rb, d), lambda i: (i, gate_blk)), pl.BlockSpec((rb, d), lambda i: (i, gate_blk + 1))],
        out_specs=row,
        compiler_params=_cparams(("parallel",)),
    )(pa, pb, qg, qg)


def _merge_bwd(dm, pa, pb, qg, gate_blk):
    t, d = pa.shape
    rb = min(ROW_BLOCK, t)

    def body(dm_ref, pa_ref, pb_ref, ga_ref, gb_ref, dpa_ref, dpb_ref, dg_ref):
        dmv = dm_ref[...]
        sa = jax.nn.sigmoid(ga_ref[...])
        sb = jax.nn.sigmoid(gb_ref[...])
        dpa_ref[...] = (dmv * sa).astype(BF16)
        dpb_ref[...] = (dmv * sb).astype(BF16)
        dg_ref[:, 0:d] = (dmv * pa_ref[...] * (sa * (1.0 - sa))).astype(BF16)
        dg_ref[:, d:2 * d] = (dmv * pb_ref[...] * (sb * (1.0 - sb))).astype(BF16)

    row = pl.BlockSpec((rb, d), lambda i: (i, 0))
    return _pcall(
        body,
        name="merge_bwd",
        out_shape=[jax.ShapeDtypeStruct((t, d), BF16), jax.ShapeDtypeStruct((t, d), BF16),
                   jax.ShapeDtypeStruct((t, 2 * d), BF16)],
        grid=(t // rb,),
        in_specs=[row, row, row, pl.BlockSpec((rb, d), lambda i: (i, gate_blk)), pl.BlockSpec((rb, d), lambda i: (i, gate_blk + 1))],
        out_specs=[row, row, pl.BlockSpec((rb, 2 * d), lambda i: (i, 0))],
        compiler_params=_cparams(("parallel",)),
    )(dm, pa, pb, qg, qg)


def _resid_norm_mod(x, branch, gain, mods, name):
    t, d = x.shape
    rb = min(ROW_BLOCK, t)

    def body(x_ref, b_ref, g_ref, mod_ref, x1_ref, z_ref):
        x1 = x_ref[...] + mod_ref[0:1, :] * b_ref[...]
        x1_ref[...] = x1
        xh, _ = _rms(x1)
        z_ref[...] = ((xh * g_ref[...]) * (1.0 + mod_ref[2:3, :]) + mod_ref[1:2, :]).astype(BF16)

    row = pl.BlockSpec((rb, d), lambda i: (i, 0))
    return _pcall(
        body,
        name=name,
        out_shape=[jax.ShapeDtypeStruct((t, d), F32), jax.ShapeDtypeStruct((t, d), BF16)],
        grid=(t // rb,),
        in_specs=[row, row, pl.BlockSpec((1, d), lambda i: (0, 0)), pl.BlockSpec((8, d), lambda i: (0, 0))],
        out_specs=[row, row],
        compiler_params=_cparams(("parallel",)),
    )(x, branch, gain, mods)


def _norm2_bwd(x1, attn, gain, mods, dz2, dx2):
    t, d = x1.shape
    rb = min(ROW_BLOCK, t)

    def body(x1_ref, at_ref, g_ref, mod_ref, dz_ref, dx2_ref, dx1_ref, da_ref, st_ref):
        @pl.when(pl.program_id(0) == 0)
        def _():
            st_ref[...] = jnp.zeros_like(st_ref)

        xh, r = _rms(x1_ref[...])
        g = g_ref[...]
        dz = dz_ref[...]
        dxn = dz * (1.0 + mod_ref[1:2, :])
        st_ref[0:1, :] += _colsum(dz)
        st_ref[1:2, :] += _colsum(dz * (xh * g))
        st_ref[2:3, :] += _colsum(dxn * xh)
        dx1 = dx2_ref[...] + _rms_bwd(dxn * g, xh, r)
        dx1_ref[...] = dx1
        st_ref[3:4, :] += _colsum(dx1 * at_ref[...])
        da_ref[...] = (dx1 * mod_ref[0:1, :]).astype(BF16)

    row = pl.BlockSpec((rb, d), lambda i: (i, 0))
    return _pcall(
        body,
        name="norm2_mod_bwd",
        out_shape=[jax.ShapeDtypeStruct((t, d), F32), jax.ShapeDtypeStruct((t, d), BF16), jax.ShapeDtypeStruct((8, d), F32)],
        grid=(t // rb,),
        in_specs=[row, row, pl.BlockSpec((1, d), lambda i: (0, 0)), pl.BlockSpec((8, d), lambda i: (0, 0)), row, row],
        out_specs=[row, row, pl.BlockSpec((8, d), lambda i: (0, 0))],
        compiler_params=_cparams(("arbitrary",)),
    )(x1, attn, gain, mods, dz2, dx2)


def _final_loss(x1, ffn, gain, mods, target):
    t, d = x1.shape
    rb = min(ROW_BLOCK, t)
    nb = t // rb

    def body(x1_ref, f_ref, g_ref, mod_ref, tg_ref, dx2_ref, df_ref, st_ref):
        i = pl.program_id(0)

        @pl.when(i == 0)
        def _():
            st_ref[...] = jnp.zeros_like(st_ref)

        ffn_v = f_ref[...]
        g2 = mod_ref[0:1, :]
        x2 = x1_ref[...] + g2 * ffn_v
        xh, r = _rms(x2)
        g = g_ref[...]
        err = xh * g - tg_ref[...]
        st_ref[2:3, :] += _colsum(err * err) * (0.5 / d)
        dy = err * (1.0 / d)
        st_ref[0:1, :] += _colsum(dy * xh)
        dx2 = _rms_bwd(dy * g, xh, r)
        dx2_ref[...] = dx2
        st_ref[1:2, :] += _colsum(dx2 * ffn_v)
        df_ref[...] = (dx2 * g2).astype(BF16)

        @pl.when(i == nb - 1)
        def _():
            st_ref[3:4, :] = jnp.broadcast_to(jnp.sum(st_ref[2:3, :], axis=-1, keepdims=True), (1, d))

    row = pl.BlockSpec((rb, d), lambda i: (i, 0))
    return _pcall(
        body,
        name="final_norm_loss",
        out_shape=[jax.ShapeDtypeStruct((t, d), F32), jax.ShapeDtypeStruct((t, d), BF16), jax.ShapeDtypeStruct((8, d), F32)],
        grid=(nb,),
        in_specs=[row, row, pl.BlockSpec((1, d), lambda i: (0, 0)), pl.BlockSpec((8, d), lambda i: (0, 0)), row],
        out_specs=[row, row, pl.BlockSpec((8, d), lambda i: (0, 0))],
        compiler_params=_cparams(("arbitrary",)),
    )(x1, ffn, gain, mods, target)


def _shift_rows(v, down):
    n = v.shape[0]
    rows = lax.broadcasted_iota(jnp.int32, v.shape, 0)
    if down:
        return jnp.where(rows == 0, 0.0, pltpu.roll(v, 1, 0))
    return jnp.where(rows == n - 1, 0.0, pltpu.roll(v, n - 1, 0))


def _conv_act(ua, ub, cwa, cwb, cba, cbb):
    a = cba + cwa[0:1, :] * _shift_rows(ua, True) + cwa[1:2, :] * ua + cwa[2:3, :] * _shift_rows(ua, False)
    b = cbb + cwb[0:1, :] * _shift_rows(ub, True) + cwb[1:2, :] * ub + cwb[2:3, :] * _shift_rows(ub, False)
    return a, b


def _conv_fwd(u, cw, cb):
    t, f2 = u.shape
    f = f2 // 2
    cbk = _tile(f, 256)
    nf = f // cbk

    def body(ua_ref, ub_ref, cwa_ref, cwb_ref, cba_ref, cbb_ref, h_ref):
        a, b = _conv_act(ua_ref[...], ub_ref[...], cwa_ref[...], cwb_ref[...], cba_ref[...], cbb_ref[...])
        h_ref[...] = (a * jax.nn.sigmoid(a) * b).astype(BF16)

    ca = lambda r: pl.BlockSpec((r, cbk), lambda j: (0, j))
    cbs = lambda r: pl.BlockSpec((r, cbk), lambda j: (0, nf + j))
    return _pcall(
        body,
        name="conv_gate_fwd",
        out_shape=jax.ShapeDtypeStruct((t, f), BF16),
        grid=(nf,),
        in_specs=[ca(t), cbs(t), ca(3), cbs(3), ca(1), cbs(1)],
        out_specs=ca(t),
        compiler_params=_cparams(("parallel",)),
    )(u, u, cw, cw, cb, cb)


def _conv_bwd(u, cw, cb, dh):
    t, f2 = u.shape
    f = f2 // 2
    cbk = _tile(f, 256)
    nf = f // cbk

    def body(ua_ref, ub_ref, cwa_ref, cwb_ref, cba_ref, cbb_ref, dh_ref, du_ref, dcw_ref, dcb_ref):
        ua, ub = ua_ref[...], ub_ref[...]
        cwa, cwb = cwa_ref[...], cwb_ref[...]
        a, b = _conv_act(ua, ub, cwa, cwb, cba_ref[...], cbb_ref[...])
        dh_v = dh_ref[...]
        sg = jax.nn.sigmoid(a)
        db = dh_v * (a * sg)
        da = dh_v * b * (sg * (1.0 + a * (1.0 - sg)))
        for idx, (dv, uu, cwv) in enumerate(((da, ua, cwa), (db, ub, cwb))):
            dcb_ref[idx] = _colsum(dv)
            dcw_ref[idx, 0:1, :] = _colsum(dv * _shift_rows(uu, True))
            dcw_ref[idx, 1:2, :] = _colsum(dv * uu)
            dcw_ref[idx, 2:3, :] = _colsum(dv * _shift_rows(uu, False))
            du = cwv[0:1, :] * _shift_rows(dv, False) + cwv[1:2, :] * dv + cwv[2:3, :] * _shift_rows(dv, True)
            du_ref[idx] = du.astype(BF16)

    ca = lambda r: pl.BlockSpec((r, cbk), lambda j: (0, j))
    cbs = lambda r: pl.BlockSpec((r, cbk), lambda j: (0, nf + j))
    o3 = lambda r: pl.BlockSpec((2, r, cbk), lambda j: (0, 0, j))
    return _pcall(
        body,
        name="conv_gate_bwd",
        out_shape=[jax.ShapeDtypeStruct((2, t, f), BF16), jax.ShapeDtypeStruct((2, 3, f), F32),
                   jax.ShapeDtypeStruct((2, 1, f), F32)],
        grid=(nf,),
        in_specs=[ca(t), cbs(t), ca(3), cbs(3), ca(1), cbs(1), ca(t)],
        out_specs=[o3(t), o3(3), o3(1)],
        compiler_params=_cparams(("parallel",)),
    )(u, u, cw, cw, cb, cb, dh)


def _attention_fwd(q, kk, vv, scale, *, hq, hkv, dk, dv, k_blk0, v_blk0, name):
    t = q.shape[0]
    tk = kk.shape[0]
    g_sz = hq // hkv
    tq = min(ATT_Q_BLOCK, t)

    def body(q_ref, k_ref, v_ref, o_ref, lse_ref):
        k = k_ref[...]
        v = v_ref[...]
        for j in range(g_sz):
            s = lax.dot_general(q_ref[:, j * dk:(j + 1) * dk], k, _DIMS["nt"], preferred_element_type=F32) * scale
            m = jnp.max(s, axis=-1, keepdims=True)
            p = jnp.exp(s - m)
            l = jnp.sum(p, axis=-1, keepdims=True)
            o = jnp.dot(p.astype(BF16), v, preferred_element_type=F32) / l
            o_ref[:, j * dv:(j + 1) * dv] = o.astype(BF16)
            lse_ref[0, :, j:j + 1] = m + jnp.log(l)

    return _pcall(
        body,
        name=name,
        out_shape=[jax.ShapeDtypeStruct((t, hq * dv), BF16), jax.ShapeDtypeStruct((hkv, t, g_sz), F32)],
        grid=(hkv, t // tq),
        in_specs=[
            pl.BlockSpec((tq, g_sz * dk), lambda g, i: (i, g)),
            pl.BlockSpec((tk, dk), lambda g, i: (0, k_blk0 + g)),
            pl.BlockSpec((tk, dv), lambda g, i: (0, v_blk0 + g)),
        ],
        out_specs=[
            pl.BlockSpec((tq, g_sz * dv), lambda g, i: (i, g)),
            pl.BlockSpec((1, tq, g_sz), lambda g, i: (g, i, 0)),
        ],
        compiler_params=_cparams(("parallel", "parallel")),
    )(q, kk, vv)


def _attention_bwd(q, kk, vv, do, lse, scale, *, hq, hkv, dk, dv, k_blk0, v_blk0, name):
    t = q.shape[0]
    tk = kk.shape[0]
    g_sz = hq // hkv
    tq = min(ATT_Q_BLOCK, t)

    def body(q_ref, k_ref, v_ref, do_ref, lse_ref, dq_ref, dk_ref, dv_ref):
        @pl.when(pl.program_id(1) == 0)
        def _():
            dk_ref[...] = jnp.zeros_like(dk_ref)
            dv_ref[...] = jnp.zeros_like(dv_ref)

        k = k_ref[...]
        v = v_ref[...]
        for j in range(g_sz):
            qj = q_ref[:, j * dk:(j + 1) * dk]
            doj = do_ref[:, j * dv:(j + 1) * dv]
            s = lax.dot_general(qj, k, _DIMS["nt"], preferred_element_type=F32) * scale
            p = jnp.exp(s - lse_ref[0, :, j:j + 1])
            dp = lax.dot_general(doj, v, _DIMS["nt"], preferred_element_type=F32)
            ds = (p * (dp - jnp.sum(p * dp, axis=-1, keepdims=True)) * scale).astype(BF16)
            dv_ref[...] += lax.dot_general(p.astype(BF16), doj, _DIMS["tn"], preferred_element_type=F32)
            dk_ref[...] += lax.dot_general(ds, qj, _DIMS["tn"], preferred_element_type=F32)
            dq_ref[:, j * dk:(j + 1) * dk] = jnp.dot(ds, k, preferred_element_type=F32)

    return _pcall(
        body,
        name=name,
        out_shape=[jax.ShapeDtypeStruct((t, hq * dk), F32), jax.ShapeDtypeStruct((tk, hkv * dk), F32),
                   jax.ShapeDtypeStruct((tk, hkv * dv), F32)],
        grid=(hkv, t // tq),
        in_specs=[
            pl.BlockSpec((tq, g_sz * dk), lambda g, i: (i, g)),
            pl.BlockSpec((tk, dk), lambda g, i: (0, k_blk0 + g)),
            pl.BlockSpec((tk, dv), lambda g, i: (0, v_blk0 + g)),
            pl.BlockSpec((tq, g_sz * dv), lambda g, i: (i, g)),
            pl.BlockSpec((1, tq, g_sz), lambda g, i: (g, i, 0)),
        ],
        out_specs=[
            pl.BlockSpec((tq, g_sz * dk), lambda g, i: (i, g)),
            pl.BlockSpec((tk, dk), lambda g, i: (0, g)),
            pl.BlockSpec((tk, dv), lambda g, i: (0, g)),
        ],
        compiler_params=_cparams(("parallel", "arbitrary")),
    )(q, kk, vv, do, lse)


def _silu(v):
    return v * jax.nn.sigmoid(v)


def _ada_fwd(conds, w_ada, b_ada_shard):
    r, d = conds.shape
    n = w_ada.shape[1]
    tn = _tile(n, 512)

    def body(c_ref, w_ref, b_ref, o_ref):
        s = _silu(c_ref[...]).astype(BF16)
        o_ref[...] = jnp.dot(s, w_ref[...].astype(BF16), preferred_element_type=F32) + b_ref[...]

    return _pcall(
        body,
        name="ada_fwd",
        out_shape=jax.ShapeDtypeStruct((r, n), F32),
        grid=(n // tn,),
        in_specs=[pl.BlockSpec((r, d), lambda j: (0, 0)), pl.BlockSpec((d, tn), lambda j: (0, j)),
                  pl.BlockSpec((1, tn), lambda j: (0, j))],
        out_specs=pl.BlockSpec((r, tn), lambda j: (0, j)),
        compiler_params=_cparams(("parallel",)),
    )(conds, w_ada, b_ada_shard)


def _cctx_partial(da16_shard, w_ada, c_ctx_row):
    d, n = w_ada.shape
    td = _tile(d, 512)

    def body(g_ref, w_ref, c_ref, o_ref):
        ds = lax.dot_general(g_ref[8:16, :].astype(BF16), w_ref[...].astype(BF16), _DIMS["nt"],
                             preferred_element_type=F32)
        cv = c_ref[...]
        sg = jax.nn.sigmoid(cv)
        o_ref[...] = ds * (sg * (1.0 + cv * (1.0 - sg)))

    return _pcall(
        body,
        name="cctx_partial",
        out_shape=jax.ShapeDtypeStruct((8, d), F32),
        grid=(d // td,),
        in_specs=[pl.BlockSpec((16, n), lambda j: (0, 0)), pl.BlockSpec((td, n), lambda j: (j, 0)),
                  pl.BlockSpec((1, td), lambda j: (0, j))],
        out_specs=pl.BlockSpec((8, td), lambda j: (0, j)),
        compiler_params=_cparams(("parallel",)),
    )(da16_shard, w_ada, c_ctx_row)


def _sum_parts(parts):
    p, _, n = parts.shape

    def body(p_ref, o_ref):
        acc = p_ref[0]
        for s in range(1, p):
            acc = acc + p_ref[s]
        o_ref[...] = acc

    return _pcall(
        body,
        name="sum_parts",
        out_shape=jax.ShapeDtypeStruct((1, n), F32),
        in_specs=[pl.BlockSpec(memory_space=pltpu.VMEM)],
        out_specs=pl.BlockSpec(memory_space=pltpu.VMEM),
    )(parts)


def _adam_math(w, g, m, v):
    m2 = ADAM_B1 * m + (1.0 - ADAM_B1) * g
    v2 = ADAM_B2 * v + (1.0 - ADAM_B2) * jnp.square(g)
    m_hat = m2 / (1.0 - ADAM_B1 ** ADAM_STEP)
    v_hat = v2 / (1.0 - ADAM_B2 ** ADAM_STEP)
    delta = -ADAM_LR * (m_hat / (jnp.sqrt(v_hat) + ADAM_EPS) + ADAM_WD * w)
    return delta, m2, v2


def _adamw(parts, w, m, v, name):
    p, r, c = parts.shape
    rb = _tile(r, max(8, (1 << 20) // (4 * c) // 8 * 8), 8)

    def body(p_ref, w_ref, m_ref, v_ref, g_ref, d_ref, m2_ref, v2_ref):
        g = p_ref[0].astype(F32)
        for s in range(1, p):
            g = g + p_ref[s].astype(F32)
        g_ref[...] = g
        d_ref[...], m2_ref[...], v2_ref[...] = _adam_math(w_ref[...], g, m_ref[...], v_ref[...])

    row = pl.BlockSpec((rb, c), lambda i: (i, 0))
    return _pcall(
        body,
        name=name,
        out_shape=[jax.ShapeDtypeStruct((r, c), F32)] * 4,
        grid=(r // rb,),
        in_specs=[pl.BlockSpec((p, rb, c), lambda i: (0, i, 0)), row, row, row],
        out_specs=[row] * 4,
        compiler_params=_cparams(("parallel",)),
    )(parts, w, m, v)


def _adamw_ada(conds, da16, w, m, v):
    d, n = w.shape
    rb = _tile(d, 256, LANE)

    def body(s_ref, da_ref, w_ref, m_ref, v_ref, g_ref, d_ref, m2_ref, v2_ref):
        g = lax.dot_general(_silu(s_ref[...]).astype(BF16), da_ref[...].astype(BF16), _DIMS["tn"],
                            preferred_element_type=F32)
        g_ref[...] = g
        d_ref[...], m2_ref[...], v2_ref[...] = _adam_math(w_ref[...], g, m_ref[...], v_ref[...])

    row = pl.BlockSpec((rb, n), lambda i: (i, 0))
    return _pcall(
        body,
        name="adamw_w_ada",
        out_shape=[jax.ShapeDtypeStruct((d, n), F32)] * 4,
        grid=(d // rb,),
        in_specs=[pl.BlockSpec((16, rb), lambda i: (0, i)), pl.BlockSpec((16, n), lambda i: (0, 0)), row, row, row],
        out_specs=[row] * 4,
        compiler_params=_cparams(("parallel",)),
    )(conds, da16, w, m, v)


def _cast_bf16(a, name):
    r, c = a.shape
    rb = _tile(r, 512, 8)

    def body(a_ref, o_ref):
        o_ref[...] = a_ref[...].astype(BF16)

    row = pl.BlockSpec((rb, c), lambda i: (i, 0))
    return _pcall(body, name=name, out_shape=jax.ShapeDtypeStruct((r, c), BF16), grid=(r // rb,),
                          in_specs=[row], out_specs=row, compiler_params=_cparams(("parallel",)))(a)


def _rope_tabs(t, rot):
    half, q = rot // 2, rot // 4
    n_rows = t // GRID_W
    row = jnp.repeat(jnp.arange(n_rows, dtype=F32), GRID_W)
    col = jnp.tile(jnp.arange(GRID_W, dtype=F32), n_rows)
    inv_freq = ROPE_THETA ** (-jnp.arange(0, half, 2, dtype=F32) / half)
    ang = jnp.concatenate([row[:, None] * inv_freq, col[:, None] * inv_freq], axis=-1)
    cos, sin = jnp.cos(ang), jnp.sin(ang)
    c0, c1, s0, s1 = cos[:, :q], cos[:, q:], sin[:, :q], sin[:, q:]
    z = jnp.zeros_like(s0)
    return (jnp.concatenate([c0, c0, c1, c1], -1), jnp.concatenate([-s0, z, -s1, z], -1),
            jnp.concatenate([z, s0, z, s1], -1))


def _pad_cols(a, left, total, fill=0.0):
    return jnp.pad(a, ((0, 0), (left, total - left - a.shape[1])), constant_values=fill)


def _with_ctx_rows(tab, tc, fill):
    return jnp.concatenate([tab, jnp.full((tc, tab.shape[1]), fill, F32)], axis=0)


def kernel(x, c, ctx, c_ctx, w_ada, b_ada, norm1_g, w_in, mla_q_norm_g, w_q_up, mla_kv_norm_g, w_kv_up, gqa_q_norm_g, gqa_k_norm_g, w_br_a, w_br_b, w_out, norm2_g, w_up, conv_w, conv_b, w_down, final_norm_g, loss_target, m_c_ctx, m_w_ada, m_b_ada, m_norm1_g, m_w_in, m_mla_q_norm_g, m_w_q_up, m_mla_kv_norm_g, m_w_kv_up, m_gqa_q_norm_g, m_gqa_k_norm_g, m_w_br_a, m_w_br_b, m_w_out, m_norm2_g, m_w_up, m_conv_w, m_conv_b, m_w_down, m_final_norm_g, v_c_ctx, v_w_ada, v_b_ada, v_norm1_g, v_w_in, v_mla_q_norm_g, v_w_q_up, v_mla_kv_norm_g, v_w_kv_up, v_gqa_q_norm_g, v_gqa_k_norm_g, v_w_br_a, v_w_br_b, v_w_out, v_norm2_g, v_w_up, v_conv_w, v_conv_b, v_w_down, v_final_norm_g):
    weights = dict(c_ctx=c_ctx, w_ada=w_ada, b_ada=b_ada, norm1_g=norm1_g, w_in=w_in, mla_q_norm_g=mla_q_norm_g,
                   w_q_up=w_q_up, mla_kv_norm_g=mla_kv_norm_g, w_kv_up=w_kv_up, gqa_q_norm_g=gqa_q_norm_g,
                   gqa_k_norm_g=gqa_k_norm_g, w_br_a=w_br_a, w_br_b=w_br_b, w_out=w_out, norm2_g=norm2_g, w_up=w_up,
                   conv_w=conv_w, conv_b=conv_b, w_down=w_down, final_norm_g=final_norm_g)
    mom_m = dict(c_ctx=m_c_ctx, w_ada=m_w_ada, b_ada=m_b_ada, norm1_g=m_norm1_g, w_in=m_w_in, mla_q_norm_g=m_mla_q_norm_g,
                 w_q_up=m_w_q_up, mla_kv_norm_g=m_mla_kv_norm_g, w_kv_up=m_w_kv_up, gqa_q_norm_g=m_gqa_q_norm_g,
                 gqa_k_norm_g=m_gqa_k_norm_g, w_br_a=m_w_br_a, w_br_b=m_w_br_b, w_out=m_w_out, norm2_g=m_norm2_g,
                 w_up=m_w_up, conv_w=m_conv_w, conv_b=m_conv_b, w_down=m_w_down, final_norm_g=m_final_norm_g)
    mom_v = dict(c_ctx=v_c_ctx, w_ada=v_w_ada, b_ada=v_b_ada, norm1_g=v_norm1_g, w_in=v_w_in, mla_q_norm_g=v_mla_q_norm_g,
                 w_q_up=v_w_q_up, mla_kv_norm_g=v_mla_kv_norm_g, w_kv_up=v_w_kv_up, gqa_q_norm_g=v_gqa_q_norm_g,
                 gqa_k_norm_g=v_gqa_k_norm_g, w_br_a=v_w_br_a, w_br_b=v_w_br_b, w_out=v_w_out, norm2_g=v_norm2_g,
                 w_up=v_w_up, conv_w=v_conv_w, conv_b=v_conv_b, w_down=v_w_down, final_norm_g=v_final_norm_g)
    order = list(weights)

    my_idx = 4 * lax.axis_index("x") + 2 * lax.axis_index("y") + lax.axis_index("c")
    xs, cts, tgt = x[0], ctx[0], loss_target[0]
    t, d = xs.shape
    tc = cts.shape[0]
    ta = t + tc
    kvl, ql = MLA_KV_LORA, MLA_Q_LORA
    nb = GQA_KV_HEADS * GQA_HEAD_DIM
    hb = GQA_HEADS * GQA_HEAD_DIM
    ha = MLA_HEADS
    f2 = w_up.shape[2] * N_DEV
    ff = f2 // 2

    big = ["w_in", "w_q_up", "w_kv_up", "w_br_a", "w_br_b", "w_out", "w_up", "w_down"]
    nw = len(big)
    del nw
    _ORDER_AFTER.clear()
    shards = {n: _cast_bf16(weights[n][0], "cast_" + n) for n in big}
    c_idx = jnp.reshape(lax.axis_index("c"), (1,)).astype(jnp.int32)

    def gather_start(names, dep):
        shs = [shards[n] for n in names]
        land = [lax.empty((N_DEV,) + s.shape, BF16) for s in shs]
        if dep is not None:
            _after(dep)
        s, r, arrs, tok = _split_start("gather_ici_start_" + names[0], shs + land, _gather_ici_copies(len(names)),
                                       5 * len(names))
        return dict(names=names, s=s, r=r, arrs=arrs, tok=tok)

    def gather_relay(g, after):
        n = len(g["names"])
        arrs = _split_wait("gather_ici_wait_" + g["names"][0], g["s"], g["r"], g["arrs"], _gather_ici_copies(n), after)
        s, r, bufs, tok = _split_start("gather_d2d_start_" + g["names"][0], arrs[n:], _gather_d2d_copies(n), 3 * n)
        g.update(s2=s, r2=r, bufs=bufs)
        return tok

    def gather_finish(g, after):
        n = len(g["names"])
        bufs = _split_wait("gather_d2d_wait_" + g["names"][0], g["s2"], g["r2"], g["bufs"], _gather_d2d_copies(n), after)
        return dict(zip(g["names"], bufs))

    g0 = gather_start(["w_in"], None)
    g1 = gather_start(["w_q_up", "w_kv_up", "w_br_a", "w_br_b", "w_out"], g0["tok"])
    g2 = gather_start(["w_up"], g1["tok"])
    g3 = gather_start(["w_down"], g2["tok"])

    _after(g3["tok"])
    (c_all,) = _all_gather([jnp.pad(c, ((0, 7), (0, 0)))], "gather_cond")
    conds = jnp.concatenate([c_all[:, 0, :], c_ctx[None, :], jnp.zeros((7, d), F32)], axis=0)
    ncol = w_ada.shape[2]
    b_shard = lax.dynamic_slice_in_dim(b_ada, my_idx * ncol, ncol, axis=1)
    ada_shard = _ada_fwd(conds, w_ada[0], b_shard)
    (ada_all,) = _all_gather([ada_shard], "gather_ada")
    ada = jnp.transpose(ada_all, (1, 0, 2)).reshape(16, N_DEV * ncol)
    lat = lax.dynamic_slice_in_dim(ada, my_idx, 1, axis=0).reshape(6, d)
    cxt = ada[8].reshape(6, d)
    zero2 = jnp.zeros((2, d), F32)
    mods1 = jnp.concatenate([lat[0:2], cxt[0:2], jnp.zeros((4, d), F32)], axis=0)
    mods2 = jnp.concatenate([lat[2:3], lat[3:4], lat[4:5], jnp.zeros((5, d), F32)], axis=0)
    mods2b = jnp.concatenate([lat[2:3], lat[4:5], jnp.zeros((6, d), F32)], axis=0)
    mods3 = jnp.concatenate([lat[5:6], jnp.zeros((7, d), F32)], axis=0)
    del zero2

    (cw_all,) = _all_gather([jnp.pad(conv_w[0], ((0, 5), (0, 0)))], "gather_conv_w")
    conv_w_f = jnp.transpose(cw_all[:, :3, :], (1, 0, 2)).reshape(3, f2)

    ca, s1a, s2a = _rope_tabs(t, MLA_ROPE)
    cb_, s1b, s2b = _rope_tabs(t, GQA_HEAD_DIM)
    q_tabs_a = (_pad_cols(jnp.concatenate([jnp.ones((t, MLA_NOPE), F32), ca], 1), 0, MLA_SLOT),
                _pad_cols(s1a, MLA_NOPE, MLA_SLOT), _pad_cols(s2a, MLA_NOPE, MLA_SLOT))
    q_tabs_b = (cb_, s1b, s2b)
    k_tabs = (_with_ctx_rows(_pad_cols(ca, 0, LANE), tc, 1.0), _with_ctx_rows(_pad_cols(s1a, 0, LANE), tc, 0.0),
              _with_ctx_rows(_pad_cols(s2a, 0, LANE), tc, 0.0),
              _with_ctx_rows(cb_, tc, 1.0), _with_ctx_rows(s1b, tc, 0.0), _with_ctx_rows(s2b, tc, 0.0))

    def cols_full(g):
        return jnp.transpose(g, (1, 0, 2)).reshape(g.shape[1], N_DEV * g.shape[2])

    _after(gather_relay(g0, mods1))
    z_all = _norm_mod_fwd(cts, xs, norm1_g, mods1)
    gathered = gather_finish(g0, z_all)
    w_in_f = cols_full(gathered["w_in"])
    o_kpe, o_kb, o_vb = kvl, kvl + MLA_ROPE, kvl + MLA_ROPE + nb
    o_q = o_vb + nb
    o_g = o_q + ql + hb
    wkv_w = kvl + 2 * nb + LANE
    w_kv_p = jnp.concatenate([w_in_f[:, :kvl], w_in_f[:, o_kb:o_q], w_in_f[:, o_kpe:o_kb],
                              jnp.zeros((d, LANE - MLA_ROPE), BF16)], axis=1)
    q_w = ql + hb
    q_pad = (-q_w) % 512 if d >= 512 else (-q_w) % d
    gate_blk = (q_w + q_pad) // d
    assert (q_w + q_pad) % d == 0
    w_qg_p = jnp.concatenate([w_in_f[:, o_q:o_g], jnp.zeros((d, q_pad), BF16), w_in_f[:, o_g:]], axis=1)
    w_cat = jnp.concatenate([w_kv_p, w_qg_p], axis=1)

    kv_all = _mm(z_all, w_kv_p, "nn", F32, "proj_kv", tm=1152, tn=wkv_w)
    qg = _mm(z_all, w_qg_p, "nn", F32, "proj_qg", tm=1024, tn=1024, rows=t)
    _after(gather_relay(g1, qg))
    kin, k_b, v_b = _key_prep_fwd(kv_all, mla_kv_norm_g, gqa_k_norm_g, k_tabs)
    cqn, q_b = _q_prep_fwd(qg, mla_q_norm_g, gqa_q_norm_g, q_tabs_b)
    gathered.update(gather_finish(g1, q_b))

    wq_f = cols_full(gathered["w_q_up"]).reshape(ql, ha, MLA_NOPE + MLA_ROPE)
    wq_ext = jnp.pad(wq_f, ((0, 0), (0, 0), (0, MLA_SLOT - MLA_NOPE - MLA_ROPE))).reshape(ql, ha * MLA_SLOT)
    wkv_f = cols_full(gathered["w_kv_up"]).reshape(kvl, ha, MLA_NOPE + MLA_V)
    wk_slots = jnp.pad(wkv_f[:, :, :MLA_NOPE], ((0, 0), (0, 0), (0, MLA_SLOT - MLA_NOPE))).reshape(kvl, ha * MLA_SLOT)
    wv_cols = wkv_f[:, :, MLA_NOPE:].reshape(kvl, ha * MLA_V)
    e_slot = jnp.pad(jnp.eye(MLA_ROPE, dtype=BF16),
                     ((0, LANE - MLA_ROPE), (MLA_NOPE, MLA_SLOT - MLA_NOPE - MLA_ROPE)))
    e_rows = jnp.concatenate([jnp.tile(e_slot, (1, ha)), jnp.zeros((LANE, ha * MLA_V), BF16)], axis=1)
    wkv_ext = jnp.concatenate([jnp.concatenate([wk_slots, wv_cols], axis=1), e_rows], axis=0)
    w_bra = cols_full(gathered["w_br_a"])
    w_brb = cols_full(gathered["w_br_b"])
    w_out_f = gathered["w_out"].reshape(d, d)

    kv_a = _mm(kin, wkv_ext, "nn", BF16, "kv_up", tm=1152, tn=1024)
    qa_raw = _mm(cqn, wq_ext, "nn", F32, "q_up", tm=1024, tn=1024)
    q_a = _rope_a(qa_raw, q_tabs_a, False, BF16, "rope_q_fwd")
    sc_a = float((MLA_NOPE + MLA_ROPE) ** -0.5)
    sc_b = float(GQA_HEAD_DIM ** -0.5)
    att_a = dict(hq=ha, hkv=ha, dk=MLA_SLOT, dv=MLA_V, k_blk0=0, v_blk0=ha * MLA_SLOT // MLA_V)
    att_b = dict(hq=GQA_HEADS, hkv=GQA_KV_HEADS, dk=GQA_HEAD_DIM, dv=GQA_HEAD_DIM, k_blk0=0, v_blk0=0)
    o_a, lse_a = _attention_fwd(q_a, kv_a, kv_a, sc_a, name="attn_a_fwd", **att_a)
    o_b, lse_b = _attention_fwd(q_b, k_b, v_b, sc_b, name="attn_b_fwd", **att_b)
    _after(gather_relay(g2, o_b))
    pa = _mm(o_a, w_bra, "nn", F32, "br_a", tm=1024, tn=1024)
    pb = _mm(o_b, w_brb, "nn", F32, "br_b", tm=1024, tn=1024)
    merged = _merge_fwd(pa, pb, qg, gate_blk)
    attn = _mm(merged, w_out_f, "nn", F32, "w_out", tm=1024, tn=1024)
    x1, z2 = _resid_norm_mod(xs, attn, norm2_g, mods2, "resid_norm2_fwd")
    w_up3 = gather_finish(g2, z2)["w_up"]
    _after(gather_relay(g3, z2))
    u = _mm_up_fwd(z2, w_up3, "w_up")
    w_down_f = gather_finish(g3, u)["w_down"].reshape(ff, d)
    h = _conv_fwd(u, conv_w_f, conv_b)
    ffn = _mm(h, w_down_f, "nn", F32, "w_down", tm=1024, tn=1024, tk=2816)

    def to_shards(g):
        return jnp.transpose(g.reshape(g.shape[0], N_DEV, g.shape[1] // N_DEV), (1, 0, 2))

    def reduce_start(tag, names, sends):
        n = len(sends)
        land = [lax.empty((4,) + s.shape[1:], s.dtype) for s in sends]
        s, r, arrs, tok = _split_start("reduce_d2d_start_" + tag, sends + land, _reduce_d2d_copies(n), 4 * n)
        return dict(tag=tag, names=names, s=s, r=r, arrs=arrs, tok=tok)

    def reduce_relay(g, after):
        n = len(g["names"])
        arrs = _split_wait("reduce_d2d_wait_" + g["tag"], g["s"], g["r"], g["arrs"], _reduce_d2d_copies(n), after)
        sums = [_pair_sum(arrs[a], arrs[n + a], c_idx, "pair_sum_" + g["names"][a]) for a in range(n)]
        land = [lax.empty(s.shape, s.dtype) for s in sums]
        s, r, arrs2, tok = _split_start("reduce_ici_start_" + g["tag"], sums + land, _reduce_ici_copies(n), 4 * n)
        g.update(s2=s, r2=r, arrs2=arrs2)
        return tok

    def reduce_finish(g, after):
        n = len(g["names"])
        arrs2 = _split_wait("reduce_ici_wait_" + g["tag"], g["s2"], g["r2"], g["arrs2"], _reduce_ici_copies(n), after)
        return dict(zip(g["names"], arrs2[n:]))

    dx2, dffn, st_fin = _final_loss(x1, ffn, final_norm_g[None, :], mods3, tgt)
    loss = lax.psum(st_fin[3, 0], MESH_AXES)
    dh = _mm(dffn, w_down_f, "nt", F32, "d_h", tm=1024, tn=1024)
    g_w_down = _mm(h, dffn, "tn", BF16, "g_w_down", tm=512, tn=1024)
    r_down = reduce_start("down", ["w_down"], [g_w_down.reshape(N_DEV, ff // N_DEV, d)])
    _after(r_down["tok"])
    du3, dcw, dcb = _conv_bwd(u, conv_w_f, conv_b, dh)
    dz2 = _mm_up_dz(du3, w_up3, "d_z2")
    g_w_up = _mm_up_gw(z2, du3, N_DEV, "g_w_up")
    g_conv_w = jnp.concatenate([dcw[0], dcw[1]], axis=1)
    tok = reduce_relay(r_down, g_w_up)
    _after(tok)
    r_up = reduce_start("up", ["w_up", "conv_w"], [g_w_up, to_shards(jnp.pad(g_conv_w, ((0, 5), (0, 0))))])
    _after(tok, r_up["tok"])
    dx1, dattn, st_n2 = _norm2_bwd(x1, attn, norm2_g, mods2b, dz2, dx2)
    dmerged = _mm(dattn, w_out_f, "nt", F32, "d_merged", tm=1024, tn=1024)
    g_w_out = _mm(merged, dattn, "tn", BF16, "g_w_out", tm=1024, tn=1024)
    dpa, dpb, dgates = _merge_bwd(dmerged, pa, pb, qg, gate_blk)
    do_a = _mm(dpa, w_bra, "nt", BF16, "d_o_a", tm=1024, tn=1024)
    do_b = _mm(dpb, w_brb, "nt", BF16, "d_o_b", tm=1024, tn=1024)
    g_w_bra = _mm(o_a, dpa, "tn", BF16, "g_w_br_a", tm=1024, tn=1024)
    g_w_brb = _mm(o_b, dpb, "tn", BF16, "g_w_br_b", tm=1024, tn=1024)
    tok = reduce_relay(r_up, g_w_brb)
    _after(tok)
    r_out = reduce_start("out", ["w_out", "w_br_a", "w_br_b"],
                         [g_w_out.reshape(N_DEV, d // N_DEV, d), to_shards(g_w_bra), to_shards(g_w_brb)])
    _after(tok, r_out["tok"])
    dq_a, dk_a, dv_a = _attention_bwd(q_a, kv_a, kv_a, do_a, lse_a, sc_a, name="attn_a_bwd", **att_a)
    dq_b, dk_b, dv_b = _attention_bwd(q_b, k_b, v_b, do_b, lse_b, sc_b, name="attn_b_bwd", **att_b)
    _after(reduce_relay(r_out, dv_b))
    dqa_raw = _rope_a(dq_a, q_tabs_a, True, BF16, "rope_q_bwd")
    dcqn = _mm(dqa_raw, wq_ext, "nt", F32, "d_cqn", tm=1024, tn=ql)
    g_wq_ext = _mm(cqn, dqa_raw, "tn", BF16, "g_w_q_up", tm=ql, tn=1024)
    dq_p, st_q, st_qb = _q_prep_bwd(qg, mla_q_norm_g, gqa_q_norm_g, q_tabs_b, dcqn, dq_b, q_pad)
    dkv_a = jnp.concatenate([dk_a.astype(BF16), dv_a.astype(BF16)], axis=1)
    dkin = _mm(dkv_a, wkv_ext, "nt", F32, "d_kin", tm=1152, tn=kvl + LANE, tk=3072)
    g_wkv_ext = _mm(kin, dkv_a, "tn", BF16, "g_w_kv_up", tm=kvl + LANE, tn=1024)
    dkv_p, st_kv, st_kb = _key_prep_bwd(kv_all, mla_kv_norm_g, gqa_k_norm_g, k_tabs, dkin, dk_b, dv_b)
    g_wq = g_wq_ext.reshape(ql, ha, MLA_SLOT)[:, :, :MLA_NOPE + MLA_ROPE].reshape(ql, ha * (MLA_NOPE + MLA_ROPE))
    g_wkv = jnp.concatenate([g_wkv_ext[:kvl, :ha * MLA_SLOT].reshape(kvl, ha, MLA_SLOT)[:, :, :MLA_NOPE],
                             g_wkv_ext[:kvl, ha * MLA_SLOT:].reshape(kvl, ha, MLA_V)], axis=2).reshape(kvl, ha * (MLA_NOPE + MLA_V))
    r_qkv = reduce_start("qkv", ["w_q_up", "w_kv_up"], [to_shards(g_wq), to_shards(g_wkv)])
    dqg = jnp.concatenate([dq_p, dgates], axis=1)
    _after(r_qkv["tok"])
    g_wkv_p = _mm(z_all, dkv_p, "tn", BF16, "g_w_in_kv", tm=1024, tn=wkv_w)
    g_wqg_p = _mm(z_all, dqg, "tn", BF16, "g_w_in_qg", tm=1024, tn=1024, rows=t)
    g_w_in = jnp.concatenate([g_wkv_p[:, :kvl], g_wkv_p[:, kvl + 2 * nb:kvl + 2 * nb + MLA_ROPE],
                              g_wkv_p[:, kvl:kvl + 2 * nb], g_wqg_p[:, :q_w], g_wqg_p[:, q_w + q_pad:]], axis=1)
    r_in = reduce_start("in", ["w_in"], [to_shards(g_w_in)])
    dcat = jnp.concatenate([dkv_p[:t], dqg], axis=1)
    _after(r_in["tok"])
    dz_lat = _mm(dcat, w_cat, "nt", F32, "d_z_lat", tm=1024, tn=1024, tk=2432)
    dz_ctx = _mm(dkv_p, w_kv_p, "nt", F32, "d_z_ctx", tm=min(ROW_BLOCK, tc), tn=1024, a_row_off=t)
    tok_q = reduce_relay(r_qkv, dz_ctx)
    tok_i = reduce_relay(r_in, dz_ctx)
    _after(tok_q, tok_i)
    grad_x, st_n1 = _norm1_bwd(cts, xs, norm1_g, mods1, dz_ctx, dz_lat, dx1)

    d_lat = jnp.concatenate([st_n1[0], st_n1[1], st_n2[3], st_n2[0], st_n2[1], st_fin[1]])
    d_cxt = jnp.concatenate([st_n1[3], st_n1[4], jnp.zeros((4 * d,), F32)])
    small = jnp.concatenate([d_lat, d_cxt, st_n1[2], st_q[0], st_kv[0], st_qb[0], st_kb[0], st_n2[2],
                             jnp.concatenate([dcb[0, 0], dcb[1, 0]]), st_fin[0]])
    n_small = small.shape[0]
    pad_small = (-n_small) % LANE
    (small_all,) = _all_gather([jnp.pad(small, (0, pad_small)).reshape(1, -1)], "gather_small")
    offs = {}
    o = 0
    for nm, ln in (("d_lat", 6 * d), ("d_cxt", 6 * d), ("norm1_g", d), ("mla_q_norm_g", ql), ("mla_kv_norm_g", kvl),
                   ("gqa_q_norm_g", GQA_HEAD_DIM), ("gqa_k_norm_g", GQA_HEAD_DIM), ("norm2_g", d), ("conv_b", f2),
                   ("final_norm_g", d)):
        offs[nm] = (o, ln)
        o += ln

    def part(nm):
        a, ln = offs[nm]
        return small_all[:, :, a:a + ln]

    d_lat_all = part("d_lat")[:, 0, :]
    d_cxt_sum = _sum_parts(part("d_cxt"))
    da16 = jnp.concatenate([d_lat_all, d_cxt_sum, jnp.zeros((7, 6 * d), F32)], axis=0)
    da16_shard = lax.dynamic_slice_in_dim(da16, my_idx * ncol, ncol, axis=1)
    cc_part = _cctx_partial(da16_shard, w_ada[0], c_ctx[None, :])
    (cc_all,) = _all_gather([cc_part], "gather_cctx")
    cc_parts = cc_all[:, 0:1, :]

    res = {}

    def upd(nm, parts, shape2):
        wv, mv, vv = (a.reshape(shape2) for a in (weights[nm], mom_m[nm], mom_v[nm]))
        outs = _adamw(parts, wv, mv, vv, "adamw_" + nm)
        res[nm] = [o_.reshape(weights[nm].shape) for o_ in outs]

    for nm in ("norm1_g", "mla_q_norm_g", "mla_kv_norm_g", "gqa_q_norm_g", "gqa_k_norm_g", "norm2_g", "conv_b",
               "final_norm_g"):
        upd(nm, part(nm), (1, offs[nm][1]))
    upd("c_ctx", cc_parts, (1, d))
    b_parts = jnp.concatenate([d_lat_all[:, None, :], d_cxt_sum[None]], axis=0)
    upd("b_ada", b_parts, (1, 6 * d))
    outs = _adamw_ada(conds, da16_shard, w_ada[0], m_w_ada[0], v_w_ada[0])
    res["w_ada"] = [o_[None] for o_ in outs]
    last = outs[0]
    for grp in (r_down, r_up, r_out, r_qkv, r_in):
        recv = reduce_finish(grp, last)
        for nm in grp["names"]:
            parts = recv[nm][:, :3, :] if nm == "conv_w" else recv[nm]
            upd(nm, parts, weights[nm].shape[1:])
            last = res[nm][0]

    return (loss, grad_x[None], *[res[n][0] for n in order], *[res[n][1] for n in order],
            *[res[n][2] for n in order], *[res[n][3] for n in order])
```

```python
import functools

import jax
import jax.numpy as jnp
from jax import lax
from jax.experimental import pallas as pl
from jax.experimental.pallas import tpu as pltpu

F32 = jnp.float32
BF16 = jnp.bfloat16

GRID_W = 64
ROPE_THETA = 10000.0
NORM_EPS = 1e-6
MLA_HEADS = 8
MLA_Q_LORA = 768
MLA_KV_LORA = 512
MLA_NOPE = 128
MLA_ROPE = 64
MLA_V = 128
GQA_HEADS = 8
GQA_KV_HEADS = 2
GQA_HEAD_DIM = 128
ADAM_LR = 0.001
ADAM_B1 = 0.9
ADAM_B2 = 0.999
ADAM_EPS = 1e-08
ADAM_WD = 0.01
ADAM_STEP = 10

N_DEV = 8
MESH_AXES = ("x", "y", "c")
LANE = 128
MLA_SLOT = 2 * LANE
VMEM_LIMIT = 56 * 1024 * 1024
ROW_BLOCK = 256
ATT_Q_BLOCK = 256
MESH_ID = pl.DeviceIdType.MESH


def _tile(n, pref, align=LANE):
    if n <= pref:
        return n
    best = None
    t = align
    while t <= pref:
        if n % t == 0:
            best = t
        t += align
    assert best is not None, (n, pref, align)
    return best


def _cparams(sem=None):
    return pltpu.CompilerParams(dimension_semantics=sem, vmem_limit_bytes=VMEM_LIMIT)


_ORDER_AFTER = []


def _after(*arrays):
    _ORDER_AFTER.extend(arrays)


def _pcall(body, *, in_specs, **kw):
    deps = tuple(_ORDER_AFTER)
    _ORDER_AFTER.clear()
    if not deps:
        return pl.pallas_call(body, in_specs=in_specs, **kw)
    n_in, n_dep = len(in_specs), len(deps)

    def with_deps(*refs):
        body(*refs[:n_in], *refs[n_in + n_dep:])

    call = pl.pallas_call(with_deps, in_specs=list(in_specs) + [pl.BlockSpec(memory_space=pl.ANY)] * n_dep, **kw)
    return lambda *args: call(*args, *deps)


def _all_gather(arrs, name):
    n = len(arrs)

    def body(*refs):
        ins = refs[:n]
        outs = refs[n:2 * n]
        send_sems, recv_sems, local_sems = refs[2 * n:]
        x, y, c = lax.axis_index("x"), lax.axis_index("y"), lax.axis_index("c")
        me, sibling = (x, y, c), (x, y, 1 - c)
        chips = [(1 - x, y), (x, 1 - y), (1 - x, 1 - y)]

        def rows(a, dev):
            px, py, pc = dev
            return outs[a].at[4 * px + 2 * py + pc]

        def copy(a, k, block, to, src=None):
            return pltpu.make_async_remote_copy(
                src_ref=rows(a, block) if src is None else src,
                dst_ref=rows(a, block),
                send_sem=send_sems.at[7 * a + k],
                recv_sem=recv_sems.at[7 * a + k],
                device_id=to,
                device_id_type=MESH_ID,
            )

        mine = [pltpu.make_async_copy(ins[a], rows(a, me), local_sems.at[a]) for a in range(n)]
        for cp in mine:
            cp.start()
        first = []
        for a in range(n):
            first.append(copy(a, 0, me, sibling, src=ins[a]))
            first += [copy(a, 1 + j, me, (*chip, c), src=ins[a]) for j, chip in enumerate(chips)]
        for cp in first:
            cp.start()
        passed = []
        for j, chip in enumerate(chips):
            for a in range(n):
                copy(a, 1 + j, (*chip, c), me).wait_recv()
                fwd = copy(a, 4 + j, (*chip, c), sibling)
                fwd.start()
                passed.append(fwd)
        for a in range(n):
            copy(a, 0, sibling, me).wait_recv()
            for j, chip in enumerate(chips):
                copy(a, 4 + j, (*chip, 1 - c), me).wait_recv()
        for cp in first + passed:
            cp.wait_send()
        for cp in mine:
            cp.wait()

    any_spec = pl.BlockSpec(memory_space=pl.ANY)
    outs = _pcall(
        body,
        name=name,
        out_shape=[jax.ShapeDtypeStruct((N_DEV,) + a.shape, a.dtype) for a in arrs],
        in_specs=[any_spec] * n,
        out_specs=[any_spec] * n,
        scratch_shapes=[
            pltpu.SemaphoreType.DMA((7 * n,)),
            pltpu.SemaphoreType.DMA((7 * n,)),
            pltpu.SemaphoreType.DMA((n,)),
        ],
    )(*arrs)
    return list(outs)


def _all_to_all(arrs, name):
    n = len(arrs)

    def body(*refs):
        ins = refs[:n]
        outs = refs[n:2 * n]
        send_sems, recv_sems, local_sems = refs[2 * n:]
        x, y, c = lax.axis_index("x"), lax.axis_index("y"), lax.axis_index("c")
        my_idx = 4 * x + 2 * y + c

        def peer(k):
            fx, fy, fc = (k >> 2) & 1, (k >> 1) & 1, k & 1
            return (x ^ fx if fx else x, y ^ fy if fy else y, c ^ fc if fc else c)

        def copy(a, k):
            px, py, pc = peer(k)
            return pltpu.make_async_remote_copy(
                src_ref=ins[a].at[4 * px + 2 * py + pc],
                dst_ref=outs[a].at[my_idx],
                send_sem=send_sems.at[7 * a + k - 1],
                recv_sem=recv_sems.at[7 * a + k - 1],
                device_id=(px, py, pc),
                device_id_type=MESH_ID,
            )

        mine = [pltpu.make_async_copy(ins[a].at[my_idx], outs[a].at[my_idx], local_sems.at[a]) for a in range(n)]
        for cp in mine:
            cp.start()
        order = [1, 4, 2, 5, 3, 6, 7]
        cps = [copy(a, k) for k in order for a in range(n)]
        for cp in cps:
            cp.start()
        for cp in cps:
            cp.wait()
        for cp in mine:
            cp.wait()

    any_spec = pl.BlockSpec(memory_space=pl.ANY)
    outs = _pcall(
        body,
        name=name,
        out_shape=[jax.ShapeDtypeStruct(a.shape, a.dtype) for a in arrs],
        in_specs=[any_spec] * n,
        out_specs=[any_spec] * n,
        scratch_shapes=[
            pltpu.SemaphoreType.DMA((7 * n,)),
            pltpu.SemaphoreType.DMA((7 * n,)),
            pltpu.SemaphoreType.DMA((n,)),
        ],
    )(*arrs)
    return list(outs)


_HBM = pl.BlockSpec(memory_space=pltpu.HBM)
_SEM = pl.BlockSpec(memory_space=pltpu.SEMAPHORE)
_EFFECT = pltpu.SideEffectType.DATAFLOW_SIDE_EFFECTING


def _descriptors(copies, send_sems, recv_sems):
    descs = []
    for i, (src, dst, dev) in enumerate(copies):
        if dev is None:
            descs.append(pltpu.make_async_copy(src, dst, recv_sems.at[i]))
        else:
            descs.append(pltpu.make_async_remote_copy(src_ref=src, dst_ref=dst, send_sem=send_sems.at[i],
                                                      recv_sem=recv_sems.at[i], device_id=dev, device_id_type=MESH_ID))
    return descs


def _split_start(name, arrays, copies_fn, n_copies):
    n = len(arrays)

    def body(*refs):
        send_sems, recv_sems = refs[n], refs[n + 1]
        token = refs[2 * n + 2]
        for dsc in _descriptors(copies_fn(refs[:n]), send_sems, recv_sems):
            dsc.start()
        token[...] = jnp.zeros_like(token)

    outs = _pcall(
        body,
        name=name,
        out_shape=(pltpu.SemaphoreType.DMA((n_copies,)), pltpu.SemaphoreType.DMA((n_copies,)),
                   *[pltpu.HBM(a.shape, a.dtype) for a in arrays], jax.ShapeDtypeStruct((8, LANE), F32)),
        in_specs=[_HBM] * n,
        out_specs=(_SEM, _SEM, *[_HBM] * n, pl.BlockSpec(memory_space=pltpu.VMEM)),
        input_output_aliases={i: 2 + i for i in range(n)},
        compiler_params=pltpu.CompilerParams(has_side_effects=_EFFECT),
    )(*[pltpu.with_memory_space_constraint(a, pltpu.HBM) for a in arrays])
    return outs[0], outs[1], list(outs[2:2 + n]), outs[2 + n]


def _split_wait(name, send_sems, recv_sems, arrays, copies_fn, after):
    n = len(arrays)

    def body(*refs):
        for dsc, (_, _, dev) in zip(_descriptors(copies_fn(refs[:n]), refs[n], refs[n + 1]), copies_fn(refs[:n])):
            if dev is None:
                dsc.wait()
            else:
                dsc.wait_send()
                dsc.wait_recv()

    outs = _pcall(
        body,
        name=name,
        out_shape=tuple(pltpu.HBM(a.shape, a.dtype) for a in arrays),
        in_specs=[_HBM] * n + [_SEM, _SEM, pl.BlockSpec(memory_space=pl.ANY)],
        out_specs=tuple([_HBM] * n),
        input_output_aliases={i: i for i in range(n)},
        compiler_params=pltpu.CompilerParams(has_side_effects=_EFFECT),
    )(*arrays, send_sems, recv_sems, after)
    return list(outs)


def _mesh_pos():
    x, y, c = lax.axis_index("x"), lax.axis_index("y"), lax.axis_index("c")
    return x, y, c, [(1 - x, y), (x, 1 - y), (1 - x, 1 - y)]


def _gather_ici_copies(n):
    def copies(refs):
        x, y, c, chips = _mesh_pos()
        me = 4 * x + 2 * y + c
        out = []
        for a in range(n):
            src, buf = refs[a], refs[n + a]
            out.append((src, buf.at[me], None))
            out.append((src, buf.at[me], (x, y, 1 - c)))
            out += [(src, buf.at[me], (cx, cy, c)) for cx, cy in chips]
        return out
    return copies


def _gather_d2d_copies(n):
    def copies(refs):
        x, y, c, chips = _mesh_pos()
        out = []
        for a in range(n):
            for cx, cy in chips:
                rows = refs[a].at[4 * cx + 2 * cy + c]
                out.append((rows, rows, (x, y, 1 - c)))
        return out
    return copies


def _reduce_d2d_copies(n):
    def copies(refs):
        x, y, c, _ = _mesh_pos()
        out = []
        for a in range(n):
            for k in range(4):
                out.append((refs[a].at[2 * k + (1 - c)], refs[n + a].at[k], (x, y, 1 - c)))
        return out
    return copies


def _reduce_ici_copies(n):
    def copies(refs):
        x, y, c, chips = _mesh_pos()
        mine = 2 * x + y
        out = []
        for a in range(n):
            src, land = refs[a], refs[n + a]
            out.append((src.at[mine], land.at[mine], None))
            out += [(src.at[2 * cx + cy], land.at[mine], (cx, cy, c)) for cx, cy in chips]
        return out
    return copies


def _pair_sum(send, land, c_idx, name):
    _, r, cols = send.shape
    rb = _tile(r, max(8, (1 << 20) // (2 * cols) // 8 * 8), 8)
    dt = send.dtype

    def body(c_ref, s_ref, l_ref, o_ref):
        o_ref[...] = (s_ref[...].astype(F32) + l_ref[...].astype(F32)).astype(dt)

    return pl.pallas_call(
        body,
        name=name,
        out_shape=jax.ShapeDtypeStruct((4, r, cols), dt),
        grid_spec=pltpu.PrefetchScalarGridSpec(
            num_scalar_prefetch=1,
            grid=(4, r // rb),
            in_specs=[pl.BlockSpec((None, rb, cols), lambda k, i, c_ref: (2 * k + c_ref[0], i, 0)),
                      pl.BlockSpec((None, rb, cols), lambda k, i, c_ref: (k, i, 0))],
            out_specs=pl.BlockSpec((None, rb, cols), lambda k, i, c_ref: (k, i, 0)),
        ),
        compiler_params=_cparams(("parallel", "parallel")),
    )(c_idx, send, land)


_DIMS = {
    "nn": (((1,), (0,)), ((), ())),
    "nt": (((1,), (1,)), ((), ())),
    "tn": (((0,), (0,)), ((), ())),
}


def _mm_call(a, b, *, mode, grid, a_spec, b_spec, o_spec, out_shape, acc_shape, name):
    nk = grid[2]
    out_dtype = out_shape.dtype

    def body(a_ref, b_ref, o_ref, *scratch):
        p = lax.dot_general(a_ref[...].astype(BF16), b_ref[...].astype(BF16), _DIMS[mode],
                            preferred_element_type=F32)
        if nk == 1:
            o_ref[...] = p.astype(out_dtype)
        else:
            acc = scratch[0]
            k = pl.program_id(2)

            @pl.when(k == 0)
            def _():
                acc[...] = p

            @pl.when(k > 0)
            def _():
                acc[...] += p

            @pl.when(k == nk - 1)
            def _():
                o_ref[...] = acc[...].astype(out_dtype)

    return _pcall(
        body,
        name=name,
        out_shape=out_shape,
        grid=grid,
        in_specs=[a_spec, b_spec],
        out_specs=o_spec,
        scratch_shapes=[pltpu.VMEM(acc_shape, F32)] if nk > 1 else [],
        compiler_params=_cparams(("parallel", "parallel", "arbitrary")),
    )(a, b)


def _mm(a, b, mode, out_dtype, name, tm=512, tn=512, tk=2432, a_row_off=0, rows=None):
    if mode == "nn":
        (m, k), (k2, n) = a.shape, b.shape
    elif mode == "nt":
        (m, k), (n, k2) = a.shape, b.shape
    else:
        (k, m), (k2, n) = a.shape, b.shape
        if rows is not None:
            k = k2 = rows
    assert k == k2, (a.shape, b.shape, mode)
    if mode != "tn":
        m = (m if rows is None else rows + a_row_off) - a_row_off
    tm, tn, tk = _tile(m, tm, 8), _tile(n, tn), _tile(k, tk, 8 if mode == "tn" else LANE)
    assert a_row_off % tm == 0
    ro = a_row_off // tm
    grid = (m // tm, n // tn, k // tk)
    if mode == "tn":
        a_spec = pl.BlockSpec((tk, tm), lambda i, j, kk: (kk, i))
    else:
        a_spec = pl.BlockSpec((tm, tk), lambda i, j, kk: (i + ro, kk))
    if mode == "nt":
        b_spec = pl.BlockSpec((tn, tk), lambda i, j, kk: (j, kk))
    else:
        b_spec = pl.BlockSpec((tk, tn), lambda i, j, kk: (kk, j))
    o_spec = pl.BlockSpec((tm, tn), lambda i, j, kk: (i, j))
    return _mm_call(a, b, mode=mode, grid=grid, a_spec=a_spec, b_spec=b_spec, o_spec=o_spec,
                    out_shape=jax.ShapeDtypeStruct((m, n), out_dtype), acc_shape=(tm, tn), name=name)


def _mm_up_fwd(z2, w3, name, tm=1024):
    t, d = z2.shape
    nsh, _, c = w3.shape
    tm = _tile(t, tm, 8)
    return _mm_call(z2, w3, mode="nn", grid=(t // tm, nsh, 1),
                    a_spec=pl.BlockSpec((tm, d), lambda i, j, kk: (i, 0)),
                    b_spec=pl.BlockSpec((None, d, c), lambda i, j, kk: (j, 0, 0)),
                    o_spec=pl.BlockSpec((tm, c), lambda i, j, kk: (i, j)),
                    out_shape=jax.ShapeDtypeStruct((t, nsh * c), F32), acc_shape=(tm, c), name=name)


def _mm_up_dz(du3, w3, name, tm=1024, tn=1024):
    _, t, f = du3.shape
    nsh, d, c = w3.shape
    half = nsh // 2
    assert f == half * c
    tm, tn = _tile(t, tm, 8), _tile(d, tn)
    return _mm_call(du3, w3, mode="nt", grid=(t // tm, d // tn, nsh),
                    a_spec=pl.BlockSpec((None, tm, c), lambda i, j, kk: (kk // half, i, kk % half)),
                    b_spec=pl.BlockSpec((None, tn, c), lambda i, j, kk: (kk, j, 0)),
                    o_spec=pl.BlockSpec((tm, tn), lambda i, j, kk: (i, j)),
                    out_shape=jax.ShapeDtypeStruct((t, d), F32), acc_shape=(tm, tn), name=name)


def _mm_up_gw(z2, du3, nsh, name, tm=1024):
    t, d = z2.shape
    f = du3.shape[2]
    half = nsh // 2
    c = f // half
    tm = _tile(d, tm)
    return _mm_call(z2, du3, mode="tn", grid=(d // tm, nsh, 1),
                    a_spec=pl.BlockSpec((t, tm), lambda i, j, kk: (0, i)),
                    b_spec=pl.BlockSpec((None, t, c), lambda i, j, kk: (j // half, 0, j % half)),
                    o_spec=pl.BlockSpec((None, tm, c), lambda i, j, kk: (j, i, 0)),
                    out_shape=jax.ShapeDtypeStruct((nsh, d, c), BF16), acc_shape=(tm, c), name=name)


def _rms(x):
    r = lax.rsqrt(jnp.mean(x * x, axis=-1, keepdims=True) + NORM_EPS)
    return x * r, r


def _rms_bwd(dxh, xh, r):
    return r * (dxh - xh * jnp.mean(dxh * xh, axis=-1, keepdims=True))


def _colsum(v):
    return jnp.sum(v, axis=0, keepdims=True)


def _rope(v, c, s1, s2, q):
    w = v.shape[-1]
    return v * c + pltpu.roll(v, w - q, 1) * s1 + pltpu.roll(v, q, 1) * s2


def _rope_t(d, c, s1, s2, q):
    w = d.shape[-1]
    return d * c + pltpu.roll(d * s1, q, 1) + pltpu.roll(d * s2, w - q, 1)


def _norm_mod_fwd(ctx, x, gain, mods):
    tc, d = ctx.shape
    t = x.shape[0]
    rb = min(ROW_BLOCK, tc)
    nbl = t // rb

    def body(ctx_ref, x_ref, g_ref, mod_ref, z_ref):
        i = pl.program_id(0)

        def emit(src, sh, sc):
            xh, _ = _rms(src[...])
            z_ref[...] = ((xh * g_ref[...]) * (1.0 + sc) + sh).astype(BF16)

        @pl.when(i >= nbl)
        def _():
            emit(ctx_ref, mod_ref[2:3, :], mod_ref[3:4, :])

        @pl.when(i < nbl)
        def _():
            emit(x_ref, mod_ref[0:1, :], mod_ref[1:2, :])

    return _pcall(
        body,
        name="norm1_mod_fwd",
        out_shape=jax.ShapeDtypeStruct((tc + t, d), BF16),
        grid=((tc + t) // rb,),
        in_specs=[
            pl.BlockSpec((rb, d), lambda i: (jnp.maximum(i - nbl, 0), 0)),
            pl.BlockSpec((rb, d), lambda i: (jnp.minimum(i, nbl - 1), 0)),
            pl.BlockSpec((1, d), lambda i: (0, 0)),
            pl.BlockSpec((8, d), lambda i: (0, 0)),
        ],
        out_specs=pl.BlockSpec((rb, d), lambda i: (i, 0)),
        compiler_params=_cparams(("arbitrary",)),
    )(ctx, x, gain, mods)


def _norm1_bwd(ctx, x, gain, mods, dz_ctx, dz_lat, dx1):
    tc, d = ctx.shape
    t = x.shape[0]
    rb = min(ROW_BLOCK, tc)
    nbl = t // rb

    def body(ctx_ref, x_ref, g_ref, mod_ref, dzc_ref, dzl_ref, dx1_ref, gx_ref, st_ref):
        i = pl.program_id(0)

        @pl.when(i == 0)
        def _():
            st_ref[...] = jnp.zeros_like(st_ref)

        def common(src, dz, sc, row_sh, row_sc):
            xh, r = _rms(src[...])
            g = g_ref[...]
            dxn = dz * (1.0 + sc)
            st_ref[row_sh:row_sh + 1, :] += _colsum(dz)
            st_ref[row_sc:row_sc + 1, :] += _colsum(dz * (xh * g))
            st_ref[2:3, :] += _colsum(dxn * xh)
            return _rms_bwd(dxn * g, xh, r)

        @pl.when(i >= nbl)
        def _():
            common(ctx_ref, dzc_ref[...], mod_ref[3:4, :], 3, 4)

        @pl.when(i < nbl)
        def _():
            gx_ref[...] = dx1_ref[...] + common(x_ref, dzl_ref[...], mod_ref[1:2, :], 0, 1)

    lat = lambda i: (jnp.minimum(i, nbl - 1), 0)
    cix = lambda i: (jnp.maximum(i - nbl, 0), 0)
    return _pcall(
        body,
        name="norm1_mod_bwd",
        out_shape=[jax.ShapeDtypeStruct((t, d), F32), jax.ShapeDtypeStruct((8, d), F32)],
        grid=((tc + t) // rb,),
        in_specs=[
            pl.BlockSpec((rb, d), cix),
            pl.BlockSpec((rb, d), lat),
            pl.BlockSpec((1, d), lambda i: (0, 0)),
            pl.BlockSpec((8, d), lambda i: (0, 0)),
            pl.BlockSpec((rb, d), cix),
            pl.BlockSpec((rb, d), lat),
            pl.BlockSpec((rb, d), lat),
        ],
        out_specs=[pl.BlockSpec((rb, d), lat), pl.BlockSpec((8, d), lambda i: (0, 0))],
        compiler_params=_cparams(("arbitrary",)),
    )(ctx, x, gain, mods, dz_ctx, dz_lat, dx1)


def _key_prep_fwd(kv, kv_gain, kb_gain, tabs):
    ta, wkv = kv.shape
    kvl = MLA_KV_LORA
    nb = GQA_KV_HEADS * GQA_HEAD_DIM
    rb = ROW_BLOCK if ta % ROW_BLOCK == 0 else LANE
    hd = GQA_HEAD_DIM

    def body(kv_ref, g_ref, gb_ref, ca, s1a, s2a, cb, s1b, s2b, kin_ref, kb_ref, vb_ref):
        xh, _ = _rms(kv_ref[:, 0:kvl])
        kin_ref[:, 0:kvl] = (xh * g_ref[...]).astype(BF16)
        kpe = kv_ref[:, kvl + 2 * nb:kvl + 2 * nb + LANE]
        kin_ref[:, kvl:kvl + LANE] = _rope(kpe, ca[...], s1a[...], s2a[...], MLA_ROPE // 4).astype(BF16)
        for h in range(GQA_KV_HEADS):
            nh, _ = _rms(kv_ref[:, kvl + h * hd:kvl + (h + 1) * hd])
            kb_ref[:, h * hd:(h + 1) * hd] = _rope(nh * gb_ref[...], cb[...], s1b[...], s2b[...], hd // 4).astype(BF16)
        vb_ref[...] = kv_ref[:, kvl + nb:kvl + 2 * nb].astype(BF16)

    row = lambda w: pl.BlockSpec((rb, w), lambda i: (i, 0))
    fix = lambda w: pl.BlockSpec((1, w), lambda i: (0, 0))
    return _pcall(
        body,
        name="key_prep_fwd",
        out_shape=[jax.ShapeDtypeStruct((ta, kvl + LANE), BF16), jax.ShapeDtypeStruct((ta, nb), BF16),
                   jax.ShapeDtypeStruct((ta, nb), BF16)],
        grid=(ta // rb,),
        in_specs=[row(wkv), fix(kvl), fix(hd)] + [row(LANE)] * 3 + [row(hd)] * 3,
        out_specs=[row(kvl + LANE), row(nb), row(nb)],
        compiler_params=_cparams(("parallel",)),
    )(kv, kv_gain, kb_gain, *tabs)


def _key_prep_bwd(kv, kv_gain, kb_gain, tabs, dkin, dkb, dvb):
    ta, wkv = kv.shape
    kvl = MLA_KV_LORA
    nb = GQA_KV_HEADS * GQA_HEAD_DIM
    rb = ROW_BLOCK if ta % ROW_BLOCK == 0 else LANE
    hd = GQA_HEAD_DIM

    def body(kv_ref, g_ref, gb_ref, ca, s1a, s2a, cb, s1b, s2b, dkin_ref, dkb_ref, dvb_ref, dkv_ref, st_ref, stb_ref):
        @pl.when(pl.program_id(0) == 0)
        def _():
            st_ref[...] = jnp.zeros_like(st_ref)
            stb_ref[...] = jnp.zeros_like(stb_ref)

        xh, r = _rms(kv_ref[:, 0:kvl])
        dn = dkin_ref[:, 0:kvl]
        st_ref[0:1, :] += _colsum(dn * xh)
        dkv_ref[:, 0:kvl] = _rms_bwd(dn * g_ref[...], xh, r).astype(BF16)
        dpe = _rope_t(dkin_ref[:, kvl:kvl + LANE], ca[...], s1a[...], s2a[...], MLA_ROPE // 4)
        dkv_ref[:, kvl + 2 * nb:kvl + 2 * nb + LANE] = dpe.astype(BF16)
        for h in range(GQA_KV_HEADS):
            nh, rh = _rms(kv_ref[:, kvl + h * hd:kvl + (h + 1) * hd])
            dn_h = _rope_t(dkb_ref[:, h * hd:(h + 1) * hd], cb[...], s1b[...], s2b[...], hd // 4)
            stb_ref[0:1, :] += _colsum(dn_h * nh)
            dkv_ref[:, kvl + h * hd:kvl + (h + 1) * hd] = _rms_bwd(dn_h * gb_ref[...], nh, rh).astype(BF16)
        dkv_ref[:, kvl + nb:kvl + 2 * nb] = dvb_ref[...].astype(BF16)

    row = lambda w: pl.BlockSpec((rb, w), lambda i: (i, 0))
    fix = lambda w: pl.BlockSpec((1, w), lambda i: (0, 0))
    return _pcall(
        body,
        name="key_prep_bwd",
        out_shape=[jax.ShapeDtypeStruct((ta, wkv), BF16), jax.ShapeDtypeStruct((8, kvl), F32),
                   jax.ShapeDtypeStruct((8, hd), F32)],
        grid=(ta // rb,),
        in_specs=[row(wkv), fix(kvl), fix(hd)] + [row(LANE)] * 3 + [row(hd)] * 3 + [row(kvl + LANE), row(nb), row(nb)],
        out_specs=[row(wkv), pl.BlockSpec((8, kvl), lambda i: (0, 0)), pl.BlockSpec((8, hd), lambda i: (0, 0))],
        compiler_params=_cparams(("arbitrary",)),
    )(kv, kv_gain, kb_gain, *tabs, dkin, dkb, dvb)


def _q_prep_fwd(qg, q_gain, qb_gain, tabs):
    t = qg.shape[0]
    ql = MLA_Q_LORA
    hd = GQA_HEAD_DIM
    hb = GQA_HEADS * hd
    rb = min(ROW_BLOCK, t)

    def body(q_ref, g_ref, gb_ref, cb, s1b, s2b, cqn_ref, qb_ref):
        xh, _ = _rms(q_ref[:, 0:ql])
        cqn_ref[...] = (xh * g_ref[...]).astype(BF16)
        for h in range(GQA_HEADS):
            nh, _ = _rms(q_ref[:, ql + h * hd:ql + (h + 1) * hd])
            qb_ref[:, h * hd:(h + 1) * hd] = _rope(nh * gb_ref[...], cb[...], s1b[...], s2b[...], hd // 4).astype(BF16)

    row = lambda w: pl.BlockSpec((rb, w), lambda i: (i, 0))
    fix = lambda w: pl.BlockSpec((1, w), lambda i: (0, 0))
    return _pcall(
        body,
        name="q_prep_fwd",
        out_shape=[jax.ShapeDtypeStruct((t, ql), BF16), jax.ShapeDtypeStruct((t, hb), BF16)],
        grid=(t // rb,),
        in_specs=[row(ql + hb), fix(ql), fix(hd)] + [row(hd)] * 3,
        out_specs=[row(ql), row(hb)],
        compiler_params=_cparams(("parallel",)),
    )(qg, q_gain, qb_gain, *tabs)


def _q_prep_bwd(qg, q_gain, qb_gain, tabs, dcqn, dqb, wpad):
    t = qg.shape[0]
    ql = MLA_Q_LORA
    hd = GQA_HEAD_DIM
    hb = GQA_HEADS * hd
    rb = min(ROW_BLOCK, t)

    def body(q_ref, g_ref, gb_ref, cb, s1b, s2b, dcqn_ref, dqb_ref, dq_ref, st_ref, stb_ref):
        @pl.when(pl.program_id(0) == 0)
        def _():
            st_ref[...] = jnp.zeros_like(st_ref)
            stb_ref[...] = jnp.zeros_like(stb_ref)

        xh, r = _rms(q_ref[:, 0:ql])
        dn = dcqn_ref[...]
        st_ref[0:1, :] += _colsum(dn * xh)
        dq_ref[:, 0:ql] = _rms_bwd(dn * g_ref[...], xh, r).astype(BF16)
        for h in range(GQA_HEADS):
            nh, rh = _rms(q_ref[:, ql + h * hd:ql + (h + 1) * hd])
            dn_h = _rope_t(dqb_ref[:, h * hd:(h + 1) * hd], cb[...], s1b[...], s2b[...], hd // 4)
            stb_ref[0:1, :] += _colsum(dn_h * nh)
            dq_ref[:, ql + h * hd:ql + (h + 1) * hd] = _rms_bwd(dn_h * gb_ref[...], nh, rh).astype(BF16)
        if wpad:
            dq_ref[:, ql + hb:ql + hb + wpad] = jnp.zeros((rb, wpad), BF16)

    row = lambda w: pl.BlockSpec((rb, w), lambda i: (i, 0))
    fix = lambda w: pl.BlockSpec((1, w), lambda i: (0, 0))
    return _pcall(
        body,
        name="q_prep_bwd",
        out_shape=[jax.ShapeDtypeStruct((t, ql + hb + wpad), BF16), jax.ShapeDtypeStruct((8, ql), F32),
                   jax.ShapeDtypeStruct((8, hd), F32)],
        grid=(t // rb,),
        in_specs=[row(ql + hb), fix(ql), fix(hd)] + [row(hd)] * 3 + [row(ql), row(hb)],
        out_specs=[row(ql + hb + wpad), pl.BlockSpec((8, ql), lambda i: (0, 0)), pl.BlockSpec((8, hd), lambda i: (0, 0))],
        compiler_params=_cparams(("arbitrary",)),
    )(qg, q_gain, qb_gain, *tabs, dcqn, dqb)


def _rope_a(v, tabs, transpose, out_dtype, name):
    t, w = v.shape
    rb = min(ROW_BLOCK, t)
    fn = _rope_t if transpose else _rope

    def body(v_ref, c, s1, s2, o_ref):
        for h in range(w // MLA_SLOT):
            sl = slice(h * MLA_SLOT, (h + 1) * MLA_SLOT)
            o_ref[:, sl] = fn(v_ref[:, sl].astype(F32), c[...], s1[...], s2[...], MLA_ROPE // 4).astype(out_dtype)

    row = lambda ww: pl.BlockSpec((rb, ww), lambda i: (i, 0))
    return _pcall(
        body,
        name=name,
        out_shape=jax.ShapeDtypeStruct((t, w), out_dtype),
        grid=(t // rb,),
        in_specs=[row(w)] + [row(MLA_SLOT)] * 3,
        out_specs=row(w),
        compiler_params=_cparams(("parallel",)),
    )(v, *tabs)


def _merge_fwd(pa, pb, qg, gate_blk):
    t, d = pa.shape
    rb = min(ROW_BLOCK, t)

    def body(pa_ref, pb_ref, ga_ref, gb_ref, o_ref):
        o_ref[...] = (jax.nn.sigmoid(ga_ref[...]) * pa_ref[...] + jax.nn.sigmoid(gb_ref[...]) * pb_ref[...]).astype(BF16)

    row = pl.BlockSpec((rb, d), lambda i: (i, 0))
    return _pcall(
        body,
        name="merge_fwd",
        out_shape=jax.ShapeDtypeStruct((t, d), BF16),
        grid=(t // rb,),
        in_specs=[row, row, pl.BlockSpec((rb, d), lambda i: (i, gate_blk)), pl.BlockSpec((rb, d), lambda i: (i, gate_blk + 1))],
        out_specs=row,
        compiler_params=_cparams(("parallel",)),
    )(pa, pb, qg, qg)


def _merge_bwd(dm, pa, pb, qg, gate_blk):
    t, d = pa.shape
    rb = min(ROW_BLOCK, t)

    def body(dm_ref, pa_ref, pb_ref, ga_ref, gb_ref, dpa_ref, dpb_ref, dg_ref):
        dmv = dm_ref[...]
        sa = jax.nn.sigmoid(ga_ref[...])
        sb = jax.nn.sigmoid(gb_ref[...])
        dpa_ref[...] = (dmv * sa).astype(BF16)
        dpb_ref[...] = (dmv * sb).astype(BF16)
        dg_ref[:, 0:d] = (dmv * pa_ref[...] * (sa * (1.0 - sa))).astype(BF16)
        dg_ref[:, d:2 * d] = (dmv * pb_ref[...] * (sb * (1.0 - sb))).astype(BF16)

    row = pl.BlockSpec((rb, d), lambda i: (i, 0))
    return _pcall(
        body,
        name="merge_bwd",
        out_shape=[jax.ShapeDtypeStruct((t, d), BF16), jax.ShapeDtypeStruct((t, d), BF16),
                   jax.ShapeDtypeStruct((t, 2 * d), BF16)],
        grid=(t // rb,),
        in_specs=[row, row, row, pl.BlockSpec((rb, d), lambda i: (i, gate_blk)), pl.BlockSpec((rb, d), lambda i: (i, gate_blk + 1))],
        out_specs=[row, row, pl.BlockSpec((rb, 2 * d), lambda i: (i, 0))],
        compiler_params=_cparams(("parallel",)),
    )(dm, pa, pb, qg, qg)


def _resid_norm_mod(x, branch, gain, mods, name):
    t, d = x.shape
    rb = min(ROW_BLOCK, t)

    def body(x_ref, b_ref, g_ref, mod_ref, x1_ref, z_ref):
        x1 = x_ref[...] + mod_ref[0:1, :] * b_ref[...]
        x1_ref[...] = x1
        xh, _ = _rms(x1)
        z_ref[...] = ((xh * g_ref[...]) * (1.0 + mod_ref[2:3, :]) + mod_ref[1:2, :]).astype(BF16)

    row = pl.BlockSpec((rb, d), lambda i: (i, 0))
    return _pcall(
        body,
        name=name,
        out_shape=[jax.ShapeDtypeStruct((t, d), F32), jax.ShapeDtypeStruct((t, d), BF16)],
        grid=(t // rb,),
        in_specs=[row, row, pl.BlockSpec((1, d), lambda i: (0, 0)), pl.BlockSpec((8, d), lambda i: (0, 0))],
        out_specs=[row, row],
        compiler_params=_cparams(("parallel",)),
    )(x, branch, gain, mods)


def _norm2_bwd(x1, attn, gain, mods, dz2, dx2):
    t, d = x1.shape
    rb = min(ROW_BLOCK, t)

    def body(x1_ref, at_ref, g_ref, mod_ref, dz_ref, dx2_ref, dx1_ref, da_ref, st_ref):
        @pl.when(pl.program_id(0) == 0)
        def _():
            st_ref[...] = jnp.zeros_like(st_ref)

        xh, r = _rms(x1_ref[...])
        g = g_ref[...]
        dz = dz_ref[...]
        dxn = dz * (1.0 + mod_ref[1:2, :])
        st_ref[0:1, :] += _colsum(dz)
        st_ref[1:2, :] += _colsum(dz * (xh * g))
        st_ref[2:3, :] += _colsum(dxn * xh)
        dx1 = dx2_ref[...] + _rms_bwd(dxn * g, xh, r)
        dx1_ref[...] = dx1
        st_ref[3:4, :] += _colsum(dx1 * at_ref[...])
        da_ref[...] = (dx1 * mod_ref[0:1, :]).astype(BF16)

    row = pl.BlockSpec((rb, d), lambda i: (i, 0))
    return _pcall(
        body,
        name="norm2_mod_bwd",
        out_shape=[jax.ShapeDtypeStruct((t, d), F32), jax.ShapeDtypeStruct((t, d), BF16), jax.ShapeDtypeStruct((8, d), F32)],
        grid=(t // rb,),
        in_specs=[row, row, pl.BlockSpec((1, d), lambda i: (0, 0)), pl.BlockSpec((8, d), lambda i: (0, 0)), row, row],
        out_specs=[row, row, pl.BlockSpec((8, d), lambda i: (0, 0))],
        compiler_params=_cparams(("arbitrary",)),
    )(x1, attn, gain, mods, dz2, dx2)


def _final_loss(x1, ffn, gain, mods, target):
    t, d = x1.shape
    rb = min(ROW_BLOCK, t)
    nb = t // rb

    def body(x1_ref, f_ref, g_ref, mod_ref, tg_ref, dx2_ref, df_ref, st_ref):
        i = pl.program_id(0)

        @pl.when(i == 0)
        def _():
            st_ref[...] = jnp.zeros_like(st_ref)

        ffn_v = f_ref[...]
        g2 = mod_ref[0:1, :]
        x2 = x1_ref[...] + g2 * ffn_v
        xh, r = _rms(x2)
        g = g_ref[...]
        err = xh * g - tg_ref[...]
        st_ref[2:3, :] += _colsum(err * err) * (0.5 / d)
        dy = err * (1.0 / d)
        st_ref[0:1, :] += _colsum(dy * xh)
        dx2 = _rms_bwd(dy * g, xh, r)
        dx2_ref[...] = dx2
        st_ref[1:2, :] += _colsum(dx2 * ffn_v)
        df_ref[...] = (dx2 * g2).astype(BF16)

        @pl.when(i == nb - 1)
        def _():
            st_ref[3:4, :] = jnp.broadcast_to(jnp.sum(st_ref[2:3, :], axis=-1, keepdims=True), (1, d))

    row = pl.BlockSpec((rb, d), lambda i: (i, 0))
    return _pcall(
        body,
        name="final_norm_loss",
        out_shape=[jax.ShapeDtypeStruct((t, d), F32), jax.ShapeDtypeStruct((t, d), BF16), jax.ShapeDtypeStruct((8, d), F32)],
        grid=(nb,),
        in_specs=[row, row, pl.BlockSpec((1, d), lambda i: (0, 0)), pl.BlockSpec((8, d), lambda i: (0, 0)), row],
        out_specs=[row, row, pl.BlockSpec((8, d), lambda i: (0, 0))],
        compiler_params=_cparams(("arbitrary",)),
    )(x1, ffn, gain, mods, target)


def _shift_rows(v, down):
    n = v.shape[0]
    rows = lax.broadcasted_iota(jnp.int32, v.shape, 0)
    if down:
        return jnp.where(rows == 0, 0.0, pltpu.roll(v, 1, 0))
    return jnp.where(rows == n - 1, 0.0, pltpu.roll(v, n - 1, 0))


def _conv_act(ua, ub, cwa, cwb, cba, cbb):
    a = cba + cwa[0:1, :] * _shift_rows(ua, True) + cwa[1:2, :] * ua + cwa[2:3, :] * _shift_rows(ua, False)
    b = cbb + cwb[0:1, :] * _shift_rows(ub, True) + cwb[1:2, :] * ub + cwb[2:3, :] * _shift_rows(ub, False)
    return a, b


def _conv_fwd(u, cw, cb):
    t, f2 = u.shape
    f = f2 // 2
    cbk = _tile(f, 256)
    nf = f // cbk

    def body(ua_ref, ub_ref, cwa_ref, cwb_ref, cba_ref, cbb_ref, h_ref):
        a, b = _conv_act(ua_ref[...], ub_ref[...], cwa_ref[...], cwb_ref[...], cba_ref[...], cbb_ref[...])
        h_ref[...] = (a * jax.nn.sigmoid(a) * b).astype(BF16)

    ca = lambda r: pl.BlockSpec((r, cbk), lambda j: (0, j))
    cbs = lambda r: pl.BlockSpec((r, cbk), lambda j: (0, nf + j))
    return _pcall(
        body,
        name="conv_gate_fwd",
        out_shape=jax.ShapeDtypeStruct((t, f), BF16),
        grid=(nf,),
        in_specs=[ca(t), cbs(t), ca(3), cbs(3), ca(1), cbs(1)],
        out_specs=ca(t),
        compiler_params=_cparams(("parallel",)),
    )(u, u, cw, cw, cb, cb)


def _conv_bwd(u, cw, cb, dh):
    t, f2 = u.shape
    f = f2 // 2
    cbk = _tile(f, 256)
    nf = f // cbk

    def body(ua_ref, ub_ref, cwa_ref, cwb_ref, cba_ref, cbb_ref, dh_ref, du_ref, dcw_ref, dcb_ref):
        ua, ub = ua_ref[...], ub_ref[...]
        cwa, cwb = cwa_ref[...], cwb_ref[...]
        a, b = _conv_act(ua, ub, cwa, cwb, cba_ref[...], cbb_ref[...])
        dh_v = dh_ref[...]
        sg = jax.nn.sigmoid(a)
        db = dh_v * (a * sg)
        da = dh_v * b * (sg * (1.0 + a * (1.0 - sg)))
        for idx, (dv, uu, cwv) in enumerate(((da, ua, cwa), (db, ub, cwb))):
            dcb_ref[idx] = _colsum(dv)
            dcw_ref[idx, 0:1, :] = _colsum(dv * _shift_rows(uu, True))
            dcw_ref[idx, 1:2, :] = _colsum(dv * uu)
            dcw_ref[idx, 2:3, :] = _colsum(dv * _shift_rows(uu, False))
            du = cwv[0:1, :] * _shift_rows(dv, False) + cwv[1:2, :] * dv + cwv[2:3, :] * _shift_rows(dv, True)
            du_ref[idx] = du.astype(BF16)

    ca = lambda r: pl.BlockSpec((r, cbk), lambda j: (0, j))
    cbs = lambda r: pl.BlockSpec((r, cbk), lambda j: (0, nf + j))
    o3 = lambda r: pl.BlockSpec((2, r, cbk), lambda j: (0, 0, j))
    return _pcall(
        body,
        name="conv_gate_bwd",
        out_shape=[jax.ShapeDtypeStruct((2, t, f), BF16), jax.ShapeDtypeStruct((2, 3, f), F32),
                   jax.ShapeDtypeStruct((2, 1, f), F32)],
        grid=(nf,),
        in_specs=[ca(t), cbs(t), ca(3), cbs(3), ca(1), cbs(1), ca(t)],
        out_specs=[o3(t), o3(3), o3(1)],
        compiler_params=_cparams(("parallel",)),
    )(u, u, cw, cw, cb, cb, dh)


def _attention_fwd(q, kk, vv, scale, *, hq, hkv, dk, dv, k_blk0, v_blk0, name):
    t = q.shape[0]
    tk = kk.shape[0]
    g_sz = hq // hkv
    tq = min(ATT_Q_BLOCK, t)

    def body(q_ref, k_ref, v_ref, o_ref, lse_ref):
        k = k_ref[...]
        v = v_ref[...]
        for j in range(g_sz):
            s = lax.dot_general(q_ref[:, j * dk:(j + 1) * dk], k, _DIMS["nt"], preferred_element_type=F32) * scale
            m = jnp.max(s, axis=-1, keepdims=True)
            p = jnp.exp(s - m)
            l = jnp.sum(p, axis=-1, keepdims=True)
            o = jnp.dot(p.astype(BF16), v, preferred_element_type=F32) / l
            o_ref[:, j * dv:(j + 1) * dv] = o.astype(BF16)
            lse_ref[0, :, j:j + 1] = m + jnp.log(l)

    return _pcall(
        body,
        name=name,
        out_shape=[jax.ShapeDtypeStruct((t, hq * dv), BF16), jax.ShapeDtypeStruct((hkv, t, g_sz), F32)],
        grid=(hkv, t // tq),
        in_specs=[
            pl.BlockSpec((tq, g_sz * dk), lambda g, i: (i, g)),
            pl.BlockSpec((tk, dk), lambda g, i: (0, k_blk0 + g)),
            pl.BlockSpec((tk, dv), lambda g, i: (0, v_blk0 + g)),
        ],
        out_specs=[
            pl.BlockSpec((tq, g_sz * dv), lambda g, i: (i, g)),
            pl.BlockSpec((1, tq, g_sz), lambda g, i: (g, i, 0)),
        ],
        compiler_params=_cparams(("parallel", "parallel")),
    )(q, kk, vv)


def _attention_bwd(q, kk, vv, do, lse, scale, *, hq, hkv, dk, dv, k_blk0, v_blk0, name):
    t = q.shape[0]
    tk = kk.shape[0]
    g_sz = hq // hkv
    tq = min(ATT_Q_BLOCK, t)

    def body(q_ref, k_ref, v_ref, do_ref, lse_ref, dq_ref, dk_ref, dv_ref):
        @pl.when(pl.program_id(1) == 0)
        def _():
            dk_ref[...] = jnp.zeros_like(dk_ref)
            dv_ref[...] = jnp.zeros_like(dv_ref)

        k = k_ref[...]
        v = v_ref[...]
        for j in range(g_sz):
            qj = q_ref[:, j * dk:(j + 1) * dk]
            doj = do_ref[:, j * dv:(j + 1) * dv]
            s = lax.dot_general(qj, k, _DIMS["nt"], preferred_element_type=F32) * scale
            p = jnp.exp(s - lse_ref[0, :, j:j + 1])
            dp = lax.dot_general(doj, v, _DIMS["nt"], preferred_element_type=F32)
            ds = (p * (dp - jnp.sum(p * dp, axis=-1, keepdims=True)) * scale).astype(BF16)
            dv_ref[...] += lax.dot_general(p.astype(BF16), doj, _DIMS["tn"], preferred_element_type=F32)
            dk_ref[...] += lax.dot_general(ds, qj, _DIMS["tn"], preferred_element_type=F32)
            dq_ref[:, j * dk:(j + 1) * dk] = jnp.dot(ds, k, preferred_element_type=F32)

    return _pcall(
        body,
        name=name,
        out_shape=[jax.ShapeDtypeStruct((t, hq * dk), F32), jax.ShapeDtypeStruct((tk, hkv * dk), F32),
                   jax.ShapeDtypeStruct((tk, hkv * dv), F32)],
        grid=(hkv, t // tq),
        in_specs=[
            pl.BlockSpec((tq, g_sz * dk), lambda g, i: (i, g)),
            pl.BlockSpec((tk, dk), lambda g, i: (0, k_blk0 + g)),
            pl.BlockSpec((tk, dv), lambda g, i: (0, v_blk0 + g)),
            pl.BlockSpec((tq, g_sz * dv), lambda g, i: (i, g)),
            pl.BlockSpec((1, tq, g_sz), lambda g, i: (g, i, 0)),
        ],
        out_specs=[
            pl.BlockSpec((tq, g_sz * dk), lambda g, i: (i, g)),
            pl.BlockSpec((tk, dk), lambda g, i: (0, g)),
            pl.BlockSpec((tk, dv), lambda g, i: (0, g)),
        ],
        compiler_params=_cparams(("parallel", "arbitrary")),
    )(q, kk, vv, do, lse)


def _silu(v):
    return v * jax.nn.sigmoid(v)


def _ada_fwd(conds, w_ada, b_ada_shard):
    r, d = conds.shape
    n = w_ada.shape[1]
    tn = _tile(n, 512)

    def body(c_ref, w_ref, b_ref, o_ref):
        s = _silu(c_ref[...]).astype(BF16)
        o_ref[...] = jnp.dot(s, w_ref[...].astype(BF16), preferred_element_type=F32) + b_ref[...]

    return _pcall(
        body,
        name="ada_fwd",
        out_shape=jax.ShapeDtypeStruct((r, n), F32),
        grid=(n // tn,),
        in_specs=[pl.BlockSpec((r, d), lambda j: (0, 0)), pl.BlockSpec((d, tn), lambda j: (0, j)),
                  pl.BlockSpec((1, tn), lambda j: (0, j))],
        out_specs=pl.BlockSpec((r, tn), lambda j: (0, j)),
        compiler_params=_cparams(("parallel",)),
    )(conds, w_ada, b_ada_shard)


def _cctx_partial(da16_shard, w_ada, c_ctx_row):
    d, n = w_ada.shape
    td = _tile(d, 512)

    def body(g_ref, w_ref, c_ref, o_ref):
        ds = lax.dot_general(g_ref[8:16, :].astype(BF16), w_ref[...].astype(BF16), _DIMS["nt"],
                             preferred_element_type=F32)
        cv = c_ref[...]
        sg = jax.nn.sigmoid(cv)
        o_ref[...] = ds * (sg * (1.0 + cv * (1.0 - sg)))

    return _pcall(
        body,
        name="cctx_partial",
        out_shape=jax.ShapeDtypeStruct((8, d), F32),
        grid=(d // td,),
        in_specs=[pl.BlockSpec((16, n), lambda j: (0, 0)), pl.BlockSpec((td, n), lambda j: (j, 0)),
                  pl.BlockSpec((1, td), lambda j: (0, j))],
        out_specs=pl.BlockSpec((8, td), lambda j: (0, j)),
        compiler_params=_cparams(("parallel",)),
    )(da16_shard, w_ada, c_ctx_row)


def _sum_parts(parts):
    p, _, n = parts.shape

    def body(p_ref, o_ref):
        acc = p_ref[0]
        for s in range(1, p):
            acc = acc + p_ref[s]
        o_ref[...] = acc

    return _pcall(
        body,
        name="sum_parts",
        out_shape=jax.ShapeDtypeStruct((1, n), F32),
        in_specs=[pl.BlockSpec(memory_space=pltpu.VMEM)],
        out_specs=pl.BlockSpec(memory_space=pltpu.VMEM),
    )(parts)


def _adam_math(w, g, m, v):
    m2 = ADAM_B1 * m + (1.0 - ADAM_B1) * g
    v2 = ADAM_B2 * v + (1.0 - ADAM_B2) * jnp.square(g)
    m_hat = m2 / (1.0 - ADAM_B1 ** ADAM_STEP)
    v_hat = v2 / (1.0 - ADAM_B2 ** ADAM_STEP)
    delta = -ADAM_LR * (m_hat / (jnp.sqrt(v_hat) + ADAM_EPS) + ADAM_WD * w)
    return delta, m2, v2


def _adamw(parts, w, m, v, name):
    p, r, c = parts.shape
    rb = _tile(r, max(8, (1 << 20) // (4 * c) // 8 * 8), 8)

    def body(p_ref, w_ref, m_ref, v_ref, g_ref, d_ref, m2_ref, v2_ref):
        g = p_ref[0].astype(F32)
        for s in range(1, p):
            g = g + p_ref[s].astype(F32)
        g_ref[...] = g
        d_ref[...], m2_ref[...], v2_ref[...] = _adam_math(w_ref[...], g, m_ref[...], v_ref[...])

    row = pl.BlockSpec((rb, c), lambda i: (i, 0))
    return _pcall(
        body,
        name=name,
        out_shape=[jax.ShapeDtypeStruct((r, c), F32)] * 4,
        grid=(r // rb,),
        in_specs=[pl.BlockSpec((p, rb, c), lambda i: (0, i, 0)), row, row, row],
        out_specs=[row] * 4,
        compiler_params=_cparams(("parallel",)),
    )(parts, w, m, v)


def _adamw_ada(conds, da16, w, m, v):
    d, n = w.shape
    rb = _tile(d, 256, LANE)

    def body(s_ref, da_ref, w_ref, m_ref, v_ref, g_ref, d_ref, m2_ref, v2_ref):
        g = lax.dot_general(_silu(s_ref[...]).astype(BF16), da_ref[...].astype(BF16), _DIMS["tn"],
                            preferred_element_type=F32)
        g_ref[...] = g
        d_ref[...], m2_ref[...], v2_ref[...] = _adam_math(w_ref[...], g, m_ref[...], v_ref[...])

    row = pl.BlockSpec((rb, n), lambda i: (i, 0))
    return _pcall(
        body,
        name="adamw_w_ada",
        out_shape=[jax.ShapeDtypeStruct((d, n), F32)] * 4,
        grid=(d // rb,),
        in_specs=[pl.BlockSpec((16, rb), lambda i: (0, i)), pl.BlockSpec((16, n), lambda i: (0, 0)), row, row, row],
        out_specs=[row] * 4,
        compiler_params=_cparams(("parallel",)),
    )(conds, da16, w, m, v)


def _cast_bf16(a, name):
    r, c = a.shape
    rb = _tile(r, 512, 8)

    def body(a_ref, o_ref):
        o_ref[...] = a_ref[...].astype(BF16)

    row = pl.BlockSpec((rb, c), lambda i: (i, 0))
    return _pcall(body, name=name, out_shape=jax.ShapeDtypeStruct((r, c), BF16), grid=(r // rb,),
                          in_specs=[row], out_specs=row, compiler_params=_cparams(("parallel",)))(a)


def _rope_tabs(t, rot):
    half, q = rot // 2, rot // 4
    n_rows = t // GRID_W
    row = jnp.repeat(jnp.arange(n_rows, dtype=F32), GRID_W)
    col = jnp.tile(jnp.arange(GRID_W, dtype=F32), n_rows)
    inv_freq = ROPE_THETA ** (-jnp.arange(0, half, 2, dtype=F32) / half)
    ang = jnp.concatenate([row[:, None] * inv_freq, col[:, None] * inv_freq], axis=-1)
    cos, sin = jnp.cos(ang), jnp.sin(ang)
    c0, c1, s0, s1 = cos[:, :q], cos[:, q:], sin[:, :q], sin[:, q:]
    z = jnp.zeros_like(s0)
    return (jnp.concatenate([c0, c0, c1, c1], -1), jnp.concatenate([-s0, z, -s1, z], -1),
            jnp.concatenate([z, s0, z, s1], -1))


def _pad_cols(a, left, total, fill=0.0):
    return jnp.pad(a, ((0, 0), (left, total - left - a.shape[1])), constant_values=fill)


def _with_ctx_rows(tab, tc, fill):
    return jnp.concatenate([tab, jnp.full((tc, tab.shape[1]), fill, F32)], axis=0)


def kernel(x, c, ctx, c_ctx, w_ada, b_ada, norm1_g, w_in, mla_q_norm_g, w_q_up, mla_kv_norm_g, w_kv_up, gqa_q_norm_g, gqa_k_norm_g, w_br_a, w_br_b, w_out, norm2_g, w_up, conv_w, conv_b, w_down, final_norm_g, loss_target, m_c_ctx, m_w_ada, m_b_ada, m_norm1_g, m_w_in, m_mla_q_norm_g, m_w_q_up, m_mla_kv_norm_g, m_w_kv_up, m_gqa_q_norm_g, m_gqa_k_norm_g, m_w_br_a, m_w_br_b, m_w_out, m_norm2_g, m_w_up, m_conv_w, m_conv_b, m_w_down, m_final_norm_g, v_c_ctx, v_w_ada, v_b_ada, v_norm1_g, v_w_in, v_mla_q_norm_g, v_w_q_up, v_mla_kv_norm_g, v_w_kv_up, v_gqa_q_norm_g, v_gqa_k_norm_g, v_w_br_a, v_w_br_b, v_w_out, v_norm2_g, v_w_up, v_conv_w, v_conv_b, v_w_down, v_final_norm_g):
    weights = dict(c_ctx=c_ctx, w_ada=w_ada, b_ada=b_ada, norm1_g=norm1_g, w_in=w_in, mla_q_norm_g=mla_q_norm_g,
                   w_q_up=w_q_up, mla_kv_norm_g=mla_kv_norm_g, w_kv_up=w_kv_up, gqa_q_norm_g=gqa_q_norm_g,
                   gqa_k_norm_g=gqa_k_norm_g, w_br_a=w_br_a, w_br_b=w_br_b, w_out=w_out, norm2_g=norm2_g, w_up=w_up,
                   conv_w=conv_w, conv_b=conv_b, w_down=w_down, final_norm_g=final_norm_g)
    mom_m = dict(c_ctx=m_c_ctx, w_ada=m_w_ada, b_ada=m_b_ada, norm1_g=m_norm1_g, w_in=m_w_in, mla_q_norm_g=m_mla_q_norm_g,
                 w_q_up=m_w_q_up, mla_kv_norm_g=m_mla_kv_norm_g, w_kv_up=m_w_kv_up, gqa_q_norm_g=m_gqa_q_norm_g,
                 gqa_k_norm_g=m_gqa_k_norm_g, w_br_a=m_w_br_a, w_br_b=m_w_br_b, w_out=m_w_out, norm2_g=m_norm2_g,
                 w_up=m_w_up, conv_w=m_conv_w, conv_b=m_conv_b, w_down=m_w_down, final_norm_g=m_final_norm_g)
    mom_v = dict(c_ctx=v_c_ctx, w_ada=v_w_ada, b_ada=v_b_ada, norm1_g=v_norm1_g, w_in=v_w_in, mla_q_norm_g=v_mla_q_norm_g,
                 w_q_up=v_w_q_up, mla_kv_norm_g=v_mla_kv_norm_g, w_kv_up=v_w_kv_up, gqa_q_norm_g=v_gqa_q_norm_g,
                 gqa_k_norm_g=v_gqa_k_norm_g, w_br_a=v_w_br_a, w_br_b=v_w_br_b, w_out=v_w_out, norm2_g=v_norm2_g,
                 w_up=v_w_up, conv_w=v_conv_w, conv_b=v_conv_b, w_down=v_w_down, final_norm_g=v_final_norm_g)
    order = list(weights)

    my_idx = 4 * lax.axis_index("x") + 2 * lax.axis_index("y") + lax.axis_index("c")
    xs, cts, tgt = x[0], ctx[0], loss_target[0]
    t, d = xs.shape
    tc = cts.shape[0]
    ta = t + tc
    kvl, ql = MLA_KV_LORA, MLA_Q_LORA
    nb = GQA_KV_HEADS * GQA_HEAD_DIM
    hb = GQA_HEADS * GQA_HEAD_DIM
    ha = MLA_HEADS
    f2 = w_up.shape[2] * N_DEV
    ff = f2 // 2

    big = ["w_in", "w_q_up", "w_kv_up", "w_br_a", "w_br_b", "w_out", "w_up", "w_down"]
    nw = len(big)
    del nw
    _ORDER_AFTER.clear()
    shards = {n: _cast_bf16(weights[n][0], "cast_" + n) for n in big}
    c_idx = jnp.reshape(lax.axis_index("c"), (1,)).astype(jnp.int32)

    def gather_start(names, dep):
        shs = [shards[n] for n in names]
        land = [lax.empty((N_DEV,) + s.shape, BF16) for s in shs]
        if dep is not None:
            _after(dep)
        s, r, arrs, tok = _split_start("gather_ici_start_" + names[0], shs + land, _gather_ici_copies(len(names)),
                                       5 * len(names))
        return dict(names=names, s=s, r=r, arrs=arrs, tok=tok)

    def gather_relay(g, after):
        n = len(g["names"])
        arrs = _split_wait("gather_ici_wait_" + g["names"][0], g["s"], g["r"], g["arrs"], _gather_ici_copies(n), after)
        s, r, bufs, tok = _split_start("gather_d2d_start_" + g["names"][0], arrs[n:], _gather_d2d_copies(n), 3 * n)
        g.update(s2=s, r2=r, bufs=bufs)
        return tok

    def gather_finish(g, after):
        n = len(g["names"])
        bufs = _split_wait("gather_d2d_wait_" + g["names"][0], g["s2"], g["r2"], g["bufs"], _gather_d2d_copies(n), after)
        return dict(zip(g["names"], bufs))

    c_all, cw_all = _all_gather([jnp.pad(c, ((0, 7), (0, 0))), jnp.pad(conv_w[0], ((0, 5), (0, 0)))], "gather_cond")
    conv_w_f = jnp.transpose(cw_all[:, :3, :], (1, 0, 2)).reshape(3, f2)
    conds = jnp.concatenate([c_all[:, 0, :], c_ctx[None, :], jnp.zeros((7, d), F32)], axis=0)
    ncol = w_ada.shape[2]
    b_shard = lax.dynamic_slice_in_dim(b_ada, my_idx * ncol, ncol, axis=1)
    ada_shard = _ada_fwd(conds, w_ada[0], b_shard)
    (ada_all,) = _all_gather([ada_shard], "gather_ada")
    ada = jnp.transpose(ada_all, (1, 0, 2)).reshape(16, N_DEV * ncol)
    lat = lax.dynamic_slice_in_dim(ada, my_idx, 1, axis=0).reshape(6, d)
    cxt = ada[8].reshape(6, d)
    zero2 = jnp.zeros((2, d), F32)
    mods1 = jnp.concatenate([lat[0:2], cxt[0:2], jnp.zeros((4, d), F32)], axis=0)
    mods2 = jnp.concatenate([lat[2:3], lat[3:4], lat[4:5], jnp.zeros((5, d), F32)], axis=0)
    mods2b = jnp.concatenate([lat[2:3], lat[4:5], jnp.zeros((6, d), F32)], axis=0)
    mods3 = jnp.concatenate([lat[5:6], jnp.zeros((7, d), F32)], axis=0)
    del zero2

    g0 = gather_start(["w_in"], ada_all)
    g1 = gather_start(["w_q_up", "w_kv_up", "w_br_a", "w_br_b", "w_out"], g0["tok"])
    g2 = gather_start(["w_up"], g1["tok"])
    g3 = gather_start(["w_down"], g2["tok"])

    ca, s1a, s2a = _rope_tabs(t, MLA_ROPE)
    cb_, s1b, s2b = _rope_tabs(t, GQA_HEAD_DIM)
    q_tabs_a = (_pad_cols(jnp.concatenate([jnp.ones((t, MLA_NOPE), F32), ca], 1), 0, MLA_SLOT),
                _pad_cols(s1a, MLA_NOPE, MLA_SLOT), _pad_cols(s2a, MLA_NOPE, MLA_SLOT))
    q_tabs_b = (cb_, s1b, s2b)
    k_tabs = (_with_ctx_rows(_pad_cols(ca, 0, LANE), tc, 1.0), _with_ctx_rows(_pad_cols(s1a, 0, LANE), tc, 0.0),
              _with_ctx_rows(_pad_cols(s2a, 0, LANE), tc, 0.0),
              _with_ctx_rows(cb_, tc, 1.0), _with_ctx_rows(s1b, tc, 0.0), _with_ctx_rows(s2b, tc, 0.0))

    def cols_full(g):
        return jnp.transpose(g, (1, 0, 2)).reshape(g.shape[1], N_DEV * g.shape[2])

    _after(gather_relay(g0, mods1))
    z_all = _norm_mod_fwd(cts, xs, norm1_g, mods1)
    gathered = gather_finish(g0, z_all)
    w_in_f = cols_full(gathered["w_in"])
    o_kpe, o_kb, o_vb = kvl, kvl + MLA_ROPE, kvl + MLA_ROPE + nb
    o_q = o_vb + nb
    o_g = o_q + ql + hb
    wkv_w = kvl + 2 * nb + LANE
    w_kv_p = jnp.concatenate([w_in_f[:, :kvl], w_in_f[:, o_kb:o_q], w_in_f[:, o_kpe:o_kb],
                              jnp.zeros((d, LANE - MLA_ROPE), BF16)], axis=1)
    q_w = ql + hb
    q_pad = (-q_w) % 512 if d >= 512 else (-q_w) % d
    gate_blk = (q_w + q_pad) // d
    assert (q_w + q_pad) % d == 0
    w_qg_p = jnp.concatenate([w_in_f[:, o_q:o_g], jnp.zeros((d, q_pad), BF16), w_in_f[:, o_g:]], axis=1)
    w_cat = jnp.concatenate([w_kv_p, w_qg_p], axis=1)

    kv_all = _mm(z_all, w_kv_p, "nn", F32, "proj_kv", tm=1152, tn=wkv_w)
    qg = _mm(z_all, w_qg_p, "nn", F32, "proj_qg", tm=1024, tn=1024, rows=t)
    _after(gather_relay(g1, qg))
    kin, k_b, v_b = _key_prep_fwd(kv_all, mla_kv_norm_g, gqa_k_norm_g, k_tabs)
    cqn, q_b = _q_prep_fwd(qg, mla_q_norm_g, gqa_q_norm_g, q_tabs_b)
    gathered.update(gather_finish(g1, q_b))

    wq_f = cols_full(gathered["w_q_up"]).reshape(ql, ha, MLA_NOPE + MLA_ROPE)
    wq_ext = jnp.pad(wq_f, ((0, 0), (0, 0), (0, MLA_SLOT - MLA_NOPE - MLA_ROPE))).reshape(ql, ha * MLA_SLOT)
    wkv_f = cols_full(gathered["w_kv_up"]).reshape(kvl, ha, MLA_NOPE + MLA_V)
    wk_slots = jnp.pad(wkv_f[:, :, :MLA_NOPE], ((0, 0), (0, 0), (0, MLA_SLOT - MLA_NOPE))).reshape(kvl, ha * MLA_SLOT)
    wv_cols = wkv_f[:, :, MLA_NOPE:].reshape(kvl, ha * MLA_V)
    e_slot = jnp.pad(jnp.eye(MLA_ROPE, dtype=BF16),
                     ((0, LANE - MLA_ROPE), (MLA_NOPE, MLA_SLOT - MLA_NOPE - MLA_ROPE)))
    e_rows = jnp.concatenate([jnp.tile(e_slot, (1, ha)), jnp.zeros((LANE, ha * MLA_V), BF16)], axis=1)
    wkv_ext = jnp.concatenate([jnp.concatenate([wk_slots, wv_cols], axis=1), e_rows], axis=0)
    w_bra = cols_full(gathered["w_br_a"])
    w_brb = cols_full(gathered["w_br_b"])
    w_out_f = gathered["w_out"].reshape(d, d)

    kv_a = _mm(kin, wkv_ext, "nn", BF16, "kv_up", tm=1152, tn=1024)
    qa_raw = _mm(cqn, wq_ext, "nn", F32, "q_up", tm=1024, tn=1024)
    q_a = _rope_a(qa_raw, q_tabs_a, False, BF16, "rope_q_fwd")
    sc_a = float((MLA_NOPE + MLA_ROPE) ** -0.5)
    sc_b = float(GQA_HEAD_DIM ** -0.5)
    att_a = dict(hq=ha, hkv=ha, dk=MLA_SLOT, dv=MLA_V, k_blk0=0, v_blk0=ha * MLA_SLOT // MLA_V)
    att_b = dict(hq=GQA_HEADS, hkv=GQA_KV_HEADS, dk=GQA_HEAD_DIM, dv=GQA_HEAD_DIM, k_blk0=0, v_blk0=0)
    o_a, lse_a = _attention_fwd(q_a, kv_a, kv_a, sc_a, name="attn_a_fwd", **att_a)
    o_b, lse_b = _attention_fwd(q_b, k_b, v_b, sc_b, name="attn_b_fwd", **att_b)
    _after(gather_relay(g2, o_b))
    pa = _mm(o_a, w_bra, "nn", F32, "br_a", tm=1024, tn=1024)
    pb = _mm(o_b, w_brb, "nn", F32, "br_b", tm=1024, tn=1024)
    merged = _merge_fwd(pa, pb, qg, gate_blk)
    attn = _mm(merged, w_out_f, "nn", F32, "w_out", tm=1024, tn=1024)
    x1, z2 = _resid_norm_mod(xs, attn, norm2_g, mods2, "resid_norm2_fwd")
    w_up3 = gather_finish(g2, z2)["w_up"]
    _after(gather_relay(g3, z2))
    u = _mm_up_fwd(z2, w_up3, "w_up")
    w_down_f = gather_finish(g3, u)["w_down"].reshape(ff, d)
    h = _conv_fwd(u, conv_w_f, conv_b)
    ffn = _mm(h, w_down_f, "nn", F32, "w_down", tm=1024, tn=1024, tk=2816)

    def to_shards(g):
        return jnp.transpose(g.reshape(g.shape[0], N_DEV, g.shape[1] // N_DEV), (1, 0, 2))

    def reduce_start(tag, names, sends):
        n = len(sends)
        land = [lax.empty((4,) + s.shape[1:], s.dtype) for s in sends]
        s, r, arrs, tok = _split_start("reduce_d2d_start_" + tag, sends + land, _reduce_d2d_copies(n), 4 * n)
        return dict(tag=tag, names=names, s=s, r=r, arrs=arrs, tok=tok)

    def reduce_relay(g, after):
        n = len(g["names"])
        arrs = _split_wait("reduce_d2d_wait_" + g["tag"], g["s"], g["r"], g["arrs"], _reduce_d2d_copies(n), after)
        sums = [_pair_sum(arrs[a], arrs[n + a], c_idx, "pair_sum_" + g["names"][a]) for a in range(n)]
        land = [lax.empty(s.shape, s.dtype) for s in sums]
        s, r, arrs2, tok = _split_start("reduce_ici_start_" + g["tag"], sums + land, _reduce_ici_copies(n), 4 * n)
        g.update(s2=s, r2=r, arrs2=arrs2)
        return tok

    def reduce_finish(g, after):
        n = len(g["names"])
        arrs2 = _split_wait("reduce_ici_wait_" + g["tag"], g["s2"], g["r2"], g["arrs2"], _reduce_ici_copies(n), after)
        return dict(zip(g["names"], arrs2[n:]))

    dx2, dffn, st_fin = _final_loss(x1, ffn, final_norm_g[None, :], mods3, tgt)
    loss = lax.psum(st_fin[3, 0], MESH_AXES)
    dh = _mm(dffn, w_down_f, "nt", F32, "d_h", tm=1024, tn=1024)
    g_w_down = _mm(h, dffn, "tn", BF16, "g_w_down", tm=512, tn=1024)
    r_down = reduce_start("down", ["w_down"], [g_w_down.reshape(N_DEV, ff // N_DEV, d)])
    _after(r_down["tok"])
    du3, dcw, dcb = _conv_bwd(u, conv_w_f, conv_b, dh)
    dz2 = _mm_up_dz(du3, w_up3, "d_z2")
    g_w_up = _mm_up_gw(z2, du3, N_DEV, "g_w_up")
    g_conv_w = jnp.concatenate([dcw[0], dcw[1]], axis=1)
    tok = reduce_relay(r_down, g_w_up)
    _after(tok)
    r_up = reduce_start("up", ["w_up", "conv_w"], [g_w_up, to_shards(jnp.pad(g_conv_w, ((0, 5), (0, 0))))])
    _after(tok, r_up["tok"])
    dx1, dattn, st_n2 = _norm2_bwd(x1, attn, norm2_g, mods2b, dz2, dx2)
    dmerged = _mm(dattn, w_out_f, "nt", F32, "d_merged", tm=1024, tn=1024)
    g_w_out = _mm(merged, dattn, "tn", BF16, "g_w_out", tm=1024, tn=1024)
    dpa, dpb, dgates = _merge_bwd(dmerged, pa, pb, qg, gate_blk)
    do_a = _mm(dpa, w_bra, "nt", BF16, "d_o_a", tm=1024, tn=1024)
    do_b = _mm(dpb, w_brb, "nt", BF16, "d_o_b", tm=1024, tn=1024)
    g_w_bra = _mm(o_a, dpa, "tn", BF16, "g_w_br_a", tm=1024, tn=1024)
    g_w_brb = _mm(o_b, dpb, "tn", BF16, "g_w_br_b", tm=1024, tn=1024)
    tok = reduce_relay(r_up, g_w_brb)
    _after(tok)
    r_out = reduce_start("out", ["w_out", "w_br_a", "w_br_b"],
                         [g_w_out.reshape(N_DEV, d // N_DEV, d), to_shards(g_w_bra), to_shards(g_w_brb)])
    _after(tok, r_out["tok"])
    dq_a, dk_a, dv_a = _attention_bwd(q_a, kv_a, kv_a, do_a, lse_a, sc_a, name="attn_a_bwd", **att_a)
    dq_b, dk_b, dv_b = _attention_bwd(q_b, k_b, v_b, do_b, lse_b, sc_b, name="attn_b_bwd", **att_b)
    _after(reduce_relay(r_out, dv_b))
    dqa_raw = _rope_a(dq_a, q_tabs_a, True, BF16, "rope_q_bwd")
    dcqn = _mm(dqa_raw, wq_ext, "nt", F32, "d_cqn", tm=1024, tn=ql)
    g_wq_ext = _mm(cqn, dqa_raw, "tn", BF16, "g_w_q_up", tm=ql, tn=1024)
    dq_p, st_q, st_qb = _q_prep_bwd(qg, mla_q_norm_g, gqa_q_norm_g, q_tabs_b, dcqn, dq_b, q_pad)
    dkv_a = jnp.concatenate([dk_a.astype(BF16), dv_a.astype(BF16)], axis=1)
    dkin = _mm(dkv_a, wkv_ext, "nt", F32, "d_kin", tm=1152, tn=kvl + LANE, tk=3072)
    g_wkv_ext = _mm(kin, dkv_a, "tn", BF16, "g_w_kv_up", tm=kvl + LANE, tn=1024)
    dkv_p, st_kv, st_kb = _key_prep_bwd(kv_all, mla_kv_norm_g, gqa_k_norm_g, k_tabs, dkin, dk_b, dv_b)
    g_wq = g_wq_ext.reshape(ql, ha, MLA_SLOT)[:, :, :MLA_NOPE + MLA_ROPE].reshape(ql, ha * (MLA_NOPE + MLA_ROPE))
    g_wkv = jnp.concatenate([g_wkv_ext[:kvl, :ha * MLA_SLOT].reshape(kvl, ha, MLA_SLOT)[:, :, :MLA_NOPE],
                             g_wkv_ext[:kvl, ha * MLA_SLOT:].reshape(kvl, ha, MLA_V)], axis=2).reshape(kvl, ha * (MLA_NOPE + MLA_V))
    r_qkv = reduce_start("qkv", ["w_q_up", "w_kv_up"], [to_shards(g_wq), to_shards(g_wkv)])
    dqg = jnp.concatenate([dq_p, dgates], axis=1)
    _after(r_qkv["tok"])
    g_wkv_p = _mm(z_all, dkv_p, "tn", BF16, "g_w_in_kv", tm=1024, tn=wkv_w)
    g_wqg_p = _mm(z_all, dqg, "tn", BF16, "g_w_in_qg", tm=1024, tn=1024, rows=t)
    g_w_in = jnp.concatenate([g_wkv_p[:, :kvl], g_wkv_p[:, kvl + 2 * nb:kvl + 2 * nb + MLA_ROPE],
                              g_wkv_p[:, kvl:kvl + 2 * nb], g_wqg_p[:, :q_w], g_wqg_p[:, q_w + q_pad:]], axis=1)
    r_in = reduce_start("in", ["w_in"], [to_shards(g_w_in)])
    dcat = jnp.concatenate([dkv_p[:t], dqg], axis=1)
    _after(r_in["tok"])
    dz_lat = _mm(dcat, w_cat, "nt", F32, "d_z_lat", tm=1024, tn=1024, tk=2432)
    dz_ctx = _mm(dkv_p, w_kv_p, "nt", F32, "d_z_ctx", tm=min(ROW_BLOCK, tc), tn=1024, a_row_off=t)
    grad_x, st_n1 = _norm1_bwd(cts, xs, norm1_g, mods1, dz_ctx, dz_lat, dx1)

    d_lat = jnp.concatenate([st_n1[0], st_n1[1], st_n2[3], st_n2[0], st_n2[1], st_fin[1]])
    d_cxt = jnp.concatenate([st_n1[3], st_n1[4], jnp.zeros((4 * d,), F32)])
    small = jnp.concatenate([d_lat, d_cxt, st_n1[2], st_q[0], st_kv[0], st_qb[0], st_kb[0], st_n2[2],
                             jnp.concatenate([dcb[0, 0], dcb[1, 0]]), st_fin[0]])
    n_small = small.shape[0]
    pad_small = (-n_small) % LANE
    (small_all,) = _all_gather([jnp.pad(small, (0, pad_small)).reshape(1, -1)], "gather_small")
    offs = {}
    o = 0
    for nm, ln in (("d_lat", 6 * d), ("d_cxt", 6 * d), ("norm1_g", d), ("mla_q_norm_g", ql), ("mla_kv_norm_g", kvl),
                   ("gqa_q_norm_g", GQA_HEAD_DIM), ("gqa_k_norm_g", GQA_HEAD_DIM), ("norm2_g", d), ("conv_b", f2),
                   ("final_norm_g", d)):
        offs[nm] = (o, ln)
        o += ln

    def part(nm):
        a, ln = offs[nm]
        return small_all[:, :, a:a + ln]

    d_lat_all = part("d_lat")[:, 0, :]
    d_cxt_sum = _sum_parts(part("d_cxt"))
    da16 = jnp.concatenate([d_lat_all, d_cxt_sum, jnp.zeros((7, 6 * d), F32)], axis=0)
    da16_shard = lax.dynamic_slice_in_dim(da16, my_idx * ncol, ncol, axis=1)
    cc_part = _cctx_partial(da16_shard, w_ada[0], c_ctx[None, :])
    (cc_all,) = _all_gather([cc_part], "gather_cctx")
    cc_parts = cc_all[:, 0:1, :]
    tok_q = reduce_relay(r_qkv, cc_all)
    tok_i = reduce_relay(r_in, cc_all)

    res = {}
    _after(tok_q, tok_i)

    def upd(nm, parts, shape2):
        wv, mv, vv = (a.reshape(shape2) for a in (weights[nm], mom_m[nm], mom_v[nm]))
        outs = _adamw(parts, wv, mv, vv, "adamw_" + nm)
        res[nm] = [o_.reshape(weights[nm].shape) for o_ in outs]

    for nm in ("norm1_g", "mla_q_norm_g", "mla_kv_norm_g", "gqa_q_norm_g", "gqa_k_norm_g", "norm2_g", "conv_b",
               "final_norm_g"):
        upd(nm, part(nm), (1, offs[nm][1]))
    upd("c_ctx", cc_parts, (1, d))
    b_parts = jnp.concatenate([d_lat_all[:, None, :], d_cxt_sum[None]], axis=0)
    upd("b_ada", b_parts, (1, 6 * d))
    outs = _adamw_ada(conds, da16_shard, w_ada[0], m_w_ada[0], v_w_ada[0])
    res["w_ada"] = [o_[None] for o_ in outs]
    last = outs[0]
    for grp in (r_down, r_up, r_out, r_qkv, r_in):
        recv = reduce_finish(grp, last)
        for nm in grp["names"]:
            parts = recv[nm][:, :3, :] if nm == "conv_w" else recv[nm]
            upd(nm, parts, weights[nm].shape[1:])
            last = res[nm][0]

    return (loss, grad_x[None], *[res[n][0] for n in order], *[res[n][1] for n in order],
            *[res[n][2] for n in order], *[res[n][3] for n in order])
```

```python
import functools

import jax
import jax.numpy as jnp
from jax import lax
from jax.experimental import pallas as pl
from jax.experimental.pallas import tpu as pltpu

F32 = jnp.float32
BF16 = jnp.bfloat16

GRID_W = 64
ROPE_THETA = 10000.0
NORM_EPS = 1e-6
MLA_HEADS = 8
MLA_Q_LORA = 768
MLA_KV_LORA = 512
MLA_NOPE = 128
MLA_ROPE = 64
MLA_V = 128
GQA_HEADS = 8
GQA_KV_HEADS = 2
GQA_HEAD_DIM = 128
ADAM_LR = 0.001
ADAM_B1 = 0.9
ADAM_B2 = 0.999
ADAM_EPS = 1e-08
ADAM_WD = 0.01
ADAM_STEP = 10

N_DEV = 8
MESH_AXES = ("x", "y", "c")
LANE = 128
MLA_SLOT = 2 * LANE
VMEM_LIMIT = 56 * 1024 * 1024
ROW_BLOCK = 256
ATT_Q_BLOCK = 256
MESH_ID = pl.DeviceIdType.MESH


def _tile(n, pref, align=LANE):
    if n <= pref:
        return n
    best = None
    t = align
    while t <= pref:
        if n % t == 0:
            best = t
        t += align
    assert best is not None, (n, pref, align)
    return best


def _cparams(sem=None):
    return pltpu.CompilerParams(dimension_semantics=sem, vmem_limit_bytes=VMEM_LIMIT)


_ORDER_AFTER = []


def _after(*arrays):
    _ORDER_AFTER.extend(arrays)


def _pcall(body, *, in_specs, **kw):
    deps = tuple(_ORDER_AFTER)
    _ORDER_AFTER.clear()
    if not deps:
        return pl.pallas_call(body, in_specs=in_specs, **kw)
    n_in, n_dep = len(in_specs), len(deps)

    def with_deps(*refs):
        body(*refs[:n_in], *refs[n_in + n_dep:])

    call = pl.pallas_call(with_deps, in_specs=list(in_specs) + [pl.BlockSpec(memory_space=pl.ANY)] * n_dep, **kw)
    return lambda *args: call(*args, *deps)


def _all_gather(arrs, name):
    n = len(arrs)

    def body(*refs):
        ins = refs[:n]
        outs = refs[n:2 * n]
        send_sems, recv_sems, local_sems = refs[2 * n:]
        x, y, c = lax.axis_index("x"), lax.axis_index("y"), lax.axis_index("c")
        me, sibling = (x, y, c), (x, y, 1 - c)
        chips = [(1 - x, y), (x, 1 - y), (1 - x, 1 - y)]

        def rows(a, dev):
            px, py, pc = dev
            return outs[a].at[4 * px + 2 * py + pc]

        def copy(a, k, block, to, src=None):
            return pltpu.make_async_remote_copy(
                src_ref=rows(a, block) if src is None else src,
                dst_ref=rows(a, block),
                send_sem=send_sems.at[7 * a + k],
                recv_sem=recv_sems.at[7 * a + k],
                device_id=to,
                device_id_type=MESH_ID,
            )

        mine = [pltpu.make_async_copy(ins[a], rows(a, me), local_sems.at[a]) for a in range(n)]
        for cp in mine:
            cp.start()
        first = []
        for a in range(n):
            first.append(copy(a, 0, me, sibling, src=ins[a]))
            first += [copy(a, 1 + j, me, (*chip, c), src=ins[a]) for j, chip in enumerate(chips)]
        for cp in first:
            cp.start()
        passed = []
        for j, chip in enumerate(chips):
            for a in range(n):
                copy(a, 1 + j, (*chip, c), me).wait_recv()
                fwd = copy(a, 4 + j, (*chip, c), sibling)
                fwd.start()
                passed.append(fwd)
        for a in range(n):
            copy(a, 0, sibling, me).wait_recv()
            for j, chip in enumerate(chips):
                copy(a, 4 + j, (*chip, 1 - c), me).wait_recv()
        for cp in first + passed:
            cp.wait_send()
        for cp in mine:
            cp.wait()

    any_spec = pl.BlockSpec(memory_space=pl.ANY)
    outs = _pcall(
        body,
        name=name,
        out_shape=[jax.ShapeDtypeStruct((N_DEV,) + a.shape, a.dtype) for a in arrs],
        in_specs=[any_spec] * n,
        out_specs=[any_spec] * n,
        scratch_shapes=[
            pltpu.SemaphoreType.DMA((7 * n,)),
            pltpu.SemaphoreType.DMA((7 * n,)),
            pltpu.SemaphoreType.DMA((n,)),
        ],
    )(*arrs)
    return list(outs)


def _all_to_all(arrs, name):
    n = len(arrs)

    def body(*refs):
        ins = refs[:n]
        outs = refs[n:2 * n]
        send_sems, recv_sems, local_sems = refs[2 * n:]
        x, y, c = lax.axis_index("x"), lax.axis_index("y"), lax.axis_index("c")
        my_idx = 4 * x + 2 * y + c

        def peer(k):
            fx, fy, fc = (k >> 2) & 1, (k >> 1) & 1, k & 1
            return (x ^ fx if fx else x, y ^ fy if fy else y, c ^ fc if fc else c)

        def copy(a, k):
            px, py, pc = peer(k)
            return pltpu.make_async_remote_copy(
                src_ref=ins[a].at[4 * px + 2 * py + pc],
                dst_ref=outs[a].at[my_idx],
                send_sem=send_sems.at[7 * a + k - 1],
                recv_sem=recv_sems.at[7 * a + k - 1],
                device_id=(px, py, pc),
                device_id_type=MESH_ID,
            )

        mine = [pltpu.make_async_copy(ins[a].at[my_idx], outs[a].at[my_idx], local_sems.at[a]) for a in range(n)]
        for cp in mine:
            cp.start()
        order = [1, 4, 2, 5, 3, 6, 7]
        cps = [copy(a, k) for k in order for a in range(n)]
        for cp in cps:
            cp.start()
        for cp in cps:
            cp.wait()
        for cp in mine:
            cp.wait()

    any_spec = pl.BlockSpec(memory_space=pl.ANY)
    outs = _pcall(
        body,
        name=name,
        out_shape=[jax.ShapeDtypeStruct(a.shape, a.dtype) for a in arrs],
        in_specs=[any_spec] * n,
        out_specs=[any_spec] * n,
        scratch_shapes=[
            pltpu.SemaphoreType.DMA((7 * n,)),
            pltpu.SemaphoreType.DMA((7 * n,)),
            pltpu.SemaphoreType.DMA((n,)),
        ],
    )(*arrs)
    return list(outs)


_HBM = pl.BlockSpec(memory_space=pltpu.HBM)
_SEM = pl.BlockSpec(memory_space=pltpu.SEMAPHORE)
_EFFECT = pltpu.SideEffectType.DATAFLOW_SIDE_EFFECTING


def _descriptors(copies, send_sems, recv_sems):
    descs = []
    for i, (src, dst, dev) in enumerate(copies):
        if dev is None:
            descs.append(pltpu.make_async_copy(src, dst, recv_sems.at[i]))
        else:
            descs.append(pltpu.make_async_remote_copy(src_ref=src, dst_ref=dst, send_sem=send_sems.at[i],
                                                      recv_sem=recv_sems.at[i], device_id=dev, device_id_type=MESH_ID))
    return descs


def _split_start(name, arrays, copies_fn, n_copies):
    n = len(arrays)

    def body(*refs):
        send_sems, recv_sems = refs[n], refs[n + 1]
        token = refs[2 * n + 2]
        for dsc in _descriptors(copies_fn(refs[:n]), send_sems, recv_sems):
            dsc.start()
        token[...] = jnp.zeros_like(token)

    outs = _pcall(
        body,
        name=name,
        out_shape=(pltpu.SemaphoreType.DMA((n_copies,)), pltpu.SemaphoreType.DMA((n_copies,)),
                   *[pltpu.HBM(a.shape, a.dtype) for a in arrays], jax.ShapeDtypeStruct((8, LANE), F32)),
        in_specs=[_HBM] * n,
        out_specs=(_SEM, _SEM, *[_HBM] * n, pl.BlockSpec(memory_space=pltpu.VMEM)),
        input_output_aliases={i: 2 + i for i in range(n)},
        compiler_params=pltpu.CompilerParams(has_side_effects=_EFFECT),
    )(*[pltpu.with_memory_space_constraint(a, pltpu.HBM) for a in arrays])
    return outs[0], outs[1], list(outs[2:2 + n]), outs[2 + n]


def _split_wait(name, send_sems, recv_sems, arrays, copies_fn, after):
    n = len(arrays)

    def body(*refs):
        for dsc, (_, _, dev) in zip(_descriptors(copies_fn(refs[:n]), refs[n], refs[n + 1]), copies_fn(refs[:n])):
            if dev is None:
                dsc.wait()
            else:
                dsc.wait_send()
                dsc.wait_recv()

    outs = _pcall(
        body,
        name=name,
        out_shape=tuple(pltpu.HBM(a.shape, a.dtype) for a in arrays),
        in_specs=[_HBM] * n + [_SEM, _SEM, pl.BlockSpec(memory_space=pl.ANY)],
        out_specs=tuple([_HBM] * n),
        input_output_aliases={i: i for i in range(n)},
        compiler_params=pltpu.CompilerParams(has_side_effects=_EFFECT),
    )(*arrays, send_sems, recv_sems, after)
    return list(outs)


def _mesh_pos():
    x, y, c = lax.axis_index("x"), lax.axis_index("y"), lax.axis_index("c")
    return x, y, c, [(1 - x, y), (x, 1 - y), (1 - x, 1 - y)]


def _gather_ici_copies(n):
    def copies(refs):
        x, y, c, chips = _mesh_pos()
        me = 4 * x + 2 * y + c
        out = []
        for a in range(n):
            src, buf = refs[a], refs[n + a]
            out.append((src, buf.at[me], None))
            out.append((src, buf.at[me], (x, y, 1 - c)))
            out += [(src, buf.at[me], (cx, cy, c)) for cx, cy in chips]
        return out
    return copies


def _gather_d2d_copies(n):
    def copies(refs):
        x, y, c, chips = _mesh_pos()
        out = []
        for a in range(n):
            for cx, cy in chips:
                rows = refs[a].at[4 * cx + 2 * cy + c]
                out.append((rows, rows, (x, y, 1 - c)))
        return out
    return copies


def _reduce_d2d_copies(n):
    def copies(refs):
        x, y, c, _ = _mesh_pos()
        out = []
        for a in range(n):
            for k in range(4):
                out.append((refs[a].at[2 * k + (1 - c)], refs[n + a].at[k], (x, y, 1 - c)))
        return out
    return copies


def _reduce_ici_copies(n):
    def copies(refs):
        x, y, c, chips = _mesh_pos()
        mine = 2 * x + y
        out = []
        for a in range(n):
            src, land = refs[a], refs[n + a]
            out.append((src.at[mine], land.at[mine], None))
            out += [(src.at[2 * cx + cy], land.at[mine], (cx, cy, c)) for cx, cy in chips]
        return out
    return copies


def _pair_sum(send, land, c_idx, name):
    _, r, cols = send.shape
    rb = _tile(r, max(8, (1 << 20) // (2 * cols) // 8 * 8), 8)
    dt = send.dtype

    def body(c_ref, s_ref, l_ref, o_ref):
        o_ref[...] = (s_ref[...].astype(F32) + l_ref[...].astype(F32)).astype(dt)

    return pl.pallas_call(
        body,
        name=name,
        out_shape=jax.ShapeDtypeStruct((4, r, cols), dt),
        grid_spec=pltpu.PrefetchScalarGridSpec(
            num_scalar_prefetch=1,
            grid=(4, r // rb),
            in_specs=[pl.BlockSpec((None, rb, cols), lambda k, i, c_ref: (2 * k + c_ref[0], i, 0)),
                      pl.BlockSpec((None, rb, cols), lambda k, i, c_ref: (k, i, 0))],
            out_specs=pl.BlockSpec((None, rb, cols), lambda k, i, c_ref: (k, i, 0)),
        ),
        compiler_params=_cparams(("parallel", "parallel")),
    )(c_idx, send, land)


_DIMS = {
    "nn": (((1,), (0,)), ((), ())),
    "nt": (((1,), (1,)), ((), ())),
    "tn": (((0,), (0,)), ((), ())),
}


def _mm_call(a, b, *, mode, grid, a_spec, b_spec, o_spec, out_shape, acc_shape, name):
    nk = grid[2]
    out_dtype = out_shape.dtype

    def body(a_ref, b_ref, o_ref, *scratch):
        p = lax.dot_general(a_ref[...].astype(BF16), b_ref[...].astype(BF16), _DIMS[mode],
                            preferred_element_type=F32)
        if nk == 1:
            o_ref[...] = p.astype(out_dtype)
        else:
            acc = scratch[0]
            k = pl.program_id(2)

            @pl.when(k == 0)
            def _():
                acc[...] = p

            @pl.when(k > 0)
            def _():
                acc[...] += p

            @pl.when(k == nk - 1)
            def _():
                o_ref[...] = acc[...].astype(out_dtype)

    return _pcall(
        body,
        name=name,
        out_shape=out_shape,
        grid=grid,
        in_specs=[a_spec, b_spec],
        out_specs=o_spec,
        scratch_shapes=[pltpu.VMEM(acc_shape, F32)] if nk > 1 else [],
        compiler_params=_cparams(("parallel", "parallel", "arbitrary")),
    )(a, b)


def _mm(a, b, mode, out_dtype, name, tm=512, tn=512, tk=2432, a_row_off=0, rows=None):
    if mode == "nn":
        (m, k), (k2, n) = a.shape, b.shape
    elif mode == "nt":
        (m, k), (n, k2) = a.shape, b.shape
    else:
        (k, m), (k2, n) = a.shape, b.shape
        if rows is not None:
            k = k2 = rows
    assert k == k2, (a.shape, b.shape, mode)
    if mode != "tn":
        m = (m if rows is None else rows + a_row_off) - a_row_off
    tm, tn, tk = _tile(m, tm, 8), _tile(n, tn), _tile(k, tk, 8 if mode == "tn" else LANE)
    assert a_row_off % tm == 0
    ro = a_row_off // tm
    grid = (m // tm, n // tn, k // tk)
    if mode == "tn":
        a_spec = pl.BlockSpec((tk, tm), lambda i, j, kk: (kk, i))
    else:
        a_spec = pl.BlockSpec((tm, tk), lambda i, j, kk: (i + ro, kk))
    if mode == "nt":
        b_spec = pl.BlockSpec((tn, tk), lambda i, j, kk: (j, kk))
    else:
        b_spec = pl.BlockSpec((tk, tn), lambda i, j, kk: (kk, j))
    o_spec = pl.BlockSpec((tm, tn), lambda i, j, kk: (i, j))
    return _mm_call(a, b, mode=mode, grid=grid, a_spec=a_spec, b_spec=b_spec, o_spec=o_spec,
                    out_shape=jax.ShapeDtypeStruct((m, n), out_dtype), acc_shape=(tm, tn), name=name)


def _mm_cat_nt(pieces, out_dtype, name, tm=1024, tn=1024, tk=2048, rows=None):
    m = pieces[0][0].shape[0] if rows is None else rows
    n = pieces[0][1].shape[0]
    tm, tn = _tile(m, tm, 8), _tile(n, tn)
    steps, starts, s = [], [], 0
    for a, b, off in pieces:
        kp = a.shape[1]
        tkp = _tile(kp, tk)
        assert off % tkp == 0 and b.shape[0] == n
        steps.append((tkp, kp // tkp, off // tkp))
        starts.append(s)
        s += kp // tkp
    nk = s
    npc = len(pieces)

    def body(*refs):
        o_ref, acc = refs[2 * npc], refs[2 * npc + 1]
        kk = pl.program_id(2)

        @pl.when(kk == 0)
        def _():
            acc[...] = jnp.zeros_like(acc)

        for p in range(npc):
            @pl.when((kk >= starts[p]) & (kk < starts[p] + steps[p][1]))
            def _(p=p):
                acc[...] += lax.dot_general(refs[2 * p][...].astype(BF16), refs[2 * p + 1][...].astype(BF16), _DIMS["nt"],
                                            preferred_element_type=F32)

        @pl.when(kk == nk - 1)
        def _():
            o_ref[...] = acc[...].astype(out_dtype)

    in_specs, args = [], []
    for p, (a, b, off) in enumerate(pieces):
        tkp, np_, ob = steps[p]

        def rel(kk, p=p, np_=np_):
            return jnp.clip(kk - starts[p], 0, np_ - 1)

        in_specs.append(pl.BlockSpec((tm, tkp), lambda i, j, kk, rel=rel: (i, rel(kk))))
        in_specs.append(pl.BlockSpec((tn, tkp), lambda i, j, kk, rel=rel, ob=ob: (j, ob + rel(kk))))
        args += [a, b]
    return _pcall(
        body,
        name=name,
        out_shape=jax.ShapeDtypeStruct((m, n), out_dtype),
        grid=(m // tm, n // tn, nk),
        in_specs=in_specs,
        out_specs=pl.BlockSpec((tm, tn), lambda i, j, kk: (i, j)),
        scratch_shapes=[pltpu.VMEM((tm, tn), F32)],
        compiler_params=_cparams(("parallel", "parallel", "arbitrary")),
    )(*args)


def _mm_cat_tn(a, pieces, out_dtype, name, tm=1024, tn=1024, rows=None):
    k = a.shape[0] if rows is None else rows
    m = a.shape[1]
    tm = _tile(m, tm)
    starts, s = [], 0
    for b in pieces:
        assert b.shape[1] % tn == 0
        starts.append(s)
        s += b.shape[1] // tn
    nj = s
    npc = len(pieces)

    def body(*refs):
        a_ref, o_ref = refs[0], refs[1 + npc]
        j = pl.program_id(1)
        for p in range(npc):
            @pl.when((j >= starts[p]) & (j < starts[p] + pieces[p].shape[1] // tn))
            def _(p=p):
                o_ref[...] = lax.dot_general(a_ref[...].astype(BF16), refs[1 + p][...].astype(BF16), _DIMS["tn"],
                                             preferred_element_type=F32).astype(out_dtype)

    in_specs = [pl.BlockSpec((k, tm), lambda i, j: (0, i))]
    for p, b in enumerate(pieces):
        np_ = b.shape[1] // tn
        in_specs.append(pl.BlockSpec((k, tn), lambda i, j, p=p, np_=np_: (0, jnp.clip(j - starts[p], 0, np_ - 1))))
    return _pcall(
        body,
        name=name,
        out_shape=jax.ShapeDtypeStruct((m, nj * tn), out_dtype),
        grid=(m // tm, nj),
        in_specs=in_specs,
        out_specs=pl.BlockSpec((tm, tn), lambda i, j: (i, j)),
        compiler_params=_cparams(("parallel", "arbitrary")),
    )(a, *pieces)


def _mm_up_fwd(z2, w3, name, tm=1024):
    t, d = z2.shape
    nsh, _, c = w3.shape
    tm = _tile(t, tm, 8)
    return _mm_call(z2, w3, mode="nn", grid=(t // tm, nsh, 1),
                    a_spec=pl.BlockSpec((tm, d), lambda i, j, kk: (i, 0)),
                    b_spec=pl.BlockSpec((None, d, c), lambda i, j, kk: (j, 0, 0)),
                    o_spec=pl.BlockSpec((tm, c), lambda i, j, kk: (i, j)),
                    out_shape=jax.ShapeDtypeStruct((t, nsh * c), F32), acc_shape=(tm, c), name=name)


def _mm_up_dz(du3, w3, name, tm=1024, tn=1024):
    _, t, f = du3.shape
    nsh, d, c = w3.shape
    half = nsh // 2
    assert f == half * c
    tm, tn = _tile(t, tm, 8), _tile(d, tn)
    return _mm_call(du3, w3, mode="nt", grid=(t // tm, d // tn, nsh),
                    a_spec=pl.BlockSpec((None, tm, c), lambda i, j, kk: (kk // half, i, kk % half)),
                    b_spec=pl.BlockSpec((None, tn, c), lambda i, j, kk: (kk, j, 0)),
                    o_spec=pl.BlockSpec((tm, tn), lambda i, j, kk: (i, j)),
                    out_shape=jax.ShapeDtypeStruct((t, d), F32), acc_shape=(tm, tn), name=name)


def _mm_up_gw(z2, du3, nsh, name, tm=1024):
    t, d = z2.shape
    f = du3.shape[2]
    half = nsh // 2
    c = f // half
    tm = _tile(d, tm)
    return _mm_call(z2, du3, mode="tn", grid=(d // tm, nsh, 1),
                    a_spec=pl.BlockSpec((t, tm), lambda i, j, kk: (0, i)),
                    b_spec=pl.BlockSpec((None, t, c), lambda i, j, kk: (j // half, 0, j % half)),
                    o_spec=pl.BlockSpec((None, tm, c), lambda i, j, kk: (j, i, 0)),
                    out_shape=jax.ShapeDtypeStruct((nsh, d, c), BF16), acc_shape=(tm, c), name=name)


def _rms(x):
    r = lax.rsqrt(jnp.mean(x * x, axis=-1, keepdims=True) + NORM_EPS)
    return x * r, r


def _rms_bwd(dxh, xh, r):
    return r * (dxh - xh * jnp.mean(dxh * xh, axis=-1, keepdims=True))


def _colsum(v):
    return jnp.sum(v, axis=0, keepdims=True)


def _rope(v, c, s1, s2, q):
    w = v.shape[-1]
    return v * c + pltpu.roll(v, w - q, 1) * s1 + pltpu.roll(v, q, 1) * s2


def _rope_t(d, c, s1, s2, q):
    w = d.shape[-1]
    return d * c + pltpu.roll(d * s1, q, 1) + pltpu.roll(d * s2, w - q, 1)


def _norm_mod_fwd(ctx, x, gain, mods):
    tc, d = ctx.shape
    t = x.shape[0]
    rb = min(ROW_BLOCK, tc)
    nbl = t // rb

    def body(ctx_ref, x_ref, g_ref, mod_ref, z_ref):
        i = pl.program_id(0)

        def emit(src, sh, sc):
            xh, _ = _rms(src[...])
            z_ref[...] = ((xh * g_ref[...]) * (1.0 + sc) + sh).astype(BF16)

        @pl.when(i >= nbl)
        def _():
            emit(ctx_ref, mod_ref[2:3, :], mod_ref[3:4, :])

        @pl.when(i < nbl)
        def _():
            emit(x_ref, mod_ref[0:1, :], mod_ref[1:2, :])

    return _pcall(
        body,
        name="norm1_mod_fwd",
        out_shape=jax.ShapeDtypeStruct((tc + t, d), BF16),
        grid=((tc + t) // rb,),
        in_specs=[
            pl.BlockSpec((rb, d), lambda i: (jnp.maximum(i - nbl, 0), 0)),
            pl.BlockSpec((rb, d), lambda i: (jnp.minimum(i, nbl - 1), 0)),
            pl.BlockSpec((1, d), lambda i: (0, 0)),
            pl.BlockSpec((8, d), lambda i: (0, 0)),
        ],
        out_specs=pl.BlockSpec((rb, d), lambda i: (i, 0)),
        compiler_params=_cparams(("arbitrary",)),
    )(ctx, x, gain, mods)


def _norm1_bwd(ctx, x, gain, mods, dz_ctx, dz_lat, dx1):
    tc, d = ctx.shape
    t = x.shape[0]
    rb = min(ROW_BLOCK, tc)
    nbl = t // rb

    def body(ctx_ref, x_ref, g_ref, mod_ref, dzc_ref, dzl_ref, dx1_ref, gx_ref, st_ref):
        i = pl.program_id(0)

        @pl.when(i == 0)
        def _():
            st_ref[...] = jnp.zeros_like(st_ref)

        def common(src, dz, sc, row_sh, row_sc):
            xh, r = _rms(src[...])
            g = g_ref[...]
            dxn = dz * (1.0 + sc)
            st_ref[row_sh:row_sh + 1, :] += _colsum(dz)
            st_ref[row_sc:row_sc + 1, :] += _colsum(dz * (xh * g))
            st_ref[2:3, :] += _colsum(dxn * xh)
            return _rms_bwd(dxn * g, xh, r)

        @pl.when(i >= nbl)
        def _():
            common(ctx_ref, dzc_ref[...], mod_ref[3:4, :], 3, 4)

        @pl.when(i < nbl)
        def _():
            gx_ref[...] = dx1_ref[...] + common(x_ref, dzl_ref[...], mod_ref[1:2, :], 0, 1)

    lat = lambda i: (jnp.minimum(i, nbl - 1), 0)
    cix = lambda i: (jnp.maximum(i - nbl, 0), 0)
    return _pcall(
        body,
        name="norm1_mod_bwd",
        out_shape=[jax.ShapeDtypeStruct((t, d), F32), jax.ShapeDtypeStruct((8, d), F32)],
        grid=((tc + t) // rb,),
        in_specs=[
            pl.BlockSpec((rb, d), cix),
            pl.BlockSpec((rb, d), lat),
            pl.BlockSpec((1, d), lambda i: (0, 0)),
            pl.BlockSpec((8, d), lambda i: (0, 0)),
            pl.BlockSpec((rb, d), cix),
            pl.BlockSpec((rb, d), lat),
            pl.BlockSpec((rb, d), lat),
        ],
        out_specs=[pl.BlockSpec((rb, d), lat), pl.BlockSpec((8, d), lambda i: (0, 0))],
        compiler_params=_cparams(("arbitrary",)),
    )(ctx, x, gain, mods, dz_ctx, dz_lat, dx1)


def _key_prep_fwd(kv, kv_gain, kb_gain, tabs):
    ta, wkv = kv.shape
    kvl = MLA_KV_LORA
    nb = GQA_KV_HEADS * GQA_HEAD_DIM
    rb = ROW_BLOCK if ta % ROW_BLOCK == 0 else LANE
    hd = GQA_HEAD_DIM

    def body(kv_ref, g_ref, gb_ref, ca, s1a, s2a, cb, s1b, s2b, kin_ref, kb_ref, vb_ref):
        xh, _ = _rms(kv_ref[:, 0:kvl])
        kin_ref[:, 0:kvl] = (xh * g_ref[...]).astype(BF16)
        kpe = kv_ref[:, kvl + 2 * nb:kvl + 2 * nb + LANE]
        kin_ref[:, kvl:kvl + LANE] = _rope(kpe, ca[...], s1a[...], s2a[...], MLA_ROPE // 4).astype(BF16)
        for h in range(GQA_KV_HEADS):
            nh, _ = _rms(kv_ref[:, kvl + h * hd:kvl + (h + 1) * hd])
            kb_ref[:, h * hd:(h + 1) * hd] = _rope(nh * gb_ref[...], cb[...], s1b[...], s2b[...], hd // 4).astype(BF16)
        vb_ref[...] = kv_ref[:, kvl + nb:kvl + 2 * nb].astype(BF16)

    row = lambda w: pl.BlockSpec((rb, w), lambda i: (i, 0))
    fix = lambda w: pl.BlockSpec((1, w), lambda i: (0, 0))
    return _pcall(
        body,
        name="key_prep_fwd",
        out_shape=[jax.ShapeDtypeStruct((ta, kvl + LANE), BF16), jax.ShapeDtypeStruct((ta, nb), BF16),
                   jax.ShapeDtypeStruct((ta, nb), BF16)],
        grid=(ta // rb,),
        in_specs=[row(wkv), fix(kvl), fix(hd)] + [row(LANE)] * 3 + [row(hd)] * 3,
        out_specs=[row(kvl + LANE), row(nb), row(nb)],
        compiler_params=_cparams(("parallel",)),
    )(kv, kv_gain, kb_gain, *tabs)


def _key_prep_bwd(kv, kv_gain, kb_gain, tabs, dkin, dkb, dvb):
    ta, wkv = kv.shape
    kvl = MLA_KV_LORA
    nb = GQA_KV_HEADS * GQA_HEAD_DIM
    rb = ROW_BLOCK if ta % ROW_BLOCK == 0 else LANE
    hd = GQA_HEAD_DIM

    def body(kv_ref, g_ref, gb_ref, ca, s1a, s2a, cb, s1b, s2b, dkin_ref, dkb_ref, dvb_ref, dkv_ref, st_ref, stb_ref):
        @pl.when(pl.program_id(0) == 0)
        def _():
            st_ref[...] = jnp.zeros_like(st_ref)
            stb_ref[...] = jnp.zeros_like(stb_ref)

        xh, r = _rms(kv_ref[:, 0:kvl])
        dn = dkin_ref[:, 0:kvl]
        st_ref[0:1, :] += _colsum(dn * xh)
        dkv_ref[:, 0:kvl] = _rms_bwd(dn * g_ref[...], xh, r).astype(BF16)
        dpe = _rope_t(dkin_ref[:, kvl:kvl + LANE], ca[...], s1a[...], s2a[...], MLA_ROPE // 4)
        dkv_ref[:, kvl + 2 * nb:kvl + 2 * nb + LANE] = dpe.astype(BF16)
        for h in range(GQA_KV_HEADS):
            nh, rh = _rms(kv_ref[:, kvl + h * hd:kvl + (h + 1) * hd])
            dn_h = _rope_t(dkb_ref[:, h * hd:(h + 1) * hd], cb[...], s1b[...], s2b[...], hd // 4)
            stb_ref[0:1, :] += _colsum(dn_h * nh)
            dkv_ref[:, kvl + h * hd:kvl + (h + 1) * hd] = _rms_bwd(dn_h * gb_ref[...], nh, rh).astype(BF16)
        dkv_ref[:, kvl + nb:kvl + 2 * nb] = dvb_ref[...].astype(BF16)

    row = lambda w: pl.BlockSpec((rb, w), lambda i: (i, 0))
    fix = lambda w: pl.BlockSpec((1, w), lambda i: (0, 0))
    return _pcall(
        body,
        name="key_prep_bwd",
        out_shape=[jax.ShapeDtypeStruct((ta, wkv), BF16), jax.ShapeDtypeStruct((8, kvl), F32),
                   jax.ShapeDtypeStruct((8, hd), F32)],
        grid=(ta // rb,),
        in_specs=[row(wkv), fix(kvl), fix(hd)] + [row(LANE)] * 3 + [row(hd)] * 3 + [row(kvl + LANE), row(nb), row(nb)],
        out_specs=[row(wkv), pl.BlockSpec((8, kvl), lambda i: (0, 0)), pl.BlockSpec((8, hd), lambda i: (0, 0))],
        compiler_params=_cparams(("arbitrary",)),
    )(kv, kv_gain, kb_gain, *tabs, dkin, dkb, dvb)


def _q_prep_fwd(qg, q_gain, qb_gain, tabs):
    t = qg.shape[0]
    ql = MLA_Q_LORA
    hd = GQA_HEAD_DIM
    hb = GQA_HEADS * hd
    rb = min(ROW_BLOCK, t)

    def body(q_ref, g_ref, gb_ref, cb, s1b, s2b, cqn_ref, qb_ref):
        xh, _ = _rms(q_ref[:, 0:ql])
        cqn_ref[...] = (xh * g_ref[...]).astype(BF16)
        for h in range(GQA_HEADS):
            nh, _ = _rms(q_ref[:, ql + h * hd:ql + (h + 1) * hd])
            qb_ref[:, h * hd:(h + 1) * hd] = _rope(nh * gb_ref[...], cb[...], s1b[...], s2b[...], hd // 4).astype(BF16)

    row = lambda w: pl.BlockSpec((rb, w), lambda i: (i, 0))
    fix = lambda w: pl.BlockSpec((1, w), lambda i: (0, 0))
    return _pcall(
        body,
        name="q_prep_fwd",
        out_shape=[jax.ShapeDtypeStruct((t, ql), BF16), jax.ShapeDtypeStruct((t, hb), BF16)],
        grid=(t // rb,),
        in_specs=[row(ql + hb), fix(ql), fix(hd)] + [row(hd)] * 3,
        out_specs=[row(ql), row(hb)],
        compiler_params=_cparams(("parallel",)),
    )(qg, q_gain, qb_gain, *tabs)


def _q_prep_bwd(qg, q_gain, qb_gain, tabs, dcqn, dqb, wpad):
    t = qg.shape[0]
    ql = MLA_Q_LORA
    hd = GQA_HEAD_DIM
    hb = GQA_HEADS * hd
    rb = min(ROW_BLOCK, t)

    def body(q_ref, g_ref, gb_ref, cb, s1b, s2b, dcqn_ref, dqb_ref, dq_ref, st_ref, stb_ref):
        @pl.when(pl.program_id(0) == 0)
        def _():
            st_ref[...] = jnp.zeros_like(st_ref)
            stb_ref[...] = jnp.zeros_like(stb_ref)

        xh, r = _rms(q_ref[:, 0:ql])
        dn = dcqn_ref[...]
        st_ref[0:1, :] += _colsum(dn * xh)
        dq_ref[:, 0:ql] = _rms_bwd(dn * g_ref[...], xh, r).astype(BF16)
        for h in range(GQA_HEADS):
            nh, rh = _rms(q_ref[:, ql + h * hd:ql + (h + 1) * hd])
            dn_h = _rope_t(dqb_ref[:, h * hd:(h + 1) * hd], cb[...], s1b[...], s2b[...], hd // 4)
            stb_ref[0:1, :] += _colsum(dn_h * nh)
            dq_ref[:, ql + h * hd:ql + (h + 1) * hd] = _rms_bwd(dn_h * gb_ref[...], nh, rh).astype(BF16)
        if wpad:
            dq_ref[:, ql + hb:ql + hb + wpad] = jnp.zeros((rb, wpad), BF16)

    row = lambda w: pl.BlockSpec((rb, w), lambda i: (i, 0))
    fix = lambda w: pl.BlockSpec((1, w), lambda i: (0, 0))
    return _pcall(
        body,
        name="q_prep_bwd",
        out_shape=[jax.ShapeDtypeStruct((t, ql + hb + wpad), BF16), jax.ShapeDtypeStruct((8, ql), F32),
                   jax.ShapeDtypeStruct((8, hd), F32)],
        grid=(t // rb,),
        in_specs=[row(ql + hb), fix(ql), fix(hd)] + [row(hd)] * 3 + [row(ql), row(hb)],
        out_specs=[row(ql + hb + wpad), pl.BlockSpec((8, ql), lambda i: (0, 0)), pl.BlockSpec((8, hd), lambda i: (0, 0))],
        compiler_params=_cparams(("arbitrary",)),
    )(qg, q_gain, qb_gain, *tabs, dcqn, dqb)


def _rope_a(v, tabs, transpose, out_dtype, name):
    t, w = v.shape
    rb = min(ROW_BLOCK, t)
    fn = _rope_t if transpose else _rope

    def body(v_ref, c, s1, s2, o_ref):
        for h in range(w // MLA_SLOT):
            sl = slice(h * MLA_SLOT, (h + 1) * MLA_SLOT)
            o_ref[:, sl] = fn(v_ref[:, sl].astype(F32), c[...], s1[...], s2[...], MLA_ROPE // 4).astype(out_dtype)

    row = lambda ww: pl.BlockSpec((rb, ww), lambda i: (i, 0))
    return _pcall(
        body,
        name=name,
        out_shape=jax.ShapeDtypeStruct((t, w), out_dtype),
        grid=(t // rb,),
        in_specs=[row(w)] + [row(MLA_SLOT)] * 3,
        out_specs=row(w),
        compiler_params=_cparams(("parallel",)),
    )(v, *tabs)


def _merge_fwd(pa, pb, qg, gate_blk):
    t, d = pa.shape
    rb = min(ROW_BLOCK, t)

    def body(pa_ref, pb_ref, ga_ref, gb_ref, o_ref):
        o_ref[...] = (jax.nn.sigmoid(ga_ref[...]) * pa_ref[...] + jax.nn.sigmoid(gb_ref[...]) * pb_ref[...]).astype(BF16)

    row = pl.BlockSpec((rb, d), lambda i: (i, 0))
    return _pcall(
        body,
        name="merge_fwd",
        out_shape=jax.ShapeDtypeStruct((t, d), BF16),
        grid=(t // rb,),
        in_specs=[row, row, pl.BlockSpec((rb, d), lambda i: (i, gate_blk)), pl.BlockSpec((rb, d), lambda i: (i, gate_blk + 1))],
        out_specs=row,
        compiler_params=_cparams(("parallel",)),
    )(pa, pb, qg, qg)


def _merge_bwd(dm, pa, pb, qg, gate_blk):
    t, d = pa.shape
    rb = min(ROW_BLOCK, t)

    def body(dm_ref, pa_ref, pb_ref, ga_ref, gb_ref, dpa_ref, dpb_ref, dg_ref):
        dmv = dm_ref[...]
        sa = jax.nn.sigmoid(ga_ref[...])
        sb = jax.nn.sigmoid(gb_ref[...])
        dpa_ref[...] = (dmv * sa).astype(BF16)
        dpb_ref[...] = (dmv * sb).astype(BF16)
        dg_ref[:, 0:d] = (dmv * pa_ref[...] * (sa * (1.0 - sa))).astype(BF16)
        dg_ref[:, d:2 * d] = (dmv * pb_ref[...] * (sb * (1.0 - sb))).astype(BF16)

    row = pl.BlockSpec((rb, d), lambda i: (i, 0))
    return _pcall(
        body,
        name="merge_bwd",
        out_shape=[jax.ShapeDtypeStruct((t, d), BF16), jax.ShapeDtypeStruct((t, d), BF16),
                   jax.ShapeDtypeStruct((t, 2 * d), BF16)],
        grid=(t // rb,),
        in_specs=[row, row, row, pl.BlockSpec((rb, d), lambda i: (i, gate_blk)), pl.BlockSpec((rb, d), lambda i: (i, gate_blk + 1))],
        out_specs=[row, row, pl.BlockSpec((rb, 2 * d), lambda i: (i, 0))],
        compiler_params=_cparams(("parallel",)),
    )(dm, pa, pb, qg, qg)


def _resid_norm_mod(x, branch, gain, mods, name):
    t, d = x.shape
    rb = min(ROW_BLOCK, t)

    def body(x_ref, b_ref, g_ref, mod_ref, x1_ref, z_ref):
        x1 = x_ref[...] + mod_ref[0:1, :] * b_ref[...]
        x1_ref[...] = x1
        xh, _ = _rms(x1)
        z_ref[...] = ((xh * g_ref[...]) * (1.0 + mod_ref[2:3, :]) + mod_ref[1:2, :]).astype(BF16)

    row = pl.BlockSpec((rb, d), lambda i: (i, 0))
    return _pcall(
        body,
        name=name,
        out_shape=[jax.ShapeDtypeStruct((t, d), F32), jax.ShapeDtypeStruct((t, d), BF16)],
        grid=(t // rb,),
        in_specs=[row, row, pl.BlockSpec((1, d), lambda i: (0, 0)), pl.BlockSpec((8, d), lambda i: (0, 0))],
        out_specs=[row, row],
        compiler_params=_cparams(("parallel",)),
    )(x, branch, gain, mods)


def _norm2_bwd(x1, attn, gain, mods, dz2, dx2):
    t, d = x1.shape
    rb = min(ROW_BLOCK, t)

    def body(x1_ref, at_ref, g_ref, mod_ref, dz_ref, dx2_ref, dx1_ref, da_ref, st_ref):
        @pl.when(pl.program_id(0) == 0)
        def _():
            st_ref[...] = jnp.zeros_like(st_ref)

        xh, r = _rms(x1_ref[...])
        g = g_ref[...]
        dz = dz_ref[...]
        dxn = dz * (1.0 + mod_ref[1:2, :])
        st_ref[0:1, :] += _colsum(dz)
        st_ref[1:2, :] += _colsum(dz * (xh * g))
        st_ref[2:3, :] += _colsum(dxn * xh)
        dx1 = dx2_ref[...] + _rms_bwd(dxn * g, xh, r)
        dx1_ref[...] = dx1
        st_ref[3:4, :] += _colsum(dx1 * at_ref[...])
        da_ref[...] = (dx1 * mod_ref[0:1, :]).astype(BF16)

    row = pl.BlockSpec((rb, d), lambda i: (i, 0))
    return _pcall(
        body,
        name="norm2_mod_bwd",
        out_shape=[jax.ShapeDtypeStruct((t, d), F32), jax.ShapeDtypeStruct((t, d), BF16), jax.ShapeDtypeStruct((8, d), F32)],
        grid=(t // rb,),
        in_specs=[row, row, pl.BlockSpec((1, d), lambda i: (0, 0)), pl.BlockSpec((8, d), lambda i: (0, 0)), row, row],
        out_specs=[row, row, pl.BlockSpec((8, d), lambda i: (0, 0))],
        compiler_params=_cparams(("arbitrary",)),
    )(x1, attn, gain, mods, dz2, dx2)


def _final_loss(x1, ffn, gain, mods, target):
    t, d = x1.shape
    rb = min(ROW_BLOCK, t)
    nb = t // rb

    def body(x1_ref, f_ref, g_ref, mod_ref, tg_ref, dx2_ref, df_ref, st_ref):
        i = pl.program_id(0)

        @pl.when(i == 0)
        def _():
            st_ref[...] = jnp.zeros_like(st_ref)

        ffn_v = f_ref[...]
        g2 = mod_ref[0:1, :]
        x2 = x1_ref[...] + g2 * ffn_v
        xh, r = _rms(x2)
        g = g_ref[...]
        err = xh * g - tg_ref[...]
        st_ref[2:3, :] += _colsum(err * err) * (0.5 / d)
        dy = err * (1.0 / d)
        st_ref[0:1, :] += _colsum(dy * xh)
        dx2 = _rms_bwd(dy * g, xh, r)
        dx2_ref[...] = dx2
        st_ref[1:2, :] += _colsum(dx2 * ffn_v)
        df_ref[...] = (dx2 * g2).astype(BF16)

        @pl.when(i == nb - 1)
        def _():
            st_ref[3:4, :] = jnp.broadcast_to(jnp.sum(st_ref[2:3, :], axis=-1, keepdims=True), (1, d))

    row = pl.BlockSpec((rb, d), lambda i: (i, 0))
    return _pcall(
        body,
        name="final_norm_loss",
        out_shape=[jax.ShapeDtypeStruct((t, d), F32), jax.ShapeDtypeStruct((t, d), BF16), jax.ShapeDtypeStruct((8, d), F32)],
        grid=(nb,),
        in_specs=[row, row, pl.BlockSpec((1, d), lambda i: (0, 0)), pl.BlockSpec((8, d), lambda i: (0, 0)), row],
        out_specs=[row, row, pl.BlockSpec((8, d), lambda i: (0, 0))],
        compiler_params=_cparams(("arbitrary",)),
    )(x1, ffn, gain, mods, target)


def _shift_rows(v, down):
    n = v.shape[0]
    rows = lax.broadcasted_iota(jnp.int32, v.shape, 0)
    if down:
        return jnp.where(rows == 0, 0.0, pltpu.roll(v, 1, 0))
    return jnp.where(rows == n - 1, 0.0, pltpu.roll(v, n - 1, 0))


def _conv_act(ua, ub, cwa, cwb, cba, cbb):
    a = cba + cwa[0:1, :] * _shift_rows(ua, True) + cwa[1:2, :] * ua + cwa[2:3, :] * _shift_rows(ua, False)
    b = cbb + cwb[0:1, :] * _shift_rows(ub, True) + cwb[1:2, :] * ub + cwb[2:3, :] * _shift_rows(ub, False)
    return a, b


def _conv_fwd(u, cw, cb):
    t, f2 = u.shape
    f = f2 // 2
    cbk = _tile(f, 256)
    nf = f // cbk

    def body(ua_ref, ub_ref, cwa_ref, cwb_ref, cba_ref, cbb_ref, h_ref):
        a, b = _conv_act(ua_ref[...], ub_ref[...], cwa_ref[...], cwb_ref[...], cba_ref[...], cbb_ref[...])
        h_ref[...] = (a * jax.nn.sigmoid(a) * b).astype(BF16)

    ca = lambda r: pl.BlockSpec((r, cbk), lambda j: (0, j))
    cbs = lambda r: pl.BlockSpec((r, cbk), lambda j: (0, nf + j))
    return _pcall(
        body,
        name="conv_gate_fwd",
        out_shape=jax.ShapeDtypeStruct((t, f), BF16),
        grid=(nf,),
        in_specs=[ca(t), cbs(t), ca(3), cbs(3), ca(1), cbs(1)],
        out_specs=ca(t),
        compiler_params=_cparams(("parallel",)),
    )(u, u, cw, cw, cb, cb)


def _conv_bwd(u, cw, cb, dh):
    t, f2 = u.shape
    f = f2 // 2
    cbk = _tile(f, 256)
    nf = f // cbk

    def body(ua_ref, ub_ref, cwa_ref, cwb_ref, cba_ref, cbb_ref, dh_ref, du_ref, dcw_ref, dcb_ref):
        ua, ub = ua_ref[...], ub_ref[...]
        cwa, cwb = cwa_ref[...], cwb_ref[...]
        a, b = _conv_act(ua, ub, cwa, cwb, cba_ref[...], cbb_ref[...])
        dh_v = dh_ref[...]
        sg = jax.nn.sigmoid(a)
        db = dh_v * (a * sg)
        da = dh_v * b * (sg * (1.0 + a * (1.0 - sg)))
        for idx, (dv, uu, cwv) in enumerate(((da, ua, cwa), (db, ub, cwb))):
            dcb_ref[idx] = _colsum(dv)
            dcw_ref[idx, 0:1, :] = _colsum(dv * _shift_rows(uu, True))
            dcw_ref[idx, 1:2, :] = _colsum(dv * uu)
            dcw_ref[idx, 2:3, :] = _colsum(dv * _shift_rows(uu, False))
            du = cwv[0:1, :] * _shift_rows(dv, False) + cwv[1:2, :] * dv + cwv[2:3, :] * _shift_rows(dv, True)
            du_ref[idx] = du.astype(BF16)

    ca = lambda r: pl.BlockSpec((r, cbk), lambda j: (0, j))
    cbs = lambda r: pl.BlockSpec((r, cbk), lambda j: (0, nf + j))
    o3 = lambda r: pl.BlockSpec((2, r, cbk), lambda j: (0, 0, j))
    return _pcall(
        body,
        name="conv_gate_bwd",
        out_shape=[jax.ShapeDtypeStruct((2, t, f), BF16), jax.ShapeDtypeStruct((2, 3, f), F32),
                   jax.ShapeDtypeStruct((2, 1, f), F32)],
        grid=(nf,),
        in_specs=[ca(t), cbs(t), ca(3), cbs(3), ca(1), cbs(1), ca(t)],
        out_specs=[o3(t), o3(3), o3(1)],
        compiler_params=_cparams(("parallel",)),
    )(u, u, cw, cw, cb, cb, dh)


def _attention_fwd(q, kk, vv, scale, *, hq, hkv, dk, dv, k_blk0, v_blk0, name):
    t = q.shape[0]
    tk = kk.shape[0]
    g_sz = hq // hkv
    tq = min(ATT_Q_BLOCK, t)

    def body(q_ref, k_ref, v_ref, o_ref, lse_ref):
        k = k_ref[...]
        v = v_ref[...]
        for j in range(g_sz):
            s = lax.dot_general(q_ref[:, j * dk:(j + 1) * dk], k, _DIMS["nt"], preferred_element_type=F32) * scale
            m = jnp.max(s, axis=-1, keepdims=True)
            p = jnp.exp(s - m)
            l = jnp.sum(p, axis=-1, keepdims=True)
            o = jnp.dot(p.astype(BF16), v, preferred_element_type=F32) / l
            o_ref[:, j * dv:(j + 1) * dv] = o.astype(BF16)
            lse_ref[0, :, j:j + 1] = m + jnp.log(l)

    return _pcall(
        body,
        name=name,
        out_shape=[jax.ShapeDtypeStruct((t, hq * dv), BF16), jax.ShapeDtypeStruct((hkv, t, g_sz), F32)],
        grid=(hkv, t // tq),
        in_specs=[
            pl.BlockSpec((tq, g_sz * dk), lambda g, i: (i, g)),
            pl.BlockSpec((tk, dk), lambda g, i: (0, k_blk0 + g)),
            pl.BlockSpec((tk, dv), lambda g, i: (0, v_blk0 + g)),
        ],
        out_specs=[
            pl.BlockSpec((tq, g_sz * dv), lambda g, i: (i, g)),
            pl.BlockSpec((1, tq, g_sz), lambda g, i: (g, i, 0)),
        ],
        compiler_params=_cparams(("parallel", "parallel")),
    )(q, kk, vv)


def _attention_bwd(q, kk, vv, do, lse, scale, *, hq, hkv, dk, dv, k_blk0, v_blk0, name):
    t = q.shape[0]
    tk = kk.shape[0]
    g_sz = hq // hkv
    tq = min(ATT_Q_BLOCK, t)

    def body(q_ref, k_ref, v_ref, do_ref, lse_ref, dq_ref, dk_ref, dv_ref):
        @pl.when(pl.program_id(1) == 0)
        def _():
            dk_ref[...] = jnp.zeros_like(dk_ref)
            dv_ref[...] = jnp.zeros_like(dv_ref)

        k = k_ref[...]
        v = v_ref[...]
        for j in range(g_sz):
            qj = q_ref[:, j * dk:(j + 1) * dk]
            doj = do_ref[:, j * dv:(j + 1) * dv]
            s = lax.dot_general(qj, k, _DIMS["nt"], preferred_element_type=F32) * scale
            p = jnp.exp(s - lse_ref[0, :, j:j + 1])
            dp = lax.dot_general(doj, v, _DIMS["nt"], preferred_element_type=F32)
            ds = (p * (dp - jnp.sum(p * dp, axis=-1, keepdims=True)) * scale).astype(BF16)
            dv_ref[...] += lax.dot_general(p.astype(BF16), doj, _DIMS["tn"], preferred_element_type=F32)
            dk_ref[...] += lax.dot_general(ds, qj, _DIMS["tn"], preferred_element_type=F32)
            dq_ref[:, j * dk:(j + 1) * dk] = jnp.dot(ds, k, preferred_element_type=F32)

    return _pcall(
        body,
        name=name,
        out_shape=[jax.ShapeDtypeStruct((t, hq * dk), F32), jax.ShapeDtypeStruct((tk, hkv * dk), F32),
                   jax.ShapeDtypeStruct((tk, hkv * dv), F32)],
        grid=(hkv, t // tq),
        in_specs=[
            pl.BlockSpec((tq, g_sz * dk), lambda g, i: (i, g)),
            pl.BlockSpec((tk, dk), lambda g, i: (0, k_blk0 + g)),
            pl.BlockSpec((tk, dv), lambda g, i: (0, v_blk0 + g)),
            pl.BlockSpec((tq, g_sz * dv), lambda g, i: (i, g)),
            pl.BlockSpec((1, tq, g_sz), lambda g, i: (g, i, 0)),
        ],
        out_specs=[
            pl.BlockSpec((tq, g_sz * dk), lambda g, i: (i, g)),
            pl.BlockSpec((tk, dk), lambda g, i: (0, g)),
            pl.BlockSpec((tk, dv), lambda g, i: (0, g)),
        ],
        compiler_params=_cparams(("parallel", "arbitrary")),
    )(q, kk, vv, do, lse)


def _silu(v):
    return v * jax.nn.sigmoid(v)


def _ada_fwd(conds, w_ada, b_ada_shard):
    r, d = conds.shape
    n = w_ada.shape[1]
    tn = _tile(n, 512)

    def body(c_ref, w_ref, b_ref, o_ref):
        s = _silu(c_ref[...]).astype(BF16)
        o_ref[...] = jnp.dot(s, w_ref[...].astype(BF16), preferred_element_type=F32) + b_ref[...]

    return _pcall(
        body,
        name="ada_fwd",
        out_shape=jax.ShapeDtypeStruct((r, n), F32),
        grid=(n // tn,),
        in_specs=[pl.BlockSpec((r, d), lambda j: (0, 0)), pl.BlockSpec((d, tn), lambda j: (0, j)),
                  pl.BlockSpec((1, tn), lambda j: (0, j))],
        out_specs=pl.BlockSpec((r, tn), lambda j: (0, j)),
        compiler_params=_cparams(("parallel",)),
    )(conds, w_ada, b_ada_shard)


def _cctx_partial(da16_shard, w_ada, c_ctx_row):
    d, n = w_ada.shape
    td = _tile(d, 512)

    def body(g_ref, w_ref, c_ref, o_ref):
        ds = lax.dot_general(g_ref[8:16, :].astype(BF16), w_ref[...].astype(BF16), _DIMS["nt"],
                             preferred_element_type=F32)
        cv = c_ref[...]
        sg = jax.nn.sigmoid(cv)
        o_ref[...] = ds * (sg * (1.0 + cv * (1.0 - sg)))

    return _pcall(
        body,
        name="cctx_partial",
        out_shape=jax.ShapeDtypeStruct((8, d), F32),
        grid=(d // td,),
        in_specs=[pl.BlockSpec((16, n), lambda j: (0, 0)), pl.BlockSpec((td, n), lambda j: (j, 0)),
                  pl.BlockSpec((1, td), lambda j: (0, j))],
        out_specs=pl.BlockSpec((8, td), lambda j: (0, j)),
        compiler_params=_cparams(("parallel",)),
    )(da16_shard, w_ada, c_ctx_row)


def _sum_parts(parts):
    p, _, n = parts.shape

    def body(p_ref, o_ref):
        acc = p_ref[0]
        for s in range(1, p):
            acc = acc + p_ref[s]
        o_ref[...] = acc

    return _pcall(
        body,
        name="sum_parts",
        out_shape=jax.ShapeDtypeStruct((1, n), F32),
        in_specs=[pl.BlockSpec(memory_space=pltpu.VMEM)],
        out_specs=pl.BlockSpec(memory_space=pltpu.VMEM),
    )(parts)


def _adam_math(w, g, m, v):
    m2 = ADAM_B1 * m + (1.0 - ADAM_B1) * g
    v2 = ADAM_B2 * v + (1.0 - ADAM_B2) * jnp.square(g)
    m_hat = m2 / (1.0 - ADAM_B1 ** ADAM_STEP)
    v_hat = v2 / (1.0 - ADAM_B2 ** ADAM_STEP)
    delta = -ADAM_LR * (m_hat / (jnp.sqrt(v_hat) + ADAM_EPS) + ADAM_WD * w)
    return delta, m2, v2


def _adamw(parts, w, m, v, name):
    p, r, c = parts.shape
    rb = _tile(r, max(8, (1 << 20) // (4 * c) // 8 * 8), 8)

    def body(p_ref, w_ref, m_ref, v_ref, g_ref, d_ref, m2_ref, v2_ref):
        g = p_ref[0].astype(F32)
        for s in range(1, p):
            g = g + p_ref[s].astype(F32)
        g_ref[...] = g
        d_ref[...], m2_ref[...], v2_ref[...] = _adam_math(w_ref[...], g, m_ref[...], v_ref[...])

    row = pl.BlockSpec((rb, c), lambda i: (i, 0))
    return _pcall(
        body,
        name=name,
        out_shape=[jax.ShapeDtypeStruct((r, c), F32)] * 4,
        grid=(r // rb,),
        in_specs=[pl.BlockSpec((p, rb, c), lambda i: (0, i, 0)), row, row, row],
        out_specs=[row] * 4,
        compiler_params=_cparams(("parallel",)),
    )(parts, w, m, v)


def _adamw_ada(conds, da16, w, m, v):
    d, n = w.shape
    rb = _tile(d, 256, LANE)

    def body(s_ref, da_ref, w_ref, m_ref, v_ref, g_ref, d_ref, m2_ref, v2_ref):
        g = lax.dot_general(_silu(s_ref[...]).astype(BF16), da_ref[...].astype(BF16), _DIMS["tn"],
                            preferred_element_type=F32)
        g_ref[...] = g
        d_ref[...], m2_ref[...], v2_ref[...] = _adam_math(w_ref[...], g, m_ref[...], v_ref[...])

    row = pl.BlockSpec((rb, n), lambda i: (i, 0))
    return _pcall(
        body,
        name="adamw_w_ada",
        out_shape=[jax.ShapeDtypeStruct((d, n), F32)] * 4,
        grid=(d // rb,),
        in_specs=[pl.BlockSpec((16, rb), lambda i: (0, i)), pl.BlockSpec((16, n), lambda i: (0, 0)), row, row, row],
        out_specs=[row] * 4,
        compiler_params=_cparams(("parallel",)),
    )(conds, da16, w, m, v)


def _cast_bf16(a, name):
    r, c = a.shape
    rb = _tile(r, 512, 8)

    def body(a_ref, o_ref):
        o_ref[...] = a_ref[...].astype(BF16)

    row = pl.BlockSpec((rb, c), lambda i: (i, 0))
    return _pcall(body, name=name, out_shape=jax.ShapeDtypeStruct((r, c), BF16), grid=(r // rb,),
                          in_specs=[row], out_specs=row, compiler_params=_cparams(("parallel",)))(a)


def _rope_tabs(t, rot):
    half, q = rot // 2, rot // 4
    n_rows = t // GRID_W
    row = jnp.repeat(jnp.arange(n_rows, dtype=F32), GRID_W)
    col = jnp.tile(jnp.arange(GRID_W, dtype=F32), n_rows)
    inv_freq = ROPE_THETA ** (-jnp.arange(0, half, 2, dtype=F32) / half)
    ang = jnp.concatenate([row[:, None] * inv_freq, col[:, None] * inv_freq], axis=-1)
    cos, sin = jnp.cos(ang), jnp.sin(ang)
    c0, c1, s0, s1 = cos[:, :q], cos[:, q:], sin[:, :q], sin[:, q:]
    z = jnp.zeros_like(s0)
    return (jnp.concatenate([c0, c0, c1, c1], -1), jnp.concatenate([-s0, z, -s1, z], -1),
            jnp.concatenate([z, s0, z, s1], -1))


def _pad_cols(a, left, total, fill=0.0):
    return jnp.pad(a, ((0, 0), (left, total - left - a.shape[1])), constant_values=fill)


def _with_ctx_rows(tab, tc, fill):
    return jnp.concatenate([tab, jnp.full((tc, tab.shape[1]), fill, F32)], axis=0)


def kernel(x, c, ctx, c_ctx, w_ada, b_ada, norm1_g, w_in, mla_q_norm_g, w_q_up, mla_kv_norm_g, w_kv_up, gqa_q_norm_g, gqa_k_norm_g, w_br_a, w_br_b, w_out, norm2_g, w_up, conv_w, conv_b, w_down, final_norm_g, loss_target, m_c_ctx, m_w_ada, m_b_ada, m_norm1_g, m_w_in, m_mla_q_norm_g, m_w_q_up, m_mla_kv_norm_g, m_w_kv_up, m_gqa_q_norm_g, m_gqa_k_norm_g, m_w_br_a, m_w_br_b, m_w_out, m_norm2_g, m_w_up, m_conv_w, m_conv_b, m_w_down, m_final_norm_g, v_c_ctx, v_w_ada, v_b_ada, v_norm1_g, v_w_in, v_mla_q_norm_g, v_w_q_up, v_mla_kv_norm_g, v_w_kv_up, v_gqa_q_norm_g, v_gqa_k_norm_g, v_w_br_a, v_w_br_b, v_w_out, v_norm2_g, v_w_up, v_conv_w, v_conv_b, v_w_down, v_final_norm_g):
    weights = dict(c_ctx=c_ctx, w_ada=w_ada, b_ada=b_ada, norm1_g=norm1_g, w_in=w_in, mla_q_norm_g=mla_q_norm_g,
                   w_q_up=w_q_up, mla_kv_norm_g=mla_kv_norm_g, w_kv_up=w_kv_up, gqa_q_norm_g=gqa_q_norm_g,
                   gqa_k_norm_g=gqa_k_norm_g, w_br_a=w_br_a, w_br_b=w_br_b, w_out=w_out, norm2_g=norm2_g, w_up=w_up,
                   conv_w=conv_w, conv_b=conv_b, w_down=w_down, final_norm_g=final_norm_g)
    mom_m = dict(c_ctx=m_c_ctx, w_ada=m_w_ada, b_ada=m_b_ada, norm1_g=m_norm1_g, w_in=m_w_in, mla_q_norm_g=m_mla_q_norm_g,
                 w_q_up=m_w_q_up, mla_kv_norm_g=m_mla_kv_norm_g, w_kv_up=m_w_kv_up, gqa_q_norm_g=m_gqa_q_norm_g,
                 gqa_k_norm_g=m_gqa_k_norm_g, w_br_a=m_w_br_a, w_br_b=m_w_br_b, w_out=m_w_out, norm2_g=m_norm2_g,
                 w_up=m_w_up, conv_w=m_conv_w, conv_b=m_conv_b, w_down=m_w_down, final_norm_g=m_final_norm_g)
    mom_v = dict(c_ctx=v_c_ctx, w_ada=v_w_ada, b_ada=v_b_ada, norm1_g=v_norm1_g, w_in=v_w_in, mla_q_norm_g=v_mla_q_norm_g,
                 w_q_up=v_w_q_up, mla_kv_norm_g=v_mla_kv_norm_g, w_kv_up=v_w_kv_up, gqa_q_norm_g=v_gqa_q_norm_g,
                 gqa_k_norm_g=v_gqa_k_norm_g, w_br_a=v_w_br_a, w_br_b=v_w_br_b, w_out=v_w_out, norm2_g=v_norm2_g,
                 w_up=v_w_up, conv_w=v_conv_w, conv_b=v_conv_b, w_down=v_w_down, final_norm_g=v_final_norm_g)
    order = list(weights)

    my_idx = 4 * lax.axis_index("x") + 2 * lax.axis_index("y") + lax.axis_index("c")
    xs, cts, tgt = x[0], ctx[0], loss_target[0]
    t, d = xs.shape
    tc = cts.shape[0]
    ta = t + tc
    kvl, ql = MLA_KV_LORA, MLA_Q_LORA
    nb = GQA_KV_HEADS * GQA_HEAD_DIM
    hb = GQA_HEADS * GQA_HEAD_DIM
    ha = MLA_HEADS
    f2 = w_up.shape[2] * N_DEV
    ff = f2 // 2

    big = ["w_in", "w_q_up", "w_kv_up", "w_br_a", "w_br_b", "w_out", "w_up", "w_down"]
    nw = len(big)
    del nw
    _ORDER_AFTER.clear()
    shards = {n: _cast_bf16(weights[n][0], "cast_" + n) for n in big}
    c_idx = jnp.reshape(lax.axis_index("c"), (1,)).astype(jnp.int32)

    def gather_start(names, dep):
        shs = [shards[n] for n in names]
        land = [lax.empty((N_DEV,) + s.shape, BF16) for s in shs]
        if dep is not None:
            _after(dep)
        s, r, arrs, tok = _split_start("gather_ici_start_" + names[0], shs + land, _gather_ici_copies(len(names)),
                                       5 * len(names))
        return dict(names=names, s=s, r=r, arrs=arrs, tok=tok)

    def gather_relay(g, after):
        n = len(g["names"])
        arrs = _split_wait("gather_ici_wait_" + g["names"][0], g["s"], g["r"], g["arrs"], _gather_ici_copies(n), after)
        s, r, bufs, tok = _split_start("gather_d2d_start_" + g["names"][0], arrs[n:], _gather_d2d_copies(n), 3 * n)
        g.update(s2=s, r2=r, bufs=bufs)
        return tok

    def gather_finish(g, after):
        n = len(g["names"])
        bufs = _split_wait("gather_d2d_wait_" + g["names"][0], g["s2"], g["r2"], g["bufs"], _gather_d2d_copies(n), after)
        return dict(zip(g["names"], bufs))

    c_all, cw_all = _all_gather([jnp.pad(c, ((0, 7), (0, 0))), jnp.pad(conv_w[0], ((0, 5), (0, 0)))], "gather_cond")
    conv_w_f = jnp.transpose(cw_all[:, :3, :], (1, 0, 2)).reshape(3, f2)
    conds = jnp.concatenate([c_all[:, 0, :], c_ctx[None, :], jnp.zeros((7, d), F32)], axis=0)
    ncol = w_ada.shape[2]
    b_shard = lax.dynamic_slice_in_dim(b_ada, my_idx * ncol, ncol, axis=1)
    ada_shard = _ada_fwd(conds, w_ada[0], b_shard)
    (ada_all,) = _all_gather([ada_shard], "gather_ada")
    ada = jnp.transpose(ada_all, (1, 0, 2)).reshape(16, N_DEV * ncol)
    lat = lax.dynamic_slice_in_dim(ada, my_idx, 1, axis=0).reshape(6, d)
    cxt = ada[8].reshape(6, d)
    zero2 = jnp.zeros((2, d), F32)
    mods1 = jnp.concatenate([lat[0:2], cxt[0:2], jnp.zeros((4, d), F32)], axis=0)
    mods2 = jnp.concatenate([lat[2:3], lat[3:4], lat[4:5], jnp.zeros((5, d), F32)], axis=0)
    mods2b = jnp.concatenate([lat[2:3], lat[4:5], jnp.zeros((6, d), F32)], axis=0)
    mods3 = jnp.concatenate([lat[5:6], jnp.zeros((7, d), F32)], axis=0)
    del zero2

    g0 = gather_start(["w_in"], ada_all)
    g1 = gather_start(["w_q_up", "w_kv_up", "w_br_a", "w_br_b", "w_out"], g0["tok"])
    g2 = gather_start(["w_up"], g1["tok"])
    g3 = gather_start(["w_down"], g2["tok"])

    ca, s1a, s2a = _rope_tabs(t, MLA_ROPE)
    cb_, s1b, s2b = _rope_tabs(t, GQA_HEAD_DIM)
    q_tabs_a = (_pad_cols(jnp.concatenate([jnp.ones((t, MLA_NOPE), F32), ca], 1), 0, MLA_SLOT),
                _pad_cols(s1a, MLA_NOPE, MLA_SLOT), _pad_cols(s2a, MLA_NOPE, MLA_SLOT))
    q_tabs_b = (cb_, s1b, s2b)
    k_tabs = (_with_ctx_rows(_pad_cols(ca, 0, LANE), tc, 1.0), _with_ctx_rows(_pad_cols(s1a, 0, LANE), tc, 0.0),
              _with_ctx_rows(_pad_cols(s2a, 0, LANE), tc, 0.0),
              _with_ctx_rows(cb_, tc, 1.0), _with_ctx_rows(s1b, tc, 0.0), _with_ctx_rows(s2b, tc, 0.0))

    def cols_full(g):
        return jnp.transpose(g, (1, 0, 2)).reshape(g.shape[1], N_DEV * g.shape[2])

    _after(gather_relay(g0, mods1))
    z_all = _norm_mod_fwd(cts, xs, norm1_g, mods1)
    gathered = gather_finish(g0, z_all)
    w_in_f = cols_full(gathered["w_in"])
    o_kpe, o_kb, o_vb = kvl, kvl + MLA_ROPE, kvl + MLA_ROPE + nb
    o_q = o_vb + nb
    o_g = o_q + ql + hb
    wkv_w = kvl + 2 * nb + LANE
    w_kv_p = jnp.concatenate([w_in_f[:, :kvl], w_in_f[:, o_kb:o_q], w_in_f[:, o_kpe:o_kb],
                              jnp.zeros((d, LANE - MLA_ROPE), BF16)], axis=1)
    q_w = ql + hb
    q_pad = (-q_w) % 512 if d >= 512 else (-q_w) % d
    gate_blk = (q_w + q_pad) // d
    assert (q_w + q_pad) % d == 0
    w_qg_p = jnp.concatenate([w_in_f[:, o_q:o_g], jnp.zeros((d, q_pad), BF16), w_in_f[:, o_g:]], axis=1)

    kv_all = _mm(z_all, w_kv_p, "nn", F32, "proj_kv", tm=1152, tn=wkv_w)
    qg = _mm(z_all, w_qg_p, "nn", F32, "proj_qg", tm=1024, tn=1024, rows=t)
    _after(gather_relay(g1, qg))
    kin, k_b, v_b = _key_prep_fwd(kv_all, mla_kv_norm_g, gqa_k_norm_g, k_tabs)
    cqn, q_b = _q_prep_fwd(qg, mla_q_norm_g, gqa_q_norm_g, q_tabs_b)
    gathered.update(gather_finish(g1, q_b))

    wq_f = cols_full(gathered["w_q_up"]).reshape(ql, ha, MLA_NOPE + MLA_ROPE)
    wq_ext = jnp.pad(wq_f, ((0, 0), (0, 0), (0, MLA_SLOT - MLA_NOPE - MLA_ROPE))).reshape(ql, ha * MLA_SLOT)
    wkv_f = cols_full(gathered["w_kv_up"]).reshape(kvl, ha, MLA_NOPE + MLA_V)
    wk_slots = jnp.pad(wkv_f[:, :, :MLA_NOPE], ((0, 0), (0, 0), (0, MLA_SLOT - MLA_NOPE))).reshape(kvl, ha * MLA_SLOT)
    wv_cols = wkv_f[:, :, MLA_NOPE:].reshape(kvl, ha * MLA_V)
    e_slot = jnp.pad(jnp.eye(MLA_ROPE, dtype=BF16),
                     ((0, LANE - MLA_ROPE), (MLA_NOPE, MLA_SLOT - MLA_NOPE - MLA_ROPE)))
    e_rows = jnp.concatenate([jnp.tile(e_slot, (1, ha)), jnp.zeros((LANE, ha * MLA_V), BF16)], axis=1)
    wkv_ext = jnp.concatenate([jnp.concatenate([wk_slots, wv_cols], axis=1), e_rows], axis=0)
    w_bra = cols_full(gathered["w_br_a"])
    w_brb = cols_full(gathered["w_br_b"])
    w_out_f = gathered["w_out"].reshape(d, d)

    kv_a = _mm(kin, wkv_ext, "nn", BF16, "kv_up", tm=1152, tn=1024)
    qa_raw = _mm(cqn, wq_ext, "nn", F32, "q_up", tm=1024, tn=1024)
    q_a = _rope_a(qa_raw, q_tabs_a, False, BF16, "rope_q_fwd")
    sc_a = float((MLA_NOPE + MLA_ROPE) ** -0.5)
    sc_b = float(GQA_HEAD_DIM ** -0.5)
    att_a = dict(hq=ha, hkv=ha, dk=MLA_SLOT, dv=MLA_V, k_blk0=0, v_blk0=ha * MLA_SLOT // MLA_V)
    att_b = dict(hq=GQA_HEADS, hkv=GQA_KV_HEADS, dk=GQA_HEAD_DIM, dv=GQA_HEAD_DIM, k_blk0=0, v_blk0=0)
    o_a, lse_a = _attention_fwd(q_a, kv_a, kv_a, sc_a, name="attn_a_fwd", **att_a)
    o_b, lse_b = _attention_fwd(q_b, k_b, v_b, sc_b, name="attn_b_fwd", **att_b)
    _after(gather_relay(g2, o_b))
    pa = _mm(o_a, w_bra, "nn", F32, "br_a", tm=1024, tn=1024)
    pb = _mm(o_b, w_brb, "nn", F32, "br_b", tm=1024, tn=1024)
    merged = _merge_fwd(pa, pb, qg, gate_blk)
    attn = _mm(merged, w_out_f, "nn", F32, "w_out", tm=1024, tn=1024)
    x1, z2 = _resid_norm_mod(xs, attn, norm2_g, mods2, "resid_norm2_fwd")
    w_up3 = gather_finish(g2, z2)["w_up"]
    _after(gather_relay(g3, z2))
    u = _mm_up_fwd(z2, w_up3, "w_up")
    w_down_f = gather_finish(g3, u)["w_down"].reshape(ff, d)
    h = _conv_fwd(u, conv_w_f, conv_b)
    ffn = _mm(h, w_down_f, "nn", F32, "w_down", tm=1024, tn=1024, tk=2816)

    def to_shards(g):
        return jnp.transpose(g.reshape(g.shape[0], N_DEV, g.shape[1] // N_DEV), (1, 0, 2))

    def reduce_start(tag, names, sends):
        n = len(sends)
        land = [lax.empty((4,) + s.shape[1:], s.dtype) for s in sends]
        s, r, arrs, tok = _split_start("reduce_d2d_start_" + tag, sends + land, _reduce_d2d_copies(n), 4 * n)
        return dict(tag=tag, names=names, s=s, r=r, arrs=arrs, tok=tok)

    def reduce_relay(g, after):
        n = len(g["names"])
        arrs = _split_wait("reduce_d2d_wait_" + g["tag"], g["s"], g["r"], g["arrs"], _reduce_d2d_copies(n), after)
        sums = [_pair_sum(arrs[a], arrs[n + a], c_idx, "pair_sum_" + g["names"][a]) for a in range(n)]
        land = [lax.empty(s.shape, s.dtype) for s in sums]
        s, r, arrs2, tok = _split_start("reduce_ici_start_" + g["tag"], sums + land, _reduce_ici_copies(n), 4 * n)
        g.update(s2=s, r2=r, arrs2=arrs2)
        return tok

    def reduce_finish(g, after):
        n = len(g["names"])
        arrs2 = _split_wait("reduce_ici_wait_" + g["tag"], g["s2"], g["r2"], g["arrs2"], _reduce_ici_copies(n), after)
        return dict(zip(g["names"], arrs2[n:]))

    dx2, dffn, st_fin = _final_loss(x1, ffn, final_norm_g[None, :], mods3, tgt)
    loss = lax.psum(st_fin[3, 0], MESH_AXES)
    dh = _mm(dffn, w_down_f, "nt", F32, "d_h", tm=1024, tn=1024)
    g_w_down = _mm(h, dffn, "tn", BF16, "g_w_down", tm=512, tn=1024)
    r_down = reduce_start("down", ["w_down"], [g_w_down.reshape(N_DEV, ff // N_DEV, d)])
    _after(r_down["tok"])
    du3, dcw, dcb = _conv_bwd(u, conv_w_f, conv_b, dh)
    dz2 = _mm_up_dz(du3, w_up3, "d_z2")
    g_w_up = _mm_up_gw(z2, du3, N_DEV, "g_w_up")
    g_conv_w = jnp.concatenate([dcw[0], dcw[1]], axis=1)
    tok = reduce_relay(r_down, g_w_up)
    _after(tok)
    r_up = reduce_start("up", ["w_up", "conv_w"], [g_w_up, to_shards(jnp.pad(g_conv_w, ((0, 5), (0, 0))))])
    _after(tok, r_up["tok"])
    dx1, dattn, st_n2 = _norm2_bwd(x1, attn, norm2_g, mods2b, dz2, dx2)
    dmerged = _mm(dattn, w_out_f, "nt", F32, "d_merged", tm=1024, tn=1024)
    g_w_out = _mm(merged, dattn, "tn", BF16, "g_w_out", tm=1024, tn=1024)
    dpa, dpb, dgates = _merge_bwd(dmerged, pa, pb, qg, gate_blk)
    do_a = _mm(dpa, w_bra, "nt", BF16, "d_o_a", tm=1024, tn=1024)
    do_b = _mm(dpb, w_brb, "nt", BF16, "d_o_b", tm=1024, tn=1024)
    g_w_bra = _mm(o_a, dpa, "tn", BF16, "g_w_br_a", tm=1024, tn=1024)
    g_w_brb = _mm(o_b, dpb, "tn", BF16, "g_w_br_b", tm=1024, tn=1024)
    tok = reduce_relay(r_up, g_w_brb)
    _after(tok)
    r_out = reduce_start("out", ["w_out", "w_br_a", "w_br_b"],
                         [g_w_out.reshape(N_DEV, d // N_DEV, d), to_shards(g_w_bra), to_shards(g_w_brb)])
    _after(tok, r_out["tok"])
    dq_a, dk_a, dv_a = _attention_bwd(q_a, kv_a, kv_a, do_a, lse_a, sc_a, name="attn_a_bwd", **att_a)
    dq_b, dk_b, dv_b = _attention_bwd(q_b, k_b, v_b, do_b, lse_b, sc_b, name="attn_b_bwd", **att_b)
    _after(reduce_relay(r_out, dv_b))
    dqa_raw = _rope_a(dq_a, q_tabs_a, True, BF16, "rope_q_bwd")
    dcqn = _mm(dqa_raw, wq_ext, "nt", F32, "d_cqn", tm=1024, tn=ql)
    g_wq_ext = _mm(cqn, dqa_raw, "tn", BF16, "g_w_q_up", tm=ql, tn=1024)
    dq_p, st_q, st_qb = _q_prep_bwd(qg, mla_q_norm_g, gqa_q_norm_g, q_tabs_b, dcqn, dq_b, q_pad)
    dkin = _mm_cat_nt([(dk_a, wkv_ext, 0), (dv_a, wkv_ext, ha * MLA_SLOT)], F32, "d_kin", tm=1152, tn=kvl + LANE)
    g_wkv_ext = _mm_cat_tn(kin, [dk_a, dv_a], BF16, "g_w_kv_up", tm=kvl + LANE, tn=min(1024, ha * MLA_V))
    dkv_p, st_kv, st_kb = _key_prep_bwd(kv_all, mla_kv_norm_g, gqa_k_norm_g, k_tabs, dkin, dk_b, dv_b)
    g_wq = g_wq_ext.reshape(ql, ha, MLA_SLOT)[:, :, :MLA_NOPE + MLA_ROPE].reshape(ql, ha * (MLA_NOPE + MLA_ROPE))
    g_wkv = jnp.concatenate([g_wkv_ext[:kvl, :ha * MLA_SLOT].reshape(kvl, ha, MLA_SLOT)[:, :, :MLA_NOPE],
                             g_wkv_ext[:kvl, ha * MLA_SLOT:].reshape(kvl, ha, MLA_V)], axis=2).reshape(kvl, ha * (MLA_NOPE + MLA_V))
    r_qkv = reduce_start("qkv", ["w_q_up", "w_kv_up"], [to_shards(g_wq), to_shards(g_wkv)])
    _after(r_qkv["tok"])
    g_wkv_p = _mm(z_all, dkv_p, "tn", BF16, "g_w_in_kv", tm=1024, tn=wkv_w)
    g_wqg_p = _mm_cat_tn(z_all, [dq_p, dgates], BF16, "g_w_in_qg", tm=1024, tn=min(1024, d), rows=t)
    g_w_in = jnp.concatenate([g_wkv_p[:, :kvl], g_wkv_p[:, kvl + 2 * nb:kvl + 2 * nb + MLA_ROPE],
                              g_wkv_p[:, kvl:kvl + 2 * nb], g_wqg_p[:, :q_w], g_wqg_p[:, q_w + q_pad:]], axis=1)
    r_in = reduce_start("in", ["w_in"], [to_shards(g_w_in)])
    _after(r_in["tok"])
    dz_lat = _mm_cat_nt([(dq_p, w_qg_p, 0), (dgates, w_qg_p, q_w + q_pad), (dkv_p, w_kv_p, 0)], F32, "d_z_lat",
                        tm=512, tn=1024, tk=min(2048, d), rows=t)
    dz_ctx = _mm(dkv_p, w_kv_p, "nt", F32, "d_z_ctx", tm=min(ROW_BLOCK, tc), tn=1024, a_row_off=t)
    grad_x, st_n1 = _norm1_bwd(cts, xs, norm1_g, mods1, dz_ctx, dz_lat, dx1)

    d_lat = jnp.concatenate([st_n1[0], st_n1[1], st_n2[3], st_n2[0], st_n2[1], st_fin[1]])
    d_cxt = jnp.concatenate([st_n1[3], st_n1[4], jnp.zeros((4 * d,), F32)])
    small = jnp.concatenate([d_lat, d_cxt, st_n1[2], st_q[0], st_kv[0], st_qb[0], st_kb[0], st_n2[2],
                             jnp.concatenate([dcb[0, 0], dcb[1, 0]]), st_fin[0]])
    n_small = small.shape[0]
    pad_small = (-n_small) % LANE
    (small_all,) = _all_gather([jnp.pad(small, (0, pad_small)).reshape(1, -1)], "gather_small")
    offs = {}
    o = 0
    for nm, ln in (("d_lat", 6 * d), ("d_cxt", 6 * d), ("norm1_g", d), ("mla_q_norm_g", ql), ("mla_kv_norm_g", kvl),
                   ("gqa_q_norm_g", GQA_HEAD_DIM), ("gqa_k_norm_g", GQA_HEAD_DIM), ("norm2_g", d), ("conv_b", f2),
                   ("final_norm_g", d)):
        offs[nm] = (o, ln)
        o += ln

    def part(nm):
        a, ln = offs[nm]
        return small_all[:, :, a:a + ln]

    d_lat_all = part("d_lat")[:, 0, :]
    d_cxt_sum = _sum_parts(part("d_cxt"))
    da16 = jnp.concatenate([d_lat_all, d_cxt_sum, jnp.zeros((7, 6 * d), F32)], axis=0)
    da16_shard = lax.dynamic_slice_in_dim(da16, my_idx * ncol, ncol, axis=1)
    cc_part = _cctx_partial(da16_shard, w_ada[0], c_ctx[None, :])
    (cc_all,) = _all_gather([cc_part], "gather_cctx")
    cc_parts = cc_all[:, 0:1, :]
    tok_q = reduce_relay(r_qkv, cc_all)
    tok_i = reduce_relay(r_in, cc_all)

    res = {}
    _after(tok_q, tok_i)

    def upd(nm, parts, shape2):
        wv, mv, vv = (a.reshape(shape2) for a in (weights[nm], mom_m[nm], mom_v[nm]))
        outs = _adamw(parts, wv, mv, vv, "adamw_" + nm)
        res[nm] = [o_.reshape(weights[nm].shape) for o_ in outs]

    for nm in ("norm1_g", "mla_q_norm_g", "mla_kv_norm_g", "gqa_q_norm_g", "gqa_k_norm_g", "norm2_g", "conv_b",
               "final_norm_g"):
        upd(nm, part(nm), (1, offs[nm][1]))
    upd("c_ctx", cc_parts, (1, d))
    b_parts = jnp.concatenate([d_lat_all[:, None, :], d_cxt_sum[None]], axis=0)
    upd("b_ada", b_parts, (1, 6 * d))
    _after(tok_q, tok_i)
    outs = _adamw_ada(conds, da16_shard, w_ada[0], m_w_ada[0], v_w_ada[0])
    res["w_ada"] = [o_[None] for o_ in outs]
    last = outs[0]
    for grp in (r_down, r_up, r_out, r_qkv, r_in):
        recv = reduce_finish(grp, last)
        for nm in grp["names"]:
            parts = recv[nm][:, :3, :] if nm == "conv_w" else recv[nm]
            upd(nm, parts, weights[nm].shape[1:])
            last = res[nm][0]

    return (loss, grad_x[None], *[res[n][0] for n in order], *[res[n][1] for n in order],
            *[res[n][2] for n in order], *[res[n][3] for n in order])
```

```python
import functools

import jax
import jax.numpy as jnp
from jax import lax
from jax.experimental import pallas as pl
from jax.experimental.pallas import tpu as pltpu

F32 = jnp.float32
BF16 = jnp.bfloat16

GRID_W = 64
ROPE_THETA = 10000.0
NORM_EPS = 1e-6
MLA_HEADS = 8
MLA_Q_LORA = 768
MLA_KV_LORA = 512
MLA_NOPE = 128
MLA_ROPE = 64
MLA_V = 128
GQA_HEADS = 8
GQA_KV_HEADS = 2
GQA_HEAD_DIM = 128
ADAM_LR = 0.001
ADAM_B1 = 0.9
ADAM_B2 = 0.999
ADAM_EPS = 1e-08
ADAM_WD = 0.01
ADAM_STEP = 10

N_DEV = 8
MESH_AXES = ("x", "y", "c")
LANE = 128
MLA_SLOT = 2 * LANE
VMEM_LIMIT = 56 * 1024 * 1024
ROW_BLOCK = 256
ATT_Q_BLOCK = 256
ATT_Q_BLOCK_FWD = 512
LN2 = 0.6931471805599453
LOG2E = 1.4426950408889634
MESH_ID = pl.DeviceIdType.MESH


def _tile(n, pref, align=LANE):
    if n <= pref:
        return n
    best = None
    t = align
    while t <= pref:
        if n % t == 0:
            best = t
        t += align
    assert best is not None, (n, pref, align)
    return best


def _cparams(sem=None):
    return pltpu.CompilerParams(dimension_semantics=sem, vmem_limit_bytes=VMEM_LIMIT)


_ORDER_AFTER = []


def _after(*arrays):
    _ORDER_AFTER.extend(arrays)


def _pcall(body, *, in_specs, **kw):
    deps = tuple(_ORDER_AFTER)
    _ORDER_AFTER.clear()
    if not deps:
        return pl.pallas_call(body, in_specs=in_specs, **kw)
    n_in, n_dep = len(in_specs), len(deps)

    def with_deps(*refs):
        body(*refs[:n_in], *refs[n_in + n_dep:])

    call = pl.pallas_call(with_deps, in_specs=list(in_specs) + [pl.BlockSpec(memory_space=pl.ANY)] * n_dep, **kw)
    return lambda *args: call(*args, *deps)


def _all_gather(arrs, name):
    n = len(arrs)

    def body(*refs):
        ins = refs[:n]
        outs = refs[n:2 * n]
        send_sems, recv_sems, local_sems = refs[2 * n:]
        x, y, c = lax.axis_index("x"), lax.axis_index("y"), lax.axis_index("c")
        me, sibling = (x, y, c), (x, y, 1 - c)
        chips = [(1 - x, y), (x, 1 - y), (1 - x, 1 - y)]

        def rows(a, dev):
            px, py, pc = dev
            return outs[a].at[4 * px + 2 * py + pc]

        def copy(a, k, block, to, src=None):
            return pltpu.make_async_remote_copy(
                src_ref=rows(a, block) if src is None else src,
                dst_ref=rows(a, block),
                send_sem=send_sems.at[7 * a + k],
                recv_sem=recv_sems.at[7 * a + k],
                device_id=to,
                device_id_type=MESH_ID,
            )

        mine = [pltpu.make_async_copy(ins[a], rows(a, me), local_sems.at[a]) for a in range(n)]
        for cp in mine:
            cp.start()
        first = []
        for a in range(n):
            first.append(copy(a, 0, me, sibling, src=ins[a]))
            first += [copy(a, 1 + j, me, (*chip, c), src=ins[a]) for j, chip in enumerate(chips)]
        for cp in first:
            cp.start()
        passed = []
        for j, chip in enumerate(chips):
            for a in range(n):
                copy(a, 1 + j, (*chip, c), me).wait_recv()
                fwd = copy(a, 4 + j, (*chip, c), sibling)
                fwd.start()
                passed.append(fwd)
        for a in range(n):
            copy(a, 0, sibling, me).wait_recv()
            for j, chip in enumerate(chips):
                copy(a, 4 + j, (*chip, 1 - c), me).wait_recv()
        for cp in first + passed:
            cp.wait_send()
        for cp in mine:
            cp.wait()

    any_spec = pl.BlockSpec(memory_space=pl.ANY)
    outs = _pcall(
        body,
        name=name,
        out_shape=[jax.ShapeDtypeStruct((N_DEV,) + a.shape, a.dtype) for a in arrs],
        in_specs=[any_spec] * n,
        out_specs=[any_spec] * n,
        scratch_shapes=[
            pltpu.SemaphoreType.DMA((7 * n,)),
            pltpu.SemaphoreType.DMA((7 * n,)),
            pltpu.SemaphoreType.DMA((n,)),
        ],
    )(*arrs)
    return list(outs)


def _all_to_all(arrs, name):
    n = len(arrs)

    def body(*refs):
        ins = refs[:n]
        outs = refs[n:2 * n]
        send_sems, recv_sems, local_sems = refs[2 * n:]
        x, y, c = lax.axis_index("x"), lax.axis_index("y"), lax.axis_index("c")
        my_idx = 4 * x + 2 * y + c

        def peer(k):
            fx, fy, fc = (k >> 2) & 1, (k >> 1) & 1, k & 1
            return (x ^ fx if fx else x, y ^ fy if fy else y, c ^ fc if fc else c)

        def copy(a, k):
            px, py, pc = peer(k)
            return pltpu.make_async_remote_copy(
                src_ref=ins[a].at[4 * px + 2 * py + pc],
                dst_ref=outs[a].at[my_idx],
                send_sem=send_sems.at[7 * a + k - 1],
                recv_sem=recv_sems.at[7 * a + k - 1],
                device_id=(px, py, pc),
                device_id_type=MESH_ID,
            )

        mine = [pltpu.make_async_copy(ins[a].at[my_idx], outs[a].at[my_idx], local_sems.at[a]) for a in range(n)]
        for cp in mine:
            cp.start()
        order = [1, 4, 2, 5, 3, 6, 7]
        cps = [copy(a, k) for k in order for a in range(n)]
        for cp in cps:
            cp.start()
        for cp in cps:
            cp.wait()
        for cp in mine:
            cp.wait()

    any_spec = pl.BlockSpec(memory_space=pl.ANY)
    outs = _pcall(
        body,
        name=name,
        out_shape=[jax.ShapeDtypeStruct(a.shape, a.dtype) for a in arrs],
        in_specs=[any_spec] * n,
        out_specs=[any_spec] * n,
        scratch_shapes=[
            pltpu.SemaphoreType.DMA((7 * n,)),
            pltpu.SemaphoreType.DMA((7 * n,)),
            pltpu.SemaphoreType.DMA((n,)),
        ],
    )(*arrs)
    return list(outs)


_HBM = pl.BlockSpec(memory_space=pltpu.HBM)
_SEM = pl.BlockSpec(memory_space=pltpu.SEMAPHORE)
_EFFECT = pltpu.SideEffectType.DATAFLOW_SIDE_EFFECTING


def _descriptors(copies, send_sems, recv_sems):
    descs = []
    for i, (src, dst, dev) in enumerate(copies):
        if dev is None:
            descs.append(pltpu.make_async_copy(src, dst, recv_sems.at[i]))
        else:
            descs.append(pltpu.make_async_remote_copy(src_ref=src, dst_ref=dst, send_sem=send_sems.at[i],
                                                      recv_sem=recv_sems.at[i], device_id=dev, device_id_type=MESH_ID))
    return descs


def _split_start(name, arrays, copies_fn, n_copies):
    n = len(arrays)

    def body(*refs):
        send_sems, recv_sems = refs[n], refs[n + 1]
        token = refs[2 * n + 2]
        for dsc in _descriptors(copies_fn(refs[:n]), send_sems, recv_sems):
            dsc.start()
        token[...] = jnp.zeros_like(token)

    outs = _pcall(
        body,
        name=name,
        out_shape=(pltpu.SemaphoreType.DMA((n_copies,)), pltpu.SemaphoreType.DMA((n_copies,)),
                   *[pltpu.HBM(a.shape, a.dtype) for a in arrays], jax.ShapeDtypeStruct((8, LANE), F32)),
        in_specs=[_HBM] * n,
        out_specs=(_SEM, _SEM, *[_HBM] * n, pl.BlockSpec(memory_space=pltpu.VMEM)),
        input_output_aliases={i: 2 + i for i in range(n)},
        compiler_params=pltpu.CompilerParams(has_side_effects=_EFFECT),
    )(*[pltpu.with_memory_space_constraint(a, pltpu.HBM) for a in arrays])
    return outs[0], outs[1], list(outs[2:2 + n]), outs[2 + n]


def _split_wait(name, send_sems, recv_sems, arrays, copies_fn, after):
    n = len(arrays)

    def body(*refs):
        for dsc, (_, _, dev) in zip(_descriptors(copies_fn(refs[:n]), refs[n], refs[n + 1]), copies_fn(refs[:n])):
            if dev is None:
                dsc.wait()
            else:
                dsc.wait_send()
                dsc.wait_recv()

    outs = _pcall(
        body,
        name=name,
        out_shape=tuple(pltpu.HBM(a.shape, a.dtype) for a in arrays),
        in_specs=[_HBM] * n + [_SEM, _SEM, pl.BlockSpec(memory_space=pl.ANY)],
        out_specs=tuple([_HBM] * n),
        input_output_aliases={i: i for i in range(n)},
        compiler_params=pltpu.CompilerParams(has_side_effects=_EFFECT),
    )(*arrays, send_sems, recv_sems, after)
    return list(outs)


def _mesh_pos():
    x, y, c = lax.axis_index("x"), lax.axis_index("y"), lax.axis_index("c")
    return x, y, c, [(1 - x, y), (x, 1 - y), (1 - x, 1 - y)]


def _gather_ici_copies(n):
    def copies(refs):
        x, y, c, chips = _mesh_pos()
        me = 4 * x + 2 * y + c
        out = []
        for a in range(n):
            src, buf = refs[a], refs[n + a]
            out.append((src, buf.at[me], None))
            out.append((src, buf.at[me], (x, y, 1 - c)))
            out += [(src, buf.at[me], (cx, cy, c)) for cx, cy in chips]
        return out
    return copies


def _gather_d2d_copies(n):
    def copies(refs):
        x, y, c, chips = _mesh_pos()
        out = []
        for a in range(n):
            for cx, cy in chips:
                rows = refs[a].at[4 * cx + 2 * cy + c]
                out.append((rows, rows, (x, y, 1 - c)))
        return out
    return copies


def _reduce_d2d_copies(n):
    def copies(refs):
        x, y, c, _ = _mesh_pos()
        out = []
        for a in range(n):
            for k in range(4):
                out.append((refs[a].at[2 * k + (1 - c)], refs[n + a].at[k], (x, y, 1 - c)))
        return out
    return copies


def _reduce_ici_copies(n):
    def copies(refs):
        x, y, c, chips = _mesh_pos()
        mine = 2 * x + y
        out = []
        for a in range(n):
            src, land = refs[a], refs[n + a]
            out.append((src.at[mine], land.at[mine], None))
            out += [(src.at[2 * cx + cy], land.at[mine], (cx, cy, c)) for cx, cy in chips]
        return out
    return copies


def _pair_sum(send, land, c_idx, name):
    _, r, cols = send.shape
    rb = _tile(r, max(8, (1 << 20) // (2 * cols) // 8 * 8), 8)
    dt = send.dtype

    def body(c_ref, s_ref, l_ref, o_ref):
        o_ref[...] = (s_ref[...].astype(F32) + l_ref[...].astype(F32)).astype(dt)

    return pl.pallas_call(
        body,
        name=name,
        out_shape=jax.ShapeDtypeStruct((4, r, cols), dt),
        grid_spec=pltpu.PrefetchScalarGridSpec(
            num_scalar_prefetch=1,
            grid=(4, r // rb),
            in_specs=[pl.BlockSpec((None, rb, cols), lambda k, i, c_ref: (2 * k + c_ref[0], i, 0)),
                      pl.BlockSpec((None, rb, cols), lambda k, i, c_ref: (k, i, 0))],
            out_specs=pl.BlockSpec((None, rb, cols), lambda k, i, c_ref: (k, i, 0)),
        ),
        compiler_params=_cparams(("parallel", "parallel")),
    )(c_idx, send, land)


_DIMS = {
    "nn": (((1,), (0,)), ((), ())),
    "nt": (((1,), (1,)), ((), ())),
    "tn": (((0,), (0,)), ((), ())),
}


def _mm_call(a, b, *, mode, grid, a_spec, b_spec, o_spec, out_shape, acc_shape, name):
    nk = grid[2]
    out_dtype = out_shape.dtype

    def body(a_ref, b_ref, o_ref, *scratch):
        p = lax.dot_general(a_ref[...].astype(BF16), b_ref[...].astype(BF16), _DIMS[mode],
                            preferred_element_type=F32)
        if nk == 1:
            o_ref[...] = p.astype(out_dtype)
        else:
            acc = scratch[0]
            k = pl.program_id(2)

            @pl.when(k == 0)
            def _():
                acc[...] = p

            @pl.when(k > 0)
            def _():
                acc[...] += p

            @pl.when(k == nk - 1)
            def _():
                o_ref[...] = acc[...].astype(out_dtype)

    return _pcall(
        body,
        name=name,
        out_shape=out_shape,
        grid=grid,
        in_specs=[a_spec, b_spec],
        out_specs=o_spec,
        scratch_shapes=[pltpu.VMEM(acc_shape, F32)] if nk > 1 else [],
        compiler_params=_cparams(("parallel", "parallel", "arbitrary")),
    )(a, b)


def _mm(a, b, mode, out_dtype, name, tm=512, tn=512, tk=2432, a_row_off=0, rows=None):
    if mode == "nn":
        (m, k), (k2, n) = a.shape, b.shape
    elif mode == "nt":
        (m, k), (n, k2) = a.shape, b.shape
    else:
        (k, m), (k2, n) = a.shape, b.shape
        if rows is not None:
            k = k2 = rows
    assert k == k2, (a.shape, b.shape, mode)
    if mode != "tn":
        m = (m if rows is None else rows + a_row_off) - a_row_off
    tm, tn, tk = _tile(m, tm, 8), _tile(n, tn), _tile(k, tk, 8 if mode == "tn" else LANE)
    assert a_row_off % tm == 0
    ro = a_row_off // tm
    grid = (m // tm, n // tn, k // tk)
    if mode == "tn":
        a_spec = pl.BlockSpec((tk, tm), lambda i, j, kk: (kk, i))
    else:
        a_spec = pl.BlockSpec((tm, tk), lambda i, j, kk: (i + ro, kk))
    if mode == "nt":
        b_spec = pl.BlockSpec((tn, tk), lambda i, j, kk: (j, kk))
    else:
        b_spec = pl.BlockSpec((tk, tn), lambda i, j, kk: (kk, j))
    o_spec = pl.BlockSpec((tm, tn), lambda i, j, kk: (i, j))
    return _mm_call(a, b, mode=mode, grid=grid, a_spec=a_spec, b_spec=b_spec, o_spec=o_spec,
                    out_shape=jax.ShapeDtypeStruct((m, n), out_dtype), acc_shape=(tm, tn), name=name)


def _mm_cat_nt(pieces, out_dtype, name, tm=1024, tn=1024, tk=2048, rows=None):
    m = pieces[0][0].shape[0] if rows is None else rows
    n = pieces[0][1].shape[0]
    tm, tn = _tile(m, tm, 8), _tile(n, tn)
    steps, starts, s = [], [], 0
    for a, b, off in pieces:
        kp = a.shape[1]
        tkp = _tile(kp, tk)
        assert off % tkp == 0 and b.shape[0] == n
        steps.append((tkp, kp // tkp, off // tkp))
        starts.append(s)
        s += kp // tkp
    nk = s
    npc = len(pieces)

    def body(*refs):
        o_ref, acc = refs[2 * npc], refs[2 * npc + 1]
        kk = pl.program_id(2)

        @pl.when(kk == 0)
        def _():
            acc[...] = jnp.zeros_like(acc)

        for p in range(npc):
            @pl.when((kk >= starts[p]) & (kk < starts[p] + steps[p][1]))
            def _(p=p):
                acc[...] += lax.dot_general(refs[2 * p][...].astype(BF16), refs[2 * p + 1][...].astype(BF16), _DIMS["nt"],
                                            preferred_element_type=F32)

        @pl.when(kk == nk - 1)
        def _():
            o_ref[...] = acc[...].astype(out_dtype)

    in_specs, args = [], []
    for p, (a, b, off) in enumerate(pieces):
        tkp, np_, ob = steps[p]

        def rel(kk, p=p, np_=np_):
            return jnp.clip(kk - starts[p], 0, np_ - 1)

        in_specs.append(pl.BlockSpec((tm, tkp), lambda i, j, kk, rel=rel: (i, rel(kk))))
        in_specs.append(pl.BlockSpec((tn, tkp), lambda i, j, kk, rel=rel, ob=ob: (j, ob + rel(kk))))
        args += [a, b]
    return _pcall(
        body,
        name=name,
        out_shape=jax.ShapeDtypeStruct((m, n), out_dtype),
        grid=(m // tm, n // tn, nk),
        in_specs=in_specs,
        out_specs=pl.BlockSpec((tm, tn), lambda i, j, kk: (i, j)),
        scratch_shapes=[pltpu.VMEM((tm, tn), F32)],
        compiler_params=_cparams(("parallel", "parallel", "arbitrary")),
    )(*args)


def _mm_cat_tn(a, pieces, out_dtype, name, tm=1024, tn=1024, rows=None):
    k = a.shape[0] if rows is None else rows
    m = a.shape[1]
    tm = _tile(m, tm)
    starts, s = [], 0
    for b in pieces:
        assert b.shape[1] % tn == 0
        starts.append(s)
        s += b.shape[1] // tn
    nj = s
    npc = len(pieces)

    def body(*refs):
        a_ref, o_ref = refs[0], refs[1 + npc]
        j = pl.program_id(1)
        for p in range(npc):
            @pl.when((j >= starts[p]) & (j < starts[p] + pieces[p].shape[1] // tn))
            def _(p=p):
                o_ref[...] = lax.dot_general(a_ref[...].astype(BF16), refs[1 + p][...].astype(BF16), _DIMS["tn"],
                                             preferred_element_type=F32).astype(out_dtype)

    in_specs = [pl.BlockSpec((k, tm), lambda i, j: (0, i))]
    for p, b in enumerate(pieces):
        np_ = b.shape[1] // tn
        in_specs.append(pl.BlockSpec((k, tn), lambda i, j, p=p, np_=np_: (0, jnp.clip(j - starts[p], 0, np_ - 1))))
    return _pcall(
        body,
        name=name,
        out_shape=jax.ShapeDtypeStruct((m, nj * tn), out_dtype),
        grid=(m // tm, nj),
        in_specs=in_specs,
        out_specs=pl.BlockSpec((tm, tn), lambda i, j: (i, j)),
        compiler_params=_cparams(("parallel", "arbitrary")),
    )(a, *pieces)


def _mm_up_fwd(z2, w3, name, tm=1024):
    t, d = z2.shape
    nsh, _, c = w3.shape
    tm = _tile(t, tm, 8)
    return _mm_call(z2, w3, mode="nn", grid=(t // tm, nsh, 1),
                    a_spec=pl.BlockSpec((tm, d), lambda i, j, kk: (i, 0)),
                    b_spec=pl.BlockSpec((None, d, c), lambda i, j, kk: (j, 0, 0)),
                    o_spec=pl.BlockSpec((tm, c), lambda i, j, kk: (i, j)),
                    out_shape=jax.ShapeDtypeStruct((t, nsh * c), F32), acc_shape=(tm, c), name=name)


def _mm_up_dz(du3, w3, name, tm=1024, tn=1024):
    _, t, f = du3.shape
    nsh, d, c = w3.shape
    half = nsh // 2
    assert f == half * c
    tm, tn = _tile(t, tm, 8), _tile(d, tn)
    return _mm_call(du3, w3, mode="nt", grid=(t // tm, d // tn, nsh),
                    a_spec=pl.BlockSpec((None, tm, c), lambda i, j, kk: (kk // half, i, kk % half)),
                    b_spec=pl.BlockSpec((None, tn, c), lambda i, j, kk: (kk, j, 0)),
                    o_spec=pl.BlockSpec((tm, tn), lambda i, j, kk: (i, j)),
                    out_shape=jax.ShapeDtypeStruct((t, d), F32), acc_shape=(tm, tn), name=name)


def _mm_up_gw(z2, du3, nsh, name, tm=1024):
    t, d = z2.shape
    f = du3.shape[2]
    half = nsh // 2
    c = f // half
    tm = _tile(d, tm)
    return _mm_call(z2, du3, mode="tn", grid=(d // tm, nsh, 1),
                    a_spec=pl.BlockSpec((t, tm), lambda i, j, kk: (0, i)),
                    b_spec=pl.BlockSpec((None, t, c), lambda i, j, kk: (j // half, 0, j % half)),
                    o_spec=pl.BlockSpec((None, tm, c), lambda i, j, kk: (j, i, 0)),
                    out_shape=jax.ShapeDtypeStruct((nsh, d, c), BF16), acc_shape=(tm, c), name=name)


def _rms(x):
    r = lax.rsqrt(jnp.mean(x * x, axis=-1, keepdims=True) + NORM_EPS)
    return x * r, r


def _rms_bwd(dxh, xh, r):
    return r * (dxh - xh * jnp.mean(dxh * xh, axis=-1, keepdims=True))


def _colsum(v):
    return jnp.sum(v, axis=0, keepdims=True)


def _rope(v, c, s1, s2, q):
    w = v.shape[-1]
    return v * c + pltpu.roll(v, w - q, 1) * s1 + pltpu.roll(v, q, 1) * s2


def _rope_t(d, c, s1, s2, q):
    w = d.shape[-1]
    return d * c + pltpu.roll(d * s1, q, 1) + pltpu.roll(d * s2, w - q, 1)


def _norm_mod_fwd(ctx, x, gain, mods):
    tc, d = ctx.shape
    t = x.shape[0]
    rb = min(ROW_BLOCK, tc)
    nbl = t // rb

    def body(ctx_ref, x_ref, g_ref, mod_ref, z_ref):
        i = pl.program_id(0)

        def emit(src, sh, sc):
            xh, _ = _rms(src[...])
            z_ref[...] = ((xh * g_ref[...]) * (1.0 + sc) + sh).astype(BF16)

        @pl.when(i >= nbl)
        def _():
            emit(ctx_ref, mod_ref[2:3, :], mod_ref[3:4, :])

        @pl.when(i < nbl)
        def _():
            emit(x_ref, mod_ref[0:1, :], mod_ref[1:2, :])

    return _pcall(
        body,
        name="norm1_mod_fwd",
        out_shape=jax.ShapeDtypeStruct((tc + t, d), BF16),
        grid=((tc + t) // rb,),
        in_specs=[
            pl.BlockSpec((rb, d), lambda i: (jnp.maximum(i - nbl, 0), 0)),
            pl.BlockSpec((rb, d), lambda i: (jnp.minimum(i, nbl - 1), 0)),
            pl.BlockSpec((1, d), lambda i: (0, 0)),
            pl.BlockSpec((8, d), lambda i: (0, 0)),
        ],
        out_specs=pl.BlockSpec((rb, d), lambda i: (i, 0)),
        compiler_params=_cparams(("arbitrary",)),
    )(ctx, x, gain, mods)


def _norm1_bwd(ctx, x, gain, mods, dz_ctx, dz_lat, dx1):
    tc, d = ctx.shape
    t = x.shape[0]
    rb = min(ROW_BLOCK, tc)
    nbl = t // rb

    def body(ctx_ref, x_ref, g_ref, mod_ref, dzc_ref, dzl_ref, dx1_ref, gx_ref, st_ref):
        i = pl.program_id(0)

        @pl.when(i == 0)
        def _():
            st_ref[...] = jnp.zeros_like(st_ref)

        def common(src, dz, sc, row_sh, row_sc):
            xh, r = _rms(src[...])
            g = g_ref[...]
            dxn = dz * (1.0 + sc)
            st_ref[row_sh:row_sh + 1, :] += _colsum(dz)
            st_ref[row_sc:row_sc + 1, :] += _colsum(dz * (xh * g))
            st_ref[2:3, :] += _colsum(dxn * xh)
            return _rms_bwd(dxn * g, xh, r)

        @pl.when(i >= nbl)
        def _():
            common(ctx_ref, dzc_ref[...], mod_ref[3:4, :], 3, 4)

        @pl.when(i < nbl)
        def _():
            gx_ref[...] = dx1_ref[...] + common(x_ref, dzl_ref[...], mod_ref[1:2, :], 0, 1)

    lat = lambda i: (jnp.minimum(i, nbl - 1), 0)
    cix = lambda i: (jnp.maximum(i - nbl, 0), 0)
    return _pcall(
        body,
        name="norm1_mod_bwd",
        out_shape=[jax.ShapeDtypeStruct((t, d), F32), jax.ShapeDtypeStruct((8, d), F32)],
        grid=((tc + t) // rb,),
        in_specs=[
            pl.BlockSpec((rb, d), cix),
            pl.BlockSpec((rb, d), lat),
            pl.BlockSpec((1, d), lambda i: (0, 0)),
            pl.BlockSpec((8, d), lambda i: (0, 0)),
            pl.BlockSpec((rb, d), cix),
            pl.BlockSpec((rb, d), lat),
            pl.BlockSpec((rb, d), lat),
        ],
        out_specs=[pl.BlockSpec((rb, d), lat), pl.BlockSpec((8, d), lambda i: (0, 0))],
        compiler_params=_cparams(("arbitrary",)),
    )(ctx, x, gain, mods, dz_ctx, dz_lat, dx1)


def _key_prep_fwd(kv, kv_gain, kb_gain, tabs):
    ta, wkv = kv.shape
    kvl = MLA_KV_LORA
    nb = GQA_KV_HEADS * GQA_HEAD_DIM
    rb = ROW_BLOCK if ta % ROW_BLOCK == 0 else LANE
    hd = GQA_HEAD_DIM

    def body(kv_ref, g_ref, gb_ref, ca, s1a, s2a, cb, s1b, s2b, kin_ref, kb_ref, vb_ref):
        xh, _ = _rms(kv_ref[:, 0:kvl])
        kin_ref[:, 0:kvl] = (xh * g_ref[...]).astype(BF16)
        kpe = kv_ref[:, kvl + 2 * nb:kvl + 2 * nb + LANE]
        kin_ref[:, kvl:kvl + LANE] = _rope(kpe, ca[...], s1a[...], s2a[...], MLA_ROPE // 4).astype(BF16)
        for h in range(GQA_KV_HEADS):
            nh, _ = _rms(kv_ref[:, kvl + h * hd:kvl + (h + 1) * hd])
            kb_ref[:, h * hd:(h + 1) * hd] = _rope(nh * gb_ref[...], cb[...], s1b[...], s2b[...], hd // 4).astype(BF16)
        vb_ref[...] = kv_ref[:, kvl + nb:kvl + 2 * nb].astype(BF16)

    row = lambda w: pl.BlockSpec((rb, w), lambda i: (i, 0))
    fix = lambda w: pl.BlockSpec((1, w), lambda i: (0, 0))
    return _pcall(
        body,
        name="key_prep_fwd",
        out_shape=[jax.ShapeDtypeStruct((ta, kvl + LANE), BF16), jax.ShapeDtypeStruct((ta, nb), BF16),
                   jax.ShapeDtypeStruct((ta, nb), BF16)],
        grid=(ta // rb,),
        in_specs=[row(wkv), fix(kvl), fix(hd)] + [row(LANE)] * 3 + [row(hd)] * 3,
        out_specs=[row(kvl + LANE), row(nb), row(nb)],
        compiler_params=_cparams(("parallel",)),
    )(kv, kv_gain, kb_gain, *tabs)


def _key_prep_bwd(kv, kv_gain, kb_gain, tabs, dkin, dkb, dvb):
    ta, wkv = kv.shape
    kvl = MLA_KV_LORA
    nb = GQA_KV_HEADS * GQA_HEAD_DIM
    rb = ROW_BLOCK if ta % ROW_BLOCK == 0 else LANE
    hd = GQA_HEAD_DIM

    def body(kv_ref, g_ref, gb_ref, ca, s1a, s2a, cb, s1b, s2b, dkin_ref, dkb_ref, dvb_ref, dkv_ref, st_ref, stb_ref):
        @pl.when(pl.program_id(0) == 0)
        def _():
            st_ref[...] = jnp.zeros_like(st_ref)
            stb_ref[...] = jnp.zeros_like(stb_ref)

        xh, r = _rms(kv_ref[:, 0:kvl])
        dn = dkin_ref[:, 0:kvl]
        st_ref[0:1, :] += _colsum(dn * xh)
        dkv_ref[:, 0:kvl] = _rms_bwd(dn * g_ref[...], xh, r).astype(BF16)
        dpe = _rope_t(dkin_ref[:, kvl:kvl + LANE], ca[...], s1a[...], s2a[...], MLA_ROPE // 4)
        dkv_ref[:, kvl + 2 * nb:kvl + 2 * nb + LANE] = dpe.astype(BF16)
        for h in range(GQA_KV_HEADS):
            nh, rh = _rms(kv_ref[:, kvl + h * hd:kvl + (h + 1) * hd])
            dn_h = _rope_t(dkb_ref[:, h * hd:(h + 1) * hd], cb[...], s1b[...], s2b[...], hd // 4)
            stb_ref[0:1, :] += _colsum(dn_h * nh)
            dkv_ref[:, kvl + h * hd:kvl + (h + 1) * hd] = _rms_bwd(dn_h * gb_ref[...], nh, rh).astype(BF16)
        dkv_ref[:, kvl + nb:kvl + 2 * nb] = dvb_ref[...].astype(BF16)

    row = lambda w: pl.BlockSpec((rb, w), lambda i: (i, 0))
    fix = lambda w: pl.BlockSpec((1, w), lambda i: (0, 0))
    return _pcall(
        body,
        name="key_prep_bwd",
        out_shape=[jax.ShapeDtypeStruct((ta, wkv), BF16), jax.ShapeDtypeStruct((8, kvl), F32),
                   jax.ShapeDtypeStruct((8, hd), F32)],
        grid=(ta // rb,),
        in_specs=[row(wkv), fix(kvl), fix(hd)] + [row(LANE)] * 3 + [row(hd)] * 3 + [row(kvl + LANE), row(nb), row(nb)],
        out_specs=[row(wkv), pl.BlockSpec((8, kvl), lambda i: (0, 0)), pl.BlockSpec((8, hd), lambda i: (0, 0))],
        compiler_params=_cparams(("arbitrary",)),
    )(kv, kv_gain, kb_gain, *tabs, dkin, dkb, dvb)


def _q_prep_fwd(qg, q_gain, qb_gain, tabs, qscale):
    t = qg.shape[0]
    ql = MLA_Q_LORA
    hd = GQA_HEAD_DIM
    hb = GQA_HEADS * hd
    rb = min(ROW_BLOCK, t)

    def body(q_ref, g_ref, gb_ref, cb, s1b, s2b, cqn_ref, qb_ref):
        xh, _ = _rms(q_ref[:, 0:ql])
        cqn_ref[...] = (xh * g_ref[...]).astype(BF16)
        for h in range(GQA_HEADS):
            nh, _ = _rms(q_ref[:, ql + h * hd:ql + (h + 1) * hd])
            qh = _rope(nh * gb_ref[...], cb[...], s1b[...], s2b[...], hd // 4)
            qb_ref[:, h * hd:(h + 1) * hd] = (qh * qscale).astype(BF16)

    row = lambda w: pl.BlockSpec((rb, w), lambda i: (i, 0))
    fix = lambda w: pl.BlockSpec((1, w), lambda i: (0, 0))
    return _pcall(
        body,
        name="q_prep_fwd",
        out_shape=[jax.ShapeDtypeStruct((t, ql), BF16), jax.ShapeDtypeStruct((t, hb), BF16)],
        grid=(t // rb,),
        in_specs=[row(ql + hb), fix(ql), fix(hd)] + [row(hd)] * 3,
        out_specs=[row(ql), row(hb)],
        compiler_params=_cparams(("parallel",)),
    )(qg, q_gain, qb_gain, *tabs)


def _q_prep_bwd(qg, q_gain, qb_gain, tabs, dcqn, dqb, wpad, qscale):
    t = qg.shape[0]
    ql = MLA_Q_LORA
    hd = GQA_HEAD_DIM
    hb = GQA_HEADS * hd
    rb = min(ROW_BLOCK, t)

    def body(q_ref, g_ref, gb_ref, cb, s1b, s2b, dcqn_ref, dqb_ref, dq_ref, st_ref, stb_ref):
        @pl.when(pl.program_id(0) == 0)
        def _():
            st_ref[...] = jnp.zeros_like(st_ref)
            stb_ref[...] = jnp.zeros_like(stb_ref)

        xh, r = _rms(q_ref[:, 0:ql])
        dn = dcqn_ref[...]
        st_ref[0:1, :] += _colsum(dn * xh)
        dq_ref[:, 0:ql] = _rms_bwd(dn * g_ref[...], xh, r).astype(BF16)
        for h in range(GQA_HEADS):
            nh, rh = _rms(q_ref[:, ql + h * hd:ql + (h + 1) * hd])
            dn_h = _rope_t(dqb_ref[:, h * hd:(h + 1) * hd] * qscale, cb[...], s1b[...], s2b[...], hd // 4)
            stb_ref[0:1, :] += _colsum(dn_h * nh)
            dq_ref[:, ql + h * hd:ql + (h + 1) * hd] = _rms_bwd(dn_h * gb_ref[...], nh, rh).astype(BF16)
        if wpad:
            dq_ref[:, ql + hb:ql + hb + wpad] = jnp.zeros((rb, wpad), BF16)

    row = lambda w: pl.BlockSpec((rb, w), lambda i: (i, 0))
    fix = lambda w: pl.BlockSpec((1, w), lambda i: (0, 0))
    return _pcall(
        body,
        name="q_prep_bwd",
        out_shape=[jax.ShapeDtypeStruct((t, ql + hb + wpad), BF16), jax.ShapeDtypeStruct((8, ql), F32),
                   jax.ShapeDtypeStruct((8, hd), F32)],
        grid=(t // rb,),
        in_specs=[row(ql + hb), fix(ql), fix(hd)] + [row(hd)] * 3 + [row(ql), row(hb)],
        out_specs=[row(ql + hb + wpad), pl.BlockSpec((8, ql), lambda i: (0, 0)), pl.BlockSpec((8, hd), lambda i: (0, 0))],
        compiler_params=_cparams(("arbitrary",)),
    )(qg, q_gain, qb_gain, *tabs, dcqn, dqb)


def _rope_a(v, tabs, transpose, out_dtype, name, qscale):
    t, w = v.shape
    rb = min(ROW_BLOCK, t)
    fn = _rope_t if transpose else _rope

    def body(v_ref, c, s1, s2, o_ref):
        for h in range(w // MLA_SLOT):
            sl = slice(h * MLA_SLOT, (h + 1) * MLA_SLOT)
            o_ref[:, sl] = (fn(v_ref[:, sl].astype(F32), c[...], s1[...], s2[...], MLA_ROPE // 4) * qscale).astype(out_dtype)

    row = lambda ww: pl.BlockSpec((rb, ww), lambda i: (i, 0))
    return _pcall(
        body,
        name=name,
        out_shape=jax.ShapeDtypeStruct((t, w), out_dtype),
        grid=(t // rb,),
        in_specs=[row(w)] + [row(MLA_SLOT)] * 3,
        out_specs=row(w),
        compiler_params=_cparams(("parallel",)),
    )(v, *tabs)


def _merge_fwd(pa, pb, qg, gate_blk):
    t, d = pa.shape
    rb = min(ROW_BLOCK, t)

    def body(pa_ref, pb_ref, ga_ref, gb_ref, o_ref):
        o_ref[...] = (jax.nn.sigmoid(ga_ref[...]) * pa_ref[...] + jax.nn.sigmoid(gb_ref[...]) * pb_ref[...]).astype(BF16)

    row = pl.BlockSpec((rb, d), lambda i: (i, 0))
    return _pcall(
        body,
        name="merge_fwd",
        out_shape=jax.ShapeDtypeStruct((t, d), BF16),
        grid=(t // rb,),
        in_specs=[row, row, pl.BlockSpec((rb, d), lambda i: (i, gate_blk)), pl.BlockSpec((rb, d), lambda i: (i, gate_blk + 1))],
        out_specs=row,
        compiler_params=_cparams(("parallel",)),
    )(pa, pb, qg, qg)


def _merge_bwd(dm, pa, pb, qg, gate_blk):
    t, d = pa.shape
    rb = min(ROW_BLOCK, t)

    def body(dm_ref, pa_ref, pb_ref, ga_ref, gb_ref, dpa_ref, dpb_ref, dg_ref):
        dmv = dm_ref[...]
        sa = jax.nn.sigmoid(ga_ref[...])
        sb = jax.nn.sigmoid(gb_ref[...])
        dpa_ref[...] = (dmv * sa).astype(BF16)
        dpb_ref[...] = (dmv * sb).astype(BF16)
        dg_ref[:, 0:d] = (dmv * pa_ref[...] * (sa * (1.0 - sa))).astype(BF16)
        dg_ref[:, d:2 * d] = (dmv * pb_ref[...] * (sb * (1.0 - sb))).astype(BF16)

    row = pl.BlockSpec((rb, d), lambda i: (i, 0))
    return _pcall(
        body,
        name="merge_bwd",
        out_shape=[jax.ShapeDtypeStruct((t, d), BF16), jax.ShapeDtypeStruct((t, d), BF16),
                   jax.ShapeDtypeStruct((t, 2 * d), BF16)],
        grid=(t // rb,),
        in_specs=[row, row, row, pl.BlockSpec((rb, d), lambda i: (i, gate_blk)), pl.BlockSpec((rb, d), lambda i: (i, gate_blk + 1))],
        out_specs=[row, row, pl.BlockSpec((rb, 2 * d), lambda i: (i, 0))],
        compiler_params=_cparams(("parallel",)),
    )(dm, pa, pb, qg, qg)


def _resid_norm_mod(x, branch, gain, mods, name):
    t, d = x.shape
    rb = min(ROW_BLOCK, t)

    def body(x_ref, b_ref, g_ref, mod_ref, x1_ref, z_ref):
        x1 = x_ref[...] + mod_ref[0:1, :] * b_ref[...]
        x1_ref[...] = x1
        xh, _ = _rms(x1)
        z_ref[...] = ((xh * g_ref[...]) * (1.0 + mod_ref[2:3, :]) + mod_ref[1:2, :]).astype(BF16)

    row = pl.BlockSpec((rb, d), lambda i: (i, 0))
    return _pcall(
        body,
        name=name,
        out_shape=[jax.ShapeDtypeStruct((t, d), F32), jax.ShapeDtypeStruct((t, d), BF16)],
        grid=(t // rb,),
        in_specs=[row, row, pl.BlockSpec((1, d), lambda i: (0, 0)), pl.BlockSpec((8, d), lambda i: (0, 0))],
        out_specs=[row, row],
        compiler_params=_cparams(("parallel",)),
    )(x, branch, gain, mods)


def _norm2_bwd(x1, attn, gain, mods, dz2, dx2):
    t, d = x1.shape
    rb = min(ROW_BLOCK, t)

    def body(x1_ref, at_ref, g_ref, mod_ref, dz_ref, dx2_ref, dx1_ref, da_ref, st_ref):
        @pl.when(pl.program_id(0) == 0)
        def _():
            st_ref[...] = jnp.zeros_like(st_ref)

        xh, r = _rms(x1_ref[...])
        g = g_ref[...]
        dz = dz_ref[...]
        dxn = dz * (1.0 + mod_ref[1:2, :])
        st_ref[0:1, :] += _colsum(dz)
        st_ref[1:2, :] += _colsum(dz * (xh * g))
        st_ref[2:3, :] += _colsum(dxn * xh)
        dx1 = dx2_ref[...] + _rms_bwd(dxn * g, xh, r)
        dx1_ref[...] = dx1
        st_ref[3:4, :] += _colsum(dx1 * at_ref[...])
        da_ref[...] = (dx1 * mod_ref[0:1, :]).astype(BF16)

    row = pl.BlockSpec((rb, d), lambda i: (i, 0))
    return _pcall(
        body,
        name="norm2_mod_bwd",
        out_shape=[jax.ShapeDtypeStruct((t, d), F32), jax.ShapeDtypeStruct((t, d), BF16), jax.ShapeDtypeStruct((8, d), F32)],
        grid=(t // rb,),
        in_specs=[row, row, pl.BlockSpec((1, d), lambda i: (0, 0)), pl.BlockSpec((8, d), lambda i: (0, 0)), row, row],
        out_specs=[row, row, pl.BlockSpec((8, d), lambda i: (0, 0))],
        compiler_params=_cparams(("arbitrary",)),
    )(x1, attn, gain, mods, dz2, dx2)


def _final_loss(x1, ffn, gain, mods, target):
    t, d = x1.shape
    rb = min(ROW_BLOCK, t)
    nb = t // rb

    def body(x1_ref, f_ref, g_ref, mod_ref, tg_ref, dx2_ref, df_ref, st_ref):
        i = pl.program_id(0)

        @pl.when(i == 0)
        def _():
            st_ref[...] = jnp.zeros_like(st_ref)

        ffn_v = f_ref[...]
        g2 = mod_ref[0:1, :]
        x2 = x1_ref[...] + g2 * ffn_v
        xh, r = _rms(x2)
        g = g_ref[...]
        err = xh * g - tg_ref[...]
        st_ref[2:3, :] += _colsum(err * err) * (0.5 / d)
        dy = err * (1.0 / d)
        st_ref[0:1, :] += _colsum(dy * xh)
        dx2 = _rms_bwd(dy * g, xh, r)
        dx2_ref[...] = dx2
        st_ref[1:2, :] += _colsum(dx2 * ffn_v)
        df_ref[...] = (dx2 * g2).astype(BF16)

        @pl.when(i == nb - 1)
        def _():
            st_ref[3:4, :] = jnp.broadcast_to(jnp.sum(st_ref[2:3, :], axis=-1, keepdims=True), (1, d))

    row = pl.BlockSpec((rb, d), lambda i: (i, 0))
    return _pcall(
        body,
        name="final_norm_loss",
        out_shape=[jax.ShapeDtypeStruct((t, d), F32), jax.ShapeDtypeStruct((t, d), BF16), jax.ShapeDtypeStruct((8, d), F32)],
        grid=(nb,),
        in_specs=[row, row, pl.BlockSpec((1, d), lambda i: (0, 0)), pl.BlockSpec((8, d), lambda i: (0, 0)), row],
        out_specs=[row, row, pl.BlockSpec((8, d), lambda i: (0, 0))],
        compiler_params=_cparams(("arbitrary",)),
    )(x1, ffn, gain, mods, target)


def _shift_rows(v, down):
    n = v.shape[0]
    rows = lax.broadcasted_iota(jnp.int32, v.shape, 0)
    if down:
        return jnp.where(rows == 0, 0.0, pltpu.roll(v, 1, 0))
    return jnp.where(rows == n - 1, 0.0, pltpu.roll(v, n - 1, 0))


def _conv_act(ua, ub, cwa, cwb, cba, cbb):
    a = cba + cwa[0:1, :] * _shift_rows(ua, True) + cwa[1:2, :] * ua + cwa[2:3, :] * _shift_rows(ua, False)
    b = cbb + cwb[0:1, :] * _shift_rows(ub, True) + cwb[1:2, :] * ub + cwb[2:3, :] * _shift_rows(ub, False)
    return a, b


def _conv_fwd(u, cw, cb):
    t, f2 = u.shape
    f = f2 // 2
    cbk = _tile(f, 256)
    nf = f // cbk

    def body(ua_ref, ub_ref, cwa_ref, cwb_ref, cba_ref, cbb_ref, h_ref):
        a, b = _conv_act(ua_ref[...], ub_ref[...], cwa_ref[...], cwb_ref[...], cba_ref[...], cbb_ref[...])
        h_ref[...] = (a * jax.nn.sigmoid(a) * b).astype(BF16)

    ca = lambda r: pl.BlockSpec((r, cbk), lambda j: (0, j))
    cbs = lambda r: pl.BlockSpec((r, cbk), lambda j: (0, nf + j))
    return _pcall(
        body,
        name="conv_gate_fwd",
        out_shape=jax.ShapeDtypeStruct((t, f), BF16),
        grid=(nf,),
        in_specs=[ca(t), cbs(t), ca(3), cbs(3), ca(1), cbs(1)],
        out_specs=ca(t),
        compiler_params=_cparams(("parallel",)),
    )(u, u, cw, cw, cb, cb)


def _conv_bwd(u, cw, cb, dh):
    t, f2 = u.shape
    f = f2 // 2
    cbk = _tile(f, 256)
    nf = f // cbk

    def body(ua_ref, ub_ref, cwa_ref, cwb_ref, cba_ref, cbb_ref, dh_ref, du_ref, dcw_ref, dcb_ref):
        ua, ub = ua_ref[...], ub_ref[...]
        cwa, cwb = cwa_ref[...], cwb_ref[...]
        a, b = _conv_act(ua, ub, cwa, cwb, cba_ref[...], cbb_ref[...])
        dh_v = dh_ref[...]
        sg = jax.nn.sigmoid(a)
        db = dh_v * (a * sg)
        da = dh_v * b * (sg * (1.0 + a * (1.0 - sg)))
        for idx, (dv, uu, cwv) in enumerate(((da, ua, cwa), (db, ub, cwb))):
            dcb_ref[idx] = _colsum(dv)
            dcw_ref[idx, 0:1, :] = _colsum(dv * _shift_rows(uu, True))
            dcw_ref[idx, 1:2, :] = _colsum(dv * uu)
            dcw_ref[idx, 2:3, :] = _colsum(dv * _shift_rows(uu, False))
            du = cwv[0:1, :] * _shift_rows(dv, False) + cwv[1:2, :] * dv + cwv[2:3, :] * _shift_rows(dv, True)
            du_ref[idx] = du.astype(BF16)

    ca = lambda r: pl.BlockSpec((r, cbk), lambda j: (0, j))
    cbs = lambda r: pl.BlockSpec((r, cbk), lambda j: (0, nf + j))
    o3 = lambda r: pl.BlockSpec((2, r, cbk), lambda j: (0, 0, j))
    return _pcall(
        body,
        name="conv_gate_bwd",
        out_shape=[jax.ShapeDtypeStruct((2, t, f), BF16), jax.ShapeDtypeStruct((2, 3, f), F32),
                   jax.ShapeDtypeStruct((2, 1, f), F32)],
        grid=(nf,),
        in_specs=[ca(t), cbs(t), ca(3), cbs(3), ca(1), cbs(1), ca(t)],
        out_specs=[o3(t), o3(3), o3(1)],
        compiler_params=_cparams(("parallel",)),
    )(u, u, cw, cw, cb, cb, dh)


def _attention_fwd(q, kk, vv, *, hq, hkv, dk, dv, k_blk0, v_blk0, name):
    t = q.shape[0]
    tk = kk.shape[0]
    g_sz = hq // hkv
    tq = min(ATT_Q_BLOCK_FWD, t)

    def body(q_ref, k_ref, v_ref, o_ref, lse_ref):
        k = k_ref[...]
        v = v_ref[...]
        for j in range(g_sz):
            s = lax.dot_general(q_ref[:, j * dk:(j + 1) * dk], k, _DIMS["nt"], preferred_element_type=F32)
            m = jnp.max(s, axis=-1, keepdims=True)
            p = jnp.exp2(s - m)
            l = jnp.sum(p, axis=-1, keepdims=True)
            o = jnp.dot(p.astype(BF16), v, preferred_element_type=F32) / l
            o_ref[:, j * dv:(j + 1) * dv] = o.astype(BF16)
            lse_ref[0, :, j:j + 1] = m + jnp.log2(l)

    return _pcall(
        body,
        name=name,
        out_shape=[jax.ShapeDtypeStruct((t, hq * dv), BF16), jax.ShapeDtypeStruct((hkv, t, g_sz), F32)],
        grid=(hkv, t // tq),
        in_specs=[
            pl.BlockSpec((tq, g_sz * dk), lambda g, i: (i, g)),
            pl.BlockSpec((tk, dk), lambda g, i: (0, k_blk0 + g)),
            pl.BlockSpec((tk, dv), lambda g, i: (0, v_blk0 + g)),
        ],
        out_specs=[
            pl.BlockSpec((tq, g_sz * dv), lambda g, i: (i, g)),
            pl.BlockSpec((1, tq, g_sz), lambda g, i: (g, i, 0)),
        ],
        compiler_params=_cparams(("parallel", "parallel")),
    )(q, kk, vv)


def _attention_bwd(q, kk, vv, do, lse, *, hq, hkv, dk, dv, k_blk0, v_blk0, name):
    t = q.shape[0]
    tk = kk.shape[0]
    g_sz = hq // hkv
    tq = min(ATT_Q_BLOCK, t)

    def body(q_ref, k_ref, v_ref, do_ref, lse_ref, dq_ref, dk_ref, dv_ref):
        @pl.when(pl.program_id(1) == 0)
        def _():
            dk_ref[...] = jnp.zeros_like(dk_ref)
            dv_ref[...] = jnp.zeros_like(dv_ref)

        k = k_ref[...]
        v = v_ref[...]
        for j in range(g_sz):
            qj = q_ref[:, j * dk:(j + 1) * dk]
            doj = do_ref[:, j * dv:(j + 1) * dv]
            s = lax.dot_general(qj, k, _DIMS["nt"], preferred_element_type=F32)
            p = jnp.exp2(s - lse_ref[0, :, j:j + 1])
            dp = lax.dot_general(doj, v, _DIMS["nt"], preferred_element_type=F32)
            ds = (p * (dp - jnp.sum(p * dp, axis=-1, keepdims=True)) * LN2).astype(BF16)
            dv_ref[...] += lax.dot_general(p.astype(BF16), doj, _DIMS["tn"], preferred_element_type=F32)
            dk_ref[...] += lax.dot_general(ds, qj, _DIMS["tn"], preferred_element_type=F32)
            dq_ref[:, j * dk:(j + 1) * dk] = jnp.dot(ds, k, preferred_element_type=F32)

    return _pcall(
        body,
        name=name,
        out_shape=[jax.ShapeDtypeStruct((t, hq * dk), F32), jax.ShapeDtypeStruct((tk, hkv * dk), F32),
                   jax.ShapeDtypeStruct((tk, hkv * dv), F32)],
        grid=(hkv, t // tq),
        in_specs=[
            pl.BlockSpec((tq, g_sz * dk), lambda g, i: (i, g)),
            pl.BlockSpec((tk, dk), lambda g, i: (0, k_blk0 + g)),
            pl.BlockSpec((tk, dv), lambda g, i: (0, v_blk0 + g)),
            pl.BlockSpec((tq, g_sz * dv), lambda g, i: (i, g)),
            pl.BlockSpec((1, tq, g_sz), lambda g, i: (g, i, 0)),
        ],
        out_specs=[
            pl.BlockSpec((tq, g_sz * dk), lambda g, i: (i, g)),
            pl.BlockSpec((tk, dk), lambda g, i: (0, g)),
            pl.BlockSpec((tk, dv), lambda g, i: (0, g)),
        ],
        compiler_params=_cparams(("parallel", "arbitrary")),
    )(q, kk, vv, do, lse)


def _silu(v):
    return v * jax.nn.sigmoid(v)


def _ada_fwd(conds, w_ada, b_ada_shard):
    r, d = conds.shape
    n = w_ada.shape[1]
    tn = _tile(n, 512)

    def body(c_ref, w_ref, b_ref, o_ref):
        s = _silu(c_ref[...]).astype(BF16)
        o_ref[...] = jnp.dot(s, w_ref[...].astype(BF16), preferred_element_type=F32) + b_ref[...]

    return _pcall(
        body,
        name="ada_fwd",
        out_shape=jax.ShapeDtypeStruct((r, n), F32),
        grid=(n // tn,),
        in_specs=[pl.BlockSpec((r, d), lambda j: (0, 0)), pl.BlockSpec((d, tn), lambda j: (0, j)),
                  pl.BlockSpec((1, tn), lambda j: (0, j))],
        out_specs=pl.BlockSpec((r, tn), lambda j: (0, j)),
        compiler_params=_cparams(("parallel",)),
    )(conds, w_ada, b_ada_shard)


def _cctx_partial(da16_shard, w_ada, c_ctx_row):
    d, n = w_ada.shape
    td = _tile(d, 512)

    def body(g_ref, w_ref, c_ref, o_ref):
        ds = lax.dot_general(g_ref[8:16, :].astype(BF16), w_ref[...].astype(BF16), _DIMS["nt"],
                             preferred_element_type=F32)
        cv = c_ref[...]
        sg = jax.nn.sigmoid(cv)
        o_ref[...] = ds * (sg * (1.0 + cv * (1.0 - sg)))

    return _pcall(
        body,
        name="cctx_partial",
        out_shape=jax.ShapeDtypeStruct((8, d), F32),
        grid=(d // td,),
        in_specs=[pl.BlockSpec((16, n), lambda j: (0, 0)), pl.BlockSpec((td, n), lambda j: (j, 0)),
                  pl.BlockSpec((1, td), lambda j: (0, j))],
        out_specs=pl.BlockSpec((8, td), lambda j: (0, j)),
        compiler_params=_cparams(("parallel",)),
    )(da16_shard, w_ada, c_ctx_row)


def _sum_parts(parts):
    p, _, n = parts.shape

    def body(p_ref, o_ref):
        acc = p_ref[0]
        for s in range(1, p):
            acc = acc + p_ref[s]
        o_ref[...] = acc

    return _pcall(
        body,
        name="sum_parts",
        out_shape=jax.ShapeDtypeStruct((1, n), F32),
        in_specs=[pl.BlockSpec(memory_space=pltpu.VMEM)],
        out_specs=pl.BlockSpec(memory_space=pltpu.VMEM),
    )(parts)


def _adam_math(w, g, m, v):
    m2 = ADAM_B1 * m + (1.0 - ADAM_B1) * g
    v2 = ADAM_B2 * v + (1.0 - ADAM_B2) * jnp.square(g)
    m_hat = m2 / (1.0 - ADAM_B1 ** ADAM_STEP)
    v_hat = v2 / (1.0 - ADAM_B2 ** ADAM_STEP)
    delta = -ADAM_LR * (m_hat / (jnp.sqrt(v_hat) + ADAM_EPS) + ADAM_WD * w)
    return delta, m2, v2


def _adamw(parts, w, m, v, name):
    p, r, c = parts.shape
    rb = _tile(r, max(8, (1 << 20) // (4 * c) // 8 * 8), 8)

    def body(p_ref, w_ref, m_ref, v_ref, g_ref, d_ref, m2_ref, v2_ref):
        g = p_ref[0].astype(F32)
        for s in range(1, p):
            g = g + p_ref[s].astype(F32)
        g_ref[...] = g
        d_ref[...], m2_ref[...], v2_ref[...] = _adam_math(w_ref[...], g, m_ref[...], v_ref[...])

    row = pl.BlockSpec((rb, c), lambda i: (i, 0))
    return _pcall(
        body,
        name=name,
        out_shape=[jax.ShapeDtypeStruct((r, c), F32)] * 4,
        grid=(r // rb,),
        in_specs=[pl.BlockSpec((p, rb, c), lambda i: (0, i, 0)), row, row, row],
        out_specs=[row] * 4,
        compiler_params=_cparams(("parallel",)),
    )(parts, w, m, v)


def _adamw_ada(conds, da16, w, m, v):
    d, n = w.shape
    rb = _tile(d, 256, LANE)

    def body(s_ref, da_ref, w_ref, m_ref, v_ref, g_ref, d_ref, m2_ref, v2_ref):
        g = lax.dot_general(_silu(s_ref[...]).astype(BF16), da_ref[...].astype(BF16), _DIMS["tn"],
                            preferred_element_type=F32)
        g_ref[...] = g
        d_ref[...], m2_ref[...], v2_ref[...] = _adam_math(w_ref[...], g, m_ref[...], v_ref[...])

    row = pl.BlockSpec((rb, n), lambda i: (i, 0))
    return _pcall(
        body,
        name="adamw_w_ada",
        out_shape=[jax.ShapeDtypeStruct((d, n), F32)] * 4,
        grid=(d // rb,),
        in_specs=[pl.BlockSpec((16, rb), lambda i: (0, i)), pl.BlockSpec((16, n), lambda i: (0, 0)), row, row, row],
        out_specs=[row] * 4,
        compiler_params=_cparams(("parallel",)),
    )(conds, da16, w, m, v)


def _cast_bf16(a, name):
    r, c = a.shape
    rb = _tile(r, 512, 8)

    def body(a_ref, o_ref):
        o_ref[...] = a_ref[...].astype(BF16)

    row = pl.BlockSpec((rb, c), lambda i: (i, 0))
    return _pcall(body, name=name, out_shape=jax.ShapeDtypeStruct((r, c), BF16), grid=(r // rb,),
                          in_specs=[row], out_specs=row, compiler_params=_cparams(("parallel",)))(a)


def _rope_tabs(t, rot):
    half, q = rot // 2, rot // 4
    n_rows = t // GRID_W
    row = jnp.repeat(jnp.arange(n_rows, dtype=F32), GRID_W)
    col = jnp.tile(jnp.arange(GRID_W, dtype=F32), n_rows)
    inv_freq = ROPE_THETA ** (-jnp.arange(0, half, 2, dtype=F32) / half)
    ang = jnp.concatenate([row[:, None] * inv_freq, col[:, None] * inv_freq], axis=-1)
    cos, sin = jnp.cos(ang), jnp.sin(ang)
    c0, c1, s0, s1 = cos[:, :q], cos[:, q:], sin[:, :q], sin[:, q:]
    z = jnp.zeros_like(s0)
    return (jnp.concatenate([c0, c0, c1, c1], -1), jnp.concatenate([-s0, z, -s1, z], -1),
            jnp.concatenate([z, s0, z, s1], -1))


def _pad_cols(a, left, total, fill=0.0):
    return jnp.pad(a, ((0, 0), (left, total - left - a.shape[1])), constant_values=fill)


def _with_ctx_rows(tab, tc, fill):
    return jnp.concatenate([tab, jnp.full((tc, tab.shape[1]), fill, F32)], axis=0)


def kernel(x, c, ctx, c_ctx, w_ada, b_ada, norm1_g, w_in, mla_q_norm_g, w_q_up, mla_kv_norm_g, w_kv_up, gqa_q_norm_g, gqa_k_norm_g, w_br_a, w_br_b, w_out, norm2_g, w_up, conv_w, conv_b, w_down, final_norm_g, loss_target, m_c_ctx, m_w_ada, m_b_ada, m_norm1_g, m_w_in, m_mla_q_norm_g, m_w_q_up, m_mla_kv_norm_g, m_w_kv_up, m_gqa_q_norm_g, m_gqa_k_norm_g, m_w_br_a, m_w_br_b, m_w_out, m_norm2_g, m_w_up, m_conv_w, m_conv_b, m_w_down, m_final_norm_g, v_c_ctx, v_w_ada, v_b_ada, v_norm1_g, v_w_in, v_mla_q_norm_g, v_w_q_up, v_mla_kv_norm_g, v_w_kv_up, v_gqa_q_norm_g, v_gqa_k_norm_g, v_w_br_a, v_w_br_b, v_w_out, v_norm2_g, v_w_up, v_conv_w, v_conv_b, v_w_down, v_final_norm_g):
    weights = dict(c_ctx=c_ctx, w_ada=w_ada, b_ada=b_ada, norm1_g=norm1_g, w_in=w_in, mla_q_norm_g=mla_q_norm_g,
                   w_q_up=w_q_up, mla_kv_norm_g=mla_kv_norm_g, w_kv_up=w_kv_up, gqa_q_norm_g=gqa_q_norm_g,
                   gqa_k_norm_g=gqa_k_norm_g, w_br_a=w_br_a, w_br_b=w_br_b, w_out=w_out, norm2_g=norm2_g, w_up=w_up,
                   conv_w=conv_w, conv_b=conv_b, w_down=w_down, final_norm_g=final_norm_g)
    mom_m = dict(c_ctx=m_c_ctx, w_ada=m_w_ada, b_ada=m_b_ada, norm1_g=m_norm1_g, w_in=m_w_in, mla_q_norm_g=m_mla_q_norm_g,
                 w_q_up=m_w_q_up, mla_kv_norm_g=m_mla_kv_norm_g, w_kv_up=m_w_kv_up, gqa_q_norm_g=m_gqa_q_norm_g,
                 gqa_k_norm_g=m_gqa_k_norm_g, w_br_a=m_w_br_a, w_br_b=m_w_br_b, w_out=m_w_out, norm2_g=m_norm2_g,
                 w_up=m_w_up, conv_w=m_conv_w, conv_b=m_conv_b, w_down=m_w_down, final_norm_g=m_final_norm_g)
    mom_v = dict(c_ctx=v_c_ctx, w_ada=v_w_ada, b_ada=v_b_ada, norm1_g=v_norm1_g, w_in=v_w_in, mla_q_norm_g=v_mla_q_norm_g,
                 w_q_up=v_w_q_up, mla_kv_norm_g=v_mla_kv_norm_g, w_kv_up=v_w_kv_up, gqa_q_norm_g=v_gqa_q_norm_g,
                 gqa_k_norm_g=v_gqa_k_norm_g, w_br_a=v_w_br_a, w_br_b=v_w_br_b, w_out=v_w_out, norm2_g=v_norm2_g,
                 w_up=v_w_up, conv_w=v_conv_w, conv_b=v_conv_b, w_down=v_w_down, final_norm_g=v_final_norm_g)
    order = list(weights)

    my_idx = 4 * lax.axis_index("x") + 2 * lax.axis_index("y") + lax.axis_index("c")
    xs, cts, tgt = x[0], ctx[0], loss_target[0]
    t, d = xs.shape
    tc = cts.shape[0]
    ta = t + tc
    kvl, ql = MLA_KV_LORA, MLA_Q_LORA
    nb = GQA_KV_HEADS * GQA_HEAD_DIM
    hb = GQA_HEADS * GQA_HEAD_DIM
    ha = MLA_HEADS
    f2 = w_up.shape[2] * N_DEV
    ff = f2 // 2

    big = ["w_in", "w_q_up", "w_kv_up", "w_br_a", "w_br_b", "w_out", "w_up", "w_down"]
    nw = len(big)
    del nw
    _ORDER_AFTER.clear()
    shards = {n: _cast_bf16(weights[n][0], "cast_" + n) for n in big}
    c_idx = jnp.reshape(lax.axis_index("c"), (1,)).astype(jnp.int32)

    def gather_start(names, dep):
        shs = [shards[n] for n in names]
        land = [lax.empty((N_DEV,) + s.shape, BF16) for s in shs]
        if dep is not None:
            _after(dep)
        s, r, arrs, tok = _split_start("gather_ici_start_" + names[0], shs + land, _gather_ici_copies(len(names)),
                                       5 * len(names))
        return dict(names=names, s=s, r=r, arrs=arrs, tok=tok)

    def gather_relay(g, after):
        n = len(g["names"])
        arrs = _split_wait("gather_ici_wait_" + g["names"][0], g["s"], g["r"], g["arrs"], _gather_ici_copies(n), after)
        s, r, bufs, tok = _split_start("gather_d2d_start_" + g["names"][0], arrs[n:], _gather_d2d_copies(n), 3 * n)
        g.update(s2=s, r2=r, bufs=bufs)
        return tok

    def gather_finish(g, after):
        n = len(g["names"])
        bufs = _split_wait("gather_d2d_wait_" + g["names"][0], g["s2"], g["r2"], g["bufs"], _gather_d2d_copies(n), after)
        return dict(zip(g["names"], bufs))

    c_all, cw_all = _all_gather([jnp.pad(c, ((0, 7), (0, 0))), jnp.pad(conv_w[0], ((0, 5), (0, 0)))], "gather_cond")
    conv_w_f = jnp.transpose(cw_all[:, :3, :], (1, 0, 2)).reshape(3, f2)
    conds = jnp.concatenate([c_all[:, 0, :], c_ctx[None, :], jnp.zeros((7, d), F32)], axis=0)
    ncol = w_ada.shape[2]
    b_shard = lax.dynamic_slice_in_dim(b_ada, my_idx * ncol, ncol, axis=1)
    ada_shard = _ada_fwd(conds, w_ada[0], b_shard)
    (ada_all,) = _all_gather([ada_shard], "gather_ada")
    ada = jnp.transpose(ada_all, (1, 0, 2)).reshape(16, N_DEV * ncol)
    lat = lax.dynamic_slice_in_dim(ada, my_idx, 1, axis=0).reshape(6, d)
    cxt = ada[8].reshape(6, d)
    zero2 = jnp.zeros((2, d), F32)
    mods1 = jnp.concatenate([lat[0:2], cxt[0:2], jnp.zeros((4, d), F32)], axis=0)
    mods2 = jnp.concatenate([lat[2:3], lat[3:4], lat[4:5], jnp.zeros((5, d), F32)], axis=0)
    mods2b = jnp.concatenate([lat[2:3], lat[4:5], jnp.zeros((6, d), F32)], axis=0)
    mods3 = jnp.concatenate([lat[5:6], jnp.zeros((7, d), F32)], axis=0)
    del zero2

    g0 = gather_start(["w_in"], ada_all)
    g1 = gather_start(["w_q_up", "w_kv_up", "w_br_a", "w_br_b", "w_out"], g0["tok"])
    g2 = gather_start(["w_up"], g1["tok"])
    g3 = gather_start(["w_down"], g2["tok"])

    ca, s1a, s2a = _rope_tabs(t, MLA_ROPE)
    cb_, s1b, s2b = _rope_tabs(t, GQA_HEAD_DIM)
    q_tabs_a = (_pad_cols(jnp.concatenate([jnp.ones((t, MLA_NOPE), F32), ca], 1), 0, MLA_SLOT),
                _pad_cols(s1a, MLA_NOPE, MLA_SLOT), _pad_cols(s2a, MLA_NOPE, MLA_SLOT))
    q_tabs_b = (cb_, s1b, s2b)
    k_tabs = (_with_ctx_rows(_pad_cols(ca, 0, LANE), tc, 1.0), _with_ctx_rows(_pad_cols(s1a, 0, LANE), tc, 0.0),
              _with_ctx_rows(_pad_cols(s2a, 0, LANE), tc, 0.0),
              _with_ctx_rows(cb_, tc, 1.0), _with_ctx_rows(s1b, tc, 0.0), _with_ctx_rows(s2b, tc, 0.0))

    def cols_full(g):
        return jnp.transpose(g, (1, 0, 2)).reshape(g.shape[1], N_DEV * g.shape[2])

    _after(gather_relay(g0, mods1))
    z_all = _norm_mod_fwd(cts, xs, norm1_g, mods1)
    gathered = gather_finish(g0, z_all)
    w_in_f = cols_full(gathered["w_in"])
    o_kpe, o_kb, o_vb = kvl, kvl + MLA_ROPE, kvl + MLA_ROPE + nb
    o_q = o_vb + nb
    o_g = o_q + ql + hb
    wkv_w = kvl + 2 * nb + LANE
    w_kv_p = jnp.concatenate([w_in_f[:, :kvl], w_in_f[:, o_kb:o_q], w_in_f[:, o_kpe:o_kb],
                              jnp.zeros((d, LANE - MLA_ROPE), BF16)], axis=1)
    q_w = ql + hb
    q_pad = (-q_w) % 512 if d >= 512 else (-q_w) % d
    gate_blk = (q_w + q_pad) // d
    assert (q_w + q_pad) % d == 0
    w_qg_p = jnp.concatenate([w_in_f[:, o_q:o_g], jnp.zeros((d, q_pad), BF16), w_in_f[:, o_g:]], axis=1)

    kv_all = _mm(z_all, w_kv_p, "nn", F32, "proj_kv", tm=1152, tn=wkv_w)
    qg = _mm(z_all, w_qg_p, "nn", F32, "proj_qg", tm=1024, tn=1024, rows=t)
    _after(gather_relay(g1, qg))
    kin, k_b, v_b = _key_prep_fwd(kv_all, mla_kv_norm_g, gqa_k_norm_g, k_tabs)
    sc_a = float((MLA_NOPE + MLA_ROPE) ** -0.5) * LOG2E
    sc_b = float(GQA_HEAD_DIM ** -0.5) * LOG2E
    cqn, q_b = _q_prep_fwd(qg, mla_q_norm_g, gqa_q_norm_g, q_tabs_b, sc_b)
    gathered.update(gather_finish(g1, q_b))

    wq_f = cols_full(gathered["w_q_up"]).reshape(ql, ha, MLA_NOPE + MLA_ROPE)
    wq_ext = jnp.pad(wq_f, ((0, 0), (0, 0), (0, MLA_SLOT - MLA_NOPE - MLA_ROPE))).reshape(ql, ha * MLA_SLOT)
    wkv_f = cols_full(gathered["w_kv_up"]).reshape(kvl, ha, MLA_NOPE + MLA_V)
    wk_slots = jnp.pad(wkv_f[:, :, :MLA_NOPE], ((0, 0), (0, 0), (0, MLA_SLOT - MLA_NOPE))).reshape(kvl, ha * MLA_SLOT)
    wv_cols = wkv_f[:, :, MLA_NOPE:].reshape(kvl, ha * MLA_V)
    e_slot = jnp.pad(jnp.eye(MLA_ROPE, dtype=BF16),
                     ((0, LANE - MLA_ROPE), (MLA_NOPE, MLA_SLOT - MLA_NOPE - MLA_ROPE)))
    e_rows = jnp.concatenate([jnp.tile(e_slot, (1, ha)), jnp.zeros((LANE, ha * MLA_V), BF16)], axis=1)
    wkv_ext = jnp.concatenate([jnp.concatenate([wk_slots, wv_cols], axis=1), e_rows], axis=0)
    w_bra = cols_full(gathered["w_br_a"])
    w_brb = cols_full(gathered["w_br_b"])
    w_out_f = gathered["w_out"].reshape(d, d)

    kv_a = _mm(kin, wkv_ext, "nn", BF16, "kv_up", tm=1152, tn=1024)
    qa_raw = _mm(cqn, wq_ext, "nn", F32, "q_up", tm=1024, tn=1024)
    q_a = _rope_a(qa_raw, q_tabs_a, False, BF16, "rope_q_fwd", sc_a)
    att_a = dict(hq=ha, hkv=ha, dk=MLA_SLOT, dv=MLA_V, k_blk0=0, v_blk0=ha * MLA_SLOT // MLA_V)
    att_b = dict(hq=GQA_HEADS, hkv=GQA_KV_HEADS, dk=GQA_HEAD_DIM, dv=GQA_HEAD_DIM, k_blk0=0, v_blk0=0)
    o_a, lse_a = _attention_fwd(q_a, kv_a, kv_a, name="attn_a_fwd", **att_a)
    o_b, lse_b = _attention_fwd(q_b, k_b, v_b, name="attn_b_fwd", **att_b)
    _after(gather_relay(g2, o_b))
    pa = _mm(o_a, w_bra, "nn", F32, "br_a", tm=1024, tn=1024)
    pb = _mm(o_b, w_brb, "nn", F32, "br_b", tm=1024, tn=1024)
    merged = _merge_fwd(pa, pb, qg, gate_blk)
    attn = _mm(merged, w_out_f, "nn", F32, "w_out", tm=1024, tn=1024)
    x1, z2 = _resid_norm_mod(xs, attn, norm2_g, mods2, "resid_norm2_fwd")
    w_up3 = gather_finish(g2, z2)["w_up"]
    _after(gather_relay(g3, z2))
    u = _mm_up_fwd(z2, w_up3, "w_up")
    w_down_f = gather_finish(g3, u)["w_down"].reshape(ff, d)
    h = _conv_fwd(u, conv_w_f, conv_b)
    ffn = _mm(h, w_down_f, "nn", F32, "w_down", tm=1024, tn=1024, tk=2816)

    def to_shards(g):
        return jnp.transpose(g.reshape(g.shape[0], N_DEV, g.shape[1] // N_DEV), (1, 0, 2))

    def reduce_start(tag, names, sends):
        n = len(sends)
        land = [lax.empty((4,) + s.shape[1:], s.dtype) for s in sends]
        s, r, arrs, tok = _split_start("reduce_d2d_start_" + tag, sends + land, _reduce_d2d_copies(n), 4 * n)
        return dict(tag=tag, names=names, s=s, r=r, arrs=arrs, tok=tok)

    def reduce_relay(g, after):
        n = len(g["names"])
        arrs = _split_wait("reduce_d2d_wait_" + g["tag"], g["s"], g["r"], g["arrs"], _reduce_d2d_copies(n), after)
        sums = [_pair_sum(arrs[a], arrs[n + a], c_idx, "pair_sum_" + g["names"][a]) for a in range(n)]
        land = [lax.empty(s.shape, s.dtype) for s in sums]
        s, r, arrs2, tok = _split_start("reduce_ici_start_" + g["tag"], sums + land, _reduce_ici_copies(n), 4 * n)
        g.update(s2=s, r2=r, arrs2=arrs2)
        return tok

    def reduce_finish(g, after):
        n = len(g["names"])
        arrs2 = _split_wait("reduce_ici_wait_" + g["tag"], g["s2"], g["r2"], g["arrs2"], _reduce_ici_copies(n), after)
        return dict(zip(g["names"], arrs2[n:]))

    dx2, dffn, st_fin = _final_loss(x1, ffn, final_norm_g[None, :], mods3, tgt)
    loss = lax.psum(st_fin[3, 0], MESH_AXES)
    dh = _mm(dffn, w_down_f, "nt", F32, "d_h", tm=1024, tn=1024)
    g_w_down = _mm(h, dffn, "tn", BF16, "g_w_down", tm=512, tn=1024)
    r_down = reduce_start("down", ["w_down"], [g_w_down.reshape(N_DEV, ff // N_DEV, d)])
    _after(r_down["tok"])
    du3, dcw, dcb = _conv_bwd(u, conv_w_f, conv_b, dh)
    dz2 = _mm_up_dz(du3, w_up3, "d_z2")
    g_w_up = _mm_up_gw(z2, du3, N_DEV, "g_w_up")
    g_conv_w = jnp.concatenate([dcw[0], dcw[1]], axis=1)
    tok = reduce_relay(r_down, g_w_up)
    _after(tok)
    r_up = reduce_start("up", ["w_up", "conv_w"], [g_w_up, to_shards(jnp.pad(g_conv_w, ((0, 5), (0, 0))))])
    _after(tok, r_up["tok"])
    dx1, dattn, st_n2 = _norm2_bwd(x1, attn, norm2_g, mods2b, dz2, dx2)
    dmerged = _mm(dattn, w_out_f, "nt", F32, "d_merged", tm=1024, tn=1024)
    g_w_out = _mm(merged, dattn, "tn", BF16, "g_w_out", tm=1024, tn=1024)
    dpa, dpb, dgates = _merge_bwd(dmerged, pa, pb, qg, gate_blk)
    do_a = _mm(dpa, w_bra, "nt", BF16, "d_o_a", tm=1024, tn=1024)
    do_b = _mm(dpb, w_brb, "nt", BF16, "d_o_b", tm=1024, tn=1024)
    g_w_bra = _mm(o_a, dpa, "tn", BF16, "g_w_br_a", tm=1024, tn=1024)
    g_w_brb = _mm(o_b, dpb, "tn", BF16, "g_w_br_b", tm=1024, tn=1024)
    tok = reduce_relay(r_up, g_w_brb)
    _after(tok)
    r_out = reduce_start("out", ["w_out", "w_br_a", "w_br_b"],
                         [g_w_out.reshape(N_DEV, d // N_DEV, d), to_shards(g_w_bra), to_shards(g_w_brb)])
    _after(tok, r_out["tok"])
    dq_a, dk_a, dv_a = _attention_bwd(q_a, kv_a, kv_a, do_a, lse_a, name="attn_a_bwd", **att_a)
    dq_b, dk_b, dv_b = _attention_bwd(q_b, k_b, v_b, do_b, lse_b, name="attn_b_bwd", **att_b)
    _after(reduce_relay(r_out, dv_b))
    dqa_raw = _rope_a(dq_a, q_tabs_a, True, BF16, "rope_q_bwd", sc_a)
    dcqn = _mm(dqa_raw, wq_ext, "nt", F32, "d_cqn", tm=1024, tn=ql)
    g_wq_ext = _mm(cqn, dqa_raw, "tn", BF16, "g_w_q_up", tm=ql, tn=1024)
    dq_p, st_q, st_qb = _q_prep_bwd(qg, mla_q_norm_g, gqa_q_norm_g, q_tabs_b, dcqn, dq_b, q_pad, sc_b)
    dkin = _mm_cat_nt([(dk_a, wkv_ext, 0), (dv_a, wkv_ext, ha * MLA_SLOT)], F32, "d_kin", tm=1152, tn=kvl + LANE)
    g_wkv_ext = _mm_cat_tn(kin, [dk_a, dv_a], BF16, "g_w_kv_up", tm=kvl + LANE, tn=min(1024, ha * MLA_V))
    dkv_p, st_kv, st_kb = _key_prep_bwd(kv_all, mla_kv_norm_g, gqa_k_norm_g, k_tabs, dkin, dk_b, dv_b)
    g_wq = g_wq_ext.reshape(ql, ha, MLA_SLOT)[:, :, :MLA_NOPE + MLA_ROPE].reshape(ql, ha * (MLA_NOPE + MLA_ROPE))
    g_wkv = jnp.concatenate([g_wkv_ext[:kvl, :ha * MLA_SLOT].reshape(kvl, ha, MLA_SLOT)[:, :, :MLA_NOPE],
                             g_wkv_ext[:kvl, ha * MLA_SLOT:].reshape(kvl, ha, MLA_V)], axis=2).reshape(kvl, ha * (MLA_NOPE + MLA_V))
    r_qkv = reduce_start("qkv", ["w_q_up", "w_kv_up"], [to_shards(g_wq), to_shards(g_wkv)])
    _after(r_qkv["tok"])
    g_wkv_p = _mm(z_all, dkv_p, "tn", BF16, "g_w_in_kv", tm=1024, tn=wkv_w)
    g_wqg_p = _mm_cat_tn(z_all, [dq_p, dgates], BF16, "g_w_in_qg", tm=1024, tn=min(1024, d), rows=t)
    g_w_in = jnp.concatenate([g_wkv_p[:, :kvl], g_wkv_p[:, kvl + 2 * nb:kvl + 2 * nb + MLA_ROPE],
                              g_wkv_p[:, kvl:kvl + 2 * nb], g_wqg_p[:, :q_w], g_wqg_p[:, q_w + q_pad:]], axis=1)
    r_in = reduce_start("in", ["w_in"], [to_shards(g_w_in)])
    _after(r_in["tok"])
    dz_lat = _mm_cat_nt([(dq_p, w_qg_p, 0), (dgates, w_qg_p, q_w + q_pad), (dkv_p, w_kv_p, 0)], F32, "d_z_lat",
                        tm=512, tn=1024, tk=min(2048, d), rows=t)
    dz_ctx = _mm(dkv_p, w_kv_p, "nt", F32, "d_z_ctx", tm=min(ROW_BLOCK, tc), tn=1024, a_row_off=t)
    tok_q = reduce_relay(r_qkv, dz_ctx)
    _after(tok_q)
    grad_x, st_n1 = _norm1_bwd(cts, xs, norm1_g, mods1, dz_ctx, dz_lat, dx1)

    d_lat = jnp.concatenate([st_n1[0], st_n1[1], st_n2[3], st_n2[0], st_n2[1], st_fin[1]])
    d_cxt = jnp.concatenate([st_n1[3], st_n1[4], jnp.zeros((4 * d,), F32)])
    small = jnp.concatenate([d_lat, d_cxt, st_n1[2], st_q[0], st_kv[0], st_qb[0], st_kb[0], st_n2[2],
                             jnp.concatenate([dcb[0, 0], dcb[1, 0]]), st_fin[0]])
    n_small = small.shape[0]
    pad_small = (-n_small) % LANE
    (small_all,) = _all_gather([jnp.pad(small, (0, pad_small)).reshape(1, -1)], "gather_small")
    offs = {}
    o = 0
    for nm, ln in (("d_lat", 6 * d), ("d_cxt", 6 * d), ("norm1_g", d), ("mla_q_norm_g", ql), ("mla_kv_norm_g", kvl),
                   ("gqa_q_norm_g", GQA_HEAD_DIM), ("gqa_k_norm_g", GQA_HEAD_DIM), ("norm2_g", d), ("conv_b", f2),
                   ("final_norm_g", d)):
        offs[nm] = (o, ln)
        o += ln

    def part(nm):
        a, ln = offs[nm]
        return small_all[:, :, a:a + ln]

    d_lat_all = part("d_lat")[:, 0, :]
    d_cxt_sum = _sum_parts(part("d_cxt"))
    da16 = jnp.concatenate([d_lat_all, d_cxt_sum, jnp.zeros((7, 6 * d), F32)], axis=0)
    da16_shard = lax.dynamic_slice_in_dim(da16, my_idx * ncol, ncol, axis=1)
    cc_part = _cctx_partial(da16_shard, w_ada[0], c_ctx[None, :])
    (cc_all,) = _all_gather([cc_part], "gather_cctx")
    cc_parts = cc_all[:, 0:1, :]
    tok_i = reduce_relay(r_in, cc_all)

    res = {}
    _after(tok_q, tok_i)

    def upd(nm, parts, shape2):
        wv, mv, vv = (a.reshape(shape2) for a in (weights[nm], mom_m[nm], mom_v[nm]))
        outs = _adamw(parts, wv, mv, vv, "adamw_" + nm)
        res[nm] = [o_.reshape(weights[nm].shape) for o_ in outs]

    for nm in ("norm1_g", "mla_q_norm_g", "mla_kv_norm_g", "gqa_q_norm_g", "gqa_k_norm_g", "norm2_g", "conv_b",
               "final_norm_g"):
        upd(nm, part(nm), (1, offs[nm][1]))
    upd("c_ctx", cc_parts, (1, d))
    b_parts = jnp.concatenate([d_lat_all[:, None, :], d_cxt_sum[None]], axis=0)
    upd("b_ada", b_parts, (1, 6 * d))
    _after(tok_q, tok_i)
    outs = _adamw_ada(conds, da16_shard, w_ada[0], m_w_ada[0], v_w_ada[0])
    res["w_ada"] = [o_[None] for o_ in outs]
    last = outs[0]
    for grp in (r_down, r_up, r_out, r_qkv, r_in):
        recv = reduce_finish(grp, last)
        for nm in grp["names"]:
            parts = recv[nm][:, :3, :] if nm == "conv_w" else recv[nm]
            upd(nm, parts, weights[nm].shape[1:])
            last = res[nm][0]

    return (loss, grad_x[None], *[res[n][0] for n in order], *[res[n][1] for n in order],
            *[res[n][2] for n in order], *[res[n][3] for n in order])
```

```python
import functools

import jax
import jax.numpy as jnp
from jax import lax
from jax.experimental import pallas as pl
from jax.experimental.pallas import tpu as pltpu

F32 = jnp.float32
BF16 = jnp.bfloat16

GRID_W = 64
ROPE_THETA = 10000.0
NORM_EPS = 1e-6
MLA_HEADS = 8
MLA_Q_LORA = 768
MLA_KV_LORA = 512
MLA_NOPE = 128
MLA_ROPE = 64
MLA_V = 128
GQA_HEADS = 8
GQA_KV_HEADS = 2
GQA_HEAD_DIM = 128
ADAM_LR = 0.001
ADAM_B1 = 0.9
ADAM_B2 = 0.999
ADAM_EPS = 1e-08
ADAM_WD = 0.01
ADAM_STEP = 10

N_DEV = 8
MESH_AXES = ("x", "y", "c")
LANE = 128
MLA_SLOT = 2 * LANE
VMEM_LIMIT = 56 * 1024 * 1024
ROW_BLOCK = 256
ATT_Q_BLOCK = 512
ATT_Q_BLOCK_FWD = 512
LN2 = 0.6931471805599453
LOG2E = 1.4426950408889634
MESH_ID = pl.DeviceIdType.MESH


def _tile(n, pref, align=LANE):
    if n <= pref:
        return n
    best = None
    t = align
    while t <= pref:
        if n % t == 0:
            best = t
        t += align
    assert best is not None, (n, pref, align)
    return best


def _cparams(sem=None):
    return pltpu.CompilerParams(dimension_semantics=sem, vmem_limit_bytes=VMEM_LIMIT)


_ORDER_AFTER = []


def _after(*arrays):
    _ORDER_AFTER.extend(arrays)


def _pcall(body, *, in_specs, **kw):
    deps = tuple(_ORDER_AFTER)
    _ORDER_AFTER.clear()
    if not deps:
        return pl.pallas_call(body, in_specs=in_specs, **kw)
    n_in, n_dep = len(in_specs), len(deps)

    def with_deps(*refs):
        body(*refs[:n_in], *refs[n_in + n_dep:])

    call = pl.pallas_call(with_deps, in_specs=list(in_specs) + [pl.BlockSpec(memory_space=pl.ANY)] * n_dep, **kw)
    return lambda *args: call(*args, *deps)


def _all_gather(arrs, name):
    n = len(arrs)

    def body(*refs):
        ins = refs[:n]
        outs = refs[n:2 * n]
        send_sems, recv_sems, local_sems = refs[2 * n:]
        x, y, c = lax.axis_index("x"), lax.axis_index("y"), lax.axis_index("c")
        me, sibling = (x, y, c), (x, y, 1 - c)
        chips = [(1 - x, y), (x, 1 - y), (1 - x, 1 - y)]

        def rows(a, dev):
            px, py, pc = dev
            return outs[a].at[4 * px + 2 * py + pc]

        def copy(a, k, block, to, src=None):
            return pltpu.make_async_remote_copy(
                src_ref=rows(a, block) if src is None else src,
                dst_ref=rows(a, block),
                send_sem=send_sems.at[7 * a + k],
                recv_sem=recv_sems.at[7 * a + k],
                device_id=to,
                device_id_type=MESH_ID,
            )

        mine = [pltpu.make_async_copy(ins[a], rows(a, me), local_sems.at[a]) for a in range(n)]
        for cp in mine:
            cp.start()
        first = []
        for a in range(n):
            first.append(copy(a, 0, me, sibling, src=ins[a]))
            first += [copy(a, 1 + j, me, (*chip, c), src=ins[a]) for j, chip in enumerate(chips)]
        for cp in first:
            cp.start()
        passed = []
        for j, chip in enumerate(chips):
            for a in range(n):
                copy(a, 1 + j, (*chip, c), me).wait_recv()
                fwd = copy(a, 4 + j, (*chip, c), sibling)
                fwd.start()
                passed.append(fwd)
        for a in range(n):
            copy(a, 0, sibling, me).wait_recv()
            for j, chip in enumerate(chips):
                copy(a, 4 + j, (*chip, 1 - c), me).wait_recv()
        for cp in first + passed:
            cp.wait_send()
        for cp in mine:
            cp.wait()

    any_spec = pl.BlockSpec(memory_space=pl.ANY)
    outs = _pcall(
        body,
        name=name,
        out_shape=[jax.ShapeDtypeStruct((N_DEV,) + a.shape, a.dtype) for a in arrs],
        in_specs=[any_spec] * n,
        out_specs=[any_spec] * n,
        scratch_shapes=[
            pltpu.SemaphoreType.DMA((7 * n,)),
            pltpu.SemaphoreType.DMA((7 * n,)),
            pltpu.SemaphoreType.DMA((n,)),
        ],
    )(*arrs)
    return list(outs)


def _all_to_all(arrs, name):
    n = len(arrs)

    def body(*refs):
        ins = refs[:n]
        outs = refs[n:2 * n]
        send_sems, recv_sems, local_sems = refs[2 * n:]
        x, y, c = lax.axis_index("x"), lax.axis_index("y"), lax.axis_index("c")
        my_idx = 4 * x + 2 * y + c

        def peer(k):
            fx, fy, fc = (k >> 2) & 1, (k >> 1) & 1, k & 1
            return (x ^ fx if fx else x, y ^ fy if fy else y, c ^ fc if fc else c)

        def copy(a, k):
            px, py, pc = peer(k)
            return pltpu.make_async_remote_copy(
                src_ref=ins[a].at[4 * px + 2 * py + pc],
                dst_ref=outs[a].at[my_idx],
                send_sem=send_sems.at[7 * a + k - 1],
                recv_sem=recv_sems.at[7 * a + k - 1],
                device_id=(px, py, pc),
                device_id_type=MESH_ID,
            )

        mine = [pltpu.make_async_copy(ins[a].at[my_idx], outs[a].at[my_idx], local_sems.at[a]) for a in range(n)]
        for cp in mine:
            cp.start()
        order = [1, 4, 2, 5, 3, 6, 7]
        cps = [copy(a, k) for k in order for a in range(n)]
        for cp in cps:
            cp.start()
        for cp in cps:
            cp.wait()
        for cp in mine:
            cp.wait()

    any_spec = pl.BlockSpec(memory_space=pl.ANY)
    outs = _pcall(
        body,
        name=name,
        out_shape=[jax.ShapeDtypeStruct(a.shape, a.dtype) for a in arrs],
        in_specs=[any_spec] * n,
        out_specs=[any_spec] * n,
        scratch_shapes=[
            pltpu.SemaphoreType.DMA((7 * n,)),
            pltpu.SemaphoreType.DMA((7 * n,)),
            pltpu.SemaphoreType.DMA((n,)),
        ],
    )(*arrs)
    return list(outs)


_HBM = pl.BlockSpec(memory_space=pltpu.HBM)
_SEM = pl.BlockSpec(memory_space=pltpu.SEMAPHORE)
_EFFECT = pltpu.SideEffectType.DATAFLOW_SIDE_EFFECTING


def _descriptors(copies, send_sems, recv_sems):
    descs = []
    for i, (src, dst, dev) in enumerate(copies):
        if dev is None:
            descs.append(pltpu.make_async_copy(src, dst, recv_sems.at[i]))
        else:
            descs.append(pltpu.make_async_remote_copy(src_ref=src, dst_ref=dst, send_sem=send_sems.at[i],
                                                      recv_sem=recv_sems.at[i], device_id=dev, device_id_type=MESH_ID))
    return descs


def _split_start(name, arrays, copies_fn, n_copies):
    n = len(arrays)

    def body(*refs):
        send_sems, recv_sems = refs[n], refs[n + 1]
        token = refs[2 * n + 2]
        for dsc in _descriptors(copies_fn(refs[:n]), send_sems, recv_sems):
            dsc.start()
        token[...] = jnp.zeros_like(token)

    outs = _pcall(
        body,
        name=name,
        out_shape=(pltpu.SemaphoreType.DMA((n_copies,)), pltpu.SemaphoreType.DMA((n_copies,)),
                   *[pltpu.HBM(a.shape, a.dtype) for a in arrays], jax.ShapeDtypeStruct((8, LANE), F32)),
        in_specs=[_HBM] * n,
        out_specs=(_SEM, _SEM, *[_HBM] * n, pl.BlockSpec(memory_space=pltpu.VMEM)),
        input_output_aliases={i: 2 + i for i in range(n)},
        compiler_params=pltpu.CompilerParams(has_side_effects=_EFFECT),
    )(*[pltpu.with_memory_space_constraint(a, pltpu.HBM) for a in arrays])
    return outs[0], outs[1], list(outs[2:2 + n]), outs[2 + n]


def _split_wait(name, send_sems, recv_sems, arrays, copies_fn, after):
    n = len(arrays)

    def body(*refs):
        for dsc, (_, _, dev) in zip(_descriptors(copies_fn(refs[:n]), refs[n], refs[n + 1]), copies_fn(refs[:n])):
            if dev is None:
                dsc.wait()
            else:
                dsc.wait_send()
                dsc.wait_recv()

    outs = _pcall(
        body,
        name=name,
        out_shape=tuple(pltpu.HBM(a.shape, a.dtype) for a in arrays),
        in_specs=[_HBM] * n + [_SEM, _SEM, pl.BlockSpec(memory_space=pl.ANY)],
        out_specs=tuple([_HBM] * n),
        input_output_aliases={i: i for i in range(n)},
        compiler_params=pltpu.CompilerParams(has_side_effects=_EFFECT),
    )(*arrays, send_sems, recv_sems, after)
    return list(outs)


def _mesh_pos():
    x, y, c = lax.axis_index("x"), lax.axis_index("y"), lax.axis_index("c")
    return x, y, c, [(1 - x, y), (x, 1 - y), (1 - x, 1 - y)]


def _gather_ici_copies(n):
    def copies(refs):
        x, y, c, chips = _mesh_pos()
        me = 4 * x + 2 * y + c
        out = []
        for a in range(n):
            src, buf = refs[a], refs[n + a]
            out.append((src, buf.at[me], None))
            out.append((src, buf.at[me], (x, y, 1 - c)))
            out += [(src, buf.at[me], (cx, cy, c)) for cx, cy in chips]
        return out
    return copies


def _gather_d2d_copies(n):
    def copies(refs):
        x, y, c, chips = _mesh_pos()
        out = []
        for a in range(n):
            for cx, cy in chips:
                rows = refs[a].at[4 * cx + 2 * cy + c]
                out.append((rows, rows, (x, y, 1 - c)))
        return out
    return copies


def _reduce_d2d_copies(n):
    def copies(refs):
        x, y, c, _ = _mesh_pos()
        out = []
        for a in range(n):
            for k in range(4):
                out.append((refs[a].at[2 * k + (1 - c)], refs[n + a].at[k], (x, y, 1 - c)))
        return out
    return copies


def _reduce_ici_copies(n):
    def copies(refs):
        x, y, c, chips = _mesh_pos()
        mine = 2 * x + y
        out = []
        for a in range(n):
            src, land = refs[a], refs[n + a]
            out.append((src.at[mine], land.at[mine], None))
            out += [(src.at[2 * cx + cy], land.at[mine], (cx, cy, c)) for cx, cy in chips]
        return out
    return copies


def _pair_sum(send, land, c_idx, name):
    _, r, cols = send.shape
    rb = _tile(r, max(8, (1 << 20) // (2 * cols) // 8 * 8), 8)
    dt = send.dtype

    def body(c_ref, s_ref, l_ref, o_ref):
        o_ref[...] = (s_ref[...].astype(F32) + l_ref[...].astype(F32)).astype(dt)

    return pl.pallas_call(
        body,
        name=name,
        out_shape=jax.ShapeDtypeStruct((4, r, cols), dt),
        grid_spec=pltpu.PrefetchScalarGridSpec(
            num_scalar_prefetch=1,
            grid=(4, r // rb),
            in_specs=[pl.BlockSpec((None, rb, cols), lambda k, i, c_ref: (2 * k + c_ref[0], i, 0)),
                      pl.BlockSpec((None, rb, cols), lambda k, i, c_ref: (k, i, 0))],
            out_specs=pl.BlockSpec((None, rb, cols), lambda k, i, c_ref: (k, i, 0)),
        ),
        compiler_params=_cparams(("parallel", "parallel")),
    )(c_idx, send, land)


_DIMS = {
    "nn": (((1,), (0,)), ((), ())),
    "nt": (((1,), (1,)), ((), ())),
    "tn": (((0,), (0,)), ((), ())),
}


def _mm_call(a, b, *, mode, grid, a_spec, b_spec, o_spec, out_shape, acc_shape, name):
    nk = grid[2]
    out_dtype = out_shape.dtype

    def body(a_ref, b_ref, o_ref, *scratch):
        p = lax.dot_general(a_ref[...].astype(BF16), b_ref[...].astype(BF16), _DIMS[mode],
                            preferred_element_type=F32)
        if nk == 1:
            o_ref[...] = p.astype(out_dtype)
        else:
            acc = scratch[0]
            k = pl.program_id(2)

            @pl.when(k == 0)
            def _():
                acc[...] = p

            @pl.when(k > 0)
            def _():
                acc[...] += p

            @pl.when(k == nk - 1)
            def _():
                o_ref[...] = acc[...].astype(out_dtype)

    return _pcall(
        body,
        name=name,
        out_shape=out_shape,
        grid=grid,
        in_specs=[a_spec, b_spec],
        out_specs=o_spec,
        scratch_shapes=[pltpu.VMEM(acc_shape, F32)] if nk > 1 else [],
        compiler_params=_cparams(("parallel", "parallel", "arbitrary")),
    )(a, b)


def _mm(a, b, mode, out_dtype, name, tm=512, tn=512, tk=2432, a_row_off=0, rows=None):
    if mode == "nn":
        (m, k), (k2, n) = a.shape, b.shape
    elif mode == "nt":
        (m, k), (n, k2) = a.shape, b.shape
    else:
        (k, m), (k2, n) = a.shape, b.shape
        if rows is not None:
            k = k2 = rows
    assert k == k2, (a.shape, b.shape, mode)
    if mode != "tn":
        m = (m if rows is None else rows + a_row_off) - a_row_off
    tm, tn, tk = _tile(m, tm, 8), _tile(n, tn), _tile(k, tk, 8 if mode == "tn" else LANE)
    assert a_row_off % tm == 0
    ro = a_row_off // tm
    grid = (m // tm, n // tn, k // tk)
    if mode == "tn":
        a_spec = pl.BlockSpec((tk, tm), lambda i, j, kk: (kk, i))
    else:
        a_spec = pl.BlockSpec((tm, tk), lambda i, j, kk: (i + ro, kk))
    if mode == "nt":
        b_spec = pl.BlockSpec((tn, tk), lambda i, j, kk: (j, kk))
    else:
        b_spec = pl.BlockSpec((tk, tn), lambda i, j, kk: (kk, j))
    o_spec = pl.BlockSpec((tm, tn), lambda i, j, kk: (i, j))
    return _mm_call(a, b, mode=mode, grid=grid, a_spec=a_spec, b_spec=b_spec, o_spec=o_spec,
                    out_shape=jax.ShapeDtypeStruct((m, n), out_dtype), acc_shape=(tm, tn), name=name)


def _mm_cat_nt(pieces, out_dtype, name, tm=1024, tn=1024, tk=2048, rows=None):
    m = pieces[0][0].shape[0] if rows is None else rows
    n = pieces[0][1].shape[0]
    tm, tn = _tile(m, tm, 8), _tile(n, tn)
    steps, starts, s = [], [], 0
    for a, b, off in pieces:
        kp = a.shape[1]
        tkp = _tile(kp, tk)
        assert off % tkp == 0 and b.shape[0] == n
        steps.append((tkp, kp // tkp, off // tkp))
        starts.append(s)
        s += kp // tkp
    nk = s
    npc = len(pieces)

    def body(*refs):
        o_ref, acc = refs[2 * npc], refs[2 * npc + 1]
        kk = pl.program_id(2)

        @pl.when(kk == 0)
        def _():
            acc[...] = jnp.zeros_like(acc)

        for p in range(npc):
            @pl.when((kk >= starts[p]) & (kk < starts[p] + steps[p][1]))
            def _(p=p):
                acc[...] += lax.dot_general(refs[2 * p][...].astype(BF16), refs[2 * p + 1][...].astype(BF16), _DIMS["nt"],
                                            preferred_element_type=F32)

        @pl.when(kk == nk - 1)
        def _():
            o_ref[...] = acc[...].astype(out_dtype)

    in_specs, args = [], []
    for p, (a, b, off) in enumerate(pieces):
        tkp, np_, ob = steps[p]

        def rel(kk, p=p, np_=np_):
            return jnp.clip(kk - starts[p], 0, np_ - 1)

        in_specs.append(pl.BlockSpec((tm, tkp), lambda i, j, kk, rel=rel: (i, rel(kk))))
        in_specs.append(pl.BlockSpec((tn, tkp), lambda i, j, kk, rel=rel, ob=ob: (j, ob + rel(kk))))
        args += [a, b]
    return _pcall(
        body,
        name=name,
        out_shape=jax.ShapeDtypeStruct((m, n), out_dtype),
        grid=(m // tm, n // tn, nk),
        in_specs=in_specs,
        out_specs=pl.BlockSpec((tm, tn), lambda i, j, kk: (i, j)),
        scratch_shapes=[pltpu.VMEM((tm, tn), F32)],
        compiler_params=_cparams(("parallel", "parallel", "arbitrary")),
    )(*args)


def _mm_cat_tn(a, pieces, out_dtype, name, tm=1024, tn=1024, rows=None):
    k = a.shape[0] if rows is None else rows
    m = a.shape[1]
    tm = _tile(m, tm)
    starts, s = [], 0
    for b in pieces:
        assert b.shape[1] % tn == 0
        starts.append(s)
        s += b.shape[1] // tn
    nj = s
    npc = len(pieces)

    def body(*refs):
        a_ref, o_ref = refs[0], refs[1 + npc]
        j = pl.program_id(1)
        for p in range(npc):
            @pl.when((j >= starts[p]) & (j < starts[p] + pieces[p].shape[1] // tn))
            def _(p=p):
                o_ref[...] = lax.dot_general(a_ref[...].astype(BF16), refs[1 + p][...].astype(BF16), _DIMS["tn"],
                                             preferred_element_type=F32).astype(out_dtype)

    in_specs = [pl.BlockSpec((k, tm), lambda i, j: (0, i))]
    for p, b in enumerate(pieces):
        np_ = b.shape[1] // tn
        in_specs.append(pl.BlockSpec((k, tn), lambda i, j, p=p, np_=np_: (0, jnp.clip(j - starts[p], 0, np_ - 1))))
    return _pcall(
        body,
        name=name,
        out_shape=jax.ShapeDtypeStruct((m, nj * tn), out_dtype),
        grid=(m // tm, nj),
        in_specs=in_specs,
        out_specs=pl.BlockSpec((tm, tn), lambda i, j: (i, j)),
        compiler_params=_cparams(("parallel", "arbitrary")),
    )(a, *pieces)


def _mm_up_fwd(z2, w3, name, tm=1024):
    t, d = z2.shape
    nsh, _, c = w3.shape
    tm = _tile(t, tm, 8)
    return _mm_call(z2, w3, mode="nn", grid=(t // tm, nsh, 1),
                    a_spec=pl.BlockSpec((tm, d), lambda i, j, kk: (i, 0)),
                    b_spec=pl.BlockSpec((None, d, c), lambda i, j, kk: (j, 0, 0)),
                    o_spec=pl.BlockSpec((tm, c), lambda i, j, kk: (i, j)),
                    out_shape=jax.ShapeDtypeStruct((t, nsh * c), F32), acc_shape=(tm, c), name=name)


def _mm_up_dz(du3, w3, name, tm=512, tn=1024):
    _, t, f = du3.shape
    nsh, d, c = w3.shape
    half = nsh // 2
    assert f == half * c
    tm, tn = _tile(t, tm, 8), _tile(d, tn)

    def body(a_ref, b_ref, o_ref, acc):
        kk = pl.program_id(2)
        p = None
        for s in range(half):
            q = lax.dot_general(a_ref[:, s * c:(s + 1) * c], b_ref[s], _DIMS["nt"], preferred_element_type=F32)
            p = q if p is None else p + q

        @pl.when(kk == 0)
        def _():
            acc[...] = p

        @pl.when(kk == 1)
        def _():
            o_ref[...] = acc[...] + p

    return _pcall(
        body,
        name=name,
        out_shape=jax.ShapeDtypeStruct((t, d), F32),
        grid=(t // tm, d // tn, 2),
        in_specs=[pl.BlockSpec((None, tm, f), lambda i, j, kk: (kk, i, 0)),
                  pl.BlockSpec((half, tn, c), lambda i, j, kk: (kk, j, 0))],
        out_specs=pl.BlockSpec((tm, tn), lambda i, j, kk: (i, j)),
        scratch_shapes=[pltpu.VMEM((tm, tn), F32)],
        compiler_params=_cparams(("parallel", "parallel", "arbitrary")),
    )(du3, w3)


def _mm_sum_nt(pieces, out_dtype, name, tm=512, tn=512, rows=None):
    m = pieces[0][0].shape[0] if rows is None else rows
    n = pieces[0][2].shape[0]
    tm, tn = _tile(m, tm, 8), _tile(n, tn)
    npc = len(pieces)

    def body(*refs):
        p = None
        for s in range(npc):
            q = lax.dot_general(refs[2 * s][...].astype(BF16), refs[2 * s + 1][...].astype(BF16), _DIMS["nt"],
                                preferred_element_type=F32)
            p = q if p is None else p + q
        refs[2 * npc][...] = p.astype(out_dtype)

    in_specs, args = [], []
    for a, ao, b, bo, kp in pieces:
        assert ao % kp == 0 and bo % kp == 0 and b.shape[0] == n
        in_specs.append(pl.BlockSpec((tm, kp), lambda i, j, ab=ao // kp: (i, ab)))
        in_specs.append(pl.BlockSpec((tn, kp), lambda i, j, bb=bo // kp: (j, bb)))
        args += [a, b]
    return _pcall(
        body,
        name=name,
        out_shape=jax.ShapeDtypeStruct((m, n), out_dtype),
        grid=(m // tm, n // tn),
        in_specs=in_specs,
        out_specs=pl.BlockSpec((tm, tn), lambda i, j: (i, j)),
        compiler_params=_cparams(("parallel", "parallel")),
    )(*args)


def _mm_up_gw(z2, du3, nsh, name, tm=1024):
    t, d = z2.shape
    f = du3.shape[2]
    half = nsh // 2
    c = f // half
    tm = _tile(d, tm)
    return _mm_call(z2, du3, mode="tn", grid=(d // tm, nsh, 1),
                    a_spec=pl.BlockSpec((t, tm), lambda i, j, kk: (0, i)),
                    b_spec=pl.BlockSpec((None, t, c), lambda i, j, kk: (j // half, 0, j % half)),
                    o_spec=pl.BlockSpec((None, tm, c), lambda i, j, kk: (j, i, 0)),
                    out_shape=jax.ShapeDtypeStruct((nsh, d, c), BF16), acc_shape=(tm, c), name=name)


def _rms(x):
    r = lax.rsqrt(jnp.mean(x * x, axis=-1, keepdims=True) + NORM_EPS)
    return x * r, r


def _rms_bwd(dxh, xh, r):
    return r * (dxh - xh * jnp.mean(dxh * xh, axis=-1, keepdims=True))


def _colsum(v):
    return jnp.sum(v, axis=0, keepdims=True)


def _rope(v, c, s1, s2, q):
    w = v.shape[-1]
    return v * c + pltpu.roll(v, w - q, 1) * s1 + pltpu.roll(v, q, 1) * s2


def _rope_t(d, c, s1, s2, q):
    w = d.shape[-1]
    return d * c + pltpu.roll(d * s1, q, 1) + pltpu.roll(d * s2, w - q, 1)


def _norm_mod_fwd(ctx, x, gain, mods):
    tc, d = ctx.shape
    t = x.shape[0]
    rb = min(ROW_BLOCK, tc)
    nbl = t // rb

    def body(ctx_ref, x_ref, g_ref, mod_ref, z_ref):
        i = pl.program_id(0)

        def emit(src, sh, sc):
            xh, _ = _rms(src[...])
            z_ref[...] = ((xh * g_ref[...]) * (1.0 + sc) + sh).astype(BF16)

        @pl.when(i >= nbl)
        def _():
            emit(ctx_ref, mod_ref[2:3, :], mod_ref[3:4, :])

        @pl.when(i < nbl)
        def _():
            emit(x_ref, mod_ref[0:1, :], mod_ref[1:2, :])

    return _pcall(
        body,
        name="norm1_mod_fwd",
        out_shape=jax.ShapeDtypeStruct((tc + t, d), BF16),
        grid=((tc + t) // rb,),
        in_specs=[
            pl.BlockSpec((rb, d), lambda i: (jnp.maximum(i - nbl, 0), 0)),
            pl.BlockSpec((rb, d), lambda i: (jnp.minimum(i, nbl - 1), 0)),
            pl.BlockSpec((1, d), lambda i: (0, 0)),
            pl.BlockSpec((8, d), lambda i: (0, 0)),
        ],
        out_specs=pl.BlockSpec((rb, d), lambda i: (i, 0)),
        compiler_params=_cparams(("arbitrary",)),
    )(ctx, x, gain, mods)


def _norm1_bwd(ctx, x, gain, mods, dz_ctx, dz_lat, dx1):
    tc, d = ctx.shape
    t = x.shape[0]
    rb = min(ROW_BLOCK, tc)
    nbl = t // rb

    def body(ctx_ref, x_ref, g_ref, mod_ref, dzc_ref, dzl_ref, dx1_ref, gx_ref, st_ref):
        i = pl.program_id(0)

        @pl.when(i == 0)
        def _():
            st_ref[...] = jnp.zeros_like(st_ref)

        def common(src, dz, sc, row_sh, row_sc):
            xh, r = _rms(src[...])
            g = g_ref[...]
            dxn = dz * (1.0 + sc)
            st_ref[row_sh:row_sh + 1, :] += _colsum(dz)
            st_ref[row_sc:row_sc + 1, :] += _colsum(dz * (xh * g))
            st_ref[2:3, :] += _colsum(dxn * xh)
            return _rms_bwd(dxn * g, xh, r)

        @pl.when(i >= nbl)
        def _():
            common(ctx_ref, dzc_ref[...], mod_ref[3:4, :], 3, 4)

        @pl.when(i < nbl)
        def _():
            gx_ref[...] = dx1_ref[...] + common(x_ref, dzl_ref[...], mod_ref[1:2, :], 0, 1)

    lat = lambda i: (jnp.minimum(i, nbl - 1), 0)
    cix = lambda i: (jnp.maximum(i - nbl, 0), 0)
    return _pcall(
        body,
        name="norm1_mod_bwd",
        out_shape=[jax.ShapeDtypeStruct((t, d), F32), jax.ShapeDtypeStruct((8, d), F32)],
        grid=((tc + t) // rb,),
        in_specs=[
            pl.BlockSpec((rb, d), cix),
            pl.BlockSpec((rb, d), lat),
            pl.BlockSpec((1, d), lambda i: (0, 0)),
            pl.BlockSpec((8, d), lambda i: (0, 0)),
            pl.BlockSpec((rb, d), cix),
            pl.BlockSpec((rb, d), lat),
            pl.BlockSpec((rb, d), lat),
        ],
        out_specs=[pl.BlockSpec((rb, d), lat), pl.BlockSpec((8, d), lambda i: (0, 0))],
        compiler_params=_cparams(("arbitrary",)),
    )(ctx, x, gain, mods, dz_ctx, dz_lat, dx1)


def _key_prep_fwd(kv, kv_gain, kb_gain, tabs):
    ta, wkv = kv.shape
    kvl = MLA_KV_LORA
    nb = GQA_KV_HEADS * GQA_HEAD_DIM
    rb = ROW_BLOCK if ta % ROW_BLOCK == 0 else LANE
    hd = GQA_HEAD_DIM

    def body(kv_ref, g_ref, gb_ref, ca, s1a, s2a, cb, s1b, s2b, kin_ref, kb_ref, vb_ref):
        xh, _ = _rms(kv_ref[:, 0:kvl])
        kin_ref[:, 0:kvl] = (xh * g_ref[...]).astype(BF16)
        kpe = kv_ref[:, kvl + 2 * nb:kvl + 2 * nb + LANE]
        kin_ref[:, kvl:kvl + LANE] = _rope(kpe, ca[...], s1a[...], s2a[...], MLA_ROPE // 4).astype(BF16)
        for h in range(GQA_KV_HEADS):
            nh, _ = _rms(kv_ref[:, kvl + h * hd:kvl + (h + 1) * hd])
            kb_ref[:, h * hd:(h + 1) * hd] = _rope(nh * gb_ref[...], cb[...], s1b[...], s2b[...], hd // 4).astype(BF16)
        vb_ref[...] = kv_ref[:, kvl + nb:kvl + 2 * nb].astype(BF16)

    row = lambda w: pl.BlockSpec((rb, w), lambda i: (i, 0))
    fix = lambda w: pl.BlockSpec((1, w), lambda i: (0, 0))
    return _pcall(
        body,
        name="key_prep_fwd",
        out_shape=[jax.ShapeDtypeStruct((ta, kvl + LANE), BF16), jax.ShapeDtypeStruct((ta, nb), BF16),
                   jax.ShapeDtypeStruct((ta, nb), BF16)],
        grid=(ta // rb,),
        in_specs=[row(wkv), fix(kvl), fix(hd)] + [row(LANE)] * 3 + [row(hd)] * 3,
        out_specs=[row(kvl + LANE), row(nb), row(nb)],
        compiler_params=_cparams(("parallel",)),
    )(kv, kv_gain, kb_gain, *tabs)


def _key_prep_bwd(kv, kv_gain, kb_gain, tabs, dkin, dkb, dvb):
    ta, wkv = kv.shape
    kvl = MLA_KV_LORA
    nb = GQA_KV_HEADS * GQA_HEAD_DIM
    rb = ROW_BLOCK if ta % ROW_BLOCK == 0 else LANE
    hd = GQA_HEAD_DIM

    def body(kv_ref, g_ref, gb_ref, ca, s1a, s2a, cb, s1b, s2b, dkin_ref, dkb_ref, dvb_ref, dkv_ref, st_ref, stb_ref):
        @pl.when(pl.program_id(0) == 0)
        def _():
            st_ref[...] = jnp.zeros_like(st_ref)
            stb_ref[...] = jnp.zeros_like(stb_ref)

        xh, r = _rms(kv_ref[:, 0:kvl])
        dn = dkin_ref[:, 0:kvl]
        st_ref[0:1, :] += _colsum(dn * xh)
        dkv_ref[:, 0:kvl] = _rms_bwd(dn * g_ref[...], xh, r).astype(BF16)
        dpe = _rope_t(dkin_ref[:, kvl:kvl + LANE], ca[...], s1a[...], s2a[...], MLA_ROPE // 4)
        dkv_ref[:, kvl + 2 * nb:kvl + 2 * nb + LANE] = dpe.astype(BF16)
        for h in range(GQA_KV_HEADS):
            nh, rh = _rms(kv_ref[:, kvl + h * hd:kvl + (h + 1) * hd])
            dn_h = _rope_t(dkb_ref[:, h * hd:(h + 1) * hd], cb[...], s1b[...], s2b[...], hd // 4)
            stb_ref[0:1, :] += _colsum(dn_h * nh)
            dkv_ref[:, kvl + h * hd:kvl + (h + 1) * hd] = _rms_bwd(dn_h * gb_ref[...], nh, rh).astype(BF16)
        dkv_ref[:, kvl + nb:kvl + 2 * nb] = dvb_ref[...].astype(BF16)

    row = lambda w: pl.BlockSpec((rb, w), lambda i: (i, 0))
    fix = lambda w: pl.BlockSpec((1, w), lambda i: (0, 0))
    return _pcall(
        body,
        name="key_prep_bwd",
        out_shape=[jax.ShapeDtypeStruct((ta, wkv), BF16), jax.ShapeDtypeStruct((8, kvl), F32),
                   jax.ShapeDtypeStruct((8, hd), F32)],
        grid=(ta // rb,),
        in_specs=[row(wkv), fix(kvl), fix(hd)] + [row(LANE)] * 3 + [row(hd)] * 3 + [row(kvl + LANE), row(nb), row(nb)],
        out_specs=[row(wkv), pl.BlockSpec((8, kvl), lambda i: (0, 0)), pl.BlockSpec((8, hd), lambda i: (0, 0))],
        compiler_params=_cparams(("arbitrary",)),
    )(kv, kv_gain, kb_gain, *tabs, dkin, dkb, dvb)


def _q_prep_fwd(qg, q_gain, qb_gain, tabs, qscale):
    t = qg.shape[0]
    ql = MLA_Q_LORA
    hd = GQA_HEAD_DIM
    hb = GQA_HEADS * hd
    rb = min(ROW_BLOCK, t)

    def body(q_ref, g_ref, gb_ref, cb, s1b, s2b, cqn_ref, qb_ref):
        xh, _ = _rms(q_ref[:, 0:ql])
        cqn_ref[...] = (xh * g_ref[...]).astype(BF16)
        for h in range(GQA_HEADS):
            nh, _ = _rms(q_ref[:, ql + h * hd:ql + (h + 1) * hd])
            qh = _rope(nh * gb_ref[...], cb[...], s1b[...], s2b[...], hd // 4)
            qb_ref[:, h * hd:(h + 1) * hd] = (qh * qscale).astype(BF16)

    row = lambda w: pl.BlockSpec((rb, w), lambda i: (i, 0))
    fix = lambda w: pl.BlockSpec((1, w), lambda i: (0, 0))
    return _pcall(
        body,
        name="q_prep_fwd",
        out_shape=[jax.ShapeDtypeStruct((t, ql), BF16), jax.ShapeDtypeStruct((t, hb), BF16)],
        grid=(t // rb,),
        in_specs=[row(ql + hb), fix(ql), fix(hd)] + [row(hd)] * 3,
        out_specs=[row(ql), row(hb)],
        compiler_params=_cparams(("parallel",)),
    )(qg, q_gain, qb_gain, *tabs)


def _q_prep_bwd(qg, q_gain, qb_gain, tabs, dcqn, dqb, wpad, qscale):
    t = qg.shape[0]
    ql = MLA_Q_LORA
    hd = GQA_HEAD_DIM
    hb = GQA_HEADS * hd
    rb = min(ROW_BLOCK, t)

    def body(q_ref, g_ref, gb_ref, cb, s1b, s2b, dcqn_ref, dqb_ref, dq_ref, st_ref, stb_ref):
        @pl.when(pl.program_id(0) == 0)
        def _():
            st_ref[...] = jnp.zeros_like(st_ref)
            stb_ref[...] = jnp.zeros_like(stb_ref)

        xh, r = _rms(q_ref[:, 0:ql])
        dn = dcqn_ref[...]
        st_ref[0:1, :] += _colsum(dn * xh)
        dq_ref[:, 0:ql] = _rms_bwd(dn * g_ref[...], xh, r).astype(BF16)
        for h in range(GQA_HEADS):
            nh, rh = _rms(q_ref[:, ql + h * hd:ql + (h + 1) * hd])
            dn_h = _rope_t(dqb_ref[:, h * hd:(h + 1) * hd] * qscale, cb[...], s1b[...], s2b[...], hd // 4)
            stb_ref[0:1, :] += _colsum(dn_h * nh)
            dq_ref[:, ql + h * hd:ql + (h + 1) * hd] = _rms_bwd(dn_h * gb_ref[...], nh, rh).astype(BF16)
        if wpad:
            dq_ref[:, ql + hb:ql + hb + wpad] = jnp.zeros((rb, wpad), BF16)

    row = lambda w: pl.BlockSpec((rb, w), lambda i: (i, 0))
    fix = lambda w: pl.BlockSpec((1, w), lambda i: (0, 0))
    return _pcall(
        body,
        name="q_prep_bwd",
        out_shape=[jax.ShapeDtypeStruct((t, ql + hb + wpad), BF16), jax.ShapeDtypeStruct((8, ql), F32),
                   jax.ShapeDtypeStruct((8, hd), F32)],
        grid=(t // rb,),
        in_specs=[row(ql + hb), fix(ql), fix(hd)] + [row(hd)] * 3 + [row(ql), row(hb)],
        out_specs=[row(ql + hb + wpad), pl.BlockSpec((8, ql), lambda i: (0, 0)), pl.BlockSpec((8, hd), lambda i: (0, 0))],
        compiler_params=_cparams(("arbitrary",)),
    )(qg, q_gain, qb_gain, *tabs, dcqn, dqb)


def _rope_a(v, tabs, transpose, out_dtype, name, qscale):
    t, w = v.shape
    rb = min(ROW_BLOCK, t)
    fn = _rope_t if transpose else _rope

    def body(v_ref, c, s1, s2, o_ref):
        for h in range(w // MLA_SLOT):
            sl = slice(h * MLA_SLOT, (h + 1) * MLA_SLOT)
            o_ref[:, sl] = (fn(v_ref[:, sl].astype(F32), c[...], s1[...], s2[...], MLA_ROPE // 4) * qscale).astype(out_dtype)

    row = lambda ww: pl.BlockSpec((rb, ww), lambda i: (i, 0))
    return _pcall(
        body,
        name=name,
        out_shape=jax.ShapeDtypeStruct((t, w), out_dtype),
        grid=(t // rb,),
        in_specs=[row(w)] + [row(MLA_SLOT)] * 3,
        out_specs=row(w),
        compiler_params=_cparams(("parallel",)),
    )(v, *tabs)


def _merge_fwd(pa, pb, qg, gate_blk):
    t, d = pa.shape
    rb = min(ROW_BLOCK, t)

    def body(pa_ref, pb_ref, ga_ref, gb_ref, o_ref):
        o_ref[...] = (jax.nn.sigmoid(ga_ref[...]) * pa_ref[...] + jax.nn.sigmoid(gb_ref[...]) * pb_ref[...]).astype(BF16)

    row = pl.BlockSpec((rb, d), lambda i: (i, 0))
    return _pcall(
        body,
        name="merge_fwd",
        out_shape=jax.ShapeDtypeStruct((t, d), BF16),
        grid=(t // rb,),
        in_specs=[row, row, pl.BlockSpec((rb, d), lambda i: (i, gate_blk)), pl.BlockSpec((rb, d), lambda i: (i, gate_blk + 1))],
        out_specs=row,
        compiler_params=_cparams(("parallel",)),
    )(pa, pb, qg, qg)


def _merge_bwd(dm, pa, pb, qg, gate_blk):
    t, d = pa.shape
    rb = min(ROW_BLOCK, t)

    def body(dm_ref, pa_ref, pb_ref, ga_ref, gb_ref, dpa_ref, dpb_ref, dg_ref):
        dmv = dm_ref[...]
        sa = jax.nn.sigmoid(ga_ref[...])
        sb = jax.nn.sigmoid(gb_ref[...])
        dpa_ref[...] = (dmv * sa).astype(BF16)
        dpb_ref[...] = (dmv * sb).astype(BF16)
        dg_ref[:, 0:d] = (dmv * pa_ref[...] * (sa * (1.0 - sa))).astype(BF16)
        dg_ref[:, d:2 * d] = (dmv * pb_ref[...] * (sb * (1.0 - sb))).astype(BF16)

    row = pl.BlockSpec((rb, d), lambda i: (i, 0))
    return _pcall(
        body,
        name="merge_bwd",
        out_shape=[jax.ShapeDtypeStruct((t, d), BF16), jax.ShapeDtypeStruct((t, d), BF16),
                   jax.ShapeDtypeStruct((t, 2 * d), BF16)],
        grid=(t // rb,),
        in_specs=[row, row, row, pl.BlockSpec((rb, d), lambda i: (i, gate_blk)), pl.BlockSpec((rb, d), lambda i: (i, gate_blk + 1))],
        out_specs=[row, row, pl.BlockSpec((rb, 2 * d), lambda i: (i, 0))],
        compiler_params=_cparams(("parallel",)),
    )(dm, pa, pb, qg, qg)


def _resid_norm_mod(x, branch, gain, mods, name):
    t, d = x.shape
    rb = min(ROW_BLOCK, t)

    def body(x_ref, b_ref, g_ref, mod_ref, x1_ref, z_ref):
        x1 = x_ref[...] + mod_ref[0:1, :] * b_ref[...]
        x1_ref[...] = x1
        xh, _ = _rms(x1)
        z_ref[...] = ((xh * g_ref[...]) * (1.0 + mod_ref[2:3, :]) + mod_ref[1:2, :]).astype(BF16)

    row = pl.BlockSpec((rb, d), lambda i: (i, 0))
    return _pcall(
        body,
        name=name,
        out_shape=[jax.ShapeDtypeStruct((t, d), F32), jax.ShapeDtypeStruct((t, d), BF16)],
        grid=(t // rb,),
        in_specs=[row, row, pl.BlockSpec((1, d), lambda i: (0, 0)), pl.BlockSpec((8, d), lambda i: (0, 0))],
        out_specs=[row, row],
        compiler_params=_cparams(("parallel",)),
    )(x, branch, gain, mods)


def _norm2_bwd(x1, attn, gain, mods, dz2, dx2):
    t, d = x1.shape
    rb = min(ROW_BLOCK, t)

    def body(x1_ref, at_ref, g_ref, mod_ref, dz_ref, dx2_ref, dx1_ref, da_ref, st_ref):
        @pl.when(pl.program_id(0) == 0)
        def _():
            st_ref[...] = jnp.zeros_like(st_ref)

        xh, r = _rms(x1_ref[...])
        g = g_ref[...]
        dz = dz_ref[...]
        dxn = dz * (1.0 + mod_ref[1:2, :])
        st_ref[0:1, :] += _colsum(dz)
        st_ref[1:2, :] += _colsum(dz * (xh * g))
        st_ref[2:3, :] += _colsum(dxn * xh)
        dx1 = dx2_ref[...] + _rms_bwd(dxn * g, xh, r)
        dx1_ref[...] = dx1
        st_ref[3:4, :] += _colsum(dx1 * at_ref[...])
        da_ref[...] = (dx1 * mod_ref[0:1, :]).astype(BF16)

    row = pl.BlockSpec((rb, d), lambda i: (i, 0))
    return _pcall(
        body,
        name="norm2_mod_bwd",
        out_shape=[jax.ShapeDtypeStruct((t, d), F32), jax.ShapeDtypeStruct((t, d), BF16), jax.ShapeDtypeStruct((8, d), F32)],
        grid=(t // rb,),
        in_specs=[row, row, pl.BlockSpec((1, d), lambda i: (0, 0)), pl.BlockSpec((8, d), lambda i: (0, 0)), row, row],
        out_specs=[row, row, pl.BlockSpec((8, d), lambda i: (0, 0))],
        compiler_params=_cparams(("arbitrary",)),
    )(x1, attn, gain, mods, dz2, dx2)


def _final_loss(x1, ffn, gain, mods, target):
    t, d = x1.shape
    rb = min(ROW_BLOCK, t)
    nb = t // rb

    def body(x1_ref, f_ref, g_ref, mod_ref, tg_ref, dx2_ref, df_ref, st_ref):
        i = pl.program_id(0)

        @pl.when(i == 0)
        def _():
            st_ref[...] = jnp.zeros_like(st_ref)

        ffn_v = f_ref[...]
        g2 = mod_ref[0:1, :]
        x2 = x1_ref[...] + g2 * ffn_v
        xh, r = _rms(x2)
        g = g_ref[...]
        err = xh * g - tg_ref[...]
        st_ref[2:3, :] += _colsum(err * err) * (0.5 / d)
        dy = err * (1.0 / d)
        st_ref[0:1, :] += _colsum(dy * xh)
        dx2 = _rms_bwd(dy * g, xh, r)
        dx2_ref[...] = dx2
        st_ref[1:2, :] += _colsum(dx2 * ffn_v)
        df_ref[...] = (dx2 * g2).astype(BF16)

        @pl.when(i == nb - 1)
        def _():
            st_ref[3:4, :] = jnp.broadcast_to(jnp.sum(st_ref[2:3, :], axis=-1, keepdims=True), (1, d))

    row = pl.BlockSpec((rb, d), lambda i: (i, 0))
    return _pcall(
        body,
        name="final_norm_loss",
        out_shape=[jax.ShapeDtypeStruct((t, d), F32), jax.ShapeDtypeStruct((t, d), BF16), jax.ShapeDtypeStruct((8, d), F32)],
        grid=(nb,),
        in_specs=[row, row, pl.BlockSpec((1, d), lambda i: (0, 0)), pl.BlockSpec((8, d), lambda i: (0, 0)), row],
        out_specs=[row, row, pl.BlockSpec((8, d), lambda i: (0, 0))],
        compiler_params=_cparams(("arbitrary",)),
    )(x1, ffn, gain, mods, target)


def _row_ends(shape):
    rows = lax.broadcasted_iota(jnp.int32, shape, 0)
    return rows == 0, rows == shape[0] - 1


def _shift_dn(v, first):
    return jnp.where(first, 0.0, pltpu.roll(v, 1, 0))


def _shift_up(v, last):
    return jnp.where(last, 0.0, pltpu.roll(v, v.shape[0] - 1, 0))


def _conv_fwd(u, cw, cb):
    t, f2 = u.shape
    f = f2 // 2
    cbk = _tile(f, 256)
    nf = f // cbk

    def body(ua_ref, ub_ref, cwa_ref, cwb_ref, cba_ref, cbb_ref, h_ref, uc_ref):
        first, last = _row_ends((t, cbk))
        outs = []
        for u_ref, cw_ref, cb_ref in ((ua_ref, cwa_ref, cba_ref), (ub_ref, cwb_ref, cbb_ref)):
            uu, cwv = u_ref[...], cw_ref[...]
            outs.append(cb_ref[...] + cwv[0:1, :] * _shift_dn(uu, first) + cwv[1:2, :] * uu
                        + cwv[2:3, :] * _shift_up(uu, last))
        a, b = outs
        uc_ref[0] = a
        uc_ref[1] = b
        h_ref[...] = (a * jax.nn.sigmoid(a) * b).astype(BF16)

    ca = lambda r: pl.BlockSpec((r, cbk), lambda j: (0, j))
    cbs = lambda r: pl.BlockSpec((r, cbk), lambda j: (0, nf + j))
    return _pcall(
        body,
        name="conv_gate_fwd",
        out_shape=[jax.ShapeDtypeStruct((t, f), BF16), jax.ShapeDtypeStruct((2, t, f), F32)],
        grid=(nf,),
        in_specs=[ca(t), cbs(t), ca(3), cbs(3), ca(1), cbs(1)],
        out_specs=[ca(t), pl.BlockSpec((2, t, cbk), lambda j: (0, 0, j))],
        compiler_params=_cparams(("parallel",)),
    )(u, u, cw, cw, cb, cb)


def _conv_bwd(u, uc, cw, dh):
    t, f2 = u.shape
    f = f2 // 2
    cbk = _tile(f, 256)
    nf = f // cbk

    def body(ua_ref, ub_ref, uc_ref, cwa_ref, cwb_ref, dh_ref, du_ref, dcw_ref, dcb_ref):
        first, last = _row_ends((t, cbk))
        a, b = uc_ref[0], uc_ref[1]
        dh_v = dh_ref[...]
        sg = jax.nn.sigmoid(a)
        db = dh_v * (a * sg)
        da = dh_v * b * (sg * (1.0 + a * (1.0 - sg)))
        for idx, (dv, u_ref, cw_ref) in enumerate(((da, ua_ref, cwa_ref), (db, ub_ref, cwb_ref))):
            uu, cwv = u_ref[...], cw_ref[...]
            up, dn = _shift_up(dv, last), _shift_dn(dv, first)
            dcb_ref[idx] = _colsum(dv)
            dcw_ref[idx, 0:1, :] = _colsum(up * uu)
            dcw_ref[idx, 1:2, :] = _colsum(dv * uu)
            dcw_ref[idx, 2:3, :] = _colsum(dn * uu)
            du_ref[idx] = (cwv[0:1, :] * up + cwv[1:2, :] * dv + cwv[2:3, :] * dn).astype(BF16)

    ca = lambda r: pl.BlockSpec((r, cbk), lambda j: (0, j))
    cbs = lambda r: pl.BlockSpec((r, cbk), lambda j: (0, nf + j))
    o3 = lambda r: pl.BlockSpec((2, r, cbk), lambda j: (0, 0, j))
    return _pcall(
        body,
        name="conv_gate_bwd",
        out_shape=[jax.ShapeDtypeStruct((2, t, f), BF16), jax.ShapeDtypeStruct((2, 3, f), F32),
                   jax.ShapeDtypeStruct((2, 1, f), F32)],
        grid=(nf,),
        in_specs=[ca(t), cbs(t), o3(t), ca(3), cbs(3), ca(t)],
        out_specs=[o3(t), o3(3), o3(1)],
        compiler_params=_cparams(("parallel",)),
    )(u, u, uc, cw, cw, dh)


def _attention_fwd(q, kk, vv, *, hq, hkv, dk, dv, k_blk0, v_blk0, name):
    t = q.shape[0]
    tk = kk.shape[0]
    g_sz = hq // hkv
    tq = min(ATT_Q_BLOCK_FWD, t)

    def body(q_ref, k_ref, v_ref, o_ref, lse_ref):
        k = k_ref[...]
        v = v_ref[...]
        for j in range(g_sz):
            s = lax.dot_general(q_ref[:, j * dk:(j + 1) * dk], k, _DIMS["nt"], preferred_element_type=F32)
            m = jnp.max(s, axis=-1, keepdims=True)
            p = jnp.exp2(s - m)
            l = jnp.sum(p, axis=-1, keepdims=True)
            o = jnp.dot(p.astype(BF16), v, preferred_element_type=F32) / l
            o_ref[:, j * dv:(j + 1) * dv] = o.astype(BF16)
            lse_ref[0, :, j:j + 1] = m + jnp.log2(l)

    return _pcall(
        body,
        name=name,
        out_shape=[jax.ShapeDtypeStruct((t, hq * dv), BF16), jax.ShapeDtypeStruct((hkv, t, g_sz), F32)],
        grid=(hkv, t // tq),
        in_specs=[
            pl.BlockSpec((tq, g_sz * dk), lambda g, i: (i, g)),
            pl.BlockSpec((tk, dk), lambda g, i: (0, k_blk0 + g)),
            pl.BlockSpec((tk, dv), lambda g, i: (0, v_blk0 + g)),
        ],
        out_specs=[
            pl.BlockSpec((tq, g_sz * dv), lambda g, i: (i, g)),
            pl.BlockSpec((1, tq, g_sz), lambda g, i: (g, i, 0)),
        ],
        compiler_params=_cparams(("parallel", "parallel")),
    )(q, kk, vv)


def _attention_bwd(q, kk, vv, do, lse, *, hq, hkv, dk, dv, k_blk0, v_blk0, name):
    t = q.shape[0]
    tk = kk.shape[0]
    g_sz = hq // hkv
    tq = min(ATT_Q_BLOCK, t)

    def body(q_ref, k_ref, v_ref, do_ref, lse_ref, dq_ref, dk_ref, dv_ref):
        @pl.when(pl.program_id(1) == 0)
        def _():
            dk_ref[...] = jnp.zeros_like(dk_ref)
            dv_ref[...] = jnp.zeros_like(dv_ref)

        k = k_ref[...]
        v = v_ref[...]
        for j in range(g_sz):
            qj = q_ref[:, j * dk:(j + 1) * dk]
            doj = do_ref[:, j * dv:(j + 1) * dv]
            s = lax.dot_general(qj, k, _DIMS["nt"], preferred_element_type=F32)
            p = jnp.exp2(s - lse_ref[0, :, j:j + 1])
            dp = lax.dot_general(doj, v, _DIMS["nt"], preferred_element_type=F32)
            ds = (p * (dp - jnp.sum(p * dp, axis=-1, keepdims=True)) * LN2).astype(BF16)
            dv_ref[...] += lax.dot_general(p.astype(BF16), doj, _DIMS["tn"], preferred_element_type=F32)
            dk_ref[...] += lax.dot_general(ds, qj, _DIMS["tn"], preferred_element_type=F32)
            dq_ref[:, j * dk:(j + 1) * dk] = jnp.dot(ds, k, preferred_element_type=F32)

    return _pcall(
        body,
        name=name,
        out_shape=[jax.ShapeDtypeStruct((t, hq * dk), F32), jax.ShapeDtypeStruct((tk, hkv * dk), F32),
                   jax.ShapeDtypeStruct((tk, hkv * dv), F32)],
        grid=(hkv, t // tq),
        in_specs=[
            pl.BlockSpec((tq, g_sz * dk), lambda g, i: (i, g)),
            pl.BlockSpec((tk, dk), lambda g, i: (0, k_blk0 + g)),
            pl.BlockSpec((tk, dv), lambda g, i: (0, v_blk0 + g)),
            pl.BlockSpec((tq, g_sz * dv), lambda g, i: (i, g)),
            pl.BlockSpec((1, tq, g_sz), lambda g, i: (g, i, 0)),
        ],
        out_specs=[
            pl.BlockSpec((tq, g_sz * dk), lambda g, i: (i, g)),
            pl.BlockSpec((tk, dk), lambda g, i: (0, g)),
            pl.BlockSpec((tk, dv), lambda g, i: (0, g)),
        ],
        compiler_params=_cparams(("parallel", "arbitrary")),
    )(q, kk, vv, do, lse)


def _silu(v):
    return v * jax.nn.sigmoid(v)


def _ada_fwd(conds, w_ada, b_ada_shard):
    r, d = conds.shape
    n = w_ada.shape[1]
    tn = _tile(n, 512)

    def body(c_ref, w_ref, b_ref, o_ref):
        s = _silu(c_ref[...]).astype(BF16)
        o_ref[...] = jnp.dot(s, w_ref[...].astype(BF16), preferred_element_type=F32) + b_ref[...]

    return _pcall(
        body,
        name="ada_fwd",
        out_shape=jax.ShapeDtypeStruct((r, n), F32),
        grid=(n // tn,),
        in_specs=[pl.BlockSpec((r, d), lambda j: (0, 0)), pl.BlockSpec((d, tn), lambda j: (0, j)),
                  pl.BlockSpec((1, tn), lambda j: (0, j))],
        out_specs=pl.BlockSpec((r, tn), lambda j: (0, j)),
        compiler_params=_cparams(("parallel",)),
    )(conds, w_ada, b_ada_shard)


def _cctx_partial(da16_shard, w_ada, c_ctx_row):
    d, n = w_ada.shape
    td = _tile(d, 512)

    def body(g_ref, w_ref, c_ref, o_ref):
        ds = lax.dot_general(g_ref[8:16, :].astype(BF16), w_ref[...].astype(BF16), _DIMS["nt"],
                             preferred_element_type=F32)
        cv = c_ref[...]
        sg = jax.nn.sigmoid(cv)
        o_ref[...] = ds * (sg * (1.0 + cv * (1.0 - sg)))

    return _pcall(
        body,
        name="cctx_partial",
        out_shape=jax.ShapeDtypeStruct((8, d), F32),
        grid=(d // td,),
        in_specs=[pl.BlockSpec((16, n), lambda j: (0, 0)), pl.BlockSpec((td, n), lambda j: (j, 0)),
                  pl.BlockSpec((1, td), lambda j: (0, j))],
        out_specs=pl.BlockSpec((8, td), lambda j: (0, j)),
        compiler_params=_cparams(("parallel",)),
    )(da16_shard, w_ada, c_ctx_row)


def _sum_parts(parts):
    p, _, n = parts.shape

    def body(p_ref, o_ref):
        acc = p_ref[0]
        for s in range(1, p):
            acc = acc + p_ref[s]
        o_ref[...] = acc

    return _pcall(
        body,
        name="sum_parts",
        out_shape=jax.ShapeDtypeStruct((1, n), F32),
        in_specs=[pl.BlockSpec(memory_space=pltpu.VMEM)],
        out_specs=pl.BlockSpec(memory_space=pltpu.VMEM),
    )(parts)


def _adam_math(w, g, m, v):
    m2 = ADAM_B1 * m + (1.0 - ADAM_B1) * g
    v2 = ADAM_B2 * v + (1.0 - ADAM_B2) * jnp.square(g)
    m_hat = m2 / (1.0 - ADAM_B1 ** ADAM_STEP)
    v_hat = v2 / (1.0 - ADAM_B2 ** ADAM_STEP)
    delta = -ADAM_LR * (m_hat / (jnp.sqrt(v_hat) + ADAM_EPS) + ADAM_WD * w)
    return delta, m2, v2


def _adamw(parts, w, m, v, name):
    p, r, c = parts.shape
    rb = _tile(r, max(8, (1 << 20) // (4 * c) // 8 * 8), 8)

    def body(p_ref, w_ref, m_ref, v_ref, g_ref, d_ref, m2_ref, v2_ref):
        g = p_ref[0].astype(F32)
        for s in range(1, p):
            g = g + p_ref[s].astype(F32)
        g_ref[...] = g
        d_ref[...], m2_ref[...], v2_ref[...] = _adam_math(w_ref[...], g, m_ref[...], v_ref[...])

    row = pl.BlockSpec((rb, c), lambda i: (i, 0))
    return _pcall(
        body,
        name=name,
        out_shape=[jax.ShapeDtypeStruct((r, c), F32)] * 4,
        grid=(r // rb,),
        in_specs=[pl.BlockSpec((p, rb, c), lambda i: (0, i, 0)), row, row, row],
        out_specs=[row] * 4,
        compiler_params=_cparams(("parallel",)),
    )(parts, w, m, v)


def _adamw_ada(conds, da16, w, m, v):
    d, n = w.shape
    rb = _tile(d, 256, LANE)

    def body(s_ref, da_ref, w_ref, m_ref, v_ref, g_ref, d_ref, m2_ref, v2_ref):
        g = lax.dot_general(_silu(s_ref[...]).astype(BF16), da_ref[...].astype(BF16), _DIMS["tn"],
                            preferred_element_type=F32)
        g_ref[...] = g
        d_ref[...], m2_ref[...], v2_ref[...] = _adam_math(w_ref[...], g, m_ref[...], v_ref[...])

    row = pl.BlockSpec((rb, n), lambda i: (i, 0))
    return _pcall(
        body,
        name="adamw_w_ada",
        out_shape=[jax.ShapeDtypeStruct((d, n), F32)] * 4,
        grid=(d // rb,),
        in_specs=[pl.BlockSpec((16, rb), lambda i: (0, i)), pl.BlockSpec((16, n), lambda i: (0, 0)), row, row, row],
        out_specs=[row] * 4,
        compiler_params=_cparams(("parallel",)),
    )(conds, da16, w, m, v)


def _cast_bf16(a, name):
    r, c = a.shape
    rb = _tile(r, 512, 8)

    def body(a_ref, o_ref):
        o_ref[...] = a_ref[...].astype(BF16)

    row = pl.BlockSpec((rb, c), lambda i: (i, 0))
    return _pcall(body, name=name, out_shape=jax.ShapeDtypeStruct((r, c), BF16), grid=(r // rb,),
                          in_specs=[row], out_specs=row, compiler_params=_cparams(("parallel",)))(a)


def _rope_tabs(t, rot):
    half, q = rot // 2, rot // 4
    n_rows = t // GRID_W
    row = jnp.repeat(jnp.arange(n_rows, dtype=F32), GRID_W)
    col = jnp.tile(jnp.arange(GRID_W, dtype=F32), n_rows)
    inv_freq = ROPE_THETA ** (-jnp.arange(0, half, 2, dtype=F32) / half)
    ang = jnp.concatenate([row[:, None] * inv_freq, col[:, None] * inv_freq], axis=-1)
    cos, sin = jnp.cos(ang), jnp.sin(ang)
    c0, c1, s0, s1 = cos[:, :q], cos[:, q:], sin[:, :q], sin[:, q:]
    z = jnp.zeros_like(s0)
    return (jnp.concatenate([c0, c0, c1, c1], -1), jnp.concatenate([-s0, z, -s1, z], -1),
            jnp.concatenate([z, s0, z, s1], -1))


def _pad_cols(a, left, total, fill=0.0):
    return jnp.pad(a, ((0, 0), (left, total - left - a.shape[1])), constant_values=fill)


def _with_ctx_rows(tab, tc, fill):
    return jnp.concatenate([tab, jnp.full((tc, tab.shape[1]), fill, F32)], axis=0)


def kernel(x, c, ctx, c_ctx, w_ada, b_ada, norm1_g, w_in, mla_q_norm_g, w_q_up, mla_kv_norm_g, w_kv_up, gqa_q_norm_g, gqa_k_norm_g, w_br_a, w_br_b, w_out, norm2_g, w_up, conv_w, conv_b, w_down, final_norm_g, loss_target, m_c_ctx, m_w_ada, m_b_ada, m_norm1_g, m_w_in, m_mla_q_norm_g, m_w_q_up, m_mla_kv_norm_g, m_w_kv_up, m_gqa_q_norm_g, m_gqa_k_norm_g, m_w_br_a, m_w_br_b, m_w_out, m_norm2_g, m_w_up, m_conv_w, m_conv_b, m_w_down, m_final_norm_g, v_c_ctx, v_w_ada, v_b_ada, v_norm1_g, v_w_in, v_mla_q_norm_g, v_w_q_up, v_mla_kv_norm_g, v_w_kv_up, v_gqa_q_norm_g, v_gqa_k_norm_g, v_w_br_a, v_w_br_b, v_w_out, v_norm2_g, v_w_up, v_conv_w, v_conv_b, v_w_down, v_final_norm_g):
    weights = dict(c_ctx=c_ctx, w_ada=w_ada, b_ada=b_ada, norm1_g=norm1_g, w_in=w_in, mla_q_norm_g=mla_q_norm_g,
                   w_q_up=w_q_up, mla_kv_norm_g=mla_kv_norm_g, w_kv_up=w_kv_up, gqa_q_norm_g=gqa_q_norm_g,
                   gqa_k_norm_g=gqa_k_norm_g, w_br_a=w_br_a, w_br_b=w_br_b, w_out=w_out, norm2_g=norm2_g, w_up=w_up,
                   conv_w=conv_w, conv_b=conv_b, w_down=w_down, final_norm_g=final_norm_g)
    mom_m = dict(c_ctx=m_c_ctx, w_ada=m_w_ada, b_ada=m_b_ada, norm1_g=m_norm1_g, w_in=m_w_in, mla_q_norm_g=m_mla_q_norm_g,
                 w_q_up=m_w_q_up, mla_kv_norm_g=m_mla_kv_norm_g, w_kv_up=m_w_kv_up, gqa_q_norm_g=m_gqa_q_norm_g,
                 gqa_k_norm_g=m_gqa_k_norm_g, w_br_a=m_w_br_a, w_br_b=m_w_br_b, w_out=m_w_out, norm2_g=m_norm2_g,
                 w_up=m_w_up, conv_w=m_conv_w, conv_b=m_conv_b, w_down=m_w_down, final_norm_g=m_final_norm_g)
    mom_v = dict(c_ctx=v_c_ctx, w_ada=v_w_ada, b_ada=v_b_ada, norm1_g=v_norm1_g, w_in=v_w_in, mla_q_norm_g=v_mla_q_norm_g,
                 w_q_up=v_w_q_up, mla_kv_norm_g=v_mla_kv_norm_g, w_kv_up=v_w_kv_up, gqa_q_norm_g=v_gqa_q_norm_g,
                 gqa_k_norm_g=v_gqa_k_norm_g, w_br_a=v_w_br_a, w_br_b=v_w_br_b, w_out=v_w_out, norm2_g=v_norm2_g,
                 w_up=v_w_up, conv_w=v_conv_w, conv_b=v_conv_b, w_down=v_w_down, final_norm_g=v_final_norm_g)
    order = list(weights)

    my_idx = 4 * lax.axis_index("x") + 2 * lax.axis_index("y") + lax.axis_index("c")
    xs, cts, tgt = x[0], ctx[0], loss_target[0]
    t, d = xs.shape
    tc = cts.shape[0]
    ta = t + tc
    kvl, ql = MLA_KV_LORA, MLA_Q_LORA
    nb = GQA_KV_HEADS * GQA_HEAD_DIM
    hb = GQA_HEADS * GQA_HEAD_DIM
    ha = MLA_HEADS
    f2 = w_up.shape[2] * N_DEV
    ff = f2 // 2

    big = ["w_in", "w_q_up", "w_kv_up", "w_br_a", "w_br_b", "w_out", "w_up", "w_down"]
    nw = len(big)
    del nw
    _ORDER_AFTER.clear()
    shards = {n: _cast_bf16(weights[n][0], "cast_" + n) for n in big}
    c_idx = jnp.reshape(lax.axis_index("c"), (1,)).astype(jnp.int32)

    def gather_start(names, dep):
        shs = [shards[n] for n in names]
        land = [lax.empty((N_DEV,) + s.shape, BF16) for s in shs]
        if dep is not None:
            _after(dep)
        s, r, arrs, tok = _split_start("gather_ici_start_" + names[0], shs + land, _gather_ici_copies(len(names)),
                                       5 * len(names))
        return dict(names=names, s=s, r=r, arrs=arrs, tok=tok)

    def gather_relay(g, after):
        n = len(g["names"])
        arrs = _split_wait("gather_ici_wait_" + g["names"][0], g["s"], g["r"], g["arrs"], _gather_ici_copies(n), after)
        s, r, bufs, tok = _split_start("gather_d2d_start_" + g["names"][0], arrs[n:], _gather_d2d_copies(n), 3 * n)
        g.update(s2=s, r2=r, bufs=bufs)
        return tok

    def gather_finish(g, after):
        n = len(g["names"])
        bufs = _split_wait("gather_d2d_wait_" + g["names"][0], g["s2"], g["r2"], g["bufs"], _gather_d2d_copies(n), after)
        return dict(zip(g["names"], bufs))

    c_all, cw_all = _all_gather([jnp.pad(c, ((0, 7), (0, 0))), jnp.pad(conv_w[0], ((0, 5), (0, 0)))], "gather_cond")
    conv_w_f = jnp.transpose(cw_all[:, :3, :], (1, 0, 2)).reshape(3, f2)
    conds = jnp.concatenate([c_all[:, 0, :], c_ctx[None, :], jnp.zeros((7, d), F32)], axis=0)
    ncol = w_ada.shape[2]
    b_shard = lax.dynamic_slice_in_dim(b_ada, my_idx * ncol, ncol, axis=1)
    ada_shard = _ada_fwd(conds, w_ada[0], b_shard)
    (ada_all,) = _all_gather([ada_shard], "gather_ada")
    ada = jnp.transpose(ada_all, (1, 0, 2)).reshape(16, N_DEV * ncol)
    lat = lax.dynamic_slice_in_dim(ada, my_idx, 1, axis=0).reshape(6, d)
    cxt = ada[8].reshape(6, d)
    zero2 = jnp.zeros((2, d), F32)
    mods1 = jnp.concatenate([lat[0:2], cxt[0:2], jnp.zeros((4, d), F32)], axis=0)
    mods2 = jnp.concatenate([lat[2:3], lat[3:4], lat[4:5], jnp.zeros((5, d), F32)], axis=0)
    mods2b = jnp.concatenate([lat[2:3], lat[4:5], jnp.zeros((6, d), F32)], axis=0)
    mods3 = jnp.concatenate([lat[5:6], jnp.zeros((7, d), F32)], axis=0)
    del zero2

    g0 = gather_start(["w_in"], ada_all)
    g1 = gather_start(["w_q_up", "w_kv_up", "w_br_a", "w_br_b", "w_out"], g0["tok"])
    g2 = gather_start(["w_up"], g1["tok"])
    g3 = gather_start(["w_down"], g2["tok"])

    ca, s1a, s2a = _rope_tabs(t, MLA_ROPE)
    cb_, s1b, s2b = _rope_tabs(t, GQA_HEAD_DIM)
    q_tabs_a = (_pad_cols(jnp.concatenate([jnp.ones((t, MLA_NOPE), F32), ca], 1), 0, MLA_SLOT),
                _pad_cols(s1a, MLA_NOPE, MLA_SLOT), _pad_cols(s2a, MLA_NOPE, MLA_SLOT))
    q_tabs_b = (cb_, s1b, s2b)
    k_tabs = (_with_ctx_rows(_pad_cols(ca, 0, LANE), tc, 1.0), _with_ctx_rows(_pad_cols(s1a, 0, LANE), tc, 0.0),
              _with_ctx_rows(_pad_cols(s2a, 0, LANE), tc, 0.0),
              _with_ctx_rows(cb_, tc, 1.0), _with_ctx_rows(s1b, tc, 0.0), _with_ctx_rows(s2b, tc, 0.0))

    def cols_full(g):
        return jnp.transpose(g, (1, 0, 2)).reshape(g.shape[1], N_DEV * g.shape[2])

    _after(gather_relay(g0, mods1))
    z_all = _norm_mod_fwd(cts, xs, norm1_g, mods1)
    gathered = gather_finish(g0, z_all)
    w_in_f = cols_full(gathered["w_in"])
    o_kpe, o_kb, o_vb = kvl, kvl + MLA_ROPE, kvl + MLA_ROPE + nb
    o_q = o_vb + nb
    o_g = o_q + ql + hb
    wkv_w = kvl + 2 * nb + LANE
    w_kv_p = jnp.concatenate([w_in_f[:, :kvl], w_in_f[:, o_kb:o_q], w_in_f[:, o_kpe:o_kb],
                              jnp.zeros((d, LANE - MLA_ROPE), BF16)], axis=1)
    q_w = ql + hb
    q_pad = (-q_w) % 512 if d >= 512 else (-q_w) % d
    gate_blk = (q_w + q_pad) // d
    assert (q_w + q_pad) % d == 0
    w_qg_p = jnp.concatenate([w_in_f[:, o_q:o_g], jnp.zeros((d, q_pad), BF16), w_in_f[:, o_g:]], axis=1)

    kv_all = _mm(z_all, w_kv_p, "nn", F32, "proj_kv", tm=1152, tn=wkv_w)
    qg = _mm(z_all, w_qg_p, "nn", F32, "proj_qg", tm=1024, tn=1024, rows=t)
    _after(gather_relay(g1, qg))
    kin, k_b, v_b = _key_prep_fwd(kv_all, mla_kv_norm_g, gqa_k_norm_g, k_tabs)
    sc_a = float((MLA_NOPE + MLA_ROPE) ** -0.5) * LOG2E
    sc_b = float(GQA_HEAD_DIM ** -0.5) * LOG2E
    cqn, q_b = _q_prep_fwd(qg, mla_q_norm_g, gqa_q_norm_g, q_tabs_b, sc_b)
    gathered.update(gather_finish(g1, q_b))

    wq_f = cols_full(gathered["w_q_up"]).reshape(ql, ha, MLA_NOPE + MLA_ROPE)
    wq_ext = jnp.pad(wq_f, ((0, 0), (0, 0), (0, MLA_SLOT - MLA_NOPE - MLA_ROPE))).reshape(ql, ha * MLA_SLOT)
    wkv_f = cols_full(gathered["w_kv_up"]).reshape(kvl, ha, MLA_NOPE + MLA_V)
    wk_slots = jnp.pad(wkv_f[:, :, :MLA_NOPE], ((0, 0), (0, 0), (0, MLA_SLOT - MLA_NOPE))).reshape(kvl, ha * MLA_SLOT)
    wv_cols = wkv_f[:, :, MLA_NOPE:].reshape(kvl, ha * MLA_V)
    e_slot = jnp.pad(jnp.eye(MLA_ROPE, dtype=BF16),
                     ((0, LANE - MLA_ROPE), (MLA_NOPE, MLA_SLOT - MLA_NOPE - MLA_ROPE)))
    e_rows = jnp.concatenate([jnp.tile(e_slot, (1, ha)), jnp.zeros((LANE, ha * MLA_V), BF16)], axis=1)
    wkv_ext = jnp.concatenate([jnp.concatenate([wk_slots, wv_cols], axis=1), e_rows], axis=0)
    w_bra = cols_full(gathered["w_br_a"])
    w_brb = cols_full(gathered["w_br_b"])
    w_out_f = gathered["w_out"].reshape(d, d)

    kv_a = _mm(kin, wkv_ext, "nn", BF16, "kv_up", tm=1152, tn=1024)
    qa_raw = _mm(cqn, wq_ext, "nn", F32, "q_up", tm=1024, tn=1024)
    q_a = _rope_a(qa_raw, q_tabs_a, False, BF16, "rope_q_fwd", sc_a)
    att_a = dict(hq=ha, hkv=ha, dk=MLA_SLOT, dv=MLA_V, k_blk0=0, v_blk0=ha * MLA_SLOT // MLA_V)
    att_b = dict(hq=GQA_HEADS, hkv=GQA_KV_HEADS, dk=GQA_HEAD_DIM, dv=GQA_HEAD_DIM, k_blk0=0, v_blk0=0)
    o_a, lse_a = _attention_fwd(q_a, kv_a, kv_a, name="attn_a_fwd", **att_a)
    o_b, lse_b = _attention_fwd(q_b, k_b, v_b, name="attn_b_fwd", **att_b)
    _after(gather_relay(g2, o_b))
    pa = _mm(o_a, w_bra, "nn", F32, "br_a", tm=1024, tn=1024)
    pb = _mm(o_b, w_brb, "nn", F32, "br_b", tm=1024, tn=1024)
    merged = _merge_fwd(pa, pb, qg, gate_blk)
    attn = _mm(merged, w_out_f, "nn", F32, "w_out", tm=1024, tn=1024)
    x1, z2 = _resid_norm_mod(xs, attn, norm2_g, mods2, "resid_norm2_fwd")
    w_up3 = gather_finish(g2, z2)["w_up"]
    _after(gather_relay(g3, z2))
    u = _mm_up_fwd(z2, w_up3, "w_up")
    w_down_f = gather_finish(g3, u)["w_down"].reshape(ff, d)
    h, uc = _conv_fwd(u, conv_w_f, conv_b)
    ffn = _mm(h, w_down_f, "nn", F32, "w_down", tm=1024, tn=1024, tk=2816)

    def to_shards(g):
        return jnp.transpose(g.reshape(g.shape[0], N_DEV, g.shape[1] // N_DEV), (1, 0, 2))

    def reduce_start(tag, names, sends):
        n = len(sends)
        land = [lax.empty((4,) + s.shape[1:], s.dtype) for s in sends]
        s, r, arrs, tok = _split_start("reduce_d2d_start_" + tag, sends + land, _reduce_d2d_copies(n), 4 * n)
        return dict(tag=tag, names=names, s=s, r=r, arrs=arrs, tok=tok)

    def reduce_relay(g, after):
        n = len(g["names"])
        arrs = _split_wait("reduce_d2d_wait_" + g["tag"], g["s"], g["r"], g["arrs"], _reduce_d2d_copies(n), after)
        sums = [_pair_sum(arrs[a], arrs[n + a], c_idx, "pair_sum_" + g["names"][a]) for a in range(n)]
        land = [lax.empty(s.shape, s.dtype) for s in sums]
        s, r, arrs2, tok = _split_start("reduce_ici_start_" + g["tag"], sums + land, _reduce_ici_copies(n), 4 * n)
        g.update(s2=s, r2=r, arrs2=arrs2)
        return tok

    def reduce_finish(g, after):
        n = len(g["names"])
        arrs2 = _split_wait("reduce_ici_wait_" + g["tag"], g["s2"], g["r2"], g["arrs2"], _reduce_ici_copies(n), after)
        return dict(zip(g["names"], arrs2[n:]))

    dx2, dffn, st_fin = _final_loss(x1, ffn, final_norm_g[None, :], mods3, tgt)
    loss = lax.psum(st_fin[3, 0], MESH_AXES)
    dh = _mm(dffn, w_down_f, "nt", F32, "d_h", tm=1024, tn=1024)
    g_w_down = _mm(h, dffn, "tn", BF16, "g_w_down", tm=512, tn=1024)
    r_down = reduce_start("down", ["w_down"], [g_w_down.reshape(N_DEV, ff // N_DEV, d)])
    _after(r_down["tok"])
    du3, dcw, dcb = _conv_bwd(u, uc, conv_w_f, dh)
    dz2 = _mm_up_dz(du3, w_up3, "d_z2")
    g_w_up = _mm_up_gw(z2, du3, N_DEV, "g_w_up")
    g_conv_w = jnp.concatenate([dcw[0], dcw[1]], axis=1)
    tok = reduce_relay(r_down, g_w_up)
    _after(tok)
    r_up = reduce_start("up", ["w_up", "conv_w"], [g_w_up, to_shards(jnp.pad(g_conv_w, ((0, 5), (0, 0))))])
    _after(tok, r_up["tok"])
    dx1, dattn, st_n2 = _norm2_bwd(x1, attn, norm2_g, mods2b, dz2, dx2)
    dmerged = _mm(dattn, w_out_f, "nt", F32, "d_merged", tm=1024, tn=1024)
    g_w_out = _mm(merged, dattn, "tn", BF16, "g_w_out", tm=1024, tn=1024)
    dpa, dpb, dgates = _merge_bwd(dmerged, pa, pb, qg, gate_blk)
    do_a = _mm(dpa, w_bra, "nt", BF16, "d_o_a", tm=1024, tn=1024)
    do_b = _mm(dpb, w_brb, "nt", BF16, "d_o_b", tm=1024, tn=1024)
    g_w_bra = _mm(o_a, dpa, "tn", BF16, "g_w_br_a", tm=1024, tn=1024)
    g_w_brb = _mm(o_b, dpb, "tn", BF16, "g_w_br_b", tm=1024, tn=1024)
    tok = reduce_relay(r_up, g_w_brb)
    _after(tok)
    r_out = reduce_start("out", ["w_out", "w_br_a", "w_br_b"],
                         [g_w_out.reshape(N_DEV, d // N_DEV, d), to_shards(g_w_bra), to_shards(g_w_brb)])
    _after(tok, r_out["tok"])
    dq_a, dk_a, dv_a = _attention_bwd(q_a, kv_a, kv_a, do_a, lse_a, name="attn_a_bwd", **att_a)
    dq_b, dk_b, dv_b = _attention_bwd(q_b, k_b, v_b, do_b, lse_b, name="attn_b_bwd", **att_b)
    _after(reduce_relay(r_out, dv_b))
    dqa_raw = _rope_a(dq_a, q_tabs_a, True, BF16, "rope_q_bwd", sc_a)
    dcqn = _mm(dqa_raw, wq_ext, "nt", F32, "d_cqn", tm=1024, tn=ql)
    g_wq_ext = _mm(cqn, dqa_raw, "tn", BF16, "g_w_q_up", tm=ql, tn=1024)
    dq_p, st_q, st_qb = _q_prep_bwd(qg, mla_q_norm_g, gqa_q_norm_g, q_tabs_b, dcqn, dq_b, q_pad, sc_b)
    dkin = _mm_cat_nt([(dk_a, wkv_ext, 0), (dv_a, wkv_ext, ha * MLA_SLOT)], F32, "d_kin", tm=1152, tn=kvl + LANE)
    g_wkv_ext = _mm_cat_tn(kin, [dk_a, dv_a], BF16, "g_w_kv_up", tm=kvl + LANE, tn=min(1024, ha * MLA_V))
    dkv_p, st_kv, st_kb = _key_prep_bwd(kv_all, mla_kv_norm_g, gqa_k_norm_g, k_tabs, dkin, dk_b, dv_b)
    g_wq = g_wq_ext.reshape(ql, ha, MLA_SLOT)[:, :, :MLA_NOPE + MLA_ROPE].reshape(ql, ha * (MLA_NOPE + MLA_ROPE))
    g_wkv = jnp.concatenate([g_wkv_ext[:kvl, :ha * MLA_SLOT].reshape(kvl, ha, MLA_SLOT)[:, :, :MLA_NOPE],
                             g_wkv_ext[:kvl, ha * MLA_SLOT:].reshape(kvl, ha, MLA_V)], axis=2).reshape(kvl, ha * (MLA_NOPE + MLA_V))
    r_qkv = reduce_start("qkv", ["w_q_up", "w_kv_up"], [to_shards(g_wq), to_shards(g_wkv)])
    _after(r_qkv["tok"])
    g_wkv_p = _mm(z_all, dkv_p, "tn", BF16, "g_w_in_kv", tm=1024, tn=wkv_w)
    g_wqg_p = _mm_cat_tn(z_all, [dq_p, dgates], BF16, "g_w_in_qg", tm=1024, tn=min(1024, d), rows=t)
    g_w_in = jnp.concatenate([g_wkv_p[:, :kvl], g_wkv_p[:, kvl + 2 * nb:kvl + 2 * nb + MLA_ROPE],
                              g_wkv_p[:, kvl:kvl + 2 * nb], g_wqg_p[:, :q_w], g_wqg_p[:, q_w + q_pad:]], axis=1)
    r_in = reduce_start("in", ["w_in"], [to_shards(g_w_in)])
    _after(r_in["tok"])
    qw_p = q_w + q_pad
    dz_lat = _mm_sum_nt([(dq_p, 0, w_qg_p, 0, qw_p), (dgates, 0, w_qg_p, qw_p, d), (dgates, d, w_qg_p, qw_p + d, d),
                         (dkv_p, 0, w_kv_p, 0, wkv_w)], F32, "d_z_lat", rows=t)
    dz_ctx = _mm(dkv_p, w_kv_p, "nt", F32, "d_z_ctx", tm=min(ROW_BLOCK, tc), tn=1024, a_row_off=t)
    tok_q = reduce_relay(r_qkv, dz_ctx)
    _after(tok_q)
    grad_x, st_n1 = _norm1_bwd(cts, xs, norm1_g, mods1, dz_ctx, dz_lat, dx1)

    d_lat = jnp.concatenate([st_n1[0], st_n1[1], st_n2[3], st_n2[0], st_n2[1], st_fin[1]])
    d_cxt = jnp.concatenate([st_n1[3], st_n1[4], jnp.zeros((4 * d,), F32)])
    small = jnp.concatenate([d_lat, d_cxt, st_n1[2], st_q[0], st_kv[0], st_qb[0], st_kb[0], st_n2[2],
                             jnp.concatenate([dcb[0, 0], dcb[1, 0]]), st_fin[0]])
    n_small = small.shape[0]
    pad_small = (-n_small) % LANE
    (small_all,) = _all_gather([jnp.pad(small, (0, pad_small)).reshape(1, -1)], "gather_small")
    offs = {}
    o = 0
    for nm, ln in (("d_lat", 6 * d), ("d_cxt", 6 * d), ("norm1_g", d), ("mla_q_norm_g", ql), ("mla_kv_norm_g", kvl),
                   ("gqa_q_norm_g", GQA_HEAD_DIM), ("gqa_k_norm_g", GQA_HEAD_DIM), ("norm2_g", d), ("conv_b", f2),
                   ("final_norm_g", d)):
        offs[nm] = (o, ln)
        o += ln

    def part(nm):
        a, ln = offs[nm]
        return small_all[:, :, a:a + ln]

    d_lat_all = part("d_lat")[:, 0, :]
    d_cxt_sum = _sum_parts(part("d_cxt"))
    da16 = jnp.concatenate([d_lat_all, d_cxt_sum, jnp.zeros((7, 6 * d), F32)], axis=0)
    da16_shard = lax.dynamic_slice_in_dim(da16, my_idx * ncol, ncol, axis=1)
    cc_part = _cctx_partial(da16_shard, w_ada[0], c_ctx[None, :])
    (cc_all,) = _all_gather([cc_part], "gather_cctx")
    cc_parts = cc_all[:, 0:1, :]
    tok_i = reduce_relay(r_in, cc_all)

    res = {}
    _after(tok_q, tok_i)

    def upd(nm, parts, shape2):
        wv, mv, vv = (a.reshape(shape2) for a in (weights[nm], mom_m[nm], mom_v[nm]))
        outs = _adamw(parts, wv, mv, vv, "adamw_" + nm)
        res[nm] = [o_.reshape(weights[nm].shape) for o_ in outs]

    for nm in ("norm1_g", "mla_q_norm_g", "mla_kv_norm_g", "gqa_q_norm_g", "gqa_k_norm_g", "norm2_g", "conv_b",
               "final_norm_g"):
        upd(nm, part(nm), (1, offs[nm][1]))
    upd("c_ctx", cc_parts, (1, d))
    b_parts = jnp.concatenate([d_lat_all[:, None, :], d_cxt_sum[None]], axis=0)
    upd("b_ada", b_parts, (1, 6 * d))
    _after(tok_q, tok_i)
    outs = _adamw_ada(conds, da16_shard, w_ada[0], m_w_ada[0], v_w_ada[0])
    res["w_ada"] = [o_[None] for o_ in outs]
    last = outs[0]
    for grp in (r_down, r_up, r_out, r_qkv, r_in):
        recv = reduce_finish(grp, last)
        for nm in grp["names"]:
            parts = recv[nm][:, :3, :] if nm == "conv_w" else recv[nm]
            upd(nm, parts, weights[nm].shape[1:])
            last = res[nm][0]

    return (loss, grad_x[None], *[res[n][0] for n in order], *[res[n][1] for n in order],
            *[res[n][2] for n in order], *[res[n][3] for n in order])
```

```python
import functools

import jax
import jax.numpy as jnp
from jax import lax
from jax.experimental import pallas as pl
from jax.experimental.pallas import tpu as pltpu

F32 = jnp.float32
BF16 = jnp.bfloat16

GRID_W = 64
ROPE_THETA = 10000.0
NORM_EPS = 1e-6
MLA_HEADS = 8
MLA_Q_LORA = 768
MLA_KV_LORA = 512
MLA_NOPE = 128
MLA_ROPE = 64
MLA_V = 128
GQA_HEADS = 8
GQA_KV_HEADS = 2
GQA_HEAD_DIM = 128
ADAM_LR = 0.001
ADAM_B1 = 0.9
ADAM_B2 = 0.999
ADAM_EPS = 1e-08
ADAM_WD = 0.01
ADAM_STEP = 10

N_DEV = 8
MESH_AXES = ("x", "y", "c")
LANE = 128
MLA_SLOT = 2 * LANE
VMEM_LIMIT = 56 * 1024 * 1024
ROW_BLOCK = 256
ATT_Q_BLOCK = 512
ATT_Q_BLOCK_FWD = 512
LN2 = 0.6931471805599453
LOG2E = 1.4426950408889634
MESH_ID = pl.DeviceIdType.MESH


def _tile(n, pref, align=LANE):
    if n <= pref:
        return n
    best = None
    t = align
    while t <= pref:
        if n % t == 0:
            best = t
        t += align
    assert best is not None, (n, pref, align)
    return best


def _cparams(sem=None):
    return pltpu.CompilerParams(dimension_semantics=sem, vmem_limit_bytes=VMEM_LIMIT)


_ORDER_AFTER = []


def _after(*arrays):
    _ORDER_AFTER.extend(arrays)


def _pcall(body, *, in_specs, **kw):
    deps = tuple(_ORDER_AFTER)
    _ORDER_AFTER.clear()
    if not deps:
        return pl.pallas_call(body, in_specs=in_specs, **kw)
    n_in, n_dep = len(in_specs), len(deps)

    def with_deps(*refs):
        body(*refs[:n_in], *refs[n_in + n_dep:])

    call = pl.pallas_call(with_deps, in_specs=list(in_specs) + [pl.BlockSpec(memory_space=pl.ANY)] * n_dep, **kw)
    return lambda *args: call(*args, *deps)


def _all_gather(arrs, name):
    n = len(arrs)

    def body(*refs):
        ins = refs[:n]
        outs = refs[n:2 * n]
        send_sems, recv_sems, local_sems = refs[2 * n:]
        x, y, c = lax.axis_index("x"), lax.axis_index("y"), lax.axis_index("c")
        me, sibling = (x, y, c), (x, y, 1 - c)
        chips = [(1 - x, y), (x, 1 - y), (1 - x, 1 - y)]

        def rows(a, dev):
            px, py, pc = dev
            return outs[a].at[4 * px + 2 * py + pc]

        def copy(a, k, block, to, src=None):
            return pltpu.make_async_remote_copy(
                src_ref=rows(a, block) if src is None else src,
                dst_ref=rows(a, block),
                send_sem=send_sems.at[7 * a + k],
                recv_sem=recv_sems.at[7 * a + k],
                device_id=to,
                device_id_type=MESH_ID,
            )

        mine = [pltpu.make_async_copy(ins[a], rows(a, me), local_sems.at[a]) for a in range(n)]
        for cp in mine:
            cp.start()
        first = []
        for a in range(n):
            first.append(copy(a, 0, me, sibling, src=ins[a]))
            first += [copy(a, 1 + j, me, (*chip, c), src=ins[a]) for j, chip in enumerate(chips)]
        for cp in first:
            cp.start()
        passed = []
        for j, chip in enumerate(chips):
            for a in range(n):
                copy(a, 1 + j, (*chip, c), me).wait_recv()
                fwd = copy(a, 4 + j, (*chip, c), sibling)
                fwd.start()
                passed.append(fwd)
        for a in range(n):
            copy(a, 0, sibling, me).wait_recv()
            for j, chip in enumerate(chips):
                copy(a, 4 + j, (*chip, 1 - c), me).wait_recv()
        for cp in first + passed:
            cp.wait_send()
        for cp in mine:
            cp.wait()

    any_spec = pl.BlockSpec(memory_space=pl.ANY)
    outs = _pcall(
        body,
        name=name,
        out_shape=[jax.ShapeDtypeStruct((N_DEV,) + a.shape, a.dtype) for a in arrs],
        in_specs=[any_spec] * n,
        out_specs=[any_spec] * n,
        scratch_shapes=[
            pltpu.SemaphoreType.DMA((7 * n,)),
            pltpu.SemaphoreType.DMA((7 * n,)),
            pltpu.SemaphoreType.DMA((n,)),
        ],
    )(*arrs)
    return list(outs)


def _all_to_all(arrs, name):
    n = len(arrs)

    def body(*refs):
        ins = refs[:n]
        outs = refs[n:2 * n]
        send_sems, recv_sems, local_sems = refs[2 * n:]
        x, y, c = lax.axis_index("x"), lax.axis_index("y"), lax.axis_index("c")
        my_idx = 4 * x + 2 * y + c

        def peer(k):
            fx, fy, fc = (k >> 2) & 1, (k >> 1) & 1, k & 1
            return (x ^ fx if fx else x, y ^ fy if fy else y, c ^ fc if fc else c)

        def copy(a, k):
            px, py, pc = peer(k)
            return pltpu.make_async_remote_copy(
                src_ref=ins[a].at[4 * px + 2 * py + pc],
                dst_ref=outs[a].at[my_idx],
                send_sem=send_sems.at[7 * a + k - 1],
                recv_sem=recv_sems.at[7 * a + k - 1],
                device_id=(px, py, pc),
                device_id_type=MESH_ID,
            )

        mine = [pltpu.make_async_copy(ins[a].at[my_idx], outs[a].at[my_idx], local_sems.at[a]) for a in range(n)]
        for cp in mine:
            cp.start()
        order = [1, 4, 2, 5, 3, 6, 7]
        cps = [copy(a, k) for k in order for a in range(n)]
        for cp in cps:
            cp.start()
        for cp in cps:
            cp.wait()
        for cp in mine:
            cp.wait()

    any_spec = pl.BlockSpec(memory_space=pl.ANY)
    outs = _pcall(
        body,
        name=name,
        out_shape=[jax.ShapeDtypeStruct(a.shape, a.dtype) for a in arrs],
        in_specs=[any_spec] * n,
        out_specs=[any_spec] * n,
        scratch_shapes=[
            pltpu.SemaphoreType.DMA((7 * n,)),
            pltpu.SemaphoreType.DMA((7 * n,)),
            pltpu.SemaphoreType.DMA((n,)),
        ],
    )(*arrs)
    return list(outs)


_HBM = pl.BlockSpec(memory_space=pltpu.HBM)
_SEM = pl.BlockSpec(memory_space=pltpu.SEMAPHORE)
_EFFECT = pltpu.SideEffectType.DATAFLOW_SIDE_EFFECTING


def _descriptors(copies, send_sems, recv_sems):
    descs = []
    for i, (src, dst, dev) in enumerate(copies):
        if dev is None:
            descs.append(pltpu.make_async_copy(src, dst, recv_sems.at[i]))
        else:
            descs.append(pltpu.make_async_remote_copy(src_ref=src, dst_ref=dst, send_sem=send_sems.at[i],
                                                      recv_sem=recv_sems.at[i], device_id=dev, device_id_type=MESH_ID))
    return descs


def _split_start(name, arrays, copies_fn, n_copies):
    n = len(arrays)

    def body(*refs):
        send_sems, recv_sems = refs[n], refs[n + 1]
        token = refs[2 * n + 2]
        for dsc in _descriptors(copies_fn(refs[:n]), send_sems, recv_sems):
            dsc.start()
        token[...] = jnp.zeros_like(token)

    outs = _pcall(
        body,
        name=name,
        out_shape=(pltpu.SemaphoreType.DMA((n_copies,)), pltpu.SemaphoreType.DMA((n_copies,)),
                   *[pltpu.HBM(a.shape, a.dtype) for a in arrays], jax.ShapeDtypeStruct((8, LANE), F32)),
        in_specs=[_HBM] * n,
        out_specs=(_SEM, _SEM, *[_HBM] * n, pl.BlockSpec(memory_space=pltpu.VMEM)),
        input_output_aliases={i: 2 + i for i in range(n)},
        compiler_params=pltpu.CompilerParams(has_side_effects=_EFFECT),
    )(*[pltpu.with_memory_space_constraint(a, pltpu.HBM) for a in arrays])
    return outs[0], outs[1], list(outs[2:2 + n]), outs[2 + n]


def _split_wait(name, send_sems, recv_sems, arrays, copies_fn, after):
    n = len(arrays)

    def body(*refs):
        for dsc, (_, _, dev) in zip(_descriptors(copies_fn(refs[:n]), refs[n], refs[n + 1]), copies_fn(refs[:n])):
            if dev is None:
                dsc.wait()
            else:
                dsc.wait_send()
                dsc.wait_recv()

    outs = _pcall(
        body,
        name=name,
        out_shape=tuple(pltpu.HBM(a.shape, a.dtype) for a in arrays),
        in_specs=[_HBM] * n + [_SEM, _SEM, pl.BlockSpec(memory_space=pl.ANY)],
        out_specs=tuple([_HBM] * n),
        input_output_aliases={i: i for i in range(n)},
        compiler_params=pltpu.CompilerParams(has_side_effects=_EFFECT),
    )(*arrays, send_sems, recv_sems, after)
    return list(outs)


def _mesh_pos():
    x, y, c = lax.axis_index("x"), lax.axis_index("y"), lax.axis_index("c")
    return x, y, c, [(1 - x, y), (x, 1 - y), (1 - x, 1 - y)]


def _gather_ici_copies(n):
    def copies(refs):
        x, y, c, chips = _mesh_pos()
        me = 4 * x + 2 * y + c
        out = []
        for a in range(n):
            src, buf = refs[a], refs[n + a]
            out.append((src, buf.at[me], None))
            out.append((src, buf.at[me], (x, y, 1 - c)))
            out += [(src, buf.at[me], (cx, cy, c)) for cx, cy in chips]
        return out
    return copies


def _gather_d2d_copies(n):
    def copies(refs):
        x, y, c, chips = _mesh_pos()
        out = []
        for a in range(n):
            for cx, cy in chips:
                rows = refs[a].at[4 * cx + 2 * cy + c]
                out.append((rows, rows, (x, y, 1 - c)))
        return out
    return copies


def _reduce_d2d_copies(n):
    def copies(refs):
        x, y, c, _ = _mesh_pos()
        out = []
        for a in range(n):
            for k in range(4):
                out.append((refs[a].at[2 * k + (1 - c)], refs[n + a].at[k], (x, y, 1 - c)))
        return out
    return copies


def _reduce_ici_copies(n):
    def copies(refs):
        x, y, c, chips = _mesh_pos()
        mine = 2 * x + y
        out = []
        for a in range(n):
            src, land = refs[a], refs[n + a]
            out.append((src.at[mine], land.at[mine], None))
            out += [(src.at[2 * cx + cy], land.at[mine], (cx, cy, c)) for cx, cy in chips]
        return out
    return copies


def _pair_sum(send, land, c_idx, name):
    _, r, cols = send.shape
    rb = _tile(r, max(8, (1 << 22) // (send.dtype.itemsize * cols) // 8 * 8), 8)
    dt = send.dtype

    def body(c_ref, s_ref, l_ref, o_ref):
        o_ref[...] = (s_ref[...].astype(F32) + l_ref[...].astype(F32)).astype(dt)

    return pl.pallas_call(
        body,
        name=name,
        out_shape=jax.ShapeDtypeStruct((4, r, cols), dt),
        grid_spec=pltpu.PrefetchScalarGridSpec(
            num_scalar_prefetch=1,
            grid=(4, r // rb),
            in_specs=[pl.BlockSpec((None, rb, cols), lambda k, i, c_ref: (2 * k + c_ref[0], i, 0)),
                      pl.BlockSpec((None, rb, cols), lambda k, i, c_ref: (k, i, 0))],
            out_specs=pl.BlockSpec((None, rb, cols), lambda k, i, c_ref: (k, i, 0)),
        ),
        compiler_params=_cparams(("parallel", "parallel")),
    )(c_idx, send, land)


_DIMS = {
    "nn": (((1,), (0,)), ((), ())),
    "nt": (((1,), (1,)), ((), ())),
    "tn": (((0,), (0,)), ((), ())),
}


def _mm_call(a, b, *, mode, grid, a_spec, b_spec, o_spec, out_shape, acc_shape, name):
    nk = grid[2]
    out_dtype = out_shape.dtype

    def body(a_ref, b_ref, o_ref, *scratch):
        p = lax.dot_general(a_ref[...].astype(BF16), b_ref[...].astype(BF16), _DIMS[mode],
                            preferred_element_type=F32)
        if nk == 1:
            o_ref[...] = p.astype(out_dtype)
        else:
            acc = scratch[0]
            k = pl.program_id(2)

            @pl.when(k == 0)
            def _():
                acc[...] = p

            @pl.when(k > 0)
            def _():
                acc[...] += p

            @pl.when(k == nk - 1)
            def _():
                o_ref[...] = acc[...].astype(out_dtype)

    return _pcall(
        body,
        name=name,
        out_shape=out_shape,
        grid=grid,
        in_specs=[a_spec, b_spec],
        out_specs=o_spec,
        scratch_shapes=[pltpu.VMEM(acc_shape, F32)] if nk > 1 else [],
        compiler_params=_cparams(("parallel", "parallel", "arbitrary")),
    )(a, b)


def _mm(a, b, mode, out_dtype, name, tm=512, tn=512, tk=2432, a_row_off=0, rows=None):
    if mode == "nn":
        (m, k), (k2, n) = a.shape, b.shape
    elif mode == "nt":
        (m, k), (n, k2) = a.shape, b.shape
    else:
        (k, m), (k2, n) = a.shape, b.shape
        if rows is not None:
            k = k2 = rows
    assert k == k2, (a.shape, b.shape, mode)
    if mode != "tn":
        m = (m if rows is None else rows + a_row_off) - a_row_off
    tm, tn, tk = _tile(m, tm, 8), _tile(n, tn), _tile(k, tk, 8 if mode == "tn" else LANE)
    assert a_row_off % tm == 0
    ro = a_row_off // tm
    grid = (m // tm, n // tn, k // tk)
    if mode == "tn":
        a_spec = pl.BlockSpec((tk, tm), lambda i, j, kk: (kk, i))
    else:
        a_spec = pl.BlockSpec((tm, tk), lambda i, j, kk: (i + ro, kk))
    if mode == "nt":
        b_spec = pl.BlockSpec((tn, tk), lambda i, j, kk: (j, kk))
    else:
        b_spec = pl.BlockSpec((tk, tn), lambda i, j, kk: (kk, j))
    o_spec = pl.BlockSpec((tm, tn), lambda i, j, kk: (i, j))
    return _mm_call(a, b, mode=mode, grid=grid, a_spec=a_spec, b_spec=b_spec, o_spec=o_spec,
                    out_shape=jax.ShapeDtypeStruct((m, n), out_dtype), acc_shape=(tm, tn), name=name)


def _mm_cat_nt(pieces, out_dtype, name, tm=1024, tn=1024, tk=2048, rows=None):
    m = pieces[0][0].shape[0] if rows is None else rows
    n = pieces[0][1].shape[0]
    tm, tn = _tile(m, tm, 8), _tile(n, tn)
    steps, starts, s = [], [], 0
    for a, b, off in pieces:
        kp = a.shape[1]
        tkp = _tile(kp, tk)
        assert off % tkp == 0 and b.shape[0] == n
        steps.append((tkp, kp // tkp, off // tkp))
        starts.append(s)
        s += kp // tkp
    nk = s
    npc = len(pieces)

    def body(*refs):
        o_ref, acc = refs[2 * npc], refs[2 * npc + 1]
        kk = pl.program_id(2)

        @pl.when(kk == 0)
        def _():
            acc[...] = jnp.zeros_like(acc)

        for p in range(npc):
            @pl.when((kk >= starts[p]) & (kk < starts[p] + steps[p][1]))
            def _(p=p):
                acc[...] += lax.dot_general(refs[2 * p][...].astype(BF16), refs[2 * p + 1][...].astype(BF16), _DIMS["nt"],
                                            preferred_element_type=F32)

        @pl.when(kk == nk - 1)
        def _():
            o_ref[...] = acc[...].astype(out_dtype)

    in_specs, args = [], []
    for p, (a, b, off) in enumerate(pieces):
        tkp, np_, ob = steps[p]

        def rel(kk, p=p, np_=np_):
            return jnp.clip(kk - starts[p], 0, np_ - 1)

        in_specs.append(pl.BlockSpec((tm, tkp), lambda i, j, kk, rel=rel: (i, rel(kk))))
        in_specs.append(pl.BlockSpec((tn, tkp), lambda i, j, kk, rel=rel, ob=ob: (j, ob + rel(kk))))
        args += [a, b]
    return _pcall(
        body,
        name=name,
        out_shape=jax.ShapeDtypeStruct((m, n), out_dtype),
        grid=(m // tm, n // tn, nk),
        in_specs=in_specs,
        out_specs=pl.BlockSpec((tm, tn), lambda i, j, kk: (i, j)),
        scratch_shapes=[pltpu.VMEM((tm, tn), F32)],
        compiler_params=_cparams(("parallel", "parallel", "arbitrary")),
    )(*args)


def _mm_cat_tn(a, pieces, out_dtype, name, tm=1024, tn=1024, rows=None):
    k = a.shape[0] if rows is None else rows
    m = a.shape[1]
    tm = _tile(m, tm)
    starts, s = [], 0
    for b in pieces:
        assert b.shape[1] % tn == 0
        starts.append(s)
        s += b.shape[1] // tn
    nj = s
    npc = len(pieces)

    def body(*refs):
        a_ref, o_ref = refs[0], refs[1 + npc]
        j = pl.program_id(1)
        for p in range(npc):
            @pl.when((j >= starts[p]) & (j < starts[p] + pieces[p].shape[1] // tn))
            def _(p=p):
                o_ref[...] = lax.dot_general(a_ref[...].astype(BF16), refs[1 + p][...].astype(BF16), _DIMS["tn"],
                                             preferred_element_type=F32).astype(out_dtype)

    in_specs = [pl.BlockSpec((k, tm), lambda i, j: (0, i))]
    for p, b in enumerate(pieces):
        np_ = b.shape[1] // tn
        in_specs.append(pl.BlockSpec((k, tn), lambda i, j, p=p, np_=np_: (0, jnp.clip(j - starts[p], 0, np_ - 1))))
    return _pcall(
        body,
        name=name,
        out_shape=jax.ShapeDtypeStruct((m, nj * tn), out_dtype),
        grid=(m // tm, nj),
        in_specs=in_specs,
        out_specs=pl.BlockSpec((tm, tn), lambda i, j: (i, j)),
        compiler_params=_cparams(("parallel", "arbitrary")),
    )(a, *pieces)


def _mm_up_fwd(z2, w3, name, tm=1024):
    t, d = z2.shape
    nsh, _, c = w3.shape
    tm = _tile(t, tm, 8)
    return _mm_call(z2, w3, mode="nn", grid=(t // tm, nsh, 1),
                    a_spec=pl.BlockSpec((tm, d), lambda i, j, kk: (i, 0)),
                    b_spec=pl.BlockSpec((None, d, c), lambda i, j, kk: (j, 0, 0)),
                    o_spec=pl.BlockSpec((tm, c), lambda i, j, kk: (i, j)),
                    out_shape=jax.ShapeDtypeStruct((t, nsh * c), F32), acc_shape=(tm, c), name=name)


def _mm_up_dz(du3, w3, name, tm=512, tn=1024):
    _, t, f = du3.shape
    nsh, d, c = w3.shape
    half = nsh // 2
    assert f == half * c
    tm, tn = _tile(t, tm, 8), _tile(d, tn)

    def body(a_ref, b_ref, o_ref, acc):
        kk = pl.program_id(2)
        p = None
        for s in range(half):
            q = lax.dot_general(a_ref[:, s * c:(s + 1) * c], b_ref[s], _DIMS["nt"], preferred_element_type=F32)
            p = q if p is None else p + q

        @pl.when(kk == 0)
        def _():
            acc[...] = p

        @pl.when(kk == 1)
        def _():
            o_ref[...] = acc[...] + p

    return _pcall(
        body,
        name=name,
        out_shape=jax.ShapeDtypeStruct((t, d), F32),
        grid=(t // tm, d // tn, 2),
        in_specs=[pl.BlockSpec((None, tm, f), lambda i, j, kk: (kk, i, 0)),
                  pl.BlockSpec((half, tn, c), lambda i, j, kk: (kk, j, 0))],
        out_specs=pl.BlockSpec((tm, tn), lambda i, j, kk: (i, j)),
        scratch_shapes=[pltpu.VMEM((tm, tn), F32)],
        compiler_params=_cparams(("parallel", "parallel", "arbitrary")),
    )(du3, w3)


def _mm_sum_nt(pieces, out_dtype, name, tm=512, tn=512, rows=None):
    m = pieces[0][0].shape[0] if rows is None else rows
    n = pieces[0][2].shape[0]
    tm, tn = _tile(m, tm, 8), _tile(n, tn)
    npc = len(pieces)

    def body(*refs):
        p = None
        for s in range(npc):
            q = lax.dot_general(refs[2 * s][...].astype(BF16), refs[2 * s + 1][...].astype(BF16), _DIMS["nt"],
                                preferred_element_type=F32)
            p = q if p is None else p + q
        refs[2 * npc][...] = p.astype(out_dtype)

    in_specs, args = [], []
    for a, ao, b, bo, kp in pieces:
        assert ao % kp == 0 and bo % kp == 0 and b.shape[0] == n
        in_specs.append(pl.BlockSpec((tm, kp), lambda i, j, ab=ao // kp: (i, ab)))
        in_specs.append(pl.BlockSpec((tn, kp), lambda i, j, bb=bo // kp: (j, bb)))
        args += [a, b]
    return _pcall(
        body,
        name=name,
        out_shape=jax.ShapeDtypeStruct((m, n), out_dtype),
        grid=(m // tm, n // tn),
        in_specs=in_specs,
        out_specs=pl.BlockSpec((tm, tn), lambda i, j: (i, j)),
        compiler_params=_cparams(("parallel", "parallel")),
    )(*args)


def _mm_up_gw(z2, du3, nsh, name, tm=1024):
    t, d = z2.shape
    f = du3.shape[2]
    half = nsh // 2
    c = f // half
    tm = _tile(d, tm)
    return _mm_call(z2, du3, mode="tn", grid=(d // tm, nsh, 1),
                    a_spec=pl.BlockSpec((t, tm), lambda i, j, kk: (0, i)),
                    b_spec=pl.BlockSpec((None, t, c), lambda i, j, kk: (j // half, 0, j % half)),
                    o_spec=pl.BlockSpec((None, tm, c), lambda i, j, kk: (j, i, 0)),
                    out_shape=jax.ShapeDtypeStruct((nsh, d, c), BF16), acc_shape=(tm, c), name=name)


def _rms(x):
    r = lax.rsqrt(jnp.mean(x * x, axis=-1, keepdims=True) + NORM_EPS)
    return x * r, r


def _rms_bwd(dxh, xh, r):
    return r * (dxh - xh * jnp.mean(dxh * xh, axis=-1, keepdims=True))


def _colsum(v):
    return jnp.sum(v, axis=0, keepdims=True)


def _rope(v, c, s1, s2, q):
    w = v.shape[-1]
    return v * c + pltpu.roll(v, w - q, 1) * s1 + pltpu.roll(v, q, 1) * s2


def _rope_t(d, c, s1, s2, q):
    w = d.shape[-1]
    return d * c + pltpu.roll(d * s1, q, 1) + pltpu.roll(d * s2, w - q, 1)


def _norm_mod_fwd(ctx, x, gain, mods):
    tc, d = ctx.shape
    t = x.shape[0]
    rb = min(ROW_BLOCK, tc)
    nbl = t // rb

    def body(ctx_ref, x_ref, g_ref, mod_ref, z_ref):
        i = pl.program_id(0)

        def emit(src, sh, sc):
            xh, _ = _rms(src[...])
            z_ref[...] = ((xh * g_ref[...]) * (1.0 + sc) + sh).astype(BF16)

        @pl.when(i >= nbl)
        def _():
            emit(ctx_ref, mod_ref[2:3, :], mod_ref[3:4, :])

        @pl.when(i < nbl)
        def _():
            emit(x_ref, mod_ref[0:1, :], mod_ref[1:2, :])

    return _pcall(
        body,
        name="norm1_mod_fwd",
        out_shape=jax.ShapeDtypeStruct((tc + t, d), BF16),
        grid=((tc + t) // rb,),
        in_specs=[
            pl.BlockSpec((rb, d), lambda i: (jnp.maximum(i - nbl, 0), 0)),
            pl.BlockSpec((rb, d), lambda i: (jnp.minimum(i, nbl - 1), 0)),
            pl.BlockSpec((1, d), lambda i: (0, 0)),
            pl.BlockSpec((8, d), lambda i: (0, 0)),
        ],
        out_specs=pl.BlockSpec((rb, d), lambda i: (i, 0)),
        compiler_params=_cparams(("arbitrary",)),
    )(ctx, x, gain, mods)


def _norm1_bwd(ctx, x, gain, mods, dz_ctx, dz_lat, dx1):
    tc, d = ctx.shape
    t = x.shape[0]
    rb = min(ROW_BLOCK, tc)
    nbl = t // rb

    def body(ctx_ref, x_ref, g_ref, mod_ref, dzc_ref, dzl_ref, dx1_ref, gx_ref, st_ref):
        i = pl.program_id(0)

        @pl.when(i == 0)
        def _():
            st_ref[...] = jnp.zeros_like(st_ref)

        def common(src, dz, sc, row_sh, row_sc):
            xh, r = _rms(src[...])
            g = g_ref[...]
            dxn = dz * (1.0 + sc)
            st_ref[row_sh:row_sh + 1, :] += _colsum(dz)
            st_ref[row_sc:row_sc + 1, :] += _colsum(dz * (xh * g))
            st_ref[2:3, :] += _colsum(dxn * xh)
            return _rms_bwd(dxn * g, xh, r)

        @pl.when(i >= nbl)
        def _():
            common(ctx_ref, dzc_ref[...], mod_ref[3:4, :], 3, 4)

        @pl.when(i < nbl)
        def _():
            gx_ref[...] = dx1_ref[...] + common(x_ref, dzl_ref[...], mod_ref[1:2, :], 0, 1)

    lat = lambda i: (jnp.minimum(i, nbl - 1), 0)
    cix = lambda i: (jnp.maximum(i - nbl, 0), 0)
    return _pcall(
        body,
        name="norm1_mod_bwd",
        out_shape=[jax.ShapeDtypeStruct((t, d), F32), jax.ShapeDtypeStruct((8, d), F32)],
        grid=((tc + t) // rb,),
        in_specs=[
            pl.BlockSpec((rb, d), cix),
            pl.BlockSpec((rb, d), lat),
            pl.BlockSpec((1, d), lambda i: (0, 0)),
            pl.BlockSpec((8, d), lambda i: (0, 0)),
            pl.BlockSpec((rb, d), cix),
            pl.BlockSpec((rb, d), lat),
            pl.BlockSpec((rb, d), lat),
        ],
        out_specs=[pl.BlockSpec((rb, d), lat), pl.BlockSpec((8, d), lambda i: (0, 0))],
        compiler_params=_cparams(("arbitrary",)),
    )(ctx, x, gain, mods, dz_ctx, dz_lat, dx1)


def _key_prep_fwd(kv, kv_gain, kb_gain, tabs):
    ta, wkv = kv.shape
    kvl = MLA_KV_LORA
    nb = GQA_KV_HEADS * GQA_HEAD_DIM
    rb = ROW_BLOCK if ta % ROW_BLOCK == 0 else LANE
    hd = GQA_HEAD_DIM

    def body(kv_ref, g_ref, gb_ref, ca, s1a, s2a, cb, s1b, s2b, kin_ref, kb_ref, vb_ref):
        xh, _ = _rms(kv_ref[:, 0:kvl])
        kin_ref[:, 0:kvl] = (xh * g_ref[...]).astype(BF16)
        kpe = kv_ref[:, kvl + 2 * nb:kvl + 2 * nb + LANE]
        kin_ref[:, kvl:kvl + LANE] = _rope(kpe, ca[...], s1a[...], s2a[...], MLA_ROPE // 4).astype(BF16)
        for h in range(GQA_KV_HEADS):
            nh, _ = _rms(kv_ref[:, kvl + h * hd:kvl + (h + 1) * hd])
            kb_ref[:, h * hd:(h + 1) * hd] = _rope(nh * gb_ref[...], cb[...], s1b[...], s2b[...], hd // 4).astype(BF16)
        vb_ref[...] = kv_ref[:, kvl + nb:kvl + 2 * nb].astype(BF16)

    row = lambda w: pl.BlockSpec((rb, w), lambda i: (i, 0))
    fix = lambda w: pl.BlockSpec((1, w), lambda i: (0, 0))
    return _pcall(
        body,
        name="key_prep_fwd",
        out_shape=[jax.ShapeDtypeStruct((ta, kvl + LANE), BF16), jax.ShapeDtypeStruct((ta, nb), BF16),
                   jax.ShapeDtypeStruct((ta, nb), BF16)],
        grid=(ta // rb,),
        in_specs=[row(wkv), fix(kvl), fix(hd)] + [row(LANE)] * 3 + [row(hd)] * 3,
        out_specs=[row(kvl + LANE), row(nb), row(nb)],
        compiler_params=_cparams(("parallel",)),
    )(kv, kv_gain, kb_gain, *tabs)


def _key_prep_bwd(kv, kv_gain, kb_gain, tabs, dkin, dkb, dvb):
    ta, wkv = kv.shape
    kvl = MLA_KV_LORA
    nb = GQA_KV_HEADS * GQA_HEAD_DIM
    rb = ROW_BLOCK if ta % ROW_BLOCK == 0 else LANE
    hd = GQA_HEAD_DIM

    def body(kv_ref, g_ref, gb_ref, ca, s1a, s2a, cb, s1b, s2b, dkin_ref, dkb_ref, dvb_ref, dkv_ref, st_ref, stb_ref):
        @pl.when(pl.program_id(0) == 0)
        def _():
            st_ref[...] = jnp.zeros_like(st_ref)
            stb_ref[...] = jnp.zeros_like(stb_ref)

        xh, r = _rms(kv_ref[:, 0:kvl])
        dn = dkin_ref[:, 0:kvl]
        st_ref[0:1, :] += _colsum(dn * xh)
        dkv_ref[:, 0:kvl] = _rms_bwd(dn * g_ref[...], xh, r).astype(BF16)
        dpe = _rope_t(dkin_ref[:, kvl:kvl + LANE], ca[...], s1a[...], s2a[...], MLA_ROPE // 4)
        dkv_ref[:, kvl + 2 * nb:kvl + 2 * nb + LANE] = dpe.astype(BF16)
        for h in range(GQA_KV_HEADS):
            nh, rh = _rms(kv_ref[:, kvl + h * hd:kvl + (h + 1) * hd])
            dn_h = _rope_t(dkb_ref[:, h * hd:(h + 1) * hd], cb[...], s1b[...], s2b[...], hd // 4)
            stb_ref[0:1, :] += _colsum(dn_h * nh)
            dkv_ref[:, kvl + h * hd:kvl + (h + 1) * hd] = _rms_bwd(dn_h * gb_ref[...], nh, rh).astype(BF16)
        dkv_ref[:, kvl + nb:kvl + 2 * nb] = dvb_ref[...].astype(BF16)

    row = lambda w: pl.BlockSpec((rb, w), lambda i: (i, 0))
    fix = lambda w: pl.BlockSpec((1, w), lambda i: (0, 0))
    return _pcall(
        body,
        name="key_prep_bwd",
        out_shape=[jax.ShapeDtypeStruct((ta, wkv), BF16), jax.ShapeDtypeStruct((8, kvl), F32),
                   jax.ShapeDtypeStruct((8, hd), F32)],
        grid=(ta // rb,),
        in_specs=[row(wkv), fix(kvl), fix(hd)] + [row(LANE)] * 3 + [row(hd)] * 3 + [row(kvl + LANE), row(nb), row(nb)],
        out_specs=[row(wkv), pl.BlockSpec((8, kvl), lambda i: (0, 0)), pl.BlockSpec((8, hd), lambda i: (0, 0))],
        compiler_params=_cparams(("arbitrary",)),
    )(kv, kv_gain, kb_gain, *tabs, dkin, dkb, dvb)


def _q_prep_fwd(qg, q_gain, qb_gain, tabs, qscale):
    t = qg.shape[0]
    ql = MLA_Q_LORA
    hd = GQA_HEAD_DIM
    hb = GQA_HEADS * hd
    rb = min(ROW_BLOCK, t)

    def body(q_ref, g_ref, gb_ref, cb, s1b, s2b, cqn_ref, qb_ref):
        xh, _ = _rms(q_ref[:, 0:ql])
        cqn_ref[...] = (xh * g_ref[...]).astype(BF16)
        for h in range(GQA_HEADS):
            nh, _ = _rms(q_ref[:, ql + h * hd:ql + (h + 1) * hd])
            qh = _rope(nh * gb_ref[...], cb[...], s1b[...], s2b[...], hd // 4)
            qb_ref[:, h * hd:(h + 1) * hd] = (qh * qscale).astype(BF16)

    row = lambda w: pl.BlockSpec((rb, w), lambda i: (i, 0))
    fix = lambda w: pl.BlockSpec((1, w), lambda i: (0, 0))
    return _pcall(
        body,
        name="q_prep_fwd",
        out_shape=[jax.ShapeDtypeStruct((t, ql), BF16), jax.ShapeDtypeStruct((t, hb), BF16)],
        grid=(t // rb,),
        in_specs=[row(ql + hb), fix(ql), fix(hd)] + [row(hd)] * 3,
        out_specs=[row(ql), row(hb)],
        compiler_params=_cparams(("parallel",)),
    )(qg, q_gain, qb_gain, *tabs)


def _q_prep_bwd(qg, q_gain, qb_gain, tabs, dcqn, dqb, wpad, qscale):
    t = qg.shape[0]
    ql = MLA_Q_LORA
    hd = GQA_HEAD_DIM
    hb = GQA_HEADS * hd
    rb = min(ROW_BLOCK, t)

    def body(q_ref, g_ref, gb_ref, cb, s1b, s2b, dcqn_ref, dqb_ref, dq_ref, st_ref, stb_ref):
        @pl.when(pl.program_id(0) == 0)
        def _():
            st_ref[...] = jnp.zeros_like(st_ref)
            stb_ref[...] = jnp.zeros_like(stb_ref)

        xh, r = _rms(q_ref[:, 0:ql])
        dn = dcqn_ref[...]
        st_ref[0:1, :] += _colsum(dn * xh)
        dq_ref[:, 0:ql] = _rms_bwd(dn * g_ref[...], xh, r).astype(BF16)
        for h in range(GQA_HEADS):
            nh, rh = _rms(q_ref[:, ql + h * hd:ql + (h + 1) * hd])
            dn_h = _rope_t(dqb_ref[:, h * hd:(h + 1) * hd] * qscale, cb[...], s1b[...], s2b[...], hd // 4)
            stb_ref[0:1, :] += _colsum(dn_h * nh)
            dq_ref[:, ql + h * hd:ql + (h + 1) * hd] = _rms_bwd(dn_h * gb_ref[...], nh, rh).astype(BF16)
        if wpad:
            dq_ref[:, ql + hb:ql + hb + wpad] = jnp.zeros((rb, wpad), BF16)

    row = lambda w: pl.BlockSpec((rb, w), lambda i: (i, 0))
    fix = lambda w: pl.BlockSpec((1, w), lambda i: (0, 0))
    return _pcall(
        body,
        name="q_prep_bwd",
        out_shape=[jax.ShapeDtypeStruct((t, ql + hb + wpad), BF16), jax.ShapeDtypeStruct((8, ql), F32),
                   jax.ShapeDtypeStruct((8, hd), F32)],
        grid=(t // rb,),
        in_specs=[row(ql + hb), fix(ql), fix(hd)] + [row(hd)] * 3 + [row(ql), row(hb)],
        out_specs=[row(ql + hb + wpad), pl.BlockSpec((8, ql), lambda i: (0, 0)), pl.BlockSpec((8, hd), lambda i: (0, 0))],
        compiler_params=_cparams(("arbitrary",)),
    )(qg, q_gain, qb_gain, *tabs, dcqn, dqb)


def _rope_a(v, tabs, transpose, out_dtype, name, qscale):
    t, w = v.shape
    rb = min(ROW_BLOCK, t)
    fn = _rope_t if transpose else _rope

    def body(v_ref, c, s1, s2, o_ref):
        for h in range(w // MLA_SLOT):
            sl = slice(h * MLA_SLOT, (h + 1) * MLA_SLOT)
            o_ref[:, sl] = (fn(v_ref[:, sl].astype(F32), c[...], s1[...], s2[...], MLA_ROPE // 4) * qscale).astype(out_dtype)

    row = lambda ww: pl.BlockSpec((rb, ww), lambda i: (i, 0))
    return _pcall(
        body,
        name=name,
        out_shape=jax.ShapeDtypeStruct((t, w), out_dtype),
        grid=(t // rb,),
        in_specs=[row(w)] + [row(MLA_SLOT)] * 3,
        out_specs=row(w),
        compiler_params=_cparams(("parallel",)),
    )(v, *tabs)


def _merge_fwd(pa, pb, qg, gate_blk):
    t, d = pa.shape
    rb = min(ROW_BLOCK, t)

    def body(pa_ref, pb_ref, ga_ref, gb_ref, o_ref):
        o_ref[...] = (jax.nn.sigmoid(ga_ref[...]) * pa_ref[...] + jax.nn.sigmoid(gb_ref[...]) * pb_ref[...]).astype(BF16)

    row = pl.BlockSpec((rb, d), lambda i: (i, 0))
    return _pcall(
        body,
        name="merge_fwd",
        out_shape=jax.ShapeDtypeStruct((t, d), BF16),
        grid=(t // rb,),
        in_specs=[row, row, pl.BlockSpec((rb, d), lambda i: (i, gate_blk)), pl.BlockSpec((rb, d), lambda i: (i, gate_blk + 1))],
        out_specs=row,
        compiler_params=_cparams(("parallel",)),
    )(pa, pb, qg, qg)


def _merge_bwd(dm, pa, pb, qg, gate_blk):
    t, d = pa.shape
    rb = min(ROW_BLOCK, t)

    def body(dm_ref, pa_ref, pb_ref, ga_ref, gb_ref, dpa_ref, dpb_ref, dg_ref):
        dmv = dm_ref[...]
        sa = jax.nn.sigmoid(ga_ref[...])
        sb = jax.nn.sigmoid(gb_ref[...])
        dpa_ref[...] = (dmv * sa).astype(BF16)
        dpb_ref[...] = (dmv * sb).astype(BF16)
        dg_ref[:, 0:d] = (dmv * pa_ref[...] * (sa * (1.0 - sa))).astype(BF16)
        dg_ref[:, d:2 * d] = (dmv * pb_ref[...] * (sb * (1.0 - sb))).astype(BF16)

    row = pl.BlockSpec((rb, d), lambda i: (i, 0))
    return _pcall(
        body,
        name="merge_bwd",
        out_shape=[jax.ShapeDtypeStruct((t, d), BF16), jax.ShapeDtypeStruct((t, d), BF16),
                   jax.ShapeDtypeStruct((t, 2 * d), BF16)],
        grid=(t // rb,),
        in_specs=[row, row, row, pl.BlockSpec((rb, d), lambda i: (i, gate_blk)), pl.BlockSpec((rb, d), lambda i: (i, gate_blk + 1))],
        out_specs=[row, row, pl.BlockSpec((rb, 2 * d), lambda i: (i, 0))],
        compiler_params=_cparams(("parallel",)),
    )(dm, pa, pb, qg, qg)


def _resid_norm_mod(x, branch, gain, mods, name):
    t, d = x.shape
    rb = min(ROW_BLOCK, t)

    def body(x_ref, b_ref, g_ref, mod_ref, x1_ref, z_ref):
        x1 = x_ref[...] + mod_ref[0:1, :] * b_ref[...]
        x1_ref[...] = x1
        xh, _ = _rms(x1)
        z_ref[...] = ((xh * g_ref[...]) * (1.0 + mod_ref[2:3, :]) + mod_ref[1:2, :]).astype(BF16)

    row = pl.BlockSpec((rb, d), lambda i: (i, 0))
    return _pcall(
        body,
        name=name,
        out_shape=[jax.ShapeDtypeStruct((t, d), F32), jax.ShapeDtypeStruct((t, d), BF16)],
        grid=(t // rb,),
        in_specs=[row, row, pl.BlockSpec((1, d), lambda i: (0, 0)), pl.BlockSpec((8, d), lambda i: (0, 0))],
        out_specs=[row, row],
        compiler_params=_cparams(("parallel",)),
    )(x, branch, gain, mods)


def _norm2_bwd(x1, attn, gain, mods, dz2, dx2):
    t, d = x1.shape
    rb = min(ROW_BLOCK, t)

    def body(x1_ref, at_ref, g_ref, mod_ref, dz_ref, dx2_ref, dx1_ref, da_ref, st_ref):
        @pl.when(pl.program_id(0) == 0)
        def _():
            st_ref[...] = jnp.zeros_like(st_ref)

        xh, r = _rms(x1_ref[...])
        g = g_ref[...]
        dz = dz_ref[...]
        dxn = dz * (1.0 + mod_ref[1:2, :])
        st_ref[0:1, :] += _colsum(dz)
        st_ref[1:2, :] += _colsum(dz * (xh * g))
        st_ref[2:3, :] += _colsum(dxn * xh)
        dx1 = dx2_ref[...] + _rms_bwd(dxn * g, xh, r)
        dx1_ref[...] = dx1
        st_ref[3:4, :] += _colsum(dx1 * at_ref[...])
        da_ref[...] = (dx1 * mod_ref[0:1, :]).astype(BF16)

    row = pl.BlockSpec((rb, d), lambda i: (i, 0))
    return _pcall(
        body,
        name="norm2_mod_bwd",
        out_shape=[jax.ShapeDtypeStruct((t, d), F32), jax.ShapeDtypeStruct((t, d), BF16), jax.ShapeDtypeStruct((8, d), F32)],
        grid=(t // rb,),
        in_specs=[row, row, pl.BlockSpec((1, d), lambda i: (0, 0)), pl.BlockSpec((8, d), lambda i: (0, 0)), row, row],
        out_specs=[row, row, pl.BlockSpec((8, d), lambda i: (0, 0))],
        compiler_params=_cparams(("arbitrary",)),
    )(x1, attn, gain, mods, dz2, dx2)


def _final_loss(x1, ffn, gain, mods, target):
    t, d = x1.shape
    rb = min(ROW_BLOCK, t)
    nb = t // rb

    def body(x1_ref, f_ref, g_ref, mod_ref, tg_ref, dx2_ref, df_ref, st_ref):
        i = pl.program_id(0)

        @pl.when(i == 0)
        def _():
            st_ref[...] = jnp.zeros_like(st_ref)

        ffn_v = f_ref[...]
        g2 = mod_ref[0:1, :]
        x2 = x1_ref[...] + g2 * ffn_v
        xh, r = _rms(x2)
        g = g_ref[...]
        err = xh * g - tg_ref[...]
        st_ref[2:3, :] += _colsum(err * err) * (0.5 / d)
        dy = err * (1.0 / d)
        st_ref[0:1, :] += _colsum(dy * xh)
        dx2 = _rms_bwd(dy * g, xh, r)
        dx2_ref[...] = dx2
        st_ref[1:2, :] += _colsum(dx2 * ffn_v)
        df_ref[...] = (dx2 * g2).astype(BF16)

        @pl.when(i == nb - 1)
        def _():
            st_ref[3:4, :] = jnp.broadcast_to(jnp.sum(st_ref[2:3, :], axis=-1, keepdims=True), (1, d))

    row = pl.BlockSpec((rb, d), lambda i: (i, 0))
    return _pcall(
        body,
        name="final_norm_loss",
        out_shape=[jax.ShapeDtypeStruct((t, d), F32), jax.ShapeDtypeStruct((t, d), BF16), jax.ShapeDtypeStruct((8, d), F32)],
        grid=(nb,),
        in_specs=[row, row, pl.BlockSpec((1, d), lambda i: (0, 0)), pl.BlockSpec((8, d), lambda i: (0, 0)), row],
        out_specs=[row, row, pl.BlockSpec((8, d), lambda i: (0, 0))],
        compiler_params=_cparams(("arbitrary",)),
    )(x1, ffn, gain, mods, target)


def _row_ends(shape):
    rows = lax.broadcasted_iota(jnp.int32, shape, 0)
    return rows == 0, rows == shape[0] - 1


def _shift_dn(v, first):
    return jnp.where(first, 0.0, pltpu.roll(v, 1, 0))


def _shift_up(v, last):
    return jnp.where(last, 0.0, pltpu.roll(v, v.shape[0] - 1, 0))


def _conv_fwd(u, cw, cb):
    t, f2 = u.shape
    f = f2 // 2
    cbk = _tile(f, 256)
    nf = f // cbk

    def body(ua_ref, ub_ref, cwa_ref, cwb_ref, cba_ref, cbb_ref, h_ref, uc_ref):
        first, last = _row_ends((t, cbk))
        outs = []
        for u_ref, cw_ref, cb_ref in ((ua_ref, cwa_ref, cba_ref), (ub_ref, cwb_ref, cbb_ref)):
            uu, cwv = u_ref[...], cw_ref[...]
            outs.append(cb_ref[...] + cwv[0:1, :] * _shift_dn(uu, first) + cwv[1:2, :] * uu
                        + cwv[2:3, :] * _shift_up(uu, last))
        a, b = outs
        uc_ref[0] = a
        uc_ref[1] = b
        h_ref[...] = (a * jax.nn.sigmoid(a) * b).astype(BF16)

    ca = lambda r: pl.BlockSpec((r, cbk), lambda j: (0, j))
    cbs = lambda r: pl.BlockSpec((r, cbk), lambda j: (0, nf + j))
    return _pcall(
        body,
        name="conv_gate_fwd",
        out_shape=[jax.ShapeDtypeStruct((t, f), BF16), jax.ShapeDtypeStruct((2, t, f), F32)],
        grid=(nf,),
        in_specs=[ca(t), cbs(t), ca(3), cbs(3), ca(1), cbs(1)],
        out_specs=[ca(t), pl.BlockSpec((2, t, cbk), lambda j: (0, 0, j))],
        compiler_params=_cparams(("parallel",)),
    )(u, u, cw, cw, cb, cb)


def _conv_bwd(u, uc, cw, dh):
    t, f2 = u.shape
    f = f2 // 2
    cbk = _tile(f, 256)
    nf = f // cbk

    def body(ua_ref, ub_ref, uc_ref, cwa_ref, cwb_ref, dh_ref, du_ref, dcw_ref, dcb_ref):
        first, last = _row_ends((t, cbk))
        a, b = uc_ref[0], uc_ref[1]
        dh_v = dh_ref[...]
        sg = jax.nn.sigmoid(a)
        db = dh_v * (a * sg)
        da = dh_v * b * (sg * (1.0 + a * (1.0 - sg)))
        for idx, (dv, u_ref, cw_ref) in enumerate(((da, ua_ref, cwa_ref), (db, ub_ref, cwb_ref))):
            uu, cwv = u_ref[...], cw_ref[...]
            up, dn = _shift_up(dv, last), _shift_dn(dv, first)
            dcb_ref[idx] = _colsum(dv)
            dcw_ref[idx, 0:1, :] = _colsum(up * uu)
            dcw_ref[idx, 1:2, :] = _colsum(dv * uu)
            dcw_ref[idx, 2:3, :] = _colsum(dn * uu)
            du_ref[idx] = (cwv[0:1, :] * up + cwv[1:2, :] * dv + cwv[2:3, :] * dn).astype(BF16)

    ca = lambda r: pl.BlockSpec((r, cbk), lambda j: (0, j))
    cbs = lambda r: pl.BlockSpec((r, cbk), lambda j: (0, nf + j))
    o3 = lambda r: pl.BlockSpec((2, r, cbk), lambda j: (0, 0, j))
    return _pcall(
        body,
        name="conv_gate_bwd",
        out_shape=[jax.ShapeDtypeStruct((2, t, f), BF16), jax.ShapeDtypeStruct((2, 3, f), F32),
                   jax.ShapeDtypeStruct((2, 1, f), F32)],
        grid=(nf,),
        in_specs=[ca(t), cbs(t), o3(t), ca(3), cbs(3), ca(t)],
        out_specs=[o3(t), o3(3), o3(1)],
        compiler_params=_cparams(("parallel",)),
    )(u, u, uc, cw, cw, dh)


def _attention_fwd(q, kk, vv, *, hq, hkv, dk, dv, k_blk0, v_blk0, name):
    t = q.shape[0]
    tk = kk.shape[0]
    g_sz = hq // hkv
    tq = min(ATT_Q_BLOCK_FWD, t)

    def body(q_ref, k_ref, v_ref, o_ref, lse_ref):
        k = k_ref[...]
        v = v_ref[...]
        for j in range(g_sz):
            s = lax.dot_general(q_ref[:, j * dk:(j + 1) * dk], k, _DIMS["nt"], preferred_element_type=F32)
            m = jnp.max(s, axis=-1, keepdims=True)
            p = jnp.exp2(s - m)
            l = jnp.sum(p, axis=-1, keepdims=True)
            o = jnp.dot(p.astype(BF16), v, preferred_element_type=F32) / l
            o_ref[:, j * dv:(j + 1) * dv] = o.astype(BF16)
            lse_ref[0, :, j:j + 1] = m + jnp.log2(l)

    return _pcall(
        body,
        name=name,
        out_shape=[jax.ShapeDtypeStruct((t, hq * dv), BF16), jax.ShapeDtypeStruct((hkv, t, g_sz), F32)],
        grid=(hkv, t // tq),
        in_specs=[
            pl.BlockSpec((tq, g_sz * dk), lambda g, i: (i, g)),
            pl.BlockSpec((tk, dk), lambda g, i: (0, k_blk0 + g)),
            pl.BlockSpec((tk, dv), lambda g, i: (0, v_blk0 + g)),
        ],
        out_specs=[
            pl.BlockSpec((tq, g_sz * dv), lambda g, i: (i, g)),
            pl.BlockSpec((1, tq, g_sz), lambda g, i: (g, i, 0)),
        ],
        compiler_params=_cparams(("parallel", "parallel")),
    )(q, kk, vv)


def _attention_bwd(q, kk, vv, do, lse, *, hq, hkv, dk, dv, k_blk0, v_blk0, name):
    t = q.shape[0]
    tk = kk.shape[0]
    g_sz = hq // hkv
    tq = min(ATT_Q_BLOCK, t)

    def body(q_ref, k_ref, v_ref, do_ref, lse_ref, dq_ref, dk_ref, dv_ref):
        @pl.when(pl.program_id(1) == 0)
        def _():
            dk_ref[...] = jnp.zeros_like(dk_ref)
            dv_ref[...] = jnp.zeros_like(dv_ref)

        k = k_ref[...]
        v = v_ref[...]
        for j in range(g_sz):
            qj = q_ref[:, j * dk:(j + 1) * dk]
            doj = do_ref[:, j * dv:(j + 1) * dv]
            s = lax.dot_general(qj, k, _DIMS["nt"], preferred_element_type=F32)
            p = jnp.exp2(s - lse_ref[0, :, j:j + 1])
            dp = lax.dot_general(doj, v, _DIMS["nt"], preferred_element_type=F32)
            ds = (p * (dp - jnp.sum(p * dp, axis=-1, keepdims=True)) * LN2).astype(BF16)
            dv_ref[...] += lax.dot_general(p.astype(BF16), doj, _DIMS["tn"], preferred_element_type=F32)
            dk_ref[...] += lax.dot_general(ds, qj, _DIMS["tn"], preferred_element_type=F32)
            dq_ref[:, j * dk:(j + 1) * dk] = jnp.dot(ds, k, preferred_element_type=F32)

    return _pcall(
        body,
        name=name,
        out_shape=[jax.ShapeDtypeStruct((t, hq * dk), F32), jax.ShapeDtypeStruct((tk, hkv * dk), F32),
                   jax.ShapeDtypeStruct((tk, hkv * dv), F32)],
        grid=(hkv, t // tq),
        in_specs=[
            pl.BlockSpec((tq, g_sz * dk), lambda g, i: (i, g)),
            pl.BlockSpec((tk, dk), lambda g, i: (0, k_blk0 + g)),
            pl.BlockSpec((tk, dv), lambda g, i: (0, v_blk0 + g)),
            pl.BlockSpec((tq, g_sz * dv), lambda g, i: (i, g)),
            pl.BlockSpec((1, tq, g_sz), lambda g, i: (g, i, 0)),
        ],
        out_specs=[
            pl.BlockSpec((tq, g_sz * dk), lambda g, i: (i, g)),
            pl.BlockSpec((tk, dk), lambda g, i: (0, g)),
            pl.BlockSpec((tk, dv), lambda g, i: (0, g)),
        ],
        compiler_params=_cparams(("parallel", "arbitrary")),
    )(q, kk, vv, do, lse)


def _silu(v):
    return v * jax.nn.sigmoid(v)


def _ada_fwd(conds, w_ada, b_ada_shard):
    r, d = conds.shape
    n = w_ada.shape[1]
    tn = _tile(n, 512)

    def body(c_ref, w_ref, b_ref, o_ref):
        s = _silu(c_ref[...]).astype(BF16)
        o_ref[...] = jnp.dot(s, w_ref[...].astype(BF16), preferred_element_type=F32) + b_ref[...]

    return _pcall(
        body,
        name="ada_fwd",
        out_shape=jax.ShapeDtypeStruct((r, n), F32),
        grid=(n // tn,),
        in_specs=[pl.BlockSpec((r, d), lambda j: (0, 0)), pl.BlockSpec((d, tn), lambda j: (0, j)),
                  pl.BlockSpec((1, tn), lambda j: (0, j))],
        out_specs=pl.BlockSpec((r, tn), lambda j: (0, j)),
        compiler_params=_cparams(("parallel",)),
    )(conds, w_ada, b_ada_shard)


def _cctx_partial(da16_shard, w_ada, c_ctx_row):
    d, n = w_ada.shape
    td = _tile(d, 512)

    def body(g_ref, w_ref, c_ref, o_ref):
        ds = lax.dot_general(g_ref[8:16, :].astype(BF16), w_ref[...].astype(BF16), _DIMS["nt"],
                             preferred_element_type=F32)
        cv = c_ref[...]
        sg = jax.nn.sigmoid(cv)
        o_ref[...] = ds * (sg * (1.0 + cv * (1.0 - sg)))

    return _pcall(
        body,
        name="cctx_partial",
        out_shape=jax.ShapeDtypeStruct((8, d), F32),
        grid=(d // td,),
        in_specs=[pl.BlockSpec((16, n), lambda j: (0, 0)), pl.BlockSpec((td, n), lambda j: (j, 0)),
                  pl.BlockSpec((1, td), lambda j: (0, j))],
        out_specs=pl.BlockSpec((8, td), lambda j: (0, j)),
        compiler_params=_cparams(("parallel",)),
    )(da16_shard, w_ada, c_ctx_row)


def _sum_parts(parts):
    p, _, n = parts.shape

    def body(p_ref, o_ref):
        acc = p_ref[0]
        for s in range(1, p):
            acc = acc + p_ref[s]
        o_ref[...] = acc

    return _pcall(
        body,
        name="sum_parts",
        out_shape=jax.ShapeDtypeStruct((1, n), F32),
        in_specs=[pl.BlockSpec(memory_space=pltpu.VMEM)],
        out_specs=pl.BlockSpec(memory_space=pltpu.VMEM),
    )(parts)


def _adam_math(w, g, m, v):
    m2 = ADAM_B1 * m + (1.0 - ADAM_B1) * g
    v2 = ADAM_B2 * v + (1.0 - ADAM_B2) * jnp.square(g)
    m_hat = m2 / (1.0 - ADAM_B1 ** ADAM_STEP)
    v_hat = v2 / (1.0 - ADAM_B2 ** ADAM_STEP)
    delta = -ADAM_LR * (m_hat / (jnp.sqrt(v_hat) + ADAM_EPS) + ADAM_WD * w)
    return delta, m2, v2


def _adamw(parts, w, m, v, name):
    p, r, c = parts.shape
    rb = _tile(r, max(8, (1 << 20) // (4 * c) // 8 * 8), 8)

    def body(p_ref, w_ref, m_ref, v_ref, g_ref, d_ref, m2_ref, v2_ref):
        g = p_ref[0].astype(F32)
        for s in range(1, p):
            g = g + p_ref[s].astype(F32)
        g_ref[...] = g
        d_ref[...], m2_ref[...], v2_ref[...] = _adam_math(w_ref[...], g, m_ref[...], v_ref[...])

    if w.ndim == 3:
        row = pl.BlockSpec((None, rb, c), lambda i: (0, i, 0))
    else:
        row = pl.BlockSpec((rb, c), lambda i: (i, 0))
    return _pcall(
        body,
        name=name,
        out_shape=[jax.ShapeDtypeStruct(w.shape, F32)] * 4,
        grid=(r // rb,),
        in_specs=[pl.BlockSpec((p, rb, c), lambda i: (0, i, 0)), row, row, row],
        out_specs=[row] * 4,
        compiler_params=_cparams(("parallel",)),
    )(parts, w, m, v)


def _adamw_ada(conds, da16, w, m, v):
    d, n = w.shape
    rb = _tile(d, 256, LANE)

    def body(s_ref, da_ref, w_ref, m_ref, v_ref, g_ref, d_ref, m2_ref, v2_ref):
        g = lax.dot_general(_silu(s_ref[...]).astype(BF16), da_ref[...].astype(BF16), _DIMS["tn"],
                            preferred_element_type=F32)
        g_ref[...] = g
        d_ref[...], m2_ref[...], v2_ref[...] = _adam_math(w_ref[...], g, m_ref[...], v_ref[...])

    row = pl.BlockSpec((rb, n), lambda i: (i, 0))
    return _pcall(
        body,
        name="adamw_w_ada",
        out_shape=[jax.ShapeDtypeStruct((d, n), F32)] * 4,
        grid=(d // rb,),
        in_specs=[pl.BlockSpec((16, rb), lambda i: (0, i)), pl.BlockSpec((16, n), lambda i: (0, 0)), row, row, row],
        out_specs=[row] * 4,
        compiler_params=_cparams(("parallel",)),
    )(conds, da16, w, m, v)


def _cast_bf16(a, name):
    _, r, c = a.shape
    rb = _tile(r, 512, 8)

    def body(a_ref, o_ref):
        o_ref[...] = a_ref[...].astype(BF16)

    return _pcall(body, name=name, out_shape=jax.ShapeDtypeStruct((r, c), BF16), grid=(r // rb,),
                  in_specs=[pl.BlockSpec((None, rb, c), lambda i: (0, i, 0))],
                  out_specs=pl.BlockSpec((rb, c), lambda i: (i, 0)), compiler_params=_cparams(("parallel",)))(a)


def _rope_tabs(t, rot):
    half, q = rot // 2, rot // 4
    n_rows = t // GRID_W
    row = jnp.repeat(jnp.arange(n_rows, dtype=F32), GRID_W)
    col = jnp.tile(jnp.arange(GRID_W, dtype=F32), n_rows)
    inv_freq = ROPE_THETA ** (-jnp.arange(0, half, 2, dtype=F32) / half)
    ang = jnp.concatenate([row[:, None] * inv_freq, col[:, None] * inv_freq], axis=-1)
    cos, sin = jnp.cos(ang), jnp.sin(ang)
    c0, c1, s0, s1 = cos[:, :q], cos[:, q:], sin[:, :q], sin[:, q:]
    z = jnp.zeros_like(s0)
    return (jnp.concatenate([c0, c0, c1, c1], -1), jnp.concatenate([-s0, z, -s1, z], -1),
            jnp.concatenate([z, s0, z, s1], -1))


def _pad_cols(a, left, total, fill=0.0):
    return jnp.pad(a, ((0, 0), (left, total - left - a.shape[1])), constant_values=fill)


def _with_ctx_rows(tab, tc, fill):
    return jnp.concatenate([tab, jnp.full((tc, tab.shape[1]), fill, F32)], axis=0)


def kernel(x, c, ctx, c_ctx, w_ada, b_ada, norm1_g, w_in, mla_q_norm_g, w_q_up, mla_kv_norm_g, w_kv_up, gqa_q_norm_g, gqa_k_norm_g, w_br_a, w_br_b, w_out, norm2_g, w_up, conv_w, conv_b, w_down, final_norm_g, loss_target, m_c_ctx, m_w_ada, m_b_ada, m_norm1_g, m_w_in, m_mla_q_norm_g, m_w_q_up, m_mla_kv_norm_g, m_w_kv_up, m_gqa_q_norm_g, m_gqa_k_norm_g, m_w_br_a, m_w_br_b, m_w_out, m_norm2_g, m_w_up, m_conv_w, m_conv_b, m_w_down, m_final_norm_g, v_c_ctx, v_w_ada, v_b_ada, v_norm1_g, v_w_in, v_mla_q_norm_g, v_w_q_up, v_mla_kv_norm_g, v_w_kv_up, v_gqa_q_norm_g, v_gqa_k_norm_g, v_w_br_a, v_w_br_b, v_w_out, v_norm2_g, v_w_up, v_conv_w, v_conv_b, v_w_down, v_final_norm_g):
    weights = dict(c_ctx=c_ctx, w_ada=w_ada, b_ada=b_ada, norm1_g=norm1_g, w_in=w_in, mla_q_norm_g=mla_q_norm_g,
                   w_q_up=w_q_up, mla_kv_norm_g=mla_kv_norm_g, w_kv_up=w_kv_up, gqa_q_norm_g=gqa_q_norm_g,
                   gqa_k_norm_g=gqa_k_norm_g, w_br_a=w_br_a, w_br_b=w_br_b, w_out=w_out, norm2_g=norm2_g, w_up=w_up,
                   conv_w=conv_w, conv_b=conv_b, w_down=w_down, final_norm_g=final_norm_g)
    mom_m = dict(c_ctx=m_c_ctx, w_ada=m_w_ada, b_ada=m_b_ada, norm1_g=m_norm1_g, w_in=m_w_in, mla_q_norm_g=m_mla_q_norm_g,
                 w_q_up=m_w_q_up, mla_kv_norm_g=m_mla_kv_norm_g, w_kv_up=m_w_kv_up, gqa_q_norm_g=m_gqa_q_norm_g,
                 gqa_k_norm_g=m_gqa_k_norm_g, w_br_a=m_w_br_a, w_br_b=m_w_br_b, w_out=m_w_out, norm2_g=m_norm2_g,
                 w_up=m_w_up, conv_w=m_conv_w, conv_b=m_conv_b, w_down=m_w_down, final_norm_g=m_final_norm_g)
    mom_v = dict(c_ctx=v_c_ctx, w_ada=v_w_ada, b_ada=v_b_ada, norm1_g=v_norm1_g, w_in=v_w_in, mla_q_norm_g=v_mla_q_norm_g,
                 w_q_up=v_w_q_up, mla_kv_norm_g=v_mla_kv_norm_g, w_kv_up=v_w_kv_up, gqa_q_norm_g=v_gqa_q_norm_g,
                 gqa_k_norm_g=v_gqa_k_norm_g, w_br_a=v_w_br_a, w_br_b=v_w_br_b, w_out=v_w_out, norm2_g=v_norm2_g,
                 w_up=v_w_up, conv_w=v_conv_w, conv_b=v_conv_b, w_down=v_w_down, final_norm_g=v_final_norm_g)
    order = list(weights)

    my_idx = 4 * lax.axis_index("x") + 2 * lax.axis_index("y") + lax.axis_index("c")
    xs, cts, tgt = x[0], ctx[0], loss_target[0]
    t, d = xs.shape
    tc = cts.shape[0]
    ta = t + tc
    kvl, ql = MLA_KV_LORA, MLA_Q_LORA
    nb = GQA_KV_HEADS * GQA_HEAD_DIM
    hb = GQA_HEADS * GQA_HEAD_DIM
    ha = MLA_HEADS
    f2 = w_up.shape[2] * N_DEV
    ff = f2 // 2

    big = ["w_in", "w_q_up", "w_kv_up", "w_br_a", "w_br_b", "w_out", "w_up", "w_down"]
    nw = len(big)
    del nw
    _ORDER_AFTER.clear()
    shards = {n: _cast_bf16(weights[n], "cast_" + n) for n in big}
    c_idx = jnp.reshape(lax.axis_index("c"), (1,)).astype(jnp.int32)

    def gather_start(names, dep):
        shs = [shards[n] for n in names]
        land = [lax.empty((N_DEV,) + s.shape, BF16) for s in shs]
        if dep is not None:
            _after(dep)
        s, r, arrs, tok = _split_start("gather_ici_start_" + names[0], shs + land, _gather_ici_copies(len(names)),
                                       5 * len(names))
        return dict(names=names, s=s, r=r, arrs=arrs, tok=tok)

    def gather_relay(g, after):
        n = len(g["names"])
        arrs = _split_wait("gather_ici_wait_" + g["names"][0], g["s"], g["r"], g["arrs"], _gather_ici_copies(n), after)
        s, r, bufs, tok = _split_start("gather_d2d_start_" + g["names"][0], arrs[n:], _gather_d2d_copies(n), 3 * n)
        g.update(s2=s, r2=r, bufs=bufs)
        return tok

    def gather_finish(g, after):
        n = len(g["names"])
        bufs = _split_wait("gather_d2d_wait_" + g["names"][0], g["s2"], g["r2"], g["bufs"], _gather_d2d_copies(n), after)
        return dict(zip(g["names"], bufs))

    c_all, cw_all = _all_gather([jnp.pad(c, ((0, 7), (0, 0))), jnp.pad(conv_w[0], ((0, 5), (0, 0)))], "gather_cond")
    conv_w_f = jnp.transpose(cw_all[:, :3, :], (1, 0, 2)).reshape(3, f2)
    conds = jnp.concatenate([c_all[:, 0, :], c_ctx[None, :], jnp.zeros((7, d), F32)], axis=0)
    ncol = w_ada.shape[2]
    b_shard = lax.dynamic_slice_in_dim(b_ada, my_idx * ncol, ncol, axis=1)
    ada_shard = _ada_fwd(conds, w_ada[0], b_shard)
    (ada_all,) = _all_gather([ada_shard], "gather_ada")
    ada = jnp.transpose(ada_all, (1, 0, 2)).reshape(16, N_DEV * ncol)
    lat = lax.dynamic_slice_in_dim(ada, my_idx, 1, axis=0).reshape(6, d)
    cxt = ada[8].reshape(6, d)
    zero2 = jnp.zeros((2, d), F32)
    mods1 = jnp.concatenate([lat[0:2], cxt[0:2], jnp.zeros((4, d), F32)], axis=0)
    mods2 = jnp.concatenate([lat[2:3], lat[3:4], lat[4:5], jnp.zeros((5, d), F32)], axis=0)
    mods2b = jnp.concatenate([lat[2:3], lat[4:5], jnp.zeros((6, d), F32)], axis=0)
    mods3 = jnp.concatenate([lat[5:6], jnp.zeros((7, d), F32)], axis=0)
    del zero2

    g0 = gather_start(["w_in"], ada_all)
    g1 = gather_start(["w_q_up", "w_kv_up", "w_br_a", "w_br_b", "w_out"], g0["tok"])
    g2 = gather_start(["w_up"], g1["tok"])
    g3 = gather_start(["w_down"], g2["tok"])

    ca, s1a, s2a = _rope_tabs(t, MLA_ROPE)
    cb_, s1b, s2b = _rope_tabs(t, GQA_HEAD_DIM)
    q_tabs_a = (_pad_cols(jnp.concatenate([jnp.ones((t, MLA_NOPE), F32), ca], 1), 0, MLA_SLOT),
                _pad_cols(s1a, MLA_NOPE, MLA_SLOT), _pad_cols(s2a, MLA_NOPE, MLA_SLOT))
    q_tabs_b = (cb_, s1b, s2b)
    k_tabs = (_with_ctx_rows(_pad_cols(ca, 0, LANE), tc, 1.0), _with_ctx_rows(_pad_cols(s1a, 0, LANE), tc, 0.0),
              _with_ctx_rows(_pad_cols(s2a, 0, LANE), tc, 0.0),
              _with_ctx_rows(cb_, tc, 1.0), _with_ctx_rows(s1b, tc, 0.0), _with_ctx_rows(s2b, tc, 0.0))

    def cols_full(g):
        return jnp.transpose(g, (1, 0, 2)).reshape(g.shape[1], N_DEV * g.shape[2])

    _after(gather_relay(g0, mods1))
    z_all = _norm_mod_fwd(cts, xs, norm1_g, mods1)
    gathered = gather_finish(g0, z_all)
    w_in_f = cols_full(gathered["w_in"])
    o_kpe, o_kb, o_vb = kvl, kvl + MLA_ROPE, kvl + MLA_ROPE + nb
    o_q = o_vb + nb
    o_g = o_q + ql + hb
    wkv_w = kvl + 2 * nb + LANE
    w_kv_p = jnp.concatenate([w_in_f[:, :kvl], w_in_f[:, o_kb:o_q], w_in_f[:, o_kpe:o_kb],
                              jnp.zeros((d, LANE - MLA_ROPE), BF16)], axis=1)
    q_w = ql + hb
    q_pad = (-q_w) % 512 if d >= 512 else (-q_w) % d
    gate_blk = (q_w + q_pad) // d
    assert (q_w + q_pad) % d == 0
    w_qg_p = jnp.concatenate([w_in_f[:, o_q:o_g], jnp.zeros((d, q_pad), BF16), w_in_f[:, o_g:]], axis=1)

    kv_all = _mm(z_all, w_kv_p, "nn", F32, "proj_kv", tm=1152, tn=wkv_w)
    qg = _mm(z_all, w_qg_p, "nn", F32, "proj_qg", tm=1024, tn=1024, rows=t)
    _after(gather_relay(g1, qg))
    kin, k_b, v_b = _key_prep_fwd(kv_all, mla_kv_norm_g, gqa_k_norm_g, k_tabs)
    sc_a = float((MLA_NOPE + MLA_ROPE) ** -0.5) * LOG2E
    sc_b = float(GQA_HEAD_DIM ** -0.5) * LOG2E
    cqn, q_b = _q_prep_fwd(qg, mla_q_norm_g, gqa_q_norm_g, q_tabs_b, sc_b)
    gathered.update(gather_finish(g1, q_b))

    wq_f = cols_full(gathered["w_q_up"]).reshape(ql, ha, MLA_NOPE + MLA_ROPE)
    wq_ext = jnp.pad(wq_f, ((0, 0), (0, 0), (0, MLA_SLOT - MLA_NOPE - MLA_ROPE))).reshape(ql, ha * MLA_SLOT)
    wkv_f = cols_full(gathered["w_kv_up"]).reshape(kvl, ha, MLA_NOPE + MLA_V)
    wk_slots = jnp.pad(wkv_f[:, :, :MLA_NOPE], ((0, 0), (0, 0), (0, MLA_SLOT - MLA_NOPE))).reshape(kvl, ha * MLA_SLOT)
    wv_cols = wkv_f[:, :, MLA_NOPE:].reshape(kvl, ha * MLA_V)
    e_slot = jnp.pad(jnp.eye(MLA_ROPE, dtype=BF16),
                     ((0, LANE - MLA_ROPE), (MLA_NOPE, MLA_SLOT - MLA_NOPE - MLA_ROPE)))
    e_rows = jnp.concatenate([jnp.tile(e_slot, (1, ha)), jnp.zeros((LANE, ha * MLA_V), BF16)], axis=1)
    wkv_ext = jnp.concatenate([jnp.concatenate([wk_slots, wv_cols], axis=1), e_rows], axis=0)
    w_bra = cols_full(gathered["w_br_a"])
    w_brb = cols_full(gathered["w_br_b"])
    w_out_f = gathered["w_out"].reshape(d, d)

    kv_a = _mm(kin, wkv_ext, "nn", BF16, "kv_up", tm=1152, tn=1024)
    qa_raw = _mm(cqn, wq_ext, "nn", F32, "q_up", tm=1024, tn=1024)
    q_a = _rope_a(qa_raw, q_tabs_a, False, BF16, "rope_q_fwd", sc_a)
    att_a = dict(hq=ha, hkv=ha, dk=MLA_SLOT, dv=MLA_V, k_blk0=0, v_blk0=ha * MLA_SLOT // MLA_V)
    att_b = dict(hq=GQA_HEADS, hkv=GQA_KV_HEADS, dk=GQA_HEAD_DIM, dv=GQA_HEAD_DIM, k_blk0=0, v_blk0=0)
    o_a, lse_a = _attention_fwd(q_a, kv_a, kv_a, name="attn_a_fwd", **att_a)
    o_b, lse_b = _attention_fwd(q_b, k_b, v_b, name="attn_b_fwd", **att_b)
    _after(gather_relay(g2, o_b))
    pa = _mm(o_a, w_bra, "nn", F32, "br_a", tm=1024, tn=1024)
    pb = _mm(o_b, w_brb, "nn", F32, "br_b", tm=1024, tn=1024)
    merged = _merge_fwd(pa, pb, qg, gate_blk)
    attn = _mm(merged, w_out_f, "nn", F32, "w_out", tm=1024, tn=1024)
    x1, z2 = _resid_norm_mod(xs, attn, norm2_g, mods2, "resid_norm2_fwd")
    w_up3 = gather_finish(g2, z2)["w_up"]
    _after(gather_relay(g3, z2))
    u = _mm_up_fwd(z2, w_up3, "w_up")
    w_down_f = gather_finish(g3, u)["w_down"].reshape(ff, d)
    h, uc = _conv_fwd(u, conv_w_f, conv_b)
    ffn = _mm(h, w_down_f, "nn", F32, "w_down", tm=1024, tn=1024, tk=2816)

    def to_shards(g):
        return jnp.transpose(g.reshape(g.shape[0], N_DEV, g.shape[1] // N_DEV), (1, 0, 2))

    def reduce_start(tag, names, sends):
        n = len(sends)
        land = [lax.empty((4,) + s.shape[1:], s.dtype) for s in sends]
        s, r, arrs, tok = _split_start("reduce_d2d_start_" + tag, sends + land, _reduce_d2d_copies(n), 4 * n)
        return dict(tag=tag, names=names, s=s, r=r, arrs=arrs, tok=tok)

    def reduce_relay(g, after):
        n = len(g["names"])
        arrs = _split_wait("reduce_d2d_wait_" + g["tag"], g["s"], g["r"], g["arrs"], _reduce_d2d_copies(n), after)
        sums = [_pair_sum(arrs[a], arrs[n + a], c_idx, "pair_sum_" + g["names"][a]) for a in range(n)]
        land = [lax.empty(s.shape, s.dtype) for s in sums]
        s, r, arrs2, tok = _split_start("reduce_ici_start_" + g["tag"], sums + land, _reduce_ici_copies(n), 4 * n)
        g.update(s2=s, r2=r, arrs2=arrs2)
        return tok

    def reduce_finish(g, after):
        n = len(g["names"])
        arrs2 = _split_wait("reduce_ici_wait_" + g["tag"], g["s2"], g["r2"], g["arrs2"], _reduce_ici_copies(n), after)
        return dict(zip(g["names"], arrs2[n:]))

    dx2, dffn, st_fin = _final_loss(x1, ffn, final_norm_g[None, :], mods3, tgt)
    loss = lax.psum(st_fin[3, 0], MESH_AXES)
    dh = _mm(dffn, w_down_f, "nt", F32, "d_h", tm=1024, tn=1024)
    g_w_down = _mm(h, dffn, "tn", BF16, "g_w_down", tm=512, tn=1024)
    r_down = reduce_start("down", ["w_down"], [g_w_down.reshape(N_DEV, ff // N_DEV, d)])
    _after(r_down["tok"])
    du3, dcw, dcb = _conv_bwd(u, uc, conv_w_f, dh)
    dz2 = _mm_up_dz(du3, w_up3, "d_z2")
    g_w_up = _mm_up_gw(z2, du3, N_DEV, "g_w_up")
    g_conv_w = jnp.concatenate([dcw[0], dcw[1]], axis=1)
    tok = reduce_relay(r_down, g_w_up)
    _after(tok)
    r_up = reduce_start("up", ["w_up", "conv_w"], [g_w_up, to_shards(jnp.pad(g_conv_w, ((0, 5), (0, 0))))])
    _after(tok, r_up["tok"])
    dx1, dattn, st_n2 = _norm2_bwd(x1, attn, norm2_g, mods2b, dz2, dx2)
    dmerged = _mm(dattn, w_out_f, "nt", F32, "d_merged", tm=1024, tn=1024)
    g_w_out = _mm(merged, dattn, "tn", BF16, "g_w_out", tm=1024, tn=1024)
    dpa, dpb, dgates = _merge_bwd(dmerged, pa, pb, qg, gate_blk)
    do_a = _mm(dpa, w_bra, "nt", BF16, "d_o_a", tm=1024, tn=1024)
    do_b = _mm(dpb, w_brb, "nt", BF16, "d_o_b", tm=1024, tn=1024)
    g_w_bra = _mm(o_a, dpa, "tn", BF16, "g_w_br_a", tm=1024, tn=1024)
    g_w_brb = _mm(o_b, dpb, "tn", BF16, "g_w_br_b", tm=1024, tn=1024)
    tok = reduce_relay(r_up, g_w_brb)
    _after(tok)
    r_out = reduce_start("out", ["w_out", "w_br_a", "w_br_b"],
                         [g_w_out.reshape(N_DEV, d // N_DEV, d), to_shards(g_w_bra), to_shards(g_w_brb)])
    _after(tok, r_out["tok"])
    dq_a, dk_a, dv_a = _attention_bwd(q_a, kv_a, kv_a, do_a, lse_a, name="attn_a_bwd", **att_a)
    dq_b, dk_b, dv_b = _attention_bwd(q_b, k_b, v_b, do_b, lse_b, name="attn_b_bwd", **att_b)
    _after(reduce_relay(r_out, dv_b))
    dqa_raw = _rope_a(dq_a, q_tabs_a, True, BF16, "rope_q_bwd", sc_a)
    dcqn = _mm(dqa_raw, wq_ext, "nt", F32, "d_cqn", tm=1024, tn=ql)
    g_wq_ext = _mm(cqn, dqa_raw, "tn", BF16, "g_w_q_up", tm=ql, tn=1024)
    dq_p, st_q, st_qb = _q_prep_bwd(qg, mla_q_norm_g, gqa_q_norm_g, q_tabs_b, dcqn, dq_b, q_pad, sc_b)
    dkin = _mm_cat_nt([(dk_a, wkv_ext, 0), (dv_a, wkv_ext, ha * MLA_SLOT)], F32, "d_kin", tm=1152, tn=kvl + LANE)
    g_wkv_ext = _mm_cat_tn(kin, [dk_a, dv_a], BF16, "g_w_kv_up", tm=kvl + LANE, tn=min(1024, ha * MLA_V))
    dkv_p, st_kv, st_kb = _key_prep_bwd(kv_all, mla_kv_norm_g, gqa_k_norm_g, k_tabs, dkin, dk_b, dv_b)
    g_wq = g_wq_ext.reshape(ql, ha, MLA_SLOT)[:, :, :MLA_NOPE + MLA_ROPE].reshape(ql, ha * (MLA_NOPE + MLA_ROPE))
    g_wkv = jnp.concatenate([g_wkv_ext[:kvl, :ha * MLA_SLOT].reshape(kvl, ha, MLA_SLOT)[:, :, :MLA_NOPE],
                             g_wkv_ext[:kvl, ha * MLA_SLOT:].reshape(kvl, ha, MLA_V)], axis=2).reshape(kvl, ha * (MLA_NOPE + MLA_V))
    r_qkv = reduce_start("qkv", ["w_q_up", "w_kv_up"], [to_shards(g_wq), to_shards(g_wkv)])
    _after(r_qkv["tok"])
    g_wkv_p = _mm(z_all, dkv_p, "tn", BF16, "g_w_in_kv", tm=1024, tn=wkv_w)
    g_wqg_p = _mm_cat_tn(z_all, [dq_p, dgates], BF16, "g_w_in_qg", tm=1024, tn=min(1024, d), rows=t)
    g_w_in = jnp.concatenate([g_wkv_p[:, :kvl], g_wkv_p[:, kvl + 2 * nb:kvl + 2 * nb + MLA_ROPE],
                              g_wkv_p[:, kvl:kvl + 2 * nb], g_wqg_p[:, :q_w], g_wqg_p[:, q_w + q_pad:]], axis=1)
    r_in = reduce_start("in", ["w_in"], [to_shards(g_w_in)])
    _after(r_in["tok"])
    qw_p = q_w + q_pad
    dz_lat = _mm_sum_nt([(dq_p, 0, w_qg_p, 0, qw_p), (dgates, 0, w_qg_p, qw_p, d), (dgates, d, w_qg_p, qw_p + d, d),
                         (dkv_p, 0, w_kv_p, 0, wkv_w)], F32, "d_z_lat", rows=t)
    dz_ctx = _mm(dkv_p, w_kv_p, "nt", F32, "d_z_ctx", tm=min(ROW_BLOCK, tc), tn=1024, a_row_off=t)
    tok_q = reduce_relay(r_qkv, dz_ctx)
    _after(tok_q)
    grad_x, st_n1 = _norm1_bwd(cts, xs, norm1_g, mods1, dz_ctx, dz_lat, dx1)

    d_lat = jnp.concatenate([st_n1[0], st_n1[1], st_n2[3], st_n2[0], st_n2[1], st_fin[1]])
    d_cxt = jnp.concatenate([st_n1[3], st_n1[4], jnp.zeros((4 * d,), F32)])
    small = jnp.concatenate([d_lat, d_cxt, st_n1[2], st_q[0], st_kv[0], st_qb[0], st_kb[0], st_n2[2],
                             jnp.concatenate([dcb[0, 0], dcb[1, 0]]), st_fin[0]])
    n_small = small.shape[0]
    pad_small = (-n_small) % LANE
    (small_all,) = _all_gather([jnp.pad(small, (0, pad_small)).reshape(1, -1)], "gather_small")
    offs = {}
    o = 0
    for nm, ln in (("d_lat", 6 * d), ("d_cxt", 6 * d), ("norm1_g", d), ("mla_q_norm_g", ql), ("mla_kv_norm_g", kvl),
                   ("gqa_q_norm_g", GQA_HEAD_DIM), ("gqa_k_norm_g", GQA_HEAD_DIM), ("norm2_g", d), ("conv_b", f2),
                   ("final_norm_g", d)):
        offs[nm] = (o, ln)
        o += ln

    def part(nm):
        a, ln = offs[nm]
        return small_all[:, :, a:a + ln]

    d_lat_all = part("d_lat")[:, 0, :]
    d_cxt_sum = _sum_parts(part("d_cxt"))
    da16 = jnp.concatenate([d_lat_all, d_cxt_sum, jnp.zeros((7, 6 * d), F32)], axis=0)
    da16_shard = lax.dynamic_slice_in_dim(da16, my_idx * ncol, ncol, axis=1)
    cc_part = _cctx_partial(da16_shard, w_ada[0], c_ctx[None, :])
    (cc_all,) = _all_gather([cc_part], "gather_cctx")
    cc_parts = cc_all[:, 0:1, :]
    tok_i = reduce_relay(r_in, cc_all)

    res = {}
    _after(tok_q, tok_i)

    def upd(nm, parts, shape2):
        del shape2
        wv, mv, vv = weights[nm], mom_m[nm], mom_v[nm]
        if wv.ndim == 1:
            wv, mv, vv = (a.reshape(1, -1) for a in (wv, mv, vv))
        outs = _adamw(parts, wv, mv, vv, "adamw_" + nm)
        res[nm] = [o_.reshape(weights[nm].shape) for o_ in outs]

    for nm in ("norm1_g", "mla_q_norm_g", "mla_kv_norm_g", "gqa_q_norm_g", "gqa_k_norm_g", "norm2_g", "conv_b",
               "final_norm_g"):
        upd(nm, part(nm), (1, offs[nm][1]))
    upd("c_ctx", cc_parts, (1, d))
    b_parts = jnp.concatenate([d_lat_all[:, None, :], d_cxt_sum[None]], axis=0)
    upd("b_ada", b_parts, (1, 6 * d))
    _after(tok_q, tok_i)
    outs = _adamw_ada(conds, da16_shard, w_ada[0], m_w_ada[0], v_w_ada[0])
    res["w_ada"] = [o_[None] for o_ in outs]
    last = outs[0]
    for grp in (r_down, r_up, r_out, r_qkv, r_in):
        recv = reduce_finish(grp, last)
        for nm in grp["names"]:
            parts = recv[nm][:, :3, :] if nm == "conv_w" else recv[nm]
            upd(nm, parts, weights[nm].shape[1:])
            last = res[nm][0]

    return (loss, grad_x[None], *[res[n][0] for n in order], *[res[n][1] for n in order],
            *[res[n][2] for n in order], *[res[n][3] for n in order])
```

```python
import functools

import jax
import jax.numpy as jnp
from jax import lax
from jax.experimental import pallas as pl
from jax.experimental.pallas import tpu as pltpu

F32 = jnp.float32
BF16 = jnp.bfloat16

GRID_W = 64
ROPE_THETA = 10000.0
NORM_EPS = 1e-6
MLA_HEADS = 8
MLA_Q_LORA = 768
MLA_KV_LORA = 512
MLA_NOPE = 128
MLA_ROPE = 64
MLA_V = 128
GQA_HEADS = 8
GQA_KV_HEADS = 2
GQA_HEAD_DIM = 128
ADAM_LR = 0.001
ADAM_B1 = 0.9
ADAM_B2 = 0.999
ADAM_EPS = 1e-08
ADAM_WD = 0.01
ADAM_STEP = 10

N_DEV = 8
MESH_AXES = ("x", "y", "c")
LANE = 128
MLA_SLOT = 2 * LANE
VMEM_LIMIT = 56 * 1024 * 1024
ROW_BLOCK = 256
ATT_Q_BLOCK = 512
ATT_Q_BLOCK_FWD = 512
LN2 = 0.6931471805599453
LOG2E = 1.4426950408889634
MESH_ID = pl.DeviceIdType.MESH


def _tile(n, pref, align=LANE):
    if n <= pref:
        return n
    best = None
    t = align
    while t <= pref:
        if n % t == 0:
            best = t
        t += align
    assert best is not None, (n, pref, align)
    return best


def _cparams(sem=None):
    return pltpu.CompilerParams(dimension_semantics=sem, vmem_limit_bytes=VMEM_LIMIT)


_ORDER_AFTER = []


def _after(*arrays):
    _ORDER_AFTER.extend(arrays)


def _pcall(body, *, in_specs, **kw):
    deps = tuple(_ORDER_AFTER)
    _ORDER_AFTER.clear()
    if not deps:
        return pl.pallas_call(body, in_specs=in_specs, **kw)
    n_in, n_dep = len(in_specs), len(deps)

    def with_deps(*refs):
        body(*refs[:n_in], *refs[n_in + n_dep:])

    call = pl.pallas_call(with_deps, in_specs=list(in_specs) + [pl.BlockSpec(memory_space=pl.ANY)] * n_dep, **kw)
    return lambda *args: call(*args, *deps)


def _all_gather(arrs, name):
    n = len(arrs)

    def body(*refs):
        ins = refs[:n]
        outs = refs[n:2 * n]
        send_sems, recv_sems, local_sems = refs[2 * n:]
        x, y, c = lax.axis_index("x"), lax.axis_index("y"), lax.axis_index("c")
        me, sibling = (x, y, c), (x, y, 1 - c)
        chips = [(1 - x, y), (x, 1 - y), (1 - x, 1 - y)]

        def rows(a, dev):
            px, py, pc = dev
            return outs[a].at[4 * px + 2 * py + pc]

        def copy(a, k, block, to, src=None):
            return pltpu.make_async_remote_copy(
                src_ref=rows(a, block) if src is None else src,
                dst_ref=rows(a, block),
                send_sem=send_sems.at[7 * a + k],
                recv_sem=recv_sems.at[7 * a + k],
                device_id=to,
                device_id_type=MESH_ID,
            )

        mine = [pltpu.make_async_copy(ins[a], rows(a, me), local_sems.at[a]) for a in range(n)]
        for cp in mine:
            cp.start()
        first = []
        for a in range(n):
            first.append(copy(a, 0, me, sibling, src=ins[a]))
            first += [copy(a, 1 + j, me, (*chip, c), src=ins[a]) for j, chip in enumerate(chips)]
        for cp in first:
            cp.start()
        passed = []
        for j, chip in enumerate(chips):
            for a in range(n):
                copy(a, 1 + j, (*chip, c), me).wait_recv()
                fwd = copy(a, 4 + j, (*chip, c), sibling)
                fwd.start()
                passed.append(fwd)
        for a in range(n):
            copy(a, 0, sibling, me).wait_recv()
            for j, chip in enumerate(chips):
                copy(a, 4 + j, (*chip, 1 - c), me).wait_recv()
        for cp in first + passed:
            cp.wait_send()
        for cp in mine:
            cp.wait()

    any_spec = pl.BlockSpec(memory_space=pl.ANY)
    outs = _pcall(
        body,
        name=name,
        out_shape=[jax.ShapeDtypeStruct((N_DEV,) + a.shape, a.dtype) for a in arrs],
        in_specs=[any_spec] * n,
        out_specs=[any_spec] * n,
        scratch_shapes=[
            pltpu.SemaphoreType.DMA((7 * n,)),
            pltpu.SemaphoreType.DMA((7 * n,)),
            pltpu.SemaphoreType.DMA((n,)),
        ],
    )(*arrs)
    return list(outs)


def _all_to_all(arrs, name):
    n = len(arrs)

    def body(*refs):
        ins = refs[:n]
        outs = refs[n:2 * n]
        send_sems, recv_sems, local_sems = refs[2 * n:]
        x, y, c = lax.axis_index("x"), lax.axis_index("y"), lax.axis_index("c")
        my_idx = 4 * x + 2 * y + c

        def peer(k):
            fx, fy, fc = (k >> 2) & 1, (k >> 1) & 1, k & 1
            return (x ^ fx if fx else x, y ^ fy if fy else y, c ^ fc if fc else c)

        def copy(a, k):
            px, py, pc = peer(k)
            return pltpu.make_async_remote_copy(
                src_ref=ins[a].at[4 * px + 2 * py + pc],
                dst_ref=outs[a].at[my_idx],
                send_sem=send_sems.at[7 * a + k - 1],
                recv_sem=recv_sems.at[7 * a + k - 1],
                device_id=(px, py, pc),
                device_id_type=MESH_ID,
            )

        mine = [pltpu.make_async_copy(ins[a].at[my_idx], outs[a].at[my_idx], local_sems.at[a]) for a in range(n)]
        for cp in mine:
            cp.start()
        order = [1, 4, 2, 5, 3, 6, 7]
        cps = [copy(a, k) for k in order for a in range(n)]
        for cp in cps:
            cp.start()
        for cp in cps:
            cp.wait()
        for cp in mine:
            cp.wait()

    any_spec = pl.BlockSpec(memory_space=pl.ANY)
    outs = _pcall(
        body,
        name=name,
        out_shape=[jax.ShapeDtypeStruct(a.shape, a.dtype) for a in arrs],
        in_specs=[any_spec] * n,
        out_specs=[any_spec] * n,
        scratch_shapes=[
            pltpu.SemaphoreType.DMA((7 * n,)),
            pltpu.SemaphoreType.DMA((7 * n,)),
            pltpu.SemaphoreType.DMA((n,)),
        ],
    )(*arrs)
    return list(outs)


_HBM = pl.BlockSpec(memory_space=pltpu.HBM)
_SEM = pl.BlockSpec(memory_space=pltpu.SEMAPHORE)
_EFFECT = pltpu.SideEffectType.DATAFLOW_SIDE_EFFECTING


def _descriptors(copies, send_sems, recv_sems):
    descs = []
    for i, (src, dst, dev) in enumerate(copies):
        if dev is None:
            descs.append(pltpu.make_async_copy(src, dst, recv_sems.at[i]))
        else:
            descs.append(pltpu.make_async_remote_copy(src_ref=src, dst_ref=dst, send_sem=send_sems.at[i],
                                                      recv_sem=recv_sems.at[i], device_id=dev, device_id_type=MESH_ID))
    return descs


def _split_start(name, arrays, copies_fn, n_copies):
    n = len(arrays)

    def body(*refs):
        send_sems, recv_sems = refs[n], refs[n + 1]
        token = refs[2 * n + 2]
        for dsc in _descriptors(copies_fn(refs[:n]), send_sems, recv_sems):
            dsc.start()
        token[...] = jnp.zeros_like(token)

    outs = _pcall(
        body,
        name=name,
        out_shape=(pltpu.SemaphoreType.DMA((n_copies,)), pltpu.SemaphoreType.DMA((n_copies,)),
                   *[pltpu.HBM(a.shape, a.dtype) for a in arrays], jax.ShapeDtypeStruct((8, LANE), F32)),
        in_specs=[_HBM] * n,
        out_specs=(_SEM, _SEM, *[_HBM] * n, pl.BlockSpec(memory_space=pltpu.VMEM)),
        input_output_aliases={i: 2 + i for i in range(n)},
        compiler_params=pltpu.CompilerParams(has_side_effects=_EFFECT),
    )(*[pltpu.with_memory_space_constraint(a, pltpu.HBM) for a in arrays])
    return outs[0], outs[1], list(outs[2:2 + n]), outs[2 + n]


def _split_wait(name, send_sems, recv_sems, arrays, copies_fn, after):
    n = len(arrays)

    def body(*refs):
        for dsc, (_, _, dev) in zip(_descriptors(copies_fn(refs[:n]), refs[n], refs[n + 1]), copies_fn(refs[:n])):
            if dev is None:
                dsc.wait()
            else:
                dsc.wait_send()
                dsc.wait_recv()

    outs = _pcall(
        body,
        name=name,
        out_shape=tuple(pltpu.HBM(a.shape, a.dtype) for a in arrays),
        in_specs=[_HBM] * n + [_SEM, _SEM, pl.BlockSpec(memory_space=pl.ANY)],
        out_specs=tuple([_HBM] * n),
        input_output_aliases={i: i for i in range(n)},
        compiler_params=pltpu.CompilerParams(has_side_effects=_EFFECT),
    )(*arrays, send_sems, recv_sems, after)
    return list(outs)


def _mesh_pos():
    x, y, c = lax.axis_index("x"), lax.axis_index("y"), lax.axis_index("c")
    return x, y, c, [(1 - x, y), (x, 1 - y), (1 - x, 1 - y)]


def _gather_ici_copies(n):
    def copies(refs):
        x, y, c, chips = _mesh_pos()
        me = 4 * x + 2 * y + c
        out = []
        for a in range(n):
            src, buf = refs[a], refs[n + a]
            out.append((src, buf.at[me], None))
            out.append((src, buf.at[me], (x, y, 1 - c)))
            out += [(src, buf.at[me], (cx, cy, c)) for cx, cy in chips]
        return out
    return copies


def _gather_d2d_copies(n):
    def copies(refs):
        x, y, c, chips = _mesh_pos()
        out = []
        for a in range(n):
            for cx, cy in chips:
                rows = refs[a].at[4 * cx + 2 * cy + c]
                out.append((rows, rows, (x, y, 1 - c)))
        return out
    return copies


def _reduce_d2d_copies(n):
    def copies(refs):
        x, y, c, _ = _mesh_pos()
        out = []
        for a in range(n):
            for k in range(4):
                out.append((refs[a].at[2 * k + (1 - c)], refs[n + a].at[k], (x, y, 1 - c)))
        return out
    return copies


def _reduce_ici_copies(n):
    def copies(refs):
        x, y, c, chips = _mesh_pos()
        mine = 2 * x + y
        out = []
        for a in range(n):
            src, land = refs[a], refs[n + a]
            out.append((src.at[mine], land.at[mine], None))
            out += [(src.at[2 * cx + cy], land.at[mine], (cx, cy, c)) for cx, cy in chips]
        return out
    return copies


def _pair_sum(send, land, c_idx, name):
    _, r, cols = send.shape
    rb = _tile(r, max(8, (1 << 22) // (send.dtype.itemsize * cols) // 8 * 8), 8)
    dt = send.dtype

    def body(c_ref, s_ref, l_ref, o_ref):
        o_ref[...] = (s_ref[...].astype(F32) + l_ref[...].astype(F32)).astype(dt)

    return pl.pallas_call(
        body,
        name=name,
        out_shape=jax.ShapeDtypeStruct((4, r, cols), dt),
        grid_spec=pltpu.PrefetchScalarGridSpec(
            num_scalar_prefetch=1,
            grid=(4, r // rb),
            in_specs=[pl.BlockSpec((None, rb, cols), lambda k, i, c_ref: (2 * k + c_ref[0], i, 0)),
                      pl.BlockSpec((None, rb, cols), lambda k, i, c_ref: (k, i, 0))],
            out_specs=pl.BlockSpec((None, rb, cols), lambda k, i, c_ref: (k, i, 0)),
        ),
        compiler_params=_cparams(("parallel", "parallel")),
    )(c_idx, send, land)


_DIMS = {
    "nn": (((1,), (0,)), ((), ())),
    "nt": (((1,), (1,)), ((), ())),
    "tn": (((0,), (0,)), ((), ())),
}


def _mm_call(a, b, *, mode, grid, a_spec, b_spec, o_spec, out_shape, acc_shape, name):
    nk = grid[2]
    out_dtype = out_shape.dtype

    def body(a_ref, b_ref, o_ref, *scratch):
        p = lax.dot_general(a_ref[...].astype(BF16), b_ref[...].astype(BF16), _DIMS[mode],
                            preferred_element_type=F32)
        if nk == 1:
            o_ref[...] = p.astype(out_dtype)
        else:
            acc = scratch[0]
            k = pl.program_id(2)

            @pl.when(k == 0)
            def _():
                acc[...] = p

            @pl.when(k > 0)
            def _():
                acc[...] += p

            @pl.when(k == nk - 1)
            def _():
                o_ref[...] = acc[...].astype(out_dtype)

    return _pcall(
        body,
        name=name,
        out_shape=out_shape,
        grid=grid,
        in_specs=[a_spec, b_spec],
        out_specs=o_spec,
        scratch_shapes=[pltpu.VMEM(acc_shape, F32)] if nk > 1 else [],
        compiler_params=_cparams(("parallel", "parallel", "arbitrary")),
    )(a, b)


def _mm(a, b, mode, out_dtype, name, tm=512, tn=512, tk=2432, a_row_off=0, rows=None):
    if mode == "nn":
        (m, k), (k2, n) = a.shape, b.shape
    elif mode == "nt":
        (m, k), (n, k2) = a.shape, b.shape
    else:
        (k, m), (k2, n) = a.shape, b.shape
        if rows is not None:
            k = k2 = rows
    assert k == k2, (a.shape, b.shape, mode)
    if mode != "tn":
        m = (m if rows is None else rows + a_row_off) - a_row_off
    tm, tn, tk = _tile(m, tm, 8), _tile(n, tn), _tile(k, tk, 8 if mode == "tn" else LANE)
    assert a_row_off % tm == 0
    ro = a_row_off // tm
    grid = (m // tm, n // tn, k // tk)
    if mode == "tn":
        a_spec = pl.BlockSpec((tk, tm), lambda i, j, kk: (kk, i))
    else:
        a_spec = pl.BlockSpec((tm, tk), lambda i, j, kk: (i + ro, kk))
    if mode == "nt":
        b_spec = pl.BlockSpec((tn, tk), lambda i, j, kk: (j, kk))
    else:
        b_spec = pl.BlockSpec((tk, tn), lambda i, j, kk: (kk, j))
    o_spec = pl.BlockSpec((tm, tn), lambda i, j, kk: (i, j))
    return _mm_call(a, b, mode=mode, grid=grid, a_spec=a_spec, b_spec=b_spec, o_spec=o_spec,
                    out_shape=jax.ShapeDtypeStruct((m, n), out_dtype), acc_shape=(tm, tn), name=name)


def _mm_cat_nt(pieces, out_dtype, name, tm=1024, tn=1024, tk=2048, rows=None):
    m = pieces[0][0].shape[0] if rows is None else rows
    n = pieces[0][1].shape[0]
    tm, tn = _tile(m, tm, 8), _tile(n, tn)
    steps, starts, s = [], [], 0
    for a, b, off in pieces:
        kp = a.shape[1]
        tkp = _tile(kp, tk)
        assert off % tkp == 0 and b.shape[0] == n
        steps.append((tkp, kp // tkp, off // tkp))
        starts.append(s)
        s += kp // tkp
    nk = s
    npc = len(pieces)

    def body(*refs):
        o_ref, acc = refs[2 * npc], refs[2 * npc + 1]
        kk = pl.program_id(2)

        @pl.when(kk == 0)
        def _():
            acc[...] = jnp.zeros_like(acc)

        for p in range(npc):
            @pl.when((kk >= starts[p]) & (kk < starts[p] + steps[p][1]))
            def _(p=p):
                acc[...] += lax.dot_general(refs[2 * p][...].astype(BF16), refs[2 * p + 1][...].astype(BF16), _DIMS["nt"],
                                            preferred_element_type=F32)

        @pl.when(kk == nk - 1)
        def _():
            o_ref[...] = acc[...].astype(out_dtype)

    in_specs, args = [], []
    for p, (a, b, off) in enumerate(pieces):
        tkp, np_, ob = steps[p]

        def rel(kk, p=p, np_=np_):
            return jnp.clip(kk - starts[p], 0, np_ - 1)

        in_specs.append(pl.BlockSpec((tm, tkp), lambda i, j, kk, rel=rel: (i, rel(kk))))
        in_specs.append(pl.BlockSpec((tn, tkp), lambda i, j, kk, rel=rel, ob=ob: (j, ob + rel(kk))))
        args += [a, b]
    return _pcall(
        body,
        name=name,
        out_shape=jax.ShapeDtypeStruct((m, n), out_dtype),
        grid=(m // tm, n // tn, nk),
        in_specs=in_specs,
        out_specs=pl.BlockSpec((tm, tn), lambda i, j, kk: (i, j)),
        scratch_shapes=[pltpu.VMEM((tm, tn), F32)],
        compiler_params=_cparams(("parallel", "parallel", "arbitrary")),
    )(*args)


def _mm_cat_tn(a, pieces, out_dtype, name, tm=1024, tn=1024, rows=None):
    k = a.shape[0] if rows is None else rows
    m = a.shape[1]
    tm = _tile(m, tm)
    starts, s = [], 0
    for b in pieces:
        assert b.shape[1] % tn == 0
        starts.append(s)
        s += b.shape[1] // tn
    nj = s
    npc = len(pieces)

    def body(*refs):
        a_ref, o_ref = refs[0], refs[1 + npc]
        j = pl.program_id(1)
        for p in range(npc):
            @pl.when((j >= starts[p]) & (j < starts[p] + pieces[p].shape[1] // tn))
            def _(p=p):
                o_ref[...] = lax.dot_general(a_ref[...].astype(BF16), refs[1 + p][...].astype(BF16), _DIMS["tn"],
                                             preferred_element_type=F32).astype(out_dtype)

    in_specs = [pl.BlockSpec((k, tm), lambda i, j: (0, i))]
    for p, b in enumerate(pieces):
        np_ = b.shape[1] // tn
        in_specs.append(pl.BlockSpec((k, tn), lambda i, j, p=p, np_=np_: (0, jnp.clip(j - starts[p], 0, np_ - 1))))
    return _pcall(
        body,
        name=name,
        out_shape=jax.ShapeDtypeStruct((m, nj * tn), out_dtype),
        grid=(m // tm, nj),
        in_specs=in_specs,
        out_specs=pl.BlockSpec((tm, tn), lambda i, j: (i, j)),
        compiler_params=_cparams(("parallel", "arbitrary")),
    )(a, *pieces)


def _mm_up_fwd(z2, w3, name, tm=1024):
    t, d = z2.shape
    nsh, _, c = w3.shape
    tm = _tile(t, tm, 8)
    return _mm_call(z2, w3, mode="nn", grid=(t // tm, nsh, 1),
                    a_spec=pl.BlockSpec((tm, d), lambda i, j, kk: (i, 0)),
                    b_spec=pl.BlockSpec((None, d, c), lambda i, j, kk: (j, 0, 0)),
                    o_spec=pl.BlockSpec((tm, c), lambda i, j, kk: (i, j)),
                    out_shape=jax.ShapeDtypeStruct((t, nsh * c), F32), acc_shape=(tm, c), name=name)


def _mm_up_dz(du3, w3, name, tm=512, tn=1024):
    _, t, f = du3.shape
    nsh, d, c = w3.shape
    half = nsh // 2
    assert f == half * c
    tm, tn = _tile(t, tm, 8), _tile(d, tn)

    def body(a_ref, b_ref, o_ref, acc):
        kk = pl.program_id(2)
        p = None
        for s in range(half):
            q = lax.dot_general(a_ref[:, s * c:(s + 1) * c], b_ref[s], _DIMS["nt"], preferred_element_type=F32)
            p = q if p is None else p + q

        @pl.when(kk == 0)
        def _():
            acc[...] = p

        @pl.when(kk == 1)
        def _():
            o_ref[...] = acc[...] + p

    return _pcall(
        body,
        name=name,
        out_shape=jax.ShapeDtypeStruct((t, d), F32),
        grid=(t // tm, d // tn, 2),
        in_specs=[pl.BlockSpec((None, tm, f), lambda i, j, kk: (kk, i, 0)),
                  pl.BlockSpec((half, tn, c), lambda i, j, kk: (kk, j, 0))],
        out_specs=pl.BlockSpec((tm, tn), lambda i, j, kk: (i, j)),
        scratch_shapes=[pltpu.VMEM((tm, tn), F32)],
        compiler_params=_cparams(("parallel", "parallel", "arbitrary")),
    )(du3, w3)


def _mm_sum_nt(pieces, out_dtype, name, tm=512, tn=512, rows=None):
    m = pieces[0][0].shape[0] if rows is None else rows
    n = pieces[0][2].shape[0]
    tm, tn = _tile(m, tm, 8), _tile(n, tn)
    npc = len(pieces)

    def body(*refs):
        p = None
        for s in range(npc):
            q = lax.dot_general(refs[2 * s][...].astype(BF16), refs[2 * s + 1][...].astype(BF16), _DIMS["nt"],
                                preferred_element_type=F32)
            p = q if p is None else p + q
        refs[2 * npc][...] = p.astype(out_dtype)

    in_specs, args = [], []
    for a, ao, b, bo, kp in pieces:
        assert ao % kp == 0 and bo % kp == 0 and b.shape[0] == n
        in_specs.append(pl.BlockSpec((tm, kp), lambda i, j, ab=ao // kp: (i, ab)))
        in_specs.append(pl.BlockSpec((tn, kp), lambda i, j, bb=bo // kp: (j, bb)))
        args += [a, b]
    return _pcall(
        body,
        name=name,
        out_shape=jax.ShapeDtypeStruct((m, n), out_dtype),
        grid=(m // tm, n // tn),
        in_specs=in_specs,
        out_specs=pl.BlockSpec((tm, tn), lambda i, j: (i, j)),
        compiler_params=_cparams(("parallel", "parallel")),
    )(*args)


def _mm_up_gw(z2, du3, nsh, name, tm=1024):
    t, d = z2.shape
    f = du3.shape[2]
    half = nsh // 2
    c = f // half
    tm = _tile(d, tm)
    return _mm_call(z2, du3, mode="tn", grid=(d // tm, nsh, 1),
                    a_spec=pl.BlockSpec((t, tm), lambda i, j, kk: (0, i)),
                    b_spec=pl.BlockSpec((None, t, c), lambda i, j, kk: (j // half, 0, j % half)),
                    o_spec=pl.BlockSpec((None, tm, c), lambda i, j, kk: (j, i, 0)),
                    out_shape=jax.ShapeDtypeStruct((nsh, d, c), BF16), acc_shape=(tm, c), name=name)


def _rms(x):
    r = lax.rsqrt(jnp.mean(x * x, axis=-1, keepdims=True) + NORM_EPS)
    return x * r, r


def _rms_bwd(dxh, xh, r):
    return r * (dxh - xh * jnp.mean(dxh * xh, axis=-1, keepdims=True))


def _colsum(v):
    return jnp.sum(v, axis=0, keepdims=True)


def _rope(v, c, s1, s2, q):
    w = v.shape[-1]
    return v * c + pltpu.roll(v, w - q, 1) * s1 + pltpu.roll(v, q, 1) * s2


def _rope_t(d, c, s1, s2, q):
    w = d.shape[-1]
    return d * c + pltpu.roll(d * s1, q, 1) + pltpu.roll(d * s2, w - q, 1)


def _norm_mod_fwd(ctx, x, gain, mods):
    tc, d = ctx.shape
    t = x.shape[0]
    rb = min(ROW_BLOCK, tc)
    nbl = t // rb

    def body(ctx_ref, x_ref, g_ref, mod_ref, z_ref):
        i = pl.program_id(0)

        def emit(src, sh, sc):
            xh, _ = _rms(src[...])
            z_ref[...] = ((xh * g_ref[...]) * (1.0 + sc) + sh).astype(BF16)

        @pl.when(i >= nbl)
        def _():
            emit(ctx_ref, mod_ref[2:3, :], mod_ref[3:4, :])

        @pl.when(i < nbl)
        def _():
            emit(x_ref, mod_ref[0:1, :], mod_ref[1:2, :])

    return _pcall(
        body,
        name="norm1_mod_fwd",
        out_shape=jax.ShapeDtypeStruct((tc + t, d), BF16),
        grid=((tc + t) // rb,),
        in_specs=[
            pl.BlockSpec((rb, d), lambda i: (jnp.maximum(i - nbl, 0), 0)),
            pl.BlockSpec((rb, d), lambda i: (jnp.minimum(i, nbl - 1), 0)),
            pl.BlockSpec((1, d), lambda i: (0, 0)),
            pl.BlockSpec((8, d), lambda i: (0, 0)),
        ],
        out_specs=pl.BlockSpec((rb, d), lambda i: (i, 0)),
        compiler_params=_cparams(("arbitrary",)),
    )(ctx, x, gain, mods)


def _norm1_bwd(ctx, x, gain, mods, dz_ctx, dz_lat, dx1):
    tc, d = ctx.shape
    t = x.shape[0]
    rb = min(ROW_BLOCK, tc)
    nbl = t // rb

    def body(ctx_ref, x_ref, g_ref, mod_ref, dzc_ref, dzl_ref, dx1_ref, gx_ref, st_ref):
        i = pl.program_id(0)

        @pl.when(i == 0)
        def _():
            st_ref[...] = jnp.zeros_like(st_ref)

        def common(src, dz, sc, row_sh, row_sc):
            xh, r = _rms(src[...])
            g = g_ref[...]
            dxn = dz * (1.0 + sc)
            st_ref[row_sh:row_sh + 1, :] += _colsum(dz)
            st_ref[row_sc:row_sc + 1, :] += _colsum(dz * (xh * g))
            st_ref[2:3, :] += _colsum(dxn * xh)
            return _rms_bwd(dxn * g, xh, r)

        @pl.when(i >= nbl)
        def _():
            common(ctx_ref, dzc_ref[...], mod_ref[3:4, :], 3, 4)

        @pl.when(i < nbl)
        def _():
            gx_ref[...] = dx1_ref[...] + common(x_ref, dzl_ref[...], mod_ref[1:2, :], 0, 1)

    lat = lambda i: (jnp.minimum(i, nbl - 1), 0)
    cix = lambda i: (jnp.maximum(i - nbl, 0), 0)
    return _pcall(
        body,
        name="norm1_mod_bwd",
        out_shape=[jax.ShapeDtypeStruct((t, d), F32), jax.ShapeDtypeStruct((8, d), F32)],
        grid=((tc + t) // rb,),
        in_specs=[
            pl.BlockSpec((rb, d), cix),
            pl.BlockSpec((rb, d), lat),
            pl.BlockSpec((1, d), lambda i: (0, 0)),
            pl.BlockSpec((8, d), lambda i: (0, 0)),
            pl.BlockSpec((rb, d), cix),
            pl.BlockSpec((rb, d), lat),
            pl.BlockSpec((rb, d), lat),
        ],
        out_specs=[pl.BlockSpec((rb, d), lat), pl.BlockSpec((8, d), lambda i: (0, 0))],
        compiler_params=_cparams(("arbitrary",)),
    )(ctx, x, gain, mods, dz_ctx, dz_lat, dx1)


def _key_prep_fwd(kv, kv_gain, kb_gain, tabs):
    ta, wkv = kv.shape
    kvl = MLA_KV_LORA
    nb = GQA_KV_HEADS * GQA_HEAD_DIM
    rb = ROW_BLOCK if ta % ROW_BLOCK == 0 else LANE
    hd = GQA_HEAD_DIM

    def body(kv_ref, g_ref, gb_ref, ca, s1a, s2a, cb, s1b, s2b, kin_ref, kb_ref, vb_ref):
        xh, _ = _rms(kv_ref[:, 0:kvl])
        kin_ref[:, 0:kvl] = (xh * g_ref[...]).astype(BF16)
        kpe = kv_ref[:, kvl + 2 * nb:kvl + 2 * nb + LANE]
        kin_ref[:, kvl:kvl + LANE] = _rope(kpe, ca[...], s1a[...], s2a[...], MLA_ROPE // 4).astype(BF16)
        for h in range(GQA_KV_HEADS):
            nh, _ = _rms(kv_ref[:, kvl + h * hd:kvl + (h + 1) * hd])
            kb_ref[:, h * hd:(h + 1) * hd] = _rope(nh * gb_ref[...], cb[...], s1b[...], s2b[...], hd // 4).astype(BF16)
        vb_ref[...] = kv_ref[:, kvl + nb:kvl + 2 * nb].astype(BF16)

    row = lambda w: pl.BlockSpec((rb, w), lambda i: (i, 0))
    fix = lambda w: pl.BlockSpec((1, w), lambda i: (0, 0))
    return _pcall(
        body,
        name="key_prep_fwd",
        out_shape=[jax.ShapeDtypeStruct((ta, kvl + LANE), BF16), jax.ShapeDtypeStruct((ta, nb), BF16),
                   jax.ShapeDtypeStruct((ta, nb), BF16)],
        grid=(ta // rb,),
        in_specs=[row(wkv), fix(kvl), fix(hd)] + [row(LANE)] * 3 + [row(hd)] * 3,
        out_specs=[row(kvl + LANE), row(nb), row(nb)],
        compiler_params=_cparams(("parallel",)),
    )(kv, kv_gain, kb_gain, *tabs)


def _key_prep_bwd(kv, kv_gain, kb_gain, tabs, dkin, dkb, dvb):
    ta, wkv = kv.shape
    kvl = MLA_KV_LORA
    nb = GQA_KV_HEADS * GQA_HEAD_DIM
    rb = ROW_BLOCK if ta % ROW_BLOCK == 0 else LANE
    hd = GQA_HEAD_DIM

    def body(kv_ref, g_ref, gb_ref, ca, s1a, s2a, cb, s1b, s2b, dkin_ref, dkb_ref, dvb_ref, dkv_ref, st_ref, stb_ref):
        @pl.when(pl.program_id(0) == 0)
        def _():
            st_ref[...] = jnp.zeros_like(st_ref)
            stb_ref[...] = jnp.zeros_like(stb_ref)

        xh, r = _rms(kv_ref[:, 0:kvl])
        dn = dkin_ref[:, 0:kvl]
        st_ref[0:1, :] += _colsum(dn * xh)
        dkv_ref[:, 0:kvl] = _rms_bwd(dn * g_ref[...], xh, r).astype(BF16)
        dpe = _rope_t(dkin_ref[:, kvl:kvl + LANE], ca[...], s1a[...], s2a[...], MLA_ROPE // 4)
        dkv_ref[:, kvl + 2 * nb:kvl + 2 * nb + LANE] = dpe.astype(BF16)
        for h in range(GQA_KV_HEADS):
            nh, rh = _rms(kv_ref[:, kvl + h * hd:kvl + (h + 1) * hd])
            dn_h = _rope_t(dkb_ref[:, h * hd:(h + 1) * hd], cb[...], s1b[...], s2b[...], hd // 4)
            stb_ref[0:1, :] += _colsum(dn_h * nh)
            dkv_ref[:, kvl + h * hd:kvl + (h + 1) * hd] = _rms_bwd(dn_h * gb_ref[...], nh, rh).astype(BF16)
        dkv_ref[:, kvl + nb:kvl + 2 * nb] = dvb_ref[...].astype(BF16)

    row = lambda w: pl.BlockSpec((rb, w), lambda i: (i, 0))
    fix = lambda w: pl.BlockSpec((1, w), lambda i: (0, 0))
    return _pcall(
        body,
        name="key_prep_bwd",
        out_shape=[jax.ShapeDtypeStruct((ta, wkv), BF16), jax.ShapeDtypeStruct((8, kvl), F32),
                   jax.ShapeDtypeStruct((8, hd), F32)],
        grid=(ta // rb,),
        in_specs=[row(wkv), fix(kvl), fix(hd)] + [row(LANE)] * 3 + [row(hd)] * 3 + [row(kvl + LANE), row(nb), row(nb)],
        out_specs=[row(wkv), pl.BlockSpec((8, kvl), lambda i: (0, 0)), pl.BlockSpec((8, hd), lambda i: (0, 0))],
        compiler_params=_cparams(("arbitrary",)),
    )(kv, kv_gain, kb_gain, *tabs, dkin, dkb, dvb)


def _q_prep_fwd(qg, q_gain, qb_gain, tabs, qscale):
    t = qg.shape[0]
    ql = MLA_Q_LORA
    hd = GQA_HEAD_DIM
    hb = GQA_HEADS * hd
    rb = min(ROW_BLOCK, t)

    def body(q_ref, g_ref, gb_ref, cb, s1b, s2b, cqn_ref, qb_ref):
        xh, _ = _rms(q_ref[:, 0:ql])
        cqn_ref[...] = (xh * g_ref[...]).astype(BF16)
        for h in range(GQA_HEADS):
            nh, _ = _rms(q_ref[:, ql + h * hd:ql + (h + 1) * hd])
            qh = _rope(nh * gb_ref[...], cb[...], s1b[...], s2b[...], hd // 4)
            qb_ref[:, h * hd:(h + 1) * hd] = (qh * qscale).astype(BF16)

    row = lambda w: pl.BlockSpec((rb, w), lambda i: (i, 0))
    fix = lambda w: pl.BlockSpec((1, w), lambda i: (0, 0))
    return _pcall(
        body,
        name="q_prep_fwd",
        out_shape=[jax.ShapeDtypeStruct((t, ql), BF16), jax.ShapeDtypeStruct((t, hb), BF16)],
        grid=(t // rb,),
        in_specs=[row(ql + hb), fix(ql), fix(hd)] + [row(hd)] * 3,
        out_specs=[row(ql), row(hb)],
        compiler_params=_cparams(("parallel",)),
    )(qg, q_gain, qb_gain, *tabs)


def _q_prep_bwd(qg, q_gain, qb_gain, tabs, dcqn, dqb, wpad, qscale):
    t = qg.shape[0]
    ql = MLA_Q_LORA
    hd = GQA_HEAD_DIM
    hb = GQA_HEADS * hd
    rb = min(ROW_BLOCK, t)

    def body(q_ref, g_ref, gb_ref, cb, s1b, s2b, dcqn_ref, dqb_ref, dq_ref, st_ref, stb_ref):
        @pl.when(pl.program_id(0) == 0)
        def _():
            st_ref[...] = jnp.zeros_like(st_ref)
            stb_ref[...] = jnp.zeros_like(stb_ref)

        xh, r = _rms(q_ref[:, 0:ql])
        dn = dcqn_ref[...]
        st_ref[0:1, :] += _colsum(dn * xh)
        dq_ref[:, 0:ql] = _rms_bwd(dn * g_ref[...], xh, r).astype(BF16)
        for h in range(GQA_HEADS):
            nh, rh = _rms(q_ref[:, ql + h * hd:ql + (h + 1) * hd])
            dn_h = _rope_t(dqb_ref[:, h * hd:(h + 1) * hd] * qscale, cb[...], s1b[...], s2b[...], hd // 4)
            stb_ref[0:1, :] += _colsum(dn_h * nh)
            dq_ref[:, ql + h * hd:ql + (h + 1) * hd] = _rms_bwd(dn_h * gb_ref[...], nh, rh).astype(BF16)
        if wpad:
            dq_ref[:, ql + hb:ql + hb + wpad] = jnp.zeros((rb, wpad), BF16)

    row = lambda w: pl.BlockSpec((rb, w), lambda i: (i, 0))
    fix = lambda w: pl.BlockSpec((1, w), lambda i: (0, 0))
    return _pcall(
        body,
        name="q_prep_bwd",
        out_shape=[jax.ShapeDtypeStruct((t, ql + hb + wpad), BF16), jax.ShapeDtypeStruct((8, ql), F32),
                   jax.ShapeDtypeStruct((8, hd), F32)],
        grid=(t // rb,),
        in_specs=[row(ql + hb), fix(ql), fix(hd)] + [row(hd)] * 3 + [row(ql), row(hb)],
        out_specs=[row(ql + hb + wpad), pl.BlockSpec((8, ql), lambda i: (0, 0)), pl.BlockSpec((8, hd), lambda i: (0, 0))],
        compiler_params=_cparams(("arbitrary",)),
    )(qg, q_gain, qb_gain, *tabs, dcqn, dqb)


def _rope_a(v, tabs, transpose, out_dtype, name, qscale):
    t, w = v.shape
    rb = min(ROW_BLOCK, t)
    fn = _rope_t if transpose else _rope

    def body(v_ref, c, s1, s2, o_ref):
        for h in range(w // MLA_SLOT):
            sl = slice(h * MLA_SLOT, (h + 1) * MLA_SLOT)
            o_ref[:, sl] = (fn(v_ref[:, sl].astype(F32), c[...], s1[...], s2[...], MLA_ROPE // 4) * qscale).astype(out_dtype)

    row = lambda ww: pl.BlockSpec((rb, ww), lambda i: (i, 0))
    return _pcall(
        body,
        name=name,
        out_shape=jax.ShapeDtypeStruct((t, w), out_dtype),
        grid=(t // rb,),
        in_specs=[row(w)] + [row(MLA_SLOT)] * 3,
        out_specs=row(w),
        compiler_params=_cparams(("parallel",)),
    )(v, *tabs)


def _merge_fwd(pa, pb, qg, gate_blk):
    t, d = pa.shape
    rb = min(ROW_BLOCK, t)

    def body(pa_ref, pb_ref, ga_ref, gb_ref, o_ref):
        o_ref[...] = (jax.nn.sigmoid(ga_ref[...]) * pa_ref[...] + jax.nn.sigmoid(gb_ref[...]) * pb_ref[...]).astype(BF16)

    row = pl.BlockSpec((rb, d), lambda i: (i, 0))
    return _pcall(
        body,
        name="merge_fwd",
        out_shape=jax.ShapeDtypeStruct((t, d), BF16),
        grid=(t // rb,),
        in_specs=[row, row, pl.BlockSpec((rb, d), lambda i: (i, gate_blk)), pl.BlockSpec((rb, d), lambda i: (i, gate_blk + 1))],
        out_specs=row,
        compiler_params=_cparams(("parallel",)),
    )(pa, pb, qg, qg)


def _merge_bwd(dm, pa, pb, qg, gate_blk):
    t, d = pa.shape
    rb = min(ROW_BLOCK, t)

    def body(dm_ref, pa_ref, pb_ref, ga_ref, gb_ref, dpa_ref, dpb_ref, dg_ref):
        dmv = dm_ref[...]
        sa = jax.nn.sigmoid(ga_ref[...])
        sb = jax.nn.sigmoid(gb_ref[...])
        dpa_ref[...] = (dmv * sa).astype(BF16)
        dpb_ref[...] = (dmv * sb).astype(BF16)
        dg_ref[:, 0:d] = (dmv * pa_ref[...] * (sa * (1.0 - sa))).astype(BF16)
        dg_ref[:, d:2 * d] = (dmv * pb_ref[...] * (sb * (1.0 - sb))).astype(BF16)

    row = pl.BlockSpec((rb, d), lambda i: (i, 0))
    return _pcall(
        body,
        name="merge_bwd",
        out_shape=[jax.ShapeDtypeStruct((t, d), BF16), jax.ShapeDtypeStruct((t, d), BF16),
                   jax.ShapeDtypeStruct((t, 2 * d), BF16)],
        grid=(t // rb,),
        in_specs=[row, row, row, pl.BlockSpec((rb, d), lambda i: (i, gate_blk)), pl.BlockSpec((rb, d), lambda i: (i, gate_blk + 1))],
        out_specs=[row, row, pl.BlockSpec((rb, 2 * d), lambda i: (i, 0))],
        compiler_params=_cparams(("parallel",)),
    )(dm, pa, pb, qg, qg)


def _resid_norm_mod(x, branch, gain, mods, name):
    t, d = x.shape
    rb = min(ROW_BLOCK, t)

    def body(x_ref, b_ref, g_ref, mod_ref, x1_ref, z_ref):
        x1 = x_ref[...] + mod_ref[0:1, :] * b_ref[...]
        x1_ref[...] = x1
        xh, _ = _rms(x1)
        z_ref[...] = ((xh * g_ref[...]) * (1.0 + mod_ref[2:3, :]) + mod_ref[1:2, :]).astype(BF16)

    row = pl.BlockSpec((rb, d), lambda i: (i, 0))
    return _pcall(
        body,
        name=name,
        out_shape=[jax.ShapeDtypeStruct((t, d), F32), jax.ShapeDtypeStruct((t, d), BF16)],
        grid=(t // rb,),
        in_specs=[row, row, pl.BlockSpec((1, d), lambda i: (0, 0)), pl.BlockSpec((8, d), lambda i: (0, 0))],
        out_specs=[row, row],
        compiler_params=_cparams(("parallel",)),
    )(x, branch, gain, mods)


def _norm2_bwd(x1, attn, gain, mods, dz2, dx2):
    t, d = x1.shape
    rb = min(ROW_BLOCK, t)

    def body(x1_ref, at_ref, g_ref, mod_ref, dz_ref, dx2_ref, dx1_ref, da_ref, st_ref):
        @pl.when(pl.program_id(0) == 0)
        def _():
            st_ref[...] = jnp.zeros_like(st_ref)

        xh, r = _rms(x1_ref[...])
        g = g_ref[...]
        dz = dz_ref[...]
        dxn = dz * (1.0 + mod_ref[1:2, :])
        st_ref[0:1, :] += _colsum(dz)
        st_ref[1:2, :] += _colsum(dz * (xh * g))
        st_ref[2:3, :] += _colsum(dxn * xh)
        dx1 = dx2_ref[...] + _rms_bwd(dxn * g, xh, r)
        dx1_ref[...] = dx1
        st_ref[3:4, :] += _colsum(dx1 * at_ref[...])
        da_ref[...] = (dx1 * mod_ref[0:1, :]).astype(BF16)

    row = pl.BlockSpec((rb, d), lambda i: (i, 0))
    return _pcall(
        body,
        name="norm2_mod_bwd",
        out_shape=[jax.ShapeDtypeStruct((t, d), F32), jax.ShapeDtypeStruct((t, d), BF16), jax.ShapeDtypeStruct((8, d), F32)],
        grid=(t // rb,),
        in_specs=[row, row, pl.BlockSpec((1, d), lambda i: (0, 0)), pl.BlockSpec((8, d), lambda i: (0, 0)), row, row],
        out_specs=[row, row, pl.BlockSpec((8, d), lambda i: (0, 0))],
        compiler_params=_cparams(("arbitrary",)),
    )(x1, attn, gain, mods, dz2, dx2)


def _final_loss(x1, ffn, gain, mods, target):
    t, d = x1.shape
    rb = min(ROW_BLOCK, t)
    nb = t // rb

    def body(x1_ref, f_ref, g_ref, mod_ref, tg_ref, dx2_ref, df_ref, st_ref):
        i = pl.program_id(0)

        @pl.when(i == 0)
        def _():
            st_ref[...] = jnp.zeros_like(st_ref)

        ffn_v = f_ref[...]
        g2 = mod_ref[0:1, :]
        x2 = x1_ref[...] + g2 * ffn_v
        xh, r = _rms(x2)
        g = g_ref[...]
        err = xh * g - tg_ref[...]
        st_ref[2:3, :] += _colsum(err * err) * (0.5 / d)
        dy = err * (1.0 / d)
        st_ref[0:1, :] += _colsum(dy * xh)
        dx2 = _rms_bwd(dy * g, xh, r)
        dx2_ref[...] = dx2
        st_ref[1:2, :] += _colsum(dx2 * ffn_v)
        df_ref[...] = (dx2 * g2).astype(BF16)

        @pl.when(i == nb - 1)
        def _():
            st_ref[3:4, :] = jnp.broadcast_to(jnp.sum(st_ref[2:3, :], axis=-1, keepdims=True), (1, d))

    row = pl.BlockSpec((rb, d), lambda i: (i, 0))
    return _pcall(
        body,
        name="final_norm_loss",
        out_shape=[jax.ShapeDtypeStruct((t, d), F32), jax.ShapeDtypeStruct((t, d), BF16), jax.ShapeDtypeStruct((8, d), F32)],
        grid=(nb,),
        in_specs=[row, row, pl.BlockSpec((1, d), lambda i: (0, 0)), pl.BlockSpec((8, d), lambda i: (0, 0)), row],
        out_specs=[row, row, pl.BlockSpec((8, d), lambda i: (0, 0))],
        compiler_params=_cparams(("arbitrary",)),
    )(x1, ffn, gain, mods, target)


def _row_ends(shape):
    rows = lax.broadcasted_iota(jnp.int32, shape, 0)
    return rows == 0, rows == shape[0] - 1


def _shift_dn(v, first):
    return jnp.where(first, 0.0, pltpu.roll(v, 1, 0))


def _shift_up(v, last):
    return jnp.where(last, 0.0, pltpu.roll(v, v.shape[0] - 1, 0))


def _conv_fwd(u, cw, cb):
    t, f2 = u.shape
    f = f2 // 2
    cbk = _tile(f, 256)
    nf = f // cbk

    def body(ua_ref, ub_ref, cwa_ref, cwb_ref, cba_ref, cbb_ref, h_ref, uc_ref):
        first, last = _row_ends((t, cbk))
        outs = []
        for u_ref, cw_ref, cb_ref in ((ua_ref, cwa_ref, cba_ref), (ub_ref, cwb_ref, cbb_ref)):
            uu, cwv = u_ref[...], cw_ref[...]
            outs.append(cb_ref[...] + cwv[0:1, :] * _shift_dn(uu, first) + cwv[1:2, :] * uu
                        + cwv[2:3, :] * _shift_up(uu, last))
        a, b = outs
        uc_ref[0] = a
        uc_ref[1] = b
        h_ref[...] = (a * jax.nn.sigmoid(a) * b).astype(BF16)

    ca = lambda r: pl.BlockSpec((r, cbk), lambda j: (0, j))
    cbs = lambda r: pl.BlockSpec((r, cbk), lambda j: (0, nf + j))
    return _pcall(
        body,
        name="conv_gate_fwd",
        out_shape=[jax.ShapeDtypeStruct((t, f), BF16), jax.ShapeDtypeStruct((2, t, f), F32)],
        grid=(nf,),
        in_specs=[ca(t), cbs(t), ca(3), cbs(3), ca(1), cbs(1)],
        out_specs=[ca(t), pl.BlockSpec((2, t, cbk), lambda j: (0, 0, j))],
        compiler_params=_cparams(("parallel",)),
    )(u, u, cw, cw, cb, cb)


def _conv_bwd(u, uc, cw, dh):
    t, f2 = u.shape
    f = f2 // 2
    cbk = _tile(f, 256)
    nf = f // cbk

    def body(ua_ref, ub_ref, uc_ref, cwa_ref, cwb_ref, dh_ref, du_ref, dcw_ref, dcb_ref):
        first, last = _row_ends((t, cbk))
        a, b = uc_ref[0], uc_ref[1]
        dh_v = dh_ref[...]
        sg = jax.nn.sigmoid(a)
        db = dh_v * (a * sg)
        da = dh_v * b * (sg * (1.0 + a * (1.0 - sg)))
        for idx, (dv, u_ref, cw_ref) in enumerate(((da, ua_ref, cwa_ref), (db, ub_ref, cwb_ref))):
            uu, cwv = u_ref[...], cw_ref[...]
            up, dn = _shift_up(dv, last), _shift_dn(dv, first)
            dcb_ref[idx] = _colsum(dv)
            dcw_ref[idx, 0:1, :] = _colsum(up * uu)
            dcw_ref[idx, 1:2, :] = _colsum(dv * uu)
            dcw_ref[idx, 2:3, :] = _colsum(dn * uu)
            du_ref[idx] = (cwv[0:1, :] * up + cwv[1:2, :] * dv + cwv[2:3, :] * dn).astype(BF16)

    ca = lambda r: pl.BlockSpec((r, cbk), lambda j: (0, j))
    cbs = lambda r: pl.BlockSpec((r, cbk), lambda j: (0, nf + j))
    o3 = lambda r: pl.BlockSpec((2, r, cbk), lambda j: (0, 0, j))
    return _pcall(
        body,
        name="conv_gate_bwd",
        out_shape=[jax.ShapeDtypeStruct((2, t, f), BF16), jax.ShapeDtypeStruct((2, 3, f), F32),
                   jax.ShapeDtypeStruct((2, 1, f), F32)],
        grid=(nf,),
        in_specs=[ca(t), cbs(t), o3(t), ca(3), cbs(3), ca(t)],
        out_specs=[o3(t), o3(3), o3(1)],
        compiler_params=_cparams(("parallel",)),
    )(u, u, uc, cw, cw, dh)


def _attention_fwd(q, kk, vv, *, hq, hkv, dk, dv, k_blk0, v_blk0, name):
    t = q.shape[0]
    tk = kk.shape[0]
    g_sz = hq // hkv
    tq = min(ATT_Q_BLOCK_FWD, t)

    def body(q_ref, k_ref, v_ref, o_ref, lse_ref):
        k = k_ref[...]
        v = v_ref[...]
        for j in range(g_sz):
            s = lax.dot_general(q_ref[:, j * dk:(j + 1) * dk], k, _DIMS["nt"], preferred_element_type=F32)
            m = jnp.max(s, axis=-1, keepdims=True)
            p = jnp.exp2(s - m)
            l = jnp.sum(p, axis=-1, keepdims=True)
            o = jnp.dot(p.astype(BF16), v, preferred_element_type=F32) / l
            o_ref[:, j * dv:(j + 1) * dv] = o.astype(BF16)
            lse_ref[0, :, j:j + 1] = m + jnp.log2(l)

    return _pcall(
        body,
        name=name,
        out_shape=[jax.ShapeDtypeStruct((t, hq * dv), BF16), jax.ShapeDtypeStruct((hkv, t, g_sz), F32)],
        grid=(hkv, t // tq),
        in_specs=[
            pl.BlockSpec((tq, g_sz * dk), lambda g, i: (i, g)),
            pl.BlockSpec((tk, dk), lambda g, i: (0, k_blk0 + g)),
            pl.BlockSpec((tk, dv), lambda g, i: (0, v_blk0 + g)),
        ],
        out_specs=[
            pl.BlockSpec((tq, g_sz * dv), lambda g, i: (i, g)),
            pl.BlockSpec((1, tq, g_sz), lambda g, i: (g, i, 0)),
        ],
        compiler_params=_cparams(("parallel", "parallel")),
    )(q, kk, vv)


def _attention_bwd(q, kk, vv, do, lse, *, hq, hkv, dk, dv, k_blk0, v_blk0, name):
    t = q.shape[0]
    tk = kk.shape[0]
    g_sz = hq // hkv
    tq = min(ATT_Q_BLOCK, t)

    def body(q_ref, k_ref, v_ref, do_ref, lse_ref, dq_ref, dk_ref, dv_ref):
        @pl.when(pl.program_id(1) == 0)
        def _():
            dk_ref[...] = jnp.zeros_like(dk_ref)
            dv_ref[...] = jnp.zeros_like(dv_ref)

        k = k_ref[...]
        v = v_ref[...]
        for j in range(g_sz):
            qj = q_ref[:, j * dk:(j + 1) * dk]
            doj = do_ref[:, j * dv:(j + 1) * dv]
            s = lax.dot_general(qj, k, _DIMS["nt"], preferred_element_type=F32)
            p = jnp.exp2(s - lse_ref[0, :, j:j + 1])
            dp = lax.dot_general(doj, v, _DIMS["nt"], preferred_element_type=F32)
            ds = (p * (dp - jnp.sum(p * dp, axis=-1, keepdims=True))).astype(BF16)
            dv_ref[...] += lax.dot_general(p.astype(BF16), doj, _DIMS["tn"], preferred_element_type=F32)
            dk_ref[...] += lax.dot_general(ds, qj, _DIMS["tn"], preferred_element_type=F32)
            dq_ref[:, j * dk:(j + 1) * dk] = jnp.dot(ds, k, preferred_element_type=F32)

        @pl.when(pl.program_id(1) == t // tq - 1)
        def _():
            dk_ref[...] *= LN2

    return _pcall(
        body,
        name=name,
        out_shape=[jax.ShapeDtypeStruct((t, hq * dk), F32), jax.ShapeDtypeStruct((tk, hkv * dk), F32),
                   jax.ShapeDtypeStruct((tk, hkv * dv), F32)],
        grid=(hkv, t // tq),
        in_specs=[
            pl.BlockSpec((tq, g_sz * dk), lambda g, i: (i, g)),
            pl.BlockSpec((tk, dk), lambda g, i: (0, k_blk0 + g)),
            pl.BlockSpec((tk, dv), lambda g, i: (0, v_blk0 + g)),
            pl.BlockSpec((tq, g_sz * dv), lambda g, i: (i, g)),
            pl.BlockSpec((1, tq, g_sz), lambda g, i: (g, i, 0)),
        ],
        out_specs=[
            pl.BlockSpec((tq, g_sz * dk), lambda g, i: (i, g)),
            pl.BlockSpec((tk, dk), lambda g, i: (0, g)),
            pl.BlockSpec((tk, dv), lambda g, i: (0, g)),
        ],
        compiler_params=_cparams(("parallel", "arbitrary")),
    )(q, kk, vv, do, lse)


def _silu(v):
    return v * jax.nn.sigmoid(v)


def _ada_fwd(conds, w_ada, b_ada_shard):
    r, d = conds.shape
    n = w_ada.shape[1]
    tn = _tile(n, 512)

    def body(c_ref, w_ref, b_ref, o_ref):
        s = _silu(c_ref[...]).astype(BF16)
        o_ref[...] = jnp.dot(s, w_ref[...].astype(BF16), preferred_element_type=F32) + b_ref[...]

    return _pcall(
        body,
        name="ada_fwd",
        out_shape=jax.ShapeDtypeStruct((r, n), F32),
        grid=(n // tn,),
        in_specs=[pl.BlockSpec((r, d), lambda j: (0, 0)), pl.BlockSpec((d, tn), lambda j: (0, j)),
                  pl.BlockSpec((1, tn), lambda j: (0, j))],
        out_specs=pl.BlockSpec((r, tn), lambda j: (0, j)),
        compiler_params=_cparams(("parallel",)),
    )(conds, w_ada, b_ada_shard)


def _cctx_partial(da16_shard, w_ada, c_ctx_row):
    d, n = w_ada.shape
    td = _tile(d, 512)

    def body(g_ref, w_ref, c_ref, o_ref):
        ds = lax.dot_general(g_ref[8:16, :].astype(BF16), w_ref[...].astype(BF16), _DIMS["nt"],
                             preferred_element_type=F32)
        cv = c_ref[...]
        sg = jax.nn.sigmoid(cv)
        o_ref[...] = ds * (sg * (1.0 + cv * (1.0 - sg)))

    return _pcall(
        body,
        name="cctx_partial",
        out_shape=jax.ShapeDtypeStruct((8, d), F32),
        grid=(d // td,),
        in_specs=[pl.BlockSpec((16, n), lambda j: (0, 0)), pl.BlockSpec((td, n), lambda j: (j, 0)),
                  pl.BlockSpec((1, td), lambda j: (0, j))],
        out_specs=pl.BlockSpec((8, td), lambda j: (0, j)),
        compiler_params=_cparams(("parallel",)),
    )(da16_shard, w_ada, c_ctx_row)


def _sum_parts(parts):
    p, _, n = parts.shape

    def body(p_ref, o_ref):
        acc = p_ref[0]
        for s in range(1, p):
            acc = acc + p_ref[s]
        o_ref[...] = acc

    return _pcall(
        body,
        name="sum_parts",
        out_shape=jax.ShapeDtypeStruct((1, n), F32),
        in_specs=[pl.BlockSpec(memory_space=pltpu.VMEM)],
        out_specs=pl.BlockSpec(memory_space=pltpu.VMEM),
    )(parts)


def _adam_math(w, g, m, v):
    m2 = ADAM_B1 * m + (1.0 - ADAM_B1) * g
    v2 = ADAM_B2 * v + (1.0 - ADAM_B2) * jnp.square(g)
    m_hat = m2 / (1.0 - ADAM_B1 ** ADAM_STEP)
    v_hat = v2 / (1.0 - ADAM_B2 ** ADAM_STEP)
    delta = -ADAM_LR * (m_hat / (jnp.sqrt(v_hat) + ADAM_EPS) + ADAM_WD * w)
    return delta, m2, v2


def _adamw(parts, w, m, v, name):
    p, r, c = parts.shape
    rb = _tile(r, max(8, (1 << 20) // (4 * c) // 8 * 8), 8)

    def body(p_ref, w_ref, m_ref, v_ref, g_ref, d_ref, m2_ref, v2_ref):
        g = p_ref[0].astype(F32)
        for s in range(1, p):
            g = g + p_ref[s].astype(F32)
        g_ref[...] = g
        d_ref[...], m2_ref[...], v2_ref[...] = _adam_math(w_ref[...], g, m_ref[...], v_ref[...])

    if w.ndim == 3:
        row = pl.BlockSpec((None, rb, c), lambda i: (0, i, 0))
    else:
        row = pl.BlockSpec((rb, c), lambda i: (i, 0))
    return _pcall(
        body,
        name=name,
        out_shape=[jax.ShapeDtypeStruct(w.shape, F32)] * 4,
        grid=(r // rb,),
        in_specs=[pl.BlockSpec((p, rb, c), lambda i: (0, i, 0)), row, row, row],
        out_specs=[row] * 4,
        compiler_params=_cparams(("parallel",)),
    )(parts, w, m, v)


def _adamw_ada(conds, da16, w, m, v):
    d, n = w.shape
    rb = _tile(d, 256, LANE)

    def body(s_ref, da_ref, w_ref, m_ref, v_ref, g_ref, d_ref, m2_ref, v2_ref):
        g = lax.dot_general(_silu(s_ref[...]).astype(BF16), da_ref[...].astype(BF16), _DIMS["tn"],
                            preferred_element_type=F32)
        g_ref[...] = g
        d_ref[...], m2_ref[...], v2_ref[...] = _adam_math(w_ref[...], g, m_ref[...], v_ref[...])

    row = pl.BlockSpec((rb, n), lambda i: (i, 0))
    return _pcall(
        body,
        name="adamw_w_ada",
        out_shape=[jax.ShapeDtypeStruct((d, n), F32)] * 4,
        grid=(d // rb,),
        in_specs=[pl.BlockSpec((16, rb), lambda i: (0, i)), pl.BlockSpec((16, n), lambda i: (0, 0)), row, row, row],
        out_specs=[row] * 4,
        compiler_params=_cparams(("parallel",)),
    )(conds, da16, w, m, v)


def _cast_bf16(a, name):
    _, r, c = a.shape
    rb = _tile(r, 512, 8)

    def body(a_ref, o_ref):
        o_ref[...] = a_ref[...].astype(BF16)

    return _pcall(body, name=name, out_shape=jax.ShapeDtypeStruct((r, c), BF16), grid=(r // rb,),
                  in_specs=[pl.BlockSpec((None, rb, c), lambda i: (0, i, 0))],
                  out_specs=pl.BlockSpec((rb, c), lambda i: (i, 0)), compiler_params=_cparams(("parallel",)))(a)


def _rope_tabs(t, rot):
    half, q = rot // 2, rot // 4
    n_rows = t // GRID_W
    row = jnp.repeat(jnp.arange(n_rows, dtype=F32), GRID_W)
    col = jnp.tile(jnp.arange(GRID_W, dtype=F32), n_rows)
    inv_freq = ROPE_THETA ** (-jnp.arange(0, half, 2, dtype=F32) / half)
    ang = jnp.concatenate([row[:, None] * inv_freq, col[:, None] * inv_freq], axis=-1)
    cos, sin = jnp.cos(ang), jnp.sin(ang)
    c0, c1, s0, s1 = cos[:, :q], cos[:, q:], sin[:, :q], sin[:, q:]
    z = jnp.zeros_like(s0)
    return (jnp.concatenate([c0, c0, c1, c1], -1), jnp.concatenate([-s0, z, -s1, z], -1),
            jnp.concatenate([z, s0, z, s1], -1))


def _pad_cols(a, left, total, fill=0.0):
    return jnp.pad(a, ((0, 0), (left, total - left - a.shape[1])), constant_values=fill)


def _with_ctx_rows(tab, tc, fill):
    return jnp.concatenate([tab, jnp.full((tc, tab.shape[1]), fill, F32)], axis=0)


def kernel(x, c, ctx, c_ctx, w_ada, b_ada, norm1_g, w_in, mla_q_norm_g, w_q_up, mla_kv_norm_g, w_kv_up, gqa_q_norm_g, gqa_k_norm_g, w_br_a, w_br_b, w_out, norm2_g, w_up, conv_w, conv_b, w_down, final_norm_g, loss_target, m_c_ctx, m_w_ada, m_b_ada, m_norm1_g, m_w_in, m_mla_q_norm_g, m_w_q_up, m_mla_kv_norm_g, m_w_kv_up, m_gqa_q_norm_g, m_gqa_k_norm_g, m_w_br_a, m_w_br_b, m_w_out, m_norm2_g, m_w_up, m_conv_w, m_conv_b, m_w_down, m_final_norm_g, v_c_ctx, v_w_ada, v_b_ada, v_norm1_g, v_w_in, v_mla_q_norm_g, v_w_q_up, v_mla_kv_norm_g, v_w_kv_up, v_gqa_q_norm_g, v_gqa_k_norm_g, v_w_br_a, v_w_br_b, v_w_out, v_norm2_g, v_w_up, v_conv_w, v_conv_b, v_w_down, v_final_norm_g):
    weights = dict(c_ctx=c_ctx, w_ada=w_ada, b_ada=b_ada, norm1_g=norm1_g, w_in=w_in, mla_q_norm_g=mla_q_norm_g,
                   w_q_up=w_q_up, mla_kv_norm_g=mla_kv_norm_g, w_kv_up=w_kv_up, gqa_q_norm_g=gqa_q_norm_g,
                   gqa_k_norm_g=gqa_k_norm_g, w_br_a=w_br_a, w_br_b=w_br_b, w_out=w_out, norm2_g=norm2_g, w_up=w_up,
                   conv_w=conv_w, conv_b=conv_b, w_down=w_down, final_norm_g=final_norm_g)
    mom_m = dict(c_ctx=m_c_ctx, w_ada=m_w_ada, b_ada=m_b_ada, norm1_g=m_norm1_g, w_in=m_w_in, mla_q_norm_g=m_mla_q_norm_g,
                 w_q_up=m_w_q_up, mla_kv_norm_g=m_mla_kv_norm_g, w_kv_up=m_w_kv_up, gqa_q_norm_g=m_gqa_q_norm_g,
                 gqa_k_norm_g=m_gqa_k_norm_g, w_br_a=m_w_br_a, w_br_b=m_w_br_b, w_out=m_w_out, norm2_g=m_norm2_g,
                 w_up=m_w_up, conv_w=m_conv_w, conv_b=m_conv_b, w_down=m_w_down, final_norm_g=m_final_norm_g)
    mom_v = dict(c_ctx=v_c_ctx, w_ada=v_w_ada, b_ada=v_b_ada, norm1_g=v_norm1_g, w_in=v_w_in, mla_q_norm_g=v_mla_q_norm_g,
                 w_q_up=v_w_q_up, mla_kv_norm_g=v_mla_kv_norm_g, w_kv_up=v_w_kv_up, gqa_q_norm_g=v_gqa_q_norm_g,
                 gqa_k_norm_g=v_gqa_k_norm_g, w_br_a=v_w_br_a, w_br_b=v_w_br_b, w_out=v_w_out, norm2_g=v_norm2_g,
                 w_up=v_w_up, conv_w=v_conv_w, conv_b=v_conv_b, w_down=v_w_down, final_norm_g=v_final_norm_g)
    order = list(weights)

    my_idx = 4 * lax.axis_index("x") + 2 * lax.axis_index("y") + lax.axis_index("c")
    xs, cts, tgt = x[0], ctx[0], loss_target[0]
    t, d = xs.shape
    tc = cts.shape[0]
    ta = t + tc
    kvl, ql = MLA_KV_LORA, MLA_Q_LORA
    nb = GQA_KV_HEADS * GQA_HEAD_DIM
    hb = GQA_HEADS * GQA_HEAD_DIM
    ha = MLA_HEADS
    f2 = w_up.shape[2] * N_DEV
    ff = f2 // 2

    big = ["w_in", "w_q_up", "w_kv_up", "w_br_a", "w_br_b", "w_out", "w_up", "w_down"]
    nw = len(big)
    del nw
    _ORDER_AFTER.clear()
    shards = {n: _cast_bf16(weights[n], "cast_" + n) for n in big}
    c_idx = jnp.reshape(lax.axis_index("c"), (1,)).astype(jnp.int32)

    def gather_start(names, dep):
        shs = [shards[n] for n in names]
        land = [lax.empty((N_DEV,) + s.shape, BF16) for s in shs]
        if dep is not None:
            _after(dep)
        s, r, arrs, tok = _split_start("gather_ici_start_" + names[0], shs + land, _gather_ici_copies(len(names)),
                                       5 * len(names))
        return dict(names=names, s=s, r=r, arrs=arrs, tok=tok)

    def gather_relay(g, after):
        n = len(g["names"])
        arrs = _split_wait("gather_ici_wait_" + g["names"][0], g["s"], g["r"], g["arrs"], _gather_ici_copies(n), after)
        s, r, bufs, tok = _split_start("gather_d2d_start_" + g["names"][0], arrs[n:], _gather_d2d_copies(n), 3 * n)
        g.update(s2=s, r2=r, bufs=bufs)
        return tok

    def gather_finish(g, after):
        n = len(g["names"])
        bufs = _split_wait("gather_d2d_wait_" + g["names"][0], g["s2"], g["r2"], g["bufs"], _gather_d2d_copies(n), after)
        return dict(zip(g["names"], bufs))

    c_all, cw_all = _all_gather([jnp.pad(c, ((0, 7), (0, 0))), jnp.pad(conv_w[0], ((0, 5), (0, 0)))], "gather_cond")
    conv_w_f = jnp.transpose(cw_all[:, :3, :], (1, 0, 2)).reshape(3, f2)
    conds = jnp.concatenate([c_all[:, 0, :], c_ctx[None, :], jnp.zeros((7, d), F32)], axis=0)
    ncol = w_ada.shape[2]
    b_shard = lax.dynamic_slice_in_dim(b_ada, my_idx * ncol, ncol, axis=1)
    ada_shard = _ada_fwd(conds, w_ada[0], b_shard)
    (ada_all,) = _all_gather([ada_shard], "gather_ada")
    ada = jnp.transpose(ada_all, (1, 0, 2)).reshape(16, N_DEV * ncol)
    lat = lax.dynamic_slice_in_dim(ada, my_idx, 1, axis=0).reshape(6, d)
    cxt = ada[8].reshape(6, d)
    zero2 = jnp.zeros((2, d), F32)
    mods1 = jnp.concatenate([lat[0:2], cxt[0:2], jnp.zeros((4, d), F32)], axis=0)
    mods2 = jnp.concatenate([lat[2:3], lat[3:4], lat[4:5], jnp.zeros((5, d), F32)], axis=0)
    mods2b = jnp.concatenate([lat[2:3], lat[4:5], jnp.zeros((6, d), F32)], axis=0)
    mods3 = jnp.concatenate([lat[5:6], jnp.zeros((7, d), F32)], axis=0)
    del zero2

    g0 = gather_start(["w_in"], ada_all)
    g1 = gather_start(["w_q_up", "w_kv_up", "w_br_a", "w_br_b", "w_out"], g0["tok"])
    g2 = gather_start(["w_up"], g1["tok"])
    g3 = gather_start(["w_down"], g2["tok"])

    ca, s1a, s2a = _rope_tabs(t, MLA_ROPE)
    cb_, s1b, s2b = _rope_tabs(t, GQA_HEAD_DIM)
    q_tabs_a = (_pad_cols(jnp.concatenate([jnp.ones((t, MLA_NOPE), F32), ca], 1), 0, MLA_SLOT),
                _pad_cols(s1a, MLA_NOPE, MLA_SLOT), _pad_cols(s2a, MLA_NOPE, MLA_SLOT))
    q_tabs_b = (cb_, s1b, s2b)
    k_tabs = (_with_ctx_rows(_pad_cols(ca, 0, LANE), tc, 1.0), _with_ctx_rows(_pad_cols(s1a, 0, LANE), tc, 0.0),
              _with_ctx_rows(_pad_cols(s2a, 0, LANE), tc, 0.0),
              _with_ctx_rows(cb_, tc, 1.0), _with_ctx_rows(s1b, tc, 0.0), _with_ctx_rows(s2b, tc, 0.0))

    def cols_full(g):
        return jnp.transpose(g, (1, 0, 2)).reshape(g.shape[1], N_DEV * g.shape[2])

    _after(gather_relay(g0, mods1))
    z_all = _norm_mod_fwd(cts, xs, norm1_g, mods1)
    gathered = gather_finish(g0, z_all)
    w_in_f = cols_full(gathered["w_in"])
    o_kpe, o_kb, o_vb = kvl, kvl + MLA_ROPE, kvl + MLA_ROPE + nb
    o_q = o_vb + nb
    o_g = o_q + ql + hb
    wkv_w = kvl + 2 * nb + LANE
    w_kv_p = jnp.concatenate([w_in_f[:, :kvl], w_in_f[:, o_kb:o_q], w_in_f[:, o_kpe:o_kb],
                              jnp.zeros((d, LANE - MLA_ROPE), BF16)], axis=1)
    q_w = ql + hb
    q_pad = (-q_w) % 512 if d >= 512 else (-q_w) % d
    gate_blk = (q_w + q_pad) // d
    assert (q_w + q_pad) % d == 0
    w_qg_p = jnp.concatenate([w_in_f[:, o_q:o_g], jnp.zeros((d, q_pad), BF16), w_in_f[:, o_g:]], axis=1)

    kv_all = _mm(z_all, w_kv_p, "nn", F32, "proj_kv", tm=1152, tn=wkv_w)
    qg = _mm(z_all, w_qg_p, "nn", F32, "proj_qg", tm=1024, tn=1024, rows=t)
    _after(gather_relay(g1, qg))
    kin, k_b, v_b = _key_prep_fwd(kv_all, mla_kv_norm_g, gqa_k_norm_g, k_tabs)
    sc_a = float((MLA_NOPE + MLA_ROPE) ** -0.5) * LOG2E
    sc_b = float(GQA_HEAD_DIM ** -0.5) * LOG2E
    cqn, q_b = _q_prep_fwd(qg, mla_q_norm_g, gqa_q_norm_g, q_tabs_b, sc_b)
    gathered.update(gather_finish(g1, q_b))

    wq_f = cols_full(gathered["w_q_up"]).reshape(ql, ha, MLA_NOPE + MLA_ROPE)
    wq_ext = jnp.pad(wq_f, ((0, 0), (0, 0), (0, MLA_SLOT - MLA_NOPE - MLA_ROPE))).reshape(ql, ha * MLA_SLOT)
    wkv_f = cols_full(gathered["w_kv_up"]).reshape(kvl, ha, MLA_NOPE + MLA_V)
    wk_slots = jnp.pad(wkv_f[:, :, :MLA_NOPE], ((0, 0), (0, 0), (0, MLA_SLOT - MLA_NOPE))).reshape(kvl, ha * MLA_SLOT)
    wv_cols = wkv_f[:, :, MLA_NOPE:].reshape(kvl, ha * MLA_V)
    e_slot = jnp.pad(jnp.eye(MLA_ROPE, dtype=BF16),
                     ((0, LANE - MLA_ROPE), (MLA_NOPE, MLA_SLOT - MLA_NOPE - MLA_ROPE)))
    e_rows = jnp.concatenate([jnp.tile(e_slot, (1, ha)), jnp.zeros((LANE, ha * MLA_V), BF16)], axis=1)
    wkv_ext = jnp.concatenate([jnp.concatenate([wk_slots, wv_cols], axis=1), e_rows], axis=0)
    w_bra = cols_full(gathered["w_br_a"])
    w_brb = cols_full(gathered["w_br_b"])
    w_out_f = gathered["w_out"].reshape(d, d)

    kv_a = _mm(kin, wkv_ext, "nn", BF16, "kv_up", tm=1152, tn=1024)
    qa_raw = _mm(cqn, wq_ext, "nn", F32, "q_up", tm=1024, tn=1024)
    q_a = _rope_a(qa_raw, q_tabs_a, False, BF16, "rope_q_fwd", sc_a)
    att_a = dict(hq=ha, hkv=ha, dk=MLA_SLOT, dv=MLA_V, k_blk0=0, v_blk0=ha * MLA_SLOT // MLA_V)
    att_b = dict(hq=GQA_HEADS, hkv=GQA_KV_HEADS, dk=GQA_HEAD_DIM, dv=GQA_HEAD_DIM, k_blk0=0, v_blk0=0)
    o_a, lse_a = _attention_fwd(q_a, kv_a, kv_a, name="attn_a_fwd", **att_a)
    o_b, lse_b = _attention_fwd(q_b, k_b, v_b, name="attn_b_fwd", **att_b)
    _after(gather_relay(g2, o_b))
    pa = _mm(o_a, w_bra, "nn", F32, "br_a", tm=1024, tn=1024)
    pb = _mm(o_b, w_brb, "nn", F32, "br_b", tm=1024, tn=1024)
    merged = _merge_fwd(pa, pb, qg, gate_blk)
    attn = _mm(merged, w_out_f, "nn", F32, "w_out", tm=1024, tn=1024)
    x1, z2 = _resid_norm_mod(xs, attn, norm2_g, mods2, "resid_norm2_fwd")
    w_up3 = gather_finish(g2, z2)["w_up"]
    _after(gather_relay(g3, z2))
    u = _mm_up_fwd(z2, w_up3, "w_up")
    w_down_f = gather_finish(g3, u)["w_down"].reshape(ff, d)
    h, uc = _conv_fwd(u, conv_w_f, conv_b)
    ffn = _mm(h, w_down_f, "nn", F32, "w_down", tm=1024, tn=1024, tk=2816)

    def to_shards(g):
        return jnp.transpose(g.reshape(g.shape[0], N_DEV, g.shape[1] // N_DEV), (1, 0, 2))

    def reduce_start(tag, names, sends):
        n = len(sends)
        land = [lax.empty((4,) + s.shape[1:], s.dtype) for s in sends]
        s, r, arrs, tok = _split_start("reduce_d2d_start_" + tag, sends + land, _reduce_d2d_copies(n), 4 * n)
        return dict(tag=tag, names=names, s=s, r=r, arrs=arrs, tok=tok)

    def reduce_relay(g, after):
        n = len(g["names"])
        arrs = _split_wait("reduce_d2d_wait_" + g["tag"], g["s"], g["r"], g["arrs"], _reduce_d2d_copies(n), after)
        sums = [_pair_sum(arrs[a], arrs[n + a], c_idx, "pair_sum_" + g["names"][a]) for a in range(n)]
        land = [lax.empty(s.shape, s.dtype) for s in sums]
        s, r, arrs2, tok = _split_start("reduce_ici_start_" + g["tag"], sums + land, _reduce_ici_copies(n), 4 * n)
        g.update(s2=s, r2=r, arrs2=arrs2)
        return tok

    def reduce_finish(g, after):
        n = len(g["names"])
        arrs2 = _split_wait("reduce_ici_wait_" + g["tag"], g["s2"], g["r2"], g["arrs2"], _reduce_ici_copies(n), after)
        return dict(zip(g["names"], arrs2[n:]))

    dx2, dffn, st_fin = _final_loss(x1, ffn, final_norm_g[None, :], mods3, tgt)
    loss = lax.psum(st_fin[3, 0], MESH_AXES)
    dh = _mm(dffn, w_down_f, "nt", F32, "d_h", tm=1024, tn=1024)
    g_w_down = _mm(h, dffn, "tn", BF16, "g_w_down", tm=512, tn=1024)
    r_down = reduce_start("down", ["w_down"], [g_w_down.reshape(N_DEV, ff // N_DEV, d)])
    _after(r_down["tok"])
    du3, dcw, dcb = _conv_bwd(u, uc, conv_w_f, dh)
    dz2 = _mm_up_dz(du3, w_up3, "d_z2")
    g_w_up = _mm_up_gw(z2, du3, N_DEV, "g_w_up")
    g_conv_w = jnp.concatenate([dcw[0], dcw[1]], axis=1)
    tok = reduce_relay(r_down, g_w_up)
    _after(tok)
    r_up = reduce_start("up", ["w_up", "conv_w"], [g_w_up, to_shards(jnp.pad(g_conv_w, ((0, 5), (0, 0))))])
    _after(tok, r_up["tok"])
    dx1, dattn, st_n2 = _norm2_bwd(x1, attn, norm2_g, mods2b, dz2, dx2)
    dmerged = _mm(dattn, w_out_f, "nt", F32, "d_merged", tm=1024, tn=1024)
    g_w_out = _mm(merged, dattn, "tn", BF16, "g_w_out", tm=1024, tn=1024)
    dpa, dpb, dgates = _merge_bwd(dmerged, pa, pb, qg, gate_blk)
    do_a = _mm(dpa, w_bra, "nt", BF16, "d_o_a", tm=1024, tn=1024)
    do_b = _mm(dpb, w_brb, "nt", BF16, "d_o_b", tm=1024, tn=1024)
    g_w_bra = _mm(o_a, dpa, "tn", BF16, "g_w_br_a", tm=1024, tn=1024)
    g_w_brb = _mm(o_b, dpb, "tn", BF16, "g_w_br_b", tm=1024, tn=1024)
    tok = reduce_relay(r_up, g_w_brb)
    _after(tok)
    r_out = reduce_start("out", ["w_out", "w_br_a", "w_br_b"],
                         [g_w_out.reshape(N_DEV, d // N_DEV, d), to_shards(g_w_bra), to_shards(g_w_brb)])
    _after(tok, r_out["tok"])
    dq_a, dk_a, dv_a = _attention_bwd(q_a, kv_a, kv_a, do_a, lse_a, name="attn_a_bwd", **att_a)
    dq_b, dk_b, dv_b = _attention_bwd(q_b, k_b, v_b, do_b, lse_b, name="attn_b_bwd", **att_b)
    _after(reduce_relay(r_out, dv_b))
    dqa_raw = _rope_a(dq_a, q_tabs_a, True, BF16, "rope_q_bwd", sc_a * LN2)
    dcqn = _mm(dqa_raw, wq_ext, "nt", F32, "d_cqn", tm=1024, tn=ql)
    g_wq_ext = _mm(cqn, dqa_raw, "tn", BF16, "g_w_q_up", tm=ql, tn=1024)
    dq_p, st_q, st_qb = _q_prep_bwd(qg, mla_q_norm_g, gqa_q_norm_g, q_tabs_b, dcqn, dq_b, q_pad, sc_b * LN2)
    dkin = _mm_cat_nt([(dk_a, wkv_ext, 0), (dv_a, wkv_ext, ha * MLA_SLOT)], F32, "d_kin", tm=1152, tn=kvl + LANE)
    g_wkv_ext = _mm_cat_tn(kin, [dk_a, dv_a], BF16, "g_w_kv_up", tm=kvl + LANE, tn=min(1024, ha * MLA_V))
    dkv_p, st_kv, st_kb = _key_prep_bwd(kv_all, mla_kv_norm_g, gqa_k_norm_g, k_tabs, dkin, dk_b, dv_b)
    g_wq = g_wq_ext.reshape(ql, ha, MLA_SLOT)[:, :, :MLA_NOPE + MLA_ROPE].reshape(ql, ha * (MLA_NOPE + MLA_ROPE))
    g_wkv = jnp.concatenate([g_wkv_ext[:kvl, :ha * MLA_SLOT].reshape(kvl, ha, MLA_SLOT)[:, :, :MLA_NOPE],
                             g_wkv_ext[:kvl, ha * MLA_SLOT:].reshape(kvl, ha, MLA_V)], axis=2).reshape(kvl, ha * (MLA_NOPE + MLA_V))
    r_qkv = reduce_start("qkv", ["w_q_up", "w_kv_up"], [to_shards(g_wq), to_shards(g_wkv)])
    _after(r_qkv["tok"])
    qw_p = q_w + q_pad
    dz_lat = _mm_sum_nt([(dq_p, 0, w_qg_p, 0, qw_p), (dgates, 0, w_qg_p, qw_p, d), (dgates, d, w_qg_p, qw_p + d, d),
                         (dkv_p, 0, w_kv_p, 0, wkv_w)], F32, "d_z_lat", rows=t)
    dz_ctx = _mm(dkv_p, w_kv_p, "nt", F32, "d_z_ctx", tm=min(ROW_BLOCK, tc), tn=1024, a_row_off=t)
    tok_q = reduce_relay(r_qkv, dz_ctx)
    _after(tok_q)
    grad_x, st_n1 = _norm1_bwd(cts, xs, norm1_g, mods1, dz_ctx, dz_lat, dx1)

    d_lat = jnp.concatenate([st_n1[0], st_n1[1], st_n2[3], st_n2[0], st_n2[1], st_fin[1]])
    d_cxt = jnp.concatenate([st_n1[3], st_n1[4], jnp.zeros((4 * d,), F32)])
    small = jnp.concatenate([d_lat, d_cxt, st_n1[2], st_q[0], st_kv[0], st_qb[0], st_kb[0], st_n2[2],
                             jnp.concatenate([dcb[0, 0], dcb[1, 0]]), st_fin[0]])
    n_small = small.shape[0]
    pad_small = (-n_small) % LANE
    (small_all,) = _all_gather([jnp.pad(small, (0, pad_small)).reshape(1, -1)], "gather_small")
    offs = {}
    o = 0
    for nm, ln in (("d_lat", 6 * d), ("d_cxt", 6 * d), ("norm1_g", d), ("mla_q_norm_g", ql), ("mla_kv_norm_g", kvl),
                   ("gqa_q_norm_g", GQA_HEAD_DIM), ("gqa_k_norm_g", GQA_HEAD_DIM), ("norm2_g", d), ("conv_b", f2),
                   ("final_norm_g", d)):
        offs[nm] = (o, ln)
        o += ln

    def part(nm):
        a, ln = offs[nm]
        return small_all[:, :, a:a + ln]

    d_lat_all = part("d_lat")[:, 0, :]
    d_cxt_sum = _sum_parts(part("d_cxt"))
    da16 = jnp.concatenate([d_lat_all, d_cxt_sum, jnp.zeros((7, 6 * d), F32)], axis=0)
    da16_shard = lax.dynamic_slice_in_dim(da16, my_idx * ncol, ncol, axis=1)
    cc_part = _cctx_partial(da16_shard, w_ada[0], c_ctx[None, :])
    (cc_all,) = _all_gather([cc_part], "gather_cctx")
    cc_parts = cc_all[:, 0:1, :]
    _after(cc_all)
    g_wkv_p = _mm(z_all, dkv_p, "tn", BF16, "g_w_in_kv", tm=1024, tn=wkv_w)
    g_wqg_p = _mm_cat_tn(z_all, [dq_p, dgates], BF16, "g_w_in_qg", tm=1024, tn=min(1024, d), rows=t)
    g_w_in = jnp.concatenate([g_wkv_p[:, :kvl], g_wkv_p[:, kvl + 2 * nb:kvl + 2 * nb + MLA_ROPE],
                              g_wkv_p[:, kvl:kvl + 2 * nb], g_wqg_p[:, :q_w], g_wqg_p[:, q_w + q_pad:]], axis=1)
    r_in = reduce_start("in", ["w_in"], [to_shards(g_w_in)])

    res = {}
    _after(r_in["tok"])

    def upd(nm, parts, shape2):
        del shape2
        wv, mv, vv = weights[nm], mom_m[nm], mom_v[nm]
        if wv.ndim == 1:
            wv, mv, vv = (a.reshape(1, -1) for a in (wv, mv, vv))
        outs = _adamw(parts, wv, mv, vv, "adamw_" + nm)
        res[nm] = [o_.reshape(weights[nm].shape) for o_ in outs]

    for nm in ("norm1_g", "mla_q_norm_g", "mla_kv_norm_g", "gqa_q_norm_g", "gqa_k_norm_g", "norm2_g", "conv_b",
               "final_norm_g"):
        upd(nm, part(nm), (1, offs[nm][1]))
    upd("c_ctx", cc_parts, (1, d))
    b_parts = jnp.concatenate([d_lat_all[:, None, :], d_cxt_sum[None]], axis=0)
    upd("b_ada", b_parts, (1, 6 * d))
    _after(r_in["tok"])
    outs = _adamw_ada(conds, da16_shard, w_ada[0], m_w_ada[0], v_w_ada[0])
    res["w_ada"] = [o_[None] for o_ in outs]
    last = reduce_relay(r_in, outs[0])
    for grp in (r_down, r_up, r_out, r_qkv, r_in):
        recv = reduce_finish(grp, last)
        for nm in grp["names"]:
            parts = recv[nm][:, :3, :] if nm == "conv_w" else recv[nm]
            upd(nm, parts, weights[nm].shape[1:])
            last = res[nm][0]

    return (loss, grad_x[None], *[res[n][0] for n in order], *[res[n][1] for n in order],
            *[res[n][2] for n in order], *[res[n][3] for n in order])
```

```python
import functools

import jax
import jax.numpy as jnp
from jax import lax
from jax.experimental import pallas as pl
from jax.experimental.pallas import tpu as pltpu

F32 = jnp.float32
BF16 = jnp.bfloat16

GRID_W = 64
ROPE_THETA = 10000.0
NORM_EPS = 1e-6
MLA_HEADS = 8
MLA_Q_LORA = 768
MLA_KV_LORA = 512
MLA_NOPE = 128
MLA_ROPE = 64
MLA_V = 128
GQA_HEADS = 8
GQA_KV_HEADS = 2
GQA_HEAD_DIM = 128
ADAM_LR = 0.001
ADAM_B1 = 0.9
ADAM_B2 = 0.999
ADAM_EPS = 1e-08
ADAM_WD = 0.01
ADAM_STEP = 10

N_DEV = 8
MESH_AXES = ("x", "y", "c")
LANE = 128
MLA_SLOT = 2 * LANE
VMEM_LIMIT = 56 * 1024 * 1024
ROW_BLOCK = 256
ATT_Q_BLOCK = 512
ATT_Q_BLOCK_FWD = 512
LN2 = 0.6931471805599453
LOG2E = 1.4426950408889634
MESH_ID = pl.DeviceIdType.MESH


def _tile(n, pref, align=LANE):
    if n <= pref:
        return n
    best = None
    t = align
    while t <= pref:
        if n % t == 0:
            best = t
        t += align
    assert best is not None, (n, pref, align)
    return best


def _cparams(sem=None):
    return pltpu.CompilerParams(dimension_semantics=sem, vmem_limit_bytes=VMEM_LIMIT)


_ORDER_AFTER = []


def _after(*arrays):
    _ORDER_AFTER.extend(arrays)


def _pcall(body, *, in_specs, **kw):
    deps = tuple(_ORDER_AFTER)
    _ORDER_AFTER.clear()
    if not deps:
        return pl.pallas_call(body, in_specs=in_specs, **kw)
    n_in, n_dep = len(in_specs), len(deps)

    def with_deps(*refs):
        body(*refs[:n_in], *refs[n_in + n_dep:])

    call = pl.pallas_call(with_deps, in_specs=list(in_specs) + [pl.BlockSpec(memory_space=pl.ANY)] * n_dep, **kw)
    return lambda *args: call(*args, *deps)


def _all_gather(arrs, name):
    n = len(arrs)

    def body(*refs):
        ins = refs[:n]
        outs = refs[n:2 * n]
        send_sems, recv_sems, local_sems = refs[2 * n:]
        x, y, c = lax.axis_index("x"), lax.axis_index("y"), lax.axis_index("c")
        me, sibling = (x, y, c), (x, y, 1 - c)
        chips = [(1 - x, y), (x, 1 - y), (1 - x, 1 - y)]

        def rows(a, dev):
            px, py, pc = dev
            return outs[a].at[4 * px + 2 * py + pc]

        def copy(a, k, block, to, src=None):
            return pltpu.make_async_remote_copy(
                src_ref=rows(a, block) if src is None else src,
                dst_ref=rows(a, block),
                send_sem=send_sems.at[7 * a + k],
                recv_sem=recv_sems.at[7 * a + k],
                device_id=to,
                device_id_type=MESH_ID,
            )

        mine = [pltpu.make_async_copy(ins[a], rows(a, me), local_sems.at[a]) for a in range(n)]
        for cp in mine:
            cp.start()
        first = []
        for a in range(n):
            first.append(copy(a, 0, me, sibling, src=ins[a]))
            first += [copy(a, 1 + j, me, (*chip, c), src=ins[a]) for j, chip in enumerate(chips)]
        for cp in first:
            cp.start()
        passed = []
        for j, chip in enumerate(chips):
            for a in range(n):
                copy(a, 1 + j, (*chip, c), me).wait_recv()
                fwd = copy(a, 4 + j, (*chip, c), sibling)
                fwd.start()
                passed.append(fwd)
        for a in range(n):
            copy(a, 0, sibling, me).wait_recv()
            for j, chip in enumerate(chips):
                copy(a, 4 + j, (*chip, 1 - c), me).wait_recv()
        for cp in first + passed:
            cp.wait_send()
        for cp in mine:
            cp.wait()

    any_spec = pl.BlockSpec(memory_space=pl.ANY)
    outs = _pcall(
        body,
        name=name,
        out_shape=[jax.ShapeDtypeStruct((N_DEV,) + a.shape, a.dtype) for a in arrs],
        in_specs=[any_spec] * n,
        out_specs=[any_spec] * n,
        scratch_shapes=[
            pltpu.SemaphoreType.DMA((7 * n,)),
            pltpu.SemaphoreType.DMA((7 * n,)),
            pltpu.SemaphoreType.DMA((n,)),
        ],
    )(*arrs)
    return list(outs)


def _all_to_all(arrs, name):
    n = len(arrs)

    def body(*refs):
        ins = refs[:n]
        outs = refs[n:2 * n]
        send_sems, recv_sems, local_sems = refs[2 * n:]
        x, y, c = lax.axis_index("x"), lax.axis_index("y"), lax.axis_index("c")
        my_idx = 4 * x + 2 * y + c

        def peer(k):
            fx, fy, fc = (k >> 2) & 1, (k >> 1) & 1, k & 1
            return (x ^ fx if fx else x, y ^ fy if fy else y, c ^ fc if fc else c)

        def copy(a, k):
            px, py, pc = peer(k)
            return pltpu.make_async_remote_copy(
                src_ref=ins[a].at[4 * px + 2 * py + pc],
                dst_ref=outs[a].at[my_idx],
                send_sem=send_sems.at[7 * a + k - 1],
                recv_sem=recv_sems.at[7 * a + k - 1],
                device_id=(px, py, pc),
                device_id_type=MESH_ID,
            )

        mine = [pltpu.make_async_copy(ins[a].at[my_idx], outs[a].at[my_idx], local_sems.at[a]) for a in range(n)]
        for cp in mine:
            cp.start()
        order = [1, 4, 2, 5, 3, 6, 7]
        cps = [copy(a, k) for k in order for a in range(n)]
        for cp in cps:
            cp.start()
        for cp in cps:
            cp.wait()
        for cp in mine:
            cp.wait()

    any_spec = pl.BlockSpec(memory_space=pl.ANY)
    outs = _pcall(
        body,
        name=name,
        out_shape=[jax.ShapeDtypeStruct(a.shape, a.dtype) for a in arrs],
        in_specs=[any_spec] * n,
        out_specs=[any_spec] * n,
        scratch_shapes=[
            pltpu.SemaphoreType.DMA((7 * n,)),
            pltpu.SemaphoreType.DMA((7 * n,)),
            pltpu.SemaphoreType.DMA((n,)),
        ],
    )(*arrs)
    return list(outs)


_HBM = pl.BlockSpec(memory_space=pltpu.HBM)
_SEM = pl.BlockSpec(memory_space=pltpu.SEMAPHORE)
_EFFECT = pltpu.SideEffectType.DATAFLOW_SIDE_EFFECTING


def _descriptors(copies, send_sems, recv_sems):
    descs = []
    for i, (src, dst, dev) in enumerate(copies):
        if dev is None:
            descs.append(pltpu.make_async_copy(src, dst, recv_sems.at[i]))
        else:
            descs.append(pltpu.make_async_remote_copy(src_ref=src, dst_ref=dst, send_sem=send_sems.at[i],
                                                      recv_sem=recv_sems.at[i], device_id=dev, device_id_type=MESH_ID))
    return descs


def _split_start(name, arrays, copies_fn, n_copies):
    n = len(arrays)

    def body(*refs):
        send_sems, recv_sems = refs[n], refs[n + 1]
        token = refs[2 * n + 2]
        for dsc in _descriptors(copies_fn(refs[:n]), send_sems, recv_sems):
            dsc.start()
        token[...] = jnp.zeros_like(token)

    outs = _pcall(
        body,
        name=name,
        out_shape=(pltpu.SemaphoreType.DMA((n_copies,)), pltpu.SemaphoreType.DMA((n_copies,)),
                   *[pltpu.HBM(a.shape, a.dtype) for a in arrays], jax.ShapeDtypeStruct((8, LANE), F32)),
        in_specs=[_HBM] * n,
        out_specs=(_SEM, _SEM, *[_HBM] * n, pl.BlockSpec(memory_space=pltpu.VMEM)),
        input_output_aliases={i: 2 + i for i in range(n)},
        compiler_params=pltpu.CompilerParams(has_side_effects=_EFFECT),
    )(*[pltpu.with_memory_space_constraint(a, pltpu.HBM) for a in arrays])
    return outs[0], outs[1], list(outs[2:2 + n]), outs[2 + n]


def _split_wait(name, send_sems, recv_sems, arrays, copies_fn, after):
    n = len(arrays)

    def body(*refs):
        for dsc, (_, _, dev) in zip(_descriptors(copies_fn(refs[:n]), refs[n], refs[n + 1]), copies_fn(refs[:n])):
            if dev is None:
                dsc.wait()
            else:
                dsc.wait_send()
                dsc.wait_recv()

    outs = _pcall(
        body,
        name=name,
        out_shape=tuple(pltpu.HBM(a.shape, a.dtype) for a in arrays),
        in_specs=[_HBM] * n + [_SEM, _SEM, pl.BlockSpec(memory_space=pl.ANY)],
        out_specs=tuple([_HBM] * n),
        input_output_aliases={i: i for i in range(n)},
        compiler_params=pltpu.CompilerParams(has_side_effects=_EFFECT),
    )(*arrays, send_sems, recv_sems, after)
    return list(outs)


def _mesh_pos():
    x, y, c = lax.axis_index("x"), lax.axis_index("y"), lax.axis_index("c")
    return x, y, c, [(1 - x, y), (x, 1 - y), (1 - x, 1 - y)]


def _gather_ici_copies(n):
    def copies(refs):
        x, y, c, chips = _mesh_pos()
        me = 4 * x + 2 * y + c
        out = []
        for a in range(n):
            src, buf = refs[a], refs[n + a]
            out.append((src, buf.at[me], None))
            out.append((src, buf.at[me], (x, y, 1 - c)))
            out += [(src, buf.at[me], (cx, cy, c)) for cx, cy in chips]
        return out
    return copies


def _gather_d2d_copies(n):
    def copies(refs):
        x, y, c, chips = _mesh_pos()
        out = []
        for a in range(n):
            for cx, cy in chips:
                rows = refs[a].at[4 * cx + 2 * cy + c]
                out.append((rows, rows, (x, y, 1 - c)))
        return out
    return copies


def _reduce_d2d_copies(n):
    def copies(refs):
        x, y, c, _ = _mesh_pos()
        out = []
        for a in range(n):
            for k in range(4):
                out.append((refs[a].at[2 * k + (1 - c)], refs[n + a].at[k], (x, y, 1 - c)))
        return out
    return copies


def _reduce_ici_copies(n):
    def copies(refs):
        x, y, c, chips = _mesh_pos()
        mine = 2 * x + y
        out = []
        for a in range(n):
            src, land = refs[a], refs[n + a]
            out.append((src.at[mine], land.at[mine], None))
            out += [(src.at[2 * cx + cy], land.at[mine], (cx, cy, c)) for cx, cy in chips]
        return out
    return copies


def _pair_sum(send, land, c_idx, name):
    _, r, cols = send.shape
    rb = _tile(r, max(8, (1 << 22) // (send.dtype.itemsize * cols) // 8 * 8), 8)
    dt = send.dtype

    def body(c_ref, s_ref, l_ref, o_ref):
        o_ref[...] = (s_ref[...].astype(F32) + l_ref[...].astype(F32)).astype(dt)

    return pl.pallas_call(
        body,
        name=name,
        out_shape=jax.ShapeDtypeStruct((4, r, cols), dt),
        grid_spec=pltpu.PrefetchScalarGridSpec(
            num_scalar_prefetch=1,
            grid=(4, r // rb),
            in_specs=[pl.BlockSpec((None, rb, cols), lambda k, i, c_ref: (2 * k + c_ref[0], i, 0)),
                      pl.BlockSpec((None, rb, cols), lambda k, i, c_ref: (k, i, 0))],
            out_specs=pl.BlockSpec((None, rb, cols), lambda k, i, c_ref: (k, i, 0)),
        ),
        compiler_params=_cparams(("parallel", "parallel")),
    )(c_idx, send, land)


_DIMS = {
    "nn": (((1,), (0,)), ((), ())),
    "nt": (((1,), (1,)), ((), ())),
    "tn": (((0,), (0,)), ((), ())),
}


def _mm_call(a, b, *, mode, grid, a_spec, b_spec, o_spec, out_shape, acc_shape, name):
    nk = grid[2]
    out_dtype = out_shape.dtype

    def body(a_ref, b_ref, o_ref, *scratch):
        p = lax.dot_general(a_ref[...].astype(BF16), b_ref[...].astype(BF16), _DIMS[mode],
                            preferred_element_type=F32)
        if nk == 1:
            o_ref[...] = p.astype(out_dtype)
        else:
            acc = scratch[0]
            k = pl.program_id(2)

            @pl.when(k == 0)
            def _():
                acc[...] = p

            @pl.when(k > 0)
            def _():
                acc[...] += p

            @pl.when(k == nk - 1)
            def _():
                o_ref[...] = acc[...].astype(out_dtype)

    return _pcall(
        body,
        name=name,
        out_shape=out_shape,
        grid=grid,
        in_specs=[a_spec, b_spec],
        out_specs=o_spec,
        scratch_shapes=[pltpu.VMEM(acc_shape, F32)] if nk > 1 else [],
        compiler_params=_cparams(("parallel", "parallel", "arbitrary")),
    )(a, b)


def _mm(a, b, mode, out_dtype, name, tm=512, tn=512, tk=2432, a_row_off=0, rows=None):
    if mode == "nn":
        (m, k), (k2, n) = a.shape, b.shape
    elif mode == "nt":
        (m, k), (n, k2) = a.shape, b.shape
    else:
        (k, m), (k2, n) = a.shape, b.shape
        if rows is not None:
            k = k2 = rows
    assert k == k2, (a.shape, b.shape, mode)
    if mode != "tn":
        m = (m if rows is None else rows + a_row_off) - a_row_off
    tm, tn, tk = _tile(m, tm, 8), _tile(n, tn), _tile(k, tk, 8 if mode == "tn" else LANE)
    assert a_row_off % tm == 0
    ro = a_row_off // tm
    grid = (m // tm, n // tn, k // tk)
    if mode == "tn":
        a_spec = pl.BlockSpec((tk, tm), lambda i, j, kk: (kk, i))
    else:
        a_spec = pl.BlockSpec((tm, tk), lambda i, j, kk: (i + ro, kk))
    if mode == "nt":
        b_spec = pl.BlockSpec((tn, tk), lambda i, j, kk: (j, kk))
    else:
        b_spec = pl.BlockSpec((tk, tn), lambda i, j, kk: (kk, j))
    o_spec = pl.BlockSpec((tm, tn), lambda i, j, kk: (i, j))
    return _mm_call(a, b, mode=mode, grid=grid, a_spec=a_spec, b_spec=b_spec, o_spec=o_spec,
                    out_shape=jax.ShapeDtypeStruct((m, n), out_dtype), acc_shape=(tm, tn), name=name)


def _mm_cat_nt(pieces, out_dtype, name, tm=1024, tn=1024, tk=2048, rows=None):
    m = pieces[0][0].shape[0] if rows is None else rows
    n = pieces[0][1].shape[0]
    tm, tn = _tile(m, tm, 8), _tile(n, tn)
    steps, starts, s = [], [], 0
    for a, b, off in pieces:
        kp = a.shape[1]
        tkp = _tile(kp, tk)
        assert off % tkp == 0 and b.shape[0] == n
        steps.append((tkp, kp // tkp, off // tkp))
        starts.append(s)
        s += kp // tkp
    nk = s
    npc = len(pieces)

    def body(*refs):
        o_ref, acc = refs[2 * npc], refs[2 * npc + 1]
        kk = pl.program_id(2)

        @pl.when(kk == 0)
        def _():
            acc[...] = jnp.zeros_like(acc)

        for p in range(npc):
            @pl.when((kk >= starts[p]) & (kk < starts[p] + steps[p][1]))
            def _(p=p):
                acc[...] += lax.dot_general(refs[2 * p][...].astype(BF16), refs[2 * p + 1][...].astype(BF16), _DIMS["nt"],
                                            preferred_element_type=F32)

        @pl.when(kk == nk - 1)
        def _():
            o_ref[...] = acc[...].astype(out_dtype)

    in_specs, args = [], []
    for p, (a, b, off) in enumerate(pieces):
        tkp, np_, ob = steps[p]

        def rel(kk, p=p, np_=np_):
            return jnp.clip(kk - starts[p], 0, np_ - 1)

        in_specs.append(pl.BlockSpec((tm, tkp), lambda i, j, kk, rel=rel: (i, rel(kk))))
        in_specs.append(pl.BlockSpec((tn, tkp), lambda i, j, kk, rel=rel, ob=ob: (j, ob + rel(kk))))
        args += [a, b]
    return _pcall(
        body,
        name=name,
        out_shape=jax.ShapeDtypeStruct((m, n), out_dtype),
        grid=(m // tm, n // tn, nk),
        in_specs=in_specs,
        out_specs=pl.BlockSpec((tm, tn), lambda i, j, kk: (i, j)),
        scratch_shapes=[pltpu.VMEM((tm, tn), F32)],
        compiler_params=_cparams(("parallel", "parallel", "arbitrary")),
    )(*args)


def _mm_cat_tn(a, pieces, out_dtype, name, tm=1024, tn=1024, rows=None):
    k = a.shape[0] if rows is None else rows
    m = a.shape[1]
    tm = _tile(m, tm)
    starts, s = [], 0
    for b in pieces:
        assert b.shape[1] % tn == 0
        starts.append(s)
        s += b.shape[1] // tn
    nj = s
    npc = len(pieces)

    def body(*refs):
        a_ref, o_ref = refs[0], refs[1 + npc]
        j = pl.program_id(1)
        for p in range(npc):
            @pl.when((j >= starts[p]) & (j < starts[p] + pieces[p].shape[1] // tn))
            def _(p=p):
                o_ref[...] = lax.dot_general(a_ref[...].astype(BF16), refs[1 + p][...].astype(BF16), _DIMS["tn"],
                                             preferred_element_type=F32).astype(out_dtype)

    in_specs = [pl.BlockSpec((k, tm), lambda i, j: (0, i))]
    for p, b in enumerate(pieces):
        np_ = b.shape[1] // tn
        in_specs.append(pl.BlockSpec((k, tn), lambda i, j, p=p, np_=np_: (0, jnp.clip(j - starts[p], 0, np_ - 1))))
    return _pcall(
        body,
        name=name,
        out_shape=jax.ShapeDtypeStruct((m, nj * tn), out_dtype),
        grid=(m // tm, nj),
        in_specs=in_specs,
        out_specs=pl.BlockSpec((tm, tn), lambda i, j: (i, j)),
        compiler_params=_cparams(("parallel", "arbitrary")),
    )(a, *pieces)


def _mm_up_fwd(z2, w3, name, tm=1024):
    t, d = z2.shape
    nsh, _, c = w3.shape
    tm = _tile(t, tm, 8)
    return _mm_call(z2, w3, mode="nn", grid=(t // tm, nsh, 1),
                    a_spec=pl.BlockSpec((tm, d), lambda i, j, kk: (i, 0)),
                    b_spec=pl.BlockSpec((None, d, c), lambda i, j, kk: (j, 0, 0)),
                    o_spec=pl.BlockSpec((tm, c), lambda i, j, kk: (i, j)),
                    out_shape=jax.ShapeDtypeStruct((t, nsh * c), F32), acc_shape=(tm, c), name=name)


def _mm_up_dz(du3, w3, name, tm=512, tn=1024):
    _, t, f = du3.shape
    nsh, d, c = w3.shape
    half = nsh // 2
    assert f == half * c
    tm, tn = _tile(t, tm, 8), _tile(d, tn)

    def body(a_ref, b_ref, o_ref, acc):
        kk = pl.program_id(2)
        p = None
        for s in range(half):
            q = lax.dot_general(a_ref[:, s * c:(s + 1) * c], b_ref[s], _DIMS["nt"], preferred_element_type=F32)
            p = q if p is None else p + q

        @pl.when(kk == 0)
        def _():
            acc[...] = p

        @pl.when(kk == 1)
        def _():
            o_ref[...] = acc[...] + p

    return _pcall(
        body,
        name=name,
        out_shape=jax.ShapeDtypeStruct((t, d), F32),
        grid=(t // tm, d // tn, 2),
        in_specs=[pl.BlockSpec((None, tm, f), lambda i, j, kk: (kk, i, 0)),
                  pl.BlockSpec((half, tn, c), lambda i, j, kk: (kk, j, 0))],
        out_specs=pl.BlockSpec((tm, tn), lambda i, j, kk: (i, j)),
        scratch_shapes=[pltpu.VMEM((tm, tn), F32)],
        compiler_params=_cparams(("parallel", "parallel", "arbitrary")),
    )(du3, w3)


def _mm_sum_nt(pieces, out_dtype, name, tm=512, tn=512, rows=None):
    m = pieces[0][0].shape[0] if rows is None else rows
    n = pieces[0][2].shape[0]
    tm, tn = _tile(m, tm, 8), _tile(n, tn)
    npc = len(pieces)

    def body(*refs):
        p = None
        for s in range(npc):
            q = lax.dot_general(refs[2 * s][...].astype(BF16), refs[2 * s + 1][...].astype(BF16), _DIMS["nt"],
                                preferred_element_type=F32)
            p = q if p is None else p + q
        refs[2 * npc][...] = p.astype(out_dtype)

    in_specs, args = [], []
    for a, ao, b, bo, kp in pieces:
        assert ao % kp == 0 and bo % kp == 0 and b.shape[0] == n
        in_specs.append(pl.BlockSpec((tm, kp), lambda i, j, ab=ao // kp: (i, ab)))
        in_specs.append(pl.BlockSpec((tn, kp), lambda i, j, bb=bo // kp: (j, bb)))
        args += [a, b]
    return _pcall(
        body,
        name=name,
        out_shape=jax.ShapeDtypeStruct((m, n), out_dtype),
        grid=(m // tm, n // tn),
        in_specs=in_specs,
        out_specs=pl.BlockSpec((tm, tn), lambda i, j: (i, j)),
        compiler_params=_cparams(("parallel", "parallel")),
    )(*args)


def _mm_up_gw(z2, du3, nsh, name, tm=1024):
    t, d = z2.shape
    f = du3.shape[2]
    half = nsh // 2
    c = f // half
    tm = _tile(d, tm)
    return _mm_call(z2, du3, mode="tn", grid=(d // tm, nsh, 1),
                    a_spec=pl.BlockSpec((t, tm), lambda i, j, kk: (0, i)),
                    b_spec=pl.BlockSpec((None, t, c), lambda i, j, kk: (j // half, 0, j % half)),
                    o_spec=pl.BlockSpec((None, tm, c), lambda i, j, kk: (j, i, 0)),
                    out_shape=jax.ShapeDtypeStruct((nsh, d, c), BF16), acc_shape=(tm, c), name=name)


def _rms(x):
    r = lax.rsqrt(jnp.mean(x * x, axis=-1, keepdims=True) + NORM_EPS)
    return x * r, r


def _rms_bwd(dxh, xh, r):
    return r * (dxh - xh * jnp.mean(dxh * xh, axis=-1, keepdims=True))


def _colsum(v):
    return jnp.sum(v, axis=0, keepdims=True)


def _rope(v, c, s1, s2, q):
    w = v.shape[-1]
    return v * c + pltpu.roll(v, w - q, 1) * s1 + pltpu.roll(v, q, 1) * s2


def _rope_t(d, c, s1, s2, q):
    w = d.shape[-1]
    return d * c + pltpu.roll(d * s1, q, 1) + pltpu.roll(d * s2, w - q, 1)


def _norm_mod_fwd(ctx, x, gain, mods):
    tc, d = ctx.shape
    t = x.shape[0]
    rb = min(ROW_BLOCK, tc)
    nbl = t // rb

    def body(ctx_ref, x_ref, g_ref, mod_ref, z_ref):
        i = pl.program_id(0)

        def emit(src, sh, sc):
            xh, _ = _rms(src[...])
            z_ref[...] = ((xh * g_ref[...]) * (1.0 + sc) + sh).astype(BF16)

        @pl.when(i >= nbl)
        def _():
            emit(ctx_ref, mod_ref[2:3, :], mod_ref[3:4, :])

        @pl.when(i < nbl)
        def _():
            emit(x_ref, mod_ref[0:1, :], mod_ref[1:2, :])

    return _pcall(
        body,
        name="norm1_mod_fwd",
        out_shape=jax.ShapeDtypeStruct((tc + t, d), BF16),
        grid=((tc + t) // rb,),
        in_specs=[
            pl.BlockSpec((rb, d), lambda i: (jnp.maximum(i - nbl, 0), 0)),
            pl.BlockSpec((rb, d), lambda i: (jnp.minimum(i, nbl - 1), 0)),
            pl.BlockSpec((1, d), lambda i: (0, 0)),
            pl.BlockSpec((8, d), lambda i: (0, 0)),
        ],
        out_specs=pl.BlockSpec((rb, d), lambda i: (i, 0)),
        compiler_params=_cparams(("arbitrary",)),
    )(ctx, x, gain, mods)


def _norm1_bwd(ctx, x, gain, mods, dz_ctx, dz_lat, dx1):
    tc, d = ctx.shape
    t = x.shape[0]
    rb = min(ROW_BLOCK, tc)
    nbl = t // rb

    def body(ctx_ref, x_ref, g_ref, mod_ref, dzc_ref, dzl_ref, dx1_ref, gx_ref, st_ref):
        i = pl.program_id(0)

        @pl.when(i == 0)
        def _():
            st_ref[...] = jnp.zeros_like(st_ref)

        def common(src, dz, sc, row_sh, row_sc):
            xh, r = _rms(src[...])
            g = g_ref[...]
            dxn = dz * (1.0 + sc)
            st_ref[row_sh:row_sh + 1, :] += _colsum(dz)
            st_ref[row_sc:row_sc + 1, :] += _colsum(dz * (xh * g))
            st_ref[2:3, :] += _colsum(dxn * xh)
            return _rms_bwd(dxn * g, xh, r)

        @pl.when(i >= nbl)
        def _():
            common(ctx_ref, dzc_ref[...], mod_ref[3:4, :], 3, 4)

        @pl.when(i < nbl)
        def _():
            gx_ref[...] = dx1_ref[...] + common(x_ref, dzl_ref[...], mod_ref[1:2, :], 0, 1)

    lat = lambda i: (jnp.minimum(i, nbl - 1), 0)
    cix = lambda i: (jnp.maximum(i - nbl, 0), 0)
    return _pcall(
        body,
        name="norm1_mod_bwd",
        out_shape=[jax.ShapeDtypeStruct((t, d), F32), jax.ShapeDtypeStruct((8, d), F32)],
        grid=((tc + t) // rb,),
        in_specs=[
            pl.BlockSpec((rb, d), cix),
            pl.BlockSpec((rb, d), lat),
            pl.BlockSpec((1, d), lambda i: (0, 0)),
            pl.BlockSpec((8, d), lambda i: (0, 0)),
            pl.BlockSpec((rb, d), cix),
            pl.BlockSpec((rb, d), lat),
            pl.BlockSpec((rb, d), lat),
        ],
        out_specs=[pl.BlockSpec((rb, d), lat), pl.BlockSpec((8, d), lambda i: (0, 0))],
        compiler_params=_cparams(("arbitrary",)),
    )(ctx, x, gain, mods, dz_ctx, dz_lat, dx1)


def _key_prep_fwd(kv, kv_gain, kb_gain, tabs):
    ta, wkv = kv.shape
    kvl = MLA_KV_LORA
    nb = GQA_KV_HEADS * GQA_HEAD_DIM
    rb = ROW_BLOCK if ta % ROW_BLOCK == 0 else LANE
    hd = GQA_HEAD_DIM

    def body(kv_ref, g_ref, gb_ref, ca, s1a, s2a, cb, s1b, s2b, kin_ref, kb_ref, vb_ref):
        xh, _ = _rms(kv_ref[:, 0:kvl])
        kin_ref[:, 0:kvl] = (xh * g_ref[...]).astype(BF16)
        kpe = kv_ref[:, kvl + 2 * nb:kvl + 2 * nb + LANE]
        kin_ref[:, kvl:kvl + LANE] = _rope(kpe, ca[...], s1a[...], s2a[...], MLA_ROPE // 4).astype(BF16)
        for h in range(GQA_KV_HEADS):
            nh, _ = _rms(kv_ref[:, kvl + h * hd:kvl + (h + 1) * hd])
            kb_ref[:, h * hd:(h + 1) * hd] = _rope(nh * gb_ref[...], cb[...], s1b[...], s2b[...], hd // 4).astype(BF16)
        vb_ref[...] = kv_ref[:, kvl + nb:kvl + 2 * nb].astype(BF16)

    row = lambda w: pl.BlockSpec((rb, w), lambda i: (i, 0))
    fix = lambda w: pl.BlockSpec((1, w), lambda i: (0, 0))
    return _pcall(
        body,
        name="key_prep_fwd",
        out_shape=[jax.ShapeDtypeStruct((ta, kvl + LANE), BF16), jax.ShapeDtypeStruct((ta, nb), BF16),
                   jax.ShapeDtypeStruct((ta, nb), BF16)],
        grid=(ta // rb,),
        in_specs=[row(wkv), fix(kvl), fix(hd)] + [row(LANE)] * 3 + [row(hd)] * 3,
        out_specs=[row(kvl + LANE), row(nb), row(nb)],
        compiler_params=_cparams(("parallel",)),
    )(kv, kv_gain, kb_gain, *tabs)


def _key_prep_bwd(kv, kv_gain, kb_gain, tabs, dkin, dkb, dvb):
    ta, wkv = kv.shape
    kvl = MLA_KV_LORA
    nb = GQA_KV_HEADS * GQA_HEAD_DIM
    rb = ROW_BLOCK if ta % ROW_BLOCK == 0 else LANE
    hd = GQA_HEAD_DIM

    def body(kv_ref, g_ref, gb_ref, ca, s1a, s2a, cb, s1b, s2b, dkin_ref, dkb_ref, dvb_ref, dkv_ref, st_ref, stb_ref):
        @pl.when(pl.program_id(0) == 0)
        def _():
            st_ref[...] = jnp.zeros_like(st_ref)
            stb_ref[...] = jnp.zeros_like(stb_ref)

        xh, r = _rms(kv_ref[:, 0:kvl])
        dn = dkin_ref[:, 0:kvl]
        st_ref[0:1, :] += _colsum(dn * xh)
        dkv_ref[:, 0:kvl] = _rms_bwd(dn * g_ref[...], xh, r).astype(BF16)
        dpe = _rope_t(dkin_ref[:, kvl:kvl + LANE], ca[...], s1a[...], s2a[...], MLA_ROPE // 4)
        dkv_ref[:, kvl + 2 * nb:kvl + 2 * nb + LANE] = dpe.astype(BF16)
        for h in range(GQA_KV_HEADS):
            nh, rh = _rms(kv_ref[:, kvl + h * hd:kvl + (h + 1) * hd])
            dn_h = _rope_t(dkb_ref[:, h * hd:(h + 1) * hd], cb[...], s1b[...], s2b[...], hd // 4)
            stb_ref[0:1, :] += _colsum(dn_h * nh)
            dkv_ref[:, kvl + h * hd:kvl + (h + 1) * hd] = _rms_bwd(dn_h * gb_ref[...], nh, rh).astype(BF16)
        dkv_ref[:, kvl + nb:kvl + 2 * nb] = dvb_ref[...].astype(BF16)

    row = lambda w: pl.BlockSpec((rb, w), lambda i: (i, 0))
    fix = lambda w: pl.BlockSpec((1, w), lambda i: (0, 0))
    return _pcall(
        body,
        name="key_prep_bwd",
        out_shape=[jax.ShapeDtypeStruct((ta, wkv), BF16), jax.ShapeDtypeStruct((8, kvl), F32),
                   jax.ShapeDtypeStruct((8, hd), F32)],
        grid=(ta // rb,),
        in_specs=[row(wkv), fix(kvl), fix(hd)] + [row(LANE)] * 3 + [row(hd)] * 3 + [row(kvl + LANE), row(nb), row(nb)],
        out_specs=[row(wkv), pl.BlockSpec((8, kvl), lambda i: (0, 0)), pl.BlockSpec((8, hd), lambda i: (0, 0))],
        compiler_params=_cparams(("arbitrary",)),
    )(kv, kv_gain, kb_gain, *tabs, dkin, dkb, dvb)


def _q_prep_fwd(qg, q_gain, qb_gain, tabs, qscale):
    t = qg.shape[0]
    ql = MLA_Q_LORA
    hd = GQA_HEAD_DIM
    hb = GQA_HEADS * hd
    rb = min(ROW_BLOCK, t)

    def body(q_ref, g_ref, gb_ref, cb, s1b, s2b, cqn_ref, qb_ref):
        xh, _ = _rms(q_ref[:, 0:ql])
        cqn_ref[...] = (xh * g_ref[...]).astype(BF16)
        for h in range(GQA_HEADS):
            nh, _ = _rms(q_ref[:, ql + h * hd:ql + (h + 1) * hd])
            qh = _rope(nh * gb_ref[...], cb[...], s1b[...], s2b[...], hd // 4)
            qb_ref[:, h * hd:(h + 1) * hd] = (qh * qscale).astype(BF16)

    row = lambda w: pl.BlockSpec((rb, w), lambda i: (i, 0))
    fix = lambda w: pl.BlockSpec((1, w), lambda i: (0, 0))
    return _pcall(
        body,
        name="q_prep_fwd",
        out_shape=[jax.ShapeDtypeStruct((t, ql), BF16), jax.ShapeDtypeStruct((t, hb), BF16)],
        grid=(t // rb,),
        in_specs=[row(ql + hb), fix(ql), fix(hd)] + [row(hd)] * 3,
        out_specs=[row(ql), row(hb)],
        compiler_params=_cparams(("parallel",)),
    )(qg, q_gain, qb_gain, *tabs)


def _q_prep_bwd(qg, q_gain, qb_gain, tabs, dcqn, dqb, wpad, qscale):
    t = qg.shape[0]
    ql = MLA_Q_LORA
    hd = GQA_HEAD_DIM
    hb = GQA_HEADS * hd
    rb = min(ROW_BLOCK, t)

    def body(q_ref, g_ref, gb_ref, cb, s1b, s2b, dcqn_ref, dqb_ref, dq_ref, st_ref, stb_ref):
        @pl.when(pl.program_id(0) == 0)
        def _():
            st_ref[...] = jnp.zeros_like(st_ref)
            stb_ref[...] = jnp.zeros_like(stb_ref)

        xh, r = _rms(q_ref[:, 0:ql])
        dn = dcqn_ref[...]
        st_ref[0:1, :] += _colsum(dn * xh)
        dq_ref[:, 0:ql] = _rms_bwd(dn * g_ref[...], xh, r).astype(BF16)
        for h in range(GQA_HEADS):
            nh, rh = _rms(q_ref[:, ql + h * hd:ql + (h + 1) * hd])
            dn_h = _rope_t(dqb_ref[:, h * hd:(h + 1) * hd] * qscale, cb[...], s1b[...], s2b[...], hd // 4)
            stb_ref[0:1, :] += _colsum(dn_h * nh)
            dq_ref[:, ql + h * hd:ql + (h + 1) * hd] = _rms_bwd(dn_h * gb_ref[...], nh, rh).astype(BF16)
        if wpad:
            dq_ref[:, ql + hb:ql + hb + wpad] = jnp.zeros((rb, wpad), BF16)

    row = lambda w: pl.BlockSpec((rb, w), lambda i: (i, 0))
    fix = lambda w: pl.BlockSpec((1, w), lambda i: (0, 0))
    return _pcall(
        body,
        name="q_prep_bwd",
        out_shape=[jax.ShapeDtypeStruct((t, ql + hb + wpad), BF16), jax.ShapeDtypeStruct((8, ql), F32),
                   jax.ShapeDtypeStruct((8, hd), F32)],
        grid=(t // rb,),
        in_specs=[row(ql + hb), fix(ql), fix(hd)] + [row(hd)] * 3 + [row(ql), row(hb)],
        out_specs=[row(ql + hb + wpad), pl.BlockSpec((8, ql), lambda i: (0, 0)), pl.BlockSpec((8, hd), lambda i: (0, 0))],
        compiler_params=_cparams(("arbitrary",)),
    )(qg, q_gain, qb_gain, *tabs, dcqn, dqb)


def _rope_a(v, tabs, transpose, out_dtype, name, qscale):
    t, w = v.shape
    rb = min(ROW_BLOCK, t)
    fn = _rope_t if transpose else _rope

    def body(v_ref, c, s1, s2, o_ref):
        for h in range(w // MLA_SLOT):
            sl = slice(h * MLA_SLOT, (h + 1) * MLA_SLOT)
            o_ref[:, sl] = (fn(v_ref[:, sl].astype(F32), c[...], s1[...], s2[...], MLA_ROPE // 4) * qscale).astype(out_dtype)

    row = lambda ww: pl.BlockSpec((rb, ww), lambda i: (i, 0))
    return _pcall(
        body,
        name=name,
        out_shape=jax.ShapeDtypeStruct((t, w), out_dtype),
        grid=(t // rb,),
        in_specs=[row(w)] + [row(MLA_SLOT)] * 3,
        out_specs=row(w),
        compiler_params=_cparams(("parallel",)),
    )(v, *tabs)


def _merge_fwd(pa, pb, qg, gate_blk):
    t, d = pa.shape
    rb = min(ROW_BLOCK, t)

    def body(pa_ref, pb_ref, ga_ref, gb_ref, o_ref):
        o_ref[...] = (jax.nn.sigmoid(ga_ref[...]) * pa_ref[...] + jax.nn.sigmoid(gb_ref[...]) * pb_ref[...]).astype(BF16)

    row = pl.BlockSpec((rb, d), lambda i: (i, 0))
    return _pcall(
        body,
        name="merge_fwd",
        out_shape=jax.ShapeDtypeStruct((t, d), BF16),
        grid=(t // rb,),
        in_specs=[row, row, pl.BlockSpec((rb, d), lambda i: (i, gate_blk)), pl.BlockSpec((rb, d), lambda i: (i, gate_blk + 1))],
        out_specs=row,
        compiler_params=_cparams(("parallel",)),
    )(pa, pb, qg, qg)


def _merge_bwd(dm, pa, pb, qg, gate_blk):
    t, d = pa.shape
    rb = min(ROW_BLOCK, t)

    def body(dm_ref, pa_ref, pb_ref, ga_ref, gb_ref, dpa_ref, dpb_ref, dg_ref):
        dmv = dm_ref[...]
        sa = jax.nn.sigmoid(ga_ref[...])
        sb = jax.nn.sigmoid(gb_ref[...])
        dpa_ref[...] = (dmv * sa).astype(BF16)
        dpb_ref[...] = (dmv * sb).astype(BF16)
        dg_ref[:, 0:d] = (dmv * pa_ref[...] * (sa * (1.0 - sa))).astype(BF16)
        dg_ref[:, d:2 * d] = (dmv * pb_ref[...] * (sb * (1.0 - sb))).astype(BF16)

    row = pl.BlockSpec((rb, d), lambda i: (i, 0))
    return _pcall(
        body,
        name="merge_bwd",
        out_shape=[jax.ShapeDtypeStruct((t, d), BF16), jax.ShapeDtypeStruct((t, d), BF16),
                   jax.ShapeDtypeStruct((t, 2 * d), BF16)],
        grid=(t // rb,),
        in_specs=[row, row, row, pl.BlockSpec((rb, d), lambda i: (i, gate_blk)), pl.BlockSpec((rb, d), lambda i: (i, gate_blk + 1))],
        out_specs=[row, row, pl.BlockSpec((rb, 2 * d), lambda i: (i, 0))],
        compiler_params=_cparams(("parallel",)),
    )(dm, pa, pb, qg, qg)


def _resid_norm_mod(x, branch, gain, mods, name):
    t, d = x.shape
    rb = min(ROW_BLOCK, t)

    def body(x_ref, b_ref, g_ref, mod_ref, x1_ref, z_ref):
        x1 = x_ref[...] + mod_ref[0:1, :] * b_ref[...]
        x1_ref[...] = x1
        xh, _ = _rms(x1)
        z_ref[...] = ((xh * g_ref[...]) * (1.0 + mod_ref[2:3, :]) + mod_ref[1:2, :]).astype(BF16)

    row = pl.BlockSpec((rb, d), lambda i: (i, 0))
    return _pcall(
        body,
        name=name,
        out_shape=[jax.ShapeDtypeStruct((t, d), F32), jax.ShapeDtypeStruct((t, d), BF16)],
        grid=(t // rb,),
        in_specs=[row, row, pl.BlockSpec((1, d), lambda i: (0, 0)), pl.BlockSpec((8, d), lambda i: (0, 0))],
        out_specs=[row, row],
        compiler_params=_cparams(("parallel",)),
    )(x, branch, gain, mods)


def _norm2_bwd(x1, attn, gain, mods, dz2, dx2):
    t, d = x1.shape
    rb = min(ROW_BLOCK, t)

    def body(x1_ref, at_ref, g_ref, mod_ref, dz_ref, dx2_ref, dx1_ref, da_ref, st_ref):
        @pl.when(pl.program_id(0) == 0)
        def _():
            st_ref[...] = jnp.zeros_like(st_ref)

        xh, r = _rms(x1_ref[...])
        g = g_ref[...]
        dz = dz_ref[...]
        dxn = dz * (1.0 + mod_ref[1:2, :])
        st_ref[0:1, :] += _colsum(dz)
        st_ref[1:2, :] += _colsum(dz * (xh * g))
        st_ref[2:3, :] += _colsum(dxn * xh)
        dx1 = dx2_ref[...] + _rms_bwd(dxn * g, xh, r)
        dx1_ref[...] = dx1
        st_ref[3:4, :] += _colsum(dx1 * at_ref[...])
        da_ref[...] = (dx1 * mod_ref[0:1, :]).astype(BF16)

    row = pl.BlockSpec((rb, d), lambda i: (i, 0))
    return _pcall(
        body,
        name="norm2_mod_bwd",
        out_shape=[jax.ShapeDtypeStruct((t, d), F32), jax.ShapeDtypeStruct((t, d), BF16), jax.ShapeDtypeStruct((8, d), F32)],
        grid=(t // rb,),
        in_specs=[row, row, pl.BlockSpec((1, d), lambda i: (0, 0)), pl.BlockSpec((8, d), lambda i: (0, 0)), row, row],
        out_specs=[row, row, pl.BlockSpec((8, d), lambda i: (0, 0))],
        compiler_params=_cparams(("arbitrary",)),
    )(x1, attn, gain, mods, dz2, dx2)


def _final_loss(x1, ffn, gain, mods, target):
    t, d = x1.shape
    rb = min(ROW_BLOCK, t)
    nb = t // rb

    def body(x1_ref, f_ref, g_ref, mod_ref, tg_ref, dx2_ref, df_ref, st_ref):
        i = pl.program_id(0)

        @pl.when(i == 0)
        def _():
            st_ref[...] = jnp.zeros_like(st_ref)

        ffn_v = f_ref[...]
        g2 = mod_ref[0:1, :]
        x2 = x1_ref[...] + g2 * ffn_v
        xh, r = _rms(x2)
        g = g_ref[...]
        err = xh * g - tg_ref[...]
        st_ref[2:3, :] += _colsum(err * err) * (0.5 / d)
        dy = err * (1.0 / d)
        st_ref[0:1, :] += _colsum(dy * xh)
        dx2 = _rms_bwd(dy * g, xh, r)
        dx2_ref[...] = dx2
        st_ref[1:2, :] += _colsum(dx2 * ffn_v)
        df_ref[...] = (dx2 * g2).astype(BF16)

        @pl.when(i == nb - 1)
        def _():
            st_ref[3:4, :] = jnp.broadcast_to(jnp.sum(st_ref[2:3, :], axis=-1, keepdims=True), (1, d))

    row = pl.BlockSpec((rb, d), lambda i: (i, 0))
    return _pcall(
        body,
        name="final_norm_loss",
        out_shape=[jax.ShapeDtypeStruct((t, d), F32), jax.ShapeDtypeStruct((t, d), BF16), jax.ShapeDtypeStruct((8, d), F32)],
        grid=(nb,),
        in_specs=[row, row, pl.BlockSpec((1, d), lambda i: (0, 0)), pl.BlockSpec((8, d), lambda i: (0, 0)), row],
        out_specs=[row, row, pl.BlockSpec((8, d), lambda i: (0, 0))],
        compiler_params=_cparams(("arbitrary",)),
    )(x1, ffn, gain, mods, target)


def _row_ends(shape):
    rows = lax.broadcasted_iota(jnp.int32, shape, 0)
    return rows == 0, rows == shape[0] - 1


def _shift_dn(v, first):
    return jnp.where(first, 0.0, pltpu.roll(v, 1, 0))


def _shift_up(v, last):
    return jnp.where(last, 0.0, pltpu.roll(v, v.shape[0] - 1, 0))


def _conv_fwd(u, cw, cb):
    t, f2 = u.shape
    f = f2 // 2
    cbk = _tile(f, 256)
    nf = f // cbk

    def body(ua_ref, ub_ref, cwa_ref, cwb_ref, cba_ref, cbb_ref, h_ref, uc_ref):
        first, last = _row_ends((t, cbk))
        outs = []
        for u_ref, cw_ref, cb_ref in ((ua_ref, cwa_ref, cba_ref), (ub_ref, cwb_ref, cbb_ref)):
            uu, cwv = u_ref[...], cw_ref[...]
            outs.append(cb_ref[...] + cwv[0:1, :] * _shift_dn(uu, first) + cwv[1:2, :] * uu
                        + cwv[2:3, :] * _shift_up(uu, last))
        a, b = outs
        uc_ref[0] = a
        uc_ref[1] = b
        h_ref[...] = (a * jax.nn.sigmoid(a) * b).astype(BF16)

    ca = lambda r: pl.BlockSpec((r, cbk), lambda j: (0, j))
    cbs = lambda r: pl.BlockSpec((r, cbk), lambda j: (0, nf + j))
    return _pcall(
        body,
        name="conv_gate_fwd",
        out_shape=[jax.ShapeDtypeStruct((t, f), BF16), jax.ShapeDtypeStruct((2, t, f), F32)],
        grid=(nf,),
        in_specs=[ca(t), cbs(t), ca(3), cbs(3), ca(1), cbs(1)],
        out_specs=[ca(t), pl.BlockSpec((2, t, cbk), lambda j: (0, 0, j))],
        compiler_params=_cparams(("parallel",)),
    )(u, u, cw, cw, cb, cb)


def _conv_bwd(u, uc, cw, dh):
    t, f2 = u.shape
    f = f2 // 2
    cbk = _tile(f, 256)
    nf = f // cbk

    def body(ua_ref, ub_ref, uc_ref, cwa_ref, cwb_ref, dh_ref, du_ref, dcw_ref, dcb_ref):
        first, last = _row_ends((t, cbk))
        a, b = uc_ref[0], uc_ref[1]
        dh_v = dh_ref[...]
        sg = jax.nn.sigmoid(a)
        db = dh_v * (a * sg)
        da = dh_v * b * (sg * (1.0 + a * (1.0 - sg)))
        for idx, (dv, u_ref, cw_ref) in enumerate(((da, ua_ref, cwa_ref), (db, ub_ref, cwb_ref))):
            uu, cwv = u_ref[...], cw_ref[...]
            up, dn = _shift_up(dv, last), _shift_dn(dv, first)
            dcb_ref[idx] = _colsum(dv)
            dcw_ref[idx, 0:1, :] = _colsum(up * uu)
            dcw_ref[idx, 1:2, :] = _colsum(dv * uu)
            dcw_ref[idx, 2:3, :] = _colsum(dn * uu)
            du_ref[idx] = (cwv[0:1, :] * up + cwv[1:2, :] * dv + cwv[2:3, :] * dn).astype(BF16)

    ca = lambda r: pl.BlockSpec((r, cbk), lambda j: (0, j))
    cbs = lambda r: pl.BlockSpec((r, cbk), lambda j: (0, nf + j))
    o3 = lambda r: pl.BlockSpec((2, r, cbk), lambda j: (0, 0, j))
    return _pcall(
        body,
        name="conv_gate_bwd",
        out_shape=[jax.ShapeDtypeStruct((2, t, f), BF16), jax.ShapeDtypeStruct((2, 3, f), F32),
                   jax.ShapeDtypeStruct((2, 1, f), F32)],
        grid=(nf,),
        in_specs=[ca(t), cbs(t), o3(t), ca(3), cbs(3), ca(t)],
        out_specs=[o3(t), o3(3), o3(1)],
        compiler_params=_cparams(("parallel",)),
    )(u, u, uc, cw, cw, dh)


def _attention_fwd(q, kk, vv, *, hq, hkv, dk, dv, k_blk0, v_blk0, name):
    t = q.shape[0]
    tk = kk.shape[0]
    g_sz = hq // hkv
    tq = min(ATT_Q_BLOCK_FWD, t)

    def body(q_ref, k_ref, v_ref, o_ref, lse_ref):
        k = k_ref[...]
        v = v_ref[...]
        for j in range(g_sz):
            s = lax.dot_general(q_ref[:, j * dk:(j + 1) * dk], k, _DIMS["nt"], preferred_element_type=F32)
            m = jnp.max(s, axis=-1, keepdims=True)
            p = jnp.exp2(s - m)
            l = jnp.sum(p, axis=-1, keepdims=True)
            o = jnp.dot(p.astype(BF16), v, preferred_element_type=F32) / l
            o_ref[:, j * dv:(j + 1) * dv] = o.astype(BF16)
            lse_ref[0, :, j:j + 1] = m + jnp.log2(l)

    return _pcall(
        body,
        name=name,
        out_shape=[jax.ShapeDtypeStruct((t, hq * dv), BF16), jax.ShapeDtypeStruct((hkv, t, g_sz), F32)],
        grid=(hkv, t // tq),
        in_specs=[
            pl.BlockSpec((tq, g_sz * dk), lambda g, i: (i, g)),
            pl.BlockSpec((tk, dk), lambda g, i: (0, k_blk0 + g)),
            pl.BlockSpec((tk, dv), lambda g, i: (0, v_blk0 + g)),
        ],
        out_specs=[
            pl.BlockSpec((tq, g_sz * dv), lambda g, i: (i, g)),
            pl.BlockSpec((1, tq, g_sz), lambda g, i: (g, i, 0)),
        ],
        compiler_params=_cparams(("parallel", "parallel")),
    )(q, kk, vv)


def _attention_bwd(q, kk, vv, do, lse, *, hq, hkv, dk, dv, k_blk0, v_blk0, name):
    t = q.shape[0]
    tk = kk.shape[0]
    g_sz = hq // hkv
    tq = min(ATT_Q_BLOCK, t)

    def body(q_ref, k_ref, v_ref, do_ref, lse_ref, dq_ref, dk_ref, dv_ref):
        @pl.when(pl.program_id(1) == 0)
        def _():
            dk_ref[...] = jnp.zeros_like(dk_ref)
            dv_ref[...] = jnp.zeros_like(dv_ref)

        k = k_ref[...]
        v = v_ref[...]
        for j in range(g_sz):
            qj = q_ref[:, j * dk:(j + 1) * dk]
            doj = do_ref[:, j * dv:(j + 1) * dv]
            s = lax.dot_general(qj, k, _DIMS["nt"], preferred_element_type=F32)
            p = jnp.exp2(s - lse_ref[0, :, j:j + 1])
            dp = lax.dot_general(doj, v, _DIMS["nt"], preferred_element_type=F32)
            ds = (p * (dp - jnp.sum(p * dp, axis=-1, keepdims=True))).astype(BF16)
            dv_ref[...] += lax.dot_general(p.astype(BF16), doj, _DIMS["tn"], preferred_element_type=F32)
            dk_ref[...] += lax.dot_general(ds, qj, _DIMS["tn"], preferred_element_type=F32)
            dq_ref[:, j * dk:(j + 1) * dk] = jnp.dot(ds, k, preferred_element_type=F32)

        @pl.when(pl.program_id(1) == t // tq - 1)
        def _():
            dk_ref[...] *= LN2

    return _pcall(
        body,
        name=name,
        out_shape=[jax.ShapeDtypeStruct((t, hq * dk), F32), jax.ShapeDtypeStruct((tk, hkv * dk), F32),
                   jax.ShapeDtypeStruct((tk, hkv * dv), F32)],
        grid=(hkv, t // tq),
        in_specs=[
            pl.BlockSpec((tq, g_sz * dk), lambda g, i: (i, g)),
            pl.BlockSpec((tk, dk), lambda g, i: (0, k_blk0 + g)),
            pl.BlockSpec((tk, dv), lambda g, i: (0, v_blk0 + g)),
            pl.BlockSpec((tq, g_sz * dv), lambda g, i: (i, g)),
            pl.BlockSpec((1, tq, g_sz), lambda g, i: (g, i, 0)),
        ],
        out_specs=[
            pl.BlockSpec((tq, g_sz * dk), lambda g, i: (i, g)),
            pl.BlockSpec((tk, dk), lambda g, i: (0, g)),
            pl.BlockSpec((tk, dv), lambda g, i: (0, g)),
        ],
        compiler_params=_cparams(("parallel", "arbitrary")),
    )(q, kk, vv, do, lse)


def _silu(v):
    return v * jax.nn.sigmoid(v)


def _ada_fwd(conds, w_ada, b_ada_shard):
    r, d = conds.shape
    n = w_ada.shape[1]
    tn = _tile(n, 512)

    def body(c_ref, w_ref, b_ref, o_ref):
        s = _silu(c_ref[...]).astype(BF16)
        o_ref[...] = jnp.dot(s, w_ref[...].astype(BF16), preferred_element_type=F32) + b_ref[...]

    return _pcall(
        body,
        name="ada_fwd",
        out_shape=jax.ShapeDtypeStruct((r, n), F32),
        grid=(n // tn,),
        in_specs=[pl.BlockSpec((r, d), lambda j: (0, 0)), pl.BlockSpec((d, tn), lambda j: (0, j)),
                  pl.BlockSpec((1, tn), lambda j: (0, j))],
        out_specs=pl.BlockSpec((r, tn), lambda j: (0, j)),
        compiler_params=_cparams(("parallel",)),
    )(conds, w_ada, b_ada_shard)


def _cctx_partial(da16_shard, w_ada, c_ctx_row):
    d, n = w_ada.shape
    td = _tile(d, 512)

    def body(g_ref, w_ref, c_ref, o_ref):
        ds = lax.dot_general(g_ref[8:16, :].astype(BF16), w_ref[...].astype(BF16), _DIMS["nt"],
                             preferred_element_type=F32)
        cv = c_ref[...]
        sg = jax.nn.sigmoid(cv)
        o_ref[...] = ds * (sg * (1.0 + cv * (1.0 - sg)))

    return _pcall(
        body,
        name="cctx_partial",
        out_shape=jax.ShapeDtypeStruct((8, d), F32),
        grid=(d // td,),
        in_specs=[pl.BlockSpec((16, n), lambda j: (0, 0)), pl.BlockSpec((td, n), lambda j: (j, 0)),
                  pl.BlockSpec((1, td), lambda j: (0, j))],
        out_specs=pl.BlockSpec((8, td), lambda j: (0, j)),
        compiler_params=_cparams(("parallel",)),
    )(da16_shard, w_ada, c_ctx_row)


def _sum_parts(parts):
    p, _, n = parts.shape

    def body(p_ref, o_ref):
        acc = p_ref[0]
        for s in range(1, p):
            acc = acc + p_ref[s]
        o_ref[...] = acc

    return _pcall(
        body,
        name="sum_parts",
        out_shape=jax.ShapeDtypeStruct((1, n), F32),
        in_specs=[pl.BlockSpec(memory_space=pltpu.VMEM)],
        out_specs=pl.BlockSpec(memory_space=pltpu.VMEM),
    )(parts)


def _adam_math(w, g, m, v):
    m2 = ADAM_B1 * m + (1.0 - ADAM_B1) * g
    v2 = ADAM_B2 * v + (1.0 - ADAM_B2) * jnp.square(g)
    m_hat = m2 / (1.0 - ADAM_B1 ** ADAM_STEP)
    v_hat = v2 / (1.0 - ADAM_B2 ** ADAM_STEP)
    delta = -ADAM_LR * (m_hat / (jnp.sqrt(v_hat) + ADAM_EPS) + ADAM_WD * w)
    return delta, m2, v2


def _adamw(parts, w, m, v, name):
    p, r, c = parts.shape
    rb = _tile(r, max(8, (1 << 20) // (4 * c) // 8 * 8), 8)

    def body(p_ref, w_ref, m_ref, v_ref, g_ref, d_ref, m2_ref, v2_ref):
        g = p_ref[0].astype(F32)
        for s in range(1, p):
            g = g + p_ref[s].astype(F32)
        g_ref[...] = g
        d_ref[...], m2_ref[...], v2_ref[...] = _adam_math(w_ref[...], g, m_ref[...], v_ref[...])

    if w.ndim == 3:
        row = pl.BlockSpec((None, rb, c), lambda i: (0, i, 0))
    else:
        row = pl.BlockSpec((rb, c), lambda i: (i, 0))
    return _pcall(
        body,
        name=name,
        out_shape=[jax.ShapeDtypeStruct(w.shape, F32)] * 4,
        grid=(r // rb,),
        in_specs=[pl.BlockSpec((p, rb, c), lambda i: (0, i, 0)), row, row, row],
        out_specs=[row] * 4,
        compiler_params=_cparams(("parallel",)),
    )(parts, w, m, v)


def _adamw_ada(conds, da16, w, m, v):
    d, n = w.shape
    rb = _tile(d, 256, LANE)

    def body(s_ref, da_ref, w_ref, m_ref, v_ref, g_ref, d_ref, m2_ref, v2_ref):
        g = lax.dot_general(_silu(s_ref[...]).astype(BF16), da_ref[...].astype(BF16), _DIMS["tn"],
                            preferred_element_type=F32)
        g_ref[...] = g
        d_ref[...], m2_ref[...], v2_ref[...] = _adam_math(w_ref[...], g, m_ref[...], v_ref[...])

    row = pl.BlockSpec((rb, n), lambda i: (i, 0))
    return _pcall(
        body,
        name="adamw_w_ada",
        out_shape=[jax.ShapeDtypeStruct((d, n), F32)] * 4,
        grid=(d // rb,),
        in_specs=[pl.BlockSpec((16, rb), lambda i: (0, i)), pl.BlockSpec((16, n), lambda i: (0, 0)), row, row, row],
        out_specs=[row] * 4,
        compiler_params=_cparams(("parallel",)),
    )(conds, da16, w, m, v)


def _cast_bf16(a, name):
    _, r, c = a.shape
    rb = _tile(r, 512, 8)

    def body(a_ref, o_ref):
        o_ref[...] = a_ref[...].astype(BF16)

    return _pcall(body, name=name, out_shape=jax.ShapeDtypeStruct((r, c), BF16), grid=(r // rb,),
                  in_specs=[pl.BlockSpec((None, rb, c), lambda i: (0, i, 0))],
                  out_specs=pl.BlockSpec((rb, c), lambda i: (i, 0)), compiler_params=_cparams(("parallel",)))(a)


def _rope_tabs(t, rot):
    half, q = rot // 2, rot // 4
    n_rows = t // GRID_W
    row = jnp.repeat(jnp.arange(n_rows, dtype=F32), GRID_W)
    col = jnp.tile(jnp.arange(GRID_W, dtype=F32), n_rows)
    inv_freq = ROPE_THETA ** (-jnp.arange(0, half, 2, dtype=F32) / half)
    ang = jnp.concatenate([row[:, None] * inv_freq, col[:, None] * inv_freq], axis=-1)
    cos, sin = jnp.cos(ang), jnp.sin(ang)
    c0, c1, s0, s1 = cos[:, :q], cos[:, q:], sin[:, :q], sin[:, q:]
    z = jnp.zeros_like(s0)
    return (jnp.concatenate([c0, c0, c1, c1], -1), jnp.concatenate([-s0, z, -s1, z], -1),
            jnp.concatenate([z, s0, z, s1], -1))


def _pad_cols(a, left, total, fill=0.0):
    return jnp.pad(a, ((0, 0), (left, total - left - a.shape[1])), constant_values=fill)


def _with_ctx_rows(tab, tc, fill):
    return jnp.concatenate([tab, jnp.full((tc, tab.shape[1]), fill, F32)], axis=0)


def kernel(x, c, ctx, c_ctx, w_ada, b_ada, norm1_g, w_in, mla_q_norm_g, w_q_up, mla_kv_norm_g, w_kv_up, gqa_q_norm_g, gqa_k_norm_g, w_br_a, w_br_b, w_out, norm2_g, w_up, conv_w, conv_b, w_down, final_norm_g, loss_target, m_c_ctx, m_w_ada, m_b_ada, m_norm1_g, m_w_in, m_mla_q_norm_g, m_w_q_up, m_mla_kv_norm_g, m_w_kv_up, m_gqa_q_norm_g, m_gqa_k_norm_g, m_w_br_a, m_w_br_b, m_w_out, m_norm2_g, m_w_up, m_conv_w, m_conv_b, m_w_down, m_final_norm_g, v_c_ctx, v_w_ada, v_b_ada, v_norm1_g, v_w_in, v_mla_q_norm_g, v_w_q_up, v_mla_kv_norm_g, v_w_kv_up, v_gqa_q_norm_g, v_gqa_k_norm_g, v_w_br_a, v_w_br_b, v_w_out, v_norm2_g, v_w_up, v_conv_w, v_conv_b, v_w_down, v_final_norm_g):
    weights = dict(c_ctx=c_ctx, w_ada=w_ada, b_ada=b_ada, norm1_g=norm1_g, w_in=w_in, mla_q_norm_g=mla_q_norm_g,
                   w_q_up=w_q_up, mla_kv_norm_g=mla_kv_norm_g, w_kv_up=w_kv_up, gqa_q_norm_g=gqa_q_norm_g,
                   gqa_k_norm_g=gqa_k_norm_g, w_br_a=w_br_a, w_br_b=w_br_b, w_out=w_out, norm2_g=norm2_g, w_up=w_up,
                   conv_w=conv_w, conv_b=conv_b, w_down=w_down, final_norm_g=final_norm_g)
    mom_m = dict(c_ctx=m_c_ctx, w_ada=m_w_ada, b_ada=m_b_ada, norm1_g=m_norm1_g, w_in=m_w_in, mla_q_norm_g=m_mla_q_norm_g,
                 w_q_up=m_w_q_up, mla_kv_norm_g=m_mla_kv_norm_g, w_kv_up=m_w_kv_up, gqa_q_norm_g=m_gqa_q_norm_g,
                 gqa_k_norm_g=m_gqa_k_norm_g, w_br_a=m_w_br_a, w_br_b=m_w_br_b, w_out=m_w_out, norm2_g=m_norm2_g,
                 w_up=m_w_up, conv_w=m_conv_w, conv_b=m_conv_b, w_down=m_w_down, final_norm_g=m_final_norm_g)
    mom_v = dict(c_ctx=v_c_ctx, w_ada=v_w_ada, b_ada=v_b_ada, norm1_g=v_norm1_g, w_in=v_w_in, mla_q_norm_g=v_mla_q_norm_g,
                 w_q_up=v_w_q_up, mla_kv_norm_g=v_mla_kv_norm_g, w_kv_up=v_w_kv_up, gqa_q_norm_g=v_gqa_q_norm_g,
                 gqa_k_norm_g=v_gqa_k_norm_g, w_br_a=v_w_br_a, w_br_b=v_w_br_b, w_out=v_w_out, norm2_g=v_norm2_g,
                 w_up=v_w_up, conv_w=v_conv_w, conv_b=v_conv_b, w_down=v_w_down, final_norm_g=v_final_norm_g)
    order = list(weights)

    my_idx = 4 * lax.axis_index("x") + 2 * lax.axis_index("y") + lax.axis_index("c")
    xs, cts, tgt = x[0], ctx[0], loss_target[0]
    t, d = xs.shape
    tc = cts.shape[0]
    ta = t + tc
    kvl, ql = MLA_KV_LORA, MLA_Q_LORA
    nb = GQA_KV_HEADS * GQA_HEAD_DIM
    hb = GQA_HEADS * GQA_HEAD_DIM
    ha = MLA_HEADS
    f2 = w_up.shape[2] * N_DEV
    ff = f2 // 2

    big = ["w_in", "w_q_up", "w_kv_up", "w_br_a", "w_br_b", "w_out", "w_up", "w_down"]
    nw = len(big)
    del nw
    _ORDER_AFTER.clear()
    shards = {n: _cast_bf16(weights[n], "cast_" + n) for n in big}
    c_idx = jnp.reshape(lax.axis_index("c"), (1,)).astype(jnp.int32)

    def gather_start(names, dep):
        shs = [shards[n] for n in names]
        land = [lax.empty((N_DEV,) + s.shape, BF16) for s in shs]
        if dep is not None:
            _after(dep)
        s, r, arrs, tok = _split_start("gather_ici_start_" + names[0], shs + land, _gather_ici_copies(len(names)),
                                       5 * len(names))
        return dict(names=names, s=s, r=r, arrs=arrs, tok=tok)

    def gather_relay(g, after):
        n = len(g["names"])
        arrs = _split_wait("gather_ici_wait_" + g["names"][0], g["s"], g["r"], g["arrs"], _gather_ici_copies(n), after)
        s, r, bufs, tok = _split_start("gather_d2d_start_" + g["names"][0], arrs[n:], _gather_d2d_copies(n), 3 * n)
        g.update(s2=s, r2=r, bufs=bufs)
        return tok

    def gather_finish(g, after):
        n = len(g["names"])
        bufs = _split_wait("gather_d2d_wait_" + g["names"][0], g["s2"], g["r2"], g["bufs"], _gather_d2d_copies(n), after)
        return dict(zip(g["names"], bufs))

    c_all, cw_all = _all_gather([jnp.pad(c, ((0, 7), (0, 0))), jnp.pad(conv_w[0], ((0, 5), (0, 0)))], "gather_cond")
    conv_w_f = jnp.transpose(cw_all[:, :3, :], (1, 0, 2)).reshape(3, f2)
    conds = jnp.concatenate([c_all[:, 0, :], c_ctx[None, :], jnp.zeros((7, d), F32)], axis=0)
    ncol = w_ada.shape[2]
    b_shard = lax.dynamic_slice_in_dim(b_ada, my_idx * ncol, ncol, axis=1)
    ada_shard = _ada_fwd(conds, w_ada[0], b_shard)
    (ada_all,) = _all_gather([ada_shard], "gather_ada")
    ada = jnp.transpose(ada_all, (1, 0, 2)).reshape(16, N_DEV * ncol)
    lat = lax.dynamic_slice_in_dim(ada, my_idx, 1, axis=0).reshape(6, d)
    cxt = ada[8].reshape(6, d)
    zero2 = jnp.zeros((2, d), F32)
    mods1 = jnp.concatenate([lat[0:2], cxt[0:2], jnp.zeros((4, d), F32)], axis=0)
    mods2 = jnp.concatenate([lat[2:3], lat[3:4], lat[4:5], jnp.zeros((5, d), F32)], axis=0)
    mods2b = jnp.concatenate([lat[2:3], lat[4:5], jnp.zeros((6, d), F32)], axis=0)
    mods3 = jnp.concatenate([lat[5:6], jnp.zeros((7, d), F32)], axis=0)
    del zero2

    g0 = gather_start(["w_in"], ada_all)
    g1 = gather_start(["w_q_up", "w_kv_up", "w_br_a", "w_br_b", "w_out"], g0["tok"])
    g2 = gather_start(["w_up"], g1["tok"])
    g3 = gather_start(["w_down"], g2["tok"])

    ca, s1a, s2a = _rope_tabs(t, MLA_ROPE)
    cb_, s1b, s2b = _rope_tabs(t, GQA_HEAD_DIM)
    q_tabs_a = (_pad_cols(jnp.concatenate([jnp.ones((t, MLA_NOPE), F32), ca], 1), 0, MLA_SLOT),
                _pad_cols(s1a, MLA_NOPE, MLA_SLOT), _pad_cols(s2a, MLA_NOPE, MLA_SLOT))
    q_tabs_b = (cb_, s1b, s2b)
    k_tabs = (_with_ctx_rows(_pad_cols(ca, 0, LANE), tc, 1.0), _with_ctx_rows(_pad_cols(s1a, 0, LANE), tc, 0.0),
              _with_ctx_rows(_pad_cols(s2a, 0, LANE), tc, 0.0),
              _with_ctx_rows(cb_, tc, 1.0), _with_ctx_rows(s1b, tc, 0.0), _with_ctx_rows(s2b, tc, 0.0))

    def cols_full(g):
        return jnp.transpose(g, (1, 0, 2)).reshape(g.shape[1], N_DEV * g.shape[2])

    _after(gather_relay(g0, mods1))
    z_all = _norm_mod_fwd(cts, xs, norm1_g, mods1)
    gathered = gather_finish(g0, z_all)
    w_in_f = cols_full(gathered["w_in"])
    o_kpe, o_kb, o_vb = kvl, kvl + MLA_ROPE, kvl + MLA_ROPE + nb
    o_q = o_vb + nb
    o_g = o_q + ql + hb
    wkv_w = kvl + 2 * nb + LANE
    w_kv_p = jnp.concatenate([w_in_f[:, :kvl], w_in_f[:, o_kb:o_q], w_in_f[:, o_kpe:o_kb],
                              jnp.zeros((d, LANE - MLA_ROPE), BF16)], axis=1)
    q_w = ql + hb
    q_pad = (-q_w) % 512 if d >= 512 else (-q_w) % d
    gate_blk = (q_w + q_pad) // d
    assert (q_w + q_pad) % d == 0
    w_qg_p = jnp.concatenate([w_in_f[:, o_q:o_g], jnp.zeros((d, q_pad), BF16), w_in_f[:, o_g:]], axis=1)

    kv_all = _mm(z_all, w_kv_p, "nn", F32, "proj_kv", tm=1152, tn=wkv_w)
    qg = _mm(z_all, w_qg_p, "nn", F32, "proj_qg", tm=1024, tn=1024, rows=t)
    _after(gather_relay(g1, qg))
    kin, k_b, v_b = _key_prep_fwd(kv_all, mla_kv_norm_g, gqa_k_norm_g, k_tabs)
    sc_a = float((MLA_NOPE + MLA_ROPE) ** -0.5) * LOG2E
    sc_b = float(GQA_HEAD_DIM ** -0.5) * LOG2E
    cqn, q_b = _q_prep_fwd(qg, mla_q_norm_g, gqa_q_norm_g, q_tabs_b, sc_b)
    gathered.update(gather_finish(g1, q_b))

    wq_f = cols_full(gathered["w_q_up"]).reshape(ql, ha, MLA_NOPE + MLA_ROPE)
    wq_ext = jnp.pad(wq_f, ((0, 0), (0, 0), (0, MLA_SLOT - MLA_NOPE - MLA_ROPE))).reshape(ql, ha * MLA_SLOT)
    wkv_f = cols_full(gathered["w_kv_up"]).reshape(kvl, ha, MLA_NOPE + MLA_V)
    wk_slots = jnp.pad(wkv_f[:, :, :MLA_NOPE], ((0, 0), (0, 0), (0, MLA_SLOT - MLA_NOPE))).reshape(kvl, ha * MLA_SLOT)
    wv_cols = wkv_f[:, :, MLA_NOPE:].reshape(kvl, ha * MLA_V)
    e_slot = jnp.pad(jnp.eye(MLA_ROPE, dtype=BF16),
                     ((0, LANE - MLA_ROPE), (MLA_NOPE, MLA_SLOT - MLA_NOPE - MLA_ROPE)))
    e_rows = jnp.concatenate([jnp.tile(e_slot, (1, ha)), jnp.zeros((LANE, ha * MLA_V), BF16)], axis=1)
    wkv_ext = jnp.concatenate([jnp.concatenate([wk_slots, wv_cols], axis=1), e_rows], axis=0)
    w_bra = cols_full(gathered["w_br_a"])
    w_brb = cols_full(gathered["w_br_b"])
    w_out_f = gathered["w_out"].reshape(d, d)

    kv_a = _mm(kin, wkv_ext, "nn", BF16, "kv_up", tm=1152, tn=1024)
    qa_raw = _mm(cqn, wq_ext, "nn", F32, "q_up", tm=1024, tn=1024)
    q_a = _rope_a(qa_raw, q_tabs_a, False, BF16, "rope_q_fwd", sc_a)
    att_a = dict(hq=ha, hkv=ha, dk=MLA_SLOT, dv=MLA_V, k_blk0=0, v_blk0=ha * MLA_SLOT // MLA_V)
    att_b = dict(hq=GQA_HEADS, hkv=GQA_KV_HEADS, dk=GQA_HEAD_DIM, dv=GQA_HEAD_DIM, k_blk0=0, v_blk0=0)
    o_a, lse_a = _attention_fwd(q_a, kv_a, kv_a, name="attn_a_fwd", **att_a)
    o_b, lse_b = _attention_fwd(q_b, k_b, v_b, name="attn_b_fwd", **att_b)
    _after(gather_relay(g2, o_b))
    pa = _mm(o_a, w_bra, "nn", F32, "br_a", tm=1024, tn=1024)
    pb = _mm(o_b, w_brb, "nn", F32, "br_b", tm=1024, tn=1024)
    merged = _merge_fwd(pa, pb, qg, gate_blk)
    attn = _mm(merged, w_out_f, "nn", F32, "w_out", tm=1024, tn=1024)
    x1, z2 = _resid_norm_mod(xs, attn, norm2_g, mods2, "resid_norm2_fwd")
    w_up3 = gather_finish(g2, z2)["w_up"]
    _after(gather_relay(g3, z2))
    u = _mm_up_fwd(z2, w_up3, "w_up")
    w_down_f = gather_finish(g3, u)["w_down"].reshape(ff, d)
    h, uc = _conv_fwd(u, conv_w_f, conv_b)
    ffn = _mm(h, w_down_f, "nn", F32, "w_down", tm=1024, tn=1024, tk=2816)

    def to_shards(g):
        return jnp.transpose(g.reshape(g.shape[0], N_DEV, g.shape[1] // N_DEV), (1, 0, 2))

    def reduce_start(tag, names, sends):
        n = len(sends)
        land = [lax.empty((4,) + s.shape[1:], s.dtype) for s in sends]
        s, r, arrs, tok = _split_start("reduce_d2d_start_" + tag, sends + land, _reduce_d2d_copies(n), 4 * n)
        return dict(tag=tag, names=names, s=s, r=r, arrs=arrs, tok=tok)

    def reduce_relay(g, after):
        n = len(g["names"])
        arrs = _split_wait("reduce_d2d_wait_" + g["tag"], g["s"], g["r"], g["arrs"], _reduce_d2d_copies(n), after)
        sums = [_pair_sum(arrs[a], arrs[n + a], c_idx, "pair_sum_" + g["names"][a]) for a in range(n)]
        land = [lax.empty(s.shape, s.dtype) for s in sums]
        s, r, arrs2, tok = _split_start("reduce_ici_start_" + g["tag"], sums + land, _reduce_ici_copies(n), 4 * n)
        g.update(s2=s, r2=r, arrs2=arrs2)
        return tok

    def reduce_finish(g, after):
        n = len(g["names"])
        arrs2 = _split_wait("reduce_ici_wait_" + g["tag"], g["s2"], g["r2"], g["arrs2"], _reduce_ici_copies(n), after)
        return dict(zip(g["names"], arrs2[n:]))

    dx2, dffn, st_fin = _final_loss(x1, ffn, final_norm_g[None, :], mods3, tgt)
    loss = lax.psum(st_fin[3, 0], MESH_AXES)
    dh = _mm(dffn, w_down_f, "nt", F32, "d_h", tm=1024, tn=1024)
    g_w_down = _mm(h, dffn, "tn", BF16, "g_w_down", tm=512, tn=1024)
    r_down = reduce_start("down", ["w_down"], [g_w_down.reshape(N_DEV, ff // N_DEV, d)])
    _after(r_down["tok"])
    du3, dcw, dcb = _conv_bwd(u, uc, conv_w_f, dh)
    dz2 = _mm_up_dz(du3, w_up3, "d_z2")
    g_w_up = _mm_up_gw(z2, du3, N_DEV, "g_w_up")
    g_conv_w = jnp.concatenate([dcw[0], dcw[1]], axis=1)
    tok = reduce_relay(r_down, g_w_up)
    _after(tok)
    r_up = reduce_start("up", ["w_up", "conv_w"], [g_w_up, to_shards(jnp.pad(g_conv_w, ((0, 5), (0, 0))))])
    _after(tok, r_up["tok"])
    dx1, dattn, st_n2 = _norm2_bwd(x1, attn, norm2_g, mods2b, dz2, dx2)
    dmerged = _mm(dattn, w_out_f, "nt", F32, "d_merged", tm=1024, tn=1024)
    g_w_out = _mm(merged, dattn, "tn", BF16, "g_w_out", tm=1024, tn=1024)
    dpa, dpb, dgates = _merge_bwd(dmerged, pa, pb, qg, gate_blk)
    do_a = _mm(dpa, w_bra, "nt", BF16, "d_o_a", tm=1024, tn=1024)
    do_b = _mm(dpb, w_brb, "nt", BF16, "d_o_b", tm=1024, tn=1024)
    g_w_bra = _mm(o_a, dpa, "tn", BF16, "g_w_br_a", tm=1024, tn=1024)
    g_w_brb = _mm(o_b, dpb, "tn", BF16, "g_w_br_b", tm=1024, tn=1024)
    tok = reduce_relay(r_up, g_w_brb)
    _after(tok)
    r_out = reduce_start("out", ["w_out", "w_br_a", "w_br_b"],
                         [g_w_out.reshape(N_DEV, d // N_DEV, d), to_shards(g_w_bra), to_shards(g_w_brb)])
    _after(tok, r_out["tok"])
    dq_a, dk_a, dv_a = _attention_bwd(q_a, kv_a, kv_a, do_a, lse_a, name="attn_a_bwd", **att_a)
    dq_b, dk_b, dv_b = _attention_bwd(q_b, k_b, v_b, do_b, lse_b, name="attn_b_bwd", **att_b)
    _after(reduce_relay(r_out, dv_b))
    dqa_raw = _rope_a(dq_a, q_tabs_a, True, BF16, "rope_q_bwd", sc_a * LN2)
    dcqn = _mm(dqa_raw, wq_ext, "nt", F32, "d_cqn", tm=1024, tn=ql)
    g_wq_ext = _mm(cqn, dqa_raw, "tn", BF16, "g_w_q_up", tm=ql, tn=1024)
    dq_p, st_q, st_qb = _q_prep_bwd(qg, mla_q_norm_g, gqa_q_norm_g, q_tabs_b, dcqn, dq_b, q_pad, sc_b * LN2)
    dkin = _mm_cat_nt([(dk_a, wkv_ext, 0), (dv_a, wkv_ext, ha * MLA_SLOT)], F32, "d_kin", tm=1152, tn=kvl + LANE)
    g_wkv_ext = _mm_cat_tn(kin, [dk_a, dv_a], BF16, "g_w_kv_up", tm=kvl + LANE, tn=min(1024, ha * MLA_V))
    dkv_p, st_kv, st_kb = _key_prep_bwd(kv_all, mla_kv_norm_g, gqa_k_norm_g, k_tabs, dkin, dk_b, dv_b)
    g_wq = g_wq_ext.reshape(ql, ha, MLA_SLOT)[:, :, :MLA_NOPE + MLA_ROPE].reshape(ql, ha * (MLA_NOPE + MLA_ROPE))
    g_wkv = jnp.concatenate([g_wkv_ext[:kvl, :ha * MLA_SLOT].reshape(kvl, ha, MLA_SLOT)[:, :, :MLA_NOPE],
                             g_wkv_ext[:kvl, ha * MLA_SLOT:].reshape(kvl, ha, MLA_V)], axis=2).reshape(kvl, ha * (MLA_NOPE + MLA_V))
    r_qkv = reduce_start("qkv", ["w_q_up", "w_kv_up"], [to_shards(g_wq), to_shards(g_wkv)])
    _after(r_qkv["tok"])
    g_wkv_p = _mm(z_all, dkv_p, "tn", BF16, "g_w_in_kv", tm=1024, tn=wkv_w)
    g_wqg_p = _mm_cat_tn(z_all, [dq_p, dgates], BF16, "g_w_in_qg", tm=1024, tn=min(1024, d), rows=t)
    g_w_in = jnp.concatenate([g_wkv_p[:, :kvl], g_wkv_p[:, kvl + 2 * nb:kvl + 2 * nb + MLA_ROPE],
                              g_wkv_p[:, kvl:kvl + 2 * nb], g_wqg_p[:, :q_w], g_wqg_p[:, q_w + q_pad:]], axis=1)
    r_in = reduce_start("in", ["w_in"], [to_shards(g_w_in)])
    _after(r_in["tok"])
    qw_p = q_w + q_pad
    dz_lat = _mm_sum_nt([(dq_p, 0, w_qg_p, 0, qw_p), (dgates, 0, w_qg_p, qw_p, d), (dgates, d, w_qg_p, qw_p + d, d),
                         (dkv_p, 0, w_kv_p, 0, wkv_w)], F32, "d_z_lat", rows=t)
    tok_q = reduce_relay(r_qkv, dz_lat)
    tok_i = reduce_relay(r_in, dz_lat)
    _after(tok_q, tok_i)
    dz_ctx = _mm(dkv_p, w_kv_p, "nt", F32, "d_z_ctx", tm=min(ROW_BLOCK, tc), tn=1024, a_row_off=t)
    grad_x, st_n1 = _norm1_bwd(cts, xs, norm1_g, mods1, dz_ctx, dz_lat, dx1)

    res = {}

    def upd(nm, parts):
        wv, mv, vv = weights[nm], mom_m[nm], mom_v[nm]
        if wv.ndim == 1:
            wv, mv, vv = (a.reshape(1, -1) for a in (wv, mv, vv))
        outs = _adamw(parts, wv, mv, vv, "adamw_" + nm)
        res[nm] = [o_.reshape(weights[nm].shape) for o_ in outs]

    last = grad_x
    for grp in (r_down, r_up, r_out, r_qkv, r_in):
        recv = reduce_finish(grp, last)
        for nm in grp["names"]:
            upd(nm, recv[nm][:, :3, :] if nm == "conv_w" else recv[nm])
            last = res[nm][0]

    d_lat = jnp.concatenate([st_n1[0], st_n1[1], st_n2[3], st_n2[0], st_n2[1], st_fin[1]])
    d_cxt = jnp.concatenate([st_n1[3], st_n1[4], jnp.zeros((4 * d,), F32)])
    small = jnp.concatenate([d_lat, d_cxt, st_n1[2], st_q[0], st_kv[0], st_qb[0], st_kb[0], st_n2[2],
                             jnp.concatenate([dcb[0, 0], dcb[1, 0]]), st_fin[0]])
    n_small = small.shape[0]
    pad_small = (-n_small) % LANE
    _after(last)
    (small_all,) = _all_gather([jnp.pad(small, (0, pad_small)).reshape(1, -1)], "gather_small")
    offs = {}
    o = 0
    for nm, ln in (("d_lat", 6 * d), ("d_cxt", 6 * d), ("norm1_g", d), ("mla_q_norm_g", ql), ("mla_kv_norm_g", kvl),
                   ("gqa_q_norm_g", GQA_HEAD_DIM), ("gqa_k_norm_g", GQA_HEAD_DIM), ("norm2_g", d), ("conv_b", f2),
                   ("final_norm_g", d)):
        offs[nm] = (o, ln)
        o += ln

    def part(nm):
        a, ln = offs[nm]
        return small_all[:, :, a:a + ln]

    d_lat_all = part("d_lat")[:, 0, :]
    d_cxt_sum = _sum_parts(part("d_cxt"))
    da16 = jnp.concatenate([d_lat_all, d_cxt_sum, jnp.zeros((7, 6 * d), F32)], axis=0)
    da16_shard = lax.dynamic_slice_in_dim(da16, my_idx * ncol, ncol, axis=1)
    cc_part = _cctx_partial(da16_shard, w_ada[0], c_ctx[None, :])
    (cc_all,) = _all_gather([cc_part], "gather_cctx")
    cc_parts = cc_all[:, 0:1, :]

    for nm in ("norm1_g", "mla_q_norm_g", "mla_kv_norm_g", "gqa_q_norm_g", "gqa_k_norm_g", "norm2_g", "conv_b",
               "final_norm_g"):
        upd(nm, part(nm))
    upd("c_ctx", cc_parts)
    b_parts = jnp.concatenate([d_lat_all[:, None, :], d_cxt_sum[None]], axis=0)
    upd("b_ada", b_parts)
    outs = _adamw_ada(conds, da16_shard, w_ada[0], m_w_ada[0], v_w_ada[0])
    res["w_ada"] = [o_[None] for o_ in outs]

    return (loss, grad_x[None], *[res[n][0] for n in order], *[res[n][1] for n in order],
            *[res[n][2] for n in order], *[res[n][3] for n in order])
```

```python
import functools

import jax
import jax.numpy as jnp
from jax import lax
from jax.experimental import pallas as pl
from jax.experimental.pallas import tpu as pltpu

F32 = jnp.float32
BF16 = jnp.bfloat16

GRID_W = 64
ROPE_THETA = 10000.0
NORM_EPS = 1e-6
MLA_HEADS = 8
MLA_Q_LORA = 768
MLA_KV_LORA = 512
MLA_NOPE = 128
MLA_ROPE = 64
MLA_V = 128
GQA_HEADS = 8
GQA_KV_HEADS = 2
GQA_HEAD_DIM = 128
ADAM_LR = 0.001
ADAM_B1 = 0.9
ADAM_B2 = 0.999
ADAM_EPS = 1e-08
ADAM_WD = 0.01
ADAM_STEP = 10

N_DEV = 8
MESH_AXES = ("x", "y", "c")
LANE = 128
MLA_SLOT = 2 * LANE
VMEM_LIMIT = 56 * 1024 * 1024
ROW_BLOCK = 256
ATT_Q_BLOCK = 512
ATT_Q_BLOCK_FWD = 512
LN2 = 0.6931471805599453
LOG2E = 1.4426950408889634
MESH_ID = pl.DeviceIdType.MESH


def _tile(n, pref, align=LANE):
    if n <= pref:
        return n
    best = None
    t = align
    while t <= pref:
        if n % t == 0:
            best = t
        t += align
    assert best is not None, (n, pref, align)
    return best


def _cparams(sem=None):
    return pltpu.CompilerParams(dimension_semantics=sem, vmem_limit_bytes=VMEM_LIMIT)


_ORDER_AFTER = []


def _after(*arrays):
    _ORDER_AFTER.extend(arrays)


def _pcall(body, *, in_specs, **kw):
    deps = tuple(_ORDER_AFTER)
    _ORDER_AFTER.clear()
    if not deps:
        return pl.pallas_call(body, in_specs=in_specs, **kw)
    n_in, n_dep = len(in_specs), len(deps)

    def with_deps(*refs):
        body(*refs[:n_in], *refs[n_in + n_dep:])

    call = pl.pallas_call(with_deps, in_specs=list(in_specs) + [pl.BlockSpec(memory_space=pl.ANY)] * n_dep, **kw)
    return lambda *args: call(*args, *deps)


def _all_gather(arrs, name):
    n = len(arrs)

    def body(*refs):
        ins = refs[:n]
        outs = refs[n:2 * n]
        send_sems, recv_sems, local_sems = refs[2 * n:]
        x, y, c = lax.axis_index("x"), lax.axis_index("y"), lax.axis_index("c")
        me, sibling = (x, y, c), (x, y, 1 - c)
        chips = [(1 - x, y), (x, 1 - y), (1 - x, 1 - y)]

        def rows(a, dev):
            px, py, pc = dev
            return outs[a].at[4 * px + 2 * py + pc]

        def copy(a, k, block, to, src=None):
            return pltpu.make_async_remote_copy(
                src_ref=rows(a, block) if src is None else src,
                dst_ref=rows(a, block),
                send_sem=send_sems.at[7 * a + k],
                recv_sem=recv_sems.at[7 * a + k],
                device_id=to,
                device_id_type=MESH_ID,
            )

        mine = [pltpu.make_async_copy(ins[a], rows(a, me), local_sems.at[a]) for a in range(n)]
        for cp in mine:
            cp.start()
        first = []
        for a in range(n):
            first.append(copy(a, 0, me, sibling, src=ins[a]))
            first += [copy(a, 1 + j, me, (*chip, c), src=ins[a]) for j, chip in enumerate(chips)]
        for cp in first:
            cp.start()
        passed = []
        for j, chip in enumerate(chips):
            for a in range(n):
                copy(a, 1 + j, (*chip, c), me).wait_recv()
                fwd = copy(a, 4 + j, (*chip, c), sibling)
                fwd.start()
                passed.append(fwd)
        for a in range(n):
            copy(a, 0, sibling, me).wait_recv()
            for j, chip in enumerate(chips):
                copy(a, 4 + j, (*chip, 1 - c), me).wait_recv()
        for cp in first + passed:
            cp.wait_send()
        for cp in mine:
            cp.wait()

    any_spec = pl.BlockSpec(memory_space=pl.ANY)
    outs = _pcall(
        body,
        name=name,
        out_shape=[jax.ShapeDtypeStruct((N_DEV,) + a.shape, a.dtype) for a in arrs],
        in_specs=[any_spec] * n,
        out_specs=[any_spec] * n,
        scratch_shapes=[
            pltpu.SemaphoreType.DMA((7 * n,)),
            pltpu.SemaphoreType.DMA((7 * n,)),
            pltpu.SemaphoreType.DMA((n,)),
        ],
    )(*arrs)
    return list(outs)


def _all_to_all(arrs, name):
    n = len(arrs)

    def body(*refs):
        ins = refs[:n]
        outs = refs[n:2 * n]
        send_sems, recv_sems, local_sems = refs[2 * n:]
        x, y, c = lax.axis_index("x"), lax.axis_index("y"), lax.axis_index("c")
        my_idx = 4 * x + 2 * y + c

        def peer(k):
            fx, fy, fc = (k >> 2) & 1, (k >> 1) & 1, k & 1
            return (x ^ fx if fx else x, y ^ fy if fy else y, c ^ fc if fc else c)

        def copy(a, k):
            px, py, pc = peer(k)
            return pltpu.make_async_remote_copy(
                src_ref=ins[a].at[4 * px + 2 * py + pc],
                dst_ref=outs[a].at[my_idx],
                send_sem=send_sems.at[7 * a + k - 1],
                recv_sem=recv_sems.at[7 * a + k - 1],
                device_id=(px, py, pc),
                device_id_type=MESH_ID,
            )

        mine = [pltpu.make_async_copy(ins[a].at[my_idx], outs[a].at[my_idx], local_sems.at[a]) for a in range(n)]
        for cp in mine:
            cp.start()
        order = [1, 4, 2, 5, 3, 6, 7]
        cps = [copy(a, k) for k in order for a in range(n)]
        for cp in cps:
            cp.start()
        for cp in cps:
            cp.wait()
        for cp in mine:
            cp.wait()

    any_spec = pl.BlockSpec(memory_space=pl.ANY)
    outs = _pcall(
        body,
        name=name,
        out_shape=[jax.ShapeDtypeStruct(a.shape, a.dtype) for a in arrs],
        in_specs=[any_spec] * n,
        out_specs=[any_spec] * n,
        scratch_shapes=[
            pltpu.SemaphoreType.DMA((7 * n,)),
            pltpu.SemaphoreType.DMA((7 * n,)),
            pltpu.SemaphoreType.DMA((n,)),
        ],
    )(*arrs)
    return list(outs)


_HBM = pl.BlockSpec(memory_space=pltpu.HBM)
_SEM = pl.BlockSpec(memory_space=pltpu.SEMAPHORE)
_EFFECT = pltpu.SideEffectType.DATAFLOW_SIDE_EFFECTING


def _descriptors(copies, send_sems, recv_sems):
    descs = []
    for i, (src, dst, dev) in enumerate(copies):
        if dev is None:
            descs.append(pltpu.make_async_copy(src, dst, recv_sems.at[i]))
        else:
            descs.append(pltpu.make_async_remote_copy(src_ref=src, dst_ref=dst, send_sem=send_sems.at[i],
                                                      recv_sem=recv_sems.at[i], device_id=dev, device_id_type=MESH_ID))
    return descs


def _split_start(name, arrays, copies_fn, n_copies):
    n = len(arrays)

    def body(*refs):
        send_sems, recv_sems = refs[n], refs[n + 1]
        token = refs[2 * n + 2]
        for dsc in _descriptors(copies_fn(refs[:n]), send_sems, recv_sems):
            dsc.start()
        token[...] = jnp.zeros_like(token)

    outs = _pcall(
        body,
        name=name,
        out_shape=(pltpu.SemaphoreType.DMA((n_copies,)), pltpu.SemaphoreType.DMA((n_copies,)),
                   *[pltpu.HBM(a.shape, a.dtype) for a in arrays], jax.ShapeDtypeStruct((8, LANE), F32)),
        in_specs=[_HBM] * n,
        out_specs=(_SEM, _SEM, *[_HBM] * n, pl.BlockSpec(memory_space=pltpu.VMEM)),
        input_output_aliases={i: 2 + i for i in range(n)},
        compiler_params=pltpu.CompilerParams(has_side_effects=_EFFECT),
    )(*[pltpu.with_memory_space_constraint(a, pltpu.HBM) for a in arrays])
    return outs[0], outs[1], list(outs[2:2 + n]), outs[2 + n]


def _split_wait(name, send_sems, recv_sems, arrays, copies_fn, after):
    n = len(arrays)

    def body(*refs):
        for dsc, (_, _, dev) in zip(_descriptors(copies_fn(refs[:n]), refs[n], refs[n + 1]), copies_fn(refs[:n])):
            if dev is None:
                dsc.wait()
            else:
                dsc.wait_send()
                dsc.wait_recv()

    outs = _pcall(
        body,
        name=name,
        out_shape=tuple(pltpu.HBM(a.shape, a.dtype) for a in arrays),
        in_specs=[_HBM] * n + [_SEM, _SEM, pl.BlockSpec(memory_space=pl.ANY)],
        out_specs=tuple([_HBM] * n),
        input_output_aliases={i: i for i in range(n)},
        compiler_params=pltpu.CompilerParams(has_side_effects=_EFFECT),
    )(*arrays, send_sems, recv_sems, after)
    return list(outs)


def _mesh_pos():
    x, y, c = lax.axis_index("x"), lax.axis_index("y"), lax.axis_index("c")
    return x, y, c, [(1 - x, y), (x, 1 - y), (1 - x, 1 - y)]


def _gather_ici_copies(n):
    def copies(refs):
        x, y, c, chips = _mesh_pos()
        me = 4 * x + 2 * y + c
        out = []
        for a in range(n):
            src, buf = refs[a], refs[n + a]
            out.append((src, buf.at[me], None))
            out.append((src, buf.at[me], (x, y, 1 - c)))
            out += [(src, buf.at[me], (cx, cy, c)) for cx, cy in chips]
        return out
    return copies


def _gather_d2d_copies(n):
    def copies(refs):
        x, y, c, chips = _mesh_pos()
        out = []
        for a in range(n):
            for cx, cy in chips:
                rows = refs[a].at[4 * cx + 2 * cy + c]
                out.append((rows, rows, (x, y, 1 - c)))
        return out
    return copies


def _reduce_d2d_copies(n):
    def copies(refs):
        x, y, c, _ = _mesh_pos()
        out = []
        for a in range(n):
            for k in range(4):
                out.append((refs[a].at[2 * k + (1 - c)], refs[n + a].at[k], (x, y, 1 - c)))
        return out
    return copies


def _reduce_ici_copies(n):
    def copies(refs):
        x, y, c, chips = _mesh_pos()
        mine = 2 * x + y
        out = []
        for a in range(n):
            src, land = refs[a], refs[n + a]
            out.append((src.at[mine], land.at[mine], None))
            out += [(src.at[2 * cx + cy], land.at[mine], (cx, cy, c)) for cx, cy in chips]
        return out
    return copies


def _pair_sum(send, land, c_idx, name):
    _, r, cols = send.shape
    rb = _tile(r, max(8, (1 << 22) // (send.dtype.itemsize * cols) // 8 * 8), 8)
    dt = send.dtype

    def body(c_ref, s_ref, l_ref, o_ref):
        o_ref[...] = (s_ref[...].astype(F32) + l_ref[...].astype(F32)).astype(dt)

    return pl.pallas_call(
        body,
        name=name,
        out_shape=jax.ShapeDtypeStruct((4, r, cols), dt),
        grid_spec=pltpu.PrefetchScalarGridSpec(
            num_scalar_prefetch=1,
            grid=(4, r // rb),
            in_specs=[pl.BlockSpec((None, rb, cols), lambda k, i, c_ref: (2 * k + c_ref[0], i, 0)),
                      pl.BlockSpec((None, rb, cols), lambda k, i, c_ref: (k, i, 0))],
            out_specs=pl.BlockSpec((None, rb, cols), lambda k, i, c_ref: (k, i, 0)),
        ),
        compiler_params=_cparams(("parallel", "parallel")),
    )(c_idx, send, land)


_DIMS = {
    "nn": (((1,), (0,)), ((), ())),
    "nt": (((1,), (1,)), ((), ())),
    "tn": (((0,), (0,)), ((), ())),
}


def _mm_call(a, b, *, mode, grid, a_spec, b_spec, o_spec, out_shape, acc_shape, name):
    nk = grid[2]
    out_dtype = out_shape.dtype

    def body(a_ref, b_ref, o_ref, *scratch):
        p = lax.dot_general(a_ref[...].astype(BF16), b_ref[...].astype(BF16), _DIMS[mode],
                            preferred_element_type=F32)
        if nk == 1:
            o_ref[...] = p.astype(out_dtype)
        else:
            acc = scratch[0]
            k = pl.program_id(2)

            @pl.when(k == 0)
            def _():
                acc[...] = p

            @pl.when(k > 0)
            def _():
                acc[...] += p

            @pl.when(k == nk - 1)
            def _():
                o_ref[...] = acc[...].astype(out_dtype)

    return _pcall(
        body,
        name=name,
        out_shape=out_shape,
        grid=grid,
        in_specs=[a_spec, b_spec],
        out_specs=o_spec,
        scratch_shapes=[pltpu.VMEM(acc_shape, F32)] if nk > 1 else [],
        compiler_params=_cparams(("parallel", "parallel", "arbitrary")),
    )(a, b)


def _mm(a, b, mode, out_dtype, name, tm=512, tn=512, tk=2432, a_row_off=0, rows=None):
    if mode == "nn":
        (m, k), (k2, n) = a.shape, b.shape
    elif mode == "nt":
        (m, k), (n, k2) = a.shape, b.shape
    else:
        (k, m), (k2, n) = a.shape, b.shape
        if rows is not None:
            k = k2 = rows
    assert k == k2, (a.shape, b.shape, mode)
    if mode != "tn":
        m = (m if rows is None else rows + a_row_off) - a_row_off
    tm, tn, tk = _tile(m, tm, 8), _tile(n, tn), _tile(k, tk, 8 if mode == "tn" else LANE)
    assert a_row_off % tm == 0
    ro = a_row_off // tm
    grid = (m // tm, n // tn, k // tk)
    if mode == "tn":
        a_spec = pl.BlockSpec((tk, tm), lambda i, j, kk: (kk, i))
    else:
        a_spec = pl.BlockSpec((tm, tk), lambda i, j, kk: (i + ro, kk))
    if mode == "nt":
        b_spec = pl.BlockSpec((tn, tk), lambda i, j, kk: (j, kk))
    else:
        b_spec = pl.BlockSpec((tk, tn), lambda i, j, kk: (kk, j))
    o_spec = pl.BlockSpec((tm, tn), lambda i, j, kk: (i, j))
    return _mm_call(a, b, mode=mode, grid=grid, a_spec=a_spec, b_spec=b_spec, o_spec=o_spec,
                    out_shape=jax.ShapeDtypeStruct((m, n), out_dtype), acc_shape=(tm, tn), name=name)


def _mm_cat_nt(pieces, out_dtype, name, tm=1024, tn=1024, tk=2048, rows=None):
    m = pieces[0][0].shape[0] if rows is None else rows
    n = pieces[0][1].shape[0]
    tm, tn = _tile(m, tm, 8), _tile(n, tn)
    steps, starts, s = [], [], 0
    for a, b, off in pieces:
        kp = a.shape[1]
        tkp = _tile(kp, tk)
        assert off % tkp == 0 and b.shape[0] == n
        steps.append((tkp, kp // tkp, off // tkp))
        starts.append(s)
        s += kp // tkp
    nk = s
    npc = len(pieces)

    def body(*refs):
        o_ref, acc = refs[2 * npc], refs[2 * npc + 1]
        kk = pl.program_id(2)

        @pl.when(kk == 0)
        def _():
            acc[...] = jnp.zeros_like(acc)

        for p in range(npc):
            @pl.when((kk >= starts[p]) & (kk < starts[p] + steps[p][1]))
            def _(p=p):
                acc[...] += lax.dot_general(refs[2 * p][...].astype(BF16), refs[2 * p + 1][...].astype(BF16), _DIMS["nt"],
                                            preferred_element_type=F32)

        @pl.when(kk == nk - 1)
        def _():
            o_ref[...] = acc[...].astype(out_dtype)

    in_specs, args = [], []
    for p, (a, b, off) in enumerate(pieces):
        tkp, np_, ob = steps[p]

        def rel(kk, p=p, np_=np_):
            return jnp.clip(kk - starts[p], 0, np_ - 1)

        in_specs.append(pl.BlockSpec((tm, tkp), lambda i, j, kk, rel=rel: (i, rel(kk))))
        in_specs.append(pl.BlockSpec((tn, tkp), lambda i, j, kk, rel=rel, ob=ob: (j, ob + rel(kk))))
        args += [a, b]
    return _pcall(
        body,
        name=name,
        out_shape=jax.ShapeDtypeStruct((m, n), out_dtype),
        grid=(m // tm, n // tn, nk),
        in_specs=in_specs,
        out_specs=pl.BlockSpec((tm, tn), lambda i, j, kk: (i, j)),
        scratch_shapes=[pltpu.VMEM((tm, tn), F32)],
        compiler_params=_cparams(("parallel", "parallel", "arbitrary")),
    )(*args)


def _mm_cat_tn(a, pieces, out_dtype, name, tm=1024, tn=1024, rows=None):
    k = a.shape[0] if rows is None else rows
    m = a.shape[1]
    tm = _tile(m, tm)
    starts, s = [], 0
    for b in pieces:
        assert b.shape[1] % tn == 0
        starts.append(s)
        s += b.shape[1] // tn
    nj = s
    npc = len(pieces)

    def body(*refs):
        a_ref, o_ref = refs[0], refs[1 + npc]
        j = pl.program_id(1)
        for p in range(npc):
            @pl.when((j >= starts[p]) & (j < starts[p] + pieces[p].shape[1] // tn))
            def _(p=p):
                o_ref[...] = lax.dot_general(a_ref[...].astype(BF16), refs[1 + p][...].astype(BF16), _DIMS["tn"],
                                             preferred_element_type=F32).astype(out_dtype)

    in_specs = [pl.BlockSpec((k, tm), lambda i, j: (0, i))]
    for p, b in enumerate(pieces):
        np_ = b.shape[1] // tn
        in_specs.append(pl.BlockSpec((k, tn), lambda i, j, p=p, np_=np_: (0, jnp.clip(j - starts[p], 0, np_ - 1))))
    return _pcall(
        body,
        name=name,
        out_shape=jax.ShapeDtypeStruct((m, nj * tn), out_dtype),
        grid=(m // tm, nj),
        in_specs=in_specs,
        out_specs=pl.BlockSpec((tm, tn), lambda i, j: (i, j)),
        compiler_params=_cparams(("parallel", "arbitrary")),
    )(a, *pieces)


def _mm_up_fwd(z2, w3, name, tm=1024):
    t, d = z2.shape
    nsh, _, c = w3.shape
    tm = _tile(t, tm, 8)
    return _mm_call(z2, w3, mode="nn", grid=(t // tm, nsh, 1),
                    a_spec=pl.BlockSpec((tm, d), lambda i, j, kk: (i, 0)),
                    b_spec=pl.BlockSpec((None, d, c), lambda i, j, kk: (j, 0, 0)),
                    o_spec=pl.BlockSpec((tm, c), lambda i, j, kk: (i, j)),
                    out_shape=jax.ShapeDtypeStruct((t, nsh * c), BF16), acc_shape=(tm, c), name=name)


def _mm_up_dz(du3, w3, name, tm=512, tn=1024):
    _, t, f = du3.shape
    nsh, d, c = w3.shape
    half = nsh // 2
    assert f == half * c
    tm, tn = _tile(t, tm, 8), _tile(d, tn)

    def body(a_ref, b_ref, o_ref, acc):
        kk = pl.program_id(2)
        p = None
        for s in range(half):
            q = lax.dot_general(a_ref[:, s * c:(s + 1) * c], b_ref[s], _DIMS["nt"], preferred_element_type=F32)
            p = q if p is None else p + q

        @pl.when(kk == 0)
        def _():
            acc[...] = p

        @pl.when(kk == 1)
        def _():
            o_ref[...] = (acc[...] + p).astype(BF16)

    return _pcall(
        body,
        name=name,
        out_shape=jax.ShapeDtypeStruct((t, d), BF16),
        grid=(t // tm, d // tn, 2),
        in_specs=[pl.BlockSpec((None, tm, f), lambda i, j, kk: (kk, i, 0)),
                  pl.BlockSpec((half, tn, c), lambda i, j, kk: (kk, j, 0))],
        out_specs=pl.BlockSpec((tm, tn), lambda i, j, kk: (i, j)),
        scratch_shapes=[pltpu.VMEM((tm, tn), F32)],
        compiler_params=_cparams(("parallel", "parallel", "arbitrary")),
    )(du3, w3)


def _mm_sum_nt(pieces, out_dtype, name, tm=512, tn=512, rows=None):
    m = pieces[0][0].shape[0] if rows is None else rows
    n = pieces[0][2].shape[0]
    tm, tn = _tile(m, tm, 8), _tile(n, tn)
    npc = len(pieces)

    def body(*refs):
        p = None
        for s in range(npc):
            q = lax.dot_general(refs[2 * s][...].astype(BF16), refs[2 * s + 1][...].astype(BF16), _DIMS["nt"],
                                preferred_element_type=F32)
            p = q if p is None else p + q
        refs[2 * npc][...] = p.astype(out_dtype)

    in_specs, args = [], []
    for a, ao, b, bo, kp in pieces:
        assert ao % kp == 0 and bo % kp == 0 and b.shape[0] == n
        in_specs.append(pl.BlockSpec((tm, kp), lambda i, j, ab=ao // kp: (i, ab)))
        in_specs.append(pl.BlockSpec((tn, kp), lambda i, j, bb=bo // kp: (j, bb)))
        args += [a, b]
    return _pcall(
        body,
        name=name,
        out_shape=jax.ShapeDtypeStruct((m, n), out_dtype),
        grid=(m // tm, n // tn),
        in_specs=in_specs,
        out_specs=pl.BlockSpec((tm, tn), lambda i, j: (i, j)),
        compiler_params=_cparams(("parallel", "parallel")),
    )(*args)


def _mm_up_gw(z2, du3, nsh, name, tm=1024):
    t, d = z2.shape
    f = du3.shape[2]
    half = nsh // 2
    c = f // half
    tm = _tile(d, tm)
    return _mm_call(z2, du3, mode="tn", grid=(d // tm, nsh, 1),
                    a_spec=pl.BlockSpec((t, tm), lambda i, j, kk: (0, i)),
                    b_spec=pl.BlockSpec((None, t, c), lambda i, j, kk: (j // half, 0, j % half)),
                    o_spec=pl.BlockSpec((None, tm, c), lambda i, j, kk: (j, i, 0)),
                    out_shape=jax.ShapeDtypeStruct((nsh, d, c), BF16), acc_shape=(tm, c), name=name)


def _rms(x):
    r = lax.rsqrt(jnp.mean(x * x, axis=-1, keepdims=True) + NORM_EPS)
    return x * r, r


def _rms_bwd(dxh, xh, r):
    return r * (dxh - xh * jnp.mean(dxh * xh, axis=-1, keepdims=True))


def _colsum(v):
    return jnp.sum(v, axis=0, keepdims=True)


def _rope(v, c, s1, s2, q):
    w = v.shape[-1]
    return v * c + pltpu.roll(v, w - q, 1) * s1 + pltpu.roll(v, q, 1) * s2


def _rope_t(d, c, s1, s2, q):
    w = d.shape[-1]
    return d * c + pltpu.roll(d * s1, q, 1) + pltpu.roll(d * s2, w - q, 1)


def _norm_mod_fwd(ctx, x, gain, mods):
    tc, d = ctx.shape
    t = x.shape[0]
    rb = min(ROW_BLOCK, tc)
    nbl = t // rb

    def body(ctx_ref, x_ref, g_ref, mod_ref, z_ref):
        i = pl.program_id(0)

        def emit(src, sh, sc):
            xh, _ = _rms(src[...])
            z_ref[...] = ((xh * g_ref[...]) * (1.0 + sc) + sh).astype(BF16)

        @pl.when(i >= nbl)
        def _():
            emit(ctx_ref, mod_ref[2:3, :], mod_ref[3:4, :])

        @pl.when(i < nbl)
        def _():
            emit(x_ref, mod_ref[0:1, :], mod_ref[1:2, :])

    return _pcall(
        body,
        name="norm1_mod_fwd",
        out_shape=jax.ShapeDtypeStruct((tc + t, d), BF16),
        grid=((tc + t) // rb,),
        in_specs=[
            pl.BlockSpec((rb, d), lambda i: (jnp.maximum(i - nbl, 0), 0)),
            pl.BlockSpec((rb, d), lambda i: (jnp.minimum(i, nbl - 1), 0)),
            pl.BlockSpec((1, d), lambda i: (0, 0)),
            pl.BlockSpec((8, d), lambda i: (0, 0)),
        ],
        out_specs=pl.BlockSpec((rb, d), lambda i: (i, 0)),
        compiler_params=_cparams(("arbitrary",)),
    )(ctx, x, gain, mods)


def _norm1_bwd(ctx, x, gain, mods, dz_ctx, dz_lat, dx1):
    tc, d = ctx.shape
    t = x.shape[0]
    rb = min(ROW_BLOCK, tc)
    nbl = t // rb

    def body(ctx_ref, x_ref, g_ref, mod_ref, dzc_ref, dzl_ref, dx1_ref, gx_ref, st_ref):
        i = pl.program_id(0)

        @pl.when(i == 0)
        def _():
            st_ref[...] = jnp.zeros_like(st_ref)

        def common(src, dz, sc, row_sh, row_sc):
            xh, r = _rms(src[...])
            g = g_ref[...]
            dxn = dz * (1.0 + sc)
            st_ref[row_sh:row_sh + 1, :] += _colsum(dz)
            st_ref[row_sc:row_sc + 1, :] += _colsum(dz * (xh * g))
            st_ref[2:3, :] += _colsum(dxn * xh)
            return _rms_bwd(dxn * g, xh, r)

        @pl.when(i >= nbl)
        def _():
            common(ctx_ref, dzc_ref[...], mod_ref[3:4, :], 3, 4)

        @pl.when(i < nbl)
        def _():
            gx_ref[...] = dx1_ref[...] + common(x_ref, dzl_ref[...], mod_ref[1:2, :], 0, 1)

    lat = lambda i: (jnp.minimum(i, nbl - 1), 0)
    cix = lambda i: (jnp.maximum(i - nbl, 0), 0)
    return _pcall(
        body,
        name="norm1_mod_bwd",
        out_shape=[jax.ShapeDtypeStruct((t, d), F32), jax.ShapeDtypeStruct((8, d), F32)],
        grid=((tc + t) // rb,),
        in_specs=[
            pl.BlockSpec((rb, d), cix),
            pl.BlockSpec((rb, d), lat),
            pl.BlockSpec((1, d), lambda i: (0, 0)),
            pl.BlockSpec((8, d), lambda i: (0, 0)),
            pl.BlockSpec((rb, d), cix),
            pl.BlockSpec((rb, d), lat),
            pl.BlockSpec((rb, d), lat),
        ],
        out_specs=[pl.BlockSpec((rb, d), lat), pl.BlockSpec((8, d), lambda i: (0, 0))],
        compiler_params=_cparams(("arbitrary",)),
    )(ctx, x, gain, mods, dz_ctx, dz_lat, dx1)


def _key_prep_fwd(kv, kv_gain, kb_gain, tabs):
    ta, wkv = kv.shape
    kvl = MLA_KV_LORA
    nb = GQA_KV_HEADS * GQA_HEAD_DIM
    rb = ROW_BLOCK if ta % ROW_BLOCK == 0 else LANE
    hd = GQA_HEAD_DIM

    def body(kv_ref, g_ref, gb_ref, ca, s1a, s2a, cb, s1b, s2b, kin_ref, kb_ref, vb_ref):
        xh, _ = _rms(kv_ref[:, 0:kvl])
        kin_ref[:, 0:kvl] = (xh * g_ref[...]).astype(BF16)
        kpe = kv_ref[:, kvl + 2 * nb:kvl + 2 * nb + LANE]
        kin_ref[:, kvl:kvl + LANE] = _rope(kpe, ca[...], s1a[...], s2a[...], MLA_ROPE // 4).astype(BF16)
        for h in range(GQA_KV_HEADS):
            nh, _ = _rms(kv_ref[:, kvl + h * hd:kvl + (h + 1) * hd])
            kb_ref[:, h * hd:(h + 1) * hd] = _rope(nh * gb_ref[...], cb[...], s1b[...], s2b[...], hd // 4).astype(BF16)
        vb_ref[...] = kv_ref[:, kvl + nb:kvl + 2 * nb].astype(BF16)

    row = lambda w: pl.BlockSpec((rb, w), lambda i: (i, 0))
    fix = lambda w: pl.BlockSpec((1, w), lambda i: (0, 0))
    return _pcall(
        body,
        name="key_prep_fwd",
        out_shape=[jax.ShapeDtypeStruct((ta, kvl + LANE), BF16), jax.ShapeDtypeStruct((ta, nb), BF16),
                   jax.ShapeDtypeStruct((ta, nb), BF16)],
        grid=(ta // rb,),
        in_specs=[row(wkv), fix(kvl), fix(hd)] + [row(LANE)] * 3 + [row(hd)] * 3,
        out_specs=[row(kvl + LANE), row(nb), row(nb)],
        compiler_params=_cparams(("parallel",)),
    )(kv, kv_gain, kb_gain, *tabs)


def _key_prep_bwd(kv, kv_gain, kb_gain, tabs, dkin, dkb, dvb):
    ta, wkv = kv.shape
    kvl = MLA_KV_LORA
    nb = GQA_KV_HEADS * GQA_HEAD_DIM
    rb = ROW_BLOCK if ta % ROW_BLOCK == 0 else LANE
    hd = GQA_HEAD_DIM

    def body(kv_ref, g_ref, gb_ref, ca, s1a, s2a, cb, s1b, s2b, dkin_ref, dkb_ref, dvb_ref, dkv_ref, st_ref, stb_ref):
        @pl.when(pl.program_id(0) == 0)
        def _():
            st_ref[...] = jnp.zeros_like(st_ref)
            stb_ref[...] = jnp.zeros_like(stb_ref)

        xh, r = _rms(kv_ref[:, 0:kvl])
        dn = dkin_ref[:, 0:kvl]
        st_ref[0:1, :] += _colsum(dn * xh)
        dkv_ref[:, 0:kvl] = _rms_bwd(dn * g_ref[...], xh, r).astype(BF16)
        dpe = _rope_t(dkin_ref[:, kvl:kvl + LANE], ca[...], s1a[...], s2a[...], MLA_ROPE // 4)
        dkv_ref[:, kvl + 2 * nb:kvl + 2 * nb + LANE] = dpe.astype(BF16)
        for h in range(GQA_KV_HEADS):
            nh, rh = _rms(kv_ref[:, kvl + h * hd:kvl + (h + 1) * hd])
            dn_h = _rope_t(dkb_ref[:, h * hd:(h + 1) * hd], cb[...], s1b[...], s2b[...], hd // 4)
            stb_ref[0:1, :] += _colsum(dn_h * nh)
            dkv_ref[:, kvl + h * hd:kvl + (h + 1) * hd] = _rms_bwd(dn_h * gb_ref[...], nh, rh).astype(BF16)
        dkv_ref[:, kvl + nb:kvl + 2 * nb] = dvb_ref[...].astype(BF16)

    row = lambda w: pl.BlockSpec((rb, w), lambda i: (i, 0))
    fix = lambda w: pl.BlockSpec((1, w), lambda i: (0, 0))
    return _pcall(
        body,
        name="key_prep_bwd",
        out_shape=[jax.ShapeDtypeStruct((ta, wkv), BF16), jax.ShapeDtypeStruct((8, kvl), F32),
                   jax.ShapeDtypeStruct((8, hd), F32)],
        grid=(ta // rb,),
        in_specs=[row(wkv), fix(kvl), fix(hd)] + [row(LANE)] * 3 + [row(hd)] * 3 + [row(kvl + LANE), row(nb), row(nb)],
        out_specs=[row(wkv), pl.BlockSpec((8, kvl), lambda i: (0, 0)), pl.BlockSpec((8, hd), lambda i: (0, 0))],
        compiler_params=_cparams(("arbitrary",)),
    )(kv, kv_gain, kb_gain, *tabs, dkin, dkb, dvb)


def _q_prep_fwd(qg, q_gain, qb_gain, tabs, qscale):
    t = qg.shape[0]
    ql = MLA_Q_LORA
    hd = GQA_HEAD_DIM
    hb = GQA_HEADS * hd
    rb = min(ROW_BLOCK, t)

    def body(q_ref, g_ref, gb_ref, cb, s1b, s2b, cqn_ref, qb_ref):
        xh, _ = _rms(q_ref[:, 0:ql])
        cqn_ref[...] = (xh * g_ref[...]).astype(BF16)
        for h in range(GQA_HEADS):
            nh, _ = _rms(q_ref[:, ql + h * hd:ql + (h + 1) * hd])
            qh = _rope(nh * gb_ref[...], cb[...], s1b[...], s2b[...], hd // 4)
            qb_ref[:, h * hd:(h + 1) * hd] = (qh * qscale).astype(BF16)

    row = lambda w: pl.BlockSpec((rb, w), lambda i: (i, 0))
    fix = lambda w: pl.BlockSpec((1, w), lambda i: (0, 0))
    return _pcall(
        body,
        name="q_prep_fwd",
        out_shape=[jax.ShapeDtypeStruct((t, ql), BF16), jax.ShapeDtypeStruct((t, hb), BF16)],
        grid=(t // rb,),
        in_specs=[row(ql + hb), fix(ql), fix(hd)] + [row(hd)] * 3,
        out_specs=[row(ql), row(hb)],
        compiler_params=_cparams(("parallel",)),
    )(qg, q_gain, qb_gain, *tabs)


def _q_prep_bwd(qg, q_gain, qb_gain, tabs, dcqn, dqb, wpad, qscale):
    t = qg.shape[0]
    ql = MLA_Q_LORA
    hd = GQA_HEAD_DIM
    hb = GQA_HEADS * hd
    rb = min(ROW_BLOCK, t)

    def body(q_ref, g_ref, gb_ref, cb, s1b, s2b, dcqn_ref, dqb_ref, dq_ref, st_ref, stb_ref):
        @pl.when(pl.program_id(0) == 0)
        def _():
            st_ref[...] = jnp.zeros_like(st_ref)
            stb_ref[...] = jnp.zeros_like(stb_ref)

        xh, r = _rms(q_ref[:, 0:ql])
        dn = dcqn_ref[...]
        st_ref[0:1, :] += _colsum(dn * xh)
        dq_ref[:, 0:ql] = _rms_bwd(dn * g_ref[...], xh, r).astype(BF16)
        for h in range(GQA_HEADS):
            nh, rh = _rms(q_ref[:, ql + h * hd:ql + (h + 1) * hd])
            dn_h = _rope_t(dqb_ref[:, h * hd:(h + 1) * hd] * qscale, cb[...], s1b[...], s2b[...], hd // 4)
            stb_ref[0:1, :] += _colsum(dn_h * nh)
            dq_ref[:, ql + h * hd:ql + (h + 1) * hd] = _rms_bwd(dn_h * gb_ref[...], nh, rh).astype(BF16)
        if wpad:
            dq_ref[:, ql + hb:ql + hb + wpad] = jnp.zeros((rb, wpad), BF16)

    row = lambda w: pl.BlockSpec((rb, w), lambda i: (i, 0))
    fix = lambda w: pl.BlockSpec((1, w), lambda i: (0, 0))
    return _pcall(
        body,
        name="q_prep_bwd",
        out_shape=[jax.ShapeDtypeStruct((t, ql + hb + wpad), BF16), jax.ShapeDtypeStruct((8, ql), F32),
                   jax.ShapeDtypeStruct((8, hd), F32)],
        grid=(t // rb,),
        in_specs=[row(ql + hb), fix(ql), fix(hd)] + [row(hd)] * 3 + [row(ql), row(hb)],
        out_specs=[row(ql + hb + wpad), pl.BlockSpec((8, ql), lambda i: (0, 0)), pl.BlockSpec((8, hd), lambda i: (0, 0))],
        compiler_params=_cparams(("arbitrary",)),
    )(qg, q_gain, qb_gain, *tabs, dcqn, dqb)


def _rope_a(v, tabs, transpose, out_dtype, name, qscale):
    t, w = v.shape
    rb = min(ROW_BLOCK, t)
    fn = _rope_t if transpose else _rope

    def body(v_ref, c, s1, s2, o_ref):
        for h in range(w // MLA_SLOT):
            sl = slice(h * MLA_SLOT, (h + 1) * MLA_SLOT)
            o_ref[:, sl] = (fn(v_ref[:, sl].astype(F32), c[...], s1[...], s2[...], MLA_ROPE // 4) * qscale).astype(out_dtype)

    row = lambda ww: pl.BlockSpec((rb, ww), lambda i: (i, 0))
    return _pcall(
        body,
        name=name,
        out_shape=jax.ShapeDtypeStruct((t, w), out_dtype),
        grid=(t // rb,),
        in_specs=[row(w)] + [row(MLA_SLOT)] * 3,
        out_specs=row(w),
        compiler_params=_cparams(("parallel",)),
    )(v, *tabs)


def _merge_fwd(pa, pb, qg, gate_blk):
    t, d = pa.shape
    rb = min(ROW_BLOCK, t)

    def body(pa_ref, pb_ref, ga_ref, gb_ref, o_ref):
        o_ref[...] = (jax.nn.sigmoid(ga_ref[...]) * pa_ref[...].astype(F32)
                      + jax.nn.sigmoid(gb_ref[...]) * pb_ref[...].astype(F32)).astype(BF16)

    row = pl.BlockSpec((rb, d), lambda i: (i, 0))
    return _pcall(
        body,
        name="merge_fwd",
        out_shape=jax.ShapeDtypeStruct((t, d), BF16),
        grid=(t // rb,),
        in_specs=[row, row, pl.BlockSpec((rb, d), lambda i: (i, gate_blk)), pl.BlockSpec((rb, d), lambda i: (i, gate_blk + 1))],
        out_specs=row,
        compiler_params=_cparams(("parallel",)),
    )(pa, pb, qg, qg)


def _merge_bwd(dm, pa, pb, qg, gate_blk):
    t, d = pa.shape
    rb = min(ROW_BLOCK, t)

    def body(dm_ref, pa_ref, pb_ref, ga_ref, gb_ref, dpa_ref, dpb_ref, dg_ref):
        dmv = dm_ref[...].astype(F32)
        sa = jax.nn.sigmoid(ga_ref[...])
        sb = jax.nn.sigmoid(gb_ref[...])
        dpa_ref[...] = (dmv * sa).astype(BF16)
        dpb_ref[...] = (dmv * sb).astype(BF16)
        dg_ref[:, 0:d] = (dmv * pa_ref[...].astype(F32) * (sa * (1.0 - sa))).astype(BF16)
        dg_ref[:, d:2 * d] = (dmv * pb_ref[...].astype(F32) * (sb * (1.0 - sb))).astype(BF16)

    row = pl.BlockSpec((rb, d), lambda i: (i, 0))
    return _pcall(
        body,
        name="merge_bwd",
        out_shape=[jax.ShapeDtypeStruct((t, d), BF16), jax.ShapeDtypeStruct((t, d), BF16),
                   jax.ShapeDtypeStruct((t, 2 * d), BF16)],
        grid=(t // rb,),
        in_specs=[row, row, row, pl.BlockSpec((rb, d), lambda i: (i, gate_blk)), pl.BlockSpec((rb, d), lambda i: (i, gate_blk + 1))],
        out_specs=[row, row, pl.BlockSpec((rb, 2 * d), lambda i: (i, 0))],
        compiler_params=_cparams(("parallel",)),
    )(dm, pa, pb, qg, qg)


def _resid_norm_mod(x, branch, gain, mods, name):
    t, d = x.shape
    rb = min(ROW_BLOCK, t)

    def body(x_ref, b_ref, g_ref, mod_ref, x1_ref, z_ref):
        x1 = x_ref[...] + mod_ref[0:1, :] * b_ref[...]
        x1_ref[...] = x1
        xh, _ = _rms(x1)
        z_ref[...] = ((xh * g_ref[...]) * (1.0 + mod_ref[2:3, :]) + mod_ref[1:2, :]).astype(BF16)

    row = pl.BlockSpec((rb, d), lambda i: (i, 0))
    return _pcall(
        body,
        name=name,
        out_shape=[jax.ShapeDtypeStruct((t, d), F32), jax.ShapeDtypeStruct((t, d), BF16)],
        grid=(t // rb,),
        in_specs=[row, row, pl.BlockSpec((1, d), lambda i: (0, 0)), pl.BlockSpec((8, d), lambda i: (0, 0))],
        out_specs=[row, row],
        compiler_params=_cparams(("parallel",)),
    )(x, branch, gain, mods)


def _norm2_bwd(x1, attn, gain, mods, dz2, dx2):
    t, d = x1.shape
    rb = min(ROW_BLOCK, t)

    def body(x1_ref, at_ref, g_ref, mod_ref, dz_ref, dx2_ref, dx1_ref, da_ref, st_ref):
        @pl.when(pl.program_id(0) == 0)
        def _():
            st_ref[...] = jnp.zeros_like(st_ref)

        xh, r = _rms(x1_ref[...])
        g = g_ref[...]
        dz = dz_ref[...].astype(F32)
        dxn = dz * (1.0 + mod_ref[1:2, :])
        st_ref[0:1, :] += _colsum(dz)
        st_ref[1:2, :] += _colsum(dz * (xh * g))
        st_ref[2:3, :] += _colsum(dxn * xh)
        dx1 = dx2_ref[...] + _rms_bwd(dxn * g, xh, r)
        dx1_ref[...] = dx1
        st_ref[3:4, :] += _colsum(dx1 * at_ref[...])
        da_ref[...] = (dx1 * mod_ref[0:1, :]).astype(BF16)

    row = pl.BlockSpec((rb, d), lambda i: (i, 0))
    return _pcall(
        body,
        name="norm2_mod_bwd",
        out_shape=[jax.ShapeDtypeStruct((t, d), F32), jax.ShapeDtypeStruct((t, d), BF16), jax.ShapeDtypeStruct((8, d), F32)],
        grid=(t // rb,),
        in_specs=[row, row, pl.BlockSpec((1, d), lambda i: (0, 0)), pl.BlockSpec((8, d), lambda i: (0, 0)), row, row],
        out_specs=[row, row, pl.BlockSpec((8, d), lambda i: (0, 0))],
        compiler_params=_cparams(("arbitrary",)),
    )(x1, attn, gain, mods, dz2, dx2)


def _final_loss(x1, ffn, gain, mods, target):
    t, d = x1.shape
    rb = min(ROW_BLOCK, t)
    nb = t // rb

    def body(x1_ref, f_ref, g_ref, mod_ref, tg_ref, dx2_ref, df_ref, st_ref):
        i = pl.program_id(0)

        @pl.when(i == 0)
        def _():
            st_ref[...] = jnp.zeros_like(st_ref)

        ffn_v = f_ref[...]
        g2 = mod_ref[0:1, :]
        x2 = x1_ref[...] + g2 * ffn_v
        xh, r = _rms(x2)
        g = g_ref[...]
        err = xh * g - tg_ref[...]
        st_ref[2:3, :] += _colsum(err * err) * (0.5 / d)
        dy = err * (1.0 / d)
        st_ref[0:1, :] += _colsum(dy * xh)
        dx2 = _rms_bwd(dy * g, xh, r)
        dx2_ref[...] = dx2
        st_ref[1:2, :] += _colsum(dx2 * ffn_v)
        df_ref[...] = (dx2 * g2).astype(BF16)

        @pl.when(i == nb - 1)
        def _():
            st_ref[3:4, :] = jnp.broadcast_to(jnp.sum(st_ref[2:3, :], axis=-1, keepdims=True), (1, d))

    row = pl.BlockSpec((rb, d), lambda i: (i, 0))
    return _pcall(
        body,
        name="final_norm_loss",
        out_shape=[jax.ShapeDtypeStruct((t, d), F32), jax.ShapeDtypeStruct((t, d), BF16), jax.ShapeDtypeStruct((8, d), F32)],
        grid=(nb,),
        in_specs=[row, row, pl.BlockSpec((1, d), lambda i: (0, 0)), pl.BlockSpec((8, d), lambda i: (0, 0)), row],
        out_specs=[row, row, pl.BlockSpec((8, d), lambda i: (0, 0))],
        compiler_params=_cparams(("arbitrary",)),
    )(x1, ffn, gain, mods, target)


def _row_ends(shape):
    rows = lax.broadcasted_iota(jnp.int32, shape, 0)
    return rows == 0, rows == shape[0] - 1


def _shift_dn(v, first):
    return jnp.where(first, 0.0, pltpu.roll(v, 1, 0))


def _shift_up(v, last):
    return jnp.where(last, 0.0, pltpu.roll(v, v.shape[0] - 1, 0))


def _conv_fwd(u, cw, cb):
    t, f2 = u.shape
    f = f2 // 2
    cbk = _tile(f, 256)
    nf = f // cbk

    def body(ua_ref, ub_ref, cwa_ref, cwb_ref, cba_ref, cbb_ref, h_ref, uc_ref):
        first, last = _row_ends((t, cbk))
        outs = []
        for u_ref, cw_ref, cb_ref in ((ua_ref, cwa_ref, cba_ref), (ub_ref, cwb_ref, cbb_ref)):
            uu, cwv = u_ref[...].astype(F32), cw_ref[...]
            outs.append(cb_ref[...] + cwv[0:1, :] * _shift_dn(uu, first) + cwv[1:2, :] * uu
                        + cwv[2:3, :] * _shift_up(uu, last))
        a, b = outs
        uc_ref[0] = a.astype(BF16)
        uc_ref[1] = b.astype(BF16)
        h_ref[...] = (a * jax.nn.sigmoid(a) * b).astype(BF16)

    ca = lambda r: pl.BlockSpec((r, cbk), lambda j: (0, j))
    cbs = lambda r: pl.BlockSpec((r, cbk), lambda j: (0, nf + j))
    return _pcall(
        body,
        name="conv_gate_fwd",
        out_shape=[jax.ShapeDtypeStruct((t, f), BF16), jax.ShapeDtypeStruct((2, t, f), BF16)],
        grid=(nf,),
        in_specs=[ca(t), cbs(t), ca(3), cbs(3), ca(1), cbs(1)],
        out_specs=[ca(t), pl.BlockSpec((2, t, cbk), lambda j: (0, 0, j))],
        compiler_params=_cparams(("parallel",)),
    )(u, u, cw, cw, cb, cb)


def _conv_bwd(u, uc, cw, dh):
    t, f2 = u.shape
    f = f2 // 2
    cbk = _tile(f, 256)
    nf = f // cbk

    def body(ua_ref, ub_ref, uc_ref, cwa_ref, cwb_ref, dh_ref, du_ref, dcw_ref, dcb_ref):
        first, last = _row_ends((t, cbk))
        a, b = uc_ref[0].astype(F32), uc_ref[1].astype(F32)
        dh_v = dh_ref[...].astype(F32)
        sg = jax.nn.sigmoid(a)
        db = dh_v * (a * sg)
        da = dh_v * b * (sg * (1.0 + a * (1.0 - sg)))
        for idx, (dv, u_ref, cw_ref) in enumerate(((da, ua_ref, cwa_ref), (db, ub_ref, cwb_ref))):
            uu, cwv = u_ref[...].astype(F32), cw_ref[...]
            up, dn = _shift_up(dv, last), _shift_dn(dv, first)
            dcb_ref[idx] = _colsum(dv)
            dcw_ref[idx, 0:1, :] = _colsum(up * uu)
            dcw_ref[idx, 1:2, :] = _colsum(dv * uu)
            dcw_ref[idx, 2:3, :] = _colsum(dn * uu)
            du_ref[idx] = (cwv[0:1, :] * up + cwv[1:2, :] * dv + cwv[2:3, :] * dn).astype(BF16)

    ca = lambda r: pl.BlockSpec((r, cbk), lambda j: (0, j))
    cbs = lambda r: pl.BlockSpec((r, cbk), lambda j: (0, nf + j))
    o3 = lambda r: pl.BlockSpec((2, r, cbk), lambda j: (0, 0, j))
    return _pcall(
        body,
        name="conv_gate_bwd",
        out_shape=[jax.ShapeDtypeStruct((2, t, f), BF16), jax.ShapeDtypeStruct((2, 3, f), F32),
                   jax.ShapeDtypeStruct((2, 1, f), F32)],
        grid=(nf,),
        in_specs=[ca(t), cbs(t), o3(t), ca(3), cbs(3), ca(t)],
        out_specs=[o3(t), o3(3), o3(1)],
        compiler_params=_cparams(("parallel",)),
    )(u, u, uc, cw, cw, dh)


def _attention_fwd(q, kk, vv, *, hq, hkv, dk, dv, k_blk0, v_blk0, name):
    t = q.shape[0]
    tk = kk.shape[0]
    g_sz = hq // hkv
    tq = min(ATT_Q_BLOCK_FWD, t)

    def body(q_ref, k_ref, v_ref, o_ref, lse_ref):
        k = k_ref[...]
        v = v_ref[...]
        for j in range(g_sz):
            s = lax.dot_general(q_ref[:, j * dk:(j + 1) * dk], k, _DIMS["nt"], preferred_element_type=F32)
            m = jnp.max(s, axis=-1, keepdims=True)
            p = jnp.exp2(s - m)
            l = jnp.sum(p, axis=-1, keepdims=True)
            o = jnp.dot(p.astype(BF16), v, preferred_element_type=F32) / l
            o_ref[:, j * dv:(j + 1) * dv] = o.astype(BF16)
            lse_ref[0, :, j:j + 1] = m + jnp.log2(l)

    return _pcall(
        body,
        name=name,
        out_shape=[jax.ShapeDtypeStruct((t, hq * dv), BF16), jax.ShapeDtypeStruct((hkv, t, g_sz), F32)],
        grid=(hkv, t // tq),
        in_specs=[
            pl.BlockSpec((tq, g_sz * dk), lambda g, i: (i, g)),
            pl.BlockSpec((tk, dk), lambda g, i: (0, k_blk0 + g)),
            pl.BlockSpec((tk, dv), lambda g, i: (0, v_blk0 + g)),
        ],
        out_specs=[
            pl.BlockSpec((tq, g_sz * dv), lambda g, i: (i, g)),
            pl.BlockSpec((1, tq, g_sz), lambda g, i: (g, i, 0)),
        ],
        compiler_params=_cparams(("parallel", "parallel")),
    )(q, kk, vv)


def _attention_bwd(q, kk, vv, do, lse, *, hq, hkv, dk, dv, k_blk0, v_blk0, name):
    t = q.shape[0]
    tk = kk.shape[0]
    g_sz = hq // hkv
    tq = min(ATT_Q_BLOCK, t)

    def body(q_ref, k_ref, v_ref, do_ref, lse_ref, dq_ref, dk_ref, dv_ref):
        @pl.when(pl.program_id(1) == 0)
        def _():
            dk_ref[...] = jnp.zeros_like(dk_ref)
            dv_ref[...] = jnp.zeros_like(dv_ref)

        k = k_ref[...]
        v = v_ref[...]
        for j in range(g_sz):
            qj = q_ref[:, j * dk:(j + 1) * dk]
            doj = do_ref[:, j * dv:(j + 1) * dv]
            s = lax.dot_general(qj, k, _DIMS["nt"], preferred_element_type=F32)
            p = jnp.exp2(s - lse_ref[0, :, j:j + 1])
            dp = lax.dot_general(doj, v, _DIMS["nt"], preferred_element_type=F32)
            ds = (p * (dp - jnp.sum(p * dp, axis=-1, keepdims=True))).astype(BF16)
            dv_ref[...] += lax.dot_general(p.astype(BF16), doj, _DIMS["tn"], preferred_element_type=F32)
            dk_ref[...] += lax.dot_general(ds, qj, _DIMS["tn"], preferred_element_type=F32)
            dq_ref[:, j * dk:(j + 1) * dk] = jnp.dot(ds, k, preferred_element_type=F32)

        @pl.when(pl.program_id(1) == t // tq - 1)
        def _():
            dk_ref[...] *= LN2

    return _pcall(
        body,
        name=name,
        out_shape=[jax.ShapeDtypeStruct((t, hq * dk), F32), jax.ShapeDtypeStruct((tk, hkv * dk), F32),
                   jax.ShapeDtypeStruct((tk, hkv * dv), F32)],
        grid=(hkv, t // tq),
        in_specs=[
            pl.BlockSpec((tq, g_sz * dk), lambda g, i: (i, g)),
            pl.BlockSpec((tk, dk), lambda g, i: (0, k_blk0 + g)),
            pl.BlockSpec((tk, dv), lambda g, i: (0, v_blk0 + g)),
            pl.BlockSpec((tq, g_sz * dv), lambda g, i: (i, g)),
            pl.BlockSpec((1, tq, g_sz), lambda g, i: (g, i, 0)),
        ],
        out_specs=[
            pl.BlockSpec((tq, g_sz * dk), lambda g, i: (i, g)),
            pl.BlockSpec((tk, dk), lambda g, i: (0, g)),
            pl.BlockSpec((tk, dv), lambda g, i: (0, g)),
        ],
        compiler_params=_cparams(("parallel", "arbitrary")),
    )(q, kk, vv, do, lse)


def _silu(v):
    return v * jax.nn.sigmoid(v)


def _ada_fwd(conds, w_ada, b_ada_shard):
    r, d = conds.shape
    n = w_ada.shape[1]
    tn = _tile(n, 512)

    def body(c_ref, w_ref, b_ref, o_ref):
        s = _silu(c_ref[...]).astype(BF16)
        o_ref[...] = jnp.dot(s, w_ref[...].astype(BF16), preferred_element_type=F32) + b_ref[...]

    return _pcall(
        body,
        name="ada_fwd",
        out_shape=jax.ShapeDtypeStruct((r, n), F32),
        grid=(n // tn,),
        in_specs=[pl.BlockSpec((r, d), lambda j: (0, 0)), pl.BlockSpec((d, tn), lambda j: (0, j)),
                  pl.BlockSpec((1, tn), lambda j: (0, j))],
        out_specs=pl.BlockSpec((r, tn), lambda j: (0, j)),
        compiler_params=_cparams(("parallel",)),
    )(conds, w_ada, b_ada_shard)


def _cctx_partial(da16_shard, w_ada, c_ctx_row):
    d, n = w_ada.shape
    td = _tile(d, 512)

    def body(g_ref, w_ref, c_ref, o_ref):
        ds = lax.dot_general(g_ref[8:16, :].astype(BF16), w_ref[...].astype(BF16), _DIMS["nt"],
                             preferred_element_type=F32)
        cv = c_ref[...]
        sg = jax.nn.sigmoid(cv)
        o_ref[...] = ds * (sg * (1.0 + cv * (1.0 - sg)))

    return _pcall(
        body,
        name="cctx_partial",
        out_shape=jax.ShapeDtypeStruct((8, d), F32),
        grid=(d // td,),
        in_specs=[pl.BlockSpec((16, n), lambda j: (0, 0)), pl.BlockSpec((td, n), lambda j: (j, 0)),
                  pl.BlockSpec((1, td), lambda j: (0, j))],
        out_specs=pl.BlockSpec((8, td), lambda j: (0, j)),
        compiler_params=_cparams(("parallel",)),
    )(da16_shard, w_ada, c_ctx_row)


def _sum_parts(parts):
    p, _, n = parts.shape

    def body(p_ref, o_ref):
        acc = p_ref[0]
        for s in range(1, p):
            acc = acc + p_ref[s]
        o_ref[...] = acc

    return _pcall(
        body,
        name="sum_parts",
        out_shape=jax.ShapeDtypeStruct((1, n), F32),
        in_specs=[pl.BlockSpec(memory_space=pltpu.VMEM)],
        out_specs=pl.BlockSpec(memory_space=pltpu.VMEM),
    )(parts)


def _adam_math(w, g, m, v):
    m2 = ADAM_B1 * m + (1.0 - ADAM_B1) * g
    v2 = ADAM_B2 * v + (1.0 - ADAM_B2) * jnp.square(g)
    m_hat = m2 / (1.0 - ADAM_B1 ** ADAM_STEP)
    v_hat = v2 / (1.0 - ADAM_B2 ** ADAM_STEP)
    delta = -ADAM_LR * (m_hat / (jnp.sqrt(v_hat) + ADAM_EPS) + ADAM_WD * w)
    return delta, m2, v2


def _adamw(parts, w, m, v, name):
    p, r, c = parts.shape
    rb = _tile(r, max(8, (1 << 20) // (4 * c) // 8 * 8), 8)

    def body(p_ref, w_ref, m_ref, v_ref, g_ref, d_ref, m2_ref, v2_ref):
        g = p_ref[0].astype(F32)
        for s in range(1, p):
            g = g + p_ref[s].astype(F32)
        g_ref[...] = g
        d_ref[...], m2_ref[...], v2_ref[...] = _adam_math(w_ref[...], g, m_ref[...], v_ref[...])

    if w.ndim == 3:
        row = pl.BlockSpec((None, rb, c), lambda i: (0, i, 0))
    else:
        row = pl.BlockSpec((rb, c), lambda i: (i, 0))
    return _pcall(
        body,
        name=name,
        out_shape=[jax.ShapeDtypeStruct(w.shape, F32)] * 4,
        grid=(r // rb,),
        in_specs=[pl.BlockSpec((p, rb, c), lambda i: (0, i, 0)), row, row, row],
        out_specs=[row] * 4,
        compiler_params=_cparams(("parallel",)),
    )(parts, w, m, v)


def _adamw_ada(conds, da16, w, m, v):
    d, n = w.shape
    rb = _tile(d, 256, LANE)

    def body(s_ref, da_ref, w_ref, m_ref, v_ref, g_ref, d_ref, m2_ref, v2_ref):
        g = lax.dot_general(_silu(s_ref[...]).astype(BF16), da_ref[...].astype(BF16), _DIMS["tn"],
                            preferred_element_type=F32)
        g_ref[...] = g
        d_ref[...], m2_ref[...], v2_ref[...] = _adam_math(w_ref[...], g, m_ref[...], v_ref[...])

    row = pl.BlockSpec((rb, n), lambda i: (i, 0))
    return _pcall(
        body,
        name="adamw_w_ada",
        out_shape=[jax.ShapeDtypeStruct((d, n), F32)] * 4,
        grid=(d // rb,),
        in_specs=[pl.BlockSpec((16, rb), lambda i: (0, i)), pl.BlockSpec((16, n), lambda i: (0, 0)), row, row, row],
        out_specs=[row] * 4,
        compiler_params=_cparams(("parallel",)),
    )(conds, da16, w, m, v)


def _cast_bf16(a, name):
    _, r, c = a.shape
    rb = _tile(r, 512, 8)

    def body(a_ref, o_ref):
        o_ref[...] = a_ref[...].astype(BF16)

    return _pcall(body, name=name, out_shape=jax.ShapeDtypeStruct((r, c), BF16), grid=(r // rb,),
                  in_specs=[pl.BlockSpec((None, rb, c), lambda i: (0, i, 0))],
                  out_specs=pl.BlockSpec((rb, c), lambda i: (i, 0)), compiler_params=_cparams(("parallel",)))(a)


def _rope_tabs(t, rot):
    half, q = rot // 2, rot // 4
    n_rows = t // GRID_W
    row = jnp.repeat(jnp.arange(n_rows, dtype=F32), GRID_W)
    col = jnp.tile(jnp.arange(GRID_W, dtype=F32), n_rows)
    inv_freq = ROPE_THETA ** (-jnp.arange(0, half, 2, dtype=F32) / half)
    ang = jnp.concatenate([row[:, None] * inv_freq, col[:, None] * inv_freq], axis=-1)
    cos, sin = jnp.cos(ang), jnp.sin(ang)
    c0, c1, s0, s1 = cos[:, :q], cos[:, q:], sin[:, :q], sin[:, q:]
    z = jnp.zeros_like(s0)
    return (jnp.concatenate([c0, c0, c1, c1], -1), jnp.concatenate([-s0, z, -s1, z], -1),
            jnp.concatenate([z, s0, z, s1], -1))


def _pad_cols(a, left, total, fill=0.0):
    return jnp.pad(a, ((0, 0), (left, total - left - a.shape[1])), constant_values=fill)


def _with_ctx_rows(tab, tc, fill):
    return jnp.concatenate([tab, jnp.full((tc, tab.shape[1]), fill, F32)], axis=0)


def kernel(x, c, ctx, c_ctx, w_ada, b_ada, norm1_g, w_in, mla_q_norm_g, w_q_up, mla_kv_norm_g, w_kv_up, gqa_q_norm_g, gqa_k_norm_g, w_br_a, w_br_b, w_out, norm2_g, w_up, conv_w, conv_b, w_down, final_norm_g, loss_target, m_c_ctx, m_w_ada, m_b_ada, m_norm1_g, m_w_in, m_mla_q_norm_g, m_w_q_up, m_mla_kv_norm_g, m_w_kv_up, m_gqa_q_norm_g, m_gqa_k_norm_g, m_w_br_a, m_w_br_b, m_w_out, m_norm2_g, m_w_up, m_conv_w, m_conv_b, m_w_down, m_final_norm_g, v_c_ctx, v_w_ada, v_b_ada, v_norm1_g, v_w_in, v_mla_q_norm_g, v_w_q_up, v_mla_kv_norm_g, v_w_kv_up, v_gqa_q_norm_g, v_gqa_k_norm_g, v_w_br_a, v_w_br_b, v_w_out, v_norm2_g, v_w_up, v_conv_w, v_conv_b, v_w_down, v_final_norm_g):
    weights = dict(c_ctx=c_ctx, w_ada=w_ada, b_ada=b_ada, norm1_g=norm1_g, w_in=w_in, mla_q_norm_g=mla_q_norm_g,
                   w_q_up=w_q_up, mla_kv_norm_g=mla_kv_norm_g, w_kv_up=w_kv_up, gqa_q_norm_g=gqa_q_norm_g,
                   gqa_k_norm_g=gqa_k_norm_g, w_br_a=w_br_a, w_br_b=w_br_b, w_out=w_out, norm2_g=norm2_g, w_up=w_up,
                   conv_w=conv_w, conv_b=conv_b, w_down=w_down, final_norm_g=final_norm_g)
    mom_m = dict(c_ctx=m_c_ctx, w_ada=m_w_ada, b_ada=m_b_ada, norm1_g=m_norm1_g, w_in=m_w_in, mla_q_norm_g=m_mla_q_norm_g,
                 w_q_up=m_w_q_up, mla_kv_norm_g=m_mla_kv_norm_g, w_kv_up=m_w_kv_up, gqa_q_norm_g=m_gqa_q_norm_g,
                 gqa_k_norm_g=m_gqa_k_norm_g, w_br_a=m_w_br_a, w_br_b=m_w_br_b, w_out=m_w_out, norm2_g=m_norm2_g,
                 w_up=m_w_up, conv_w=m_conv_w, conv_b=m_conv_b, w_down=m_w_down, final_norm_g=m_final_norm_g)
    mom_v = dict(c_ctx=v_c_ctx, w_ada=v_w_ada, b_ada=v_b_ada, norm1_g=v_norm1_g, w_in=v_w_in, mla_q_norm_g=v_mla_q_norm_g,
                 w_q_up=v_w_q_up, mla_kv_norm_g=v_mla_kv_norm_g, w_kv_up=v_w_kv_up, gqa_q_norm_g=v_gqa_q_norm_g,
                 gqa_k_norm_g=v_gqa_k_norm_g, w_br_a=v_w_br_a, w_br_b=v_w_br_b, w_out=v_w_out, norm2_g=v_norm2_g,
                 w_up=v_w_up, conv_w=v_conv_w, conv_b=v_conv_b, w_down=v_w_down, final_norm_g=v_final_norm_g)
    order = list(weights)

    my_idx = 4 * lax.axis_index("x") + 2 * lax.axis_index("y") + lax.axis_index("c")
    xs, cts, tgt = x[0], ctx[0], loss_target[0]
    t, d = xs.shape
    tc = cts.shape[0]
    ta = t + tc
    kvl, ql = MLA_KV_LORA, MLA_Q_LORA
    nb = GQA_KV_HEADS * GQA_HEAD_DIM
    hb = GQA_HEADS * GQA_HEAD_DIM
    ha = MLA_HEADS
    f2 = w_up.shape[2] * N_DEV
    ff = f2 // 2

    big = ["w_in", "w_q_up", "w_kv_up", "w_br_a", "w_br_b", "w_out", "w_up", "w_down"]
    nw = len(big)
    del nw
    _ORDER_AFTER.clear()
    shards = {n: _cast_bf16(weights[n], "cast_" + n) for n in big}
    c_idx = jnp.reshape(lax.axis_index("c"), (1,)).astype(jnp.int32)

    def gather_start(names, dep):
        shs = [shards[n] for n in names]
        land = [lax.empty((N_DEV,) + s.shape, BF16) for s in shs]
        if dep is not None:
            _after(dep)
        s, r, arrs, tok = _split_start("gather_ici_start_" + names[0], shs + land, _gather_ici_copies(len(names)),
                                       5 * len(names))
        return dict(names=names, s=s, r=r, arrs=arrs, tok=tok)

    def gather_relay(g, after):
        n = len(g["names"])
        arrs = _split_wait("gather_ici_wait_" + g["names"][0], g["s"], g["r"], g["arrs"], _gather_ici_copies(n), after)
        s, r, bufs, tok = _split_start("gather_d2d_start_" + g["names"][0], arrs[n:], _gather_d2d_copies(n), 3 * n)
        g.update(s2=s, r2=r, bufs=bufs)
        return tok

    def gather_finish(g, after):
        n = len(g["names"])
        bufs = _split_wait("gather_d2d_wait_" + g["names"][0], g["s2"], g["r2"], g["bufs"], _gather_d2d_copies(n), after)
        return dict(zip(g["names"], bufs))

    c_all, cw_all = _all_gather([jnp.pad(c, ((0, 7), (0, 0))), jnp.pad(conv_w[0], ((0, 5), (0, 0)))], "gather_cond")
    conv_w_f = jnp.transpose(cw_all[:, :3, :], (1, 0, 2)).reshape(3, f2)
    conds = jnp.concatenate([c_all[:, 0, :], c_ctx[None, :], jnp.zeros((7, d), F32)], axis=0)
    ncol = w_ada.shape[2]
    b_shard = lax.dynamic_slice_in_dim(b_ada, my_idx * ncol, ncol, axis=1)
    ada_shard = _ada_fwd(conds, w_ada[0], b_shard)
    (ada_all,) = _all_gather([ada_shard], "gather_ada")
    ada = jnp.transpose(ada_all, (1, 0, 2)).reshape(16, N_DEV * ncol)
    lat = lax.dynamic_slice_in_dim(ada, my_idx, 1, axis=0).reshape(6, d)
    cxt = ada[8].reshape(6, d)
    zero2 = jnp.zeros((2, d), F32)
    mods1 = jnp.concatenate([lat[0:2], cxt[0:2], jnp.zeros((4, d), F32)], axis=0)
    mods2 = jnp.concatenate([lat[2:3], lat[3:4], lat[4:5], jnp.zeros((5, d), F32)], axis=0)
    mods2b = jnp.concatenate([lat[2:3], lat[4:5], jnp.zeros((6, d), F32)], axis=0)
    mods3 = jnp.concatenate([lat[5:6], jnp.zeros((7, d), F32)], axis=0)
    del zero2

    g0 = gather_start(["w_in"], ada_all)
    g1 = gather_start(["w_q_up", "w_kv_up", "w_br_a", "w_br_b", "w_out"], g0["tok"])
    g2 = gather_start(["w_up"], g1["tok"])
    g3 = gather_start(["w_down"], g2["tok"])

    ca, s1a, s2a = _rope_tabs(t, MLA_ROPE)
    cb_, s1b, s2b = _rope_tabs(t, GQA_HEAD_DIM)
    q_tabs_a = (_pad_cols(jnp.concatenate([jnp.ones((t, MLA_NOPE), F32), ca], 1), 0, MLA_SLOT),
                _pad_cols(s1a, MLA_NOPE, MLA_SLOT), _pad_cols(s2a, MLA_NOPE, MLA_SLOT))
    q_tabs_b = (cb_, s1b, s2b)
    k_tabs = (_with_ctx_rows(_pad_cols(ca, 0, LANE), tc, 1.0), _with_ctx_rows(_pad_cols(s1a, 0, LANE), tc, 0.0),
              _with_ctx_rows(_pad_cols(s2a, 0, LANE), tc, 0.0),
              _with_ctx_rows(cb_, tc, 1.0), _with_ctx_rows(s1b, tc, 0.0), _with_ctx_rows(s2b, tc, 0.0))

    def cols_full(g):
        return jnp.transpose(g, (1, 0, 2)).reshape(g.shape[1], N_DEV * g.shape[2])

    _after(gather_relay(g0, mods1))
    z_all = _norm_mod_fwd(cts, xs, norm1_g, mods1)
    gathered = gather_finish(g0, z_all)
    w_in_f = cols_full(gathered["w_in"])
    o_kpe, o_kb, o_vb = kvl, kvl + MLA_ROPE, kvl + MLA_ROPE + nb
    o_q = o_vb + nb
    o_g = o_q + ql + hb
    wkv_w = kvl + 2 * nb + LANE
    w_kv_p = jnp.concatenate([w_in_f[:, :kvl], w_in_f[:, o_kb:o_q], w_in_f[:, o_kpe:o_kb],
                              jnp.zeros((d, LANE - MLA_ROPE), BF16)], axis=1)
    q_w = ql + hb
    q_pad = (-q_w) % 512 if d >= 512 else (-q_w) % d
    gate_blk = (q_w + q_pad) // d
    assert (q_w + q_pad) % d == 0
    w_qg_p = jnp.concatenate([w_in_f[:, o_q:o_g], jnp.zeros((d, q_pad), BF16), w_in_f[:, o_g:]], axis=1)

    kv_all = _mm(z_all, w_kv_p, "nn", F32, "proj_kv", tm=1152, tn=wkv_w)
    qg = _mm(z_all, w_qg_p, "nn", F32, "proj_qg", tm=1024, tn=1024, rows=t)
    _after(gather_relay(g1, qg))
    kin, k_b, v_b = _key_prep_fwd(kv_all, mla_kv_norm_g, gqa_k_norm_g, k_tabs)
    sc_a = float((MLA_NOPE + MLA_ROPE) ** -0.5) * LOG2E
    sc_b = float(GQA_HEAD_DIM ** -0.5) * LOG2E
    cqn, q_b = _q_prep_fwd(qg, mla_q_norm_g, gqa_q_norm_g, q_tabs_b, sc_b)
    gathered.update(gather_finish(g1, q_b))

    wq_f = cols_full(gathered["w_q_up"]).reshape(ql, ha, MLA_NOPE + MLA_ROPE)
    wq_ext = jnp.pad(wq_f, ((0, 0), (0, 0), (0, MLA_SLOT - MLA_NOPE - MLA_ROPE))).reshape(ql, ha * MLA_SLOT)
    wkv_f = cols_full(gathered["w_kv_up"]).reshape(kvl, ha, MLA_NOPE + MLA_V)
    wk_slots = jnp.pad(wkv_f[:, :, :MLA_NOPE], ((0, 0), (0, 0), (0, MLA_SLOT - MLA_NOPE))).reshape(kvl, ha * MLA_SLOT)
    wv_cols = wkv_f[:, :, MLA_NOPE:].reshape(kvl, ha * MLA_V)
    e_slot = jnp.pad(jnp.eye(MLA_ROPE, dtype=BF16),
                     ((0, LANE - MLA_ROPE), (MLA_NOPE, MLA_SLOT - MLA_NOPE - MLA_ROPE)))
    e_rows = jnp.concatenate([jnp.tile(e_slot, (1, ha)), jnp.zeros((LANE, ha * MLA_V), BF16)], axis=1)
    wkv_ext = jnp.concatenate([jnp.concatenate([wk_slots, wv_cols], axis=1), e_rows], axis=0)
    w_bra = cols_full(gathered["w_br_a"])
    w_brb = cols_full(gathered["w_br_b"])
    w_out_f = gathered["w_out"].reshape(d, d)

    kv_a = _mm(kin, wkv_ext, "nn", BF16, "kv_up", tm=1152, tn=1024)
    qa_raw = _mm(cqn, wq_ext, "nn", F32, "q_up", tm=1024, tn=1024)
    q_a = _rope_a(qa_raw, q_tabs_a, False, BF16, "rope_q_fwd", sc_a)
    att_a = dict(hq=ha, hkv=ha, dk=MLA_SLOT, dv=MLA_V, k_blk0=0, v_blk0=ha * MLA_SLOT // MLA_V)
    att_b = dict(hq=GQA_HEADS, hkv=GQA_KV_HEADS, dk=GQA_HEAD_DIM, dv=GQA_HEAD_DIM, k_blk0=0, v_blk0=0)
    o_a, lse_a = _attention_fwd(q_a, kv_a, kv_a, name="attn_a_fwd", **att_a)
    o_b, lse_b = _attention_fwd(q_b, k_b, v_b, name="attn_b_fwd", **att_b)
    _after(gather_relay(g2, o_b))
    pa = _mm(o_a, w_bra, "nn", BF16, "br_a", tm=1024, tn=1024)
    pb = _mm(o_b, w_brb, "nn", BF16, "br_b", tm=1024, tn=1024)
    merged = _merge_fwd(pa, pb, qg, gate_blk)
    attn = _mm(merged, w_out_f, "nn", F32, "w_out", tm=1024, tn=1024)
    x1, z2 = _resid_norm_mod(xs, attn, norm2_g, mods2, "resid_norm2_fwd")
    w_up3 = gather_finish(g2, z2)["w_up"]
    _after(gather_relay(g3, z2))
    u = _mm_up_fwd(z2, w_up3, "w_up")
    w_down_f = gather_finish(g3, u)["w_down"].reshape(ff, d)
    h, uc = _conv_fwd(u, conv_w_f, conv_b)
    ffn = _mm(h, w_down_f, "nn", F32, "w_down", tm=1024, tn=1024, tk=2816)

    def to_shards(g):
        return jnp.transpose(g.reshape(g.shape[0], N_DEV, g.shape[1] // N_DEV), (1, 0, 2))

    def reduce_start(tag, names, sends):
        n = len(sends)
        land = [lax.empty((4,) + s.shape[1:], s.dtype) for s in sends]
        s, r, arrs, tok = _split_start("reduce_d2d_start_" + tag, sends + land, _reduce_d2d_copies(n), 4 * n)
        return dict(tag=tag, names=names, s=s, r=r, arrs=arrs, tok=tok)

    def reduce_relay(g, after):
        n = len(g["names"])
        arrs = _split_wait("reduce_d2d_wait_" + g["tag"], g["s"], g["r"], g["arrs"], _reduce_d2d_copies(n), after)
        sums = [_pair_sum(arrs[a], arrs[n + a], c_idx, "pair_sum_" + g["names"][a]) for a in range(n)]
        land = [lax.empty(s.shape, s.dtype) for s in sums]
        s, r, arrs2, tok = _split_start("reduce_ici_start_" + g["tag"], sums + land, _reduce_ici_copies(n), 4 * n)
        g.update(s2=s, r2=r, arrs2=arrs2)
        return tok

    def reduce_finish(g, after):
        n = len(g["names"])
        arrs2 = _split_wait("reduce_ici_wait_" + g["tag"], g["s2"], g["r2"], g["arrs2"], _reduce_ici_copies(n), after)
        return dict(zip(g["names"], arrs2[n:]))

    dx2, dffn, st_fin = _final_loss(x1, ffn, final_norm_g[None, :], mods3, tgt)
    loss = lax.psum(st_fin[3, 0], MESH_AXES)
    dh = _mm(dffn, w_down_f, "nt", BF16, "d_h", tm=1024, tn=1024)
    g_w_down = _mm(h, dffn, "tn", BF16, "g_w_down", tm=512, tn=1024)
    r_down = reduce_start("down", ["w_down"], [g_w_down.reshape(N_DEV, ff // N_DEV, d)])
    _after(r_down["tok"])
    du3, dcw, dcb = _conv_bwd(u, uc, conv_w_f, dh)
    dz2 = _mm_up_dz(du3, w_up3, "d_z2")
    g_w_up = _mm_up_gw(z2, du3, N_DEV, "g_w_up")
    g_conv_w = jnp.concatenate([dcw[0], dcw[1]], axis=1)
    tok = reduce_relay(r_down, g_w_up)
    _after(tok)
    r_up = reduce_start("up", ["w_up", "conv_w"], [g_w_up, to_shards(jnp.pad(g_conv_w, ((0, 5), (0, 0))))])
    _after(tok, r_up["tok"])
    dx1, dattn, st_n2 = _norm2_bwd(x1, attn, norm2_g, mods2b, dz2, dx2)
    dmerged = _mm(dattn, w_out_f, "nt", BF16, "d_merged", tm=1024, tn=1024)
    g_w_out = _mm(merged, dattn, "tn", BF16, "g_w_out", tm=1024, tn=1024)
    dpa, dpb, dgates = _merge_bwd(dmerged, pa, pb, qg, gate_blk)
    do_a = _mm(dpa, w_bra, "nt", BF16, "d_o_a", tm=1024, tn=1024)
    do_b = _mm(dpb, w_brb, "nt", BF16, "d_o_b", tm=1024, tn=1024)
    g_w_bra = _mm(o_a, dpa, "tn", BF16, "g_w_br_a", tm=1024, tn=1024)
    g_w_brb = _mm(o_b, dpb, "tn", BF16, "g_w_br_b", tm=1024, tn=1024)
    tok = reduce_relay(r_up, g_w_brb)
    _after(tok)
    r_out = reduce_start("out", ["w_out", "w_br_a", "w_br_b"],
                         [g_w_out.reshape(N_DEV, d // N_DEV, d), to_shards(g_w_bra), to_shards(g_w_brb)])
    _after(tok, r_out["tok"])
    dq_a, dk_a, dv_a = _attention_bwd(q_a, kv_a, kv_a, do_a, lse_a, name="attn_a_bwd", **att_a)
    dq_b, dk_b, dv_b = _attention_bwd(q_b, k_b, v_b, do_b, lse_b, name="attn_b_bwd", **att_b)
    _after(reduce_relay(r_out, dv_b))
    dqa_raw = _rope_a(dq_a, q_tabs_a, True, BF16, "rope_q_bwd", sc_a * LN2)
    dcqn = _mm(dqa_raw, wq_ext, "nt", F32, "d_cqn", tm=1024, tn=ql)
    g_wq_ext = _mm(cqn, dqa_raw, "tn", BF16, "g_w_q_up", tm=ql, tn=1024)
    dq_p, st_q, st_qb = _q_prep_bwd(qg, mla_q_norm_g, gqa_q_norm_g, q_tabs_b, dcqn, dq_b, q_pad, sc_b * LN2)
    dkin = _mm_cat_nt([(dk_a, wkv_ext, 0), (dv_a, wkv_ext, ha * MLA_SLOT)], F32, "d_kin", tm=1152, tn=kvl + LANE)
    g_wkv_ext = _mm_cat_tn(kin, [dk_a, dv_a], BF16, "g_w_kv_up", tm=kvl + LANE, tn=min(1024, ha * MLA_V))
    dkv_p, st_kv, st_kb = _key_prep_bwd(kv_all, mla_kv_norm_g, gqa_k_norm_g, k_tabs, dkin, dk_b, dv_b)
    g_wq = g_wq_ext.reshape(ql, ha, MLA_SLOT)[:, :, :MLA_NOPE + MLA_ROPE].reshape(ql, ha * (MLA_NOPE + MLA_ROPE))
    g_wkv = jnp.concatenate([g_wkv_ext[:kvl, :ha * MLA_SLOT].reshape(kvl, ha, MLA_SLOT)[:, :, :MLA_NOPE],
                             g_wkv_ext[:kvl, ha * MLA_SLOT:].reshape(kvl, ha, MLA_V)], axis=2).reshape(kvl, ha * (MLA_NOPE + MLA_V))
    r_qkv = reduce_start("qkv", ["w_q_up", "w_kv_up"], [to_shards(g_wq), to_shards(g_wkv)])
    _after(r_qkv["tok"])
    g_wkv_p = _mm(z_all, dkv_p, "tn", BF16, "g_w_in_kv", tm=1024, tn=wkv_w)
    g_wqg_p = _mm_cat_tn(z_all, [dq_p, dgates], BF16, "g_w_in_qg", tm=1024, tn=min(1024, d), rows=t)
    g_w_in = jnp.concatenate([g_wkv_p[:, :kvl], g_wkv_p[:, kvl + 2 * nb:kvl + 2 * nb + MLA_ROPE],
                              g_wkv_p[:, kvl:kvl + 2 * nb], g_wqg_p[:, :q_w], g_wqg_p[:, q_w + q_pad:]], axis=1)
    r_in = reduce_start("in", ["w_in"], [to_shards(g_w_in)])
    _after(r_in["tok"])
    qw_p = q_w + q_pad
    dz_lat = _mm_sum_nt([(dq_p, 0, w_qg_p, 0, qw_p), (dgates, 0, w_qg_p, qw_p, d), (dgates, d, w_qg_p, qw_p + d, d),
                         (dkv_p, 0, w_kv_p, 0, wkv_w)], F32, "d_z_lat", rows=t)
    dz_ctx = _mm(dkv_p, w_kv_p, "nt", F32, "d_z_ctx", tm=min(ROW_BLOCK, tc), tn=1024, a_row_off=t)
    tok_q = reduce_relay(r_qkv, dz_ctx)
    _after(tok_q)
    grad_x, st_n1 = _norm1_bwd(cts, xs, norm1_g, mods1, dz_ctx, dz_lat, dx1)

    res = {}

    def upd(nm, parts):
        wv, mv, vv = weights[nm], mom_m[nm], mom_v[nm]
        if wv.ndim == 1:
            wv, mv, vv = (a.reshape(1, -1) for a in (wv, mv, vv))
        outs = _adamw(parts, wv, mv, vv, "adamw_" + nm)
        res[nm] = [o_.reshape(weights[nm].shape) for o_ in outs]

    d_lat = jnp.concatenate([st_n1[0], st_n1[1], st_n2[3], st_n2[0], st_n2[1], st_fin[1]])
    d_cxt = jnp.concatenate([st_n1[3], st_n1[4], jnp.zeros((4 * d,), F32)])
    small = jnp.concatenate([d_lat, d_cxt, st_n1[2], st_q[0], st_kv[0], st_qb[0], st_kb[0], st_n2[2],
                             jnp.concatenate([dcb[0, 0], dcb[1, 0]]), st_fin[0]])
    n_small = small.shape[0]
    pad_small = (-n_small) % LANE
    (small_all,) = _all_gather([jnp.pad(small, (0, pad_small)).reshape(1, -1)], "gather_small")
    offs = {}
    o = 0
    for nm, ln in (("d_lat", 6 * d), ("d_cxt", 6 * d), ("norm1_g", d), ("mla_q_norm_g", ql), ("mla_kv_norm_g", kvl),
                   ("gqa_q_norm_g", GQA_HEAD_DIM), ("gqa_k_norm_g", GQA_HEAD_DIM), ("norm2_g", d), ("conv_b", f2),
                   ("final_norm_g", d)):
        offs[nm] = (o, ln)
        o += ln

    def part(nm):
        a, ln = offs[nm]
        return small_all[:, :, a:a + ln]

    d_lat_all = part("d_lat")[:, 0, :]
    d_cxt_sum = _sum_parts(part("d_cxt"))
    da16 = jnp.concatenate([d_lat_all, d_cxt_sum, jnp.zeros((7, 6 * d), F32)], axis=0)
    da16_shard = lax.dynamic_slice_in_dim(da16, my_idx * ncol, ncol, axis=1)
    cc_part = _cctx_partial(da16_shard, w_ada[0], c_ctx[None, :])
    (cc_all,) = _all_gather([cc_part], "gather_cctx")
    cc_parts = cc_all[:, 0:1, :]
    tok_i = reduce_relay(r_in, cc_all)

    _after(tok_i)
    for nm in ("norm1_g", "mla_q_norm_g", "mla_kv_norm_g", "gqa_q_norm_g", "gqa_k_norm_g", "norm2_g", "conv_b",
               "final_norm_g"):
        upd(nm, part(nm))
    upd("c_ctx", cc_parts)
    b_parts = jnp.concatenate([d_lat_all[:, None, :], d_cxt_sum[None]], axis=0)
    upd("b_ada", b_parts)
    _after(tok_i)
    outs = _adamw_ada(conds, da16_shard, w_ada[0], m_w_ada[0], v_w_ada[0])
    res["w_ada"] = [o_[None] for o_ in outs]
    last = outs[0]
    for grp in (r_down, r_up, r_out, r_qkv, r_in):
        recv = reduce_finish(grp, last)
        for nm in grp["names"]:
            upd(nm, recv[nm][:, :3, :] if nm == "conv_w" else recv[nm])
            last = res[nm][0]

    return (loss, grad_x[None], *[res[n][0] for n in order], *[res[n][1] for n in order],
            *[res[n][2] for n in order], *[res[n][3] for n in order])
```

```python
import functools

import jax
import jax.numpy as jnp
from jax import lax
from jax.experimental import pallas as pl
from jax.experimental.pallas import tpu as pltpu

F32 = jnp.float32
BF16 = jnp.bfloat16

GRID_W = 64
ROPE_THETA = 10000.0
NORM_EPS = 1e-6
MLA_HEADS = 8
MLA_Q_LORA = 768
MLA_KV_LORA = 512
MLA_NOPE = 128
MLA_ROPE = 64
MLA_V = 128
GQA_HEADS = 8
GQA_KV_HEADS = 2
GQA_HEAD_DIM = 128
ADAM_LR = 0.001
ADAM_B1 = 0.9
ADAM_B2 = 0.999
ADAM_EPS = 1e-08
ADAM_WD = 0.01
ADAM_STEP = 10

N_DEV = 8
MESH_AXES = ("x", "y", "c")
LANE = 128
MLA_SLOT = 2 * LANE
VMEM_LIMIT = 56 * 1024 * 1024
ROW_BLOCK = 256
ATT_Q_BLOCK = 512
ATT_Q_BLOCK_FWD = 512
LN2 = 0.6931471805599453
LOG2E = 1.4426950408889634
MESH_ID = pl.DeviceIdType.MESH


def _tile(n, pref, align=LANE):
    if n <= pref:
        return n
    best = None
    t = align
    while t <= pref:
        if n % t == 0:
            best = t
        t += align
    assert best is not None, (n, pref, align)
    return best


def _cparams(sem=None):
    return pltpu.CompilerParams(dimension_semantics=sem, vmem_limit_bytes=VMEM_LIMIT)


_ORDER_AFTER = []


def _after(*arrays):
    _ORDER_AFTER.extend(arrays)


def _pcall(body, *, in_specs, **kw):
    deps = tuple(_ORDER_AFTER)
    _ORDER_AFTER.clear()
    if not deps:
        return pl.pallas_call(body, in_specs=in_specs, **kw)
    n_in, n_dep = len(in_specs), len(deps)

    def with_deps(*refs):
        body(*refs[:n_in], *refs[n_in + n_dep:])

    call = pl.pallas_call(with_deps, in_specs=list(in_specs) + [pl.BlockSpec(memory_space=pl.ANY)] * n_dep, **kw)
    return lambda *args: call(*args, *deps)


def _all_gather(arrs, name):
    n = len(arrs)

    def body(*refs):
        ins = refs[:n]
        outs = refs[n:2 * n]
        send_sems, recv_sems, local_sems = refs[2 * n:]
        x, y, c = lax.axis_index("x"), lax.axis_index("y"), lax.axis_index("c")
        me, sibling = (x, y, c), (x, y, 1 - c)
        chips = [(1 - x, y), (x, 1 - y), (1 - x, 1 - y)]

        def rows(a, dev):
            px, py, pc = dev
            return outs[a].at[4 * px + 2 * py + pc]

        def copy(a, k, block, to, src=None):
            return pltpu.make_async_remote_copy(
                src_ref=rows(a, block) if src is None else src,
                dst_ref=rows(a, block),
                send_sem=send_sems.at[7 * a + k],
                recv_sem=recv_sems.at[7 * a + k],
                device_id=to,
                device_id_type=MESH_ID,
            )

        mine = [pltpu.make_async_copy(ins[a], rows(a, me), local_sems.at[a]) for a in range(n)]
        for cp in mine:
            cp.start()
        first = []
        for a in range(n):
            first.append(copy(a, 0, me, sibling, src=ins[a]))
            first += [copy(a, 1 + j, me, (*chip, c), src=ins[a]) for j, chip in enumerate(chips)]
        for cp in first:
            cp.start()
        passed = []
        for j, chip in enumerate(chips):
            for a in range(n):
                copy(a, 1 + j, (*chip, c), me).wait_recv()
                fwd = copy(a, 4 + j, (*chip, c), sibling)
                fwd.start()
                passed.append(fwd)
        for a in range(n):
            copy(a, 0, sibling, me).wait_recv()
            for j, chip in enumerate(chips):
                copy(a, 4 + j, (*chip, 1 - c), me).wait_recv()
        for cp in first + passed:
            cp.wait_send()
        for cp in mine:
            cp.wait()

    any_spec = pl.BlockSpec(memory_space=pl.ANY)
    outs = _pcall(
        body,
        name=name,
        out_shape=[jax.ShapeDtypeStruct((N_DEV,) + a.shape, a.dtype) for a in arrs],
        in_specs=[any_spec] * n,
        out_specs=[any_spec] * n,
        scratch_shapes=[
            pltpu.SemaphoreType.DMA((7 * n,)),
            pltpu.SemaphoreType.DMA((7 * n,)),
            pltpu.SemaphoreType.DMA((n,)),
        ],
    )(*arrs)
    return list(outs)


def _all_to_all(arrs, name):
    n = len(arrs)

    def body(*refs):
        ins = refs[:n]
        outs = refs[n:2 * n]
        send_sems, recv_sems, local_sems = refs[2 * n:]
        x, y, c = lax.axis_index("x"), lax.axis_index("y"), lax.axis_index("c")
        my_idx = 4 * x + 2 * y + c

        def peer(k):
            fx, fy, fc = (k >> 2) & 1, (k >> 1) & 1, k & 1
            return (x ^ fx if fx else x, y ^ fy if fy else y, c ^ fc if fc else c)

        def copy(a, k):
            px, py, pc = peer(k)
            return pltpu.make_async_remote_copy(
                src_ref=ins[a].at[4 * px + 2 * py + pc],
                dst_ref=outs[a].at[my_idx],
                send_sem=send_sems.at[7 * a + k - 1],
                recv_sem=recv_sems.at[7 * a + k - 1],
                device_id=(px, py, pc),
                device_id_type=MESH_ID,
            )

        mine = [pltpu.make_async_copy(ins[a].at[my_idx], outs[a].at[my_idx], local_sems.at[a]) for a in range(n)]
        for cp in mine:
            cp.start()
        order = [1, 4, 2, 5, 3, 6, 7]
        cps = [copy(a, k) for k in order for a in range(n)]
        for cp in cps:
            cp.start()
        for cp in cps:
            cp.wait()
        for cp in mine:
            cp.wait()

    any_spec = pl.BlockSpec(memory_space=pl.ANY)
    outs = _pcall(
        body,
        name=name,
        out_shape=[jax.ShapeDtypeStruct(a.shape, a.dtype) for a in arrs],
        in_specs=[any_spec] * n,
        out_specs=[any_spec] * n,
        scratch_shapes=[
            pltpu.SemaphoreType.DMA((7 * n,)),
            pltpu.SemaphoreType.DMA((7 * n,)),
            pltpu.SemaphoreType.DMA((n,)),
        ],
    )(*arrs)
    return list(outs)


_HBM = pl.BlockSpec(memory_space=pltpu.HBM)
_SEM = pl.BlockSpec(memory_space=pltpu.SEMAPHORE)
_EFFECT = pltpu.SideEffectType.DATAFLOW_SIDE_EFFECTING


def _descriptors(copies, send_sems, recv_sems):
    descs = []
    for i, (src, dst, dev) in enumerate(copies):
        if dev is None:
            descs.append(pltpu.make_async_copy(src, dst, recv_sems.at[i]))
        else:
            descs.append(pltpu.make_async_remote_copy(src_ref=src, dst_ref=dst, send_sem=send_sems.at[i],
                                                      recv_sem=recv_sems.at[i], device_id=dev, device_id_type=MESH_ID))
    return descs


def _split_start(name, arrays, copies_fn, n_copies):
    n = len(arrays)

    def body(*refs):
        send_sems, recv_sems = refs[n], refs[n + 1]
        token = refs[2 * n + 2]
        for dsc in _descriptors(copies_fn(refs[:n]), send_sems, recv_sems):
            dsc.start()
        token[...] = jnp.zeros_like(token)

    outs = _pcall(
        body,
        name=name,
        out_shape=(pltpu.SemaphoreType.DMA((n_copies,)), pltpu.SemaphoreType.DMA((n_copies,)),
                   *[pltpu.HBM(a.shape, a.dtype) for a in arrays], jax.ShapeDtypeStruct((8, LANE), F32)),
        in_specs=[_HBM] * n,
        out_specs=(_SEM, _SEM, *[_HBM] * n, pl.BlockSpec(memory_space=pltpu.VMEM)),
        input_output_aliases={i: 2 + i for i in range(n)},
        compiler_params=pltpu.CompilerParams(has_side_effects=_EFFECT),
    )(*[pltpu.with_memory_space_constraint(a, pltpu.HBM) for a in arrays])
    return outs[0], outs[1], list(outs[2:2 + n]), outs[2 + n]


def _split_wait(name, send_sems, recv_sems, arrays, copies_fn, after):
    n = len(arrays)

    def body(*refs):
        for dsc, (_, _, dev) in zip(_descriptors(copies_fn(refs[:n]), refs[n], refs[n + 1]), copies_fn(refs[:n])):
            if dev is None:
                dsc.wait()
            else:
                dsc.wait_send()
                dsc.wait_recv()

    outs = _pcall(
        body,
        name=name,
        out_shape=tuple(pltpu.HBM(a.shape, a.dtype) for a in arrays),
        in_specs=[_HBM] * n + [_SEM, _SEM, pl.BlockSpec(memory_space=pl.ANY)],
        out_specs=tuple([_HBM] * n),
        input_output_aliases={i: i for i in range(n)},
        compiler_params=pltpu.CompilerParams(has_side_effects=_EFFECT),
    )(*arrays, send_sems, recv_sems, after)
    return list(outs)


def _mesh_pos():
    x, y, c = lax.axis_index("x"), lax.axis_index("y"), lax.axis_index("c")
    return x, y, c, [(1 - x, y), (x, 1 - y), (1 - x, 1 - y)]


def _gather_ici_copies(n):
    def copies(refs):
        x, y, c, chips = _mesh_pos()
        me = 4 * x + 2 * y + c
        out = []
        for a in range(n):
            src, buf = refs[a], refs[n + a]
            out.append((src, buf.at[me], None))
            out.append((src, buf.at[me], (x, y, 1 - c)))
            out += [(src, buf.at[me], (cx, cy, c)) for cx, cy in chips[:2]]
        return out
    return copies


def _gather_pass_copies(n):
    def copies(refs):
        x, y, c, chips = _mesh_pos()
        south = c == 0
        bx, by = jnp.where(south, 1 - x, x), jnp.where(south, y, 1 - y)
        tx, ty = jnp.where(south, x, 1 - x), jnp.where(south, 1 - y, y)
        out = []
        for a in range(n):
            rows = refs[a].at[4 * bx + 2 * by + c]
            out.append((rows, rows, (tx, ty, c)))
            for cx, cy in chips[:2]:
                rows = refs[a].at[4 * cx + 2 * cy + c]
                out.append((rows, rows, (x, y, 1 - c)))
        return out
    return copies


def _gather_d2d_copies(n):
    def copies(refs):
        x, y, c, chips = _mesh_pos()
        cx, cy = chips[2]
        out = []
        for a in range(n):
            rows = refs[a].at[4 * cx + 2 * cy + c]
            out.append((rows, rows, (x, y, 1 - c)))
        return out
    return copies


def _reduce_d2d_copies(n):
    def copies(refs):
        x, y, c, _ = _mesh_pos()
        out = []
        for a in range(n):
            for k in range(4):
                out.append((refs[a].at[2 * k + (1 - c)], refs[n + a].at[k], (x, y, 1 - c)))
        return out
    return copies


def _reduce_ici_copies(n):
    def copies(refs):
        x, y, c, chips = _mesh_pos()
        mine = 2 * x + y
        out = []
        for a in range(n):
            src, land = refs[a], refs[n + a]
            out.append((src.at[mine], land.at[mine], None))
            out += [(src.at[2 * cx + cy], land.at[mine], (cx, cy, c)) for cx, cy in chips]
        return out
    return copies


def _pair_sum(send, land, c_idx, name):
    _, r, cols = send.shape
    rb = _tile(r, max(8, (1 << 22) // (send.dtype.itemsize * cols) // 8 * 8), 8)
    dt = send.dtype

    def body(c_ref, s_ref, l_ref, o_ref):
        o_ref[...] = (s_ref[...].astype(F32) + l_ref[...].astype(F32)).astype(dt)

    return pl.pallas_call(
        body,
        name=name,
        out_shape=jax.ShapeDtypeStruct((4, r, cols), dt),
        grid_spec=pltpu.PrefetchScalarGridSpec(
            num_scalar_prefetch=1,
            grid=(4, r // rb),
            in_specs=[pl.BlockSpec((None, rb, cols), lambda k, i, c_ref: (2 * k + c_ref[0], i, 0)),
                      pl.BlockSpec((None, rb, cols), lambda k, i, c_ref: (k, i, 0))],
            out_specs=pl.BlockSpec((None, rb, cols), lambda k, i, c_ref: (k, i, 0)),
        ),
        compiler_params=_cparams(("parallel", "parallel")),
    )(c_idx, send, land)


_DIMS = {
    "nn": (((1,), (0,)), ((), ())),
    "nt": (((1,), (1,)), ((), ())),
    "tn": (((0,), (0,)), ((), ())),
}


def _mm_call(a, b, *, mode, grid, a_spec, b_spec, o_spec, out_shape, acc_shape, name):
    nk = grid[2]
    out_dtype = out_shape.dtype

    def body(a_ref, b_ref, o_ref, *scratch):
        p = lax.dot_general(a_ref[...].astype(BF16), b_ref[...].astype(BF16), _DIMS[mode],
                            preferred_element_type=F32)
        if nk == 1:
            o_ref[...] = p.astype(out_dtype)
        else:
            acc = scratch[0]
            k = pl.program_id(2)

            @pl.when(k == 0)
            def _():
                acc[...] = p

            @pl.when(k > 0)
            def _():
                acc[...] += p

            @pl.when(k == nk - 1)
            def _():
                o_ref[...] = acc[...].astype(out_dtype)

    return _pcall(
        body,
        name=name,
        out_shape=out_shape,
        grid=grid,
        in_specs=[a_spec, b_spec],
        out_specs=o_spec,
        scratch_shapes=[pltpu.VMEM(acc_shape, F32)] if nk > 1 else [],
        compiler_params=_cparams(("parallel", "parallel", "arbitrary")),
    )(a, b)


def _mm(a, b, mode, out_dtype, name, tm=512, tn=512, tk=2432, a_row_off=0, rows=None):
    if mode == "nn":
        (m, k), (k2, n) = a.shape, b.shape
    elif mode == "nt":
        (m, k), (n, k2) = a.shape, b.shape
    else:
        (k, m), (k2, n) = a.shape, b.shape
        if rows is not None:
            k = k2 = rows
    assert k == k2, (a.shape, b.shape, mode)
    if mode != "tn":
        m = (m if rows is None else rows + a_row_off) - a_row_off
    tm, tn, tk = _tile(m, tm, 8), _tile(n, tn), _tile(k, tk, 8 if mode == "tn" else LANE)
    assert a_row_off % tm == 0
    ro = a_row_off // tm
    grid = (m // tm, n // tn, k // tk)
    if mode == "tn":
        a_spec = pl.BlockSpec((tk, tm), lambda i, j, kk: (kk, i))
    else:
        a_spec = pl.BlockSpec((tm, tk), lambda i, j, kk: (i + ro, kk))
    if mode == "nt":
        b_spec = pl.BlockSpec((tn, tk), lambda i, j, kk: (j, kk))
    else:
        b_spec = pl.BlockSpec((tk, tn), lambda i, j, kk: (kk, j))
    o_spec = pl.BlockSpec((tm, tn), lambda i, j, kk: (i, j))
    return _mm_call(a, b, mode=mode, grid=grid, a_spec=a_spec, b_spec=b_spec, o_spec=o_spec,
                    out_shape=jax.ShapeDtypeStruct((m, n), out_dtype), acc_shape=(tm, tn), name=name)


def _mm_cat_nt(pieces, out_dtype, name, tm=1024, tn=1024, tk=2048, rows=None):
    m = pieces[0][0].shape[0] if rows is None else rows
    n = pieces[0][1].shape[0]
    tm, tn = _tile(m, tm, 8), _tile(n, tn)
    steps, starts, s = [], [], 0
    for a, b, off in pieces:
        kp = a.shape[1]
        tkp = _tile(kp, tk)
        assert off % tkp == 0 and b.shape[0] == n
        steps.append((tkp, kp // tkp, off // tkp))
        starts.append(s)
        s += kp // tkp
    nk = s
    npc = len(pieces)

    def body(*refs):
        o_ref, acc = refs[2 * npc], refs[2 * npc + 1]
        kk = pl.program_id(2)

        @pl.when(kk == 0)
        def _():
            acc[...] = jnp.zeros_like(acc)

        for p in range(npc):
            @pl.when((kk >= starts[p]) & (kk < starts[p] + steps[p][1]))
            def _(p=p):
                acc[...] += lax.dot_general(refs[2 * p][...].astype(BF16), refs[2 * p + 1][...].astype(BF16), _DIMS["nt"],
                                            preferred_element_type=F32)

        @pl.when(kk == nk - 1)
        def _():
            o_ref[...] = acc[...].astype(out_dtype)

    in_specs, args = [], []
    for p, (a, b, off) in enumerate(pieces):
        tkp, np_, ob = steps[p]

        def rel(kk, p=p, np_=np_):
            return jnp.clip(kk - starts[p], 0, np_ - 1)

        in_specs.append(pl.BlockSpec((tm, tkp), lambda i, j, kk, rel=rel: (i, rel(kk))))
        in_specs.append(pl.BlockSpec((tn, tkp), lambda i, j, kk, rel=rel, ob=ob: (j, ob + rel(kk))))
        args += [a, b]
    return _pcall(
        body,
        name=name,
        out_shape=jax.ShapeDtypeStruct((m, n), out_dtype),
        grid=(m // tm, n // tn, nk),
        in_specs=in_specs,
        out_specs=pl.BlockSpec((tm, tn), lambda i, j, kk: (i, j)),
        scratch_shapes=[pltpu.VMEM((tm, tn), F32)],
        compiler_params=_cparams(("parallel", "parallel", "arbitrary")),
    )(*args)


def _mm_cat_tn(a, pieces, out_dtype, name, tm=1024, tn=1024, rows=None):
    k = a.shape[0] if rows is None else rows
    m = a.shape[1]
    tm = _tile(m, tm)
    starts, s = [], 0
    for b in pieces:
        assert b.shape[1] % tn == 0
        starts.append(s)
        s += b.shape[1] // tn
    nj = s
    npc = len(pieces)

    def body(*refs):
        a_ref, o_ref = refs[0], refs[1 + npc]
        j = pl.program_id(1)
        for p in range(npc):
            @pl.when((j >= starts[p]) & (j < starts[p] + pieces[p].shape[1] // tn))
            def _(p=p):
                o_ref[...] = lax.dot_general(a_ref[...].astype(BF16), refs[1 + p][...].astype(BF16), _DIMS["tn"],
                                             preferred_element_type=F32).astype(out_dtype)

    in_specs = [pl.BlockSpec((k, tm), lambda i, j: (0, i))]
    for p, b in enumerate(pieces):
        np_ = b.shape[1] // tn
        in_specs.append(pl.BlockSpec((k, tn), lambda i, j, p=p, np_=np_: (0, jnp.clip(j - starts[p], 0, np_ - 1))))
    return _pcall(
        body,
        name=name,
        out_shape=jax.ShapeDtypeStruct((m, nj * tn), out_dtype),
        grid=(m // tm, nj),
        in_specs=in_specs,
        out_specs=pl.BlockSpec((tm, tn), lambda i, j: (i, j)),
        compiler_params=_cparams(("parallel", "arbitrary")),
    )(a, *pieces)


def _mm_up_fwd(z2, w3, name, tm=1024):
    t, d = z2.shape
    nsh, _, c = w3.shape
    tm = _tile(t, tm, 8)
    return _mm_call(z2, w3, mode="nn", grid=(t // tm, nsh, 1),
                    a_spec=pl.BlockSpec((tm, d), lambda i, j, kk: (i, 0)),
                    b_spec=pl.BlockSpec((None, d, c), lambda i, j, kk: (j, 0, 0)),
                    o_spec=pl.BlockSpec((tm, c), lambda i, j, kk: (i, j)),
                    out_shape=jax.ShapeDtypeStruct((t, nsh * c), BF16), acc_shape=(tm, c), name=name)


def _mm_up_dz(du3, w3, name, tm=512, tn=1024):
    _, t, f = du3.shape
    nsh, d, c = w3.shape
    half = nsh // 2
    assert f == half * c
    tm, tn = _tile(t, tm, 8), _tile(d, tn)

    def body(a_ref, b_ref, o_ref, acc):
        kk = pl.program_id(2)
        p = None
        for s in range(half):
            q = lax.dot_general(a_ref[:, s * c:(s + 1) * c], b_ref[s], _DIMS["nt"], preferred_element_type=F32)
            p = q if p is None else p + q

        @pl.when(kk == 0)
        def _():
            acc[...] = p

        @pl.when(kk == 1)
        def _():
            o_ref[...] = (acc[...] + p).astype(BF16)

    return _pcall(
        body,
        name=name,
        out_shape=jax.ShapeDtypeStruct((t, d), BF16),
        grid=(t // tm, d // tn, 2),
        in_specs=[pl.BlockSpec((None, tm, f), lambda i, j, kk: (kk, i, 0)),
                  pl.BlockSpec((half, tn, c), lambda i, j, kk: (kk, j, 0))],
        out_specs=pl.BlockSpec((tm, tn), lambda i, j, kk: (i, j)),
        scratch_shapes=[pltpu.VMEM((tm, tn), F32)],
        compiler_params=_cparams(("parallel", "parallel", "arbitrary")),
    )(du3, w3)


def _mm_sum_nt(pieces, out_dtype, name, tm=512, tn=512, rows=None):
    m = pieces[0][0].shape[0] if rows is None else rows
    n = pieces[0][2].shape[0]
    tm, tn = _tile(m, tm, 8), _tile(n, tn)
    npc = len(pieces)

    def body(*refs):
        p = None
        for s in range(npc):
            q = lax.dot_general(refs[2 * s][...].astype(BF16), refs[2 * s + 1][...].astype(BF16), _DIMS["nt"],
                                preferred_element_type=F32)
            p = q if p is None else p + q
        refs[2 * npc][...] = p.astype(out_dtype)

    in_specs, args = [], []
    for a, ao, b, bo, kp in pieces:
        assert ao % kp == 0 and bo % kp == 0 and b.shape[0] == n
        in_specs.append(pl.BlockSpec((tm, kp), lambda i, j, ab=ao // kp: (i, ab)))
        in_specs.append(pl.BlockSpec((tn, kp), lambda i, j, bb=bo // kp: (j, bb)))
        args += [a, b]
    return _pcall(
        body,
        name=name,
        out_shape=jax.ShapeDtypeStruct((m, n), out_dtype),
        grid=(m // tm, n // tn),
        in_specs=in_specs,
        out_specs=pl.BlockSpec((tm, tn), lambda i, j: (i, j)),
        compiler_params=_cparams(("parallel", "parallel")),
    )(*args)


def _mm_up_gw(z2, du3, nsh, name, tm=1024):
    t, d = z2.shape
    f = du3.shape[2]
    half = nsh // 2
    c = f // half
    tm = _tile(d, tm)
    return _mm_call(z2, du3, mode="tn", grid=(d // tm, nsh, 1),
                    a_spec=pl.BlockSpec((t, tm), lambda i, j, kk: (0, i)),
                    b_spec=pl.BlockSpec((None, t, c), lambda i, j, kk: (j // half, 0, j % half)),
                    o_spec=pl.BlockSpec((None, tm, c), lambda i, j, kk: (j, i, 0)),
                    out_shape=jax.ShapeDtypeStruct((nsh, d, c), BF16), acc_shape=(tm, c), name=name)


def _rms(x):
    r = lax.rsqrt(jnp.mean(x * x, axis=-1, keepdims=True) + NORM_EPS)
    return x * r, r


def _rms_bwd(dxh, xh, r):
    return r * (dxh - xh * jnp.mean(dxh * xh, axis=-1, keepdims=True))


def _colsum(v):
    return jnp.sum(v, axis=0, keepdims=True)


def _rope(v, c, s1, s2, q):
    w = v.shape[-1]
    return v * c + pltpu.roll(v, w - q, 1) * s1 + pltpu.roll(v, q, 1) * s2


def _rope_t(d, c, s1, s2, q):
    w = d.shape[-1]
    return d * c + pltpu.roll(d * s1, q, 1) + pltpu.roll(d * s2, w - q, 1)


def _norm_mod_fwd(ctx, x, gain, mods):
    tc, d = ctx.shape
    t = x.shape[0]
    rb = min(ROW_BLOCK, tc)
    nbl = t // rb

    def body(ctx_ref, x_ref, g_ref, mod_ref, z_ref):
        i = pl.program_id(0)

        def emit(src, sh, sc):
            xh, _ = _rms(src[...])
            z_ref[...] = ((xh * g_ref[...]) * (1.0 + sc) + sh).astype(BF16)

        @pl.when(i >= nbl)
        def _():
            emit(ctx_ref, mod_ref[2:3, :], mod_ref[3:4, :])

        @pl.when(i < nbl)
        def _():
            emit(x_ref, mod_ref[0:1, :], mod_ref[1:2, :])

    return _pcall(
        body,
        name="norm1_mod_fwd",
        out_shape=jax.ShapeDtypeStruct((tc + t, d), BF16),
        grid=((tc + t) // rb,),
        in_specs=[
            pl.BlockSpec((rb, d), lambda i: (jnp.maximum(i - nbl, 0), 0)),
            pl.BlockSpec((rb, d), lambda i: (jnp.minimum(i, nbl - 1), 0)),
            pl.BlockSpec((1, d), lambda i: (0, 0)),
            pl.BlockSpec((8, d), lambda i: (0, 0)),
        ],
        out_specs=pl.BlockSpec((rb, d), lambda i: (i, 0)),
        compiler_params=_cparams(("arbitrary",)),
    )(ctx, x, gain, mods)


def _norm1_bwd(ctx, x, gain, mods, dz_ctx, dz_lat, dx1):
    tc, d = ctx.shape
    t = x.shape[0]
    rb = min(ROW_BLOCK, tc)
    nbl = t // rb

    def body(ctx_ref, x_ref, g_ref, mod_ref, dzc_ref, dzl_ref, dx1_ref, gx_ref, st_ref):
        i = pl.program_id(0)

        @pl.when(i == 0)
        def _():
            st_ref[...] = jnp.zeros_like(st_ref)

        def common(src, dz, sc, row_sh, row_sc):
            xh, r = _rms(src[...])
            g = g_ref[...]
            dxn = dz * (1.0 + sc)
            st_ref[row_sh:row_sh + 1, :] += _colsum(dz)
            st_ref[row_sc:row_sc + 1, :] += _colsum(dz * (xh * g))
            st_ref[2:3, :] += _colsum(dxn * xh)
            return _rms_bwd(dxn * g, xh, r)

        @pl.when(i >= nbl)
        def _():
            common(ctx_ref, dzc_ref[...], mod_ref[3:4, :], 3, 4)

        @pl.when(i < nbl)
        def _():
            gx_ref[...] = dx1_ref[...] + common(x_ref, dzl_ref[...], mod_ref[1:2, :], 0, 1)

    lat = lambda i: (jnp.minimum(i, nbl - 1), 0)
    cix = lambda i: (jnp.maximum(i - nbl, 0), 0)
    return _pcall(
        body,
        name="norm1_mod_bwd",
        out_shape=[jax.ShapeDtypeStruct((t, d), F32), jax.ShapeDtypeStruct((8, d), F32)],
        grid=((tc + t) // rb,),
        in_specs=[
            pl.BlockSpec((rb, d), cix),
            pl.BlockSpec((rb, d), lat),
            pl.BlockSpec((1, d), lambda i: (0, 0)),
            pl.BlockSpec((8, d), lambda i: (0, 0)),
            pl.BlockSpec((rb, d), cix),
            pl.BlockSpec((rb, d), lat),
            pl.BlockSpec((rb, d), lat),
        ],
        out_specs=[pl.BlockSpec((rb, d), lat), pl.BlockSpec((8, d), lambda i: (0, 0))],
        compiler_params=_cparams(("arbitrary",)),
    )(ctx, x, gain, mods, dz_ctx, dz_lat, dx1)


def _key_prep_fwd(kv, kv_gain, kb_gain, tabs):
    ta, wkv = kv.shape
    kvl = MLA_KV_LORA
    nb = GQA_KV_HEADS * GQA_HEAD_DIM
    rb = ROW_BLOCK if ta % ROW_BLOCK == 0 else LANE
    hd = GQA_HEAD_DIM

    def body(kv_ref, g_ref, gb_ref, ca, s1a, s2a, cb, s1b, s2b, kin_ref, kb_ref, vb_ref):
        xh, _ = _rms(kv_ref[:, 0:kvl])
        kin_ref[:, 0:kvl] = (xh * g_ref[...]).astype(BF16)
        kpe = kv_ref[:, kvl + 2 * nb:kvl + 2 * nb + LANE]
        kin_ref[:, kvl:kvl + LANE] = _rope(kpe, ca[...], s1a[...], s2a[...], MLA_ROPE // 4).astype(BF16)
        for h in range(GQA_KV_HEADS):
            nh, _ = _rms(kv_ref[:, kvl + h * hd:kvl + (h + 1) * hd])
            kb_ref[:, h * hd:(h + 1) * hd] = _rope(nh * gb_ref[...], cb[...], s1b[...], s2b[...], hd // 4).astype(BF16)
        vb_ref[...] = kv_ref[:, kvl + nb:kvl + 2 * nb].astype(BF16)

    row = lambda w: pl.BlockSpec((rb, w), lambda i: (i, 0))
    fix = lambda w: pl.BlockSpec((1, w), lambda i: (0, 0))
    return _pcall(
        body,
        name="key_prep_fwd",
        out_shape=[jax.ShapeDtypeStruct((ta, kvl + LANE), BF16), jax.ShapeDtypeStruct((ta, nb), BF16),
                   jax.ShapeDtypeStruct((ta, nb), BF16)],
        grid=(ta // rb,),
        in_specs=[row(wkv), fix(kvl), fix(hd)] + [row(LANE)] * 3 + [row(hd)] * 3,
        out_specs=[row(kvl + LANE), row(nb), row(nb)],
        compiler_params=_cparams(("parallel",)),
    )(kv, kv_gain, kb_gain, *tabs)


def _key_prep_bwd(kv, kv_gain, kb_gain, tabs, dkin, dkb, dvb):
    ta, wkv = kv.shape
    kvl = MLA_KV_LORA
    nb = GQA_KV_HEADS * GQA_HEAD_DIM
    rb = ROW_BLOCK if ta % ROW_BLOCK == 0 else LANE
    hd = GQA_HEAD_DIM

    def body(kv_ref, g_ref, gb_ref, ca, s1a, s2a, cb, s1b, s2b, dkin_ref, dkb_ref, dvb_ref, dkv_ref, st_ref, stb_ref):
        @pl.when(pl.program_id(0) == 0)
        def _():
            st_ref[...] = jnp.zeros_like(st_ref)
            stb_ref[...] = jnp.zeros_like(stb_ref)

        xh, r = _rms(kv_ref[:, 0:kvl])
        dn = dkin_ref[:, 0:kvl]
        st_ref[0:1, :] += _colsum(dn * xh)
        dkv_ref[:, 0:kvl] = _rms_bwd(dn * g_ref[...], xh, r).astype(BF16)
        dpe = _rope_t(dkin_ref[:, kvl:kvl + LANE], ca[...], s1a[...], s2a[...], MLA_ROPE // 4)
        dkv_ref[:, kvl + 2 * nb:kvl + 2 * nb + LANE] = dpe.astype(BF16)
        for h in range(GQA_KV_HEADS):
            nh, rh = _rms(kv_ref[:, kvl + h * hd:kvl + (h + 1) * hd])
            dn_h = _rope_t(dkb_ref[:, h * hd:(h + 1) * hd], cb[...], s1b[...], s2b[...], hd // 4)
            stb_ref[0:1, :] += _colsum(dn_h * nh)
            dkv_ref[:, kvl + h * hd:kvl + (h + 1) * hd] = _rms_bwd(dn_h * gb_ref[...], nh, rh).astype(BF16)
        dkv_ref[:, kvl + nb:kvl + 2 * nb] = dvb_ref[...].astype(BF16)

    row = lambda w: pl.BlockSpec((rb, w), lambda i: (i, 0))
    fix = lambda w: pl.BlockSpec((1, w), lambda i: (0, 0))
    return _pcall(
        body,
        name="key_prep_bwd",
        out_shape=[jax.ShapeDtypeStruct((ta, wkv), BF16), jax.ShapeDtypeStruct((8, kvl), F32),
                   jax.ShapeDtypeStruct((8, hd), F32)],
        grid=(ta // rb,),
        in_specs=[row(wkv), fix(kvl), fix(hd)] + [row(LANE)] * 3 + [row(hd)] * 3 + [row(kvl + LANE), row(nb), row(nb)],
        out_specs=[row(wkv), pl.BlockSpec((8, kvl), lambda i: (0, 0)), pl.BlockSpec((8, hd), lambda i: (0, 0))],
        compiler_params=_cparams(("arbitrary",)),
    )(kv, kv_gain, kb_gain, *tabs, dkin, dkb, dvb)


def _q_prep_fwd(qg, q_gain, qb_gain, tabs, qscale):
    t = qg.shape[0]
    ql = MLA_Q_LORA
    hd = GQA_HEAD_DIM
    hb = GQA_HEADS * hd
    rb = min(ROW_BLOCK, t)

    def body(q_ref, g_ref, gb_ref, cb, s1b, s2b, cqn_ref, qb_ref):
        xh, _ = _rms(q_ref[:, 0:ql])
        cqn_ref[...] = (xh * g_ref[...]).astype(BF16)
        for h in range(GQA_HEADS):
            nh, _ = _rms(q_ref[:, ql + h * hd:ql + (h + 1) * hd])
            qh = _rope(nh * gb_ref[...], cb[...], s1b[...], s2b[...], hd // 4)
            qb_ref[:, h * hd:(h + 1) * hd] = (qh * qscale).astype(BF16)

    row = lambda w: pl.BlockSpec((rb, w), lambda i: (i, 0))
    fix = lambda w: pl.BlockSpec((1, w), lambda i: (0, 0))
    return _pcall(
        body,
        name="q_prep_fwd",
        out_shape=[jax.ShapeDtypeStruct((t, ql), BF16), jax.ShapeDtypeStruct((t, hb), BF16)],
        grid=(t // rb,),
        in_specs=[row(ql + hb), fix(ql), fix(hd)] + [row(hd)] * 3,
        out_specs=[row(ql), row(hb)],
        compiler_params=_cparams(("parallel",)),
    )(qg, q_gain, qb_gain, *tabs)


def _q_prep_bwd(qg, q_gain, qb_gain, tabs, dcqn, dqb, wpad, qscale):
    t = qg.shape[0]
    ql = MLA_Q_LORA
    hd = GQA_HEAD_DIM
    hb = GQA_HEADS * hd
    rb = min(ROW_BLOCK, t)

    def body(q_ref, g_ref, gb_ref, cb, s1b, s2b, dcqn_ref, dqb_ref, dq_ref, st_ref, stb_ref):
        @pl.when(pl.program_id(0) == 0)
        def _():
            st_ref[...] = jnp.zeros_like(st_ref)
            stb_ref[...] = jnp.zeros_like(stb_ref)

        xh, r = _rms(q_ref[:, 0:ql])
        dn = dcqn_ref[...]
        st_ref[0:1, :] += _colsum(dn * xh)
        dq_ref[:, 0:ql] = _rms_bwd(dn * g_ref[...], xh, r).astype(BF16)
        for h in range(GQA_HEADS):
            nh, rh = _rms(q_ref[:, ql + h * hd:ql + (h + 1) * hd])
            dn_h = _rope_t(dqb_ref[:, h * hd:(h + 1) * hd] * qscale, cb[...], s1b[...], s2b[...], hd // 4)
            stb_ref[0:1, :] += _colsum(dn_h * nh)
            dq_ref[:, ql + h * hd:ql + (h + 1) * hd] = _rms_bwd(dn_h * gb_ref[...], nh, rh).astype(BF16)
        if wpad:
            dq_ref[:, ql + hb:ql + hb + wpad] = jnp.zeros((rb, wpad), BF16)

    row = lambda w: pl.BlockSpec((rb, w), lambda i: (i, 0))
    fix = lambda w: pl.BlockSpec((1, w), lambda i: (0, 0))
    return _pcall(
        body,
        name="q_prep_bwd",
        out_shape=[jax.ShapeDtypeStruct((t, ql + hb + wpad), BF16), jax.ShapeDtypeStruct((8, ql), F32),
                   jax.ShapeDtypeStruct((8, hd), F32)],
        grid=(t // rb,),
        in_specs=[row(ql + hb), fix(ql), fix(hd)] + [row(hd)] * 3 + [row(ql), row(hb)],
        out_specs=[row(ql + hb + wpad), pl.BlockSpec((8, ql), lambda i: (0, 0)), pl.BlockSpec((8, hd), lambda i: (0, 0))],
        compiler_params=_cparams(("arbitrary",)),
    )(qg, q_gain, qb_gain, *tabs, dcqn, dqb)


def _rope_a(v, tabs, transpose, out_dtype, name, qscale):
    t, w = v.shape
    rb = min(ROW_BLOCK, t)
    fn = _rope_t if transpose else _rope

    def body(v_ref, c, s1, s2, o_ref):
        for h in range(w // MLA_SLOT):
            sl = slice(h * MLA_SLOT, (h + 1) * MLA_SLOT)
            o_ref[:, sl] = (fn(v_ref[:, sl].astype(F32), c[...], s1[...], s2[...], MLA_ROPE // 4) * qscale).astype(out_dtype)

    row = lambda ww: pl.BlockSpec((rb, ww), lambda i: (i, 0))
    return _pcall(
        body,
        name=name,
        out_shape=jax.ShapeDtypeStruct((t, w), out_dtype),
        grid=(t // rb,),
        in_specs=[row(w)] + [row(MLA_SLOT)] * 3,
        out_specs=row(w),
        compiler_params=_cparams(("parallel",)),
    )(v, *tabs)


def _merge_fwd(pa, pb, qg, gate_blk):
    t, d = pa.shape
    rb = min(ROW_BLOCK, t)

    def body(pa_ref, pb_ref, ga_ref, gb_ref, o_ref):
        o_ref[...] = (jax.nn.sigmoid(ga_ref[...]) * pa_ref[...].astype(F32)
                      + jax.nn.sigmoid(gb_ref[...]) * pb_ref[...].astype(F32)).astype(BF16)

    row = pl.BlockSpec((rb, d), lambda i: (i, 0))
    return _pcall(
        body,
        name="merge_fwd",
        out_shape=jax.ShapeDtypeStruct((t, d), BF16),
        grid=(t // rb,),
        in_specs=[row, row, pl.BlockSpec((rb, d), lambda i: (i, gate_blk)), pl.BlockSpec((rb, d), lambda i: (i, gate_blk + 1))],
        out_specs=row,
        compiler_params=_cparams(("parallel",)),
    )(pa, pb, qg, qg)


def _merge_bwd(dm, pa, pb, qg, gate_blk):
    t, d = pa.shape
    rb = min(ROW_BLOCK, t)

    def body(dm_ref, pa_ref, pb_ref, ga_ref, gb_ref, dpa_ref, dpb_ref, dg_ref):
        dmv = dm_ref[...].astype(F32)
        sa = jax.nn.sigmoid(ga_ref[...])
        sb = jax.nn.sigmoid(gb_ref[...])
        dpa_ref[...] = (dmv * sa).astype(BF16)
        dpb_ref[...] = (dmv * sb).astype(BF16)
        dg_ref[:, 0:d] = (dmv * pa_ref[...].astype(F32) * (sa * (1.0 - sa))).astype(BF16)
        dg_ref[:, d:2 * d] = (dmv * pb_ref[...].astype(F32) * (sb * (1.0 - sb))).astype(BF16)

    row = pl.BlockSpec((rb, d), lambda i: (i, 0))
    return _pcall(
        body,
        name="merge_bwd",
        out_shape=[jax.ShapeDtypeStruct((t, d), BF16), jax.ShapeDtypeStruct((t, d), BF16),
                   jax.ShapeDtypeStruct((t, 2 * d), BF16)],
        grid=(t // rb,),
        in_specs=[row, row, row, pl.BlockSpec((rb, d), lambda i: (i, gate_blk)), pl.BlockSpec((rb, d), lambda i: (i, gate_blk + 1))],
        out_specs=[row, row, pl.BlockSpec((rb, 2 * d), lambda i: (i, 0))],
        compiler_params=_cparams(("parallel",)),
    )(dm, pa, pb, qg, qg)


def _resid_norm_mod(x, branch, gain, mods, name):
    t, d = x.shape
    rb = min(ROW_BLOCK, t)

    def body(x_ref, b_ref, g_ref, mod_ref, x1_ref, z_ref):
        x1 = x_ref[...] + mod_ref[0:1, :] * b_ref[...]
        x1_ref[...] = x1
        xh, _ = _rms(x1)
        z_ref[...] = ((xh * g_ref[...]) * (1.0 + mod_ref[2:3, :]) + mod_ref[1:2, :]).astype(BF16)

    row = pl.BlockSpec((rb, d), lambda i: (i, 0))
    return _pcall(
        body,
        name=name,
        out_shape=[jax.ShapeDtypeStruct((t, d), F32), jax.ShapeDtypeStruct((t, d), BF16)],
        grid=(t // rb,),
        in_specs=[row, row, pl.BlockSpec((1, d), lambda i: (0, 0)), pl.BlockSpec((8, d), lambda i: (0, 0))],
        out_specs=[row, row],
        compiler_params=_cparams(("parallel",)),
    )(x, branch, gain, mods)


def _norm2_bwd(x1, attn, gain, mods, dz2, dx2):
    t, d = x1.shape
    rb = min(ROW_BLOCK, t)

    def body(x1_ref, at_ref, g_ref, mod_ref, dz_ref, dx2_ref, dx1_ref, da_ref, st_ref):
        @pl.when(pl.program_id(0) == 0)
        def _():
            st_ref[...] = jnp.zeros_like(st_ref)

        xh, r = _rms(x1_ref[...])
        g = g_ref[...]
        dz = dz_ref[...].astype(F32)
        dxn = dz * (1.0 + mod_ref[1:2, :])
        st_ref[0:1, :] += _colsum(dz)
        st_ref[1:2, :] += _colsum(dz * (xh * g))
        st_ref[2:3, :] += _colsum(dxn * xh)
        dx1 = dx2_ref[...] + _rms_bwd(dxn * g, xh, r)
        dx1_ref[...] = dx1
        st_ref[3:4, :] += _colsum(dx1 * at_ref[...])
        da_ref[...] = (dx1 * mod_ref[0:1, :]).astype(BF16)

    row = pl.BlockSpec((rb, d), lambda i: (i, 0))
    return _pcall(
        body,
        name="norm2_mod_bwd",
        out_shape=[jax.ShapeDtypeStruct((t, d), F32), jax.ShapeDtypeStruct((t, d), BF16), jax.ShapeDtypeStruct((8, d), F32)],
        grid=(t // rb,),
        in_specs=[row, row, pl.BlockSpec((1, d), lambda i: (0, 0)), pl.BlockSpec((8, d), lambda i: (0, 0)), row, row],
        out_specs=[row, row, pl.BlockSpec((8, d), lambda i: (0, 0))],
        compiler_params=_cparams(("arbitrary",)),
    )(x1, attn, gain, mods, dz2, dx2)


def _final_loss(x1, ffn, gain, mods, target):
    t, d = x1.shape
    rb = min(ROW_BLOCK, t)
    nb = t // rb

    def body(x1_ref, f_ref, g_ref, mod_ref, tg_ref, dx2_ref, df_ref, st_ref):
        i = pl.program_id(0)

        @pl.when(i == 0)
        def _():
            st_ref[...] = jnp.zeros_like(st_ref)

        ffn_v = f_ref[...]
        g2 = mod_ref[0:1, :]
        x2 = x1_ref[...] + g2 * ffn_v
        xh, r = _rms(x2)
        g = g_ref[...]
        err = xh * g - tg_ref[...]
        st_ref[2:3, :] += _colsum(err * err) * (0.5 / d)
        dy = err * (1.0 / d)
        st_ref[0:1, :] += _colsum(dy * xh)
        dx2 = _rms_bwd(dy * g, xh, r)
        dx2_ref[...] = dx2
        st_ref[1:2, :] += _colsum(dx2 * ffn_v)
        df_ref[...] = (dx2 * g2).astype(BF16)

        @pl.when(i == nb - 1)
        def _():
            st_ref[3:4, :] = jnp.broadcast_to(jnp.sum(st_ref[2:3, :], axis=-1, keepdims=True), (1, d))

    row = pl.BlockSpec((rb, d), lambda i: (i, 0))
    return _pcall(
        body,
        name="final_norm_loss",
        out_shape=[jax.ShapeDtypeStruct((t, d), F32), jax.ShapeDtypeStruct((t, d), BF16), jax.ShapeDtypeStruct((8, d), F32)],
        grid=(nb,),
        in_specs=[row, row, pl.BlockSpec((1, d), lambda i: (0, 0)), pl.BlockSpec((8, d), lambda i: (0, 0)), row],
        out_specs=[row, row, pl.BlockSpec((8, d), lambda i: (0, 0))],
        compiler_params=_cparams(("arbitrary",)),
    )(x1, ffn, gain, mods, target)


def _row_ends(shape):
    rows = lax.broadcasted_iota(jnp.int32, shape, 0)
    return rows == 0, rows == shape[0] - 1


def _shift_dn(v, first):
    return jnp.where(first, 0.0, pltpu.roll(v, 1, 0))


def _shift_up(v, last):
    return jnp.where(last, 0.0, pltpu.roll(v, v.shape[0] - 1, 0))


def _conv_fwd(u, cw, cb):
    t, f2 = u.shape
    f = f2 // 2
    cbk = _tile(f, 256)
    nf = f // cbk

    def body(ua_ref, ub_ref, cwa_ref, cwb_ref, cba_ref, cbb_ref, h_ref, uc_ref):
        first, last = _row_ends((t, cbk))
        outs = []
        for u_ref, cw_ref, cb_ref in ((ua_ref, cwa_ref, cba_ref), (ub_ref, cwb_ref, cbb_ref)):
            uu, cwv = u_ref[...].astype(F32), cw_ref[...]
            outs.append(cb_ref[...] + cwv[0:1, :] * _shift_dn(uu, first) + cwv[1:2, :] * uu
                        + cwv[2:3, :] * _shift_up(uu, last))
        a, b = outs
        uc_ref[0] = a.astype(BF16)
        uc_ref[1] = b.astype(BF16)
        h_ref[...] = (a * jax.nn.sigmoid(a) * b).astype(BF16)

    ca = lambda r: pl.BlockSpec((r, cbk), lambda j: (0, j))
    cbs = lambda r: pl.BlockSpec((r, cbk), lambda j: (0, nf + j))
    return _pcall(
        body,
        name="conv_gate_fwd",
        out_shape=[jax.ShapeDtypeStruct((t, f), BF16), jax.ShapeDtypeStruct((2, t, f), BF16)],
        grid=(nf,),
        in_specs=[ca(t), cbs(t), ca(3), cbs(3), ca(1), cbs(1)],
        out_specs=[ca(t), pl.BlockSpec((2, t, cbk), lambda j: (0, 0, j))],
        compiler_params=_cparams(("parallel",)),
    )(u, u, cw, cw, cb, cb)


def _conv_bwd(u, uc, cw, dh):
    t, f2 = u.shape
    f = f2 // 2
    cbk = _tile(f, 256)
    nf = f // cbk

    def body(ua_ref, ub_ref, uc_ref, cwa_ref, cwb_ref, dh_ref, du_ref, dcw_ref, dcb_ref):
        first, last = _row_ends((t, cbk))
        a, b = uc_ref[0].astype(F32), uc_ref[1].astype(F32)
        dh_v = dh_ref[...].astype(F32)
        sg = jax.nn.sigmoid(a)
        db = dh_v * (a * sg)
        da = dh_v * b * (sg * (1.0 + a * (1.0 - sg)))
        for idx, (dv, u_ref, cw_ref) in enumerate(((da, ua_ref, cwa_ref), (db, ub_ref, cwb_ref))):
            uu, cwv = u_ref[...].astype(F32), cw_ref[...]
            up, dn = _shift_up(dv, last), _shift_dn(dv, first)
            dcb_ref[idx] = _colsum(dv)
            dcw_ref[idx, 0:1, :] = _colsum(up * uu)
            dcw_ref[idx, 1:2, :] = _colsum(dv * uu)
            dcw_ref[idx, 2:3, :] = _colsum(dn * uu)
            du_ref[idx] = (cwv[0:1, :] * up + cwv[1:2, :] * dv + cwv[2:3, :] * dn).astype(BF16)

    ca = lambda r: pl.BlockSpec((r, cbk), lambda j: (0, j))
    cbs = lambda r: pl.BlockSpec((r, cbk), lambda j: (0, nf + j))
    o3 = lambda r: pl.BlockSpec((2, r, cbk), lambda j: (0, 0, j))
    return _pcall(
        body,
        name="conv_gate_bwd",
        out_shape=[jax.ShapeDtypeStruct((2, t, f), BF16), jax.ShapeDtypeStruct((2, 3, f), F32),
                   jax.ShapeDtypeStruct((2, 1, f), F32)],
        grid=(nf,),
        in_specs=[ca(t), cbs(t), o3(t), ca(3), cbs(3), ca(t)],
        out_specs=[o3(t), o3(3), o3(1)],
        compiler_params=_cparams(("parallel",)),
    )(u, u, uc, cw, cw, dh)


def _attention_fwd(q, kk, vv, *, hq, hkv, dk, dv, k_blk0, v_blk0, name):
    t = q.shape[0]
    tk = kk.shape[0]
    g_sz = hq // hkv
    tq = min(ATT_Q_BLOCK_FWD, t)

    def body(q_ref, k_ref, v_ref, o_ref, lse_ref):
        k = k_ref[...]
        v = v_ref[...]
        for j in range(g_sz):
            s = lax.dot_general(q_ref[:, j * dk:(j + 1) * dk], k, _DIMS["nt"], preferred_element_type=F32)
            m = jnp.max(s, axis=-1, keepdims=True)
            p = jnp.exp2(s - m)
            l = jnp.sum(p, axis=-1, keepdims=True)
            o = jnp.dot(p.astype(BF16), v, preferred_element_type=F32) / l
            o_ref[:, j * dv:(j + 1) * dv] = o.astype(BF16)
            lse_ref[0, :, j:j + 1] = m + jnp.log2(l)

    return _pcall(
        body,
        name=name,
        out_shape=[jax.ShapeDtypeStruct((t, hq * dv), BF16), jax.ShapeDtypeStruct((hkv, t, g_sz), F32)],
        grid=(hkv, t // tq),
        in_specs=[
            pl.BlockSpec((tq, g_sz * dk), lambda g, i: (i, g)),
            pl.BlockSpec((tk, dk), lambda g, i: (0, k_blk0 + g)),
            pl.BlockSpec((tk, dv), lambda g, i: (0, v_blk0 + g)),
        ],
        out_specs=[
            pl.BlockSpec((tq, g_sz * dv), lambda g, i: (i, g)),
            pl.BlockSpec((1, tq, g_sz), lambda g, i: (g, i, 0)),
        ],
        compiler_params=_cparams(("parallel", "parallel")),
    )(q, kk, vv)


def _attention_bwd(q, kk, vv, do, lse, *, hq, hkv, dk, dv, k_blk0, v_blk0, name):
    t = q.shape[0]
    tk = kk.shape[0]
    g_sz = hq // hkv
    tq = min(ATT_Q_BLOCK, t)

    def body(q_ref, k_ref, v_ref, do_ref, lse_ref, dq_ref, dk_ref, dv_ref):
        @pl.when(pl.program_id(1) == 0)
        def _():
            dk_ref[...] = jnp.zeros_like(dk_ref)
            dv_ref[...] = jnp.zeros_like(dv_ref)

        k = k_ref[...]
        v = v_ref[...]
        for j in range(g_sz):
            qj = q_ref[:, j * dk:(j + 1) * dk]
            doj = do_ref[:, j * dv:(j + 1) * dv]
            s = lax.dot_general(qj, k, _DIMS["nt"], preferred_element_type=F32)
            p = jnp.exp2(s - lse_ref[0, :, j:j + 1])
            dp = lax.dot_general(doj, v, _DIMS["nt"], preferred_element_type=F32)
            ds = (p * (dp - jnp.sum(p * dp, axis=-1, keepdims=True))).astype(BF16)
            dv_ref[...] += lax.dot_general(p.astype(BF16), doj, _DIMS["tn"], preferred_element_type=F32)
            dk_ref[...] += lax.dot_general(ds, qj, _DIMS["tn"], preferred_element_type=F32)
            dq_ref[:, j * dk:(j + 1) * dk] = jnp.dot(ds, k, preferred_element_type=F32)

        @pl.when(pl.program_id(1) == t // tq - 1)
        def _():
            dk_ref[...] *= LN2

    return _pcall(
        body,
        name=name,
        out_shape=[jax.ShapeDtypeStruct((t, hq * dk), F32), jax.ShapeDtypeStruct((tk, hkv * dk), F32),
                   jax.ShapeDtypeStruct((tk, hkv * dv), F32)],
        grid=(hkv, t // tq),
        in_specs=[
            pl.BlockSpec((tq, g_sz * dk), lambda g, i: (i, g)),
            pl.BlockSpec((tk, dk), lambda g, i: (0, k_blk0 + g)),
            pl.BlockSpec((tk, dv), lambda g, i: (0, v_blk0 + g)),
            pl.BlockSpec((tq, g_sz * dv), lambda g, i: (i, g)),
            pl.BlockSpec((1, tq, g_sz), lambda g, i: (g, i, 0)),
        ],
        out_specs=[
            pl.BlockSpec((tq, g_sz * dk), lambda g, i: (i, g)),
            pl.BlockSpec((tk, dk), lambda g, i: (0, g)),
            pl.BlockSpec((tk, dv), lambda g, i: (0, g)),
        ],
        compiler_params=_cparams(("parallel", "arbitrary")),
    )(q, kk, vv, do, lse)


def _silu(v):
    return v * jax.nn.sigmoid(v)


def _ada_fwd(conds, w_ada, b_ada_shard):
    r, d = conds.shape
    n = w_ada.shape[1]
    tn = _tile(n, 512)

    def body(c_ref, w_ref, b_ref, o_ref):
        s = _silu(c_ref[...]).astype(BF16)
        o_ref[...] = jnp.dot(s, w_ref[...].astype(BF16), preferred_element_type=F32) + b_ref[...]

    return _pcall(
        body,
        name="ada_fwd",
        out_shape=jax.ShapeDtypeStruct((r, n), F32),
        grid=(n // tn,),
        in_specs=[pl.BlockSpec((r, d), lambda j: (0, 0)), pl.BlockSpec((d, tn), lambda j: (0, j)),
                  pl.BlockSpec((1, tn), lambda j: (0, j))],
        out_specs=pl.BlockSpec((r, tn), lambda j: (0, j)),
        compiler_params=_cparams(("parallel",)),
    )(conds, w_ada, b_ada_shard)


def _cctx_partial(da16_shard, w_ada, c_ctx_row):
    d, n = w_ada.shape
    td = _tile(d, 512)

    def body(g_ref, w_ref, c_ref, o_ref):
        ds = lax.dot_general(g_ref[8:16, :].astype(BF16), w_ref[...].astype(BF16), _DIMS["nt"],
                             preferred_element_type=F32)
        cv = c_ref[...]
        sg = jax.nn.sigmoid(cv)
        o_ref[...] = ds * (sg * (1.0 + cv * (1.0 - sg)))

    return _pcall(
        body,
        name="cctx_partial",
        out_shape=jax.ShapeDtypeStruct((8, d), F32),
        grid=(d // td,),
        in_specs=[pl.BlockSpec((16, n), lambda j: (0, 0)), pl.BlockSpec((td, n), lambda j: (j, 0)),
                  pl.BlockSpec((1, td), lambda j: (0, j))],
        out_specs=pl.BlockSpec((8, td), lambda j: (0, j)),
        compiler_params=_cparams(("parallel",)),
    )(da16_shard, w_ada, c_ctx_row)


def _sum_parts(parts):
    p, _, n = parts.shape

    def body(p_ref, o_ref):
        acc = p_ref[0]
        for s in range(1, p):
            acc = acc + p_ref[s]
        o_ref[...] = acc

    return _pcall(
        body,
        name="sum_parts",
        out_shape=jax.ShapeDtypeStruct((1, n), F32),
        in_specs=[pl.BlockSpec(memory_space=pltpu.VMEM)],
        out_specs=pl.BlockSpec(memory_space=pltpu.VMEM),
    )(parts)


def _adam_math(w, g, m, v):
    m2 = ADAM_B1 * m + (1.0 - ADAM_B1) * g
    v2 = ADAM_B2 * v + (1.0 - ADAM_B2) * jnp.square(g)
    m_hat = m2 / (1.0 - ADAM_B1 ** ADAM_STEP)
    v_hat = v2 / (1.0 - ADAM_B2 ** ADAM_STEP)
    delta = -ADAM_LR * (m_hat / (jnp.sqrt(v_hat) + ADAM_EPS) + ADAM_WD * w)
    return delta, m2, v2


def _adamw(parts, w, m, v, name):
    p, r, c = parts.shape
    rb = _tile(r, max(8, (1 << 20) // (4 * c) // 8 * 8), 8)

    def body(p_ref, w_ref, m_ref, v_ref, g_ref, d_ref, m2_ref, v2_ref):
        g = p_ref[0].astype(F32)
        for s in range(1, p):
            g = g + p_ref[s].astype(F32)
        g_ref[...] = g
        d_ref[...], m2_ref[...], v2_ref[...] = _adam_math(w_ref[...], g, m_ref[...], v_ref[...])

    if w.ndim == 3:
        row = pl.BlockSpec((None, rb, c), lambda i: (0, i, 0))
    else:
        row = pl.BlockSpec((rb, c), lambda i: (i, 0))
    return _pcall(
        body,
        name=name,
        out_shape=[jax.ShapeDtypeStruct(w.shape, F32)] * 4,
        grid=(r // rb,),
        in_specs=[pl.BlockSpec((p, rb, c), lambda i: (0, i, 0)), row, row, row],
        out_specs=[row] * 4,
        compiler_params=_cparams(("parallel",)),
    )(parts, w, m, v)


def _adamw_ada(conds, da16, w, m, v):
    d, n = w.shape
    rb = _tile(d, 256, LANE)

    def body(s_ref, da_ref, w_ref, m_ref, v_ref, g_ref, d_ref, m2_ref, v2_ref):
        g = lax.dot_general(_silu(s_ref[...]).astype(BF16), da_ref[...].astype(BF16), _DIMS["tn"],
                            preferred_element_type=F32)
        g_ref[...] = g
        d_ref[...], m2_ref[...], v2_ref[...] = _adam_math(w_ref[...], g, m_ref[...], v_ref[...])

    row = pl.BlockSpec((rb, n), lambda i: (i, 0))
    return _pcall(
        body,
        name="adamw_w_ada",
        out_shape=[jax.ShapeDtypeStruct((d, n), F32)] * 4,
        grid=(d // rb,),
        in_specs=[pl.BlockSpec((16, rb), lambda i: (0, i)), pl.BlockSpec((16, n), lambda i: (0, 0)), row, row, row],
        out_specs=[row] * 4,
        compiler_params=_cparams(("parallel",)),
    )(conds, da16, w, m, v)


def _cast_bf16(a, name):
    _, r, c = a.shape
    rb = _tile(r, 512, 8)

    def body(a_ref, o_ref):
        o_ref[...] = a_ref[...].astype(BF16)

    return _pcall(body, name=name, out_shape=jax.ShapeDtypeStruct((r, c), BF16), grid=(r // rb,),
                  in_specs=[pl.BlockSpec((None, rb, c), lambda i: (0, i, 0))],
                  out_specs=pl.BlockSpec((rb, c), lambda i: (i, 0)), compiler_params=_cparams(("parallel",)))(a)


def _rope_tabs(t, rot):
    half, q = rot // 2, rot // 4
    n_rows = t // GRID_W
    row = jnp.repeat(jnp.arange(n_rows, dtype=F32), GRID_W)
    col = jnp.tile(jnp.arange(GRID_W, dtype=F32), n_rows)
    inv_freq = ROPE_THETA ** (-jnp.arange(0, half, 2, dtype=F32) / half)
    ang = jnp.concatenate([row[:, None] * inv_freq, col[:, None] * inv_freq], axis=-1)
    cos, sin = jnp.cos(ang), jnp.sin(ang)
    c0, c1, s0, s1 = cos[:, :q], cos[:, q:], sin[:, :q], sin[:, q:]
    z = jnp.zeros_like(s0)
    return (jnp.concatenate([c0, c0, c1, c1], -1), jnp.concatenate([-s0, z, -s1, z], -1),
            jnp.concatenate([z, s0, z, s1], -1))


def _pad_cols(a, left, total, fill=0.0):
    return jnp.pad(a, ((0, 0), (left, total - left - a.shape[1])), constant_values=fill)


def _with_ctx_rows(tab, tc, fill):
    return jnp.concatenate([tab, jnp.full((tc, tab.shape[1]), fill, F32)], axis=0)


def kernel(x, c, ctx, c_ctx, w_ada, b_ada, norm1_g, w_in, mla_q_norm_g, w_q_up, mla_kv_norm_g, w_kv_up, gqa_q_norm_g, gqa_k_norm_g, w_br_a, w_br_b, w_out, norm2_g, w_up, conv_w, conv_b, w_down, final_norm_g, loss_target, m_c_ctx, m_w_ada, m_b_ada, m_norm1_g, m_w_in, m_mla_q_norm_g, m_w_q_up, m_mla_kv_norm_g, m_w_kv_up, m_gqa_q_norm_g, m_gqa_k_norm_g, m_w_br_a, m_w_br_b, m_w_out, m_norm2_g, m_w_up, m_conv_w, m_conv_b, m_w_down, m_final_norm_g, v_c_ctx, v_w_ada, v_b_ada, v_norm1_g, v_w_in, v_mla_q_norm_g, v_w_q_up, v_mla_kv_norm_g, v_w_kv_up, v_gqa_q_norm_g, v_gqa_k_norm_g, v_w_br_a, v_w_br_b, v_w_out, v_norm2_g, v_w_up, v_conv_w, v_conv_b, v_w_down, v_final_norm_g):
    weights = dict(c_ctx=c_ctx, w_ada=w_ada, b_ada=b_ada, norm1_g=norm1_g, w_in=w_in, mla_q_norm_g=mla_q_norm_g,
                   w_q_up=w_q_up, mla_kv_norm_g=mla_kv_norm_g, w_kv_up=w_kv_up, gqa_q_norm_g=gqa_q_norm_g,
                   gqa_k_norm_g=gqa_k_norm_g, w_br_a=w_br_a, w_br_b=w_br_b, w_out=w_out, norm2_g=norm2_g, w_up=w_up,
                   conv_w=conv_w, conv_b=conv_b, w_down=w_down, final_norm_g=final_norm_g)
    mom_m = dict(c_ctx=m_c_ctx, w_ada=m_w_ada, b_ada=m_b_ada, norm1_g=m_norm1_g, w_in=m_w_in, mla_q_norm_g=m_mla_q_norm_g,
                 w_q_up=m_w_q_up, mla_kv_norm_g=m_mla_kv_norm_g, w_kv_up=m_w_kv_up, gqa_q_norm_g=m_gqa_q_norm_g,
                 gqa_k_norm_g=m_gqa_k_norm_g, w_br_a=m_w_br_a, w_br_b=m_w_br_b, w_out=m_w_out, norm2_g=m_norm2_g,
                 w_up=m_w_up, conv_w=m_conv_w, conv_b=m_conv_b, w_down=m_w_down, final_norm_g=m_final_norm_g)
    mom_v = dict(c_ctx=v_c_ctx, w_ada=v_w_ada, b_ada=v_b_ada, norm1_g=v_norm1_g, w_in=v_w_in, mla_q_norm_g=v_mla_q_norm_g,
                 w_q_up=v_w_q_up, mla_kv_norm_g=v_mla_kv_norm_g, w_kv_up=v_w_kv_up, gqa_q_norm_g=v_gqa_q_norm_g,
                 gqa_k_norm_g=v_gqa_k_norm_g, w_br_a=v_w_br_a, w_br_b=v_w_br_b, w_out=v_w_out, norm2_g=v_norm2_g,
                 w_up=v_w_up, conv_w=v_conv_w, conv_b=v_conv_b, w_down=v_w_down, final_norm_g=v_final_norm_g)
    order = list(weights)

    my_idx = 4 * lax.axis_index("x") + 2 * lax.axis_index("y") + lax.axis_index("c")
    xs, cts, tgt = x[0], ctx[0], loss_target[0]
    t, d = xs.shape
    tc = cts.shape[0]
    ta = t + tc
    kvl, ql = MLA_KV_LORA, MLA_Q_LORA
    nb = GQA_KV_HEADS * GQA_HEAD_DIM
    hb = GQA_HEADS * GQA_HEAD_DIM
    ha = MLA_HEADS
    f2 = w_up.shape[2] * N_DEV
    ff = f2 // 2

    big = ["w_in", "w_q_up", "w_kv_up", "w_br_a", "w_br_b", "w_out", "w_up", "w_down"]
    nw = len(big)
    del nw
    _ORDER_AFTER.clear()
    shards = {n: _cast_bf16(weights[n], "cast_" + n) for n in big}
    c_idx = jnp.reshape(lax.axis_index("c"), (1,)).astype(jnp.int32)

    def gather_start(names, dep):
        shs = [shards[n] for n in names]
        land = [lax.empty((N_DEV,) + s.shape, BF16) for s in shs]
        if dep is not None:
            _after(dep)
        s, r, arrs, tok = _split_start("gather_ici_start_" + names[0], shs + land, _gather_ici_copies(len(names)),
                                       4 * len(names))
        return dict(names=names, s=s, r=r, arrs=arrs, tok=tok)

    def gather_pass(g, after):
        n = len(g["names"])
        arrs = _split_wait("gather_ici_wait_" + g["names"][0], g["s"], g["r"], g["arrs"], _gather_ici_copies(n), after)
        s, r, bufs, tok = _split_start("gather_pass_start_" + g["names"][0], arrs[n:], _gather_pass_copies(n), 3 * n)
        g.update(s2=s, r2=r, bufs=bufs)
        return tok

    def gather_relay(g, after):
        n = len(g["names"])
        bufs = _split_wait("gather_pass_wait_" + g["names"][0], g["s2"], g["r2"], g["bufs"], _gather_pass_copies(n), after)
        s, r, bufs, tok = _split_start("gather_d2d_start_" + g["names"][0], bufs, _gather_d2d_copies(n), n)
        g.update(s3=s, r3=r, bufs=bufs)
        return tok

    def gather_finish(g, after):
        n = len(g["names"])
        bufs = _split_wait("gather_d2d_wait_" + g["names"][0], g["s3"], g["r3"], g["bufs"], _gather_d2d_copies(n), after)
        return dict(zip(g["names"], bufs))

    c_all, cw_all = _all_gather([jnp.pad(c, ((0, 7), (0, 0))), jnp.pad(conv_w[0], ((0, 5), (0, 0)))], "gather_cond")
    conv_w_f = jnp.transpose(cw_all[:, :3, :], (1, 0, 2)).reshape(3, f2)
    conds = jnp.concatenate([c_all[:, 0, :], c_ctx[None, :], jnp.zeros((7, d), F32)], axis=0)
    ncol = w_ada.shape[2]
    b_shard = lax.dynamic_slice_in_dim(b_ada, my_idx * ncol, ncol, axis=1)
    ada_shard = _ada_fwd(conds, w_ada[0], b_shard)
    (ada_all,) = _all_gather([ada_shard], "gather_ada")
    ada = jnp.transpose(ada_all, (1, 0, 2)).reshape(16, N_DEV * ncol)
    lat = lax.dynamic_slice_in_dim(ada, my_idx, 1, axis=0).reshape(6, d)
    cxt = ada[8].reshape(6, d)
    zero2 = jnp.zeros((2, d), F32)
    mods1 = jnp.concatenate([lat[0:2], cxt[0:2], jnp.zeros((4, d), F32)], axis=0)
    mods2 = jnp.concatenate([lat[2:3], lat[3:4], lat[4:5], jnp.zeros((5, d), F32)], axis=0)
    mods2b = jnp.concatenate([lat[2:3], lat[4:5], jnp.zeros((6, d), F32)], axis=0)
    mods3 = jnp.concatenate([lat[5:6], jnp.zeros((7, d), F32)], axis=0)
    del zero2

    g0 = gather_start(["w_in"], ada_all)
    g1 = gather_start(["w_q_up", "w_kv_up", "w_br_a", "w_br_b", "w_out"], g0["tok"])
    g2 = gather_start(["w_up"], g1["tok"])
    g3 = gather_start(["w_down"], g2["tok"])

    ca, s1a, s2a = _rope_tabs(t, MLA_ROPE)
    cb_, s1b, s2b = _rope_tabs(t, GQA_HEAD_DIM)
    q_tabs_a = (_pad_cols(jnp.concatenate([jnp.ones((t, MLA_NOPE), F32), ca], 1), 0, MLA_SLOT),
                _pad_cols(s1a, MLA_NOPE, MLA_SLOT), _pad_cols(s2a, MLA_NOPE, MLA_SLOT))
    q_tabs_b = (cb_, s1b, s2b)
    k_tabs = (_with_ctx_rows(_pad_cols(ca, 0, LANE), tc, 1.0), _with_ctx_rows(_pad_cols(s1a, 0, LANE), tc, 0.0),
              _with_ctx_rows(_pad_cols(s2a, 0, LANE), tc, 0.0),
              _with_ctx_rows(cb_, tc, 1.0), _with_ctx_rows(s1b, tc, 0.0), _with_ctx_rows(s2b, tc, 0.0))

    def cols_full(g):
        return jnp.transpose(g, (1, 0, 2)).reshape(g.shape[1], N_DEV * g.shape[2])

    _after(gather_pass(g0, mods1))
    z_all = _norm_mod_fwd(cts, xs, norm1_g, mods1)
    gathered = gather_finish(g0, gather_relay(g0, z_all))
    w_in_f = cols_full(gathered["w_in"])
    o_kpe, o_kb, o_vb = kvl, kvl + MLA_ROPE, kvl + MLA_ROPE + nb
    o_q = o_vb + nb
    o_g = o_q + ql + hb
    wkv_w = kvl + 2 * nb + LANE
    w_kv_p = jnp.concatenate([w_in_f[:, :kvl], w_in_f[:, o_kb:o_q], w_in_f[:, o_kpe:o_kb],
                              jnp.zeros((d, LANE - MLA_ROPE), BF16)], axis=1)
    q_w = ql + hb
    q_pad = (-q_w) % 512 if d >= 512 else (-q_w) % d
    gate_blk = (q_w + q_pad) // d
    assert (q_w + q_pad) % d == 0
    w_qg_p = jnp.concatenate([w_in_f[:, o_q:o_g], jnp.zeros((d, q_pad), BF16), w_in_f[:, o_g:]], axis=1)

    kv_all = _mm(z_all, w_kv_p, "nn", F32, "proj_kv", tm=1152, tn=wkv_w)
    qg = _mm(z_all, w_qg_p, "nn", F32, "proj_qg", tm=1024, tn=1024, rows=t)
    _after(gather_pass(g1, qg), gather_pass(g2, qg))
    kin, k_b, v_b = _key_prep_fwd(kv_all, mla_kv_norm_g, gqa_k_norm_g, k_tabs)
    sc_a = float((MLA_NOPE + MLA_ROPE) ** -0.5) * LOG2E
    sc_b = float(GQA_HEAD_DIM ** -0.5) * LOG2E
    cqn, q_b = _q_prep_fwd(qg, mla_q_norm_g, gqa_q_norm_g, q_tabs_b, sc_b)
    gathered.update(gather_finish(g1, gather_relay(g1, q_b)))

    wq_f = cols_full(gathered["w_q_up"]).reshape(ql, ha, MLA_NOPE + MLA_ROPE)
    wq_ext = jnp.pad(wq_f, ((0, 0), (0, 0), (0, MLA_SLOT - MLA_NOPE - MLA_ROPE))).reshape(ql, ha * MLA_SLOT)
    wkv_f = cols_full(gathered["w_kv_up"]).reshape(kvl, ha, MLA_NOPE + MLA_V)
    wk_slots = jnp.pad(wkv_f[:, :, :MLA_NOPE], ((0, 0), (0, 0), (0, MLA_SLOT - MLA_NOPE))).reshape(kvl, ha * MLA_SLOT)
    wv_cols = wkv_f[:, :, MLA_NOPE:].reshape(kvl, ha * MLA_V)
    e_slot = jnp.pad(jnp.eye(MLA_ROPE, dtype=BF16),
                     ((0, LANE - MLA_ROPE), (MLA_NOPE, MLA_SLOT - MLA_NOPE - MLA_ROPE)))
    e_rows = jnp.concatenate([jnp.tile(e_slot, (1, ha)), jnp.zeros((LANE, ha * MLA_V), BF16)], axis=1)
    wkv_ext = jnp.concatenate([jnp.concatenate([wk_slots, wv_cols], axis=1), e_rows], axis=0)
    w_bra = cols_full(gathered["w_br_a"])
    w_brb = cols_full(gathered["w_br_b"])
    w_out_f = gathered["w_out"].reshape(d, d)

    kv_a = _mm(kin, wkv_ext, "nn", BF16, "kv_up", tm=1152, tn=1024)
    qa_raw = _mm(cqn, wq_ext, "nn", F32, "q_up", tm=1024, tn=1024)
    q_a = _rope_a(qa_raw, q_tabs_a, False, BF16, "rope_q_fwd", sc_a)
    att_a = dict(hq=ha, hkv=ha, dk=MLA_SLOT, dv=MLA_V, k_blk0=0, v_blk0=ha * MLA_SLOT // MLA_V)
    att_b = dict(hq=GQA_HEADS, hkv=GQA_KV_HEADS, dk=GQA_HEAD_DIM, dv=GQA_HEAD_DIM, k_blk0=0, v_blk0=0)
    o_a, lse_a = _attention_fwd(q_a, kv_a, kv_a, name="attn_a_fwd", **att_a)
    o_b, lse_b = _attention_fwd(q_b, k_b, v_b, name="attn_b_fwd", **att_b)
    _after(gather_relay(g2, o_b), gather_pass(g3, o_b))
    pa = _mm(o_a, w_bra, "nn", BF16, "br_a", tm=1024, tn=1024)
    pb = _mm(o_b, w_brb, "nn", BF16, "br_b", tm=1024, tn=1024)
    merged = _merge_fwd(pa, pb, qg, gate_blk)
    attn = _mm(merged, w_out_f, "nn", F32, "w_out", tm=1024, tn=1024)
    x1, z2 = _resid_norm_mod(xs, attn, norm2_g, mods2, "resid_norm2_fwd")
    w_up3 = gather_finish(g2, z2)["w_up"]
    _after(gather_relay(g3, z2))
    u = _mm_up_fwd(z2, w_up3, "w_up")
    w_down_f = gather_finish(g3, u)["w_down"].reshape(ff, d)
    h, uc = _conv_fwd(u, conv_w_f, conv_b)
    ffn = _mm(h, w_down_f, "nn", F32, "w_down", tm=1024, tn=1024, tk=2816)

    def to_shards(g):
        return jnp.transpose(g.reshape(g.shape[0], N_DEV, g.shape[1] // N_DEV), (1, 0, 2))

    def reduce_start(tag, names, sends):
        n = len(sends)
        land = [lax.empty((4,) + s.shape[1:], s.dtype) for s in sends]
        s, r, arrs, tok = _split_start("reduce_d2d_start_" + tag, sends + land, _reduce_d2d_copies(n), 4 * n)
        return dict(tag=tag, names=names, s=s, r=r, arrs=arrs, tok=tok)

    def reduce_relay(g, after):
        n = len(g["names"])
        arrs = _split_wait("reduce_d2d_wait_" + g["tag"], g["s"], g["r"], g["arrs"], _reduce_d2d_copies(n), after)
        sums = [_pair_sum(arrs[a], arrs[n + a], c_idx, "pair_sum_" + g["names"][a]) for a in range(n)]
        land = [lax.empty(s.shape, s.dtype) for s in sums]
        s, r, arrs2, tok = _split_start("reduce_ici_start_" + g["tag"], sums + land, _reduce_ici_copies(n), 4 * n)
        g.update(s2=s, r2=r, arrs2=arrs2)
        return tok

    def reduce_finish(g, after):
        n = len(g["names"])
        arrs2 = _split_wait("reduce_ici_wait_" + g["tag"], g["s2"], g["r2"], g["arrs2"], _reduce_ici_copies(n), after)
        return dict(zip(g["names"], arrs2[n:]))

    dx2, dffn, st_fin = _final_loss(x1, ffn, final_norm_g[None, :], mods3, tgt)
    loss = lax.psum(st_fin[3, 0], MESH_AXES)
    dh = _mm(dffn, w_down_f, "nt", BF16, "d_h", tm=1024, tn=1024)
    g_w_down = _mm(h, dffn, "tn", BF16, "g_w_down", tm=512, tn=1024)
    r_down = reduce_start("down", ["w_down"], [g_w_down.reshape(N_DEV, ff // N_DEV, d)])
    _after(r_down["tok"])
    du3, dcw, dcb = _conv_bwd(u, uc, conv_w_f, dh)
    dz2 = _mm_up_dz(du3, w_up3, "d_z2")
    g_w_up = _mm_up_gw(z2, du3, N_DEV, "g_w_up")
    g_conv_w = jnp.concatenate([dcw[0], dcw[1]], axis=1)
    tok = reduce_relay(r_down, g_w_up)
    _after(tok)
    r_up = reduce_start("up", ["w_up", "conv_w"], [g_w_up, to_shards(jnp.pad(g_conv_w, ((0, 5), (0, 0))))])
    _after(tok, r_up["tok"])
    dx1, dattn, st_n2 = _norm2_bwd(x1, attn, norm2_g, mods2b, dz2, dx2)
    dmerged = _mm(dattn, w_out_f, "nt", BF16, "d_merged", tm=1024, tn=1024)
    g_w_out = _mm(merged, dattn, "tn", BF16, "g_w_out", tm=1024, tn=1024)
    dpa, dpb, dgates = _merge_bwd(dmerged, pa, pb, qg, gate_blk)
    do_a = _mm(dpa, w_bra, "nt", BF16, "d_o_a", tm=1024, tn=1024)
    do_b = _mm(dpb, w_brb, "nt", BF16, "d_o_b", tm=1024, tn=1024)
    g_w_bra = _mm(o_a, dpa, "tn", BF16, "g_w_br_a", tm=1024, tn=1024)
    g_w_brb = _mm(o_b, dpb, "tn", BF16, "g_w_br_b", tm=1024, tn=1024)
    tok = reduce_relay(r_up, g_w_brb)
    _after(tok)
    r_out = reduce_start("out", ["w_out", "w_br_a", "w_br_b"],
                         [g_w_out.reshape(N_DEV, d // N_DEV, d), to_shards(g_w_bra), to_shards(g_w_brb)])
    _after(tok, r_out["tok"])
    dq_a, dk_a, dv_a = _attention_bwd(q_a, kv_a, kv_a, do_a, lse_a, name="attn_a_bwd", **att_a)
    dq_b, dk_b, dv_b = _attention_bwd(q_b, k_b, v_b, do_b, lse_b, name="attn_b_bwd", **att_b)
    _after(reduce_relay(r_out, dv_b))
    dqa_raw = _rope_a(dq_a, q_tabs_a, True, BF16, "rope_q_bwd", sc_a * LN2)
    dcqn = _mm(dqa_raw, wq_ext, "nt", F32, "d_cqn", tm=1024, tn=ql)
    g_wq_ext = _mm(cqn, dqa_raw, "tn", BF16, "g_w_q_up", tm=ql, tn=1024)
    dq_p, st_q, st_qb = _q_prep_bwd(qg, mla_q_norm_g, gqa_q_norm_g, q_tabs_b, dcqn, dq_b, q_pad, sc_b * LN2)
    dkin = _mm_cat_nt([(dk_a, wkv_ext, 0), (dv_a, wkv_ext, ha * MLA_SLOT)], F32, "d_kin", tm=1152, tn=kvl + LANE)
    g_wkv_ext = _mm_cat_tn(kin, [dk_a, dv_a], BF16, "g_w_kv_up", tm=kvl + LANE, tn=min(1024, ha * MLA_V))
    dkv_p, st_kv, st_kb = _key_prep_bwd(kv_all, mla_kv_norm_g, gqa_k_norm_g, k_tabs, dkin, dk_b, dv_b)
    g_wq = g_wq_ext.reshape(ql, ha, MLA_SLOT)[:, :, :MLA_NOPE + MLA_ROPE].reshape(ql, ha * (MLA_NOPE + MLA_ROPE))
    g_wkv = jnp.concatenate([g_wkv_ext[:kvl, :ha * MLA_SLOT].reshape(kvl, ha, MLA_SLOT)[:, :, :MLA_NOPE],
                             g_wkv_ext[:kvl, ha * MLA_SLOT:].reshape(kvl, ha, MLA_V)], axis=2).reshape(kvl, ha * (MLA_NOPE + MLA_V))
    r_qkv = reduce_start("qkv", ["w_q_up", "w_kv_up"], [to_shards(g_wq), to_shards(g_wkv)])
    _after(r_qkv["tok"])
    g_wkv_p = _mm(z_all, dkv_p, "tn", BF16, "g_w_in_kv", tm=1024, tn=wkv_w)
    g_wqg_p = _mm_cat_tn(z_all, [dq_p, dgates], BF16, "g_w_in_qg", tm=1024, tn=min(1024, d), rows=t)
    g_w_in = jnp.concatenate([g_wkv_p[:, :kvl], g_wkv_p[:, kvl + 2 * nb:kvl + 2 * nb + MLA_ROPE],
                              g_wkv_p[:, kvl:kvl + 2 * nb], g_wqg_p[:, :q_w], g_wqg_p[:, q_w + q_pad:]], axis=1)
    r_in = reduce_start("in", ["w_in"], [to_shards(g_w_in)])
    _after(r_in["tok"])
    qw_p = q_w + q_pad
    dz_lat = _mm_sum_nt([(dq_p, 0, w_qg_p, 0, qw_p), (dgates, 0, w_qg_p, qw_p, d), (dgates, d, w_qg_p, qw_p + d, d),
                         (dkv_p, 0, w_kv_p, 0, wkv_w)], F32, "d_z_lat", rows=t)
    dz_ctx = _mm(dkv_p, w_kv_p, "nt", F32, "d_z_ctx", tm=min(ROW_BLOCK, tc), tn=1024, a_row_off=t)
    tok_q = reduce_relay(r_qkv, dz_ctx)
    _after(tok_q)
    grad_x, st_n1 = _norm1_bwd(cts, xs, norm1_g, mods1, dz_ctx, dz_lat, dx1)

    res = {}

    def upd(nm, parts):
        wv, mv, vv = weights[nm], mom_m[nm], mom_v[nm]
        if wv.ndim == 1:
            wv, mv, vv = (a.reshape(1, -1) for a in (wv, mv, vv))
        outs = _adamw(parts, wv, mv, vv, "adamw_" + nm)
        res[nm] = [o_.reshape(weights[nm].shape) for o_ in outs]

    d_lat = jnp.concatenate([st_n1[0], st_n1[1], st_n2[3], st_n2[0], st_n2[1], st_fin[1]])
    d_cxt = jnp.concatenate([st_n1[3], st_n1[4], jnp.zeros((4 * d,), F32)])
    small = jnp.concatenate([d_lat, d_cxt, st_n1[2], st_q[0], st_kv[0], st_qb[0], st_kb[0], st_n2[2],
                             jnp.concatenate([dcb[0, 0], dcb[1, 0]]), st_fin[0]])
    n_small = small.shape[0]
    pad_small = (-n_small) % LANE
    (small_all,) = _all_gather([jnp.pad(small, (0, pad_small)).reshape(1, -1)], "gather_small")
    offs = {}
    o = 0
    for nm, ln in (("d_lat", 6 * d), ("d_cxt", 6 * d), ("norm1_g", d), ("mla_q_norm_g", ql), ("mla_kv_norm_g", kvl),
                   ("gqa_q_norm_g", GQA_HEAD_DIM), ("gqa_k_norm_g", GQA_HEAD_DIM), ("norm2_g", d), ("conv_b", f2),
                   ("final_norm_g", d)):
        offs[nm] = (o, ln)
        o += ln

    def part(nm):
        a, ln = offs[nm]
        return small_all[:, :, a:a + ln]

    d_lat_all = part("d_lat")[:, 0, :]
    d_cxt_sum = _sum_parts(part("d_cxt"))
    da16 = jnp.concatenate([d_lat_all, d_cxt_sum, jnp.zeros((7, 6 * d), F32)], axis=0)
    da16_shard = lax.dynamic_slice_in_dim(da16, my_idx * ncol, ncol, axis=1)
    cc_part = _cctx_partial(da16_shard, w_ada[0], c_ctx[None, :])
    (cc_all,) = _all_gather([cc_part], "gather_cctx")
    cc_parts = cc_all[:, 0:1, :]
    tok_i = reduce_relay(r_in, cc_all)

    _after(tok_i)
    for nm in ("norm1_g", "mla_q_norm_g", "mla_kv_norm_g", "gqa_q_norm_g", "gqa_k_norm_g", "norm2_g", "conv_b",
               "final_norm_g"):
        upd(nm, part(nm))
    upd("c_ctx", cc_parts)
    b_parts = jnp.concatenate([d_lat_all[:, None, :], d_cxt_sum[None]], axis=0)
    upd("b_ada", b_parts)
    _after(tok_i)
    outs = _adamw_ada(conds, da16_shard, w_ada[0], m_w_ada[0], v_w_ada[0])
    res["w_ada"] = [o_[None] for o_ in outs]
    last = outs[0]
    for grp in (r_down, r_up, r_out, r_qkv, r_in):
        recv = reduce_finish(grp, last)
        for nm in grp["names"]:
            upd(nm, recv[nm][:, :3, :] if nm == "conv_w" else recv[nm])
            last = res[nm][0]

    return (loss, grad_x[None], *[res[n][0] for n in order], *[res[n][1] for n in order],
            *[res[n][2] for n in order], *[res[n][3] for n in order])
```

```python
import functools

import jax
import jax.numpy as jnp
from jax import lax
from jax.experimental import pallas as pl
from jax.experimental.pallas import tpu as pltpu

F32 = jnp.float32
BF16 = jnp.bfloat16

GRID_W = 64
ROPE_THETA = 10000.0
NORM_EPS = 1e-6
MLA_HEADS = 8
MLA_Q_LORA = 768
MLA_KV_LORA = 512
MLA_NOPE = 128
MLA_ROPE = 64
MLA_V = 128
GQA_HEADS = 8
GQA_KV_HEADS = 2
GQA_HEAD_DIM = 128
ADAM_LR = 0.001
ADAM_B1 = 0.9
ADAM_B2 = 0.999
ADAM_EPS = 1e-08
ADAM_WD = 0.01
ADAM_STEP = 10

N_DEV = 8
MESH_AXES = ("x", "y", "c")
LANE = 128
MLA_SLOT = 2 * LANE
VMEM_LIMIT = 56 * 1024 * 1024
ROW_BLOCK = 256
ATT_Q_BLOCK = 512
ATT_Q_BLOCK_FWD = 512
LN2 = 0.6931471805599453
LOG2E = 1.4426950408889634
MESH_ID = pl.DeviceIdType.MESH


def _tile(n, pref, align=LANE):
    if n <= pref:
        return n
    best = None
    t = align
    while t <= pref:
        if n % t == 0:
            best = t
        t += align
    assert best is not None, (n, pref, align)
    return best


def _cparams(sem=None):
    return pltpu.CompilerParams(dimension_semantics=sem, vmem_limit_bytes=VMEM_LIMIT)


_ORDER_AFTER = []


def _after(*arrays):
    _ORDER_AFTER.extend(arrays)


def _pcall(body, *, in_specs, **kw):
    deps = tuple(_ORDER_AFTER)
    _ORDER_AFTER.clear()
    if not deps:
        return pl.pallas_call(body, in_specs=in_specs, **kw)
    n_in, n_dep = len(in_specs), len(deps)

    def with_deps(*refs):
        body(*refs[:n_in], *refs[n_in + n_dep:])

    call = pl.pallas_call(with_deps, in_specs=list(in_specs) + [pl.BlockSpec(memory_space=pl.ANY)] * n_dep, **kw)
    return lambda *args: call(*args, *deps)


def _all_gather(arrs, name):
    n = len(arrs)

    def body(*refs):
        ins = refs[:n]
        outs = refs[n:2 * n]
        send_sems, recv_sems, local_sems = refs[2 * n:]
        x, y, c = lax.axis_index("x"), lax.axis_index("y"), lax.axis_index("c")
        me, sibling = (x, y, c), (x, y, 1 - c)
        chips = [(1 - x, y), (x, 1 - y), (1 - x, 1 - y)]

        def rows(a, dev):
            px, py, pc = dev
            return outs[a].at[4 * px + 2 * py + pc]

        def copy(a, k, block, to, src=None):
            return pltpu.make_async_remote_copy(
                src_ref=rows(a, block) if src is None else src,
                dst_ref=rows(a, block),
                send_sem=send_sems.at[7 * a + k],
                recv_sem=recv_sems.at[7 * a + k],
                device_id=to,
                device_id_type=MESH_ID,
            )

        mine = [pltpu.make_async_copy(ins[a], rows(a, me), local_sems.at[a]) for a in range(n)]
        for cp in mine:
            cp.start()
        first = []
        for a in range(n):
            first.append(copy(a, 0, me, sibling, src=ins[a]))
            first += [copy(a, 1 + j, me, (*chip, c), src=ins[a]) for j, chip in enumerate(chips)]
        for cp in first:
            cp.start()
        passed = []
        for j, chip in enumerate(chips):
            for a in range(n):
                copy(a, 1 + j, (*chip, c), me).wait_recv()
                fwd = copy(a, 4 + j, (*chip, c), sibling)
                fwd.start()
                passed.append(fwd)
        for a in range(n):
            copy(a, 0, sibling, me).wait_recv()
            for j, chip in enumerate(chips):
                copy(a, 4 + j, (*chip, 1 - c), me).wait_recv()
        for cp in first + passed:
            cp.wait_send()
        for cp in mine:
            cp.wait()

    any_spec = pl.BlockSpec(memory_space=pl.ANY)
    outs = _pcall(
        body,
        name=name,
        out_shape=[jax.ShapeDtypeStruct((N_DEV,) + a.shape, a.dtype) for a in arrs],
        in_specs=[any_spec] * n,
        out_specs=[any_spec] * n,
        scratch_shapes=[
            pltpu.SemaphoreType.DMA((7 * n,)),
            pltpu.SemaphoreType.DMA((7 * n,)),
            pltpu.SemaphoreType.DMA((n,)),
        ],
    )(*arrs)
    return list(outs)


def _all_to_all(arrs, name):
    n = len(arrs)

    def body(*refs):
        ins = refs[:n]
        outs = refs[n:2 * n]
        send_sems, recv_sems, local_sems = refs[2 * n:]
        x, y, c = lax.axis_index("x"), lax.axis_index("y"), lax.axis_index("c")
        my_idx = 4 * x + 2 * y + c

        def peer(k):
            fx, fy, fc = (k >> 2) & 1, (k >> 1) & 1, k & 1
            return (x ^ fx if fx else x, y ^ fy if fy else y, c ^ fc if fc else c)

        def copy(a, k):
            px, py, pc = peer(k)
            return pltpu.make_async_remote_copy(
                src_ref=ins[a].at[4 * px + 2 * py + pc],
                dst_ref=outs[a].at[my_idx],
                send_sem=send_sems.at[7 * a + k - 1],
                recv_sem=recv_sems.at[7 * a + k - 1],
                device_id=(px, py, pc),
                device_id_type=MESH_ID,
            )

        mine = [pltpu.make_async_copy(ins[a].at[my_idx], outs[a].at[my_idx], local_sems.at[a]) for a in range(n)]
        for cp in mine:
            cp.start()
        order = [1, 4, 2, 5, 3, 6, 7]
        cps = [copy(a, k) for k in order for a in range(n)]
        for cp in cps:
            cp.start()
        for cp in cps:
            cp.wait()
        for cp in mine:
            cp.wait()

    any_spec = pl.BlockSpec(memory_space=pl.ANY)
    outs = _pcall(
        body,
        name=name,
        out_shape=[jax.ShapeDtypeStruct(a.shape, a.dtype) for a in arrs],
        in_specs=[any_spec] * n,
        out_specs=[any_spec] * n,
        scratch_shapes=[
            pltpu.SemaphoreType.DMA((7 * n,)),
            pltpu.SemaphoreType.DMA((7 * n,)),
            pltpu.SemaphoreType.DMA((n,)),
        ],
    )(*arrs)
    return list(outs)


_HBM = pl.BlockSpec(memory_space=pltpu.HBM)
_SEM = pl.BlockSpec(memory_space=pltpu.SEMAPHORE)
_EFFECT = pltpu.SideEffectType.DATAFLOW_SIDE_EFFECTING


def _descriptors(copies, send_sems, recv_sems):
    descs = []
    for i, (src, dst, dev) in enumerate(copies):
        if dev is None:
            descs.append(pltpu.make_async_copy(src, dst, recv_sems.at[i]))
        else:
            descs.append(pltpu.make_async_remote_copy(src_ref=src, dst_ref=dst, send_sem=send_sems.at[i],
                                                      recv_sem=recv_sems.at[i], device_id=dev, device_id_type=MESH_ID))
    return descs


def _split_start(name, arrays, copies_fn, n_copies):
    n = len(arrays)

    def body(*refs):
        send_sems, recv_sems = refs[n], refs[n + 1]
        token = refs[2 * n + 2]
        for dsc in _descriptors(copies_fn(refs[:n]), send_sems, recv_sems):
            dsc.start()
        token[...] = jnp.zeros_like(token)

    outs = _pcall(
        body,
        name=name,
        out_shape=(pltpu.SemaphoreType.DMA((n_copies,)), pltpu.SemaphoreType.DMA((n_copies,)),
                   *[pltpu.HBM(a.shape, a.dtype) for a in arrays], jax.ShapeDtypeStruct((8, LANE), F32)),
        in_specs=[_HBM] * n,
        out_specs=(_SEM, _SEM, *[_HBM] * n, pl.BlockSpec(memory_space=pltpu.VMEM)),
        input_output_aliases={i: 2 + i for i in range(n)},
        compiler_params=pltpu.CompilerParams(has_side_effects=_EFFECT),
    )(*[pltpu.with_memory_space_constraint(a, pltpu.HBM) for a in arrays])
    return outs[0], outs[1], list(outs[2:2 + n]), outs[2 + n]


def _split_wait(name, send_sems, recv_sems, arrays, copies_fn, after):
    n = len(arrays)

    def body(*refs):
        for dsc, (_, _, dev) in zip(_descriptors(copies_fn(refs[:n]), refs[n], refs[n + 1]), copies_fn(refs[:n])):
            if dev is None:
                dsc.wait()
            else:
                dsc.wait_send()
                dsc.wait_recv()

    outs = _pcall(
        body,
        name=name,
        out_shape=tuple(pltpu.HBM(a.shape, a.dtype) for a in arrays),
        in_specs=[_HBM] * n + [_SEM, _SEM, pl.BlockSpec(memory_space=pl.ANY)],
        out_specs=tuple([_HBM] * n),
        input_output_aliases={i: i for i in range(n)},
        compiler_params=pltpu.CompilerParams(has_side_effects=_EFFECT),
    )(*arrays, send_sems, recv_sems, after)
    return list(outs)


def _mesh_pos():
    x, y, c = lax.axis_index("x"), lax.axis_index("y"), lax.axis_index("c")
    return x, y, c, [(1 - x, y), (x, 1 - y), (1 - x, 1 - y)]


def _gather_ici_copies(n):
    def copies(refs):
        x, y, c, chips = _mesh_pos()
        me = 4 * x + 2 * y + c
        out = []
        for a in range(n):
            src, buf = refs[a], refs[n + a]
            out.append((src, buf.at[me], None))
            out.append((src, buf.at[me], (x, y, 1 - c)))
            out += [(src, buf.at[me], (cx, cy, c)) for cx, cy in chips[:2]]
        return out
    return copies


def _gather_pass_copies(n):
    def copies(refs):
        x, y, c, chips = _mesh_pos()
        south = c == 0
        bx, by = jnp.where(south, 1 - x, x), jnp.where(south, y, 1 - y)
        tx, ty = jnp.where(south, x, 1 - x), jnp.where(south, 1 - y, y)
        out = []
        for a in range(n):
            rows = refs[a].at[4 * bx + 2 * by + c]
            out.append((rows, rows, (tx, ty, c)))
            for cx, cy in chips[:2]:
                rows = refs[a].at[4 * cx + 2 * cy + c]
                out.append((rows, rows, (x, y, 1 - c)))
        return out
    return copies


def _gather_d2d_copies(n):
    def copies(refs):
        x, y, c, chips = _mesh_pos()
        cx, cy = chips[2]
        out = []
        for a in range(n):
            rows = refs[a].at[4 * cx + 2 * cy + c]
            out.append((rows, rows, (x, y, 1 - c)))
        return out
    return copies


def _reduce_d2d_copies(n):
    def copies(refs):
        x, y, c, _ = _mesh_pos()
        out = []
        for a in range(n):
            for k in range(4):
                out.append((refs[a].at[2 * k + (1 - c)], refs[n + a].at[k], (x, y, 1 - c)))
        return out
    return copies


def _reduce_ici_copies(n):
    def copies(refs):
        x, y, c, chips = _mesh_pos()
        mine = 2 * x + y
        out = []
        for a in range(n):
            src, land = refs[a], refs[n + a]
            out.append((src.at[mine], land.at[mine], None))
            out += [(src.at[2 * cx + cy], land.at[mine], (cx, cy, c)) for cx, cy in chips]
        return out
    return copies


def _pair_sum(send, land, c_idx, name):
    _, r, cols = send.shape
    rb = _tile(r, max(8, (1 << 22) // (send.dtype.itemsize * cols) // 8 * 8), 8)
    dt = send.dtype

    def body(c_ref, s_ref, l_ref, o_ref):
        o_ref[...] = (s_ref[...].astype(F32) + l_ref[...].astype(F32)).astype(dt)

    return pl.pallas_call(
        body,
        name=name,
        out_shape=jax.ShapeDtypeStruct((4, r, cols), dt),
        grid_spec=pltpu.PrefetchScalarGridSpec(
            num_scalar_prefetch=1,
            grid=(4, r // rb),
            in_specs=[pl.BlockSpec((None, rb, cols), lambda k, i, c_ref: (2 * k + c_ref[0], i, 0)),
                      pl.BlockSpec((None, rb, cols), lambda k, i, c_ref: (k, i, 0))],
            out_specs=pl.BlockSpec((None, rb, cols), lambda k, i, c_ref: (k, i, 0)),
        ),
        compiler_params=_cparams(("parallel", "parallel")),
    )(c_idx, send, land)


_DIMS = {
    "nn": (((1,), (0,)), ((), ())),
    "nt": (((1,), (1,)), ((), ())),
    "tn": (((0,), (0,)), ((), ())),
}


def _mm_call(a, b, *, mode, grid, a_spec, b_spec, o_spec, out_shape, acc_shape, name):
    nk = grid[2]
    out_dtype = out_shape.dtype

    def body(a_ref, b_ref, o_ref, *scratch):
        p = lax.dot_general(a_ref[...].astype(BF16), b_ref[...].astype(BF16), _DIMS[mode],
                            preferred_element_type=F32)
        if nk == 1:
            o_ref[...] = p.astype(out_dtype)
        else:
            acc = scratch[0]
            k = pl.program_id(2)

            @pl.when(k == 0)
            def _():
                acc[...] = p

            @pl.when(k > 0)
            def _():
                acc[...] += p

            @pl.when(k == nk - 1)
            def _():
                o_ref[...] = acc[...].astype(out_dtype)

    return _pcall(
        body,
        name=name,
        out_shape=out_shape,
        grid=grid,
        in_specs=[a_spec, b_spec],
        out_specs=o_spec,
        scratch_shapes=[pltpu.VMEM(acc_shape, F32)] if nk > 1 else [],
        compiler_params=_cparams(("parallel", "parallel", "arbitrary")),
    )(a, b)


def _mm(a, b, mode, out_dtype, name, tm=512, tn=512, tk=2432, a_row_off=0, rows=None):
    if mode == "nn":
        (m, k), (k2, n) = a.shape, b.shape
    elif mode == "nt":
        (m, k), (n, k2) = a.shape, b.shape
    else:
        (k, m), (k2, n) = a.shape, b.shape
        if rows is not None:
            k = k2 = rows
    assert k == k2, (a.shape, b.shape, mode)
    if mode != "tn":
        m = (m if rows is None else rows + a_row_off) - a_row_off
    tm, tn, tk = _tile(m, tm, 8), _tile(n, tn), _tile(k, tk, 8 if mode == "tn" else LANE)
    assert a_row_off % tm == 0
    ro = a_row_off // tm
    grid = (m // tm, n // tn, k // tk)
    if mode == "tn":
        a_spec = pl.BlockSpec((tk, tm), lambda i, j, kk: (kk, i))
    else:
        a_spec = pl.BlockSpec((tm, tk), lambda i, j, kk: (i + ro, kk))
    if mode == "nt":
        b_spec = pl.BlockSpec((tn, tk), lambda i, j, kk: (j, kk))
    else:
        b_spec = pl.BlockSpec((tk, tn), lambda i, j, kk: (kk, j))
    o_spec = pl.BlockSpec((tm, tn), lambda i, j, kk: (i, j))
    return _mm_call(a, b, mode=mode, grid=grid, a_spec=a_spec, b_spec=b_spec, o_spec=o_spec,
                    out_shape=jax.ShapeDtypeStruct((m, n), out_dtype), acc_shape=(tm, tn), name=name)


def _mm_cat_nt(pieces, out_dtype, name, tm=1024, tn=1024, tk=2048, rows=None):
    m = pieces[0][0].shape[0] if rows is None else rows
    n = pieces[0][1].shape[0]
    tm, tn = _tile(m, tm, 8), _tile(n, tn)
    steps, starts, s = [], [], 0
    for a, b, off in pieces:
        kp = a.shape[1]
        tkp = _tile(kp, tk)
        assert off % tkp == 0 and b.shape[0] == n
        steps.append((tkp, kp // tkp, off // tkp))
        starts.append(s)
        s += kp // tkp
    nk = s
    npc = len(pieces)

    def body(*refs):
        o_ref, acc = refs[2 * npc], refs[2 * npc + 1]
        kk = pl.program_id(2)

        @pl.when(kk == 0)
        def _():
            acc[...] = jnp.zeros_like(acc)

        for p in range(npc):
            @pl.when((kk >= starts[p]) & (kk < starts[p] + steps[p][1]))
            def _(p=p):
                acc[...] += lax.dot_general(refs[2 * p][...].astype(BF16), refs[2 * p + 1][...].astype(BF16), _DIMS["nt"],
                                            preferred_element_type=F32)

        @pl.when(kk == nk - 1)
        def _():
            o_ref[...] = acc[...].astype(out_dtype)

    in_specs, args = [], []
    for p, (a, b, off) in enumerate(pieces):
        tkp, np_, ob = steps[p]

        def rel(kk, p=p, np_=np_):
            return jnp.clip(kk - starts[p], 0, np_ - 1)

        in_specs.append(pl.BlockSpec((tm, tkp), lambda i, j, kk, rel=rel: (i, rel(kk))))
        in_specs.append(pl.BlockSpec((tn, tkp), lambda i, j, kk, rel=rel, ob=ob: (j, ob + rel(kk))))
        args += [a, b]
    return _pcall(
        body,
        name=name,
        out_shape=jax.ShapeDtypeStruct((m, n), out_dtype),
        grid=(m // tm, n // tn, nk),
        in_specs=in_specs,
        out_specs=pl.BlockSpec((tm, tn), lambda i, j, kk: (i, j)),
        scratch_shapes=[pltpu.VMEM((tm, tn), F32)],
        compiler_params=_cparams(("parallel", "parallel", "arbitrary")),
    )(*args)


def _mm_cat_tn(a, pieces, out_dtype, name, tm=1024, tn=1024, rows=None):
    k = a.shape[0] if rows is None else rows
    m = a.shape[1]
    tm = _tile(m, tm)
    starts, s = [], 0
    for b in pieces:
        assert b.shape[1] % tn == 0
        starts.append(s)
        s += b.shape[1] // tn
    nj = s
    npc = len(pieces)

    def body(*refs):
        a_ref, o_ref = refs[0], refs[1 + npc]
        j = pl.program_id(1)
        for p in range(npc):
            @pl.when((j >= starts[p]) & (j < starts[p] + pieces[p].shape[1] // tn))
            def _(p=p):
                o_ref[...] = lax.dot_general(a_ref[...].astype(BF16), refs[1 + p][...].astype(BF16), _DIMS["tn"],
                                             preferred_element_type=F32).astype(out_dtype)

    in_specs = [pl.BlockSpec((k, tm), lambda i, j: (0, i))]
    for p, b in enumerate(pieces):
        np_ = b.shape[1] // tn
        in_specs.append(pl.BlockSpec((k, tn), lambda i, j, p=p, np_=np_: (0, jnp.clip(j - starts[p], 0, np_ - 1))))
    return _pcall(
        body,
        name=name,
        out_shape=jax.ShapeDtypeStruct((m, nj * tn), out_dtype),
        grid=(m // tm, nj),
        in_specs=in_specs,
        out_specs=pl.BlockSpec((tm, tn), lambda i, j: (i, j)),
        compiler_params=_cparams(("parallel", "arbitrary")),
    )(a, *pieces)


def _mm_up_fwd(z2, w3, name, tm=1024):
    t, d = z2.shape
    nsh, _, c = w3.shape
    tm = _tile(t, tm, 8)
    return _mm_call(z2, w3, mode="nn", grid=(t // tm, nsh, 1),
                    a_spec=pl.BlockSpec((tm, d), lambda i, j, kk: (i, 0)),
                    b_spec=pl.BlockSpec((None, d, c), lambda i, j, kk: (j, 0, 0)),
                    o_spec=pl.BlockSpec((tm, c), lambda i, j, kk: (i, j)),
                    out_shape=jax.ShapeDtypeStruct((t, nsh * c), BF16), acc_shape=(tm, c), name=name)


def _mm_up_dz(du3, w3, name, tm=512, tn=1024):
    _, t, f = du3.shape
    nsh, d, c = w3.shape
    half = nsh // 2
    assert f == half * c
    tm, tn = _tile(t, tm, 8), _tile(d, tn)

    def body(a_ref, b_ref, o_ref, acc):
        kk = pl.program_id(2)
        p = None
        for s in range(half):
            q = lax.dot_general(a_ref[:, s * c:(s + 1) * c], b_ref[s], _DIMS["nt"], preferred_element_type=F32)
            p = q if p is None else p + q

        @pl.when(kk == 0)
        def _():
            acc[...] = p

        @pl.when(kk == 1)
        def _():
            o_ref[...] = (acc[...] + p).astype(BF16)

    return _pcall(
        body,
        name=name,
        out_shape=jax.ShapeDtypeStruct((t, d), BF16),
        grid=(t // tm, d // tn, 2),
        in_specs=[pl.BlockSpec((None, tm, f), lambda i, j, kk: (kk, i, 0)),
                  pl.BlockSpec((half, tn, c), lambda i, j, kk: (kk, j, 0))],
        out_specs=pl.BlockSpec((tm, tn), lambda i, j, kk: (i, j)),
        scratch_shapes=[pltpu.VMEM((tm, tn), F32)],
        compiler_params=_cparams(("parallel", "parallel", "arbitrary")),
    )(du3, w3)


def _mm_sum_nt(pieces, out_dtype, name, tm=512, tn=512, rows=None):
    m = pieces[0][0].shape[0] if rows is None else rows
    n = pieces[0][2].shape[0]
    tm, tn = _tile(m, tm, 8), _tile(n, tn)
    npc = len(pieces)

    def body(*refs):
        p = None
        for s in range(npc):
            q = lax.dot_general(refs[2 * s][...].astype(BF16), refs[2 * s + 1][...].astype(BF16), _DIMS["nt"],
                                preferred_element_type=F32)
            p = q if p is None else p + q
        refs[2 * npc][...] = p.astype(out_dtype)

    in_specs, args = [], []
    for a, ao, b, bo, kp in pieces:
        assert ao % kp == 0 and bo % kp == 0 and b.shape[0] == n
        in_specs.append(pl.BlockSpec((tm, kp), lambda i, j, ab=ao // kp: (i, ab)))
        in_specs.append(pl.BlockSpec((tn, kp), lambda i, j, bb=bo // kp: (j, bb)))
        args += [a, b]
    return _pcall(
        body,
        name=name,
        out_shape=jax.ShapeDtypeStruct((m, n), out_dtype),
        grid=(m // tm, n // tn),
        in_specs=in_specs,
        out_specs=pl.BlockSpec((tm, tn), lambda i, j: (i, j)),
        compiler_params=_cparams(("parallel", "parallel")),
    )(*args)


def _mm_up_gw(z2, du3, nsh, name, tm=1024):
    t, d = z2.shape
    f = du3.shape[2]
    half = nsh // 2
    c = f // half
    tm = _tile(d, tm)
    return _mm_call(z2, du3, mode="tn", grid=(d // tm, nsh, 1),
                    a_spec=pl.BlockSpec((t, tm), lambda i, j, kk: (0, i)),
                    b_spec=pl.BlockSpec((None, t, c), lambda i, j, kk: (j // half, 0, j % half)),
                    o_spec=pl.BlockSpec((None, tm, c), lambda i, j, kk: (j, i, 0)),
                    out_shape=jax.ShapeDtypeStruct((nsh, d, c), BF16), acc_shape=(tm, c), name=name)


def _rms(x):
    r = lax.rsqrt(jnp.mean(x * x, axis=-1, keepdims=True) + NORM_EPS)
    return x * r, r


def _rms_bwd(dxh, xh, r):
    return r * (dxh - xh * jnp.mean(dxh * xh, axis=-1, keepdims=True))


def _colsum(v):
    return jnp.sum(v, axis=0, keepdims=True)


def _rope(v, c, s1, s2, q):
    w = v.shape[-1]
    return v * c + pltpu.roll(v, w - q, 1) * s1 + pltpu.roll(v, q, 1) * s2


def _rope_t(d, c, s1, s2, q):
    w = d.shape[-1]
    return d * c + pltpu.roll(d * s1, q, 1) + pltpu.roll(d * s2, w - q, 1)


def _norm_mod_fwd(ctx, x, gain, mods):
    tc, d = ctx.shape
    t = x.shape[0]
    rb = min(ROW_BLOCK, tc)
    nbl = t // rb

    def body(ctx_ref, x_ref, g_ref, mod_ref, z_ref):
        i = pl.program_id(0)

        def emit(src, sh, sc):
            xh, _ = _rms(src[...])
            z_ref[...] = ((xh * g_ref[...]) * (1.0 + sc) + sh).astype(BF16)

        @pl.when(i >= nbl)
        def _():
            emit(ctx_ref, mod_ref[2:3, :], mod_ref[3:4, :])

        @pl.when(i < nbl)
        def _():
            emit(x_ref, mod_ref[0:1, :], mod_ref[1:2, :])

    return _pcall(
        body,
        name="norm1_mod_fwd",
        out_shape=jax.ShapeDtypeStruct((tc + t, d), BF16),
        grid=((tc + t) // rb,),
        in_specs=[
            pl.BlockSpec((rb, d), lambda i: (jnp.maximum(i - nbl, 0), 0)),
            pl.BlockSpec((rb, d), lambda i: (jnp.minimum(i, nbl - 1), 0)),
            pl.BlockSpec((1, d), lambda i: (0, 0)),
            pl.BlockSpec((8, d), lambda i: (0, 0)),
        ],
        out_specs=pl.BlockSpec((rb, d), lambda i: (i, 0)),
        compiler_params=_cparams(("arbitrary",)),
    )(ctx, x, gain, mods)


def _norm1_bwd(ctx, x, gain, mods, dz_ctx, dz_lat, dx1):
    tc, d = ctx.shape
    t = x.shape[0]
    rb = min(ROW_BLOCK, tc)
    nbl = t // rb

    def body(ctx_ref, x_ref, g_ref, mod_ref, dzc_ref, dzl_ref, dx1_ref, gx_ref, st_ref):
        i = pl.program_id(0)

        @pl.when(i == 0)
        def _():
            st_ref[...] = jnp.zeros_like(st_ref)

        def common(src, dz, sc, row_sh, row_sc):
            xh, r = _rms(src[...])
            g = g_ref[...]
            dxn = dz * (1.0 + sc)
            st_ref[row_sh:row_sh + 1, :] += _colsum(dz)
            st_ref[row_sc:row_sc + 1, :] += _colsum(dz * (xh * g))
            st_ref[2:3, :] += _colsum(dxn * xh)
            return _rms_bwd(dxn * g, xh, r)

        @pl.when(i >= nbl)
        def _():
            common(ctx_ref, dzc_ref[...], mod_ref[3:4, :], 3, 4)

        @pl.when(i < nbl)
        def _():
            gx_ref[...] = dx1_ref[...] + common(x_ref, dzl_ref[...], mod_ref[1:2, :], 0, 1)

    lat = lambda i: (jnp.minimum(i, nbl - 1), 0)
    cix = lambda i: (jnp.maximum(i - nbl, 0), 0)
    return _pcall(
        body,
        name="norm1_mod_bwd",
        out_shape=[jax.ShapeDtypeStruct((t, d), F32), jax.ShapeDtypeStruct((8, d), F32)],
        grid=((tc + t) // rb,),
        in_specs=[
            pl.BlockSpec((rb, d), cix),
            pl.BlockSpec((rb, d), lat),
            pl.BlockSpec((1, d), lambda i: (0, 0)),
            pl.BlockSpec((8, d), lambda i: (0, 0)),
            pl.BlockSpec((rb, d), cix),
            pl.BlockSpec((rb, d), lat),
            pl.BlockSpec((rb, d), lat),
        ],
        out_specs=[pl.BlockSpec((rb, d), lat), pl.BlockSpec((8, d), lambda i: (0, 0))],
        compiler_params=_cparams(("arbitrary",)),
    )(ctx, x, gain, mods, dz_ctx, dz_lat, dx1)


def _key_prep_fwd(kv, kv_gain, kb_gain, tabs):
    ta, wkv = kv.shape
    kvl = MLA_KV_LORA
    nb = GQA_KV_HEADS * GQA_HEAD_DIM
    rb = ROW_BLOCK if ta % ROW_BLOCK == 0 else LANE
    hd = GQA_HEAD_DIM

    def body(kv_ref, g_ref, gb_ref, ca, s1a, s2a, cb, s1b, s2b, kin_ref, kb_ref, vb_ref):
        xh, _ = _rms(kv_ref[:, 0:kvl])
        kin_ref[:, 0:kvl] = (xh * g_ref[...]).astype(BF16)
        kpe = kv_ref[:, kvl + 2 * nb:kvl + 2 * nb + LANE]
        kin_ref[:, kvl:kvl + LANE] = _rope(kpe, ca[...], s1a[...], s2a[...], MLA_ROPE // 4).astype(BF16)
        for h in range(GQA_KV_HEADS):
            nh, _ = _rms(kv_ref[:, kvl + h * hd:kvl + (h + 1) * hd])
            kb_ref[:, h * hd:(h + 1) * hd] = _rope(nh * gb_ref[...], cb[...], s1b[...], s2b[...], hd // 4).astype(BF16)
        vb_ref[...] = kv_ref[:, kvl + nb:kvl + 2 * nb].astype(BF16)

    row = lambda w: pl.BlockSpec((rb, w), lambda i: (i, 0))
    fix = lambda w: pl.BlockSpec((1, w), lambda i: (0, 0))
    return _pcall(
        body,
        name="key_prep_fwd",
        out_shape=[jax.ShapeDtypeStruct((ta, kvl + LANE), BF16), jax.ShapeDtypeStruct((ta, nb), BF16),
                   jax.ShapeDtypeStruct((ta, nb), BF16)],
        grid=(ta // rb,),
        in_specs=[row(wkv), fix(kvl), fix(hd)] + [row(LANE)] * 3 + [row(hd)] * 3,
        out_specs=[row(kvl + LANE), row(nb), row(nb)],
        compiler_params=_cparams(("parallel",)),
    )(kv, kv_gain, kb_gain, *tabs)


def _key_prep_bwd(kv, kv_gain, kb_gain, tabs, dkin, dkb, dvb):
    ta, wkv = kv.shape
    kvl = MLA_KV_LORA
    nb = GQA_KV_HEADS * GQA_HEAD_DIM
    rb = ROW_BLOCK if ta % ROW_BLOCK == 0 else LANE
    hd = GQA_HEAD_DIM

    def body(kv_ref, g_ref, gb_ref, ca, s1a, s2a, cb, s1b, s2b, dkin_ref, dkb_ref, dvb_ref, dkv_ref, st_ref, stb_ref):
        @pl.when(pl.program_id(0) == 0)
        def _():
            st_ref[...] = jnp.zeros_like(st_ref)
            stb_ref[...] = jnp.zeros_like(stb_ref)

        xh, r = _rms(kv_ref[:, 0:kvl])
        dn = dkin_ref[:, 0:kvl]
        st_ref[0:1, :] += _colsum(dn * xh)
        dkv_ref[:, 0:kvl] = _rms_bwd(dn * g_ref[...], xh, r).astype(BF16)
        dpe = _rope_t(dkin_ref[:, kvl:kvl + LANE], ca[...], s1a[...], s2a[...], MLA_ROPE // 4)
        dkv_ref[:, kvl + 2 * nb:kvl + 2 * nb + LANE] = dpe.astype(BF16)
        for h in range(GQA_KV_HEADS):
            nh, rh = _rms(kv_ref[:, kvl + h * hd:kvl + (h + 1) * hd])
            dn_h = _rope_t(dkb_ref[:, h * hd:(h + 1) * hd], cb[...], s1b[...], s2b[...], hd // 4)
            stb_ref[0:1, :] += _colsum(dn_h * nh)
            dkv_ref[:, kvl + h * hd:kvl + (h + 1) * hd] = _rms_bwd(dn_h * gb_ref[...], nh, rh).astype(BF16)
        dkv_ref[:, kvl + nb:kvl + 2 * nb] = dvb_ref[...].astype(BF16)

    row = lambda w: pl.BlockSpec((rb, w), lambda i: (i, 0))
    fix = lambda w: pl.BlockSpec((1, w), lambda i: (0, 0))
    return _pcall(
        body,
        name="key_prep_bwd",
        out_shape=[jax.ShapeDtypeStruct((ta, wkv), BF16), jax.ShapeDtypeStruct((8, kvl), F32),
                   jax.ShapeDtypeStruct((8, hd), F32)],
        grid=(ta // rb,),
        in_specs=[row(wkv), fix(kvl), fix(hd)] + [row(LANE)] * 3 + [row(hd)] * 3 + [row(kvl + LANE), row(nb), row(nb)],
        out_specs=[row(wkv), pl.BlockSpec((8, kvl), lambda i: (0, 0)), pl.BlockSpec((8, hd), lambda i: (0, 0))],
        compiler_params=_cparams(("arbitrary",)),
    )(kv, kv_gain, kb_gain, *tabs, dkin, dkb, dvb)


def _q_prep_fwd(qg, q_gain, qb_gain, tabs, qscale):
    t = qg.shape[0]
    ql = MLA_Q_LORA
    hd = GQA_HEAD_DIM
    hb = GQA_HEADS * hd
    rb = min(ROW_BLOCK, t)

    def body(q_ref, g_ref, gb_ref, cb, s1b, s2b, cqn_ref, qb_ref):
        xh, _ = _rms(q_ref[:, 0:ql])
        cqn_ref[...] = (xh * g_ref[...]).astype(BF16)
        for h in range(GQA_HEADS):
            nh, _ = _rms(q_ref[:, ql + h * hd:ql + (h + 1) * hd])
            qh = _rope(nh * gb_ref[...], cb[...], s1b[...], s2b[...], hd // 4)
            qb_ref[:, h * hd:(h + 1) * hd] = (qh * qscale).astype(BF16)

    row = lambda w: pl.BlockSpec((rb, w), lambda i: (i, 0))
    fix = lambda w: pl.BlockSpec((1, w), lambda i: (0, 0))
    return _pcall(
        body,
        name="q_prep_fwd",
        out_shape=[jax.ShapeDtypeStruct((t, ql), BF16), jax.ShapeDtypeStruct((t, hb), BF16)],
        grid=(t // rb,),
        in_specs=[row(ql + hb), fix(ql), fix(hd)] + [row(hd)] * 3,
        out_specs=[row(ql), row(hb)],
        compiler_params=_cparams(("parallel",)),
    )(qg, q_gain, qb_gain, *tabs)


def _q_prep_bwd(qg, q_gain, qb_gain, tabs, dcqn, dqb, wpad, qscale):
    t = qg.shape[0]
    ql = MLA_Q_LORA
    hd = GQA_HEAD_DIM
    hb = GQA_HEADS * hd
    rb = min(ROW_BLOCK, t)

    def body(q_ref, g_ref, gb_ref, cb, s1b, s2b, dcqn_ref, dqb_ref, dq_ref, st_ref, stb_ref):
        @pl.when(pl.program_id(0) == 0)
        def _():
            st_ref[...] = jnp.zeros_like(st_ref)
            stb_ref[...] = jnp.zeros_like(stb_ref)

        xh, r = _rms(q_ref[:, 0:ql])
        dn = dcqn_ref[...]
        st_ref[0:1, :] += _colsum(dn * xh)
        dq_ref[:, 0:ql] = _rms_bwd(dn * g_ref[...], xh, r).astype(BF16)
        for h in range(GQA_HEADS):
            nh, rh = _rms(q_ref[:, ql + h * hd:ql + (h + 1) * hd])
            dn_h = _rope_t(dqb_ref[:, h * hd:(h + 1) * hd] * qscale, cb[...], s1b[...], s2b[...], hd // 4)
            stb_ref[0:1, :] += _colsum(dn_h * nh)
            dq_ref[:, ql + h * hd:ql + (h + 1) * hd] = _rms_bwd(dn_h * gb_ref[...], nh, rh).astype(BF16)
        if wpad:
            dq_ref[:, ql + hb:ql + hb + wpad] = jnp.zeros((rb, wpad), BF16)

    row = lambda w: pl.BlockSpec((rb, w), lambda i: (i, 0))
    fix = lambda w: pl.BlockSpec((1, w), lambda i: (0, 0))
    return _pcall(
        body,
        name="q_prep_bwd",
        out_shape=[jax.ShapeDtypeStruct((t, ql + hb + wpad), BF16), jax.ShapeDtypeStruct((8, ql), F32),
                   jax.ShapeDtypeStruct((8, hd), F32)],
        grid=(t // rb,),
        in_specs=[row(ql + hb), fix(ql), fix(hd)] + [row(hd)] * 3 + [row(ql), row(hb)],
        out_specs=[row(ql + hb + wpad), pl.BlockSpec((8, ql), lambda i: (0, 0)), pl.BlockSpec((8, hd), lambda i: (0, 0))],
        compiler_params=_cparams(("arbitrary",)),
    )(qg, q_gain, qb_gain, *tabs, dcqn, dqb)


def _rope_a(v, tabs, transpose, out_dtype, name, qscale):
    t, w = v.shape
    rb = min(ROW_BLOCK, t)
    fn = _rope_t if transpose else _rope

    def body(v_ref, c, s1, s2, o_ref):
        for h in range(w // MLA_SLOT):
            sl = slice(h * MLA_SLOT, (h + 1) * MLA_SLOT)
            o_ref[:, sl] = (fn(v_ref[:, sl].astype(F32), c[...], s1[...], s2[...], MLA_ROPE // 4) * qscale).astype(out_dtype)

    row = lambda ww: pl.BlockSpec((rb, ww), lambda i: (i, 0))
    return _pcall(
        body,
        name=name,
        out_shape=jax.ShapeDtypeStruct((t, w), out_dtype),
        grid=(t // rb,),
        in_specs=[row(w)] + [row(MLA_SLOT)] * 3,
        out_specs=row(w),
        compiler_params=_cparams(("parallel",)),
    )(v, *tabs)


def _merge_fwd(pa, pb, qg, gate_blk):
    t, d = pa.shape
    rb = min(ROW_BLOCK, t)

    def body(pa_ref, pb_ref, ga_ref, gb_ref, o_ref):
        o_ref[...] = (jax.nn.sigmoid(ga_ref[...]) * pa_ref[...].astype(F32)
                      + jax.nn.sigmoid(gb_ref[...]) * pb_ref[...].astype(F32)).astype(BF16)

    row = pl.BlockSpec((rb, d), lambda i: (i, 0))
    return _pcall(
        body,
        name="merge_fwd",
        out_shape=jax.ShapeDtypeStruct((t, d), BF16),
        grid=(t // rb,),
        in_specs=[row, row, pl.BlockSpec((rb, d), lambda i: (i, gate_blk)), pl.BlockSpec((rb, d), lambda i: (i, gate_blk + 1))],
        out_specs=row,
        compiler_params=_cparams(("parallel",)),
    )(pa, pb, qg, qg)


def _merge_bwd(dm, pa, pb, qg, gate_blk):
    t, d = pa.shape
    rb = min(ROW_BLOCK, t)

    def body(dm_ref, pa_ref, pb_ref, ga_ref, gb_ref, dpa_ref, dpb_ref, dg_ref):
        dmv = dm_ref[...].astype(F32)
        sa = jax.nn.sigmoid(ga_ref[...])
        sb = jax.nn.sigmoid(gb_ref[...])
        dpa_ref[...] = (dmv * sa).astype(BF16)
        dpb_ref[...] = (dmv * sb).astype(BF16)
        dg_ref[:, 0:d] = (dmv * pa_ref[...].astype(F32) * (sa * (1.0 - sa))).astype(BF16)
        dg_ref[:, d:2 * d] = (dmv * pb_ref[...].astype(F32) * (sb * (1.0 - sb))).astype(BF16)

    row = pl.BlockSpec((rb, d), lambda i: (i, 0))
    return _pcall(
        body,
        name="merge_bwd",
        out_shape=[jax.ShapeDtypeStruct((t, d), BF16), jax.ShapeDtypeStruct((t, d), BF16),
                   jax.ShapeDtypeStruct((t, 2 * d), BF16)],
        grid=(t // rb,),
        in_specs=[row, row, row, pl.BlockSpec((rb, d), lambda i: (i, gate_blk)), pl.BlockSpec((rb, d), lambda i: (i, gate_blk + 1))],
        out_specs=[row, row, pl.BlockSpec((rb, 2 * d), lambda i: (i, 0))],
        compiler_params=_cparams(("parallel",)),
    )(dm, pa, pb, qg, qg)


def _resid_norm_mod(x, branch, gain, mods, name):
    t, d = x.shape
    rb = min(ROW_BLOCK, t)

    def body(x_ref, b_ref, g_ref, mod_ref, x1_ref, z_ref):
        x1 = x_ref[...] + mod_ref[0:1, :] * b_ref[...]
        x1_ref[...] = x1
        xh, _ = _rms(x1)
        z_ref[...] = ((xh * g_ref[...]) * (1.0 + mod_ref[2:3, :]) + mod_ref[1:2, :]).astype(BF16)

    row = pl.BlockSpec((rb, d), lambda i: (i, 0))
    return _pcall(
        body,
        name=name,
        out_shape=[jax.ShapeDtypeStruct((t, d), F32), jax.ShapeDtypeStruct((t, d), BF16)],
        grid=(t // rb,),
        in_specs=[row, row, pl.BlockSpec((1, d), lambda i: (0, 0)), pl.BlockSpec((8, d), lambda i: (0, 0))],
        out_specs=[row, row],
        compiler_params=_cparams(("parallel",)),
    )(x, branch, gain, mods)


def _norm2_bwd(x1, attn, gain, mods, dz2, dx2):
    t, d = x1.shape
    rb = min(ROW_BLOCK, t)

    def body(x1_ref, at_ref, g_ref, mod_ref, dz_ref, dx2_ref, dx1_ref, da_ref, st_ref):
        @pl.when(pl.program_id(0) == 0)
        def _():
            st_ref[...] = jnp.zeros_like(st_ref)

        xh, r = _rms(x1_ref[...])
        g = g_ref[...]
        dz = dz_ref[...].astype(F32)
        dxn = dz * (1.0 + mod_ref[1:2, :])
        st_ref[0:1, :] += _colsum(dz)
        st_ref[1:2, :] += _colsum(dz * (xh * g))
        st_ref[2:3, :] += _colsum(dxn * xh)
        dx1 = dx2_ref[...] + _rms_bwd(dxn * g, xh, r)
        dx1_ref[...] = dx1
        st_ref[3:4, :] += _colsum(dx1 * at_ref[...])
        da_ref[...] = (dx1 * mod_ref[0:1, :]).astype(BF16)

    row = pl.BlockSpec((rb, d), lambda i: (i, 0))
    return _pcall(
        body,
        name="norm2_mod_bwd",
        out_shape=[jax.ShapeDtypeStruct((t, d), F32), jax.ShapeDtypeStruct((t, d), BF16), jax.ShapeDtypeStruct((8, d), F32)],
        grid=(t // rb,),
        in_specs=[row, row, pl.BlockSpec((1, d), lambda i: (0, 0)), pl.BlockSpec((8, d), lambda i: (0, 0)), row, row],
        out_specs=[row, row, pl.BlockSpec((8, d), lambda i: (0, 0))],
        compiler_params=_cparams(("arbitrary",)),
    )(x1, attn, gain, mods, dz2, dx2)


def _final_loss(x1, ffn, gain, mods, target):
    t, d = x1.shape
    rb = min(ROW_BLOCK, t)
    nb = t // rb

    def body(x1_ref, f_ref, g_ref, mod_ref, tg_ref, dx2_ref, df_ref, st_ref):
        i = pl.program_id(0)

        @pl.when(i == 0)
        def _():
            st_ref[...] = jnp.zeros_like(st_ref)

        ffn_v = f_ref[...]
        g2 = mod_ref[0:1, :]
        x2 = x1_ref[...] + g2 * ffn_v
        xh, r = _rms(x2)
        g = g_ref[...]
        err = xh * g - tg_ref[...]
        st_ref[2:3, :] += _colsum(err * err) * (0.5 / d)
        dy = err * (1.0 / d)
        st_ref[0:1, :] += _colsum(dy * xh)
        dx2 = _rms_bwd(dy * g, xh, r)
        dx2_ref[...] = dx2
        st_ref[1:2, :] += _colsum(dx2 * ffn_v)
        df_ref[...] = (dx2 * g2).astype(BF16)

        @pl.when(i == nb - 1)
        def _():
            st_ref[3:4, :] = jnp.broadcast_to(jnp.sum(st_ref[2:3, :], axis=-1, keepdims=True), (1, d))

    row = pl.BlockSpec((rb, d), lambda i: (i, 0))
    return _pcall(
        body,
        name="final_norm_loss",
        out_shape=[jax.ShapeDtypeStruct((t, d), F32), jax.ShapeDtypeStruct((t, d), BF16), jax.ShapeDtypeStruct((8, d), F32)],
        grid=(nb,),
        in_specs=[row, row, pl.BlockSpec((1, d), lambda i: (0, 0)), pl.BlockSpec((8, d), lambda i: (0, 0)), row],
        out_specs=[row, row, pl.BlockSpec((8, d), lambda i: (0, 0))],
        compiler_params=_cparams(("arbitrary",)),
    )(x1, ffn, gain, mods, target)


def _row_ends(shape):
    rows = lax.broadcasted_iota(jnp.int32, shape, 0)
    return rows == 0, rows == shape[0] - 1


def _shift_dn(v, first):
    return jnp.where(first, 0.0, pltpu.roll(v, 1, 0))


def _shift_up(v, last):
    return jnp.where(last, 0.0, pltpu.roll(v, v.shape[0] - 1, 0))


def _conv_fwd(u, cw, cb):
    t, f2 = u.shape
    f = f2 // 2
    cbk = _tile(f, 256)
    nf = f // cbk

    def body(ua_ref, ub_ref, cwa_ref, cwb_ref, cba_ref, cbb_ref, h_ref, uc_ref):
        first, last = _row_ends((t, cbk))
        outs = []
        for u_ref, cw_ref, cb_ref in ((ua_ref, cwa_ref, cba_ref), (ub_ref, cwb_ref, cbb_ref)):
            uu, cwv = u_ref[...].astype(F32), cw_ref[...]
            outs.append(cb_ref[...] + cwv[0:1, :] * _shift_dn(uu, first) + cwv[1:2, :] * uu
                        + cwv[2:3, :] * _shift_up(uu, last))
        a, b = outs
        uc_ref[0] = a.astype(BF16)
        uc_ref[1] = b.astype(BF16)
        h_ref[...] = (a * jax.nn.sigmoid(a) * b).astype(BF16)

    ca = lambda r: pl.BlockSpec((r, cbk), lambda j: (0, j))
    cbs = lambda r: pl.BlockSpec((r, cbk), lambda j: (0, nf + j))
    return _pcall(
        body,
        name="conv_gate_fwd",
        out_shape=[jax.ShapeDtypeStruct((t, f), BF16), jax.ShapeDtypeStruct((2, t, f), BF16)],
        grid=(nf,),
        in_specs=[ca(t), cbs(t), ca(3), cbs(3), ca(1), cbs(1)],
        out_specs=[ca(t), pl.BlockSpec((2, t, cbk), lambda j: (0, 0, j))],
        compiler_params=_cparams(("parallel",)),
    )(u, u, cw, cw, cb, cb)


def _conv_bwd(u, uc, cw, dh):
    t, f2 = u.shape
    f = f2 // 2
    cbk = _tile(f, 256)
    nf = f // cbk

    def body(ua_ref, ub_ref, uc_ref, cwa_ref, cwb_ref, dh_ref, du_ref, dcw_ref, dcb_ref):
        first, last = _row_ends((t, cbk))
        a, b = uc_ref[0].astype(F32), uc_ref[1].astype(F32)
        dh_v = dh_ref[...].astype(F32)
        sg = jax.nn.sigmoid(a)
        db = dh_v * (a * sg)
        da = dh_v * b * (sg * (1.0 + a * (1.0 - sg)))
        for idx, (dv, u_ref, cw_ref) in enumerate(((da, ua_ref, cwa_ref), (db, ub_ref, cwb_ref))):
            uu, cwv = u_ref[...].astype(F32), cw_ref[...]
            up, dn = _shift_up(dv, last), _shift_dn(dv, first)
            dcb_ref[idx] = _colsum(dv)
            dcw_ref[idx, 0:1, :] = _colsum(up * uu)
            dcw_ref[idx, 1:2, :] = _colsum(dv * uu)
            dcw_ref[idx, 2:3, :] = _colsum(dn * uu)
            du_ref[idx] = (cwv[0:1, :] * up + cwv[1:2, :] * dv + cwv[2:3, :] * dn).astype(BF16)

    ca = lambda r: pl.BlockSpec((r, cbk), lambda j: (0, j))
    cbs = lambda r: pl.BlockSpec((r, cbk), lambda j: (0, nf + j))
    o3 = lambda r: pl.BlockSpec((2, r, cbk), lambda j: (0, 0, j))
    return _pcall(
        body,
        name="conv_gate_bwd",
        out_shape=[jax.ShapeDtypeStruct((2, t, f), BF16), jax.ShapeDtypeStruct((2, 3, f), F32),
                   jax.ShapeDtypeStruct((2, 1, f), F32)],
        grid=(nf,),
        in_specs=[ca(t), cbs(t), o3(t), ca(3), cbs(3), ca(t)],
        out_specs=[o3(t), o3(3), o3(1)],
        compiler_params=_cparams(("parallel",)),
    )(u, u, uc, cw, cw, dh)


def _attention_fwd(q, kk, vv, *, hq, hkv, dk, dv, k_blk0, v_blk0, name):
    t = q.shape[0]
    tk = kk.shape[0]
    g_sz = hq // hkv
    tq = min(ATT_Q_BLOCK_FWD, t)

    def body(q_ref, k_ref, v_ref, o_ref, lse_ref):
        k = k_ref[...]
        v = v_ref[...]
        for j in range(g_sz):
            s = lax.dot_general(q_ref[:, j * dk:(j + 1) * dk], k, _DIMS["nt"], preferred_element_type=F32)
            m = jnp.max(s, axis=-1, keepdims=True)
            p = jnp.exp2(s - m)
            l = jnp.sum(p, axis=-1, keepdims=True)
            o = jnp.dot(p.astype(BF16), v, preferred_element_type=F32) / l
            o_ref[:, j * dv:(j + 1) * dv] = o.astype(BF16)
            lse_ref[0, :, j:j + 1] = m + jnp.log2(l)

    return _pcall(
        body,
        name=name,
        out_shape=[jax.ShapeDtypeStruct((t, hq * dv), BF16), jax.ShapeDtypeStruct((hkv, t, g_sz), F32)],
        grid=(hkv, t // tq),
        in_specs=[
            pl.BlockSpec((tq, g_sz * dk), lambda g, i: (i, g)),
            pl.BlockSpec((tk, dk), lambda g, i: (0, k_blk0 + g)),
            pl.BlockSpec((tk, dv), lambda g, i: (0, v_blk0 + g)),
        ],
        out_specs=[
            pl.BlockSpec((tq, g_sz * dv), lambda g, i: (i, g)),
            pl.BlockSpec((1, tq, g_sz), lambda g, i: (g, i, 0)),
        ],
        compiler_params=_cparams(("parallel", "parallel")),
    )(q, kk, vv)


def _attention_bwd(q, kk, vv, do, lse, *, hq, hkv, dk, dv, k_blk0, v_blk0, name):
    t = q.shape[0]
    tk = kk.shape[0]
    g_sz = hq // hkv
    tq = min(ATT_Q_BLOCK, t)

    def body(q_ref, k_ref, v_ref, do_ref, lse_ref, dq_ref, dk_ref, dv_ref):
        @pl.when(pl.program_id(1) == 0)
        def _():
            dk_ref[...] = jnp.zeros_like(dk_ref)
            dv_ref[...] = jnp.zeros_like(dv_ref)

        k = k_ref[...]
        v = v_ref[...]
        for j in range(g_sz):
            qj = q_ref[:, j * dk:(j + 1) * dk]
            doj = do_ref[:, j * dv:(j + 1) * dv]
            s = lax.dot_general(qj, k, _DIMS["nt"], preferred_element_type=F32)
            p = jnp.exp2(s - lse_ref[0, :, j:j + 1])
            dp = lax.dot_general(doj, v, _DIMS["nt"], preferred_element_type=F32)
            ds = (p * (dp - jnp.sum(p * dp, axis=-1, keepdims=True))).astype(BF16)
            dv_ref[...] += lax.dot_general(p.astype(BF16), doj, _DIMS["tn"], preferred_element_type=F32)
            dk_ref[...] += lax.dot_general(ds, qj, _DIMS["tn"], preferred_element_type=F32)
            dq_ref[:, j * dk:(j + 1) * dk] = jnp.dot(ds, k, preferred_element_type=F32)

        @pl.when(pl.program_id(1) == t // tq - 1)
        def _():
            dk_ref[...] *= LN2

    return _pcall(
        body,
        name=name,
        out_shape=[jax.ShapeDtypeStruct((t, hq * dk), F32), jax.ShapeDtypeStruct((tk, hkv * dk), F32),
                   jax.ShapeDtypeStruct((tk, hkv * dv), F32)],
        grid=(hkv, t // tq),
        in_specs=[
            pl.BlockSpec((tq, g_sz * dk), lambda g, i: (i, g)),
            pl.BlockSpec((tk, dk), lambda g, i: (0, k_blk0 + g)),
            pl.BlockSpec((tk, dv), lambda g, i: (0, v_blk0 + g)),
            pl.BlockSpec((tq, g_sz * dv), lambda g, i: (i, g)),
            pl.BlockSpec((1, tq, g_sz), lambda g, i: (g, i, 0)),
        ],
        out_specs=[
            pl.BlockSpec((tq, g_sz * dk), lambda g, i: (i, g)),
            pl.BlockSpec((tk, dk), lambda g, i: (0, g)),
            pl.BlockSpec((tk, dv), lambda g, i: (0, g)),
        ],
        compiler_params=_cparams(("parallel", "arbitrary")),
    )(q, kk, vv, do, lse)


def _silu(v):
    return v * jax.nn.sigmoid(v)


def _ada_fwd(conds, w_ada, b_ada_shard):
    r, d = conds.shape
    n = w_ada.shape[1]
    tn = _tile(n, 512)

    def body(c_ref, w_ref, b_ref, o_ref):
        s = _silu(c_ref[...]).astype(BF16)
        o_ref[...] = jnp.dot(s, w_ref[...].astype(BF16), preferred_element_type=F32) + b_ref[...]

    return _pcall(
        body,
        name="ada_fwd",
        out_shape=jax.ShapeDtypeStruct((r, n), F32),
        grid=(n // tn,),
        in_specs=[pl.BlockSpec((r, d), lambda j: (0, 0)), pl.BlockSpec((d, tn), lambda j: (0, j)),
                  pl.BlockSpec((1, tn), lambda j: (0, j))],
        out_specs=pl.BlockSpec((r, tn), lambda j: (0, j)),
        compiler_params=_cparams(("parallel",)),
    )(conds, w_ada, b_ada_shard)


def _cctx_partial(da16_shard, w_ada, c_ctx_row):
    d, n = w_ada.shape
    td = _tile(d, 512)

    def body(g_ref, w_ref, c_ref, o_ref):
        ds = lax.dot_general(g_ref[8:16, :].astype(BF16), w_ref[...].astype(BF16), _DIMS["nt"],
                             preferred_element_type=F32)
        cv = c_ref[...]
        sg = jax.nn.sigmoid(cv)
        o_ref[...] = ds * (sg * (1.0 + cv * (1.0 - sg)))

    return _pcall(
        body,
        name="cctx_partial",
        out_shape=jax.ShapeDtypeStruct((8, d), F32),
        grid=(d // td,),
        in_specs=[pl.BlockSpec((16, n), lambda j: (0, 0)), pl.BlockSpec((td, n), lambda j: (j, 0)),
                  pl.BlockSpec((1, td), lambda j: (0, j))],
        out_specs=pl.BlockSpec((8, td), lambda j: (0, j)),
        compiler_params=_cparams(("parallel",)),
    )(da16_shard, w_ada, c_ctx_row)


def _sum_parts(parts):
    p, _, n = parts.shape

    def body(p_ref, o_ref):
        acc = p_ref[0]
        for s in range(1, p):
            acc = acc + p_ref[s]
        o_ref[...] = acc

    return _pcall(
        body,
        name="sum_parts",
        out_shape=jax.ShapeDtypeStruct((1, n), F32),
        in_specs=[pl.BlockSpec(memory_space=pltpu.VMEM)],
        out_specs=pl.BlockSpec(memory_space=pltpu.VMEM),
    )(parts)


def _adam_math(w, g, m, v):
    m2 = ADAM_B1 * m + (1.0 - ADAM_B1) * g
    v2 = ADAM_B2 * v + (1.0 - ADAM_B2) * jnp.square(g)
    m_hat = m2 / (1.0 - ADAM_B1 ** ADAM_STEP)
    v_hat = v2 / (1.0 - ADAM_B2 ** ADAM_STEP)
    delta = -ADAM_LR * (m_hat / (jnp.sqrt(v_hat) + ADAM_EPS) + ADAM_WD * w)
    return delta, m2, v2


def _adamw(parts, w, m, v, name):
    p, r, c = parts.shape
    rb = _tile(r, max(8, (1 << 20) // (4 * c) // 8 * 8), 8)

    def body(p_ref, w_ref, m_ref, v_ref, g_ref, d_ref, m2_ref, v2_ref):
        g = p_ref[0].astype(F32)
        for s in range(1, p):
            g = g + p_ref[s].astype(F32)
        g_ref[...] = g
        d_ref[...], m2_ref[...], v2_ref[...] = _adam_math(w_ref[...], g, m_ref[...], v_ref[...])

    if w.ndim == 3:
        row = pl.BlockSpec((None, rb, c), lambda i: (0, i, 0))
    else:
        row = pl.BlockSpec((rb, c), lambda i: (i, 0))
    return _pcall(
        body,
        name=name,
        out_shape=[jax.ShapeDtypeStruct(w.shape, F32)] * 4,
        grid=(r // rb,),
        in_specs=[pl.BlockSpec((p, rb, c), lambda i: (0, i, 0)), row, row, row],
        out_specs=[row] * 4,
        compiler_params=_cparams(("parallel",)),
    )(parts, w, m, v)


def _adamw_ada(conds, da16, w, m, v):
    d, n = w.shape
    rb = _tile(d, 256, LANE)

    def body(s_ref, da_ref, w_ref, m_ref, v_ref, g_ref, d_ref, m2_ref, v2_ref):
        g = lax.dot_general(_silu(s_ref[...]).astype(BF16), da_ref[...].astype(BF16), _DIMS["tn"],
                            preferred_element_type=F32)
        g_ref[...] = g
        d_ref[...], m2_ref[...], v2_ref[...] = _adam_math(w_ref[...], g, m_ref[...], v_ref[...])

    row = pl.BlockSpec((rb, n), lambda i: (i, 0))
    return _pcall(
        body,
        name="adamw_w_ada",
        out_shape=[jax.ShapeDtypeStruct((d, n), F32)] * 4,
        grid=(d // rb,),
        in_specs=[pl.BlockSpec((16, rb), lambda i: (0, i)), pl.BlockSpec((16, n), lambda i: (0, 0)), row, row, row],
        out_specs=[row] * 4,
        compiler_params=_cparams(("parallel",)),
    )(conds, da16, w, m, v)


def _cast_bf16(a, name):
    _, r, c = a.shape
    rb = _tile(r, 512, 8)

    def body(a_ref, o_ref):
        o_ref[...] = a_ref[...].astype(BF16)

    return _pcall(body, name=name, out_shape=jax.ShapeDtypeStruct((r, c), BF16), grid=(r // rb,),
                  in_specs=[pl.BlockSpec((None, rb, c), lambda i: (0, i, 0))],
                  out_specs=pl.BlockSpec((rb, c), lambda i: (i, 0)), compiler_params=_cparams(("parallel",)))(a)


def _rope_tabs(t, rot):
    half, q = rot // 2, rot // 4
    n_rows = t // GRID_W
    row = jnp.repeat(jnp.arange(n_rows, dtype=F32), GRID_W)
    col = jnp.tile(jnp.arange(GRID_W, dtype=F32), n_rows)
    inv_freq = ROPE_THETA ** (-jnp.arange(0, half, 2, dtype=F32) / half)
    ang = jnp.concatenate([row[:, None] * inv_freq, col[:, None] * inv_freq], axis=-1)
    cos, sin = jnp.cos(ang), jnp.sin(ang)
    c0, c1, s0, s1 = cos[:, :q], cos[:, q:], sin[:, :q], sin[:, q:]
    z = jnp.zeros_like(s0)
    return (jnp.concatenate([c0, c0, c1, c1], -1), jnp.concatenate([-s0, z, -s1, z], -1),
            jnp.concatenate([z, s0, z, s1], -1))


def _pad_cols(a, left, total, fill=0.0):
    return jnp.pad(a, ((0, 0), (left, total - left - a.shape[1])), constant_values=fill)


def _with_ctx_rows(tab, tc, fill):
    return jnp.concatenate([tab, jnp.full((tc, tab.shape[1]), fill, F32)], axis=0)


def kernel(x, c, ctx, c_ctx, w_ada, b_ada, norm1_g, w_in, mla_q_norm_g, w_q_up, mla_kv_norm_g, w_kv_up, gqa_q_norm_g, gqa_k_norm_g, w_br_a, w_br_b, w_out, norm2_g, w_up, conv_w, conv_b, w_down, final_norm_g, loss_target, m_c_ctx, m_w_ada, m_b_ada, m_norm1_g, m_w_in, m_mla_q_norm_g, m_w_q_up, m_mla_kv_norm_g, m_w_kv_up, m_gqa_q_norm_g, m_gqa_k_norm_g, m_w_br_a, m_w_br_b, m_w_out, m_norm2_g, m_w_up, m_conv_w, m_conv_b, m_w_down, m_final_norm_g, v_c_ctx, v_w_ada, v_b_ada, v_norm1_g, v_w_in, v_mla_q_norm_g, v_w_q_up, v_mla_kv_norm_g, v_w_kv_up, v_gqa_q_norm_g, v_gqa_k_norm_g, v_w_br_a, v_w_br_b, v_w_out, v_norm2_g, v_w_up, v_conv_w, v_conv_b, v_w_down, v_final_norm_g):
    weights = dict(c_ctx=c_ctx, w_ada=w_ada, b_ada=b_ada, norm1_g=norm1_g, w_in=w_in, mla_q_norm_g=mla_q_norm_g,
                   w_q_up=w_q_up, mla_kv_norm_g=mla_kv_norm_g, w_kv_up=w_kv_up, gqa_q_norm_g=gqa_q_norm_g,
                   gqa_k_norm_g=gqa_k_norm_g, w_br_a=w_br_a, w_br_b=w_br_b, w_out=w_out, norm2_g=norm2_g, w_up=w_up,
                   conv_w=conv_w, conv_b=conv_b, w_down=w_down, final_norm_g=final_norm_g)
    mom_m = dict(c_ctx=m_c_ctx, w_ada=m_w_ada, b_ada=m_b_ada, norm1_g=m_norm1_g, w_in=m_w_in, mla_q_norm_g=m_mla_q_norm_g,
                 w_q_up=m_w_q_up, mla_kv_norm_g=m_mla_kv_norm_g, w_kv_up=m_w_kv_up, gqa_q_norm_g=m_gqa_q_norm_g,
                 gqa_k_norm_g=m_gqa_k_norm_g, w_br_a=m_w_br_a, w_br_b=m_w_br_b, w_out=m_w_out, norm2_g=m_norm2_g,
                 w_up=m_w_up, conv_w=m_conv_w, conv_b=m_conv_b, w_down=m_w_down, final_norm_g=m_final_norm_g)
    mom_v = dict(c_ctx=v_c_ctx, w_ada=v_w_ada, b_ada=v_b_ada, norm1_g=v_norm1_g, w_in=v_w_in, mla_q_norm_g=v_mla_q_norm_g,
                 w_q_up=v_w_q_up, mla_kv_norm_g=v_mla_kv_norm_g, w_kv_up=v_w_kv_up, gqa_q_norm_g=v_gqa_q_norm_g,
                 gqa_k_norm_g=v_gqa_k_norm_g, w_br_a=v_w_br_a, w_br_b=v_w_br_b, w_out=v_w_out, norm2_g=v_norm2_g,
                 w_up=v_w_up, conv_w=v_conv_w, conv_b=v_conv_b, w_down=v_w_down, final_norm_g=v_final_norm_g)
    order = list(weights)

    my_idx = 4 * lax.axis_index("x") + 2 * lax.axis_index("y") + lax.axis_index("c")
    xs, cts, tgt = x[0], ctx[0], loss_target[0]
    t, d = xs.shape
    tc = cts.shape[0]
    ta = t + tc
    kvl, ql = MLA_KV_LORA, MLA_Q_LORA
    nb = GQA_KV_HEADS * GQA_HEAD_DIM
    hb = GQA_HEADS * GQA_HEAD_DIM
    ha = MLA_HEADS
    f2 = w_up.shape[2] * N_DEV
    ff = f2 // 2

    big = ["w_in", "w_q_up", "w_kv_up", "w_br_a", "w_br_b", "w_out", "w_up", "w_down"]
    nw = len(big)
    del nw
    _ORDER_AFTER.clear()
    shards = {n: _cast_bf16(weights[n], "cast_" + n) for n in big}
    c_idx = jnp.reshape(lax.axis_index("c"), (1,)).astype(jnp.int32)

    def gather_start(names, dep):
        shs = [shards[n] for n in names]
        land = [lax.empty((N_DEV,) + s.shape, BF16) for s in shs]
        if dep is not None:
            _after(dep)
        s, r, arrs, tok = _split_start("gather_ici_start_" + names[0], shs + land, _gather_ici_copies(len(names)),
                                       4 * len(names))
        return dict(names=names, s=s, r=r, arrs=arrs, tok=tok)

    def gather_pass(g, after):
        n = len(g["names"])
        arrs = _split_wait("gather_ici_wait_" + g["names"][0], g["s"], g["r"], g["arrs"], _gather_ici_copies(n), after)
        s, r, bufs, tok = _split_start("gather_pass_start_" + g["names"][0], arrs[n:], _gather_pass_copies(n), 3 * n)
        g.update(s2=s, r2=r, bufs=bufs)
        return tok

    def gather_relay(g, after):
        n = len(g["names"])
        bufs = _split_wait("gather_pass_wait_" + g["names"][0], g["s2"], g["r2"], g["bufs"], _gather_pass_copies(n), after)
        s, r, bufs, tok = _split_start("gather_d2d_start_" + g["names"][0], bufs, _gather_d2d_copies(n), n)
        g.update(s3=s, r3=r, bufs=bufs)
        return tok

    def gather_finish(g, after):
        n = len(g["names"])
        bufs = _split_wait("gather_d2d_wait_" + g["names"][0], g["s3"], g["r3"], g["bufs"], _gather_d2d_copies(n), after)
        return dict(zip(g["names"], bufs))

    c_all, cw_all = _all_gather([jnp.pad(c, ((0, 7), (0, 0))), jnp.pad(conv_w[0], ((0, 5), (0, 0)))], "gather_cond")
    conv_w_f = jnp.transpose(cw_all[:, :3, :], (1, 0, 2)).reshape(3, f2)
    conds = jnp.concatenate([c_all[:, 0, :], c_ctx[None, :], jnp.zeros((7, d), F32)], axis=0)
    ncol = w_ada.shape[2]
    b_shard = lax.dynamic_slice_in_dim(b_ada, my_idx * ncol, ncol, axis=1)
    ada_shard = _ada_fwd(conds, w_ada[0], b_shard)
    (ada_all,) = _all_gather([ada_shard], "gather_ada")
    ada = jnp.transpose(ada_all, (1, 0, 2)).reshape(16, N_DEV * ncol)
    lat = lax.dynamic_slice_in_dim(ada, my_idx, 1, axis=0).reshape(6, d)
    cxt = ada[8].reshape(6, d)
    zero2 = jnp.zeros((2, d), F32)
    mods1 = jnp.concatenate([lat[0:2], cxt[0:2], jnp.zeros((4, d), F32)], axis=0)
    mods2 = jnp.concatenate([lat[2:3], lat[3:4], lat[4:5], jnp.zeros((5, d), F32)], axis=0)
    mods2b = jnp.concatenate([lat[2:3], lat[4:5], jnp.zeros((6, d), F32)], axis=0)
    mods3 = jnp.concatenate([lat[5:6], jnp.zeros((7, d), F32)], axis=0)
    del zero2

    g0 = gather_start(["w_in"], ada_all)
    g1 = gather_start(["w_q_up", "w_kv_up", "w_br_a", "w_br_b", "w_out"], g0["tok"])
    g2 = gather_start(["w_up"], g1["tok"])
    g3 = gather_start(["w_down"], g2["tok"])

    ca, s1a, s2a = _rope_tabs(t, MLA_ROPE)
    cb_, s1b, s2b = _rope_tabs(t, GQA_HEAD_DIM)
    q_tabs_a = (_pad_cols(jnp.concatenate([jnp.ones((t, MLA_NOPE), F32), ca], 1), 0, MLA_SLOT),
                _pad_cols(s1a, MLA_NOPE, MLA_SLOT), _pad_cols(s2a, MLA_NOPE, MLA_SLOT))
    q_tabs_b = (cb_, s1b, s2b)
    k_tabs = (_with_ctx_rows(_pad_cols(ca, 0, LANE), tc, 1.0), _with_ctx_rows(_pad_cols(s1a, 0, LANE), tc, 0.0),
              _with_ctx_rows(_pad_cols(s2a, 0, LANE), tc, 0.0),
              _with_ctx_rows(cb_, tc, 1.0), _with_ctx_rows(s1b, tc, 0.0), _with_ctx_rows(s2b, tc, 0.0))

    def cols_full(g):
        return jnp.transpose(g, (1, 0, 2)).reshape(g.shape[1], N_DEV * g.shape[2])

    _after(gather_pass(g0, mods1))
    z_all = _norm_mod_fwd(cts, xs, norm1_g, mods1)
    gathered = gather_finish(g0, gather_relay(g0, z_all))
    w_in_f = cols_full(gathered["w_in"])
    o_kpe, o_kb, o_vb = kvl, kvl + MLA_ROPE, kvl + MLA_ROPE + nb
    o_q = o_vb + nb
    o_g = o_q + ql + hb
    wkv_w = kvl + 2 * nb + LANE
    w_kv_p = jnp.concatenate([w_in_f[:, :kvl], w_in_f[:, o_kb:o_q], w_in_f[:, o_kpe:o_kb],
                              jnp.zeros((d, LANE - MLA_ROPE), BF16)], axis=1)
    q_w = ql + hb
    q_pad = (-q_w) % 512 if d >= 512 else (-q_w) % d
    gate_blk = (q_w + q_pad) // d
    assert (q_w + q_pad) % d == 0
    w_qg_p = jnp.concatenate([w_in_f[:, o_q:o_g], jnp.zeros((d, q_pad), BF16), w_in_f[:, o_g:]], axis=1)

    kv_all = _mm(z_all, w_kv_p, "nn", F32, "proj_kv", tm=1152, tn=wkv_w)
    qg = _mm(z_all, w_qg_p, "nn", F32, "proj_qg", tm=1024, tn=1024, rows=t)
    _after(gather_pass(g1, qg))
    kin, k_b, v_b = _key_prep_fwd(kv_all, mla_kv_norm_g, gqa_k_norm_g, k_tabs)
    sc_a = float((MLA_NOPE + MLA_ROPE) ** -0.5) * LOG2E
    sc_b = float(GQA_HEAD_DIM ** -0.5) * LOG2E
    cqn, q_b = _q_prep_fwd(qg, mla_q_norm_g, gqa_q_norm_g, q_tabs_b, sc_b)
    tok_g1 = gather_relay(g1, q_b)
    tok_g2 = gather_pass(g2, tok_g1)
    gathered.update(gather_finish(g1, tok_g2))

    wq_f = cols_full(gathered["w_q_up"]).reshape(ql, ha, MLA_NOPE + MLA_ROPE)
    wq_ext = jnp.pad(wq_f, ((0, 0), (0, 0), (0, MLA_SLOT - MLA_NOPE - MLA_ROPE))).reshape(ql, ha * MLA_SLOT)
    wkv_f = cols_full(gathered["w_kv_up"]).reshape(kvl, ha, MLA_NOPE + MLA_V)
    wk_slots = jnp.pad(wkv_f[:, :, :MLA_NOPE], ((0, 0), (0, 0), (0, MLA_SLOT - MLA_NOPE))).reshape(kvl, ha * MLA_SLOT)
    wv_cols = wkv_f[:, :, MLA_NOPE:].reshape(kvl, ha * MLA_V)
    e_slot = jnp.pad(jnp.eye(MLA_ROPE, dtype=BF16),
                     ((0, LANE - MLA_ROPE), (MLA_NOPE, MLA_SLOT - MLA_NOPE - MLA_ROPE)))
    e_rows = jnp.concatenate([jnp.tile(e_slot, (1, ha)), jnp.zeros((LANE, ha * MLA_V), BF16)], axis=1)
    wkv_ext = jnp.concatenate([jnp.concatenate([wk_slots, wv_cols], axis=1), e_rows], axis=0)
    w_bra = cols_full(gathered["w_br_a"])
    w_brb = cols_full(gathered["w_br_b"])
    w_out_f = gathered["w_out"].reshape(d, d)

    kv_a = _mm(kin, wkv_ext, "nn", BF16, "kv_up", tm=1152, tn=1024)
    qa_raw = _mm(cqn, wq_ext, "nn", F32, "q_up", tm=1024, tn=1024)
    q_a = _rope_a(qa_raw, q_tabs_a, False, BF16, "rope_q_fwd", sc_a)
    att_a = dict(hq=ha, hkv=ha, dk=MLA_SLOT, dv=MLA_V, k_blk0=0, v_blk0=ha * MLA_SLOT // MLA_V)
    att_b = dict(hq=GQA_HEADS, hkv=GQA_KV_HEADS, dk=GQA_HEAD_DIM, dv=GQA_HEAD_DIM, k_blk0=0, v_blk0=0)
    o_a, lse_a = _attention_fwd(q_a, kv_a, kv_a, name="attn_a_fwd", **att_a)
    o_b, lse_b = _attention_fwd(q_b, k_b, v_b, name="attn_b_fwd", **att_b)
    _after(gather_relay(g2, o_b), gather_pass(g3, o_b))
    pa = _mm(o_a, w_bra, "nn", BF16, "br_a", tm=1024, tn=1024)
    pb = _mm(o_b, w_brb, "nn", BF16, "br_b", tm=1024, tn=1024)
    merged = _merge_fwd(pa, pb, qg, gate_blk)
    attn = _mm(merged, w_out_f, "nn", F32, "w_out", tm=1024, tn=1024)
    x1, z2 = _resid_norm_mod(xs, attn, norm2_g, mods2, "resid_norm2_fwd")
    w_up3 = gather_finish(g2, z2)["w_up"]
    _after(gather_relay(g3, z2))
    u = _mm_up_fwd(z2, w_up3, "w_up")
    w_down_f = gather_finish(g3, u)["w_down"].reshape(ff, d)
    h, uc = _conv_fwd(u, conv_w_f, conv_b)
    ffn = _mm(h, w_down_f, "nn", F32, "w_down", tm=1024, tn=1024, tk=2816)

    def to_shards(g):
        return jnp.transpose(g.reshape(g.shape[0], N_DEV, g.shape[1] // N_DEV), (1, 0, 2))

    def reduce_start(tag, names, sends):
        n = len(sends)
        land = [lax.empty((4,) + s.shape[1:], s.dtype) for s in sends]
        s, r, arrs, tok = _split_start("reduce_d2d_start_" + tag, sends + land, _reduce_d2d_copies(n), 4 * n)
        return dict(tag=tag, names=names, s=s, r=r, arrs=arrs, tok=tok)

    def reduce_relay(g, after):
        n = len(g["names"])
        arrs = _split_wait("reduce_d2d_wait_" + g["tag"], g["s"], g["r"], g["arrs"], _reduce_d2d_copies(n), after)
        sums = [_pair_sum(arrs[a], arrs[n + a], c_idx, "pair_sum_" + g["names"][a]) for a in range(n)]
        land = [lax.empty(s.shape, s.dtype) for s in sums]
        s, r, arrs2, tok = _split_start("reduce_ici_start_" + g["tag"], sums + land, _reduce_ici_copies(n), 4 * n)
        g.update(s2=s, r2=r, arrs2=arrs2)
        return tok

    def reduce_finish(g, after):
        n = len(g["names"])
        arrs2 = _split_wait("reduce_ici_wait_" + g["tag"], g["s2"], g["r2"], g["arrs2"], _reduce_ici_copies(n), after)
        return dict(zip(g["names"], arrs2[n:]))

    dx2, dffn, st_fin = _final_loss(x1, ffn, final_norm_g[None, :], mods3, tgt)
    loss = lax.psum(st_fin[3, 0], MESH_AXES)
    dh = _mm(dffn, w_down_f, "nt", BF16, "d_h", tm=1024, tn=1024)
    g_w_down = _mm(h, dffn, "tn", BF16, "g_w_down", tm=512, tn=1024)
    r_down = reduce_start("down", ["w_down"], [g_w_down.reshape(N_DEV, ff // N_DEV, d)])
    _after(r_down["tok"])
    du3, dcw, dcb = _conv_bwd(u, uc, conv_w_f, dh)
    dz2 = _mm_up_dz(du3, w_up3, "d_z2")
    g_w_up = _mm_up_gw(z2, du3, N_DEV, "g_w_up")
    g_conv_w = jnp.concatenate([dcw[0], dcw[1]], axis=1)
    tok = reduce_relay(r_down, g_w_up)
    _after(tok)
    r_up = reduce_start("up", ["w_up", "conv_w"], [g_w_up, to_shards(jnp.pad(g_conv_w, ((0, 5), (0, 0))))])
    _after(tok, r_up["tok"])
    dx1, dattn, st_n2 = _norm2_bwd(x1, attn, norm2_g, mods2b, dz2, dx2)
    dmerged = _mm(dattn, w_out_f, "nt", BF16, "d_merged", tm=1024, tn=1024)
    g_w_out = _mm(merged, dattn, "tn", BF16, "g_w_out", tm=1024, tn=1024)
    dpa, dpb, dgates = _merge_bwd(dmerged, pa, pb, qg, gate_blk)
    do_a = _mm(dpa, w_bra, "nt", BF16, "d_o_a", tm=1024, tn=1024)
    do_b = _mm(dpb, w_brb, "nt", BF16, "d_o_b", tm=1024, tn=1024)
    g_w_bra = _mm(o_a, dpa, "tn", BF16, "g_w_br_a", tm=1024, tn=1024)
    g_w_brb = _mm(o_b, dpb, "tn", BF16, "g_w_br_b", tm=1024, tn=1024)
    tok = reduce_relay(r_up, g_w_brb)
    _after(tok)
    r_out = reduce_start("out", ["w_out", "w_br_a", "w_br_b"],
                         [g_w_out.reshape(N_DEV, d // N_DEV, d), to_shards(g_w_bra), to_shards(g_w_brb)])
    _after(tok, r_out["tok"])
    dq_a, dk_a, dv_a = _attention_bwd(q_a, kv_a, kv_a, do_a, lse_a, name="attn_a_bwd", **att_a)
    dq_b, dk_b, dv_b = _attention_bwd(q_b, k_b, v_b, do_b, lse_b, name="attn_b_bwd", **att_b)
    _after(reduce_relay(r_out, dv_b))
    dqa_raw = _rope_a(dq_a, q_tabs_a, True, BF16, "rope_q_bwd", sc_a * LN2)
    dcqn = _mm(dqa_raw, wq_ext, "nt", F32, "d_cqn", tm=1024, tn=ql)
    g_wq_ext = _mm(cqn, dqa_raw, "tn", BF16, "g_w_q_up", tm=ql, tn=1024)
    dq_p, st_q, st_qb = _q_prep_bwd(qg, mla_q_norm_g, gqa_q_norm_g, q_tabs_b, dcqn, dq_b, q_pad, sc_b * LN2)
    dkin = _mm_cat_nt([(dk_a, wkv_ext, 0), (dv_a, wkv_ext, ha * MLA_SLOT)], F32, "d_kin", tm=1152, tn=kvl + LANE)
    g_wkv_ext = _mm_cat_tn(kin, [dk_a, dv_a], BF16, "g_w_kv_up", tm=kvl + LANE, tn=min(1024, ha * MLA_V))
    dkv_p, st_kv, st_kb = _key_prep_bwd(kv_all, mla_kv_norm_g, gqa_k_norm_g, k_tabs, dkin, dk_b, dv_b)
    g_wq = g_wq_ext.reshape(ql, ha, MLA_SLOT)[:, :, :MLA_NOPE + MLA_ROPE].reshape(ql, ha * (MLA_NOPE + MLA_ROPE))
    g_wkv = jnp.concatenate([g_wkv_ext[:kvl, :ha * MLA_SLOT].reshape(kvl, ha, MLA_SLOT)[:, :, :MLA_NOPE],
                             g_wkv_ext[:kvl, ha * MLA_SLOT:].reshape(kvl, ha, MLA_V)], axis=2).reshape(kvl, ha * (MLA_NOPE + MLA_V))
    r_qkv = reduce_start("qkv", ["w_q_up", "w_kv_up"], [to_shards(g_wq), to_shards(g_wkv)])
    _after(r_qkv["tok"])
    g_wkv_p = _mm(z_all, dkv_p, "tn", BF16, "g_w_in_kv", tm=1024, tn=wkv_w)
    g_wqg_p = _mm_cat_tn(z_all, [dq_p, dgates], BF16, "g_w_in_qg", tm=1024, tn=min(1024, d), rows=t)
    g_w_in = jnp.concatenate([g_wkv_p[:, :kvl], g_wkv_p[:, kvl + 2 * nb:kvl + 2 * nb + MLA_ROPE],
                              g_wkv_p[:, kvl:kvl + 2 * nb], g_wqg_p[:, :q_w], g_wqg_p[:, q_w + q_pad:]], axis=1)
    r_in = reduce_start("in", ["w_in"], [to_shards(g_w_in)])
    _after(r_in["tok"])
    qw_p = q_w + q_pad
    dz_lat = _mm_sum_nt([(dq_p, 0, w_qg_p, 0, qw_p), (dgates, 0, w_qg_p, qw_p, d), (dgates, d, w_qg_p, qw_p + d, d),
                         (dkv_p, 0, w_kv_p, 0, wkv_w)], F32, "d_z_lat", rows=t)
    dz_ctx = _mm(dkv_p, w_kv_p, "nt", F32, "d_z_ctx", tm=min(ROW_BLOCK, tc), tn=1024, a_row_off=t)
    tok_q = reduce_relay(r_qkv, dz_ctx)
    _after(tok_q)
    grad_x, st_n1 = _norm1_bwd(cts, xs, norm1_g, mods1, dz_ctx, dz_lat, dx1)

    res = {}

    def upd(nm, parts):
        wv, mv, vv = weights[nm], mom_m[nm], mom_v[nm]
        if wv.ndim == 1:
            wv, mv, vv = (a.reshape(1, -1) for a in (wv, mv, vv))
        outs = _adamw(parts, wv, mv, vv, "adamw_" + nm)
        res[nm] = [o_.reshape(weights[nm].shape) for o_ in outs]

    d_lat = jnp.concatenate([st_n1[0], st_n1[1], st_n2[3], st_n2[0], st_n2[1], st_fin[1]])
    d_cxt = jnp.concatenate([st_n1[3], st_n1[4], jnp.zeros((4 * d,), F32)])
    small = jnp.concatenate([d_lat, d_cxt, st_n1[2], st_q[0], st_kv[0], st_qb[0], st_kb[0], st_n2[2],
                             jnp.concatenate([dcb[0, 0], dcb[1, 0]]), st_fin[0]])
    n_small = small.shape[0]
    pad_small = (-n_small) % LANE
    (small_all,) = _all_gather([jnp.pad(small, (0, pad_small)).reshape(1, -1)], "gather_small")
    offs = {}
    o = 0
    for nm, ln in (("d_lat", 6 * d), ("d_cxt", 6 * d), ("norm1_g", d), ("mla_q_norm_g", ql), ("mla_kv_norm_g", kvl),
                   ("gqa_q_norm_g", GQA_HEAD_DIM), ("gqa_k_norm_g", GQA_HEAD_DIM), ("norm2_g", d), ("conv_b", f2),
                   ("final_norm_g", d)):
        offs[nm] = (o, ln)
        o += ln

    def part(nm):
        a, ln = offs[nm]
        return small_all[:, :, a:a + ln]

    d_lat_all = part("d_lat")[:, 0, :]
    d_cxt_sum = _sum_parts(part("d_cxt"))
    da16 = jnp.concatenate([d_lat_all, d_cxt_sum, jnp.zeros((7, 6 * d), F32)], axis=0)
    da16_shard = lax.dynamic_slice_in_dim(da16, my_idx * ncol, ncol, axis=1)
    cc_part = _cctx_partial(da16_shard, w_ada[0], c_ctx[None, :])
    (cc_all,) = _all_gather([cc_part], "gather_cctx")
    cc_parts = cc_all[:, 0:1, :]
    tok_i = reduce_relay(r_in, cc_all)

    _after(tok_i)
    for nm in ("norm1_g", "mla_q_norm_g", "mla_kv_norm_g", "gqa_q_norm_g", "gqa_k_norm_g", "norm2_g", "conv_b",
               "final_norm_g"):
        upd(nm, part(nm))
    upd("c_ctx", cc_parts)
    b_parts = jnp.concatenate([d_lat_all[:, None, :], d_cxt_sum[None]], axis=0)
    upd("b_ada", b_parts)
    _after(tok_i)
    outs = _adamw_ada(conds, da16_shard, w_ada[0], m_w_ada[0], v_w_ada[0])
    res["w_ada"] = [o_[None] for o_ in outs]
    last = outs[0]
    for grp in (r_down, r_up, r_out, r_qkv, r_in):
        recv = reduce_finish(grp, last)
        for nm in grp["names"]:
            upd(nm, recv[nm][:, :3, :] if nm == "conv_w" else recv[nm])
            last = res[nm][0]

    return (loss, grad_x[None], *[res[n][0] for n in order], *[res[n][1] for n in order],
            *[res[n][2] for n in order], *[res[n][3] for n in order])
```

```python
import functools

import jax
import jax.numpy as jnp
from jax import lax
from jax.experimental import pallas as pl
from jax.experimental.pallas import tpu as pltpu

F32 = jnp.float32
BF16 = jnp.bfloat16

GRID_W = 64
ROPE_THETA = 10000.0
NORM_EPS = 1e-6
MLA_HEADS = 8
MLA_Q_LORA = 768
MLA_KV_LORA = 512
MLA_NOPE = 128
MLA_ROPE = 64
MLA_V = 128
GQA_HEADS = 8
GQA_KV_HEADS = 2
GQA_HEAD_DIM = 128
ADAM_LR = 0.001
ADAM_B1 = 0.9
ADAM_B2 = 0.999
ADAM_EPS = 1e-08
ADAM_WD = 0.01
ADAM_STEP = 10

N_DEV = 8
MESH_AXES = ("x", "y", "c")
LANE = 128
MLA_SLOT = 2 * LANE
VMEM_LIMIT = 56 * 1024 * 1024
ROW_BLOCK = 256
ATT_Q_BLOCK = 512
ATT_Q_BLOCK_FWD = 512
LN2 = 0.6931471805599453
LOG2E = 1.4426950408889634
MESH_ID = pl.DeviceIdType.MESH


def _tile(n, pref, align=LANE):
    if n <= pref:
        return n
    best = None
    t = align
    while t <= pref:
        if n % t == 0:
            best = t
        t += align
    assert best is not None, (n, pref, align)
    return best


def _cparams(sem=None):
    return pltpu.CompilerParams(dimension_semantics=sem, vmem_limit_bytes=VMEM_LIMIT)


_ORDER_AFTER = []


def _after(*arrays):
    _ORDER_AFTER.extend(arrays)


def _pcall(body, *, in_specs, **kw):
    deps = tuple(_ORDER_AFTER)
    _ORDER_AFTER.clear()
    if not deps:
        return pl.pallas_call(body, in_specs=in_specs, **kw)
    n_in, n_dep = len(in_specs), len(deps)

    def with_deps(*refs):
        body(*refs[:n_in], *refs[n_in + n_dep:])

    call = pl.pallas_call(with_deps, in_specs=list(in_specs) + [pl.BlockSpec(memory_space=pl.ANY)] * n_dep, **kw)
    return lambda *args: call(*args, *deps)


def _all_gather(arrs, name):
    n = len(arrs)

    def body(*refs):
        ins = refs[:n]
        outs = refs[n:2 * n]
        send_sems, recv_sems, local_sems = refs[2 * n:]
        x, y, c = lax.axis_index("x"), lax.axis_index("y"), lax.axis_index("c")
        me, sibling = (x, y, c), (x, y, 1 - c)
        chips = [(1 - x, y), (x, 1 - y), (1 - x, 1 - y)]

        def rows(a, dev):
            px, py, pc = dev
            return outs[a].at[4 * px + 2 * py + pc]

        def copy(a, k, block, to, src=None):
            return pltpu.make_async_remote_copy(
                src_ref=rows(a, block) if src is None else src,
                dst_ref=rows(a, block),
                send_sem=send_sems.at[7 * a + k],
                recv_sem=recv_sems.at[7 * a + k],
                device_id=to,
                device_id_type=MESH_ID,
            )

        mine = [pltpu.make_async_copy(ins[a], rows(a, me), local_sems.at[a]) for a in range(n)]
        for cp in mine:
            cp.start()
        first = []
        for a in range(n):
            first.append(copy(a, 0, me, sibling, src=ins[a]))
            first += [copy(a, 1 + j, me, (*chip, c), src=ins[a]) for j, chip in enumerate(chips)]
        for cp in first:
            cp.start()
        passed = []
        for j, chip in enumerate(chips):
            for a in range(n):
                copy(a, 1 + j, (*chip, c), me).wait_recv()
                fwd = copy(a, 4 + j, (*chip, c), sibling)
                fwd.start()
                passed.append(fwd)
        for a in range(n):
            copy(a, 0, sibling, me).wait_recv()
            for j, chip in enumerate(chips):
                copy(a, 4 + j, (*chip, 1 - c), me).wait_recv()
        for cp in first + passed:
            cp.wait_send()
        for cp in mine:
            cp.wait()

    any_spec = pl.BlockSpec(memory_space=pl.ANY)
    outs = _pcall(
        body,
        name=name,
        out_shape=[jax.ShapeDtypeStruct((N_DEV,) + a.shape, a.dtype) for a in arrs],
        in_specs=[any_spec] * n,
        out_specs=[any_spec] * n,
        scratch_shapes=[
            pltpu.SemaphoreType.DMA((7 * n,)),
            pltpu.SemaphoreType.DMA((7 * n,)),
            pltpu.SemaphoreType.DMA((n,)),
        ],
    )(*arrs)
    return list(outs)


def _all_to_all(arrs, name):
    n = len(arrs)

    def body(*refs):
        ins = refs[:n]
        outs = refs[n:2 * n]
        send_sems, recv_sems, local_sems = refs[2 * n:]
        x, y, c = lax.axis_index("x"), lax.axis_index("y"), lax.axis_index("c")
        my_idx = 4 * x + 2 * y + c

        def peer(k):
            fx, fy, fc = (k >> 2) & 1, (k >> 1) & 1, k & 1
            return (x ^ fx if fx else x, y ^ fy if fy else y, c ^ fc if fc else c)

        def copy(a, k):
            px, py, pc = peer(k)
            return pltpu.make_async_remote_copy(
                src_ref=ins[a].at[4 * px + 2 * py + pc],
                dst_ref=outs[a].at[my_idx],
                send_sem=send_sems.at[7 * a + k - 1],
                recv_sem=recv_sems.at[7 * a + k - 1],
                device_id=(px, py, pc),
                device_id_type=MESH_ID,
            )

        mine = [pltpu.make_async_copy(ins[a].at[my_idx], outs[a].at[my_idx], local_sems.at[a]) for a in range(n)]
        for cp in mine:
            cp.start()
        order = [1, 4, 2, 5, 3, 6, 7]
        cps = [copy(a, k) for k in order for a in range(n)]
        for cp in cps:
            cp.start()
        for cp in cps:
            cp.wait()
        for cp in mine:
            cp.wait()

    any_spec = pl.BlockSpec(memory_space=pl.ANY)
    outs = _pcall(
        body,
        name=name,
        out_shape=[jax.ShapeDtypeStruct(a.shape, a.dtype) for a in arrs],
        in_specs=[any_spec] * n,
        out_specs=[any_spec] * n,
        scratch_shapes=[
            pltpu.SemaphoreType.DMA((7 * n,)),
            pltpu.SemaphoreType.DMA((7 * n,)),
            pltpu.SemaphoreType.DMA((n,)),
        ],
    )(*arrs)
    return list(outs)


_HBM = pl.BlockSpec(memory_space=pltpu.HBM)
_SEM = pl.BlockSpec(memory_space=pltpu.SEMAPHORE)
_EFFECT = pltpu.SideEffectType.DATAFLOW_SIDE_EFFECTING


def _descriptors(copies, send_sems, recv_sems):
    descs = []
    for i, (src, dst, dev) in enumerate(copies):
        if dev is None:
            descs.append(pltpu.make_async_copy(src, dst, recv_sems.at[i]))
        else:
            descs.append(pltpu.make_async_remote_copy(src_ref=src, dst_ref=dst, send_sem=send_sems.at[i],
                                                      recv_sem=recv_sems.at[i], device_id=dev, device_id_type=MESH_ID))
    return descs


def _split_start(name, arrays, copies_fn, n_copies):
    n = len(arrays)

    def body(*refs):
        send_sems, recv_sems = refs[n], refs[n + 1]
        token = refs[2 * n + 2]
        for dsc in _descriptors(copies_fn(refs[:n]), send_sems, recv_sems):
            dsc.start()
        token[...] = jnp.zeros_like(token)

    outs = _pcall(
        body,
        name=name,
        out_shape=(pltpu.SemaphoreType.DMA((n_copies,)), pltpu.SemaphoreType.DMA((n_copies,)),
                   *[pltpu.HBM(a.shape, a.dtype) for a in arrays], jax.ShapeDtypeStruct((8, LANE), F32)),
        in_specs=[_HBM] * n,
        out_specs=(_SEM, _SEM, *[_HBM] * n, pl.BlockSpec(memory_space=pltpu.VMEM)),
        input_output_aliases={i: 2 + i for i in range(n)},
        compiler_params=pltpu.CompilerParams(has_side_effects=_EFFECT),
    )(*[pltpu.with_memory_space_constraint(a, pltpu.HBM) for a in arrays])
    return outs[0], outs[1], list(outs[2:2 + n]), outs[2 + n]


def _split_wait(name, send_sems, recv_sems, arrays, copies_fn, after):
    n = len(arrays)

    def body(*refs):
        for dsc, (_, _, dev) in zip(_descriptors(copies_fn(refs[:n]), refs[n], refs[n + 1]), copies_fn(refs[:n])):
            if dev is None:
                dsc.wait()
            else:
                dsc.wait_send()
                dsc.wait_recv()

    outs = _pcall(
        body,
        name=name,
        out_shape=tuple(pltpu.HBM(a.shape, a.dtype) for a in arrays),
        in_specs=[_HBM] * n + [_SEM, _SEM, pl.BlockSpec(memory_space=pl.ANY)],
        out_specs=tuple([_HBM] * n),
        input_output_aliases={i: i for i in range(n)},
        compiler_params=pltpu.CompilerParams(has_side_effects=_EFFECT),
    )(*arrays, send_sems, recv_sems, after)
    return list(outs)


def _mesh_pos():
    x, y, c = lax.axis_index("x"), lax.axis_index("y"), lax.axis_index("c")
    return x, y, c, [(1 - x, y), (x, 1 - y), (1 - x, 1 - y)]


def _gather_ici_copies(n):
    def copies(refs):
        x, y, c, chips = _mesh_pos()
        me = 4 * x + 2 * y + c
        out = []
        for a in range(n):
            src, buf = refs[a], refs[n + a]
            out.append((src, buf.at[me], None))
            out.append((src, buf.at[me], (x, y, 1 - c)))
            out += [(src, buf.at[me], (cx, cy, c)) for cx, cy in chips[:2]]
        return out
    return copies


def _gather_pass_copies(n):
    def copies(refs):
        x, y, c, chips = _mesh_pos()
        south = c == 0
        bx, by = jnp.where(south, 1 - x, x), jnp.where(south, y, 1 - y)
        tx, ty = jnp.where(south, x, 1 - x), jnp.where(south, 1 - y, y)
        out = []
        for a in range(n):
            rows = refs[a].at[4 * bx + 2 * by + c]
            out.append((rows, rows, (tx, ty, c)))
            for cx, cy in chips[:2]:
                rows = refs[a].at[4 * cx + 2 * cy + c]
                out.append((rows, rows, (x, y, 1 - c)))
        return out
    return copies


def _gather_d2d_copies(n):
    def copies(refs):
        x, y, c, chips = _mesh_pos()
        cx, cy = chips[2]
        out = []
        for a in range(n):
            rows = refs[a].at[4 * cx + 2 * cy + c]
            out.append((rows, rows, (x, y, 1 - c)))
        return out
    return copies


def _reduce_d2d_copies(n):
    def copies(refs):
        x, y, c, _ = _mesh_pos()
        out = []
        for a in range(n):
            for k in range(4):
                out.append((refs[a].at[2 * k + (1 - c)], refs[n + a].at[k], (x, y, 1 - c)))
        return out
    return copies


def _reduce_ici_copies(n):
    def copies(refs):
        x, y, c, chips = _mesh_pos()
        mine = 2 * x + y
        out = []
        for a in range(n):
            src, land = refs[a], refs[n + a]
            out.append((src.at[mine], land.at[mine], None))
            out += [(src.at[2 * cx + cy], land.at[mine], (cx, cy, c)) for cx, cy in chips]
        return out
    return copies


def _pair_sum(send, land, c_idx, name):
    _, r, cols = send.shape
    rb = _tile(r, max(8, (1 << 22) // (send.dtype.itemsize * cols) // 8 * 8), 8)
    dt = send.dtype

    def body(c_ref, s_ref, l_ref, o_ref):
        o_ref[...] = (s_ref[...].astype(F32) + l_ref[...].astype(F32)).astype(dt)

    return pl.pallas_call(
        body,
        name=name,
        out_shape=jax.ShapeDtypeStruct((4, r, cols), dt),
        grid_spec=pltpu.PrefetchScalarGridSpec(
            num_scalar_prefetch=1,
            grid=(4, r // rb),
            in_specs=[pl.BlockSpec((None, rb, cols), lambda k, i, c_ref: (2 * k + c_ref[0], i, 0)),
                      pl.BlockSpec((None, rb, cols), lambda k, i, c_ref: (k, i, 0))],
            out_specs=pl.BlockSpec((None, rb, cols), lambda k, i, c_ref: (k, i, 0)),
        ),
        compiler_params=_cparams(("parallel", "parallel")),
    )(c_idx, send, land)


_DIMS = {
    "nn": (((1,), (0,)), ((), ())),
    "nt": (((1,), (1,)), ((), ())),
    "tn": (((0,), (0,)), ((), ())),
}


def _mm_call(a, b, *, mode, grid, a_spec, b_spec, o_spec, out_shape, acc_shape, name):
    nk = grid[2]
    out_dtype = out_shape.dtype

    def body(a_ref, b_ref, o_ref, *scratch):
        p = lax.dot_general(a_ref[...].astype(BF16), b_ref[...].astype(BF16), _DIMS[mode],
                            preferred_element_type=F32)
        if nk == 1:
            o_ref[...] = p.astype(out_dtype)
        else:
            acc = scratch[0]
            k = pl.program_id(2)

            @pl.when(k == 0)
            def _():
                acc[...] = p

            @pl.when(k > 0)
            def _():
                acc[...] += p

            @pl.when(k == nk - 1)
            def _():
                o_ref[...] = acc[...].astype(out_dtype)

    return _pcall(
        body,
        name=name,
        out_shape=out_shape,
        grid=grid,
        in_specs=[a_spec, b_spec],
        out_specs=o_spec,
        scratch_shapes=[pltpu.VMEM(acc_shape, F32)] if nk > 1 else [],
        compiler_params=_cparams(("parallel", "parallel", "arbitrary")),
    )(a, b)


def _mm(a, b, mode, out_dtype, name, tm=512, tn=512, tk=2432, a_row_off=0, rows=None):
    if mode == "nn":
        (m, k), (k2, n) = a.shape, b.shape
    elif mode == "nt":
        (m, k), (n, k2) = a.shape, b.shape
    else:
        (k, m), (k2, n) = a.shape, b.shape
        if rows is not None:
            k = k2 = rows
    assert k == k2, (a.shape, b.shape, mode)
    if mode != "tn":
        m = (m if rows is None else rows + a_row_off) - a_row_off
    tm, tn, tk = _tile(m, tm, 8), _tile(n, tn), _tile(k, tk, 8 if mode == "tn" else LANE)
    assert a_row_off % tm == 0
    ro = a_row_off // tm
    grid = (m // tm, n // tn, k // tk)
    if mode == "tn":
        a_spec = pl.BlockSpec((tk, tm), lambda i, j, kk: (kk, i))
    else:
        a_spec = pl.BlockSpec((tm, tk), lambda i, j, kk: (i + ro, kk))
    if mode == "nt":
        b_spec = pl.BlockSpec((tn, tk), lambda i, j, kk: (j, kk))
    else:
        b_spec = pl.BlockSpec((tk, tn), lambda i, j, kk: (kk, j))
    o_spec = pl.BlockSpec((tm, tn), lambda i, j, kk: (i, j))
    return _mm_call(a, b, mode=mode, grid=grid, a_spec=a_spec, b_spec=b_spec, o_spec=o_spec,
                    out_shape=jax.ShapeDtypeStruct((m, n), out_dtype), acc_shape=(tm, tn), name=name)


def _mm_cat_nt(pieces, out_dtype, name, tm=1024, tn=1024, tk=2048, rows=None):
    m = pieces[0][0].shape[0] if rows is None else rows
    n = pieces[0][1].shape[0]
    tm, tn = _tile(m, tm, 8), _tile(n, tn)
    steps, starts, s = [], [], 0
    for a, b, off in pieces:
        kp = a.shape[1]
        tkp = _tile(kp, tk)
        assert off % tkp == 0 and b.shape[0] == n
        steps.append((tkp, kp // tkp, off // tkp))
        starts.append(s)
        s += kp // tkp
    nk = s
    npc = len(pieces)

    def body(*refs):
        o_ref, acc = refs[2 * npc], refs[2 * npc + 1]
        kk = pl.program_id(2)

        @pl.when(kk == 0)
        def _():
            acc[...] = jnp.zeros_like(acc)

        for p in range(npc):
            @pl.when((kk >= starts[p]) & (kk < starts[p] + steps[p][1]))
            def _(p=p):
                acc[...] += lax.dot_general(refs[2 * p][...].astype(BF16), refs[2 * p + 1][...].astype(BF16), _DIMS["nt"],
                                            preferred_element_type=F32)

        @pl.when(kk == nk - 1)
        def _():
            o_ref[...] = acc[...].astype(out_dtype)

    in_specs, args = [], []
    for p, (a, b, off) in enumerate(pieces):
        tkp, np_, ob = steps[p]

        def rel(kk, p=p, np_=np_):
            return jnp.clip(kk - starts[p], 0, np_ - 1)

        in_specs.append(pl.BlockSpec((tm, tkp), lambda i, j, kk, rel=rel: (i, rel(kk))))
        in_specs.append(pl.BlockSpec((tn, tkp), lambda i, j, kk, rel=rel, ob=ob: (j, ob + rel(kk))))
        args += [a, b]
    return _pcall(
        body,
        name=name,
        out_shape=jax.ShapeDtypeStruct((m, n), out_dtype),
        grid=(m // tm, n // tn, nk),
        in_specs=in_specs,
        out_specs=pl.BlockSpec((tm, tn), lambda i, j, kk: (i, j)),
        scratch_shapes=[pltpu.VMEM((tm, tn), F32)],
        compiler_params=_cparams(("parallel", "parallel", "arbitrary")),
    )(*args)


def _mm_cat_tn(a, pieces, out_dtype, name, tm=1024, tn=1024, rows=None):
    k = a.shape[0] if rows is None else rows
    m = a.shape[1]
    tm = _tile(m, tm)
    starts, s = [], 0
    for b in pieces:
        assert b.shape[1] % tn == 0
        starts.append(s)
        s += b.shape[1] // tn
    nj = s
    npc = len(pieces)

    def body(*refs):
        a_ref, o_ref = refs[0], refs[1 + npc]
        j = pl.program_id(1)
        for p in range(npc):
            @pl.when((j >= starts[p]) & (j < starts[p] + pieces[p].shape[1] // tn))
            def _(p=p):
                o_ref[...] = lax.dot_general(a_ref[...].astype(BF16), refs[1 + p][...].astype(BF16), _DIMS["tn"],
                                             preferred_element_type=F32).astype(out_dtype)

    in_specs = [pl.BlockSpec((k, tm), lambda i, j: (0, i))]
    for p, b in enumerate(pieces):
        np_ = b.shape[1] // tn
        in_specs.append(pl.BlockSpec((k, tn), lambda i, j, p=p, np_=np_: (0, jnp.clip(j - starts[p], 0, np_ - 1))))
    return _pcall(
        body,
        name=name,
        out_shape=jax.ShapeDtypeStruct((m, nj * tn), out_dtype),
        grid=(m // tm, nj),
        in_specs=in_specs,
        out_specs=pl.BlockSpec((tm, tn), lambda i, j: (i, j)),
        compiler_params=_cparams(("parallel", "arbitrary")),
    )(a, *pieces)


def _mm_up_fwd(z2, w3, name, tm=1024):
    t, d = z2.shape
    nsh, _, c = w3.shape
    tm = _tile(t, tm, 8)
    return _mm_call(z2, w3, mode="nn", grid=(t // tm, nsh, 1),
                    a_spec=pl.BlockSpec((tm, d), lambda i, j, kk: (i, 0)),
                    b_spec=pl.BlockSpec((None, d, c), lambda i, j, kk: (j, 0, 0)),
                    o_spec=pl.BlockSpec((tm, c), lambda i, j, kk: (i, j)),
                    out_shape=jax.ShapeDtypeStruct((t, nsh * c), BF16), acc_shape=(tm, c), name=name)


def _mm_up_dz(du3, w3, name, tm=512, tn=1024):
    _, t, f = du3.shape
    nsh, d, c = w3.shape
    half = nsh // 2
    assert f == half * c
    tm, tn = _tile(t, tm, 8), _tile(d, tn)

    def body(a_ref, b_ref, o_ref, acc):
        kk = pl.program_id(2)
        p = None
        for s in range(half):
            q = lax.dot_general(a_ref[:, s * c:(s + 1) * c], b_ref[s], _DIMS["nt"], preferred_element_type=F32)
            p = q if p is None else p + q

        @pl.when(kk == 0)
        def _():
            acc[...] = p

        @pl.when(kk == 1)
        def _():
            o_ref[...] = (acc[...] + p).astype(BF16)

    return _pcall(
        body,
        name=name,
        out_shape=jax.ShapeDtypeStruct((t, d), BF16),
        grid=(t // tm, d // tn, 2),
        in_specs=[pl.BlockSpec((None, tm, f), lambda i, j, kk: (kk, i, 0)),
                  pl.BlockSpec((half, tn, c), lambda i, j, kk: (kk, j, 0))],
        out_specs=pl.BlockSpec((tm, tn), lambda i, j, kk: (i, j)),
        scratch_shapes=[pltpu.VMEM((tm, tn), F32)],
        compiler_params=_cparams(("parallel", "parallel", "arbitrary")),
    )(du3, w3)


def _mm_sum_nt(pieces, out_dtype, name, tm=512, tn=512, rows=None):
    m = pieces[0][0].shape[0] if rows is None else rows
    n = pieces[0][2].shape[0]
    tm, tn = _tile(m, tm, 8), _tile(n, tn)
    npc = len(pieces)

    def body(*refs):
        p = None
        for s in range(npc):
            q = lax.dot_general(refs[2 * s][...].astype(BF16), refs[2 * s + 1][...].astype(BF16), _DIMS["nt"],
                                preferred_element_type=F32)
            p = q if p is None else p + q
        refs[2 * npc][...] = p.astype(out_dtype)

    in_specs, args = [], []
    for a, ao, b, bo, kp in pieces:
        assert ao % kp == 0 and bo % kp == 0 and b.shape[0] == n
        in_specs.append(pl.BlockSpec((tm, kp), lambda i, j, ab=ao // kp: (i, ab)))
        in_specs.append(pl.BlockSpec((tn, kp), lambda i, j, bb=bo // kp: (j, bb)))
        args += [a, b]
    return _pcall(
        body,
        name=name,
        out_shape=jax.ShapeDtypeStruct((m, n), out_dtype),
        grid=(m // tm, n // tn),
        in_specs=in_specs,
        out_specs=pl.BlockSpec((tm, tn), lambda i, j: (i, j)),
        compiler_params=_cparams(("parallel", "parallel")),
    )(*args)


def _mm_up_gw(z2, du3, nsh, name, tm=1024):
    t, d = z2.shape
    f = du3.shape[2]
    half = nsh // 2
    c = f // half
    tm = _tile(d, tm)
    return _mm_call(z2, du3, mode="tn", grid=(d // tm, nsh, 1),
                    a_spec=pl.BlockSpec((t, tm), lambda i, j, kk: (0, i)),
                    b_spec=pl.BlockSpec((None, t, c), lambda i, j, kk: (j // half, 0, j % half)),
                    o_spec=pl.BlockSpec((None, tm, c), lambda i, j, kk: (j, i, 0)),
                    out_shape=jax.ShapeDtypeStruct((nsh, d, c), BF16), acc_shape=(tm, c), name=name)


def _rms(x):
    r = lax.rsqrt(jnp.mean(x * x, axis=-1, keepdims=True) + NORM_EPS)
    return x * r, r


def _rms_bwd(dxh, xh, r):
    return r * (dxh - xh * jnp.mean(dxh * xh, axis=-1, keepdims=True))


def _colsum(v):
    return jnp.sum(v, axis=0, keepdims=True)


def _rope(v, c, s1, s2, q):
    w = v.shape[-1]
    return v * c + pltpu.roll(v, w - q, 1) * s1 + pltpu.roll(v, q, 1) * s2


def _rope_t(d, c, s1, s2, q):
    w = d.shape[-1]
    return d * c + pltpu.roll(d * s1, q, 1) + pltpu.roll(d * s2, w - q, 1)


def _norm_mod_fwd(ctx, x, gain, mods):
    tc, d = ctx.shape
    t = x.shape[0]
    rb = min(ROW_BLOCK, tc)
    nbl = t // rb

    def body(ctx_ref, x_ref, g_ref, mod_ref, z_ref):
        i = pl.program_id(0)

        def emit(src, sh, sc):
            xh, _ = _rms(src[...])
            z_ref[...] = ((xh * g_ref[...]) * (1.0 + sc) + sh).astype(BF16)

        @pl.when(i >= nbl)
        def _():
            emit(ctx_ref, mod_ref[2:3, :], mod_ref[3:4, :])

        @pl.when(i < nbl)
        def _():
            emit(x_ref, mod_ref[0:1, :], mod_ref[1:2, :])

    return _pcall(
        body,
        name="norm1_mod_fwd",
        out_shape=jax.ShapeDtypeStruct((tc + t, d), BF16),
        grid=((tc + t) // rb,),
        in_specs=[
            pl.BlockSpec((rb, d), lambda i: (jnp.maximum(i - nbl, 0), 0)),
            pl.BlockSpec((rb, d), lambda i: (jnp.minimum(i, nbl - 1), 0)),
            pl.BlockSpec((1, d), lambda i: (0, 0)),
            pl.BlockSpec((8, d), lambda i: (0, 0)),
        ],
        out_specs=pl.BlockSpec((rb, d), lambda i: (i, 0)),
        compiler_params=_cparams(("arbitrary",)),
    )(ctx, x, gain, mods)


def _norm1_bwd(ctx, x, gain, mods, dz_ctx, dz_lat, dx1):
    tc, d = ctx.shape
    t = x.shape[0]
    rb = min(ROW_BLOCK, tc)
    nbl = t // rb

    def body(ctx_ref, x_ref, g_ref, mod_ref, dzc_ref, dzl_ref, dx1_ref, gx_ref, st_ref):
        i = pl.program_id(0)

        @pl.when(i == 0)
        def _():
            st_ref[...] = jnp.zeros_like(st_ref)

        def common(src, dz, sc, row_sh, row_sc):
            xh, r = _rms(src[...])
            g = g_ref[...]
            dxn = dz * (1.0 + sc)
            st_ref[row_sh:row_sh + 1, :] += _colsum(dz)
            st_ref[row_sc:row_sc + 1, :] += _colsum(dz * (xh * g))
            st_ref[2:3, :] += _colsum(dxn * xh)
            return _rms_bwd(dxn * g, xh, r)

        @pl.when(i >= nbl)
        def _():
            common(ctx_ref, dzc_ref[...], mod_ref[3:4, :], 3, 4)

        @pl.when(i < nbl)
        def _():
            gx_ref[...] = dx1_ref[...] + common(x_ref, dzl_ref[...], mod_ref[1:2, :], 0, 1)

    lat = lambda i: (jnp.minimum(i, nbl - 1), 0)
    cix = lambda i: (jnp.maximum(i - nbl, 0), 0)
    return _pcall(
        body,
        name="norm1_mod_bwd",
        out_shape=[jax.ShapeDtypeStruct((t, d), F32), jax.ShapeDtypeStruct((8, d), F32)],
        grid=((tc + t) // rb,),
        in_specs=[
            pl.BlockSpec((rb, d), cix),
            pl.BlockSpec((rb, d), lat),
            pl.BlockSpec((1, d), lambda i: (0, 0)),
            pl.BlockSpec((8, d), lambda i: (0, 0)),
            pl.BlockSpec((rb, d), cix),
            pl.BlockSpec((rb, d), lat),
            pl.BlockSpec((rb, d), lat),
        ],
        out_specs=[pl.BlockSpec((rb, d), lat), pl.BlockSpec((8, d), lambda i: (0, 0))],
        compiler_params=_cparams(("arbitrary",)),
    )(ctx, x, gain, mods, dz_ctx, dz_lat, dx1)


def _key_prep_fwd(kv, kv_gain, kb_gain, tabs):
    ta, wkv = kv.shape
    kvl = MLA_KV_LORA
    nb = GQA_KV_HEADS * GQA_HEAD_DIM
    rb = ROW_BLOCK if ta % ROW_BLOCK == 0 else LANE
    hd = GQA_HEAD_DIM

    def body(kv_ref, g_ref, gb_ref, ca, s1a, s2a, cb, s1b, s2b, kin_ref, kb_ref, vb_ref):
        xh, _ = _rms(kv_ref[:, 0:kvl])
        kin_ref[:, 0:kvl] = (xh * g_ref[...]).astype(BF16)
        kpe = kv_ref[:, kvl + 2 * nb:kvl + 2 * nb + LANE]
        kin_ref[:, kvl:kvl + LANE] = _rope(kpe, ca[...], s1a[...], s2a[...], MLA_ROPE // 4).astype(BF16)
        for h in range(GQA_KV_HEADS):
            nh, _ = _rms(kv_ref[:, kvl + h * hd:kvl + (h + 1) * hd])
            kb_ref[:, h * hd:(h + 1) * hd] = _rope(nh * gb_ref[...], cb[...], s1b[...], s2b[...], hd // 4).astype(BF16)
        vb_ref[...] = kv_ref[:, kvl + nb:kvl + 2 * nb].astype(BF16)

    row = lambda w: pl.BlockSpec((rb, w), lambda i: (i, 0))
    fix = lambda w: pl.BlockSpec((1, w), lambda i: (0, 0))
    return _pcall(
        body,
        name="key_prep_fwd",
        out_shape=[jax.ShapeDtypeStruct((ta, kvl + LANE), BF16), jax.ShapeDtypeStruct((ta, nb), BF16),
                   jax.ShapeDtypeStruct((ta, nb), BF16)],
        grid=(ta // rb,),
        in_specs=[row(wkv), fix(kvl), fix(hd)] + [row(LANE)] * 3 + [row(hd)] * 3,
        out_specs=[row(kvl + LANE), row(nb), row(nb)],
        compiler_params=_cparams(("parallel",)),
    )(kv, kv_gain, kb_gain, *tabs)


def _key_prep_bwd(kv, kv_gain, kb_gain, tabs, dkin, dkb, dvb):
    ta, wkv = kv.shape
    kvl = MLA_KV_LORA
    nb = GQA_KV_HEADS * GQA_HEAD_DIM
    rb = ROW_BLOCK if ta % ROW_BLOCK == 0 else LANE
    hd = GQA_HEAD_DIM

    def body(kv_ref, g_ref, gb_ref, ca, s1a, s2a, cb, s1b, s2b, dkin_ref, dkb_ref, dvb_ref, dkv_ref, st_ref, stb_ref):
        @pl.when(pl.program_id(0) == 0)
        def _():
            st_ref[...] = jnp.zeros_like(st_ref)
            stb_ref[...] = jnp.zeros_like(stb_ref)

        xh, r = _rms(kv_ref[:, 0:kvl])
        dn = dkin_ref[:, 0:kvl]
        st_ref[0:1, :] += _colsum(dn * xh)
        dkv_ref[:, 0:kvl] = _rms_bwd(dn * g_ref[...], xh, r).astype(BF16)
        dpe = _rope_t(dkin_ref[:, kvl:kvl + LANE], ca[...], s1a[...], s2a[...], MLA_ROPE // 4)
        dkv_ref[:, kvl + 2 * nb:kvl + 2 * nb + LANE] = dpe.astype(BF16)
        for h in range(GQA_KV_HEADS):
            nh, rh = _rms(kv_ref[:, kvl + h * hd:kvl + (h + 1) * hd])
            dn_h = _rope_t(dkb_ref[:, h * hd:(h + 1) * hd], cb[...], s1b[...], s2b[...], hd // 4)
            stb_ref[0:1, :] += _colsum(dn_h * nh)
            dkv_ref[:, kvl + h * hd:kvl + (h + 1) * hd] = _rms_bwd(dn_h * gb_ref[...], nh, rh).astype(BF16)
        dkv_ref[:, kvl + nb:kvl + 2 * nb] = dvb_ref[...].astype(BF16)

    row = lambda w: pl.BlockSpec((rb, w), lambda i: (i, 0))
    fix = lambda w: pl.BlockSpec((1, w), lambda i: (0, 0))
    return _pcall(
        body,
        name="key_prep_bwd",
        out_shape=[jax.ShapeDtypeStruct((ta, wkv), BF16), jax.ShapeDtypeStruct((8, kvl), F32),
                   jax.ShapeDtypeStruct((8, hd), F32)],
        grid=(ta // rb,),
        in_specs=[row(wkv), fix(kvl), fix(hd)] + [row(LANE)] * 3 + [row(hd)] * 3 + [row(kvl + LANE), row(nb), row(nb)],
        out_specs=[row(wkv), pl.BlockSpec((8, kvl), lambda i: (0, 0)), pl.BlockSpec((8, hd), lambda i: (0, 0))],
        compiler_params=_cparams(("arbitrary",)),
    )(kv, kv_gain, kb_gain, *tabs, dkin, dkb, dvb)


def _q_prep_fwd(qg, q_gain, qb_gain, tabs, qscale):
    t = qg.shape[0]
    ql = MLA_Q_LORA
    hd = GQA_HEAD_DIM
    hb = GQA_HEADS * hd
    rb = min(ROW_BLOCK, t)

    def body(q_ref, g_ref, gb_ref, cb, s1b, s2b, cqn_ref, qb_ref):
        xh, _ = _rms(q_ref[:, 0:ql])
        cqn_ref[...] = (xh * g_ref[...]).astype(BF16)
        for h in range(GQA_HEADS):
            nh, _ = _rms(q_ref[:, ql + h * hd:ql + (h + 1) * hd])
            qh = _rope(nh * gb_ref[...], cb[...], s1b[...], s2b[...], hd // 4)
            qb_ref[:, h * hd:(h + 1) * hd] = (qh * qscale).astype(BF16)

    row = lambda w: pl.BlockSpec((rb, w), lambda i: (i, 0))
    fix = lambda w: pl.BlockSpec((1, w), lambda i: (0, 0))
    return _pcall(
        body,
        name="q_prep_fwd",
        out_shape=[jax.ShapeDtypeStruct((t, ql), BF16), jax.ShapeDtypeStruct((t, hb), BF16)],
        grid=(t // rb,),
        in_specs=[row(ql + hb), fix(ql), fix(hd)] + [row(hd)] * 3,
        out_specs=[row(ql), row(hb)],
        compiler_params=_cparams(("parallel",)),
    )(qg, q_gain, qb_gain, *tabs)


def _q_prep_bwd(qg, q_gain, qb_gain, tabs, dcqn, dqb, wpad, qscale):
    t = qg.shape[0]
    ql = MLA_Q_LORA
    hd = GQA_HEAD_DIM
    hb = GQA_HEADS * hd
    rb = min(ROW_BLOCK, t)

    def body(q_ref, g_ref, gb_ref, cb, s1b, s2b, dcqn_ref, dqb_ref, dq_ref, st_ref, stb_ref):
        @pl.when(pl.program_id(0) == 0)
        def _():
            st_ref[...] = jnp.zeros_like(st_ref)
            stb_ref[...] = jnp.zeros_like(stb_ref)

        xh, r = _rms(q_ref[:, 0:ql])
        dn = dcqn_ref[...]
        st_ref[0:1, :] += _colsum(dn * xh)
        dq_ref[:, 0:ql] = _rms_bwd(dn * g_ref[...], xh, r).astype(BF16)
        for h in range(GQA_HEADS):
            nh, rh = _rms(q_ref[:, ql + h * hd:ql + (h + 1) * hd])
            dn_h = _rope_t(dqb_ref[:, h * hd:(h + 1) * hd] * qscale, cb[...], s1b[...], s2b[...], hd // 4)
            stb_ref[0:1, :] += _colsum(dn_h * nh)
            dq_ref[:, ql + h * hd:ql + (h + 1) * hd] = _rms_bwd(dn_h * gb_ref[...], nh, rh).astype(BF16)
        if wpad:
            dq_ref[:, ql + hb:ql + hb + wpad] = jnp.zeros((rb, wpad), BF16)

    row = lambda w: pl.BlockSpec((rb, w), lambda i: (i, 0))
    fix = lambda w: pl.BlockSpec((1, w), lambda i: (0, 0))
    return _pcall(
        body,
        name="q_prep_bwd",
        out_shape=[jax.ShapeDtypeStruct((t, ql + hb + wpad), BF16), jax.ShapeDtypeStruct((8, ql), F32),
                   jax.ShapeDtypeStruct((8, hd), F32)],
        grid=(t // rb,),
        in_specs=[row(ql + hb), fix(ql), fix(hd)] + [row(hd)] * 3 + [row(ql), row(hb)],
        out_specs=[row(ql + hb + wpad), pl.BlockSpec((8, ql), lambda i: (0, 0)), pl.BlockSpec((8, hd), lambda i: (0, 0))],
        compiler_params=_cparams(("arbitrary",)),
    )(qg, q_gain, qb_gain, *tabs, dcqn, dqb)


def _rope_a(v, tabs, transpose, out_dtype, name, qscale):
    t, w = v.shape
    rb = min(ROW_BLOCK, t)
    fn = _rope_t if transpose else _rope

    def body(v_ref, c, s1, s2, o_ref):
        for h in range(w // MLA_SLOT):
            sl = slice(h * MLA_SLOT, (h + 1) * MLA_SLOT)
            o_ref[:, sl] = (fn(v_ref[:, sl].astype(F32), c[...], s1[...], s2[...], MLA_ROPE // 4) * qscale).astype(out_dtype)

    row = lambda ww: pl.BlockSpec((rb, ww), lambda i: (i, 0))
    return _pcall(
        body,
        name=name,
        out_shape=jax.ShapeDtypeStruct((t, w), out_dtype),
        grid=(t // rb,),
        in_specs=[row(w)] + [row(MLA_SLOT)] * 3,
        out_specs=row(w),
        compiler_params=_cparams(("parallel",)),
    )(v, *tabs)


def _merge_fwd(pa, pb, qg, gate_blk):
    t, d = pa.shape
    rb = min(ROW_BLOCK, t)

    def body(pa_ref, pb_ref, ga_ref, gb_ref, o_ref):
        o_ref[...] = (jax.nn.sigmoid(ga_ref[...]) * pa_ref[...].astype(F32)
                      + jax.nn.sigmoid(gb_ref[...]) * pb_ref[...].astype(F32)).astype(BF16)

    row = pl.BlockSpec((rb, d), lambda i: (i, 0))
    return _pcall(
        body,
        name="merge_fwd",
        out_shape=jax.ShapeDtypeStruct((t, d), BF16),
        grid=(t // rb,),
        in_specs=[row, row, pl.BlockSpec((rb, d), lambda i: (i, gate_blk)), pl.BlockSpec((rb, d), lambda i: (i, gate_blk + 1))],
        out_specs=row,
        compiler_params=_cparams(("parallel",)),
    )(pa, pb, qg, qg)


def _merge_bwd(dm, pa, pb, qg, gate_blk):
    t, d = pa.shape
    rb = min(ROW_BLOCK, t)

    def body(dm_ref, pa_ref, pb_ref, ga_ref, gb_ref, dpa_ref, dpb_ref, dg_ref):
        dmv = dm_ref[...].astype(F32)
        sa = jax.nn.sigmoid(ga_ref[...])
        sb = jax.nn.sigmoid(gb_ref[...])
        dpa_ref[...] = (dmv * sa).astype(BF16)
        dpb_ref[...] = (dmv * sb).astype(BF16)
        dg_ref[:, 0:d] = (dmv * pa_ref[...].astype(F32) * (sa * (1.0 - sa))).astype(BF16)
        dg_ref[:, d:2 * d] = (dmv * pb_ref[...].astype(F32) * (sb * (1.0 - sb))).astype(BF16)

    row = pl.BlockSpec((rb, d), lambda i: (i, 0))
    return _pcall(
        body,
        name="merge_bwd",
        out_shape=[jax.ShapeDtypeStruct((t, d), BF16), jax.ShapeDtypeStruct((t, d), BF16),
                   jax.ShapeDtypeStruct((t, 2 * d), BF16)],
        grid=(t // rb,),
        in_specs=[row, row, row, pl.BlockSpec((rb, d), lambda i: (i, gate_blk)), pl.BlockSpec((rb, d), lambda i: (i, gate_blk + 1))],
        out_specs=[row, row, pl.BlockSpec((rb, 2 * d), lambda i: (i, 0))],
        compiler_params=_cparams(("parallel",)),
    )(dm, pa, pb, qg, qg)


def _resid_norm_mod(x, branch, gain, mods, name):
    t, d = x.shape
    rb = min(ROW_BLOCK, t)

    def body(x_ref, b_ref, g_ref, mod_ref, x1_ref, z_ref):
        x1 = x_ref[...] + mod_ref[0:1, :] * b_ref[...]
        x1_ref[...] = x1
        xh, _ = _rms(x1)
        z_ref[...] = ((xh * g_ref[...]) * (1.0 + mod_ref[2:3, :]) + mod_ref[1:2, :]).astype(BF16)

    row = pl.BlockSpec((rb, d), lambda i: (i, 0))
    return _pcall(
        body,
        name=name,
        out_shape=[jax.ShapeDtypeStruct((t, d), F32), jax.ShapeDtypeStruct((t, d), BF16)],
        grid=(t // rb,),
        in_specs=[row, row, pl.BlockSpec((1, d), lambda i: (0, 0)), pl.BlockSpec((8, d), lambda i: (0, 0))],
        out_specs=[row, row],
        compiler_params=_cparams(("parallel",)),
    )(x, branch, gain, mods)


def _norm2_bwd(x1, attn, gain, mods, dz2, dx2):
    t, d = x1.shape
    rb = min(ROW_BLOCK, t)

    def body(x1_ref, at_ref, g_ref, mod_ref, dz_ref, dx2_ref, dx1_ref, da_ref, st_ref):
        @pl.when(pl.program_id(0) == 0)
        def _():
            st_ref[...] = jnp.zeros_like(st_ref)

        xh, r = _rms(x1_ref[...])
        g = g_ref[...]
        dz = dz_ref[...].astype(F32)
        dxn = dz * (1.0 + mod_ref[1:2, :])
        st_ref[0:1, :] += _colsum(dz)
        st_ref[1:2, :] += _colsum(dz * (xh * g))
        st_ref[2:3, :] += _colsum(dxn * xh)
        dx1 = dx2_ref[...] + _rms_bwd(dxn * g, xh, r)
        dx1_ref[...] = dx1
        st_ref[3:4, :] += _colsum(dx1 * at_ref[...])
        da_ref[...] = (dx1 * mod_ref[0:1, :]).astype(BF16)

    row = pl.BlockSpec((rb, d), lambda i: (i, 0))
    return _pcall(
        body,
        name="norm2_mod_bwd",
        out_shape=[jax.ShapeDtypeStruct((t, d), F32), jax.ShapeDtypeStruct((t, d), BF16), jax.ShapeDtypeStruct((8, d), F32)],
        grid=(t // rb,),
        in_specs=[row, row, pl.BlockSpec((1, d), lambda i: (0, 0)), pl.BlockSpec((8, d), lambda i: (0, 0)), row, row],
        out_specs=[row, row, pl.BlockSpec((8, d), lambda i: (0, 0))],
        compiler_params=_cparams(("arbitrary",)),
    )(x1, attn, gain, mods, dz2, dx2)


def _final_loss(x1, ffn, gain, mods, target):
    t, d = x1.shape
    rb = min(ROW_BLOCK, t)
    nb = t // rb

    def body(x1_ref, f_ref, g_ref, mod_ref, tg_ref, dx2_ref, df_ref, st_ref):
        i = pl.program_id(0)

        @pl.when(i == 0)
        def _():
            st_ref[...] = jnp.zeros_like(st_ref)

        ffn_v = f_ref[...]
        g2 = mod_ref[0:1, :]
        x2 = x1_ref[...] + g2 * ffn_v
        xh, r = _rms(x2)
        g = g_ref[...]
        err = xh * g - tg_ref[...]
        st_ref[2:3, :] += _colsum(err * err) * (0.5 / d)
        dy = err * (1.0 / d)
        st_ref[0:1, :] += _colsum(dy * xh)
        dx2 = _rms_bwd(dy * g, xh, r)
        dx2_ref[...] = dx2
        st_ref[1:2, :] += _colsum(dx2 * ffn_v)
        df_ref[...] = (dx2 * g2).astype(BF16)

        @pl.when(i == nb - 1)
        def _():
            st_ref[3:4, :] = jnp.broadcast_to(jnp.sum(st_ref[2:3, :], axis=-1, keepdims=True), (1, d))

    row = pl.BlockSpec((rb, d), lambda i: (i, 0))
    return _pcall(
        body,
        name="final_norm_loss",
        out_shape=[jax.ShapeDtypeStruct((t, d), F32), jax.ShapeDtypeStruct((t, d), BF16), jax.ShapeDtypeStruct((8, d), F32)],
        grid=(nb,),
        in_specs=[row, row, pl.BlockSpec((1, d), lambda i: (0, 0)), pl.BlockSpec((8, d), lambda i: (0, 0)), row],
        out_specs=[row, row, pl.BlockSpec((8, d), lambda i: (0, 0))],
        compiler_params=_cparams(("arbitrary",)),
    )(x1, ffn, gain, mods, target)


def _row_ends(shape):
    rows = lax.broadcasted_iota(jnp.int32, shape, 0)
    return rows == 0, rows == shape[0] - 1


def _shift_dn(v, first):
    return jnp.where(first, 0.0, pltpu.roll(v, 1, 0))


def _shift_up(v, last):
    return jnp.where(last, 0.0, pltpu.roll(v, v.shape[0] - 1, 0))


def _conv_fwd(u, cw, cb):
    t, f2 = u.shape
    f = f2 // 2
    cbk = _tile(f, 256)
    nf = f // cbk

    def body(ua_ref, ub_ref, cwa_ref, cwb_ref, cba_ref, cbb_ref, h_ref, uc_ref):
        first, last = _row_ends((t, cbk))
        outs = []
        for u_ref, cw_ref, cb_ref in ((ua_ref, cwa_ref, cba_ref), (ub_ref, cwb_ref, cbb_ref)):
            uu, cwv = u_ref[...].astype(F32), cw_ref[...]
            outs.append(cb_ref[...] + cwv[0:1, :] * _shift_dn(uu, first) + cwv[1:2, :] * uu
                        + cwv[2:3, :] * _shift_up(uu, last))
        a, b = outs
        uc_ref[0] = a.astype(BF16)
        uc_ref[1] = b.astype(BF16)
        h_ref[...] = (a * jax.nn.sigmoid(a) * b).astype(BF16)

    ca = lambda r: pl.BlockSpec((r, cbk), lambda j: (0, j))
    cbs = lambda r: pl.BlockSpec((r, cbk), lambda j: (0, nf + j))
    return _pcall(
        body,
        name="conv_gate_fwd",
        out_shape=[jax.ShapeDtypeStruct((t, f), BF16), jax.ShapeDtypeStruct((2, t, f), BF16)],
        grid=(nf,),
        in_specs=[ca(t), cbs(t), ca(3), cbs(3), ca(1), cbs(1)],
        out_specs=[ca(t), pl.BlockSpec((2, t, cbk), lambda j: (0, 0, j))],
        compiler_params=_cparams(("parallel",)),
    )(u, u, cw, cw, cb, cb)


def _conv_bwd(u, uc, cw, dh):
    t, f2 = u.shape
    f = f2 // 2
    cbk = _tile(f, 256)
    nf = f // cbk

    def body(ua_ref, ub_ref, uc_ref, cwa_ref, cwb_ref, dh_ref, du_ref, dcw_ref, dcb_ref):
        first, last = _row_ends((t, cbk))
        a, b = uc_ref[0].astype(F32), uc_ref[1].astype(F32)
        dh_v = dh_ref[...].astype(F32)
        sg = jax.nn.sigmoid(a)
        db = dh_v * (a * sg)
        da = dh_v * b * (sg * (1.0 + a * (1.0 - sg)))
        for idx, (dv, u_ref, cw_ref) in enumerate(((da, ua_ref, cwa_ref), (db, ub_ref, cwb_ref))):
            uu, cwv = u_ref[...].astype(F32), cw_ref[...]
            up, dn = _shift_up(dv, last), _shift_dn(dv, first)
            dcb_ref[idx] = _colsum(dv)
            dcw_ref[idx, 0:1, :] = _colsum(up * uu)
            dcw_ref[idx, 1:2, :] = _colsum(dv * uu)
            dcw_ref[idx, 2:3, :] = _colsum(dn * uu)
            du_ref[idx] = (cwv[0:1, :] * up + cwv[1:2, :] * dv + cwv[2:3, :] * dn).astype(BF16)

    ca = lambda r: pl.BlockSpec((r, cbk), lambda j: (0, j))
    cbs = lambda r: pl.BlockSpec((r, cbk), lambda j: (0, nf + j))
    o3 = lambda r: pl.BlockSpec((2, r, cbk), lambda j: (0, 0, j))
    return _pcall(
        body,
        name="conv_gate_bwd",
        out_shape=[jax.ShapeDtypeStruct((2, t, f), BF16), jax.ShapeDtypeStruct((2, 3, f), F32),
                   jax.ShapeDtypeStruct((2, 1, f), F32)],
        grid=(nf,),
        in_specs=[ca(t), cbs(t), o3(t), ca(3), cbs(3), ca(t)],
        out_specs=[o3(t), o3(3), o3(1)],
        compiler_params=_cparams(("parallel",)),
    )(u, u, uc, cw, cw, dh)


def _attention_fwd(q, kk, vv, *, hq, hkv, dk, dv, k_blk0, v_blk0, name):
    t = q.shape[0]
    tk = kk.shape[0]
    g_sz = hq // hkv
    tq = min(ATT_Q_BLOCK_FWD, t)

    def body(q_ref, k_ref, v_ref, o_ref, lse_ref):
        k = k_ref[...]
        v = v_ref[...]
        for j in range(g_sz):
            s = lax.dot_general(q_ref[:, j * dk:(j + 1) * dk], k, _DIMS["nt"], preferred_element_type=F32)
            m = jnp.max(s, axis=-1, keepdims=True)
            p = jnp.exp2(s - m)
            l = jnp.sum(p, axis=-1, keepdims=True)
            o = jnp.dot(p.astype(BF16), v, preferred_element_type=F32) / l
            o_ref[:, j * dv:(j + 1) * dv] = o.astype(BF16)
            lse_ref[0, :, j:j + 1] = m + jnp.log2(l)

    return _pcall(
        body,
        name=name,
        out_shape=[jax.ShapeDtypeStruct((t, hq * dv), BF16), jax.ShapeDtypeStruct((hkv, t, g_sz), F32)],
        grid=(hkv, t // tq),
        in_specs=[
            pl.BlockSpec((tq, g_sz * dk), lambda g, i: (i, g)),
            pl.BlockSpec((tk, dk), lambda g, i: (0, k_blk0 + g)),
            pl.BlockSpec((tk, dv), lambda g, i: (0, v_blk0 + g)),
        ],
        out_specs=[
            pl.BlockSpec((tq, g_sz * dv), lambda g, i: (i, g)),
            pl.BlockSpec((1, tq, g_sz), lambda g, i: (g, i, 0)),
        ],
        compiler_params=_cparams(("parallel", "parallel")),
    )(q, kk, vv)


def _attention_bwd(q, kk, vv, do, lse, *, hq, hkv, dk, dv, k_blk0, v_blk0, name):
    t = q.shape[0]
    tk = kk.shape[0]
    g_sz = hq // hkv
    tq = min(ATT_Q_BLOCK, t)

    def body(q_ref, k_ref, v_ref, do_ref, lse_ref, dq_ref, dk_ref, dv_ref):
        @pl.when(pl.program_id(1) == 0)
        def _():
            dk_ref[...] = jnp.zeros_like(dk_ref)
            dv_ref[...] = jnp.zeros_like(dv_ref)

        k = k_ref[...]
        v = v_ref[...]
        for j in range(g_sz):
            qj = q_ref[:, j * dk:(j + 1) * dk]
            doj = do_ref[:, j * dv:(j + 1) * dv]
            s = lax.dot_general(qj, k, _DIMS["nt"], preferred_element_type=F32)
            p = jnp.exp2(s - lse_ref[0, :, j:j + 1])
            dp = lax.dot_general(doj, v, _DIMS["nt"], preferred_element_type=F32)
            ds = (p * (dp - jnp.sum(p * dp, axis=-1, keepdims=True))).astype(BF16)
            dv_ref[...] += lax.dot_general(p.astype(BF16), doj, _DIMS["tn"], preferred_element_type=F32)
            dk_ref[...] += lax.dot_general(ds, qj, _DIMS["tn"], preferred_element_type=F32)
            dq_ref[:, j * dk:(j + 1) * dk] = jnp.dot(ds, k, preferred_element_type=F32)

        @pl.when(pl.program_id(1) == t // tq - 1)
        def _():
            dk_ref[...] *= LN2

    return _pcall(
        body,
        name=name,
        out_shape=[jax.ShapeDtypeStruct((t, hq * dk), F32), jax.ShapeDtypeStruct((tk, hkv * dk), F32),
                   jax.ShapeDtypeStruct((tk, hkv * dv), F32)],
        grid=(hkv, t // tq),
        in_specs=[
            pl.BlockSpec((tq, g_sz * dk), lambda g, i: (i, g)),
            pl.BlockSpec((tk, dk), lambda g, i: (0, k_blk0 + g)),
            pl.BlockSpec((tk, dv), lambda g, i: (0, v_blk0 + g)),
            pl.BlockSpec((tq, g_sz * dv), lambda g, i: (i, g)),
            pl.BlockSpec((1, tq, g_sz), lambda g, i: (g, i, 0)),
        ],
        out_specs=[
            pl.BlockSpec((tq, g_sz * dk), lambda g, i: (i, g)),
            pl.BlockSpec((tk, dk), lambda g, i: (0, g)),
            pl.BlockSpec((tk, dv), lambda g, i: (0, g)),
        ],
        compiler_params=_cparams(("parallel", "arbitrary")),
    )(q, kk, vv, do, lse)


def _silu(v):
    return v * jax.nn.sigmoid(v)


def _ada_fwd(conds, w_ada, b_ada_shard):
    r, d = conds.shape
    n = w_ada.shape[1]
    tn = _tile(n, 512)

    def body(c_ref, w_ref, b_ref, o_ref):
        s = _silu(c_ref[...]).astype(BF16)
        o_ref[...] = jnp.dot(s, w_ref[...].astype(BF16), preferred_element_type=F32) + b_ref[...]

    return _pcall(
        body,
        name="ada_fwd",
        out_shape=jax.ShapeDtypeStruct((r, n), F32),
        grid=(n // tn,),
        in_specs=[pl.BlockSpec((r, d), lambda j: (0, 0)), pl.BlockSpec((d, tn), lambda j: (0, j)),
                  pl.BlockSpec((1, tn), lambda j: (0, j))],
        out_specs=pl.BlockSpec((r, tn), lambda j: (0, j)),
        compiler_params=_cparams(("parallel",)),
    )(conds, w_ada, b_ada_shard)


def _cctx_partial(da16_shard, w_ada, c_ctx_row):
    d, n = w_ada.shape
    td = _tile(d, 512)

    def body(g_ref, w_ref, c_ref, o_ref):
        ds = lax.dot_general(g_ref[8:16, :].astype(BF16), w_ref[...].astype(BF16), _DIMS["nt"],
                             preferred_element_type=F32)
        cv = c_ref[...]
        sg = jax.nn.sigmoid(cv)
        o_ref[...] = ds * (sg * (1.0 + cv * (1.0 - sg)))

    return _pcall(
        body,
        name="cctx_partial",
        out_shape=jax.ShapeDtypeStruct((8, d), F32),
        grid=(d // td,),
        in_specs=[pl.BlockSpec((16, n), lambda j: (0, 0)), pl.BlockSpec((td, n), lambda j: (j, 0)),
                  pl.BlockSpec((1, td), lambda j: (0, j))],
        out_specs=pl.BlockSpec((8, td), lambda j: (0, j)),
        compiler_params=_cparams(("parallel",)),
    )(da16_shard, w_ada, c_ctx_row)


def _sum_parts(parts):
    p, _, n = parts.shape

    def body(p_ref, o_ref):
        acc = p_ref[0]
        for s in range(1, p):
            acc = acc + p_ref[s]
        o_ref[...] = acc

    return _pcall(
        body,
        name="sum_parts",
        out_shape=jax.ShapeDtypeStruct((1, n), F32),
        in_specs=[pl.BlockSpec(memory_space=pltpu.VMEM)],
        out_specs=pl.BlockSpec(memory_space=pltpu.VMEM),
    )(parts)


def _adam_math(w, g, m, v):
    m2 = ADAM_B1 * m + (1.0 - ADAM_B1) * g
    v2 = ADAM_B2 * v + (1.0 - ADAM_B2) * jnp.square(g)
    m_hat = m2 / (1.0 - ADAM_B1 ** ADAM_STEP)
    v_hat = v2 / (1.0 - ADAM_B2 ** ADAM_STEP)
    delta = -ADAM_LR * (m_hat / (jnp.sqrt(v_hat) + ADAM_EPS) + ADAM_WD * w)
    return delta, m2, v2


def _adamw(parts, w, m, v, name):
    p, r, c = parts.shape
    rb = _tile(r, max(8, (1 << 20) // (4 * c) // 8 * 8), 8)

    def body(p_ref, w_ref, m_ref, v_ref, g_ref, d_ref, m2_ref, v2_ref):
        g = p_ref[0].astype(F32)
        for s in range(1, p):
            g = g + p_ref[s].astype(F32)
        g_ref[...] = g
        d_ref[...], m2_ref[...], v2_ref[...] = _adam_math(w_ref[...], g, m_ref[...], v_ref[...])

    if w.ndim == 3:
        row = pl.BlockSpec((None, rb, c), lambda i: (0, i, 0))
    else:
        row = pl.BlockSpec((rb, c), lambda i: (i, 0))
    return _pcall(
        body,
        name=name,
        out_shape=[jax.ShapeDtypeStruct(w.shape, F32)] * 4,
        grid=(r // rb,),
        in_specs=[pl.BlockSpec((p, rb, c), lambda i: (0, i, 0)), row, row, row],
        out_specs=[row] * 4,
        compiler_params=_cparams(("parallel",)),
    )(parts, w, m, v)


def _adamw_ada(conds, da16, w, m, v):
    d, n = w.shape
    rb = _tile(d, 256, LANE)

    def body(s_ref, da_ref, w_ref, m_ref, v_ref, g_ref, d_ref, m2_ref, v2_ref):
        g = lax.dot_general(_silu(s_ref[...]).astype(BF16), da_ref[...].astype(BF16), _DIMS["tn"],
                            preferred_element_type=F32)
        g_ref[...] = g
        d_ref[...], m2_ref[...], v2_ref[...] = _adam_math(w_ref[...], g, m_ref[...], v_ref[...])

    row = pl.BlockSpec((rb, n), lambda i: (i, 0))
    return _pcall(
        body,
        name="adamw_w_ada",
        out_shape=[jax.ShapeDtypeStruct((d, n), F32)] * 4,
        grid=(d // rb,),
        in_specs=[pl.BlockSpec((16, rb), lambda i: (0, i)), pl.BlockSpec((16, n), lambda i: (0, 0)), row, row, row],
        out_specs=[row] * 4,
        compiler_params=_cparams(("parallel",)),
    )(conds, da16, w, m, v)


def _touch(arrays, name):
    def body(*refs):
        refs[-1][...] = jnp.zeros((8, LANE), F32)

    return _pcall(body, name=name, out_shape=jax.ShapeDtypeStruct((8, LANE), F32),
                  in_specs=[pl.BlockSpec(memory_space=pl.ANY)] * len(arrays),
                  out_specs=pl.BlockSpec(memory_space=pltpu.VMEM))(*arrays)


def _cast_bf16(a, name):
    _, r, c = a.shape
    rb = _tile(r, 512, 8)

    def body(a_ref, o_ref):
        o_ref[...] = a_ref[...].astype(BF16)

    return _pcall(body, name=name, out_shape=jax.ShapeDtypeStruct((r, c), BF16), grid=(r // rb,),
                  in_specs=[pl.BlockSpec((None, rb, c), lambda i: (0, i, 0))],
                  out_specs=pl.BlockSpec((rb, c), lambda i: (i, 0)), compiler_params=_cparams(("parallel",)))(a)


def _rope_tabs(t, rot):
    half, q = rot // 2, rot // 4
    n_rows = t // GRID_W
    row = jnp.repeat(jnp.arange(n_rows, dtype=F32), GRID_W)
    col = jnp.tile(jnp.arange(GRID_W, dtype=F32), n_rows)
    inv_freq = ROPE_THETA ** (-jnp.arange(0, half, 2, dtype=F32) / half)
    ang = jnp.concatenate([row[:, None] * inv_freq, col[:, None] * inv_freq], axis=-1)
    cos, sin = jnp.cos(ang), jnp.sin(ang)
    c0, c1, s0, s1 = cos[:, :q], cos[:, q:], sin[:, :q], sin[:, q:]
    z = jnp.zeros_like(s0)
    return (jnp.concatenate([c0, c0, c1, c1], -1), jnp.concatenate([-s0, z, -s1, z], -1),
            jnp.concatenate([z, s0, z, s1], -1))


def _pad_cols(a, left, total, fill=0.0):
    return jnp.pad(a, ((0, 0), (left, total - left - a.shape[1])), constant_values=fill)


def _with_ctx_rows(tab, tc, fill):
    return jnp.concatenate([tab, jnp.full((tc, tab.shape[1]), fill, F32)], axis=0)


def kernel(x, c, ctx, c_ctx, w_ada, b_ada, norm1_g, w_in, mla_q_norm_g, w_q_up, mla_kv_norm_g, w_kv_up, gqa_q_norm_g, gqa_k_norm_g, w_br_a, w_br_b, w_out, norm2_g, w_up, conv_w, conv_b, w_down, final_norm_g, loss_target, m_c_ctx, m_w_ada, m_b_ada, m_norm1_g, m_w_in, m_mla_q_norm_g, m_w_q_up, m_mla_kv_norm_g, m_w_kv_up, m_gqa_q_norm_g, m_gqa_k_norm_g, m_w_br_a, m_w_br_b, m_w_out, m_norm2_g, m_w_up, m_conv_w, m_conv_b, m_w_down, m_final_norm_g, v_c_ctx, v_w_ada, v_b_ada, v_norm1_g, v_w_in, v_mla_q_norm_g, v_w_q_up, v_mla_kv_norm_g, v_w_kv_up, v_gqa_q_norm_g, v_gqa_k_norm_g, v_w_br_a, v_w_br_b, v_w_out, v_norm2_g, v_w_up, v_conv_w, v_conv_b, v_w_down, v_final_norm_g):
    weights = dict(c_ctx=c_ctx, w_ada=w_ada, b_ada=b_ada, norm1_g=norm1_g, w_in=w_in, mla_q_norm_g=mla_q_norm_g,
                   w_q_up=w_q_up, mla_kv_norm_g=mla_kv_norm_g, w_kv_up=w_kv_up, gqa_q_norm_g=gqa_q_norm_g,
                   gqa_k_norm_g=gqa_k_norm_g, w_br_a=w_br_a, w_br_b=w_br_b, w_out=w_out, norm2_g=norm2_g, w_up=w_up,
                   conv_w=conv_w, conv_b=conv_b, w_down=w_down, final_norm_g=final_norm_g)
    mom_m = dict(c_ctx=m_c_ctx, w_ada=m_w_ada, b_ada=m_b_ada, norm1_g=m_norm1_g, w_in=m_w_in, mla_q_norm_g=m_mla_q_norm_g,
                 w_q_up=m_w_q_up, mla_kv_norm_g=m_mla_kv_norm_g, w_kv_up=m_w_kv_up, gqa_q_norm_g=m_gqa_q_norm_g,
                 gqa_k_norm_g=m_gqa_k_norm_g, w_br_a=m_w_br_a, w_br_b=m_w_br_b, w_out=m_w_out, norm2_g=m_norm2_g,
                 w_up=m_w_up, conv_w=m_conv_w, conv_b=m_conv_b, w_down=m_w_down, final_norm_g=m_final_norm_g)
    mom_v = dict(c_ctx=v_c_ctx, w_ada=v_w_ada, b_ada=v_b_ada, norm1_g=v_norm1_g, w_in=v_w_in, mla_q_norm_g=v_mla_q_norm_g,
                 w_q_up=v_w_q_up, mla_kv_norm_g=v_mla_kv_norm_g, w_kv_up=v_w_kv_up, gqa_q_norm_g=v_gqa_q_norm_g,
                 gqa_k_norm_g=v_gqa_k_norm_g, w_br_a=v_w_br_a, w_br_b=v_w_br_b, w_out=v_w_out, norm2_g=v_norm2_g,
                 w_up=v_w_up, conv_w=v_conv_w, conv_b=v_conv_b, w_down=v_w_down, final_norm_g=v_final_norm_g)
    order = list(weights)

    my_idx = 4 * lax.axis_index("x") + 2 * lax.axis_index("y") + lax.axis_index("c")
    xs, cts, tgt = x[0], ctx[0], loss_target[0]
    t, d = xs.shape
    tc = cts.shape[0]
    ta = t + tc
    kvl, ql = MLA_KV_LORA, MLA_Q_LORA
    nb = GQA_KV_HEADS * GQA_HEAD_DIM
    hb = GQA_HEADS * GQA_HEAD_DIM
    ha = MLA_HEADS
    f2 = w_up.shape[2] * N_DEV
    ff = f2 // 2

    big = ["w_in", "w_q_up", "w_kv_up", "w_br_a", "w_br_b", "w_out", "w_up", "w_down"]
    nw = len(big)
    del nw
    _ORDER_AFTER.clear()
    shards = {"w_in": _cast_bf16(weights["w_in"], "cast_w_in")}
    c_idx = jnp.reshape(lax.axis_index("c"), (1,)).astype(jnp.int32)

    def gather_start(names, dep):
        shs = [shards[n] for n in names]
        land = [lax.empty((N_DEV,) + s.shape, BF16) for s in shs]
        if dep is not None:
            _after(dep)
        s, r, arrs, tok = _split_start("gather_ici_start_" + names[0], shs + land, _gather_ici_copies(len(names)),
                                       4 * len(names))
        return dict(names=names, s=s, r=r, arrs=arrs, tok=tok)

    def gather_pass(g, after):
        n = len(g["names"])
        arrs = _split_wait("gather_ici_wait_" + g["names"][0], g["s"], g["r"], g["arrs"], _gather_ici_copies(n), after)
        s, r, bufs, tok = _split_start("gather_pass_start_" + g["names"][0], arrs[n:], _gather_pass_copies(n), 3 * n)
        g.update(s2=s, r2=r, bufs=bufs)
        return tok

    def gather_relay(g, after):
        n = len(g["names"])
        bufs = _split_wait("gather_pass_wait_" + g["names"][0], g["s2"], g["r2"], g["bufs"], _gather_pass_copies(n), after)
        s, r, bufs, tok = _split_start("gather_d2d_start_" + g["names"][0], bufs, _gather_d2d_copies(n), n)
        g.update(s3=s, r3=r, bufs=bufs)
        return tok

    def gather_finish(g, after):
        n = len(g["names"])
        bufs = _split_wait("gather_d2d_wait_" + g["names"][0], g["s3"], g["r3"], g["bufs"], _gather_d2d_copies(n), after)
        return dict(zip(g["names"], bufs))

    c_all, cw_all = _all_gather([jnp.pad(c, ((0, 7), (0, 0))), jnp.pad(conv_w[0], ((0, 5), (0, 0)))], "gather_cond")
    conv_w_f = jnp.transpose(cw_all[:, :3, :], (1, 0, 2)).reshape(3, f2)
    conds = jnp.concatenate([c_all[:, 0, :], c_ctx[None, :], jnp.zeros((7, d), F32)], axis=0)
    ncol = w_ada.shape[2]
    b_shard = lax.dynamic_slice_in_dim(b_ada, my_idx * ncol, ncol, axis=1)
    ada_shard = _ada_fwd(conds, w_ada[0], b_shard)
    (ada_all,) = _all_gather([ada_shard], "gather_ada")
    ada = jnp.transpose(ada_all, (1, 0, 2)).reshape(16, N_DEV * ncol)
    lat = lax.dynamic_slice_in_dim(ada, my_idx, 1, axis=0).reshape(6, d)
    cxt = ada[8].reshape(6, d)
    zero2 = jnp.zeros((2, d), F32)
    mods1 = jnp.concatenate([lat[0:2], cxt[0:2], jnp.zeros((4, d), F32)], axis=0)
    mods2 = jnp.concatenate([lat[2:3], lat[3:4], lat[4:5], jnp.zeros((5, d), F32)], axis=0)
    mods2b = jnp.concatenate([lat[2:3], lat[4:5], jnp.zeros((6, d), F32)], axis=0)
    mods3 = jnp.concatenate([lat[5:6], jnp.zeros((7, d), F32)], axis=0)
    del zero2

    g0 = gather_start(["w_in"], ada_all)
    for n in big[1:]:
        _after(g0["tok"])
        shards[n] = _cast_bf16(weights[n], "cast_" + n)
    g1 = gather_start(["w_q_up", "w_kv_up", "w_br_a", "w_br_b", "w_out"], g0["tok"])
    g2 = gather_start(["w_up"], g1["tok"])
    g3 = gather_start(["w_down"], g2["tok"])

    ca, s1a, s2a = _rope_tabs(t, MLA_ROPE)
    cb_, s1b, s2b = _rope_tabs(t, GQA_HEAD_DIM)
    q_tabs_a = (_pad_cols(jnp.concatenate([jnp.ones((t, MLA_NOPE), F32), ca], 1), 0, MLA_SLOT),
                _pad_cols(s1a, MLA_NOPE, MLA_SLOT), _pad_cols(s2a, MLA_NOPE, MLA_SLOT))
    q_tabs_b = (cb_, s1b, s2b)
    k_tabs = (_with_ctx_rows(_pad_cols(ca, 0, LANE), tc, 1.0), _with_ctx_rows(_pad_cols(s1a, 0, LANE), tc, 0.0),
              _with_ctx_rows(_pad_cols(s2a, 0, LANE), tc, 0.0),
              _with_ctx_rows(cb_, tc, 1.0), _with_ctx_rows(s1b, tc, 0.0), _with_ctx_rows(s2b, tc, 0.0))

    def cols_full(g):
        return jnp.transpose(g, (1, 0, 2)).reshape(g.shape[1], N_DEV * g.shape[2])

    _after(g3["tok"])
    early = _touch([mom_m["w_in"], mom_v["w_in"], mom_m["w_q_up"], mom_v["w_q_up"]], "touch_moments")
    _after(early, *q_tabs_a, *q_tabs_b, *k_tabs)
    _after(gather_pass(g0, mods1))
    z_all = _norm_mod_fwd(cts, xs, norm1_g, mods1)
    gathered = gather_finish(g0, gather_relay(g0, z_all))
    w_in_f = cols_full(gathered["w_in"])
    o_kpe, o_kb, o_vb = kvl, kvl + MLA_ROPE, kvl + MLA_ROPE + nb
    o_q = o_vb + nb
    o_g = o_q + ql + hb
    wkv_w = kvl + 2 * nb + LANE
    w_kv_p = jnp.concatenate([w_in_f[:, :kvl], w_in_f[:, o_kb:o_q], w_in_f[:, o_kpe:o_kb],
                              jnp.zeros((d, LANE - MLA_ROPE), BF16)], axis=1)
    q_w = ql + hb
    q_pad = (-q_w) % 512 if d >= 512 else (-q_w) % d
    gate_blk = (q_w + q_pad) // d
    assert (q_w + q_pad) % d == 0
    w_qg_p = jnp.concatenate([w_in_f[:, o_q:o_g], jnp.zeros((d, q_pad), BF16), w_in_f[:, o_g:]], axis=1)

    kv_all = _mm(z_all, w_kv_p, "nn", F32, "proj_kv", tm=1152, tn=wkv_w)
    qg = _mm(z_all, w_qg_p, "nn", F32, "proj_qg", tm=1024, tn=1024, rows=t)
    _after(gather_pass(g1, qg))
    kin, k_b, v_b = _key_prep_fwd(kv_all, mla_kv_norm_g, gqa_k_norm_g, k_tabs)
    sc_a = float((MLA_NOPE + MLA_ROPE) ** -0.5) * LOG2E
    sc_b = float(GQA_HEAD_DIM ** -0.5) * LOG2E
    cqn, q_b = _q_prep_fwd(qg, mla_q_norm_g, gqa_q_norm_g, q_tabs_b, sc_b)
    tok_g1 = gather_relay(g1, q_b)
    tok_g2 = gather_pass(g2, tok_g1)
    gathered.update(gather_finish(g1, tok_g2))

    wq_f = cols_full(gathered["w_q_up"]).reshape(ql, ha, MLA_NOPE + MLA_ROPE)
    wq_ext = jnp.pad(wq_f, ((0, 0), (0, 0), (0, MLA_SLOT - MLA_NOPE - MLA_ROPE))).reshape(ql, ha * MLA_SLOT)
    wkv_f = cols_full(gathered["w_kv_up"]).reshape(kvl, ha, MLA_NOPE + MLA_V)
    wk_slots = jnp.pad(wkv_f[:, :, :MLA_NOPE], ((0, 0), (0, 0), (0, MLA_SLOT - MLA_NOPE))).reshape(kvl, ha * MLA_SLOT)
    wv_cols = wkv_f[:, :, MLA_NOPE:].reshape(kvl, ha * MLA_V)
    e_slot = jnp.pad(jnp.eye(MLA_ROPE, dtype=BF16),
                     ((0, LANE - MLA_ROPE), (MLA_NOPE, MLA_SLOT - MLA_NOPE - MLA_ROPE)))
    e_rows = jnp.concatenate([jnp.tile(e_slot, (1, ha)), jnp.zeros((LANE, ha * MLA_V), BF16)], axis=1)
    wkv_ext = jnp.concatenate([jnp.concatenate([wk_slots, wv_cols], axis=1), e_rows], axis=0)
    w_bra = cols_full(gathered["w_br_a"])
    w_brb = cols_full(gathered["w_br_b"])
    w_out_f = gathered["w_out"].reshape(d, d)

    kv_a = _mm(kin, wkv_ext, "nn", BF16, "kv_up", tm=1152, tn=1024)
    qa_raw = _mm(cqn, wq_ext, "nn", F32, "q_up", tm=1024, tn=1024)
    q_a = _rope_a(qa_raw, q_tabs_a, False, BF16, "rope_q_fwd", sc_a)
    att_a = dict(hq=ha, hkv=ha, dk=MLA_SLOT, dv=MLA_V, k_blk0=0, v_blk0=ha * MLA_SLOT // MLA_V)
    att_b = dict(hq=GQA_HEADS, hkv=GQA_KV_HEADS, dk=GQA_HEAD_DIM, dv=GQA_HEAD_DIM, k_blk0=0, v_blk0=0)
    o_a, lse_a = _attention_fwd(q_a, kv_a, kv_a, name="attn_a_fwd", **att_a)
    o_b, lse_b = _attention_fwd(q_b, k_b, v_b, name="attn_b_fwd", **att_b)
    _after(gather_relay(g2, o_b), gather_pass(g3, o_b))
    pa = _mm(o_a, w_bra, "nn", BF16, "br_a", tm=1024, tn=1024)
    pb = _mm(o_b, w_brb, "nn", BF16, "br_b", tm=1024, tn=1024)
    merged = _merge_fwd(pa, pb, qg, gate_blk)
    attn = _mm(merged, w_out_f, "nn", F32, "w_out", tm=1024, tn=1024)
    x1, z2 = _resid_norm_mod(xs, attn, norm2_g, mods2, "resid_norm2_fwd")
    w_up3 = gather_finish(g2, z2)["w_up"]
    _after(gather_relay(g3, z2))
    u = _mm_up_fwd(z2, w_up3, "w_up")
    w_down_f = gather_finish(g3, u)["w_down"].reshape(ff, d)
    h, uc = _conv_fwd(u, conv_w_f, conv_b)
    ffn = _mm(h, w_down_f, "nn", F32, "w_down", tm=1024, tn=1024, tk=2816)

    def to_shards(g):
        return jnp.transpose(g.reshape(g.shape[0], N_DEV, g.shape[1] // N_DEV), (1, 0, 2))

    def reduce_start(tag, names, sends):
        n = len(sends)
        land = [lax.empty((4,) + s.shape[1:], s.dtype) for s in sends]
        s, r, arrs, tok = _split_start("reduce_d2d_start_" + tag, sends + land, _reduce_d2d_copies(n), 4 * n)
        return dict(tag=tag, names=names, s=s, r=r, arrs=arrs, tok=tok)

    def reduce_relay(g, after):
        n = len(g["names"])
        arrs = _split_wait("reduce_d2d_wait_" + g["tag"], g["s"], g["r"], g["arrs"], _reduce_d2d_copies(n), after)
        sums = [_pair_sum(arrs[a], arrs[n + a], c_idx, "pair_sum_" + g["names"][a]) for a in range(n)]
        land = [lax.empty(s.shape, s.dtype) for s in sums]
        s, r, arrs2, tok = _split_start("reduce_ici_start_" + g["tag"], sums + land, _reduce_ici_copies(n), 4 * n)
        g.update(s2=s, r2=r, arrs2=arrs2)
        return tok

    def reduce_finish(g, after):
        n = len(g["names"])
        arrs2 = _split_wait("reduce_ici_wait_" + g["tag"], g["s2"], g["r2"], g["arrs2"], _reduce_ici_copies(n), after)
        return dict(zip(g["names"], arrs2[n:]))

    dx2, dffn, st_fin = _final_loss(x1, ffn, final_norm_g[None, :], mods3, tgt)
    loss = lax.psum(st_fin[3, 0], MESH_AXES)
    dh = _mm(dffn, w_down_f, "nt", BF16, "d_h", tm=1024, tn=1024)
    g_w_down = _mm(h, dffn, "tn", BF16, "g_w_down", tm=512, tn=1024)
    r_down = reduce_start("down", ["w_down"], [g_w_down.reshape(N_DEV, ff // N_DEV, d)])
    _after(r_down["tok"])
    du3, dcw, dcb = _conv_bwd(u, uc, conv_w_f, dh)
    dz2 = _mm_up_dz(du3, w_up3, "d_z2")
    g_w_up = _mm_up_gw(z2, du3, N_DEV, "g_w_up")
    g_conv_w = jnp.concatenate([dcw[0], dcw[1]], axis=1)
    tok = reduce_relay(r_down, g_w_up)
    _after(tok)
    r_up = reduce_start("up", ["w_up", "conv_w"], [g_w_up, to_shards(jnp.pad(g_conv_w, ((0, 5), (0, 0))))])
    _after(tok, r_up["tok"])
    dx1, dattn, st_n2 = _norm2_bwd(x1, attn, norm2_g, mods2b, dz2, dx2)
    dmerged = _mm(dattn, w_out_f, "nt", BF16, "d_merged", tm=1024, tn=1024)
    g_w_out = _mm(merged, dattn, "tn", BF16, "g_w_out", tm=1024, tn=1024)
    dpa, dpb, dgates = _merge_bwd(dmerged, pa, pb, qg, gate_blk)
    do_a = _mm(dpa, w_bra, "nt", BF16, "d_o_a", tm=1024, tn=1024)
    do_b = _mm(dpb, w_brb, "nt", BF16, "d_o_b", tm=1024, tn=1024)
    g_w_bra = _mm(o_a, dpa, "tn", BF16, "g_w_br_a", tm=1024, tn=1024)
    g_w_brb = _mm(o_b, dpb, "tn", BF16, "g_w_br_b", tm=1024, tn=1024)
    tok = reduce_relay(r_up, g_w_brb)
    _after(tok)
    r_out = reduce_start("out", ["w_out", "w_br_a", "w_br_b"],
                         [g_w_out.reshape(N_DEV, d // N_DEV, d), to_shards(g_w_bra), to_shards(g_w_brb)])
    _after(tok, r_out["tok"])
    dq_a, dk_a, dv_a = _attention_bwd(q_a, kv_a, kv_a, do_a, lse_a, name="attn_a_bwd", **att_a)
    dq_b, dk_b, dv_b = _attention_bwd(q_b, k_b, v_b, do_b, lse_b, name="attn_b_bwd", **att_b)
    _after(reduce_relay(r_out, dv_b))
    dqa_raw = _rope_a(dq_a, q_tabs_a, True, BF16, "rope_q_bwd", sc_a * LN2)
    dcqn = _mm(dqa_raw, wq_ext, "nt", F32, "d_cqn", tm=1024, tn=ql)
    g_wq_ext = _mm(cqn, dqa_raw, "tn", BF16, "g_w_q_up", tm=ql, tn=1024)
    dq_p, st_q, st_qb = _q_prep_bwd(qg, mla_q_norm_g, gqa_q_norm_g, q_tabs_b, dcqn, dq_b, q_pad, sc_b * LN2)
    dkin = _mm_cat_nt([(dk_a, wkv_ext, 0), (dv_a, wkv_ext, ha * MLA_SLOT)], F32, "d_kin", tm=1152, tn=kvl + LANE)
    g_wkv_ext = _mm_cat_tn(kin, [dk_a, dv_a], BF16, "g_w_kv_up", tm=kvl + LANE, tn=min(1024, ha * MLA_V))
    dkv_p, st_kv, st_kb = _key_prep_bwd(kv_all, mla_kv_norm_g, gqa_k_norm_g, k_tabs, dkin, dk_b, dv_b)
    g_wq = g_wq_ext.reshape(ql, ha, MLA_SLOT)[:, :, :MLA_NOPE + MLA_ROPE].reshape(ql, ha * (MLA_NOPE + MLA_ROPE))
    g_wkv = jnp.concatenate([g_wkv_ext[:kvl, :ha * MLA_SLOT].reshape(kvl, ha, MLA_SLOT)[:, :, :MLA_NOPE],
                             g_wkv_ext[:kvl, ha * MLA_SLOT:].reshape(kvl, ha, MLA_V)], axis=2).reshape(kvl, ha * (MLA_NOPE + MLA_V))
    r_qkv = reduce_start("qkv", ["w_q_up", "w_kv_up"], [to_shards(g_wq), to_shards(g_wkv)])
    _after(r_qkv["tok"])
    g_wkv_p = _mm(z_all, dkv_p, "tn", BF16, "g_w_in_kv", tm=1024, tn=wkv_w)
    g_wqg_p = _mm_cat_tn(z_all, [dq_p, dgates], BF16, "g_w_in_qg", tm=1024, tn=min(1024, d), rows=t)
    g_w_in = jnp.concatenate([g_wkv_p[:, :kvl], g_wkv_p[:, kvl + 2 * nb:kvl + 2 * nb + MLA_ROPE],
                              g_wkv_p[:, kvl:kvl + 2 * nb], g_wqg_p[:, :q_w], g_wqg_p[:, q_w + q_pad:]], axis=1)
    r_in = reduce_start("in", ["w_in"], [to_shards(g_w_in)])
    _after(r_in["tok"])
    qw_p = q_w + q_pad
    dz_lat = _mm_sum_nt([(dq_p, 0, w_qg_p, 0, qw_p), (dgates, 0, w_qg_p, qw_p, d), (dgates, d, w_qg_p, qw_p + d, d),
                         (dkv_p, 0, w_kv_p, 0, wkv_w)], F32, "d_z_lat", rows=t)
    dz_ctx = _mm(dkv_p, w_kv_p, "nt", F32, "d_z_ctx", tm=min(ROW_BLOCK, tc), tn=1024, a_row_off=t)
    tok_q = reduce_relay(r_qkv, dz_ctx)
    _after(tok_q)
    grad_x, st_n1 = _norm1_bwd(cts, xs, norm1_g, mods1, dz_ctx, dz_lat, dx1)

    res = {}

    def upd(nm, parts):
        wv, mv, vv = weights[nm], mom_m[nm], mom_v[nm]
        if wv.ndim == 1:
            wv, mv, vv = (a.reshape(1, -1) for a in (wv, mv, vv))
        outs = _adamw(parts, wv, mv, vv, "adamw_" + nm)
        res[nm] = [o_.reshape(weights[nm].shape) for o_ in outs]

    d_lat = jnp.concatenate([st_n1[0], st_n1[1], st_n2[3], st_n2[0], st_n2[1], st_fin[1]])
    d_cxt = jnp.concatenate([st_n1[3], st_n1[4], jnp.zeros((4 * d,), F32)])
    small = jnp.concatenate([d_lat, d_cxt, st_n1[2], st_q[0], st_kv[0], st_qb[0], st_kb[0], st_n2[2],
                             jnp.concatenate([dcb[0, 0], dcb[1, 0]]), st_fin[0]])
    n_small = small.shape[0]
    pad_small = (-n_small) % LANE
    (small_all,) = _all_gather([jnp.pad(small, (0, pad_small)).reshape(1, -1)], "gather_small")
    offs = {}
    o = 0
    for nm, ln in (("d_lat", 6 * d), ("d_cxt", 6 * d), ("norm1_g", d), ("mla_q_norm_g", ql), ("mla_kv_norm_g", kvl),
                   ("gqa_q_norm_g", GQA_HEAD_DIM), ("gqa_k_norm_g", GQA_HEAD_DIM), ("norm2_g", d), ("conv_b", f2),
                   ("final_norm_g", d)):
        offs[nm] = (o, ln)
        o += ln

    def part(nm):
        a, ln = offs[nm]
        return small_all[:, :, a:a + ln]

    d_lat_all = part("d_lat")[:, 0, :]
    d_cxt_sum = _sum_parts(part("d_cxt"))
    da16 = jnp.concatenate([d_lat_all, d_cxt_sum, jnp.zeros((7, 6 * d), F32)], axis=0)
    da16_shard = lax.dynamic_slice_in_dim(da16, my_idx * ncol, ncol, axis=1)
    cc_part = _cctx_partial(da16_shard, w_ada[0], c_ctx[None, :])
    (cc_all,) = _all_gather([cc_part], "gather_cctx")
    cc_parts = cc_all[:, 0:1, :]
    tok_i = reduce_relay(r_in, cc_all)

    _after(tok_i)
    for nm in ("norm1_g", "mla_q_norm_g", "mla_kv_norm_g", "gqa_q_norm_g", "gqa_k_norm_g", "norm2_g", "conv_b",
               "final_norm_g"):
        upd(nm, part(nm))
    upd("c_ctx", cc_parts)
    b_parts = jnp.concatenate([d_lat_all[:, None, :], d_cxt_sum[None]], axis=0)
    upd("b_ada", b_parts)
    _after(tok_i)
    outs = _adamw_ada(conds, da16_shard, w_ada[0], m_w_ada[0], v_w_ada[0])
    res["w_ada"] = [o_[None] for o_ in outs]
    last = outs[0]
    done = [last]
    for grp in (r_down, r_up, r_out, r_qkv, r_in):
        _after(*done)
        recv = reduce_finish(grp, last)
        for nm in grp["names"]:
            upd(nm, recv[nm][:, :3, :] if nm == "conv_w" else recv[nm])
            last = res[nm][0]
            done.append(last)

    return (loss, grad_x[None], *[res[n][0] for n in order], *[res[n][1] for n in order],
            *[res[n][2] for n in order], *[res[n][3] for n in order])
```

```python
import functools

import jax
import jax.numpy as jnp
from jax import lax
from jax.experimental import pallas as pl
from jax.experimental.pallas import tpu as pltpu

F32 = jnp.float32
BF16 = jnp.bfloat16

GRID_W = 64
ROPE_THETA = 10000.0
NORM_EPS = 1e-6
MLA_HEADS = 8
MLA_Q_LORA = 768
MLA_KV_LORA = 512
MLA_NOPE = 128
MLA_ROPE = 64
MLA_V = 128
GQA_HEADS = 8
GQA_KV_HEADS = 2
GQA_HEAD_DIM = 128
ADAM_LR = 0.001
ADAM_B1 = 0.9
ADAM_B2 = 0.999
ADAM_EPS = 1e-08
ADAM_WD = 0.01
ADAM_STEP = 10

N_DEV = 8
MESH_AXES = ("x", "y", "c")
LANE = 128
MLA_SLOT = 2 * LANE
VMEM_LIMIT = 56 * 1024 * 1024
ROW_BLOCK = 256
ATT_Q_BLOCK = 512
ATT_Q_BLOCK_FWD = 512
LN2 = 0.6931471805599453
LOG2E = 1.4426950408889634
MESH_ID = pl.DeviceIdType.MESH


def _tile(n, pref, align=LANE):
    if n <= pref:
        return n
    best = None
    t = align
    while t <= pref:
        if n % t == 0:
            best = t
        t += align
    assert best is not None, (n, pref, align)
    return best


def _cparams(sem=None):
    return pltpu.CompilerParams(dimension_semantics=sem, vmem_limit_bytes=VMEM_LIMIT)


_ORDER_AFTER = []


def _after(*arrays):
    _ORDER_AFTER.extend(arrays)


def _pcall(body, *, in_specs, **kw):
    deps = tuple(_ORDER_AFTER)
    _ORDER_AFTER.clear()
    if not deps:
        return pl.pallas_call(body, in_specs=in_specs, **kw)
    n_in, n_dep = len(in_specs), len(deps)

    def with_deps(*refs):
        body(*refs[:n_in], *refs[n_in + n_dep:])

    call = pl.pallas_call(with_deps, in_specs=list(in_specs) + [pl.BlockSpec(memory_space=pl.ANY)] * n_dep, **kw)
    return lambda *args: call(*args, *deps)


def _all_gather(arrs, name):
    n = len(arrs)

    def body(*refs):
        ins = refs[:n]
        outs = refs[n:2 * n]
        send_sems, recv_sems, local_sems = refs[2 * n:]
        x, y, c = lax.axis_index("x"), lax.axis_index("y"), lax.axis_index("c")
        me, sibling = (x, y, c), (x, y, 1 - c)
        chips = [(1 - x, y), (x, 1 - y), (1 - x, 1 - y)]

        def rows(a, dev):
            px, py, pc = dev
            return outs[a].at[4 * px + 2 * py + pc]

        def copy(a, k, block, to, src=None):
            return pltpu.make_async_remote_copy(
                src_ref=rows(a, block) if src is None else src,
                dst_ref=rows(a, block),
                send_sem=send_sems.at[7 * a + k],
                recv_sem=recv_sems.at[7 * a + k],
                device_id=to,
                device_id_type=MESH_ID,
            )

        mine = [pltpu.make_async_copy(ins[a], rows(a, me), local_sems.at[a]) for a in range(n)]
        for cp in mine:
            cp.start()
        first = []
        for a in range(n):
            first.append(copy(a, 0, me, sibling, src=ins[a]))
            first += [copy(a, 1 + j, me, (*chip, c), src=ins[a]) for j, chip in enumerate(chips)]
        for cp in first:
            cp.start()
        passed = []
        for j, chip in enumerate(chips):
            for a in range(n):
                copy(a, 1 + j, (*chip, c), me).wait_recv()
                fwd = copy(a, 4 + j, (*chip, c), sibling)
                fwd.start()
                passed.append(fwd)
        for a in range(n):
            copy(a, 0, sibling, me).wait_recv()
            for j, chip in enumerate(chips):
                copy(a, 4 + j, (*chip, 1 - c), me).wait_recv()
        for cp in first + passed:
            cp.wait_send()
        for cp in mine:
            cp.wait()

    any_spec = pl.BlockSpec(memory_space=pl.ANY)
    outs = _pcall(
        body,
        name=name,
        out_shape=[jax.ShapeDtypeStruct((N_DEV,) + a.shape, a.dtype) for a in arrs],
        in_specs=[any_spec] * n,
        out_specs=[any_spec] * n,
        scratch_shapes=[
            pltpu.SemaphoreType.DMA((7 * n,)),
            pltpu.SemaphoreType.DMA((7 * n,)),
            pltpu.SemaphoreType.DMA((n,)),
        ],
    )(*arrs)
    return list(outs)


def _all_to_all(arrs, name):
    n = len(arrs)

    def body(*refs):
        ins = refs[:n]
        outs = refs[n:2 * n]
        send_sems, recv_sems, local_sems = refs[2 * n:]
        x, y, c = lax.axis_index("x"), lax.axis_index("y"), lax.axis_index("c")
        my_idx = 4 * x + 2 * y + c

        def peer(k):
            fx, fy, fc = (k >> 2) & 1, (k >> 1) & 1, k & 1
            return (x ^ fx if fx else x, y ^ fy if fy else y, c ^ fc if fc else c)

        def copy(a, k):
            px, py, pc = peer(k)
            return pltpu.make_async_remote_copy(
                src_ref=ins[a].at[4 * px + 2 * py + pc],
                dst_ref=outs[a].at[my_idx],
                send_sem=send_sems.at[7 * a + k - 1],
                recv_sem=recv_sems.at[7 * a + k - 1],
                device_id=(px, py, pc),
                device_id_type=MESH_ID,
            )

        mine = [pltpu.make_async_copy(ins[a].at[my_idx], outs[a].at[my_idx], local_sems.at[a]) for a in range(n)]
        for cp in mine:
            cp.start()
        order = [1, 4, 2, 5, 3, 6, 7]
        cps = [copy(a, k) for k in order for a in range(n)]
        for cp in cps:
            cp.start()
        for cp in cps:
            cp.wait()
        for cp in mine:
            cp.wait()

    any_spec = pl.BlockSpec(memory_space=pl.ANY)
    outs = _pcall(
        body,
        name=name,
        out_shape=[jax.ShapeDtypeStruct(a.shape, a.dtype) for a in arrs],
        in_specs=[any_spec] * n,
        out_specs=[any_spec] * n,
        scratch_shapes=[
            pltpu.SemaphoreType.DMA((7 * n,)),
            pltpu.SemaphoreType.DMA((7 * n,)),
            pltpu.SemaphoreType.DMA((n,)),
        ],
    )(*arrs)
    return list(outs)


_HBM = pl.BlockSpec(memory_space=pltpu.HBM)
_SEM = pl.BlockSpec(memory_space=pltpu.SEMAPHORE)
_EFFECT = pltpu.SideEffectType.DATAFLOW_SIDE_EFFECTING


def _descriptors(copies, send_sems, recv_sems):
    descs = []
    for i, (src, dst, dev) in enumerate(copies):
        if dev is None:
            descs.append(pltpu.make_async_copy(src, dst, recv_sems.at[i]))
        else:
            descs.append(pltpu.make_async_remote_copy(src_ref=src, dst_ref=dst, send_sem=send_sems.at[i],
                                                      recv_sem=recv_sems.at[i], device_id=dev, device_id_type=MESH_ID))
    return descs


def _split_start(name, arrays, copies_fn, n_copies):
    n = len(arrays)

    def body(*refs):
        send_sems, recv_sems = refs[n], refs[n + 1]
        token = refs[2 * n + 2]
        for dsc in _descriptors(copies_fn(refs[:n]), send_sems, recv_sems):
            dsc.start()
        token[...] = jnp.zeros_like(token)

    outs = _pcall(
        body,
        name=name,
        out_shape=(pltpu.SemaphoreType.DMA((n_copies,)), pltpu.SemaphoreType.DMA((n_copies,)),
                   *[pltpu.HBM(a.shape, a.dtype) for a in arrays], jax.ShapeDtypeStruct((8, LANE), F32)),
        in_specs=[_HBM] * n,
        out_specs=(_SEM, _SEM, *[_HBM] * n, pl.BlockSpec(memory_space=pltpu.VMEM)),
        input_output_aliases={i: 2 + i for i in range(n)},
        compiler_params=pltpu.CompilerParams(has_side_effects=_EFFECT),
    )(*[pltpu.with_memory_space_constraint(a, pltpu.HBM) for a in arrays])
    return outs[0], outs[1], list(outs[2:2 + n]), outs[2 + n]


def _split_wait(name, send_sems, recv_sems, arrays, copies_fn, after):
    n = len(arrays)

    def body(*refs):
        for dsc, (_, _, dev) in zip(_descriptors(copies_fn(refs[:n]), refs[n], refs[n + 1]), copies_fn(refs[:n])):
            if dev is None:
                dsc.wait()
            else:
                dsc.wait_send()
                dsc.wait_recv()

    outs = _pcall(
        body,
        name=name,
        out_shape=tuple(pltpu.HBM(a.shape, a.dtype) for a in arrays),
        in_specs=[_HBM] * n + [_SEM, _SEM, pl.BlockSpec(memory_space=pl.ANY)],
        out_specs=tuple([_HBM] * n),
        input_output_aliases={i: i for i in range(n)},
        compiler_params=pltpu.CompilerParams(has_side_effects=_EFFECT),
    )(*arrays, send_sems, recv_sems, after)
    return list(outs)


def _mesh_pos():
    x, y, c = lax.axis_index("x"), lax.axis_index("y"), lax.axis_index("c")
    return x, y, c, [(1 - x, y), (x, 1 - y), (1 - x, 1 - y)]


def _gather_ici_copies(n):
    def copies(refs):
        x, y, c, chips = _mesh_pos()
        me = 4 * x + 2 * y + c
        out = []
        for a in range(n):
            src, buf = refs[a], refs[n + a]
            out.append((src, buf.at[me], None))
            out.append((src, buf.at[me], (x, y, 1 - c)))
            out += [(src, buf.at[me], (cx, cy, c)) for cx, cy in chips[:2]]
        return out
    return copies


def _gather_pass_copies(n):
    def copies(refs):
        x, y, c, chips = _mesh_pos()
        south = c == 0
        bx, by = jnp.where(south, 1 - x, x), jnp.where(south, y, 1 - y)
        tx, ty = jnp.where(south, x, 1 - x), jnp.where(south, 1 - y, y)
        out = []
        for a in range(n):
            rows = refs[a].at[4 * bx + 2 * by + c]
            out.append((rows, rows, (tx, ty, c)))
            for cx, cy in chips[:2]:
                rows = refs[a].at[4 * cx + 2 * cy + c]
                out.append((rows, rows, (x, y, 1 - c)))
        return out
    return copies


def _gather_d2d_copies(n):
    def copies(refs):
        x, y, c, chips = _mesh_pos()
        cx, cy = chips[2]
        out = []
        for a in range(n):
            rows = refs[a].at[4 * cx + 2 * cy + c]
            out.append((rows, rows, (x, y, 1 - c)))
        return out
    return copies


def _reduce_d2d_copies(n):
    def copies(refs):
        x, y, c, _ = _mesh_pos()
        out = []
        for a in range(n):
            for k in range(4):
                out.append((refs[a].at[2 * k + (1 - c)], refs[n + a].at[k], (x, y, 1 - c)))
        return out
    return copies


def _reduce_ici_copies(n):
    def copies(refs):
        x, y, c, chips = _mesh_pos()
        mine = 2 * x + y
        out = []
        for a in range(n):
            src, land = refs[a], refs[n + a]
            out.append((src.at[mine], land.at[mine], None))
            out += [(src.at[2 * cx + cy], land.at[mine], (cx, cy, c)) for cx, cy in chips]
        return out
    return copies


def _pair_sum(send, land, c_idx, name):
    _, r, cols = send.shape
    rb = _tile(r, max(8, (1 << 22) // (send.dtype.itemsize * cols) // 8 * 8), 8)
    dt = send.dtype

    def body(c_ref, s_ref, l_ref, o_ref):
        o_ref[...] = (s_ref[...].astype(F32) + l_ref[...].astype(F32)).astype(dt)

    return pl.pallas_call(
        body,
        name=name,
        out_shape=jax.ShapeDtypeStruct((4, r, cols), dt),
        grid_spec=pltpu.PrefetchScalarGridSpec(
            num_scalar_prefetch=1,
            grid=(4, r // rb),
            in_specs=[pl.BlockSpec((None, rb, cols), lambda k, i, c_ref: (2 * k + c_ref[0], i, 0)),
                      pl.BlockSpec((None, rb, cols), lambda k, i, c_ref: (k, i, 0))],
            out_specs=pl.BlockSpec((None, rb, cols), lambda k, i, c_ref: (k, i, 0)),
        ),
        compiler_params=_cparams(("parallel", "parallel")),
    )(c_idx, send, land)


_DIMS = {
    "nn": (((1,), (0,)), ((), ())),
    "nt": (((1,), (1,)), ((), ())),
    "tn": (((0,), (0,)), ((), ())),
}


def _mm_call(a, b, *, mode, grid, a_spec, b_spec, o_spec, out_shape, acc_shape, name):
    nk = grid[2]
    out_dtype = out_shape.dtype

    def body(a_ref, b_ref, o_ref, *scratch):
        p = lax.dot_general(a_ref[...].astype(BF16), b_ref[...].astype(BF16), _DIMS[mode],
                            preferred_element_type=F32)
        if nk == 1:
            o_ref[...] = p.astype(out_dtype)
        else:
            acc = scratch[0]
            k = pl.program_id(2)

            @pl.when(k == 0)
            def _():
                acc[...] = p

            @pl.when(k > 0)
            def _():
                acc[...] += p

            @pl.when(k == nk - 1)
            def _():
                o_ref[...] = acc[...].astype(out_dtype)

    return _pcall(
        body,
        name=name,
        out_shape=out_shape,
        grid=grid,
        in_specs=[a_spec, b_spec],
        out_specs=o_spec,
        scratch_shapes=[pltpu.VMEM(acc_shape, F32)] if nk > 1 else [],
        compiler_params=_cparams(("parallel", "parallel", "arbitrary")),
    )(a, b)


def _mm(a, b, mode, out_dtype, name, tm=512, tn=512, tk=2432, a_row_off=0, rows=None):
    if mode == "nn":
        (m, k), (k2, n) = a.shape, b.shape
    elif mode == "nt":
        (m, k), (n, k2) = a.shape, b.shape
    else:
        (k, m), (k2, n) = a.shape, b.shape
        if rows is not None:
            k = k2 = rows
    assert k == k2, (a.shape, b.shape, mode)
    if mode != "tn":
        m = (m if rows is None else rows + a_row_off) - a_row_off
    tm, tn, tk = _tile(m, tm, 8), _tile(n, tn), _tile(k, tk, 8 if mode == "tn" else LANE)
    assert a_row_off % tm == 0
    ro = a_row_off // tm
    grid = (m // tm, n // tn, k // tk)
    if mode == "tn":
        a_spec = pl.BlockSpec((tk, tm), lambda i, j, kk: (kk, i))
    else:
        a_spec = pl.BlockSpec((tm, tk), lambda i, j, kk: (i + ro, kk))
    if mode == "nt":
        b_spec = pl.BlockSpec((tn, tk), lambda i, j, kk: (j, kk))
    else:
        b_spec = pl.BlockSpec((tk, tn), lambda i, j, kk: (kk, j))
    o_spec = pl.BlockSpec((tm, tn), lambda i, j, kk: (i, j))
    return _mm_call(a, b, mode=mode, grid=grid, a_spec=a_spec, b_spec=b_spec, o_spec=o_spec,
                    out_shape=jax.ShapeDtypeStruct((m, n), out_dtype), acc_shape=(tm, tn), name=name)


def _mm_cat_nt(pieces, out_dtype, name, tm=1024, tn=1024, tk=2048, rows=None):
    m = pieces[0][0].shape[0] if rows is None else rows
    n = pieces[0][1].shape[0]
    tm, tn = _tile(m, tm, 8), _tile(n, tn)
    steps, starts, s = [], [], 0
    for a, b, off in pieces:
        kp = a.shape[1]
        tkp = _tile(kp, tk)
        assert off % tkp == 0 and b.shape[0] == n
        steps.append((tkp, kp // tkp, off // tkp))
        starts.append(s)
        s += kp // tkp
    nk = s
    npc = len(pieces)

    def body(*refs):
        o_ref, acc = refs[2 * npc], refs[2 * npc + 1]
        kk = pl.program_id(2)

        @pl.when(kk == 0)
        def _():
            acc[...] = jnp.zeros_like(acc)

        for p in range(npc):
            @pl.when((kk >= starts[p]) & (kk < starts[p] + steps[p][1]))
            def _(p=p):
                acc[...] += lax.dot_general(refs[2 * p][...].astype(BF16), refs[2 * p + 1][...].astype(BF16), _DIMS["nt"],
                                            preferred_element_type=F32)

        @pl.when(kk == nk - 1)
        def _():
            o_ref[...] = acc[...].astype(out_dtype)

    in_specs, args = [], []
    for p, (a, b, off) in enumerate(pieces):
        tkp, np_, ob = steps[p]

        def rel(kk, p=p, np_=np_):
            return jnp.clip(kk - starts[p], 0, np_ - 1)

        in_specs.append(pl.BlockSpec((tm, tkp), lambda i, j, kk, rel=rel: (i, rel(kk))))
        in_specs.append(pl.BlockSpec((tn, tkp), lambda i, j, kk, rel=rel, ob=ob: (j, ob + rel(kk))))
        args += [a, b]
    return _pcall(
        body,
        name=name,
        out_shape=jax.ShapeDtypeStruct((m, n), out_dtype),
        grid=(m // tm, n // tn, nk),
        in_specs=in_specs,
        out_specs=pl.BlockSpec((tm, tn), lambda i, j, kk: (i, j)),
        scratch_shapes=[pltpu.VMEM((tm, tn), F32)],
        compiler_params=_cparams(("parallel", "parallel", "arbitrary")),
    )(*args)


def _mm_cat_tn(a, pieces, out_dtype, name, tm=1024, tn=1024, rows=None):
    k = a.shape[0] if rows is None else rows
    m = a.shape[1]
    tm = _tile(m, tm)
    starts, s = [], 0
    for b in pieces:
        assert b.shape[1] % tn == 0
        starts.append(s)
        s += b.shape[1] // tn
    nj = s
    npc = len(pieces)

    def body(*refs):
        a_ref, o_ref = refs[0], refs[1 + npc]
        j = pl.program_id(1)
        for p in range(npc):
            @pl.when((j >= starts[p]) & (j < starts[p] + pieces[p].shape[1] // tn))
            def _(p=p):
                o_ref[...] = lax.dot_general(a_ref[...].astype(BF16), refs[1 + p][...].astype(BF16), _DIMS["tn"],
                                             preferred_element_type=F32).astype(out_dtype)

    in_specs = [pl.BlockSpec((k, tm), lambda i, j: (0, i))]
    for p, b in enumerate(pieces):
        np_ = b.shape[1] // tn
        in_specs.append(pl.BlockSpec((k, tn), lambda i, j, p=p, np_=np_: (0, jnp.clip(j - starts[p], 0, np_ - 1))))
    return _pcall(
        body,
        name=name,
        out_shape=jax.ShapeDtypeStruct((m, nj * tn), out_dtype),
        grid=(m // tm, nj),
        in_specs=in_specs,
        out_specs=pl.BlockSpec((tm, tn), lambda i, j: (i, j)),
        compiler_params=_cparams(("parallel", "arbitrary")),
    )(a, *pieces)


def _mm_up_fwd(z2, w3, name, tm=1024):
    t, d = z2.shape
    nsh, _, c = w3.shape
    tm = _tile(t, tm, 8)
    return _mm_call(z2, w3, mode="nn", grid=(t // tm, nsh, 1),
                    a_spec=pl.BlockSpec((tm, d), lambda i, j, kk: (i, 0)),
                    b_spec=pl.BlockSpec((None, d, c), lambda i, j, kk: (j, 0, 0)),
                    o_spec=pl.BlockSpec((tm, c), lambda i, j, kk: (i, j)),
                    out_shape=jax.ShapeDtypeStruct((t, nsh * c), BF16), acc_shape=(tm, c), name=name)


def _mm_up_dz(du3, w3, name, tm=512, tn=1024):
    _, t, f = du3.shape
    nsh, d, c = w3.shape
    half = nsh // 2
    assert f == half * c
    tm, tn = _tile(t, tm, 8), _tile(d, tn)

    def body(a_ref, b_ref, o_ref, acc):
        kk = pl.program_id(2)
        p = None
        for s in range(half):
            q = lax.dot_general(a_ref[:, s * c:(s + 1) * c], b_ref[s], _DIMS["nt"], preferred_element_type=F32)
            p = q if p is None else p + q

        @pl.when(kk == 0)
        def _():
            acc[...] = p

        @pl.when(kk == 1)
        def _():
            o_ref[...] = (acc[...] + p).astype(BF16)

    return _pcall(
        body,
        name=name,
        out_shape=jax.ShapeDtypeStruct((t, d), BF16),
        grid=(t // tm, d // tn, 2),
        in_specs=[pl.BlockSpec((None, tm, f), lambda i, j, kk: (kk, i, 0)),
                  pl.BlockSpec((half, tn, c), lambda i, j, kk: (kk, j, 0))],
        out_specs=pl.BlockSpec((tm, tn), lambda i, j, kk: (i, j)),
        scratch_shapes=[pltpu.VMEM((tm, tn), F32)],
        compiler_params=_cparams(("parallel", "parallel", "arbitrary")),
    )(du3, w3)


def _mm_sum_nt(pieces, out_dtype, name, tm=512, tn=512, rows=None):
    m = pieces[0][0].shape[0] if rows is None else rows
    n = pieces[0][2].shape[0]
    tm, tn = _tile(m, tm, 8), _tile(n, tn)
    npc = len(pieces)

    def body(*refs):
        p = None
        for s in range(npc):
            q = lax.dot_general(refs[2 * s][...].astype(BF16), refs[2 * s + 1][...].astype(BF16), _DIMS["nt"],
                                preferred_element_type=F32)
            p = q if p is None else p + q
        refs[2 * npc][...] = p.astype(out_dtype)

    in_specs, args = [], []
    for a, ao, b, bo, kp in pieces:
        assert ao % kp == 0 and bo % kp == 0 and b.shape[0] == n
        in_specs.append(pl.BlockSpec((tm, kp), lambda i, j, ab=ao // kp: (i, ab)))
        in_specs.append(pl.BlockSpec((tn, kp), lambda i, j, bb=bo // kp: (j, bb)))
        args += [a, b]
    return _pcall(
        body,
        name=name,
        out_shape=jax.ShapeDtypeStruct((m, n), out_dtype),
        grid=(m // tm, n // tn),
        in_specs=in_specs,
        out_specs=pl.BlockSpec((tm, tn), lambda i, j: (i, j)),
        compiler_params=_cparams(("parallel", "parallel")),
    )(*args)


def _mm_up_gw(z2, du3, nsh, name, tm=1024):
    t, d = z2.shape
    f = du3.shape[2]
    half = nsh // 2
    c = f // half
    tm = _tile(d, tm)
    return _mm_call(z2, du3, mode="tn", grid=(d // tm, nsh, 1),
                    a_spec=pl.BlockSpec((t, tm), lambda i, j, kk: (0, i)),
                    b_spec=pl.BlockSpec((None, t, c), lambda i, j, kk: (j // half, 0, j % half)),
                    o_spec=pl.BlockSpec((None, tm, c), lambda i, j, kk: (j, i, 0)),
                    out_shape=jax.ShapeDtypeStruct((nsh, d, c), BF16), acc_shape=(tm, c), name=name)


def _rms(x):
    r = lax.rsqrt(jnp.mean(x * x, axis=-1, keepdims=True) + NORM_EPS)
    return x * r, r


def _rms_bwd(dxh, xh, r):
    return r * (dxh - xh * jnp.mean(dxh * xh, axis=-1, keepdims=True))


def _colsum(v):
    return jnp.sum(v, axis=0, keepdims=True)


def _rope(v, c, s1, s2, q):
    w = v.shape[-1]
    return v * c + pltpu.roll(v, w - q, 1) * s1 + pltpu.roll(v, q, 1) * s2


def _rope_t(d, c, s1, s2, q):
    w = d.shape[-1]
    return d * c + pltpu.roll(d * s1, q, 1) + pltpu.roll(d * s2, w - q, 1)


def _norm_mod_fwd(ctx, x, gain, mods):
    tc, d = ctx.shape
    t = x.shape[0]
    rb = min(ROW_BLOCK, tc)
    nbl = t // rb

    def body(ctx_ref, x_ref, g_ref, mod_ref, z_ref):
        i = pl.program_id(0)

        def emit(src, sh, sc):
            xh, _ = _rms(src[...])
            z_ref[...] = ((xh * g_ref[...]) * (1.0 + sc) + sh).astype(BF16)

        @pl.when(i >= nbl)
        def _():
            emit(ctx_ref, mod_ref[2:3, :], mod_ref[3:4, :])

        @pl.when(i < nbl)
        def _():
            emit(x_ref, mod_ref[0:1, :], mod_ref[1:2, :])

    return _pcall(
        body,
        name="norm1_mod_fwd",
        out_shape=jax.ShapeDtypeStruct((tc + t, d), BF16),
        grid=((tc + t) // rb,),
        in_specs=[
            pl.BlockSpec((rb, d), lambda i: (jnp.maximum(i - nbl, 0), 0)),
            pl.BlockSpec((rb, d), lambda i: (jnp.minimum(i, nbl - 1), 0)),
            pl.BlockSpec((1, d), lambda i: (0, 0)),
            pl.BlockSpec((8, d), lambda i: (0, 0)),
        ],
        out_specs=pl.BlockSpec((rb, d), lambda i: (i, 0)),
        compiler_params=_cparams(("arbitrary",)),
    )(ctx, x, gain, mods)


def _norm1_bwd(ctx, x, gain, mods, dz_ctx, dz_lat, dx1):
    tc, d = ctx.shape
    t = x.shape[0]
    rb = min(ROW_BLOCK, tc)
    nbl = t // rb

    def body(ctx_ref, x_ref, g_ref, mod_ref, dzc_ref, dzl_ref, dx1_ref, gx_ref, st_ref):
        i = pl.program_id(0)

        @pl.when(i == 0)
        def _():
            st_ref[...] = jnp.zeros_like(st_ref)

        def common(src, dz, sc, row_sh, row_sc):
            xh, r = _rms(src[...])
            g = g_ref[...]
            dxn = dz * (1.0 + sc)
            st_ref[row_sh:row_sh + 1, :] += _colsum(dz)
            st_ref[row_sc:row_sc + 1, :] += _colsum(dz * (xh * g))
            st_ref[2:3, :] += _colsum(dxn * xh)
            return _rms_bwd(dxn * g, xh, r)

        @pl.when(i >= nbl)
        def _():
            common(ctx_ref, dzc_ref[...], mod_ref[3:4, :], 3, 4)

        @pl.when(i < nbl)
        def _():
            gx_ref[...] = dx1_ref[...] + common(x_ref, dzl_ref[...], mod_ref[1:2, :], 0, 1)

    lat = lambda i: (jnp.minimum(i, nbl - 1), 0)
    cix = lambda i: (jnp.maximum(i - nbl, 0), 0)
    return _pcall(
        body,
        name="norm1_mod_bwd",
        out_shape=[jax.ShapeDtypeStruct((t, d), F32), jax.ShapeDtypeStruct((8, d), F32)],
        grid=((tc + t) // rb,),
        in_specs=[
            pl.BlockSpec((rb, d), cix),
            pl.BlockSpec((rb, d), lat),
            pl.BlockSpec((1, d), lambda i: (0, 0)),
            pl.BlockSpec((8, d), lambda i: (0, 0)),
            pl.BlockSpec((rb, d), cix),
            pl.BlockSpec((rb, d), lat),
            pl.BlockSpec((rb, d), lat),
        ],
        out_specs=[pl.BlockSpec((rb, d), lat), pl.BlockSpec((8, d), lambda i: (0, 0))],
        compiler_params=_cparams(("arbitrary",)),
    )(ctx, x, gain, mods, dz_ctx, dz_lat, dx1)


def _key_prep_fwd(kv, kv_gain, kb_gain, tabs):
    ta, wkv = kv.shape
    kvl = MLA_KV_LORA
    nb = GQA_KV_HEADS * GQA_HEAD_DIM
    rb = ROW_BLOCK if ta % ROW_BLOCK == 0 else LANE
    hd = GQA_HEAD_DIM

    def body(kv_ref, g_ref, gb_ref, ca, s1a, s2a, cb, s1b, s2b, kin_ref, kb_ref, vb_ref):
        xh, _ = _rms(kv_ref[:, 0:kvl])
        kin_ref[:, 0:kvl] = (xh * g_ref[...]).astype(BF16)
        kpe = kv_ref[:, kvl + 2 * nb:kvl + 2 * nb + LANE]
        kin_ref[:, kvl:kvl + LANE] = _rope(kpe, ca[...], s1a[...], s2a[...], MLA_ROPE // 4).astype(BF16)
        for h in range(GQA_KV_HEADS):
            nh, _ = _rms(kv_ref[:, kvl + h * hd:kvl + (h + 1) * hd])
            kb_ref[:, h * hd:(h + 1) * hd] = _rope(nh * gb_ref[...], cb[...], s1b[...], s2b[...], hd // 4).astype(BF16)
        vb_ref[...] = kv_ref[:, kvl + nb:kvl + 2 * nb].astype(BF16)

    row = lambda w: pl.BlockSpec((rb, w), lambda i: (i, 0))
    fix = lambda w: pl.BlockSpec((1, w), lambda i: (0, 0))
    return _pcall(
        body,
        name="key_prep_fwd",
        out_shape=[jax.ShapeDtypeStruct((ta, kvl + LANE), BF16), jax.ShapeDtypeStruct((ta, nb), BF16),
                   jax.ShapeDtypeStruct((ta, nb), BF16)],
        grid=(ta // rb,),
        in_specs=[row(wkv), fix(kvl), fix(hd)] + [row(LANE)] * 3 + [row(hd)] * 3,
        out_specs=[row(kvl + LANE), row(nb), row(nb)],
        compiler_params=_cparams(("parallel",)),
    )(kv, kv_gain, kb_gain, *tabs)


def _key_prep_bwd(kv, kv_gain, kb_gain, tabs, dkin, dkb, dvb):
    ta, wkv = kv.shape
    kvl = MLA_KV_LORA
    nb = GQA_KV_HEADS * GQA_HEAD_DIM
    rb = ROW_BLOCK if ta % ROW_BLOCK == 0 else LANE
    hd = GQA_HEAD_DIM

    def body(kv_ref, g_ref, gb_ref, ca, s1a, s2a, cb, s1b, s2b, dkin_ref, dkb_ref, dvb_ref, dkv_ref, st_ref, stb_ref):
        @pl.when(pl.program_id(0) == 0)
        def _():
            st_ref[...] = jnp.zeros_like(st_ref)
            stb_ref[...] = jnp.zeros_like(stb_ref)

        xh, r = _rms(kv_ref[:, 0:kvl])
        dn = dkin_ref[:, 0:kvl]
        st_ref[0:1, :] += _colsum(dn * xh)
        dkv_ref[:, 0:kvl] = _rms_bwd(dn * g_ref[...], xh, r).astype(BF16)
        dpe = _rope_t(dkin_ref[:, kvl:kvl + LANE], ca[...], s1a[...], s2a[...], MLA_ROPE // 4)
        dkv_ref[:, kvl + 2 * nb:kvl + 2 * nb + LANE] = dpe.astype(BF16)
        for h in range(GQA_KV_HEADS):
            nh, rh = _rms(kv_ref[:, kvl + h * hd:kvl + (h + 1) * hd])
            dn_h = _rope_t(dkb_ref[:, h * hd:(h + 1) * hd], cb[...], s1b[...], s2b[...], hd // 4)
            stb_ref[0:1, :] += _colsum(dn_h * nh)
            dkv_ref[:, kvl + h * hd:kvl + (h + 1) * hd] = _rms_bwd(dn_h * gb_ref[...], nh, rh).astype(BF16)
        dkv_ref[:, kvl + nb:kvl + 2 * nb] = dvb_ref[...].astype(BF16)

    row = lambda w: pl.BlockSpec((rb, w), lambda i: (i, 0))
    fix = lambda w: pl.BlockSpec((1, w), lambda i: (0, 0))
    return _pcall(
        body,
        name="key_prep_bwd",
        out_shape=[jax.ShapeDtypeStruct((ta, wkv), BF16), jax.ShapeDtypeStruct((8, kvl), F32),
                   jax.ShapeDtypeStruct((8, hd), F32)],
        grid=(ta // rb,),
        in_specs=[row(wkv), fix(kvl), fix(hd)] + [row(LANE)] * 3 + [row(hd)] * 3 + [row(kvl + LANE), row(nb), row(nb)],
        out_specs=[row(wkv), pl.BlockSpec((8, kvl), lambda i: (0, 0)), pl.BlockSpec((8, hd), lambda i: (0, 0))],
        compiler_params=_cparams(("arbitrary",)),
    )(kv, kv_gain, kb_gain, *tabs, dkin, dkb, dvb)


def _q_prep_fwd(qg, q_gain, qb_gain, tabs, qscale):
    t = qg.shape[0]
    ql = MLA_Q_LORA
    hd = GQA_HEAD_DIM
    hb = GQA_HEADS * hd
    rb = min(ROW_BLOCK, t)

    def body(q_ref, g_ref, gb_ref, cb, s1b, s2b, cqn_ref, qb_ref):
        xh, _ = _rms(q_ref[:, 0:ql])
        cqn_ref[...] = (xh * g_ref[...]).astype(BF16)
        for h in range(GQA_HEADS):
            nh, _ = _rms(q_ref[:, ql + h * hd:ql + (h + 1) * hd])
            qh = _rope(nh * gb_ref[...], cb[...], s1b[...], s2b[...], hd // 4)
            qb_ref[:, h * hd:(h + 1) * hd] = (qh * qscale).astype(BF16)

    row = lambda w: pl.BlockSpec((rb, w), lambda i: (i, 0))
    fix = lambda w: pl.BlockSpec((1, w), lambda i: (0, 0))
    return _pcall(
        body,
        name="q_prep_fwd",
        out_shape=[jax.ShapeDtypeStruct((t, ql), BF16), jax.ShapeDtypeStruct((t, hb), BF16)],
        grid=(t // rb,),
        in_specs=[row(ql + hb), fix(ql), fix(hd)] + [row(hd)] * 3,
        out_specs=[row(ql), row(hb)],
        compiler_params=_cparams(("parallel",)),
    )(qg, q_gain, qb_gain, *tabs)


def _q_prep_bwd(qg, q_gain, qb_gain, tabs, dcqn, dqb, wpad, qscale):
    t = qg.shape[0]
    ql = MLA_Q_LORA
    hd = GQA_HEAD_DIM
    hb = GQA_HEADS * hd
    rb = min(ROW_BLOCK, t)

    def body(q_ref, g_ref, gb_ref, cb, s1b, s2b, dcqn_ref, dqb_ref, dq_ref, st_ref, stb_ref):
        @pl.when(pl.program_id(0) == 0)
        def _():
            st_ref[...] = jnp.zeros_like(st_ref)
            stb_ref[...] = jnp.zeros_like(stb_ref)

        xh, r = _rms(q_ref[:, 0:ql])
        dn = dcqn_ref[...]
        st_ref[0:1, :] += _colsum(dn * xh)
        dq_ref[:, 0:ql] = _rms_bwd(dn * g_ref[...], xh, r).astype(BF16)
        for h in range(GQA_HEADS):
            nh, rh = _rms(q_ref[:, ql + h * hd:ql + (h + 1) * hd])
            dn_h = _rope_t(dqb_ref[:, h * hd:(h + 1) * hd] * qscale, cb[...], s1b[...], s2b[...], hd // 4)
            stb_ref[0:1, :] += _colsum(dn_h * nh)
            dq_ref[:, ql + h * hd:ql + (h + 1) * hd] = _rms_bwd(dn_h * gb_ref[...], nh, rh).astype(BF16)
        if wpad:
            dq_ref[:, ql + hb:ql + hb + wpad] = jnp.zeros((rb, wpad), BF16)

    row = lambda w: pl.BlockSpec((rb, w), lambda i: (i, 0))
    fix = lambda w: pl.BlockSpec((1, w), lambda i: (0, 0))
    return _pcall(
        body,
        name="q_prep_bwd",
        out_shape=[jax.ShapeDtypeStruct((t, ql + hb + wpad), BF16), jax.ShapeDtypeStruct((8, ql), F32),
                   jax.ShapeDtypeStruct((8, hd), F32)],
        grid=(t // rb,),
        in_specs=[row(ql + hb), fix(ql), fix(hd)] + [row(hd)] * 3 + [row(ql), row(hb)],
        out_specs=[row(ql + hb + wpad), pl.BlockSpec((8, ql), lambda i: (0, 0)), pl.BlockSpec((8, hd), lambda i: (0, 0))],
        compiler_params=_cparams(("arbitrary",)),
    )(qg, q_gain, qb_gain, *tabs, dcqn, dqb)


def _rope_a(v, tabs, transpose, out_dtype, name, qscale):
    t, w = v.shape
    rb = min(ROW_BLOCK, t)
    fn = _rope_t if transpose else _rope

    def body(v_ref, c, s1, s2, o_ref):
        for h in range(w // MLA_SLOT):
            sl = slice(h * MLA_SLOT, (h + 1) * MLA_SLOT)
            o_ref[:, sl] = (fn(v_ref[:, sl].astype(F32), c[...], s1[...], s2[...], MLA_ROPE // 4) * qscale).astype(out_dtype)

    row = lambda ww: pl.BlockSpec((rb, ww), lambda i: (i, 0))
    return _pcall(
        body,
        name=name,
        out_shape=jax.ShapeDtypeStruct((t, w), out_dtype),
        grid=(t // rb,),
        in_specs=[row(w)] + [row(MLA_SLOT)] * 3,
        out_specs=row(w),
        compiler_params=_cparams(("parallel",)),
    )(v, *tabs)


def _merge_fwd(pa, pb, qg, gate_blk):
    t, d = pa.shape
    rb = min(ROW_BLOCK, t)

    def body(pa_ref, pb_ref, ga_ref, gb_ref, o_ref):
        o_ref[...] = (jax.nn.sigmoid(ga_ref[...]) * pa_ref[...].astype(F32)
                      + jax.nn.sigmoid(gb_ref[...]) * pb_ref[...].astype(F32)).astype(BF16)

    row = pl.BlockSpec((rb, d), lambda i: (i, 0))
    return _pcall(
        body,
        name="merge_fwd",
        out_shape=jax.ShapeDtypeStruct((t, d), BF16),
        grid=(t // rb,),
        in_specs=[row, row, pl.BlockSpec((rb, d), lambda i: (i, gate_blk)), pl.BlockSpec((rb, d), lambda i: (i, gate_blk + 1))],
        out_specs=row,
        compiler_params=_cparams(("parallel",)),
    )(pa, pb, qg, qg)


def _merge_bwd(dm, pa, pb, qg, gate_blk):
    t, d = pa.shape
    rb = min(ROW_BLOCK, t)

    def body(dm_ref, pa_ref, pb_ref, ga_ref, gb_ref, dpa_ref, dpb_ref, dg_ref):
        dmv = dm_ref[...].astype(F32)
        sa = jax.nn.sigmoid(ga_ref[...])
        sb = jax.nn.sigmoid(gb_ref[...])
        dpa_ref[...] = (dmv * sa).astype(BF16)
        dpb_ref[...] = (dmv * sb).astype(BF16)
        dg_ref[:, 0:d] = (dmv * pa_ref[...].astype(F32) * (sa * (1.0 - sa))).astype(BF16)
        dg_ref[:, d:2 * d] = (dmv * pb_ref[...].astype(F32) * (sb * (1.0 - sb))).astype(BF16)

    row = pl.BlockSpec((rb, d), lambda i: (i, 0))
    return _pcall(
        body,
        name="merge_bwd",
        out_shape=[jax.ShapeDtypeStruct((t, d), BF16), jax.ShapeDtypeStruct((t, d), BF16),
                   jax.ShapeDtypeStruct((t, 2 * d), BF16)],
        grid=(t // rb,),
        in_specs=[row, row, row, pl.BlockSpec((rb, d), lambda i: (i, gate_blk)), pl.BlockSpec((rb, d), lambda i: (i, gate_blk + 1))],
        out_specs=[row, row, pl.BlockSpec((rb, 2 * d), lambda i: (i, 0))],
        compiler_params=_cparams(("parallel",)),
    )(dm, pa, pb, qg, qg)


def _resid_norm_mod(x, branch, gain, mods, name):
    t, d = x.shape
    rb = min(ROW_BLOCK, t)

    def body(x_ref, b_ref, g_ref, mod_ref, x1_ref, z_ref):
        x1 = x_ref[...] + mod_ref[0:1, :] * b_ref[...]
        x1_ref[...] = x1
        xh, _ = _rms(x1)
        z_ref[...] = ((xh * g_ref[...]) * (1.0 + mod_ref[2:3, :]) + mod_ref[1:2, :]).astype(BF16)

    row = pl.BlockSpec((rb, d), lambda i: (i, 0))
    return _pcall(
        body,
        name=name,
        out_shape=[jax.ShapeDtypeStruct((t, d), F32), jax.ShapeDtypeStruct((t, d), BF16)],
        grid=(t // rb,),
        in_specs=[row, row, pl.BlockSpec((1, d), lambda i: (0, 0)), pl.BlockSpec((8, d), lambda i: (0, 0))],
        out_specs=[row, row],
        compiler_params=_cparams(("parallel",)),
    )(x, branch, gain, mods)


def _norm2_bwd(x1, attn, gain, mods, dz2, dx2):
    t, d = x1.shape
    rb = min(ROW_BLOCK, t)

    def body(x1_ref, at_ref, g_ref, mod_ref, dz_ref, dx2_ref, dx1_ref, da_ref, st_ref):
        @pl.when(pl.program_id(0) == 0)
        def _():
            st_ref[...] = jnp.zeros_like(st_ref)

        xh, r = _rms(x1_ref[...])
        g = g_ref[...]
        dz = dz_ref[...].astype(F32)
        dxn = dz * (1.0 + mod_ref[1:2, :])
        st_ref[0:1, :] += _colsum(dz)
        st_ref[1:2, :] += _colsum(dz * (xh * g))
        st_ref[2:3, :] += _colsum(dxn * xh)
        dx1 = dx2_ref[...] + _rms_bwd(dxn * g, xh, r)
        dx1_ref[...] = dx1
        st_ref[3:4, :] += _colsum(dx1 * at_ref[...])
        da_ref[...] = (dx1 * mod_ref[0:1, :]).astype(BF16)

    row = pl.BlockSpec((rb, d), lambda i: (i, 0))
    return _pcall(
        body,
        name="norm2_mod_bwd",
        out_shape=[jax.ShapeDtypeStruct((t, d), F32), jax.ShapeDtypeStruct((t, d), BF16), jax.ShapeDtypeStruct((8, d), F32)],
        grid=(t // rb,),
        in_specs=[row, row, pl.BlockSpec((1, d), lambda i: (0, 0)), pl.BlockSpec((8, d), lambda i: (0, 0)), row, row],
        out_specs=[row, row, pl.BlockSpec((8, d), lambda i: (0, 0))],
        compiler_params=_cparams(("arbitrary",)),
    )(x1, attn, gain, mods, dz2, dx2)


def _final_loss(x1, ffn, gain, mods, target):
    t, d = x1.shape
    rb = min(ROW_BLOCK, t)
    nb = t // rb

    def body(x1_ref, f_ref, g_ref, mod_ref, tg_ref, dx2_ref, df_ref, st_ref):
        i = pl.program_id(0)

        @pl.when(i == 0)
        def _():
            st_ref[...] = jnp.zeros_like(st_ref)

        ffn_v = f_ref[...]
        g2 = mod_ref[0:1, :]
        x2 = x1_ref[...] + g2 * ffn_v
        xh, r = _rms(x2)
        g = g_ref[...]
        err = xh * g - tg_ref[...]
        st_ref[2:3, :] += _colsum(err * err) * (0.5 / d)
        dy = err * (1.0 / d)
        st_ref[0:1, :] += _colsum(dy * xh)
        dx2 = _rms_bwd(dy * g, xh, r)
        dx2_ref[...] = dx2
        st_ref[1:2, :] += _colsum(dx2 * ffn_v)
        df_ref[...] = (dx2 * g2).astype(BF16)

        @pl.when(i == nb - 1)
        def _():
            st_ref[3:4, :] = jnp.broadcast_to(jnp.sum(st_ref[2:3, :], axis=-1, keepdims=True), (1, d))

    row = pl.BlockSpec((rb, d), lambda i: (i, 0))
    return _pcall(
        body,
        name="final_norm_loss",
        out_shape=[jax.ShapeDtypeStruct((t, d), F32), jax.ShapeDtypeStruct((t, d), BF16), jax.ShapeDtypeStruct((8, d), F32)],
        grid=(nb,),
        in_specs=[row, row, pl.BlockSpec((1, d), lambda i: (0, 0)), pl.BlockSpec((8, d), lambda i: (0, 0)), row],
        out_specs=[row, row, pl.BlockSpec((8, d), lambda i: (0, 0))],
        compiler_params=_cparams(("arbitrary",)),
    )(x1, ffn, gain, mods, target)


def _row_ends(shape):
    rows = lax.broadcasted_iota(jnp.int32, shape, 0)
    return rows == 0, rows == shape[0] - 1


def _shift_dn(v, first):
    return jnp.where(first, 0.0, pltpu.roll(v, 1, 0))


def _shift_up(v, last):
    return jnp.where(last, 0.0, pltpu.roll(v, v.shape[0] - 1, 0))


def _conv_fwd(u, cw, cb):
    t, f2 = u.shape
    f = f2 // 2
    cbk = _tile(f, 256)
    nf = f // cbk

    def body(ua_ref, ub_ref, cwa_ref, cwb_ref, cba_ref, cbb_ref, h_ref, uc_ref):
        first, last = _row_ends((t, cbk))
        outs = []
        for u_ref, cw_ref, cb_ref in ((ua_ref, cwa_ref, cba_ref), (ub_ref, cwb_ref, cbb_ref)):
            uu, cwv = u_ref[...].astype(F32), cw_ref[...]
            outs.append(cb_ref[...] + cwv[0:1, :] * _shift_dn(uu, first) + cwv[1:2, :] * uu
                        + cwv[2:3, :] * _shift_up(uu, last))
        a, b = outs
        uc_ref[0] = a.astype(BF16)
        uc_ref[1] = b.astype(BF16)
        h_ref[...] = (a * jax.nn.sigmoid(a) * b).astype(BF16)

    ca = lambda r: pl.BlockSpec((r, cbk), lambda j: (0, j))
    cbs = lambda r: pl.BlockSpec((r, cbk), lambda j: (0, nf + j))
    return _pcall(
        body,
        name="conv_gate_fwd",
        out_shape=[jax.ShapeDtypeStruct((t, f), BF16), jax.ShapeDtypeStruct((2, t, f), BF16)],
        grid=(nf,),
        in_specs=[ca(t), cbs(t), ca(3), cbs(3), ca(1), cbs(1)],
        out_specs=[ca(t), pl.BlockSpec((2, t, cbk), lambda j: (0, 0, j))],
        compiler_params=_cparams(("parallel",)),
    )(u, u, cw, cw, cb, cb)


def _conv_bwd(u, uc, cw, dh):
    t, f2 = u.shape
    f = f2 // 2
    cbk = _tile(f, 256)
    nf = f // cbk

    def body(ua_ref, ub_ref, uc_ref, cwa_ref, cwb_ref, dh_ref, du_ref, dcw_ref, dcb_ref):
        first, last = _row_ends((t, cbk))
        a, b = uc_ref[0].astype(F32), uc_ref[1].astype(F32)
        dh_v = dh_ref[...].astype(F32)
        sg = jax.nn.sigmoid(a)
        db = dh_v * (a * sg)
        da = dh_v * b * (sg * (1.0 + a * (1.0 - sg)))
        for idx, (dv, u_ref, cw_ref) in enumerate(((da, ua_ref, cwa_ref), (db, ub_ref, cwb_ref))):
            uu, cwv = u_ref[...].astype(F32), cw_ref[...]
            up, dn = _shift_up(dv, last), _shift_dn(dv, first)
            dcb_ref[idx] = _colsum(dv)
            dcw_ref[idx, 0:1, :] = _colsum(up * uu)
            dcw_ref[idx, 1:2, :] = _colsum(dv * uu)
            dcw_ref[idx, 2:3, :] = _colsum(dn * uu)
            du_ref[idx] = (cwv[0:1, :] * up + cwv[1:2, :] * dv + cwv[2:3, :] * dn).astype(BF16)

    ca = lambda r: pl.BlockSpec((r, cbk), lambda j: (0, j))
    cbs = lambda r: pl.BlockSpec((r, cbk), lambda j: (0, nf + j))
    o3 = lambda r: pl.BlockSpec((2, r, cbk), lambda j: (0, 0, j))
    return _pcall(
        body,
        name="conv_gate_bwd",
        out_shape=[jax.ShapeDtypeStruct((2, t, f), BF16), jax.ShapeDtypeStruct((2, 3, f), F32),
                   jax.ShapeDtypeStruct((2, 1, f), F32)],
        grid=(nf,),
        in_specs=[ca(t), cbs(t), o3(t), ca(3), cbs(3), ca(t)],
        out_specs=[o3(t), o3(3), o3(1)],
        compiler_params=_cparams(("parallel",)),
    )(u, u, uc, cw, cw, dh)


def _attention_fwd(q, kk, vv, *, hq, hkv, dk, dv, k_blk0, v_blk0, name):
    t = q.shape[0]
    tk = kk.shape[0]
    g_sz = hq // hkv
    tq = min(ATT_Q_BLOCK_FWD, t)

    def body(q_ref, k_ref, v_ref, o_ref, lse_ref):
        k = k_ref[...]
        v = v_ref[...]
        for j in range(g_sz):
            s = lax.dot_general(q_ref[:, j * dk:(j + 1) * dk], k, _DIMS["nt"], preferred_element_type=F32)
            m = jnp.max(s, axis=-1, keepdims=True)
            p = jnp.exp2(s - m)
            l = jnp.sum(p, axis=-1, keepdims=True)
            o = jnp.dot(p.astype(BF16), v, preferred_element_type=F32) / l
            o_ref[:, j * dv:(j + 1) * dv] = o.astype(BF16)
            lse_ref[0, :, j:j + 1] = m + jnp.log2(l)

    return _pcall(
        body,
        name=name,
        out_shape=[jax.ShapeDtypeStruct((t, hq * dv), BF16), jax.ShapeDtypeStruct((hkv, t, g_sz), F32)],
        grid=(hkv, t // tq),
        in_specs=[
            pl.BlockSpec((tq, g_sz * dk), lambda g, i: (i, g)),
            pl.BlockSpec((tk, dk), lambda g, i: (0, k_blk0 + g)),
            pl.BlockSpec((tk, dv), lambda g, i: (0, v_blk0 + g)),
        ],
        out_specs=[
            pl.BlockSpec((tq, g_sz * dv), lambda g, i: (i, g)),
            pl.BlockSpec((1, tq, g_sz), lambda g, i: (g, i, 0)),
        ],
        compiler_params=_cparams(("parallel", "parallel")),
    )(q, kk, vv)


def _attention_bwd(q, kk, vv, do, lse, *, hq, hkv, dk, dv, k_blk0, v_blk0, name):
    t = q.shape[0]
    tk = kk.shape[0]
    g_sz = hq // hkv
    tq = min(ATT_Q_BLOCK, t)

    def body(q_ref, k_ref, v_ref, do_ref, lse_ref, dq_ref, dk_ref, dv_ref):
        @pl.when(pl.program_id(1) == 0)
        def _():
            dk_ref[...] = jnp.zeros_like(dk_ref)
            dv_ref[...] = jnp.zeros_like(dv_ref)

        k = k_ref[...]
        v = v_ref[...]
        for j in range(g_sz):
            qj = q_ref[:, j * dk:(j + 1) * dk]
            doj = do_ref[:, j * dv:(j + 1) * dv]
            s = lax.dot_general(qj, k, _DIMS["nt"], preferred_element_type=F32)
            p = jnp.exp2(s - lse_ref[0, :, j:j + 1])
            dp = lax.dot_general(doj, v, _DIMS["nt"], preferred_element_type=F32)
            ds = (p * (dp - jnp.sum(p * dp, axis=-1, keepdims=True))).astype(BF16)
            dv_ref[...] += lax.dot_general(p.astype(BF16), doj, _DIMS["tn"], preferred_element_type=F32)
            dk_ref[...] += lax.dot_general(ds, qj, _DIMS["tn"], preferred_element_type=F32)
            dq_ref[:, j * dk:(j + 1) * dk] = jnp.dot(ds, k, preferred_element_type=F32)

        @pl.when(pl.program_id(1) == t // tq - 1)
        def _():
            dk_ref[...] *= LN2

    return _pcall(
        body,
        name=name,
        out_shape=[jax.ShapeDtypeStruct((t, hq * dk), F32), jax.ShapeDtypeStruct((tk, hkv * dk), F32),
                   jax.ShapeDtypeStruct((tk, hkv * dv), F32)],
        grid=(hkv, t // tq),
        in_specs=[
            pl.BlockSpec((tq, g_sz * dk), lambda g, i: (i, g)),
            pl.BlockSpec((tk, dk), lambda g, i: (0, k_blk0 + g)),
            pl.BlockSpec((tk, dv), lambda g, i: (0, v_blk0 + g)),
            pl.BlockSpec((tq, g_sz * dv), lambda g, i: (i, g)),
            pl.BlockSpec((1, tq, g_sz), lambda g, i: (g, i, 0)),
        ],
        out_specs=[
            pl.BlockSpec((tq, g_sz * dk), lambda g, i: (i, g)),
            pl.BlockSpec((tk, dk), lambda g, i: (0, g)),
            pl.BlockSpec((tk, dv), lambda g, i: (0, g)),
        ],
        compiler_params=_cparams(("parallel", "arbitrary")),
    )(q, kk, vv, do, lse)


def _silu(v):
    return v * jax.nn.sigmoid(v)


def _ada_fwd(conds, w_ada, b_ada_shard):
    r, d = conds.shape
    n = w_ada.shape[1]
    tn = _tile(n, 512)

    def body(c_ref, w_ref, b_ref, o_ref):
        s = _silu(c_ref[...]).astype(BF16)
        o_ref[...] = jnp.dot(s, w_ref[...].astype(BF16), preferred_element_type=F32) + b_ref[...]

    return _pcall(
        body,
        name="ada_fwd",
        out_shape=jax.ShapeDtypeStruct((r, n), F32),
        grid=(n // tn,),
        in_specs=[pl.BlockSpec((r, d), lambda j: (0, 0)), pl.BlockSpec((d, tn), lambda j: (0, j)),
                  pl.BlockSpec((1, tn), lambda j: (0, j))],
        out_specs=pl.BlockSpec((r, tn), lambda j: (0, j)),
        compiler_params=_cparams(("parallel",)),
    )(conds, w_ada, b_ada_shard)


def _cctx_partial(da16_shard, w_ada, c_ctx_row):
    d, n = w_ada.shape
    td = _tile(d, 512)

    def body(g_ref, w_ref, c_ref, o_ref):
        ds = lax.dot_general(g_ref[8:16, :].astype(BF16), w_ref[...].astype(BF16), _DIMS["nt"],
                             preferred_element_type=F32)
        cv = c_ref[...]
        sg = jax.nn.sigmoid(cv)
        o_ref[...] = ds * (sg * (1.0 + cv * (1.0 - sg)))

    return _pcall(
        body,
        name="cctx_partial",
        out_shape=jax.ShapeDtypeStruct((8, d), F32),
        grid=(d // td,),
        in_specs=[pl.BlockSpec((16, n), lambda j: (0, 0)), pl.BlockSpec((td, n), lambda j: (j, 0)),
                  pl.BlockSpec((1, td), lambda j: (0, j))],
        out_specs=pl.BlockSpec((8, td), lambda j: (0, j)),
        compiler_params=_cparams(("parallel",)),
    )(da16_shard, w_ada, c_ctx_row)


def _sum_parts(parts):
    p, _, n = parts.shape

    def body(p_ref, o_ref):
        acc = p_ref[0]
        for s in range(1, p):
            acc = acc + p_ref[s]
        o_ref[...] = acc

    return _pcall(
        body,
        name="sum_parts",
        out_shape=jax.ShapeDtypeStruct((1, n), F32),
        in_specs=[pl.BlockSpec(memory_space=pltpu.VMEM)],
        out_specs=pl.BlockSpec(memory_space=pltpu.VMEM),
    )(parts)


def _adam_math(w, g, m, v):
    m2 = ADAM_B1 * m + (1.0 - ADAM_B1) * g
    v2 = ADAM_B2 * v + (1.0 - ADAM_B2) * jnp.square(g)
    m_hat = m2 / (1.0 - ADAM_B1 ** ADAM_STEP)
    v_hat = v2 / (1.0 - ADAM_B2 ** ADAM_STEP)
    delta = -ADAM_LR * (m_hat / (jnp.sqrt(v_hat) + ADAM_EPS) + ADAM_WD * w)
    return delta, m2, v2


def _adamw(parts, w, m, v, name):
    p, r, c = parts.shape
    rb = _tile(r, max(8, (1 << 20) // (4 * c) // 8 * 8), 8)

    def body(p_ref, w_ref, m_ref, v_ref, g_ref, d_ref, m2_ref, v2_ref):
        g = p_ref[0].astype(F32)
        for s in range(1, p):
            g = g + p_ref[s].astype(F32)
        g_ref[...] = g
        d_ref[...], m2_ref[...], v2_ref[...] = _adam_math(w_ref[...], g, m_ref[...], v_ref[...])

    if w.ndim == 3:
        row = pl.BlockSpec((None, rb, c), lambda i: (0, i, 0))
    else:
        row = pl.BlockSpec((rb, c), lambda i: (i, 0))
    return _pcall(
        body,
        name=name,
        out_shape=[jax.ShapeDtypeStruct(w.shape, F32)] * 4,
        grid=(r // rb,),
        in_specs=[pl.BlockSpec((p, rb, c), lambda i: (0, i, 0)), row, row, row],
        out_specs=[row] * 4,
        compiler_params=_cparams(("parallel",)),
    )(parts, w, m, v)


def _adamw_ada(conds, da16, w, m, v):
    d, n = w.shape
    rb = _tile(d, 256, LANE)

    def body(s_ref, da_ref, w_ref, m_ref, v_ref, g_ref, d_ref, m2_ref, v2_ref):
        g = lax.dot_general(_silu(s_ref[...]).astype(BF16), da_ref[...].astype(BF16), _DIMS["tn"],
                            preferred_element_type=F32)
        g_ref[...] = g
        d_ref[...], m2_ref[...], v2_ref[...] = _adam_math(w_ref[...], g, m_ref[...], v_ref[...])

    row = pl.BlockSpec((rb, n), lambda i: (i, 0))
    return _pcall(
        body,
        name="adamw_w_ada",
        out_shape=[jax.ShapeDtypeStruct((d, n), F32)] * 4,
        grid=(d // rb,),
        in_specs=[pl.BlockSpec((16, rb), lambda i: (0, i)), pl.BlockSpec((16, n), lambda i: (0, 0)), row, row, row],
        out_specs=[row] * 4,
        compiler_params=_cparams(("parallel",)),
    )(conds, da16, w, m, v)


def _touch(arrays, name):
    def body(*refs):
        refs[-1][...] = jnp.zeros((8, LANE), F32)

    return _pcall(body, name=name, out_shape=jax.ShapeDtypeStruct((8, LANE), F32),
                  in_specs=[pl.BlockSpec(memory_space=pl.ANY)] * len(arrays),
                  out_specs=pl.BlockSpec(memory_space=pltpu.VMEM))(*arrays)


def _cast_bf16(a, name):
    _, r, c = a.shape
    rb = _tile(r, 512, 8)

    def body(a_ref, o_ref):
        o_ref[...] = a_ref[...].astype(BF16)

    return _pcall(body, name=name, out_shape=jax.ShapeDtypeStruct((r, c), BF16), grid=(r // rb,),
                  in_specs=[pl.BlockSpec((None, rb, c), lambda i: (0, i, 0))],
                  out_specs=pl.BlockSpec((rb, c), lambda i: (i, 0)), compiler_params=_cparams(("parallel",)))(a)


def _rope_tabs(t, rot):
    half, q = rot // 2, rot // 4
    n_rows = t // GRID_W
    row = jnp.repeat(jnp.arange(n_rows, dtype=F32), GRID_W)
    col = jnp.tile(jnp.arange(GRID_W, dtype=F32), n_rows)
    inv_freq = ROPE_THETA ** (-jnp.arange(0, half, 2, dtype=F32) / half)
    ang = jnp.concatenate([row[:, None] * inv_freq, col[:, None] * inv_freq], axis=-1)
    cos, sin = jnp.cos(ang), jnp.sin(ang)
    c0, c1, s0, s1 = cos[:, :q], cos[:, q:], sin[:, :q], sin[:, q:]
    z = jnp.zeros_like(s0)
    return (jnp.concatenate([c0, c0, c1, c1], -1), jnp.concatenate([-s0, z, -s1, z], -1),
            jnp.concatenate([z, s0, z, s1], -1))


def _pad_cols(a, left, total, fill=0.0):
    return jnp.pad(a, ((0, 0), (left, total - left - a.shape[1])), constant_values=fill)


def _with_ctx_rows(tab, tc, fill):
    return jnp.concatenate([tab, jnp.full((tc, tab.shape[1]), fill, F32)], axis=0)


def kernel(x, c, ctx, c_ctx, w_ada, b_ada, norm1_g, w_in, mla_q_norm_g, w_q_up, mla_kv_norm_g, w_kv_up, gqa_q_norm_g, gqa_k_norm_g, w_br_a, w_br_b, w_out, norm2_g, w_up, conv_w, conv_b, w_down, final_norm_g, loss_target, m_c_ctx, m_w_ada, m_b_ada, m_norm1_g, m_w_in, m_mla_q_norm_g, m_w_q_up, m_mla_kv_norm_g, m_w_kv_up, m_gqa_q_norm_g, m_gqa_k_norm_g, m_w_br_a, m_w_br_b, m_w_out, m_norm2_g, m_w_up, m_conv_w, m_conv_b, m_w_down, m_final_norm_g, v_c_ctx, v_w_ada, v_b_ada, v_norm1_g, v_w_in, v_mla_q_norm_g, v_w_q_up, v_mla_kv_norm_g, v_w_kv_up, v_gqa_q_norm_g, v_gqa_k_norm_g, v_w_br_a, v_w_br_b, v_w_out, v_norm2_g, v_w_up, v_conv_w, v_conv_b, v_w_down, v_final_norm_g):
    weights = dict(c_ctx=c_ctx, w_ada=w_ada, b_ada=b_ada, norm1_g=norm1_g, w_in=w_in, mla_q_norm_g=mla_q_norm_g,
                   w_q_up=w_q_up, mla_kv_norm_g=mla_kv_norm_g, w_kv_up=w_kv_up, gqa_q_norm_g=gqa_q_norm_g,
                   gqa_k_norm_g=gqa_k_norm_g, w_br_a=w_br_a, w_br_b=w_br_b, w_out=w_out, norm2_g=norm2_g, w_up=w_up,
                   conv_w=conv_w, conv_b=conv_b, w_down=w_down, final_norm_g=final_norm_g)
    mom_m = dict(c_ctx=m_c_ctx, w_ada=m_w_ada, b_ada=m_b_ada, norm1_g=m_norm1_g, w_in=m_w_in, mla_q_norm_g=m_mla_q_norm_g,
                 w_q_up=m_w_q_up, mla_kv_norm_g=m_mla_kv_norm_g, w_kv_up=m_w_kv_up, gqa_q_norm_g=m_gqa_q_norm_g,
                 gqa_k_norm_g=m_gqa_k_norm_g, w_br_a=m_w_br_a, w_br_b=m_w_br_b, w_out=m_w_out, norm2_g=m_norm2_g,
                 w_up=m_w_up, conv_w=m_conv_w, conv_b=m_conv_b, w_down=m_w_down, final_norm_g=m_final_norm_g)
    mom_v = dict(c_ctx=v_c_ctx, w_ada=v_w_ada, b_ada=v_b_ada, norm1_g=v_norm1_g, w_in=v_w_in, mla_q_norm_g=v_mla_q_norm_g,
                 w_q_up=v_w_q_up, mla_kv_norm_g=v_mla_kv_norm_g, w_kv_up=v_w_kv_up, gqa_q_norm_g=v_gqa_q_norm_g,
                 gqa_k_norm_g=v_gqa_k_norm_g, w_br_a=v_w_br_a, w_br_b=v_w_br_b, w_out=v_w_out, norm2_g=v_norm2_g,
                 w_up=v_w_up, conv_w=v_conv_w, conv_b=v_conv_b, w_down=v_w_down, final_norm_g=v_final_norm_g)
    order = list(weights)

    my_idx = 4 * lax.axis_index("x") + 2 * lax.axis_index("y") + lax.axis_index("c")
    xs, cts, tgt = x[0], ctx[0], loss_target[0]
    t, d = xs.shape
    tc = cts.shape[0]
    ta = t + tc
    kvl, ql = MLA_KV_LORA, MLA_Q_LORA
    nb = GQA_KV_HEADS * GQA_HEAD_DIM
    hb = GQA_HEADS * GQA_HEAD_DIM
    ha = MLA_HEADS
    f2 = w_up.shape[2] * N_DEV
    ff = f2 // 2

    big = ["w_in", "w_q_up", "w_kv_up", "w_br_a", "w_br_b", "w_out", "w_up", "w_down"]
    nw = len(big)
    del nw
    _ORDER_AFTER.clear()
    shards = {"w_in": _cast_bf16(weights["w_in"], "cast_w_in")}
    c_idx = jnp.reshape(lax.axis_index("c"), (1,)).astype(jnp.int32)

    def gather_start(names, dep):
        shs = [shards[n] for n in names]
        land = [lax.empty((N_DEV,) + s.shape, BF16) for s in shs]
        if dep is not None:
            _after(dep)
        s, r, arrs, tok = _split_start("gather_ici_start_" + names[0], shs + land, _gather_ici_copies(len(names)),
                                       4 * len(names))
        return dict(names=names, s=s, r=r, arrs=arrs, tok=tok)

    def gather_pass(g, after):
        n = len(g["names"])
        arrs = _split_wait("gather_ici_wait_" + g["names"][0], g["s"], g["r"], g["arrs"], _gather_ici_copies(n), after)
        s, r, bufs, tok = _split_start("gather_pass_start_" + g["names"][0], arrs[n:], _gather_pass_copies(n), 3 * n)
        g.update(s2=s, r2=r, bufs=bufs)
        return tok

    def gather_relay(g, after):
        n = len(g["names"])
        bufs = _split_wait("gather_pass_wait_" + g["names"][0], g["s2"], g["r2"], g["bufs"], _gather_pass_copies(n), after)
        s, r, bufs, tok = _split_start("gather_d2d_start_" + g["names"][0], bufs, _gather_d2d_copies(n), n)
        g.update(s3=s, r3=r, bufs=bufs)
        return tok

    def gather_finish(g, after):
        n = len(g["names"])
        bufs = _split_wait("gather_d2d_wait_" + g["names"][0], g["s3"], g["r3"], g["bufs"], _gather_d2d_copies(n), after)
        return dict(zip(g["names"], bufs))

    c_all, cw_all = _all_gather([jnp.pad(c, ((0, 7), (0, 0))), jnp.pad(conv_w[0], ((0, 5), (0, 0)))], "gather_cond")
    conv_w_f = jnp.transpose(cw_all[:, :3, :], (1, 0, 2)).reshape(3, f2)
    conds = jnp.concatenate([c_all[:, 0, :], c_ctx[None, :], jnp.zeros((7, d), F32)], axis=0)
    ncol = w_ada.shape[2]
    b_shard = lax.dynamic_slice_in_dim(b_ada, my_idx * ncol, ncol, axis=1)
    ada_shard = _ada_fwd(conds, w_ada[0], b_shard)
    (ada_all,) = _all_gather([ada_shard], "gather_ada")
    ada = jnp.transpose(ada_all, (1, 0, 2)).reshape(16, N_DEV * ncol)
    lat = lax.dynamic_slice_in_dim(ada, my_idx, 1, axis=0).reshape(6, d)
    cxt = ada[8].reshape(6, d)
    zero2 = jnp.zeros((2, d), F32)
    mods1 = jnp.concatenate([lat[0:2], cxt[0:2], jnp.zeros((4, d), F32)], axis=0)
    mods2 = jnp.concatenate([lat[2:3], lat[3:4], lat[4:5], jnp.zeros((5, d), F32)], axis=0)
    mods2b = jnp.concatenate([lat[2:3], lat[4:5], jnp.zeros((6, d), F32)], axis=0)
    mods3 = jnp.concatenate([lat[5:6], jnp.zeros((7, d), F32)], axis=0)
    del zero2

    g0 = gather_start(["w_in"], ada_all)
    for n in big[1:]:
        _after(g0["tok"])
        shards[n] = _cast_bf16(weights[n], "cast_" + n)

    ca, s1a, s2a = _rope_tabs(t, MLA_ROPE)
    cb_, s1b, s2b = _rope_tabs(t, GQA_HEAD_DIM)
    q_tabs_a = (_pad_cols(jnp.concatenate([jnp.ones((t, MLA_NOPE), F32), ca], 1), 0, MLA_SLOT),
                _pad_cols(s1a, MLA_NOPE, MLA_SLOT), _pad_cols(s2a, MLA_NOPE, MLA_SLOT))
    q_tabs_b = (cb_, s1b, s2b)
    k_tabs = (_with_ctx_rows(_pad_cols(ca, 0, LANE), tc, 1.0), _with_ctx_rows(_pad_cols(s1a, 0, LANE), tc, 0.0),
              _with_ctx_rows(_pad_cols(s2a, 0, LANE), tc, 0.0),
              _with_ctx_rows(cb_, tc, 1.0), _with_ctx_rows(s1b, tc, 0.0), _with_ctx_rows(s2b, tc, 0.0))

    def cols_full(g):
        return jnp.transpose(g, (1, 0, 2)).reshape(g.shape[1], N_DEV * g.shape[2])

    _after(g0["tok"])
    early = _touch([mom_m["w_in"], mom_v["w_in"], mom_m["w_q_up"], mom_v["w_q_up"]], "touch_moments")
    _after(early, *q_tabs_a, *q_tabs_b, *k_tabs, *[shards[n] for n in big[1:]])
    tok_p0 = gather_pass(g0, mods1)
    g1 = gather_start(["w_q_up", "w_kv_up", "w_br_a", "w_br_b", "w_out"], tok_p0)
    _after(g1["tok"])
    z_all = _norm_mod_fwd(cts, xs, norm1_g, mods1)
    gathered = gather_finish(g0, gather_relay(g0, z_all))
    w_in_f = cols_full(gathered["w_in"])
    o_kpe, o_kb, o_vb = kvl, kvl + MLA_ROPE, kvl + MLA_ROPE + nb
    o_q = o_vb + nb
    o_g = o_q + ql + hb
    wkv_w = kvl + 2 * nb + LANE
    w_kv_p = jnp.concatenate([w_in_f[:, :kvl], w_in_f[:, o_kb:o_q], w_in_f[:, o_kpe:o_kb],
                              jnp.zeros((d, LANE - MLA_ROPE), BF16)], axis=1)
    q_w = ql + hb
    q_pad = (-q_w) % 512 if d >= 512 else (-q_w) % d
    gate_blk = (q_w + q_pad) // d
    assert (q_w + q_pad) % d == 0
    w_qg_p = jnp.concatenate([w_in_f[:, o_q:o_g], jnp.zeros((d, q_pad), BF16), w_in_f[:, o_g:]], axis=1)

    kv_all = _mm(z_all, w_kv_p, "nn", F32, "proj_kv", tm=1152, tn=wkv_w)
    qg = _mm(z_all, w_qg_p, "nn", F32, "proj_qg", tm=1024, tn=1024, rows=t)
    tok_p1 = gather_pass(g1, qg)
    g2 = gather_start(["w_up"], tok_p1)
    g3 = gather_start(["w_down"], g2["tok"])
    _after(g3["tok"])
    kin, k_b, v_b = _key_prep_fwd(kv_all, mla_kv_norm_g, gqa_k_norm_g, k_tabs)
    sc_a = float((MLA_NOPE + MLA_ROPE) ** -0.5) * LOG2E
    sc_b = float(GQA_HEAD_DIM ** -0.5) * LOG2E
    cqn, q_b = _q_prep_fwd(qg, mla_q_norm_g, gqa_q_norm_g, q_tabs_b, sc_b)
    gathered.update(gather_finish(g1, gather_relay(g1, q_b)))

    wq_f = cols_full(gathered["w_q_up"]).reshape(ql, ha, MLA_NOPE + MLA_ROPE)
    wq_ext = jnp.pad(wq_f, ((0, 0), (0, 0), (0, MLA_SLOT - MLA_NOPE - MLA_ROPE))).reshape(ql, ha * MLA_SLOT)
    wkv_f = cols_full(gathered["w_kv_up"]).reshape(kvl, ha, MLA_NOPE + MLA_V)
    wk_slots = jnp.pad(wkv_f[:, :, :MLA_NOPE], ((0, 0), (0, 0), (0, MLA_SLOT - MLA_NOPE))).reshape(kvl, ha * MLA_SLOT)
    wv_cols = wkv_f[:, :, MLA_NOPE:].reshape(kvl, ha * MLA_V)
    e_slot = jnp.pad(jnp.eye(MLA_ROPE, dtype=BF16),
                     ((0, LANE - MLA_ROPE), (MLA_NOPE, MLA_SLOT - MLA_NOPE - MLA_ROPE)))
    e_rows = jnp.concatenate([jnp.tile(e_slot, (1, ha)), jnp.zeros((LANE, ha * MLA_V), BF16)], axis=1)
    wkv_ext = jnp.concatenate([jnp.concatenate([wk_slots, wv_cols], axis=1), e_rows], axis=0)
    w_bra = cols_full(gathered["w_br_a"])
    w_brb = cols_full(gathered["w_br_b"])
    w_out_f = gathered["w_out"].reshape(d, d)

    kv_a = _mm(kin, wkv_ext, "nn", BF16, "kv_up", tm=1152, tn=1024)
    qa_raw = _mm(cqn, wq_ext, "nn", F32, "q_up", tm=1024, tn=1024)
    q_a = _rope_a(qa_raw, q_tabs_a, False, BF16, "rope_q_fwd", sc_a)
    att_a = dict(hq=ha, hkv=ha, dk=MLA_SLOT, dv=MLA_V, k_blk0=0, v_blk0=ha * MLA_SLOT // MLA_V)
    att_b = dict(hq=GQA_HEADS, hkv=GQA_KV_HEADS, dk=GQA_HEAD_DIM, dv=GQA_HEAD_DIM, k_blk0=0, v_blk0=0)
    o_a, lse_a = _attention_fwd(q_a, kv_a, kv_a, name="attn_a_fwd", **att_a)
    o_b, lse_b = _attention_fwd(q_b, k_b, v_b, name="attn_b_fwd", **att_b)
    _after(gather_pass(g2, o_b))
    pa = _mm(o_a, w_bra, "nn", BF16, "br_a", tm=1024, tn=1024)
    pb = _mm(o_b, w_brb, "nn", BF16, "br_b", tm=1024, tn=1024)
    merged = _merge_fwd(pa, pb, qg, gate_blk)
    attn = _mm(merged, w_out_f, "nn", F32, "w_out", tm=1024, tn=1024)
    x1, z2 = _resid_norm_mod(xs, attn, norm2_g, mods2, "resid_norm2_fwd")
    tok_r2 = gather_relay(g2, z2)
    tok_p3 = gather_pass(g3, tok_r2)
    w_up3 = gather_finish(g2, tok_p3)["w_up"]
    u = _mm_up_fwd(z2, w_up3, "w_up")
    tok_r3 = gather_relay(g3, u)
    _after(tok_r3)
    h, uc = _conv_fwd(u, conv_w_f, conv_b)
    w_down_f = gather_finish(g3, h)["w_down"].reshape(ff, d)
    ffn = _mm(h, w_down_f, "nn", F32, "w_down", tm=1024, tn=1024, tk=2816)

    def to_shards(g):
        return jnp.transpose(g.reshape(g.shape[0], N_DEV, g.shape[1] // N_DEV), (1, 0, 2))

    def reduce_start(tag, names, sends):
        n = len(sends)
        land = [lax.empty((4,) + s.shape[1:], s.dtype) for s in sends]
        s, r, arrs, tok = _split_start("reduce_d2d_start_" + tag, sends + land, _reduce_d2d_copies(n), 4 * n)
        return dict(tag=tag, names=names, s=s, r=r, arrs=arrs, tok=tok)

    def reduce_relay(g, after):
        n = len(g["names"])
        arrs = _split_wait("reduce_d2d_wait_" + g["tag"], g["s"], g["r"], g["arrs"], _reduce_d2d_copies(n), after)
        sums = [_pair_sum(arrs[a], arrs[n + a], c_idx, "pair_sum_" + g["names"][a]) for a in range(n)]
        land = [lax.empty(s.shape, s.dtype) for s in sums]
        s, r, arrs2, tok = _split_start("reduce_ici_start_" + g["tag"], sums + land, _reduce_ici_copies(n), 4 * n)
        g.update(s2=s, r2=r, arrs2=arrs2)
        return tok

    def reduce_finish(g, after):
        n = len(g["names"])
        arrs2 = _split_wait("reduce_ici_wait_" + g["tag"], g["s2"], g["r2"], g["arrs2"], _reduce_ici_copies(n), after)
        return dict(zip(g["names"], arrs2[n:]))

    dx2, dffn, st_fin = _final_loss(x1, ffn, final_norm_g[None, :], mods3, tgt)
    loss = lax.psum(st_fin[3, 0], MESH_AXES)
    dh = _mm(dffn, w_down_f, "nt", BF16, "d_h", tm=1024, tn=1024)
    g_w_down = _mm(h, dffn, "tn", BF16, "g_w_down", tm=512, tn=1024)
    r_down = reduce_start("down", ["w_down"], [g_w_down.reshape(N_DEV, ff // N_DEV, d)])
    _after(r_down["tok"])
    du3, dcw, dcb = _conv_bwd(u, uc, conv_w_f, dh)
    dz2 = _mm_up_dz(du3, w_up3, "d_z2")
    g_w_up = _mm_up_gw(z2, du3, N_DEV, "g_w_up")
    g_conv_w = jnp.concatenate([dcw[0], dcw[1]], axis=1)
    tok = reduce_relay(r_down, g_w_up)
    _after(tok)
    r_up = reduce_start("up", ["w_up", "conv_w"], [g_w_up, to_shards(jnp.pad(g_conv_w, ((0, 5), (0, 0))))])
    _after(tok, r_up["tok"])
    dx1, dattn, st_n2 = _norm2_bwd(x1, attn, norm2_g, mods2b, dz2, dx2)
    dmerged = _mm(dattn, w_out_f, "nt", BF16, "d_merged", tm=1024, tn=1024)
    g_w_out = _mm(merged, dattn, "tn", BF16, "g_w_out", tm=1024, tn=1024)
    dpa, dpb, dgates = _merge_bwd(dmerged, pa, pb, qg, gate_blk)
    do_a = _mm(dpa, w_bra, "nt", BF16, "d_o_a", tm=1024, tn=1024)
    do_b = _mm(dpb, w_brb, "nt", BF16, "d_o_b", tm=1024, tn=1024)
    g_w_bra = _mm(o_a, dpa, "tn", BF16, "g_w_br_a", tm=1024, tn=1024)
    g_w_brb = _mm(o_b, dpb, "tn", BF16, "g_w_br_b", tm=1024, tn=1024)
    tok = reduce_relay(r_up, g_w_brb)
    _after(tok)
    r_out = reduce_start("out", ["w_out", "w_br_a", "w_br_b"],
                         [g_w_out.reshape(N_DEV, d // N_DEV, d), to_shards(g_w_bra), to_shards(g_w_brb)])
    _after(tok, r_out["tok"])
    dq_a, dk_a, dv_a = _attention_bwd(q_a, kv_a, kv_a, do_a, lse_a, name="attn_a_bwd", **att_a)
    dq_b, dk_b, dv_b = _attention_bwd(q_b, k_b, v_b, do_b, lse_b, name="attn_b_bwd", **att_b)
    _after(reduce_relay(r_out, dv_b))
    dqa_raw = _rope_a(dq_a, q_tabs_a, True, BF16, "rope_q_bwd", sc_a * LN2)
    dcqn = _mm(dqa_raw, wq_ext, "nt", F32, "d_cqn", tm=1024, tn=ql)
    g_wq_ext = _mm(cqn, dqa_raw, "tn", BF16, "g_w_q_up", tm=ql, tn=1024)
    dq_p, st_q, st_qb = _q_prep_bwd(qg, mla_q_norm_g, gqa_q_norm_g, q_tabs_b, dcqn, dq_b, q_pad, sc_b * LN2)
    dkin = _mm_cat_nt([(dk_a, wkv_ext, 0), (dv_a, wkv_ext, ha * MLA_SLOT)], F32, "d_kin", tm=1152, tn=kvl + LANE)
    g_wkv_ext = _mm_cat_tn(kin, [dk_a, dv_a], BF16, "g_w_kv_up", tm=kvl + LANE, tn=min(1024, ha * MLA_V))
    dkv_p, st_kv, st_kb = _key_prep_bwd(kv_all, mla_kv_norm_g, gqa_k_norm_g, k_tabs, dkin, dk_b, dv_b)
    g_wq = g_wq_ext.reshape(ql, ha, MLA_SLOT)[:, :, :MLA_NOPE + MLA_ROPE].reshape(ql, ha * (MLA_NOPE + MLA_ROPE))
    g_wkv = jnp.concatenate([g_wkv_ext[:kvl, :ha * MLA_SLOT].reshape(kvl, ha, MLA_SLOT)[:, :, :MLA_NOPE],
                             g_wkv_ext[:kvl, ha * MLA_SLOT:].reshape(kvl, ha, MLA_V)], axis=2).reshape(kvl, ha * (MLA_NOPE + MLA_V))
    r_qkv = reduce_start("qkv", ["w_q_up", "w_kv_up"], [to_shards(g_wq), to_shards(g_wkv)])
    _after(r_qkv["tok"])
    g_wkv_p = _mm(z_all, dkv_p, "tn", BF16, "g_w_in_kv", tm=1024, tn=wkv_w)
    g_wqg_p = _mm_cat_tn(z_all, [dq_p, dgates], BF16, "g_w_in_qg", tm=1024, tn=min(1024, d), rows=t)
    g_w_in = jnp.concatenate([g_wkv_p[:, :kvl], g_wkv_p[:, kvl + 2 * nb:kvl + 2 * nb + MLA_ROPE],
                              g_wkv_p[:, kvl:kvl + 2 * nb], g_wqg_p[:, :q_w], g_wqg_p[:, q_w + q_pad:]], axis=1)
    r_in = reduce_start("in", ["w_in"], [to_shards(g_w_in)])
    _after(r_in["tok"])
    qw_p = q_w + q_pad
    dz_lat = _mm_sum_nt([(dq_p, 0, w_qg_p, 0, qw_p), (dgates, 0, w_qg_p, qw_p, d), (dgates, d, w_qg_p, qw_p + d, d),
                         (dkv_p, 0, w_kv_p, 0, wkv_w)], F32, "d_z_lat", rows=t)
    dz_ctx = _mm(dkv_p, w_kv_p, "nt", F32, "d_z_ctx", tm=min(ROW_BLOCK, tc), tn=1024, a_row_off=t)
    tok_q = reduce_relay(r_qkv, dz_ctx)
    _after(tok_q)
    grad_x, st_n1 = _norm1_bwd(cts, xs, norm1_g, mods1, dz_ctx, dz_lat, dx1)

    res = {}

    def upd(nm, parts):
        wv, mv, vv = weights[nm], mom_m[nm], mom_v[nm]
        if wv.ndim == 1:
            wv, mv, vv = (a.reshape(1, -1) for a in (wv, mv, vv))
        outs = _adamw(parts, wv, mv, vv, "adamw_" + nm)
        res[nm] = [o_.reshape(weights[nm].shape) for o_ in outs]

    d_lat = jnp.concatenate([st_n1[0], st_n1[1], st_n2[3], st_n2[0], st_n2[1], st_fin[1]])
    d_cxt = jnp.concatenate([st_n1[3], st_n1[4], jnp.zeros((4 * d,), F32)])
    small = jnp.concatenate([d_lat, d_cxt, st_n1[2], st_q[0], st_kv[0], st_qb[0], st_kb[0], st_n2[2],
                             jnp.concatenate([dcb[0, 0], dcb[1, 0]]), st_fin[0]])
    n_small = small.shape[0]
    pad_small = (-n_small) % LANE
    (small_all,) = _all_gather([jnp.pad(small, (0, pad_small)).reshape(1, -1)], "gather_small")
    offs = {}
    o = 0
    for nm, ln in (("d_lat", 6 * d), ("d_cxt", 6 * d), ("norm1_g", d), ("mla_q_norm_g", ql), ("mla_kv_norm_g", kvl),
                   ("gqa_q_norm_g", GQA_HEAD_DIM), ("gqa_k_norm_g", GQA_HEAD_DIM), ("norm2_g", d), ("conv_b", f2),
                   ("final_norm_g", d)):
        offs[nm] = (o, ln)
        o += ln

    def part(nm):
        a, ln = offs[nm]
        return small_all[:, :, a:a + ln]

    d_lat_all = part("d_lat")[:, 0, :]
    d_cxt_sum = _sum_parts(part("d_cxt"))
    da16 = jnp.concatenate([d_lat_all, d_cxt_sum, jnp.zeros((7, 6 * d), F32)], axis=0)
    da16_shard = lax.dynamic_slice_in_dim(da16, my_idx * ncol, ncol, axis=1)
    cc_part = _cctx_partial(da16_shard, w_ada[0], c_ctx[None, :])
    (cc_all,) = _all_gather([cc_part], "gather_cctx")
    cc_parts = cc_all[:, 0:1, :]
    tok_i = reduce_relay(r_in, cc_all)

    _after(tok_i)
    for nm in ("norm1_g", "mla_q_norm_g", "mla_kv_norm_g", "gqa_q_norm_g", "gqa_k_norm_g", "norm2_g", "conv_b",
               "final_norm_g"):
        upd(nm, part(nm))
    upd("c_ctx", cc_parts)
    b_parts = jnp.concatenate([d_lat_all[:, None, :], d_cxt_sum[None]], axis=0)
    upd("b_ada", b_parts)
    _after(tok_i)
    outs = _adamw_ada(conds, da16_shard, w_ada[0], m_w_ada[0], v_w_ada[0])
    res["w_ada"] = [o_[None] for o_ in outs]
    last = outs[0]
    done = [last]
    for grp in (r_down, r_up, r_out, r_qkv, r_in):
        _after(*done)
        recv = reduce_finish(grp, last)
        for nm in grp["names"]:
            upd(nm, recv[nm][:, :3, :] if nm == "conv_w" else recv[nm])
            last = res[nm][0]
            done.append(last)

    return (loss, grad_x[None], *[res[n][0] for n in order], *[res[n][1] for n in order],
            *[res[n][2] for n in order], *[res[n][3] for n in order])
```

```python
import functools

import jax
import jax.numpy as jnp
from jax import lax
from jax.experimental import pallas as pl
from jax.experimental.pallas import tpu as pltpu

F32 = jnp.float32
BF16 = jnp.bfloat16

GRID_W = 64
ROPE_THETA = 10000.0
NORM_EPS = 1e-6
MLA_HEADS = 8
MLA_Q_LORA = 768
MLA_KV_LORA = 512
MLA_NOPE = 128
MLA_ROPE = 64
MLA_V = 128
GQA_HEADS = 8
GQA_KV_HEADS = 2
GQA_HEAD_DIM = 128
ADAM_LR = 0.001
ADAM_B1 = 0.9
ADAM_B2 = 0.999
ADAM_EPS = 1e-08
ADAM_WD = 0.01
ADAM_STEP = 10

N_DEV = 8
MESH_AXES = ("x", "y", "c")
LANE = 128
MLA_SLOT = 2 * LANE
VMEM_LIMIT = 56 * 1024 * 1024
ROW_BLOCK = 256
ATT_Q_BLOCK = 512
ATT_Q_BLOCK_FWD = 512
LN2 = 0.6931471805599453
LOG2E = 1.4426950408889634
MESH_ID = pl.DeviceIdType.MESH


def _tile(n, pref, align=LANE):
    if n <= pref:
        return n
    best = None
    t = align
    while t <= pref:
        if n % t == 0:
            best = t
        t += align
    assert best is not None, (n, pref, align)
    return best


def _cparams(sem=None):
    return pltpu.CompilerParams(dimension_semantics=sem, vmem_limit_bytes=VMEM_LIMIT)


_ORDER_AFTER = []


def _after(*arrays):
    _ORDER_AFTER.extend(arrays)


def _pcall(body, *, in_specs, **kw):
    deps = tuple(_ORDER_AFTER)
    _ORDER_AFTER.clear()
    if not deps:
        return pl.pallas_call(body, in_specs=in_specs, **kw)
    n_in, n_dep = len(in_specs), len(deps)

    def with_deps(*refs):
        body(*refs[:n_in], *refs[n_in + n_dep:])

    call = pl.pallas_call(with_deps, in_specs=list(in_specs) + [pl.BlockSpec(memory_space=pl.ANY)] * n_dep, **kw)
    return lambda *args: call(*args, *deps)


def _all_gather(arrs, name):
    n = len(arrs)

    def body(*refs):
        ins = refs[:n]
        outs = refs[n:2 * n]
        send_sems, recv_sems, local_sems = refs[2 * n:]
        x, y, c = lax.axis_index("x"), lax.axis_index("y"), lax.axis_index("c")
        me, sibling = (x, y, c), (x, y, 1 - c)
        chips = [(1 - x, y), (x, 1 - y), (1 - x, 1 - y)]

        def rows(a, dev):
            px, py, pc = dev
            return outs[a].at[4 * px + 2 * py + pc]

        def copy(a, k, block, to, src=None):
            return pltpu.make_async_remote_copy(
                src_ref=rows(a, block) if src is None else src,
                dst_ref=rows(a, block),
                send_sem=send_sems.at[7 * a + k],
                recv_sem=recv_sems.at[7 * a + k],
                device_id=to,
                device_id_type=MESH_ID,
            )

        mine = [pltpu.make_async_copy(ins[a], rows(a, me), local_sems.at[a]) for a in range(n)]
        for cp in mine:
            cp.start()
        first = []
        for a in range(n):
            first.append(copy(a, 0, me, sibling, src=ins[a]))
            first += [copy(a, 1 + j, me, (*chip, c), src=ins[a]) for j, chip in enumerate(chips)]
        for cp in first:
            cp.start()
        passed = []
        for j, chip in enumerate(chips):
            for a in range(n):
                copy(a, 1 + j, (*chip, c), me).wait_recv()
                fwd = copy(a, 4 + j, (*chip, c), sibling)
                fwd.start()
                passed.append(fwd)
        for a in range(n):
            copy(a, 0, sibling, me).wait_recv()
            for j, chip in enumerate(chips):
                copy(a, 4 + j, (*chip, 1 - c), me).wait_recv()
        for cp in first + passed:
            cp.wait_send()
        for cp in mine:
            cp.wait()

    any_spec = pl.BlockSpec(memory_space=pl.ANY)
    outs = _pcall(
        body,
        name=name,
        out_shape=[jax.ShapeDtypeStruct((N_DEV,) + a.shape, a.dtype) for a in arrs],
        in_specs=[any_spec] * n,
        out_specs=[any_spec] * n,
        scratch_shapes=[
            pltpu.SemaphoreType.DMA((7 * n,)),
            pltpu.SemaphoreType.DMA((7 * n,)),
            pltpu.SemaphoreType.DMA((n,)),
        ],
    )(*arrs)
    return list(outs)


def _all_to_all(arrs, name):
    n = len(arrs)

    def body(*refs):
        ins = refs[:n]
        outs = refs[n:2 * n]
        send_sems, recv_sems, local_sems = refs[2 * n:]
        x, y, c = lax.axis_index("x"), lax.axis_index("y"), lax.axis_index("c")
        my_idx = 4 * x + 2 * y + c

        def peer(k):
            fx, fy, fc = (k >> 2) & 1, (k >> 1) & 1, k & 1
            return (x ^ fx if fx else x, y ^ fy if fy else y, c ^ fc if fc else c)

        def copy(a, k):
            px, py, pc = peer(k)
            return pltpu.make_async_remote_copy(
                src_ref=ins[a].at[4 * px + 2 * py + pc],
                dst_ref=outs[a].at[my_idx],
                send_sem=send_sems.at[7 * a + k - 1],
                recv_sem=recv_sems.at[7 * a + k - 1],
                device_id=(px, py, pc),
                device_id_type=MESH_ID,
            )

        mine = [pltpu.make_async_copy(ins[a].at[my_idx], outs[a].at[my_idx], local_sems.at[a]) for a in range(n)]
        for cp in mine:
            cp.start()
        order = [1, 4, 2, 5, 3, 6, 7]
        cps = [copy(a, k) for k in order for a in range(n)]
        for cp in cps:
            cp.start()
        for cp in cps:
            cp.wait()
        for cp in mine:
            cp.wait()

    any_spec = pl.BlockSpec(memory_space=pl.ANY)
    outs = _pcall(
        body,
        name=name,
        out_shape=[jax.ShapeDtypeStruct(a.shape, a.dtype) for a in arrs],
        in_specs=[any_spec] * n,
        out_specs=[any_spec] * n,
        scratch_shapes=[
            pltpu.SemaphoreType.DMA((7 * n,)),
            pltpu.SemaphoreType.DMA((7 * n,)),
            pltpu.SemaphoreType.DMA((n,)),
        ],
    )(*arrs)
    return list(outs)


_HBM = pl.BlockSpec(memory_space=pltpu.HBM)
_SEM = pl.BlockSpec(memory_space=pltpu.SEMAPHORE)
_EFFECT = pltpu.SideEffectType.DATAFLOW_SIDE_EFFECTING


def _descriptors(copies, send_sems, recv_sems):
    descs = []
    for i, (src, dst, dev) in enumerate(copies):
        if dev is None:
            descs.append(pltpu.make_async_copy(src, dst, recv_sems.at[i]))
        else:
            descs.append(pltpu.make_async_remote_copy(src_ref=src, dst_ref=dst, send_sem=send_sems.at[i],
                                                      recv_sem=recv_sems.at[i], device_id=dev, device_id_type=MESH_ID))
    return descs


def _split_start(name, arrays, copies_fn, n_copies):
    n = len(arrays)

    def body(*refs):
        send_sems, recv_sems = refs[n], refs[n + 1]
        token = refs[2 * n + 2]
        for dsc in _descriptors(copies_fn(refs[:n]), send_sems, recv_sems):
            dsc.start()
        token[...] = jnp.zeros_like(token)

    outs = _pcall(
        body,
        name=name,
        out_shape=(pltpu.SemaphoreType.DMA((n_copies,)), pltpu.SemaphoreType.DMA((n_copies,)),
                   *[pltpu.HBM(a.shape, a.dtype) for a in arrays], jax.ShapeDtypeStruct((8, LANE), F32)),
        in_specs=[_HBM] * n,
        out_specs=(_SEM, _SEM, *[_HBM] * n, pl.BlockSpec(memory_space=pltpu.VMEM)),
        input_output_aliases={i: 2 + i for i in range(n)},
        compiler_params=pltpu.CompilerParams(has_side_effects=_EFFECT),
    )(*[pltpu.with_memory_space_constraint(a, pltpu.HBM) for a in arrays])
    return outs[0], outs[1], list(outs[2:2 + n]), outs[2 + n]


def _split_wait(name, send_sems, recv_sems, arrays, copies_fn, after):
    n = len(arrays)

    def body(*refs):
        for dsc, (_, _, dev) in zip(_descriptors(copies_fn(refs[:n]), refs[n], refs[n + 1]), copies_fn(refs[:n])):
            if dev is None:
                dsc.wait()
            else:
                dsc.wait_send()
                dsc.wait_recv()

    outs = _pcall(
        body,
        name=name,
        out_shape=tuple(pltpu.HBM(a.shape, a.dtype) for a in arrays),
        in_specs=[_HBM] * n + [_SEM, _SEM, pl.BlockSpec(memory_space=pl.ANY)],
        out_specs=tuple([_HBM] * n),
        input_output_aliases={i: i for i in range(n)},
        compiler_params=pltpu.CompilerParams(has_side_effects=_EFFECT),
    )(*arrays, send_sems, recv_sems, after)
    return list(outs)


def _mesh_pos():
    x, y, c = lax.axis_index("x"), lax.axis_index("y"), lax.axis_index("c")
    return x, y, c, [(1 - x, y), (x, 1 - y), (1 - x, 1 - y)]


def _gather_ici_copies(n):
    def copies(refs):
        x, y, c, chips = _mesh_pos()
        me = 4 * x + 2 * y + c
        out = []
        for a in range(n):
            src, buf = refs[a], refs[n + a]
            out.append((src, buf.at[me], None))
            out.append((src, buf.at[me], (x, y, 1 - c)))
            out += [(src, buf.at[me], (cx, cy, c)) for cx, cy in chips[:2]]
        return out
    return copies


def _gather_pass_copies(n):
    def copies(refs):
        x, y, c, chips = _mesh_pos()
        south = c == 0
        bx, by = jnp.where(south, 1 - x, x), jnp.where(south, y, 1 - y)
        tx, ty = jnp.where(south, x, 1 - x), jnp.where(south, 1 - y, y)
        out = []
        for a in range(n):
            rows = refs[a].at[4 * bx + 2 * by + c]
            out.append((rows, rows, (tx, ty, c)))
            for cx, cy in chips[:2]:
                rows = refs[a].at[4 * cx + 2 * cy + c]
                out.append((rows, rows, (x, y, 1 - c)))
        return out
    return copies


def _gather_d2d_copies(n):
    def copies(refs):
        x, y, c, chips = _mesh_pos()
        cx, cy = chips[2]
        out = []
        for a in range(n):
            rows = refs[a].at[4 * cx + 2 * cy + c]
            out.append((rows, rows, (x, y, 1 - c)))
        return out
    return copies


def _reduce_d2d_copies(n):
    def copies(refs):
        x, y, c, _ = _mesh_pos()
        out = []
        for a in range(n):
            for k in range(4):
                out.append((refs[a].at[2 * k + (1 - c)], refs[n + a].at[k], (x, y, 1 - c)))
        return out
    return copies


def _reduce_ici_copies(n):
    def copies(refs):
        x, y, c, chips = _mesh_pos()
        mine = 2 * x + y
        out = []
        for a in range(n):
            src, land = refs[a], refs[n + a]
            out.append((src.at[mine], land.at[mine], None))
            out += [(src.at[2 * cx + cy], land.at[mine], (cx, cy, c)) for cx, cy in chips]
        return out
    return copies


def _pair_sum(send, land, c_idx, name):
    _, r, cols = send.shape
    rb = _tile(r, max(8, (1 << 22) // (send.dtype.itemsize * cols) // 8 * 8), 8)
    dt = send.dtype

    def body(c_ref, s_ref, l_ref, o_ref):
        o_ref[...] = (s_ref[...].astype(F32) + l_ref[...].astype(F32)).astype(dt)

    return pl.pallas_call(
        body,
        name=name,
        out_shape=jax.ShapeDtypeStruct((4, r, cols), dt),
        grid_spec=pltpu.PrefetchScalarGridSpec(
            num_scalar_prefetch=1,
            grid=(4, r // rb),
            in_specs=[pl.BlockSpec((None, rb, cols), lambda k, i, c_ref: (2 * k + c_ref[0], i, 0)),
                      pl.BlockSpec((None, rb, cols), lambda k, i, c_ref: (k, i, 0))],
            out_specs=pl.BlockSpec((None, rb, cols), lambda k, i, c_ref: (k, i, 0)),
        ),
        compiler_params=_cparams(("parallel", "parallel")),
    )(c_idx, send, land)


_DIMS = {
    "nn": (((1,), (0,)), ((), ())),
    "nt": (((1,), (1,)), ((), ())),
    "tn": (((0,), (0,)), ((), ())),
}


def _mm_call(a, b, *, mode, grid, a_spec, b_spec, o_spec, out_shape, acc_shape, name):
    nk = grid[2]
    out_dtype = out_shape.dtype

    def body(a_ref, b_ref, o_ref, *scratch):
        p = lax.dot_general(a_ref[...].astype(BF16), b_ref[...].astype(BF16), _DIMS[mode],
                            preferred_element_type=F32)
        if nk == 1:
            o_ref[...] = p.astype(out_dtype)
        else:
            acc = scratch[0]
            k = pl.program_id(2)

            @pl.when(k == 0)
            def _():
                acc[...] = p

            @pl.when(k > 0)
            def _():
                acc[...] += p

            @pl.when(k == nk - 1)
            def _():
                o_ref[...] = acc[...].astype(out_dtype)

    return _pcall(
        body,
        name=name,
        out_shape=out_shape,
        grid=grid,
        in_specs=[a_spec, b_spec],
        out_specs=o_spec,
        scratch_shapes=[pltpu.VMEM(acc_shape, F32)] if nk > 1 else [],
        compiler_params=_cparams(("parallel", "parallel", "arbitrary")),
    )(a, b)


def _mm(a, b, mode, out_dtype, name, tm=512, tn=512, tk=2432, a_row_off=0, rows=None):
    if mode == "nn":
        (m, k), (k2, n) = a.shape, b.shape
    elif mode == "nt":
        (m, k), (n, k2) = a.shape, b.shape
    else:
        (k, m), (k2, n) = a.shape, b.shape
        if rows is not None:
            k = k2 = rows
    assert k == k2, (a.shape, b.shape, mode)
    if mode != "tn":
        m = (m if rows is None else rows + a_row_off) - a_row_off
    tm, tn, tk = _tile(m, tm, 8), _tile(n, tn), _tile(k, tk, 8 if mode == "tn" else LANE)
    assert a_row_off % tm == 0
    ro = a_row_off // tm
    grid = (m // tm, n // tn, k // tk)
    if mode == "tn":
        a_spec = pl.BlockSpec((tk, tm), lambda i, j, kk: (kk, i))
    else:
        a_spec = pl.BlockSpec((tm, tk), lambda i, j, kk: (i + ro, kk))
    if mode == "nt":
        b_spec = pl.BlockSpec((tn, tk), lambda i, j, kk: (j, kk))
    else:
        b_spec = pl.BlockSpec((tk, tn), lambda i, j, kk: (kk, j))
    o_spec = pl.BlockSpec((tm, tn), lambda i, j, kk: (i, j))
    return _mm_call(a, b, mode=mode, grid=grid, a_spec=a_spec, b_spec=b_spec, o_spec=o_spec,
                    out_shape=jax.ShapeDtypeStruct((m, n), out_dtype), acc_shape=(tm, tn), name=name)


def _mm_cat_nt(pieces, out_dtype, name, tm=1024, tn=1024, tk=2048, rows=None):
    m = pieces[0][0].shape[0] if rows is None else rows
    n = pieces[0][1].shape[0]
    tm, tn = _tile(m, tm, 8), _tile(n, tn)
    steps, starts, s = [], [], 0
    for a, b, off in pieces:
        kp = a.shape[1]
        tkp = _tile(kp, tk)
        assert off % tkp == 0 and b.shape[0] == n
        steps.append((tkp, kp // tkp, off // tkp))
        starts.append(s)
        s += kp // tkp
    nk = s
    npc = len(pieces)

    def body(*refs):
        o_ref, acc = refs[2 * npc], refs[2 * npc + 1]
        kk = pl.program_id(2)

        @pl.when(kk == 0)
        def _():
            acc[...] = jnp.zeros_like(acc)

        for p in range(npc):
            @pl.when((kk >= starts[p]) & (kk < starts[p] + steps[p][1]))
            def _(p=p):
                acc[...] += lax.dot_general(refs[2 * p][...].astype(BF16), refs[2 * p + 1][...].astype(BF16), _DIMS["nt"],
                                            preferred_element_type=F32)

        @pl.when(kk == nk - 1)
        def _():
            o_ref[...] = acc[...].astype(out_dtype)

    in_specs, args = [], []
    for p, (a, b, off) in enumerate(pieces):
        tkp, np_, ob = steps[p]

        def rel(kk, p=p, np_=np_):
            return jnp.clip(kk - starts[p], 0, np_ - 1)

        in_specs.append(pl.BlockSpec((tm, tkp), lambda i, j, kk, rel=rel: (i, rel(kk))))
        in_specs.append(pl.BlockSpec((tn, tkp), lambda i, j, kk, rel=rel, ob=ob: (j, ob + rel(kk))))
        args += [a, b]
    return _pcall(
        body,
        name=name,
        out_shape=jax.ShapeDtypeStruct((m, n), out_dtype),
        grid=(m // tm, n // tn, nk),
        in_specs=in_specs,
        out_specs=pl.BlockSpec((tm, tn), lambda i, j, kk: (i, j)),
        scratch_shapes=[pltpu.VMEM((tm, tn), F32)],
        compiler_params=_cparams(("parallel", "parallel", "arbitrary")),
    )(*args)


def _mm_cat_tn(a, pieces, out_dtype, name, tm=1024, tn=1024, rows=None):
    k = a.shape[0] if rows is None else rows
    m = a.shape[1]
    tm = _tile(m, tm)
    starts, s = [], 0
    for b in pieces:
        assert b.shape[1] % tn == 0
        starts.append(s)
        s += b.shape[1] // tn
    nj = s
    npc = len(pieces)

    def body(*refs):
        a_ref, o_ref = refs[0], refs[1 + npc]
        j = pl.program_id(1)
        for p in range(npc):
            @pl.when((j >= starts[p]) & (j < starts[p] + pieces[p].shape[1] // tn))
            def _(p=p):
                o_ref[...] = lax.dot_general(a_ref[...].astype(BF16), refs[1 + p][...].astype(BF16), _DIMS["tn"],
                                             preferred_element_type=F32).astype(out_dtype)

    in_specs = [pl.BlockSpec((k, tm), lambda i, j: (0, i))]
    for p, b in enumerate(pieces):
        np_ = b.shape[1] // tn
        in_specs.append(pl.BlockSpec((k, tn), lambda i, j, p=p, np_=np_: (0, jnp.clip(j - starts[p], 0, np_ - 1))))
    return _pcall(
        body,
        name=name,
        out_shape=jax.ShapeDtypeStruct((m, nj * tn), out_dtype),
        grid=(m // tm, nj),
        in_specs=in_specs,
        out_specs=pl.BlockSpec((tm, tn), lambda i, j: (i, j)),
        compiler_params=_cparams(("parallel", "arbitrary")),
    )(a, *pieces)


def _mm_up_fwd(z2, w3, name, tm=1024):
    t, d = z2.shape
    nsh, _, c = w3.shape
    tm = _tile(t, tm, 8)
    return _mm_call(z2, w3, mode="nn", grid=(t // tm, nsh, 1),
                    a_spec=pl.BlockSpec((tm, d), lambda i, j, kk: (i, 0)),
                    b_spec=pl.BlockSpec((None, d, c), lambda i, j, kk: (j, 0, 0)),
                    o_spec=pl.BlockSpec((tm, c), lambda i, j, kk: (i, j)),
                    out_shape=jax.ShapeDtypeStruct((t, nsh * c), BF16), acc_shape=(tm, c), name=name)


def _mm_up_dz(du3, w3, name, tm=512, tn=1024):
    _, t, f = du3.shape
    nsh, d, c = w3.shape
    half = nsh // 2
    assert f == half * c
    tm, tn = _tile(t, tm, 8), _tile(d, tn)

    def body(a_ref, b_ref, o_ref, acc):
        kk = pl.program_id(2)
        p = None
        for s in range(half):
            q = lax.dot_general(a_ref[:, s * c:(s + 1) * c], b_ref[s], _DIMS["nt"], preferred_element_type=F32)
            p = q if p is None else p + q

        @pl.when(kk == 0)
        def _():
            acc[...] = p

        @pl.when(kk == 1)
        def _():
            o_ref[...] = (acc[...] + p).astype(BF16)

    return _pcall(
        body,
        name=name,
        out_shape=jax.ShapeDtypeStruct((t, d), BF16),
        grid=(t // tm, d // tn, 2),
        in_specs=[pl.BlockSpec((None, tm, f), lambda i, j, kk: (kk, i, 0)),
                  pl.BlockSpec((half, tn, c), lambda i, j, kk: (kk, j, 0))],
        out_specs=pl.BlockSpec((tm, tn), lambda i, j, kk: (i, j)),
        scratch_shapes=[pltpu.VMEM((tm, tn), F32)],
        compiler_params=_cparams(("parallel", "parallel", "arbitrary")),
    )(du3, w3)


def _mm_sum_nt(pieces, out_dtype, name, tm=512, tn=512, rows=None):
    m = pieces[0][0].shape[0] if rows is None else rows
    n = pieces[0][2].shape[0]
    tm, tn = _tile(m, tm, 8), _tile(n, tn)
    npc = len(pieces)

    def body(*refs):
        p = None
        for s in range(npc):
            q = lax.dot_general(refs[2 * s][...].astype(BF16), refs[2 * s + 1][...].astype(BF16), _DIMS["nt"],
                                preferred_element_type=F32)
            p = q if p is None else p + q
        refs[2 * npc][...] = p.astype(out_dtype)

    in_specs, args = [], []
    for a, ao, b, bo, kp in pieces:
        assert ao % kp == 0 and bo % kp == 0 and b.shape[0] == n
        in_specs.append(pl.BlockSpec((tm, kp), lambda i, j, ab=ao // kp: (i, ab)))
        in_specs.append(pl.BlockSpec((tn, kp), lambda i, j, bb=bo // kp: (j, bb)))
        args += [a, b]
    return _pcall(
        body,
        name=name,
        out_shape=jax.ShapeDtypeStruct((m, n), out_dtype),
        grid=(m // tm, n // tn),
        in_specs=in_specs,
        out_specs=pl.BlockSpec((tm, tn), lambda i, j: (i, j)),
        compiler_params=_cparams(("parallel", "parallel")),
    )(*args)


def _mm_up_gw(z2, du3, nsh, name, tm=1024):
    t, d = z2.shape
    f = du3.shape[2]
    half = nsh // 2
    c = f // half
    tm = _tile(d, tm)
    return _mm_call(z2, du3, mode="tn", grid=(d // tm, nsh, 1),
                    a_spec=pl.BlockSpec((t, tm), lambda i, j, kk: (0, i)),
                    b_spec=pl.BlockSpec((None, t, c), lambda i, j, kk: (j // half, 0, j % half)),
                    o_spec=pl.BlockSpec((None, tm, c), lambda i, j, kk: (j, i, 0)),
                    out_shape=jax.ShapeDtypeStruct((nsh, d, c), BF16), acc_shape=(tm, c), name=name)


def _rms(x):
    r = lax.rsqrt(jnp.mean(x * x, axis=-1, keepdims=True) + NORM_EPS)
    return x * r, r


def _rms_bwd(dxh, xh, r):
    return r * (dxh - xh * jnp.mean(dxh * xh, axis=-1, keepdims=True))


def _colsum(v):
    return jnp.sum(v, axis=0, keepdims=True)


def _rope(v, c, s1, s2, q):
    w = v.shape[-1]
    return v * c + pltpu.roll(v, w - q, 1) * s1 + pltpu.roll(v, q, 1) * s2


def _rope_t(d, c, s1, s2, q):
    w = d.shape[-1]
    return d * c + pltpu.roll(d * s1, q, 1) + pltpu.roll(d * s2, w - q, 1)


def _norm_mod_fwd(ctx, x, gain, mods):
    tc, d = ctx.shape
    t = x.shape[0]
    rb = min(ROW_BLOCK, tc)
    nbl = t // rb

    def body(ctx_ref, x_ref, g_ref, mod_ref, z_ref):
        i = pl.program_id(0)

        def emit(src, sh, sc):
            xh, _ = _rms(src[...])
            z_ref[...] = ((xh * g_ref[...]) * (1.0 + sc) + sh).astype(BF16)

        @pl.when(i >= nbl)
        def _():
            emit(ctx_ref, mod_ref[2:3, :], mod_ref[3:4, :])

        @pl.when(i < nbl)
        def _():
            emit(x_ref, mod_ref[0:1, :], mod_ref[1:2, :])

    return _pcall(
        body,
        name="norm1_mod_fwd",
        out_shape=jax.ShapeDtypeStruct((tc + t, d), BF16),
        grid=((tc + t) // rb,),
        in_specs=[
            pl.BlockSpec((rb, d), lambda i: (jnp.maximum(i - nbl, 0), 0)),
            pl.BlockSpec((rb, d), lambda i: (jnp.minimum(i, nbl - 1), 0)),
            pl.BlockSpec((1, d), lambda i: (0, 0)),
            pl.BlockSpec((8, d), lambda i: (0, 0)),
        ],
        out_specs=pl.BlockSpec((rb, d), lambda i: (i, 0)),
        compiler_params=_cparams(("arbitrary",)),
    )(ctx, x, gain, mods)


def _norm1_bwd(ctx, x, gain, mods, dz_ctx, dz_lat, dx1):
    tc, d = ctx.shape
    t = x.shape[0]
    rb = min(ROW_BLOCK, tc)
    nbl = t // rb

    def body(ctx_ref, x_ref, g_ref, mod_ref, dzc_ref, dzl_ref, dx1_ref, gx_ref, st_ref):
        i = pl.program_id(0)

        @pl.when(i == 0)
        def _():
            st_ref[...] = jnp.zeros_like(st_ref)

        def common(src, dz, sc, row_sh, row_sc):
            xh, r = _rms(src[...])
            g = g_ref[...]
            dxn = dz * (1.0 + sc)
            st_ref[row_sh:row_sh + 1, :] += _colsum(dz)
            st_ref[row_sc:row_sc + 1, :] += _colsum(dz * (xh * g))
            st_ref[2:3, :] += _colsum(dxn * xh)
            return _rms_bwd(dxn * g, xh, r)

        @pl.when(i >= nbl)
        def _():
            common(ctx_ref, dzc_ref[...], mod_ref[3:4, :], 3, 4)

        @pl.when(i < nbl)
        def _():
            gx_ref[...] = dx1_ref[...] + common(x_ref, dzl_ref[...], mod_ref[1:2, :], 0, 1)

    lat = lambda i: (jnp.minimum(i, nbl - 1), 0)
    cix = lambda i: (jnp.maximum(i - nbl, 0), 0)
    return _pcall(
        body,
        name="norm1_mod_bwd",
        out_shape=[jax.ShapeDtypeStruct((t, d), F32), jax.ShapeDtypeStruct((8, d), F32)],
        grid=((tc + t) // rb,),
        in_specs=[
            pl.BlockSpec((rb, d), cix),
            pl.BlockSpec((rb, d), lat),
            pl.BlockSpec((1, d), lambda i: (0, 0)),
            pl.BlockSpec((8, d), lambda i: (0, 0)),
            pl.BlockSpec((rb, d), cix),
            pl.BlockSpec((rb, d), lat),
            pl.BlockSpec((rb, d), lat),
        ],
        out_specs=[pl.BlockSpec((rb, d), lat), pl.BlockSpec((8, d), lambda i: (0, 0))],
        compiler_params=_cparams(("arbitrary",)),
    )(ctx, x, gain, mods, dz_ctx, dz_lat, dx1)


def _key_prep_fwd(kv, kv_gain, kb_gain, tabs):
    ta, wkv = kv.shape
    kvl = MLA_KV_LORA
    nb = GQA_KV_HEADS * GQA_HEAD_DIM
    rb = ROW_BLOCK if ta % ROW_BLOCK == 0 else LANE
    hd = GQA_HEAD_DIM

    def body(kv_ref, g_ref, gb_ref, ca, s1a, s2a, cb, s1b, s2b, kin_ref, kb_ref, vb_ref):
        xh, _ = _rms(kv_ref[:, 0:kvl])
        kin_ref[:, 0:kvl] = (xh * g_ref[...]).astype(BF16)
        kpe = kv_ref[:, kvl + 2 * nb:kvl + 2 * nb + LANE]
        kin_ref[:, kvl:kvl + LANE] = _rope(kpe, ca[...], s1a[...], s2a[...], MLA_ROPE // 4).astype(BF16)
        for h in range(GQA_KV_HEADS):
            nh, _ = _rms(kv_ref[:, kvl + h * hd:kvl + (h + 1) * hd])
            kb_ref[:, h * hd:(h + 1) * hd] = _rope(nh * gb_ref[...], cb[...], s1b[...], s2b[...], hd // 4).astype(BF16)
        vb_ref[...] = kv_ref[:, kvl + nb:kvl + 2 * nb].astype(BF16)

    row = lambda w: pl.BlockSpec((rb, w), lambda i: (i, 0))
    fix = lambda w: pl.BlockSpec((1, w), lambda i: (0, 0))
    return _pcall(
        body,
        name="key_prep_fwd",
        out_shape=[jax.ShapeDtypeStruct((ta, kvl + LANE), BF16), jax.ShapeDtypeStruct((ta, nb), BF16),
                   jax.ShapeDtypeStruct((ta, nb), BF16)],
        grid=(ta // rb,),
        in_specs=[row(wkv), fix(kvl), fix(hd)] + [row(LANE)] * 3 + [row(hd)] * 3,
        out_specs=[row(kvl + LANE), row(nb), row(nb)],
        compiler_params=_cparams(("parallel",)),
    )(kv, kv_gain, kb_gain, *tabs)


def _key_prep_bwd(kv, kv_gain, kb_gain, tabs, dkin, dkb, dvb):
    ta, wkv = kv.shape
    kvl = MLA_KV_LORA
    nb = GQA_KV_HEADS * GQA_HEAD_DIM
    rb = ROW_BLOCK if ta % ROW_BLOCK == 0 else LANE
    hd = GQA_HEAD_DIM

    def body(kv_ref, g_ref, gb_ref, ca, s1a, s2a, cb, s1b, s2b, dkin_ref, dkb_ref, dvb_ref, dkv_ref, st_ref, stb_ref):
        @pl.when(pl.program_id(0) == 0)
        def _():
            st_ref[...] = jnp.zeros_like(st_ref)
            stb_ref[...] = jnp.zeros_like(stb_ref)

        xh, r = _rms(kv_ref[:, 0:kvl])
        dn = dkin_ref[:, 0:kvl]
        st_ref[0:1, :] += _colsum(dn * xh)
        dkv_ref[:, 0:kvl] = _rms_bwd(dn * g_ref[...], xh, r).astype(BF16)
        dpe = _rope_t(dkin_ref[:, kvl:kvl + LANE], ca[...], s1a[...], s2a[...], MLA_ROPE // 4)
        dkv_ref[:, kvl + 2 * nb:kvl + 2 * nb + LANE] = dpe.astype(BF16)
        for h in range(GQA_KV_HEADS):
            nh, rh = _rms(kv_ref[:, kvl + h * hd:kvl + (h + 1) * hd])
            dn_h = _rope_t(dkb_ref[:, h * hd:(h + 1) * hd], cb[...], s1b[...], s2b[...], hd // 4)
            stb_ref[0:1, :] += _colsum(dn_h * nh)
            dkv_ref[:, kvl + h * hd:kvl + (h + 1) * hd] = _rms_bwd(dn_h * gb_ref[...], nh, rh).astype(BF16)
        dkv_ref[:, kvl + nb:kvl + 2 * nb] = dvb_ref[...].astype(BF16)

    row = lambda w: pl.BlockSpec((rb, w), lambda i: (i, 0))
    fix = lambda w: pl.BlockSpec((1, w), lambda i: (0, 0))
    return _pcall(
        body,
        name="key_prep_bwd",
        out_shape=[jax.ShapeDtypeStruct((ta, wkv), BF16), jax.ShapeDtypeStruct((8, kvl), F32),
                   jax.ShapeDtypeStruct((8, hd), F32)],
        grid=(ta // rb,),
        in_specs=[row(wkv), fix(kvl), fix(hd)] + [row(LANE)] * 3 + [row(hd)] * 3 + [row(kvl + LANE), row(nb), row(nb)],
        out_specs=[row(wkv), pl.BlockSpec((8, kvl), lambda i: (0, 0)), pl.BlockSpec((8, hd), lambda i: (0, 0))],
        compiler_params=_cparams(("arbitrary",)),
    )(kv, kv_gain, kb_gain, *tabs, dkin, dkb, dvb)


def _q_prep_fwd(qg, q_gain, qb_gain, tabs, qscale):
    t = qg.shape[0]
    ql = MLA_Q_LORA
    hd = GQA_HEAD_DIM
    hb = GQA_HEADS * hd
    rb = min(ROW_BLOCK, t)

    def body(q_ref, g_ref, gb_ref, cb, s1b, s2b, cqn_ref, qb_ref):
        xh, _ = _rms(q_ref[:, 0:ql])
        cqn_ref[...] = (xh * g_ref[...]).astype(BF16)
        for h in range(GQA_HEADS):
            nh, _ = _rms(q_ref[:, ql + h * hd:ql + (h + 1) * hd])
            qh = _rope(nh * gb_ref[...], cb[...], s1b[...], s2b[...], hd // 4)
            qb_ref[:, h * hd:(h + 1) * hd] = (qh * qscale).astype(BF16)

    row = lambda w: pl.BlockSpec((rb, w), lambda i: (i, 0))
    fix = lambda w: pl.BlockSpec((1, w), lambda i: (0, 0))
    return _pcall(
        body,
        name="q_prep_fwd",
        out_shape=[jax.ShapeDtypeStruct((t, ql), BF16), jax.ShapeDtypeStruct((t, hb), BF16)],
        grid=(t // rb,),
        in_specs=[row(ql + hb), fix(ql), fix(hd)] + [row(hd)] * 3,
        out_specs=[row(ql), row(hb)],
        compiler_params=_cparams(("parallel",)),
    )(qg, q_gain, qb_gain, *tabs)


def _q_prep_bwd(qg, q_gain, qb_gain, tabs, dcqn, dqb, wpad, qscale):
    t = qg.shape[0]
    ql = MLA_Q_LORA
    hd = GQA_HEAD_DIM
    hb = GQA_HEADS * hd
    rb = min(ROW_BLOCK, t)

    def body(q_ref, g_ref, gb_ref, cb, s1b, s2b, dcqn_ref, dqb_ref, dq_ref, st_ref, stb_ref):
        @pl.when(pl.program_id(0) == 0)
        def _():
            st_ref[...] = jnp.zeros_like(st_ref)
            stb_ref[...] = jnp.zeros_like(stb_ref)

        xh, r = _rms(q_ref[:, 0:ql])
        dn = dcqn_ref[...]
        st_ref[0:1, :] += _colsum(dn * xh)
        dq_ref[:, 0:ql] = _rms_bwd(dn * g_ref[...], xh, r).astype(BF16)
        for h in range(GQA_HEADS):
            nh, rh = _rms(q_ref[:, ql + h * hd:ql + (h + 1) * hd])
            dn_h = _rope_t(dqb_ref[:, h * hd:(h + 1) * hd] * qscale, cb[...], s1b[...], s2b[...], hd // 4)
            stb_ref[0:1, :] += _colsum(dn_h * nh)
            dq_ref[:, ql + h * hd:ql + (h + 1) * hd] = _rms_bwd(dn_h * gb_ref[...], nh, rh).astype(BF16)
        if wpad:
            dq_ref[:, ql + hb:ql + hb + wpad] = jnp.zeros((rb, wpad), BF16)

    row = lambda w: pl.BlockSpec((rb, w), lambda i: (i, 0))
    fix = lambda w: pl.BlockSpec((1, w), lambda i: (0, 0))
    return _pcall(
        body,
        name="q_prep_bwd",
        out_shape=[jax.ShapeDtypeStruct((t, ql + hb + wpad), BF16), jax.ShapeDtypeStruct((8, ql), F32),
                   jax.ShapeDtypeStruct((8, hd), F32)],
        grid=(t // rb,),
        in_specs=[row(ql + hb), fix(ql), fix(hd)] + [row(hd)] * 3 + [row(ql), row(hb)],
        out_specs=[row(ql + hb + wpad), pl.BlockSpec((8, ql), lambda i: (0, 0)), pl.BlockSpec((8, hd), lambda i: (0, 0))],
        compiler_params=_cparams(("arbitrary",)),
    )(qg, q_gain, qb_gain, *tabs, dcqn, dqb)


def _rope_a(v, tabs, transpose, out_dtype, name, qscale):
    t, w = v.shape
    rb = min(ROW_BLOCK, t)
    fn = _rope_t if transpose else _rope

    def body(v_ref, c, s1, s2, o_ref):
        for h in range(w // MLA_SLOT):
            sl = slice(h * MLA_SLOT, (h + 1) * MLA_SLOT)
            o_ref[:, sl] = (fn(v_ref[:, sl].astype(F32), c[...], s1[...], s2[...], MLA_ROPE // 4) * qscale).astype(out_dtype)

    row = lambda ww: pl.BlockSpec((rb, ww), lambda i: (i, 0))
    return _pcall(
        body,
        name=name,
        out_shape=jax.ShapeDtypeStruct((t, w), out_dtype),
        grid=(t // rb,),
        in_specs=[row(w)] + [row(MLA_SLOT)] * 3,
        out_specs=row(w),
        compiler_params=_cparams(("parallel",)),
    )(v, *tabs)


def _merge_fwd(pa, pb, qg, gate_blk):
    t, d = pa.shape
    rb = min(ROW_BLOCK, t)

    def body(pa_ref, pb_ref, ga_ref, gb_ref, o_ref):
        o_ref[...] = (jax.nn.sigmoid(ga_ref[...]) * pa_ref[...].astype(F32)
                      + jax.nn.sigmoid(gb_ref[...]) * pb_ref[...].astype(F32)).astype(BF16)

    row = pl.BlockSpec((rb, d), lambda i: (i, 0))
    return _pcall(
        body,
        name="merge_fwd",
        out_shape=jax.ShapeDtypeStruct((t, d), BF16),
        grid=(t // rb,),
        in_specs=[row, row, pl.BlockSpec((rb, d), lambda i: (i, gate_blk)), pl.BlockSpec((rb, d), lambda i: (i, gate_blk + 1))],
        out_specs=row,
        compiler_params=_cparams(("parallel",)),
    )(pa, pb, qg, qg)


def _merge_bwd(dm, pa, pb, qg, gate_blk):
    t, d = pa.shape
    rb = min(ROW_BLOCK, t)

    def body(dm_ref, pa_ref, pb_ref, ga_ref, gb_ref, dpa_ref, dpb_ref, dg_ref):
        dmv = dm_ref[...].astype(F32)
        sa = jax.nn.sigmoid(ga_ref[...])
        sb = jax.nn.sigmoid(gb_ref[...])
        dpa_ref[...] = (dmv * sa).astype(BF16)
        dpb_ref[...] = (dmv * sb).astype(BF16)
        dg_ref[:, 0:d] = (dmv * pa_ref[...].astype(F32) * (sa * (1.0 - sa))).astype(BF16)
        dg_ref[:, d:2 * d] = (dmv * pb_ref[...].astype(F32) * (sb * (1.0 - sb))).astype(BF16)

    row = pl.BlockSpec((rb, d), lambda i: (i, 0))
    return _pcall(
        body,
        name="merge_bwd",
        out_shape=[jax.ShapeDtypeStruct((t, d), BF16), jax.ShapeDtypeStruct((t, d), BF16),
                   jax.ShapeDtypeStruct((t, 2 * d), BF16)],
        grid=(t // rb,),
        in_specs=[row, row, row, pl.BlockSpec((rb, d), lambda i: (i, gate_blk)), pl.BlockSpec((rb, d), lambda i: (i, gate_blk + 1))],
        out_specs=[row, row, pl.BlockSpec((rb, 2 * d), lambda i: (i, 0))],
        compiler_params=_cparams(("parallel",)),
    )(dm, pa, pb, qg, qg)


def _resid_norm_mod(x, branch, gain, mods, name):
    t, d = x.shape
    rb = min(ROW_BLOCK, t)

    def body(x_ref, b_ref, g_ref, mod_ref, x1_ref, z_ref):
        x1 = x_ref[...] + mod_ref[0:1, :] * b_ref[...]
        x1_ref[...] = x1
        xh, _ = _rms(x1)
        z_ref[...] = ((xh * g_ref[...]) * (1.0 + mod_ref[2:3, :]) + mod_ref[1:2, :]).astype(BF16)

    row = pl.BlockSpec((rb, d), lambda i: (i, 0))
    return _pcall(
        body,
        name=name,
        out_shape=[jax.ShapeDtypeStruct((t, d), F32), jax.ShapeDtypeStruct((t, d), BF16)],
        grid=(t // rb,),
        in_specs=[row, row, pl.BlockSpec((1, d), lambda i: (0, 0)), pl.BlockSpec((8, d), lambda i: (0, 0))],
        out_specs=[row, row],
        compiler_params=_cparams(("parallel",)),
    )(x, branch, gain, mods)


def _norm2_bwd(x1, attn, gain, mods, dz2, dx2):
    t, d = x1.shape
    rb = min(ROW_BLOCK, t)

    def body(x1_ref, at_ref, g_ref, mod_ref, dz_ref, dx2_ref, dx1_ref, da_ref, st_ref):
        @pl.when(pl.program_id(0) == 0)
        def _():
            st_ref[...] = jnp.zeros_like(st_ref)

        xh, r = _rms(x1_ref[...])
        g = g_ref[...]
        dz = dz_ref[...].astype(F32)
        dxn = dz * (1.0 + mod_ref[1:2, :])
        st_ref[0:1, :] += _colsum(dz)
        st_ref[1:2, :] += _colsum(dz * (xh * g))
        st_ref[2:3, :] += _colsum(dxn * xh)
        dx1 = dx2_ref[...] + _rms_bwd(dxn * g, xh, r)
        dx1_ref[...] = dx1
        st_ref[3:4, :] += _colsum(dx1 * at_ref[...])
        da_ref[...] = (dx1 * mod_ref[0:1, :]).astype(BF16)

    row = pl.BlockSpec((rb, d), lambda i: (i, 0))
    return _pcall(
        body,
        name="norm2_mod_bwd",
        out_shape=[jax.ShapeDtypeStruct((t, d), F32), jax.ShapeDtypeStruct((t, d), BF16), jax.ShapeDtypeStruct((8, d), F32)],
        grid=(t // rb,),
        in_specs=[row, row, pl.BlockSpec((1, d), lambda i: (0, 0)), pl.BlockSpec((8, d), lambda i: (0, 0)), row, row],
        out_specs=[row, row, pl.BlockSpec((8, d), lambda i: (0, 0))],
        compiler_params=_cparams(("arbitrary",)),
    )(x1, attn, gain, mods, dz2, dx2)


def _final_loss(x1, ffn, gain, mods, target):
    t, d = x1.shape
    rb = min(ROW_BLOCK, t)
    nb = t // rb

    def body(x1_ref, f_ref, g_ref, mod_ref, tg_ref, dx2_ref, df_ref, st_ref):
        i = pl.program_id(0)

        @pl.when(i == 0)
        def _():
            st_ref[...] = jnp.zeros_like(st_ref)

        ffn_v = f_ref[...]
        g2 = mod_ref[0:1, :]
        x2 = x1_ref[...] + g2 * ffn_v
        xh, r = _rms(x2)
        g = g_ref[...]
        err = xh * g - tg_ref[...]
        st_ref[2:3, :] += _colsum(err * err) * (0.5 / d)
        dy = err * (1.0 / d)
        st_ref[0:1, :] += _colsum(dy * xh)
        dx2 = _rms_bwd(dy * g, xh, r)
        dx2_ref[...] = dx2
        st_ref[1:2, :] += _colsum(dx2 * ffn_v)
        df_ref[...] = (dx2 * g2).astype(BF16)

        @pl.when(i == nb - 1)
        def _():
            st_ref[3:4, :] = jnp.broadcast_to(jnp.sum(st_ref[2:3, :], axis=-1, keepdims=True), (1, d))

    row = pl.BlockSpec((rb, d), lambda i: (i, 0))
    return _pcall(
        body,
        name="final_norm_loss",
        out_shape=[jax.ShapeDtypeStruct((t, d), F32), jax.ShapeDtypeStruct((t, d), BF16), jax.ShapeDtypeStruct((8, d), F32)],
        grid=(nb,),
        in_specs=[row, row, pl.BlockSpec((1, d), lambda i: (0, 0)), pl.BlockSpec((8, d), lambda i: (0, 0)), row],
        out_specs=[row, row, pl.BlockSpec((8, d), lambda i: (0, 0))],
        compiler_params=_cparams(("arbitrary",)),
    )(x1, ffn, gain, mods, target)


def _row_ends(shape):
    rows = lax.broadcasted_iota(jnp.int32, shape, 0)
    return rows == 0, rows == shape[0] - 1


def _shift_dn(v, first):
    return jnp.where(first, 0.0, pltpu.roll(v, 1, 0))


def _shift_up(v, last):
    return jnp.where(last, 0.0, pltpu.roll(v, v.shape[0] - 1, 0))


def _conv_fwd(u, cw, cb):
    t, f2 = u.shape
    f = f2 // 2
    cbk = _tile(f, 256)
    nf = f // cbk

    def body(ua_ref, ub_ref, cwa_ref, cwb_ref, cba_ref, cbb_ref, h_ref, uc_ref):
        first, last = _row_ends((t, cbk))
        outs = []
        for u_ref, cw_ref, cb_ref in ((ua_ref, cwa_ref, cba_ref), (ub_ref, cwb_ref, cbb_ref)):
            uu, cwv = u_ref[...].astype(F32), cw_ref[...]
            outs.append(cb_ref[...] + cwv[0:1, :] * _shift_dn(uu, first) + cwv[1:2, :] * uu
                        + cwv[2:3, :] * _shift_up(uu, last))
        a, b = outs
        uc_ref[0] = a.astype(BF16)
        uc_ref[1] = b.astype(BF16)
        h_ref[...] = (a * jax.nn.sigmoid(a) * b).astype(BF16)

    ca = lambda r: pl.BlockSpec((r, cbk), lambda j: (0, j))
    cbs = lambda r: pl.BlockSpec((r, cbk), lambda j: (0, nf + j))
    return _pcall(
        body,
        name="conv_gate_fwd",
        out_shape=[jax.ShapeDtypeStruct((t, f), BF16), jax.ShapeDtypeStruct((2, t, f), BF16)],
        grid=(nf,),
        in_specs=[ca(t), cbs(t), ca(3), cbs(3), ca(1), cbs(1)],
        out_specs=[ca(t), pl.BlockSpec((2, t, cbk), lambda j: (0, 0, j))],
        compiler_params=_cparams(("parallel",)),
    )(u, u, cw, cw, cb, cb)


def _conv_bwd(u, uc, cw, dh):
    t, f2 = u.shape
    f = f2 // 2
    cbk = _tile(f, 256)
    nf = f // cbk

    def body(ua_ref, ub_ref, uc_ref, cwa_ref, cwb_ref, dh_ref, du_ref, dcw_ref, dcb_ref):
        first, last = _row_ends((t, cbk))
        a, b = uc_ref[0].astype(F32), uc_ref[1].astype(F32)
        dh_v = dh_ref[...].astype(F32)
        sg = jax.nn.sigmoid(a)
        db = dh_v * (a * sg)
        da = dh_v * b * (sg * (1.0 + a * (1.0 - sg)))
        for idx, (dv, u_ref, cw_ref) in enumerate(((da, ua_ref, cwa_ref), (db, ub_ref, cwb_ref))):
            uu, cwv = u_ref[...].astype(F32), cw_ref[...]
            up, dn = _shift_up(dv, last), _shift_dn(dv, first)
            dcb_ref[idx] = _colsum(dv)
            dcw_ref[idx, 0:1, :] = _colsum(up * uu)
            dcw_ref[idx, 1:2, :] = _colsum(dv * uu)
            dcw_ref[idx, 2:3, :] = _colsum(dn * uu)
            du_ref[idx] = (cwv[0:1, :] * up + cwv[1:2, :] * dv + cwv[2:3, :] * dn).astype(BF16)

    ca = lambda r: pl.BlockSpec((r, cbk), lambda j: (0, j))
    cbs = lambda r: pl.BlockSpec((r, cbk), lambda j: (0, nf + j))
    o3 = lambda r: pl.BlockSpec((2, r, cbk), lambda j: (0, 0, j))
    return _pcall(
        body,
        name="conv_gate_bwd",
        out_shape=[jax.ShapeDtypeStruct((2, t, f), BF16), jax.ShapeDtypeStruct((2, 3, f), F32),
                   jax.ShapeDtypeStruct((2, 1, f), F32)],
        grid=(nf,),
        in_specs=[ca(t), cbs(t), o3(t), ca(3), cbs(3), ca(t)],
        out_specs=[o3(t), o3(3), o3(1)],
        compiler_params=_cparams(("parallel",)),
    )(u, u, uc, cw, cw, dh)


def _attention_fwd(q, kk, vv, *, hq, hkv, dk, dv, k_blk0, v_blk0, name):
    t = q.shape[0]
    tk = kk.shape[0]
    g_sz = hq // hkv
    tq = min(ATT_Q_BLOCK_FWD, t)

    def body(q_ref, k_ref, v_ref, o_ref, lse_ref):
        k = k_ref[...]
        v = v_ref[...]
        for j in range(g_sz):
            s = lax.dot_general(q_ref[:, j * dk:(j + 1) * dk], k, _DIMS["nt"], preferred_element_type=F32)
            m = jnp.max(s, axis=-1, keepdims=True)
            p = jnp.exp2(s - m)
            l = jnp.sum(p, axis=-1, keepdims=True)
            o = jnp.dot(p.astype(BF16), v, preferred_element_type=F32) / l
            o_ref[:, j * dv:(j + 1) * dv] = o.astype(BF16)
            lse_ref[0, :, j:j + 1] = m + jnp.log2(l)

    return _pcall(
        body,
        name=name,
        out_shape=[jax.ShapeDtypeStruct((t, hq * dv), BF16), jax.ShapeDtypeStruct((hkv, t, g_sz), F32)],
        grid=(hkv, t // tq),
        in_specs=[
            pl.BlockSpec((tq, g_sz * dk), lambda g, i: (i, g)),
            pl.BlockSpec((tk, dk), lambda g, i: (0, k_blk0 + g)),
            pl.BlockSpec((tk, dv), lambda g, i: (0, v_blk0 + g)),
        ],
        out_specs=[
            pl.BlockSpec((tq, g_sz * dv), lambda g, i: (i, g)),
            pl.BlockSpec((1, tq, g_sz), lambda g, i: (g, i, 0)),
        ],
        compiler_params=_cparams(("parallel", "parallel")),
    )(q, kk, vv)


def _attention_bwd(q, kk, vv, do, lse, *, hq, hkv, dk, dv, k_blk0, v_blk0, name):
    t = q.shape[0]
    tk = kk.shape[0]
    g_sz = hq // hkv
    tq = min(ATT_Q_BLOCK, t)

    def body(q_ref, k_ref, v_ref, do_ref, lse_ref, dq_ref, dk_ref, dv_ref):
        @pl.when(pl.program_id(1) == 0)
        def _():
            dk_ref[...] = jnp.zeros_like(dk_ref)
            dv_ref[...] = jnp.zeros_like(dv_ref)

        k = k_ref[...]
        v = v_ref[...]
        for j in range(g_sz):
            qj = q_ref[:, j * dk:(j + 1) * dk]
            doj = do_ref[:, j * dv:(j + 1) * dv]
            s = lax.dot_general(qj, k, _DIMS["nt"], preferred_element_type=F32)
            p = jnp.exp2(s - lse_ref[0, :, j:j + 1])
            dp = lax.dot_general(doj, v, _DIMS["nt"], preferred_element_type=F32)
            ds = (p * (dp - jnp.sum(p * dp, axis=-1, keepdims=True))).astype(BF16)
            dv_ref[...] += lax.dot_general(p.astype(BF16), doj, _DIMS["tn"], preferred_element_type=F32)
            dk_ref[...] += lax.dot_general(ds, qj, _DIMS["tn"], preferred_element_type=F32)
            dq_ref[:, j * dk:(j + 1) * dk] = jnp.dot(ds, k, preferred_element_type=F32)

        @pl.when(pl.program_id(1) == t // tq - 1)
        def _():
            dk_ref[...] *= LN2

    return _pcall(
        body,
        name=name,
        out_shape=[jax.ShapeDtypeStruct((t, hq * dk), F32), jax.ShapeDtypeStruct((tk, hkv * dk), F32),
                   jax.ShapeDtypeStruct((tk, hkv * dv), F32)],
        grid=(hkv, t // tq),
        in_specs=[
            pl.BlockSpec((tq, g_sz * dk), lambda g, i: (i, g)),
            pl.BlockSpec((tk, dk), lambda g, i: (0, k_blk0 + g)),
            pl.BlockSpec((tk, dv), lambda g, i: (0, v_blk0 + g)),
            pl.BlockSpec((tq, g_sz * dv), lambda g, i: (i, g)),
            pl.BlockSpec((1, tq, g_sz), lambda g, i: (g, i, 0)),
        ],
        out_specs=[
            pl.BlockSpec((tq, g_sz * dk), lambda g, i: (i, g)),
            pl.BlockSpec((tk, dk), lambda g, i: (0, g)),
            pl.BlockSpec((tk, dv), lambda g, i: (0, g)),
        ],
        compiler_params=_cparams(("parallel", "arbitrary")),
    )(q, kk, vv, do, lse)


def _silu(v):
    return v * jax.nn.sigmoid(v)


def _ada_fwd(conds, w_ada, b_ada_shard):
    r, d = conds.shape
    n = w_ada.shape[1]
    tn = _tile(n, 512)

    def body(c_ref, w_ref, b_ref, o_ref):
        s = _silu(c_ref[...]).astype(BF16)
        o_ref[...] = jnp.dot(s, w_ref[...].astype(BF16), preferred_element_type=F32) + b_ref[...]

    return _pcall(
        body,
        name="ada_fwd",
        out_shape=jax.ShapeDtypeStruct((r, n), F32),
        grid=(n // tn,),
        in_specs=[pl.BlockSpec((r, d), lambda j: (0, 0)), pl.BlockSpec((d, tn), lambda j: (0, j)),
                  pl.BlockSpec((1, tn), lambda j: (0, j))],
        out_specs=pl.BlockSpec((r, tn), lambda j: (0, j)),
        compiler_params=_cparams(("parallel",)),
    )(conds, w_ada, b_ada_shard)


def _cctx_partial(da16_shard, w_ada, c_ctx_row):
    d, n = w_ada.shape
    td = _tile(d, 512)

    def body(g_ref, w_ref, c_ref, o_ref):
        ds = lax.dot_general(g_ref[8:16, :].astype(BF16), w_ref[...].astype(BF16), _DIMS["nt"],
                             preferred_element_type=F32)
        cv = c_ref[...]
        sg = jax.nn.sigmoid(cv)
        o_ref[...] = ds * (sg * (1.0 + cv * (1.0 - sg)))

    return _pcall(
        body,
        name="cctx_partial",
        out_shape=jax.ShapeDtypeStruct((8, d), F32),
        grid=(d // td,),
        in_specs=[pl.BlockSpec((16, n), lambda j: (0, 0)), pl.BlockSpec((td, n), lambda j: (j, 0)),
                  pl.BlockSpec((1, td), lambda j: (0, j))],
        out_specs=pl.BlockSpec((8, td), lambda j: (0, j)),
        compiler_params=_cparams(("parallel",)),
    )(da16_shard, w_ada, c_ctx_row)


def _sum_parts(parts):
    p, _, n = parts.shape

    def body(p_ref, o_ref):
        acc = p_ref[0]
        for s in range(1, p):
            acc = acc + p_ref[s]
        o_ref[...] = acc

    return _pcall(
        body,
        name="sum_parts",
        out_shape=jax.ShapeDtypeStruct((1, n), F32),
        in_specs=[pl.BlockSpec(memory_space=pltpu.VMEM)],
        out_specs=pl.BlockSpec(memory_space=pltpu.VMEM),
    )(parts)


def _adam_math(w, g, m, v):
    m2 = ADAM_B1 * m + (1.0 - ADAM_B1) * g
    v2 = ADAM_B2 * v + (1.0 - ADAM_B2) * jnp.square(g)
    m_hat = m2 / (1.0 - ADAM_B1 ** ADAM_STEP)
    v_hat = v2 / (1.0 - ADAM_B2 ** ADAM_STEP)
    delta = -ADAM_LR * (m_hat / (jnp.sqrt(v_hat) + ADAM_EPS) + ADAM_WD * w)
    return delta, m2, v2


def _adamw(parts, w, m, v, name):
    p, r, c = parts.shape
    rb = _tile(r, max(8, (1 << 20) // (4 * c) // 8 * 8), 8)

    def body(p_ref, w_ref, m_ref, v_ref, g_ref, d_ref, m2_ref, v2_ref):
        g = p_ref[0].astype(F32)
        for s in range(1, p):
            g = g + p_ref[s].astype(F32)
        g_ref[...] = g
        d_ref[...], m2_ref[...], v2_ref[...] = _adam_math(w_ref[...], g, m_ref[...], v_ref[...])

    if w.ndim == 3:
        row = pl.BlockSpec((None, rb, c), lambda i: (0, i, 0))
    else:
        row = pl.BlockSpec((rb, c), lambda i: (i, 0))
    return _pcall(
        body,
        name=name,
        out_shape=[jax.ShapeDtypeStruct(w.shape, F32)] * 4,
        grid=(r // rb,),
        in_specs=[pl.BlockSpec((p, rb, c), lambda i: (0, i, 0)), row, row, row],
        out_specs=[row] * 4,
        compiler_params=_cparams(("parallel",)),
    )(parts, w, m, v)


def _adamw_ada(conds, da16, w, m, v):
    d, n = w.shape
    rb = _tile(d, 256, LANE)

    def body(s_ref, da_ref, w_ref, m_ref, v_ref, g_ref, d_ref, m2_ref, v2_ref):
        g = lax.dot_general(_silu(s_ref[...]).astype(BF16), da_ref[...].astype(BF16), _DIMS["tn"],
                            preferred_element_type=F32)
        g_ref[...] = g
        d_ref[...], m2_ref[...], v2_ref[...] = _adam_math(w_ref[...], g, m_ref[...], v_ref[...])

    row = pl.BlockSpec((rb, n), lambda i: (i, 0))
    return _pcall(
        body,
        name="adamw_w_ada",
        out_shape=[jax.ShapeDtypeStruct((d, n), F32)] * 4,
        grid=(d // rb,),
        in_specs=[pl.BlockSpec((16, rb), lambda i: (0, i)), pl.BlockSpec((16, n), lambda i: (0, 0)), row, row, row],
        out_specs=[row] * 4,
        compiler_params=_cparams(("parallel",)),
    )(conds, da16, w, m, v)


def _touch(arrays, name):
    def body(*refs):
        refs[-1][...] = jnp.zeros((8, LANE), F32)

    return _pcall(body, name=name, out_shape=jax.ShapeDtypeStruct((8, LANE), F32),
                  in_specs=[pl.BlockSpec(memory_space=pl.ANY)] * len(arrays),
                  out_specs=pl.BlockSpec(memory_space=pltpu.VMEM))(*arrays)


def _cast_bf16(a, name):
    _, r, c = a.shape
    rb = _tile(r, 512, 8)

    def body(a_ref, o_ref):
        o_ref[...] = a_ref[...].astype(BF16)

    return _pcall(body, name=name, out_shape=jax.ShapeDtypeStruct((r, c), BF16), grid=(r // rb,),
                  in_specs=[pl.BlockSpec((None, rb, c), lambda i: (0, i, 0))],
                  out_specs=pl.BlockSpec((rb, c), lambda i: (i, 0)), compiler_params=_cparams(("parallel",)))(a)


def _rope_tabs(t, rot):
    half, q = rot // 2, rot // 4
    n_rows = t // GRID_W
    row = jnp.repeat(jnp.arange(n_rows, dtype=F32), GRID_W)
    col = jnp.tile(jnp.arange(GRID_W, dtype=F32), n_rows)
    inv_freq = ROPE_THETA ** (-jnp.arange(0, half, 2, dtype=F32) / half)
    ang = jnp.concatenate([row[:, None] * inv_freq, col[:, None] * inv_freq], axis=-1)
    cos, sin = jnp.cos(ang), jnp.sin(ang)
    c0, c1, s0, s1 = cos[:, :q], cos[:, q:], sin[:, :q], sin[:, q:]
    z = jnp.zeros_like(s0)
    return (jnp.concatenate([c0, c0, c1, c1], -1), jnp.concatenate([-s0, z, -s1, z], -1),
            jnp.concatenate([z, s0, z, s1], -1))


def _pad_cols(a, left, total, fill=0.0):
    return jnp.pad(a, ((0, 0), (left, total - left - a.shape[1])), constant_values=fill)


def _with_ctx_rows(tab, tc, fill):
    return jnp.concatenate([tab, jnp.full((tc, tab.shape[1]), fill, F32)], axis=0)


def kernel(x, c, ctx, c_ctx, w_ada, b_ada, norm1_g, w_in, mla_q_norm_g, w_q_up, mla_kv_norm_g, w_kv_up, gqa_q_norm_g, gqa_k_norm_g, w_br_a, w_br_b, w_out, norm2_g, w_up, conv_w, conv_b, w_down, final_norm_g, loss_target, m_c_ctx, m_w_ada, m_b_ada, m_norm1_g, m_w_in, m_mla_q_norm_g, m_w_q_up, m_mla_kv_norm_g, m_w_kv_up, m_gqa_q_norm_g, m_gqa_k_norm_g, m_w_br_a, m_w_br_b, m_w_out, m_norm2_g, m_w_up, m_conv_w, m_conv_b, m_w_down, m_final_norm_g, v_c_ctx, v_w_ada, v_b_ada, v_norm1_g, v_w_in, v_mla_q_norm_g, v_w_q_up, v_mla_kv_norm_g, v_w_kv_up, v_gqa_q_norm_g, v_gqa_k_norm_g, v_w_br_a, v_w_br_b, v_w_out, v_norm2_g, v_w_up, v_conv_w, v_conv_b, v_w_down, v_final_norm_g):
    weights = dict(c_ctx=c_ctx, w_ada=w_ada, b_ada=b_ada, norm1_g=norm1_g, w_in=w_in, mla_q_norm_g=mla_q_norm_g,
                   w_q_up=w_q_up, mla_kv_norm_g=mla_kv_norm_g, w_kv_up=w_kv_up, gqa_q_norm_g=gqa_q_norm_g,
                   gqa_k_norm_g=gqa_k_norm_g, w_br_a=w_br_a, w_br_b=w_br_b, w_out=w_out, norm2_g=norm2_g, w_up=w_up,
                   conv_w=conv_w, conv_b=conv_b, w_down=w_down, final_norm_g=final_norm_g)
    mom_m = dict(c_ctx=m_c_ctx, w_ada=m_w_ada, b_ada=m_b_ada, norm1_g=m_norm1_g, w_in=m_w_in, mla_q_norm_g=m_mla_q_norm_g,
                 w_q_up=m_w_q_up, mla_kv_norm_g=m_mla_kv_norm_g, w_kv_up=m_w_kv_up, gqa_q_norm_g=m_gqa_q_norm_g,
                 gqa_k_norm_g=m_gqa_k_norm_g, w_br_a=m_w_br_a, w_br_b=m_w_br_b, w_out=m_w_out, norm2_g=m_norm2_g,
                 w_up=m_w_up, conv_w=m_conv_w, conv_b=m_conv_b, w_down=m_w_down, final_norm_g=m_final_norm_g)
    mom_v = dict(c_ctx=v_c_ctx, w_ada=v_w_ada, b_ada=v_b_ada, norm1_g=v_norm1_g, w_in=v_w_in, mla_q_norm_g=v_mla_q_norm_g,
                 w_q_up=v_w_q_up, mla_kv_norm_g=v_mla_kv_norm_g, w_kv_up=v_w_kv_up, gqa_q_norm_g=v_gqa_q_norm_g,
                 gqa_k_norm_g=v_gqa_k_norm_g, w_br_a=v_w_br_a, w_br_b=v_w_br_b, w_out=v_w_out, norm2_g=v_norm2_g,
                 w_up=v_w_up, conv_w=v_conv_w, conv_b=v_conv_b, w_down=v_w_down, final_norm_g=v_final_norm_g)
    order = list(weights)

    my_idx = 4 * lax.axis_index("x") + 2 * lax.axis_index("y") + lax.axis_index("c")
    xs, cts, tgt = x[0], ctx[0], loss_target[0]
    t, d = xs.shape
    tc = cts.shape[0]
    ta = t + tc
    kvl, ql = MLA_KV_LORA, MLA_Q_LORA
    nb = GQA_KV_HEADS * GQA_HEAD_DIM
    hb = GQA_HEADS * GQA_HEAD_DIM
    ha = MLA_HEADS
    f2 = w_up.shape[2] * N_DEV
    ff = f2 // 2

    big = ["w_in", "w_q_up", "w_kv_up", "w_br_a", "w_br_b", "w_out", "w_up", "w_down"]
    nw = len(big)
    del nw
    _ORDER_AFTER.clear()
    shards = {"w_in": _cast_bf16(weights["w_in"], "cast_w_in")}
    c_idx = jnp.reshape(lax.axis_index("c"), (1,)).astype(jnp.int32)

    def gather_start(names, dep):
        shs = [shards[n] for n in names]
        land = [lax.empty((N_DEV,) + s.shape, BF16) for s in shs]
        if dep is not None:
            _after(dep)
        s, r, arrs, tok = _split_start("gather_ici_start_" + names[0], shs + land, _gather_ici_copies(len(names)),
                                       4 * len(names))
        return dict(names=names, s=s, r=r, arrs=arrs, tok=tok)

    def gather_pass(g, after):
        n = len(g["names"])
        arrs = _split_wait("gather_ici_wait_" + g["names"][0], g["s"], g["r"], g["arrs"], _gather_ici_copies(n), after)
        s, r, bufs, tok = _split_start("gather_pass_start_" + g["names"][0], arrs[n:], _gather_pass_copies(n), 3 * n)
        g.update(s2=s, r2=r, bufs=bufs)
        return tok

    def gather_relay(g, after):
        n = len(g["names"])
        bufs = _split_wait("gather_pass_wait_" + g["names"][0], g["s2"], g["r2"], g["bufs"], _gather_pass_copies(n), after)
        s, r, bufs, tok = _split_start("gather_d2d_start_" + g["names"][0], bufs, _gather_d2d_copies(n), n)
        g.update(s3=s, r3=r, bufs=bufs)
        return tok

    def gather_finish(g, after):
        n = len(g["names"])
        bufs = _split_wait("gather_d2d_wait_" + g["names"][0], g["s3"], g["r3"], g["bufs"], _gather_d2d_copies(n), after)
        return dict(zip(g["names"], bufs))

    c_all, cw_all = _all_gather([jnp.pad(c, ((0, 7), (0, 0))), jnp.pad(conv_w[0], ((0, 5), (0, 0)))], "gather_cond")
    conv_w_f = jnp.transpose(cw_all[:, :3, :], (1, 0, 2)).reshape(3, f2)
    conds = jnp.concatenate([c_all[:, 0, :], c_ctx[None, :], jnp.zeros((7, d), F32)], axis=0)
    ncol = w_ada.shape[2]
    b_shard = lax.dynamic_slice_in_dim(b_ada, my_idx * ncol, ncol, axis=1)
    ada_shard = _ada_fwd(conds, w_ada[0], b_shard)
    (ada_all,) = _all_gather([ada_shard], "gather_ada")
    ada = jnp.transpose(ada_all, (1, 0, 2)).reshape(16, N_DEV * ncol)
    lat = lax.dynamic_slice_in_dim(ada, my_idx, 1, axis=0).reshape(6, d)
    cxt = ada[8].reshape(6, d)
    zero2 = jnp.zeros((2, d), F32)
    mods1 = jnp.concatenate([lat[0:2], cxt[0:2], jnp.zeros((4, d), F32)], axis=0)
    mods2 = jnp.concatenate([lat[2:3], lat[3:4], lat[4:5], jnp.zeros((5, d), F32)], axis=0)
    mods2b = jnp.concatenate([lat[2:3], lat[4:5], jnp.zeros((6, d), F32)], axis=0)
    mods3 = jnp.concatenate([lat[5:6], jnp.zeros((7, d), F32)], axis=0)
    del zero2

    g0 = gather_start(["w_in"], ada_all)
    for n in big[1:]:
        _after(g0["tok"])
        shards[n] = _cast_bf16(weights[n], "cast_" + n)

    ca, s1a, s2a = _rope_tabs(t, MLA_ROPE)
    cb_, s1b, s2b = _rope_tabs(t, GQA_HEAD_DIM)
    q_tabs_a = (_pad_cols(jnp.concatenate([jnp.ones((t, MLA_NOPE), F32), ca], 1), 0, MLA_SLOT),
                _pad_cols(s1a, MLA_NOPE, MLA_SLOT), _pad_cols(s2a, MLA_NOPE, MLA_SLOT))
    q_tabs_b = (cb_, s1b, s2b)
    k_tabs = (_with_ctx_rows(_pad_cols(ca, 0, LANE), tc, 1.0), _with_ctx_rows(_pad_cols(s1a, 0, LANE), tc, 0.0),
              _with_ctx_rows(_pad_cols(s2a, 0, LANE), tc, 0.0),
              _with_ctx_rows(cb_, tc, 1.0), _with_ctx_rows(s1b, tc, 0.0), _with_ctx_rows(s2b, tc, 0.0))

    def cols_full(g):
        return jnp.transpose(g, (1, 0, 2)).reshape(g.shape[1], N_DEV * g.shape[2])

    _after(g0["tok"])
    early = _touch([mom_m["w_in"], mom_v["w_in"], mom_m["w_q_up"], mom_v["w_q_up"]], "touch_moments")
    _after(early, *q_tabs_a, *q_tabs_b, *k_tabs, *[shards[n] for n in big[1:]])
    tok_p0 = gather_pass(g0, mods1)
    g1 = gather_start(["w_q_up", "w_kv_up", "w_br_a", "w_br_b", "w_out"], tok_p0)
    _after(g1["tok"])
    z_all = _norm_mod_fwd(cts, xs, norm1_g, mods1)
    gathered = gather_finish(g0, gather_relay(g0, z_all))
    w_in_f = cols_full(gathered["w_in"])
    o_kpe, o_kb, o_vb = kvl, kvl + MLA_ROPE, kvl + MLA_ROPE + nb
    o_q = o_vb + nb
    o_g = o_q + ql + hb
    wkv_w = kvl + 2 * nb + LANE
    w_kv_p = jnp.concatenate([w_in_f[:, :kvl], w_in_f[:, o_kb:o_q], w_in_f[:, o_kpe:o_kb],
                              jnp.zeros((d, LANE - MLA_ROPE), BF16)], axis=1)
    q_w = ql + hb
    q_pad = (-q_w) % 512 if d >= 512 else (-q_w) % d
    gate_blk = (q_w + q_pad) // d
    assert (q_w + q_pad) % d == 0
    w_qg_p = jnp.concatenate([w_in_f[:, o_q:o_g], jnp.zeros((d, q_pad), BF16), w_in_f[:, o_g:]], axis=1)

    kv_all = _mm(z_all, w_kv_p, "nn", F32, "proj_kv", tm=1152, tn=wkv_w)
    qg = _mm(z_all, w_qg_p, "nn", F32, "proj_qg", tm=1024, tn=1024, rows=t)
    tok_p1 = gather_pass(g1, qg)
    g2 = gather_start(["w_up"], tok_p1)
    g3 = gather_start(["w_down"], g2["tok"])
    _after(g3["tok"])
    kin, k_b, v_b = _key_prep_fwd(kv_all, mla_kv_norm_g, gqa_k_norm_g, k_tabs)
    sc_a = float((MLA_NOPE + MLA_ROPE) ** -0.5) * LOG2E
    sc_b = float(GQA_HEAD_DIM ** -0.5) * LOG2E
    _after(g3["tok"])
    cqn, q_b = _q_prep_fwd(qg, mla_q_norm_g, gqa_q_norm_g, q_tabs_b, sc_b)
    _after(kin, g3["tok"])
    gathered.update(gather_finish(g1, gather_relay(g1, q_b)))

    wq_f = cols_full(gathered["w_q_up"]).reshape(ql, ha, MLA_NOPE + MLA_ROPE)
    wq_ext = jnp.pad(wq_f, ((0, 0), (0, 0), (0, MLA_SLOT - MLA_NOPE - MLA_ROPE))).reshape(ql, ha * MLA_SLOT)
    wkv_f = cols_full(gathered["w_kv_up"]).reshape(kvl, ha, MLA_NOPE + MLA_V)
    wk_slots = jnp.pad(wkv_f[:, :, :MLA_NOPE], ((0, 0), (0, 0), (0, MLA_SLOT - MLA_NOPE))).reshape(kvl, ha * MLA_SLOT)
    wv_cols = wkv_f[:, :, MLA_NOPE:].reshape(kvl, ha * MLA_V)
    e_slot = jnp.pad(jnp.eye(MLA_ROPE, dtype=BF16),
                     ((0, LANE - MLA_ROPE), (MLA_NOPE, MLA_SLOT - MLA_NOPE - MLA_ROPE)))
    e_rows = jnp.concatenate([jnp.tile(e_slot, (1, ha)), jnp.zeros((LANE, ha * MLA_V), BF16)], axis=1)
    wkv_ext = jnp.concatenate([jnp.concatenate([wk_slots, wv_cols], axis=1), e_rows], axis=0)
    w_bra = cols_full(gathered["w_br_a"])
    w_brb = cols_full(gathered["w_br_b"])
    w_out_f = gathered["w_out"].reshape(d, d)

    kv_a = _mm(kin, wkv_ext, "nn", BF16, "kv_up", tm=1152, tn=1024)
    qa_raw = _mm(cqn, wq_ext, "nn", F32, "q_up", tm=1024, tn=1024)
    q_a = _rope_a(qa_raw, q_tabs_a, False, BF16, "rope_q_fwd", sc_a)
    att_a = dict(hq=ha, hkv=ha, dk=MLA_SLOT, dv=MLA_V, k_blk0=0, v_blk0=ha * MLA_SLOT // MLA_V)
    att_b = dict(hq=GQA_HEADS, hkv=GQA_KV_HEADS, dk=GQA_HEAD_DIM, dv=GQA_HEAD_DIM, k_blk0=0, v_blk0=0)
    o_a, lse_a = _attention_fwd(q_a, kv_a, kv_a, name="attn_a_fwd", **att_a)
    o_b, lse_b = _attention_fwd(q_b, k_b, v_b, name="attn_b_fwd", **att_b)
    _after(o_a)
    _after(gather_pass(g2, o_b))
    pa = _mm(o_a, w_bra, "nn", BF16, "br_a", tm=1024, tn=1024)
    pb = _mm(o_b, w_brb, "nn", BF16, "br_b", tm=1024, tn=1024)
    merged = _merge_fwd(pa, pb, qg, gate_blk)
    attn = _mm(merged, w_out_f, "nn", F32, "w_out", tm=1024, tn=1024)
    x1, z2 = _resid_norm_mod(xs, attn, norm2_g, mods2, "resid_norm2_fwd")
    tok_r2 = gather_relay(g2, z2)
    tok_p3 = gather_pass(g3, tok_r2)
    w_up3 = gather_finish(g2, tok_p3)["w_up"]
    u = _mm_up_fwd(z2, w_up3, "w_up")
    tok_r3 = gather_relay(g3, u)
    _after(tok_r3)
    h, uc = _conv_fwd(u, conv_w_f, conv_b)
    w_down_f = gather_finish(g3, h)["w_down"].reshape(ff, d)
    ffn = _mm(h, w_down_f, "nn", F32, "w_down", tm=1024, tn=1024, tk=2816)

    def to_shards(g):
        return jnp.transpose(g.reshape(g.shape[0], N_DEV, g.shape[1] // N_DEV), (1, 0, 2))

    def reduce_start(tag, names, sends):
        n = len(sends)
        land = [lax.empty((4,) + s.shape[1:], s.dtype) for s in sends]
        s, r, arrs, tok = _split_start("reduce_d2d_start_" + tag, sends + land, _reduce_d2d_copies(n), 4 * n)
        return dict(tag=tag, names=names, s=s, r=r, arrs=arrs, tok=tok)

    def reduce_relay(g, after):
        n = len(g["names"])
        arrs = _split_wait("reduce_d2d_wait_" + g["tag"], g["s"], g["r"], g["arrs"], _reduce_d2d_copies(n), after)
        sums = [_pair_sum(arrs[a], arrs[n + a], c_idx, "pair_sum_" + g["names"][a]) for a in range(n)]
        land = [lax.empty(s.shape, s.dtype) for s in sums]
        s, r, arrs2, tok = _split_start("reduce_ici_start_" + g["tag"], sums + land, _reduce_ici_copies(n), 4 * n)
        g.update(s2=s, r2=r, arrs2=arrs2)
        return tok

    def reduce_finish(g, after):
        n = len(g["names"])
        arrs2 = _split_wait("reduce_ici_wait_" + g["tag"], g["s2"], g["r2"], g["arrs2"], _reduce_ici_copies(n), after)
        return dict(zip(g["names"], arrs2[n:]))

    dx2, dffn, st_fin = _final_loss(x1, ffn, final_norm_g[None, :], mods3, tgt)
    loss = lax.psum(st_fin[3, 0], MESH_AXES)
    dh = _mm(dffn, w_down_f, "nt", BF16, "d_h", tm=1024, tn=1024)
    g_w_down = _mm(h, dffn, "tn", BF16, "g_w_down", tm=512, tn=1024)
    r_down = reduce_start("down", ["w_down"], [g_w_down.reshape(N_DEV, ff // N_DEV, d)])
    _after(r_down["tok"])
    du3, dcw, dcb = _conv_bwd(u, uc, conv_w_f, dh)
    dz2 = _mm_up_dz(du3, w_up3, "d_z2")
    g_w_up = _mm_up_gw(z2, du3, N_DEV, "g_w_up")
    g_conv_w = jnp.concatenate([dcw[0], dcw[1]], axis=1)
    tok = reduce_relay(r_down, g_w_up)
    _after(tok)
    r_up = reduce_start("up", ["w_up", "conv_w"], [g_w_up, to_shards(jnp.pad(g_conv_w, ((0, 5), (0, 0))))])
    _after(tok, r_up["tok"])
    dx1, dattn, st_n2 = _norm2_bwd(x1, attn, norm2_g, mods2b, dz2, dx2)
    dmerged = _mm(dattn, w_out_f, "nt", BF16, "d_merged", tm=1024, tn=1024)
    g_w_out = _mm(merged, dattn, "tn", BF16, "g_w_out", tm=1024, tn=1024)
    dpa, dpb, dgates = _merge_bwd(dmerged, pa, pb, qg, gate_blk)
    do_a = _mm(dpa, w_bra, "nt", BF16, "d_o_a", tm=1024, tn=1024)
    do_b = _mm(dpb, w_brb, "nt", BF16, "d_o_b", tm=1024, tn=1024)
    g_w_bra = _mm(o_a, dpa, "tn", BF16, "g_w_br_a", tm=1024, tn=1024)
    g_w_brb = _mm(o_b, dpb, "tn", BF16, "g_w_br_b", tm=1024, tn=1024)
    tok = reduce_relay(r_up, g_w_brb)
    _after(tok)
    r_out = reduce_start("out", ["w_out", "w_br_a", "w_br_b"],
                         [g_w_out.reshape(N_DEV, d // N_DEV, d), to_shards(g_w_bra), to_shards(g_w_brb)])
    _after(tok, r_out["tok"])
    dq_a, dk_a, dv_a = _attention_bwd(q_a, kv_a, kv_a, do_a, lse_a, name="attn_a_bwd", **att_a)
    dq_b, dk_b, dv_b = _attention_bwd(q_b, k_b, v_b, do_b, lse_b, name="attn_b_bwd", **att_b)
    _after(reduce_relay(r_out, dv_b))
    dqa_raw = _rope_a(dq_a, q_tabs_a, True, BF16, "rope_q_bwd", sc_a * LN2)
    dcqn = _mm(dqa_raw, wq_ext, "nt", F32, "d_cqn", tm=1024, tn=ql)
    g_wq_ext = _mm(cqn, dqa_raw, "tn", BF16, "g_w_q_up", tm=ql, tn=1024)
    dq_p, st_q, st_qb = _q_prep_bwd(qg, mla_q_norm_g, gqa_q_norm_g, q_tabs_b, dcqn, dq_b, q_pad, sc_b * LN2)
    dkin = _mm_cat_nt([(dk_a, wkv_ext, 0), (dv_a, wkv_ext, ha * MLA_SLOT)], F32, "d_kin", tm=1152, tn=kvl + LANE)
    g_wkv_ext = _mm_cat_tn(kin, [dk_a, dv_a], BF16, "g_w_kv_up", tm=kvl + LANE, tn=min(1024, ha * MLA_V))
    dkv_p, st_kv, st_kb = _key_prep_bwd(kv_all, mla_kv_norm_g, gqa_k_norm_g, k_tabs, dkin, dk_b, dv_b)
    g_wq = g_wq_ext.reshape(ql, ha, MLA_SLOT)[:, :, :MLA_NOPE + MLA_ROPE].reshape(ql, ha * (MLA_NOPE + MLA_ROPE))
    g_wkv = jnp.concatenate([g_wkv_ext[:kvl, :ha * MLA_SLOT].reshape(kvl, ha, MLA_SLOT)[:, :, :MLA_NOPE],
                             g_wkv_ext[:kvl, ha * MLA_SLOT:].reshape(kvl, ha, MLA_V)], axis=2).reshape(kvl, ha * (MLA_NOPE + MLA_V))
    r_qkv = reduce_start("qkv", ["w_q_up", "w_kv_up"], [to_shards(g_wq), to_shards(g_wkv)])
    _after(r_qkv["tok"])
    g_wkv_p = _mm(z_all, dkv_p, "tn", BF16, "g_w_in_kv", tm=1024, tn=wkv_w)
    g_wqg_p = _mm_cat_tn(z_all, [dq_p, dgates], BF16, "g_w_in_qg", tm=1024, tn=min(1024, d), rows=t)
    g_w_in = jnp.concatenate([g_wkv_p[:, :kvl], g_wkv_p[:, kvl + 2 * nb:kvl + 2 * nb + MLA_ROPE],
                              g_wkv_p[:, kvl:kvl + 2 * nb], g_wqg_p[:, :q_w], g_wqg_p[:, q_w + q_pad:]], axis=1)
    r_in = reduce_start("in", ["w_in"], [to_shards(g_w_in)])
    _after(r_in["tok"])
    qw_p = q_w + q_pad
    dz_lat = _mm_sum_nt([(dq_p, 0, w_qg_p, 0, qw_p), (dgates, 0, w_qg_p, qw_p, d), (dgates, d, w_qg_p, qw_p + d, d),
                         (dkv_p, 0, w_kv_p, 0, wkv_w)], F32, "d_z_lat", rows=t)
    dz_ctx = _mm(dkv_p, w_kv_p, "nt", F32, "d_z_ctx", tm=min(ROW_BLOCK, tc), tn=1024, a_row_off=t)
    tok_q = reduce_relay(r_qkv, dz_ctx)
    _after(tok_q)
    grad_x, st_n1 = _norm1_bwd(cts, xs, norm1_g, mods1, dz_ctx, dz_lat, dx1)

    res = {}

    def upd(nm, parts):
        wv, mv, vv = weights[nm], mom_m[nm], mom_v[nm]
        if wv.ndim == 1:
            wv, mv, vv = (a.reshape(1, -1) for a in (wv, mv, vv))
        outs = _adamw(parts, wv, mv, vv, "adamw_" + nm)
        res[nm] = [o_.reshape(weights[nm].shape) for o_ in outs]

    d_lat = jnp.concatenate([st_n1[0], st_n1[1], st_n2[3], st_n2[0], st_n2[1], st_fin[1]])
    d_cxt = jnp.concatenate([st_n1[3], st_n1[4], jnp.zeros((4 * d,), F32)])
    small = jnp.concatenate([d_lat, d_cxt, st_n1[2], st_q[0], st_kv[0], st_qb[0], st_kb[0], st_n2[2],
                             jnp.concatenate([dcb[0, 0], dcb[1, 0]]), st_fin[0]])
    n_small = small.shape[0]
    pad_small = (-n_small) % LANE
    (small_all,) = _all_gather([jnp.pad(small, (0, pad_small)).reshape(1, -1)], "gather_small")
    offs = {}
    o = 0
    for nm, ln in (("d_lat", 6 * d), ("d_cxt", 6 * d), ("norm1_g", d), ("mla_q_norm_g", ql), ("mla_kv_norm_g", kvl),
                   ("gqa_q_norm_g", GQA_HEAD_DIM), ("gqa_k_norm_g", GQA_HEAD_DIM), ("norm2_g", d), ("conv_b", f2),
                   ("final_norm_g", d)):
        offs[nm] = (o, ln)
        o += ln

    def part(nm):
        a, ln = offs[nm]
        return small_all[:, :, a:a + ln]

    d_lat_all = part("d_lat")[:, 0, :]
    d_cxt_sum = _sum_parts(part("d_cxt"))
    da16 = jnp.concatenate([d_lat_all, d_cxt_sum, jnp.zeros((7, 6 * d), F32)], axis=0)
    da16_shard = lax.dynamic_slice_in_dim(da16, my_idx * ncol, ncol, axis=1)
    cc_part = _cctx_partial(da16_shard, w_ada[0], c_ctx[None, :])
    (cc_all,) = _all_gather([cc_part], "gather_cctx")
    cc_parts = cc_all[:, 0:1, :]
    tok_i = reduce_relay(r_in, cc_all)

    _after(tok_i)
    for nm in ("norm1_g", "mla_q_norm_g", "mla_kv_norm_g", "gqa_q_norm_g", "gqa_k_norm_g", "norm2_g", "conv_b",
               "final_norm_g"):
        upd(nm, part(nm))
    upd("c_ctx", cc_parts)
    b_parts = jnp.concatenate([d_lat_all[:, None, :], d_cxt_sum[None]], axis=0)
    upd("b_ada", b_parts)
    _after(tok_i)
    outs = _adamw_ada(conds, da16_shard, w_ada[0], m_w_ada[0], v_w_ada[0])
    res["w_ada"] = [o_[None] for o_ in outs]
    last = outs[0]
    done = [last]
    for grp in (r_down, r_up, r_out, r_qkv, r_in):
        _after(*done)
        recv = reduce_finish(grp, last)
        for nm in grp["names"]:
            upd(nm, recv[nm][:, :3, :] if nm == "conv_w" else recv[nm])
            last = res[nm][0]
            done.append(last)

    return (loss, grad_x[None], *[res[n][0] for n in order], *[res[n][1] for n in order],
            *[res[n][2] for n in order], *[res[n][3] for n in order])
```

```python
import functools

import jax
import jax.numpy as jnp
from jax import lax
from jax.experimental import pallas as pl
from jax.experimental.pallas import tpu as pltpu

F32 = jnp.float32
BF16 = jnp.bfloat16

GRID_W = 64
ROPE_THETA = 10000.0
NORM_EPS = 1e-6
MLA_HEADS = 8
MLA_Q_LORA = 768
MLA_KV_LORA = 512
MLA_NOPE = 128
MLA_ROPE = 64
MLA_V = 128
GQA_HEADS = 8
GQA_KV_HEADS = 2
GQA_HEAD_DIM = 128
ADAM_LR = 0.001
ADAM_B1 = 0.9
ADAM_B2 = 0.999
ADAM_EPS = 1e-08
ADAM_WD = 0.01
ADAM_STEP = 10

N_DEV = 8
MESH_AXES = ("x", "y", "c")
LANE = 128
MLA_SLOT = 2 * LANE
VMEM_LIMIT = 56 * 1024 * 1024
ROW_BLOCK = 256
ATT_Q_BLOCK = 512
ATT_Q_BLOCK_FWD = 512
LN2 = 0.6931471805599453
LOG2E = 1.4426950408889634
MESH_ID = pl.DeviceIdType.MESH


def _tile(n, pref, align=LANE):
    if n <= pref:
        return n
    best = None
    t = align
    while t <= pref:
        if n % t == 0:
            best = t
        t += align
    assert best is not None, (n, pref, align)
    return best


def _cparams(sem=None):
    return pltpu.CompilerParams(dimension_semantics=sem, vmem_limit_bytes=VMEM_LIMIT)


_ORDER_AFTER = []


def _after(*arrays):
    _ORDER_AFTER.extend(arrays)


def _pcall(body, *, in_specs, **kw):
    deps = tuple(_ORDER_AFTER)
    _ORDER_AFTER.clear()
    if not deps:
        return pl.pallas_call(body, in_specs=in_specs, **kw)
    n_in, n_dep = len(in_specs), len(deps)

    def with_deps(*refs):
        body(*refs[:n_in], *refs[n_in + n_dep:])

    call = pl.pallas_call(with_deps, in_specs=list(in_specs) + [pl.BlockSpec(memory_space=pl.ANY)] * n_dep, **kw)
    return lambda *args: call(*args, *deps)


def _all_gather(arrs, name):
    n = len(arrs)

    def body(*refs):
        ins = refs[:n]
        outs = refs[n:2 * n]
        send_sems, recv_sems, local_sems = refs[2 * n:]
        x, y, c = lax.axis_index("x"), lax.axis_index("y"), lax.axis_index("c")
        me, sibling = (x, y, c), (x, y, 1 - c)
        chips = [(1 - x, y), (x, 1 - y), (1 - x, 1 - y)]

        def rows(a, dev):
            px, py, pc = dev
            return outs[a].at[4 * px + 2 * py + pc]

        def copy(a, k, block, to, src=None):
            return pltpu.make_async_remote_copy(
                src_ref=rows(a, block) if src is None else src,
                dst_ref=rows(a, block),
                send_sem=send_sems.at[7 * a + k],
                recv_sem=recv_sems.at[7 * a + k],
                device_id=to,
                device_id_type=MESH_ID,
            )

        mine = [pltpu.make_async_copy(ins[a], rows(a, me), local_sems.at[a]) for a in range(n)]
        for cp in mine:
            cp.start()
        first = []
        for a in range(n):
            first.append(copy(a, 0, me, sibling, src=ins[a]))
            first += [copy(a, 1 + j, me, (*chip, c), src=ins[a]) for j, chip in enumerate(chips)]
        for cp in first:
            cp.start()
        passed = []
        for j, chip in enumerate(chips):
            for a in range(n):
                copy(a, 1 + j, (*chip, c), me).wait_recv()
                fwd = copy(a, 4 + j, (*chip, c), sibling)
                fwd.start()
                passed.append(fwd)
        for a in range(n):
            copy(a, 0, sibling, me).wait_recv()
            for j, chip in enumerate(chips):
                copy(a, 4 + j, (*chip, 1 - c), me).wait_recv()
        for cp in first + passed:
            cp.wait_send()
        for cp in mine:
            cp.wait()

    any_spec = pl.BlockSpec(memory_space=pl.ANY)
    outs = _pcall(
        body,
        name=name,
        out_shape=[jax.ShapeDtypeStruct((N_DEV,) + a.shape, a.dtype) for a in arrs],
        in_specs=[any_spec] * n,
        out_specs=[any_spec] * n,
        scratch_shapes=[
            pltpu.SemaphoreType.DMA((7 * n,)),
            pltpu.SemaphoreType.DMA((7 * n,)),
            pltpu.SemaphoreType.DMA((n,)),
        ],
    )(*arrs)
    return list(outs)


def _all_to_all(arrs, name):
    n = len(arrs)

    def body(*refs):
        ins = refs[:n]
        outs = refs[n:2 * n]
        send_sems, recv_sems, local_sems = refs[2 * n:]
        x, y, c = lax.axis_index("x"), lax.axis_index("y"), lax.axis_index("c")
        my_idx = 4 * x + 2 * y + c

        def peer(k):
            fx, fy, fc = (k >> 2) & 1, (k >> 1) & 1, k & 1
            return (x ^ fx if fx else x, y ^ fy if fy else y, c ^ fc if fc else c)

        def copy(a, k):
            px, py, pc = peer(k)
            return pltpu.make_async_remote_copy(
                src_ref=ins[a].at[4 * px + 2 * py + pc],
                dst_ref=outs[a].at[my_idx],
                send_sem=send_sems.at[7 * a + k - 1],
                recv_sem=recv_sems.at[7 * a + k - 1],
                device_id=(px, py, pc),
                device_id_type=MESH_ID,
            )

        mine = [pltpu.make_async_copy(ins[a].at[my_idx], outs[a].at[my_idx], local_sems.at[a]) for a in range(n)]
        for cp in mine:
            cp.start()
        order = [1, 4, 2, 5, 3, 6, 7]
        cps = [copy(a, k) for k in order for a in range(n)]
        for cp in cps:
            cp.start()
        for cp in cps:
            cp.wait()
        for cp in mine:
            cp.wait()

    any_spec = pl.BlockSpec(memory_space=pl.ANY)
    outs = _pcall(
        body,
        name=name,
        out_shape=[jax.ShapeDtypeStruct(a.shape, a.dtype) for a in arrs],
        in_specs=[any_spec] * n,
        out_specs=[any_spec] * n,
        scratch_shapes=[
            pltpu.SemaphoreType.DMA((7 * n,)),
            pltpu.SemaphoreType.DMA((7 * n,)),
            pltpu.SemaphoreType.DMA((n,)),
        ],
    )(*arrs)
    return list(outs)


_HBM = pl.BlockSpec(memory_space=pltpu.HBM)
_SEM = pl.BlockSpec(memory_space=pltpu.SEMAPHORE)
_EFFECT = pltpu.SideEffectType.DATAFLOW_SIDE_EFFECTING


def _descriptors(copies, send_sems, recv_sems):
    descs = []
    for i, (src, dst, dev) in enumerate(copies):
        if dev is None:
            descs.append(pltpu.make_async_copy(src, dst, recv_sems.at[i]))
        else:
            descs.append(pltpu.make_async_remote_copy(src_ref=src, dst_ref=dst, send_sem=send_sems.at[i],
                                                      recv_sem=recv_sems.at[i], device_id=dev, device_id_type=MESH_ID))
    return descs


def _split_start(name, arrays, copies_fn, n_copies):
    n = len(arrays)

    def body(*refs):
        send_sems, recv_sems = refs[n], refs[n + 1]
        token = refs[2 * n + 2]
        for dsc in _descriptors(copies_fn(refs[:n]), send_sems, recv_sems):
            dsc.start()
        token[...] = jnp.zeros_like(token)

    outs = _pcall(
        body,
        name=name,
        out_shape=(pltpu.SemaphoreType.DMA((n_copies,)), pltpu.SemaphoreType.DMA((n_copies,)),
                   *[pltpu.HBM(a.shape, a.dtype) for a in arrays], jax.ShapeDtypeStruct((8, LANE), F32)),
        in_specs=[_HBM] * n,
        out_specs=(_SEM, _SEM, *[_HBM] * n, pl.BlockSpec(memory_space=pltpu.VMEM)),
        input_output_aliases={i: 2 + i for i in range(n)},
        compiler_params=pltpu.CompilerParams(has_side_effects=_EFFECT),
    )(*[pltpu.with_memory_space_constraint(a, pltpu.HBM) for a in arrays])
    return outs[0], outs[1], list(outs[2:2 + n]), outs[2 + n]


def _split_wait(name, send_sems, recv_sems, arrays, copies_fn, after):
    n = len(arrays)

    def body(*refs):
        for dsc, (_, _, dev) in zip(_descriptors(copies_fn(refs[:n]), refs[n], refs[n + 1]), copies_fn(refs[:n])):
            if dev is None:
                dsc.wait()
            else:
                dsc.wait_send()
                dsc.wait_recv()

    outs = _pcall(
        body,
        name=name,
        out_shape=tuple(pltpu.HBM(a.shape, a.dtype) for a in arrays),
        in_specs=[_HBM] * n + [_SEM, _SEM, pl.BlockSpec(memory_space=pl.ANY)],
        out_specs=tuple([_HBM] * n),
        input_output_aliases={i: i for i in range(n)},
        compiler_params=pltpu.CompilerParams(has_side_effects=_EFFECT),
    )(*arrays, send_sems, recv_sems, after)
    return list(outs)


def _mesh_pos():
    x, y, c = lax.axis_index("x"), lax.axis_index("y"), lax.axis_index("c")
    return x, y, c, [(1 - x, y), (x, 1 - y), (1 - x, 1 - y)]


def _gather_ici_copies(n):
    def copies(refs):
        x, y, c, chips = _mesh_pos()
        me = 4 * x + 2 * y + c
        out = []
        for a in range(n):
            src, buf = refs[a], refs[n + a]
            out.append((src, buf.at[me], None))
            out.append((src, buf.at[me], (x, y, 1 - c)))
            out += [(src, buf.at[me], (cx, cy, c)) for cx, cy in chips[:2]]
        return out
    return copies


def _gather_pass_copies(n):
    def copies(refs):
        x, y, c, chips = _mesh_pos()
        south = c == 0
        bx, by = jnp.where(south, 1 - x, x), jnp.where(south, y, 1 - y)
        tx, ty = jnp.where(south, x, 1 - x), jnp.where(south, 1 - y, y)
        out = []
        for a in range(n):
            rows = refs[a].at[4 * bx + 2 * by + c]
            out.append((rows, rows, (tx, ty, c)))
            for cx, cy in chips[:2]:
                rows = refs[a].at[4 * cx + 2 * cy + c]
                out.append((rows, rows, (x, y, 1 - c)))
        return out
    return copies


def _gather_d2d_copies(n):
    def copies(refs):
        x, y, c, chips = _mesh_pos()
        cx, cy = chips[2]
        out = []
        for a in range(n):
            rows = refs[a].at[4 * cx + 2 * cy + c]
            out.append((rows, rows, (x, y, 1 - c)))
        return out
    return copies


def _reduce_d2d_copies(n):
    def copies(refs):
        x, y, c, _ = _mesh_pos()
        out = []
        for a in range(n):
            for k in range(4):
                out.append((refs[a].at[2 * k + (1 - c)], refs[n + a].at[k], (x, y, 1 - c)))
        return out
    return copies


def _reduce_ici_copies(n):
    def copies(refs):
        x, y, c, chips = _mesh_pos()
        mine = 2 * x + y
        out = []
        for a in range(n):
            src, land = refs[a], refs[n + a]
            out.append((src.at[mine], land.at[mine], None))
            out += [(src.at[2 * cx + cy], land.at[mine], (cx, cy, c)) for cx, cy in chips]
        return out
    return copies


def _pair_sum(send, land, c_idx, name):
    _, r, cols = send.shape
    rb = _tile(r, max(8, (1 << 22) // (send.dtype.itemsize * cols) // 8 * 8), 8)
    dt = send.dtype

    def body(c_ref, s_ref, l_ref, o_ref):
        o_ref[...] = (s_ref[...].astype(F32) + l_ref[...].astype(F32)).astype(dt)

    return pl.pallas_call(
        body,
        name=name,
        out_shape=jax.ShapeDtypeStruct((4, r, cols), dt),
        grid_spec=pltpu.PrefetchScalarGridSpec(
            num_scalar_prefetch=1,
            grid=(4, r // rb),
            in_specs=[pl.BlockSpec((None, rb, cols), lambda k, i, c_ref: (2 * k + c_ref[0], i, 0)),
                      pl.BlockSpec((None, rb, cols), lambda k, i, c_ref: (k, i, 0))],
            out_specs=pl.BlockSpec((None, rb, cols), lambda k, i, c_ref: (k, i, 0)),
        ),
        compiler_params=_cparams(("parallel", "parallel")),
    )(c_idx, send, land)


_DIMS = {
    "nn": (((1,), (0,)), ((), ())),
    "nt": (((1,), (1,)), ((), ())),
    "tn": (((0,), (0,)), ((), ())),
}


def _mm_call(a, b, *, mode, grid, a_spec, b_spec, o_spec, out_shape, acc_shape, name):
    nk = grid[2]
    out_dtype = out_shape.dtype

    def body(a_ref, b_ref, o_ref, *scratch):
        p = lax.dot_general(a_ref[...].astype(BF16), b_ref[...].astype(BF16), _DIMS[mode],
                            preferred_element_type=F32)
        if nk == 1:
            o_ref[...] = p.astype(out_dtype)
        else:
            acc = scratch[0]
            k = pl.program_id(2)

            @pl.when(k == 0)
            def _():
                acc[...] = p

            @pl.when(k > 0)
            def _():
                acc[...] += p

            @pl.when(k == nk - 1)
            def _():
                o_ref[...] = acc[...].astype(out_dtype)

    return _pcall(
        body,
        name=name,
        out_shape=out_shape,
        grid=grid,
        in_specs=[a_spec, b_spec],
        out_specs=o_spec,
        scratch_shapes=[pltpu.VMEM(acc_shape, F32)] if nk > 1 else [],
        compiler_params=_cparams(("parallel", "parallel", "arbitrary")),
    )(a, b)


def _mm(a, b, mode, out_dtype, name, tm=512, tn=512, tk=2432, a_row_off=0, rows=None):
    if mode == "nn":
        (m, k), (k2, n) = a.shape, b.shape
    elif mode == "nt":
        (m, k), (n, k2) = a.shape, b.shape
    else:
        (k, m), (k2, n) = a.shape, b.shape
        if rows is not None:
            k = k2 = rows
    assert k == k2, (a.shape, b.shape, mode)
    if mode != "tn":
        m = (m if rows is None else rows + a_row_off) - a_row_off
    tm, tn, tk = _tile(m, tm, 8), _tile(n, tn), _tile(k, tk, 8 if mode == "tn" else LANE)
    assert a_row_off % tm == 0
    ro = a_row_off // tm
    grid = (m // tm, n // tn, k // tk)
    if mode == "tn":
        a_spec = pl.BlockSpec((tk, tm), lambda i, j, kk: (kk, i))
    else:
        a_spec = pl.BlockSpec((tm, tk), lambda i, j, kk: (i + ro, kk))
    if mode == "nt":
        b_spec = pl.BlockSpec((tn, tk), lambda i, j, kk: (j, kk))
    else:
        b_spec = pl.BlockSpec((tk, tn), lambda i, j, kk: (kk, j))
    o_spec = pl.BlockSpec((tm, tn), lambda i, j, kk: (i, j))
    return _mm_call(a, b, mode=mode, grid=grid, a_spec=a_spec, b_spec=b_spec, o_spec=o_spec,
                    out_shape=jax.ShapeDtypeStruct((m, n), out_dtype), acc_shape=(tm, tn), name=name)


def _mm_cat_nt(pieces, out_dtype, name, tm=1024, tn=1024, tk=2048, rows=None):
    m = pieces[0][0].shape[0] if rows is None else rows
    n = pieces[0][1].shape[0]
    tm, tn = _tile(m, tm, 8), _tile(n, tn)
    steps, starts, s = [], [], 0
    for a, b, off in pieces:
        kp = a.shape[1]
        tkp = _tile(kp, tk)
        assert off % tkp == 0 and b.shape[0] == n
        steps.append((tkp, kp // tkp, off // tkp))
        starts.append(s)
        s += kp // tkp
    nk = s
    npc = len(pieces)

    def body(*refs):
        o_ref, acc = refs[2 * npc], refs[2 * npc + 1]
        kk = pl.program_id(2)

        @pl.when(kk == 0)
        def _():
            acc[...] = jnp.zeros_like(acc)

        for p in range(npc):
            @pl.when((kk >= starts[p]) & (kk < starts[p] + steps[p][1]))
            def _(p=p):
                acc[...] += lax.dot_general(refs[2 * p][...].astype(BF16), refs[2 * p + 1][...].astype(BF16), _DIMS["nt"],
                                            preferred_element_type=F32)

        @pl.when(kk == nk - 1)
        def _():
            o_ref[...] = acc[...].astype(out_dtype)

    in_specs, args = [], []
    for p, (a, b, off) in enumerate(pieces):
        tkp, np_, ob = steps[p]

        def rel(kk, p=p, np_=np_):
            return jnp.clip(kk - starts[p], 0, np_ - 1)

        in_specs.append(pl.BlockSpec((tm, tkp), lambda i, j, kk, rel=rel: (i, rel(kk))))
        in_specs.append(pl.BlockSpec((tn, tkp), lambda i, j, kk, rel=rel, ob=ob: (j, ob + rel(kk))))
        args += [a, b]
    return _pcall(
        body,
        name=name,
        out_shape=jax.ShapeDtypeStruct((m, n), out_dtype),
        grid=(m // tm, n // tn, nk),
        in_specs=in_specs,
        out_specs=pl.BlockSpec((tm, tn), lambda i, j, kk: (i, j)),
        scratch_shapes=[pltpu.VMEM((tm, tn), F32)],
        compiler_params=_cparams(("parallel", "parallel", "arbitrary")),
    )(*args)


def _mm_cat_tn(a, pieces, out_dtype, name, tm=1024, tn=1024, rows=None):
    k = a.shape[0] if rows is None else rows
    m = a.shape[1]
    tm = _tile(m, tm)
    starts, s = [], 0
    for b in pieces:
        assert b.shape[1] % tn == 0
        starts.append(s)
        s += b.shape[1] // tn
    nj = s
    npc = len(pieces)

    def body(*refs):
        a_ref, o_ref = refs[0], refs[1 + npc]
        j = pl.program_id(1)
        for p in range(npc):
            @pl.when((j >= starts[p]) & (j < starts[p] + pieces[p].shape[1] // tn))
            def _(p=p):
                o_ref[...] = lax.dot_general(a_ref[...].astype(BF16), refs[1 + p][...].astype(BF16), _DIMS["tn"],
                                             preferred_element_type=F32).astype(out_dtype)

    in_specs = [pl.BlockSpec((k, tm), lambda i, j: (0, i))]
    for p, b in enumerate(pieces):
        np_ = b.shape[1] // tn
        in_specs.append(pl.BlockSpec((k, tn), lambda i, j, p=p, np_=np_: (0, jnp.clip(j - starts[p], 0, np_ - 1))))
    return _pcall(
        body,
        name=name,
        out_shape=jax.ShapeDtypeStruct((m, nj * tn), out_dtype),
        grid=(m // tm, nj),
        in_specs=in_specs,
        out_specs=pl.BlockSpec((tm, tn), lambda i, j: (i, j)),
        compiler_params=_cparams(("parallel", "arbitrary")),
    )(a, *pieces)


def _mm_up_fwd(z2, w3, name, tm=1024):
    t, d = z2.shape
    nsh, _, c = w3.shape
    tm = _tile(t, tm, 8)
    return _mm_call(z2, w3, mode="nn", grid=(t // tm, nsh, 1),
                    a_spec=pl.BlockSpec((tm, d), lambda i, j, kk: (i, 0)),
                    b_spec=pl.BlockSpec((None, d, c), lambda i, j, kk: (j, 0, 0)),
                    o_spec=pl.BlockSpec((tm, c), lambda i, j, kk: (i, j)),
                    out_shape=jax.ShapeDtypeStruct((t, nsh * c), BF16), acc_shape=(tm, c), name=name)


def _mm_up_dz(du3, w3, name, tm=512, tn=1024):
    _, t, f = du3.shape
    nsh, d, c = w3.shape
    half = nsh // 2
    assert f == half * c
    tm, tn = _tile(t, tm, 8), _tile(d, tn)

    def body(a_ref, b_ref, o_ref, acc):
        kk = pl.program_id(2)
        p = None
        for s in range(half):
            q = lax.dot_general(a_ref[:, s * c:(s + 1) * c], b_ref[s], _DIMS["nt"], preferred_element_type=F32)
            p = q if p is None else p + q

        @pl.when(kk == 0)
        def _():
            acc[...] = p

        @pl.when(kk == 1)
        def _():
            o_ref[...] = (acc[...] + p).astype(BF16)

    return _pcall(
        body,
        name=name,
        out_shape=jax.ShapeDtypeStruct((t, d), BF16),
        grid=(t // tm, d // tn, 2),
        in_specs=[pl.BlockSpec((None, tm, f), lambda i, j, kk: (kk, i, 0)),
                  pl.BlockSpec((half, tn, c), lambda i, j, kk: (kk, j, 0))],
        out_specs=pl.BlockSpec((tm, tn), lambda i, j, kk: (i, j)),
        scratch_shapes=[pltpu.VMEM((tm, tn), F32)],
        compiler_params=_cparams(("parallel", "parallel", "arbitrary")),
    )(du3, w3)


def _mm_sum_nn(pieces, out_dtype, name, tm=512, tn=512, rows=None):
    m = pieces[0][0].shape[0] if rows is None else rows
    n = pieces[0][2].shape[1]
    tm, tn = _tile(m, tm, 8), _tile(n, tn)
    npc = len(pieces)

    def body(*refs):
        p = None
        for s in range(npc):
            q = jnp.dot(refs[2 * s][...].astype(BF16), refs[2 * s + 1][...].astype(BF16), preferred_element_type=F32)
            p = q if p is None else p + q
        refs[2 * npc][...] = p.astype(out_dtype)

    in_specs, args = [], []
    for a, ao, b, bo, kp in pieces:
        assert ao % kp == 0 and bo % kp == 0 and b.shape[1] == n
        in_specs.append(pl.BlockSpec((tm, kp), lambda i, j, ab=ao // kp: (i, ab)))
        in_specs.append(pl.BlockSpec((kp, tn), lambda i, j, bb=bo // kp: (bb, j)))
        args += [a, b]
    return _pcall(
        body,
        name=name,
        out_shape=jax.ShapeDtypeStruct((m, n), out_dtype),
        grid=(m // tm, n // tn),
        in_specs=in_specs,
        out_specs=pl.BlockSpec((tm, tn), lambda i, j: (i, j)),
        compiler_params=_cparams(("parallel", "parallel")),
    )(*args)


def _mm_rows_tn(pieces, b, out_dtype, name, tm=1024, tn=1024, rows=None):
    k = b.shape[0] if rows is None else rows
    n = b.shape[1]
    tn = _tile(n, tn)
    starts, s = [], 0
    for a in pieces:
        assert a.shape[1] % tm == 0
        starts.append(s)
        s += a.shape[1] // tm
    ni = s
    npc = len(pieces)

    def body(*refs):
        b_ref, o_ref = refs[npc], refs[npc + 1]
        i = pl.program_id(0)
        for p in range(npc):
            @pl.when((i >= starts[p]) & (i < starts[p] + pieces[p].shape[1] // tm))
            def _(p=p):
                o_ref[...] = lax.dot_general(refs[p][...].astype(BF16), b_ref[...].astype(BF16), _DIMS["tn"],
                                             preferred_element_type=F32).astype(out_dtype)

    in_specs = []
    for p, a in enumerate(pieces):
        np_ = a.shape[1] // tm
        in_specs.append(pl.BlockSpec((k, tm), lambda i, j, p=p, np_=np_: (0, jnp.clip(i - starts[p], 0, np_ - 1))))
    in_specs.append(pl.BlockSpec((k, tn), lambda i, j: (0, j)))
    return _pcall(
        body,
        name=name,
        out_shape=jax.ShapeDtypeStruct((ni * tm, n), out_dtype),
        grid=(ni, n // tn),
        in_specs=in_specs,
        out_specs=pl.BlockSpec((tm, tn), lambda i, j: (i, j)),
        compiler_params=_cparams(("parallel", "parallel")),
    )(*pieces, b)


def _mm_up_gw(z2, du3, nsh, name, tm=1024):
    t, d = z2.shape
    f = du3.shape[2]
    half = nsh // 2
    c = f // half
    tm = _tile(d, tm)
    return _mm_call(z2, du3, mode="tn", grid=(d // tm, nsh, 1),
                    a_spec=pl.BlockSpec((t, tm), lambda i, j, kk: (0, i)),
                    b_spec=pl.BlockSpec((None, t, c), lambda i, j, kk: (j // half, 0, j % half)),
                    o_spec=pl.BlockSpec((None, tm, c), lambda i, j, kk: (j, i, 0)),
                    out_shape=jax.ShapeDtypeStruct((nsh, d, c), BF16), acc_shape=(tm, c), name=name)


def _rms(x):
    r = lax.rsqrt(jnp.mean(x * x, axis=-1, keepdims=True) + NORM_EPS)
    return x * r, r


def _rms_bwd(dxh, xh, r):
    return r * (dxh - xh * jnp.mean(dxh * xh, axis=-1, keepdims=True))


def _colsum(v):
    return jnp.sum(v, axis=0, keepdims=True)


def _rope(v, c, s1, s2, q):
    w = v.shape[-1]
    return v * c + pltpu.roll(v, w - q, 1) * s1 + pltpu.roll(v, q, 1) * s2


def _rope_t(d, c, s1, s2, q):
    w = d.shape[-1]
    return d * c + pltpu.roll(d * s1, q, 1) + pltpu.roll(d * s2, w - q, 1)


def _norm_mod_fwd(ctx, x, gain, mods):
    tc, d = ctx.shape
    t = x.shape[0]
    rb = min(ROW_BLOCK, tc)
    nbl = t // rb

    def body(ctx_ref, x_ref, g_ref, mod_ref, z_ref):
        i = pl.program_id(0)

        def emit(src, sh, sc):
            xh, _ = _rms(src[...])
            z_ref[...] = ((xh * g_ref[...]) * (1.0 + sc) + sh).astype(BF16)

        @pl.when(i >= nbl)
        def _():
            emit(ctx_ref, mod_ref[2:3, :], mod_ref[3:4, :])

        @pl.when(i < nbl)
        def _():
            emit(x_ref, mod_ref[0:1, :], mod_ref[1:2, :])

    return _pcall(
        body,
        name="norm1_mod_fwd",
        out_shape=jax.ShapeDtypeStruct((tc + t, d), BF16),
        grid=((tc + t) // rb,),
        in_specs=[
            pl.BlockSpec((rb, d), lambda i: (jnp.maximum(i - nbl, 0), 0)),
            pl.BlockSpec((rb, d), lambda i: (jnp.minimum(i, nbl - 1), 0)),
            pl.BlockSpec((1, d), lambda i: (0, 0)),
            pl.BlockSpec((8, d), lambda i: (0, 0)),
        ],
        out_specs=pl.BlockSpec((rb, d), lambda i: (i, 0)),
        compiler_params=_cparams(("arbitrary",)),
    )(ctx, x, gain, mods)


def _norm1_bwd(ctx, x, gain, mods, dz_ctx, dz_lat, dx1):
    tc, d = ctx.shape
    t = x.shape[0]
    rb = min(ROW_BLOCK, tc)
    nbl = t // rb

    def body(ctx_ref, x_ref, g_ref, mod_ref, dzc_ref, dzl_ref, dx1_ref, gx_ref, st_ref):
        i = pl.program_id(0)

        @pl.when(i == 0)
        def _():
            st_ref[...] = jnp.zeros_like(st_ref)

        def common(src, dz, sc, row_sh, row_sc):
            xh, r = _rms(src[...])
            g = g_ref[...]
            dxn = dz * (1.0 + sc)
            st_ref[row_sh:row_sh + 1, :] += _colsum(dz)
            st_ref[row_sc:row_sc + 1, :] += _colsum(dz * (xh * g))
            st_ref[2:3, :] += _colsum(dxn * xh)
            return _rms_bwd(dxn * g, xh, r)

        @pl.when(i >= nbl)
        def _():
            common(ctx_ref, dzc_ref[...], mod_ref[3:4, :], 3, 4)

        @pl.when(i < nbl)
        def _():
            gx_ref[...] = dx1_ref[...] + common(x_ref, dzl_ref[...], mod_ref[1:2, :], 0, 1)

    lat = lambda i: (jnp.minimum(i, nbl - 1), 0)
    cix = lambda i: (jnp.maximum(i - nbl, 0), 0)
    return _pcall(
        body,
        name="norm1_mod_bwd",
        out_shape=[jax.ShapeDtypeStruct((t, d), F32), jax.ShapeDtypeStruct((8, d), F32)],
        grid=((tc + t) // rb,),
        in_specs=[
            pl.BlockSpec((rb, d), cix),
            pl.BlockSpec((rb, d), lat),
            pl.BlockSpec((1, d), lambda i: (0, 0)),
            pl.BlockSpec((8, d), lambda i: (0, 0)),
            pl.BlockSpec((rb, d), cix),
            pl.BlockSpec((rb, d), lat),
            pl.BlockSpec((rb, d), lat),
        ],
        out_specs=[pl.BlockSpec((rb, d), lat), pl.BlockSpec((8, d), lambda i: (0, 0))],
        compiler_params=_cparams(("arbitrary",)),
    )(ctx, x, gain, mods, dz_ctx, dz_lat, dx1)


def _key_prep_fwd(kv, kv_gain, kb_gain, tabs):
    ta, wkv = kv.shape
    kvl = MLA_KV_LORA
    nb = GQA_KV_HEADS * GQA_HEAD_DIM
    rb = ROW_BLOCK if ta % ROW_BLOCK == 0 else LANE
    hd = GQA_HEAD_DIM

    def body(kv_ref, g_ref, gb_ref, ca, s1a, s2a, cb, s1b, s2b, kin_ref, kb_ref, vb_ref):
        xh, _ = _rms(kv_ref[:, 0:kvl])
        kin_ref[:, 0:kvl] = (xh * g_ref[...]).astype(BF16)
        kpe = kv_ref[:, kvl + 2 * nb:kvl + 2 * nb + LANE]
        kin_ref[:, kvl:kvl + LANE] = _rope(kpe, ca[...], s1a[...], s2a[...], MLA_ROPE // 4).astype(BF16)
        for h in range(GQA_KV_HEADS):
            nh, _ = _rms(kv_ref[:, kvl + h * hd:kvl + (h + 1) * hd])
            kb_ref[:, h * hd:(h + 1) * hd] = _rope(nh * gb_ref[...], cb[...], s1b[...], s2b[...], hd // 4).astype(BF16)
        vb_ref[...] = kv_ref[:, kvl + nb:kvl + 2 * nb].astype(BF16)

    row = lambda w: pl.BlockSpec((rb, w), lambda i: (i, 0))
    fix = lambda w: pl.BlockSpec((1, w), lambda i: (0, 0))
    return _pcall(
        body,
        name="key_prep_fwd",
        out_shape=[jax.ShapeDtypeStruct((ta, kvl + LANE), BF16), jax.ShapeDtypeStruct((ta, nb), BF16),
                   jax.ShapeDtypeStruct((ta, nb), BF16)],
        grid=(ta // rb,),
        in_specs=[row(wkv), fix(kvl), fix(hd)] + [row(LANE)] * 3 + [row(hd)] * 3,
        out_specs=[row(kvl + LANE), row(nb), row(nb)],
        compiler_params=_cparams(("parallel",)),
    )(kv, kv_gain, kb_gain, *tabs)


def _key_prep_bwd(kv, kv_gain, kb_gain, tabs, dkin, dkb, dvb):
    ta, wkv = kv.shape
    kvl = MLA_KV_LORA
    nb = GQA_KV_HEADS * GQA_HEAD_DIM
    rb = ROW_BLOCK if ta % ROW_BLOCK == 0 else LANE
    hd = GQA_HEAD_DIM

    def body(kv_ref, g_ref, gb_ref, ca, s1a, s2a, cb, s1b, s2b, dkin_ref, dkb_ref, dvb_ref, dkv_ref, st_ref, stb_ref):
        @pl.when(pl.program_id(0) == 0)
        def _():
            st_ref[...] = jnp.zeros_like(st_ref)
            stb_ref[...] = jnp.zeros_like(stb_ref)

        xh, r = _rms(kv_ref[:, 0:kvl])
        dn = dkin_ref[:, 0:kvl]
        st_ref[0:1, :] += _colsum(dn * xh)
        dkv_ref[:, 0:kvl] = _rms_bwd(dn * g_ref[...], xh, r).astype(BF16)
        dpe = _rope_t(dkin_ref[:, kvl:kvl + LANE], ca[...], s1a[...], s2a[...], MLA_ROPE // 4)
        dkv_ref[:, kvl + 2 * nb:kvl + 2 * nb + LANE] = dpe.astype(BF16)
        for h in range(GQA_KV_HEADS):
            nh, rh = _rms(kv_ref[:, kvl + h * hd:kvl + (h + 1) * hd])
            dn_h = _rope_t(dkb_ref[:, h * hd:(h + 1) * hd], cb[...], s1b[...], s2b[...], hd // 4)
            stb_ref[0:1, :] += _colsum(dn_h * nh)
            dkv_ref[:, kvl + h * hd:kvl + (h + 1) * hd] = _rms_bwd(dn_h * gb_ref[...], nh, rh).astype(BF16)
        dkv_ref[:, kvl + nb:kvl + 2 * nb] = dvb_ref[...].astype(BF16)

    row = lambda w: pl.BlockSpec((rb, w), lambda i: (i, 0))
    fix = lambda w: pl.BlockSpec((1, w), lambda i: (0, 0))
    return _pcall(
        body,
        name="key_prep_bwd",
        out_shape=[jax.ShapeDtypeStruct((ta, wkv), BF16), jax.ShapeDtypeStruct((8, kvl), F32),
                   jax.ShapeDtypeStruct((8, hd), F32)],
        grid=(ta // rb,),
        in_specs=[row(wkv), fix(kvl), fix(hd)] + [row(LANE)] * 3 + [row(hd)] * 3 + [row(kvl + LANE), row(nb), row(nb)],
        out_specs=[row(wkv), pl.BlockSpec((8, kvl), lambda i: (0, 0)), pl.BlockSpec((8, hd), lambda i: (0, 0))],
        compiler_params=_cparams(("arbitrary",)),
    )(kv, kv_gain, kb_gain, *tabs, dkin, dkb, dvb)


def _q_prep_fwd(qg, q_gain, qb_gain, tabs, qscale):
    t = qg.shape[0]
    ql = MLA_Q_LORA
    hd = GQA_HEAD_DIM
    hb = GQA_HEADS * hd
    rb = min(ROW_BLOCK, t)

    def body(q_ref, g_ref, gb_ref, cb, s1b, s2b, cqn_ref, qb_ref):
        xh, _ = _rms(q_ref[:, 0:ql])
        cqn_ref[...] = (xh * g_ref[...]).astype(BF16)
        for h in range(GQA_HEADS):
            nh, _ = _rms(q_ref[:, ql + h * hd:ql + (h + 1) * hd])
            qh = _rope(nh * gb_ref[...], cb[...], s1b[...], s2b[...], hd // 4)
            qb_ref[:, h * hd:(h + 1) * hd] = (qh * qscale).astype(BF16)

    row = lambda w: pl.BlockSpec((rb, w), lambda i: (i, 0))
    fix = lambda w: pl.BlockSpec((1, w), lambda i: (0, 0))
    return _pcall(
        body,
        name="q_prep_fwd",
        out_shape=[jax.ShapeDtypeStruct((t, ql), BF16), jax.ShapeDtypeStruct((t, hb), BF16)],
        grid=(t // rb,),
        in_specs=[row(ql + hb), fix(ql), fix(hd)] + [row(hd)] * 3,
        out_specs=[row(ql), row(hb)],
        compiler_params=_cparams(("parallel",)),
    )(qg, q_gain, qb_gain, *tabs)


def _q_prep_bwd(qg, q_gain, qb_gain, tabs, dcqn, dqb, wpad, qscale):
    t = qg.shape[0]
    ql = MLA_Q_LORA
    hd = GQA_HEAD_DIM
    hb = GQA_HEADS * hd
    rb = min(ROW_BLOCK, t)

    def body(q_ref, g_ref, gb_ref, cb, s1b, s2b, dcqn_ref, dqb_ref, dq_ref, st_ref, stb_ref):
        @pl.when(pl.program_id(0) == 0)
        def _():
            st_ref[...] = jnp.zeros_like(st_ref)
            stb_ref[...] = jnp.zeros_like(stb_ref)

        xh, r = _rms(q_ref[:, 0:ql])
        dn = dcqn_ref[...]
        st_ref[0:1, :] += _colsum(dn * xh)
        dq_ref[:, 0:ql] = _rms_bwd(dn * g_ref[...], xh, r).astype(BF16)
        for h in range(GQA_HEADS):
            nh, rh = _rms(q_ref[:, ql + h * hd:ql + (h + 1) * hd])
            dn_h = _rope_t(dqb_ref[:, h * hd:(h + 1) * hd] * qscale, cb[...], s1b[...], s2b[...], hd // 4)
            stb_ref[0:1, :] += _colsum(dn_h * nh)
            dq_ref[:, ql + h * hd:ql + (h + 1) * hd] = _rms_bwd(dn_h * gb_ref[...], nh, rh).astype(BF16)
        if wpad:
            dq_ref[:, ql + hb:ql + hb + wpad] = jnp.zeros((rb, wpad), BF16)

    row = lambda w: pl.BlockSpec((rb, w), lambda i: (i, 0))
    fix = lambda w: pl.BlockSpec((1, w), lambda i: (0, 0))
    return _pcall(
        body,
        name="q_prep_bwd",
        out_shape=[jax.ShapeDtypeStruct((t, ql + hb + wpad), BF16), jax.ShapeDtypeStruct((8, ql), F32),
                   jax.ShapeDtypeStruct((8, hd), F32)],
        grid=(t // rb,),
        in_specs=[row(ql + hb), fix(ql), fix(hd)] + [row(hd)] * 3 + [row(ql), row(hb)],
        out_specs=[row(ql + hb + wpad), pl.BlockSpec((8, ql), lambda i: (0, 0)), pl.BlockSpec((8, hd), lambda i: (0, 0))],
        compiler_params=_cparams(("arbitrary",)),
    )(qg, q_gain, qb_gain, *tabs, dcqn, dqb)


def _rope_a(v, tabs, transpose, out_dtype, name, qscale):
    t, w = v.shape
    rb = min(ROW_BLOCK, t)
    fn = _rope_t if transpose else _rope

    def body(v_ref, c, s1, s2, o_ref):
        for h in range(w // MLA_SLOT):
            sl = slice(h * MLA_SLOT, (h + 1) * MLA_SLOT)
            o_ref[:, sl] = (fn(v_ref[:, sl].astype(F32), c[...], s1[...], s2[...], MLA_ROPE // 4) * qscale).astype(out_dtype)

    row = lambda ww: pl.BlockSpec((rb, ww), lambda i: (i, 0))
    return _pcall(
        body,
        name=name,
        out_shape=jax.ShapeDtypeStruct((t, w), out_dtype),
        grid=(t // rb,),
        in_specs=[row(w)] + [row(MLA_SLOT)] * 3,
        out_specs=row(w),
        compiler_params=_cparams(("parallel",)),
    )(v, *tabs)


def _merge_fwd(pa, pb, qg, gate_blk):
    t, d = pa.shape
    rb = min(ROW_BLOCK, t)

    def body(pa_ref, pb_ref, ga_ref, gb_ref, o_ref):
        o_ref[...] = (jax.nn.sigmoid(ga_ref[...]) * pa_ref[...].astype(F32)
                      + jax.nn.sigmoid(gb_ref[...]) * pb_ref[...].astype(F32)).astype(BF16)

    row = pl.BlockSpec((rb, d), lambda i: (i, 0))
    return _pcall(
        body,
        name="merge_fwd",
        out_shape=jax.ShapeDtypeStruct((t, d), BF16),
        grid=(t // rb,),
        in_specs=[row, row, pl.BlockSpec((rb, d), lambda i: (i, gate_blk)), pl.BlockSpec((rb, d), lambda i: (i, gate_blk + 1))],
        out_specs=row,
        compiler_params=_cparams(("parallel",)),
    )(pa, pb, qg, qg)


def _merge_bwd(dm, pa, pb, qg, gate_blk):
    t, d = pa.shape
    rb = min(ROW_BLOCK, t)

    def body(dm_ref, pa_ref, pb_ref, ga_ref, gb_ref, dpa_ref, dpb_ref, dg_ref):
        dmv = dm_ref[...].astype(F32)
        sa = jax.nn.sigmoid(ga_ref[...])
        sb = jax.nn.sigmoid(gb_ref[...])
        dpa_ref[...] = (dmv * sa).astype(BF16)
        dpb_ref[...] = (dmv * sb).astype(BF16)
        dg_ref[:, 0:d] = (dmv * pa_ref[...].astype(F32) * (sa * (1.0 - sa))).astype(BF16)
        dg_ref[:, d:2 * d] = (dmv * pb_ref[...].astype(F32) * (sb * (1.0 - sb))).astype(BF16)

    row = pl.BlockSpec((rb, d), lambda i: (i, 0))
    return _pcall(
        body,
        name="merge_bwd",
        out_shape=[jax.ShapeDtypeStruct((t, d), BF16), jax.ShapeDtypeStruct((t, d), BF16),
                   jax.ShapeDtypeStruct((t, 2 * d), BF16)],
        grid=(t // rb,),
        in_specs=[row, row, row, pl.BlockSpec((rb, d), lambda i: (i, gate_blk)), pl.BlockSpec((rb, d), lambda i: (i, gate_blk + 1))],
        out_specs=[row, row, pl.BlockSpec((rb, 2 * d), lambda i: (i, 0))],
        compiler_params=_cparams(("parallel",)),
    )(dm, pa, pb, qg, qg)


def _resid_norm_mod(x, branch, gain, mods, name):
    t, d = x.shape
    rb = min(ROW_BLOCK, t)

    def body(x_ref, b_ref, g_ref, mod_ref, x1_ref, z_ref):
        x1 = x_ref[...] + mod_ref[0:1, :] * b_ref[...]
        x1_ref[...] = x1
        xh, _ = _rms(x1)
        z_ref[...] = ((xh * g_ref[...]) * (1.0 + mod_ref[2:3, :]) + mod_ref[1:2, :]).astype(BF16)

    row = pl.BlockSpec((rb, d), lambda i: (i, 0))
    return _pcall(
        body,
        name=name,
        out_shape=[jax.ShapeDtypeStruct((t, d), F32), jax.ShapeDtypeStruct((t, d), BF16)],
        grid=(t // rb,),
        in_specs=[row, row, pl.BlockSpec((1, d), lambda i: (0, 0)), pl.BlockSpec((8, d), lambda i: (0, 0))],
        out_specs=[row, row],
        compiler_params=_cparams(("parallel",)),
    )(x, branch, gain, mods)


def _norm2_bwd(x1, attn, gain, mods, dz2, dx2):
    t, d = x1.shape
    rb = min(ROW_BLOCK, t)

    def body(x1_ref, at_ref, g_ref, mod_ref, dz_ref, dx2_ref, dx1_ref, da_ref, st_ref):
        @pl.when(pl.program_id(0) == 0)
        def _():
            st_ref[...] = jnp.zeros_like(st_ref)

        xh, r = _rms(x1_ref[...])
        g = g_ref[...]
        dz = dz_ref[...].astype(F32)
        dxn = dz * (1.0 + mod_ref[1:2, :])
        st_ref[0:1, :] += _colsum(dz)
        st_ref[1:2, :] += _colsum(dz * (xh * g))
        st_ref[2:3, :] += _colsum(dxn * xh)
        dx1 = dx2_ref[...] + _rms_bwd(dxn * g, xh, r)
        dx1_ref[...] = dx1
        st_ref[3:4, :] += _colsum(dx1 * at_ref[...])
        da_ref[...] = (dx1 * mod_ref[0:1, :]).astype(BF16)

    row = pl.BlockSpec((rb, d), lambda i: (i, 0))
    return _pcall(
        body,
        name="norm2_mod_bwd",
        out_shape=[jax.ShapeDtypeStruct((t, d), F32), jax.ShapeDtypeStruct((t, d), BF16), jax.ShapeDtypeStruct((8, d), F32)],
        grid=(t // rb,),
        in_specs=[row, row, pl.BlockSpec((1, d), lambda i: (0, 0)), pl.BlockSpec((8, d), lambda i: (0, 0)), row, row],
        out_specs=[row, row, pl.BlockSpec((8, d), lambda i: (0, 0))],
        compiler_params=_cparams(("arbitrary",)),
    )(x1, attn, gain, mods, dz2, dx2)


def _final_loss(x1, ffn, gain, mods, target):
    t, d = x1.shape
    rb = min(ROW_BLOCK, t)
    nb = t // rb

    def body(x1_ref, f_ref, g_ref, mod_ref, tg_ref, dx2_ref, df_ref, st_ref):
        i = pl.program_id(0)

        @pl.when(i == 0)
        def _():
            st_ref[...] = jnp.zeros_like(st_ref)

        ffn_v = f_ref[...]
        g2 = mod_ref[0:1, :]
        x2 = x1_ref[...] + g2 * ffn_v
        xh, r = _rms(x2)
        g = g_ref[...]
        err = xh * g - tg_ref[...]
        st_ref[2:3, :] += _colsum(err * err) * (0.5 / d)
        dy = err * (1.0 / d)
        st_ref[0:1, :] += _colsum(dy * xh)
        dx2 = _rms_bwd(dy * g, xh, r)
        dx2_ref[...] = dx2
        st_ref[1:2, :] += _colsum(dx2 * ffn_v)
        df_ref[...] = (dx2 * g2).astype(BF16)

        @pl.when(i == nb - 1)
        def _():
            st_ref[3:4, :] = jnp.broadcast_to(jnp.sum(st_ref[2:3, :], axis=-1, keepdims=True), (1, d))

    row = pl.BlockSpec((rb, d), lambda i: (i, 0))
    return _pcall(
        body,
        name="final_norm_loss",
        out_shape=[jax.ShapeDtypeStruct((t, d), F32), jax.ShapeDtypeStruct((t, d), BF16), jax.ShapeDtypeStruct((8, d), F32)],
        grid=(nb,),
        in_specs=[row, row, pl.BlockSpec((1, d), lambda i: (0, 0)), pl.BlockSpec((8, d), lambda i: (0, 0)), row],
        out_specs=[row, row, pl.BlockSpec((8, d), lambda i: (0, 0))],
        compiler_params=_cparams(("arbitrary",)),
    )(x1, ffn, gain, mods, target)


def _row_ends(shape):
    rows = lax.broadcasted_iota(jnp.int32, shape, 0)
    return rows == 0, rows == shape[0] - 1


def _shift_dn(v, first):
    return jnp.where(first, 0.0, pltpu.roll(v, 1, 0))


def _shift_up(v, last):
    return jnp.where(last, 0.0, pltpu.roll(v, v.shape[0] - 1, 0))


def _conv_fwd(u, cw, cb):
    t, f2 = u.shape
    f = f2 // 2
    cbk = _tile(f, 256)
    nf = f // cbk

    def body(ua_ref, ub_ref, cwa_ref, cwb_ref, cba_ref, cbb_ref, h_ref, uc_ref):
        first, last = _row_ends((t, cbk))
        outs = []
        for u_ref, cw_ref, cb_ref in ((ua_ref, cwa_ref, cba_ref), (ub_ref, cwb_ref, cbb_ref)):
            uu, cwv = u_ref[...].astype(F32), cw_ref[...]
            outs.append(cb_ref[...] + cwv[0:1, :] * _shift_dn(uu, first) + cwv[1:2, :] * uu
                        + cwv[2:3, :] * _shift_up(uu, last))
        a, b = outs
        uc_ref[0] = a.astype(BF16)
        uc_ref[1] = b.astype(BF16)
        h_ref[...] = (a * jax.nn.sigmoid(a) * b).astype(BF16)

    ca = lambda r: pl.BlockSpec((r, cbk), lambda j: (0, j))
    cbs = lambda r: pl.BlockSpec((r, cbk), lambda j: (0, nf + j))
    return _pcall(
        body,
        name="conv_gate_fwd",
        out_shape=[jax.ShapeDtypeStruct((t, f), BF16), jax.ShapeDtypeStruct((2, t, f), BF16)],
        grid=(nf,),
        in_specs=[ca(t), cbs(t), ca(3), cbs(3), ca(1), cbs(1)],
        out_specs=[ca(t), pl.BlockSpec((2, t, cbk), lambda j: (0, 0, j))],
        compiler_params=_cparams(("parallel",)),
    )(u, u, cw, cw, cb, cb)


def _conv_bwd(u, uc, cw, dh):
    t, f2 = u.shape
    f = f2 // 2
    cbk = _tile(f, 256)
    nf = f // cbk

    def body(ua_ref, ub_ref, uc_ref, cwa_ref, cwb_ref, dh_ref, du_ref, dcw_ref, dcb_ref):
        first, last = _row_ends((t, cbk))
        a, b = uc_ref[0].astype(F32), uc_ref[1].astype(F32)
        dh_v = dh_ref[...].astype(F32)
        sg = jax.nn.sigmoid(a)
        db = dh_v * (a * sg)
        da = dh_v * b * (sg * (1.0 + a * (1.0 - sg)))
        for idx, (dv, u_ref, cw_ref) in enumerate(((da, ua_ref, cwa_ref), (db, ub_ref, cwb_ref))):
            uu, cwv = u_ref[...].astype(F32), cw_ref[...]
            up, dn = _shift_up(dv, last), _shift_dn(dv, first)
            dcb_ref[idx] = _colsum(dv)
            dcw_ref[idx, 0:1, :] = _colsum(up * uu)
            dcw_ref[idx, 1:2, :] = _colsum(dv * uu)
            dcw_ref[idx, 2:3, :] = _colsum(dn * uu)
            du_ref[idx] = (cwv[0:1, :] * up + cwv[1:2, :] * dv + cwv[2:3, :] * dn).astype(BF16)

    ca = lambda r: pl.BlockSpec((r, cbk), lambda j: (0, j))
    cbs = lambda r: pl.BlockSpec((r, cbk), lambda j: (0, nf + j))
    o3 = lambda r: pl.BlockSpec((2, r, cbk), lambda j: (0, 0, j))
    return _pcall(
        body,
        name="conv_gate_bwd",
        out_shape=[jax.ShapeDtypeStruct((2, t, f), BF16), jax.ShapeDtypeStruct((2, 3, f), F32),
                   jax.ShapeDtypeStruct((2, 1, f), F32)],
        grid=(nf,),
        in_specs=[ca(t), cbs(t), o3(t), ca(3), cbs(3), ca(t)],
        out_specs=[o3(t), o3(3), o3(1)],
        compiler_params=_cparams(("parallel",)),
    )(u, u, uc, cw, cw, dh)


def _attention_fwd(q, kk, vv, *, hq, hkv, dk, dv, k_blk0, v_blk0, name):
    t = q.shape[0]
    tk = kk.shape[0]
    g_sz = hq // hkv
    tq = min(ATT_Q_BLOCK_FWD, t)

    def body(q_ref, k_ref, v_ref, o_ref, lse_ref):
        k = k_ref[...]
        v = v_ref[...]
        for j in range(g_sz):
            s = lax.dot_general(q_ref[:, j * dk:(j + 1) * dk], k, _DIMS["nt"], preferred_element_type=F32)
            m = jnp.max(s, axis=-1, keepdims=True)
            p = jnp.exp2(s - m)
            l = jnp.sum(p, axis=-1, keepdims=True)
            o = jnp.dot(p.astype(BF16), v, preferred_element_type=F32) / l
            o_ref[:, j * dv:(j + 1) * dv] = o.astype(BF16)
            lse_ref[0, :, j:j + 1] = m + jnp.log2(l)

    return _pcall(
        body,
        name=name,
        out_shape=[jax.ShapeDtypeStruct((t, hq * dv), BF16), jax.ShapeDtypeStruct((hkv, t, g_sz), F32)],
        grid=(hkv, t // tq),
        in_specs=[
            pl.BlockSpec((tq, g_sz * dk), lambda g, i: (i, g)),
            pl.BlockSpec((tk, dk), lambda g, i: (0, k_blk0 + g)),
            pl.BlockSpec((tk, dv), lambda g, i: (0, v_blk0 + g)),
        ],
        out_specs=[
            pl.BlockSpec((tq, g_sz * dv), lambda g, i: (i, g)),
            pl.BlockSpec((1, tq, g_sz), lambda g, i: (g, i, 0)),
        ],
        compiler_params=_cparams(("parallel", "parallel")),
    )(q, kk, vv)


def _attention_bwd(q, kk, vv, do, lse, *, hq, hkv, dk, dv, k_blk0, v_blk0, name):
    t = q.shape[0]
    tk = kk.shape[0]
    g_sz = hq // hkv
    tq = min(ATT_Q_BLOCK, t)

    def body(q_ref, k_ref, v_ref, do_ref, lse_ref, dq_ref, dk_ref, dv_ref):
        @pl.when(pl.program_id(1) == 0)
        def _():
            dk_ref[...] = jnp.zeros_like(dk_ref)
            dv_ref[...] = jnp.zeros_like(dv_ref)

        k = k_ref[...]
        v = v_ref[...]
        for j in range(g_sz):
            qj = q_ref[:, j * dk:(j + 1) * dk]
            doj = do_ref[:, j * dv:(j + 1) * dv]
            s = lax.dot_general(qj, k, _DIMS["nt"], preferred_element_type=F32)
            p = jnp.exp2(s - lse_ref[0, :, j:j + 1])
            dp = lax.dot_general(doj, v, _DIMS["nt"], preferred_element_type=F32)
            ds = (p * (dp - jnp.sum(p * dp, axis=-1, keepdims=True))).astype(BF16)
            dv_ref[...] += lax.dot_general(p.astype(BF16), doj, _DIMS["tn"], preferred_element_type=F32)
            dk_ref[...] += lax.dot_general(ds, qj, _DIMS["tn"], preferred_element_type=F32)
            dq_ref[:, j * dk:(j + 1) * dk] = jnp.dot(ds, k, preferred_element_type=F32)

        @pl.when(pl.program_id(1) == t // tq - 1)
        def _():
            dk_ref[...] *= LN2

    return _pcall(
        body,
        name=name,
        out_shape=[jax.ShapeDtypeStruct((t, hq * dk), F32), jax.ShapeDtypeStruct((tk, hkv * dk), F32),
                   jax.ShapeDtypeStruct((tk, hkv * dv), F32)],
        grid=(hkv, t // tq),
        in_specs=[
            pl.BlockSpec((tq, g_sz * dk), lambda g, i: (i, g)),
            pl.BlockSpec((tk, dk), lambda g, i: (0, k_blk0 + g)),
            pl.BlockSpec((tk, dv), lambda g, i: (0, v_blk0 + g)),
            pl.BlockSpec((tq, g_sz * dv), lambda g, i: (i, g)),
            pl.BlockSpec((1, tq, g_sz), lambda g, i: (g, i, 0)),
        ],
        out_specs=[
            pl.BlockSpec((tq, g_sz * dk), lambda g, i: (i, g)),
            pl.BlockSpec((tk, dk), lambda g, i: (0, g)),
            pl.BlockSpec((tk, dv), lambda g, i: (0, g)),
        ],
        compiler_params=_cparams(("parallel", "arbitrary")),
    )(q, kk, vv, do, lse)


def _silu(v):
    return v * jax.nn.sigmoid(v)


def _ada_fwd(conds, w_ada, b_ada_shard):
    r, d = conds.shape
    n = w_ada.shape[1]
    tn = _tile(n, 512)

    def body(c_ref, w_ref, b_ref, o_ref):
        s = _silu(c_ref[...]).astype(BF16)
        o_ref[...] = jnp.dot(s, w_ref[...].astype(BF16), preferred_element_type=F32) + b_ref[...]

    return _pcall(
        body,
        name="ada_fwd",
        out_shape=jax.ShapeDtypeStruct((r, n), F32),
        grid=(n // tn,),
        in_specs=[pl.BlockSpec((r, d), lambda j: (0, 0)), pl.BlockSpec((d, tn), lambda j: (0, j)),
                  pl.BlockSpec((1, tn), lambda j: (0, j))],
        out_specs=pl.BlockSpec((r, tn), lambda j: (0, j)),
        compiler_params=_cparams(("parallel",)),
    )(conds, w_ada, b_ada_shard)


def _cctx_partial(da16_shard, w_ada, c_ctx_row):
    d, n = w_ada.shape
    td = _tile(d, 512)

    def body(g_ref, w_ref, c_ref, o_ref):
        ds = lax.dot_general(g_ref[8:16, :].astype(BF16), w_ref[...].astype(BF16), _DIMS["nt"],
                             preferred_element_type=F32)
        cv = c_ref[...]
        sg = jax.nn.sigmoid(cv)
        o_ref[...] = ds * (sg * (1.0 + cv * (1.0 - sg)))

    return _pcall(
        body,
        name="cctx_partial",
        out_shape=jax.ShapeDtypeStruct((8, d), F32),
        grid=(d // td,),
        in_specs=[pl.BlockSpec((16, n), lambda j: (0, 0)), pl.BlockSpec((td, n), lambda j: (j, 0)),
                  pl.BlockSpec((1, td), lambda j: (0, j))],
        out_specs=pl.BlockSpec((8, td), lambda j: (0, j)),
        compiler_params=_cparams(("parallel",)),
    )(da16_shard, w_ada, c_ctx_row)


def _sum_parts(parts):
    p, _, n = parts.shape

    def body(p_ref, o_ref):
        acc = p_ref[0]
        for s in range(1, p):
            acc = acc + p_ref[s]
        o_ref[...] = acc

    return _pcall(
        body,
        name="sum_parts",
        out_shape=jax.ShapeDtypeStruct((1, n), F32),
        in_specs=[pl.BlockSpec(memory_space=pltpu.VMEM)],
        out_specs=pl.BlockSpec(memory_space=pltpu.VMEM),
    )(parts)


def _adam_math(w, g, m, v):
    m2 = ADAM_B1 * m + (1.0 - ADAM_B1) * g
    v2 = ADAM_B2 * v + (1.0 - ADAM_B2) * jnp.square(g)
    m_hat = m2 / (1.0 - ADAM_B1 ** ADAM_STEP)
    v_hat = v2 / (1.0 - ADAM_B2 ** ADAM_STEP)
    delta = -ADAM_LR * (m_hat / (jnp.sqrt(v_hat) + ADAM_EPS) + ADAM_WD * w)
    return delta, m2, v2


def _adamw(parts, w, m, v, name):
    p, r, c = parts.shape
    block_elems = 1 << 18
    rb, cb = _tile(r, max(8, block_elems // c // 8 * 8), 8), c
    if rb * c < block_elems // 4 and r * c > block_elems:
        rb, cb = r, _tile(c, max(LANE, block_elems // r // LANE * LANE))

    def body(p_ref, w_ref, m_ref, v_ref, g_ref, d_ref, m2_ref, v2_ref):
        g = p_ref[0].astype(F32)
        for s in range(1, p):
            g = g + p_ref[s].astype(F32)
        g_ref[...] = g
        d_ref[...], m2_ref[...], v2_ref[...] = _adam_math(w_ref[...], g, m_ref[...], v_ref[...])

    if w.ndim == 3:
        blk = pl.BlockSpec((None, rb, cb), lambda i, j: (0, i, j))
    else:
        blk = pl.BlockSpec((rb, cb), lambda i, j: (i, j))
    return _pcall(
        body,
        name=name,
        out_shape=[jax.ShapeDtypeStruct(w.shape, F32)] * 4,
        grid=(r // rb, c // cb),
        in_specs=[pl.BlockSpec((p, rb, cb), lambda i, j: (0, i, j)), blk, blk, blk],
        out_specs=[blk] * 4,
        compiler_params=_cparams(("parallel", "parallel")),
    )(parts, w, m, v)


def _adamw_ada(conds, da16, w, m, v):
    d, n = w.shape
    rb = _tile(d, 256, LANE)

    def body(s_ref, da_ref, w_ref, m_ref, v_ref, g_ref, d_ref, m2_ref, v2_ref):
        g = lax.dot_general(_silu(s_ref[...]).astype(BF16), da_ref[...].astype(BF16), _DIMS["tn"],
                            preferred_element_type=F32)
        g_ref[...] = g
        d_ref[...], m2_ref[...], v2_ref[...] = _adam_math(w_ref[...], g, m_ref[...], v_ref[...])

    row = pl.BlockSpec((rb, n), lambda i: (i, 0))
    return _pcall(
        body,
        name="adamw_w_ada",
        out_shape=[jax.ShapeDtypeStruct((d, n), F32)] * 4,
        grid=(d // rb,),
        in_specs=[pl.BlockSpec((16, rb), lambda i: (0, i)), pl.BlockSpec((16, n), lambda i: (0, 0)), row, row, row],
        out_specs=[row] * 4,
        compiler_params=_cparams(("parallel",)),
    )(conds, da16, w, m, v)


def _touch(arrays, name):
    def body(*refs):
        refs[-1][...] = jnp.zeros((8, LANE), F32)

    return _pcall(body, name=name, out_shape=jax.ShapeDtypeStruct((8, LANE), F32),
                  in_specs=[pl.BlockSpec(memory_space=pl.ANY)] * len(arrays),
                  out_specs=pl.BlockSpec(memory_space=pltpu.VMEM))(*arrays)


def _cast_bf16(a, name):
    _, r, c = a.shape
    rb = _tile(r, 512, 8)

    def body(a_ref, o_ref):
        o_ref[...] = a_ref[...].astype(BF16)

    return _pcall(body, name=name, out_shape=jax.ShapeDtypeStruct((r, c), BF16), grid=(r // rb,),
                  in_specs=[pl.BlockSpec((None, rb, c), lambda i: (0, i, 0))],
                  out_specs=pl.BlockSpec((rb, c), lambda i: (i, 0)), compiler_params=_cparams(("parallel",)))(a)


def _rope_tabs(t, rot):
    half, q = rot // 2, rot // 4
    n_rows = t // GRID_W
    row = jnp.repeat(jnp.arange(n_rows, dtype=F32), GRID_W)
    col = jnp.tile(jnp.arange(GRID_W, dtype=F32), n_rows)
    inv_freq = ROPE_THETA ** (-jnp.arange(0, half, 2, dtype=F32) / half)
    ang = jnp.concatenate([row[:, None] * inv_freq, col[:, None] * inv_freq], axis=-1)
    cos, sin = jnp.cos(ang), jnp.sin(ang)
    c0, c1, s0, s1 = cos[:, :q], cos[:, q:], sin[:, :q], sin[:, q:]
    z = jnp.zeros_like(s0)
    return (jnp.concatenate([c0, c0, c1, c1], -1), jnp.concatenate([-s0, z, -s1, z], -1),
            jnp.concatenate([z, s0, z, s1], -1))


def _pad_cols(a, left, total, fill=0.0):
    return jnp.pad(a, ((0, 0), (left, total - left - a.shape[1])), constant_values=fill)


def _with_ctx_rows(tab, tc, fill):
    return jnp.concatenate([tab, jnp.full((tc, tab.shape[1]), fill, F32)], axis=0)


def kernel(x, c, ctx, c_ctx, w_ada, b_ada, norm1_g, w_in, mla_q_norm_g, w_q_up, mla_kv_norm_g, w_kv_up, gqa_q_norm_g, gqa_k_norm_g, w_br_a, w_br_b, w_out, norm2_g, w_up, conv_w, conv_b, w_down, final_norm_g, loss_target, m_c_ctx, m_w_ada, m_b_ada, m_norm1_g, m_w_in, m_mla_q_norm_g, m_w_q_up, m_mla_kv_norm_g, m_w_kv_up, m_gqa_q_norm_g, m_gqa_k_norm_g, m_w_br_a, m_w_br_b, m_w_out, m_norm2_g, m_w_up, m_conv_w, m_conv_b, m_w_down, m_final_norm_g, v_c_ctx, v_w_ada, v_b_ada, v_norm1_g, v_w_in, v_mla_q_norm_g, v_w_q_up, v_mla_kv_norm_g, v_w_kv_up, v_gqa_q_norm_g, v_gqa_k_norm_g, v_w_br_a, v_w_br_b, v_w_out, v_norm2_g, v_w_up, v_conv_w, v_conv_b, v_w_down, v_final_norm_g):
    weights = dict(c_ctx=c_ctx, w_ada=w_ada, b_ada=b_ada, norm1_g=norm1_g, w_in=w_in, mla_q_norm_g=mla_q_norm_g,
                   w_q_up=w_q_up, mla_kv_norm_g=mla_kv_norm_g, w_kv_up=w_kv_up, gqa_q_norm_g=gqa_q_norm_g,
                   gqa_k_norm_g=gqa_k_norm_g, w_br_a=w_br_a, w_br_b=w_br_b, w_out=w_out, norm2_g=norm2_g, w_up=w_up,
                   conv_w=conv_w, conv_b=conv_b, w_down=w_down, final_norm_g=final_norm_g)
    mom_m = dict(c_ctx=m_c_ctx, w_ada=m_w_ada, b_ada=m_b_ada, norm1_g=m_norm1_g, w_in=m_w_in, mla_q_norm_g=m_mla_q_norm_g,
                 w_q_up=m_w_q_up, mla_kv_norm_g=m_mla_kv_norm_g, w_kv_up=m_w_kv_up, gqa_q_norm_g=m_gqa_q_norm_g,
                 gqa_k_norm_g=m_gqa_k_norm_g, w_br_a=m_w_br_a, w_br_b=m_w_br_b, w_out=m_w_out, norm2_g=m_norm2_g,
                 w_up=m_w_up, conv_w=m_conv_w, conv_b=m_conv_b, w_down=m_w_down, final_norm_g=m_final_norm_g)
    mom_v = dict(c_ctx=v_c_ctx, w_ada=v_w_ada, b_ada=v_b_ada, norm1_g=v_norm1_g, w_in=v_w_in, mla_q_norm_g=v_mla_q_norm_g,
                 w_q_up=v_w_q_up, mla_kv_norm_g=v_mla_kv_norm_g, w_kv_up=v_w_kv_up, gqa_q_norm_g=v_gqa_q_norm_g,
                 gqa_k_norm_g=v_gqa_k_norm_g, w_br_a=v_w_br_a, w_br_b=v_w_br_b, w_out=v_w_out, norm2_g=v_norm2_g,
                 w_up=v_w_up, conv_w=v_conv_w, conv_b=v_conv_b, w_down=v_w_down, final_norm_g=v_final_norm_g)
    order = list(weights)

    my_idx = 4 * lax.axis_index("x") + 2 * lax.axis_index("y") + lax.axis_index("c")
    xs, cts, tgt = x[0], ctx[0], loss_target[0]
    t, d = xs.shape
    tc = cts.shape[0]
    ta = t + tc
    kvl, ql = MLA_KV_LORA, MLA_Q_LORA
    nb = GQA_KV_HEADS * GQA_HEAD_DIM
    hb = GQA_HEADS * GQA_HEAD_DIM
    ha = MLA_HEADS
    f2 = w_up.shape[2] * N_DEV
    ff = f2 // 2

    big = ["w_in", "w_q_up", "w_kv_up", "w_br_a", "w_br_b", "w_out", "w_up", "w_down"]
    nw = len(big)
    del nw
    _ORDER_AFTER.clear()
    def tview(a):
        return jnp.transpose(a, (0, 2, 1))

    shards = {"w_in": _cast_bf16(tview(weights["w_in"]), "cast_w_in")}
    c_idx = jnp.reshape(lax.axis_index("c"), (1,)).astype(jnp.int32)

    def gather_start(names, dep):
        shs = [shards[n] for n in names]
        land = [lax.empty((N_DEV,) + s.shape, BF16) for s in shs]
        if dep is not None:
            _after(dep)
        s, r, arrs, tok = _split_start("gather_ici_start_" + names[0], shs + land, _gather_ici_copies(len(names)),
                                       4 * len(names))
        return dict(names=names, s=s, r=r, arrs=arrs, tok=tok)

    def gather_pass(g, after):
        n = len(g["names"])
        arrs = _split_wait("gather_ici_wait_" + g["names"][0], g["s"], g["r"], g["arrs"], _gather_ici_copies(n), after)
        s, r, bufs, tok = _split_start("gather_pass_start_" + g["names"][0], arrs[n:], _gather_pass_copies(n), 3 * n)
        g.update(s2=s, r2=r, bufs=bufs)
        return tok

    def gather_relay(g, after):
        n = len(g["names"])
        bufs = _split_wait("gather_pass_wait_" + g["names"][0], g["s2"], g["r2"], g["bufs"], _gather_pass_copies(n), after)
        s, r, bufs, tok = _split_start("gather_d2d_start_" + g["names"][0], bufs, _gather_d2d_copies(n), n)
        g.update(s3=s, r3=r, bufs=bufs)
        return tok

    def gather_finish(g, after):
        n = len(g["names"])
        bufs = _split_wait("gather_d2d_wait_" + g["names"][0], g["s3"], g["r3"], g["bufs"], _gather_d2d_copies(n), after)
        return dict(zip(g["names"], bufs))

    c_all, cw_all = _all_gather([jnp.pad(c, ((0, 7), (0, 0))), jnp.pad(conv_w[0], ((0, 5), (0, 0)))], "gather_cond")
    conv_w_f = jnp.transpose(cw_all[:, :3, :], (1, 0, 2)).reshape(3, f2)
    conds = jnp.concatenate([c_all[:, 0, :], c_ctx[None, :], jnp.zeros((7, d), F32)], axis=0)
    ncol = w_ada.shape[2]
    b_shard = lax.dynamic_slice_in_dim(b_ada, my_idx * ncol, ncol, axis=1)
    ada_shard = _ada_fwd(conds, w_ada[0], b_shard)
    (ada_all,) = _all_gather([ada_shard], "gather_ada")
    ada = jnp.transpose(ada_all, (1, 0, 2)).reshape(16, N_DEV * ncol)
    lat = lax.dynamic_slice_in_dim(ada, my_idx, 1, axis=0).reshape(6, d)
    cxt = ada[8].reshape(6, d)
    zero2 = jnp.zeros((2, d), F32)
    mods1 = jnp.concatenate([lat[0:2], cxt[0:2], jnp.zeros((4, d), F32)], axis=0)
    mods2 = jnp.concatenate([lat[2:3], lat[3:4], lat[4:5], jnp.zeros((5, d), F32)], axis=0)
    mods2b = jnp.concatenate([lat[2:3], lat[4:5], jnp.zeros((6, d), F32)], axis=0)
    mods3 = jnp.concatenate([lat[5:6], jnp.zeros((7, d), F32)], axis=0)
    del zero2

    g0 = gather_start(["w_in"], ada_all)
    for n in big[1:]:
        _after(g0["tok"])
        shards[n] = _cast_bf16(weights[n], "cast_" + n)

    ca, s1a, s2a = _rope_tabs(t, MLA_ROPE)
    cb_, s1b, s2b = _rope_tabs(t, GQA_HEAD_DIM)
    q_tabs_a = (_pad_cols(jnp.concatenate([jnp.ones((t, MLA_NOPE), F32), ca], 1), 0, MLA_SLOT),
                _pad_cols(s1a, MLA_NOPE, MLA_SLOT), _pad_cols(s2a, MLA_NOPE, MLA_SLOT))
    q_tabs_b = (cb_, s1b, s2b)
    k_tabs = (_with_ctx_rows(_pad_cols(ca, 0, LANE), tc, 1.0), _with_ctx_rows(_pad_cols(s1a, 0, LANE), tc, 0.0),
              _with_ctx_rows(_pad_cols(s2a, 0, LANE), tc, 0.0),
              _with_ctx_rows(cb_, tc, 1.0), _with_ctx_rows(s1b, tc, 0.0), _with_ctx_rows(s2b, tc, 0.0))

    def cols_full(g):
        return jnp.transpose(g, (1, 0, 2)).reshape(g.shape[1], N_DEV * g.shape[2])

    _after(g0["tok"])
    early = _touch([mom_m["w_q_up"], mom_v["w_q_up"]], "touch_moments")
    _after(early, *q_tabs_a, *q_tabs_b, *k_tabs, *[shards[n] for n in big[1:]])
    tok_p0 = gather_pass(g0, mods1)
    g1 = gather_start(["w_q_up", "w_kv_up", "w_br_a", "w_br_b", "w_out"], tok_p0)
    _after(g1["tok"])
    z_all = _norm_mod_fwd(cts, xs, norm1_g, mods1)
    gathered = gather_finish(g0, gather_relay(g0, z_all))
    wt_in = gathered["w_in"].reshape(-1, d)
    o_kpe, o_kb, o_vb = kvl, kvl + MLA_ROPE, kvl + MLA_ROPE + nb
    o_q = o_vb + nb
    o_g = o_q + ql + hb
    wkv_w = kvl + 2 * nb + LANE
    wt_kv_p = jnp.concatenate([wt_in[:kvl], wt_in[o_kb:o_q], wt_in[o_kpe:o_kb],
                               jnp.zeros((LANE - MLA_ROPE, d), BF16)], axis=0)
    q_w = ql + hb
    q_pad = (-q_w) % 512 if d >= 512 else (-q_w) % d
    gate_blk = (q_w + q_pad) // d
    assert (q_w + q_pad) % d == 0
    wt_qg_p = jnp.concatenate([wt_in[o_q:o_g], jnp.zeros((q_pad, d), BF16), wt_in[o_g:]], axis=0)

    kv_all = _mm(z_all, wt_kv_p, "nt", F32, "proj_kv", tm=1152, tn=wkv_w)
    qg = _mm(z_all, wt_qg_p, "nt", F32, "proj_qg", tm=1024, tn=1024, rows=t)
    tok_p1 = gather_pass(g1, qg)
    g2 = gather_start(["w_up"], tok_p1)
    g3 = gather_start(["w_down"], g2["tok"])
    _after(g3["tok"])
    kin, k_b, v_b = _key_prep_fwd(kv_all, mla_kv_norm_g, gqa_k_norm_g, k_tabs)
    sc_a = float((MLA_NOPE + MLA_ROPE) ** -0.5) * LOG2E
    sc_b = float(GQA_HEAD_DIM ** -0.5) * LOG2E
    _after(g3["tok"])
    cqn, q_b = _q_prep_fwd(qg, mla_q_norm_g, gqa_q_norm_g, q_tabs_b, sc_b)
    _after(kin, g3["tok"])
    gathered.update(gather_finish(g1, gather_relay(g1, q_b)))

    wq_f = cols_full(gathered["w_q_up"]).reshape(ql, ha, MLA_NOPE + MLA_ROPE)
    wq_ext = jnp.pad(wq_f, ((0, 0), (0, 0), (0, MLA_SLOT - MLA_NOPE - MLA_ROPE))).reshape(ql, ha * MLA_SLOT)
    wkv_f = cols_full(gathered["w_kv_up"]).reshape(kvl, ha, MLA_NOPE + MLA_V)
    wk_slots = jnp.pad(wkv_f[:, :, :MLA_NOPE], ((0, 0), (0, 0), (0, MLA_SLOT - MLA_NOPE))).reshape(kvl, ha * MLA_SLOT)
    wv_cols = wkv_f[:, :, MLA_NOPE:].reshape(kvl, ha * MLA_V)
    e_slot = jnp.pad(jnp.eye(MLA_ROPE, dtype=BF16),
                     ((0, LANE - MLA_ROPE), (MLA_NOPE, MLA_SLOT - MLA_NOPE - MLA_ROPE)))
    e_rows = jnp.concatenate([jnp.tile(e_slot, (1, ha)), jnp.zeros((LANE, ha * MLA_V), BF16)], axis=1)
    wkv_ext = jnp.concatenate([jnp.concatenate([wk_slots, wv_cols], axis=1), e_rows], axis=0)
    w_bra = cols_full(gathered["w_br_a"])
    w_brb = cols_full(gathered["w_br_b"])
    w_out_f = gathered["w_out"].reshape(d, d)

    kv_a = _mm(kin, wkv_ext, "nn", BF16, "kv_up", tm=1152, tn=1024)
    qa_raw = _mm(cqn, wq_ext, "nn", F32, "q_up", tm=1024, tn=1024)
    q_a = _rope_a(qa_raw, q_tabs_a, False, BF16, "rope_q_fwd", sc_a)
    att_a = dict(hq=ha, hkv=ha, dk=MLA_SLOT, dv=MLA_V, k_blk0=0, v_blk0=ha * MLA_SLOT // MLA_V)
    att_b = dict(hq=GQA_HEADS, hkv=GQA_KV_HEADS, dk=GQA_HEAD_DIM, dv=GQA_HEAD_DIM, k_blk0=0, v_blk0=0)
    o_a, lse_a = _attention_fwd(q_a, kv_a, kv_a, name="attn_a_fwd", **att_a)
    o_b, lse_b = _attention_fwd(q_b, k_b, v_b, name="attn_b_fwd", **att_b)
    _after(o_a)
    _after(gather_pass(g2, o_b))
    pa = _mm(o_a, w_bra, "nn", BF16, "br_a", tm=1024, tn=1024)
    pb = _mm(o_b, w_brb, "nn", BF16, "br_b", tm=1024, tn=1024)
    merged = _merge_fwd(pa, pb, qg, gate_blk)
    attn = _mm(merged, w_out_f, "nn", F32, "w_out", tm=1024, tn=1024)
    x1, z2 = _resid_norm_mod(xs, attn, norm2_g, mods2, "resid_norm2_fwd")
    tok_r2 = gather_relay(g2, z2)
    tok_p3 = gather_pass(g3, tok_r2)
    w_up3 = gather_finish(g2, tok_p3)["w_up"]
    u = _mm_up_fwd(z2, w_up3, "w_up")
    tok_r3 = gather_relay(g3, u)
    _after(tok_r3)
    h, uc = _conv_fwd(u, conv_w_f, conv_b)
    w_down_f = gather_finish(g3, h)["w_down"].reshape(ff, d)
    ffn = _mm(h, w_down_f, "nn", F32, "w_down", tm=1024, tn=1024, tk=2816)

    def to_shards(g):
        return jnp.transpose(g.reshape(g.shape[0], N_DEV, g.shape[1] // N_DEV), (1, 0, 2))

    def reduce_start(tag, names, sends):
        n = len(sends)
        land = [lax.empty((4,) + s.shape[1:], s.dtype) for s in sends]
        s, r, arrs, tok = _split_start("reduce_d2d_start_" + tag, sends + land, _reduce_d2d_copies(n), 4 * n)
        return dict(tag=tag, names=names, s=s, r=r, arrs=arrs, tok=tok)

    def reduce_relay(g, after):
        n = len(g["names"])
        arrs = _split_wait("reduce_d2d_wait_" + g["tag"], g["s"], g["r"], g["arrs"], _reduce_d2d_copies(n), after)
        sums = [_pair_sum(arrs[a], arrs[n + a], c_idx, "pair_sum_" + g["names"][a]) for a in range(n)]
        land = [lax.empty(s.shape, s.dtype) for s in sums]
        s, r, arrs2, tok = _split_start("reduce_ici_start_" + g["tag"], sums + land, _reduce_ici_copies(n), 4 * n)
        g.update(s2=s, r2=r, arrs2=arrs2)
        return tok

    def reduce_finish(g, after):
        n = len(g["names"])
        arrs2 = _split_wait("reduce_ici_wait_" + g["tag"], g["s2"], g["r2"], g["arrs2"], _reduce_ici_copies(n), after)
        return dict(zip(g["names"], arrs2[n:]))

    dx2, dffn, st_fin = _final_loss(x1, ffn, final_norm_g[None, :], mods3, tgt)
    dh = _mm(dffn, w_down_f, "nt", BF16, "d_h", tm=1024, tn=1024)
    g_w_down = _mm(h, dffn, "tn", BF16, "g_w_down", tm=512, tn=1024)
    r_down = reduce_start("down", ["w_down"], [g_w_down.reshape(N_DEV, ff // N_DEV, d)])
    _after(r_down["tok"])
    du3, dcw, dcb = _conv_bwd(u, uc, conv_w_f, dh)
    dz2 = _mm_up_dz(du3, w_up3, "d_z2")
    g_w_up = _mm_up_gw(z2, du3, N_DEV, "g_w_up")
    g_conv_w = jnp.concatenate([dcw[0], dcw[1]], axis=1)
    tok = reduce_relay(r_down, g_w_up)
    _after(tok)
    r_up = reduce_start("up", ["w_up", "conv_w"], [g_w_up, to_shards(jnp.pad(g_conv_w, ((0, 5), (0, 0))))])
    _after(tok, r_up["tok"])
    dx1, dattn, st_n2 = _norm2_bwd(x1, attn, norm2_g, mods2b, dz2, dx2)
    dmerged = _mm(dattn, w_out_f, "nt", BF16, "d_merged", tm=1024, tn=1024)
    g_w_out = _mm(merged, dattn, "tn", BF16, "g_w_out", tm=1024, tn=1024)
    dpa, dpb, dgates = _merge_bwd(dmerged, pa, pb, qg, gate_blk)
    do_a = _mm(dpa, w_bra, "nt", BF16, "d_o_a", tm=1024, tn=1024)
    do_b = _mm(dpb, w_brb, "nt", BF16, "d_o_b", tm=1024, tn=1024)
    g_w_bra = _mm(o_a, dpa, "tn", BF16, "g_w_br_a", tm=1024, tn=1024)
    g_w_brb = _mm(o_b, dpb, "tn", BF16, "g_w_br_b", tm=1024, tn=1024)
    tok = reduce_relay(r_up, g_w_brb)
    _after(tok)
    r_out = reduce_start("out", ["w_out", "w_br_a", "w_br_b"],
                         [g_w_out.reshape(N_DEV, d // N_DEV, d), to_shards(g_w_bra), to_shards(g_w_brb)])
    _after(tok, r_out["tok"])
    dq_a, dk_a, dv_a = _attention_bwd(q_a, kv_a, kv_a, do_a, lse_a, name="attn_a_bwd", **att_a)
    dq_b, dk_b, dv_b = _attention_bwd(q_b, k_b, v_b, do_b, lse_b, name="attn_b_bwd", **att_b)
    _after(reduce_relay(r_out, dv_b))
    dqa_raw = _rope_a(dq_a, q_tabs_a, True, BF16, "rope_q_bwd", sc_a * LN2)
    dcqn = _mm(dqa_raw, wq_ext, "nt", F32, "d_cqn", tm=1024, tn=ql)
    g_wq_ext = _mm(cqn, dqa_raw, "tn", BF16, "g_w_q_up", tm=ql, tn=1024)
    dq_p, st_q, st_qb = _q_prep_bwd(qg, mla_q_norm_g, gqa_q_norm_g, q_tabs_b, dcqn, dq_b, q_pad, sc_b * LN2)
    dkin = _mm_cat_nt([(dk_a, wkv_ext, 0), (dv_a, wkv_ext, ha * MLA_SLOT)], F32, "d_kin", tm=1152, tn=kvl + LANE)
    g_wkv_ext = _mm_cat_tn(kin, [dk_a, dv_a], BF16, "g_w_kv_up", tm=kvl + LANE, tn=min(1024, ha * MLA_V))
    dkv_p, st_kv, st_kb = _key_prep_bwd(kv_all, mla_kv_norm_g, gqa_k_norm_g, k_tabs, dkin, dk_b, dv_b)
    g_wq = g_wq_ext.reshape(ql, ha, MLA_SLOT)[:, :, :MLA_NOPE + MLA_ROPE].reshape(ql, ha * (MLA_NOPE + MLA_ROPE))
    g_wkv = jnp.concatenate([g_wkv_ext[:kvl, :ha * MLA_SLOT].reshape(kvl, ha, MLA_SLOT)[:, :, :MLA_NOPE],
                             g_wkv_ext[:kvl, ha * MLA_SLOT:].reshape(kvl, ha, MLA_V)], axis=2).reshape(kvl, ha * (MLA_NOPE + MLA_V))
    r_qkv = reduce_start("qkv", ["w_q_up", "w_kv_up"], [to_shards(g_wq), to_shards(g_wkv)])
    _after(r_qkv["tok"])
    g_wkv_p = _mm(dkv_p, z_all, "tn", BF16, "g_w_in_kv", tm=wkv_w, tn=1024)
    g_wqg_p = _mm_rows_tn([dq_p, dgates], z_all, BF16, "g_w_in_qg", tm=min(1024, d), tn=1024, rows=t)
    g_wt_in = jnp.concatenate([g_wkv_p[:kvl], g_wkv_p[kvl + 2 * nb:kvl + 2 * nb + MLA_ROPE],
                               g_wkv_p[kvl:kvl + 2 * nb], g_wqg_p[:q_w], g_wqg_p[q_w + q_pad:]], axis=0)
    r_in = reduce_start("in", ["w_in"], [g_wt_in.reshape(N_DEV, -1, d)])
    _after(r_in["tok"])
    qw_p = q_w + q_pad
    dz_lat = _mm_sum_nn([(dq_p, 0, wt_qg_p, 0, qw_p), (dgates, 0, wt_qg_p, qw_p, d), (dgates, d, wt_qg_p, qw_p + d, d),
                         (dkv_p, 0, wt_kv_p, 0, wkv_w)], F32, "d_z_lat", rows=t)
    dz_ctx = _mm(dkv_p, wt_kv_p, "nn", F32, "d_z_ctx", tm=min(ROW_BLOCK, tc), tn=1024, a_row_off=t)
    tok_q = reduce_relay(r_qkv, dz_ctx)
    _after(tok_q)
    grad_x, st_n1 = _norm1_bwd(cts, xs, norm1_g, mods1, dz_ctx, dz_lat, dx1)

    res = {}

    def upd(nm, parts):
        wv, mv, vv = weights[nm], mom_m[nm], mom_v[nm]
        if wv.ndim == 1:
            wv, mv, vv = (a.reshape(1, -1) for a in (wv, mv, vv))
        if nm == "w_in":
            wv, mv, vv = tview(wv), tview(mv), tview(vv)
        outs = _adamw(parts, wv, mv, vv, "adamw_" + nm)
        if nm == "w_in":
            outs = [tview(o_) for o_ in outs]
        res[nm] = [o_.reshape(weights[nm].shape) for o_ in outs]

    d_lat = jnp.concatenate([st_n1[0], st_n1[1], st_n2[3], st_n2[0], st_n2[1], st_fin[1]])
    d_cxt = jnp.concatenate([st_n1[3], st_n1[4], jnp.zeros((4 * d,), F32)])
    small = jnp.concatenate([d_lat, d_cxt, st_n1[2], st_q[0], st_kv[0], st_qb[0], st_kb[0], st_n2[2],
                             jnp.concatenate([dcb[0, 0], dcb[1, 0]]), st_fin[0], st_fin[3, :LANE]])
    n_small = small.shape[0]
    pad_small = (-n_small) % LANE
    (small_all,) = _all_gather([jnp.pad(small, (0, pad_small)).reshape(1, -1)], "gather_small")
    offs = {}
    o = 0
    for nm, ln in (("d_lat", 6 * d), ("d_cxt", 6 * d), ("norm1_g", d), ("mla_q_norm_g", ql), ("mla_kv_norm_g", kvl),
                   ("gqa_q_norm_g", GQA_HEAD_DIM), ("gqa_k_norm_g", GQA_HEAD_DIM), ("norm2_g", d), ("conv_b", f2),
                   ("final_norm_g", d), ("loss", LANE)):
        offs[nm] = (o, ln)
        o += ln

    def part(nm):
        a, ln = offs[nm]
        return small_all[:, :, a:a + ln]

    loss = _sum_parts(part("loss"))[0, 0]
    d_lat_all = part("d_lat")[:, 0, :]
    d_cxt_sum = _sum_parts(part("d_cxt"))
    da16 = jnp.concatenate([d_lat_all, d_cxt_sum, jnp.zeros((7, 6 * d), F32)], axis=0)
    da16_shard = lax.dynamic_slice_in_dim(da16, my_idx * ncol, ncol, axis=1)
    cc_part = _cctx_partial(da16_shard, w_ada[0], c_ctx[None, :])
    (cc_all,) = _all_gather([cc_part], "gather_cctx")
    cc_parts = cc_all[:, 0:1, :]
    tok_i = reduce_relay(r_in, cc_all)

    _after(tok_i)
    for nm in ("norm1_g", "mla_q_norm_g", "mla_kv_norm_g", "gqa_q_norm_g", "gqa_k_norm_g", "norm2_g", "conv_b",
               "final_norm_g"):
        upd(nm, part(nm))
    upd("c_ctx", cc_parts)
    b_parts = jnp.concatenate([d_lat_all[:, None, :], d_cxt_sum[None]], axis=0)
    upd("b_ada", b_parts)
    _after(tok_i)
    outs = _adamw_ada(conds, da16_shard, w_ada[0], m_w_ada[0], v_w_ada[0])
    res["w_ada"] = [o_[None] for o_ in outs]
    last = outs[0]
    done = [last]
    for grp in (r_down, r_up, r_out, r_qkv, r_in):
        _after(*done)
        recv = reduce_finish(grp, last)
        for nm in grp["names"]:
            upd(nm, recv[nm][:, :3, :] if nm == "conv_w" else recv[nm])
            last = res[nm][0]
            done.append(last)

    return (loss, grad_x[None], *[res[n][0] for n in order], *[res[n][1] for n in order],
            *[res[n][2] for n in order], *[res[n][3] for n in order])
```

```python
import functools

import jax
import jax.numpy as jnp
from jax import lax
from jax.experimental import pallas as pl
from jax.experimental.pallas import tpu as pltpu

F32 = jnp.float32
BF16 = jnp.bfloat16

GRID_W = 64
ROPE_THETA = 10000.0
NORM_EPS = 1e-6
MLA_HEADS = 8
MLA_Q_LORA = 768
MLA_KV_LORA = 512
MLA_NOPE = 128
MLA_ROPE = 64
MLA_V = 128
GQA_HEADS = 8
GQA_KV_HEADS = 2
GQA_HEAD_DIM = 128
ADAM_LR = 0.001
ADAM_B1 = 0.9
ADAM_B2 = 0.999
ADAM_EPS = 1e-08
ADAM_WD = 0.01
ADAM_STEP = 10

N_DEV = 8
MESH_AXES = ("x", "y", "c")
LANE = 128
MLA_SLOT = 2 * LANE
VMEM_LIMIT = 56 * 1024 * 1024
ROW_BLOCK = 256
ATT_Q_BLOCK = 512
ATT_Q_BLOCK_FWD = 512
LN2 = 0.6931471805599453
LOG2E = 1.4426950408889634
MESH_ID = pl.DeviceIdType.MESH


def _tile(n, pref, align=LANE):
    if n <= pref:
        return n
    best = None
    t = align
    while t <= pref:
        if n % t == 0:
            best = t
        t += align
    assert best is not None, (n, pref, align)
    return best


def _cparams(sem=None):
    return pltpu.CompilerParams(dimension_semantics=sem, vmem_limit_bytes=VMEM_LIMIT)


_ORDER_AFTER = []


def _after(*arrays):
    _ORDER_AFTER.extend(arrays)


def _pcall(body, *, in_specs, **kw):
    deps = tuple(_ORDER_AFTER)
    _ORDER_AFTER.clear()
    if not deps:
        return pl.pallas_call(body, in_specs=in_specs, **kw)
    n_in, n_dep = len(in_specs), len(deps)

    def with_deps(*refs):
        body(*refs[:n_in], *refs[n_in + n_dep:])

    call = pl.pallas_call(with_deps, in_specs=list(in_specs) + [pl.BlockSpec(memory_space=pl.ANY)] * n_dep, **kw)
    return lambda *args: call(*args, *deps)


def _all_gather(arrs, name):
    n = len(arrs)

    def body(*refs):
        ins = refs[:n]
        outs = refs[n:2 * n]
        send_sems, recv_sems, local_sems = refs[2 * n:]
        x, y, c = lax.axis_index("x"), lax.axis_index("y"), lax.axis_index("c")
        me, sibling = (x, y, c), (x, y, 1 - c)
        chips = [(1 - x, y), (x, 1 - y), (1 - x, 1 - y)]

        def rows(a, dev):
            px, py, pc = dev
            return outs[a].at[4 * px + 2 * py + pc]

        def copy(a, k, block, to, src=None):
            return pltpu.make_async_remote_copy(
                src_ref=rows(a, block) if src is None else src,
                dst_ref=rows(a, block),
                send_sem=send_sems.at[7 * a + k],
                recv_sem=recv_sems.at[7 * a + k],
                device_id=to,
                device_id_type=MESH_ID,
            )

        mine = [pltpu.make_async_copy(ins[a], rows(a, me), local_sems.at[a]) for a in range(n)]
        for cp in mine:
            cp.start()
        first = []
        for a in range(n):
            first.append(copy(a, 0, me, sibling, src=ins[a]))
            first += [copy(a, 1 + j, me, (*chip, c), src=ins[a]) for j, chip in enumerate(chips)]
        for cp in first:
            cp.start()
        passed = []
        for j, chip in enumerate(chips):
            for a in range(n):
                copy(a, 1 + j, (*chip, c), me).wait_recv()
                fwd = copy(a, 4 + j, (*chip, c), sibling)
                fwd.start()
                passed.append(fwd)
        for a in range(n):
            copy(a, 0, sibling, me).wait_recv()
            for j, chip in enumerate(chips):
                copy(a, 4 + j, (*chip, 1 - c), me).wait_recv()
        for cp in first + passed:
            cp.wait_send()
        for cp in mine:
            cp.wait()

    any_spec = pl.BlockSpec(memory_space=pl.ANY)
    outs = _pcall(
        body,
        name=name,
        out_shape=[jax.ShapeDtypeStruct((N_DEV,) + a.shape, a.dtype) for a in arrs],
        in_specs=[any_spec] * n,
        out_specs=[any_spec] * n,
        scratch_shapes=[
            pltpu.SemaphoreType.DMA((7 * n,)),
            pltpu.SemaphoreType.DMA((7 * n,)),
            pltpu.SemaphoreType.DMA((n,)),
        ],
    )(*arrs)
    return list(outs)


def _all_to_all(arrs, name):
    n = len(arrs)

    def body(*refs):
        ins = refs[:n]
        outs = refs[n:2 * n]
        send_sems, recv_sems, local_sems = refs[2 * n:]
        x, y, c = lax.axis_index("x"), lax.axis_index("y"), lax.axis_index("c")
        my_idx = 4 * x + 2 * y + c

        def peer(k):
            fx, fy, fc = (k >> 2) & 1, (k >> 1) & 1, k & 1
            return (x ^ fx if fx else x, y ^ fy if fy else y, c ^ fc if fc else c)

        def copy(a, k):
            px, py, pc = peer(k)
            return pltpu.make_async_remote_copy(
                src_ref=ins[a].at[4 * px + 2 * py + pc],
                dst_ref=outs[a].at[my_idx],
                send_sem=send_sems.at[7 * a + k - 1],
                recv_sem=recv_sems.at[7 * a + k - 1],
                device_id=(px, py, pc),
                device_id_type=MESH_ID,
            )

        mine = [pltpu.make_async_copy(ins[a].at[my_idx], outs[a].at[my_idx], local_sems.at[a]) for a in range(n)]
        for cp in mine:
            cp.start()
        order = [1, 4, 2, 5, 3, 6, 7]
        cps = [copy(a, k) for k in order for a in range(n)]
        for cp in cps:
            cp.start()
        for cp in cps:
            cp.wait()
        for cp in mine:
            cp.wait()

    any_spec = pl.BlockSpec(memory_space=pl.ANY)
    outs = _pcall(
        body,
        name=name,
        out_shape=[jax.ShapeDtypeStruct(a.shape, a.dtype) for a in arrs],
        in_specs=[any_spec] * n,
        out_specs=[any_spec] * n,
        scratch_shapes=[
            pltpu.SemaphoreType.DMA((7 * n,)),
            pltpu.SemaphoreType.DMA((7 * n,)),
            pltpu.SemaphoreType.DMA((n,)),
        ],
    )(*arrs)
    return list(outs)


_HBM = pl.BlockSpec(memory_space=pltpu.HBM)
_SEM = pl.BlockSpec(memory_space=pltpu.SEMAPHORE)
_EFFECT = pltpu.SideEffectType.DATAFLOW_SIDE_EFFECTING


def _descriptors(copies, send_sems, recv_sems):
    descs = []
    for i, (src, dst, dev) in enumerate(copies):
        if dev is None:
            descs.append(pltpu.make_async_copy(src, dst, recv_sems.at[i]))
        else:
            descs.append(pltpu.make_async_remote_copy(src_ref=src, dst_ref=dst, send_sem=send_sems.at[i],
                                                      recv_sem=recv_sems.at[i], device_id=dev, device_id_type=MESH_ID))
    return descs


def _split_start(name, arrays, copies_fn, n_copies):
    n = len(arrays)

    def body(*refs):
        send_sems, recv_sems = refs[n], refs[n + 1]
        token = refs[2 * n + 2]
        for dsc in _descriptors(copies_fn(refs[:n]), send_sems, recv_sems):
            dsc.start()
        token[...] = jnp.zeros_like(token)

    outs = _pcall(
        body,
        name=name,
        out_shape=(pltpu.SemaphoreType.DMA((n_copies,)), pltpu.SemaphoreType.DMA((n_copies,)),
                   *[pltpu.HBM(a.shape, a.dtype) for a in arrays], jax.ShapeDtypeStruct((8, LANE), F32)),
        in_specs=[_HBM] * n,
        out_specs=(_SEM, _SEM, *[_HBM] * n, pl.BlockSpec(memory_space=pltpu.VMEM)),
        input_output_aliases={i: 2 + i for i in range(n)},
        compiler_params=pltpu.CompilerParams(has_side_effects=_EFFECT),
    )(*[pltpu.with_memory_space_constraint(a, pltpu.HBM) for a in arrays])
    return outs[0], outs[1], list(outs[2:2 + n]), outs[2 + n]


def _split_wait(name, send_sems, recv_sems, arrays, copies_fn, after):
    n = len(arrays)

    def body(*refs):
        for dsc, (_, _, dev) in zip(_descriptors(copies_fn(refs[:n]), refs[n], refs[n + 1]), copies_fn(refs[:n])):
            if dev is None:
                dsc.wait()
            else:
                dsc.wait_send()
                dsc.wait_recv()

    outs = _pcall(
        body,
        name=name,
        out_shape=tuple(pltpu.HBM(a.shape, a.dtype) for a in arrays),
        in_specs=[_HBM] * n + [_SEM, _SEM, pl.BlockSpec(memory_space=pl.ANY)],
        out_specs=tuple([_HBM] * n),
        input_output_aliases={i: i for i in range(n)},
        compiler_params=pltpu.CompilerParams(has_side_effects=_EFFECT),
    )(*arrays, send_sems, recv_sems, after)
    return list(outs)


def _mesh_pos():
    x, y, c = lax.axis_index("x"), lax.axis_index("y"), lax.axis_index("c")
    return x, y, c, [(1 - x, y), (x, 1 - y), (1 - x, 1 - y)]


def _gather_ici_copies(n):
    def copies(refs):
        x, y, c, chips = _mesh_pos()
        me = 4 * x + 2 * y + c
        out = []
        for a in range(n):
            src, buf = refs[a], refs[n + a]
            out.append((src, buf.at[me], None))
            out.append((src, buf.at[me], (x, y, 1 - c)))
            out += [(src, buf.at[me], (cx, cy, c)) for cx, cy in chips[:2]]
        return out
    return copies


def _gather_pass_copies(n):
    def copies(refs):
        x, y, c, chips = _mesh_pos()
        south = c == 0
        bx, by = jnp.where(south, 1 - x, x), jnp.where(south, y, 1 - y)
        tx, ty = jnp.where(south, x, 1 - x), jnp.where(south, 1 - y, y)
        out = []
        for a in range(n):
            rows = refs[a].at[4 * bx + 2 * by + c]
            out.append((rows, rows, (tx, ty, c)))
            for cx, cy in chips[:2]:
                rows = refs[a].at[4 * cx + 2 * cy + c]
                out.append((rows, rows, (x, y, 1 - c)))
        return out
    return copies


def _gather_d2d_copies(n):
    def copies(refs):
        x, y, c, chips = _mesh_pos()
        cx, cy = chips[2]
        out = []
        for a in range(n):
            rows = refs[a].at[4 * cx + 2 * cy + c]
            out.append((rows, rows, (x, y, 1 - c)))
        return out
    return copies


def _reduce_d2d_copies(n):
    def copies(refs):
        x, y, c, _ = _mesh_pos()
        out = []
        for a in range(n):
            for k in range(4):
                out.append((refs[a].at[2 * k + (1 - c)], refs[n + a].at[k], (x, y, 1 - c)))
        return out
    return copies


def _reduce_ici_copies(n):
    def copies(refs):
        x, y, c, chips = _mesh_pos()
        mine = 2 * x + y
        out = []
        for a in range(n):
            src, land = refs[a], refs[n + a]
            out.append((src.at[mine], land.at[mine], None))
            out += [(src.at[2 * cx + cy], land.at[mine], (cx, cy, c)) for cx, cy in chips]
        return out
    return copies


def _pair_sum(send, land, c_idx, name):
    _, r, cols = send.shape
    rb = _tile(r, max(8, (1 << 22) // (send.dtype.itemsize * cols) // 8 * 8), 8)
    dt = send.dtype

    def body(c_ref, s_ref, l_ref, o_ref):
        o_ref[...] = (s_ref[...].astype(F32) + l_ref[...].astype(F32)).astype(dt)

    return pl.pallas_call(
        body,
        name=name,
        out_shape=jax.ShapeDtypeStruct((4, r, cols), dt),
        grid_spec=pltpu.PrefetchScalarGridSpec(
            num_scalar_prefetch=1,
            grid=(4, r // rb),
            in_specs=[pl.BlockSpec((None, rb, cols), lambda k, i, c_ref: (2 * k + c_ref[0], i, 0)),
                      pl.BlockSpec((None, rb, cols), lambda k, i, c_ref: (k, i, 0))],
            out_specs=pl.BlockSpec((None, rb, cols), lambda k, i, c_ref: (k, i, 0)),
        ),
        compiler_params=_cparams(("parallel", "parallel")),
    )(c_idx, send, land)


_DIMS = {
    "nn": (((1,), (0,)), ((), ())),
    "nt": (((1,), (1,)), ((), ())),
    "tn": (((0,), (0,)), ((), ())),
}


def _mm_call(a, b, *, mode, grid, a_spec, b_spec, o_spec, out_shape, acc_shape, name):
    nk = grid[2]
    out_dtype = out_shape.dtype

    def body(a_ref, b_ref, o_ref, *scratch):
        p = lax.dot_general(a_ref[...].astype(BF16), b_ref[...].astype(BF16), _DIMS[mode],
                            preferred_element_type=F32)
        if nk == 1:
            o_ref[...] = p.astype(out_dtype)
        else:
            acc = scratch[0]
            k = pl.program_id(2)

            @pl.when(k == 0)
            def _():
                acc[...] = p

            @pl.when(k > 0)
            def _():
                acc[...] += p

            @pl.when(k == nk - 1)
            def _():
                o_ref[...] = acc[...].astype(out_dtype)

    return _pcall(
        body,
        name=name,
        out_shape=out_shape,
        grid=grid,
        in_specs=[a_spec, b_spec],
        out_specs=o_spec,
        scratch_shapes=[pltpu.VMEM(acc_shape, F32)] if nk > 1 else [],
        compiler_params=_cparams(("parallel", "parallel", "arbitrary")),
    )(a, b)


def _mm(a, b, mode, out_dtype, name, tm=512, tn=512, tk=2432, a_row_off=0, rows=None):
    if mode == "nn":
        (m, k), (k2, n) = a.shape, b.shape
    elif mode == "nt":
        (m, k), (n, k2) = a.shape, b.shape
    else:
        (k, m), (k2, n) = a.shape, b.shape
        if rows is not None:
            k = k2 = rows
    assert k == k2, (a.shape, b.shape, mode)
    if mode != "tn":
        m = (m if rows is None else rows + a_row_off) - a_row_off
    tm, tn, tk = _tile(m, tm, 8), _tile(n, tn), _tile(k, tk, 8 if mode == "tn" else LANE)
    assert a_row_off % tm == 0
    ro = a_row_off // tm
    grid = (m // tm, n // tn, k // tk)
    if mode == "tn":
        a_spec = pl.BlockSpec((tk, tm), lambda i, j, kk: (kk, i))
    else:
        a_spec = pl.BlockSpec((tm, tk), lambda i, j, kk: (i + ro, kk))
    if mode == "nt":
        b_spec = pl.BlockSpec((tn, tk), lambda i, j, kk: (j, kk))
    else:
        b_spec = pl.BlockSpec((tk, tn), lambda i, j, kk: (kk, j))
    o_spec = pl.BlockSpec((tm, tn), lambda i, j, kk: (i, j))
    return _mm_call(a, b, mode=mode, grid=grid, a_spec=a_spec, b_spec=b_spec, o_spec=o_spec,
                    out_shape=jax.ShapeDtypeStruct((m, n), out_dtype), acc_shape=(tm, tn), name=name)


def _mm_cat_nt(pieces, out_dtype, name, tm=1024, tn=1024, tk=2048, rows=None):
    m = pieces[0][0].shape[0] if rows is None else rows
    n = pieces[0][1].shape[0]
    tm, tn = _tile(m, tm, 8), _tile(n, tn)
    steps, starts, s = [], [], 0
    for a, b, off in pieces:
        kp = a.shape[1]
        tkp = _tile(kp, tk)
        assert off % tkp == 0 and b.shape[0] == n
        steps.append((tkp, kp // tkp, off // tkp))
        starts.append(s)
        s += kp // tkp
    nk = s
    npc = len(pieces)

    def body(*refs):
        o_ref, acc = refs[2 * npc], refs[2 * npc + 1]
        kk = pl.program_id(2)

        @pl.when(kk == 0)
        def _():
            acc[...] = jnp.zeros_like(acc)

        for p in range(npc):
            @pl.when((kk >= starts[p]) & (kk < starts[p] + steps[p][1]))
            def _(p=p):
                acc[...] += lax.dot_general(refs[2 * p][...].astype(BF16), refs[2 * p + 1][...].astype(BF16), _DIMS["nt"],
                                            preferred_element_type=F32)

        @pl.when(kk == nk - 1)
        def _():
            o_ref[...] = acc[...].astype(out_dtype)

    in_specs, args = [], []
    for p, (a, b, off) in enumerate(pieces):
        tkp, np_, ob = steps[p]

        def rel(kk, p=p, np_=np_):
            return jnp.clip(kk - starts[p], 0, np_ - 1)

        in_specs.append(pl.BlockSpec((tm, tkp), lambda i, j, kk, rel=rel: (i, rel(kk))))
        in_specs.append(pl.BlockSpec((tn, tkp), lambda i, j, kk, rel=rel, ob=ob: (j, ob + rel(kk))))
        args += [a, b]
    return _pcall(
        body,
        name=name,
        out_shape=jax.ShapeDtypeStruct((m, n), out_dtype),
        grid=(m // tm, n // tn, nk),
        in_specs=in_specs,
        out_specs=pl.BlockSpec((tm, tn), lambda i, j, kk: (i, j)),
        scratch_shapes=[pltpu.VMEM((tm, tn), F32)],
        compiler_params=_cparams(("parallel", "parallel", "arbitrary")),
    )(*args)


def _mm_cat_tn(a, pieces, out_dtype, name, tm=1024, tn=1024, rows=None):
    k = a.shape[0] if rows is None else rows
    m = a.shape[1]
    tm = _tile(m, tm)
    starts, s = [], 0
    for b in pieces:
        assert b.shape[1] % tn == 0
        starts.append(s)
        s += b.shape[1] // tn
    nj = s
    npc = len(pieces)

    def body(*refs):
        a_ref, o_ref = refs[0], refs[1 + npc]
        j = pl.program_id(1)
        for p in range(npc):
            @pl.when((j >= starts[p]) & (j < starts[p] + pieces[p].shape[1] // tn))
            def _(p=p):
                o_ref[...] = lax.dot_general(a_ref[...].astype(BF16), refs[1 + p][...].astype(BF16), _DIMS["tn"],
                                             preferred_element_type=F32).astype(out_dtype)

    in_specs = [pl.BlockSpec((k, tm), lambda i, j: (0, i))]
    for p, b in enumerate(pieces):
        np_ = b.shape[1] // tn
        in_specs.append(pl.BlockSpec((k, tn), lambda i, j, p=p, np_=np_: (0, jnp.clip(j - starts[p], 0, np_ - 1))))
    return _pcall(
        body,
        name=name,
        out_shape=jax.ShapeDtypeStruct((m, nj * tn), out_dtype),
        grid=(m // tm, nj),
        in_specs=in_specs,
        out_specs=pl.BlockSpec((tm, tn), lambda i, j: (i, j)),
        compiler_params=_cparams(("parallel", "arbitrary")),
    )(a, *pieces)


def _mm_up_fwd(z2, w3, name, tm=1024):
    t, d = z2.shape
    nsh, _, c = w3.shape
    tm = _tile(t, tm, 8)
    return _mm_call(z2, w3, mode="nn", grid=(t // tm, nsh, 1),
                    a_spec=pl.BlockSpec((tm, d), lambda i, j, kk: (i, 0)),
                    b_spec=pl.BlockSpec((None, d, c), lambda i, j, kk: (j, 0, 0)),
                    o_spec=pl.BlockSpec((tm, c), lambda i, j, kk: (i, j)),
                    out_shape=jax.ShapeDtypeStruct((t, nsh * c), BF16), acc_shape=(tm, c), name=name)


def _mm_up_dz(du3, w3, name, tm=512, tn=1024):
    _, t, f = du3.shape
    nsh, d, c = w3.shape
    half = nsh // 2
    assert f == half * c
    tm, tn = _tile(t, tm, 8), _tile(d, tn)

    def body(a_ref, b_ref, o_ref, acc):
        kk = pl.program_id(2)
        p = None
        for s in range(half):
            q = lax.dot_general(a_ref[:, s * c:(s + 1) * c], b_ref[s], _DIMS["nt"], preferred_element_type=F32)
            p = q if p is None else p + q

        @pl.when(kk == 0)
        def _():
            acc[...] = p

        @pl.when(kk == 1)
        def _():
            o_ref[...] = (acc[...] + p).astype(BF16)

    return _pcall(
        body,
        name=name,
        out_shape=jax.ShapeDtypeStruct((t, d), BF16),
        grid=(t // tm, d // tn, 2),
        in_specs=[pl.BlockSpec((None, tm, f), lambda i, j, kk: (kk, i, 0)),
                  pl.BlockSpec((half, tn, c), lambda i, j, kk: (kk, j, 0))],
        out_specs=pl.BlockSpec((tm, tn), lambda i, j, kk: (i, j)),
        scratch_shapes=[pltpu.VMEM((tm, tn), F32)],
        compiler_params=_cparams(("parallel", "parallel", "arbitrary")),
    )(du3, w3)


def _mm_sum_nn(pieces, out_dtype, name, tm=512, tn=512, rows=None):
    m = pieces[0][0].shape[0] if rows is None else rows
    n = pieces[0][2].shape[1]
    tm, tn = _tile(m, tm, 8), _tile(n, tn)
    npc = len(pieces)

    def body(*refs):
        p = None
        for s in range(npc):
            q = jnp.dot(refs[2 * s][...].astype(BF16), refs[2 * s + 1][...].astype(BF16), preferred_element_type=F32)
            p = q if p is None else p + q
        refs[2 * npc][...] = p.astype(out_dtype)

    in_specs, args = [], []
    for a, ao, b, bo, kp in pieces:
        assert ao % kp == 0 and bo % kp == 0 and b.shape[1] == n
        in_specs.append(pl.BlockSpec((tm, kp), lambda i, j, ab=ao // kp: (i, ab)))
        in_specs.append(pl.BlockSpec((kp, tn), lambda i, j, bb=bo // kp: (bb, j)))
        args += [a, b]
    return _pcall(
        body,
        name=name,
        out_shape=jax.ShapeDtypeStruct((m, n), out_dtype),
        grid=(m // tm, n // tn),
        in_specs=in_specs,
        out_specs=pl.BlockSpec((tm, tn), lambda i, j: (i, j)),
        compiler_params=_cparams(("parallel", "parallel")),
    )(*args)


def _mm_rows_tn(pieces, b, out_dtype, name, tm=1024, tn=1024, rows=None):
    k = b.shape[0] if rows is None else rows
    n = b.shape[1]
    tn = _tile(n, tn)
    starts, s = [], 0
    for a in pieces:
        assert a.shape[1] % tm == 0
        starts.append(s)
        s += a.shape[1] // tm
    ni = s
    npc = len(pieces)

    def body(*refs):
        b_ref, o_ref = refs[npc], refs[npc + 1]
        i = pl.program_id(0)
        for p in range(npc):
            @pl.when((i >= starts[p]) & (i < starts[p] + pieces[p].shape[1] // tm))
            def _(p=p):
                o_ref[...] = lax.dot_general(refs[p][...].astype(BF16), b_ref[...].astype(BF16), _DIMS["tn"],
                                             preferred_element_type=F32).astype(out_dtype)

    in_specs = []
    for p, a in enumerate(pieces):
        np_ = a.shape[1] // tm
        in_specs.append(pl.BlockSpec((k, tm), lambda i, j, p=p, np_=np_: (0, jnp.clip(i - starts[p], 0, np_ - 1))))
    in_specs.append(pl.BlockSpec((k, tn), lambda i, j: (0, j)))
    return _pcall(
        body,
        name=name,
        out_shape=jax.ShapeDtypeStruct((ni * tm, n), out_dtype),
        grid=(ni, n // tn),
        in_specs=in_specs,
        out_specs=pl.BlockSpec((tm, tn), lambda i, j: (i, j)),
        compiler_params=_cparams(("parallel", "parallel")),
    )(*pieces, b)


def _mm_up_gw(z2, du3, nsh, name, tm=1024):
    t, d = z2.shape
    f = du3.shape[2]
    half = nsh // 2
    c = f // half
    tm = _tile(d, tm)
    return _mm_call(z2, du3, mode="tn", grid=(d // tm, nsh, 1),
                    a_spec=pl.BlockSpec((t, tm), lambda i, j, kk: (0, i)),
                    b_spec=pl.BlockSpec((None, t, c), lambda i, j, kk: (j // half, 0, j % half)),
                    o_spec=pl.BlockSpec((None, tm, c), lambda i, j, kk: (j, i, 0)),
                    out_shape=jax.ShapeDtypeStruct((nsh, d, c), BF16), acc_shape=(tm, c), name=name)


def _rms(x):
    r = lax.rsqrt(jnp.mean(x * x, axis=-1, keepdims=True) + NORM_EPS)
    return x * r, r


def _rms_bwd(dxh, xh, r):
    return r * (dxh - xh * jnp.mean(dxh * xh, axis=-1, keepdims=True))


def _colsum(v):
    return jnp.sum(v, axis=0, keepdims=True)


def _rope(v, c, s1, s2, q):
    w = v.shape[-1]
    return v * c + pltpu.roll(v, w - q, 1) * s1 + pltpu.roll(v, q, 1) * s2


def _rope_t(d, c, s1, s2, q):
    w = d.shape[-1]
    return d * c + pltpu.roll(d * s1, q, 1) + pltpu.roll(d * s2, w - q, 1)


def _norm_mod_fwd(ctx, x, gain, mods):
    tc, d = ctx.shape
    t = x.shape[0]
    rb = min(ROW_BLOCK, tc)
    nbl = t // rb

    def body(ctx_ref, x_ref, g_ref, mod_ref, z_ref):
        i = pl.program_id(0)

        def emit(src, sh, sc):
            xh, _ = _rms(src[...])
            z_ref[...] = ((xh * g_ref[...]) * (1.0 + sc) + sh).astype(BF16)

        @pl.when(i >= nbl)
        def _():
            emit(ctx_ref, mod_ref[2:3, :], mod_ref[3:4, :])

        @pl.when(i < nbl)
        def _():
            emit(x_ref, mod_ref[0:1, :], mod_ref[1:2, :])

    return _pcall(
        body,
        name="norm1_mod_fwd",
        out_shape=jax.ShapeDtypeStruct((tc + t, d), BF16),
        grid=((tc + t) // rb,),
        in_specs=[
            pl.BlockSpec((rb, d), lambda i: (jnp.maximum(i - nbl, 0), 0)),
            pl.BlockSpec((rb, d), lambda i: (jnp.minimum(i, nbl - 1), 0)),
            pl.BlockSpec((1, d), lambda i: (0, 0)),
            pl.BlockSpec((8, d), lambda i: (0, 0)),
        ],
        out_specs=pl.BlockSpec((rb, d), lambda i: (i, 0)),
        compiler_params=_cparams(("arbitrary",)),
    )(ctx, x, gain, mods)


def _norm1_bwd(ctx, x, gain, mods, dz_ctx, dz_lat, dx1):
    tc, d = ctx.shape
    t = x.shape[0]
    rb = min(ROW_BLOCK, tc)
    nbl = t // rb

    def body(ctx_ref, x_ref, g_ref, mod_ref, dzc_ref, dzl_ref, dx1_ref, gx_ref, st_ref):
        i = pl.program_id(0)

        @pl.when(i == 0)
        def _():
            st_ref[...] = jnp.zeros_like(st_ref)

        def common(src, dz, sc, row_sh, row_sc):
            xh, r = _rms(src[...])
            g = g_ref[...]
            dxn = dz * (1.0 + sc)
            st_ref[row_sh:row_sh + 1, :] += _colsum(dz)
            st_ref[row_sc:row_sc + 1, :] += _colsum(dz * (xh * g))
            st_ref[2:3, :] += _colsum(dxn * xh)
            return _rms_bwd(dxn * g, xh, r)

        @pl.when(i >= nbl)
        def _():
            common(ctx_ref, dzc_ref[...], mod_ref[3:4, :], 3, 4)

        @pl.when(i < nbl)
        def _():
            gx_ref[...] = dx1_ref[...] + common(x_ref, dzl_ref[...], mod_ref[1:2, :], 0, 1)

    lat = lambda i: (jnp.minimum(i, nbl - 1), 0)
    cix = lambda i: (jnp.maximum(i - nbl, 0), 0)
    return _pcall(
        body,
        name="norm1_mod_bwd",
        out_shape=[jax.ShapeDtypeStruct((t, d), F32), jax.ShapeDtypeStruct((8, d), F32)],
        grid=((tc + t) // rb,),
        in_specs=[
            pl.BlockSpec((rb, d), cix),
            pl.BlockSpec((rb, d), lat),
            pl.BlockSpec((1, d), lambda i: (0, 0)),
            pl.BlockSpec((8, d), lambda i: (0, 0)),
            pl.BlockSpec((rb, d), cix),
            pl.BlockSpec((rb, d), lat),
            pl.BlockSpec((rb, d), lat),
        ],
        out_specs=[pl.BlockSpec((rb, d), lat), pl.BlockSpec((8, d), lambda i: (0, 0))],
        compiler_params=_cparams(("arbitrary",)),
    )(ctx, x, gain, mods, dz_ctx, dz_lat, dx1)


def _key_prep_fwd(kv, kv_gain, kb_gain, tabs):
    ta, wkv = kv.shape
    kvl = MLA_KV_LORA
    nb = GQA_KV_HEADS * GQA_HEAD_DIM
    rb = ROW_BLOCK if ta % ROW_BLOCK == 0 else LANE
    hd = GQA_HEAD_DIM

    def body(kv_ref, g_ref, gb_ref, ca, s1a, s2a, cb, s1b, s2b, kin_ref, kb_ref, vb_ref):
        xh, _ = _rms(kv_ref[:, 0:kvl])
        kin_ref[:, 0:kvl] = (xh * g_ref[...]).astype(BF16)
        kpe = kv_ref[:, kvl + 2 * nb:kvl + 2 * nb + LANE]
        kin_ref[:, kvl:kvl + LANE] = _rope(kpe, ca[...], s1a[...], s2a[...], MLA_ROPE // 4).astype(BF16)
        for h in range(GQA_KV_HEADS):
            nh, _ = _rms(kv_ref[:, kvl + h * hd:kvl + (h + 1) * hd])
            kb_ref[:, h * hd:(h + 1) * hd] = _rope(nh * gb_ref[...], cb[...], s1b[...], s2b[...], hd // 4).astype(BF16)
        vb_ref[...] = kv_ref[:, kvl + nb:kvl + 2 * nb].astype(BF16)

    row = lambda w: pl.BlockSpec((rb, w), lambda i: (i, 0))
    fix = lambda w: pl.BlockSpec((1, w), lambda i: (0, 0))
    return _pcall(
        body,
        name="key_prep_fwd",
        out_shape=[jax.ShapeDtypeStruct((ta, kvl + LANE), BF16), jax.ShapeDtypeStruct((ta, nb), BF16),
                   jax.ShapeDtypeStruct((ta, nb), BF16)],
        grid=(ta // rb,),
        in_specs=[row(wkv), fix(kvl), fix(hd)] + [row(LANE)] * 3 + [row(hd)] * 3,
        out_specs=[row(kvl + LANE), row(nb), row(nb)],
        compiler_params=_cparams(("parallel",)),
    )(kv, kv_gain, kb_gain, *tabs)


def _key_prep_bwd(kv, kv_gain, kb_gain, tabs, dkin, dkb, dvb):
    ta, wkv = kv.shape
    kvl = MLA_KV_LORA
    nb = GQA_KV_HEADS * GQA_HEAD_DIM
    rb = ROW_BLOCK if ta % ROW_BLOCK == 0 else LANE
    hd = GQA_HEAD_DIM

    def body(kv_ref, g_ref, gb_ref, ca, s1a, s2a, cb, s1b, s2b, dkin_ref, dkb_ref, dvb_ref, dkv_ref, st_ref, stb_ref):
        @pl.when(pl.program_id(0) == 0)
        def _():
            st_ref[...] = jnp.zeros_like(st_ref)
            stb_ref[...] = jnp.zeros_like(stb_ref)

        xh, r = _rms(kv_ref[:, 0:kvl])
        dn = dkin_ref[:, 0:kvl]
        st_ref[0:1, :] += _colsum(dn * xh)
        dkv_ref[:, 0:kvl] = _rms_bwd(dn * g_ref[...], xh, r).astype(BF16)
        dpe = _rope_t(dkin_ref[:, kvl:kvl + LANE], ca[...], s1a[...], s2a[...], MLA_ROPE // 4)
        dkv_ref[:, kvl + 2 * nb:kvl + 2 * nb + LANE] = dpe.astype(BF16)
        for h in range(GQA_KV_HEADS):
            nh, rh = _rms(kv_ref[:, kvl + h * hd:kvl + (h + 1) * hd])
            dn_h = _rope_t(dkb_ref[:, h * hd:(h + 1) * hd], cb[...], s1b[...], s2b[...], hd // 4)
            stb_ref[0:1, :] += _colsum(dn_h * nh)
            dkv_ref[:, kvl + h * hd:kvl + (h + 1) * hd] = _rms_bwd(dn_h * gb_ref[...], nh, rh).astype(BF16)
        dkv_ref[:, kvl + nb:kvl + 2 * nb] = dvb_ref[...].astype(BF16)

    row = lambda w: pl.BlockSpec((rb, w), lambda i: (i, 0))
    fix = lambda w: pl.BlockSpec((1, w), lambda i: (0, 0))
    return _pcall(
        body,
        name="key_prep_bwd",
        out_shape=[jax.ShapeDtypeStruct((ta, wkv), BF16), jax.ShapeDtypeStruct((8, kvl), F32),
                   jax.ShapeDtypeStruct((8, hd), F32)],
        grid=(ta // rb,),
        in_specs=[row(wkv), fix(kvl), fix(hd)] + [row(LANE)] * 3 + [row(hd)] * 3 + [row(kvl + LANE), row(nb), row(nb)],
        out_specs=[row(wkv), pl.BlockSpec((8, kvl), lambda i: (0, 0)), pl.BlockSpec((8, hd), lambda i: (0, 0))],
        compiler_params=_cparams(("arbitrary",)),
    )(kv, kv_gain, kb_gain, *tabs, dkin, dkb, dvb)


def _q_prep_fwd(qg, q_gain, qb_gain, tabs, qscale):
    t = qg.shape[0]
    ql = MLA_Q_LORA
    hd = GQA_HEAD_DIM
    hb = GQA_HEADS * hd
    rb = min(ROW_BLOCK, t)

    def body(q_ref, g_ref, gb_ref, cb, s1b, s2b, cqn_ref, qb_ref):
        xh, _ = _rms(q_ref[:, 0:ql])
        cqn_ref[...] = (xh * g_ref[...]).astype(BF16)
        for h in range(GQA_HEADS):
            nh, _ = _rms(q_ref[:, ql + h * hd:ql + (h + 1) * hd])
            qh = _rope(nh * gb_ref[...], cb[...], s1b[...], s2b[...], hd // 4)
            qb_ref[:, h * hd:(h + 1) * hd] = (qh * qscale).astype(BF16)

    row = lambda w: pl.BlockSpec((rb, w), lambda i: (i, 0))
    fix = lambda w: pl.BlockSpec((1, w), lambda i: (0, 0))
    return _pcall(
        body,
        name="q_prep_fwd",
        out_shape=[jax.ShapeDtypeStruct((t, ql), BF16), jax.ShapeDtypeStruct((t, hb), BF16)],
        grid=(t // rb,),
        in_specs=[row(ql + hb), fix(ql), fix(hd)] + [row(hd)] * 3,
        out_specs=[row(ql), row(hb)],
        compiler_params=_cparams(("parallel",)),
    )(qg, q_gain, qb_gain, *tabs)


def _q_prep_bwd(qg, q_gain, qb_gain, tabs, dcqn, dqb, wpad, qscale):
    t = qg.shape[0]
    ql = MLA_Q_LORA
    hd = GQA_HEAD_DIM
    hb = GQA_HEADS * hd
    rb = min(ROW_BLOCK, t)

    def body(q_ref, g_ref, gb_ref, cb, s1b, s2b, dcqn_ref, dqb_ref, dq_ref, st_ref, stb_ref):
        @pl.when(pl.program_id(0) == 0)
        def _():
            st_ref[...] = jnp.zeros_like(st_ref)
            stb_ref[...] = jnp.zeros_like(stb_ref)

        xh, r = _rms(q_ref[:, 0:ql])
        dn = dcqn_ref[...]
        st_ref[0:1, :] += _colsum(dn * xh)
        dq_ref[:, 0:ql] = _rms_bwd(dn * g_ref[...], xh, r).astype(BF16)
        for h in range(GQA_HEADS):
            nh, rh = _rms(q_ref[:, ql + h * hd:ql + (h + 1) * hd])
            dn_h = _rope_t(dqb_ref[:, h * hd:(h + 1) * hd] * qscale, cb[...], s1b[...], s2b[...], hd // 4)
            stb_ref[0:1, :] += _colsum(dn_h * nh)
            dq_ref[:, ql + h * hd:ql + (h + 1) * hd] = _rms_bwd(dn_h * gb_ref[...], nh, rh).astype(BF16)
        if wpad:
            dq_ref[:, ql + hb:ql + hb + wpad] = jnp.zeros((rb, wpad), BF16)

    row = lambda w: pl.BlockSpec((rb, w), lambda i: (i, 0))
    fix = lambda w: pl.BlockSpec((1, w), lambda i: (0, 0))
    return _pcall(
        body,
        name="q_prep_bwd",
        out_shape=[jax.ShapeDtypeStruct((t, ql + hb + wpad), BF16), jax.ShapeDtypeStruct((8, ql), F32),
                   jax.ShapeDtypeStruct((8, hd), F32)],
        grid=(t // rb,),
        in_specs=[row(ql + hb), fix(ql), fix(hd)] + [row(hd)] * 3 + [row(ql), row(hb)],
        out_specs=[row(ql + hb + wpad), pl.BlockSpec((8, ql), lambda i: (0, 0)), pl.BlockSpec((8, hd), lambda i: (0, 0))],
        compiler_params=_cparams(("arbitrary",)),
    )(qg, q_gain, qb_gain, *tabs, dcqn, dqb)


def _rope_a(v, tabs, transpose, out_dtype, name, qscale):
    t, w = v.shape
    rb = min(ROW_BLOCK, t)
    fn = _rope_t if transpose else _rope

    def body(v_ref, c, s1, s2, o_ref):
        for h in range(w // MLA_SLOT):
            sl = slice(h * MLA_SLOT, (h + 1) * MLA_SLOT)
            o_ref[:, sl] = (fn(v_ref[:, sl].astype(F32), c[...], s1[...], s2[...], MLA_ROPE // 4) * qscale).astype(out_dtype)

    row = lambda ww: pl.BlockSpec((rb, ww), lambda i: (i, 0))
    return _pcall(
        body,
        name=name,
        out_shape=jax.ShapeDtypeStruct((t, w), out_dtype),
        grid=(t // rb,),
        in_specs=[row(w)] + [row(MLA_SLOT)] * 3,
        out_specs=row(w),
        compiler_params=_cparams(("parallel",)),
    )(v, *tabs)


def _merge_fwd(pa, pb, qg, gate_blk):
    t, d = pa.shape
    rb = min(ROW_BLOCK, t)

    def body(pa_ref, pb_ref, ga_ref, gb_ref, o_ref):
        o_ref[...] = (jax.nn.sigmoid(ga_ref[...]) * pa_ref[...].astype(F32)
                      + jax.nn.sigmoid(gb_ref[...]) * pb_ref[...].astype(F32)).astype(BF16)

    row = pl.BlockSpec((rb, d), lambda i: (i, 0))
    return _pcall(
        body,
        name="merge_fwd",
        out_shape=jax.ShapeDtypeStruct((t, d), BF16),
        grid=(t // rb,),
        in_specs=[row, row, pl.BlockSpec((rb, d), lambda i: (i, gate_blk)), pl.BlockSpec((rb, d), lambda i: (i, gate_blk + 1))],
        out_specs=row,
        compiler_params=_cparams(("parallel",)),
    )(pa, pb, qg, qg)


def _merge_bwd(dm, pa, pb, qg, gate_blk):
    t, d = pa.shape
    rb = min(ROW_BLOCK, t)

    def body(dm_ref, pa_ref, pb_ref, ga_ref, gb_ref, dpa_ref, dpb_ref, dg_ref):
        dmv = dm_ref[...].astype(F32)
        sa = jax.nn.sigmoid(ga_ref[...])
        sb = jax.nn.sigmoid(gb_ref[...])
        dpa_ref[...] = (dmv * sa).astype(BF16)
        dpb_ref[...] = (dmv * sb).astype(BF16)
        dg_ref[:, 0:d] = (dmv * pa_ref[...].astype(F32) * (sa * (1.0 - sa))).astype(BF16)
        dg_ref[:, d:2 * d] = (dmv * pb_ref[...].astype(F32) * (sb * (1.0 - sb))).astype(BF16)

    row = pl.BlockSpec((rb, d), lambda i: (i, 0))
    return _pcall(
        body,
        name="merge_bwd",
        out_shape=[jax.ShapeDtypeStruct((t, d), BF16), jax.ShapeDtypeStruct((t, d), BF16),
                   jax.ShapeDtypeStruct((t, 2 * d), BF16)],
        grid=(t // rb,),
        in_specs=[row, row, row, pl.BlockSpec((rb, d), lambda i: (i, gate_blk)), pl.BlockSpec((rb, d), lambda i: (i, gate_blk + 1))],
        out_specs=[row, row, pl.BlockSpec((rb, 2 * d), lambda i: (i, 0))],
        compiler_params=_cparams(("parallel",)),
    )(dm, pa, pb, qg, qg)


def _resid_norm_mod(x, branch, gain, mods, name):
    t, d = x.shape
    rb = min(ROW_BLOCK, t)

    def body(x_ref, b_ref, g_ref, mod_ref, x1_ref, z_ref):
        x1 = x_ref[...] + mod_ref[0:1, :] * b_ref[...]
        x1_ref[...] = x1
        xh, _ = _rms(x1)
        z_ref[...] = ((xh * g_ref[...]) * (1.0 + mod_ref[2:3, :]) + mod_ref[1:2, :]).astype(BF16)

    row = pl.BlockSpec((rb, d), lambda i: (i, 0))
    return _pcall(
        body,
        name=name,
        out_shape=[jax.ShapeDtypeStruct((t, d), F32), jax.ShapeDtypeStruct((t, d), BF16)],
        grid=(t // rb,),
        in_specs=[row, row, pl.BlockSpec((1, d), lambda i: (0, 0)), pl.BlockSpec((8, d), lambda i: (0, 0))],
        out_specs=[row, row],
        compiler_params=_cparams(("parallel",)),
    )(x, branch, gain, mods)


def _norm2_bwd(x1, attn, gain, mods, dz2, dx2):
    t, d = x1.shape
    rb = min(ROW_BLOCK, t)

    def body(x1_ref, at_ref, g_ref, mod_ref, dz_ref, dx2_ref, dx1_ref, da_ref, st_ref):
        @pl.when(pl.program_id(0) == 0)
        def _():
            st_ref[...] = jnp.zeros_like(st_ref)

        xh, r = _rms(x1_ref[...])
        g = g_ref[...]
        dz = dz_ref[...].astype(F32)
        dxn = dz * (1.0 + mod_ref[1:2, :])
        st_ref[0:1, :] += _colsum(dz)
        st_ref[1:2, :] += _colsum(dz * (xh * g))
        st_ref[2:3, :] += _colsum(dxn * xh)
        dx1 = dx2_ref[...] + _rms_bwd(dxn * g, xh, r)
        dx1_ref[...] = dx1
        st_ref[3:4, :] += _colsum(dx1 * at_ref[...])
        da_ref[...] = (dx1 * mod_ref[0:1, :]).astype(BF16)

    row = pl.BlockSpec((rb, d), lambda i: (i, 0))
    return _pcall(
        body,
        name="norm2_mod_bwd",
        out_shape=[jax.ShapeDtypeStruct((t, d), F32), jax.ShapeDtypeStruct((t, d), BF16), jax.ShapeDtypeStruct((8, d), F32)],
        grid=(t // rb,),
        in_specs=[row, row, pl.BlockSpec((1, d), lambda i: (0, 0)), pl.BlockSpec((8, d), lambda i: (0, 0)), row, row],
        out_specs=[row, row, pl.BlockSpec((8, d), lambda i: (0, 0))],
        compiler_params=_cparams(("arbitrary",)),
    )(x1, attn, gain, mods, dz2, dx2)


def _final_loss(x1, ffn, gain, mods, target):
    t, d = x1.shape
    rb = min(ROW_BLOCK, t)
    nb = t // rb

    def body(x1_ref, f_ref, g_ref, mod_ref, tg_ref, dx2_ref, df_ref, st_ref):
        i = pl.program_id(0)

        @pl.when(i == 0)
        def _():
            st_ref[...] = jnp.zeros_like(st_ref)

        ffn_v = f_ref[...]
        g2 = mod_ref[0:1, :]
        x2 = x1_ref[...] + g2 * ffn_v
        xh, r = _rms(x2)
        g = g_ref[...]
        err = xh * g - tg_ref[...]
        st_ref[2:3, :] += _colsum(err * err) * (0.5 / d)
        dy = err * (1.0 / d)
        st_ref[0:1, :] += _colsum(dy * xh)
        dx2 = _rms_bwd(dy * g, xh, r)
        dx2_ref[...] = dx2
        st_ref[1:2, :] += _colsum(dx2 * ffn_v)
        df_ref[...] = (dx2 * g2).astype(BF16)

        @pl.when(i == nb - 1)
        def _():
            st_ref[3:4, :] = jnp.broadcast_to(jnp.sum(st_ref[2:3, :], axis=-1, keepdims=True), (1, d))

    row = pl.BlockSpec((rb, d), lambda i: (i, 0))
    return _pcall(
        body,
        name="final_norm_loss",
        out_shape=[jax.ShapeDtypeStruct((t, d), F32), jax.ShapeDtypeStruct((t, d), BF16), jax.ShapeDtypeStruct((8, d), F32)],
        grid=(nb,),
        in_specs=[row, row, pl.BlockSpec((1, d), lambda i: (0, 0)), pl.BlockSpec((8, d), lambda i: (0, 0)), row],
        out_specs=[row, row, pl.BlockSpec((8, d), lambda i: (0, 0))],
        compiler_params=_cparams(("arbitrary",)),
    )(x1, ffn, gain, mods, target)


def _row_ends(shape):
    rows = lax.broadcasted_iota(jnp.int32, shape, 0)
    return rows == 0, rows == shape[0] - 1


def _shift_dn(v, first):
    return jnp.where(first, 0.0, pltpu.roll(v, 1, 0))


def _shift_up(v, last):
    return jnp.where(last, 0.0, pltpu.roll(v, v.shape[0] - 1, 0))


def _conv_fwd(u, cw, cb):
    t, f2 = u.shape
    f = f2 // 2
    cbk = _tile(f, 256)
    nf = f // cbk

    def body(ua_ref, ub_ref, cwa_ref, cwb_ref, cba_ref, cbb_ref, h_ref, uc_ref):
        first, last = _row_ends((t, cbk))
        outs = []
        for u_ref, cw_ref, cb_ref in ((ua_ref, cwa_ref, cba_ref), (ub_ref, cwb_ref, cbb_ref)):
            uu, cwv = u_ref[...].astype(F32), cw_ref[...]
            outs.append(cb_ref[...] + cwv[0:1, :] * _shift_dn(uu, first) + cwv[1:2, :] * uu
                        + cwv[2:3, :] * _shift_up(uu, last))
        a, b = outs
        uc_ref[0] = a.astype(BF16)
        uc_ref[1] = b.astype(BF16)
        h_ref[...] = (a * jax.nn.sigmoid(a) * b).astype(BF16)

    ca = lambda r: pl.BlockSpec((r, cbk), lambda j: (0, j))
    cbs = lambda r: pl.BlockSpec((r, cbk), lambda j: (0, nf + j))
    return _pcall(
        body,
        name="conv_gate_fwd",
        out_shape=[jax.ShapeDtypeStruct((t, f), BF16), jax.ShapeDtypeStruct((2, t, f), BF16)],
        grid=(nf,),
        in_specs=[ca(t), cbs(t), ca(3), cbs(3), ca(1), cbs(1)],
        out_specs=[ca(t), pl.BlockSpec((2, t, cbk), lambda j: (0, 0, j))],
        compiler_params=_cparams(("parallel",)),
    )(u, u, cw, cw, cb, cb)


def _conv_bwd(u, uc, cw, dh):
    t, f2 = u.shape
    f = f2 // 2
    cbk = _tile(f, 256)
    nf = f // cbk

    def body(ua_ref, ub_ref, uc_ref, cwa_ref, cwb_ref, dh_ref, du_ref, dcw_ref, dcb_ref):
        first, last = _row_ends((t, cbk))
        a, b = uc_ref[0].astype(F32), uc_ref[1].astype(F32)
        dh_v = dh_ref[...].astype(F32)
        sg = jax.nn.sigmoid(a)
        db = dh_v * (a * sg)
        da = dh_v * b * (sg * (1.0 + a * (1.0 - sg)))
        for idx, (dv, u_ref, cw_ref) in enumerate(((da, ua_ref, cwa_ref), (db, ub_ref, cwb_ref))):
            uu, cwv = u_ref[...].astype(F32), cw_ref[...]
            up, dn = _shift_up(dv, last), _shift_dn(dv, first)
            dcb_ref[idx] = _colsum(dv)
            dcw_ref[idx, 0:1, :] = _colsum(up * uu)
            dcw_ref[idx, 1:2, :] = _colsum(dv * uu)
            dcw_ref[idx, 2:3, :] = _colsum(dn * uu)
            du_ref[idx] = (cwv[0:1, :] * up + cwv[1:2, :] * dv + cwv[2:3, :] * dn).astype(BF16)

    ca = lambda r: pl.BlockSpec((r, cbk), lambda j: (0, j))
    cbs = lambda r: pl.BlockSpec((r, cbk), lambda j: (0, nf + j))
    o3 = lambda r: pl.BlockSpec((2, r, cbk), lambda j: (0, 0, j))
    return _pcall(
        body,
        name="conv_gate_bwd",
        out_shape=[jax.ShapeDtypeStruct((2, t, f), BF16), jax.ShapeDtypeStruct((2, 3, f), F32),
                   jax.ShapeDtypeStruct((2, 1, f), F32)],
        grid=(nf,),
        in_specs=[ca(t), cbs(t), o3(t), ca(3), cbs(3), ca(t)],
        out_specs=[o3(t), o3(3), o3(1)],
        compiler_params=_cparams(("parallel",)),
    )(u, u, uc, cw, cw, dh)


def _attention_fwd(q, kk, vv, *, hq, hkv, dk, dv, k_blk0, v_blk0, name):
    t = q.shape[0]
    tk = kk.shape[0]
    g_sz = hq // hkv
    tq = min(ATT_Q_BLOCK_FWD, t)

    def body(q_ref, k_ref, v_ref, o_ref, lse_ref):
        k = k_ref[...]
        v = v_ref[...]
        for j in range(g_sz):
            s = lax.dot_general(q_ref[:, j * dk:(j + 1) * dk], k, _DIMS["nt"], preferred_element_type=F32)
            m = jnp.max(s, axis=-1, keepdims=True)
            p = jnp.exp2(s - m)
            l = jnp.sum(p, axis=-1, keepdims=True)
            o = jnp.dot(p.astype(BF16), v, preferred_element_type=F32) / l
            o_ref[:, j * dv:(j + 1) * dv] = o.astype(BF16)
            lse_ref[0, :, j:j + 1] = m + jnp.log2(l)

    return _pcall(
        body,
        name=name,
        out_shape=[jax.ShapeDtypeStruct((t, hq * dv), BF16), jax.ShapeDtypeStruct((hkv, t, g_sz), F32)],
        grid=(hkv, t // tq),
        in_specs=[
            pl.BlockSpec((tq, g_sz * dk), lambda g, i: (i, g)),
            pl.BlockSpec((tk, dk), lambda g, i: (0, k_blk0 + g)),
            pl.BlockSpec((tk, dv), lambda g, i: (0, v_blk0 + g)),
        ],
        out_specs=[
            pl.BlockSpec((tq, g_sz * dv), lambda g, i: (i, g)),
            pl.BlockSpec((1, tq, g_sz), lambda g, i: (g, i, 0)),
        ],
        compiler_params=_cparams(("parallel", "parallel")),
    )(q, kk, vv)


def _attention_bwd(q, kk, vv, do, lse, *, hq, hkv, dk, dv, k_blk0, v_blk0, name):
    t = q.shape[0]
    tk = kk.shape[0]
    g_sz = hq // hkv
    tq = min(ATT_Q_BLOCK, t)

    def body(q_ref, k_ref, v_ref, do_ref, lse_ref, dq_ref, dk_ref, dv_ref):
        @pl.when(pl.program_id(1) == 0)
        def _():
            dk_ref[...] = jnp.zeros_like(dk_ref)
            dv_ref[...] = jnp.zeros_like(dv_ref)

        k = k_ref[...]
        v = v_ref[...]
        for j in range(g_sz):
            qj = q_ref[:, j * dk:(j + 1) * dk]
            doj = do_ref[:, j * dv:(j + 1) * dv]
            s = lax.dot_general(qj, k, _DIMS["nt"], preferred_element_type=F32)
            p = jnp.exp2(s - lse_ref[0, :, j:j + 1])
            dp = lax.dot_general(doj, v, _DIMS["nt"], preferred_element_type=F32)
            ds = (p * (dp - jnp.sum(p * dp, axis=-1, keepdims=True))).astype(BF16)
            dv_ref[...] += lax.dot_general(p.astype(BF16), doj, _DIMS["tn"], preferred_element_type=F32)
            dk_ref[...] += lax.dot_general(ds, qj, _DIMS["tn"], preferred_element_type=F32)
            dq_ref[:, j * dk:(j + 1) * dk] = jnp.dot(ds, k, preferred_element_type=F32)

        @pl.when(pl.program_id(1) == t // tq - 1)
        def _():
            dk_ref[...] *= LN2

    return _pcall(
        body,
        name=name,
        out_shape=[jax.ShapeDtypeStruct((t, hq * dk), F32), jax.ShapeDtypeStruct((tk, hkv * dk), F32),
                   jax.ShapeDtypeStruct((tk, hkv * dv), F32)],
        grid=(hkv, t // tq),
        in_specs=[
            pl.BlockSpec((tq, g_sz * dk), lambda g, i: (i, g)),
            pl.BlockSpec((tk, dk), lambda g, i: (0, k_blk0 + g)),
            pl.BlockSpec((tk, dv), lambda g, i: (0, v_blk0 + g)),
            pl.BlockSpec((tq, g_sz * dv), lambda g, i: (i, g)),
            pl.BlockSpec((1, tq, g_sz), lambda g, i: (g, i, 0)),
        ],
        out_specs=[
            pl.BlockSpec((tq, g_sz * dk), lambda g, i: (i, g)),
            pl.BlockSpec((tk, dk), lambda g, i: (0, g)),
            pl.BlockSpec((tk, dv), lambda g, i: (0, g)),
        ],
        compiler_params=_cparams(("parallel", "arbitrary")),
    )(q, kk, vv, do, lse)


def _silu(v):
    return v * jax.nn.sigmoid(v)


def _ada_fwd(conds, w_ada, b_ada_shard):
    r, d = conds.shape
    n = w_ada.shape[1]
    tn = _tile(n, 512)

    def body(c_ref, w_ref, b_ref, o_ref):
        s = _silu(c_ref[...]).astype(BF16)
        o_ref[...] = jnp.dot(s, w_ref[...].astype(BF16), preferred_element_type=F32) + b_ref[...]

    return _pcall(
        body,
        name="ada_fwd",
        out_shape=jax.ShapeDtypeStruct((r, n), F32),
        grid=(n // tn,),
        in_specs=[pl.BlockSpec((r, d), lambda j: (0, 0)), pl.BlockSpec((d, tn), lambda j: (0, j)),
                  pl.BlockSpec((1, tn), lambda j: (0, j))],
        out_specs=pl.BlockSpec((r, tn), lambda j: (0, j)),
        compiler_params=_cparams(("parallel",)),
    )(conds, w_ada, b_ada_shard)


def _cctx_partial(da16_shard, w_ada, c_ctx_row):
    d, n = w_ada.shape
    td = _tile(d, 512)

    def body(g_ref, w_ref, c_ref, o_ref):
        ds = lax.dot_general(g_ref[8:16, :].astype(BF16), w_ref[...].astype(BF16), _DIMS["nt"],
                             preferred_element_type=F32)
        cv = c_ref[...]
        sg = jax.nn.sigmoid(cv)
        o_ref[...] = ds * (sg * (1.0 + cv * (1.0 - sg)))

    return _pcall(
        body,
        name="cctx_partial",
        out_shape=jax.ShapeDtypeStruct((8, d), F32),
        grid=(d // td,),
        in_specs=[pl.BlockSpec((16, n), lambda j: (0, 0)), pl.BlockSpec((td, n), lambda j: (j, 0)),
                  pl.BlockSpec((1, td), lambda j: (0, j))],
        out_specs=pl.BlockSpec((8, td), lambda j: (0, j)),
        compiler_params=_cparams(("parallel",)),
    )(da16_shard, w_ada, c_ctx_row)


def _sum_parts(parts):
    p, _, n = parts.shape

    def body(p_ref, o_ref):
        acc = p_ref[0]
        for s in range(1, p):
            acc = acc + p_ref[s]
        o_ref[...] = acc

    return _pcall(
        body,
        name="sum_parts",
        out_shape=jax.ShapeDtypeStruct((1, n), F32),
        in_specs=[pl.BlockSpec(memory_space=pltpu.VMEM)],
        out_specs=pl.BlockSpec(memory_space=pltpu.VMEM),
    )(parts)


def _adam_math(w, g, m, v):
    m2 = ADAM_B1 * m + (1.0 - ADAM_B1) * g
    v2 = ADAM_B2 * v + (1.0 - ADAM_B2) * jnp.square(g)
    m_hat = m2 / (1.0 - ADAM_B1 ** ADAM_STEP)
    v_hat = v2 / (1.0 - ADAM_B2 ** ADAM_STEP)
    delta = -ADAM_LR * (m_hat / (jnp.sqrt(v_hat) + ADAM_EPS) + ADAM_WD * w)
    return delta, m2, v2


def _adamw(parts, w, m, v, name):
    p, r, c = parts.shape
    block_elems = 1 << 18
    rb, cb = _tile(r, max(8, block_elems // c // 8 * 8), 8), c
    if rb * c < block_elems // 4 and r * c > block_elems:
        rb, cb = r, _tile(c, max(LANE, block_elems // r // LANE * LANE))

    def body(p_ref, w_ref, m_ref, v_ref, g_ref, d_ref, m2_ref, v2_ref):
        g = p_ref[0].astype(F32)
        for s in range(1, p):
            g = g + p_ref[s].astype(F32)
        g_ref[...] = g
        d_ref[...], m2_ref[...], v2_ref[...] = _adam_math(w_ref[...], g, m_ref[...], v_ref[...])

    if w.ndim == 3:
        blk = pl.BlockSpec((None, rb, cb), lambda i, j: (0, i, j))
    else:
        blk = pl.BlockSpec((rb, cb), lambda i, j: (i, j))
    return _pcall(
        body,
        name=name,
        out_shape=[jax.ShapeDtypeStruct(w.shape, F32)] * 4,
        grid=(r // rb, c // cb),
        in_specs=[pl.BlockSpec((p, rb, cb), lambda i, j: (0, i, j)), blk, blk, blk],
        out_specs=[blk] * 4,
        compiler_params=_cparams(("parallel", "parallel")),
    )(parts, w, m, v)


def _adamw_ada(conds, da16, w, m, v):
    d, n = w.shape
    rb = _tile(d, 256, LANE)

    def body(s_ref, da_ref, w_ref, m_ref, v_ref, g_ref, d_ref, m2_ref, v2_ref):
        g = lax.dot_general(_silu(s_ref[...]).astype(BF16), da_ref[...].astype(BF16), _DIMS["tn"],
                            preferred_element_type=F32)
        g_ref[...] = g
        d_ref[...], m2_ref[...], v2_ref[...] = _adam_math(w_ref[...], g, m_ref[...], v_ref[...])

    row = pl.BlockSpec((rb, n), lambda i: (i, 0))
    return _pcall(
        body,
        name="adamw_w_ada",
        out_shape=[jax.ShapeDtypeStruct((d, n), F32)] * 4,
        grid=(d // rb,),
        in_specs=[pl.BlockSpec((16, rb), lambda i: (0, i)), pl.BlockSpec((16, n), lambda i: (0, 0)), row, row, row],
        out_specs=[row] * 4,
        compiler_params=_cparams(("parallel",)),
    )(conds, da16, w, m, v)


def _touch(arrays, name):
    def body(*refs):
        refs[-1][...] = jnp.zeros((8, LANE), F32)

    return _pcall(body, name=name, out_shape=jax.ShapeDtypeStruct((8, LANE), F32),
                  in_specs=[pl.BlockSpec(memory_space=pl.ANY)] * len(arrays),
                  out_specs=pl.BlockSpec(memory_space=pltpu.VMEM))(*arrays)


def _cast_bf16(a, name):
    _, r, c = a.shape
    rb = _tile(r, 512, 8)

    def body(a_ref, o_ref):
        o_ref[...] = a_ref[...].astype(BF16)

    return _pcall(body, name=name, out_shape=jax.ShapeDtypeStruct((r, c), BF16), grid=(r // rb,),
                  in_specs=[pl.BlockSpec((None, rb, c), lambda i: (0, i, 0))],
                  out_specs=pl.BlockSpec((rb, c), lambda i: (i, 0)), compiler_params=_cparams(("parallel",)))(a)


def _rope_tabs(t, rot):
    half, q = rot // 2, rot // 4
    n_rows = t // GRID_W
    row = jnp.repeat(jnp.arange(n_rows, dtype=F32), GRID_W)
    col = jnp.tile(jnp.arange(GRID_W, dtype=F32), n_rows)
    inv_freq = ROPE_THETA ** (-jnp.arange(0, half, 2, dtype=F32) / half)
    ang = jnp.concatenate([row[:, None] * inv_freq, col[:, None] * inv_freq], axis=-1)
    cos, sin = jnp.cos(ang), jnp.sin(ang)
    c0, c1, s0, s1 = cos[:, :q], cos[:, q:], sin[:, :q], sin[:, q:]
    z = jnp.zeros_like(s0)
    return (jnp.concatenate([c0, c0, c1, c1], -1), jnp.concatenate([-s0, z, -s1, z], -1),
            jnp.concatenate([z, s0, z, s1], -1))


def _pad_cols(a, left, total, fill=0.0):
    return jnp.pad(a, ((0, 0), (left, total - left - a.shape[1])), constant_values=fill)


def _with_ctx_rows(tab, tc, fill):
    return jnp.concatenate([tab, jnp.full((tc, tab.shape[1]), fill, F32)], axis=0)


def kernel(x, c, ctx, c_ctx, w_ada, b_ada, norm1_g, w_in, mla_q_norm_g, w_q_up, mla_kv_norm_g, w_kv_up, gqa_q_norm_g, gqa_k_norm_g, w_br_a, w_br_b, w_out, norm2_g, w_up, conv_w, conv_b, w_down, final_norm_g, loss_target, m_c_ctx, m_w_ada, m_b_ada, m_norm1_g, m_w_in, m_mla_q_norm_g, m_w_q_up, m_mla_kv_norm_g, m_w_kv_up, m_gqa_q_norm_g, m_gqa_k_norm_g, m_w_br_a, m_w_br_b, m_w_out, m_norm2_g, m_w_up, m_conv_w, m_conv_b, m_w_down, m_final_norm_g, v_c_ctx, v_w_ada, v_b_ada, v_norm1_g, v_w_in, v_mla_q_norm_g, v_w_q_up, v_mla_kv_norm_g, v_w_kv_up, v_gqa_q_norm_g, v_gqa_k_norm_g, v_w_br_a, v_w_br_b, v_w_out, v_norm2_g, v_w_up, v_conv_w, v_conv_b, v_w_down, v_final_norm_g):
    weights = dict(c_ctx=c_ctx, w_ada=w_ada, b_ada=b_ada, norm1_g=norm1_g, w_in=w_in, mla_q_norm_g=mla_q_norm_g,
                   w_q_up=w_q_up, mla_kv_norm_g=mla_kv_norm_g, w_kv_up=w_kv_up, gqa_q_norm_g=gqa_q_norm_g,
                   gqa_k_norm_g=gqa_k_norm_g, w_br_a=w_br_a, w_br_b=w_br_b, w_out=w_out, norm2_g=norm2_g, w_up=w_up,
                   conv_w=conv_w, conv_b=conv_b, w_down=w_down, final_norm_g=final_norm_g)
    mom_m = dict(c_ctx=m_c_ctx, w_ada=m_w_ada, b_ada=m_b_ada, norm1_g=m_norm1_g, w_in=m_w_in, mla_q_norm_g=m_mla_q_norm_g,
                 w_q_up=m_w_q_up, mla_kv_norm_g=m_mla_kv_norm_g, w_kv_up=m_w_kv_up, gqa_q_norm_g=m_gqa_q_norm_g,
                 gqa_k_norm_g=m_gqa_k_norm_g, w_br_a=m_w_br_a, w_br_b=m_w_br_b, w_out=m_w_out, norm2_g=m_norm2_g,
                 w_up=m_w_up, conv_w=m_conv_w, conv_b=m_conv_b, w_down=m_w_down, final_norm_g=m_final_norm_g)
    mom_v = dict(c_ctx=v_c_ctx, w_ada=v_w_ada, b_ada=v_b_ada, norm1_g=v_norm1_g, w_in=v_w_in, mla_q_norm_g=v_mla_q_norm_g,
                 w_q_up=v_w_q_up, mla_kv_norm_g=v_mla_kv_norm_g, w_kv_up=v_w_kv_up, gqa_q_norm_g=v_gqa_q_norm_g,
                 gqa_k_norm_g=v_gqa_k_norm_g, w_br_a=v_w_br_a, w_br_b=v_w_br_b, w_out=v_w_out, norm2_g=v_norm2_g,
                 w_up=v_w_up, conv_w=v_conv_w, conv_b=v_conv_b, w_down=v_w_down, final_norm_g=v_final_norm_g)
    order = list(weights)

    my_idx = 4 * lax.axis_index("x") + 2 * lax.axis_index("y") + lax.axis_index("c")
    xs, cts, tgt = x[0], ctx[0], loss_target[0]
    t, d = xs.shape
    tc = cts.shape[0]
    ta = t + tc
    kvl, ql = MLA_KV_LORA, MLA_Q_LORA
    nb = GQA_KV_HEADS * GQA_HEAD_DIM
    hb = GQA_HEADS * GQA_HEAD_DIM
    ha = MLA_HEADS
    f2 = w_up.shape[2] * N_DEV
    ff = f2 // 2

    big = ["w_in", "w_q_up", "w_kv_up", "w_br_a", "w_br_b", "w_out", "w_up", "w_down"]
    nw = len(big)
    del nw
    _ORDER_AFTER.clear()
    narrow = ("w_in", "w_q_up")

    def tview(a):
        return jnp.transpose(a, (0, 2, 1))

    shards = {"w_in": _cast_bf16(tview(weights["w_in"]), "cast_w_in")}
    c_idx = jnp.reshape(lax.axis_index("c"), (1,)).astype(jnp.int32)

    def gather_start(names, dep):
        shs = [shards[n] for n in names]
        land = [lax.empty((N_DEV,) + s.shape, BF16) for s in shs]
        if dep is not None:
            _after(dep)
        s, r, arrs, tok = _split_start("gather_ici_start_" + names[0], shs + land, _gather_ici_copies(len(names)),
                                       4 * len(names))
        return dict(names=names, s=s, r=r, arrs=arrs, tok=tok)

    def gather_pass(g, after):
        n = len(g["names"])
        arrs = _split_wait("gather_ici_wait_" + g["names"][0], g["s"], g["r"], g["arrs"], _gather_ici_copies(n), after)
        s, r, bufs, tok = _split_start("gather_pass_start_" + g["names"][0], arrs[n:], _gather_pass_copies(n), 3 * n)
        g.update(s2=s, r2=r, bufs=bufs)
        return tok

    def gather_relay(g, after):
        n = len(g["names"])
        bufs = _split_wait("gather_pass_wait_" + g["names"][0], g["s2"], g["r2"], g["bufs"], _gather_pass_copies(n), after)
        s, r, bufs, tok = _split_start("gather_d2d_start_" + g["names"][0], bufs, _gather_d2d_copies(n), n)
        g.update(s3=s, r3=r, bufs=bufs)
        return tok

    def gather_finish(g, after):
        n = len(g["names"])
        bufs = _split_wait("gather_d2d_wait_" + g["names"][0], g["s3"], g["r3"], g["bufs"], _gather_d2d_copies(n), after)
        return dict(zip(g["names"], bufs))

    c_all, cw_all = _all_gather([jnp.pad(c, ((0, 7), (0, 0))), jnp.pad(conv_w[0], ((0, 5), (0, 0)))], "gather_cond")
    conv_w_f = jnp.transpose(cw_all[:, :3, :], (1, 0, 2)).reshape(3, f2)
    conds = jnp.concatenate([c_all[:, 0, :], c_ctx[None, :], jnp.zeros((7, d), F32)], axis=0)
    ncol = w_ada.shape[2]
    b_shard = lax.dynamic_slice_in_dim(b_ada, my_idx * ncol, ncol, axis=1)
    ada_shard = _ada_fwd(conds, w_ada[0], b_shard)
    (ada_all,) = _all_gather([ada_shard], "gather_ada")
    ada = jnp.transpose(ada_all, (1, 0, 2)).reshape(16, N_DEV * ncol)
    lat = lax.dynamic_slice_in_dim(ada, my_idx, 1, axis=0).reshape(6, d)
    cxt = ada[8].reshape(6, d)
    zero2 = jnp.zeros((2, d), F32)
    mods1 = jnp.concatenate([lat[0:2], cxt[0:2], jnp.zeros((4, d), F32)], axis=0)
    mods2 = jnp.concatenate([lat[2:3], lat[3:4], lat[4:5], jnp.zeros((5, d), F32)], axis=0)
    mods2b = jnp.concatenate([lat[2:3], lat[4:5], jnp.zeros((6, d), F32)], axis=0)
    mods3 = jnp.concatenate([lat[5:6], jnp.zeros((7, d), F32)], axis=0)
    del zero2

    g0 = gather_start(["w_in"], ada_all)
    for n in big[1:]:
        _after(g0["tok"])
        shards[n] = _cast_bf16(tview(weights[n]) if n in narrow else weights[n], "cast_" + n)

    ca, s1a, s2a = _rope_tabs(t, MLA_ROPE)
    cb_, s1b, s2b = _rope_tabs(t, GQA_HEAD_DIM)
    q_tabs_a = (_pad_cols(jnp.concatenate([jnp.ones((t, MLA_NOPE), F32), ca], 1), 0, MLA_SLOT),
                _pad_cols(s1a, MLA_NOPE, MLA_SLOT), _pad_cols(s2a, MLA_NOPE, MLA_SLOT))
    q_tabs_b = (cb_, s1b, s2b)
    k_tabs = (_with_ctx_rows(_pad_cols(ca, 0, LANE), tc, 1.0), _with_ctx_rows(_pad_cols(s1a, 0, LANE), tc, 0.0),
              _with_ctx_rows(_pad_cols(s2a, 0, LANE), tc, 0.0),
              _with_ctx_rows(cb_, tc, 1.0), _with_ctx_rows(s1b, tc, 0.0), _with_ctx_rows(s2b, tc, 0.0))

    def cols_full(g):
        return jnp.transpose(g, (1, 0, 2)).reshape(g.shape[1], N_DEV * g.shape[2])

    _after(*q_tabs_a, *q_tabs_b, *k_tabs, *[shards[n] for n in big[1:]])
    tok_p0 = gather_pass(g0, mods1)
    g1 = gather_start(["w_q_up", "w_kv_up", "w_br_a", "w_br_b", "w_out"], tok_p0)
    _after(g1["tok"])
    z_all = _norm_mod_fwd(cts, xs, norm1_g, mods1)
    gathered = gather_finish(g0, gather_relay(g0, z_all))
    wt_in = gathered["w_in"].reshape(-1, d)
    o_kpe, o_kb, o_vb = kvl, kvl + MLA_ROPE, kvl + MLA_ROPE + nb
    o_q = o_vb + nb
    o_g = o_q + ql + hb
    wkv_w = kvl + 2 * nb + LANE
    wt_kv_p = jnp.concatenate([wt_in[:kvl], wt_in[o_kb:o_q], wt_in[o_kpe:o_kb],
                               jnp.zeros((LANE - MLA_ROPE, d), BF16)], axis=0)
    q_w = ql + hb
    q_pad = (-q_w) % 512 if d >= 512 else (-q_w) % d
    gate_blk = (q_w + q_pad) // d
    assert (q_w + q_pad) % d == 0
    wt_qg_p = jnp.concatenate([wt_in[o_q:o_g], jnp.zeros((q_pad, d), BF16), wt_in[o_g:]], axis=0)

    kv_all = _mm(z_all, wt_kv_p, "nt", F32, "proj_kv", tm=1152, tn=wkv_w)
    qg = _mm(z_all, wt_qg_p, "nt", F32, "proj_qg", tm=1024, tn=1024, rows=t)
    tok_p1 = gather_pass(g1, qg)
    g2 = gather_start(["w_up"], tok_p1)
    g3 = gather_start(["w_down"], g2["tok"])
    _after(g3["tok"])
    kin, k_b, v_b = _key_prep_fwd(kv_all, mla_kv_norm_g, gqa_k_norm_g, k_tabs)
    sc_a = float((MLA_NOPE + MLA_ROPE) ** -0.5) * LOG2E
    sc_b = float(GQA_HEAD_DIM ** -0.5) * LOG2E
    _after(g3["tok"])
    cqn, q_b = _q_prep_fwd(qg, mla_q_norm_g, gqa_q_norm_g, q_tabs_b, sc_b)
    _after(kin, g3["tok"])
    gathered.update(gather_finish(g1, gather_relay(g1, q_b)))

    wqt_f = gathered["w_q_up"].reshape(ha, MLA_NOPE + MLA_ROPE, ql)
    wqt_ext = jnp.pad(wqt_f, ((0, 0), (0, MLA_SLOT - MLA_NOPE - MLA_ROPE), (0, 0))).reshape(ha * MLA_SLOT, ql)
    wkv_f = cols_full(gathered["w_kv_up"]).reshape(kvl, ha, MLA_NOPE + MLA_V)
    wk_slots = jnp.pad(wkv_f[:, :, :MLA_NOPE], ((0, 0), (0, 0), (0, MLA_SLOT - MLA_NOPE))).reshape(kvl, ha * MLA_SLOT)
    wv_cols = wkv_f[:, :, MLA_NOPE:].reshape(kvl, ha * MLA_V)
    e_slot = jnp.pad(jnp.eye(MLA_ROPE, dtype=BF16),
                     ((0, LANE - MLA_ROPE), (MLA_NOPE, MLA_SLOT - MLA_NOPE - MLA_ROPE)))
    e_rows = jnp.concatenate([jnp.tile(e_slot, (1, ha)), jnp.zeros((LANE, ha * MLA_V), BF16)], axis=1)
    wkv_ext = jnp.concatenate([jnp.concatenate([wk_slots, wv_cols], axis=1), e_rows], axis=0)
    w_bra = cols_full(gathered["w_br_a"])
    w_brb = cols_full(gathered["w_br_b"])
    w_out_f = gathered["w_out"].reshape(d, d)

    kv_a = _mm(kin, wkv_ext, "nn", BF16, "kv_up", tm=1152, tn=1024)
    qa_raw = _mm(cqn, wqt_ext, "nt", F32, "q_up", tm=1024, tn=1024)
    q_a = _rope_a(qa_raw, q_tabs_a, False, BF16, "rope_q_fwd", sc_a)
    att_a = dict(hq=ha, hkv=ha, dk=MLA_SLOT, dv=MLA_V, k_blk0=0, v_blk0=ha * MLA_SLOT // MLA_V)
    att_b = dict(hq=GQA_HEADS, hkv=GQA_KV_HEADS, dk=GQA_HEAD_DIM, dv=GQA_HEAD_DIM, k_blk0=0, v_blk0=0)
    o_a, lse_a = _attention_fwd(q_a, kv_a, kv_a, name="attn_a_fwd", **att_a)
    o_b, lse_b = _attention_fwd(q_b, k_b, v_b, name="attn_b_fwd", **att_b)
    _after(o_a)
    _after(gather_pass(g2, o_b))
    pa = _mm(o_a, w_bra, "nn", BF16, "br_a", tm=1024, tn=1024)
    pb = _mm(o_b, w_brb, "nn", BF16, "br_b", tm=1024, tn=1024)
    merged = _merge_fwd(pa, pb, qg, gate_blk)
    attn = _mm(merged, w_out_f, "nn", F32, "w_out", tm=1024, tn=1024)
    x1, z2 = _resid_norm_mod(xs, attn, norm2_g, mods2, "resid_norm2_fwd")
    tok_r2 = gather_relay(g2, z2)
    tok_p3 = gather_pass(g3, tok_r2)
    w_up3 = gather_finish(g2, tok_p3)["w_up"]
    u = _mm_up_fwd(z2, w_up3, "w_up")
    tok_r3 = gather_relay(g3, u)
    _after(tok_r3)
    h, uc = _conv_fwd(u, conv_w_f, conv_b)
    w_down_f = gather_finish(g3, h)["w_down"].reshape(ff, d)
    ffn = _mm(h, w_down_f, "nn", F32, "w_down", tm=1024, tn=1024, tk=2816)

    def to_shards(g):
        return jnp.transpose(g.reshape(g.shape[0], N_DEV, g.shape[1] // N_DEV), (1, 0, 2))

    def reduce_start(tag, names, sends):
        n = len(sends)
        land = [lax.empty((4,) + s.shape[1:], s.dtype) for s in sends]
        s, r, arrs, tok = _split_start("reduce_d2d_start_" + tag, sends + land, _reduce_d2d_copies(n), 4 * n)
        return dict(tag=tag, names=names, s=s, r=r, arrs=arrs, tok=tok)

    def reduce_relay(g, after):
        n = len(g["names"])
        arrs = _split_wait("reduce_d2d_wait_" + g["tag"], g["s"], g["r"], g["arrs"], _reduce_d2d_copies(n), after)
        sums = [_pair_sum(arrs[a], arrs[n + a], c_idx, "pair_sum_" + g["names"][a]) for a in range(n)]
        land = [lax.empty(s.shape, s.dtype) for s in sums]
        s, r, arrs2, tok = _split_start("reduce_ici_start_" + g["tag"], sums + land, _reduce_ici_copies(n), 4 * n)
        g.update(s2=s, r2=r, arrs2=arrs2)
        return tok

    def reduce_finish(g, after):
        n = len(g["names"])
        arrs2 = _split_wait("reduce_ici_wait_" + g["tag"], g["s2"], g["r2"], g["arrs2"], _reduce_ici_copies(n), after)
        return dict(zip(g["names"], arrs2[n:]))

    dx2, dffn, st_fin = _final_loss(x1, ffn, final_norm_g[None, :], mods3, tgt)
    dh = _mm(dffn, w_down_f, "nt", BF16, "d_h", tm=1024, tn=1024)
    g_w_down = _mm(h, dffn, "tn", BF16, "g_w_down", tm=512, tn=1024)
    r_down = reduce_start("down", ["w_down"], [g_w_down.reshape(N_DEV, ff // N_DEV, d)])
    _after(r_down["tok"])
    du3, dcw, dcb = _conv_bwd(u, uc, conv_w_f, dh)
    dz2 = _mm_up_dz(du3, w_up3, "d_z2")
    g_w_up = _mm_up_gw(z2, du3, N_DEV, "g_w_up")
    g_conv_w = jnp.concatenate([dcw[0], dcw[1]], axis=1)
    tok = reduce_relay(r_down, g_w_up)
    _after(tok)
    r_up = reduce_start("up", ["w_up", "conv_w"], [g_w_up, to_shards(jnp.pad(g_conv_w, ((0, 5), (0, 0))))])
    _after(tok, r_up["tok"])
    dx1, dattn, st_n2 = _norm2_bwd(x1, attn, norm2_g, mods2b, dz2, dx2)
    dmerged = _mm(dattn, w_out_f, "nt", BF16, "d_merged", tm=1024, tn=1024)
    g_w_out = _mm(merged, dattn, "tn", BF16, "g_w_out", tm=1024, tn=1024)
    dpa, dpb, dgates = _merge_bwd(dmerged, pa, pb, qg, gate_blk)
    do_a = _mm(dpa, w_bra, "nt", BF16, "d_o_a", tm=1024, tn=1024)
    do_b = _mm(dpb, w_brb, "nt", BF16, "d_o_b", tm=1024, tn=1024)
    g_w_bra = _mm(o_a, dpa, "tn", BF16, "g_w_br_a", tm=1024, tn=1024)
    g_w_brb = _mm(o_b, dpb, "tn", BF16, "g_w_br_b", tm=1024, tn=1024)
    tok = reduce_relay(r_up, g_w_brb)
    _after(tok)
    r_out = reduce_start("out", ["w_out", "w_br_a", "w_br_b"],
                         [g_w_out.reshape(N_DEV, d // N_DEV, d), to_shards(g_w_bra), to_shards(g_w_brb)])
    _after(tok, r_out["tok"])
    dq_a, dk_a, dv_a = _attention_bwd(q_a, kv_a, kv_a, do_a, lse_a, name="attn_a_bwd", **att_a)
    dq_b, dk_b, dv_b = _attention_bwd(q_b, k_b, v_b, do_b, lse_b, name="attn_b_bwd", **att_b)
    _after(reduce_relay(r_out, dv_b))
    dqa_raw = _rope_a(dq_a, q_tabs_a, True, BF16, "rope_q_bwd", sc_a * LN2)
    dcqn = _mm(dqa_raw, wqt_ext, "nn", F32, "d_cqn", tm=1024, tn=ql)
    g_wqt_ext = _mm(dqa_raw, cqn, "tn", BF16, "g_w_q_up", tm=1024, tn=ql)
    dq_p, st_q, st_qb = _q_prep_bwd(qg, mla_q_norm_g, gqa_q_norm_g, q_tabs_b, dcqn, dq_b, q_pad, sc_b * LN2)
    dkin = _mm_cat_nt([(dk_a, wkv_ext, 0), (dv_a, wkv_ext, ha * MLA_SLOT)], F32, "d_kin", tm=1152, tn=kvl + LANE)
    g_wkv_ext = _mm_cat_tn(kin, [dk_a, dv_a], BF16, "g_w_kv_up", tm=kvl + LANE, tn=min(1024, ha * MLA_V))
    dkv_p, st_kv, st_kb = _key_prep_bwd(kv_all, mla_kv_norm_g, gqa_k_norm_g, k_tabs, dkin, dk_b, dv_b)
    g_wqt = g_wqt_ext.reshape(ha, MLA_SLOT, ql)[:, :MLA_NOPE + MLA_ROPE, :].reshape(N_DEV, -1, ql)
    g_wkv = jnp.concatenate([g_wkv_ext[:kvl, :ha * MLA_SLOT].reshape(kvl, ha, MLA_SLOT)[:, :, :MLA_NOPE],
                             g_wkv_ext[:kvl, ha * MLA_SLOT:].reshape(kvl, ha, MLA_V)], axis=2).reshape(kvl, ha * (MLA_NOPE + MLA_V))
    r_qkv = reduce_start("qkv", ["w_q_up", "w_kv_up"], [g_wqt, to_shards(g_wkv)])
    _after(r_qkv["tok"])
    g_wkv_p = _mm(dkv_p, z_all, "tn", BF16, "g_w_in_kv", tm=wkv_w, tn=1024)
    g_wqg_p = _mm_rows_tn([dq_p, dgates], z_all, BF16, "g_w_in_qg", tm=min(1024, d), tn=1024, rows=t)
    g_wt_in = jnp.concatenate([g_wkv_p[:kvl], g_wkv_p[kvl + 2 * nb:kvl + 2 * nb + MLA_ROPE],
                               g_wkv_p[kvl:kvl + 2 * nb], g_wqg_p[:q_w], g_wqg_p[q_w + q_pad:]], axis=0)
    r_in = reduce_start("in", ["w_in"], [g_wt_in.reshape(N_DEV, -1, d)])
    _after(r_in["tok"])
    qw_p = q_w + q_pad
    dz_lat = _mm_sum_nn([(dq_p, 0, wt_qg_p, 0, qw_p), (dgates, 0, wt_qg_p, qw_p, d), (dgates, d, wt_qg_p, qw_p + d, d),
                         (dkv_p, 0, wt_kv_p, 0, wkv_w)], F32, "d_z_lat", rows=t)
    dz_ctx = _mm(dkv_p, wt_kv_p, "nn", F32, "d_z_ctx", tm=min(ROW_BLOCK, tc), tn=1024, a_row_off=t)
    tok_q = reduce_relay(r_qkv, dz_ctx)
    _after(tok_q)
    grad_x, st_n1 = _norm1_bwd(cts, xs, norm1_g, mods1, dz_ctx, dz_lat, dx1)

    res = {}

    def upd(nm, parts):
        wv, mv, vv = weights[nm], mom_m[nm], mom_v[nm]
        if wv.ndim == 1:
            wv, mv, vv = (a.reshape(1, -1) for a in (wv, mv, vv))
        if nm in narrow:
            wv, mv, vv = tview(wv), tview(mv), tview(vv)
        outs = _adamw(parts, wv, mv, vv, "adamw_" + nm)
        if nm in narrow:
            outs = [tview(o_) for o_ in outs]
        res[nm] = [o_.reshape(weights[nm].shape) for o_ in outs]

    d_lat = jnp.concatenate([st_n1[0], st_n1[1], st_n2[3], st_n2[0], st_n2[1], st_fin[1]])
    d_cxt = jnp.concatenate([st_n1[3], st_n1[4], jnp.zeros((4 * d,), F32)])
    small = jnp.concatenate([d_lat, d_cxt, st_n1[2], st_q[0], st_kv[0], st_qb[0], st_kb[0], st_n2[2],
                             jnp.concatenate([dcb[0, 0], dcb[1, 0]]), st_fin[0], st_fin[3, :LANE]])
    n_small = small.shape[0]
    pad_small = (-n_small) % LANE
    (small_all,) = _all_gather([jnp.pad(small, (0, pad_small)).reshape(1, -1)], "gather_small")
    offs = {}
    o = 0
    for nm, ln in (("d_lat", 6 * d), ("d_cxt", 6 * d), ("norm1_g", d), ("mla_q_norm_g", ql), ("mla_kv_norm_g", kvl),
                   ("gqa_q_norm_g", GQA_HEAD_DIM), ("gqa_k_norm_g", GQA_HEAD_DIM), ("norm2_g", d), ("conv_b", f2),
                   ("final_norm_g", d), ("loss", LANE)):
        offs[nm] = (o, ln)
        o += ln

    def part(nm):
        a, ln = offs[nm]
        return small_all[:, :, a:a + ln]

    loss = _sum_parts(part("loss"))[0, 0]
    d_lat_all = part("d_lat")[:, 0, :]
    d_cxt_sum = _sum_parts(part("d_cxt"))
    da16 = jnp.concatenate([d_lat_all, d_cxt_sum, jnp.zeros((7, 6 * d), F32)], axis=0)
    da16_shard = lax.dynamic_slice_in_dim(da16, my_idx * ncol, ncol, axis=1)
    cc_part = _cctx_partial(da16_shard, w_ada[0], c_ctx[None, :])
    (cc_all,) = _all_gather([cc_part], "gather_cctx")
    cc_parts = cc_all[:, 0:1, :]
    tok_i = reduce_relay(r_in, cc_all)

    _after(tok_i)
    for nm in ("norm1_g", "mla_q_norm_g", "mla_kv_norm_g", "gqa_q_norm_g", "gqa_k_norm_g", "norm2_g", "conv_b",
               "final_norm_g"):
        upd(nm, part(nm))
    upd("c_ctx", cc_parts)
    b_parts = jnp.concatenate([d_lat_all[:, None, :], d_cxt_sum[None]], axis=0)
    upd("b_ada", b_parts)
    _after(tok_i)
    outs = _adamw_ada(conds, da16_shard, w_ada[0], m_w_ada[0], v_w_ada[0])
    res["w_ada"] = [o_[None] for o_ in outs]
    last = outs[0]
    done = [last]
    for grp in (r_down, r_up, r_out, r_qkv, r_in):
        _after(*done)
        recv = reduce_finish(grp, last)
        for nm in grp["names"]:
            upd(nm, recv[nm][:, :3, :] if nm == "conv_w" else recv[nm])
            last = res[nm][0]
            done.append(last)

    return (loss, grad_x[None], *[res[n][0] for n in order], *[res[n][1] for n in order],
            *[res[n][2] for n in order], *[res[n][3] for n in order])
```

```python
import functools

import jax
import jax.numpy as jnp
from jax import lax
from jax.experimental import pallas as pl
from jax.experimental.pallas import tpu as pltpu

F32 = jnp.float32
BF16 = jnp.bfloat16

GRID_W = 64
ROPE_THETA = 10000.0
NORM_EPS = 1e-6
MLA_HEADS = 8
MLA_Q_LORA = 768
MLA_KV_LORA = 512
MLA_NOPE = 128
MLA_ROPE = 64
MLA_V = 128
GQA_HEADS = 8
GQA_KV_HEADS = 2
GQA_HEAD_DIM = 128
ADAM_LR = 0.001
ADAM_B1 = 0.9
ADAM_B2 = 0.999
ADAM_EPS = 1e-08
ADAM_WD = 0.01
ADAM_STEP = 10

N_DEV = 8
MESH_AXES = ("x", "y", "c")
LANE = 128
MLA_SLOT = 2 * LANE
VMEM_LIMIT = 56 * 1024 * 1024
ROW_BLOCK = 256
ATT_Q_BLOCK = 512
ATT_Q_BLOCK_FWD = 512
LN2 = 0.6931471805599453
LOG2E = 1.4426950408889634
MESH_ID = pl.DeviceIdType.MESH


def _tile(n, pref, align=LANE):
    if n <= pref:
        return n
    best = None
    t = align
    while t <= pref:
        if n % t == 0:
            best = t
        t += align
    assert best is not None, (n, pref, align)
    return best


def _cparams(sem=None):
    return pltpu.CompilerParams(dimension_semantics=sem, vmem_limit_bytes=VMEM_LIMIT)


_ORDER_AFTER = []


def _after(*arrays):
    _ORDER_AFTER.extend(arrays)


def _pcall(body, *, in_specs, **kw):
    deps = tuple(_ORDER_AFTER)
    _ORDER_AFTER.clear()
    if not deps:
        return pl.pallas_call(body, in_specs=in_specs, **kw)
    n_in, n_dep = len(in_specs), len(deps)

    def with_deps(*refs):
        body(*refs[:n_in], *refs[n_in + n_dep:])

    call = pl.pallas_call(with_deps, in_specs=list(in_specs) + [pl.BlockSpec(memory_space=pl.ANY)] * n_dep, **kw)
    return lambda *args: call(*args, *deps)


def _all_gather(arrs, name):
    n = len(arrs)

    def body(*refs):
        ins = refs[:n]
        outs = refs[n:2 * n]
        send_sems, recv_sems, local_sems = refs[2 * n:]
        x, y, c = lax.axis_index("x"), lax.axis_index("y"), lax.axis_index("c")
        me, sibling = (x, y, c), (x, y, 1 - c)
        chips = [(1 - x, y), (x, 1 - y), (1 - x, 1 - y)]

        def rows(a, dev):
            px, py, pc = dev
            return outs[a].at[4 * px + 2 * py + pc]

        def copy(a, k, block, to, src=None):
            return pltpu.make_async_remote_copy(
                src_ref=rows(a, block) if src is None else src,
                dst_ref=rows(a, block),
                send_sem=send_sems.at[7 * a + k],
                recv_sem=recv_sems.at[7 * a + k],
                device_id=to,
                device_id_type=MESH_ID,
            )

        mine = [pltpu.make_async_copy(ins[a], rows(a, me), local_sems.at[a]) for a in range(n)]
        for cp in mine:
            cp.start()
        first = []
        for a in range(n):
            first.append(copy(a, 0, me, sibling, src=ins[a]))
            first += [copy(a, 1 + j, me, (*chip, c), src=ins[a]) for j, chip in enumerate(chips)]
        for cp in first:
            cp.start()
        passed = []
        for j, chip in enumerate(chips):
            for a in range(n):
                copy(a, 1 + j, (*chip, c), me).wait_recv()
                fwd = copy(a, 4 + j, (*chip, c), sibling)
                fwd.start()
                passed.append(fwd)
        for a in range(n):
            copy(a, 0, sibling, me).wait_recv()
            for j, chip in enumerate(chips):
                copy(a, 4 + j, (*chip, 1 - c), me).wait_recv()
        for cp in first + passed:
            cp.wait_send()
        for cp in mine:
            cp.wait()

    any_spec = pl.BlockSpec(memory_space=pl.ANY)
    outs = _pcall(
        body,
        name=name,
        out_shape=[jax.ShapeDtypeStruct((N_DEV,) + a.shape, a.dtype) for a in arrs],
        in_specs=[any_spec] * n,
        out_specs=[any_spec] * n,
        scratch_shapes=[
            pltpu.SemaphoreType.DMA((7 * n,)),
            pltpu.SemaphoreType.DMA((7 * n,)),
            pltpu.SemaphoreType.DMA((n,)),
        ],
    )(*arrs)
    return list(outs)


def _all_to_all(arrs, name):
    n = len(arrs)

    def body(*refs):
        ins = refs[:n]
        outs = refs[n:2 * n]
        send_sems, recv_sems, local_sems = refs[2 * n:]
        x, y, c = lax.axis_index("x"), lax.axis_index("y"), lax.axis_index("c")
        my_idx = 4 * x + 2 * y + c

        def peer(k):
            fx, fy, fc = (k >> 2) & 1, (k >> 1) & 1, k & 1
            return (x ^ fx if fx else x, y ^ fy if fy else y, c ^ fc if fc else c)

        def copy(a, k):
            px, py, pc = peer(k)
            return pltpu.make_async_remote_copy(
                src_ref=ins[a].at[4 * px + 2 * py + pc],
                dst_ref=outs[a].at[my_idx],
                send_sem=send_sems.at[7 * a + k - 1],
                recv_sem=recv_sems.at[7 * a + k - 1],
                device_id=(px, py, pc),
                device_id_type=MESH_ID,
            )

        mine = [pltpu.make_async_copy(ins[a].at[my_idx], outs[a].at[my_idx], local_sems.at[a]) for a in range(n)]
        for cp in mine:
            cp.start()
        order = [1, 4, 2, 5, 3, 6, 7]
        cps = [copy(a, k) for k in order for a in range(n)]
        for cp in cps:
            cp.start()
        for cp in cps:
            cp.wait()
        for cp in mine:
            cp.wait()

    any_spec = pl.BlockSpec(memory_space=pl.ANY)
    outs = _pcall(
        body,
        name=name,
        out_shape=[jax.ShapeDtypeStruct(a.shape, a.dtype) for a in arrs],
        in_specs=[any_spec] * n,
        out_specs=[any_spec] * n,
        scratch_shapes=[
            pltpu.SemaphoreType.DMA((7 * n,)),
            pltpu.SemaphoreType.DMA((7 * n,)),
            pltpu.SemaphoreType.DMA((n,)),
        ],
    )(*arrs)
    return list(outs)


_HBM = pl.BlockSpec(memory_space=pltpu.HBM)
_SEM = pl.BlockSpec(memory_space=pltpu.SEMAPHORE)
_EFFECT = pltpu.SideEffectType.DATAFLOW_SIDE_EFFECTING


def _descriptors(copies, send_sems, recv_sems):
    descs = []
    for i, (src, dst, dev) in enumerate(copies):
        if dev is None:
            descs.append(pltpu.make_async_copy(src, dst, recv_sems.at[i]))
        else:
            descs.append(pltpu.make_async_remote_copy(src_ref=src, dst_ref=dst, send_sem=send_sems.at[i],
                                                      recv_sem=recv_sems.at[i], device_id=dev, device_id_type=MESH_ID))
    return descs


def _split_start(name, arrays, copies_fn, n_copies):
    n = len(arrays)

    def body(*refs):
        send_sems, recv_sems = refs[n], refs[n + 1]
        token = refs[2 * n + 2]
        for dsc in _descriptors(copies_fn(refs[:n]), send_sems, recv_sems):
            dsc.start()
        token[...] = jnp.zeros_like(token)

    outs = _pcall(
        body,
        name=name,
        out_shape=(pltpu.SemaphoreType.DMA((n_copies,)), pltpu.SemaphoreType.DMA((n_copies,)),
                   *[pltpu.HBM(a.shape, a.dtype) for a in arrays], jax.ShapeDtypeStruct((8, LANE), F32)),
        in_specs=[_HBM] * n,
        out_specs=(_SEM, _SEM, *[_HBM] * n, pl.BlockSpec(memory_space=pltpu.VMEM)),
        input_output_aliases={i: 2 + i for i in range(n)},
        compiler_params=pltpu.CompilerParams(has_side_effects=_EFFECT),
    )(*[pltpu.with_memory_space_constraint(a, pltpu.HBM) for a in arrays])
    return outs[0], outs[1], list(outs[2:2 + n]), outs[2 + n]


def _split_wait(name, send_sems, recv_sems, arrays, copies_fn, after):
    n = len(arrays)

    def body(*refs):
        for dsc, (_, _, dev) in zip(_descriptors(copies_fn(refs[:n]), refs[n], refs[n + 1]), copies_fn(refs[:n])):
            if dev is None:
                dsc.wait()
            else:
                dsc.wait_send()
                dsc.wait_recv()

    outs = _pcall(
        body,
        name=name,
        out_shape=tuple(pltpu.HBM(a.shape, a.dtype) for a in arrays),
        in_specs=[_HBM] * n + [_SEM, _SEM, pl.BlockSpec(memory_space=pl.ANY)],
        out_specs=tuple([_HBM] * n),
        input_output_aliases={i: i for i in range(n)},
        compiler_params=pltpu.CompilerParams(has_side_effects=_EFFECT),
    )(*arrays, send_sems, recv_sems, after)
    return list(outs)


def _mesh_pos():
    x, y, c = lax.axis_index("x"), lax.axis_index("y"), lax.axis_index("c")
    return x, y, c, [(1 - x, y), (x, 1 - y), (1 - x, 1 - y)]


def _gather_ici_copies(n):
    def copies(refs):
        x, y, c, chips = _mesh_pos()
        me = 4 * x + 2 * y + c
        out = []
        for a in range(n):
            src, buf = refs[a], refs[n + a]
            out.append((src, buf.at[me], None))
            out.append((src, buf.at[me], (x, y, 1 - c)))
            out += [(src, buf.at[me], (cx, cy, c)) for cx, cy in chips[:2]]
        return out
    return copies


def _gather_pass_copies(n):
    def copies(refs):
        x, y, c, chips = _mesh_pos()
        south = c == 0
        bx, by = jnp.where(south, 1 - x, x), jnp.where(south, y, 1 - y)
        tx, ty = jnp.where(south, x, 1 - x), jnp.where(south, 1 - y, y)
        out = []
        for a in range(n):
            rows = refs[a].at[4 * bx + 2 * by + c]
            out.append((rows, rows, (tx, ty, c)))
            for cx, cy in chips[:2]:
                rows = refs[a].at[4 * cx + 2 * cy + c]
                out.append((rows, rows, (x, y, 1 - c)))
        return out
    return copies


def _gather_d2d_copies(n):
    def copies(refs):
        x, y, c, chips = _mesh_pos()
        cx, cy = chips[2]
        out = []
        for a in range(n):
            rows = refs[a].at[4 * cx + 2 * cy + c]
            out.append((rows, rows, (x, y, 1 - c)))
        return out
    return copies


def _reduce_d2d_copies(n):
    def copies(refs):
        x, y, c, _ = _mesh_pos()
        out = []
        for a in range(n):
            for k in range(4):
                out.append((refs[a].at[2 * k + (1 - c)], refs[n + a].at[k], (x, y, 1 - c)))
        return out
    return copies


def _reduce_ici_copies(n):
    def copies(refs):
        x, y, c, chips = _mesh_pos()
        mine = 2 * x + y
        out = []
        for a in range(n):
            src, land = refs[a], refs[n + a]
            out.append((src.at[mine], land.at[mine], None))
            out += [(src.at[2 * cx + cy], land.at[mine], (cx, cy, c)) for cx, cy in chips]
        return out
    return copies


def _pair_sum(send, land, c_idx, name):
    _, r, cols = send.shape
    rb = _tile(r, max(8, (1 << 22) // (send.dtype.itemsize * cols) // 8 * 8), 8)
    dt = send.dtype

    def body(c_ref, s_ref, l_ref, o_ref):
        o_ref[...] = (s_ref[...].astype(F32) + l_ref[...].astype(F32)).astype(dt)

    return pl.pallas_call(
        body,
        name=name,
        out_shape=jax.ShapeDtypeStruct((4, r, cols), dt),
        grid_spec=pltpu.PrefetchScalarGridSpec(
            num_scalar_prefetch=1,
            grid=(4, r // rb),
            in_specs=[pl.BlockSpec((None, rb, cols), lambda k, i, c_ref: (2 * k + c_ref[0], i, 0)),
                      pl.BlockSpec((None, rb, cols), lambda k, i, c_ref: (k, i, 0))],
            out_specs=pl.BlockSpec((None, rb, cols), lambda k, i, c_ref: (k, i, 0)),
        ),
        compiler_params=_cparams(("parallel", "parallel")),
    )(c_idx, send, land)


_DIMS = {
    "nn": (((1,), (0,)), ((), ())),
    "nt": (((1,), (1,)), ((), ())),
    "tn": (((0,), (0,)), ((), ())),
}


def _mm_call(a, b, *, mode, grid, a_spec, b_spec, o_spec, out_shape, acc_shape, name):
    nk = grid[2]
    out_dtype = out_shape.dtype

    def body(a_ref, b_ref, o_ref, *scratch):
        p = lax.dot_general(a_ref[...].astype(BF16), b_ref[...].astype(BF16), _DIMS[mode],
                            preferred_element_type=F32)
        if nk == 1:
            o_ref[...] = p.astype(out_dtype)
        else:
            acc = scratch[0]
            k = pl.program_id(2)

            @pl.when(k == 0)
            def _():
                acc[...] = p

            @pl.when(k > 0)
            def _():
                acc[...] += p

            @pl.when(k == nk - 1)
            def _():
                o_ref[...] = acc[...].astype(out_dtype)

    return _pcall(
        body,
        name=name,
        out_shape=out_shape,
        grid=grid,
        in_specs=[a_spec, b_spec],
        out_specs=o_spec,
        scratch_shapes=[pltpu.VMEM(acc_shape, F32)] if nk > 1 else [],
        compiler_params=_cparams(("parallel", "parallel", "arbitrary")),
    )(a, b)


def _mm(a, b, mode, out_dtype, name, tm=512, tn=512, tk=2432, a_row_off=0, rows=None):
    if mode == "nn":
        (m, k), (k2, n) = a.shape, b.shape
    elif mode == "nt":
        (m, k), (n, k2) = a.shape, b.shape
    else:
        (k, m), (k2, n) = a.shape, b.shape
        if rows is not None:
            k = k2 = rows
    assert k == k2, (a.shape, b.shape, mode)
    if mode != "tn":
        m = (m if rows is None else rows + a_row_off) - a_row_off
    tm, tn, tk = _tile(m, tm, 8), _tile(n, tn), _tile(k, tk, 8 if mode == "tn" else LANE)
    assert a_row_off % tm == 0
    ro = a_row_off // tm
    grid = (m // tm, n // tn, k // tk)
    if mode == "tn":
        a_spec = pl.BlockSpec((tk, tm), lambda i, j, kk: (kk, i))
    else:
        a_spec = pl.BlockSpec((tm, tk), lambda i, j, kk: (i + ro, kk))
    if mode == "nt":
        b_spec = pl.BlockSpec((tn, tk), lambda i, j, kk: (j, kk))
    else:
        b_spec = pl.BlockSpec((tk, tn), lambda i, j, kk: (kk, j))
    o_spec = pl.BlockSpec((tm, tn), lambda i, j, kk: (i, j))
    return _mm_call(a, b, mode=mode, grid=grid, a_spec=a_spec, b_spec=b_spec, o_spec=o_spec,
                    out_shape=jax.ShapeDtypeStruct((m, n), out_dtype), acc_shape=(tm, tn), name=name)


def _mm_cat_nt(pieces, out_dtype, name, tm=1024, tn=1024, tk=2048, rows=None):
    m = pieces[0][0].shape[0] if rows is None else rows
    n = pieces[0][1].shape[0]
    tm, tn = _tile(m, tm, 8), _tile(n, tn)
    steps, starts, s = [], [], 0
    for a, b, off in pieces:
        kp = a.shape[1]
        tkp = _tile(kp, tk)
        assert off % tkp == 0 and b.shape[0] == n
        steps.append((tkp, kp // tkp, off // tkp))
        starts.append(s)
        s += kp // tkp
    nk = s
    npc = len(pieces)

    def body(*refs):
        o_ref, acc = refs[2 * npc], refs[2 * npc + 1]
        kk = pl.program_id(2)

        @pl.when(kk == 0)
        def _():
            acc[...] = jnp.zeros_like(acc)

        for p in range(npc):
            @pl.when((kk >= starts[p]) & (kk < starts[p] + steps[p][1]))
            def _(p=p):
                acc[...] += lax.dot_general(refs[2 * p][...].astype(BF16), refs[2 * p + 1][...].astype(BF16), _DIMS["nt"],
                                            preferred_element_type=F32)

        @pl.when(kk == nk - 1)
        def _():
            o_ref[...] = acc[...].astype(out_dtype)

    in_specs, args = [], []
    for p, (a, b, off) in enumerate(pieces):
        tkp, np_, ob = steps[p]

        def rel(kk, p=p, np_=np_):
            return jnp.clip(kk - starts[p], 0, np_ - 1)

        in_specs.append(pl.BlockSpec((tm, tkp), lambda i, j, kk, rel=rel: (i, rel(kk))))
        in_specs.append(pl.BlockSpec((tn, tkp), lambda i, j, kk, rel=rel, ob=ob: (j, ob + rel(kk))))
        args += [a, b]
    return _pcall(
        body,
        name=name,
        out_shape=jax.ShapeDtypeStruct((m, n), out_dtype),
        grid=(m // tm, n // tn, nk),
        in_specs=in_specs,
        out_specs=pl.BlockSpec((tm, tn), lambda i, j, kk: (i, j)),
        scratch_shapes=[pltpu.VMEM((tm, tn), F32)],
        compiler_params=_cparams(("parallel", "parallel", "arbitrary")),
    )(*args)


def _mm_cat_tn(a, pieces, out_dtype, name, tm=1024, tn=1024, rows=None):
    k = a.shape[0] if rows is None else rows
    m = a.shape[1]
    tm = _tile(m, tm)
    starts, s = [], 0
    for b in pieces:
        assert b.shape[1] % tn == 0
        starts.append(s)
        s += b.shape[1] // tn
    nj = s
    npc = len(pieces)

    def body(*refs):
        a_ref, o_ref = refs[0], refs[1 + npc]
        j = pl.program_id(1)
        for p in range(npc):
            @pl.when((j >= starts[p]) & (j < starts[p] + pieces[p].shape[1] // tn))
            def _(p=p):
                o_ref[...] = lax.dot_general(a_ref[...].astype(BF16), refs[1 + p][...].astype(BF16), _DIMS["tn"],
                                             preferred_element_type=F32).astype(out_dtype)

    in_specs = [pl.BlockSpec((k, tm), lambda i, j: (0, i))]
    for p, b in enumerate(pieces):
        np_ = b.shape[1] // tn
        in_specs.append(pl.BlockSpec((k, tn), lambda i, j, p=p, np_=np_: (0, jnp.clip(j - starts[p], 0, np_ - 1))))
    return _pcall(
        body,
        name=name,
        out_shape=jax.ShapeDtypeStruct((m, nj * tn), out_dtype),
        grid=(m // tm, nj),
        in_specs=in_specs,
        out_specs=pl.BlockSpec((tm, tn), lambda i, j: (i, j)),
        compiler_params=_cparams(("parallel", "arbitrary")),
    )(a, *pieces)


def _mm_up_fwd(z2, w3, name, tm=1024):
    t, d = z2.shape
    nsh, _, c = w3.shape
    tm = _tile(t, tm, 8)
    return _mm_call(z2, w3, mode="nn", grid=(t // tm, nsh, 1),
                    a_spec=pl.BlockSpec((tm, d), lambda i, j, kk: (i, 0)),
                    b_spec=pl.BlockSpec((None, d, c), lambda i, j, kk: (j, 0, 0)),
                    o_spec=pl.BlockSpec((tm, c), lambda i, j, kk: (i, j)),
                    out_shape=jax.ShapeDtypeStruct((t, nsh * c), BF16), acc_shape=(tm, c), name=name)


def _mm_up_dz(du3, w3, name, tm=512, tn=1024):
    _, t, f = du3.shape
    nsh, d, c = w3.shape
    half = nsh // 2
    assert f == half * c
    tm, tn = _tile(t, tm, 8), _tile(d, tn)

    def body(a_ref, b_ref, o_ref, acc):
        kk = pl.program_id(2)
        p = None
        for s in range(half):
            q = lax.dot_general(a_ref[:, s * c:(s + 1) * c], b_ref[s], _DIMS["nt"], preferred_element_type=F32)
            p = q if p is None else p + q

        @pl.when(kk == 0)
        def _():
            acc[...] = p

        @pl.when(kk == 1)
        def _():
            o_ref[...] = (acc[...] + p).astype(BF16)

    return _pcall(
        body,
        name=name,
        out_shape=jax.ShapeDtypeStruct((t, d), BF16),
        grid=(t // tm, d // tn, 2),
        in_specs=[pl.BlockSpec((None, tm, f), lambda i, j, kk: (kk, i, 0)),
                  pl.BlockSpec((half, tn, c), lambda i, j, kk: (kk, j, 0))],
        out_specs=pl.BlockSpec((tm, tn), lambda i, j, kk: (i, j)),
        scratch_shapes=[pltpu.VMEM((tm, tn), F32)],
        compiler_params=_cparams(("parallel", "parallel", "arbitrary")),
    )(du3, w3)


def _mm_sum_nn(pieces, out_dtype, name, tm=512, tn=512, rows=None):
    m = pieces[0][0].shape[0] if rows is None else rows
    n = pieces[0][2].shape[1]
    tm, tn = _tile(m, tm, 8), _tile(n, tn)
    npc = len(pieces)

    def body(*refs):
        p = None
        for s in range(npc):
            q = jnp.dot(refs[2 * s][...].astype(BF16), refs[2 * s + 1][...].astype(BF16), preferred_element_type=F32)
            p = q if p is None else p + q
        refs[2 * npc][...] = p.astype(out_dtype)

    in_specs, args = [], []
    for a, ao, b, bo, kp in pieces:
        assert ao % kp == 0 and bo % kp == 0 and b.shape[1] == n
        in_specs.append(pl.BlockSpec((tm, kp), lambda i, j, ab=ao // kp: (i, ab)))
        in_specs.append(pl.BlockSpec((kp, tn), lambda i, j, bb=bo // kp: (bb, j)))
        args += [a, b]
    return _pcall(
        body,
        name=name,
        out_shape=jax.ShapeDtypeStruct((m, n), out_dtype),
        grid=(m // tm, n // tn),
        in_specs=in_specs,
        out_specs=pl.BlockSpec((tm, tn), lambda i, j: (i, j)),
        compiler_params=_cparams(("parallel", "parallel")),
    )(*args)


def _mm_rows_tn(pieces, b, out_dtype, name, tm=1024, tn=1024, rows=None):
    k = b.shape[0] if rows is None else rows
    n = b.shape[1]
    tn = _tile(n, tn)
    starts, s = [], 0
    for a in pieces:
        assert a.shape[1] % tm == 0
        starts.append(s)
        s += a.shape[1] // tm
    ni = s
    npc = len(pieces)

    def body(*refs):
        b_ref, o_ref = refs[npc], refs[npc + 1]
        i = pl.program_id(0)
        for p in range(npc):
            @pl.when((i >= starts[p]) & (i < starts[p] + pieces[p].shape[1] // tm))
            def _(p=p):
                o_ref[...] = lax.dot_general(refs[p][...].astype(BF16), b_ref[...].astype(BF16), _DIMS["tn"],
                                             preferred_element_type=F32).astype(out_dtype)

    in_specs = []
    for p, a in enumerate(pieces):
        np_ = a.shape[1] // tm
        in_specs.append(pl.BlockSpec((k, tm), lambda i, j, p=p, np_=np_: (0, jnp.clip(i - starts[p], 0, np_ - 1))))
    in_specs.append(pl.BlockSpec((k, tn), lambda i, j: (0, j)))
    return _pcall(
        body,
        name=name,
        out_shape=jax.ShapeDtypeStruct((ni * tm, n), out_dtype),
        grid=(ni, n // tn),
        in_specs=in_specs,
        out_specs=pl.BlockSpec((tm, tn), lambda i, j: (i, j)),
        compiler_params=_cparams(("parallel", "parallel")),
    )(*pieces, b)


def _mm_up_gw(z2, du3, nsh, name, tm=1024):
    t, d = z2.shape
    f = du3.shape[2]
    half = nsh // 2
    c = f // half
    tm = _tile(d, tm)
    return _mm_call(z2, du3, mode="tn", grid=(d // tm, nsh, 1),
                    a_spec=pl.BlockSpec((t, tm), lambda i, j, kk: (0, i)),
                    b_spec=pl.BlockSpec((None, t, c), lambda i, j, kk: (j // half, 0, j % half)),
                    o_spec=pl.BlockSpec((None, tm, c), lambda i, j, kk: (j, i, 0)),
                    out_shape=jax.ShapeDtypeStruct((nsh, d, c), BF16), acc_shape=(tm, c), name=name)


def _rms(x):
    r = lax.rsqrt(jnp.mean(x * x, axis=-1, keepdims=True) + NORM_EPS)
    return x * r, r


def _rms_bwd(dxh, xh, r):
    return r * (dxh - xh * jnp.mean(dxh * xh, axis=-1, keepdims=True))


def _colsum(v):
    return jnp.sum(v, axis=0, keepdims=True)


def _rope(v, c, s1, s2, q):
    w = v.shape[-1]
    return v * c + pltpu.roll(v, w - q, 1) * s1 + pltpu.roll(v, q, 1) * s2


def _rope_t(d, c, s1, s2, q):
    w = d.shape[-1]
    return d * c + pltpu.roll(d * s1, q, 1) + pltpu.roll(d * s2, w - q, 1)


def _norm_mod_fwd(ctx, x, gain, mods):
    tc, d = ctx.shape
    t = x.shape[0]
    rb = min(ROW_BLOCK, tc)
    nbl = t // rb

    def body(ctx_ref, x_ref, g_ref, mod_ref, z_ref):
        i = pl.program_id(0)

        def emit(src, sh, sc):
            xh, _ = _rms(src[...])
            z_ref[...] = ((xh * g_ref[...]) * (1.0 + sc) + sh).astype(BF16)

        @pl.when(i >= nbl)
        def _():
            emit(ctx_ref, mod_ref[2:3, :], mod_ref[3:4, :])

        @pl.when(i < nbl)
        def _():
            emit(x_ref, mod_ref[0:1, :], mod_ref[1:2, :])

    return _pcall(
        body,
        name="norm1_mod_fwd",
        out_shape=jax.ShapeDtypeStruct((tc + t, d), BF16),
        grid=((tc + t) // rb,),
        in_specs=[
            pl.BlockSpec((rb, d), lambda i: (jnp.maximum(i - nbl, 0), 0)),
            pl.BlockSpec((rb, d), lambda i: (jnp.minimum(i, nbl - 1), 0)),
            pl.BlockSpec((1, d), lambda i: (0, 0)),
            pl.BlockSpec((8, d), lambda i: (0, 0)),
        ],
        out_specs=pl.BlockSpec((rb, d), lambda i: (i, 0)),
        compiler_params=_cparams(("arbitrary",)),
    )(ctx, x, gain, mods)


def _norm1_bwd(ctx, x, gain, mods, dz_ctx, dz_lat, dx1):
    tc, d = ctx.shape
    t = x.shape[0]
    rb = min(ROW_BLOCK, tc)
    nbl = t // rb

    def body(ctx_ref, x_ref, g_ref, mod_ref, dzc_ref, dzl_ref, dx1_ref, gx_ref, st_ref):
        i = pl.program_id(0)

        @pl.when(i == 0)
        def _():
            st_ref[...] = jnp.zeros_like(st_ref)

        def common(src, dz, sc, row_sh, row_sc):
            xh, r = _rms(src[...])
            g = g_ref[...]
            dxn = dz * (1.0 + sc)
            st_ref[row_sh:row_sh + 1, :] += _colsum(dz)
            st_ref[row_sc:row_sc + 1, :] += _colsum(dz * (xh * g))
            st_ref[2:3, :] += _colsum(dxn * xh)
            return _rms_bwd(dxn * g, xh, r)

        @pl.when(i >= nbl)
        def _():
            common(ctx_ref, dzc_ref[...], mod_ref[3:4, :], 3, 4)

        @pl.when(i < nbl)
        def _():
            gx_ref[...] = dx1_ref[...] + common(x_ref, dzl_ref[...], mod_ref[1:2, :], 0, 1)

    lat = lambda i: (jnp.minimum(i, nbl - 1), 0)
    cix = lambda i: (jnp.maximum(i - nbl, 0), 0)
    return _pcall(
        body,
        name="norm1_mod_bwd",
        out_shape=[jax.ShapeDtypeStruct((t, d), F32), jax.ShapeDtypeStruct((8, d), F32)],
        grid=((tc + t) // rb,),
        in_specs=[
            pl.BlockSpec((rb, d), cix),
            pl.BlockSpec((rb, d), lat),
            pl.BlockSpec((1, d), lambda i: (0, 0)),
            pl.BlockSpec((8, d), lambda i: (0, 0)),
            pl.BlockSpec((rb, d), cix),
            pl.BlockSpec((rb, d), lat),
            pl.BlockSpec((rb, d), lat),
        ],
        out_specs=[pl.BlockSpec((rb, d), lat), pl.BlockSpec((8, d), lambda i: (0, 0))],
        compiler_params=_cparams(("arbitrary",)),
    )(ctx, x, gain, mods, dz_ctx, dz_lat, dx1)


def _key_prep_fwd(kv, kv_gain, kb_gain, tabs):
    ta, wkv = kv.shape
    kvl = MLA_KV_LORA
    nb = GQA_KV_HEADS * GQA_HEAD_DIM
    rb = ROW_BLOCK if ta % ROW_BLOCK == 0 else LANE
    hd = GQA_HEAD_DIM

    def body(kv_ref, g_ref, gb_ref, ca, s1a, s2a, cb, s1b, s2b, kin_ref, kb_ref, vb_ref):
        xh, _ = _rms(kv_ref[:, 0:kvl])
        kin_ref[:, 0:kvl] = (xh * g_ref[...]).astype(BF16)
        kpe = kv_ref[:, kvl + 2 * nb:kvl + 2 * nb + LANE]
        kin_ref[:, kvl:kvl + LANE] = _rope(kpe, ca[...], s1a[...], s2a[...], MLA_ROPE // 4).astype(BF16)
        for h in range(GQA_KV_HEADS):
            nh, _ = _rms(kv_ref[:, kvl + h * hd:kvl + (h + 1) * hd])
            kb_ref[:, h * hd:(h + 1) * hd] = _rope(nh * gb_ref[...], cb[...], s1b[...], s2b[...], hd // 4).astype(BF16)
        vb_ref[...] = kv_ref[:, kvl + nb:kvl + 2 * nb].astype(BF16)

    row = lambda w: pl.BlockSpec((rb, w), lambda i: (i, 0))
    fix = lambda w: pl.BlockSpec((1, w), lambda i: (0, 0))
    return _pcall(
        body,
        name="key_prep_fwd",
        out_shape=[jax.ShapeDtypeStruct((ta, kvl + LANE), BF16), jax.ShapeDtypeStruct((ta, nb), BF16),
                   jax.ShapeDtypeStruct((ta, nb), BF16)],
        grid=(ta // rb,),
        in_specs=[row(wkv), fix(kvl), fix(hd)] + [row(LANE)] * 3 + [row(hd)] * 3,
        out_specs=[row(kvl + LANE), row(nb), row(nb)],
        compiler_params=_cparams(("parallel",)),
    )(kv, kv_gain, kb_gain, *tabs)


def _key_prep_bwd(kv, kv_gain, kb_gain, tabs, dkin, dkb, dvb):
    ta, wkv = kv.shape
    kvl = MLA_KV_LORA
    nb = GQA_KV_HEADS * GQA_HEAD_DIM
    rb = ROW_BLOCK if ta % ROW_BLOCK == 0 else LANE
    hd = GQA_HEAD_DIM

    def body(kv_ref, g_ref, gb_ref, ca, s1a, s2a, cb, s1b, s2b, dkin_ref, dkb_ref, dvb_ref, dkv_ref, st_ref, stb_ref):
        @pl.when(pl.program_id(0) == 0)
        def _():
            st_ref[...] = jnp.zeros_like(st_ref)
            stb_ref[...] = jnp.zeros_like(stb_ref)

        xh, r = _rms(kv_ref[:, 0:kvl])
        dn = dkin_ref[:, 0:kvl]
        st_ref[0:1, :] += _colsum(dn * xh)
        dkv_ref[:, 0:kvl] = _rms_bwd(dn * g_ref[...], xh, r).astype(BF16)
        dpe = _rope_t(dkin_ref[:, kvl:kvl + LANE], ca[...], s1a[...], s2a[...], MLA_ROPE // 4)
        dkv_ref[:, kvl + 2 * nb:kvl + 2 * nb + LANE] = dpe.astype(BF16)
        for h in range(GQA_KV_HEADS):
            nh, rh = _rms(kv_ref[:, kvl + h * hd:kvl + (h + 1) * hd])
            dn_h = _rope_t(dkb_ref[:, h * hd:(h + 1) * hd], cb[...], s1b[...], s2b[...], hd // 4)
            stb_ref[0:1, :] += _colsum(dn_h * nh)
            dkv_ref[:, kvl + h * hd:kvl + (h + 1) * hd] = _rms_bwd(dn_h * gb_ref[...], nh, rh).astype(BF16)
        dkv_ref[:, kvl + nb:kvl + 2 * nb] = dvb_ref[...].astype(BF16)

    row = lambda w: pl.BlockSpec((rb, w), lambda i: (i, 0))
    fix = lambda w: pl.BlockSpec((1, w), lambda i: (0, 0))
    return _pcall(
        body,
        name="key_prep_bwd",
        out_shape=[jax.ShapeDtypeStruct((ta, wkv), BF16), jax.ShapeDtypeStruct((8, kvl), F32),
                   jax.ShapeDtypeStruct((8, hd), F32)],
        grid=(ta // rb,),
        in_specs=[row(wkv), fix(kvl), fix(hd)] + [row(LANE)] * 3 + [row(hd)] * 3 + [row(kvl + LANE), row(nb), row(nb)],
        out_specs=[row(wkv), pl.BlockSpec((8, kvl), lambda i: (0, 0)), pl.BlockSpec((8, hd), lambda i: (0, 0))],
        compiler_params=_cparams(("arbitrary",)),
    )(kv, kv_gain, kb_gain, *tabs, dkin, dkb, dvb)


def _q_prep_fwd(qg, q_gain, qb_gain, tabs, qscale):
    t = qg.shape[0]
    ql = MLA_Q_LORA
    hd = GQA_HEAD_DIM
    hb = GQA_HEADS * hd
    rb = min(ROW_BLOCK, t)

    def body(q_ref, g_ref, gb_ref, cb, s1b, s2b, cqn_ref, qb_ref):
        xh, _ = _rms(q_ref[:, 0:ql])
        cqn_ref[...] = (xh * g_ref[...]).astype(BF16)
        for h in range(GQA_HEADS):
            nh, _ = _rms(q_ref[:, ql + h * hd:ql + (h + 1) * hd])
            qh = _rope(nh * gb_ref[...], cb[...], s1b[...], s2b[...], hd // 4)
            qb_ref[:, h * hd:(h + 1) * hd] = (qh * qscale).astype(BF16)

    row = lambda w: pl.BlockSpec((rb, w), lambda i: (i, 0))
    fix = lambda w: pl.BlockSpec((1, w), lambda i: (0, 0))
    return _pcall(
        body,
        name="q_prep_fwd",
        out_shape=[jax.ShapeDtypeStruct((t, ql), BF16), jax.ShapeDtypeStruct((t, hb), BF16)],
        grid=(t // rb,),
        in_specs=[row(ql + hb), fix(ql), fix(hd)] + [row(hd)] * 3,
        out_specs=[row(ql), row(hb)],
        compiler_params=_cparams(("parallel",)),
    )(qg, q_gain, qb_gain, *tabs)


def _q_prep_bwd(qg, q_gain, qb_gain, tabs, dcqn, dqb, wpad, qscale):
    t = qg.shape[0]
    ql = MLA_Q_LORA
    hd = GQA_HEAD_DIM
    hb = GQA_HEADS * hd
    rb = min(ROW_BLOCK, t)

    def body(q_ref, g_ref, gb_ref, cb, s1b, s2b, dcqn_ref, dqb_ref, dq_ref, st_ref, stb_ref):
        @pl.when(pl.program_id(0) == 0)
        def _():
            st_ref[...] = jnp.zeros_like(st_ref)
            stb_ref[...] = jnp.zeros_like(stb_ref)

        xh, r = _rms(q_ref[:, 0:ql])
        dn = dcqn_ref[...]
        st_ref[0:1, :] += _colsum(dn * xh)
        dq_ref[:, 0:ql] = _rms_bwd(dn * g_ref[...], xh, r).astype(BF16)
        for h in range(GQA_HEADS):
            nh, rh = _rms(q_ref[:, ql + h * hd:ql + (h + 1) * hd])
            dn_h = _rope_t(dqb_ref[:, h * hd:(h + 1) * hd] * qscale, cb[...], s1b[...], s2b[...], hd // 4)
            stb_ref[0:1, :] += _colsum(dn_h * nh)
            dq_ref[:, ql + h * hd:ql + (h + 1) * hd] = _rms_bwd(dn_h * gb_ref[...], nh, rh).astype(BF16)
        if wpad:
            dq_ref[:, ql + hb:ql + hb + wpad] = jnp.zeros((rb, wpad), BF16)

    row = lambda w: pl.BlockSpec((rb, w), lambda i: (i, 0))
    fix = lambda w: pl.BlockSpec((1, w), lambda i: (0, 0))
    return _pcall(
        body,
        name="q_prep_bwd",
        out_shape=[jax.ShapeDtypeStruct((t, ql + hb + wpad), BF16), jax.ShapeDtypeStruct((8, ql), F32),
                   jax.ShapeDtypeStruct((8, hd), F32)],
        grid=(t // rb,),
        in_specs=[row(ql + hb), fix(ql), fix(hd)] + [row(hd)] * 3 + [row(ql), row(hb)],
        out_specs=[row(ql + hb + wpad), pl.BlockSpec((8, ql), lambda i: (0, 0)), pl.BlockSpec((8, hd), lambda i: (0, 0))],
        compiler_params=_cparams(("arbitrary",)),
    )(qg, q_gain, qb_gain, *tabs, dcqn, dqb)


def _rope_a(v, tabs, transpose, out_dtype, name, qscale):
    t, w = v.shape
    rb = min(ROW_BLOCK, t)
    fn = _rope_t if transpose else _rope

    def body(v_ref, c, s1, s2, o_ref):
        for h in range(w // MLA_SLOT):
            sl = slice(h * MLA_SLOT, (h + 1) * MLA_SLOT)
            o_ref[:, sl] = (fn(v_ref[:, sl].astype(F32), c[...], s1[...], s2[...], MLA_ROPE // 4) * qscale).astype(out_dtype)

    row = lambda ww: pl.BlockSpec((rb, ww), lambda i: (i, 0))
    return _pcall(
        body,
        name=name,
        out_shape=jax.ShapeDtypeStruct((t, w), out_dtype),
        grid=(t // rb,),
        in_specs=[row(w)] + [row(MLA_SLOT)] * 3,
        out_specs=row(w),
        compiler_params=_cparams(("parallel",)),
    )(v, *tabs)


def _merge_fwd(pa, pb, qg, gate_blk):
    t, d = pa.shape
    rb = min(ROW_BLOCK, t)

    def body(pa_ref, pb_ref, ga_ref, gb_ref, o_ref):
        o_ref[...] = (jax.nn.sigmoid(ga_ref[...]) * pa_ref[...].astype(F32)
                      + jax.nn.sigmoid(gb_ref[...]) * pb_ref[...].astype(F32)).astype(BF16)

    row = pl.BlockSpec((rb, d), lambda i: (i, 0))
    return _pcall(
        body,
        name="merge_fwd",
        out_shape=jax.ShapeDtypeStruct((t, d), BF16),
        grid=(t // rb,),
        in_specs=[row, row, pl.BlockSpec((rb, d), lambda i: (i, gate_blk)), pl.BlockSpec((rb, d), lambda i: (i, gate_blk + 1))],
        out_specs=row,
        compiler_params=_cparams(("parallel",)),
    )(pa, pb, qg, qg)


def _merge_bwd(dm, pa, pb, qg, gate_blk):
    t, d = pa.shape
    rb = min(ROW_BLOCK, t)

    def body(dm_ref, pa_ref, pb_ref, ga_ref, gb_ref, dpa_ref, dpb_ref, dg_ref):
        dmv = dm_ref[...].astype(F32)
        sa = jax.nn.sigmoid(ga_ref[...])
        sb = jax.nn.sigmoid(gb_ref[...])
        dpa_ref[...] = (dmv * sa).astype(BF16)
        dpb_ref[...] = (dmv * sb).astype(BF16)
        dg_ref[:, 0:d] = (dmv * pa_ref[...].astype(F32) * (sa * (1.0 - sa))).astype(BF16)
        dg_ref[:, d:2 * d] = (dmv * pb_ref[...].astype(F32) * (sb * (1.0 - sb))).astype(BF16)

    row = pl.BlockSpec((rb, d), lambda i: (i, 0))
    return _pcall(
        body,
        name="merge_bwd",
        out_shape=[jax.ShapeDtypeStruct((t, d), BF16), jax.ShapeDtypeStruct((t, d), BF16),
                   jax.ShapeDtypeStruct((t, 2 * d), BF16)],
        grid=(t // rb,),
        in_specs=[row, row, row, pl.BlockSpec((rb, d), lambda i: (i, gate_blk)), pl.BlockSpec((rb, d), lambda i: (i, gate_blk + 1))],
        out_specs=[row, row, pl.BlockSpec((rb, 2 * d), lambda i: (i, 0))],
        compiler_params=_cparams(("parallel",)),
    )(dm, pa, pb, qg, qg)


def _resid_norm_mod(x, branch, gain, mods, name):
    t, d = x.shape
    rb = min(ROW_BLOCK, t)

    def body(x_ref, b_ref, g_ref, mod_ref, x1_ref, z_ref):
        x1 = x_ref[...] + mod_ref[0:1, :] * b_ref[...]
        x1_ref[...] = x1
        xh, _ = _rms(x1)
        z_ref[...] = ((xh * g_ref[...]) * (1.0 + mod_ref[2:3, :]) + mod_ref[1:2, :]).astype(BF16)

    row = pl.BlockSpec((rb, d), lambda i: (i, 0))
    return _pcall(
        body,
        name=name,
        out_shape=[jax.ShapeDtypeStruct((t, d), F32), jax.ShapeDtypeStruct((t, d), BF16)],
        grid=(t // rb,),
        in_specs=[row, row, pl.BlockSpec((1, d), lambda i: (0, 0)), pl.BlockSpec((8, d), lambda i: (0, 0))],
        out_specs=[row, row],
        compiler_params=_cparams(("parallel",)),
    )(x, branch, gain, mods)


def _norm2_bwd(x1, attn, gain, mods, dz2, dx2):
    t, d = x1.shape
    rb = min(ROW_BLOCK, t)

    def body(x1_ref, at_ref, g_ref, mod_ref, dz_ref, dx2_ref, dx1_ref, da_ref, st_ref):
        @pl.when(pl.program_id(0) == 0)
        def _():
            st_ref[...] = jnp.zeros_like(st_ref)

        xh, r = _rms(x1_ref[...])
        g = g_ref[...]
        dz = dz_ref[...].astype(F32)
        dxn = dz * (1.0 + mod_ref[1:2, :])
        st_ref[0:1, :] += _colsum(dz)
        st_ref[1:2, :] += _colsum(dz * (xh * g))
        st_ref[2:3, :] += _colsum(dxn * xh)
        dx1 = dx2_ref[...] + _rms_bwd(dxn * g, xh, r)
        dx1_ref[...] = dx1
        st_ref[3:4, :] += _colsum(dx1 * at_ref[...])
        da_ref[...] = (dx1 * mod_ref[0:1, :]).astype(BF16)

    row = pl.BlockSpec((rb, d), lambda i: (i, 0))
    return _pcall(
        body,
        name="norm2_mod_bwd",
        out_shape=[jax.ShapeDtypeStruct((t, d), F32), jax.ShapeDtypeStruct((t, d), BF16), jax.ShapeDtypeStruct((8, d), F32)],
        grid=(t // rb,),
        in_specs=[row, row, pl.BlockSpec((1, d), lambda i: (0, 0)), pl.BlockSpec((8, d), lambda i: (0, 0)), row, row],
        out_specs=[row, row, pl.BlockSpec((8, d), lambda i: (0, 0))],
        compiler_params=_cparams(("arbitrary",)),
    )(x1, attn, gain, mods, dz2, dx2)


def _final_loss(x1, ffn, gain, mods, target):
    t, d = x1.shape
    rb = min(ROW_BLOCK, t)
    nb = t // rb

    def body(x1_ref, f_ref, g_ref, mod_ref, tg_ref, dx2_ref, df_ref, st_ref):
        i = pl.program_id(0)

        @pl.when(i == 0)
        def _():
            st_ref[...] = jnp.zeros_like(st_ref)

        ffn_v = f_ref[...]
        g2 = mod_ref[0:1, :]
        x2 = x1_ref[...] + g2 * ffn_v
        xh, r = _rms(x2)
        g = g_ref[...]
        err = xh * g - tg_ref[...]
        st_ref[2:3, :] += _colsum(err * err) * (0.5 / d)
        dy = err * (1.0 / d)
        st_ref[0:1, :] += _colsum(dy * xh)
        dx2 = _rms_bwd(dy * g, xh, r)
        dx2_ref[...] = dx2
        st_ref[1:2, :] += _colsum(dx2 * ffn_v)
        df_ref[...] = (dx2 * g2).astype(BF16)

        @pl.when(i == nb - 1)
        def _():
            st_ref[3:4, :] = jnp.broadcast_to(jnp.sum(st_ref[2:3, :], axis=-1, keepdims=True), (1, d))

    row = pl.BlockSpec((rb, d), lambda i: (i, 0))
    return _pcall(
        body,
        name="final_norm_loss",
        out_shape=[jax.ShapeDtypeStruct((t, d), F32), jax.ShapeDtypeStruct((t, d), BF16), jax.ShapeDtypeStruct((8, d), F32)],
        grid=(nb,),
        in_specs=[row, row, pl.BlockSpec((1, d), lambda i: (0, 0)), pl.BlockSpec((8, d), lambda i: (0, 0)), row],
        out_specs=[row, row, pl.BlockSpec((8, d), lambda i: (0, 0))],
        compiler_params=_cparams(("arbitrary",)),
    )(x1, ffn, gain, mods, target)


def _row_ends(shape):
    rows = lax.broadcasted_iota(jnp.int32, shape, 0)
    return rows == 0, rows == shape[0] - 1


def _shift_dn(v, first):
    return jnp.where(first, 0.0, pltpu.roll(v, 1, 0))


def _shift_up(v, last):
    return jnp.where(last, 0.0, pltpu.roll(v, v.shape[0] - 1, 0))


def _conv_fwd(u, cw, cb):
    t, f2 = u.shape
    f = f2 // 2
    cbk = _tile(f, 256)
    nf = f // cbk

    def body(ua_ref, ub_ref, cwa_ref, cwb_ref, cba_ref, cbb_ref, h_ref, uc_ref):
        first, last = _row_ends((t, cbk))
        outs = []
        for u_ref, cw_ref, cb_ref in ((ua_ref, cwa_ref, cba_ref), (ub_ref, cwb_ref, cbb_ref)):
            uu, cwv = u_ref[...].astype(F32), cw_ref[...]
            outs.append(cb_ref[...] + cwv[0:1, :] * _shift_dn(uu, first) + cwv[1:2, :] * uu
                        + cwv[2:3, :] * _shift_up(uu, last))
        a, b = outs
        uc_ref[0] = a.astype(BF16)
        uc_ref[1] = b.astype(BF16)
        h_ref[...] = (a * jax.nn.sigmoid(a) * b).astype(BF16)

    ca = lambda r: pl.BlockSpec((r, cbk), lambda j: (0, j))
    cbs = lambda r: pl.BlockSpec((r, cbk), lambda j: (0, nf + j))
    return _pcall(
        body,
        name="conv_gate_fwd",
        out_shape=[jax.ShapeDtypeStruct((t, f), BF16), jax.ShapeDtypeStruct((2, t, f), BF16)],
        grid=(nf,),
        in_specs=[ca(t), cbs(t), ca(3), cbs(3), ca(1), cbs(1)],
        out_specs=[ca(t), pl.BlockSpec((2, t, cbk), lambda j: (0, 0, j))],
        compiler_params=_cparams(("parallel",)),
    )(u, u, cw, cw, cb, cb)


def _conv_bwd(u, uc, cw, dh):
    t, f2 = u.shape
    f = f2 // 2
    cbk = _tile(f, 256)
    nf = f // cbk

    def body(ua_ref, ub_ref, uc_ref, cwa_ref, cwb_ref, dh_ref, du_ref, dcw_ref, dcb_ref):
        first, last = _row_ends((t, cbk))
        a, b = uc_ref[0].astype(F32), uc_ref[1].astype(F32)
        dh_v = dh_ref[...].astype(F32)
        sg = jax.nn.sigmoid(a)
        db = dh_v * (a * sg)
        da = dh_v * b * (sg * (1.0 + a * (1.0 - sg)))
        for idx, (dv, u_ref, cw_ref) in enumerate(((da, ua_ref, cwa_ref), (db, ub_ref, cwb_ref))):
            uu, cwv = u_ref[...].astype(F32), cw_ref[...]
            up, dn = _shift_up(dv, last), _shift_dn(dv, first)
            dcb_ref[idx] = _colsum(dv)
            dcw_ref[idx, 0:1, :] = _colsum(up * uu)
            dcw_ref[idx, 1:2, :] = _colsum(dv * uu)
            dcw_ref[idx, 2:3, :] = _colsum(dn * uu)
            du_ref[idx] = (cwv[0:1, :] * up + cwv[1:2, :] * dv + cwv[2:3, :] * dn).astype(BF16)

    ca = lambda r: pl.BlockSpec((r, cbk), lambda j: (0, j))
    cbs = lambda r: pl.BlockSpec((r, cbk), lambda j: (0, nf + j))
    o3 = lambda r: pl.BlockSpec((2, r, cbk), lambda j: (0, 0, j))
    return _pcall(
        body,
        name="conv_gate_bwd",
        out_shape=[jax.ShapeDtypeStruct((2, t, f), BF16), jax.ShapeDtypeStruct((2, 3, f), F32),
                   jax.ShapeDtypeStruct((2, 1, f), F32)],
        grid=(nf,),
        in_specs=[ca(t), cbs(t), o3(t), ca(3), cbs(3), ca(t)],
        out_specs=[o3(t), o3(3), o3(1)],
        compiler_params=_cparams(("parallel",)),
    )(u, u, uc, cw, cw, dh)


def _attention_fwd(q, kk, vv, *, hq, hkv, dk, dv, k_blk0, v_blk0, name):
    t = q.shape[0]
    tk = kk.shape[0]
    g_sz = hq // hkv
    tq = min(ATT_Q_BLOCK_FWD, t)

    def body(q_ref, k_ref, v_ref, o_ref, lse_ref):
        k = k_ref[...]
        v = v_ref[...]
        for j in range(g_sz):
            s = lax.dot_general(q_ref[:, j * dk:(j + 1) * dk], k, _DIMS["nt"], preferred_element_type=F32)
            m = jnp.max(s, axis=-1, keepdims=True)
            p = jnp.exp2(s - m)
            l = jnp.sum(p, axis=-1, keepdims=True)
            o = jnp.dot(p.astype(BF16), v, preferred_element_type=F32) / l
            o_ref[:, j * dv:(j + 1) * dv] = o.astype(BF16)
            lse_ref[0, :, j:j + 1] = m + jnp.log2(l)

    return _pcall(
        body,
        name=name,
        out_shape=[jax.ShapeDtypeStruct((t, hq * dv), BF16), jax.ShapeDtypeStruct((hkv, t, g_sz), F32)],
        grid=(hkv, t // tq),
        in_specs=[
            pl.BlockSpec((tq, g_sz * dk), lambda g, i: (i, g)),
            pl.BlockSpec((tk, dk), lambda g, i: (0, k_blk0 + g)),
            pl.BlockSpec((tk, dv), lambda g, i: (0, v_blk0 + g)),
        ],
        out_specs=[
            pl.BlockSpec((tq, g_sz * dv), lambda g, i: (i, g)),
            pl.BlockSpec((1, tq, g_sz), lambda g, i: (g, i, 0)),
        ],
        compiler_params=_cparams(("parallel", "parallel")),
    )(q, kk, vv)


def _attention_bwd(q, kk, vv, do, lse, *, hq, hkv, dk, dv, k_blk0, v_blk0, name):
    t = q.shape[0]
    tk = kk.shape[0]
    g_sz = hq // hkv
    tq = min(ATT_Q_BLOCK, t)

    def body(q_ref, k_ref, v_ref, do_ref, lse_ref, dq_ref, dk_ref, dv_ref):
        @pl.when(pl.program_id(1) == 0)
        def _():
            dk_ref[...] = jnp.zeros_like(dk_ref)
            dv_ref[...] = jnp.zeros_like(dv_ref)

        k = k_ref[...]
        v = v_ref[...]
        for j in range(g_sz):
            qj = q_ref[:, j * dk:(j + 1) * dk]
            doj = do_ref[:, j * dv:(j + 1) * dv]
            s = lax.dot_general(qj, k, _DIMS["nt"], preferred_element_type=F32)
            p = jnp.exp2(s - lse_ref[0, :, j:j + 1])
            dp = lax.dot_general(doj, v, _DIMS["nt"], preferred_element_type=F32)
            ds = (p * (dp - jnp.sum(p * dp, axis=-1, keepdims=True))).astype(BF16)
            dv_ref[...] += lax.dot_general(p.astype(BF16), doj, _DIMS["tn"], preferred_element_type=F32)
            dk_ref[...] += lax.dot_general(ds, qj, _DIMS["tn"], preferred_element_type=F32)
            dq_ref[:, j * dk:(j + 1) * dk] = jnp.dot(ds, k, preferred_element_type=F32)

        @pl.when(pl.program_id(1) == t // tq - 1)
        def _():
            dk_ref[...] *= LN2

    return _pcall(
        body,
        name=name,
        out_shape=[jax.ShapeDtypeStruct((t, hq * dk), F32), jax.ShapeDtypeStruct((tk, hkv * dk), F32),
                   jax.ShapeDtypeStruct((tk, hkv * dv), F32)],
        grid=(hkv, t // tq),
        in_specs=[
            pl.BlockSpec((tq, g_sz * dk), lambda g, i: (i, g)),
            pl.BlockSpec((tk, dk), lambda g, i: (0, k_blk0 + g)),
            pl.BlockSpec((tk, dv), lambda g, i: (0, v_blk0 + g)),
            pl.BlockSpec((tq, g_sz * dv), lambda g, i: (i, g)),
            pl.BlockSpec((1, tq, g_sz), lambda g, i: (g, i, 0)),
        ],
        out_specs=[
            pl.BlockSpec((tq, g_sz * dk), lambda g, i: (i, g)),
            pl.BlockSpec((tk, dk), lambda g, i: (0, g)),
            pl.BlockSpec((tk, dv), lambda g, i: (0, g)),
        ],
        compiler_params=_cparams(("parallel", "arbitrary")),
    )(q, kk, vv, do, lse)


def _silu(v):
    return v * jax.nn.sigmoid(v)


def _ada_fwd(conds, w_ada, b_ada_shard):
    r, d = conds.shape
    n = w_ada.shape[1]
    tn = _tile(n, 512)

    def body(c_ref, w_ref, b_ref, o_ref):
        s = _silu(c_ref[...]).astype(BF16)
        o_ref[...] = jnp.dot(s, w_ref[...].astype(BF16), preferred_element_type=F32) + b_ref[...]

    return _pcall(
        body,
        name="ada_fwd",
        out_shape=jax.ShapeDtypeStruct((r, n), F32),
        grid=(n // tn,),
        in_specs=[pl.BlockSpec((r, d), lambda j: (0, 0)), pl.BlockSpec((d, tn), lambda j: (0, j)),
                  pl.BlockSpec((1, tn), lambda j: (0, j))],
        out_specs=pl.BlockSpec((r, tn), lambda j: (0, j)),
        compiler_params=_cparams(("parallel",)),
    )(conds, w_ada, b_ada_shard)


def _cctx_partial(da16_shard, w_ada, c_ctx_row):
    d, n = w_ada.shape
    td = _tile(d, 512)

    def body(g_ref, w_ref, c_ref, o_ref):
        ds = lax.dot_general(g_ref[8:16, :].astype(BF16), w_ref[...].astype(BF16), _DIMS["nt"],
                             preferred_element_type=F32)
        cv = c_ref[...]
        sg = jax.nn.sigmoid(cv)
        o_ref[...] = ds * (sg * (1.0 + cv * (1.0 - sg)))

    return _pcall(
        body,
        name="cctx_partial",
        out_shape=jax.ShapeDtypeStruct((8, d), F32),
        grid=(d // td,),
        in_specs=[pl.BlockSpec((16, n), lambda j: (0, 0)), pl.BlockSpec((td, n), lambda j: (j, 0)),
                  pl.BlockSpec((1, td), lambda j: (0, j))],
        out_specs=pl.BlockSpec((8, td), lambda j: (0, j)),
        compiler_params=_cparams(("parallel",)),
    )(da16_shard, w_ada, c_ctx_row)


def _sum_parts(parts):
    p, _, n = parts.shape

    def body(p_ref, o_ref):
        acc = p_ref[0]
        for s in range(1, p):
            acc = acc + p_ref[s]
        o_ref[...] = acc

    return _pcall(
        body,
        name="sum_parts",
        out_shape=jax.ShapeDtypeStruct((1, n), F32),
        in_specs=[pl.BlockSpec(memory_space=pltpu.VMEM)],
        out_specs=pl.BlockSpec(memory_space=pltpu.VMEM),
    )(parts)


def _adam_math(w, g, m, v):
    m2 = ADAM_B1 * m + (1.0 - ADAM_B1) * g
    v2 = ADAM_B2 * v + (1.0 - ADAM_B2) * jnp.square(g)
    m_hat = m2 / (1.0 - ADAM_B1 ** ADAM_STEP)
    v_hat = v2 / (1.0 - ADAM_B2 ** ADAM_STEP)
    delta = -ADAM_LR * (m_hat / (jnp.sqrt(v_hat) + ADAM_EPS) + ADAM_WD * w)
    return delta, m2, v2


def _adamw(parts, w, m, v, name):
    p, r, c = parts.shape
    block_elems = 1 << 18
    rb, cb = _tile(r, max(8, block_elems // c // 8 * 8), 8), c
    if rb * c < block_elems // 4 and r * c > block_elems:
        rb, cb = r, _tile(c, max(LANE, block_elems // r // LANE * LANE))

    def body(p_ref, w_ref, m_ref, v_ref, g_ref, d_ref, m2_ref, v2_ref):
        g = p_ref[0].astype(F32)
        for s in range(1, p):
            g = g + p_ref[s].astype(F32)
        g_ref[...] = g
        d_ref[...], m2_ref[...], v2_ref[...] = _adam_math(w_ref[...], g, m_ref[...], v_ref[...])

    if w.ndim == 3:
        blk = pl.BlockSpec((None, rb, cb), lambda i, j: (0, i, j))
    else:
        blk = pl.BlockSpec((rb, cb), lambda i, j: (i, j))
    return _pcall(
        body,
        name=name,
        out_shape=[jax.ShapeDtypeStruct(w.shape, F32)] * 4,
        grid=(r // rb, c // cb),
        in_specs=[pl.BlockSpec((p, rb, cb), lambda i, j: (0, i, j)), blk, blk, blk],
        out_specs=[blk] * 4,
        compiler_params=_cparams(("parallel", "parallel")),
    )(parts, w, m, v)


def _adamw_ada(conds, da16, w, m, v):
    d, n = w.shape
    rb = _tile(d, 256, LANE)

    def body(s_ref, da_ref, w_ref, m_ref, v_ref, g_ref, d_ref, m2_ref, v2_ref):
        g = lax.dot_general(_silu(s_ref[...]).astype(BF16), da_ref[...].astype(BF16), _DIMS["tn"],
                            preferred_element_type=F32)
        g_ref[...] = g
        d_ref[...], m2_ref[...], v2_ref[...] = _adam_math(w_ref[...], g, m_ref[...], v_ref[...])

    row = pl.BlockSpec((rb, n), lambda i: (i, 0))
    return _pcall(
        body,
        name="adamw_w_ada",
        out_shape=[jax.ShapeDtypeStruct((d, n), F32)] * 4,
        grid=(d // rb,),
        in_specs=[pl.BlockSpec((16, rb), lambda i: (0, i)), pl.BlockSpec((16, n), lambda i: (0, 0)), row, row, row],
        out_specs=[row] * 4,
        compiler_params=_cparams(("parallel",)),
    )(conds, da16, w, m, v)


def _take_rows(src, runs, n_rows, name):
    c = src.shape[1]
    covered = sorted((dst, dst + n) for _, n, dst in runs)
    gaps, at = [], 0
    for lo, hi in covered + [(n_rows, n_rows)]:
        if lo > at:
            gaps.append((at, lo - at))
        at = max(at, hi)
    zrows = max([n for _, n in gaps], default=0)
    zeros = jnp.zeros((max(zrows, 8), c), src.dtype)

    def body(src_ref, z_ref, o_ref, sems):
        cps = [pltpu.make_async_copy(src_ref.at[pl.ds(off, n)], o_ref.at[pl.ds(dst, n)], sems.at[i])
               for i, (off, n, dst) in enumerate(runs)]
        cps += [pltpu.make_async_copy(z_ref.at[pl.ds(0, n)], o_ref.at[pl.ds(lo, n)], sems.at[len(runs) + i])
                for i, (lo, n) in enumerate(gaps)]
        for cp in cps:
            cp.start()
        for cp in cps:
            cp.wait()

    any_spec = pl.BlockSpec(memory_space=pl.ANY)
    return _pcall(body, name=name, out_shape=jax.ShapeDtypeStruct((n_rows, c), src.dtype),
                  in_specs=[any_spec, any_spec], out_specs=any_spec,
                  scratch_shapes=[pltpu.SemaphoreType.DMA((len(runs) + len(gaps),))])(src, zeros)


def _touch(arrays, name):
    def body(*refs):
        refs[-1][...] = jnp.zeros((8, LANE), F32)

    return _pcall(body, name=name, out_shape=jax.ShapeDtypeStruct((8, LANE), F32),
                  in_specs=[pl.BlockSpec(memory_space=pl.ANY)] * len(arrays),
                  out_specs=pl.BlockSpec(memory_space=pltpu.VMEM))(*arrays)


def _cast_bf16(a, name):
    _, r, c = a.shape
    rb, cb = _tile(r, 512, 8), c
    if rb < 64 < r:
        rb, cb = r, _tile(c, 512)

    def body(a_ref, o_ref):
        o_ref[...] = a_ref[...].astype(BF16)

    return _pcall(body, name=name, out_shape=jax.ShapeDtypeStruct((r, c), BF16), grid=(r // rb, c // cb),
                  in_specs=[pl.BlockSpec((None, rb, cb), lambda i, j: (0, i, j))],
                  out_specs=pl.BlockSpec((rb, cb), lambda i, j: (i, j)),
                  compiler_params=_cparams(("parallel", "parallel")))(a)


def _rope_tabs(t, rot):
    half, q = rot // 2, rot // 4
    n_rows = t // GRID_W
    row = jnp.repeat(jnp.arange(n_rows, dtype=F32), GRID_W)
    col = jnp.tile(jnp.arange(GRID_W, dtype=F32), n_rows)
    inv_freq = ROPE_THETA ** (-jnp.arange(0, half, 2, dtype=F32) / half)
    ang = jnp.concatenate([row[:, None] * inv_freq, col[:, None] * inv_freq], axis=-1)
    cos, sin = jnp.cos(ang), jnp.sin(ang)
    c0, c1, s0, s1 = cos[:, :q], cos[:, q:], sin[:, :q], sin[:, q:]
    z = jnp.zeros_like(s0)
    return (jnp.concatenate([c0, c0, c1, c1], -1), jnp.concatenate([-s0, z, -s1, z], -1),
            jnp.concatenate([z, s0, z, s1], -1))


def _pad_cols(a, left, total, fill=0.0):
    return jnp.pad(a, ((0, 0), (left, total - left - a.shape[1])), constant_values=fill)


def _with_ctx_rows(tab, tc, fill):
    return jnp.concatenate([tab, jnp.full((tc, tab.shape[1]), fill, F32)], axis=0)


def kernel(x, c, ctx, c_ctx, w_ada, b_ada, norm1_g, w_in, mla_q_norm_g, w_q_up, mla_kv_norm_g, w_kv_up, gqa_q_norm_g, gqa_k_norm_g, w_br_a, w_br_b, w_out, norm2_g, w_up, conv_w, conv_b, w_down, final_norm_g, loss_target, m_c_ctx, m_w_ada, m_b_ada, m_norm1_g, m_w_in, m_mla_q_norm_g, m_w_q_up, m_mla_kv_norm_g, m_w_kv_up, m_gqa_q_norm_g, m_gqa_k_norm_g, m_w_br_a, m_w_br_b, m_w_out, m_norm2_g, m_w_up, m_conv_w, m_conv_b, m_w_down, m_final_norm_g, v_c_ctx, v_w_ada, v_b_ada, v_norm1_g, v_w_in, v_mla_q_norm_g, v_w_q_up, v_mla_kv_norm_g, v_w_kv_up, v_gqa_q_norm_g, v_gqa_k_norm_g, v_w_br_a, v_w_br_b, v_w_out, v_norm2_g, v_w_up, v_conv_w, v_conv_b, v_w_down, v_final_norm_g):
    weights = dict(c_ctx=c_ctx, w_ada=w_ada, b_ada=b_ada, norm1_g=norm1_g, w_in=w_in, mla_q_norm_g=mla_q_norm_g,
                   w_q_up=w_q_up, mla_kv_norm_g=mla_kv_norm_g, w_kv_up=w_kv_up, gqa_q_norm_g=gqa_q_norm_g,
                   gqa_k_norm_g=gqa_k_norm_g, w_br_a=w_br_a, w_br_b=w_br_b, w_out=w_out, norm2_g=norm2_g, w_up=w_up,
                   conv_w=conv_w, conv_b=conv_b, w_down=w_down, final_norm_g=final_norm_g)
    mom_m = dict(c_ctx=m_c_ctx, w_ada=m_w_ada, b_ada=m_b_ada, norm1_g=m_norm1_g, w_in=m_w_in, mla_q_norm_g=m_mla_q_norm_g,
                 w_q_up=m_w_q_up, mla_kv_norm_g=m_mla_kv_norm_g, w_kv_up=m_w_kv_up, gqa_q_norm_g=m_gqa_q_norm_g,
                 gqa_k_norm_g=m_gqa_k_norm_g, w_br_a=m_w_br_a, w_br_b=m_w_br_b, w_out=m_w_out, norm2_g=m_norm2_g,
                 w_up=m_w_up, conv_w=m_conv_w, conv_b=m_conv_b, w_down=m_w_down, final_norm_g=m_final_norm_g)
    mom_v = dict(c_ctx=v_c_ctx, w_ada=v_w_ada, b_ada=v_b_ada, norm1_g=v_norm1_g, w_in=v_w_in, mla_q_norm_g=v_mla_q_norm_g,
                 w_q_up=v_w_q_up, mla_kv_norm_g=v_mla_kv_norm_g, w_kv_up=v_w_kv_up, gqa_q_norm_g=v_gqa_q_norm_g,
                 gqa_k_norm_g=v_gqa_k_norm_g, w_br_a=v_w_br_a, w_br_b=v_w_br_b, w_out=v_w_out, norm2_g=v_norm2_g,
                 w_up=v_w_up, conv_w=v_conv_w, conv_b=v_conv_b, w_down=v_w_down, final_norm_g=v_final_norm_g)
    order = list(weights)

    my_idx = 4 * lax.axis_index("x") + 2 * lax.axis_index("y") + lax.axis_index("c")
    xs, cts, tgt = x[0], ctx[0], loss_target[0]
    t, d = xs.shape
    tc = cts.shape[0]
    ta = t + tc
    kvl, ql = MLA_KV_LORA, MLA_Q_LORA
    nb = GQA_KV_HEADS * GQA_HEAD_DIM
    hb = GQA_HEADS * GQA_HEAD_DIM
    ha = MLA_HEADS
    f2 = w_up.shape[2] * N_DEV
    ff = f2 // 2

    big = ["w_in", "w_q_up", "w_kv_up", "w_br_a", "w_br_b", "w_out", "w_up", "w_down"]
    nw = len(big)
    del nw
    _ORDER_AFTER.clear()
    narrow = ("w_in", "w_q_up")

    def tview(a):
        return jnp.transpose(a, (0, 2, 1))

    shards = {"w_in": _cast_bf16(tview(weights["w_in"]), "cast_w_in")}
    c_idx = jnp.reshape(lax.axis_index("c"), (1,)).astype(jnp.int32)

    def gather_start(names, dep):
        shs = [shards[n] for n in names]
        land = [lax.empty((N_DEV,) + s.shape, BF16) for s in shs]
        if dep is not None:
            _after(dep)
        s, r, arrs, tok = _split_start("gather_ici_start_" + names[0], shs + land, _gather_ici_copies(len(names)),
                                       4 * len(names))
        return dict(names=names, s=s, r=r, arrs=arrs, tok=tok)

    def gather_pass(g, after):
        n = len(g["names"])
        arrs = _split_wait("gather_ici_wait_" + g["names"][0], g["s"], g["r"], g["arrs"], _gather_ici_copies(n), after)
        s, r, bufs, tok = _split_start("gather_pass_start_" + g["names"][0], arrs[n:], _gather_pass_copies(n), 3 * n)
        g.update(s2=s, r2=r, bufs=bufs)
        return tok

    def gather_relay(g, after):
        n = len(g["names"])
        bufs = _split_wait("gather_pass_wait_" + g["names"][0], g["s2"], g["r2"], g["bufs"], _gather_pass_copies(n), after)
        s, r, bufs, tok = _split_start("gather_d2d_start_" + g["names"][0], bufs, _gather_d2d_copies(n), n)
        g.update(s3=s, r3=r, bufs=bufs)
        return tok

    def gather_finish(g, after):
        n = len(g["names"])
        bufs = _split_wait("gather_d2d_wait_" + g["names"][0], g["s3"], g["r3"], g["bufs"], _gather_d2d_copies(n), after)
        return dict(zip(g["names"], bufs))

    c_all, cw_all = _all_gather([jnp.pad(c, ((0, 7), (0, 0))), jnp.pad(conv_w[0], ((0, 5), (0, 0)))], "gather_cond")
    conv_w_f = jnp.transpose(cw_all[:, :3, :], (1, 0, 2)).reshape(3, f2)
    conds = jnp.concatenate([c_all[:, 0, :], c_ctx[None, :], jnp.zeros((7, d), F32)], axis=0)
    ncol = w_ada.shape[2]
    b_shard = lax.dynamic_slice_in_dim(b_ada, my_idx * ncol, ncol, axis=1)
    ada_shard = _ada_fwd(conds, w_ada[0], b_shard)
    (ada_all,) = _all_gather([ada_shard], "gather_ada")
    ada = jnp.transpose(ada_all, (1, 0, 2)).reshape(16, N_DEV * ncol)
    lat = lax.dynamic_slice_in_dim(ada, my_idx, 1, axis=0).reshape(6, d)
    cxt = ada[8].reshape(6, d)
    zero2 = jnp.zeros((2, d), F32)
    mods1 = jnp.concatenate([lat[0:2], cxt[0:2], jnp.zeros((4, d), F32)], axis=0)
    mods2 = jnp.concatenate([lat[2:3], lat[3:4], lat[4:5], jnp.zeros((5, d), F32)], axis=0)
    mods2b = jnp.concatenate([lat[2:3], lat[4:5], jnp.zeros((6, d), F32)], axis=0)
    mods3 = jnp.concatenate([lat[5:6], jnp.zeros((7, d), F32)], axis=0)
    del zero2

    g0 = gather_start(["w_in"], ada_all)
    for n in big[1:]:
        _after(g0["tok"])
        shards[n] = _cast_bf16(tview(weights[n]) if n in narrow else weights[n], "cast_" + n)

    ca, s1a, s2a = _rope_tabs(t, MLA_ROPE)
    cb_, s1b, s2b = _rope_tabs(t, GQA_HEAD_DIM)
    q_tabs_a = (_pad_cols(jnp.concatenate([jnp.ones((t, MLA_NOPE), F32), ca], 1), 0, MLA_SLOT),
                _pad_cols(s1a, MLA_NOPE, MLA_SLOT), _pad_cols(s2a, MLA_NOPE, MLA_SLOT))
    q_tabs_b = (cb_, s1b, s2b)
    k_tabs = (_with_ctx_rows(_pad_cols(ca, 0, LANE), tc, 1.0), _with_ctx_rows(_pad_cols(s1a, 0, LANE), tc, 0.0),
              _with_ctx_rows(_pad_cols(s2a, 0, LANE), tc, 0.0),
              _with_ctx_rows(cb_, tc, 1.0), _with_ctx_rows(s1b, tc, 0.0), _with_ctx_rows(s2b, tc, 0.0))

    def cols_full(g):
        return jnp.transpose(g, (1, 0, 2)).reshape(g.shape[1], N_DEV * g.shape[2])

    _after(*q_tabs_a, *q_tabs_b, *k_tabs, *[shards[n] for n in big[1:]])
    tok_p0 = gather_pass(g0, mods1)
    g1 = gather_start(["w_q_up", "w_kv_up", "w_br_a", "w_br_b", "w_out"], tok_p0)
    _after(g1["tok"])
    z_all = _norm_mod_fwd(cts, xs, norm1_g, mods1)
    gathered = gather_finish(g0, gather_relay(g0, z_all))
    wt_in = gathered["w_in"].reshape(-1, d)
    o_kpe, o_kb, o_vb = kvl, kvl + MLA_ROPE, kvl + MLA_ROPE + nb
    o_q = o_vb + nb
    o_g = o_q + ql + hb
    wkv_w = kvl + 2 * nb + LANE
    wt_kv_p = _take_rows(wt_in, [(0, kvl, 0), (o_kb, 2 * nb, kvl), (o_kpe, MLA_ROPE, kvl + 2 * nb)], wkv_w,
                         "take_w_in_kv")
    q_w = ql + hb
    q_pad = (-q_w) % 512 if d >= 512 else (-q_w) % d
    gate_blk = (q_w + q_pad) // d
    assert (q_w + q_pad) % d == 0
    wt_qg_p = _take_rows(wt_in, [(o_q, q_w, 0), (o_g, 2 * d, q_w + q_pad)], q_w + q_pad + 2 * d,
                         "take_w_in_qg")

    kv_all = _mm(z_all, wt_kv_p, "nt", F32, "proj_kv", tm=1152, tn=wkv_w)
    qg = _mm(z_all, wt_qg_p, "nt", F32, "proj_qg", tm=1024, tn=1024, rows=t)
    tok_p1 = gather_pass(g1, qg)
    g2 = gather_start(["w_up"], tok_p1)
    g3 = gather_start(["w_down"], g2["tok"])
    _after(g3["tok"])
    kin, k_b, v_b = _key_prep_fwd(kv_all, mla_kv_norm_g, gqa_k_norm_g, k_tabs)
    sc_a = float((MLA_NOPE + MLA_ROPE) ** -0.5) * LOG2E
    sc_b = float(GQA_HEAD_DIM ** -0.5) * LOG2E
    _after(g3["tok"])
    cqn, q_b = _q_prep_fwd(qg, mla_q_norm_g, gqa_q_norm_g, q_tabs_b, sc_b)
    _after(kin, g3["tok"])
    gathered.update(gather_finish(g1, gather_relay(g1, q_b)))

    wqt_f = gathered["w_q_up"].reshape(ha, MLA_NOPE + MLA_ROPE, ql)
    wqt_ext = jnp.pad(wqt_f, ((0, 0), (0, MLA_SLOT - MLA_NOPE - MLA_ROPE), (0, 0))).reshape(ha * MLA_SLOT, ql)
    wkv_f = cols_full(gathered["w_kv_up"]).reshape(kvl, ha, MLA_NOPE + MLA_V)
    wk_slots = jnp.pad(wkv_f[:, :, :MLA_NOPE], ((0, 0), (0, 0), (0, MLA_SLOT - MLA_NOPE))).reshape(kvl, ha * MLA_SLOT)
    wv_cols = wkv_f[:, :, MLA_NOPE:].reshape(kvl, ha * MLA_V)
    e_slot = jnp.pad(jnp.eye(MLA_ROPE, dtype=BF16),
                     ((0, LANE - MLA_ROPE), (MLA_NOPE, MLA_SLOT - MLA_NOPE - MLA_ROPE)))
    e_rows = jnp.concatenate([jnp.tile(e_slot, (1, ha)), jnp.zeros((LANE, ha * MLA_V), BF16)], axis=1)
    wkv_ext = jnp.concatenate([jnp.concatenate([wk_slots, wv_cols], axis=1), e_rows], axis=0)
    w_bra = cols_full(gathered["w_br_a"])
    w_brb = cols_full(gathered["w_br_b"])
    w_out_f = gathered["w_out"].reshape(d, d)

    kv_a = _mm(kin, wkv_ext, "nn", BF16, "kv_up", tm=1152, tn=1024)
    qa_raw = _mm(cqn, wqt_ext, "nt", F32, "q_up", tm=1024, tn=1024)
    q_a = _rope_a(qa_raw, q_tabs_a, False, BF16, "rope_q_fwd", sc_a)
    att_a = dict(hq=ha, hkv=ha, dk=MLA_SLOT, dv=MLA_V, k_blk0=0, v_blk0=ha * MLA_SLOT // MLA_V)
    att_b = dict(hq=GQA_HEADS, hkv=GQA_KV_HEADS, dk=GQA_HEAD_DIM, dv=GQA_HEAD_DIM, k_blk0=0, v_blk0=0)
    o_a, lse_a = _attention_fwd(q_a, kv_a, kv_a, name="attn_a_fwd", **att_a)
    o_b, lse_b = _attention_fwd(q_b, k_b, v_b, name="attn_b_fwd", **att_b)
    _after(o_a)
    _after(gather_pass(g2, o_b))
    pa = _mm(o_a, w_bra, "nn", BF16, "br_a", tm=1024, tn=1024)
    pb = _mm(o_b, w_brb, "nn", BF16, "br_b", tm=1024, tn=1024)
    merged = _merge_fwd(pa, pb, qg, gate_blk)
    attn = _mm(merged, w_out_f, "nn", F32, "w_out", tm=1024, tn=1024)
    x1, z2 = _resid_norm_mod(xs, attn, norm2_g, mods2, "resid_norm2_fwd")
    tok_r2 = gather_relay(g2, z2)
    tok_p3 = gather_pass(g3, tok_r2)
    w_up3 = gather_finish(g2, tok_p3)["w_up"]
    u = _mm_up_fwd(z2, w_up3, "w_up")
    tok_r3 = gather_relay(g3, u)
    _after(tok_r3)
    h, uc = _conv_fwd(u, conv_w_f, conv_b)
    w_down_f = gather_finish(g3, h)["w_down"].reshape(ff, d)
    ffn = _mm(h, w_down_f, "nn", F32, "w_down", tm=1024, tn=1024, tk=2816)

    def to_shards(g):
        return jnp.transpose(g.reshape(g.shape[0], N_DEV, g.shape[1] // N_DEV), (1, 0, 2))

    def reduce_start(tag, names, sends):
        n = len(sends)
        land = [lax.empty((4,) + s.shape[1:], s.dtype) for s in sends]
        s, r, arrs, tok = _split_start("reduce_d2d_start_" + tag, sends + land, _reduce_d2d_copies(n), 4 * n)
        return dict(tag=tag, names=names, s=s, r=r, arrs=arrs, tok=tok)

    def reduce_relay(g, after):
        n = len(g["names"])
        arrs = _split_wait("reduce_d2d_wait_" + g["tag"], g["s"], g["r"], g["arrs"], _reduce_d2d_copies(n), after)
        sums = [_pair_sum(arrs[a], arrs[n + a], c_idx, "pair_sum_" + g["names"][a]) for a in range(n)]
        land = [lax.empty(s.shape, s.dtype) for s in sums]
        s, r, arrs2, tok = _split_start("reduce_ici_start_" + g["tag"], sums + land, _reduce_ici_copies(n), 4 * n)
        g.update(s2=s, r2=r, arrs2=arrs2)
        return tok

    def reduce_finish(g, after):
        n = len(g["names"])
        arrs2 = _split_wait("reduce_ici_wait_" + g["tag"], g["s2"], g["r2"], g["arrs2"], _reduce_ici_copies(n), after)
        return dict(zip(g["names"], arrs2[n:]))

    dx2, dffn, st_fin = _final_loss(x1, ffn, final_norm_g[None, :], mods3, tgt)
    dh = _mm(dffn, w_down_f, "nt", BF16, "d_h", tm=1024, tn=1024)
    g_w_down = _mm(h, dffn, "tn", BF16, "g_w_down", tm=512, tn=1024)
    r_down = reduce_start("down", ["w_down"], [g_w_down.reshape(N_DEV, ff // N_DEV, d)])
    _after(r_down["tok"])
    du3, dcw, dcb = _conv_bwd(u, uc, conv_w_f, dh)
    dz2 = _mm_up_dz(du3, w_up3, "d_z2")
    g_w_up = _mm_up_gw(z2, du3, N_DEV, "g_w_up")
    g_conv_w = jnp.concatenate([dcw[0], dcw[1]], axis=1)
    tok = reduce_relay(r_down, g_w_up)
    _after(tok)
    r_up = reduce_start("up", ["w_up", "conv_w"], [g_w_up, to_shards(jnp.pad(g_conv_w, ((0, 5), (0, 0))))])
    _after(tok, r_up["tok"])
    dx1, dattn, st_n2 = _norm2_bwd(x1, attn, norm2_g, mods2b, dz2, dx2)
    dmerged = _mm(dattn, w_out_f, "nt", BF16, "d_merged", tm=1024, tn=1024)
    g_w_out = _mm(merged, dattn, "tn", BF16, "g_w_out", tm=1024, tn=1024)
    dpa, dpb, dgates = _merge_bwd(dmerged, pa, pb, qg, gate_blk)
    do_a = _mm(dpa, w_bra, "nt", BF16, "d_o_a", tm=1024, tn=1024)
    do_b = _mm(dpb, w_brb, "nt", BF16, "d_o_b", tm=1024, tn=1024)
    g_w_bra = _mm(o_a, dpa, "tn", BF16, "g_w_br_a", tm=1024, tn=1024)
    g_w_brb = _mm(o_b, dpb, "tn", BF16, "g_w_br_b", tm=1024, tn=1024)
    tok = reduce_relay(r_up, g_w_brb)
    _after(tok)
    r_out = reduce_start("out", ["w_out", "w_br_a", "w_br_b"],
                         [g_w_out.reshape(N_DEV, d // N_DEV, d), to_shards(g_w_bra), to_shards(g_w_brb)])
    _after(tok, r_out["tok"])
    dq_a, dk_a, dv_a = _attention_bwd(q_a, kv_a, kv_a, do_a, lse_a, name="attn_a_bwd", **att_a)
    dq_b, dk_b, dv_b = _attention_bwd(q_b, k_b, v_b, do_b, lse_b, name="attn_b_bwd", **att_b)
    _after(reduce_relay(r_out, dv_b))
    dqa_raw = _rope_a(dq_a, q_tabs_a, True, BF16, "rope_q_bwd", sc_a * LN2)
    dcqn = _mm(dqa_raw, wqt_ext, "nn", F32, "d_cqn", tm=1024, tn=ql)
    g_wqt_ext = _mm(dqa_raw, cqn, "tn", BF16, "g_w_q_up", tm=1024, tn=ql)
    dq_p, st_q, st_qb = _q_prep_bwd(qg, mla_q_norm_g, gqa_q_norm_g, q_tabs_b, dcqn, dq_b, q_pad, sc_b * LN2)
    dkin = _mm_cat_nt([(dk_a, wkv_ext, 0), (dv_a, wkv_ext, ha * MLA_SLOT)], F32, "d_kin", tm=1152, tn=kvl + LANE)
    g_wkv_ext = _mm_cat_tn(kin, [dk_a, dv_a], BF16, "g_w_kv_up", tm=kvl + LANE, tn=min(1024, ha * MLA_V))
    dkv_p, st_kv, st_kb = _key_prep_bwd(kv_all, mla_kv_norm_g, gqa_k_norm_g, k_tabs, dkin, dk_b, dv_b)
    g_wqt = g_wqt_ext.reshape(ha, MLA_SLOT, ql)[:, :MLA_NOPE + MLA_ROPE, :].reshape(N_DEV, -1, ql)
    g_wkv = jnp.concatenate([g_wkv_ext[:kvl, :ha * MLA_SLOT].reshape(kvl, ha, MLA_SLOT)[:, :, :MLA_NOPE],
                             g_wkv_ext[:kvl, ha * MLA_SLOT:].reshape(kvl, ha, MLA_V)], axis=2).reshape(kvl, ha * (MLA_NOPE + MLA_V))
    r_qkv = reduce_start("qkv", ["w_q_up", "w_kv_up"], [g_wqt, to_shards(g_wkv)])
    _after(r_qkv["tok"])
    g_wkv_p = _mm(dkv_p, z_all, "tn", BF16, "g_w_in_kv", tm=wkv_w, tn=1024)
    g_wqg_p = _mm_rows_tn([dq_p, dgates], z_all, BF16, "g_w_in_qg", tm=min(1024, d), tn=1024, rows=t)
    g_wt_in = jnp.concatenate([g_wkv_p[:kvl], g_wkv_p[kvl + 2 * nb:kvl + 2 * nb + MLA_ROPE],
                               g_wkv_p[kvl:kvl + 2 * nb], g_wqg_p[:q_w], g_wqg_p[q_w + q_pad:]], axis=0)
    r_in = reduce_start("in", ["w_in"], [g_wt_in.reshape(N_DEV, -1, d)])
    _after(r_in["tok"])
    qw_p = q_w + q_pad
    dz_lat = _mm_sum_nn([(dq_p, 0, wt_qg_p, 0, qw_p), (dgates, 0, wt_qg_p, qw_p, d), (dgates, d, wt_qg_p, qw_p + d, d),
                         (dkv_p, 0, wt_kv_p, 0, wkv_w)], F32, "d_z_lat", rows=t)
    dz_ctx = _mm(dkv_p, wt_kv_p, "nn", F32, "d_z_ctx", tm=min(ROW_BLOCK, tc), tn=1024, a_row_off=t)
    tok_q = reduce_relay(r_qkv, dz_ctx)
    _after(tok_q)
    grad_x, st_n1 = _norm1_bwd(cts, xs, norm1_g, mods1, dz_ctx, dz_lat, dx1)

    res = {}

    def upd(nm, parts):
        wv, mv, vv = weights[nm], mom_m[nm], mom_v[nm]
        if wv.ndim == 1:
            wv, mv, vv = (a.reshape(1, -1) for a in (wv, mv, vv))
        if nm in narrow:
            wv, mv, vv = tview(wv), tview(mv), tview(vv)
        outs = _adamw(parts, wv, mv, vv, "adamw_" + nm)
        if nm in narrow:
            outs = [tview(o_) for o_ in outs]
        res[nm] = [o_.reshape(weights[nm].shape) for o_ in outs]

    d_lat = jnp.concatenate([st_n1[0], st_n1[1], st_n2[3], st_n2[0], st_n2[1], st_fin[1]])
    d_cxt = jnp.concatenate([st_n1[3], st_n1[4], jnp.zeros((4 * d,), F32)])
    small = jnp.concatenate([d_lat, d_cxt, st_n1[2], st_q[0], st_kv[0], st_qb[0], st_kb[0], st_n2[2],
                             jnp.concatenate([dcb[0, 0], dcb[1, 0]]), st_fin[0], st_fin[3, :LANE]])
    n_small = small.shape[0]
    pad_small = (-n_small) % LANE
    (small_all,) = _all_gather([jnp.pad(small, (0, pad_small)).reshape(1, -1)], "gather_small")
    offs = {}
    o = 0
    for nm, ln in (("d_lat", 6 * d), ("d_cxt", 6 * d), ("norm1_g", d), ("mla_q_norm_g", ql), ("mla_kv_norm_g", kvl),
                   ("gqa_q_norm_g", GQA_HEAD_DIM), ("gqa_k_norm_g", GQA_HEAD_DIM), ("norm2_g", d), ("conv_b", f2),
                   ("final_norm_g", d), ("loss", LANE)):
        offs[nm] = (o, ln)
        o += ln

    def part(nm):
        a, ln = offs[nm]
        return small_all[:, :, a:a + ln]

    loss = _sum_parts(part("loss"))[0, 0]
    d_lat_all = part("d_lat")[:, 0, :]
    d_cxt_sum = _sum_parts(part("d_cxt"))
    da16 = jnp.concatenate([d_lat_all, d_cxt_sum, jnp.zeros((7, 6 * d), F32)], axis=0)
    da16_shard = lax.dynamic_slice_in_dim(da16, my_idx * ncol, ncol, axis=1)
    cc_part = _cctx_partial(da16_shard, w_ada[0], c_ctx[None, :])
    (cc_all,) = _all_gather([cc_part], "gather_cctx")
    cc_parts = cc_all[:, 0:1, :]
    tok_i = reduce_relay(r_in, cc_all)

    _after(tok_i)
    for nm in ("norm1_g", "mla_q_norm_g", "mla_kv_norm_g", "gqa_q_norm_g", "gqa_k_norm_g", "norm2_g", "conv_b",
               "final_norm_g"):
        upd(nm, part(nm))
    upd("c_ctx", cc_parts)
    b_parts = jnp.concatenate([d_lat_all[:, None, :], d_cxt_sum[None]], axis=0)
    upd("b_ada", b_parts)
    _after(tok_i)
    outs = _adamw_ada(conds, da16_shard, w_ada[0], m_w_ada[0], v_w_ada[0])
    res["w_ada"] = [o_[None] for o_ in outs]
    last = outs[0]
    done = [last]
    for grp in (r_down, r_up, r_out, r_qkv, r_in):
        _after(*done)
        recv = reduce_finish(grp, last)
        for nm in grp["names"]:
            upd(nm, recv[nm][:, :3, :] if nm == "conv_w" else recv[nm])
            last = res[nm][0]
            done.append(last)

    return (loss, grad_x[None], *[res[n][0] for n in order], *[res[n][1] for n in order],
            *[res[n][2] for n in order], *[res[n][3] for n in order])
```

```python
import functools

import jax
import jax.numpy as jnp
from jax import lax
from jax.experimental import pallas as pl
from jax.experimental.pallas import tpu as pltpu

F32 = jnp.float32
BF16 = jnp.bfloat16

GRID_W = 64
ROPE_THETA = 10000.0
NORM_EPS = 1e-6
MLA_HEADS = 8
MLA_Q_LORA = 768
MLA_KV_LORA = 512
MLA_NOPE = 128
MLA_ROPE = 64
MLA_V = 128
GQA_HEADS = 8
GQA_KV_HEADS = 2
GQA_HEAD_DIM = 128
ADAM_LR = 0.001
ADAM_B1 = 0.9
ADAM_B2 = 0.999
ADAM_EPS = 1e-08
ADAM_WD = 0.01
ADAM_STEP = 10

N_DEV = 8
MESH_AXES = ("x", "y", "c")
LANE = 128
MLA_SLOT = 2 * LANE
VMEM_LIMIT = 56 * 1024 * 1024
ROW_BLOCK = 256
ATT_Q_BLOCK = 512
ATT_Q_BLOCK_FWD = 512
LN2 = 0.6931471805599453
LOG2E = 1.4426950408889634
MESH_ID = pl.DeviceIdType.MESH


def _tile(n, pref, align=LANE):
    if n <= pref:
        return n
    best = None
    t = align
    while t <= pref:
        if n % t == 0:
            best = t
        t += align
    assert best is not None, (n, pref, align)
    return best


def _cparams(sem=None):
    return pltpu.CompilerParams(dimension_semantics=sem, vmem_limit_bytes=VMEM_LIMIT)


_ORDER_AFTER = []


def _after(*arrays):
    _ORDER_AFTER.extend(arrays)


def _pcall(body, *, in_specs, **kw):
    deps = tuple(_ORDER_AFTER)
    _ORDER_AFTER.clear()
    if not deps:
        return pl.pallas_call(body, in_specs=in_specs, **kw)
    n_in, n_dep = len(in_specs), len(deps)

    def with_deps(*refs):
        body(*refs[:n_in], *refs[n_in + n_dep:])

    call = pl.pallas_call(with_deps, in_specs=list(in_specs) + [pl.BlockSpec(memory_space=pl.ANY)] * n_dep, **kw)
    return lambda *args: call(*args, *deps)


def _all_gather(arrs, name):
    n = len(arrs)

    def body(*refs):
        ins = refs[:n]
        outs = refs[n:2 * n]
        send_sems, recv_sems, local_sems = refs[2 * n:]
        x, y, c = lax.axis_index("x"), lax.axis_index("y"), lax.axis_index("c")
        me, sibling = (x, y, c), (x, y, 1 - c)
        chips = [(1 - x, y), (x, 1 - y), (1 - x, 1 - y)]

        def rows(a, dev):
            px, py, pc = dev
            return outs[a].at[4 * px + 2 * py + pc]

        def copy(a, k, block, to, src=None):
            return pltpu.make_async_remote_copy(
                src_ref=rows(a, block) if src is None else src,
                dst_ref=rows(a, block),
                send_sem=send_sems.at[7 * a + k],
                recv_sem=recv_sems.at[7 * a + k],
                device_id=to,
                device_id_type=MESH_ID,
            )

        mine = [pltpu.make_async_copy(ins[a], rows(a, me), local_sems.at[a]) for a in range(n)]
        for cp in mine:
            cp.start()
        first = []
        for a in range(n):
            first.append(copy(a, 0, me, sibling, src=ins[a]))
            first += [copy(a, 1 + j, me, (*chip, c), src=ins[a]) for j, chip in enumerate(chips)]
        for cp in first:
            cp.start()
        passed = []
        for j, chip in enumerate(chips):
            for a in range(n):
                copy(a, 1 + j, (*chip, c), me).wait_recv()
                fwd = copy(a, 4 + j, (*chip, c), sibling)
                fwd.start()
                passed.append(fwd)
        for a in range(n):
            copy(a, 0, sibling, me).wait_recv()
            for j, chip in enumerate(chips):
                copy(a, 4 + j, (*chip, 1 - c), me).wait_recv()
        for cp in first + passed:
            cp.wait_send()
        for cp in mine:
            cp.wait()

    any_spec = pl.BlockSpec(memory_space=pl.ANY)
    outs = _pcall(
        body,
        name=name,
        out_shape=[jax.ShapeDtypeStruct((N_DEV,) + a.shape, a.dtype) for a in arrs],
        in_specs=[any_spec] * n,
        out_specs=[any_spec] * n,
        scratch_shapes=[
            pltpu.SemaphoreType.DMA((7 * n,)),
            pltpu.SemaphoreType.DMA((7 * n,)),
            pltpu.SemaphoreType.DMA((n,)),
        ],
    )(*arrs)
    return list(outs)


def _all_to_all(arrs, name):
    n = len(arrs)

    def body(*refs):
        ins = refs[:n]
        outs = refs[n:2 * n]
        send_sems, recv_sems, local_sems = refs[2 * n:]
        x, y, c = lax.axis_index("x"), lax.axis_index("y"), lax.axis_index("c")
        my_idx = 4 * x + 2 * y + c

        def peer(k):
            fx, fy, fc = (k >> 2) & 1, (k >> 1) & 1, k & 1
            return (x ^ fx if fx else x, y ^ fy if fy else y, c ^ fc if fc else c)

        def copy(a, k):
            px, py, pc = peer(k)
            return pltpu.make_async_remote_copy(
                src_ref=ins[a].at[4 * px + 2 * py + pc],
                dst_ref=outs[a].at[my_idx],
                send_sem=send_sems.at[7 * a + k - 1],
                recv_sem=recv_sems.at[7 * a + k - 1],
                device_id=(px, py, pc),
                device_id_type=MESH_ID,
            )

        mine = [pltpu.make_async_copy(ins[a].at[my_idx], outs[a].at[my_idx], local_sems.at[a]) for a in range(n)]
        for cp in mine:
            cp.start()
        order = [1, 4, 2, 5, 3, 6, 7]
        cps = [copy(a, k) for k in order for a in range(n)]
        for cp in cps:
            cp.start()
        for cp in cps:
            cp.wait()
        for cp in mine:
            cp.wait()

    any_spec = pl.BlockSpec(memory_space=pl.ANY)
    outs = _pcall(
        body,
        name=name,
        out_shape=[jax.ShapeDtypeStruct(a.shape, a.dtype) for a in arrs],
        in_specs=[any_spec] * n,
        out_specs=[any_spec] * n,
        scratch_shapes=[
            pltpu.SemaphoreType.DMA((7 * n,)),
            pltpu.SemaphoreType.DMA((7 * n,)),
            pltpu.SemaphoreType.DMA((n,)),
        ],
    )(*arrs)
    return list(outs)


_HBM = pl.BlockSpec(memory_space=pltpu.HBM)
_SEM = pl.BlockSpec(memory_space=pltpu.SEMAPHORE)
_EFFECT = pltpu.SideEffectType.DATAFLOW_SIDE_EFFECTING


def _descriptors(copies, send_sems, recv_sems):
    descs = []
    for i, (src, dst, dev) in enumerate(copies):
        if dev is None:
            descs.append(pltpu.make_async_copy(src, dst, recv_sems.at[i]))
        else:
            descs.append(pltpu.make_async_remote_copy(src_ref=src, dst_ref=dst, send_sem=send_sems.at[i],
                                                      recv_sem=recv_sems.at[i], device_id=dev, device_id_type=MESH_ID))
    return descs


def _split_start(name, arrays, copies_fn, n_copies):
    n = len(arrays)

    def body(*refs):
        send_sems, recv_sems = refs[n], refs[n + 1]
        token = refs[2 * n + 2]
        for dsc in _descriptors(copies_fn(refs[:n]), send_sems, recv_sems):
            dsc.start()
        token[...] = jnp.zeros_like(token)

    outs = _pcall(
        body,
        name=name,
        out_shape=(pltpu.SemaphoreType.DMA((n_copies,)), pltpu.SemaphoreType.DMA((n_copies,)),
                   *[pltpu.HBM(a.shape, a.dtype) for a in arrays], jax.ShapeDtypeStruct((8, LANE), F32)),
        in_specs=[_HBM] * n,
        out_specs=(_SEM, _SEM, *[_HBM] * n, pl.BlockSpec(memory_space=pltpu.VMEM)),
        input_output_aliases={i: 2 + i for i in range(n)},
        compiler_params=pltpu.CompilerParams(has_side_effects=_EFFECT),
    )(*[pltpu.with_memory_space_constraint(a, pltpu.HBM) for a in arrays])
    return outs[0], outs[1], list(outs[2:2 + n]), outs[2 + n]


def _split_wait(name, send_sems, recv_sems, arrays, copies_fn, after):
    n = len(arrays)

    def body(*refs):
        for dsc, (_, _, dev) in zip(_descriptors(copies_fn(refs[:n]), refs[n], refs[n + 1]), copies_fn(refs[:n])):
            if dev is None:
                dsc.wait()
            else:
                dsc.wait_send()
                dsc.wait_recv()

    outs = _pcall(
        body,
        name=name,
        out_shape=tuple(pltpu.HBM(a.shape, a.dtype) for a in arrays),
        in_specs=[_HBM] * n + [_SEM, _SEM, pl.BlockSpec(memory_space=pl.ANY)],
        out_specs=tuple([_HBM] * n),
        input_output_aliases={i: i for i in range(n)},
        compiler_params=pltpu.CompilerParams(has_side_effects=_EFFECT),
    )(*arrays, send_sems, recv_sems, after)
    return list(outs)


def _mesh_pos():
    x, y, c = lax.axis_index("x"), lax.axis_index("y"), lax.axis_index("c")
    return x, y, c, [(1 - x, y), (x, 1 - y), (1 - x, 1 - y)]


def _gather_ici_copies(n):
    def copies(refs):
        x, y, c, chips = _mesh_pos()
        me = 4 * x + 2 * y + c
        out = []
        for a in range(n):
            src, buf = refs[a], refs[n + a]
            out.append((src, buf.at[me], None))
            out.append((src, buf.at[me], (x, y, 1 - c)))
            out += [(src, buf.at[me], (cx, cy, c)) for cx, cy in chips[:2]]
        return out
    return copies


def _gather_pass_copies(n):
    def copies(refs):
        x, y, c, chips = _mesh_pos()
        south = c == 0
        bx, by = jnp.where(south, 1 - x, x), jnp.where(south, y, 1 - y)
        tx, ty = jnp.where(south, x, 1 - x), jnp.where(south, 1 - y, y)
        out = []
        for a in range(n):
            rows = refs[a].at[4 * bx + 2 * by + c]
            out.append((rows, rows, (tx, ty, c)))
            for cx, cy in chips[:2]:
                rows = refs[a].at[4 * cx + 2 * cy + c]
                out.append((rows, rows, (x, y, 1 - c)))
        return out
    return copies


def _gather_d2d_copies(n):
    def copies(refs):
        x, y, c, chips = _mesh_pos()
        cx, cy = chips[2]
        out = []
        for a in range(n):
            rows = refs[a].at[4 * cx + 2 * cy + c]
            out.append((rows, rows, (x, y, 1 - c)))
        return out
    return copies


def _reduce_d2d_copies(n):
    def copies(refs):
        x, y, c, _ = _mesh_pos()
        out = []
        for a in range(n):
            for k in range(4):
                out.append((refs[a].at[2 * k + (1 - c)], refs[n + a].at[k], (x, y, 1 - c)))
        return out
    return copies


def _reduce_ici_copies(n):
    def copies(refs):
        x, y, c, chips = _mesh_pos()
        mine = 2 * x + y
        out = []
        for a in range(n):
            src, land = refs[a], refs[n + a]
            out.append((src.at[mine], land.at[mine], None))
            out += [(src.at[2 * cx + cy], land.at[mine], (cx, cy, c)) for cx, cy in chips]
        return out
    return copies


def _pair_sum(send, land, c_idx, name):
    _, r, cols = send.shape
    rb = _tile(r, max(8, (1 << 22) // (send.dtype.itemsize * cols) // 8 * 8), 8)
    dt = send.dtype

    def body(c_ref, s_ref, l_ref, o_ref):
        o_ref[...] = (s_ref[...].astype(F32) + l_ref[...].astype(F32)).astype(dt)

    return pl.pallas_call(
        body,
        name=name,
        out_shape=jax.ShapeDtypeStruct((4, r, cols), dt),
        grid_spec=pltpu.PrefetchScalarGridSpec(
            num_scalar_prefetch=1,
            grid=(4, r // rb),
            in_specs=[pl.BlockSpec((None, rb, cols), lambda k, i, c_ref: (2 * k + c_ref[0], i, 0)),
                      pl.BlockSpec((None, rb, cols), lambda k, i, c_ref: (k, i, 0))],
            out_specs=pl.BlockSpec((None, rb, cols), lambda k, i, c_ref: (k, i, 0)),
        ),
        compiler_params=_cparams(("parallel", "parallel")),
    )(c_idx, send, land)


_DIMS = {
    "nn": (((1,), (0,)), ((), ())),
    "nt": (((1,), (1,)), ((), ())),
    "tn": (((0,), (0,)), ((), ())),
}


def _mm_call(a, b, *, mode, grid, a_spec, b_spec, o_spec, out_shape, acc_shape, name):
    nk = grid[2]
    out_dtype = out_shape.dtype

    def body(a_ref, b_ref, o_ref, *scratch):
        p = lax.dot_general(a_ref[...].astype(BF16), b_ref[...].astype(BF16), _DIMS[mode],
                            preferred_element_type=F32)
        if nk == 1:
            o_ref[...] = p.astype(out_dtype)
        else:
            acc = scratch[0]
            k = pl.program_id(2)

            @pl.when(k == 0)
            def _():
                acc[...] = p

            @pl.when(k > 0)
            def _():
                acc[...] += p

            @pl.when(k == nk - 1)
            def _():
                o_ref[...] = acc[...].astype(out_dtype)

    return _pcall(
        body,
        name=name,
        out_shape=out_shape,
        grid=grid,
        in_specs=[a_spec, b_spec],
        out_specs=o_spec,
        scratch_shapes=[pltpu.VMEM(acc_shape, F32)] if nk > 1 else [],
        compiler_params=_cparams(("parallel", "parallel", "arbitrary")),
    )(a, b)


def _mm(a, b, mode, out_dtype, name, tm=512, tn=512, tk=2432, a_row_off=0, rows=None):
    if mode == "nn":
        (m, k), (k2, n) = a.shape, b.shape
    elif mode == "nt":
        (m, k), (n, k2) = a.shape, b.shape
    else:
        (k, m), (k2, n) = a.shape, b.shape
        if rows is not None:
            k = k2 = rows
    assert k == k2, (a.shape, b.shape, mode)
    if mode != "tn":
        m = (m if rows is None else rows + a_row_off) - a_row_off
    tm, tn, tk = _tile(m, tm, 8), _tile(n, tn), _tile(k, tk, 8 if mode == "tn" else LANE)
    assert a_row_off % tm == 0
    ro = a_row_off // tm
    grid = (m // tm, n // tn, k // tk)
    if mode == "tn":
        a_spec = pl.BlockSpec((tk, tm), lambda i, j, kk: (kk, i))
    else:
        a_spec = pl.BlockSpec((tm, tk), lambda i, j, kk: (i + ro, kk))
    if mode == "nt":
        b_spec = pl.BlockSpec((tn, tk), lambda i, j, kk: (j, kk))
    else:
        b_spec = pl.BlockSpec((tk, tn), lambda i, j, kk: (kk, j))
    o_spec = pl.BlockSpec((tm, tn), lambda i, j, kk: (i, j))
    return _mm_call(a, b, mode=mode, grid=grid, a_spec=a_spec, b_spec=b_spec, o_spec=o_spec,
                    out_shape=jax.ShapeDtypeStruct((m, n), out_dtype), acc_shape=(tm, tn), name=name)


def _mm_cat_nt(pieces, out_dtype, name, tm=1024, tn=1024, tk=2048, rows=None):
    m = pieces[0][0].shape[0] if rows is None else rows
    n = pieces[0][1].shape[0]
    tm, tn = _tile(m, tm, 8), _tile(n, tn)
    steps, starts, s = [], [], 0
    for a, b, off in pieces:
        kp = a.shape[1]
        tkp = _tile(kp, tk)
        assert off % tkp == 0 and b.shape[0] == n
        steps.append((tkp, kp // tkp, off // tkp))
        starts.append(s)
        s += kp // tkp
    nk = s
    npc = len(pieces)

    def body(*refs):
        o_ref, acc = refs[2 * npc], refs[2 * npc + 1]
        kk = pl.program_id(2)

        @pl.when(kk == 0)
        def _():
            acc[...] = jnp.zeros_like(acc)

        for p in range(npc):
            @pl.when((kk >= starts[p]) & (kk < starts[p] + steps[p][1]))
            def _(p=p):
                acc[...] += lax.dot_general(refs[2 * p][...].astype(BF16), refs[2 * p + 1][...].astype(BF16), _DIMS["nt"],
                                            preferred_element_type=F32)

        @pl.when(kk == nk - 1)
        def _():
            o_ref[...] = acc[...].astype(out_dtype)

    in_specs, args = [], []
    for p, (a, b, off) in enumerate(pieces):
        tkp, np_, ob = steps[p]

        def rel(kk, p=p, np_=np_):
            return jnp.clip(kk - starts[p], 0, np_ - 1)

        in_specs.append(pl.BlockSpec((tm, tkp), lambda i, j, kk, rel=rel: (i, rel(kk))))
        in_specs.append(pl.BlockSpec((tn, tkp), lambda i, j, kk, rel=rel, ob=ob: (j, ob + rel(kk))))
        args += [a, b]
    return _pcall(
        body,
        name=name,
        out_shape=jax.ShapeDtypeStruct((m, n), out_dtype),
        grid=(m // tm, n // tn, nk),
        in_specs=in_specs,
        out_specs=pl.BlockSpec((tm, tn), lambda i, j, kk: (i, j)),
        scratch_shapes=[pltpu.VMEM((tm, tn), F32)],
        compiler_params=_cparams(("parallel", "parallel", "arbitrary")),
    )(*args)


def _mm_cat_tn(a, pieces, out_dtype, name, tm=1024, tn=1024, rows=None):
    k = a.shape[0] if rows is None else rows
    m = a.shape[1]
    tm = _tile(m, tm)
    starts, s = [], 0
    for b in pieces:
        assert b.shape[1] % tn == 0
        starts.append(s)
        s += b.shape[1] // tn
    nj = s
    npc = len(pieces)

    def body(*refs):
        a_ref, o_ref = refs[0], refs[1 + npc]
        j = pl.program_id(1)
        for p in range(npc):
            @pl.when((j >= starts[p]) & (j < starts[p] + pieces[p].shape[1] // tn))
            def _(p=p):
                o_ref[...] = lax.dot_general(a_ref[...].astype(BF16), refs[1 + p][...].astype(BF16), _DIMS["tn"],
                                             preferred_element_type=F32).astype(out_dtype)

    in_specs = [pl.BlockSpec((k, tm), lambda i, j: (0, i))]
    for p, b in enumerate(pieces):
        np_ = b.shape[1] // tn
        in_specs.append(pl.BlockSpec((k, tn), lambda i, j, p=p, np_=np_: (0, jnp.clip(j - starts[p], 0, np_ - 1))))
    return _pcall(
        body,
        name=name,
        out_shape=jax.ShapeDtypeStruct((m, nj * tn), out_dtype),
        grid=(m // tm, nj),
        in_specs=in_specs,
        out_specs=pl.BlockSpec((tm, tn), lambda i, j: (i, j)),
        compiler_params=_cparams(("parallel", "arbitrary")),
    )(a, *pieces)


def _mm_up_fwd(z2, w3, name, tm=1024):
    t, d = z2.shape
    nsh, _, c = w3.shape
    tm = _tile(t, tm, 8)
    return _mm_call(z2, w3, mode="nn", grid=(t // tm, nsh, 1),
                    a_spec=pl.BlockSpec((tm, d), lambda i, j, kk: (i, 0)),
                    b_spec=pl.BlockSpec((None, d, c), lambda i, j, kk: (j, 0, 0)),
                    o_spec=pl.BlockSpec((tm, c), lambda i, j, kk: (i, j)),
                    out_shape=jax.ShapeDtypeStruct((t, nsh * c), BF16), acc_shape=(tm, c), name=name)


def _mm_up_dz(du3, w3, name, tm=512, tn=1024):
    _, t, f = du3.shape
    nsh, d, c = w3.shape
    half = nsh // 2
    assert f == half * c
    tm, tn = _tile(t, tm, 8), _tile(d, tn)

    def body(a_ref, b_ref, o_ref, acc):
        kk = pl.program_id(2)
        p = None
        for s in range(half):
            q = lax.dot_general(a_ref[:, s * c:(s + 1) * c], b_ref[s], _DIMS["nt"], preferred_element_type=F32)
            p = q if p is None else p + q

        @pl.when(kk == 0)
        def _():
            acc[...] = p

        @pl.when(kk == 1)
        def _():
            o_ref[...] = (acc[...] + p).astype(BF16)

    return _pcall(
        body,
        name=name,
        out_shape=jax.ShapeDtypeStruct((t, d), BF16),
        grid=(t // tm, d // tn, 2),
        in_specs=[pl.BlockSpec((None, tm, f), lambda i, j, kk: (kk, i, 0)),
                  pl.BlockSpec((half, tn, c), lambda i, j, kk: (kk, j, 0))],
        out_specs=pl.BlockSpec((tm, tn), lambda i, j, kk: (i, j)),
        scratch_shapes=[pltpu.VMEM((tm, tn), F32)],
        compiler_params=_cparams(("parallel", "parallel", "arbitrary")),
    )(du3, w3)


def _mm_sum_nn(pieces, out_dtype, name, tm=512, tn=512, rows=None):
    m = pieces[0][0].shape[0] if rows is None else rows
    n = pieces[0][2].shape[1]
    tm, tn = _tile(m, tm, 8), _tile(n, tn)
    npc = len(pieces)

    def body(*refs):
        p = None
        for s in range(npc):
            q = jnp.dot(refs[2 * s][...].astype(BF16), refs[2 * s + 1][...].astype(BF16), preferred_element_type=F32)
            p = q if p is None else p + q
        refs[2 * npc][...] = p.astype(out_dtype)

    in_specs, args = [], []
    for a, ao, b, bo, kp in pieces:
        assert ao % kp == 0 and bo % kp == 0 and b.shape[1] == n
        in_specs.append(pl.BlockSpec((tm, kp), lambda i, j, ab=ao // kp: (i, ab)))
        in_specs.append(pl.BlockSpec((kp, tn), lambda i, j, bb=bo // kp: (bb, j)))
        args += [a, b]
    return _pcall(
        body,
        name=name,
        out_shape=jax.ShapeDtypeStruct((m, n), out_dtype),
        grid=(m // tm, n // tn),
        in_specs=in_specs,
        out_specs=pl.BlockSpec((tm, tn), lambda i, j: (i, j)),
        compiler_params=_cparams(("parallel", "parallel")),
    )(*args)


def _mm_rows_tn(pieces, b, out_dtype, name, tm=1024, tn=1024, rows=None):
    k = b.shape[0] if rows is None else rows
    n = b.shape[1]
    tn = _tile(n, tn)
    starts, s = [], 0
    for a in pieces:
        assert a.shape[1] % tm == 0
        starts.append(s)
        s += a.shape[1] // tm
    ni = s
    npc = len(pieces)

    def body(*refs):
        b_ref, o_ref = refs[npc], refs[npc + 1]
        i = pl.program_id(0)
        for p in range(npc):
            @pl.when((i >= starts[p]) & (i < starts[p] + pieces[p].shape[1] // tm))
            def _(p=p):
                o_ref[...] = lax.dot_general(refs[p][...].astype(BF16), b_ref[...].astype(BF16), _DIMS["tn"],
                                             preferred_element_type=F32).astype(out_dtype)

    in_specs = []
    for p, a in enumerate(pieces):
        np_ = a.shape[1] // tm
        in_specs.append(pl.BlockSpec((k, tm), lambda i, j, p=p, np_=np_: (0, jnp.clip(i - starts[p], 0, np_ - 1))))
    in_specs.append(pl.BlockSpec((k, tn), lambda i, j: (0, j)))
    return _pcall(
        body,
        name=name,
        out_shape=jax.ShapeDtypeStruct((ni * tm, n), out_dtype),
        grid=(ni, n // tn),
        in_specs=in_specs,
        out_specs=pl.BlockSpec((tm, tn), lambda i, j: (i, j)),
        compiler_params=_cparams(("parallel", "parallel")),
    )(*pieces, b)


def _mm_up_gw(z2, du3, nsh, name, tm=1024):
    t, d = z2.shape
    f = du3.shape[2]
    half = nsh // 2
    c = f // half
    tm = _tile(d, tm)
    return _mm_call(z2, du3, mode="tn", grid=(d // tm, nsh, 1),
                    a_spec=pl.BlockSpec((t, tm), lambda i, j, kk: (0, i)),
                    b_spec=pl.BlockSpec((None, t, c), lambda i, j, kk: (j // half, 0, j % half)),
                    o_spec=pl.BlockSpec((None, tm, c), lambda i, j, kk: (j, i, 0)),
                    out_shape=jax.ShapeDtypeStruct((nsh, d, c), BF16), acc_shape=(tm, c), name=name)


def _rms(x):
    r = lax.rsqrt(jnp.mean(x * x, axis=-1, keepdims=True) + NORM_EPS)
    return x * r, r


def _rms_bwd(dxh, xh, r):
    return r * (dxh - xh * jnp.mean(dxh * xh, axis=-1, keepdims=True))


def _colsum(v):
    return jnp.sum(v, axis=0, keepdims=True)


def _rope(v, c, s1, s2, q):
    w = v.shape[-1]
    return v * c + pltpu.roll(v, w - q, 1) * s1 + pltpu.roll(v, q, 1) * s2


def _rope_t(d, c, s1, s2, q):
    w = d.shape[-1]
    return d * c + pltpu.roll(d * s1, q, 1) + pltpu.roll(d * s2, w - q, 1)


def _norm_mod_fwd(ctx, x, gain, mods):
    tc, d = ctx.shape
    t = x.shape[0]
    rb = min(ROW_BLOCK, tc)
    nbl = t // rb

    def body(ctx_ref, x_ref, g_ref, mod_ref, z_ref):
        i = pl.program_id(0)

        def emit(src, sh, sc):
            xh, _ = _rms(src[...])
            z_ref[...] = ((xh * g_ref[...]) * (1.0 + sc) + sh).astype(BF16)

        @pl.when(i >= nbl)
        def _():
            emit(ctx_ref, mod_ref[2:3, :], mod_ref[3:4, :])

        @pl.when(i < nbl)
        def _():
            emit(x_ref, mod_ref[0:1, :], mod_ref[1:2, :])

    return _pcall(
        body,
        name="norm1_mod_fwd",
        out_shape=jax.ShapeDtypeStruct((tc + t, d), BF16),
        grid=((tc + t) // rb,),
        in_specs=[
            pl.BlockSpec((rb, d), lambda i: (jnp.maximum(i - nbl, 0), 0)),
            pl.BlockSpec((rb, d), lambda i: (jnp.minimum(i, nbl - 1), 0)),
            pl.BlockSpec((1, d), lambda i: (0, 0)),
            pl.BlockSpec((8, d), lambda i: (0, 0)),
        ],
        out_specs=pl.BlockSpec((rb, d), lambda i: (i, 0)),
        compiler_params=_cparams(("arbitrary",)),
    )(ctx, x, gain, mods)


def _norm1_bwd(ctx, x, gain, mods, dz_ctx, dz_lat, dx1):
    tc, d = ctx.shape
    t = x.shape[0]
    rb = min(ROW_BLOCK, tc)
    nbl = t // rb

    def body(ctx_ref, x_ref, g_ref, mod_ref, dzc_ref, dzl_ref, dx1_ref, gx_ref, st_ref):
        i = pl.program_id(0)

        @pl.when(i == 0)
        def _():
            st_ref[...] = jnp.zeros_like(st_ref)

        def common(src, dz, sc, row_sh, row_sc):
            xh, r = _rms(src[...])
            g = g_ref[...]
            dxn = dz * (1.0 + sc)
            st_ref[row_sh:row_sh + 1, :] += _colsum(dz)
            st_ref[row_sc:row_sc + 1, :] += _colsum(dz * (xh * g))
            st_ref[2:3, :] += _colsum(dxn * xh)
            return _rms_bwd(dxn * g, xh, r)

        @pl.when(i >= nbl)
        def _():
            common(ctx_ref, dzc_ref[...], mod_ref[3:4, :], 3, 4)

        @pl.when(i < nbl)
        def _():
            gx_ref[...] = dx1_ref[...] + common(x_ref, dzl_ref[...], mod_ref[1:2, :], 0, 1)

    lat = lambda i: (jnp.minimum(i, nbl - 1), 0)
    cix = lambda i: (jnp.maximum(i - nbl, 0), 0)
    return _pcall(
        body,
        name="norm1_mod_bwd",
        out_shape=[jax.ShapeDtypeStruct((t, d), F32), jax.ShapeDtypeStruct((8, d), F32)],
        grid=((tc + t) // rb,),
        in_specs=[
            pl.BlockSpec((rb, d), cix),
            pl.BlockSpec((rb, d), lat),
            pl.BlockSpec((1, d), lambda i: (0, 0)),
            pl.BlockSpec((8, d), lambda i: (0, 0)),
            pl.BlockSpec((rb, d), cix),
            pl.BlockSpec((rb, d), lat),
            pl.BlockSpec((rb, d), lat),
        ],
        out_specs=[pl.BlockSpec((rb, d), lat), pl.BlockSpec((8, d), lambda i: (0, 0))],
        compiler_params=_cparams(("arbitrary",)),
    )(ctx, x, gain, mods, dz_ctx, dz_lat, dx1)


def _key_prep_fwd(kv, kv_gain, kb_gain, tabs):
    ta, wkv = kv.shape
    kvl = MLA_KV_LORA
    nb = GQA_KV_HEADS * GQA_HEAD_DIM
    rb = ROW_BLOCK if ta % ROW_BLOCK == 0 else LANE
    hd = GQA_HEAD_DIM

    def body(kv_ref, g_ref, gb_ref, ca, s1a, s2a, cb, s1b, s2b, kin_ref, kb_ref, vb_ref):
        xh, _ = _rms(kv_ref[:, 0:kvl])
        kin_ref[:, 0:kvl] = (xh * g_ref[...]).astype(BF16)
        kpe = kv_ref[:, kvl + 2 * nb:kvl + 2 * nb + LANE]
        kin_ref[:, kvl:kvl + LANE] = _rope(kpe, ca[...], s1a[...], s2a[...], MLA_ROPE // 4).astype(BF16)
        for h in range(GQA_KV_HEADS):
            nh, _ = _rms(kv_ref[:, kvl + h * hd:kvl + (h + 1) * hd])
            kb_ref[:, h * hd:(h + 1) * hd] = _rope(nh * gb_ref[...], cb[...], s1b[...], s2b[...], hd // 4).astype(BF16)
        vb_ref[...] = kv_ref[:, kvl + nb:kvl + 2 * nb].astype(BF16)

    row = lambda w: pl.BlockSpec((rb, w), lambda i: (i, 0))
    fix = lambda w: pl.BlockSpec((1, w), lambda i: (0, 0))
    return _pcall(
        body,
        name="key_prep_fwd",
        out_shape=[jax.ShapeDtypeStruct((ta, kvl + LANE), BF16), jax.ShapeDtypeStruct((ta, nb), BF16),
                   jax.ShapeDtypeStruct((ta, nb), BF16)],
        grid=(ta // rb,),
        in_specs=[row(wkv), fix(kvl), fix(hd)] + [row(LANE)] * 3 + [row(hd)] * 3,
        out_specs=[row(kvl + LANE), row(nb), row(nb)],
        compiler_params=_cparams(("parallel",)),
    )(kv, kv_gain, kb_gain, *tabs)


def _key_prep_bwd(kv, kv_gain, kb_gain, tabs, dkin, dkb, dvb):
    ta, wkv = kv.shape
    kvl = MLA_KV_LORA
    nb = GQA_KV_HEADS * GQA_HEAD_DIM
    rb = ROW_BLOCK if ta % ROW_BLOCK == 0 else LANE
    hd = GQA_HEAD_DIM

    def body(kv_ref, g_ref, gb_ref, ca, s1a, s2a, cb, s1b, s2b, dkin_ref, dkb_ref, dvb_ref, dkv_ref, st_ref, stb_ref):
        @pl.when(pl.program_id(0) == 0)
        def _():
            st_ref[...] = jnp.zeros_like(st_ref)
            stb_ref[...] = jnp.zeros_like(stb_ref)

        xh, r = _rms(kv_ref[:, 0:kvl])
        dn = dkin_ref[:, 0:kvl]
        st_ref[0:1, :] += _colsum(dn * xh)
        dkv_ref[:, 0:kvl] = _rms_bwd(dn * g_ref[...], xh, r).astype(BF16)
        dpe = _rope_t(dkin_ref[:, kvl:kvl + LANE], ca[...], s1a[...], s2a[...], MLA_ROPE // 4)
        dkv_ref[:, kvl + 2 * nb:kvl + 2 * nb + LANE] = dpe.astype(BF16)
        for h in range(GQA_KV_HEADS):
            nh, rh = _rms(kv_ref[:, kvl + h * hd:kvl + (h + 1) * hd])
            dn_h = _rope_t(dkb_ref[:, h * hd:(h + 1) * hd], cb[...], s1b[...], s2b[...], hd // 4)
            stb_ref[0:1, :] += _colsum(dn_h * nh)
            dkv_ref[:, kvl + h * hd:kvl + (h + 1) * hd] = _rms_bwd(dn_h * gb_ref[...], nh, rh).astype(BF16)
        dkv_ref[:, kvl + nb:kvl + 2 * nb] = dvb_ref[...].astype(BF16)

    row = lambda w: pl.BlockSpec((rb, w), lambda i: (i, 0))
    fix = lambda w: pl.BlockSpec((1, w), lambda i: (0, 0))
    return _pcall(
        body,
        name="key_prep_bwd",
        out_shape=[jax.ShapeDtypeStruct((ta, wkv), BF16), jax.ShapeDtypeStruct((8, kvl), F32),
                   jax.ShapeDtypeStruct((8, hd), F32)],
        grid=(ta // rb,),
        in_specs=[row(wkv), fix(kvl), fix(hd)] + [row(LANE)] * 3 + [row(hd)] * 3 + [row(kvl + LANE), row(nb), row(nb)],
        out_specs=[row(wkv), pl.BlockSpec((8, kvl), lambda i: (0, 0)), pl.BlockSpec((8, hd), lambda i: (0, 0))],
        compiler_params=_cparams(("arbitrary",)),
    )(kv, kv_gain, kb_gain, *tabs, dkin, dkb, dvb)


def _q_prep_fwd(qg, q_gain, qb_gain, tabs, qscale):
    t = qg.shape[0]
    ql = MLA_Q_LORA
    hd = GQA_HEAD_DIM
    hb = GQA_HEADS * hd
    rb = min(ROW_BLOCK, t)

    def body(q_ref, g_ref, gb_ref, cb, s1b, s2b, cqn_ref, qb_ref):
        xh, _ = _rms(q_ref[:, 0:ql])
        cqn_ref[...] = (xh * g_ref[...]).astype(BF16)
        for h in range(GQA_HEADS):
            nh, _ = _rms(q_ref[:, ql + h * hd:ql + (h + 1) * hd])
            qh = _rope(nh * gb_ref[...], cb[...], s1b[...], s2b[...], hd // 4)
            qb_ref[:, h * hd:(h + 1) * hd] = (qh * qscale).astype(BF16)

    row = lambda w: pl.BlockSpec((rb, w), lambda i: (i, 0))
    fix = lambda w: pl.BlockSpec((1, w), lambda i: (0, 0))
    return _pcall(
        body,
        name="q_prep_fwd",
        out_shape=[jax.ShapeDtypeStruct((t, ql), BF16), jax.ShapeDtypeStruct((t, hb), BF16)],
        grid=(t // rb,),
        in_specs=[row(ql + hb), fix(ql), fix(hd)] + [row(hd)] * 3,
        out_specs=[row(ql), row(hb)],
        compiler_params=_cparams(("parallel",)),
    )(qg, q_gain, qb_gain, *tabs)


def _q_prep_bwd(qg, q_gain, qb_gain, tabs, dcqn, dqb, wpad, qscale):
    t = qg.shape[0]
    ql = MLA_Q_LORA
    hd = GQA_HEAD_DIM
    hb = GQA_HEADS * hd
    rb = min(ROW_BLOCK, t)

    def body(q_ref, g_ref, gb_ref, cb, s1b, s2b, dcqn_ref, dqb_ref, dq_ref, st_ref, stb_ref):
        @pl.when(pl.program_id(0) == 0)
        def _():
            st_ref[...] = jnp.zeros_like(st_ref)
            stb_ref[...] = jnp.zeros_like(stb_ref)

        xh, r = _rms(q_ref[:, 0:ql])
        dn = dcqn_ref[...]
        st_ref[0:1, :] += _colsum(dn * xh)
        dq_ref[:, 0:ql] = _rms_bwd(dn * g_ref[...], xh, r).astype(BF16)
        for h in range(GQA_HEADS):
            nh, rh = _rms(q_ref[:, ql + h * hd:ql + (h + 1) * hd])
            dn_h = _rope_t(dqb_ref[:, h * hd:(h + 1) * hd] * qscale, cb[...], s1b[...], s2b[...], hd // 4)
            stb_ref[0:1, :] += _colsum(dn_h * nh)
            dq_ref[:, ql + h * hd:ql + (h + 1) * hd] = _rms_bwd(dn_h * gb_ref[...], nh, rh).astype(BF16)
        if wpad:
            dq_ref[:, ql + hb:ql + hb + wpad] = jnp.zeros((rb, wpad), BF16)

    row = lambda w: pl.BlockSpec((rb, w), lambda i: (i, 0))
    fix = lambda w: pl.BlockSpec((1, w), lambda i: (0, 0))
    return _pcall(
        body,
        name="q_prep_bwd",
        out_shape=[jax.ShapeDtypeStruct((t, ql + hb + wpad), BF16), jax.ShapeDtypeStruct((8, ql), F32),
                   jax.ShapeDtypeStruct((8, hd), F32)],
        grid=(t // rb,),
        in_specs=[row(ql + hb), fix(ql), fix(hd)] + [row(hd)] * 3 + [row(ql), row(hb)],
        out_specs=[row(ql + hb + wpad), pl.BlockSpec((8, ql), lambda i: (0, 0)), pl.BlockSpec((8, hd), lambda i: (0, 0))],
        compiler_params=_cparams(("arbitrary",)),
    )(qg, q_gain, qb_gain, *tabs, dcqn, dqb)


def _rope_a(v, tabs, transpose, out_dtype, name, qscale):
    t, w = v.shape
    rb = min(ROW_BLOCK, t)
    fn = _rope_t if transpose else _rope

    def body(v_ref, c, s1, s2, o_ref):
        for h in range(w // MLA_SLOT):
            sl = slice(h * MLA_SLOT, (h + 1) * MLA_SLOT)
            o_ref[:, sl] = (fn(v_ref[:, sl].astype(F32), c[...], s1[...], s2[...], MLA_ROPE // 4) * qscale).astype(out_dtype)

    row = lambda ww: pl.BlockSpec((rb, ww), lambda i: (i, 0))
    return _pcall(
        body,
        name=name,
        out_shape=jax.ShapeDtypeStruct((t, w), out_dtype),
        grid=(t // rb,),
        in_specs=[row(w)] + [row(MLA_SLOT)] * 3,
        out_specs=row(w),
        compiler_params=_cparams(("parallel",)),
    )(v, *tabs)


def _merge_fwd(pa, pb, qg, gate_blk):
    t, d = pa.shape
    rb = min(ROW_BLOCK, t)

    def body(pa_ref, pb_ref, ga_ref, gb_ref, o_ref):
        o_ref[...] = (jax.nn.sigmoid(ga_ref[...]) * pa_ref[...].astype(F32)
                      + jax.nn.sigmoid(gb_ref[...]) * pb_ref[...].astype(F32)).astype(BF16)

    row = pl.BlockSpec((rb, d), lambda i: (i, 0))
    return _pcall(
        body,
        name="merge_fwd",
        out_shape=jax.ShapeDtypeStruct((t, d), BF16),
        grid=(t // rb,),
        in_specs=[row, row, pl.BlockSpec((rb, d), lambda i: (i, gate_blk)), pl.BlockSpec((rb, d), lambda i: (i, gate_blk + 1))],
        out_specs=row,
        compiler_params=_cparams(("parallel",)),
    )(pa, pb, qg, qg)


def _merge_bwd(dm, pa, pb, qg, gate_blk):
    t, d = pa.shape
    rb = min(ROW_BLOCK, t)

    def body(dm_ref, pa_ref, pb_ref, ga_ref, gb_ref, dpa_ref, dpb_ref, dg_ref):
        dmv = dm_ref[...].astype(F32)
        sa = jax.nn.sigmoid(ga_ref[...])
        sb = jax.nn.sigmoid(gb_ref[...])
        dpa_ref[...] = (dmv * sa).astype(BF16)
        dpb_ref[...] = (dmv * sb).astype(BF16)
        dg_ref[:, 0:d] = (dmv * pa_ref[...].astype(F32) * (sa * (1.0 - sa))).astype(BF16)
        dg_ref[:, d:2 * d] = (dmv * pb_ref[...].astype(F32) * (sb * (1.0 - sb))).astype(BF16)

    row = pl.BlockSpec((rb, d), lambda i: (i, 0))
    return _pcall(
        body,
        name="merge_bwd",
        out_shape=[jax.ShapeDtypeStruct((t, d), BF16), jax.ShapeDtypeStruct((t, d), BF16),
                   jax.ShapeDtypeStruct((t, 2 * d), BF16)],
        grid=(t // rb,),
        in_specs=[row, row, row, pl.BlockSpec((rb, d), lambda i: (i, gate_blk)), pl.BlockSpec((rb, d), lambda i: (i, gate_blk + 1))],
        out_specs=[row, row, pl.BlockSpec((rb, 2 * d), lambda i: (i, 0))],
        compiler_params=_cparams(("parallel",)),
    )(dm, pa, pb, qg, qg)


def _resid_norm_mod(x, branch, gain, mods, name):
    t, d = x.shape
    rb = min(ROW_BLOCK, t)

    def body(x_ref, b_ref, g_ref, mod_ref, x1_ref, z_ref):
        x1 = x_ref[...] + mod_ref[0:1, :] * b_ref[...]
        x1_ref[...] = x1
        xh, _ = _rms(x1)
        z_ref[...] = ((xh * g_ref[...]) * (1.0 + mod_ref[2:3, :]) + mod_ref[1:2, :]).astype(BF16)

    row = pl.BlockSpec((rb, d), lambda i: (i, 0))
    return _pcall(
        body,
        name=name,
        out_shape=[jax.ShapeDtypeStruct((t, d), F32), jax.ShapeDtypeStruct((t, d), BF16)],
        grid=(t // rb,),
        in_specs=[row, row, pl.BlockSpec((1, d), lambda i: (0, 0)), pl.BlockSpec((8, d), lambda i: (0, 0))],
        out_specs=[row, row],
        compiler_params=_cparams(("parallel",)),
    )(x, branch, gain, mods)


def _norm2_bwd(x1, attn, gain, mods, dz2, dx2):
    t, d = x1.shape
    rb = min(ROW_BLOCK, t)

    def body(x1_ref, at_ref, g_ref, mod_ref, dz_ref, dx2_ref, dx1_ref, da_ref, st_ref):
        @pl.when(pl.program_id(0) == 0)
        def _():
            st_ref[...] = jnp.zeros_like(st_ref)

        xh, r = _rms(x1_ref[...])
        g = g_ref[...]
        dz = dz_ref[...].astype(F32)
        dxn = dz * (1.0 + mod_ref[1:2, :])
        st_ref[0:1, :] += _colsum(dz)
        st_ref[1:2, :] += _colsum(dz * (xh * g))
        st_ref[2:3, :] += _colsum(dxn * xh)
        dx1 = dx2_ref[...] + _rms_bwd(dxn * g, xh, r)
        dx1_ref[...] = dx1
        st_ref[3:4, :] += _colsum(dx1 * at_ref[...])
        da_ref[...] = (dx1 * mod_ref[0:1, :]).astype(BF16)

    row = pl.BlockSpec((rb, d), lambda i: (i, 0))
    return _pcall(
        body,
        name="norm2_mod_bwd",
        out_shape=[jax.ShapeDtypeStruct((t, d), F32), jax.ShapeDtypeStruct((t, d), BF16), jax.ShapeDtypeStruct((8, d), F32)],
        grid=(t // rb,),
        in_specs=[row, row, pl.BlockSpec((1, d), lambda i: (0, 0)), pl.BlockSpec((8, d), lambda i: (0, 0)), row, row],
        out_specs=[row, row, pl.BlockSpec((8, d), lambda i: (0, 0))],
        compiler_params=_cparams(("arbitrary",)),
    )(x1, attn, gain, mods, dz2, dx2)


def _final_loss(x1, ffn, gain, mods, target):
    t, d = x1.shape
    rb = min(ROW_BLOCK, t)
    nb = t // rb

    def body(x1_ref, f_ref, g_ref, mod_ref, tg_ref, dx2_ref, df_ref, st_ref):
        i = pl.program_id(0)

        @pl.when(i == 0)
        def _():
            st_ref[...] = jnp.zeros_like(st_ref)

        ffn_v = f_ref[...]
        g2 = mod_ref[0:1, :]
        x2 = x1_ref[...] + g2 * ffn_v
        xh, r = _rms(x2)
        g = g_ref[...]
        err = xh * g - tg_ref[...]
        st_ref[2:3, :] += _colsum(err * err) * (0.5 / d)
        dy = err * (1.0 / d)
        st_ref[0:1, :] += _colsum(dy * xh)
        dx2 = _rms_bwd(dy * g, xh, r)
        dx2_ref[...] = dx2
        st_ref[1:2, :] += _colsum(dx2 * ffn_v)
        df_ref[...] = (dx2 * g2).astype(BF16)

        @pl.when(i == nb - 1)
        def _():
            st_ref[3:4, :] = jnp.broadcast_to(jnp.sum(st_ref[2:3, :], axis=-1, keepdims=True), (1, d))

    row = pl.BlockSpec((rb, d), lambda i: (i, 0))
    return _pcall(
        body,
        name="final_norm_loss",
        out_shape=[jax.ShapeDtypeStruct((t, d), F32), jax.ShapeDtypeStruct((t, d), BF16), jax.ShapeDtypeStruct((8, d), F32)],
        grid=(nb,),
        in_specs=[row, row, pl.BlockSpec((1, d), lambda i: (0, 0)), pl.BlockSpec((8, d), lambda i: (0, 0)), row],
        out_specs=[row, row, pl.BlockSpec((8, d), lambda i: (0, 0))],
        compiler_params=_cparams(("arbitrary",)),
    )(x1, ffn, gain, mods, target)


def _row_ends(shape):
    rows = lax.broadcasted_iota(jnp.int32, shape, 0)
    return rows == 0, rows == shape[0] - 1


def _shift_dn(v, first):
    return jnp.where(first, 0.0, pltpu.roll(v, 1, 0))


def _shift_up(v, last):
    return jnp.where(last, 0.0, pltpu.roll(v, v.shape[0] - 1, 0))


def _conv_fwd(u, cw, cb):
    t, f2 = u.shape
    f = f2 // 2
    cbk = _tile(f, 256)
    nf = f // cbk

    def body(ua_ref, ub_ref, cwa_ref, cwb_ref, cba_ref, cbb_ref, h_ref, uc_ref):
        first, last = _row_ends((t, cbk))
        outs = []
        for u_ref, cw_ref, cb_ref in ((ua_ref, cwa_ref, cba_ref), (ub_ref, cwb_ref, cbb_ref)):
            uu, cwv = u_ref[...].astype(F32), cw_ref[...]
            outs.append(cb_ref[...] + cwv[0:1, :] * _shift_dn(uu, first) + cwv[1:2, :] * uu
                        + cwv[2:3, :] * _shift_up(uu, last))
        a, b = outs
        uc_ref[0] = a.astype(BF16)
        uc_ref[1] = b.astype(BF16)
        h_ref[...] = (a * jax.nn.sigmoid(a) * b).astype(BF16)

    ca = lambda r: pl.BlockSpec((r, cbk), lambda j: (0, j))
    cbs = lambda r: pl.BlockSpec((r, cbk), lambda j: (0, nf + j))
    return _pcall(
        body,
        name="conv_gate_fwd",
        out_shape=[jax.ShapeDtypeStruct((t, f), BF16), jax.ShapeDtypeStruct((2, t, f), BF16)],
        grid=(nf,),
        in_specs=[ca(t), cbs(t), ca(3), cbs(3), ca(1), cbs(1)],
        out_specs=[ca(t), pl.BlockSpec((2, t, cbk), lambda j: (0, 0, j))],
        compiler_params=_cparams(("parallel",)),
    )(u, u, cw, cw, cb, cb)


def _conv_bwd(u, uc, cw, dh):
    t, f2 = u.shape
    f = f2 // 2
    cbk = _tile(f, 256)
    nf = f // cbk

    def body(ua_ref, ub_ref, uc_ref, cwa_ref, cwb_ref, dh_ref, du_ref, dcw_ref, dcb_ref):
        first, last = _row_ends((t, cbk))
        a, b = uc_ref[0].astype(F32), uc_ref[1].astype(F32)
        dh_v = dh_ref[...].astype(F32)
        sg = jax.nn.sigmoid(a)
        db = dh_v * (a * sg)
        da = dh_v * b * (sg * (1.0 + a * (1.0 - sg)))
        for idx, (dv, u_ref, cw_ref) in enumerate(((da, ua_ref, cwa_ref), (db, ub_ref, cwb_ref))):
            uu, cwv = u_ref[...].astype(F32), cw_ref[...]
            up, dn = _shift_up(dv, last), _shift_dn(dv, first)
            dcb_ref[idx] = _colsum(dv)
            dcw_ref[idx, 0:1, :] = _colsum(up * uu)
            dcw_ref[idx, 1:2, :] = _colsum(dv * uu)
            dcw_ref[idx, 2:3, :] = _colsum(dn * uu)
            du_ref[idx] = (cwv[0:1, :] * up + cwv[1:2, :] * dv + cwv[2:3, :] * dn).astype(BF16)

    ca = lambda r: pl.BlockSpec((r, cbk), lambda j: (0, j))
    cbs = lambda r: pl.BlockSpec((r, cbk), lambda j: (0, nf + j))
    o3 = lambda r: pl.BlockSpec((2, r, cbk), lambda j: (0, 0, j))
    return _pcall(
        body,
        name="conv_gate_bwd",
        out_shape=[jax.ShapeDtypeStruct((2, t, f), BF16), jax.ShapeDtypeStruct((2, 3, f), F32),
                   jax.ShapeDtypeStruct((2, 1, f), F32)],
        grid=(nf,),
        in_specs=[ca(t), cbs(t), o3(t), ca(3), cbs(3), ca(t)],
        out_specs=[o3(t), o3(3), o3(1)],
        compiler_params=_cparams(("parallel",)),
    )(u, u, uc, cw, cw, dh)


def _attention_fwd(q, kk, vv, *, hq, hkv, dk, dv, k_blk0, v_blk0, name):
    t = q.shape[0]
    tk = kk.shape[0]
    g_sz = hq // hkv
    tq = min(ATT_Q_BLOCK_FWD, t)

    def body(q_ref, k_ref, v_ref, o_ref, lse_ref):
        k = k_ref[...]
        v = v_ref[...]
        for j in range(g_sz):
            s = lax.dot_general(q_ref[:, j * dk:(j + 1) * dk], k, _DIMS["nt"], preferred_element_type=F32)
            m = jnp.max(s, axis=-1, keepdims=True)
            p = jnp.exp2(s - m)
            l = jnp.sum(p, axis=-1, keepdims=True)
            o = jnp.dot(p.astype(BF16), v, preferred_element_type=F32) / l
            o_ref[:, j * dv:(j + 1) * dv] = o.astype(BF16)
            lse_ref[0, :, j:j + 1] = m + jnp.log2(l)

    return _pcall(
        body,
        name=name,
        out_shape=[jax.ShapeDtypeStruct((t, hq * dv), BF16), jax.ShapeDtypeStruct((hkv, t, g_sz), F32)],
        grid=(hkv, t // tq),
        in_specs=[
            pl.BlockSpec((tq, g_sz * dk), lambda g, i: (i, g)),
            pl.BlockSpec((tk, dk), lambda g, i: (0, k_blk0 + g)),
            pl.BlockSpec((tk, dv), lambda g, i: (0, v_blk0 + g)),
        ],
        out_specs=[
            pl.BlockSpec((tq, g_sz * dv), lambda g, i: (i, g)),
            pl.BlockSpec((1, tq, g_sz), lambda g, i: (g, i, 0)),
        ],
        compiler_params=_cparams(("parallel", "parallel")),
    )(q, kk, vv)


def _attention_bwd(q, kk, vv, do, lse, *, hq, hkv, dk, dv, k_blk0, v_blk0, name):
    t = q.shape[0]
    tk = kk.shape[0]
    g_sz = hq // hkv
    tq = min(ATT_Q_BLOCK, t)

    def body(q_ref, k_ref, v_ref, do_ref, lse_ref, dq_ref, dk_ref, dv_ref):
        @pl.when(pl.program_id(1) == 0)
        def _():
            dk_ref[...] = jnp.zeros_like(dk_ref)
            dv_ref[...] = jnp.zeros_like(dv_ref)

        k = k_ref[...]
        v = v_ref[...]
        for j in range(g_sz):
            qj = q_ref[:, j * dk:(j + 1) * dk]
            doj = do_ref[:, j * dv:(j + 1) * dv]
            s = lax.dot_general(qj, k, _DIMS["nt"], preferred_element_type=F32)
            p = jnp.exp2(s - lse_ref[0, :, j:j + 1])
            dp = lax.dot_general(doj, v, _DIMS["nt"], preferred_element_type=F32)
            ds = (p * (dp - jnp.sum(p * dp, axis=-1, keepdims=True))).astype(BF16)
            dv_ref[...] += lax.dot_general(p.astype(BF16), doj, _DIMS["tn"], preferred_element_type=F32)
            dk_ref[...] += lax.dot_general(ds, qj, _DIMS["tn"], preferred_element_type=F32)
            dq_ref[:, j * dk:(j + 1) * dk] = jnp.dot(ds, k, preferred_element_type=F32)

        @pl.when(pl.program_id(1) == t // tq - 1)
        def _():
            dk_ref[...] *= LN2

    return _pcall(
        body,
        name=name,
        out_shape=[jax.ShapeDtypeStruct((t, hq * dk), F32), jax.ShapeDtypeStruct((tk, hkv * dk), F32),
                   jax.ShapeDtypeStruct((tk, hkv * dv), F32)],
        grid=(hkv, t // tq),
        in_specs=[
            pl.BlockSpec((tq, g_sz * dk), lambda g, i: (i, g)),
            pl.BlockSpec((tk, dk), lambda g, i: (0, k_blk0 + g)),
            pl.BlockSpec((tk, dv), lambda g, i: (0, v_blk0 + g)),
            pl.BlockSpec((tq, g_sz * dv), lambda g, i: (i, g)),
            pl.BlockSpec((1, tq, g_sz), lambda g, i: (g, i, 0)),
        ],
        out_specs=[
            pl.BlockSpec((tq, g_sz * dk), lambda g, i: (i, g)),
            pl.BlockSpec((tk, dk), lambda g, i: (0, g)),
            pl.BlockSpec((tk, dv), lambda g, i: (0, g)),
        ],
        compiler_params=_cparams(("parallel", "arbitrary")),
    )(q, kk, vv, do, lse)


def _silu(v):
    return v * jax.nn.sigmoid(v)


def _ada_fwd(conds, w_ada, b_ada_shard):
    r, d = conds.shape
    n = w_ada.shape[1]
    tn = _tile(n, 512)

    def body(c_ref, w_ref, b_ref, o_ref):
        s = _silu(c_ref[...]).astype(BF16)
        o_ref[...] = jnp.dot(s, w_ref[...].astype(BF16), preferred_element_type=F32) + b_ref[...]

    return _pcall(
        body,
        name="ada_fwd",
        out_shape=jax.ShapeDtypeStruct((r, n), F32),
        grid=(n // tn,),
        in_specs=[pl.BlockSpec((r, d), lambda j: (0, 0)), pl.BlockSpec((d, tn), lambda j: (0, j)),
                  pl.BlockSpec((1, tn), lambda j: (0, j))],
        out_specs=pl.BlockSpec((r, tn), lambda j: (0, j)),
        compiler_params=_cparams(("parallel",)),
    )(conds, w_ada, b_ada_shard)


def _cctx_partial(da16_shard, w_ada, c_ctx_row):
    d, n = w_ada.shape
    td = _tile(d, 512)

    def body(g_ref, w_ref, c_ref, o_ref):
        ds = lax.dot_general(g_ref[8:16, :].astype(BF16), w_ref[...].astype(BF16), _DIMS["nt"],
                             preferred_element_type=F32)
        cv = c_ref[...]
        sg = jax.nn.sigmoid(cv)
        o_ref[...] = ds * (sg * (1.0 + cv * (1.0 - sg)))

    return _pcall(
        body,
        name="cctx_partial",
        out_shape=jax.ShapeDtypeStruct((8, d), F32),
        grid=(d // td,),
        in_specs=[pl.BlockSpec((16, n), lambda j: (0, 0)), pl.BlockSpec((td, n), lambda j: (j, 0)),
                  pl.BlockSpec((1, td), lambda j: (0, j))],
        out_specs=pl.BlockSpec((8, td), lambda j: (0, j)),
        compiler_params=_cparams(("parallel",)),
    )(da16_shard, w_ada, c_ctx_row)


def _sum_parts(parts):
    p, _, n = parts.shape

    def body(p_ref, o_ref):
        acc = p_ref[0]
        for s in range(1, p):
            acc = acc + p_ref[s]
        o_ref[...] = acc

    return _pcall(
        body,
        name="sum_parts",
        out_shape=jax.ShapeDtypeStruct((1, n), F32),
        in_specs=[pl.BlockSpec(memory_space=pltpu.VMEM)],
        out_specs=pl.BlockSpec(memory_space=pltpu.VMEM),
    )(parts)


def _adam_math(w, g, m, v):
    m2 = ADAM_B1 * m + (1.0 - ADAM_B1) * g
    v2 = ADAM_B2 * v + (1.0 - ADAM_B2) * jnp.square(g)
    m_hat = m2 / (1.0 - ADAM_B1 ** ADAM_STEP)
    v_hat = v2 / (1.0 - ADAM_B2 ** ADAM_STEP)
    delta = -ADAM_LR * (m_hat / (jnp.sqrt(v_hat) + ADAM_EPS) + ADAM_WD * w)
    return delta, m2, v2


def _adamw(parts, w, m, v, name):
    p, r, c = parts.shape
    block_elems = 1 << 18
    rb, cb = _tile(r, max(8, block_elems // c // 8 * 8), 8), c
    if rb * c < block_elems // 4 and r * c > block_elems:
        rb, cb = r, _tile(c, max(LANE, block_elems // r // LANE * LANE))

    def body(p_ref, w_ref, m_ref, v_ref, g_ref, d_ref, m2_ref, v2_ref):
        g = p_ref[0].astype(F32)
        for s in range(1, p):
            g = g + p_ref[s].astype(F32)
        g_ref[...] = g
        d_ref[...], m2_ref[...], v2_ref[...] = _adam_math(w_ref[...], g, m_ref[...], v_ref[...])

    if w.ndim == 3:
        blk = pl.BlockSpec((None, rb, cb), lambda i, j: (0, i, j))
    else:
        blk = pl.BlockSpec((rb, cb), lambda i, j: (i, j))
    return _pcall(
        body,
        name=name,
        out_shape=[jax.ShapeDtypeStruct(w.shape, F32)] * 4,
        grid=(r // rb, c // cb),
        in_specs=[pl.BlockSpec((p, rb, cb), lambda i, j: (0, i, j)), blk, blk, blk],
        out_specs=[blk] * 4,
        compiler_params=_cparams(("parallel", "parallel")),
    )(parts, w, m, v)


def _adamw_ada(conds, da16, w, m, v):
    d, n = w.shape
    rb = _tile(d, 256, LANE)

    def body(s_ref, da_ref, w_ref, m_ref, v_ref, g_ref, d_ref, m2_ref, v2_ref):
        g = lax.dot_general(_silu(s_ref[...]).astype(BF16), da_ref[...].astype(BF16), _DIMS["tn"],
                            preferred_element_type=F32)
        g_ref[...] = g
        d_ref[...], m2_ref[...], v2_ref[...] = _adam_math(w_ref[...], g, m_ref[...], v_ref[...])

    row = pl.BlockSpec((rb, n), lambda i: (i, 0))
    return _pcall(
        body,
        name="adamw_w_ada",
        out_shape=[jax.ShapeDtypeStruct((d, n), F32)] * 4,
        grid=(d // rb,),
        in_specs=[pl.BlockSpec((16, rb), lambda i: (0, i)), pl.BlockSpec((16, n), lambda i: (0, 0)), row, row, row],
        out_specs=[row] * 4,
        compiler_params=_cparams(("parallel",)),
    )(conds, da16, w, m, v)


def _cast_bf16(a, name):
    _, r, c = a.shape
    rb, cb = _tile(r, 512, 8), c
    if rb < 64 < r:
        rb, cb = r, _tile(c, 512)

    def body(a_ref, o_ref):
        o_ref[...] = a_ref[...].astype(BF16)

    return _pcall(body, name=name, out_shape=jax.ShapeDtypeStruct((r, c), BF16), grid=(r // rb, c // cb),
                  in_specs=[pl.BlockSpec((None, rb, cb), lambda i, j: (0, i, j))],
                  out_specs=pl.BlockSpec((rb, cb), lambda i, j: (i, j)),
                  compiler_params=_cparams(("parallel", "parallel")))(a)


def _rope_tabs(t, rot):
    half, q = rot // 2, rot // 4
    n_rows = t // GRID_W
    row = jnp.repeat(jnp.arange(n_rows, dtype=F32), GRID_W)
    col = jnp.tile(jnp.arange(GRID_W, dtype=F32), n_rows)
    inv_freq = ROPE_THETA ** (-jnp.arange(0, half, 2, dtype=F32) / half)
    ang = jnp.concatenate([row[:, None] * inv_freq, col[:, None] * inv_freq], axis=-1)
    cos, sin = jnp.cos(ang), jnp.sin(ang)
    c0, c1, s0, s1 = cos[:, :q], cos[:, q:], sin[:, :q], sin[:, q:]
    z = jnp.zeros_like(s0)
    return (jnp.concatenate([c0, c0, c1, c1], -1), jnp.concatenate([-s0, z, -s1, z], -1),
            jnp.concatenate([z, s0, z, s1], -1))


def _pad_cols(a, left, total, fill=0.0):
    return jnp.pad(a, ((0, 0), (left, total - left - a.shape[1])), constant_values=fill)


def _with_ctx_rows(tab, tc, fill):
    return jnp.concatenate([tab, jnp.full((tc, tab.shape[1]), fill, F32)], axis=0)


def kernel(x, c, ctx, c_ctx, w_ada, b_ada, norm1_g, w_in, mla_q_norm_g, w_q_up, mla_kv_norm_g, w_kv_up, gqa_q_norm_g, gqa_k_norm_g, w_br_a, w_br_b, w_out, norm2_g, w_up, conv_w, conv_b, w_down, final_norm_g, loss_target, m_c_ctx, m_w_ada, m_b_ada, m_norm1_g, m_w_in, m_mla_q_norm_g, m_w_q_up, m_mla_kv_norm_g, m_w_kv_up, m_gqa_q_norm_g, m_gqa_k_norm_g, m_w_br_a, m_w_br_b, m_w_out, m_norm2_g, m_w_up, m_conv_w, m_conv_b, m_w_down, m_final_norm_g, v_c_ctx, v_w_ada, v_b_ada, v_norm1_g, v_w_in, v_mla_q_norm_g, v_w_q_up, v_mla_kv_norm_g, v_w_kv_up, v_gqa_q_norm_g, v_gqa_k_norm_g, v_w_br_a, v_w_br_b, v_w_out, v_norm2_g, v_w_up, v_conv_w, v_conv_b, v_w_down, v_final_norm_g):
    weights = dict(c_ctx=c_ctx, w_ada=w_ada, b_ada=b_ada, norm1_g=norm1_g, w_in=w_in, mla_q_norm_g=mla_q_norm_g,
                   w_q_up=w_q_up, mla_kv_norm_g=mla_kv_norm_g, w_kv_up=w_kv_up, gqa_q_norm_g=gqa_q_norm_g,
                   gqa_k_norm_g=gqa_k_norm_g, w_br_a=w_br_a, w_br_b=w_br_b, w_out=w_out, norm2_g=norm2_g, w_up=w_up,
                   conv_w=conv_w, conv_b=conv_b, w_down=w_down, final_norm_g=final_norm_g)
    mom_m = dict(c_ctx=m_c_ctx, w_ada=m_w_ada, b_ada=m_b_ada, norm1_g=m_norm1_g, w_in=m_w_in, mla_q_norm_g=m_mla_q_norm_g,
                 w_q_up=m_w_q_up, mla_kv_norm_g=m_mla_kv_norm_g, w_kv_up=m_w_kv_up, gqa_q_norm_g=m_gqa_q_norm_g,
                 gqa_k_norm_g=m_gqa_k_norm_g, w_br_a=m_w_br_a, w_br_b=m_w_br_b, w_out=m_w_out, norm2_g=m_norm2_g,
                 w_up=m_w_up, conv_w=m_conv_w, conv_b=m_conv_b, w_down=m_w_down, final_norm_g=m_final_norm_g)
    mom_v = dict(c_ctx=v_c_ctx, w_ada=v_w_ada, b_ada=v_b_ada, norm1_g=v_norm1_g, w_in=v_w_in, mla_q_norm_g=v_mla_q_norm_g,
                 w_q_up=v_w_q_up, mla_kv_norm_g=v_mla_kv_norm_g, w_kv_up=v_w_kv_up, gqa_q_norm_g=v_gqa_q_norm_g,
                 gqa_k_norm_g=v_gqa_k_norm_g, w_br_a=v_w_br_a, w_br_b=v_w_br_b, w_out=v_w_out, norm2_g=v_norm2_g,
                 w_up=v_w_up, conv_w=v_conv_w, conv_b=v_conv_b, w_down=v_w_down, final_norm_g=v_final_norm_g)
    order = list(weights)

    my_idx = 4 * lax.axis_index("x") + 2 * lax.axis_index("y") + lax.axis_index("c")
    xs, cts, tgt = x[0], ctx[0], loss_target[0]
    t, d = xs.shape
    tc = cts.shape[0]
    ta = t + tc
    kvl, ql = MLA_KV_LORA, MLA_Q_LORA
    nb = GQA_KV_HEADS * GQA_HEAD_DIM
    hb = GQA_HEADS * GQA_HEAD_DIM
    ha = MLA_HEADS
    f2 = w_up.shape[2] * N_DEV
    ff = f2 // 2

    big = ["w_in", "w_q_up", "w_kv_up", "w_br_a", "w_br_b", "w_out", "w_up", "w_down"]
    nw = len(big)
    del nw
    _ORDER_AFTER.clear()
    narrow = ("w_in", "w_q_up")

    def tview(a):
        return jnp.transpose(a, (0, 2, 1))

    shards = {"w_in": _cast_bf16(tview(weights["w_in"]), "cast_w_in")}
    c_idx = jnp.reshape(lax.axis_index("c"), (1,)).astype(jnp.int32)

    def gather_start(names, dep):
        shs = [shards[n] for n in names]
        land = [lax.empty((N_DEV,) + s.shape, BF16) for s in shs]
        if dep is not None:
            _after(dep)
        s, r, arrs, tok = _split_start("gather_ici_start_" + names[0], shs + land, _gather_ici_copies(len(names)),
                                       4 * len(names))
        return dict(names=names, s=s, r=r, arrs=arrs, tok=tok)

    def gather_pass(g, after):
        n = len(g["names"])
        arrs = _split_wait("gather_ici_wait_" + g["names"][0], g["s"], g["r"], g["arrs"], _gather_ici_copies(n), after)
        s, r, bufs, tok = _split_start("gather_pass_start_" + g["names"][0], arrs[n:], _gather_pass_copies(n), 3 * n)
        g.update(s2=s, r2=r, bufs=bufs)
        return tok

    def gather_relay(g, after):
        n = len(g["names"])
        bufs = _split_wait("gather_pass_wait_" + g["names"][0], g["s2"], g["r2"], g["bufs"], _gather_pass_copies(n), after)
        s, r, bufs, tok = _split_start("gather_d2d_start_" + g["names"][0], bufs, _gather_d2d_copies(n), n)
        g.update(s3=s, r3=r, bufs=bufs)
        return tok

    def gather_finish(g, after):
        n = len(g["names"])
        bufs = _split_wait("gather_d2d_wait_" + g["names"][0], g["s3"], g["r3"], g["bufs"], _gather_d2d_copies(n), after)
        return dict(zip(g["names"], bufs))

    c_all, cw_all = _all_gather([jnp.pad(c, ((0, 7), (0, 0))), jnp.pad(conv_w[0], ((0, 5), (0, 0)))], "gather_cond")
    conv_w_f = jnp.transpose(cw_all[:, :3, :], (1, 0, 2)).reshape(3, f2)
    conds = jnp.concatenate([c_all[:, 0, :], c_ctx[None, :], jnp.zeros((7, d), F32)], axis=0)
    ncol = w_ada.shape[2]
    b_shard = lax.dynamic_slice_in_dim(b_ada, my_idx * ncol, ncol, axis=1)
    ada_shard = _ada_fwd(conds, w_ada[0], b_shard)
    (ada_all,) = _all_gather([ada_shard], "gather_ada")
    ada = jnp.transpose(ada_all, (1, 0, 2)).reshape(16, N_DEV * ncol)
    lat = lax.dynamic_slice_in_dim(ada, my_idx, 1, axis=0).reshape(6, d)
    cxt = ada[8].reshape(6, d)
    zero2 = jnp.zeros((2, d), F32)
    mods1 = jnp.concatenate([lat[0:2], cxt[0:2], jnp.zeros((4, d), F32)], axis=0)
    mods2 = jnp.concatenate([lat[2:3], lat[3:4], lat[4:5], jnp.zeros((5, d), F32)], axis=0)
    mods2b = jnp.concatenate([lat[2:3], lat[4:5], jnp.zeros((6, d), F32)], axis=0)
    mods3 = jnp.concatenate([lat[5:6], jnp.zeros((7, d), F32)], axis=0)
    del zero2

    g0 = gather_start(["w_in"], ada_all)
    for n in big[1:]:
        _after(g0["tok"])
        shards[n] = _cast_bf16(tview(weights[n]) if n in narrow else weights[n], "cast_" + n)

    ca, s1a, s2a = _rope_tabs(t, MLA_ROPE)
    cb_, s1b, s2b = _rope_tabs(t, GQA_HEAD_DIM)
    q_tabs_a = (_pad_cols(jnp.concatenate([jnp.ones((t, MLA_NOPE), F32), ca], 1), 0, MLA_SLOT),
                _pad_cols(s1a, MLA_NOPE, MLA_SLOT), _pad_cols(s2a, MLA_NOPE, MLA_SLOT))
    q_tabs_b = (cb_, s1b, s2b)
    k_tabs = (_with_ctx_rows(_pad_cols(ca, 0, LANE), tc, 1.0), _with_ctx_rows(_pad_cols(s1a, 0, LANE), tc, 0.0),
              _with_ctx_rows(_pad_cols(s2a, 0, LANE), tc, 0.0),
              _with_ctx_rows(cb_, tc, 1.0), _with_ctx_rows(s1b, tc, 0.0), _with_ctx_rows(s2b, tc, 0.0))

    def cols_full(g):
        return jnp.transpose(g, (1, 0, 2)).reshape(g.shape[1], N_DEV * g.shape[2])

    _after(*q_tabs_a, *q_tabs_b, *k_tabs, *[shards[n] for n in big[1:]])
    tok_p0 = gather_pass(g0, mods1)
    g1 = gather_start(["w_q_up", "w_kv_up", "w_br_a", "w_br_b", "w_out"], tok_p0)
    _after(g1["tok"])
    z_all = _norm_mod_fwd(cts, xs, norm1_g, mods1)
    gathered = gather_finish(g0, gather_relay(g0, z_all))
    wt_in = gathered["w_in"].reshape(-1, d)
    o_kpe, o_kb, o_vb = kvl, kvl + MLA_ROPE, kvl + MLA_ROPE + nb
    o_q = o_vb + nb
    o_g = o_q + ql + hb
    wkv_w = kvl + 2 * nb + LANE
    wt_kv_p = jnp.concatenate([wt_in[:kvl], wt_in[o_kb:o_q], wt_in[o_kpe:o_kb],
                               jnp.zeros((LANE - MLA_ROPE, d), BF16)], axis=0)
    q_w = ql + hb
    q_pad = (-q_w) % 512 if d >= 512 else (-q_w) % d
    gate_blk = (q_w + q_pad) // d
    assert (q_w + q_pad) % d == 0
    wt_qg_p = jnp.concatenate([wt_in[o_q:o_g], jnp.zeros((q_pad, d), BF16), wt_in[o_g:]], axis=0)

    kv_all = _mm(z_all, wt_kv_p, "nt", F32, "proj_kv", tm=1152, tn=wkv_w)
    qg = _mm(z_all, wt_qg_p, "nt", F32, "proj_qg", tm=1024, tn=1024, rows=t)
    tok_p1 = gather_pass(g1, qg)
    g2 = gather_start(["w_up"], tok_p1)
    g3 = gather_start(["w_down"], g2["tok"])
    _after(g3["tok"])
    kin, k_b, v_b = _key_prep_fwd(kv_all, mla_kv_norm_g, gqa_k_norm_g, k_tabs)
    sc_a = float((MLA_NOPE + MLA_ROPE) ** -0.5) * LOG2E
    sc_b = float(GQA_HEAD_DIM ** -0.5) * LOG2E
    _after(g3["tok"])
    cqn, q_b = _q_prep_fwd(qg, mla_q_norm_g, gqa_q_norm_g, q_tabs_b, sc_b)
    _after(kin, g3["tok"])
    gathered.update(gather_finish(g1, gather_relay(g1, q_b)))

    wqt_f = gathered["w_q_up"].reshape(ha, MLA_NOPE + MLA_ROPE, ql)
    wqt_ext = jnp.pad(wqt_f, ((0, 0), (0, MLA_SLOT - MLA_NOPE - MLA_ROPE), (0, 0))).reshape(ha * MLA_SLOT, ql)
    wkv_f = cols_full(gathered["w_kv_up"]).reshape(kvl, ha, MLA_NOPE + MLA_V)
    wk_slots = jnp.pad(wkv_f[:, :, :MLA_NOPE], ((0, 0), (0, 0), (0, MLA_SLOT - MLA_NOPE))).reshape(kvl, ha * MLA_SLOT)
    wv_cols = wkv_f[:, :, MLA_NOPE:].reshape(kvl, ha * MLA_V)
    e_slot = jnp.pad(jnp.eye(MLA_ROPE, dtype=BF16),
                     ((0, LANE - MLA_ROPE), (MLA_NOPE, MLA_SLOT - MLA_NOPE - MLA_ROPE)))
    e_rows = jnp.concatenate([jnp.tile(e_slot, (1, ha)), jnp.zeros((LANE, ha * MLA_V), BF16)], axis=1)
    wkv_ext = jnp.concatenate([jnp.concatenate([wk_slots, wv_cols], axis=1), e_rows], axis=0)
    w_bra = cols_full(gathered["w_br_a"])
    w_brb = cols_full(gathered["w_br_b"])
    w_out_f = gathered["w_out"].reshape(d, d)

    kv_a = _mm(kin, wkv_ext, "nn", BF16, "kv_up", tm=1152, tn=1024)
    qa_raw = _mm(cqn, wqt_ext, "nt", F32, "q_up", tm=1024, tn=1024)
    q_a = _rope_a(qa_raw, q_tabs_a, False, BF16, "rope_q_fwd", sc_a)
    att_a = dict(hq=ha, hkv=ha, dk=MLA_SLOT, dv=MLA_V, k_blk0=0, v_blk0=ha * MLA_SLOT // MLA_V)
    att_b = dict(hq=GQA_HEADS, hkv=GQA_KV_HEADS, dk=GQA_HEAD_DIM, dv=GQA_HEAD_DIM, k_blk0=0, v_blk0=0)
    o_a, lse_a = _attention_fwd(q_a, kv_a, kv_a, name="attn_a_fwd", **att_a)
    o_b, lse_b = _attention_fwd(q_b, k_b, v_b, name="attn_b_fwd", **att_b)
    _after(o_a)
    _after(gather_pass(g2, o_b))
    pa = _mm(o_a, w_bra, "nn", BF16, "br_a", tm=1024, tn=1024)
    pb = _mm(o_b, w_brb, "nn", BF16, "br_b", tm=1024, tn=1024)
    merged = _merge_fwd(pa, pb, qg, gate_blk)
    attn = _mm(merged, w_out_f, "nn", F32, "w_out", tm=1024, tn=1024)
    x1, z2 = _resid_norm_mod(xs, attn, norm2_g, mods2, "resid_norm2_fwd")
    tok_r2 = gather_relay(g2, z2)
    tok_p3 = gather_pass(g3, tok_r2)
    w_up3 = gather_finish(g2, tok_p3)["w_up"]
    u = _mm_up_fwd(z2, w_up3, "w_up")
    tok_r3 = gather_relay(g3, u)
    _after(tok_r3)
    h, uc = _conv_fwd(u, conv_w_f, conv_b)
    w_down_f = gather_finish(g3, h)["w_down"].reshape(ff, d)
    ffn = _mm(h, w_down_f, "nn", F32, "w_down", tm=1024, tn=1024, tk=2816)

    def to_shards(g):
        return jnp.transpose(g.reshape(g.shape[0], N_DEV, g.shape[1] // N_DEV), (1, 0, 2))

    def reduce_start(tag, names, sends):
        n = len(sends)
        land = [lax.empty((4,) + s.shape[1:], s.dtype) for s in sends]
        s, r, arrs, tok = _split_start("reduce_d2d_start_" + tag, sends + land, _reduce_d2d_copies(n), 4 * n)
        return dict(tag=tag, names=names, s=s, r=r, arrs=arrs, tok=tok)

    def reduce_relay(g, after):
        n = len(g["names"])
        arrs = _split_wait("reduce_d2d_wait_" + g["tag"], g["s"], g["r"], g["arrs"], _reduce_d2d_copies(n), after)
        sums = [_pair_sum(arrs[a], arrs[n + a], c_idx, "pair_sum_" + g["names"][a]) for a in range(n)]
        land = [lax.empty(s.shape, s.dtype) for s in sums]
        s, r, arrs2, tok = _split_start("reduce_ici_start_" + g["tag"], sums + land, _reduce_ici_copies(n), 4 * n)
        g.update(s2=s, r2=r, arrs2=arrs2)
        return tok

    def reduce_finish(g, after):
        n = len(g["names"])
        arrs2 = _split_wait("reduce_ici_wait_" + g["tag"], g["s2"], g["r2"], g["arrs2"], _reduce_ici_copies(n), after)
        return dict(zip(g["names"], arrs2[n:]))

    dx2, dffn, st_fin = _final_loss(x1, ffn, final_norm_g[None, :], mods3, tgt)
    dh = _mm(dffn, w_down_f, "nt", BF16, "d_h", tm=1024, tn=1024)
    g_w_down = _mm(h, dffn, "tn", BF16, "g_w_down", tm=512, tn=1024)
    r_down = reduce_start("down", ["w_down"], [g_w_down.reshape(N_DEV, ff // N_DEV, d)])
    _after(r_down["tok"])
    du3, dcw, dcb = _conv_bwd(u, uc, conv_w_f, dh)
    dz2 = _mm_up_dz(du3, w_up3, "d_z2")
    g_w_up = _mm_up_gw(z2, du3, N_DEV, "g_w_up")
    g_conv_w = jnp.concatenate([dcw[0], dcw[1]], axis=1)
    tok = reduce_relay(r_down, g_w_up)
    _after(tok)
    r_up = reduce_start("up", ["w_up", "conv_w"], [g_w_up, to_shards(jnp.pad(g_conv_w, ((0, 5), (0, 0))))])
    _after(tok, r_up["tok"])
    dx1, dattn, st_n2 = _norm2_bwd(x1, attn, norm2_g, mods2b, dz2, dx2)
    dmerged = _mm(dattn, w_out_f, "nt", BF16, "d_merged", tm=1024, tn=1024)
    g_w_out = _mm(merged, dattn, "tn", BF16, "g_w_out", tm=1024, tn=1024)
    dpa, dpb, dgates = _merge_bwd(dmerged, pa, pb, qg, gate_blk)
    do_a = _mm(dpa, w_bra, "nt", BF16, "d_o_a", tm=1024, tn=1024)
    do_b = _mm(dpb, w_brb, "nt", BF16, "d_o_b", tm=1024, tn=1024)
    g_w_bra = _mm(o_a, dpa, "tn", BF16, "g_w_br_a", tm=1024, tn=1024)
    g_w_brb = _mm(o_b, dpb, "tn", BF16, "g_w_br_b", tm=1024, tn=1024)
    tok = reduce_relay(r_up, g_w_brb)
    _after(tok)
    r_out = reduce_start("out", ["w_out", "w_br_a", "w_br_b"],
                         [g_w_out.reshape(N_DEV, d // N_DEV, d), to_shards(g_w_bra), to_shards(g_w_brb)])
    _after(tok, r_out["tok"])
    dq_a, dk_a, dv_a = _attention_bwd(q_a, kv_a, kv_a, do_a, lse_a, name="attn_a_bwd", **att_a)
    dq_b, dk_b, dv_b = _attention_bwd(q_b, k_b, v_b, do_b, lse_b, name="attn_b_bwd", **att_b)
    _after(reduce_relay(r_out, dv_b))
    dqa_raw = _rope_a(dq_a, q_tabs_a, True, BF16, "rope_q_bwd", sc_a * LN2)
    dcqn = _mm(dqa_raw, wqt_ext, "nn", F32, "d_cqn", tm=1024, tn=ql)
    g_wqt_ext = _mm(dqa_raw, cqn, "tn", BF16, "g_w_q_up", tm=1024, tn=ql)
    dq_p, st_q, st_qb = _q_prep_bwd(qg, mla_q_norm_g, gqa_q_norm_g, q_tabs_b, dcqn, dq_b, q_pad, sc_b * LN2)
    dkin = _mm_cat_nt([(dk_a, wkv_ext, 0), (dv_a, wkv_ext, ha * MLA_SLOT)], F32, "d_kin", tm=1152, tn=kvl + LANE)
    g_wkv_ext = _mm_cat_tn(kin, [dk_a, dv_a], BF16, "g_w_kv_up", tm=kvl + LANE, tn=min(1024, ha * MLA_V))
    dkv_p, st_kv, st_kb = _key_prep_bwd(kv_all, mla_kv_norm_g, gqa_k_norm_g, k_tabs, dkin, dk_b, dv_b)
    g_wqt = g_wqt_ext.reshape(ha, MLA_SLOT, ql)[:, :MLA_NOPE + MLA_ROPE, :].reshape(N_DEV, -1, ql)
    g_wkv = jnp.concatenate([g_wkv_ext[:kvl, :ha * MLA_SLOT].reshape(kvl, ha, MLA_SLOT)[:, :, :MLA_NOPE],
                             g_wkv_ext[:kvl, ha * MLA_SLOT:].reshape(kvl, ha, MLA_V)], axis=2).reshape(kvl, ha * (MLA_NOPE + MLA_V))
    r_qkv = reduce_start("qkv", ["w_q_up", "w_kv_up"], [g_wqt, to_shards(g_wkv)])
    _after(r_qkv["tok"])
    g_wkv_p = _mm(dkv_p, z_all, "tn", BF16, "g_w_in_kv", tm=wkv_w, tn=1024)
    g_wqg_p = _mm_rows_tn([dq_p, dgates], z_all, BF16, "g_w_in_qg", tm=min(1024, d), tn=1024, rows=t)
    g_wt_in = jnp.concatenate([g_wkv_p[:kvl], g_wkv_p[kvl + 2 * nb:kvl + 2 * nb + MLA_ROPE],
                               g_wkv_p[kvl:kvl + 2 * nb], g_wqg_p[:q_w], g_wqg_p[q_w + q_pad:]], axis=0)
    r_in = reduce_start("in", ["w_in"], [g_wt_in.reshape(N_DEV, -1, d)])
    _after(r_in["tok"])
    qw_p = q_w + q_pad
    dz_lat = _mm_sum_nn([(dq_p, 0, wt_qg_p, 0, qw_p), (dgates, 0, wt_qg_p, qw_p, d), (dgates, d, wt_qg_p, qw_p + d, d),
                         (dkv_p, 0, wt_kv_p, 0, wkv_w)], F32, "d_z_lat", rows=t)
    dz_ctx = _mm(dkv_p, wt_kv_p, "nn", F32, "d_z_ctx", tm=min(ROW_BLOCK, tc), tn=1024, a_row_off=t)
    tok_q = reduce_relay(r_qkv, dz_ctx)
    _after(tok_q)
    grad_x, st_n1 = _norm1_bwd(cts, xs, norm1_g, mods1, dz_ctx, dz_lat, dx1)

    res = {}

    def upd(nm, parts):
        wv, mv, vv = weights[nm], mom_m[nm], mom_v[nm]
        if wv.ndim == 1:
            wv, mv, vv = (a.reshape(1, -1) for a in (wv, mv, vv))
        if nm in narrow:
            wv, mv, vv = tview(wv), tview(mv), tview(vv)
        outs = _adamw(parts, wv, mv, vv, "adamw_" + nm)
        if nm in narrow:
            outs = [tview(o_) for o_ in outs]
        res[nm] = [o_.reshape(weights[nm].shape) for o_ in outs]

    d_lat = jnp.concatenate([st_n1[0], st_n1[1], st_n2[3], st_n2[0], st_n2[1], st_fin[1]])
    d_cxt = jnp.concatenate([st_n1[3], st_n1[4], jnp.zeros((4 * d,), F32)])
    small = jnp.concatenate([d_lat, d_cxt, st_n1[2], st_q[0], st_kv[0], st_qb[0], st_kb[0], st_n2[2],
                             jnp.concatenate([dcb[0, 0], dcb[1, 0]]), st_fin[0], st_fin[3, :LANE]])
    n_small = small.shape[0]
    pad_small = (-n_small) % LANE
    (small_all,) = _all_gather([jnp.pad(small, (0, pad_small)).reshape(1, -1)], "gather_small")
    offs = {}
    o = 0
    for nm, ln in (("d_lat", 6 * d), ("d_cxt", 6 * d), ("norm1_g", d), ("mla_q_norm_g", ql), ("mla_kv_norm_g", kvl),
                   ("gqa_q_norm_g", GQA_HEAD_DIM), ("gqa_k_norm_g", GQA_HEAD_DIM), ("norm2_g", d), ("conv_b", f2),
                   ("final_norm_g", d), ("loss", LANE)):
        offs[nm] = (o, ln)
        o += ln

    def part(nm):
        a, ln = offs[nm]
        return small_all[:, :, a:a + ln]

    loss = _sum_parts(part("loss"))[0, 0]
    d_lat_all = part("d_lat")[:, 0, :]
    d_cxt_sum = _sum_parts(part("d_cxt"))
    da16 = jnp.concatenate([d_lat_all, d_cxt_sum, jnp.zeros((7, 6 * d), F32)], axis=0)
    da16_shard = lax.dynamic_slice_in_dim(da16, my_idx * ncol, ncol, axis=1)
    cc_part = _cctx_partial(da16_shard, w_ada[0], c_ctx[None, :])
    (cc_all,) = _all_gather([cc_part], "gather_cctx")
    cc_parts = cc_all[:, 0:1, :]
    tok_i = reduce_relay(r_in, cc_all)

    _after(tok_i)
    for nm in ("norm1_g", "mla_q_norm_g", "mla_kv_norm_g", "gqa_q_norm_g", "gqa_k_norm_g", "norm2_g", "conv_b",
               "final_norm_g"):
        upd(nm, part(nm))
    upd("c_ctx", cc_parts)
    b_parts = jnp.concatenate([d_lat_all[:, None, :], d_cxt_sum[None]], axis=0)
    upd("b_ada", b_parts)
    _after(tok_i)
    outs = _adamw_ada(conds, da16_shard, w_ada[0], m_w_ada[0], v_w_ada[0])
    res["w_ada"] = [o_[None] for o_ in outs]
    last = outs[0]
    done = [last]
    for grp in (r_down, r_up, r_out, r_qkv, r_in):
        _after(*done)
        recv = reduce_finish(grp, last)
        for nm in grp["names"]:
            upd(nm, recv[nm][:, :3, :] if nm == "conv_w" else recv[nm])
            last = res[nm][0]
            done.append(last)

    return (loss, grad_x[None], *[res[n][0] for n in order], *[res[n][1] for n in order],
            *[res[n][2] for n in order], *[res[n][3] for n in order])
```

```python
import functools

import jax
import jax.numpy as jnp
from jax import lax
from jax.experimental import pallas as pl
from jax.experimental.pallas import tpu as pltpu

F32 = jnp.float32
BF16 = jnp.bfloat16

GRID_W = 64
ROPE_THETA = 10000.0
NORM_EPS = 1e-6
MLA_HEADS = 8
MLA_Q_LORA = 768
MLA_KV_LORA = 512
MLA_NOPE = 128
MLA_ROPE = 64
MLA_V = 128
GQA_HEADS = 8
GQA_KV_HEADS = 2
GQA_HEAD_DIM = 128
ADAM_LR = 0.001
ADAM_B1 = 0.9
ADAM_B2 = 0.999
ADAM_EPS = 1e-08
ADAM_WD = 0.01
ADAM_STEP = 10

N_DEV = 8
MESH_AXES = ("x", "y", "c")
LANE = 128
MLA_SLOT = 2 * LANE
VMEM_LIMIT = 56 * 1024 * 1024
ROW_BLOCK = 256
ATT_Q_BLOCK = 512
ATT_Q_BLOCK_FWD = 512
LN2 = 0.6931471805599453
LOG2E = 1.4426950408889634
MESH_ID = pl.DeviceIdType.MESH


def _tile(n, pref, align=LANE):
    if n <= pref:
        return n
    best = None
    t = align
    while t <= pref:
        if n % t == 0:
            best = t
        t += align
    assert best is not None, (n, pref, align)
    return best


def _cparams(sem=None):
    return pltpu.CompilerParams(dimension_semantics=sem, vmem_limit_bytes=VMEM_LIMIT)


_ORDER_AFTER = []


def _after(*arrays):
    _ORDER_AFTER.extend(arrays)


def _pcall(body, *, in_specs, **kw):
    deps = tuple(_ORDER_AFTER)
    _ORDER_AFTER.clear()
    if not deps:
        return pl.pallas_call(body, in_specs=in_specs, **kw)
    n_in, n_dep = len(in_specs), len(deps)

    def with_deps(*refs):
        body(*refs[:n_in], *refs[n_in + n_dep:])

    call = pl.pallas_call(with_deps, in_specs=list(in_specs) + [pl.BlockSpec(memory_space=pl.ANY)] * n_dep, **kw)
    return lambda *args: call(*args, *deps)


def _all_gather(arrs, name):
    n = len(arrs)

    def body(*refs):
        ins = refs[:n]
        outs = refs[n:2 * n]
        send_sems, recv_sems, local_sems = refs[2 * n:]
        x, y, c = lax.axis_index("x"), lax.axis_index("y"), lax.axis_index("c")
        me, sibling = (x, y, c), (x, y, 1 - c)
        chips = [(1 - x, y), (x, 1 - y), (1 - x, 1 - y)]

        def rows(a, dev):
            px, py, pc = dev
            return outs[a].at[4 * px + 2 * py + pc]

        def copy(a, k, block, to, src=None):
            return pltpu.make_async_remote_copy(
                src_ref=rows(a, block) if src is None else src,
                dst_ref=rows(a, block),
                send_sem=send_sems.at[7 * a + k],
                recv_sem=recv_sems.at[7 * a + k],
                device_id=to,
                device_id_type=MESH_ID,
            )

        mine = [pltpu.make_async_copy(ins[a], rows(a, me), local_sems.at[a]) for a in range(n)]
        for cp in mine:
            cp.start()
        first = []
        for a in range(n):
            first.append(copy(a, 0, me, sibling, src=ins[a]))
            first += [copy(a, 1 + j, me, (*chip, c), src=ins[a]) for j, chip in enumerate(chips)]
        for cp in first:
            cp.start()
        passed = []
        for j, chip in enumerate(chips):
            for a in range(n):
                copy(a, 1 + j, (*chip, c), me).wait_recv()
                fwd = copy(a, 4 + j, (*chip, c), sibling)
                fwd.start()
                passed.append(fwd)
        for a in range(n):
            copy(a, 0, sibling, me).wait_recv()
            for j, chip in enumerate(chips):
                copy(a, 4 + j, (*chip, 1 - c), me).wait_recv()
        for cp in first + passed:
            cp.wait_send()
        for cp in mine:
            cp.wait()

    any_spec = pl.BlockSpec(memory_space=pl.ANY)
    outs = _pcall(
        body,
        name=name,
        out_shape=[jax.ShapeDtypeStruct((N_DEV,) + a.shape, a.dtype) for a in arrs],
        in_specs=[any_spec] * n,
        out_specs=[any_spec] * n,
        scratch_shapes=[
            pltpu.SemaphoreType.DMA((7 * n,)),
            pltpu.SemaphoreType.DMA((7 * n,)),
            pltpu.SemaphoreType.DMA((n,)),
        ],
    )(*arrs)
    return list(outs)


def _all_to_all(arrs, name):
    n = len(arrs)

    def body(*refs):
        ins = refs[:n]
        outs = refs[n:2 * n]
        send_sems, recv_sems, local_sems = refs[2 * n:]
        x, y, c = lax.axis_index("x"), lax.axis_index("y"), lax.axis_index("c")
        my_idx = 4 * x + 2 * y + c

        def peer(k):
            fx, fy, fc = (k >> 2) & 1, (k >> 1) & 1, k & 1
            return (x ^ fx if fx else x, y ^ fy if fy else y, c ^ fc if fc else c)

        def copy(a, k):
            px, py, pc = peer(k)
            return pltpu.make_async_remote_copy(
                src_ref=ins[a].at[4 * px + 2 * py + pc],
                dst_ref=outs[a].at[my_idx],
                send_sem=send_sems.at[7 * a + k - 1],
                recv_sem=recv_sems.at[7 * a + k - 1],
                device_id=(px, py, pc),
                device_id_type=MESH_ID,
            )

        mine = [pltpu.make_async_copy(ins[a].at[my_idx], outs[a].at[my_idx], local_sems.at[a]) for a in range(n)]
        for cp in mine:
            cp.start()
        order = [1, 4, 2, 5, 3, 6, 7]
        cps = [copy(a, k) for k in order for a in range(n)]
        for cp in cps:
            cp.start()
        for cp in cps:
            cp.wait()
        for cp in mine:
            cp.wait()

    any_spec = pl.BlockSpec(memory_space=pl.ANY)
    outs = _pcall(
        body,
        name=name,
        out_shape=[jax.ShapeDtypeStruct(a.shape, a.dtype) for a in arrs],
        in_specs=[any_spec] * n,
        out_specs=[any_spec] * n,
        scratch_shapes=[
            pltpu.SemaphoreType.DMA((7 * n,)),
            pltpu.SemaphoreType.DMA((7 * n,)),
            pltpu.SemaphoreType.DMA((n,)),
        ],
    )(*arrs)
    return list(outs)


_HBM = pl.BlockSpec(memory_space=pltpu.HBM)
_SEM = pl.BlockSpec(memory_space=pltpu.SEMAPHORE)
_EFFECT = pltpu.SideEffectType.DATAFLOW_SIDE_EFFECTING


def _descriptors(copies, send_sems, recv_sems):
    descs = []
    for i, (src, dst, dev) in enumerate(copies):
        if dev is None:
            descs.append(pltpu.make_async_copy(src, dst, recv_sems.at[i]))
        else:
            descs.append(pltpu.make_async_remote_copy(src_ref=src, dst_ref=dst, send_sem=send_sems.at[i],
                                                      recv_sem=recv_sems.at[i], device_id=dev, device_id_type=MESH_ID))
    return descs


def _split_start(name, arrays, copies_fn, n_copies):
    n = len(arrays)

    def body(*refs):
        send_sems, recv_sems = refs[n], refs[n + 1]
        token = refs[2 * n + 2]
        for dsc in _descriptors(copies_fn(refs[:n]), send_sems, recv_sems):
            dsc.start()
        token[...] = jnp.zeros_like(token)

    outs = _pcall(
        body,
        name=name,
        out_shape=(pltpu.SemaphoreType.DMA((n_copies,)), pltpu.SemaphoreType.DMA((n_copies,)),
                   *[pltpu.HBM(a.shape, a.dtype) for a in arrays], jax.ShapeDtypeStruct((8, LANE), F32)),
        in_specs=[_HBM] * n,
        out_specs=(_SEM, _SEM, *[_HBM] * n, pl.BlockSpec(memory_space=pltpu.VMEM)),
        input_output_aliases={i: 2 + i for i in range(n)},
        compiler_params=pltpu.CompilerParams(has_side_effects=_EFFECT),
    )(*[pltpu.with_memory_space_constraint(a, pltpu.HBM) for a in arrays])
    return outs[0], outs[1], list(outs[2:2 + n]), outs[2 + n]


def _split_wait(name, send_sems, recv_sems, arrays, copies_fn, after):
    n = len(arrays)

    def body(*refs):
        for dsc, (_, _, dev) in zip(_descriptors(copies_fn(refs[:n]), refs[n], refs[n + 1]), copies_fn(refs[:n])):
            if dev is None:
                dsc.wait()
            else:
                dsc.wait_send()
                dsc.wait_recv()

    outs = _pcall(
        body,
        name=name,
        out_shape=tuple(pltpu.HBM(a.shape, a.dtype) for a in arrays),
        in_specs=[_HBM] * n + [_SEM, _SEM, pl.BlockSpec(memory_space=pl.ANY)],
        out_specs=tuple([_HBM] * n),
        input_output_aliases={i: i for i in range(n)},
        compiler_params=pltpu.CompilerParams(has_side_effects=_EFFECT),
    )(*arrays, send_sems, recv_sems, after)
    return list(outs)


def _mesh_pos():
    x, y, c = lax.axis_index("x"), lax.axis_index("y"), lax.axis_index("c")
    return x, y, c, [(1 - x, y), (x, 1 - y), (1 - x, 1 - y)]


def _gather_ici_copies(n):
    def copies(refs):
        x, y, c, chips = _mesh_pos()
        me = 4 * x + 2 * y + c
        out = []
        for a in range(n):
            src, buf = refs[a], refs[n + a]
            out.append((src, buf.at[me], None))
            out.append((src, buf.at[me], (x, y, 1 - c)))
            out += [(src, buf.at[me], (cx, cy, c)) for cx, cy in chips[:2]]
        return out
    return copies


def _gather_pass_copies(n):
    def copies(refs):
        x, y, c, chips = _mesh_pos()
        south = c == 0
        bx, by = jnp.where(south, 1 - x, x), jnp.where(south, y, 1 - y)
        tx, ty = jnp.where(south, x, 1 - x), jnp.where(south, 1 - y, y)
        out = []
        for a in range(n):
            rows = refs[a].at[4 * bx + 2 * by + c]
            out.append((rows, rows, (tx, ty, c)))
            for cx, cy in chips[:2]:
                rows = refs[a].at[4 * cx + 2 * cy + c]
                out.append((rows, rows, (x, y, 1 - c)))
        return out
    return copies


def _gather_d2d_copies(n):
    def copies(refs):
        x, y, c, chips = _mesh_pos()
        cx, cy = chips[2]
        out = []
        for a in range(n):
            rows = refs[a].at[4 * cx + 2 * cy + c]
            out.append((rows, rows, (x, y, 1 - c)))
        return out
    return copies


def _reduce_d2d_copies(n):
    def copies(refs):
        x, y, c, _ = _mesh_pos()
        out = []
        for a in range(n):
            for k in range(4):
                out.append((refs[a].at[2 * k + (1 - c)], refs[n + a].at[k], (x, y, 1 - c)))
        return out
    return copies


def _reduce_ici_copies(n):
    def copies(refs):
        x, y, c, chips = _mesh_pos()
        mine = 2 * x + y
        out = []
        for a in range(n):
            src, land = refs[a], refs[n + a]
            out.append((src.at[mine], land.at[mine], None))
            out += [(src.at[2 * cx + cy], land.at[mine], (cx, cy, c)) for cx, cy in chips]
        return out
    return copies


def _pair_sum(send, land, c_idx, name):
    _, r, cols = send.shape
    rb = _tile(r, max(8, (1 << 22) // (send.dtype.itemsize * cols) // 8 * 8), 8)
    dt = send.dtype

    def body(c_ref, s_ref, l_ref, o_ref):
        o_ref[...] = (s_ref[...].astype(F32) + l_ref[...].astype(F32)).astype(dt)

    return pl.pallas_call(
        body,
        name=name,
        out_shape=jax.ShapeDtypeStruct((4, r, cols), dt),
        grid_spec=pltpu.PrefetchScalarGridSpec(
            num_scalar_prefetch=1,
            grid=(4, r // rb),
            in_specs=[pl.BlockSpec((None, rb, cols), lambda k, i, c_ref: (2 * k + c_ref[0], i, 0)),
                      pl.BlockSpec((None, rb, cols), lambda k, i, c_ref: (k, i, 0))],
            out_specs=pl.BlockSpec((None, rb, cols), lambda k, i, c_ref: (k, i, 0)),
        ),
        compiler_params=_cparams(("parallel", "parallel")),
    )(c_idx, send, land)


_DIMS = {
    "nn": (((1,), (0,)), ((), ())),
    "nt": (((1,), (1,)), ((), ())),
    "tn": (((0,), (0,)), ((), ())),
}


def _mm_call(a, b, *, mode, grid, a_spec, b_spec, o_spec, out_shape, acc_shape, name):
    nk = grid[2]
    out_dtype = out_shape.dtype

    def body(a_ref, b_ref, o_ref, *scratch):
        p = lax.dot_general(a_ref[...].astype(BF16), b_ref[...].astype(BF16), _DIMS[mode],
                            preferred_element_type=F32)
        if nk == 1:
            o_ref[...] = p.astype(out_dtype)
        else:
            acc = scratch[0]
            k = pl.program_id(2)

            @pl.when(k == 0)
            def _():
                acc[...] = p

            @pl.when(k > 0)
            def _():
                acc[...] += p

            @pl.when(k == nk - 1)
            def _():
                o_ref[...] = acc[...].astype(out_dtype)

    return _pcall(
        body,
        name=name,
        out_shape=out_shape,
        grid=grid,
        in_specs=[a_spec, b_spec],
        out_specs=o_spec,
        scratch_shapes=[pltpu.VMEM(acc_shape, F32)] if nk > 1 else [],
        compiler_params=_cparams(("parallel", "parallel", "arbitrary")),
    )(a, b)


def _mm(a, b, mode, out_dtype, name, tm=512, tn=512, tk=2432, a_row_off=0, rows=None):
    if mode == "nn":
        (m, k), (k2, n) = a.shape, b.shape
    elif mode == "nt":
        (m, k), (n, k2) = a.shape, b.shape
    else:
        (k, m), (k2, n) = a.shape, b.shape
        if rows is not None:
            k = k2 = rows
    assert k == k2, (a.shape, b.shape, mode)
    if mode != "tn":
        m = (m if rows is None else rows + a_row_off) - a_row_off
    tm, tn, tk = _tile(m, tm, 8), _tile(n, tn), _tile(k, tk, 8 if mode == "tn" else LANE)
    assert a_row_off % tm == 0
    ro = a_row_off // tm
    grid = (m // tm, n // tn, k // tk)
    if mode == "tn":
        a_spec = pl.BlockSpec((tk, tm), lambda i, j, kk: (kk, i))
    else:
        a_spec = pl.BlockSpec((tm, tk), lambda i, j, kk: (i + ro, kk))
    if mode == "nt":
        b_spec = pl.BlockSpec((tn, tk), lambda i, j, kk: (j, kk))
    else:
        b_spec = pl.BlockSpec((tk, tn), lambda i, j, kk: (kk, j))
    o_spec = pl.BlockSpec((tm, tn), lambda i, j, kk: (i, j))
    return _mm_call(a, b, mode=mode, grid=grid, a_spec=a_spec, b_spec=b_spec, o_spec=o_spec,
                    out_shape=jax.ShapeDtypeStruct((m, n), out_dtype), acc_shape=(tm, tn), name=name)


def _mm_cat_nt(pieces, out_dtype, name, tm=1024, tn=1024, tk=2048, rows=None):
    m = pieces[0][0].shape[0] if rows is None else rows
    n = pieces[0][1].shape[0]
    tm, tn = _tile(m, tm, 8), _tile(n, tn)
    steps, starts, s = [], [], 0
    for a, b, off in pieces:
        kp = a.shape[1]
        tkp = _tile(kp, tk)
        assert off % tkp == 0 and b.shape[0] == n
        steps.append((tkp, kp // tkp, off // tkp))
        starts.append(s)
        s += kp // tkp
    nk = s
    npc = len(pieces)

    def body(*refs):
        o_ref, acc = refs[2 * npc], refs[2 * npc + 1]
        kk = pl.program_id(2)

        @pl.when(kk == 0)
        def _():
            acc[...] = jnp.zeros_like(acc)

        for p in range(npc):
            @pl.when((kk >= starts[p]) & (kk < starts[p] + steps[p][1]))
            def _(p=p):
                acc[...] += lax.dot_general(refs[2 * p][...].astype(BF16), refs[2 * p + 1][...].astype(BF16), _DIMS["nt"],
                                            preferred_element_type=F32)

        @pl.when(kk == nk - 1)
        def _():
            o_ref[...] = acc[...].astype(out_dtype)

    in_specs, args = [], []
    for p, (a, b, off) in enumerate(pieces):
        tkp, np_, ob = steps[p]

        def rel(kk, p=p, np_=np_):
            return jnp.clip(kk - starts[p], 0, np_ - 1)

        in_specs.append(pl.BlockSpec((tm, tkp), lambda i, j, kk, rel=rel: (i, rel(kk))))
        in_specs.append(pl.BlockSpec((tn, tkp), lambda i, j, kk, rel=rel, ob=ob: (j, ob + rel(kk))))
        args += [a, b]
    return _pcall(
        body,
        name=name,
        out_shape=jax.ShapeDtypeStruct((m, n), out_dtype),
        grid=(m // tm, n // tn, nk),
        in_specs=in_specs,
        out_specs=pl.BlockSpec((tm, tn), lambda i, j, kk: (i, j)),
        scratch_shapes=[pltpu.VMEM((tm, tn), F32)],
        compiler_params=_cparams(("parallel", "parallel", "arbitrary")),
    )(*args)


def _mm_cat_tn(a, pieces, out_dtype, name, tm=1024, tn=1024, rows=None):
    k = a.shape[0] if rows is None else rows
    m = a.shape[1]
    tm = _tile(m, tm)
    starts, s = [], 0
    for b in pieces:
        assert b.shape[1] % tn == 0
        starts.append(s)
        s += b.shape[1] // tn
    nj = s
    npc = len(pieces)

    def body(*refs):
        a_ref, o_ref = refs[0], refs[1 + npc]
        j = pl.program_id(1)
        for p in range(npc):
            @pl.when((j >= starts[p]) & (j < starts[p] + pieces[p].shape[1] // tn))
            def _(p=p):
                o_ref[...] = lax.dot_general(a_ref[...].astype(BF16), refs[1 + p][...].astype(BF16), _DIMS["tn"],
                                             preferred_element_type=F32).astype(out_dtype)

    in_specs = [pl.BlockSpec((k, tm), lambda i, j: (0, i))]
    for p, b in enumerate(pieces):
        np_ = b.shape[1] // tn
        in_specs.append(pl.BlockSpec((k, tn), lambda i, j, p=p, np_=np_: (0, jnp.clip(j - starts[p], 0, np_ - 1))))
    return _pcall(
        body,
        name=name,
        out_shape=jax.ShapeDtypeStruct((m, nj * tn), out_dtype),
        grid=(m // tm, nj),
        in_specs=in_specs,
        out_specs=pl.BlockSpec((tm, tn), lambda i, j: (i, j)),
        compiler_params=_cparams(("parallel", "arbitrary")),
    )(a, *pieces)


def _mm_up_fwd(z2, w3, name, tm=1024):
    t, d = z2.shape
    nsh, _, c = w3.shape
    tm = _tile(t, tm, 8)
    return _mm_call(z2, w3, mode="nn", grid=(t // tm, nsh, 1),
                    a_spec=pl.BlockSpec((tm, d), lambda i, j, kk: (i, 0)),
                    b_spec=pl.BlockSpec((None, d, c), lambda i, j, kk: (j, 0, 0)),
                    o_spec=pl.BlockSpec((tm, c), lambda i, j, kk: (i, j)),
                    out_shape=jax.ShapeDtypeStruct((t, nsh * c), BF16), acc_shape=(tm, c), name=name)


def _mm_up_dz(du3, w3, name, tm=512, tn=1024):
    _, t, f = du3.shape
    nsh, d, c = w3.shape
    half = nsh // 2
    assert f == half * c
    tm, tn = _tile(t, tm, 8), _tile(d, tn)

    def body(a_ref, b_ref, o_ref, acc):
        kk = pl.program_id(2)
        p = None
        for s in range(half):
            q = lax.dot_general(a_ref[:, s * c:(s + 1) * c], b_ref[s], _DIMS["nt"], preferred_element_type=F32)
            p = q if p is None else p + q

        @pl.when(kk == 0)
        def _():
            acc[...] = p

        @pl.when(kk == 1)
        def _():
            o_ref[...] = (acc[...] + p).astype(BF16)

    return _pcall(
        body,
        name=name,
        out_shape=jax.ShapeDtypeStruct((t, d), BF16),
        grid=(t // tm, d // tn, 2),
        in_specs=[pl.BlockSpec((None, tm, f), lambda i, j, kk: (kk, i, 0)),
                  pl.BlockSpec((half, tn, c), lambda i, j, kk: (kk, j, 0))],
        out_specs=pl.BlockSpec((tm, tn), lambda i, j, kk: (i, j)),
        scratch_shapes=[pltpu.VMEM((tm, tn), F32)],
        compiler_params=_cparams(("parallel", "parallel", "arbitrary")),
    )(du3, w3)


def _mm_sum_nn(pieces, out_dtype, name, tm=512, tn=512, rows=None):
    m = pieces[0][0].shape[0] if rows is None else rows
    n = pieces[0][2].shape[1]
    tm, tn = _tile(m, tm, 8), _tile(n, tn)
    npc = len(pieces)

    def body(*refs):
        p = None
        for s in range(npc):
            q = jnp.dot(refs[2 * s][...].astype(BF16), refs[2 * s + 1][...].astype(BF16), preferred_element_type=F32)
            p = q if p is None else p + q
        refs[2 * npc][...] = p.astype(out_dtype)

    in_specs, args = [], []
    for a, ao, b, bo, kp in pieces:
        assert ao % kp == 0 and bo % kp == 0 and b.shape[1] == n
        in_specs.append(pl.BlockSpec((tm, kp), lambda i, j, ab=ao // kp: (i, ab)))
        in_specs.append(pl.BlockSpec((kp, tn), lambda i, j, bb=bo // kp: (bb, j)))
        args += [a, b]
    return _pcall(
        body,
        name=name,
        out_shape=jax.ShapeDtypeStruct((m, n), out_dtype),
        grid=(m // tm, n // tn),
        in_specs=in_specs,
        out_specs=pl.BlockSpec((tm, tn), lambda i, j: (i, j)),
        compiler_params=_cparams(("parallel", "parallel")),
    )(*args)


def _mm_rows_tn(pieces, b, out_dtype, name, tm=1024, tn=1024, rows=None):
    k = b.shape[0] if rows is None else rows
    n = b.shape[1]
    tn = _tile(n, tn)
    starts, s = [], 0
    for a in pieces:
        assert a.shape[1] % tm == 0
        starts.append(s)
        s += a.shape[1] // tm
    ni = s
    npc = len(pieces)

    def body(*refs):
        b_ref, o_ref = refs[npc], refs[npc + 1]
        i = pl.program_id(0)
        for p in range(npc):
            @pl.when((i >= starts[p]) & (i < starts[p] + pieces[p].shape[1] // tm))
            def _(p=p):
                o_ref[...] = lax.dot_general(refs[p][...].astype(BF16), b_ref[...].astype(BF16), _DIMS["tn"],
                                             preferred_element_type=F32).astype(out_dtype)

    in_specs = []
    for p, a in enumerate(pieces):
        np_ = a.shape[1] // tm
        in_specs.append(pl.BlockSpec((k, tm), lambda i, j, p=p, np_=np_: (0, jnp.clip(i - starts[p], 0, np_ - 1))))
    in_specs.append(pl.BlockSpec((k, tn), lambda i, j: (0, j)))
    return _pcall(
        body,
        name=name,
        out_shape=jax.ShapeDtypeStruct((ni * tm, n), out_dtype),
        grid=(ni, n // tn),
        in_specs=in_specs,
        out_specs=pl.BlockSpec((tm, tn), lambda i, j: (i, j)),
        compiler_params=_cparams(("parallel", "parallel")),
    )(*pieces, b)


def _mm_up_gw(z2, du3, nsh, name, tm=1024):
    t, d = z2.shape
    f = du3.shape[2]
    half = nsh // 2
    c = f // half
    tm = _tile(d, tm)
    return _mm_call(z2, du3, mode="tn", grid=(d // tm, nsh, 1),
                    a_spec=pl.BlockSpec((t, tm), lambda i, j, kk: (0, i)),
                    b_spec=pl.BlockSpec((None, t, c), lambda i, j, kk: (j // half, 0, j % half)),
                    o_spec=pl.BlockSpec((None, tm, c), lambda i, j, kk: (j, i, 0)),
                    out_shape=jax.ShapeDtypeStruct((nsh, d, c), BF16), acc_shape=(tm, c), name=name)


def _rms(x):
    r = lax.rsqrt(jnp.mean(x * x, axis=-1, keepdims=True) + NORM_EPS)
    return x * r, r


def _rms_bwd(dxh, xh, r):
    return r * (dxh - xh * jnp.mean(dxh * xh, axis=-1, keepdims=True))


def _colsum(v):
    return jnp.sum(v, axis=0, keepdims=True)


def _rope(v, c, s1, s2, q):
    w = v.shape[-1]
    return v * c + pltpu.roll(v, w - q, 1) * s1 + pltpu.roll(v, q, 1) * s2


def _rope_t(d, c, s1, s2, q):
    w = d.shape[-1]
    return d * c + pltpu.roll(d * s1, q, 1) + pltpu.roll(d * s2, w - q, 1)


def _norm_mod_fwd(ctx, x, gain, mods):
    tc, d = ctx.shape
    t = x.shape[0]
    rb = min(ROW_BLOCK, tc)
    nbl = t // rb

    def body(ctx_ref, x_ref, g_ref, mod_ref, z_ref):
        i = pl.program_id(0)

        def emit(src, sh, sc):
            xh, _ = _rms(src[...])
            z_ref[...] = ((xh * g_ref[...]) * (1.0 + sc) + sh).astype(BF16)

        @pl.when(i >= nbl)
        def _():
            emit(ctx_ref, mod_ref[2:3, :], mod_ref[3:4, :])

        @pl.when(i < nbl)
        def _():
            emit(x_ref, mod_ref[0:1, :], mod_ref[1:2, :])

    return _pcall(
        body,
        name="norm1_mod_fwd",
        out_shape=jax.ShapeDtypeStruct((tc + t, d), BF16),
        grid=((tc + t) // rb,),
        in_specs=[
            pl.BlockSpec((rb, d), lambda i: (jnp.maximum(i - nbl, 0), 0)),
            pl.BlockSpec((rb, d), lambda i: (jnp.minimum(i, nbl - 1), 0)),
            pl.BlockSpec((1, d), lambda i: (0, 0)),
            pl.BlockSpec((8, d), lambda i: (0, 0)),
        ],
        out_specs=pl.BlockSpec((rb, d), lambda i: (i, 0)),
        compiler_params=_cparams(("arbitrary",)),
    )(ctx, x, gain, mods)


def _norm1_bwd(ctx, x, gain, mods, dz_ctx, dz_lat, dx1):
    tc, d = ctx.shape
    t = x.shape[0]
    rb = min(ROW_BLOCK, tc)
    nbl = t // rb

    def body(ctx_ref, x_ref, g_ref, mod_ref, dzc_ref, dzl_ref, dx1_ref, gx_ref, st_ref):
        i = pl.program_id(0)

        @pl.when(i == 0)
        def _():
            st_ref[...] = jnp.zeros_like(st_ref)

        def common(src, dz, sc, row_sh, row_sc):
            xh, r = _rms(src[...])
            g = g_ref[...]
            dxn = dz * (1.0 + sc)
            st_ref[row_sh:row_sh + 1, :] += _colsum(dz)
            st_ref[row_sc:row_sc + 1, :] += _colsum(dz * (xh * g))
            st_ref[2:3, :] += _colsum(dxn * xh)
            return _rms_bwd(dxn * g, xh, r)

        @pl.when(i >= nbl)
        def _():
            common(ctx_ref, dzc_ref[...], mod_ref[3:4, :], 3, 4)

        @pl.when(i < nbl)
        def _():
            gx_ref[...] = dx1_ref[...] + common(x_ref, dzl_ref[...], mod_ref[1:2, :], 0, 1)

    lat = lambda i: (jnp.minimum(i, nbl - 1), 0)
    cix = lambda i: (jnp.maximum(i - nbl, 0), 0)
    return _pcall(
        body,
        name="norm1_mod_bwd",
        out_shape=[jax.ShapeDtypeStruct((t, d), F32), jax.ShapeDtypeStruct((8, d), F32)],
        grid=((tc + t) // rb,),
        in_specs=[
            pl.BlockSpec((rb, d), cix),
            pl.BlockSpec((rb, d), lat),
            pl.BlockSpec((1, d), lambda i: (0, 0)),
            pl.BlockSpec((8, d), lambda i: (0, 0)),
            pl.BlockSpec((rb, d), cix),
            pl.BlockSpec((rb, d), lat),
            pl.BlockSpec((rb, d), lat),
        ],
        out_specs=[pl.BlockSpec((rb, d), lat), pl.BlockSpec((8, d), lambda i: (0, 0))],
        compiler_params=_cparams(("arbitrary",)),
    )(ctx, x, gain, mods, dz_ctx, dz_lat, dx1)


def _key_prep_fwd(kv, kv_gain, kb_gain, tabs):
    ta, wkv = kv.shape
    kvl = MLA_KV_LORA
    nb = GQA_KV_HEADS * GQA_HEAD_DIM
    rb = ROW_BLOCK if ta % ROW_BLOCK == 0 else LANE
    hd = GQA_HEAD_DIM

    def body(kv_ref, g_ref, gb_ref, ca, s1a, s2a, cb, s1b, s2b, kin_ref, kb_ref, vb_ref):
        xh, _ = _rms(kv_ref[:, 0:kvl])
        kin_ref[:, 0:kvl] = (xh * g_ref[...]).astype(BF16)
        kpe = kv_ref[:, kvl + 2 * nb:kvl + 2 * nb + LANE]
        kin_ref[:, kvl:kvl + LANE] = _rope(kpe, ca[...], s1a[...], s2a[...], MLA_ROPE // 4).astype(BF16)
        for h in range(GQA_KV_HEADS):
            nh, _ = _rms(kv_ref[:, kvl + h * hd:kvl + (h + 1) * hd])
            kb_ref[:, h * hd:(h + 1) * hd] = _rope(nh * gb_ref[...], cb[...], s1b[...], s2b[...], hd // 4).astype(BF16)
        vb_ref[...] = kv_ref[:, kvl + nb:kvl + 2 * nb].astype(BF16)

    row = lambda w: pl.BlockSpec((rb, w), lambda i: (i, 0))
    fix = lambda w: pl.BlockSpec((1, w), lambda i: (0, 0))
    return _pcall(
        body,
        name="key_prep_fwd",
        out_shape=[jax.ShapeDtypeStruct((ta, kvl + LANE), BF16), jax.ShapeDtypeStruct((ta, nb), BF16),
                   jax.ShapeDtypeStruct((ta, nb), BF16)],
        grid=(ta // rb,),
        in_specs=[row(wkv), fix(kvl), fix(hd)] + [row(LANE)] * 3 + [row(hd)] * 3,
        out_specs=[row(kvl + LANE), row(nb), row(nb)],
        compiler_params=_cparams(("parallel",)),
    )(kv, kv_gain, kb_gain, *tabs)


def _key_prep_bwd(kv, kv_gain, kb_gain, tabs, dkin, dkb, dvb):
    ta, wkv = kv.shape
    kvl = MLA_KV_LORA
    nb = GQA_KV_HEADS * GQA_HEAD_DIM
    rb = ROW_BLOCK if ta % ROW_BLOCK == 0 else LANE
    hd = GQA_HEAD_DIM

    def body(kv_ref, g_ref, gb_ref, ca, s1a, s2a, cb, s1b, s2b, dkin_ref, dkb_ref, dvb_ref, dkv_ref, st_ref, stb_ref):
        @pl.when(pl.program_id(0) == 0)
        def _():
            st_ref[...] = jnp.zeros_like(st_ref)
            stb_ref[...] = jnp.zeros_like(stb_ref)

        xh, r = _rms(kv_ref[:, 0:kvl])
        dn = dkin_ref[:, 0:kvl]
        st_ref[0:1, :] += _colsum(dn * xh)
        dkv_ref[:, 0:kvl] = _rms_bwd(dn * g_ref[...], xh, r).astype(BF16)
        dpe = _rope_t(dkin_ref[:, kvl:kvl + LANE], ca[...], s1a[...], s2a[...], MLA_ROPE // 4)
        dkv_ref[:, kvl + 2 * nb:kvl + 2 * nb + LANE] = dpe.astype(BF16)
        for h in range(GQA_KV_HEADS):
            nh, rh = _rms(kv_ref[:, kvl + h * hd:kvl + (h + 1) * hd])
            dn_h = _rope_t(dkb_ref[:, h * hd:(h + 1) * hd], cb[...], s1b[...], s2b[...], hd // 4)
            stb_ref[0:1, :] += _colsum(dn_h * nh)
            dkv_ref[:, kvl + h * hd:kvl + (h + 1) * hd] = _rms_bwd(dn_h * gb_ref[...], nh, rh).astype(BF16)
        dkv_ref[:, kvl + nb:kvl + 2 * nb] = dvb_ref[...].astype(BF16)

    row = lambda w: pl.BlockSpec((rb, w), lambda i: (i, 0))
    fix = lambda w: pl.BlockSpec((1, w), lambda i: (0, 0))
    return _pcall(
        body,
        name="key_prep_bwd",
        out_shape=[jax.ShapeDtypeStruct((ta, wkv), BF16), jax.ShapeDtypeStruct((8, kvl), F32),
                   jax.ShapeDtypeStruct((8, hd), F32)],
        grid=(ta // rb,),
        in_specs=[row(wkv), fix(kvl), fix(hd)] + [row(LANE)] * 3 + [row(hd)] * 3 + [row(kvl + LANE), row(nb), row(nb)],
        out_specs=[row(wkv), pl.BlockSpec((8, kvl), lambda i: (0, 0)), pl.BlockSpec((8, hd), lambda i: (0, 0))],
        compiler_params=_cparams(("arbitrary",)),
    )(kv, kv_gain, kb_gain, *tabs, dkin, dkb, dvb)


def _q_prep_fwd(qg, q_gain, qb_gain, tabs, qscale):
    t = qg.shape[0]
    ql = MLA_Q_LORA
    hd = GQA_HEAD_DIM
    hb = GQA_HEADS * hd
    rb = min(ROW_BLOCK, t)

    def body(q_ref, g_ref, gb_ref, cb, s1b, s2b, cqn_ref, qb_ref):
        xh, _ = _rms(q_ref[:, 0:ql])
        cqn_ref[...] = (xh * g_ref[...]).astype(BF16)
        for h in range(GQA_HEADS):
            nh, _ = _rms(q_ref[:, ql + h * hd:ql + (h + 1) * hd])
            qh = _rope(nh * gb_ref[...], cb[...], s1b[...], s2b[...], hd // 4)
            qb_ref[:, h * hd:(h + 1) * hd] = (qh * qscale).astype(BF16)

    row = lambda w: pl.BlockSpec((rb, w), lambda i: (i, 0))
    fix = lambda w: pl.BlockSpec((1, w), lambda i: (0, 0))
    return _pcall(
        body,
        name="q_prep_fwd",
        out_shape=[jax.ShapeDtypeStruct((t, ql), BF16), jax.ShapeDtypeStruct((t, hb), BF16)],
        grid=(t // rb,),
        in_specs=[row(ql + hb), fix(ql), fix(hd)] + [row(hd)] * 3,
        out_specs=[row(ql), row(hb)],
        compiler_params=_cparams(("parallel",)),
    )(qg, q_gain, qb_gain, *tabs)


def _q_prep_bwd(qg, q_gain, qb_gain, tabs, dcqn, dqb, wpad, qscale):
    t = qg.shape[0]
    ql = MLA_Q_LORA
    hd = GQA_HEAD_DIM
    hb = GQA_HEADS * hd
    rb = min(ROW_BLOCK, t)

    def body(q_ref, g_ref, gb_ref, cb, s1b, s2b, dcqn_ref, dqb_ref, dq_ref, st_ref, stb_ref):
        @pl.when(pl.program_id(0) == 0)
        def _():
            st_ref[...] = jnp.zeros_like(st_ref)
            stb_ref[...] = jnp.zeros_like(stb_ref)

        xh, r = _rms(q_ref[:, 0:ql])
        dn = dcqn_ref[...]
        st_ref[0:1, :] += _colsum(dn * xh)
        dq_ref[:, 0:ql] = _rms_bwd(dn * g_ref[...], xh, r).astype(BF16)
        for h in range(GQA_HEADS):
            nh, rh = _rms(q_ref[:, ql + h * hd:ql + (h + 1) * hd])
            dn_h = _rope_t(dqb_ref[:, h * hd:(h + 1) * hd] * qscale, cb[...], s1b[...], s2b[...], hd // 4)
            stb_ref[0:1, :] += _colsum(dn_h * nh)
            dq_ref[:, ql + h * hd:ql + (h + 1) * hd] = _rms_bwd(dn_h * gb_ref[...], nh, rh).astype(BF16)
        if wpad:
            dq_ref[:, ql + hb:ql + hb + wpad] = jnp.zeros((rb, wpad), BF16)

    row = lambda w: pl.BlockSpec((rb, w), lambda i: (i, 0))
    fix = lambda w: pl.BlockSpec((1, w), lambda i: (0, 0))
    return _pcall(
        body,
        name="q_prep_bwd",
        out_shape=[jax.ShapeDtypeStruct((t, ql + hb + wpad), BF16), jax.ShapeDtypeStruct((8, ql), F32),
                   jax.ShapeDtypeStruct((8, hd), F32)],
        grid=(t // rb,),
        in_specs=[row(ql + hb), fix(ql), fix(hd)] + [row(hd)] * 3 + [row(ql), row(hb)],
        out_specs=[row(ql + hb + wpad), pl.BlockSpec((8, ql), lambda i: (0, 0)), pl.BlockSpec((8, hd), lambda i: (0, 0))],
        compiler_params=_cparams(("arbitrary",)),
    )(qg, q_gain, qb_gain, *tabs, dcqn, dqb)


def _rope_a(v, tabs, transpose, out_dtype, name, qscale):
    t, w = v.shape
    rb = min(ROW_BLOCK, t)
    fn = _rope_t if transpose else _rope

    def body(v_ref, c, s1, s2, o_ref):
        for h in range(w // MLA_SLOT):
            sl = slice(h * MLA_SLOT, (h + 1) * MLA_SLOT)
            o_ref[:, sl] = (fn(v_ref[:, sl].astype(F32), c[...], s1[...], s2[...], MLA_ROPE // 4) * qscale).astype(out_dtype)

    row = lambda ww: pl.BlockSpec((rb, ww), lambda i: (i, 0))
    return _pcall(
        body,
        name=name,
        out_shape=jax.ShapeDtypeStruct((t, w), out_dtype),
        grid=(t // rb,),
        in_specs=[row(w)] + [row(MLA_SLOT)] * 3,
        out_specs=row(w),
        compiler_params=_cparams(("parallel",)),
    )(v, *tabs)


def _merge_fwd(pa, pb, qg, gate_blk):
    t, d = pa.shape
    rb = min(ROW_BLOCK, t)

    def body(pa_ref, pb_ref, ga_ref, gb_ref, o_ref):
        o_ref[...] = (jax.nn.sigmoid(ga_ref[...]) * pa_ref[...].astype(F32)
                      + jax.nn.sigmoid(gb_ref[...]) * pb_ref[...].astype(F32)).astype(BF16)

    row = pl.BlockSpec((rb, d), lambda i: (i, 0))
    return _pcall(
        body,
        name="merge_fwd",
        out_shape=jax.ShapeDtypeStruct((t, d), BF16),
        grid=(t // rb,),
        in_specs=[row, row, pl.BlockSpec((rb, d), lambda i: (i, gate_blk)), pl.BlockSpec((rb, d), lambda i: (i, gate_blk + 1))],
        out_specs=row,
        compiler_params=_cparams(("parallel",)),
    )(pa, pb, qg, qg)


def _merge_bwd(dm, pa, pb, qg, gate_blk):
    t, d = pa.shape
    rb = min(ROW_BLOCK, t)

    def body(dm_ref, pa_ref, pb_ref, ga_ref, gb_ref, dpa_ref, dpb_ref, dg_ref):
        dmv = dm_ref[...].astype(F32)
        sa = jax.nn.sigmoid(ga_ref[...])
        sb = jax.nn.sigmoid(gb_ref[...])
        dpa_ref[...] = (dmv * sa).astype(BF16)
        dpb_ref[...] = (dmv * sb).astype(BF16)
        dg_ref[:, 0:d] = (dmv * pa_ref[...].astype(F32) * (sa * (1.0 - sa))).astype(BF16)
        dg_ref[:, d:2 * d] = (dmv * pb_ref[...].astype(F32) * (sb * (1.0 - sb))).astype(BF16)

    row = pl.BlockSpec((rb, d), lambda i: (i, 0))
    return _pcall(
        body,
        name="merge_bwd",
        out_shape=[jax.ShapeDtypeStruct((t, d), BF16), jax.ShapeDtypeStruct((t, d), BF16),
                   jax.ShapeDtypeStruct((t, 2 * d), BF16)],
        grid=(t // rb,),
        in_specs=[row, row, row, pl.BlockSpec((rb, d), lambda i: (i, gate_blk)), pl.BlockSpec((rb, d), lambda i: (i, gate_blk + 1))],
        out_specs=[row, row, pl.BlockSpec((rb, 2 * d), lambda i: (i, 0))],
        compiler_params=_cparams(("parallel",)),
    )(dm, pa, pb, qg, qg)


def _resid_norm_mod(x, branch, gain, mods, name):
    t, d = x.shape
    rb = min(ROW_BLOCK, t)

    def body(x_ref, b_ref, g_ref, mod_ref, x1_ref, z_ref):
        x1 = x_ref[...] + mod_ref[0:1, :] * b_ref[...]
        x1_ref[...] = x1
        xh, _ = _rms(x1)
        z_ref[...] = ((xh * g_ref[...]) * (1.0 + mod_ref[2:3, :]) + mod_ref[1:2, :]).astype(BF16)

    row = pl.BlockSpec((rb, d), lambda i: (i, 0))
    return _pcall(
        body,
        name=name,
        out_shape=[jax.ShapeDtypeStruct((t, d), F32), jax.ShapeDtypeStruct((t, d), BF16)],
        grid=(t // rb,),
        in_specs=[row, row, pl.BlockSpec((1, d), lambda i: (0, 0)), pl.BlockSpec((8, d), lambda i: (0, 0))],
        out_specs=[row, row],
        compiler_params=_cparams(("parallel",)),
    )(x, branch, gain, mods)


def _norm2_bwd(x1, attn, gain, mods, dz2, dx2):
    t, d = x1.shape
    rb = min(ROW_BLOCK, t)

    def body(x1_ref, at_ref, g_ref, mod_ref, dz_ref, dx2_ref, dx1_ref, da_ref, st_ref):
        @pl.when(pl.program_id(0) == 0)
        def _():
            st_ref[...] = jnp.zeros_like(st_ref)

        xh, r = _rms(x1_ref[...])
        g = g_ref[...]
        dz = dz_ref[...].astype(F32)
        dxn = dz * (1.0 + mod_ref[1:2, :])
        st_ref[0:1, :] += _colsum(dz)
        st_ref[1:2, :] += _colsum(dz * (xh * g))
        st_ref[2:3, :] += _colsum(dxn * xh)
        dx1 = dx2_ref[...] + _rms_bwd(dxn * g, xh, r)
        dx1_ref[...] = dx1
        st_ref[3:4, :] += _colsum(dx1 * at_ref[...])
        da_ref[...] = (dx1 * mod_ref[0:1, :]).astype(BF16)

    row = pl.BlockSpec((rb, d), lambda i: (i, 0))
    return _pcall(
        body,
        name="norm2_mod_bwd",
        out_shape=[jax.ShapeDtypeStruct((t, d), F32), jax.ShapeDtypeStruct((t, d), BF16), jax.ShapeDtypeStruct((8, d), F32)],
        grid=(t // rb,),
        in_specs=[row, row, pl.BlockSpec((1, d), lambda i: (0, 0)), pl.BlockSpec((8, d), lambda i: (0, 0)), row, row],
        out_specs=[row, row, pl.BlockSpec((8, d), lambda i: (0, 0))],
        compiler_params=_cparams(("arbitrary",)),
    )(x1, attn, gain, mods, dz2, dx2)


def _final_loss(x1, ffn, gain, mods, target):
    t, d = x1.shape
    rb = min(ROW_BLOCK, t)
    nb = t // rb

    def body(x1_ref, f_ref, g_ref, mod_ref, tg_ref, dx2_ref, df_ref, st_ref):
        i = pl.program_id(0)

        @pl.when(i == 0)
        def _():
            st_ref[...] = jnp.zeros_like(st_ref)

        ffn_v = f_ref[...]
        g2 = mod_ref[0:1, :]
        x2 = x1_ref[...] + g2 * ffn_v
        xh, r = _rms(x2)
        g = g_ref[...]
        err = xh * g - tg_ref[...]
        st_ref[2:3, :] += _colsum(err * err) * (0.5 / d)
        dy = err * (1.0 / d)
        st_ref[0:1, :] += _colsum(dy * xh)
        dx2 = _rms_bwd(dy * g, xh, r)
        dx2_ref[...] = dx2
        st_ref[1:2, :] += _colsum(dx2 * ffn_v)
        df_ref[...] = (dx2 * g2).astype(BF16)

        @pl.when(i == nb - 1)
        def _():
            st_ref[3:4, :] = jnp.broadcast_to(jnp.sum(st_ref[2:3, :], axis=-1, keepdims=True), (1, d))

    row = pl.BlockSpec((rb, d), lambda i: (i, 0))
    return _pcall(
        body,
        name="final_norm_loss",
        out_shape=[jax.ShapeDtypeStruct((t, d), F32), jax.ShapeDtypeStruct((t, d), BF16), jax.ShapeDtypeStruct((8, d), F32)],
        grid=(nb,),
        in_specs=[row, row, pl.BlockSpec((1, d), lambda i: (0, 0)), pl.BlockSpec((8, d), lambda i: (0, 0)), row],
        out_specs=[row, row, pl.BlockSpec((8, d), lambda i: (0, 0))],
        compiler_params=_cparams(("arbitrary",)),
    )(x1, ffn, gain, mods, target)


def _row_ends(shape):
    rows = lax.broadcasted_iota(jnp.int32, shape, 0)
    return rows == 0, rows == shape[0] - 1


def _shift_dn(v, first):
    return jnp.where(first, 0.0, pltpu.roll(v, 1, 0))


def _shift_up(v, last):
    return jnp.where(last, 0.0, pltpu.roll(v, v.shape[0] - 1, 0))


def _conv_fwd(u, cw, cb):
    t, f2 = u.shape
    f = f2 // 2
    cbk = _tile(f, 256)
    nf = f // cbk

    def body(ua_ref, ub_ref, cwa_ref, cwb_ref, cba_ref, cbb_ref, h_ref, uc_ref):
        first, last = _row_ends((t, cbk))
        outs = []
        for u_ref, cw_ref, cb_ref in ((ua_ref, cwa_ref, cba_ref), (ub_ref, cwb_ref, cbb_ref)):
            uu, cwv = u_ref[...].astype(F32), cw_ref[...]
            outs.append(cb_ref[...] + cwv[0:1, :] * _shift_dn(uu, first) + cwv[1:2, :] * uu
                        + cwv[2:3, :] * _shift_up(uu, last))
        a, b = outs
        uc_ref[0] = a.astype(BF16)
        uc_ref[1] = b.astype(BF16)
        h_ref[...] = (a * jax.nn.sigmoid(a) * b).astype(BF16)

    ca = lambda r: pl.BlockSpec((r, cbk), lambda j: (0, j))
    cbs = lambda r: pl.BlockSpec((r, cbk), lambda j: (0, nf + j))
    return _pcall(
        body,
        name="conv_gate_fwd",
        out_shape=[jax.ShapeDtypeStruct((t, f), BF16), jax.ShapeDtypeStruct((2, t, f), BF16)],
        grid=(nf,),
        in_specs=[ca(t), cbs(t), ca(3), cbs(3), ca(1), cbs(1)],
        out_specs=[ca(t), pl.BlockSpec((2, t, cbk), lambda j: (0, 0, j))],
        compiler_params=_cparams(("parallel",)),
    )(u, u, cw, cw, cb, cb)


def _conv_bwd(u, uc, cw, dh):
    t, f2 = u.shape
    f = f2 // 2
    cbk = _tile(f, 256)
    nf = f // cbk

    def body(ua_ref, ub_ref, uc_ref, cwa_ref, cwb_ref, dh_ref, du_ref, dcw_ref, dcb_ref):
        first, last = _row_ends((t, cbk))
        a, b = uc_ref[0].astype(F32), uc_ref[1].astype(F32)
        dh_v = dh_ref[...].astype(F32)
        sg = jax.nn.sigmoid(a)
        db = dh_v * (a * sg)
        da = dh_v * b * (sg * (1.0 + a * (1.0 - sg)))
        for idx, (dv, u_ref, cw_ref) in enumerate(((da, ua_ref, cwa_ref), (db, ub_ref, cwb_ref))):
            uu, cwv = u_ref[...].astype(F32), cw_ref[...]
            up, dn = _shift_up(dv, last), _shift_dn(dv, first)
            dcb_ref[idx] = _colsum(dv)
            dcw_ref[idx, 0:1, :] = _colsum(up * uu)
            dcw_ref[idx, 1:2, :] = _colsum(dv * uu)
            dcw_ref[idx, 2:3, :] = _colsum(dn * uu)
            du_ref[idx] = (cwv[0:1, :] * up + cwv[1:2, :] * dv + cwv[2:3, :] * dn).astype(BF16)

    ca = lambda r: pl.BlockSpec((r, cbk), lambda j: (0, j))
    cbs = lambda r: pl.BlockSpec((r, cbk), lambda j: (0, nf + j))
    o3 = lambda r: pl.BlockSpec((2, r, cbk), lambda j: (0, 0, j))
    return _pcall(
        body,
        name="conv_gate_bwd",
        out_shape=[jax.ShapeDtypeStruct((2, t, f), BF16), jax.ShapeDtypeStruct((2, 3, f), F32),
                   jax.ShapeDtypeStruct((2, 1, f), F32)],
        grid=(nf,),
        in_specs=[ca(t), cbs(t), o3(t), ca(3), cbs(3), ca(t)],
        out_specs=[o3(t), o3(3), o3(1)],
        compiler_params=_cparams(("parallel",)),
    )(u, u, uc, cw, cw, dh)


def _attention_fwd(q, kk, vv, *, hq, hkv, dk, dv, k_blk0, v_blk0, name):
    t = q.shape[0]
    tk = kk.shape[0]
    g_sz = hq // hkv
    tq = min(ATT_Q_BLOCK_FWD, t)

    def body(q_ref, k_ref, v_ref, o_ref, lse_ref):
        k = k_ref[...]
        v = v_ref[...]
        for j in range(g_sz):
            s = lax.dot_general(q_ref[:, j * dk:(j + 1) * dk], k, _DIMS["nt"], preferred_element_type=F32)
            m = jnp.max(s, axis=-1, keepdims=True)
            p = jnp.exp2(s - m)
            l = jnp.sum(p, axis=-1, keepdims=True)
            o = jnp.dot(p.astype(BF16), v, preferred_element_type=F32) / l
            o_ref[:, j * dv:(j + 1) * dv] = o.astype(BF16)
            lse_ref[0, :, j:j + 1] = m + jnp.log2(l)

    return _pcall(
        body,
        name=name,
        out_shape=[jax.ShapeDtypeStruct((t, hq * dv), BF16), jax.ShapeDtypeStruct((hkv, t, g_sz), F32)],
        grid=(hkv, t // tq),
        in_specs=[
            pl.BlockSpec((tq, g_sz * dk), lambda g, i: (i, g)),
            pl.BlockSpec((tk, dk), lambda g, i: (0, k_blk0 + g)),
            pl.BlockSpec((tk, dv), lambda g, i: (0, v_blk0 + g)),
        ],
        out_specs=[
            pl.BlockSpec((tq, g_sz * dv), lambda g, i: (i, g)),
            pl.BlockSpec((1, tq, g_sz), lambda g, i: (g, i, 0)),
        ],
        compiler_params=_cparams(("parallel", "parallel")),
    )(q, kk, vv)


def _attention_bwd(q, kk, vv, do, lse, *, hq, hkv, dk, dv, k_blk0, v_blk0, name):
    t = q.shape[0]
    tk = kk.shape[0]
    g_sz = hq // hkv
    tq = min(ATT_Q_BLOCK, t)

    def body(q_ref, k_ref, v_ref, do_ref, lse_ref, dq_ref, dk_ref, dv_ref):
        @pl.when(pl.program_id(1) == 0)
        def _():
            dk_ref[...] = jnp.zeros_like(dk_ref)
            dv_ref[...] = jnp.zeros_like(dv_ref)

        k = k_ref[...]
        v = v_ref[...]
        dk_acc = dv_acc = None
        for j in range(g_sz):
            qj = q_ref[:, j * dk:(j + 1) * dk]
            doj = do_ref[:, j * dv:(j + 1) * dv]
            s = lax.dot_general(qj, k, _DIMS["nt"], preferred_element_type=F32)
            p = jnp.exp2(s - lse_ref[0, :, j:j + 1])
            dp = lax.dot_general(doj, v, _DIMS["nt"], preferred_element_type=F32)
            ds = (p * (dp - jnp.sum(p * dp, axis=-1, keepdims=True))).astype(BF16)
            dv_j = lax.dot_general(p.astype(BF16), doj, _DIMS["tn"], preferred_element_type=F32)
            dk_j = lax.dot_general(ds, qj, _DIMS["tn"], preferred_element_type=F32)
            dv_acc = dv_j if dv_acc is None else dv_acc + dv_j
            dk_acc = dk_j if dk_acc is None else dk_acc + dk_j
            dq_ref[:, j * dk:(j + 1) * dk] = jnp.dot(ds, k, preferred_element_type=F32)
        dv_ref[...] += dv_acc
        dk_ref[...] += dk_acc

        @pl.when(pl.program_id(1) == t // tq - 1)
        def _():
            dk_ref[...] *= LN2

    return _pcall(
        body,
        name=name,
        out_shape=[jax.ShapeDtypeStruct((t, hq * dk), F32), jax.ShapeDtypeStruct((tk, hkv * dk), F32),
                   jax.ShapeDtypeStruct((tk, hkv * dv), F32)],
        grid=(hkv, t // tq),
        in_specs=[
            pl.BlockSpec((tq, g_sz * dk), lambda g, i: (i, g)),
            pl.BlockSpec((tk, dk), lambda g, i: (0, k_blk0 + g)),
            pl.BlockSpec((tk, dv), lambda g, i: (0, v_blk0 + g)),
            pl.BlockSpec((tq, g_sz * dv), lambda g, i: (i, g)),
            pl.BlockSpec((1, tq, g_sz), lambda g, i: (g, i, 0)),
        ],
        out_specs=[
            pl.BlockSpec((tq, g_sz * dk), lambda g, i: (i, g)),
            pl.BlockSpec((tk, dk), lambda g, i: (0, g)),
            pl.BlockSpec((tk, dv), lambda g, i: (0, g)),
        ],
        compiler_params=_cparams(("parallel", "arbitrary")),
    )(q, kk, vv, do, lse)


def _silu(v):
    return v * jax.nn.sigmoid(v)


def _ada_fwd(conds, w_ada, b_ada_shard):
    r, d = conds.shape
    n = w_ada.shape[1]
    tn = _tile(n, 512)

    def body(c_ref, w_ref, b_ref, o_ref):
        s = _silu(c_ref[...]).astype(BF16)
        o_ref[...] = jnp.dot(s, w_ref[...].astype(BF16), preferred_element_type=F32) + b_ref[...]

    return _pcall(
        body,
        name="ada_fwd",
        out_shape=jax.ShapeDtypeStruct((r, n), F32),
        grid=(n // tn,),
        in_specs=[pl.BlockSpec((r, d), lambda j: (0, 0)), pl.BlockSpec((d, tn), lambda j: (0, j)),
                  pl.BlockSpec((1, tn), lambda j: (0, j))],
        out_specs=pl.BlockSpec((r, tn), lambda j: (0, j)),
        compiler_params=_cparams(("parallel",)),
    )(conds, w_ada, b_ada_shard)


def _cctx_partial(da16_shard, w_ada, c_ctx_row):
    d, n = w_ada.shape
    td = _tile(d, 512)

    def body(g_ref, w_ref, c_ref, o_ref):
        ds = lax.dot_general(g_ref[8:16, :].astype(BF16), w_ref[...].astype(BF16), _DIMS["nt"],
                             preferred_element_type=F32)
        cv = c_ref[...]
        sg = jax.nn.sigmoid(cv)
        o_ref[...] = ds * (sg * (1.0 + cv * (1.0 - sg)))

    return _pcall(
        body,
        name="cctx_partial",
        out_shape=jax.ShapeDtypeStruct((8, d), F32),
        grid=(d // td,),
        in_specs=[pl.BlockSpec((16, n), lambda j: (0, 0)), pl.BlockSpec((td, n), lambda j: (j, 0)),
                  pl.BlockSpec((1, td), lambda j: (0, j))],
        out_specs=pl.BlockSpec((8, td), lambda j: (0, j)),
        compiler_params=_cparams(("parallel",)),
    )(da16_shard, w_ada, c_ctx_row)


def _sum_parts(parts):
    p, _, n = parts.shape

    def body(p_ref, o_ref):
        acc = p_ref[0]
        for s in range(1, p):
            acc = acc + p_ref[s]
        o_ref[...] = acc

    return _pcall(
        body,
        name="sum_parts",
        out_shape=jax.ShapeDtypeStruct((1, n), F32),
        in_specs=[pl.BlockSpec(memory_space=pltpu.VMEM)],
        out_specs=pl.BlockSpec(memory_space=pltpu.VMEM),
    )(parts)


def _adam_math(w, g, m, v):
    m2 = ADAM_B1 * m + (1.0 - ADAM_B1) * g
    v2 = ADAM_B2 * v + (1.0 - ADAM_B2) * jnp.square(g)
    m_hat = m2 / (1.0 - ADAM_B1 ** ADAM_STEP)
    v_hat = v2 / (1.0 - ADAM_B2 ** ADAM_STEP)
    delta = -ADAM_LR * (m_hat / (jnp.sqrt(v_hat) + ADAM_EPS) + ADAM_WD * w)
    return delta, m2, v2


def _adamw(parts, w, m, v, name):
    p, r, c = parts.shape
    block_elems = 1 << 18
    rb, cb = _tile(r, max(8, block_elems // c // 8 * 8), 8), c
    if rb * c < block_elems // 4 and r * c > block_elems:
        rb, cb = r, _tile(c, max(LANE, block_elems // r // LANE * LANE))

    def body(p_ref, w_ref, m_ref, v_ref, g_ref, d_ref, m2_ref, v2_ref):
        g = p_ref[0].astype(F32)
        for s in range(1, p):
            g = g + p_ref[s].astype(F32)
        g_ref[...] = g
        d_ref[...], m2_ref[...], v2_ref[...] = _adam_math(w_ref[...], g, m_ref[...], v_ref[...])

    if w.ndim == 3:
        blk = pl.BlockSpec((None, rb, cb), lambda i, j: (0, i, j))
    else:
        blk = pl.BlockSpec((rb, cb), lambda i, j: (i, j))
    return _pcall(
        body,
        name=name,
        out_shape=[jax.ShapeDtypeStruct(w.shape, F32)] * 4,
        grid=(r // rb, c // cb),
        in_specs=[pl.BlockSpec((p, rb, cb), lambda i, j: (0, i, j)), blk, blk, blk],
        out_specs=[blk] * 4,
        compiler_params=_cparams(("parallel", "parallel")),
    )(parts, w, m, v)


def _adamw_ada(conds, da16, w, m, v):
    d, n = w.shape
    rb = _tile(d, 256, LANE)

    def body(s_ref, da_ref, w_ref, m_ref, v_ref, g_ref, d_ref, m2_ref, v2_ref):
        g = lax.dot_general(_silu(s_ref[...]).astype(BF16), da_ref[...].astype(BF16), _DIMS["tn"],
                            preferred_element_type=F32)
        g_ref[...] = g
        d_ref[...], m2_ref[...], v2_ref[...] = _adam_math(w_ref[...], g, m_ref[...], v_ref[...])

    row = pl.BlockSpec((rb, n), lambda i: (i, 0))
    return _pcall(
        body,
        name="adamw_w_ada",
        out_shape=[jax.ShapeDtypeStruct((d, n), F32)] * 4,
        grid=(d // rb,),
        in_specs=[pl.BlockSpec((16, rb), lambda i: (0, i)), pl.BlockSpec((16, n), lambda i: (0, 0)), row, row, row],
        out_specs=[row] * 4,
        compiler_params=_cparams(("parallel",)),
    )(conds, da16, w, m, v)


def _cast_bf16(a, name):
    _, r, c = a.shape
    rb, cb = _tile(r, 512, 8), c
    if rb < 64 < r:
        rb, cb = r, _tile(c, 512)

    def body(a_ref, o_ref):
        o_ref[...] = a_ref[...].astype(BF16)

    return _pcall(body, name=name, out_shape=jax.ShapeDtypeStruct((r, c), BF16), grid=(r // rb, c // cb),
                  in_specs=[pl.BlockSpec((None, rb, cb), lambda i, j: (0, i, j))],
                  out_specs=pl.BlockSpec((rb, cb), lambda i, j: (i, j)),
                  compiler_params=_cparams(("parallel", "parallel")))(a)


def _rope_tabs(t, rot):
    half, q = rot // 2, rot // 4
    n_rows = t // GRID_W
    row = jnp.repeat(jnp.arange(n_rows, dtype=F32), GRID_W)
    col = jnp.tile(jnp.arange(GRID_W, dtype=F32), n_rows)
    inv_freq = ROPE_THETA ** (-jnp.arange(0, half, 2, dtype=F32) / half)
    ang = jnp.concatenate([row[:, None] * inv_freq, col[:, None] * inv_freq], axis=-1)
    cos, sin = jnp.cos(ang), jnp.sin(ang)
    c0, c1, s0, s1 = cos[:, :q], cos[:, q:], sin[:, :q], sin[:, q:]
    z = jnp.zeros_like(s0)
    return (jnp.concatenate([c0, c0, c1, c1], -1), jnp.concatenate([-s0, z, -s1, z], -1),
            jnp.concatenate([z, s0, z, s1], -1))


def _pad_cols(a, left, total, fill=0.0):
    return jnp.pad(a, ((0, 0), (left, total - left - a.shape[1])), constant_values=fill)


def _with_ctx_rows(tab, tc, fill):
    return jnp.concatenate([tab, jnp.full((tc, tab.shape[1]), fill, F32)], axis=0)


def kernel(x, c, ctx, c_ctx, w_ada, b_ada, norm1_g, w_in, mla_q_norm_g, w_q_up, mla_kv_norm_g, w_kv_up, gqa_q_norm_g, gqa_k_norm_g, w_br_a, w_br_b, w_out, norm2_g, w_up, conv_w, conv_b, w_down, final_norm_g, loss_target, m_c_ctx, m_w_ada, m_b_ada, m_norm1_g, m_w_in, m_mla_q_norm_g, m_w_q_up, m_mla_kv_norm_g, m_w_kv_up, m_gqa_q_norm_g, m_gqa_k_norm_g, m_w_br_a, m_w_br_b, m_w_out, m_norm2_g, m_w_up, m_conv_w, m_conv_b, m_w_down, m_final_norm_g, v_c_ctx, v_w_ada, v_b_ada, v_norm1_g, v_w_in, v_mla_q_norm_g, v_w_q_up, v_mla_kv_norm_g, v_w_kv_up, v_gqa_q_norm_g, v_gqa_k_norm_g, v_w_br_a, v_w_br_b, v_w_out, v_norm2_g, v_w_up, v_conv_w, v_conv_b, v_w_down, v_final_norm_g):
    weights = dict(c_ctx=c_ctx, w_ada=w_ada, b_ada=b_ada, norm1_g=norm1_g, w_in=w_in, mla_q_norm_g=mla_q_norm_g,
                   w_q_up=w_q_up, mla_kv_norm_g=mla_kv_norm_g, w_kv_up=w_kv_up, gqa_q_norm_g=gqa_q_norm_g,
                   gqa_k_norm_g=gqa_k_norm_g, w_br_a=w_br_a, w_br_b=w_br_b, w_out=w_out, norm2_g=norm2_g, w_up=w_up,
                   conv_w=conv_w, conv_b=conv_b, w_down=w_down, final_norm_g=final_norm_g)
    mom_m = dict(c_ctx=m_c_ctx, w_ada=m_w_ada, b_ada=m_b_ada, norm1_g=m_norm1_g, w_in=m_w_in, mla_q_norm_g=m_mla_q_norm_g,
                 w_q_up=m_w_q_up, mla_kv_norm_g=m_mla_kv_norm_g, w_kv_up=m_w_kv_up, gqa_q_norm_g=m_gqa_q_norm_g,
                 gqa_k_norm_g=m_gqa_k_norm_g, w_br_a=m_w_br_a, w_br_b=m_w_br_b, w_out=m_w_out, norm2_g=m_norm2_g,
                 w_up=m_w_up, conv_w=m_conv_w, conv_b=m_conv_b, w_down=m_w_down, final_norm_g=m_final_norm_g)
    mom_v = dict(c_ctx=v_c_ctx, w_ada=v_w_ada, b_ada=v_b_ada, norm1_g=v_norm1_g, w_in=v_w_in, mla_q_norm_g=v_mla_q_norm_g,
                 w_q_up=v_w_q_up, mla_kv_norm_g=v_mla_kv_norm_g, w_kv_up=v_w_kv_up, gqa_q_norm_g=v_gqa_q_norm_g,
                 gqa_k_norm_g=v_gqa_k_norm_g, w_br_a=v_w_br_a, w_br_b=v_w_br_b, w_out=v_w_out, norm2_g=v_norm2_g,
                 w_up=v_w_up, conv_w=v_conv_w, conv_b=v_conv_b, w_down=v_w_down, final_norm_g=v_final_norm_g)
    order = list(weights)

    my_idx = 4 * lax.axis_index("x") + 2 * lax.axis_index("y") + lax.axis_index("c")
    xs, cts, tgt = x[0], ctx[0], loss_target[0]
    t, d = xs.shape
    tc = cts.shape[0]
    ta = t + tc
    kvl, ql = MLA_KV_LORA, MLA_Q_LORA
    nb = GQA_KV_HEADS * GQA_HEAD_DIM
    hb = GQA_HEADS * GQA_HEAD_DIM
    ha = MLA_HEADS
    f2 = w_up.shape[2] * N_DEV
    ff = f2 // 2

    big = ["w_in", "w_q_up", "w_kv_up", "w_br_a", "w_br_b", "w_out", "w_up", "w_down"]
    nw = len(big)
    del nw
    _ORDER_AFTER.clear()
    narrow = ("w_in", "w_q_up")

    def tview(a):
        return jnp.transpose(a, (0, 2, 1))

    shards = {"w_in": _cast_bf16(tview(weights["w_in"]), "cast_w_in")}
    c_idx = jnp.reshape(lax.axis_index("c"), (1,)).astype(jnp.int32)

    def gather_start(names, dep):
        shs = [shards[n] for n in names]
        land = [lax.empty((N_DEV,) + s.shape, BF16) for s in shs]
        if dep is not None:
            _after(dep)
        s, r, arrs, tok = _split_start("gather_ici_start_" + names[0], shs + land, _gather_ici_copies(len(names)),
                                       4 * len(names))
        return dict(names=names, s=s, r=r, arrs=arrs, tok=tok)

    def gather_pass(g, after):
        n = len(g["names"])
        arrs = _split_wait("gather_ici_wait_" + g["names"][0], g["s"], g["r"], g["arrs"], _gather_ici_copies(n), after)
        s, r, bufs, tok = _split_start("gather_pass_start_" + g["names"][0], arrs[n:], _gather_pass_copies(n), 3 * n)
        g.update(s2=s, r2=r, bufs=bufs)
        return tok

    def gather_relay(g, after):
        n = len(g["names"])
        bufs = _split_wait("gather_pass_wait_" + g["names"][0], g["s2"], g["r2"], g["bufs"], _gather_pass_copies(n), after)
        s, r, bufs, tok = _split_start("gather_d2d_start_" + g["names"][0], bufs, _gather_d2d_copies(n), n)
        g.update(s3=s, r3=r, bufs=bufs)
        return tok

    def gather_finish(g, after):
        n = len(g["names"])
        bufs = _split_wait("gather_d2d_wait_" + g["names"][0], g["s3"], g["r3"], g["bufs"], _gather_d2d_copies(n), after)
        return dict(zip(g["names"], bufs))

    g0 = gather_start(["w_in"], None)

    _after(g0["tok"])
    c_all, cw_all = _all_gather([jnp.pad(c, ((0, 7), (0, 0))), jnp.pad(conv_w[0], ((0, 5), (0, 0)))], "gather_cond")
    conv_w_f = jnp.transpose(cw_all[:, :3, :], (1, 0, 2)).reshape(3, f2)
    conds = jnp.concatenate([c_all[:, 0, :], c_ctx[None, :], jnp.zeros((7, d), F32)], axis=0)
    ncol = w_ada.shape[2]
    b_shard = lax.dynamic_slice_in_dim(b_ada, my_idx * ncol, ncol, axis=1)
    ada_shard = _ada_fwd(conds, w_ada[0], b_shard)
    (ada_all,) = _all_gather([ada_shard], "gather_ada")
    ada = jnp.transpose(ada_all, (1, 0, 2)).reshape(16, N_DEV * ncol)
    lat = lax.dynamic_slice_in_dim(ada, my_idx, 1, axis=0).reshape(6, d)
    cxt = ada[8].reshape(6, d)
    zero2 = jnp.zeros((2, d), F32)
    mods1 = jnp.concatenate([lat[0:2], cxt[0:2], jnp.zeros((4, d), F32)], axis=0)
    mods2 = jnp.concatenate([lat[2:3], lat[3:4], lat[4:5], jnp.zeros((5, d), F32)], axis=0)
    mods2b = jnp.concatenate([lat[2:3], lat[4:5], jnp.zeros((6, d), F32)], axis=0)
    mods3 = jnp.concatenate([lat[5:6], jnp.zeros((7, d), F32)], axis=0)
    del zero2

    for n in big[1:]:
        _after(ada_all)
        shards[n] = _cast_bf16(tview(weights[n]) if n in narrow else weights[n], "cast_" + n)

    ca, s1a, s2a = _rope_tabs(t, MLA_ROPE)
    cb_, s1b, s2b = _rope_tabs(t, GQA_HEAD_DIM)
    q_tabs_a = (_pad_cols(jnp.concatenate([jnp.ones((t, MLA_NOPE), F32), ca], 1), 0, MLA_SLOT),
                _pad_cols(s1a, MLA_NOPE, MLA_SLOT), _pad_cols(s2a, MLA_NOPE, MLA_SLOT))
    q_tabs_b = (cb_, s1b, s2b)
    k_tabs = (_with_ctx_rows(_pad_cols(ca, 0, LANE), tc, 1.0), _with_ctx_rows(_pad_cols(s1a, 0, LANE), tc, 0.0),
              _with_ctx_rows(_pad_cols(s2a, 0, LANE), tc, 0.0),
              _with_ctx_rows(cb_, tc, 1.0), _with_ctx_rows(s1b, tc, 0.0), _with_ctx_rows(s2b, tc, 0.0))

    def cols_full(g):
        return jnp.transpose(g, (1, 0, 2)).reshape(g.shape[1], N_DEV * g.shape[2])

    _after(*q_tabs_a, *q_tabs_b, *k_tabs, *[shards[n] for n in big[1:]])
    tok_p0 = gather_pass(g0, mods1)
    g1 = gather_start(["w_q_up", "w_kv_up", "w_br_a", "w_br_b", "w_out"], tok_p0)
    _after(g1["tok"])
    z_all = _norm_mod_fwd(cts, xs, norm1_g, mods1)
    gathered = gather_finish(g0, gather_relay(g0, z_all))
    wt_in = gathered["w_in"].reshape(-1, d)
    o_kpe, o_kb, o_vb = kvl, kvl + MLA_ROPE, kvl + MLA_ROPE + nb
    o_q = o_vb + nb
    o_g = o_q + ql + hb
    wkv_w = kvl + 2 * nb + LANE
    wt_kv_p = jnp.concatenate([wt_in[:kvl], wt_in[o_kb:o_q], wt_in[o_kpe:o_kb],
                               jnp.zeros((LANE - MLA_ROPE, d), BF16)], axis=0)
    q_w = ql + hb
    q_pad = (-q_w) % 512 if d >= 512 else (-q_w) % d
    gate_blk = (q_w + q_pad) // d
    assert (q_w + q_pad) % d == 0
    wt_qg_p = jnp.concatenate([wt_in[o_q:o_g], jnp.zeros((q_pad, d), BF16), wt_in[o_g:]], axis=0)

    kv_all = _mm(z_all, wt_kv_p, "nt", F32, "proj_kv", tm=1152, tn=wkv_w)
    qg = _mm(z_all, wt_qg_p, "nt", F32, "proj_qg", tm=1024, tn=1024, rows=t)
    tok_p1 = gather_pass(g1, qg)
    g2 = gather_start(["w_up"], tok_p1)
    g3 = gather_start(["w_down"], g2["tok"])
    _after(g3["tok"])
    kin, k_b, v_b = _key_prep_fwd(kv_all, mla_kv_norm_g, gqa_k_norm_g, k_tabs)
    sc_a = float((MLA_NOPE + MLA_ROPE) ** -0.5) * LOG2E
    sc_b = float(GQA_HEAD_DIM ** -0.5) * LOG2E
    _after(g3["tok"])
    cqn, q_b = _q_prep_fwd(qg, mla_q_norm_g, gqa_q_norm_g, q_tabs_b, sc_b)
    _after(kin, g3["tok"])
    gathered.update(gather_finish(g1, gather_relay(g1, q_b)))

    wqt_f = gathered["w_q_up"].reshape(ha, MLA_NOPE + MLA_ROPE, ql)
    wqt_ext = jnp.pad(wqt_f, ((0, 0), (0, MLA_SLOT - MLA_NOPE - MLA_ROPE), (0, 0))).reshape(ha * MLA_SLOT, ql)
    wkv_f = cols_full(gathered["w_kv_up"]).reshape(kvl, ha, MLA_NOPE + MLA_V)
    wk_slots = jnp.pad(wkv_f[:, :, :MLA_NOPE], ((0, 0), (0, 0), (0, MLA_SLOT - MLA_NOPE))).reshape(kvl, ha * MLA_SLOT)
    wv_cols = wkv_f[:, :, MLA_NOPE:].reshape(kvl, ha * MLA_V)
    e_slot = jnp.pad(jnp.eye(MLA_ROPE, dtype=BF16),
                     ((0, LANE - MLA_ROPE), (MLA_NOPE, MLA_SLOT - MLA_NOPE - MLA_ROPE)))
    e_rows = jnp.concatenate([jnp.tile(e_slot, (1, ha)), jnp.zeros((LANE, ha * MLA_V), BF16)], axis=1)
    wkv_ext = jnp.concatenate([jnp.concatenate([wk_slots, wv_cols], axis=1), e_rows], axis=0)
    w_bra = cols_full(gathered["w_br_a"])
    w_brb = cols_full(gathered["w_br_b"])
    w_out_f = gathered["w_out"].reshape(d, d)

    kv_a = _mm(kin, wkv_ext, "nn", BF16, "kv_up", tm=1152, tn=1024)
    qa_raw = _mm(cqn, wqt_ext, "nt", F32, "q_up", tm=1024, tn=1024)
    q_a = _rope_a(qa_raw, q_tabs_a, False, BF16, "rope_q_fwd", sc_a)
    att_a = dict(hq=ha, hkv=ha, dk=MLA_SLOT, dv=MLA_V, k_blk0=0, v_blk0=ha * MLA_SLOT // MLA_V)
    att_b = dict(hq=GQA_HEADS, hkv=GQA_KV_HEADS, dk=GQA_HEAD_DIM, dv=GQA_HEAD_DIM, k_blk0=0, v_blk0=0)
    o_a, lse_a = _attention_fwd(q_a, kv_a, kv_a, name="attn_a_fwd", **att_a)
    o_b, lse_b = _attention_fwd(q_b, k_b, v_b, name="attn_b_fwd", **att_b)
    _after(o_a)
    _after(gather_pass(g2, o_b))
    pa = _mm(o_a, w_bra, "nn", BF16, "br_a", tm=1024, tn=1024)
    pb = _mm(o_b, w_brb, "nn", BF16, "br_b", tm=1024, tn=1024)
    merged = _merge_fwd(pa, pb, qg, gate_blk)
    attn = _mm(merged, w_out_f, "nn", F32, "w_out", tm=1024, tn=1024)
    x1, z2 = _resid_norm_mod(xs, attn, norm2_g, mods2, "resid_norm2_fwd")
    tok_r2 = gather_relay(g2, z2)
    tok_p3 = gather_pass(g3, tok_r2)
    w_up3 = gather_finish(g2, tok_p3)["w_up"]
    u = _mm_up_fwd(z2, w_up3, "w_up")
    tok_r3 = gather_relay(g3, u)
    _after(tok_r3)
    h, uc = _conv_fwd(u, conv_w_f, conv_b)
    w_down_f = gather_finish(g3, h)["w_down"].reshape(ff, d)
    ffn = _mm(h, w_down_f, "nn", F32, "w_down", tm=1024, tn=1024, tk=2816)

    def to_shards(g):
        return jnp.transpose(g.reshape(g.shape[0], N_DEV, g.shape[1] // N_DEV), (1, 0, 2))

    def reduce_start(tag, names, sends):
        n = len(sends)
        land = [lax.empty((4,) + s.shape[1:], s.dtype) for s in sends]
        s, r, arrs, tok = _split_start("reduce_d2d_start_" + tag, sends + land, _reduce_d2d_copies(n), 4 * n)
        return dict(tag=tag, names=names, s=s, r=r, arrs=arrs, tok=tok)

    def reduce_relay(g, after):
        n = len(g["names"])
        arrs = _split_wait("reduce_d2d_wait_" + g["tag"], g["s"], g["r"], g["arrs"], _reduce_d2d_copies(n), after)
        sums = [_pair_sum(arrs[a], arrs[n + a], c_idx, "pair_sum_" + g["names"][a]) for a in range(n)]
        land = [lax.empty(s.shape, s.dtype) for s in sums]
        s, r, arrs2, tok = _split_start("reduce_ici_start_" + g["tag"], sums + land, _reduce_ici_copies(n), 4 * n)
        g.update(s2=s, r2=r, arrs2=arrs2)
        return tok

    def reduce_finish(g, after):
        n = len(g["names"])
        arrs2 = _split_wait("reduce_ici_wait_" + g["tag"], g["s2"], g["r2"], g["arrs2"], _reduce_ici_copies(n), after)
        return dict(zip(g["names"], arrs2[n:]))

    dx2, dffn, st_fin = _final_loss(x1, ffn, final_norm_g[None, :], mods3, tgt)
    dh = _mm(dffn, w_down_f, "nt", BF16, "d_h", tm=1024, tn=1024)
    g_w_down = _mm(h, dffn, "tn", BF16, "g_w_down", tm=512, tn=1024)
    r_down = reduce_start("down", ["w_down"], [g_w_down.reshape(N_DEV, ff // N_DEV, d)])
    _after(r_down["tok"])
    du3, dcw, dcb = _conv_bwd(u, uc, conv_w_f, dh)
    dz2 = _mm_up_dz(du3, w_up3, "d_z2")
    g_w_up = _mm_up_gw(z2, du3, N_DEV, "g_w_up")
    g_conv_w = jnp.concatenate([dcw[0], dcw[1]], axis=1)
    tok = reduce_relay(r_down, g_w_up)
    _after(tok)
    r_up = reduce_start("up", ["w_up", "conv_w"], [g_w_up, to_shards(jnp.pad(g_conv_w, ((0, 5), (0, 0))))])
    _after(tok, r_up["tok"])
    dx1, dattn, st_n2 = _norm2_bwd(x1, attn, norm2_g, mods2b, dz2, dx2)
    dmerged = _mm(dattn, w_out_f, "nt", BF16, "d_merged", tm=1024, tn=1024)
    g_w_out = _mm(merged, dattn, "tn", BF16, "g_w_out", tm=1024, tn=1024)
    dpa, dpb, dgates = _merge_bwd(dmerged, pa, pb, qg, gate_blk)
    do_a = _mm(dpa, w_bra, "nt", BF16, "d_o_a", tm=1024, tn=1024)
    do_b = _mm(dpb, w_brb, "nt", BF16, "d_o_b", tm=1024, tn=1024)
    g_w_bra = _mm(o_a, dpa, "tn", BF16, "g_w_br_a", tm=1024, tn=1024)
    g_w_brb = _mm(o_b, dpb, "tn", BF16, "g_w_br_b", tm=1024, tn=1024)
    tok = reduce_relay(r_up, g_w_brb)
    _after(tok)
    r_out = reduce_start("out", ["w_out", "w_br_a", "w_br_b"],
                         [g_w_out.reshape(N_DEV, d // N_DEV, d), to_shards(g_w_bra), to_shards(g_w_brb)])
    _after(tok, r_out["tok"])
    dq_a, dk_a, dv_a = _attention_bwd(q_a, kv_a, kv_a, do_a, lse_a, name="attn_a_bwd", **att_a)
    dq_b, dk_b, dv_b = _attention_bwd(q_b, k_b, v_b, do_b, lse_b, name="attn_b_bwd", **att_b)
    _after(reduce_relay(r_out, dv_b))
    dqa_raw = _rope_a(dq_a, q_tabs_a, True, BF16, "rope_q_bwd", sc_a * LN2)
    dcqn = _mm(dqa_raw, wqt_ext, "nn", F32, "d_cqn", tm=1024, tn=ql)
    g_wqt_ext = _mm(dqa_raw, cqn, "tn", BF16, "g_w_q_up", tm=1024, tn=ql)
    dq_p, st_q, st_qb = _q_prep_bwd(qg, mla_q_norm_g, gqa_q_norm_g, q_tabs_b, dcqn, dq_b, q_pad, sc_b * LN2)
    dkin = _mm_cat_nt([(dk_a, wkv_ext, 0), (dv_a, wkv_ext, ha * MLA_SLOT)], F32, "d_kin", tm=1152, tn=kvl + LANE)
    g_wkv_ext = _mm_cat_tn(kin, [dk_a, dv_a], BF16, "g_w_kv_up", tm=kvl + LANE, tn=min(1024, ha * MLA_V))
    dkv_p, st_kv, st_kb = _key_prep_bwd(kv_all, mla_kv_norm_g, gqa_k_norm_g, k_tabs, dkin, dk_b, dv_b)
    g_wqt = g_wqt_ext.reshape(ha, MLA_SLOT, ql)[:, :MLA_NOPE + MLA_ROPE, :].reshape(N_DEV, -1, ql)
    g_wkv = jnp.concatenate([g_wkv_ext[:kvl, :ha * MLA_SLOT].reshape(kvl, ha, MLA_SLOT)[:, :, :MLA_NOPE],
                             g_wkv_ext[:kvl, ha * MLA_SLOT:].reshape(kvl, ha, MLA_V)], axis=2).reshape(kvl, ha * (MLA_NOPE + MLA_V))
    r_qkv = reduce_start("qkv", ["w_q_up", "w_kv_up"], [g_wqt, to_shards(g_wkv)])
    _after(r_qkv["tok"])
    g_wkv_p = _mm(dkv_p, z_all, "tn", BF16, "g_w_in_kv", tm=wkv_w, tn=1024)
    g_wqg_p = _mm_rows_tn([dq_p, dgates], z_all, BF16, "g_w_in_qg", tm=min(1024, d), tn=1024, rows=t)
    g_wt_in = jnp.concatenate([g_wkv_p[:kvl], g_wkv_p[kvl + 2 * nb:kvl + 2 * nb + MLA_ROPE],
                               g_wkv_p[kvl:kvl + 2 * nb], g_wqg_p[:q_w], g_wqg_p[q_w + q_pad:]], axis=0)
    r_in = reduce_start("in", ["w_in"], [g_wt_in.reshape(N_DEV, -1, d)])
    _after(r_in["tok"])
    qw_p = q_w + q_pad
    dz_lat = _mm_sum_nn([(dq_p, 0, wt_qg_p, 0, qw_p), (dgates, 0, wt_qg_p, qw_p, d), (dgates, d, wt_qg_p, qw_p + d, d),
                         (dkv_p, 0, wt_kv_p, 0, wkv_w)], F32, "d_z_lat", rows=t)
    dz_ctx = _mm(dkv_p, wt_kv_p, "nn", F32, "d_z_ctx", tm=min(ROW_BLOCK, tc), tn=1024, a_row_off=t)
    tok_q = reduce_relay(r_qkv, dz_ctx)
    _after(tok_q)
    grad_x, st_n1 = _norm1_bwd(cts, xs, norm1_g, mods1, dz_ctx, dz_lat, dx1)

    res = {}

    def upd(nm, parts):
        wv, mv, vv = weights[nm], mom_m[nm], mom_v[nm]
        if wv.ndim == 1:
            wv, mv, vv = (a.reshape(1, -1) for a in (wv, mv, vv))
        if nm in narrow:
            wv, mv, vv = tview(wv), tview(mv), tview(vv)
        outs = _adamw(parts, wv, mv, vv, "adamw_" + nm)
        if nm in narrow:
            outs = [tview(o_) for o_ in outs]
        res[nm] = [o_.reshape(weights[nm].shape) for o_ in outs]

    d_lat = jnp.concatenate([st_n1[0], st_n1[1], st_n2[3], st_n2[0], st_n2[1], st_fin[1]])
    d_cxt = jnp.concatenate([st_n1[3], st_n1[4], jnp.zeros((4 * d,), F32)])
    small = jnp.concatenate([d_lat, d_cxt, st_n1[2], st_q[0], st_kv[0], st_qb[0], st_kb[0], st_n2[2],
                             jnp.concatenate([dcb[0, 0], dcb[1, 0]]), st_fin[0], st_fin[3, :LANE]])
    n_small = small.shape[0]
    pad_small = (-n_small) % LANE
    (small_all,) = _all_gather([jnp.pad(small, (0, pad_small)).reshape(1, -1)], "gather_small")
    offs = {}
    o = 0
    for nm, ln in (("d_lat", 6 * d), ("d_cxt", 6 * d), ("norm1_g", d), ("mla_q_norm_g", ql), ("mla_kv_norm_g", kvl),
                   ("gqa_q_norm_g", GQA_HEAD_DIM), ("gqa_k_norm_g", GQA_HEAD_DIM), ("norm2_g", d), ("conv_b", f2),
                   ("final_norm_g", d), ("loss", LANE)):
        offs[nm] = (o, ln)
        o += ln

    def part(nm):
        a, ln = offs[nm]
        return small_all[:, :, a:a + ln]

    loss = _sum_parts(part("loss"))[0, 0]
    d_lat_all = part("d_lat")[:, 0, :]
    d_cxt_sum = _sum_parts(part("d_cxt"))
    da16 = jnp.concatenate([d_lat_all, d_cxt_sum, jnp.zeros((7, 6 * d), F32)], axis=0)
    da16_shard = lax.dynamic_slice_in_dim(da16, my_idx * ncol, ncol, axis=1)
    cc_part = _cctx_partial(da16_shard, w_ada[0], c_ctx[None, :])
    (cc_all,) = _all_gather([cc_part], "gather_cctx")
    cc_parts = cc_all[:, 0:1, :]
    tok_i = reduce_relay(r_in, cc_all)

    _after(tok_i)
    for nm in ("norm1_g", "mla_q_norm_g", "mla_kv_norm_g", "gqa_q_norm_g", "gqa_k_norm_g", "norm2_g", "conv_b",
               "final_norm_g"):
        upd(nm, part(nm))
    upd("c_ctx", cc_parts)
    b_parts = jnp.concatenate([d_lat_all[:, None, :], d_cxt_sum[None]], axis=0)
    upd("b_ada", b_parts)
    _after(tok_i)
    outs = _adamw_ada(conds, da16_shard, w_ada[0], m_w_ada[0], v_w_ada[0])
    res["w_ada"] = [o_[None] for o_ in outs]
    last = outs[0]
    done = [last]
    for grp in (r_down, r_up, r_out, r_qkv, r_in):
        _after(*done)
        recv = reduce_finish(grp, last)
        for nm in grp["names"]:
            upd(nm, recv[nm][:, :3, :] if nm == "conv_w" else recv[nm])
            last = res[nm][0]
            done.append(last)

    return (loss, grad_x[None], *[res[n][0] for n in order], *[res[n][1] for n in order],
            *[res[n][2] for n in order], *[res[n][3] for n in order])
```

```python
import functools

import jax
import jax.numpy as jnp
from jax import lax
from jax.experimental import pallas as pl
from jax.experimental.pallas import tpu as pltpu

F32 = jnp.float32
BF16 = jnp.bfloat16

GRID_W = 64
ROPE_THETA = 10000.0
NORM_EPS = 1e-6
MLA_HEADS = 8
MLA_Q_LORA = 768
MLA_KV_LORA = 512
MLA_NOPE = 128
MLA_ROPE = 64
MLA_V = 128
GQA_HEADS = 8
GQA_KV_HEADS = 2
GQA_HEAD_DIM = 128
ADAM_LR = 0.001
ADAM_B1 = 0.9
ADAM_B2 = 0.999
ADAM_EPS = 1e-08
ADAM_WD = 0.01
ADAM_STEP = 10

N_DEV = 8
MESH_AXES = ("x", "y", "c")
LANE = 128
MLA_SLOT = 2 * LANE
VMEM_LIMIT = 56 * 1024 * 1024
ROW_BLOCK = 256
ATT_Q_BLOCK = 512
ATT_Q_BLOCK_FWD = 512
LN2 = 0.6931471805599453
LOG2E = 1.4426950408889634
MESH_ID = pl.DeviceIdType.MESH


def _tile(n, pref, align=LANE):
    if n <= pref:
        return n
    best = None
    t = align
    while t <= pref:
        if n % t == 0:
            best = t
        t += align
    assert best is not None, (n, pref, align)
    return best


def _cparams(sem=None):
    return pltpu.CompilerParams(dimension_semantics=sem, vmem_limit_bytes=VMEM_LIMIT)


_ORDER_AFTER = []


def _after(*arrays):
    _ORDER_AFTER.extend(arrays)


def _pcall(body, *, in_specs, **kw):
    deps = tuple(_ORDER_AFTER)
    _ORDER_AFTER.clear()
    if not deps:
        return pl.pallas_call(body, in_specs=in_specs, **kw)
    n_in, n_dep = len(in_specs), len(deps)

    def with_deps(*refs):
        body(*refs[:n_in], *refs[n_in + n_dep:])

    call = pl.pallas_call(with_deps, in_specs=list(in_specs) + [pl.BlockSpec(memory_space=pl.ANY)] * n_dep, **kw)
    return lambda *args: call(*args, *deps)


def _all_gather(arrs, name):
    n = len(arrs)

    def body(*refs):
        ins = refs[:n]
        outs = refs[n:2 * n]
        send_sems, recv_sems, local_sems = refs[2 * n:]
        x, y, c = lax.axis_index("x"), lax.axis_index("y"), lax.axis_index("c")
        me, sibling = (x, y, c), (x, y, 1 - c)
        chips = [(1 - x, y), (x, 1 - y), (1 - x, 1 - y)]

        def rows(a, dev):
            px, py, pc = dev
            return outs[a].at[4 * px + 2 * py + pc]

        def copy(a, k, block, to, src=None):
            return pltpu.make_async_remote_copy(
                src_ref=rows(a, block) if src is None else src,
                dst_ref=rows(a, block),
                send_sem=send_sems.at[7 * a + k],
                recv_sem=recv_sems.at[7 * a + k],
                device_id=to,
                device_id_type=MESH_ID,
            )

        mine = [pltpu.make_async_copy(ins[a], rows(a, me), local_sems.at[a]) for a in range(n)]
        for cp in mine:
            cp.start()
        first = []
        for a in range(n):
            first.append(copy(a, 0, me, sibling, src=ins[a]))
            first += [copy(a, 1 + j, me, (*chip, c), src=ins[a]) for j, chip in enumerate(chips)]
        for cp in first:
            cp.start()
        passed = []
        for j, chip in enumerate(chips):
            for a in range(n):
                copy(a, 1 + j, (*chip, c), me).wait_recv()
                fwd = copy(a, 4 + j, (*chip, c), sibling)
                fwd.start()
                passed.append(fwd)
        for a in range(n):
            copy(a, 0, sibling, me).wait_recv()
            for j, chip in enumerate(chips):
                copy(a, 4 + j, (*chip, 1 - c), me).wait_recv()
        for cp in first + passed:
            cp.wait_send()
        for cp in mine:
            cp.wait()

    any_spec = pl.BlockSpec(memory_space=pl.ANY)
    outs = _pcall(
        body,
        name=name,
        out_shape=[jax.ShapeDtypeStruct((N_DEV,) + a.shape, a.dtype) for a in arrs],
        in_specs=[any_spec] * n,
        out_specs=[any_spec] * n,
        scratch_shapes=[
            pltpu.SemaphoreType.DMA((7 * n,)),
            pltpu.SemaphoreType.DMA((7 * n,)),
            pltpu.SemaphoreType.DMA((n,)),
        ],
    )(*arrs)
    return list(outs)


def _all_to_all(arrs, name):
    n = len(arrs)

    def body(*refs):
        ins = refs[:n]
        outs = refs[n:2 * n]
        send_sems, recv_sems, local_sems = refs[2 * n:]
        x, y, c = lax.axis_index("x"), lax.axis_index("y"), lax.axis_index("c")
        my_idx = 4 * x + 2 * y + c

        def peer(k):
            fx, fy, fc = (k >> 2) & 1, (k >> 1) & 1, k & 1
            return (x ^ fx if fx else x, y ^ fy if fy else y, c ^ fc if fc else c)

        def copy(a, k):
            px, py, pc = peer(k)
            return pltpu.make_async_remote_copy(
                src_ref=ins[a].at[4 * px + 2 * py + pc],
                dst_ref=outs[a].at[my_idx],
                send_sem=send_sems.at[7 * a + k - 1],
                recv_sem=recv_sems.at[7 * a + k - 1],
                device_id=(px, py, pc),
                device_id_type=MESH_ID,
            )

        mine = [pltpu.make_async_copy(ins[a].at[my_idx], outs[a].at[my_idx], local_sems.at[a]) for a in range(n)]
        for cp in mine:
            cp.start()
        order = [1, 4, 2, 5, 3, 6, 7]
        cps = [copy(a, k) for k in order for a in range(n)]
        for cp in cps:
            cp.start()
        for cp in cps:
            cp.wait()
        for cp in mine:
            cp.wait()

    any_spec = pl.BlockSpec(memory_space=pl.ANY)
    outs = _pcall(
        body,
        name=name,
        out_shape=[jax.ShapeDtypeStruct(a.shape, a.dtype) for a in arrs],
        in_specs=[any_spec] * n,
        out_specs=[any_spec] * n,
        scratch_shapes=[
            pltpu.SemaphoreType.DMA((7 * n,)),
            pltpu.SemaphoreType.DMA((7 * n,)),
            pltpu.SemaphoreType.DMA((n,)),
        ],
    )(*arrs)
    return list(outs)


_HBM = pl.BlockSpec(memory_space=pltpu.HBM)
_SEM = pl.BlockSpec(memory_space=pltpu.SEMAPHORE)
_EFFECT = pltpu.SideEffectType.DATAFLOW_SIDE_EFFECTING


def _descriptors(copies, send_sems, recv_sems):
    descs = []
    for i, (src, dst, dev) in enumerate(copies):
        if dev is None:
            descs.append(pltpu.make_async_copy(src, dst, recv_sems.at[i]))
        else:
            descs.append(pltpu.make_async_remote_copy(src_ref=src, dst_ref=dst, send_sem=send_sems.at[i],
                                                      recv_sem=recv_sems.at[i], device_id=dev, device_id_type=MESH_ID))
    return descs


def _split_start(name, arrays, copies_fn, n_copies):
    n = len(arrays)

    def body(*refs):
        send_sems, recv_sems = refs[n], refs[n + 1]
        token = refs[2 * n + 2]
        for dsc in _descriptors(copies_fn(refs[:n]), send_sems, recv_sems):
            dsc.start()
        token[...] = jnp.zeros_like(token)

    outs = _pcall(
        body,
        name=name,
        out_shape=(pltpu.SemaphoreType.DMA((n_copies,)), pltpu.SemaphoreType.DMA((n_copies,)),
                   *[pltpu.HBM(a.shape, a.dtype) for a in arrays], jax.ShapeDtypeStruct((8, LANE), F32)),
        in_specs=[_HBM] * n,
        out_specs=(_SEM, _SEM, *[_HBM] * n, pl.BlockSpec(memory_space=pltpu.VMEM)),
        input_output_aliases={i: 2 + i for i in range(n)},
        compiler_params=pltpu.CompilerParams(has_side_effects=_EFFECT),
    )(*[pltpu.with_memory_space_constraint(a, pltpu.HBM) for a in arrays])
    return outs[0], outs[1], list(outs[2:2 + n]), outs[2 + n]


def _split_wait(name, send_sems, recv_sems, arrays, copies_fn, after):
    n = len(arrays)

    def body(*refs):
        for dsc, (_, _, dev) in zip(_descriptors(copies_fn(refs[:n]), refs[n], refs[n + 1]), copies_fn(refs[:n])):
            if dev is None:
                dsc.wait()
            else:
                dsc.wait_send()
                dsc.wait_recv()

    outs = _pcall(
        body,
        name=name,
        out_shape=tuple(pltpu.HBM(a.shape, a.dtype) for a in arrays),
        in_specs=[_HBM] * n + [_SEM, _SEM, pl.BlockSpec(memory_space=pl.ANY)],
        out_specs=tuple([_HBM] * n),
        input_output_aliases={i: i for i in range(n)},
        compiler_params=pltpu.CompilerParams(has_side_effects=_EFFECT),
    )(*arrays, send_sems, recv_sems, after)
    return list(outs)


def _mesh_pos():
    x, y, c = lax.axis_index("x"), lax.axis_index("y"), lax.axis_index("c")
    return x, y, c, [(1 - x, y), (x, 1 - y), (1 - x, 1 - y)]


def _gather_ici_copies(n):
    def copies(refs):
        x, y, c, chips = _mesh_pos()
        me = 4 * x + 2 * y + c
        out = []
        for a in range(n):
            src, buf = refs[a], refs[n + a]
            out.append((src, buf.at[me], None))
            out.append((src, buf.at[me], (x, y, 1 - c)))
            out += [(src, buf.at[me], (cx, cy, c)) for cx, cy in chips[:2]]
        return out
    return copies


def _gather_pass_copies(n):
    def copies(refs):
        x, y, c, chips = _mesh_pos()
        south = c == 0
        bx, by = jnp.where(south, 1 - x, x), jnp.where(south, y, 1 - y)
        tx, ty = jnp.where(south, x, 1 - x), jnp.where(south, 1 - y, y)
        out = []
        for a in range(n):
            rows = refs[a].at[4 * bx + 2 * by + c]
            out.append((rows, rows, (tx, ty, c)))
            for cx, cy in chips[:2]:
                rows = refs[a].at[4 * cx + 2 * cy + c]
                out.append((rows, rows, (x, y, 1 - c)))
        return out
    return copies


def _gather_d2d_copies(n):
    def copies(refs):
        x, y, c, chips = _mesh_pos()
        cx, cy = chips[2]
        out = []
        for a in range(n):
            rows = refs[a].at[4 * cx + 2 * cy + c]
            out.append((rows, rows, (x, y, 1 - c)))
        return out
    return copies


def _reduce_d2d_copies(n):
    def copies(refs):
        x, y, c, _ = _mesh_pos()
        out = []
        for a in range(n):
            for k in range(4):
                out.append((refs[a].at[2 * k + (1 - c)], refs[n + a].at[k], (x, y, 1 - c)))
        return out
    return copies


def _reduce_ici_copies(n):
    def copies(refs):
        x, y, c, chips = _mesh_pos()
        mine = 2 * x + y
        out = []
        for a in range(n):
            src, land = refs[a], refs[n + a]
            out.append((src.at[mine], land.at[mine], None))
            out += [(src.at[2 * cx + cy], land.at[mine], (cx, cy, c)) for cx, cy in chips]
        return out
    return copies


def _pair_sum(send, land, c_idx, name):
    _, r, cols = send.shape
    rb = _tile(r, max(8, (1 << 22) // (send.dtype.itemsize * cols) // 8 * 8), 8)
    dt = send.dtype

    def body(c_ref, s_ref, l_ref, o_ref):
        o_ref[...] = (s_ref[...].astype(F32) + l_ref[...].astype(F32)).astype(dt)

    return pl.pallas_call(
        body,
        name=name,
        out_shape=jax.ShapeDtypeStruct((4, r, cols), dt),
        grid_spec=pltpu.PrefetchScalarGridSpec(
            num_scalar_prefetch=1,
            grid=(4, r // rb),
            in_specs=[pl.BlockSpec((None, rb, cols), lambda k, i, c_ref: (2 * k + c_ref[0], i, 0)),
                      pl.BlockSpec((None, rb, cols), lambda k, i, c_ref: (k, i, 0))],
            out_specs=pl.BlockSpec((None, rb, cols), lambda k, i, c_ref: (k, i, 0)),
        ),
        compiler_params=_cparams(("parallel", "parallel")),
    )(c_idx, send, land)


_DIMS = {
    "nn": (((1,), (0,)), ((), ())),
    "nt": (((1,), (1,)), ((), ())),
    "tn": (((0,), (0,)), ((), ())),
}


def _mm_call(a, b, *, mode, grid, a_spec, b_spec, o_spec, out_shape, acc_shape, name):
    nk = grid[2]
    out_dtype = out_shape.dtype

    def body(a_ref, b_ref, o_ref, *scratch):
        p = lax.dot_general(a_ref[...].astype(BF16), b_ref[...].astype(BF16), _DIMS[mode],
                            preferred_element_type=F32)
        if nk == 1:
            o_ref[...] = p.astype(out_dtype)
        else:
            acc = scratch[0]
            k = pl.program_id(2)

            @pl.when(k == 0)
            def _():
                acc[...] = p

            @pl.when(k > 0)
            def _():
                acc[...] += p

            @pl.when(k == nk - 1)
            def _():
                o_ref[...] = acc[...].astype(out_dtype)

    return _pcall(
        body,
        name=name,
        out_shape=out_shape,
        grid=grid,
        in_specs=[a_spec, b_spec],
        out_specs=o_spec,
        scratch_shapes=[pltpu.VMEM(acc_shape, F32)] if nk > 1 else [],
        compiler_params=_cparams(("parallel", "parallel", "arbitrary")),
    )(a, b)


def _mm(a, b, mode, out_dtype, name, tm=512, tn=512, tk=2432, a_row_off=0, rows=None):
    if mode == "nn":
        (m, k), (k2, n) = a.shape, b.shape
    elif mode == "nt":
        (m, k), (n, k2) = a.shape, b.shape
    else:
        (k, m), (k2, n) = a.shape, b.shape
        if rows is not None:
            k = k2 = rows
    assert k == k2, (a.shape, b.shape, mode)
    if mode != "tn":
        m = (m if rows is None else rows + a_row_off) - a_row_off
    tm, tn, tk = _tile(m, tm, 8), _tile(n, tn), _tile(k, tk, 8 if mode == "tn" else LANE)
    assert a_row_off % tm == 0
    ro = a_row_off // tm
    grid = (m // tm, n // tn, k // tk)
    if mode == "tn":
        a_spec = pl.BlockSpec((tk, tm), lambda i, j, kk: (kk, i))
    else:
        a_spec = pl.BlockSpec((tm, tk), lambda i, j, kk: (i + ro, kk))
    if mode == "nt":
        b_spec = pl.BlockSpec((tn, tk), lambda i, j, kk: (j, kk))
    else:
        b_spec = pl.BlockSpec((tk, tn), lambda i, j, kk: (kk, j))
    o_spec = pl.BlockSpec((tm, tn), lambda i, j, kk: (i, j))
    return _mm_call(a, b, mode=mode, grid=grid, a_spec=a_spec, b_spec=b_spec, o_spec=o_spec,
                    out_shape=jax.ShapeDtypeStruct((m, n), out_dtype), acc_shape=(tm, tn), name=name)


def _mm_cat_nt(pieces, out_dtype, name, tm=1024, tn=1024, tk=2048, rows=None):
    m = pieces[0][0].shape[0] if rows is None else rows
    n = pieces[0][1].shape[0]
    tm, tn = _tile(m, tm, 8), _tile(n, tn)
    steps, starts, s = [], [], 0
    for a, b, off in pieces:
        kp = a.shape[1]
        tkp = _tile(kp, tk)
        assert off % tkp == 0 and b.shape[0] == n
        steps.append((tkp, kp // tkp, off // tkp))
        starts.append(s)
        s += kp // tkp
    nk = s
    npc = len(pieces)

    def body(*refs):
        o_ref, acc = refs[2 * npc], refs[2 * npc + 1]
        kk = pl.program_id(2)

        @pl.when(kk == 0)
        def _():
            acc[...] = jnp.zeros_like(acc)

        for p in range(npc):
            @pl.when((kk >= starts[p]) & (kk < starts[p] + steps[p][1]))
            def _(p=p):
                acc[...] += lax.dot_general(refs[2 * p][...].astype(BF16), refs[2 * p + 1][...].astype(BF16), _DIMS["nt"],
                                            preferred_element_type=F32)

        @pl.when(kk == nk - 1)
        def _():
            o_ref[...] = acc[...].astype(out_dtype)

    in_specs, args = [], []
    for p, (a, b, off) in enumerate(pieces):
        tkp, np_, ob = steps[p]

        def rel(kk, p=p, np_=np_):
            return jnp.clip(kk - starts[p], 0, np_ - 1)

        in_specs.append(pl.BlockSpec((tm, tkp), lambda i, j, kk, rel=rel: (i, rel(kk))))
        in_specs.append(pl.BlockSpec((tn, tkp), lambda i, j, kk, rel=rel, ob=ob: (j, ob + rel(kk))))
        args += [a, b]
    return _pcall(
        body,
        name=name,
        out_shape=jax.ShapeDtypeStruct((m, n), out_dtype),
        grid=(m // tm, n // tn, nk),
        in_specs=in_specs,
        out_specs=pl.BlockSpec((tm, tn), lambda i, j, kk: (i, j)),
        scratch_shapes=[pltpu.VMEM((tm, tn), F32)],
        compiler_params=_cparams(("parallel", "parallel", "arbitrary")),
    )(*args)


def _mm_cat_tn(a, pieces, out_dtype, name, tm=1024, tn=1024, rows=None):
    k = a.shape[0] if rows is None else rows
    m = a.shape[1]
    tm = _tile(m, tm)
    starts, s = [], 0
    for b in pieces:
        assert b.shape[1] % tn == 0
        starts.append(s)
        s += b.shape[1] // tn
    nj = s
    npc = len(pieces)

    def body(*refs):
        a_ref, o_ref = refs[0], refs[1 + npc]
        j = pl.program_id(1)
        for p in range(npc):
            @pl.when((j >= starts[p]) & (j < starts[p] + pieces[p].shape[1] // tn))
            def _(p=p):
                o_ref[...] = lax.dot_general(a_ref[...].astype(BF16), refs[1 + p][...].astype(BF16), _DIMS["tn"],
                                             preferred_element_type=F32).astype(out_dtype)

    in_specs = [pl.BlockSpec((k, tm), lambda i, j: (0, i))]
    for p, b in enumerate(pieces):
        np_ = b.shape[1] // tn
        in_specs.append(pl.BlockSpec((k, tn), lambda i, j, p=p, np_=np_: (0, jnp.clip(j - starts[p], 0, np_ - 1))))
    return _pcall(
        body,
        name=name,
        out_shape=jax.ShapeDtypeStruct((m, nj * tn), out_dtype),
        grid=(m // tm, nj),
        in_specs=in_specs,
        out_specs=pl.BlockSpec((tm, tn), lambda i, j: (i, j)),
        compiler_params=_cparams(("parallel", "arbitrary")),
    )(a, *pieces)


def _mm_up_fwd(z2, w3, name, tm=1024):
    t, d = z2.shape
    nsh, _, c = w3.shape
    tm = _tile(t, tm, 8)
    return _mm_call(z2, w3, mode="nn", grid=(t // tm, nsh, 1),
                    a_spec=pl.BlockSpec((tm, d), lambda i, j, kk: (i, 0)),
                    b_spec=pl.BlockSpec((None, d, c), lambda i, j, kk: (j, 0, 0)),
                    o_spec=pl.BlockSpec((tm, c), lambda i, j, kk: (i, j)),
                    out_shape=jax.ShapeDtypeStruct((t, nsh * c), BF16), acc_shape=(tm, c), name=name)


def _mm_up_dz(du3, w3, name, tm=512, tn=1024):
    _, t, f = du3.shape
    nsh, d, c = w3.shape
    half = nsh // 2
    assert f == half * c
    tm, tn = _tile(t, tm, 8), _tile(d, tn)

    def body(a_ref, b_ref, o_ref, acc):
        kk = pl.program_id(2)
        p = None
        for s in range(half):
            q = lax.dot_general(a_ref[:, s * c:(s + 1) * c], b_ref[s], _DIMS["nt"], preferred_element_type=F32)
            p = q if p is None else p + q

        @pl.when(kk == 0)
        def _():
            acc[...] = p

        @pl.when(kk == 1)
        def _():
            o_ref[...] = (acc[...] + p).astype(BF16)

    return _pcall(
        body,
        name=name,
        out_shape=jax.ShapeDtypeStruct((t, d), BF16),
        grid=(t // tm, d // tn, 2),
        in_specs=[pl.BlockSpec((None, tm, f), lambda i, j, kk: (kk, i, 0)),
                  pl.BlockSpec((half, tn, c), lambda i, j, kk: (kk, j, 0))],
        out_specs=pl.BlockSpec((tm, tn), lambda i, j, kk: (i, j)),
        scratch_shapes=[pltpu.VMEM((tm, tn), F32)],
        compiler_params=_cparams(("parallel", "parallel", "arbitrary")),
    )(du3, w3)


def _mm_sum_nn(pieces, out_dtype, name, tm=512, tn=512, rows=None):
    m = pieces[0][0].shape[0] if rows is None else rows
    n = pieces[0][2].shape[1]
    tm, tn = _tile(m, tm, 8), _tile(n, tn)
    npc = len(pieces)

    def body(*refs):
        p = None
        for s in range(npc):
            q = jnp.dot(refs[2 * s][...].astype(BF16), refs[2 * s + 1][...].astype(BF16), preferred_element_type=F32)
            p = q if p is None else p + q
        refs[2 * npc][...] = p.astype(out_dtype)

    in_specs, args = [], []
    for a, ao, b, bo, kp in pieces:
        assert ao % kp == 0 and bo % kp == 0 and b.shape[1] == n
        in_specs.append(pl.BlockSpec((tm, kp), lambda i, j, ab=ao // kp: (i, ab)))
        in_specs.append(pl.BlockSpec((kp, tn), lambda i, j, bb=bo // kp: (bb, j)))
        args += [a, b]
    return _pcall(
        body,
        name=name,
        out_shape=jax.ShapeDtypeStruct((m, n), out_dtype),
        grid=(m // tm, n // tn),
        in_specs=in_specs,
        out_specs=pl.BlockSpec((tm, tn), lambda i, j: (i, j)),
        compiler_params=_cparams(("parallel", "parallel")),
    )(*args)


def _mm_rows_tn(pieces, b, out_dtype, name, tm=1024, tn=1024, rows=None):
    k = b.shape[0] if rows is None else rows
    n = b.shape[1]
    tn = _tile(n, tn)
    starts, s = [], 0
    for a in pieces:
        assert a.shape[1] % tm == 0
        starts.append(s)
        s += a.shape[1] // tm
    ni = s
    npc = len(pieces)

    def body(*refs):
        b_ref, o_ref = refs[npc], refs[npc + 1]
        i = pl.program_id(0)
        for p in range(npc):
            @pl.when((i >= starts[p]) & (i < starts[p] + pieces[p].shape[1] // tm))
            def _(p=p):
                o_ref[...] = lax.dot_general(refs[p][...].astype(BF16), b_ref[...].astype(BF16), _DIMS["tn"],
                                             preferred_element_type=F32).astype(out_dtype)

    in_specs = []
    for p, a in enumerate(pieces):
        np_ = a.shape[1] // tm
        in_specs.append(pl.BlockSpec((k, tm), lambda i, j, p=p, np_=np_: (0, jnp.clip(i - starts[p], 0, np_ - 1))))
    in_specs.append(pl.BlockSpec((k, tn), lambda i, j: (0, j)))
    return _pcall(
        body,
        name=name,
        out_shape=jax.ShapeDtypeStruct((ni * tm, n), out_dtype),
        grid=(ni, n // tn),
        in_specs=in_specs,
        out_specs=pl.BlockSpec((tm, tn), lambda i, j: (i, j)),
        compiler_params=_cparams(("parallel", "parallel")),
    )(*pieces, b)


def _mm_up_gw(z2, du3, nsh, name, tm=1024):
    t, d = z2.shape
    f = du3.shape[2]
    half = nsh // 2
    c = f // half
    tm = _tile(d, tm)
    return _mm_call(z2, du3, mode="tn", grid=(d // tm, nsh, 1),
                    a_spec=pl.BlockSpec((t, tm), lambda i, j, kk: (0, i)),
                    b_spec=pl.BlockSpec((None, t, c), lambda i, j, kk: (j // half, 0, j % half)),
                    o_spec=pl.BlockSpec((None, tm, c), lambda i, j, kk: (j, i, 0)),
                    out_shape=jax.ShapeDtypeStruct((nsh, d, c), BF16), acc_shape=(tm, c), name=name)


def _rms(x):
    r = lax.rsqrt(jnp.mean(x * x, axis=-1, keepdims=True) + NORM_EPS)
    return x * r, r


def _rms_bwd(dxh, xh, r):
    return r * (dxh - xh * jnp.mean(dxh * xh, axis=-1, keepdims=True))


def _colsum(v):
    return jnp.sum(v, axis=0, keepdims=True)


def _rope(v, c, s1, s2, q):
    w = v.shape[-1]
    return v * c + pltpu.roll(v, w - q, 1) * s1 + pltpu.roll(v, q, 1) * s2


def _rope_t(d, c, s1, s2, q):
    w = d.shape[-1]
    return d * c + pltpu.roll(d * s1, q, 1) + pltpu.roll(d * s2, w - q, 1)


def _norm_mod_fwd(ctx, x, gain, mods):
    tc, d = ctx.shape
    t = x.shape[0]
    rb = min(ROW_BLOCK, tc)
    nbl = t // rb

    def body(ctx_ref, x_ref, g_ref, mod_ref, z_ref):
        i = pl.program_id(0)

        def emit(src, sh, sc):
            xh, _ = _rms(src[...])
            z_ref[...] = ((xh * g_ref[...]) * (1.0 + sc) + sh).astype(BF16)

        @pl.when(i >= nbl)
        def _():
            emit(ctx_ref, mod_ref[2:3, :], mod_ref[3:4, :])

        @pl.when(i < nbl)
        def _():
            emit(x_ref, mod_ref[0:1, :], mod_ref[1:2, :])

    return _pcall(
        body,
        name="norm1_mod_fwd",
        out_shape=jax.ShapeDtypeStruct((tc + t, d), BF16),
        grid=((tc + t) // rb,),
        in_specs=[
            pl.BlockSpec((rb, d), lambda i: (jnp.maximum(i - nbl, 0), 0)),
            pl.BlockSpec((rb, d), lambda i: (jnp.minimum(i, nbl - 1), 0)),
            pl.BlockSpec((1, d), lambda i: (0, 0)),
            pl.BlockSpec((8, d), lambda i: (0, 0)),
        ],
        out_specs=pl.BlockSpec((rb, d), lambda i: (i, 0)),
        compiler_params=_cparams(("arbitrary",)),
    )(ctx, x, gain, mods)


def _norm1_bwd(ctx, x, gain, mods, dz_ctx, dz_lat, dx1):
    tc, d = ctx.shape
    t = x.shape[0]
    rb = min(ROW_BLOCK, tc)
    nbl = t // rb

    def body(ctx_ref, x_ref, g_ref, mod_ref, dzc_ref, dzl_ref, dx1_ref, gx_ref, st_ref):
        i = pl.program_id(0)

        @pl.when(i == 0)
        def _():
            st_ref[...] = jnp.zeros_like(st_ref)

        def common(src, dz, sc, row_sh, row_sc):
            xh, r = _rms(src[...])
            g = g_ref[...]
            dxn = dz * (1.0 + sc)
            st_ref[row_sh:row_sh + 1, :] += _colsum(dz)
            st_ref[row_sc:row_sc + 1, :] += _colsum(dz * (xh * g))
            st_ref[2:3, :] += _colsum(dxn * xh)
            return _rms_bwd(dxn * g, xh, r)

        @pl.when(i >= nbl)
        def _():
            common(ctx_ref, dzc_ref[...], mod_ref[3:4, :], 3, 4)

        @pl.when(i < nbl)
        def _():
            gx_ref[...] = dx1_ref[...] + common(x_ref, dzl_ref[...], mod_ref[1:2, :], 0, 1)

    lat = lambda i: (jnp.minimum(i, nbl - 1), 0)
    cix = lambda i: (jnp.maximum(i - nbl, 0), 0)
    return _pcall(
        body,
        name="norm1_mod_bwd",
        out_shape=[jax.ShapeDtypeStruct((t, d), F32), jax.ShapeDtypeStruct((8, d), F32)],
        grid=((tc + t) // rb,),
        in_specs=[
            pl.BlockSpec((rb, d), cix),
            pl.BlockSpec((rb, d), lat),
            pl.BlockSpec((1, d), lambda i: (0, 0)),
            pl.BlockSpec((8, d), lambda i: (0, 0)),
            pl.BlockSpec((rb, d), cix),
            pl.BlockSpec((rb, d), lat),
            pl.BlockSpec((rb, d), lat),
        ],
        out_specs=[pl.BlockSpec((rb, d), lat), pl.BlockSpec((8, d), lambda i: (0, 0))],
        compiler_params=_cparams(("arbitrary",)),
    )(ctx, x, gain, mods, dz_ctx, dz_lat, dx1)


def _key_prep_fwd(kv, kv_gain, kb_gain, tabs):
    ta, wkv = kv.shape
    kvl = MLA_KV_LORA
    nb = GQA_KV_HEADS * GQA_HEAD_DIM
    rb = ROW_BLOCK if ta % ROW_BLOCK == 0 else LANE
    hd = GQA_HEAD_DIM

    def body(kv_ref, g_ref, gb_ref, ca, s1a, s2a, cb, s1b, s2b, kin_ref, kb_ref, vb_ref):
        xh, _ = _rms(kv_ref[:, 0:kvl])
        kin_ref[:, 0:kvl] = (xh * g_ref[...]).astype(BF16)
        kpe = kv_ref[:, kvl + 2 * nb:kvl + 2 * nb + LANE]
        kin_ref[:, kvl:kvl + LANE] = _rope(kpe, ca[...], s1a[...], s2a[...], MLA_ROPE // 4).astype(BF16)
        for h in range(GQA_KV_HEADS):
            nh, _ = _rms(kv_ref[:, kvl + h * hd:kvl + (h + 1) * hd])
            kb_ref[:, h * hd:(h + 1) * hd] = _rope(nh * gb_ref[...], cb[...], s1b[...], s2b[...], hd // 4).astype(BF16)
        vb_ref[...] = kv_ref[:, kvl + nb:kvl + 2 * nb].astype(BF16)

    row = lambda w: pl.BlockSpec((rb, w), lambda i: (i, 0))
    fix = lambda w: pl.BlockSpec((1, w), lambda i: (0, 0))
    return _pcall(
        body,
        name="key_prep_fwd",
        out_shape=[jax.ShapeDtypeStruct((ta, kvl + LANE), BF16), jax.ShapeDtypeStruct((ta, nb), BF16),
                   jax.ShapeDtypeStruct((ta, nb), BF16)],
        grid=(ta // rb,),
        in_specs=[row(wkv), fix(kvl), fix(hd)] + [row(LANE)] * 3 + [row(hd)] * 3,
        out_specs=[row(kvl + LANE), row(nb), row(nb)],
        compiler_params=_cparams(("parallel",)),
    )(kv, kv_gain, kb_gain, *tabs)


def _key_prep_bwd(kv, kv_gain, kb_gain, tabs, dkin, dkb, dvb):
    ta, wkv = kv.shape
    kvl = MLA_KV_LORA
    nb = GQA_KV_HEADS * GQA_HEAD_DIM
    rb = ROW_BLOCK if ta % ROW_BLOCK == 0 else LANE
    hd = GQA_HEAD_DIM

    def body(kv_ref, g_ref, gb_ref, ca, s1a, s2a, cb, s1b, s2b, dkin_ref, dkb_ref, dvb_ref, dkv_ref, st_ref, stb_ref):
        @pl.when(pl.program_id(0) == 0)
        def _():
            st_ref[...] = jnp.zeros_like(st_ref)
            stb_ref[...] = jnp.zeros_like(stb_ref)

        xh, r = _rms(kv_ref[:, 0:kvl])
        dn = dkin_ref[:, 0:kvl]
        st_ref[0:1, :] += _colsum(dn * xh)
        dkv_ref[:, 0:kvl] = _rms_bwd(dn * g_ref[...], xh, r).astype(BF16)
        dpe = _rope_t(dkin_ref[:, kvl:kvl + LANE], ca[...], s1a[...], s2a[...], MLA_ROPE // 4)
        dkv_ref[:, kvl + 2 * nb:kvl + 2 * nb + LANE] = dpe.astype(BF16)
        for h in range(GQA_KV_HEADS):
            nh, rh = _rms(kv_ref[:, kvl + h * hd:kvl + (h + 1) * hd])
            dn_h = _rope_t(dkb_ref[:, h * hd:(h + 1) * hd], cb[...], s1b[...], s2b[...], hd // 4)
            stb_ref[0:1, :] += _colsum(dn_h * nh)
            dkv_ref[:, kvl + h * hd:kvl + (h + 1) * hd] = _rms_bwd(dn_h * gb_ref[...], nh, rh).astype(BF16)
        dkv_ref[:, kvl + nb:kvl + 2 * nb] = dvb_ref[...].astype(BF16)

    row = lambda w: pl.BlockSpec((rb, w), lambda i: (i, 0))
    fix = lambda w: pl.BlockSpec((1, w), lambda i: (0, 0))
    return _pcall(
        body,
        name="key_prep_bwd",
        out_shape=[jax.ShapeDtypeStruct((ta, wkv), BF16), jax.ShapeDtypeStruct((8, kvl), F32),
                   jax.ShapeDtypeStruct((8, hd), F32)],
        grid=(ta // rb,),
        in_specs=[row(wkv), fix(kvl), fix(hd)] + [row(LANE)] * 3 + [row(hd)] * 3 + [row(kvl + LANE), row(nb), row(nb)],
        out_specs=[row(wkv), pl.BlockSpec((8, kvl), lambda i: (0, 0)), pl.BlockSpec((8, hd), lambda i: (0, 0))],
        compiler_params=_cparams(("arbitrary",)),
    )(kv, kv_gain, kb_gain, *tabs, dkin, dkb, dvb)


def _q_prep_fwd(qg, q_gain, qb_gain, tabs, qscale):
    t = qg.shape[0]
    ql = MLA_Q_LORA
    hd = GQA_HEAD_DIM
    hb = GQA_HEADS * hd
    rb = min(ROW_BLOCK, t)

    def body(q_ref, g_ref, gb_ref, cb, s1b, s2b, cqn_ref, qb_ref):
        xh, _ = _rms(q_ref[:, 0:ql])
        cqn_ref[...] = (xh * g_ref[...]).astype(BF16)
        for h in range(GQA_HEADS):
            nh, _ = _rms(q_ref[:, ql + h * hd:ql + (h + 1) * hd])
            qh = _rope(nh * gb_ref[...], cb[...], s1b[...], s2b[...], hd // 4)
            qb_ref[:, h * hd:(h + 1) * hd] = (qh * qscale).astype(BF16)

    row = lambda w: pl.BlockSpec((rb, w), lambda i: (i, 0))
    fix = lambda w: pl.BlockSpec((1, w), lambda i: (0, 0))
    return _pcall(
        body,
        name="q_prep_fwd",
        out_shape=[jax.ShapeDtypeStruct((t, ql), BF16), jax.ShapeDtypeStruct((t, hb), BF16)],
        grid=(t // rb,),
        in_specs=[row(ql + hb), fix(ql), fix(hd)] + [row(hd)] * 3,
        out_specs=[row(ql), row(hb)],
        compiler_params=_cparams(("parallel",)),
    )(qg, q_gain, qb_gain, *tabs)


def _q_prep_bwd(qg, q_gain, qb_gain, tabs, dcqn, dqb, wpad, qscale):
    t = qg.shape[0]
    ql = MLA_Q_LORA
    hd = GQA_HEAD_DIM
    hb = GQA_HEADS * hd
    rb = min(ROW_BLOCK, t)

    def body(q_ref, g_ref, gb_ref, cb, s1b, s2b, dcqn_ref, dqb_ref, dq_ref, st_ref, stb_ref):
        @pl.when(pl.program_id(0) == 0)
        def _():
            st_ref[...] = jnp.zeros_like(st_ref)
            stb_ref[...] = jnp.zeros_like(stb_ref)

        xh, r = _rms(q_ref[:, 0:ql])
        dn = dcqn_ref[...]
        st_ref[0:1, :] += _colsum(dn * xh)
        dq_ref[:, 0:ql] = _rms_bwd(dn * g_ref[...], xh, r).astype(BF16)
        for h in range(GQA_HEADS):
            nh, rh = _rms(q_ref[:, ql + h * hd:ql + (h + 1) * hd])
            dn_h = _rope_t(dqb_ref[:, h * hd:(h + 1) * hd] * qscale, cb[...], s1b[...], s2b[...], hd // 4)
            stb_ref[0:1, :] += _colsum(dn_h * nh)
            dq_ref[:, ql + h * hd:ql + (h + 1) * hd] = _rms_bwd(dn_h * gb_ref[...], nh, rh).astype(BF16)
        if wpad:
            dq_ref[:, ql + hb:ql + hb + wpad] = jnp.zeros((rb, wpad), BF16)

    row = lambda w: pl.BlockSpec((rb, w), lambda i: (i, 0))
    fix = lambda w: pl.BlockSpec((1, w), lambda i: (0, 0))
    return _pcall(
        body,
        name="q_prep_bwd",
        out_shape=[jax.ShapeDtypeStruct((t, ql + hb + wpad), BF16), jax.ShapeDtypeStruct((8, ql), F32),
                   jax.ShapeDtypeStruct((8, hd), F32)],
        grid=(t // rb,),
        in_specs=[row(ql + hb), fix(ql), fix(hd)] + [row(hd)] * 3 + [row(ql), row(hb)],
        out_specs=[row(ql + hb + wpad), pl.BlockSpec((8, ql), lambda i: (0, 0)), pl.BlockSpec((8, hd), lambda i: (0, 0))],
        compiler_params=_cparams(("arbitrary",)),
    )(qg, q_gain, qb_gain, *tabs, dcqn, dqb)


def _rope_a(v, tabs, transpose, out_dtype, name, qscale):
    t, w = v.shape
    rb = min(ROW_BLOCK, t)
    fn = _rope_t if transpose else _rope

    def body(v_ref, c, s1, s2, o_ref):
        for h in range(w // MLA_SLOT):
            sl = slice(h * MLA_SLOT, (h + 1) * MLA_SLOT)
            o_ref[:, sl] = (fn(v_ref[:, sl].astype(F32), c[...], s1[...], s2[...], MLA_ROPE // 4) * qscale).astype(out_dtype)

    row = lambda ww: pl.BlockSpec((rb, ww), lambda i: (i, 0))
    return _pcall(
        body,
        name=name,
        out_shape=jax.ShapeDtypeStruct((t, w), out_dtype),
        grid=(t // rb,),
        in_specs=[row(w)] + [row(MLA_SLOT)] * 3,
        out_specs=row(w),
        compiler_params=_cparams(("parallel",)),
    )(v, *tabs)


def _merge_fwd(pa, pb, qg, gate_blk):
    t, d = pa.shape
    rb = min(ROW_BLOCK, t)

    def body(pa_ref, pb_ref, ga_ref, gb_ref, o_ref):
        o_ref[...] = (jax.nn.sigmoid(ga_ref[...]) * pa_ref[...].astype(F32)
                      + jax.nn.sigmoid(gb_ref[...]) * pb_ref[...].astype(F32)).astype(BF16)

    row = pl.BlockSpec((rb, d), lambda i: (i, 0))
    return _pcall(
        body,
        name="merge_fwd",
        out_shape=jax.ShapeDtypeStruct((t, d), BF16),
        grid=(t // rb,),
        in_specs=[row, row, pl.BlockSpec((rb, d), lambda i: (i, gate_blk)), pl.BlockSpec((rb, d), lambda i: (i, gate_blk + 1))],
        out_specs=row,
        compiler_params=_cparams(("parallel",)),
    )(pa, pb, qg, qg)


def _merge_bwd(dm, pa, pb, qg, gate_blk):
    t, d = pa.shape
    rb = min(ROW_BLOCK, t)

    def body(dm_ref, pa_ref, pb_ref, ga_ref, gb_ref, dpa_ref, dpb_ref, dg_ref):
        dmv = dm_ref[...].astype(F32)
        sa = jax.nn.sigmoid(ga_ref[...])
        sb = jax.nn.sigmoid(gb_ref[...])
        dpa_ref[...] = (dmv * sa).astype(BF16)
        dpb_ref[...] = (dmv * sb).astype(BF16)
        dg_ref[:, 0:d] = (dmv * pa_ref[...].astype(F32) * (sa * (1.0 - sa))).astype(BF16)
        dg_ref[:, d:2 * d] = (dmv * pb_ref[...].astype(F32) * (sb * (1.0 - sb))).astype(BF16)

    row = pl.BlockSpec((rb, d), lambda i: (i, 0))
    return _pcall(
        body,
        name="merge_bwd",
        out_shape=[jax.ShapeDtypeStruct((t, d), BF16), jax.ShapeDtypeStruct((t, d), BF16),
                   jax.ShapeDtypeStruct((t, 2 * d), BF16)],
        grid=(t // rb,),
        in_specs=[row, row, row, pl.BlockSpec((rb, d), lambda i: (i, gate_blk)), pl.BlockSpec((rb, d), lambda i: (i, gate_blk + 1))],
        out_specs=[row, row, pl.BlockSpec((rb, 2 * d), lambda i: (i, 0))],
        compiler_params=_cparams(("parallel",)),
    )(dm, pa, pb, qg, qg)


def _resid_norm_mod(x, branch, gain, mods, name):
    t, d = x.shape
    rb = min(ROW_BLOCK, t)

    def body(x_ref, b_ref, g_ref, mod_ref, x1_ref, z_ref):
        x1 = x_ref[...] + mod_ref[0:1, :] * b_ref[...]
        x1_ref[...] = x1
        xh, _ = _rms(x1)
        z_ref[...] = ((xh * g_ref[...]) * (1.0 + mod_ref[2:3, :]) + mod_ref[1:2, :]).astype(BF16)

    row = pl.BlockSpec((rb, d), lambda i: (i, 0))
    return _pcall(
        body,
        name=name,
        out_shape=[jax.ShapeDtypeStruct((t, d), F32), jax.ShapeDtypeStruct((t, d), BF16)],
        grid=(t // rb,),
        in_specs=[row, row, pl.BlockSpec((1, d), lambda i: (0, 0)), pl.BlockSpec((8, d), lambda i: (0, 0))],
        out_specs=[row, row],
        compiler_params=_cparams(("parallel",)),
    )(x, branch, gain, mods)


def _norm2_bwd(x1, attn, gain, mods, dz2, dx2):
    t, d = x1.shape
    rb = min(ROW_BLOCK, t)

    def body(x1_ref, at_ref, g_ref, mod_ref, dz_ref, dx2_ref, dx1_ref, da_ref, st_ref):
        @pl.when(pl.program_id(0) == 0)
        def _():
            st_ref[...] = jnp.zeros_like(st_ref)

        xh, r = _rms(x1_ref[...])
        g = g_ref[...]
        dz = dz_ref[...].astype(F32)
        dxn = dz * (1.0 + mod_ref[1:2, :])
        st_ref[0:1, :] += _colsum(dz)
        st_ref[1:2, :] += _colsum(dz * (xh * g))
        st_ref[2:3, :] += _colsum(dxn * xh)
        dx1 = dx2_ref[...] + _rms_bwd(dxn * g, xh, r)
        dx1_ref[...] = dx1
        st_ref[3:4, :] += _colsum(dx1 * at_ref[...])
        da_ref[...] = (dx1 * mod_ref[0:1, :]).astype(BF16)

    row = pl.BlockSpec((rb, d), lambda i: (i, 0))
    return _pcall(
        body,
        name="norm2_mod_bwd",
        out_shape=[jax.ShapeDtypeStruct((t, d), F32), jax.ShapeDtypeStruct((t, d), BF16), jax.ShapeDtypeStruct((8, d), F32)],
        grid=(t // rb,),
        in_specs=[row, row, pl.BlockSpec((1, d), lambda i: (0, 0)), pl.BlockSpec((8, d), lambda i: (0, 0)), row, row],
        out_specs=[row, row, pl.BlockSpec((8, d), lambda i: (0, 0))],
        compiler_params=_cparams(("arbitrary",)),
    )(x1, attn, gain, mods, dz2, dx2)


def _final_loss(x1, ffn, gain, mods, target):
    t, d = x1.shape
    rb = min(ROW_BLOCK, t)
    nb = t // rb

    def body(x1_ref, f_ref, g_ref, mod_ref, tg_ref, dx2_ref, df_ref, st_ref):
        i = pl.program_id(0)

        @pl.when(i == 0)
        def _():
            st_ref[...] = jnp.zeros_like(st_ref)

        ffn_v = f_ref[...]
        g2 = mod_ref[0:1, :]
        x2 = x1_ref[...] + g2 * ffn_v
        xh, r = _rms(x2)
        g = g_ref[...]
        err = xh * g - tg_ref[...]
        st_ref[2:3, :] += _colsum(err * err) * (0.5 / d)
        dy = err * (1.0 / d)
        st_ref[0:1, :] += _colsum(dy * xh)
        dx2 = _rms_bwd(dy * g, xh, r)
        dx2_ref[...] = dx2
        st_ref[1:2, :] += _colsum(dx2 * ffn_v)
        df_ref[...] = (dx2 * g2).astype(BF16)

        @pl.when(i == nb - 1)
        def _():
            st_ref[3:4, :] = jnp.broadcast_to(jnp.sum(st_ref[2:3, :], axis=-1, keepdims=True), (1, d))

    row = pl.BlockSpec((rb, d), lambda i: (i, 0))
    return _pcall(
        body,
        name="final_norm_loss",
        out_shape=[jax.ShapeDtypeStruct((t, d), F32), jax.ShapeDtypeStruct((t, d), BF16), jax.ShapeDtypeStruct((8, d), F32)],
        grid=(nb,),
        in_specs=[row, row, pl.BlockSpec((1, d), lambda i: (0, 0)), pl.BlockSpec((8, d), lambda i: (0, 0)), row],
        out_specs=[row, row, pl.BlockSpec((8, d), lambda i: (0, 0))],
        compiler_params=_cparams(("arbitrary",)),
    )(x1, ffn, gain, mods, target)


def _row_ends(shape):
    rows = lax.broadcasted_iota(jnp.int32, shape, 0)
    return rows == 0, rows == shape[0] - 1


def _shift_dn(v, first):
    return jnp.where(first, 0.0, pltpu.roll(v, 1, 0))


def _shift_up(v, last):
    return jnp.where(last, 0.0, pltpu.roll(v, v.shape[0] - 1, 0))


def _conv_fwd(u, cw, cb):
    t, f2 = u.shape
    f = f2 // 2
    cbk = _tile(f, 256)
    nf = f // cbk

    def body(ua_ref, ub_ref, cwa_ref, cwb_ref, cba_ref, cbb_ref, h_ref, uc_ref):
        first, last = _row_ends((t, cbk))
        outs = []
        for u_ref, cw_ref, cb_ref in ((ua_ref, cwa_ref, cba_ref), (ub_ref, cwb_ref, cbb_ref)):
            uu, cwv = u_ref[...].astype(F32), cw_ref[...]
            outs.append(cb_ref[...] + cwv[0:1, :] * _shift_dn(uu, first) + cwv[1:2, :] * uu
                        + cwv[2:3, :] * _shift_up(uu, last))
        a, b = outs
        uc_ref[0] = a.astype(BF16)
        uc_ref[1] = b.astype(BF16)
        h_ref[...] = (a * jax.nn.sigmoid(a) * b).astype(BF16)

    ca = lambda r: pl.BlockSpec((r, cbk), lambda j: (0, j))
    cbs = lambda r: pl.BlockSpec((r, cbk), lambda j: (0, nf + j))
    return _pcall(
        body,
        name="conv_gate_fwd",
        out_shape=[jax.ShapeDtypeStruct((t, f), BF16), jax.ShapeDtypeStruct((2, t, f), BF16)],
        grid=(nf,),
        in_specs=[ca(t), cbs(t), ca(3), cbs(3), ca(1), cbs(1)],
        out_specs=[ca(t), pl.BlockSpec((2, t, cbk), lambda j: (0, 0, j))],
        compiler_params=_cparams(("parallel",)),
    )(u, u, cw, cw, cb, cb)


def _conv_bwd(u, uc, cw, dh):
    t, f2 = u.shape
    f = f2 // 2
    cbk = _tile(f, 256)
    nf = f // cbk

    def body(ua_ref, ub_ref, uc_ref, cwa_ref, cwb_ref, dh_ref, du_ref, dcw_ref, dcb_ref):
        first, last = _row_ends((t, cbk))
        a, b = uc_ref[0].astype(F32), uc_ref[1].astype(F32)
        dh_v = dh_ref[...].astype(F32)
        sg = jax.nn.sigmoid(a)
        db = dh_v * (a * sg)
        da = dh_v * b * (sg * (1.0 + a * (1.0 - sg)))
        for idx, (dv, u_ref, cw_ref) in enumerate(((da, ua_ref, cwa_ref), (db, ub_ref, cwb_ref))):
            uu, cwv = u_ref[...].astype(F32), cw_ref[...]
            up, dn = _shift_up(dv, last), _shift_dn(dv, first)
            dcb_ref[idx] = _colsum(dv)
            dcw_ref[idx, 0:1, :] = _colsum(up * uu)
            dcw_ref[idx, 1:2, :] = _colsum(dv * uu)
            dcw_ref[idx, 2:3, :] = _colsum(dn * uu)
            du_ref[idx] = (cwv[0:1, :] * up + cwv[1:2, :] * dv + cwv[2:3, :] * dn).astype(BF16)

    ca = lambda r: pl.BlockSpec((r, cbk), lambda j: (0, j))
    cbs = lambda r: pl.BlockSpec((r, cbk), lambda j: (0, nf + j))
    o3 = lambda r: pl.BlockSpec((2, r, cbk), lambda j: (0, 0, j))
    return _pcall(
        body,
        name="conv_gate_bwd",
        out_shape=[jax.ShapeDtypeStruct((2, t, f), BF16), jax.ShapeDtypeStruct((2, 3, f), F32),
                   jax.ShapeDtypeStruct((2, 1, f), F32)],
        grid=(nf,),
        in_specs=[ca(t), cbs(t), o3(t), ca(3), cbs(3), ca(t)],
        out_specs=[o3(t), o3(3), o3(1)],
        compiler_params=_cparams(("parallel",)),
    )(u, u, uc, cw, cw, dh)


def _attention_fwd(q, kk, vv, *, hq, hkv, dk, dv, k_blk0, v_blk0, name):
    t = q.shape[0]
    tk = kk.shape[0]
    g_sz = hq // hkv
    tq = min(ATT_Q_BLOCK_FWD, t)

    def body(q_ref, k_ref, v_ref, o_ref, lse_ref):
        k = k_ref[...]
        v = v_ref[...]
        for j in range(g_sz):
            s = lax.dot_general(q_ref[:, j * dk:(j + 1) * dk], k, _DIMS["nt"], preferred_element_type=F32)
            m = jnp.max(s, axis=-1, keepdims=True)
            p = jnp.exp2(s - m)
            l = jnp.sum(p, axis=-1, keepdims=True)
            o = jnp.dot(p.astype(BF16), v, preferred_element_type=F32) / l
            o_ref[:, j * dv:(j + 1) * dv] = o.astype(BF16)
            lse_ref[0, :, j:j + 1] = m + jnp.log2(l)

    return _pcall(
        body,
        name=name,
        out_shape=[jax.ShapeDtypeStruct((t, hq * dv), BF16), jax.ShapeDtypeStruct((hkv, t, g_sz), F32)],
        grid=(hkv, t // tq),
        in_specs=[
            pl.BlockSpec((tq, g_sz * dk), lambda g, i: (i, g)),
            pl.BlockSpec((tk, dk), lambda g, i: (0, k_blk0 + g)),
            pl.BlockSpec((tk, dv), lambda g, i: (0, v_blk0 + g)),
        ],
        out_specs=[
            pl.BlockSpec((tq, g_sz * dv), lambda g, i: (i, g)),
            pl.BlockSpec((1, tq, g_sz), lambda g, i: (g, i, 0)),
        ],
        compiler_params=_cparams(("parallel", "parallel")),
    )(q, kk, vv)


def _attention_bwd(q, kk, vv, do, lse, *, hq, hkv, dk, dv, k_blk0, v_blk0, name):
    t = q.shape[0]
    tk = kk.shape[0]
    g_sz = hq // hkv
    tq = min(ATT_Q_BLOCK, t)

    def body(q_ref, k_ref, v_ref, do_ref, lse_ref, dq_ref, dk_ref, dv_ref):
        @pl.when(pl.program_id(1) == 0)
        def _():
            dk_ref[...] = jnp.zeros_like(dk_ref)
            dv_ref[...] = jnp.zeros_like(dv_ref)

        k = k_ref[...]
        v = v_ref[...]
        dk_acc = dv_acc = None
        for j in range(g_sz):
            qj = q_ref[:, j * dk:(j + 1) * dk]
            doj = do_ref[:, j * dv:(j + 1) * dv]
            s = lax.dot_general(qj, k, _DIMS["nt"], preferred_element_type=F32)
            p = jnp.exp2(s - lse_ref[0, :, j:j + 1])
            dp = lax.dot_general(doj, v, _DIMS["nt"], preferred_element_type=F32)
            ds = (p * (dp - jnp.sum(p * dp, axis=-1, keepdims=True))).astype(BF16)
            dv_j = lax.dot_general(p.astype(BF16), doj, _DIMS["tn"], preferred_element_type=F32)
            dk_j = lax.dot_general(ds, qj, _DIMS["tn"], preferred_element_type=F32)
            dv_acc = dv_j if dv_acc is None else dv_acc + dv_j
            dk_acc = dk_j if dk_acc is None else dk_acc + dk_j
            dq_ref[:, j * dk:(j + 1) * dk] = jnp.dot(ds, k, preferred_element_type=F32)
        dv_ref[...] += dv_acc
        dk_ref[...] += dk_acc

        @pl.when(pl.program_id(1) == t // tq - 1)
        def _():
            dk_ref[...] *= LN2

    return _pcall(
        body,
        name=name,
        out_shape=[jax.ShapeDtypeStruct((t, hq * dk), F32), jax.ShapeDtypeStruct((tk, hkv * dk), F32),
                   jax.ShapeDtypeStruct((tk, hkv * dv), F32)],
        grid=(hkv, t // tq),
        in_specs=[
            pl.BlockSpec((tq, g_sz * dk), lambda g, i: (i, g)),
            pl.BlockSpec((tk, dk), lambda g, i: (0, k_blk0 + g)),
            pl.BlockSpec((tk, dv), lambda g, i: (0, v_blk0 + g)),
            pl.BlockSpec((tq, g_sz * dv), lambda g, i: (i, g)),
            pl.BlockSpec((1, tq, g_sz), lambda g, i: (g, i, 0)),
        ],
        out_specs=[
            pl.BlockSpec((tq, g_sz * dk), lambda g, i: (i, g)),
            pl.BlockSpec((tk, dk), lambda g, i: (0, g)),
            pl.BlockSpec((tk, dv), lambda g, i: (0, g)),
        ],
        compiler_params=_cparams(("parallel", "arbitrary")),
    )(q, kk, vv, do, lse)


def _silu(v):
    return v * jax.nn.sigmoid(v)


def _ada_fwd(conds, w_ada, b_ada_shard):
    r, d = conds.shape
    n = w_ada.shape[1]
    tn = _tile(n, 512)

    def body(c_ref, w_ref, b_ref, o_ref):
        s = _silu(c_ref[...]).astype(BF16)
        o_ref[...] = jnp.dot(s, w_ref[...].astype(BF16), preferred_element_type=F32) + b_ref[...]

    return _pcall(
        body,
        name="ada_fwd",
        out_shape=jax.ShapeDtypeStruct((r, n), F32),
        grid=(n // tn,),
        in_specs=[pl.BlockSpec((r, d), lambda j: (0, 0)), pl.BlockSpec((d, tn), lambda j: (0, j)),
                  pl.BlockSpec((1, tn), lambda j: (0, j))],
        out_specs=pl.BlockSpec((r, tn), lambda j: (0, j)),
        compiler_params=_cparams(("parallel",)),
    )(conds, w_ada, b_ada_shard)


def _cctx_partial(da16_shard, w_ada, c_ctx_row):
    d, n = w_ada.shape
    td = _tile(d, 512)

    def body(g_ref, w_ref, c_ref, o_ref):
        ds = lax.dot_general(g_ref[8:16, :].astype(BF16), w_ref[...].astype(BF16), _DIMS["nt"],
                             preferred_element_type=F32)
        cv = c_ref[...]
        sg = jax.nn.sigmoid(cv)
        o_ref[...] = ds * (sg * (1.0 + cv * (1.0 - sg)))

    return _pcall(
        body,
        name="cctx_partial",
        out_shape=jax.ShapeDtypeStruct((8, d), F32),
        grid=(d // td,),
        in_specs=[pl.BlockSpec((16, n), lambda j: (0, 0)), pl.BlockSpec((td, n), lambda j: (j, 0)),
                  pl.BlockSpec((1, td), lambda j: (0, j))],
        out_specs=pl.BlockSpec((8, td), lambda j: (0, j)),
        compiler_params=_cparams(("parallel",)),
    )(da16_shard, w_ada, c_ctx_row)


def _sum_parts(parts):
    p, _, n = parts.shape

    def body(p_ref, o_ref):
        acc = p_ref[0]
        for s in range(1, p):
            acc = acc + p_ref[s]
        o_ref[...] = acc

    return _pcall(
        body,
        name="sum_parts",
        out_shape=jax.ShapeDtypeStruct((1, n), F32),
        in_specs=[pl.BlockSpec(memory_space=pltpu.VMEM)],
        out_specs=pl.BlockSpec(memory_space=pltpu.VMEM),
    )(parts)


def _adam_math(w, g, m, v):
    m2 = ADAM_B1 * m + (1.0 - ADAM_B1) * g
    v2 = ADAM_B2 * v + (1.0 - ADAM_B2) * jnp.square(g)
    m_hat = m2 / (1.0 - ADAM_B1 ** ADAM_STEP)
    v_hat = v2 / (1.0 - ADAM_B2 ** ADAM_STEP)
    delta = -ADAM_LR * (m_hat / (jnp.sqrt(v_hat) + ADAM_EPS) + ADAM_WD * w)
    return delta, m2, v2


def _adamw(parts, w, m, v, name):
    p, r, c = parts.shape
    block_elems = 1 << 18
    rb, cb = _tile(r, max(8, block_elems // c // 8 * 8), 8), c
    if rb * c < block_elems // 4 and r * c > block_elems:
        rb, cb = r, _tile(c, max(LANE, block_elems // r // LANE * LANE))

    def body(p_ref, w_ref, m_ref, v_ref, g_ref, d_ref, m2_ref, v2_ref):
        g = p_ref[0].astype(F32)
        for s in range(1, p):
            g = g + p_ref[s].astype(F32)
        g_ref[...] = g
        d_ref[...], m2_ref[...], v2_ref[...] = _adam_math(w_ref[...], g, m_ref[...], v_ref[...])

    if w.ndim == 3:
        blk = pl.BlockSpec((None, rb, cb), lambda i, j: (0, i, j))
    else:
        blk = pl.BlockSpec((rb, cb), lambda i, j: (i, j))
    return _pcall(
        body,
        name=name,
        out_shape=[jax.ShapeDtypeStruct(w.shape, F32)] * 4,
        grid=(r // rb, c // cb),
        in_specs=[pl.BlockSpec((p, rb, cb), lambda i, j: (0, i, j)), blk, blk, blk],
        out_specs=[blk] * 4,
        compiler_params=_cparams(("parallel", "parallel")),
    )(parts, w, m, v)


def _adamw_ada(conds, da16, w, m, v):
    d, n = w.shape
    rb = _tile(d, 256, LANE)

    def body(s_ref, da_ref, w_ref, m_ref, v_ref, g_ref, d_ref, m2_ref, v2_ref):
        g = lax.dot_general(_silu(s_ref[...]).astype(BF16), da_ref[...].astype(BF16), _DIMS["tn"],
                            preferred_element_type=F32)
        g_ref[...] = g
        d_ref[...], m2_ref[...], v2_ref[...] = _adam_math(w_ref[...], g, m_ref[...], v_ref[...])

    row = pl.BlockSpec((rb, n), lambda i: (i, 0))
    return _pcall(
        body,
        name="adamw_w_ada",
        out_shape=[jax.ShapeDtypeStruct((d, n), F32)] * 4,
        grid=(d // rb,),
        in_specs=[pl.BlockSpec((16, rb), lambda i: (0, i)), pl.BlockSpec((16, n), lambda i: (0, 0)), row, row, row],
        out_specs=[row] * 4,
        compiler_params=_cparams(("parallel",)),
    )(conds, da16, w, m, v)


def _cast_bf16(a, name):
    _, r, c = a.shape
    rb, cb = _tile(r, 512, 8), c
    if rb < 64 < r:
        rb, cb = r, _tile(c, 512)

    def body(a_ref, o_ref):
        o_ref[...] = a_ref[...].astype(BF16)

    return _pcall(body, name=name, out_shape=jax.ShapeDtypeStruct((r, c), BF16), grid=(r // rb, c // cb),
                  in_specs=[pl.BlockSpec((None, rb, cb), lambda i, j: (0, i, j))],
                  out_specs=pl.BlockSpec((rb, cb), lambda i, j: (i, j)),
                  compiler_params=_cparams(("parallel", "parallel")))(a)


def _rope_tabs(t, rot):
    half, q = rot // 2, rot // 4
    n_rows = t // GRID_W
    row = jnp.repeat(jnp.arange(n_rows, dtype=F32), GRID_W)
    col = jnp.tile(jnp.arange(GRID_W, dtype=F32), n_rows)
    inv_freq = ROPE_THETA ** (-jnp.arange(0, half, 2, dtype=F32) / half)
    ang = jnp.concatenate([row[:, None] * inv_freq, col[:, None] * inv_freq], axis=-1)
    cos, sin = jnp.cos(ang), jnp.sin(ang)
    c0, c1, s0, s1 = cos[:, :q], cos[:, q:], sin[:, :q], sin[:, q:]
    z = jnp.zeros_like(s0)
    return (jnp.concatenate([c0, c0, c1, c1], -1), jnp.concatenate([-s0, z, -s1, z], -1),
            jnp.concatenate([z, s0, z, s1], -1))


def _pad_cols(a, left, total, fill=0.0):
    return jnp.pad(a, ((0, 0), (left, total - left - a.shape[1])), constant_values=fill)


def _with_ctx_rows(tab, tc, fill):
    return jnp.concatenate([tab, jnp.full((tc, tab.shape[1]), fill, F32)], axis=0)


def kernel(x, c, ctx, c_ctx, w_ada, b_ada, norm1_g, w_in, mla_q_norm_g, w_q_up, mla_kv_norm_g, w_kv_up, gqa_q_norm_g, gqa_k_norm_g, w_br_a, w_br_b, w_out, norm2_g, w_up, conv_w, conv_b, w_down, final_norm_g, loss_target, m_c_ctx, m_w_ada, m_b_ada, m_norm1_g, m_w_in, m_mla_q_norm_g, m_w_q_up, m_mla_kv_norm_g, m_w_kv_up, m_gqa_q_norm_g, m_gqa_k_norm_g, m_w_br_a, m_w_br_b, m_w_out, m_norm2_g, m_w_up, m_conv_w, m_conv_b, m_w_down, m_final_norm_g, v_c_ctx, v_w_ada, v_b_ada, v_norm1_g, v_w_in, v_mla_q_norm_g, v_w_q_up, v_mla_kv_norm_g, v_w_kv_up, v_gqa_q_norm_g, v_gqa_k_norm_g, v_w_br_a, v_w_br_b, v_w_out, v_norm2_g, v_w_up, v_conv_w, v_conv_b, v_w_down, v_final_norm_g):
    weights = dict(c_ctx=c_ctx, w_ada=w_ada, b_ada=b_ada, norm1_g=norm1_g, w_in=w_in, mla_q_norm_g=mla_q_norm_g,
                   w_q_up=w_q_up, mla_kv_norm_g=mla_kv_norm_g, w_kv_up=w_kv_up, gqa_q_norm_g=gqa_q_norm_g,
                   gqa_k_norm_g=gqa_k_norm_g, w_br_a=w_br_a, w_br_b=w_br_b, w_out=w_out, norm2_g=norm2_g, w_up=w_up,
                   conv_w=conv_w, conv_b=conv_b, w_down=w_down, final_norm_g=final_norm_g)
    mom_m = dict(c_ctx=m_c_ctx, w_ada=m_w_ada, b_ada=m_b_ada, norm1_g=m_norm1_g, w_in=m_w_in, mla_q_norm_g=m_mla_q_norm_g,
                 w_q_up=m_w_q_up, mla_kv_norm_g=m_mla_kv_norm_g, w_kv_up=m_w_kv_up, gqa_q_norm_g=m_gqa_q_norm_g,
                 gqa_k_norm_g=m_gqa_k_norm_g, w_br_a=m_w_br_a, w_br_b=m_w_br_b, w_out=m_w_out, norm2_g=m_norm2_g,
                 w_up=m_w_up, conv_w=m_conv_w, conv_b=m_conv_b, w_down=m_w_down, final_norm_g=m_final_norm_g)
    mom_v = dict(c_ctx=v_c_ctx, w_ada=v_w_ada, b_ada=v_b_ada, norm1_g=v_norm1_g, w_in=v_w_in, mla_q_norm_g=v_mla_q_norm_g,
                 w_q_up=v_w_q_up, mla_kv_norm_g=v_mla_kv_norm_g, w_kv_up=v_w_kv_up, gqa_q_norm_g=v_gqa_q_norm_g,
                 gqa_k_norm_g=v_gqa_k_norm_g, w_br_a=v_w_br_a, w_br_b=v_w_br_b, w_out=v_w_out, norm2_g=v_norm2_g,
                 w_up=v_w_up, conv_w=v_conv_w, conv_b=v_conv_b, w_down=v_w_down, final_norm_g=v_final_norm_g)
    order = list(weights)

    my_idx = 4 * lax.axis_index("x") + 2 * lax.axis_index("y") + lax.axis_index("c")
    xs, cts, tgt = x[0], ctx[0], loss_target[0]
    t, d = xs.shape
    tc = cts.shape[0]
    ta = t + tc
    kvl, ql = MLA_KV_LORA, MLA_Q_LORA
    nb = GQA_KV_HEADS * GQA_HEAD_DIM
    hb = GQA_HEADS * GQA_HEAD_DIM
    ha = MLA_HEADS
    f2 = w_up.shape[2] * N_DEV
    ff = f2 // 2

    big = ["w_in", "w_q_up", "w_kv_up", "w_br_a", "w_br_b", "w_out", "w_up", "w_down"]
    nw = len(big)
    del nw
    _ORDER_AFTER.clear()
    narrow = ("w_in", "w_q_up")

    def tview(a):
        return jnp.transpose(a, (0, 2, 1))

    shards = {"w_in": _cast_bf16(tview(weights["w_in"]), "cast_w_in")}
    c_idx = jnp.reshape(lax.axis_index("c"), (1,)).astype(jnp.int32)

    def gather_start(names, dep):
        shs = [shards[n] for n in names]
        land = [lax.empty((N_DEV,) + s.shape, BF16) for s in shs]
        if dep is not None:
            _after(dep)
        s, r, arrs, tok = _split_start("gather_ici_start_" + names[0], shs + land, _gather_ici_copies(len(names)),
                                       4 * len(names))
        return dict(names=names, s=s, r=r, arrs=arrs, tok=tok)

    def gather_pass(g, after):
        n = len(g["names"])
        arrs = _split_wait("gather_ici_wait_" + g["names"][0], g["s"], g["r"], g["arrs"], _gather_ici_copies(n), after)
        s, r, bufs, tok = _split_start("gather_pass_start_" + g["names"][0], arrs[n:], _gather_pass_copies(n), 3 * n)
        g.update(s2=s, r2=r, bufs=bufs)
        return tok

    def gather_relay(g, after):
        n = len(g["names"])
        bufs = _split_wait("gather_pass_wait_" + g["names"][0], g["s2"], g["r2"], g["bufs"], _gather_pass_copies(n), after)
        s, r, bufs, tok = _split_start("gather_d2d_start_" + g["names"][0], bufs, _gather_d2d_copies(n), n)
        g.update(s3=s, r3=r, bufs=bufs)
        return tok

    def gather_finish(g, after):
        n = len(g["names"])
        bufs = _split_wait("gather_d2d_wait_" + g["names"][0], g["s3"], g["r3"], g["bufs"], _gather_d2d_copies(n), after)
        return dict(zip(g["names"], bufs))

    c_all, cw_all = _all_gather([jnp.pad(c, ((0, 7), (0, 0))), jnp.pad(conv_w[0], ((0, 5), (0, 0)))], "gather_cond")
    conv_w_f = jnp.transpose(cw_all[:, :3, :], (1, 0, 2)).reshape(3, f2)
    conds = jnp.concatenate([c_all[:, 0, :], c_ctx[None, :], jnp.zeros((7, d), F32)], axis=0)
    ncol = w_ada.shape[2]
    b_shard = lax.dynamic_slice_in_dim(b_ada, my_idx * ncol, ncol, axis=1)
    g0 = gather_start(["w_in"], c_all)
    _after(g0["tok"])
    ada_shard = _ada_fwd(conds, w_ada[0], b_shard)
    (ada_all,) = _all_gather([ada_shard], "gather_ada")
    ada = jnp.transpose(ada_all, (1, 0, 2)).reshape(16, N_DEV * ncol)
    lat = lax.dynamic_slice_in_dim(ada, my_idx, 1, axis=0).reshape(6, d)
    cxt = ada[8].reshape(6, d)
    zero2 = jnp.zeros((2, d), F32)
    mods1 = jnp.concatenate([lat[0:2], cxt[0:2], jnp.zeros((4, d), F32)], axis=0)
    mods2 = jnp.concatenate([lat[2:3], lat[3:4], lat[4:5], jnp.zeros((5, d), F32)], axis=0)
    mods2b = jnp.concatenate([lat[2:3], lat[4:5], jnp.zeros((6, d), F32)], axis=0)
    mods3 = jnp.concatenate([lat[5:6], jnp.zeros((7, d), F32)], axis=0)
    del zero2

    for n in big[1:]:
        _after(ada_all)
        shards[n] = _cast_bf16(tview(weights[n]) if n in narrow else weights[n], "cast_" + n)

    ca, s1a, s2a = _rope_tabs(t, MLA_ROPE)
    cb_, s1b, s2b = _rope_tabs(t, GQA_HEAD_DIM)
    q_tabs_a = (_pad_cols(jnp.concatenate([jnp.ones((t, MLA_NOPE), F32), ca], 1), 0, MLA_SLOT),
                _pad_cols(s1a, MLA_NOPE, MLA_SLOT), _pad_cols(s2a, MLA_NOPE, MLA_SLOT))
    q_tabs_b = (cb_, s1b, s2b)
    k_tabs = (_with_ctx_rows(_pad_cols(ca, 0, LANE), tc, 1.0), _with_ctx_rows(_pad_cols(s1a, 0, LANE), tc, 0.0),
              _with_ctx_rows(_pad_cols(s2a, 0, LANE), tc, 0.0),
              _with_ctx_rows(cb_, tc, 1.0), _with_ctx_rows(s1b, tc, 0.0), _with_ctx_rows(s2b, tc, 0.0))

    def cols_full(g):
        return jnp.transpose(g, (1, 0, 2)).reshape(g.shape[1], N_DEV * g.shape[2])

    _after(*q_tabs_a, *q_tabs_b, *k_tabs, *[shards[n] for n in big[1:]])
    tok_p0 = gather_pass(g0, mods1)
    g1 = gather_start(["w_q_up", "w_kv_up", "w_br_a", "w_br_b", "w_out"], tok_p0)
    _after(g1["tok"])
    z_all = _norm_mod_fwd(cts, xs, norm1_g, mods1)
    gathered = gather_finish(g0, gather_relay(g0, z_all))
    wt_in = gathered["w_in"].reshape(-1, d)
    o_kpe, o_kb, o_vb = kvl, kvl + MLA_ROPE, kvl + MLA_ROPE + nb
    o_q = o_vb + nb
    o_g = o_q + ql + hb
    wkv_w = kvl + 2 * nb + LANE
    wt_kv_p = jnp.concatenate([wt_in[:kvl], wt_in[o_kb:o_q], wt_in[o_kpe:o_kb],
                               jnp.zeros((LANE - MLA_ROPE, d), BF16)], axis=0)
    q_w = ql + hb
    q_pad = (-q_w) % 512 if d >= 512 else (-q_w) % d
    gate_blk = (q_w + q_pad) // d
    assert (q_w + q_pad) % d == 0
    wt_qg_p = jnp.concatenate([wt_in[o_q:o_g], jnp.zeros((q_pad, d), BF16), wt_in[o_g:]], axis=0)

    kv_all = _mm(z_all, wt_kv_p, "nt", F32, "proj_kv", tm=1152, tn=wkv_w)
    qg = _mm(z_all, wt_qg_p, "nt", F32, "proj_qg", tm=1024, tn=1024, rows=t)
    tok_p1 = gather_pass(g1, qg)
    g2 = gather_start(["w_up"], tok_p1)
    g3 = gather_start(["w_down"], g2["tok"])
    _after(g3["tok"])
    kin, k_b, v_b = _key_prep_fwd(kv_all, mla_kv_norm_g, gqa_k_norm_g, k_tabs)
    sc_a = float((MLA_NOPE + MLA_ROPE) ** -0.5) * LOG2E
    sc_b = float(GQA_HEAD_DIM ** -0.5) * LOG2E
    _after(g3["tok"])
    cqn, q_b = _q_prep_fwd(qg, mla_q_norm_g, gqa_q_norm_g, q_tabs_b, sc_b)
    _after(kin, g3["tok"])
    gathered.update(gather_finish(g1, gather_relay(g1, q_b)))

    wqt_f = gathered["w_q_up"].reshape(ha, MLA_NOPE + MLA_ROPE, ql)
    wqt_ext = jnp.pad(wqt_f, ((0, 0), (0, MLA_SLOT - MLA_NOPE - MLA_ROPE), (0, 0))).reshape(ha * MLA_SLOT, ql)
    wkv_f = cols_full(gathered["w_kv_up"]).reshape(kvl, ha, MLA_NOPE + MLA_V)
    wk_slots = jnp.pad(wkv_f[:, :, :MLA_NOPE], ((0, 0), (0, 0), (0, MLA_SLOT - MLA_NOPE))).reshape(kvl, ha * MLA_SLOT)
    wv_cols = wkv_f[:, :, MLA_NOPE:].reshape(kvl, ha * MLA_V)
    e_slot = jnp.pad(jnp.eye(MLA_ROPE, dtype=BF16),
                     ((0, LANE - MLA_ROPE), (MLA_NOPE, MLA_SLOT - MLA_NOPE - MLA_ROPE)))
    e_rows = jnp.concatenate([jnp.tile(e_slot, (1, ha)), jnp.zeros((LANE, ha * MLA_V), BF16)], axis=1)
    wkv_ext = jnp.concatenate([jnp.concatenate([wk_slots, wv_cols], axis=1), e_rows], axis=0)
    w_bra = cols_full(gathered["w_br_a"])
    w_brb = cols_full(gathered["w_br_b"])
    w_out_f = gathered["w_out"].reshape(d, d)

    kv_a = _mm(kin, wkv_ext, "nn", BF16, "kv_up", tm=1152, tn=1024)
    qa_raw = _mm(cqn, wqt_ext, "nt", F32, "q_up", tm=1024, tn=1024)
    q_a = _rope_a(qa_raw, q_tabs_a, False, BF16, "rope_q_fwd", sc_a)
    att_a = dict(hq=ha, hkv=ha, dk=MLA_SLOT, dv=MLA_V, k_blk0=0, v_blk0=ha * MLA_SLOT // MLA_V)
    att_b = dict(hq=GQA_HEADS, hkv=GQA_KV_HEADS, dk=GQA_HEAD_DIM, dv=GQA_HEAD_DIM, k_blk0=0, v_blk0=0)
    o_a, lse_a = _attention_fwd(q_a, kv_a, kv_a, name="attn_a_fwd", **att_a)
    o_b, lse_b = _attention_fwd(q_b, k_b, v_b, name="attn_b_fwd", **att_b)
    _after(o_a)
    _after(gather_pass(g2, o_b))
    pa = _mm(o_a, w_bra, "nn", BF16, "br_a", tm=1024, tn=1024)
    pb = _mm(o_b, w_brb, "nn", BF16, "br_b", tm=1024, tn=1024)
    merged = _merge_fwd(pa, pb, qg, gate_blk)
    attn = _mm(merged, w_out_f, "nn", F32, "w_out", tm=1024, tn=1024)
    x1, z2 = _resid_norm_mod(xs, attn, norm2_g, mods2, "resid_norm2_fwd")
    tok_r2 = gather_relay(g2, z2)
    tok_p3 = gather_pass(g3, tok_r2)
    w_up3 = gather_finish(g2, tok_p3)["w_up"]
    u = _mm_up_fwd(z2, w_up3, "w_up")
    tok_r3 = gather_relay(g3, u)
    _after(tok_r3)
    h, uc = _conv_fwd(u, conv_w_f, conv_b)
    w_down_f = gather_finish(g3, h)["w_down"].reshape(ff, d)
    ffn = _mm(h, w_down_f, "nn", F32, "w_down", tm=1024, tn=1024, tk=2816)

    def to_shards(g):
        return jnp.transpose(g.reshape(g.shape[0], N_DEV, g.shape[1] // N_DEV), (1, 0, 2))

    def reduce_start(tag, names, sends):
        n = len(sends)
        land = [lax.empty((4,) + s.shape[1:], s.dtype) for s in sends]
        s, r, arrs, tok = _split_start("reduce_d2d_start_" + tag, sends + land, _reduce_d2d_copies(n), 4 * n)
        return dict(tag=tag, names=names, s=s, r=r, arrs=arrs, tok=tok)

    def reduce_relay(g, after):
        n = len(g["names"])
        arrs = _split_wait("reduce_d2d_wait_" + g["tag"], g["s"], g["r"], g["arrs"], _reduce_d2d_copies(n), after)
        sums = [_pair_sum(arrs[a], arrs[n + a], c_idx, "pair_sum_" + g["names"][a]) for a in range(n)]
        land = [lax.empty(s.shape, s.dtype) for s in sums]
        s, r, arrs2, tok = _split_start("reduce_ici_start_" + g["tag"], sums + land, _reduce_ici_copies(n), 4 * n)
        g.update(s2=s, r2=r, arrs2=arrs2)
        return tok

    def reduce_finish(g, after):
        n = len(g["names"])
        arrs2 = _split_wait("reduce_ici_wait_" + g["tag"], g["s2"], g["r2"], g["arrs2"], _reduce_ici_copies(n), after)
        return dict(zip(g["names"], arrs2[n:]))

    dx2, dffn, st_fin = _final_loss(x1, ffn, final_norm_g[None, :], mods3, tgt)
    dh = _mm(dffn, w_down_f, "nt", BF16, "d_h", tm=1024, tn=1024)
    g_w_down = _mm(h, dffn, "tn", BF16, "g_w_down", tm=512, tn=1024)
    r_down = reduce_start("down", ["w_down"], [g_w_down.reshape(N_DEV, ff // N_DEV, d)])
    _after(r_down["tok"])
    du3, dcw, dcb = _conv_bwd(u, uc, conv_w_f, dh)
    dz2 = _mm_up_dz(du3, w_up3, "d_z2")
    g_w_up = _mm_up_gw(z2, du3, N_DEV, "g_w_up")
    g_conv_w = jnp.concatenate([dcw[0], dcw[1]], axis=1)
    tok = reduce_relay(r_down, g_w_up)
    _after(tok)
    r_up = reduce_start("up", ["w_up", "conv_w"], [g_w_up, to_shards(jnp.pad(g_conv_w, ((0, 5), (0, 0))))])
    _after(tok, r_up["tok"])
    dx1, dattn, st_n2 = _norm2_bwd(x1, attn, norm2_g, mods2b, dz2, dx2)
    dmerged = _mm(dattn, w_out_f, "nt", BF16, "d_merged", tm=1024, tn=1024)
    g_w_out = _mm(merged, dattn, "tn", BF16, "g_w_out", tm=1024, tn=1024)
    dpa, dpb, dgates = _merge_bwd(dmerged, pa, pb, qg, gate_blk)
    do_a = _mm(dpa, w_bra, "nt", BF16, "d_o_a", tm=1024, tn=1024)
    do_b = _mm(dpb, w_brb, "nt", BF16, "d_o_b", tm=1024, tn=1024)
    g_w_bra = _mm(o_a, dpa, "tn", BF16, "g_w_br_a", tm=1024, tn=1024)
    g_w_brb = _mm(o_b, dpb, "tn", BF16, "g_w_br_b", tm=1024, tn=1024)
    tok = reduce_relay(r_up, g_w_brb)
    _after(tok)
    r_out = reduce_start("out", ["w_out", "w_br_a", "w_br_b"],
                         [g_w_out.reshape(N_DEV, d // N_DEV, d), to_shards(g_w_bra), to_shards(g_w_brb)])
    _after(tok, r_out["tok"])
    dq_a, dk_a, dv_a = _attention_bwd(q_a, kv_a, kv_a, do_a, lse_a, name="attn_a_bwd", **att_a)
    dq_b, dk_b, dv_b = _attention_bwd(q_b, k_b, v_b, do_b, lse_b, name="attn_b_bwd", **att_b)
    _after(reduce_relay(r_out, dv_b))
    dqa_raw = _rope_a(dq_a, q_tabs_a, True, BF16, "rope_q_bwd", sc_a * LN2)
    dcqn = _mm(dqa_raw, wqt_ext, "nn", F32, "d_cqn", tm=1024, tn=ql)
    g_wqt_ext = _mm(dqa_raw, cqn, "tn", BF16, "g_w_q_up", tm=1024, tn=ql)
    dq_p, st_q, st_qb = _q_prep_bwd(qg, mla_q_norm_g, gqa_q_norm_g, q_tabs_b, dcqn, dq_b, q_pad, sc_b * LN2)
    dkin = _mm_cat_nt([(dk_a, wkv_ext, 0), (dv_a, wkv_ext, ha * MLA_SLOT)], F32, "d_kin", tm=1152, tn=kvl + LANE)
    g_wkv_ext = _mm_cat_tn(kin, [dk_a, dv_a], BF16, "g_w_kv_up", tm=kvl + LANE, tn=min(1024, ha * MLA_V))
    dkv_p, st_kv, st_kb = _key_prep_bwd(kv_all, mla_kv_norm_g, gqa_k_norm_g, k_tabs, dkin, dk_b, dv_b)
    g_wqt = g_wqt_ext.reshape(ha, MLA_SLOT, ql)[:, :MLA_NOPE + MLA_ROPE, :].reshape(N_DEV, -1, ql)
    g_wkv = jnp.concatenate([g_wkv_ext[:kvl, :ha * MLA_SLOT].reshape(kvl, ha, MLA_SLOT)[:, :, :MLA_NOPE],
                             g_wkv_ext[:kvl, ha * MLA_SLOT:].reshape(kvl, ha, MLA_V)], axis=2).reshape(kvl, ha * (MLA_NOPE + MLA_V))
    r_qkv = reduce_start("qkv", ["w_q_up", "w_kv_up"], [g_wqt, to_shards(g_wkv)])
    _after(r_qkv["tok"])
    g_wkv_p = _mm(dkv_p, z_all, "tn", BF16, "g_w_in_kv", tm=wkv_w, tn=1024)
    g_wqg_p = _mm_rows_tn([dq_p, dgates], z_all, BF16, "g_w_in_qg", tm=min(1024, d), tn=1024, rows=t)
    g_wt_in = jnp.concatenate([g_wkv_p[:kvl], g_wkv_p[kvl + 2 * nb:kvl + 2 * nb + MLA_ROPE],
                               g_wkv_p[kvl:kvl + 2 * nb], g_wqg_p[:q_w], g_wqg_p[q_w + q_pad:]], axis=0)
    r_in = reduce_start("in", ["w_in"], [g_wt_in.reshape(N_DEV, -1, d)])
    _after(r_in["tok"])
    qw_p = q_w + q_pad
    dz_lat = _mm_sum_nn([(dq_p, 0, wt_qg_p, 0, qw_p), (dgates, 0, wt_qg_p, qw_p, d), (dgates, d, wt_qg_p, qw_p + d, d),
                         (dkv_p, 0, wt_kv_p, 0, wkv_w)], F32, "d_z_lat", rows=t)
    dz_ctx = _mm(dkv_p, wt_kv_p, "nn", F32, "d_z_ctx", tm=min(ROW_BLOCK, tc), tn=1024, a_row_off=t)
    tok_q = reduce_relay(r_qkv, dz_ctx)
    _after(tok_q)
    grad_x, st_n1 = _norm1_bwd(cts, xs, norm1_g, mods1, dz_ctx, dz_lat, dx1)

    res = {}

    def upd(nm, parts):
        wv, mv, vv = weights[nm], mom_m[nm], mom_v[nm]
        if wv.ndim == 1:
            wv, mv, vv = (a.reshape(1, -1) for a in (wv, mv, vv))
        if nm in narrow:
            wv, mv, vv = tview(wv), tview(mv), tview(vv)
        outs = _adamw(parts, wv, mv, vv, "adamw_" + nm)
        if nm in narrow:
            outs = [tview(o_) for o_ in outs]
        res[nm] = [o_.reshape(weights[nm].shape) for o_ in outs]

    d_lat = jnp.concatenate([st_n1[0], st_n1[1], st_n2[3], st_n2[0], st_n2[1], st_fin[1]])
    d_cxt = jnp.concatenate([st_n1[3], st_n1[4], jnp.zeros((4 * d,), F32)])
    small = jnp.concatenate([d_lat, d_cxt, st_n1[2], st_q[0], st_kv[0], st_qb[0], st_kb[0], st_n2[2],
                             jnp.concatenate([dcb[0, 0], dcb[1, 0]]), st_fin[0], st_fin[3, :LANE]])
    n_small = small.shape[0]
    pad_small = (-n_small) % LANE
    (small_all,) = _all_gather([jnp.pad(small, (0, pad_small)).reshape(1, -1)], "gather_small")
    offs = {}
    o = 0
    for nm, ln in (("d_lat", 6 * d), ("d_cxt", 6 * d), ("norm1_g", d), ("mla_q_norm_g", ql), ("mla_kv_norm_g", kvl),
                   ("gqa_q_norm_g", GQA_HEAD_DIM), ("gqa_k_norm_g", GQA_HEAD_DIM), ("norm2_g", d), ("conv_b", f2),
                   ("final_norm_g", d), ("loss", LANE)):
        offs[nm] = (o, ln)
        o += ln

    def part(nm):
        a, ln = offs[nm]
        return small_all[:, :, a:a + ln]

    loss = _sum_parts(part("loss"))[0, 0]
    d_lat_all = part("d_lat")[:, 0, :]
    d_cxt_sum = _sum_parts(part("d_cxt"))
    da16 = jnp.concatenate([d_lat_all, d_cxt_sum, jnp.zeros((7, 6 * d), F32)], axis=0)
    da16_shard = lax.dynamic_slice_in_dim(da16, my_idx * ncol, ncol, axis=1)
    cc_part = _cctx_partial(da16_shard, w_ada[0], c_ctx[None, :])
    (cc_all,) = _all_gather([cc_part], "gather_cctx")
    cc_parts = cc_all[:, 0:1, :]
    tok_i = reduce_relay(r_in, cc_all)

    _after(tok_i)
    for nm in ("norm1_g", "mla_q_norm_g", "mla_kv_norm_g", "gqa_q_norm_g", "gqa_k_norm_g", "norm2_g", "conv_b",
               "final_norm_g"):
        upd(nm, part(nm))
    upd("c_ctx", cc_parts)
    b_parts = jnp.concatenate([d_lat_all[:, None, :], d_cxt_sum[None]], axis=0)
    upd("b_ada", b_parts)
    _after(tok_i)
    outs = _adamw_ada(conds, da16_shard, w_ada[0], m_w_ada[0], v_w_ada[0])
    res["w_ada"] = [o_[None] for o_ in outs]
    last = outs[0]
    done = [last]
    for grp in (r_down, r_up, r_out, r_qkv, r_in):
        _after(*done)
        recv = reduce_finish(grp, last)
        for nm in grp["names"]:
            upd(nm, recv[nm][:, :3, :] if nm == "conv_w" else recv[nm])
            last = res[nm][0]
            done.append(last)

    return (loss, grad_x[None], *[res[n][0] for n in order], *[res[n][1] for n in order],
            *[res[n][2] for n in order], *[res[n][3] for n in order])
```

```python
import jax
import jax.numpy as jnp
from jax import lax
from jax.experimental import pallas as pl
from jax.experimental.pallas import tpu as pltpu

F32 = jnp.float32
BF16 = jnp.bfloat16

GRID_W = 64
ROPE_THETA = 10000.0
NORM_EPS = 1e-6
MLA_HEADS = 8
MLA_Q_LORA = 768
MLA_KV_LORA = 512
MLA_NOPE = 128
MLA_ROPE = 64
MLA_V = 128
GQA_HEADS = 8
GQA_KV_HEADS = 2
GQA_HEAD_DIM = 128
ADAM_LR = 0.001
ADAM_B1 = 0.9
ADAM_B2 = 0.999
ADAM_EPS = 1e-08
ADAM_WD = 0.01
ADAM_STEP = 10

N_DEV = 8
LANE = 128
MLA_SLOT = 2 * LANE
VMEM_LIMIT = 56 * 1024 * 1024
ROW_BLOCK = 256
ATT_Q_BLOCK = 512
ATT_Q_BLOCK_FWD = 512
LN2 = 0.6931471805599453
LOG2E = 1.4426950408889634
MESH_ID = pl.DeviceIdType.MESH


def _tile(n, pref, align=LANE):
    if n <= pref:
        return n
    best = None
    t = align
    while t <= pref:
        if n % t == 0:
            best = t
        t += align
    assert best is not None, (n, pref, align)
    return best


def _cparams(sem=None):
    return pltpu.CompilerParams(dimension_semantics=sem, vmem_limit_bytes=VMEM_LIMIT)


_ORDER_AFTER = []


def _after(*arrays):
    _ORDER_AFTER.extend(arrays)


def _pcall(body, *, in_specs, **kw):
    deps = tuple(_ORDER_AFTER)
    _ORDER_AFTER.clear()
    if not deps:
        return pl.pallas_call(body, in_specs=in_specs, **kw)
    n_in, n_dep = len(in_specs), len(deps)

    def with_deps(*refs):
        body(*refs[:n_in], *refs[n_in + n_dep:])

    call = pl.pallas_call(with_deps, in_specs=list(in_specs) + [pl.BlockSpec(memory_space=pl.ANY)] * n_dep, **kw)
    return lambda *args: call(*args, *deps)


def _all_gather(arrs, name):
    n = len(arrs)

    def body(*refs):
        ins = refs[:n]
        outs = refs[n:2 * n]
        send_sems, recv_sems, local_sems = refs[2 * n:]
        x, y, c = lax.axis_index("x"), lax.axis_index("y"), lax.axis_index("c")
        me, sibling = (x, y, c), (x, y, 1 - c)
        chips = [(1 - x, y), (x, 1 - y), (1 - x, 1 - y)]

        def rows(a, dev):
            px, py, pc = dev
            return outs[a].at[4 * px + 2 * py + pc]

        def copy(a, k, block, to, src=None):
            return pltpu.make_async_remote_copy(
                src_ref=rows(a, block) if src is None else src,
                dst_ref=rows(a, block),
                send_sem=send_sems.at[7 * a + k],
                recv_sem=recv_sems.at[7 * a + k],
                device_id=to,
                device_id_type=MESH_ID,
            )

        mine = [pltpu.make_async_copy(ins[a], rows(a, me), local_sems.at[a]) for a in range(n)]
        for cp in mine:
            cp.start()
        first = []
        for a in range(n):
            first.append(copy(a, 0, me, sibling, src=ins[a]))
            first += [copy(a, 1 + j, me, (*chip, c), src=ins[a]) for j, chip in enumerate(chips)]
        for cp in first:
            cp.start()
        passed = []
        for j, chip in enumerate(chips):
            for a in range(n):
                copy(a, 1 + j, (*chip, c), me).wait_recv()
                fwd = copy(a, 4 + j, (*chip, c), sibling)
                fwd.start()
                passed.append(fwd)
        for a in range(n):
            copy(a, 0, sibling, me).wait_recv()
            for j, chip in enumerate(chips):
                copy(a, 4 + j, (*chip, 1 - c), me).wait_recv()
        for cp in first + passed:
            cp.wait_send()
        for cp in mine:
            cp.wait()

    any_spec = pl.BlockSpec(memory_space=pl.ANY)
    outs = _pcall(
        body,
        name=name,
        out_shape=[jax.ShapeDtypeStruct((N_DEV,) + a.shape, a.dtype) for a in arrs],
        in_specs=[any_spec] * n,
        out_specs=[any_spec] * n,
        scratch_shapes=[
            pltpu.SemaphoreType.DMA((7 * n,)),
            pltpu.SemaphoreType.DMA((7 * n,)),
            pltpu.SemaphoreType.DMA((n,)),
        ],
    )(*arrs)
    return list(outs)


_HBM = pl.BlockSpec(memory_space=pltpu.HBM)
_SEM = pl.BlockSpec(memory_space=pltpu.SEMAPHORE)
_EFFECT = pltpu.SideEffectType.DATAFLOW_SIDE_EFFECTING


def _descriptors(copies, send_sems, recv_sems):
    descs = []
    for i, (src, dst, dev) in enumerate(copies):
        if dev is None:
            descs.append(pltpu.make_async_copy(src, dst, recv_sems.at[i]))
        else:
            descs.append(pltpu.make_async_remote_copy(src_ref=src, dst_ref=dst, send_sem=send_sems.at[i],
                                                      recv_sem=recv_sems.at[i], device_id=dev, device_id_type=MESH_ID))
    return descs


def _split_start(name, arrays, copies_fn, n_copies):
    n = len(arrays)

    def body(*refs):
        send_sems, recv_sems = refs[n], refs[n + 1]
        token = refs[2 * n + 2]
        for dsc in _descriptors(copies_fn(refs[:n]), send_sems, recv_sems):
            dsc.start()
        token[...] = jnp.zeros_like(token)

    outs = _pcall(
        body,
        name=name,
        out_shape=(pltpu.SemaphoreType.DMA((n_copies,)), pltpu.SemaphoreType.DMA((n_copies,)),
                   *[pltpu.HBM(a.shape, a.dtype) for a in arrays], jax.ShapeDtypeStruct((8, LANE), F32)),
        in_specs=[_HBM] * n,
        out_specs=(_SEM, _SEM, *[_HBM] * n, pl.BlockSpec(memory_space=pltpu.VMEM)),
        input_output_aliases={i: 2 + i for i in range(n)},
        compiler_params=pltpu.CompilerParams(has_side_effects=_EFFECT),
    )(*[pltpu.with_memory_space_constraint(a, pltpu.HBM) for a in arrays])
    return outs[0], outs[1], list(outs[2:2 + n]), outs[2 + n]


def _split_wait(name, send_sems, recv_sems, arrays, copies_fn, after):
    n = len(arrays)

    def body(*refs):
        for dsc, (_, _, dev) in zip(_descriptors(copies_fn(refs[:n]), refs[n], refs[n + 1]), copies_fn(refs[:n])):
            if dev is None:
                dsc.wait()
            else:
                dsc.wait_send()
                dsc.wait_recv()

    outs = _pcall(
        body,
        name=name,
        out_shape=tuple(pltpu.HBM(a.shape, a.dtype) for a in arrays),
        in_specs=[_HBM] * n + [_SEM, _SEM, pl.BlockSpec(memory_space=pl.ANY)],
        out_specs=tuple([_HBM] * n),
        input_output_aliases={i: i for i in range(n)},
        compiler_params=pltpu.CompilerParams(has_side_effects=_EFFECT),
    )(*arrays, send_sems, recv_sems, after)
    return list(outs)


def _mesh_pos():
    x, y, c = lax.axis_index("x"), lax.axis_index("y"), lax.axis_index("c")
    return x, y, c, [(1 - x, y), (x, 1 - y), (1 - x, 1 - y)]


def _gather_ici_copies(n):
    def copies(refs):
        x, y, c, chips = _mesh_pos()
        me = 4 * x + 2 * y + c
        out = []
        for a in range(n):
            src, buf = refs[a], refs[n + a]
            out.append((src, buf.at[me], None))
            out.append((src, buf.at[me], (x, y, 1 - c)))
            out += [(src, buf.at[me], (cx, cy, c)) for cx, cy in chips[:2]]
        return out
    return copies


def _gather_pass_copies(n):
    def copies(refs):
        x, y, c, chips = _mesh_pos()
        south = c == 0
        bx, by = jnp.where(south, 1 - x, x), jnp.where(south, y, 1 - y)
        tx, ty = jnp.where(south, x, 1 - x), jnp.where(south, 1 - y, y)
        out = []
        for a in range(n):
            rows = refs[a].at[4 * bx + 2 * by + c]
            out.append((rows, rows, (tx, ty, c)))
            for cx, cy in chips[:2]:
                rows = refs[a].at[4 * cx + 2 * cy + c]
                out.append((rows, rows, (x, y, 1 - c)))
        return out
    return copies


def _gather_d2d_copies(n):
    def copies(refs):
        x, y, c, chips = _mesh_pos()
        cx, cy = chips[2]
        out = []
        for a in range(n):
            rows = refs[a].at[4 * cx + 2 * cy + c]
            out.append((rows, rows, (x, y, 1 - c)))
        return out
    return copies


def _reduce_d2d_copies(n):
    def copies(refs):
        x, y, c, _ = _mesh_pos()
        out = []
        for a in range(n):
            for k in range(4):
                out.append((refs[a].at[2 * k + (1 - c)], refs[n + a].at[k], (x, y, 1 - c)))
        return out
    return copies


def _reduce_ici_copies(n):
    def copies(refs):
        x, y, c, chips = _mesh_pos()
        mine = 2 * x + y
        out = []
        for a in range(n):
            src, land = refs[a], refs[n + a]
            out.append((src.at[mine], land.at[mine], None))
            out += [(src.at[2 * cx + cy], land.at[mine], (cx, cy, c)) for cx, cy in chips]
        return out
    return copies


def _pair_sum(send, land, c_idx, name):
    _, r, cols = send.shape
    rb = _tile(r, max(8, (1 << 22) // (send.dtype.itemsize * cols) // 8 * 8), 8)
    dt = send.dtype

    def body(c_ref, s_ref, l_ref, o_ref):
        o_ref[...] = (s_ref[...].astype(F32) + l_ref[...].astype(F32)).astype(dt)

    return pl.pallas_call(
        body,
        name=name,
        out_shape=jax.ShapeDtypeStruct((4, r, cols), dt),
        grid_spec=pltpu.PrefetchScalarGridSpec(
            num_scalar_prefetch=1,
            grid=(4, r // rb),
            in_specs=[pl.BlockSpec((None, rb, cols), lambda k, i, c_ref: (2 * k + c_ref[0], i, 0)),
                      pl.BlockSpec((None, rb, cols), lambda k, i, c_ref: (k, i, 0))],
            out_specs=pl.BlockSpec((None, rb, cols), lambda k, i, c_ref: (k, i, 0)),
        ),
        compiler_params=_cparams(("parallel", "parallel")),
    )(c_idx, send, land)


_DIMS = {
    "nn": (((1,), (0,)), ((), ())),
    "nt": (((1,), (1,)), ((), ())),
    "tn": (((0,), (0,)), ((), ())),
}


def _mm_call(a, b, *, mode, grid, a_spec, b_spec, o_spec, out_shape, acc_shape, name):
    nk = grid[2]
    out_dtype = out_shape.dtype

    def body(a_ref, b_ref, o_ref, *scratch):
        p = lax.dot_general(a_ref[...].astype(BF16), b_ref[...].astype(BF16), _DIMS[mode],
                            preferred_element_type=F32)
        if nk == 1:
            o_ref[...] = p.astype(out_dtype)
        else:
            acc = scratch[0]
            k = pl.program_id(2)

            @pl.when(k == 0)
            def _():
                acc[...] = p

            @pl.when(k > 0)
            def _():
                acc[...] += p

            @pl.when(k == nk - 1)
            def _():
                o_ref[...] = acc[...].astype(out_dtype)

    return _pcall(
        body,
        name=name,
        out_shape=out_shape,
        grid=grid,
        in_specs=[a_spec, b_spec],
        out_specs=o_spec,
        scratch_shapes=[pltpu.VMEM(acc_shape, F32)] if nk > 1 else [],
        compiler_params=_cparams(("parallel", "parallel", "arbitrary")),
    )(a, b)


def _mm(a, b, mode, out_dtype, name, tm=512, tn=512, tk=2432, a_row_off=0, rows=None):
    if mode == "nn":
        (m, k), (k2, n) = a.shape, b.shape
    elif mode == "nt":
        (m, k), (n, k2) = a.shape, b.shape
    else:
        (k, m), (k2, n) = a.shape, b.shape
        if rows is not None:
            k = k2 = rows
    assert k == k2, (a.shape, b.shape, mode)
    if mode != "tn":
        m = (m if rows is None else rows + a_row_off) - a_row_off
    tm, tn, tk = _tile(m, tm, 8), _tile(n, tn), _tile(k, tk, 8 if mode == "tn" else LANE)
    assert a_row_off % tm == 0
    ro = a_row_off // tm
    grid = (m // tm, n // tn, k // tk)
    if mode == "tn":
        a_spec = pl.BlockSpec((tk, tm), lambda i, j, kk: (kk, i))
    else:
        a_spec = pl.BlockSpec((tm, tk), lambda i, j, kk: (i + ro, kk))
    if mode == "nt":
        b_spec = pl.BlockSpec((tn, tk), lambda i, j, kk: (j, kk))
    else:
        b_spec = pl.BlockSpec((tk, tn), lambda i, j, kk: (kk, j))
    o_spec = pl.BlockSpec((tm, tn), lambda i, j, kk: (i, j))
    return _mm_call(a, b, mode=mode, grid=grid, a_spec=a_spec, b_spec=b_spec, o_spec=o_spec,
                    out_shape=jax.ShapeDtypeStruct((m, n), out_dtype), acc_shape=(tm, tn), name=name)


def _mm_cat_nt(pieces, out_dtype, name, tm=1024, tn=1024, tk=2048, rows=None):
    m = pieces[0][0].shape[0] if rows is None else rows
    n = pieces[0][1].shape[0]
    tm, tn = _tile(m, tm, 8), _tile(n, tn)
    steps, starts, s = [], [], 0
    for a, b, off in pieces:
        kp = a.shape[1]
        tkp = _tile(kp, tk)
        assert off % tkp == 0 and b.shape[0] == n
        steps.append((tkp, kp // tkp, off // tkp))
        starts.append(s)
        s += kp // tkp
    nk = s
    npc = len(pieces)

    def body(*refs):
        o_ref, acc = refs[2 * npc], refs[2 * npc + 1]
        kk = pl.program_id(2)

        @pl.when(kk == 0)
        def _():
            acc[...] = jnp.zeros_like(acc)

        for p in range(npc):
            @pl.when((kk >= starts[p]) & (kk < starts[p] + steps[p][1]))
            def _(p=p):
                acc[...] += lax.dot_general(refs[2 * p][...].astype(BF16), refs[2 * p + 1][...].astype(BF16), _DIMS["nt"],
                                            preferred_element_type=F32)

        @pl.when(kk == nk - 1)
        def _():
            o_ref[...] = acc[...].astype(out_dtype)

    in_specs, args = [], []
    for p, (a, b, off) in enumerate(pieces):
        tkp, np_, ob = steps[p]

        def rel(kk, p=p, np_=np_):
            return jnp.clip(kk - starts[p], 0, np_ - 1)

        in_specs.append(pl.BlockSpec((tm, tkp), lambda i, j, kk, rel=rel: (i, rel(kk))))
        in_specs.append(pl.BlockSpec((tn, tkp), lambda i, j, kk, rel=rel, ob=ob: (j, ob + rel(kk))))
        args += [a, b]
    return _pcall(
        body,
        name=name,
        out_shape=jax.ShapeDtypeStruct((m, n), out_dtype),
        grid=(m // tm, n // tn, nk),
        in_specs=in_specs,
        out_specs=pl.BlockSpec((tm, tn), lambda i, j, kk: (i, j)),
        scratch_shapes=[pltpu.VMEM((tm, tn), F32)],
        compiler_params=_cparams(("parallel", "parallel", "arbitrary")),
    )(*args)


def _mm_cat_tn(a, pieces, out_dtype, name, tm=1024, tn=1024, rows=None):
    k = a.shape[0] if rows is None else rows
    m = a.shape[1]
    tm = _tile(m, tm)
    starts, s = [], 0
    for b in pieces:
        assert b.shape[1] % tn == 0
        starts.append(s)
        s += b.shape[1] // tn
    nj = s
    npc = len(pieces)

    def body(*refs):
        a_ref, o_ref = refs[0], refs[1 + npc]
        j = pl.program_id(1)
        for p in range(npc):
            @pl.when((j >= starts[p]) & (j < starts[p] + pieces[p].shape[1] // tn))
            def _(p=p):
                o_ref[...] = lax.dot_general(a_ref[...].astype(BF16), refs[1 + p][...].astype(BF16), _DIMS["tn"],
                                             preferred_element_type=F32).astype(out_dtype)

    in_specs = [pl.BlockSpec((k, tm), lambda i, j: (0, i))]
    for p, b in enumerate(pieces):
        np_ = b.shape[1] // tn
        in_specs.append(pl.BlockSpec((k, tn), lambda i, j, p=p, np_=np_: (0, jnp.clip(j - starts[p], 0, np_ - 1))))
    return _pcall(
        body,
        name=name,
        out_shape=jax.ShapeDtypeStruct((m, nj * tn), out_dtype),
        grid=(m // tm, nj),
        in_specs=in_specs,
        out_specs=pl.BlockSpec((tm, tn), lambda i, j: (i, j)),
        compiler_params=_cparams(("parallel", "arbitrary")),
    )(a, *pieces)


def _mm_up_fwd(z2, w3, name, tm=1024):
    t, d = z2.shape
    nsh, _, c = w3.shape
    tm = _tile(t, tm, 8)
    return _mm_call(z2, w3, mode="nn", grid=(t // tm, nsh, 1),
                    a_spec=pl.BlockSpec((tm, d), lambda i, j, kk: (i, 0)),
                    b_spec=pl.BlockSpec((None, d, c), lambda i, j, kk: (j, 0, 0)),
                    o_spec=pl.BlockSpec((tm, c), lambda i, j, kk: (i, j)),
                    out_shape=jax.ShapeDtypeStruct((t, nsh * c), BF16), acc_shape=(tm, c), name=name)


def _mm_up_dz(du3, w3, name, tm=512, tn=1024):
    _, t, f = du3.shape
    nsh, d, c = w3.shape
    half = nsh // 2
    assert f == half * c
    tm, tn = _tile(t, tm, 8), _tile(d, tn)

    def body(a_ref, b_ref, o_ref, acc):
        kk = pl.program_id(2)
        p = None
        for s in range(half):
            q = lax.dot_general(a_ref[:, s * c:(s + 1) * c], b_ref[s], _DIMS["nt"], preferred_element_type=F32)
            p = q if p is None else p + q

        @pl.when(kk == 0)
        def _():
            acc[...] = p

        @pl.when(kk == 1)
        def _():
            o_ref[...] = (acc[...] + p).astype(BF16)

    return _pcall(
        body,
        name=name,
        out_shape=jax.ShapeDtypeStruct((t, d), BF16),
        grid=(t // tm, d // tn, 2),
        in_specs=[pl.BlockSpec((None, tm, f), lambda i, j, kk: (kk, i, 0)),
                  pl.BlockSpec((half, tn, c), lambda i, j, kk: (kk, j, 0))],
        out_specs=pl.BlockSpec((tm, tn), lambda i, j, kk: (i, j)),
        scratch_shapes=[pltpu.VMEM((tm, tn), F32)],
        compiler_params=_cparams(("parallel", "parallel", "arbitrary")),
    )(du3, w3)


def _mm_sum_nn(pieces, out_dtype, name, tm=512, tn=512, rows=None):
    m = pieces[0][0].shape[0] if rows is None else rows
    n = pieces[0][2].shape[1]
    tm, tn = _tile(m, tm, 8), _tile(n, tn)
    npc = len(pieces)

    def body(*refs):
        p = None
        for s in range(npc):
            q = jnp.dot(refs[2 * s][...].astype(BF16), refs[2 * s + 1][...].astype(BF16), preferred_element_type=F32)
            p = q if p is None else p + q
        refs[2 * npc][...] = p.astype(out_dtype)

    in_specs, args = [], []
    for a, ao, b, bo, kp in pieces:
        assert ao % kp == 0 and bo % kp == 0 and b.shape[1] == n
        in_specs.append(pl.BlockSpec((tm, kp), lambda i, j, ab=ao // kp: (i, ab)))
        in_specs.append(pl.BlockSpec((kp, tn), lambda i, j, bb=bo // kp: (bb, j)))
        args += [a, b]
    return _pcall(
        body,
        name=name,
        out_shape=jax.ShapeDtypeStruct((m, n), out_dtype),
        grid=(m // tm, n // tn),
        in_specs=in_specs,
        out_specs=pl.BlockSpec((tm, tn), lambda i, j: (i, j)),
        compiler_params=_cparams(("parallel", "parallel")),
    )(*args)


def _mm_rows_tn(pieces, b, out_dtype, name, tm=1024, tn=1024, rows=None):
    k = b.shape[0] if rows is None else rows
    n = b.shape[1]
    tn = _tile(n, tn)
    starts, s = [], 0
    for a in pieces:
        assert a.shape[1] % tm == 0
        starts.append(s)
        s += a.shape[1] // tm
    ni = s
    npc = len(pieces)

    def body(*refs):
        b_ref, o_ref = refs[npc], refs[npc + 1]
        i = pl.program_id(0)
        for p in range(npc):
            @pl.when((i >= starts[p]) & (i < starts[p] + pieces[p].shape[1] // tm))
            def _(p=p):
                o_ref[...] = lax.dot_general(refs[p][...].astype(BF16), b_ref[...].astype(BF16), _DIMS["tn"],
                                             preferred_element_type=F32).astype(out_dtype)

    in_specs = []
    for p, a in enumerate(pieces):
        np_ = a.shape[1] // tm
        in_specs.append(pl.BlockSpec((k, tm), lambda i, j, p=p, np_=np_: (0, jnp.clip(i - starts[p], 0, np_ - 1))))
    in_specs.append(pl.BlockSpec((k, tn), lambda i, j: (0, j)))
    return _pcall(
        body,
        name=name,
        out_shape=jax.ShapeDtypeStruct((ni * tm, n), out_dtype),
        grid=(ni, n // tn),
        in_specs=in_specs,
        out_specs=pl.BlockSpec((tm, tn), lambda i, j: (i, j)),
        compiler_params=_cparams(("parallel", "parallel")),
    )(*pieces, b)


def _mm_up_gw(z2, du3, nsh, name, tm=1024):
    t, d = z2.shape
    f = du3.shape[2]
    half = nsh // 2
    c = f // half
    tm = _tile(d, tm)
    return _mm_call(z2, du3, mode="tn", grid=(d // tm, nsh, 1),
                    a_spec=pl.BlockSpec((t, tm), lambda i, j, kk: (0, i)),
                    b_spec=pl.BlockSpec((None, t, c), lambda i, j, kk: (j // half, 0, j % half)),
                    o_spec=pl.BlockSpec((None, tm, c), lambda i, j, kk: (j, i, 0)),
                    out_shape=jax.ShapeDtypeStruct((nsh, d, c), BF16), acc_shape=(tm, c), name=name)


def _rms(x):
    r = lax.rsqrt(jnp.mean(x * x, axis=-1, keepdims=True) + NORM_EPS)
    return x * r, r


def _rms_bwd(dxh, xh, r):
    return r * (dxh - xh * jnp.mean(dxh * xh, axis=-1, keepdims=True))


def _colsum(v):
    return jnp.sum(v, axis=0, keepdims=True)


def _rope(v, c, s1, s2, q):
    w = v.shape[-1]
    return v * c + pltpu.roll(v, w - q, 1) * s1 + pltpu.roll(v, q, 1) * s2


def _rope_t(d, c, s1, s2, q):
    w = d.shape[-1]
    return d * c + pltpu.roll(d * s1, q, 1) + pltpu.roll(d * s2, w - q, 1)


def _norm_mod_fwd(ctx, x, gain, mods):
    tc, d = ctx.shape
    t = x.shape[0]
    rb = min(ROW_BLOCK, tc)
    nbl = t // rb

    def body(ctx_ref, x_ref, g_ref, mod_ref, z_ref):
        i = pl.program_id(0)

        def emit(src, sh, sc):
            xh, _ = _rms(src[...])
            z_ref[...] = ((xh * g_ref[...]) * (1.0 + sc) + sh).astype(BF16)

        @pl.when(i >= nbl)
        def _():
            emit(ctx_ref, mod_ref[2:3, :], mod_ref[3:4, :])

        @pl.when(i < nbl)
        def _():
            emit(x_ref, mod_ref[0:1, :], mod_ref[1:2, :])

    return _pcall(
        body,
        name="norm1_mod_fwd",
        out_shape=jax.ShapeDtypeStruct((tc + t, d), BF16),
        grid=((tc + t) // rb,),
        in_specs=[
            pl.BlockSpec((rb, d), lambda i: (jnp.maximum(i - nbl, 0), 0)),
            pl.BlockSpec((rb, d), lambda i: (jnp.minimum(i, nbl - 1), 0)),
            pl.BlockSpec((1, d), lambda i: (0, 0)),
            pl.BlockSpec((8, d), lambda i: (0, 0)),
        ],
        out_specs=pl.BlockSpec((rb, d), lambda i: (i, 0)),
        compiler_params=_cparams(("arbitrary",)),
    )(ctx, x, gain, mods)


def _norm1_bwd(ctx, x, gain, mods, dz_ctx, dz_lat, dx1):
    tc, d = ctx.shape
    t = x.shape[0]
    rb = min(ROW_BLOCK, tc)
    nbl = t // rb

    def body(ctx_ref, x_ref, g_ref, mod_ref, dzc_ref, dzl_ref, dx1_ref, gx_ref, st_ref):
        i = pl.program_id(0)

        @pl.when(i == 0)
        def _():
            st_ref[...] = jnp.zeros_like(st_ref)

        def common(src, dz, sc, row_sh, row_sc):
            xh, r = _rms(src[...])
            g = g_ref[...]
            dxn = dz * (1.0 + sc)
            st_ref[row_sh:row_sh + 1, :] += _colsum(dz)
            st_ref[row_sc:row_sc + 1, :] += _colsum(dz * (xh * g))
            st_ref[2:3, :] += _colsum(dxn * xh)
            return _rms_bwd(dxn * g, xh, r)

        @pl.when(i >= nbl)
        def _():
            common(ctx_ref, dzc_ref[...], mod_ref[3:4, :], 3, 4)

        @pl.when(i < nbl)
        def _():
            gx_ref[...] = dx1_ref[...] + common(x_ref, dzl_ref[...], mod_ref[1:2, :], 0, 1)

    lat = lambda i: (jnp.minimum(i, nbl - 1), 0)
    cix = lambda i: (jnp.maximum(i - nbl, 0), 0)
    return _pcall(
        body,
        name="norm1_mod_bwd",
        out_shape=[jax.ShapeDtypeStruct((t, d), F32), jax.ShapeDtypeStruct((8, d), F32)],
        grid=((tc + t) // rb,),
        in_specs=[
            pl.BlockSpec((rb, d), cix),
            pl.BlockSpec((rb, d), lat),
            pl.BlockSpec((1, d), lambda i: (0, 0)),
            pl.BlockSpec((8, d), lambda i: (0, 0)),
            pl.BlockSpec((rb, d), cix),
            pl.BlockSpec((rb, d), lat),
            pl.BlockSpec((rb, d), lat),
        ],
        out_specs=[pl.BlockSpec((rb, d), lat), pl.BlockSpec((8, d), lambda i: (0, 0))],
        compiler_params=_cparams(("arbitrary",)),
    )(ctx, x, gain, mods, dz_ctx, dz_lat, dx1)


def _key_prep_fwd(kv, kv_gain, kb_gain, tabs):
    ta, wkv = kv.shape
    kvl = MLA_KV_LORA
    nb = GQA_KV_HEADS * GQA_HEAD_DIM
    rb = ROW_BLOCK if ta % ROW_BLOCK == 0 else LANE
    hd = GQA_HEAD_DIM

    def body(kv_ref, g_ref, gb_ref, ca, s1a, s2a, cb, s1b, s2b, kin_ref, kb_ref, vb_ref):
        xh, _ = _rms(kv_ref[:, 0:kvl])
        kin_ref[:, 0:kvl] = (xh * g_ref[...]).astype(BF16)
        kpe = kv_ref[:, kvl + 2 * nb:kvl + 2 * nb + LANE]
        kin_ref[:, kvl:kvl + LANE] = _rope(kpe, ca[...], s1a[...], s2a[...], MLA_ROPE // 4).astype(BF16)
        for h in range(GQA_KV_HEADS):
            nh, _ = _rms(kv_ref[:, kvl + h * hd:kvl + (h + 1) * hd])
            kb_ref[:, h * hd:(h + 1) * hd] = _rope(nh * gb_ref[...], cb[...], s1b[...], s2b[...], hd // 4).astype(BF16)
        vb_ref[...] = kv_ref[:, kvl + nb:kvl + 2 * nb].astype(BF16)

    row = lambda w: pl.BlockSpec((rb, w), lambda i: (i, 0))
    fix = lambda w: pl.BlockSpec((1, w), lambda i: (0, 0))
    return _pcall(
        body,
        name="key_prep_fwd",
        out_shape=[jax.ShapeDtypeStruct((ta, kvl + LANE), BF16), jax.ShapeDtypeStruct((ta, nb), BF16),
                   jax.ShapeDtypeStruct((ta, nb), BF16)],
        grid=(ta // rb,),
        in_specs=[row(wkv), fix(kvl), fix(hd)] + [row(LANE)] * 3 + [row(hd)] * 3,
        out_specs=[row(kvl + LANE), row(nb), row(nb)],
        compiler_params=_cparams(("parallel",)),
    )(kv, kv_gain, kb_gain, *tabs)


def _key_prep_bwd(kv, kv_gain, kb_gain, tabs, dkin, dkb, dvb):
    ta, wkv = kv.shape
    kvl = MLA_KV_LORA
    nb = GQA_KV_HEADS * GQA_HEAD_DIM
    rb = ROW_BLOCK if ta % ROW_BLOCK == 0 else LANE
    hd = GQA_HEAD_DIM

    def body(kv_ref, g_ref, gb_ref, ca, s1a, s2a, cb, s1b, s2b, dkin_ref, dkb_ref, dvb_ref, dkv_ref, st_ref, stb_ref):
        @pl.when(pl.program_id(0) == 0)
        def _():
            st_ref[...] = jnp.zeros_like(st_ref)
            stb_ref[...] = jnp.zeros_like(stb_ref)

        xh, r = _rms(kv_ref[:, 0:kvl])
        dn = dkin_ref[:, 0:kvl]
        st_ref[0:1, :] += _colsum(dn * xh)
        dkv_ref[:, 0:kvl] = _rms_bwd(dn * g_ref[...], xh, r).astype(BF16)
        dpe = _rope_t(dkin_ref[:, kvl:kvl + LANE], ca[...], s1a[...], s2a[...], MLA_ROPE // 4)
        dkv_ref[:, kvl + 2 * nb:kvl + 2 * nb + LANE] = dpe.astype(BF16)
        for h in range(GQA_KV_HEADS):
            nh, rh = _rms(kv_ref[:, kvl + h * hd:kvl + (h + 1) * hd])
            dn_h = _rope_t(dkb_ref[:, h * hd:(h + 1) * hd], cb[...], s1b[...], s2b[...], hd // 4)
            stb_ref[0:1, :] += _colsum(dn_h * nh)
            dkv_ref[:, kvl + h * hd:kvl + (h + 1) * hd] = _rms_bwd(dn_h * gb_ref[...], nh, rh).astype(BF16)
        dkv_ref[:, kvl + nb:kvl + 2 * nb] = dvb_ref[...].astype(BF16)

    row = lambda w: pl.BlockSpec((rb, w), lambda i: (i, 0))
    fix = lambda w: pl.BlockSpec((1, w), lambda i: (0, 0))
    return _pcall(
        body,
        name="key_prep_bwd",
        out_shape=[jax.ShapeDtypeStruct((ta, wkv), BF16), jax.ShapeDtypeStruct((8, kvl), F32),
                   jax.ShapeDtypeStruct((8, hd), F32)],
        grid=(ta // rb,),
        in_specs=[row(wkv), fix(kvl), fix(hd)] + [row(LANE)] * 3 + [row(hd)] * 3 + [row(kvl + LANE), row(nb), row(nb)],
        out_specs=[row(wkv), pl.BlockSpec((8, kvl), lambda i: (0, 0)), pl.BlockSpec((8, hd), lambda i: (0, 0))],
        compiler_params=_cparams(("arbitrary",)),
    )(kv, kv_gain, kb_gain, *tabs, dkin, dkb, dvb)


def _q_prep_fwd(qg, q_gain, qb_gain, tabs, qscale):
    t = qg.shape[0]
    ql = MLA_Q_LORA
    hd = GQA_HEAD_DIM
    hb = GQA_HEADS * hd
    rb = min(ROW_BLOCK, t)

    def body(q_ref, g_ref, gb_ref, cb, s1b, s2b, cqn_ref, qb_ref):
        xh, _ = _rms(q_ref[:, 0:ql])
        cqn_ref[...] = (xh * g_ref[...]).astype(BF16)
        for h in range(GQA_HEADS):
            nh, _ = _rms(q_ref[:, ql + h * hd:ql + (h + 1) * hd])
            qh = _rope(nh * gb_ref[...], cb[...], s1b[...], s2b[...], hd // 4)
            qb_ref[:, h * hd:(h + 1) * hd] = (qh * qscale).astype(BF16)

    row = lambda w: pl.BlockSpec((rb, w), lambda i: (i, 0))
    fix = lambda w: pl.BlockSpec((1, w), lambda i: (0, 0))
    return _pcall(
        body,
        name="q_prep_fwd",
        out_shape=[jax.ShapeDtypeStruct((t, ql), BF16), jax.ShapeDtypeStruct((t, hb), BF16)],
        grid=(t // rb,),
        in_specs=[row(ql + hb), fix(ql), fix(hd)] + [row(hd)] * 3,
        out_specs=[row(ql), row(hb)],
        compiler_params=_cparams(("parallel",)),
    )(qg, q_gain, qb_gain, *tabs)


def _q_prep_bwd(qg, q_gain, qb_gain, tabs, dcqn, dqb, wpad, qscale):
    t = qg.shape[0]
    ql = MLA_Q_LORA
    hd = GQA_HEAD_DIM
    hb = GQA_HEADS * hd
    rb = min(ROW_BLOCK, t)

    def body(q_ref, g_ref, gb_ref, cb, s1b, s2b, dcqn_ref, dqb_ref, dq_ref, st_ref, stb_ref):
        @pl.when(pl.program_id(0) == 0)
        def _():
            st_ref[...] = jnp.zeros_like(st_ref)
            stb_ref[...] = jnp.zeros_like(stb_ref)

        xh, r = _rms(q_ref[:, 0:ql])
        dn = dcqn_ref[...]
        st_ref[0:1, :] += _colsum(dn * xh)
        dq_ref[:, 0:ql] = _rms_bwd(dn * g_ref[...], xh, r).astype(BF16)
        for h in range(GQA_HEADS):
            nh, rh = _rms(q_ref[:, ql + h * hd:ql + (h + 1) * hd])
            dn_h = _rope_t(dqb_ref[:, h * hd:(h + 1) * hd] * qscale, cb[...], s1b[...], s2b[...], hd // 4)
            stb_ref[0:1, :] += _colsum(dn_h * nh)
            dq_ref[:, ql + h * hd:ql + (h + 1) * hd] = _rms_bwd(dn_h * gb_ref[...], nh, rh).astype(BF16)
        if wpad:
            dq_ref[:, ql + hb:ql + hb + wpad] = jnp.zeros((rb, wpad), BF16)

    row = lambda w: pl.BlockSpec((rb, w), lambda i: (i, 0))
    fix = lambda w: pl.BlockSpec((1, w), lambda i: (0, 0))
    return _pcall(
        body,
        name="q_prep_bwd",
        out_shape=[jax.ShapeDtypeStruct((t, ql + hb + wpad), BF16), jax.ShapeDtypeStruct((8, ql), F32),
                   jax.ShapeDtypeStruct((8, hd), F32)],
        grid=(t // rb,),
        in_specs=[row(ql + hb), fix(ql), fix(hd)] + [row(hd)] * 3 + [row(ql), row(hb)],
        out_specs=[row(ql + hb + wpad), pl.BlockSpec((8, ql), lambda i: (0, 0)), pl.BlockSpec((8, hd), lambda i: (0, 0))],
        compiler_params=_cparams(("arbitrary",)),
    )(qg, q_gain, qb_gain, *tabs, dcqn, dqb)


def _rope_a(v, tabs, transpose, out_dtype, name, qscale):
    t, w = v.shape
    rb = min(ROW_BLOCK, t)
    fn = _rope_t if transpose else _rope

    def body(v_ref, c, s1, s2, o_ref):
        for h in range(w // MLA_SLOT):
            sl = slice(h * MLA_SLOT, (h + 1) * MLA_SLOT)
            o_ref[:, sl] = (fn(v_ref[:, sl].astype(F32), c[...], s1[...], s2[...], MLA_ROPE // 4) * qscale).astype(out_dtype)

    row = lambda ww: pl.BlockSpec((rb, ww), lambda i: (i, 0))
    return _pcall(
        body,
        name=name,
        out_shape=jax.ShapeDtypeStruct((t, w), out_dtype),
        grid=(t // rb,),
        in_specs=[row(w)] + [row(MLA_SLOT)] * 3,
        out_specs=row(w),
        compiler_params=_cparams(("parallel",)),
    )(v, *tabs)


def _merge_fwd(pa, pb, qg, gate_blk):
    t, d = pa.shape
    rb = min(ROW_BLOCK, t)

    def body(pa_ref, pb_ref, ga_ref, gb_ref, o_ref):
        o_ref[...] = (jax.nn.sigmoid(ga_ref[...]) * pa_ref[...].astype(F32)
                      + jax.nn.sigmoid(gb_ref[...]) * pb_ref[...].astype(F32)).astype(BF16)

    row = pl.BlockSpec((rb, d), lambda i: (i, 0))
    return _pcall(
        body,
        name="merge_fwd",
        out_shape=jax.ShapeDtypeStruct((t, d), BF16),
        grid=(t // rb,),
        in_specs=[row, row, pl.BlockSpec((rb, d), lambda i: (i, gate_blk)), pl.BlockSpec((rb, d), lambda i: (i, gate_blk + 1))],
        out_specs=row,
        compiler_params=_cparams(("parallel",)),
    )(pa, pb, qg, qg)


def _merge_bwd(dm, pa, pb, qg, gate_blk):
    t, d = pa.shape
    rb = min(ROW_BLOCK, t)

    def body(dm_ref, pa_ref, pb_ref, ga_ref, gb_ref, dpa_ref, dpb_ref, dg_ref):
        dmv = dm_ref[...].astype(F32)
        sa = jax.nn.sigmoid(ga_ref[...])
        sb = jax.nn.sigmoid(gb_ref[...])
        dpa_ref[...] = (dmv * sa).astype(BF16)
        dpb_ref[...] = (dmv * sb).astype(BF16)
        dg_ref[:, 0:d] = (dmv * pa_ref[...].astype(F32) * (sa * (1.0 - sa))).astype(BF16)
        dg_ref[:, d:2 * d] = (dmv * pb_ref[...].astype(F32) * (sb * (1.0 - sb))).astype(BF16)

    row = pl.BlockSpec((rb, d), lambda i: (i, 0))
    return _pcall(
        body,
        name="merge_bwd",
        out_shape=[jax.ShapeDtypeStruct((t, d), BF16), jax.ShapeDtypeStruct((t, d), BF16),
                   jax.ShapeDtypeStruct((t, 2 * d), BF16)],
        grid=(t // rb,),
        in_specs=[row, row, row, pl.BlockSpec((rb, d), lambda i: (i, gate_blk)), pl.BlockSpec((rb, d), lambda i: (i, gate_blk + 1))],
        out_specs=[row, row, pl.BlockSpec((rb, 2 * d), lambda i: (i, 0))],
        compiler_params=_cparams(("parallel",)),
    )(dm, pa, pb, qg, qg)


def _resid_norm_mod(x, branch, gain, mods, name):
    t, d = x.shape
    rb = min(ROW_BLOCK, t)

    def body(x_ref, b_ref, g_ref, mod_ref, x1_ref, z_ref):
        x1 = x_ref[...] + mod_ref[0:1, :] * b_ref[...]
        x1_ref[...] = x1
        xh, _ = _rms(x1)
        z_ref[...] = ((xh * g_ref[...]) * (1.0 + mod_ref[2:3, :]) + mod_ref[1:2, :]).astype(BF16)

    row = pl.BlockSpec((rb, d), lambda i: (i, 0))
    return _pcall(
        body,
        name=name,
        out_shape=[jax.ShapeDtypeStruct((t, d), F32), jax.ShapeDtypeStruct((t, d), BF16)],
        grid=(t // rb,),
        in_specs=[row, row, pl.BlockSpec((1, d), lambda i: (0, 0)), pl.BlockSpec((8, d), lambda i: (0, 0))],
        out_specs=[row, row],
        compiler_params=_cparams(("parallel",)),
    )(x, branch, gain, mods)


def _norm2_bwd(x1, attn, gain, mods, dz2, dx2):
    t, d = x1.shape
    rb = min(ROW_BLOCK, t)

    def body(x1_ref, at_ref, g_ref, mod_ref, dz_ref, dx2_ref, dx1_ref, da_ref, st_ref):
        @pl.when(pl.program_id(0) == 0)
        def _():
            st_ref[...] = jnp.zeros_like(st_ref)

        xh, r = _rms(x1_ref[...])
        g = g_ref[...]
        dz = dz_ref[...].astype(F32)
        dxn = dz * (1.0 + mod_ref[1:2, :])
        st_ref[0:1, :] += _colsum(dz)
        st_ref[1:2, :] += _colsum(dz * (xh * g))
        st_ref[2:3, :] += _colsum(dxn * xh)
        dx1 = dx2_ref[...] + _rms_bwd(dxn * g, xh, r)
        dx1_ref[...] = dx1
        st_ref[3:4, :] += _colsum(dx1 * at_ref[...])
        da_ref[...] = (dx1 * mod_ref[0:1, :]).astype(BF16)

    row = pl.BlockSpec((rb, d), lambda i: (i, 0))
    return _pcall(
        body,
        name="norm2_mod_bwd",
        out_shape=[jax.ShapeDtypeStruct((t, d), F32), jax.ShapeDtypeStruct((t, d), BF16), jax.ShapeDtypeStruct((8, d), F32)],
        grid=(t // rb,),
        in_specs=[row, row, pl.BlockSpec((1, d), lambda i: (0, 0)), pl.BlockSpec((8, d), lambda i: (0, 0)), row, row],
        out_specs=[row, row, pl.BlockSpec((8, d), lambda i: (0, 0))],
        compiler_params=_cparams(("arbitrary",)),
    )(x1, attn, gain, mods, dz2, dx2)


def _final_loss(x1, ffn, gain, mods, target):
    t, d = x1.shape
    rb = min(ROW_BLOCK, t)
    nb = t // rb

    def body(x1_ref, f_ref, g_ref, mod_ref, tg_ref, dx2_ref, df_ref, st_ref):
        i = pl.program_id(0)

        @pl.when(i == 0)
        def _():
            st_ref[...] = jnp.zeros_like(st_ref)

        ffn_v = f_ref[...]
        g2 = mod_ref[0:1, :]
        x2 = x1_ref[...] + g2 * ffn_v
        xh, r = _rms(x2)
        g = g_ref[...]
        err = xh * g - tg_ref[...]
        st_ref[2:3, :] += _colsum(err * err) * (0.5 / d)
        dy = err * (1.0 / d)
        st_ref[0:1, :] += _colsum(dy * xh)
        dx2 = _rms_bwd(dy * g, xh, r)
        dx2_ref[...] = dx2
        st_ref[1:2, :] += _colsum(dx2 * ffn_v)
        df_ref[...] = (dx2 * g2).astype(BF16)

        @pl.when(i == nb - 1)
        def _():
            st_ref[3:4, :] = jnp.broadcast_to(jnp.sum(st_ref[2:3, :], axis=-1, keepdims=True), (1, d))

    row = pl.BlockSpec((rb, d), lambda i: (i, 0))
    return _pcall(
        body,
        name="final_norm_loss",
        out_shape=[jax.ShapeDtypeStruct((t, d), F32), jax.ShapeDtypeStruct((t, d), BF16), jax.ShapeDtypeStruct((8, d), F32)],
        grid=(nb,),
        in_specs=[row, row, pl.BlockSpec((1, d), lambda i: (0, 0)), pl.BlockSpec((8, d), lambda i: (0, 0)), row],
        out_specs=[row, row, pl.BlockSpec((8, d), lambda i: (0, 0))],
        compiler_params=_cparams(("arbitrary",)),
    )(x1, ffn, gain, mods, target)


def _row_ends(shape):
    rows = lax.broadcasted_iota(jnp.int32, shape, 0)
    return rows == 0, rows == shape[0] - 1


def _shift_dn(v, first):
    return jnp.where(first, 0.0, pltpu.roll(v, 1, 0))


def _shift_up(v, last):
    return jnp.where(last, 0.0, pltpu.roll(v, v.shape[0] - 1, 0))


def _conv_fwd(u, cw, cb):
    t, f2 = u.shape
    f = f2 // 2
    cbk = _tile(f, 256)
    nf = f // cbk

    def body(ua_ref, ub_ref, cwa_ref, cwb_ref, cba_ref, cbb_ref, h_ref, uc_ref):
        first, last = _row_ends((t, cbk))
        outs = []
        for u_ref, cw_ref, cb_ref in ((ua_ref, cwa_ref, cba_ref), (ub_ref, cwb_ref, cbb_ref)):
            uu, cwv = u_ref[...].astype(F32), cw_ref[...]
            outs.append(cb_ref[...] + cwv[0:1, :] * _shift_dn(uu, first) + cwv[1:2, :] * uu
                        + cwv[2:3, :] * _shift_up(uu, last))
        a, b = outs
        uc_ref[0] = a.astype(BF16)
        uc_ref[1] = b.astype(BF16)
        h_ref[...] = (a * jax.nn.sigmoid(a) * b).astype(BF16)

    ca = lambda r: pl.BlockSpec((r, cbk), lambda j: (0, j))
    cbs = lambda r: pl.BlockSpec((r, cbk), lambda j: (0, nf + j))
    return _pcall(
        body,
        name="conv_gate_fwd",
        out_shape=[jax.ShapeDtypeStruct((t, f), BF16), jax.ShapeDtypeStruct((2, t, f), BF16)],
        grid=(nf,),
        in_specs=[ca(t), cbs(t), ca(3), cbs(3), ca(1), cbs(1)],
        out_specs=[ca(t), pl.BlockSpec((2, t, cbk), lambda j: (0, 0, j))],
        compiler_params=_cparams(("parallel",)),
    )(u, u, cw, cw, cb, cb)


def _conv_bwd(u, uc, cw, dh):
    t, f2 = u.shape
    f = f2 // 2
    cbk = _tile(f, 256)
    nf = f // cbk

    def body(ua_ref, ub_ref, uc_ref, cwa_ref, cwb_ref, dh_ref, du_ref, dcw_ref, dcb_ref):
        first, last = _row_ends((t, cbk))
        a, b = uc_ref[0].astype(F32), uc_ref[1].astype(F32)
        dh_v = dh_ref[...].astype(F32)
        sg = jax.nn.sigmoid(a)
        db = dh_v * (a * sg)
        da = dh_v * b * (sg * (1.0 + a * (1.0 - sg)))
        for idx, (dv, u_ref, cw_ref) in enumerate(((da, ua_ref, cwa_ref), (db, ub_ref, cwb_ref))):
            uu, cwv = u_ref[...].astype(F32), cw_ref[...]
            up, dn = _shift_up(dv, last), _shift_dn(dv, first)
            dcb_ref[idx] = _colsum(dv)
            dcw_ref[idx, 0:1, :] = _colsum(up * uu)
            dcw_ref[idx, 1:2, :] = _colsum(dv * uu)
            dcw_ref[idx, 2:3, :] = _colsum(dn * uu)
            du_ref[idx] = (cwv[0:1, :] * up + cwv[1:2, :] * dv + cwv[2:3, :] * dn).astype(BF16)

    ca = lambda r: pl.BlockSpec((r, cbk), lambda j: (0, j))
    cbs = lambda r: pl.BlockSpec((r, cbk), lambda j: (0, nf + j))
    o3 = lambda r: pl.BlockSpec((2, r, cbk), lambda j: (0, 0, j))
    return _pcall(
        body,
        name="conv_gate_bwd",
        out_shape=[jax.ShapeDtypeStruct((2, t, f), BF16), jax.ShapeDtypeStruct((2, 3, f), F32),
                   jax.ShapeDtypeStruct((2, 1, f), F32)],
        grid=(nf,),
        in_specs=[ca(t), cbs(t), o3(t), ca(3), cbs(3), ca(t)],
        out_specs=[o3(t), o3(3), o3(1)],
        compiler_params=_cparams(("parallel",)),
    )(u, u, uc, cw, cw, dh)


def _attention_fwd(q, kk, vv, *, hq, hkv, dk, dv, k_blk0, v_blk0, name):
    t = q.shape[0]
    tk = kk.shape[0]
    g_sz = hq // hkv
    tq = min(ATT_Q_BLOCK_FWD, t)

    def body(q_ref, k_ref, v_ref, o_ref, lse_ref):
        k = k_ref[...]
        v = v_ref[...]
        for j in range(g_sz):
            s = lax.dot_general(q_ref[:, j * dk:(j + 1) * dk], k, _DIMS["nt"], preferred_element_type=F32)
            m = jnp.max(s, axis=-1, keepdims=True)
            p = jnp.exp2(s - m)
            l = jnp.sum(p, axis=-1, keepdims=True)
            o = jnp.dot(p.astype(BF16), v, preferred_element_type=F32) / l
            o_ref[:, j * dv:(j + 1) * dv] = o.astype(BF16)
            lse_ref[0, :, j:j + 1] = m + jnp.log2(l)

    return _pcall(
        body,
        name=name,
        out_shape=[jax.ShapeDtypeStruct((t, hq * dv), BF16), jax.ShapeDtypeStruct((hkv, t, g_sz), F32)],
        grid=(hkv, t // tq),
        in_specs=[
            pl.BlockSpec((tq, g_sz * dk), lambda g, i: (i, g)),
            pl.BlockSpec((tk, dk), lambda g, i: (0, k_blk0 + g)),
            pl.BlockSpec((tk, dv), lambda g, i: (0, v_blk0 + g)),
        ],
        out_specs=[
            pl.BlockSpec((tq, g_sz * dv), lambda g, i: (i, g)),
            pl.BlockSpec((1, tq, g_sz), lambda g, i: (g, i, 0)),
        ],
        compiler_params=_cparams(("parallel", "parallel")),
    )(q, kk, vv)


def _attention_bwd(q, kk, vv, do, lse, *, hq, hkv, dk, dv, k_blk0, v_blk0, name):
    t = q.shape[0]
    tk = kk.shape[0]
    g_sz = hq // hkv
    tq = min(ATT_Q_BLOCK, t)

    def body(q_ref, k_ref, v_ref, do_ref, lse_ref, dq_ref, dk_ref, dv_ref):
        @pl.when(pl.program_id(1) == 0)
        def _():
            dk_ref[...] = jnp.zeros_like(dk_ref)
            dv_ref[...] = jnp.zeros_like(dv_ref)

        k = k_ref[...]
        v = v_ref[...]
        dk_acc = dv_acc = None
        for j in range(g_sz):
            qj = q_ref[:, j * dk:(j + 1) * dk]
            doj = do_ref[:, j * dv:(j + 1) * dv]
            s = lax.dot_general(qj, k, _DIMS["nt"], preferred_element_type=F32)
            p = jnp.exp2(s - lse_ref[0, :, j:j + 1])
            dp = lax.dot_general(doj, v, _DIMS["nt"], preferred_element_type=F32)
            ds = (p * (dp - jnp.sum(p * dp, axis=-1, keepdims=True))).astype(BF16)
            dv_j = lax.dot_general(p.astype(BF16), doj, _DIMS["tn"], preferred_element_type=F32)
            dk_j = lax.dot_general(ds, qj, _DIMS["tn"], preferred_element_type=F32)
            dv_acc = dv_j if dv_acc is None else dv_acc + dv_j
            dk_acc = dk_j if dk_acc is None else dk_acc + dk_j
            dq_ref[:, j * dk:(j + 1) * dk] = jnp.dot(ds, k, preferred_element_type=F32)
        dv_ref[...] += dv_acc
        dk_ref[...] += dk_acc

        @pl.when(pl.program_id(1) == t // tq - 1)
        def _():
            dk_ref[...] *= LN2

    return _pcall(
        body,
        name=name,
        out_shape=[jax.ShapeDtypeStruct((t, hq * dk), F32), jax.ShapeDtypeStruct((tk, hkv * dk), F32),
                   jax.ShapeDtypeStruct((tk, hkv * dv), F32)],
        grid=(hkv, t // tq),
        in_specs=[
            pl.BlockSpec((tq, g_sz * dk), lambda g, i: (i, g)),
            pl.BlockSpec((tk, dk), lambda g, i: (0, k_blk0 + g)),
            pl.BlockSpec((tk, dv), lambda g, i: (0, v_blk0 + g)),
            pl.BlockSpec((tq, g_sz * dv), lambda g, i: (i, g)),
            pl.BlockSpec((1, tq, g_sz), lambda g, i: (g, i, 0)),
        ],
        out_specs=[
            pl.BlockSpec((tq, g_sz * dk), lambda g, i: (i, g)),
            pl.BlockSpec((tk, dk), lambda g, i: (0, g)),
            pl.BlockSpec((tk, dv), lambda g, i: (0, g)),
        ],
        compiler_params=_cparams(("parallel", "arbitrary")),
    )(q, kk, vv, do, lse)


def _silu(v):
    return v * jax.nn.sigmoid(v)


def _ada_fwd(conds, w_ada, b_ada_shard):
    r, d = conds.shape
    n = w_ada.shape[1]
    tn = _tile(n, 512)

    def body(c_ref, w_ref, b_ref, o_ref):
        s = _silu(c_ref[...]).astype(BF16)
        o_ref[...] = jnp.dot(s, w_ref[...].astype(BF16), preferred_element_type=F32) + b_ref[...]

    return _pcall(
        body,
        name="ada_fwd",
        out_shape=jax.ShapeDtypeStruct((r, n), F32),
        grid=(n // tn,),
        in_specs=[pl.BlockSpec((r, d), lambda j: (0, 0)), pl.BlockSpec((d, tn), lambda j: (0, j)),
                  pl.BlockSpec((1, tn), lambda j: (0, j))],
        out_specs=pl.BlockSpec((r, tn), lambda j: (0, j)),
        compiler_params=_cparams(("parallel",)),
    )(conds, w_ada, b_ada_shard)


def _cctx_partial(da16_shard, w_ada, c_ctx_row):
    d, n = w_ada.shape
    td = _tile(d, 512)

    def body(g_ref, w_ref, c_ref, o_ref):
        ds = lax.dot_general(g_ref[8:16, :].astype(BF16), w_ref[...].astype(BF16), _DIMS["nt"],
                             preferred_element_type=F32)
        cv = c_ref[...]
        sg = jax.nn.sigmoid(cv)
        o_ref[...] = ds * (sg * (1.0 + cv * (1.0 - sg)))

    return _pcall(
        body,
        name="cctx_partial",
        out_shape=jax.ShapeDtypeStruct((8, d), F32),
        grid=(d // td,),
        in_specs=[pl.BlockSpec((16, n), lambda j: (0, 0)), pl.BlockSpec((td, n), lambda j: (j, 0)),
                  pl.BlockSpec((1, td), lambda j: (0, j))],
        out_specs=pl.BlockSpec((8, td), lambda j: (0, j)),
        compiler_params=_cparams(("parallel",)),
    )(da16_shard, w_ada, c_ctx_row)


def _sum_parts(parts):
    p, _, n = parts.shape

    def body(p_ref, o_ref):
        acc = p_ref[0]
        for s in range(1, p):
            acc = acc + p_ref[s]
        o_ref[...] = acc

    return _pcall(
        body,
        name="sum_parts",
        out_shape=jax.ShapeDtypeStruct((1, n), F32),
        in_specs=[pl.BlockSpec(memory_space=pltpu.VMEM)],
        out_specs=pl.BlockSpec(memory_space=pltpu.VMEM),
    )(parts)


def _adam_math(w, g, m, v):
    m2 = ADAM_B1 * m + (1.0 - ADAM_B1) * g
    v2 = ADAM_B2 * v + (1.0 - ADAM_B2) * jnp.square(g)
    m_hat = m2 / (1.0 - ADAM_B1 ** ADAM_STEP)
    v_hat = v2 / (1.0 - ADAM_B2 ** ADAM_STEP)
    delta = -ADAM_LR * (m_hat / (jnp.sqrt(v_hat) + ADAM_EPS) + ADAM_WD * w)
    return delta, m2, v2


def _adamw(parts, w, m, v, name):
    p, r, c = parts.shape
    block_elems = 1 << 18
    rb, cb = _tile(r, max(8, block_elems // c // 8 * 8), 8), c
    if rb * c < block_elems // 4 and r * c > block_elems:
        rb, cb = r, _tile(c, max(LANE, block_elems // r // LANE * LANE))

    def body(p_ref, w_ref, m_ref, v_ref, g_ref, d_ref, m2_ref, v2_ref):
        g = p_ref[0].astype(F32)
        for s in range(1, p):
            g = g + p_ref[s].astype(F32)
        g_ref[...] = g
        d_ref[...], m2_ref[...], v2_ref[...] = _adam_math(w_ref[...], g, m_ref[...], v_ref[...])

    if w.ndim == 3:
        blk = pl.BlockSpec((None, rb, cb), lambda i, j: (0, i, j))
    else:
        blk = pl.BlockSpec((rb, cb), lambda i, j: (i, j))
    return _pcall(
        body,
        name=name,
        out_shape=[jax.ShapeDtypeStruct(w.shape, F32)] * 4,
        grid=(r // rb, c // cb),
        in_specs=[pl.BlockSpec((p, rb, cb), lambda i, j: (0, i, j)), blk, blk, blk],
        out_specs=[blk] * 4,
        compiler_params=_cparams(("parallel", "parallel")),
    )(parts, w, m, v)


def _adamw_ada(conds, da16, w, m, v):
    d, n = w.shape
    rb = _tile(d, 256, LANE)

    def body(s_ref, da_ref, w_ref, m_ref, v_ref, g_ref, d_ref, m2_ref, v2_ref):
        g = lax.dot_general(_silu(s_ref[...]).astype(BF16), da_ref[...].astype(BF16), _DIMS["tn"],
                            preferred_element_type=F32)
        g_ref[...] = g
        d_ref[...], m2_ref[...], v2_ref[...] = _adam_math(w_ref[...], g, m_ref[...], v_ref[...])

    row = pl.BlockSpec((rb, n), lambda i: (i, 0))
    return _pcall(
        body,
        name="adamw_w_ada",
        out_shape=[jax.ShapeDtypeStruct((d, n), F32)] * 4,
        grid=(d // rb,),
        in_specs=[pl.BlockSpec((16, rb), lambda i: (0, i)), pl.BlockSpec((16, n), lambda i: (0, 0)), row, row, row],
        out_specs=[row] * 4,
        compiler_params=_cparams(("parallel",)),
    )(conds, da16, w, m, v)


def _cast_bf16(a, name):
    _, r, c = a.shape
    rb, cb = _tile(r, 512, 8), c
    if rb < 64 < r:
        rb, cb = r, _tile(c, 512)

    def body(a_ref, o_ref):
        o_ref[...] = a_ref[...].astype(BF16)

    return _pcall(body, name=name, out_shape=jax.ShapeDtypeStruct((r, c), BF16), grid=(r // rb, c // cb),
                  in_specs=[pl.BlockSpec((None, rb, cb), lambda i, j: (0, i, j))],
                  out_specs=pl.BlockSpec((rb, cb), lambda i, j: (i, j)),
                  compiler_params=_cparams(("parallel", "parallel")))(a)


def _rope_tabs(t, rot):
    half, q = rot // 2, rot // 4
    n_rows = t // GRID_W
    row = jnp.repeat(jnp.arange(n_rows, dtype=F32), GRID_W)
    col = jnp.tile(jnp.arange(GRID_W, dtype=F32), n_rows)
    inv_freq = ROPE_THETA ** (-jnp.arange(0, half, 2, dtype=F32) / half)
    ang = jnp.concatenate([row[:, None] * inv_freq, col[:, None] * inv_freq], axis=-1)
    cos, sin = jnp.cos(ang), jnp.sin(ang)
    c0, c1, s0, s1 = cos[:, :q], cos[:, q:], sin[:, :q], sin[:, q:]
    z = jnp.zeros_like(s0)
    return (jnp.concatenate([c0, c0, c1, c1], -1), jnp.concatenate([-s0, z, -s1, z], -1),
            jnp.concatenate([z, s0, z, s1], -1))


def _pad_cols(a, left, total, fill=0.0):
    return jnp.pad(a, ((0, 0), (left, total - left - a.shape[1])), constant_values=fill)


def _with_ctx_rows(tab, tc, fill):
    return jnp.concatenate([tab, jnp.full((tc, tab.shape[1]), fill, F32)], axis=0)


def kernel(x, c, ctx, c_ctx, w_ada, b_ada, norm1_g, w_in, mla_q_norm_g, w_q_up, mla_kv_norm_g, w_kv_up, gqa_q_norm_g, gqa_k_norm_g, w_br_a, w_br_b, w_out, norm2_g, w_up, conv_w, conv_b, w_down, final_norm_g, loss_target, m_c_ctx, m_w_ada, m_b_ada, m_norm1_g, m_w_in, m_mla_q_norm_g, m_w_q_up, m_mla_kv_norm_g, m_w_kv_up, m_gqa_q_norm_g, m_gqa_k_norm_g, m_w_br_a, m_w_br_b, m_w_out, m_norm2_g, m_w_up, m_conv_w, m_conv_b, m_w_down, m_final_norm_g, v_c_ctx, v_w_ada, v_b_ada, v_norm1_g, v_w_in, v_mla_q_norm_g, v_w_q_up, v_mla_kv_norm_g, v_w_kv_up, v_gqa_q_norm_g, v_gqa_k_norm_g, v_w_br_a, v_w_br_b, v_w_out, v_norm2_g, v_w_up, v_conv_w, v_conv_b, v_w_down, v_final_norm_g):
    weights = dict(c_ctx=c_ctx, w_ada=w_ada, b_ada=b_ada, norm1_g=norm1_g, w_in=w_in, mla_q_norm_g=mla_q_norm_g,
                   w_q_up=w_q_up, mla_kv_norm_g=mla_kv_norm_g, w_kv_up=w_kv_up, gqa_q_norm_g=gqa_q_norm_g,
                   gqa_k_norm_g=gqa_k_norm_g, w_br_a=w_br_a, w_br_b=w_br_b, w_out=w_out, norm2_g=norm2_g, w_up=w_up,
                   conv_w=conv_w, conv_b=conv_b, w_down=w_down, final_norm_g=final_norm_g)
    mom_m = dict(c_ctx=m_c_ctx, w_ada=m_w_ada, b_ada=m_b_ada, norm1_g=m_norm1_g, w_in=m_w_in, mla_q_norm_g=m_mla_q_norm_g,
                 w_q_up=m_w_q_up, mla_kv_norm_g=m_mla_kv_norm_g, w_kv_up=m_w_kv_up, gqa_q_norm_g=m_gqa_q_norm_g,
                 gqa_k_norm_g=m_gqa_k_norm_g, w_br_a=m_w_br_a, w_br_b=m_w_br_b, w_out=m_w_out, norm2_g=m_norm2_g,
                 w_up=m_w_up, conv_w=m_conv_w, conv_b=m_conv_b, w_down=m_w_down, final_norm_g=m_final_norm_g)
    mom_v = dict(c_ctx=v_c_ctx, w_ada=v_w_ada, b_ada=v_b_ada, norm1_g=v_norm1_g, w_in=v_w_in, mla_q_norm_g=v_mla_q_norm_g,
                 w_q_up=v_w_q_up, mla_kv_norm_g=v_mla_kv_norm_g, w_kv_up=v_w_kv_up, gqa_q_norm_g=v_gqa_q_norm_g,
                 gqa_k_norm_g=v_gqa_k_norm_g, w_br_a=v_w_br_a, w_br_b=v_w_br_b, w_out=v_w_out, norm2_g=v_norm2_g,
                 w_up=v_w_up, conv_w=v_conv_w, conv_b=v_conv_b, w_down=v_w_down, final_norm_g=v_final_norm_g)
    order = list(weights)

    my_idx = 4 * lax.axis_index("x") + 2 * lax.axis_index("y") + lax.axis_index("c")
    xs, cts, tgt = x[0], ctx[0], loss_target[0]
    t, d = xs.shape
    tc = cts.shape[0]
    ta = t + tc
    kvl, ql = MLA_KV_LORA, MLA_Q_LORA
    nb = GQA_KV_HEADS * GQA_HEAD_DIM
    hb = GQA_HEADS * GQA_HEAD_DIM
    ha = MLA_HEADS
    f2 = w_up.shape[2] * N_DEV
    ff = f2 // 2

    big = ["w_in", "w_q_up", "w_kv_up", "w_br_a", "w_br_b", "w_out", "w_up", "w_down"]
    _ORDER_AFTER.clear()
    narrow = ("w_in", "w_q_up")

    def tview(a):
        return jnp.transpose(a, (0, 2, 1))

    shards = {"w_in": _cast_bf16(tview(weights["w_in"]), "cast_w_in")}
    c_idx = jnp.reshape(lax.axis_index("c"), (1,)).astype(jnp.int32)

    def gather_start(names, dep):
        shs = [shards[n] for n in names]
        land = [lax.empty((N_DEV,) + s.shape, BF16) for s in shs]
        if dep is not None:
            _after(dep)
        s, r, arrs, tok = _split_start("gather_ici_start_" + names[0], shs + land, _gather_ici_copies(len(names)),
                                       4 * len(names))
        return dict(names=names, s=s, r=r, arrs=arrs, tok=tok)

    def gather_pass(g, after):
        n = len(g["names"])
        arrs = _split_wait("gather_ici_wait_" + g["names"][0], g["s"], g["r"], g["arrs"], _gather_ici_copies(n), after)
        s, r, bufs, tok = _split_start("gather_pass_start_" + g["names"][0], arrs[n:], _gather_pass_copies(n), 3 * n)
        g.update(s2=s, r2=r, bufs=bufs)
        return tok

    def gather_relay(g, after):
        n = len(g["names"])
        bufs = _split_wait("gather_pass_wait_" + g["names"][0], g["s2"], g["r2"], g["bufs"], _gather_pass_copies(n), after)
        s, r, bufs, tok = _split_start("gather_d2d_start_" + g["names"][0], bufs, _gather_d2d_copies(n), n)
        g.update(s3=s, r3=r, bufs=bufs)
        return tok

    def gather_finish(g, after):
        n = len(g["names"])
        bufs = _split_wait("gather_d2d_wait_" + g["names"][0], g["s3"], g["r3"], g["bufs"], _gather_d2d_copies(n), after)
        return dict(zip(g["names"], bufs))

    c_all, cw_all = _all_gather([jnp.pad(c, ((0, 7), (0, 0))), jnp.pad(conv_w[0], ((0, 5), (0, 0)))], "gather_cond")
    conv_w_f = jnp.transpose(cw_all[:, :3, :], (1, 0, 2)).reshape(3, f2)
    conds = jnp.concatenate([c_all[:, 0, :], c_ctx[None, :], jnp.zeros((7, d), F32)], axis=0)
    ncol = w_ada.shape[2]
    b_shard = lax.dynamic_slice_in_dim(b_ada, my_idx * ncol, ncol, axis=1)
    ada_shard = _ada_fwd(conds, w_ada[0], b_shard)
    (ada_all,) = _all_gather([ada_shard], "gather_ada")
    ada = jnp.transpose(ada_all, (1, 0, 2)).reshape(16, N_DEV * ncol)
    lat = lax.dynamic_slice_in_dim(ada, my_idx, 1, axis=0).reshape(6, d)
    cxt = ada[8].reshape(6, d)
    mods1 = jnp.concatenate([lat[0:2], cxt[0:2], jnp.zeros((4, d), F32)], axis=0)
    mods2 = jnp.concatenate([lat[2:3], lat[3:4], lat[4:5], jnp.zeros((5, d), F32)], axis=0)
    mods2b = jnp.concatenate([lat[2:3], lat[4:5], jnp.zeros((6, d), F32)], axis=0)
    mods3 = jnp.concatenate([lat[5:6], jnp.zeros((7, d), F32)], axis=0)

    g0 = gather_start(["w_in"], ada_all)
    for n in big[1:]:
        _after(g0["tok"])
        shards[n] = _cast_bf16(tview(weights[n]) if n in narrow else weights[n], "cast_" + n)

    ca, s1a, s2a = _rope_tabs(t, MLA_ROPE)
    cb_, s1b, s2b = _rope_tabs(t, GQA_HEAD_DIM)
    q_tabs_a = (_pad_cols(jnp.concatenate([jnp.ones((t, MLA_NOPE), F32), ca], 1), 0, MLA_SLOT),
                _pad_cols(s1a, MLA_NOPE, MLA_SLOT), _pad_cols(s2a, MLA_NOPE, MLA_SLOT))
    q_tabs_b = (cb_, s1b, s2b)
    k_tabs = (_with_ctx_rows(_pad_cols(ca, 0, LANE), tc, 1.0), _with_ctx_rows(_pad_cols(s1a, 0, LANE), tc, 0.0),
              _with_ctx_rows(_pad_cols(s2a, 0, LANE), tc, 0.0),
              _with_ctx_rows(cb_, tc, 1.0), _with_ctx_rows(s1b, tc, 0.0), _with_ctx_rows(s2b, tc, 0.0))

    def cols_full(g):
        return jnp.transpose(g, (1, 0, 2)).reshape(g.shape[1], N_DEV * g.shape[2])

    _after(*q_tabs_a, *q_tabs_b, *k_tabs, *[shards[n] for n in big[1:]])
    tok_p0 = gather_pass(g0, mods1)
    g1 = gather_start(["w_q_up", "w_kv_up", "w_br_a", "w_br_b", "w_out"], tok_p0)
    _after(g1["tok"])
    z_all = _norm_mod_fwd(cts, xs, norm1_g, mods1)
    gathered = gather_finish(g0, gather_relay(g0, z_all))
    wt_in = gathered["w_in"].reshape(-1, d)
    o_kpe, o_kb, o_vb = kvl, kvl + MLA_ROPE, kvl + MLA_ROPE + nb
    o_q = o_vb + nb
    o_g = o_q + ql + hb
    wkv_w = kvl + 2 * nb + LANE
    wt_kv_p = jnp.concatenate([wt_in[:kvl], wt_in[o_kb:o_q], wt_in[o_kpe:o_kb],
                               jnp.zeros((LANE - MLA_ROPE, d), BF16)], axis=0)
    q_w = ql + hb
    q_pad = (-q_w) % 512 if d >= 512 else (-q_w) % d
    gate_blk = (q_w + q_pad) // d
    assert (q_w + q_pad) % d == 0
    wt_qg_p = jnp.concatenate([wt_in[o_q:o_g], jnp.zeros((q_pad, d), BF16), wt_in[o_g:]], axis=0)

    kv_all = _mm(z_all, wt_kv_p, "nt", F32, "proj_kv", tm=1152, tn=wkv_w)
    qg = _mm(z_all, wt_qg_p, "nt", F32, "proj_qg", tm=1024, tn=1024, rows=t)
    tok_p1 = gather_pass(g1, qg)
    g2 = gather_start(["w_up"], tok_p1)
    g3 = gather_start(["w_down"], g2["tok"])
    _after(g3["tok"])
    kin, k_b, v_b = _key_prep_fwd(kv_all, mla_kv_norm_g, gqa_k_norm_g, k_tabs)
    sc_a = float((MLA_NOPE + MLA_ROPE) ** -0.5) * LOG2E
    sc_b = float(GQA_HEAD_DIM ** -0.5) * LOG2E
    _after(g3["tok"])
    cqn, q_b = _q_prep_fwd(qg, mla_q_norm_g, gqa_q_norm_g, q_tabs_b, sc_b)
    _after(kin, g3["tok"])
    gathered.update(gather_finish(g1, gather_relay(g1, q_b)))

    wqt_f = gathered["w_q_up"].reshape(ha, MLA_NOPE + MLA_ROPE, ql)
    wqt_ext = jnp.pad(wqt_f, ((0, 0), (0, MLA_SLOT - MLA_NOPE - MLA_ROPE), (0, 0))).reshape(ha * MLA_SLOT, ql)
    wkv_f = cols_full(gathered["w_kv_up"]).reshape(kvl, ha, MLA_NOPE + MLA_V)
    wk_slots = jnp.pad(wkv_f[:, :, :MLA_NOPE], ((0, 0), (0, 0), (0, MLA_SLOT - MLA_NOPE))).reshape(kvl, ha * MLA_SLOT)
    wv_cols = wkv_f[:, :, MLA_NOPE:].reshape(kvl, ha * MLA_V)
    e_slot = jnp.pad(jnp.eye(MLA_ROPE, dtype=BF16),
                     ((0, LANE - MLA_ROPE), (MLA_NOPE, MLA_SLOT - MLA_NOPE - MLA_ROPE)))
    e_rows = jnp.concatenate([jnp.tile(e_slot, (1, ha)), jnp.zeros((LANE, ha * MLA_V), BF16)], axis=1)
    wkv_ext = jnp.concatenate([jnp.concatenate([wk_slots, wv_cols], axis=1), e_rows], axis=0)
    w_bra = cols_full(gathered["w_br_a"])
    w_brb = cols_full(gathered["w_br_b"])
    w_out_f = gathered["w_out"].reshape(d, d)

    kv_a = _mm(kin, wkv_ext, "nn", BF16, "kv_up", tm=1152, tn=1024)
    qa_raw = _mm(cqn, wqt_ext, "nt", F32, "q_up", tm=1024, tn=1024)
    q_a = _rope_a(qa_raw, q_tabs_a, False, BF16, "rope_q_fwd", sc_a)
    att_a = dict(hq=ha, hkv=ha, dk=MLA_SLOT, dv=MLA_V, k_blk0=0, v_blk0=ha * MLA_SLOT // MLA_V)
    att_b = dict(hq=GQA_HEADS, hkv=GQA_KV_HEADS, dk=GQA_HEAD_DIM, dv=GQA_HEAD_DIM, k_blk0=0, v_blk0=0)
    o_a, lse_a = _attention_fwd(q_a, kv_a, kv_a, name="attn_a_fwd", **att_a)
    o_b, lse_b = _attention_fwd(q_b, k_b, v_b, name="attn_b_fwd", **att_b)
    _after(o_a)
    _after(gather_pass(g2, o_b))
    pa = _mm(o_a, w_bra, "nn", BF16, "br_a", tm=1024, tn=1024)
    pb = _mm(o_b, w_brb, "nn", BF16, "br_b", tm=1024, tn=1024)
    merged = _merge_fwd(pa, pb, qg, gate_blk)
    attn = _mm(merged, w_out_f, "nn", F32, "w_out", tm=1024, tn=1024)
    x1, z2 = _resid_norm_mod(xs, attn, norm2_g, mods2, "resid_norm2_fwd")
    tok_r2 = gather_relay(g2, z2)
    tok_p3 = gather_pass(g3, tok_r2)
    w_up3 = gather_finish(g2, tok_p3)["w_up"]
    u = _mm_up_fwd(z2, w_up3, "w_up")
    tok_r3 = gather_relay(g3, u)
    _after(tok_r3)
    h, uc = _conv_fwd(u, conv_w_f, conv_b)
    w_down_f = gather_finish(g3, h)["w_down"].reshape(ff, d)
    ffn = _mm(h, w_down_f, "nn", F32, "w_down", tm=1024, tn=1024, tk=2816)

    def to_shards(g):
        return jnp.transpose(g.reshape(g.shape[0], N_DEV, g.shape[1] // N_DEV), (1, 0, 2))

    def reduce_start(tag, names, sends):
        n = len(sends)
        land = [lax.empty((4,) + s.shape[1:], s.dtype) for s in sends]
        s, r, arrs, tok = _split_start("reduce_d2d_start_" + tag, sends + land, _reduce_d2d_copies(n), 4 * n)
        return dict(tag=tag, names=names, s=s, r=r, arrs=arrs, tok=tok)

    def reduce_relay(g, after):
        n = len(g["names"])
        arrs = _split_wait("reduce_d2d_wait_" + g["tag"], g["s"], g["r"], g["arrs"], _reduce_d2d_copies(n), after)
        sums = [_pair_sum(arrs[a], arrs[n + a], c_idx, "pair_sum_" + g["names"][a]) for a in range(n)]
        land = [lax.empty(s.shape, s.dtype) for s in sums]
        s, r, arrs2, tok = _split_start("reduce_ici_start_" + g["tag"], sums + land, _reduce_ici_copies(n), 4 * n)
        g.update(s2=s, r2=r, arrs2=arrs2)
        return tok

    def reduce_finish(g, after):
        n = len(g["names"])
        arrs2 = _split_wait("reduce_ici_wait_" + g["tag"], g["s2"], g["r2"], g["arrs2"], _reduce_ici_copies(n), after)
        return dict(zip(g["names"], arrs2[n:]))

    dx2, dffn, st_fin = _final_loss(x1, ffn, final_norm_g[None, :], mods3, tgt)
    dh = _mm(dffn, w_down_f, "nt", BF16, "d_h", tm=1024, tn=1024)
    g_w_down = _mm(h, dffn, "tn", BF16, "g_w_down", tm=512, tn=1024)
    r_down = reduce_start("down", ["w_down"], [g_w_down.reshape(N_DEV, ff // N_DEV, d)])
    _after(r_down["tok"])
    du3, dcw, dcb = _conv_bwd(u, uc, conv_w_f, dh)
    dz2 = _mm_up_dz(du3, w_up3, "d_z2")
    g_w_up = _mm_up_gw(z2, du3, N_DEV, "g_w_up")
    g_conv_w = jnp.concatenate([dcw[0], dcw[1]], axis=1)
    tok = reduce_relay(r_down, g_w_up)
    _after(tok)
    r_up = reduce_start("up", ["w_up", "conv_w"], [g_w_up, to_shards(jnp.pad(g_conv_w, ((0, 5), (0, 0))))])
    _after(tok, r_up["tok"])
    dx1, dattn, st_n2 = _norm2_bwd(x1, attn, norm2_g, mods2b, dz2, dx2)
    dmerged = _mm(dattn, w_out_f, "nt", BF16, "d_merged", tm=1024, tn=1024)
    g_w_out = _mm(merged, dattn, "tn", BF16, "g_w_out", tm=1024, tn=1024)
    dpa, dpb, dgates = _merge_bwd(dmerged, pa, pb, qg, gate_blk)
    do_a = _mm(dpa, w_bra, "nt", BF16, "d_o_a", tm=1024, tn=1024)
    do_b = _mm(dpb, w_brb, "nt", BF16, "d_o_b", tm=1024, tn=1024)
    g_w_bra = _mm(o_a, dpa, "tn", BF16, "g_w_br_a", tm=1024, tn=1024)
    g_w_brb = _mm(o_b, dpb, "tn", BF16, "g_w_br_b", tm=1024, tn=1024)
    tok = reduce_relay(r_up, g_w_brb)
    _after(tok)
    r_out = reduce_start("out", ["w_out", "w_br_a", "w_br_b"],
                         [g_w_out.reshape(N_DEV, d // N_DEV, d), to_shards(g_w_bra), to_shards(g_w_brb)])
    _after(tok, r_out["tok"])
    dq_a, dk_a, dv_a = _attention_bwd(q_a, kv_a, kv_a, do_a, lse_a, name="attn_a_bwd", **att_a)
    dq_b, dk_b, dv_b = _attention_bwd(q_b, k_b, v_b, do_b, lse_b, name="attn_b_bwd", **att_b)
    _after(reduce_relay(r_out, dv_b))
    dqa_raw = _rope_a(dq_a, q_tabs_a, True, BF16, "rope_q_bwd", sc_a * LN2)
    dcqn = _mm(dqa_raw, wqt_ext, "nn", F32, "d_cqn", tm=1024, tn=ql)
    g_wqt_ext = _mm(dqa_raw, cqn, "tn", BF16, "g_w_q_up", tm=1024, tn=ql)
    dq_p, st_q, st_qb = _q_prep_bwd(qg, mla_q_norm_g, gqa_q_norm_g, q_tabs_b, dcqn, dq_b, q_pad, sc_b * LN2)
    dkin = _mm_cat_nt([(dk_a, wkv_ext, 0), (dv_a, wkv_ext, ha * MLA_SLOT)], F32, "d_kin", tm=1152, tn=kvl + LANE)
    g_wkv_ext = _mm_cat_tn(kin, [dk_a, dv_a], BF16, "g_w_kv_up", tm=kvl + LANE, tn=min(1024, ha * MLA_V))
    dkv_p, st_kv, st_kb = _key_prep_bwd(kv_all, mla_kv_norm_g, gqa_k_norm_g, k_tabs, dkin, dk_b, dv_b)
    g_wqt = g_wqt_ext.reshape(ha, MLA_SLOT, ql)[:, :MLA_NOPE + MLA_ROPE, :].reshape(N_DEV, -1, ql)
    g_wkv = jnp.concatenate([g_wkv_ext[:kvl, :ha * MLA_SLOT].reshape(kvl, ha, MLA_SLOT)[:, :, :MLA_NOPE],
                             g_wkv_ext[:kvl, ha * MLA_SLOT:].reshape(kvl, ha, MLA_V)], axis=2).reshape(kvl, ha * (MLA_NOPE + MLA_V))
    r_qkv = reduce_start("qkv", ["w_q_up", "w_kv_up"], [g_wqt, to_shards(g_wkv)])
    _after(r_qkv["tok"])
    g_wkv_p = _mm(dkv_p, z_all, "tn", BF16, "g_w_in_kv", tm=wkv_w, tn=1024)
    g_wqg_p = _mm_rows_tn([dq_p, dgates], z_all, BF16, "g_w_in_qg", tm=min(1024, d), tn=1024, rows=t)
    g_wt_in = jnp.concatenate([g_wkv_p[:kvl], g_wkv_p[kvl + 2 * nb:kvl + 2 * nb + MLA_ROPE],
                               g_wkv_p[kvl:kvl + 2 * nb], g_wqg_p[:q_w], g_wqg_p[q_w + q_pad:]], axis=0)
    r_in = reduce_start("in", ["w_in"], [g_wt_in.reshape(N_DEV, -1, d)])
    _after(r_in["tok"])
    qw_p = q_w + q_pad
    dz_lat = _mm_sum_nn([(dq_p, 0, wt_qg_p, 0, qw_p), (dgates, 0, wt_qg_p, qw_p, d), (dgates, d, wt_qg_p, qw_p + d, d),
                         (dkv_p, 0, wt_kv_p, 0, wkv_w)], F32, "d_z_lat", rows=t)
    dz_ctx = _mm(dkv_p, wt_kv_p, "nn", F32, "d_z_ctx", tm=min(ROW_BLOCK, tc), tn=1024, a_row_off=t)
    tok_q = reduce_relay(r_qkv, dz_ctx)
    _after(tok_q)
    grad_x, st_n1 = _norm1_bwd(cts, xs, norm1_g, mods1, dz_ctx, dz_lat, dx1)

    res = {}

    def upd(nm, parts):
        wv, mv, vv = weights[nm], mom_m[nm], mom_v[nm]
        if wv.ndim == 1:
            wv, mv, vv = (a.reshape(1, -1) for a in (wv, mv, vv))
        if nm in narrow:
            wv, mv, vv = tview(wv), tview(mv), tview(vv)
        outs = _adamw(parts, wv, mv, vv, "adamw_" + nm)
        if nm in narrow:
            outs = [tview(o_) for o_ in outs]
        res[nm] = [o_.reshape(weights[nm].shape) for o_ in outs]

    d_lat = jnp.concatenate([st_n1[0], st_n1[1], st_n2[3], st_n2[0], st_n2[1], st_fin[1]])
    d_cxt = jnp.concatenate([st_n1[3], st_n1[4], jnp.zeros((4 * d,), F32)])
    small = jnp.concatenate([d_lat, d_cxt, st_n1[2], st_q[0], st_kv[0], st_qb[0], st_kb[0], st_n2[2],
                             jnp.concatenate([dcb[0, 0], dcb[1, 0]]), st_fin[0], st_fin[3, :LANE]])
    n_small = small.shape[0]
    pad_small = (-n_small) % LANE
    (small_all,) = _all_gather([jnp.pad(small, (0, pad_small)).reshape(1, -1)], "gather_small")
    offs = {}
    o = 0
    for nm, ln in (("d_lat", 6 * d), ("d_cxt", 6 * d), ("norm1_g", d), ("mla_q_norm_g", ql), ("mla_kv_norm_g", kvl),
                   ("gqa_q_norm_g", GQA_HEAD_DIM), ("gqa_k_norm_g", GQA_HEAD_DIM), ("norm2_g", d), ("conv_b", f2),
                   ("final_norm_g", d), ("loss", LANE)):
        offs[nm] = (o, ln)
        o += ln

    def part(nm):
        a, ln = offs[nm]
        return small_all[:, :, a:a + ln]

    loss = _sum_parts(part("loss"))[0, 0]
    d_lat_all = part("d_lat")[:, 0, :]
    d_cxt_sum = _sum_parts(part("d_cxt"))
    da16 = jnp.concatenate([d_lat_all, d_cxt_sum, jnp.zeros((7, 6 * d), F32)], axis=0)
    da16_shard = lax.dynamic_slice_in_dim(da16, my_idx * ncol, ncol, axis=1)
    cc_part = _cctx_partial(da16_shard, w_ada[0], c_ctx[None, :])
    (cc_all,) = _all_gather([cc_part], "gather_cctx")
    cc_parts = cc_all[:, 0:1, :]
    tok_i = reduce_relay(r_in, cc_all)

    _after(tok_i)
    for nm in ("norm1_g", "mla_q_norm_g", "mla_kv_norm_g", "gqa_q_norm_g", "gqa_k_norm_g", "norm2_g", "conv_b",
               "final_norm_g"):
        upd(nm, part(nm))
    upd("c_ctx", cc_parts)
    b_parts = jnp.concatenate([d_lat_all[:, None, :], d_cxt_sum[None]], axis=0)
    upd("b_ada", b_parts)
    _after(tok_i)
    outs = _adamw_ada(conds, da16_shard, w_ada[0], m_w_ada[0], v_w_ada[0])
    res["w_ada"] = [o_[None] for o_ in outs]
    last = outs[0]
    done = [last]
    for grp in (r_down, r_up, r_out, r_qkv, r_in):
        _after(*done)
        recv = reduce_finish(grp, last)
        for nm in grp["names"]:
            upd(nm, recv[nm][:, :3, :] if nm == "conv_w" else recv[nm])
            last = res[nm][0]
            done.append(last)

    return (loss, grad_x[None], *[res[n][0] for n in order], *[res[n][1] for n in order],
            *[res[n][2] for n in order], *[res[n][3] for n in order])
```

```python
import jax
import jax.numpy as jnp
from jax import lax
from jax.experimental import pallas as pl
from jax.experimental.pallas import tpu as pltpu

F32 = jnp.float32
BF16 = jnp.bfloat16

GRID_W = 64
ROPE_THETA = 10000.0
NORM_EPS = 1e-6
MLA_HEADS = 8
MLA_Q_LORA = 768
MLA_KV_LORA = 512
MLA_NOPE = 128
MLA_ROPE = 64
MLA_V = 128
GQA_HEADS = 8
GQA_KV_HEADS = 2
GQA_HEAD_DIM = 128
ADAM_LR = 0.001
ADAM_B1 = 0.9
ADAM_B2 = 0.999
ADAM_EPS = 1e-08
ADAM_WD = 0.01
ADAM_STEP = 10

N_DEV = 8
LANE = 128
MLA_SLOT = 2 * LANE
VMEM_LIMIT = 56 * 1024 * 1024
ROW_BLOCK = 256
ATT_Q_BLOCK = 512
ATT_Q_BLOCK_FWD = 512
LN2 = 0.6931471805599453
LOG2E = 1.4426950408889634
MESH_ID = pl.DeviceIdType.MESH


def _tile(n, pref, align=LANE):
    if n <= pref:
        return n
    best = None
    t = align
    while t <= pref:
        if n % t == 0:
            best = t
        t += align
    assert best is not None, (n, pref, align)
    return best


def _cparams(sem=None):
    return pltpu.CompilerParams(dimension_semantics=sem, vmem_limit_bytes=VMEM_LIMIT)


_ORDER_AFTER = []


def _after(*arrays):
    _ORDER_AFTER.extend(arrays)


def _pcall(body, *, in_specs, **kw):
    deps = tuple(_ORDER_AFTER)
    _ORDER_AFTER.clear()
    if not deps:
        return pl.pallas_call(body, in_specs=in_specs, **kw)
    n_in, n_dep = len(in_specs), len(deps)

    def with_deps(*refs):
        body(*refs[:n_in], *refs[n_in + n_dep:])

    call = pl.pallas_call(with_deps, in_specs=list(in_specs) + [pl.BlockSpec(memory_space=pl.ANY)] * n_dep, **kw)
    return lambda *args: call(*args, *deps)


def _all_gather(arrs, name):
    n = len(arrs)

    def body(*refs):
        ins = refs[:n]
        outs = refs[n:2 * n]
        send_sems, recv_sems, local_sems = refs[2 * n:]
        x, y, c = lax.axis_index("x"), lax.axis_index("y"), lax.axis_index("c")
        me, sibling = (x, y, c), (x, y, 1 - c)
        chips = [(1 - x, y), (x, 1 - y), (1 - x, 1 - y)]

        def rows(a, dev):
            px, py, pc = dev
            return outs[a].at[4 * px + 2 * py + pc]

        def copy(a, k, block, to, src=None):
            return pltpu.make_async_remote_copy(
                src_ref=rows(a, block) if src is None else src,
                dst_ref=rows(a, block),
                send_sem=send_sems.at[7 * a + k],
                recv_sem=recv_sems.at[7 * a + k],
                device_id=to,
                device_id_type=MESH_ID,
            )

        mine = [pltpu.make_async_copy(ins[a], rows(a, me), local_sems.at[a]) for a in range(n)]
        for cp in mine:
            cp.start()
        first = []
        for a in range(n):
            first.append(copy(a, 0, me, sibling, src=ins[a]))
            first += [copy(a, 1 + j, me, (*chip, c), src=ins[a]) for j, chip in enumerate(chips)]
        for cp in first:
            cp.start()
        passed = []
        for j, chip in enumerate(chips):
            for a in range(n):
                copy(a, 1 + j, (*chip, c), me).wait_recv()
                fwd = copy(a, 4 + j, (*chip, c), sibling)
                fwd.start()
                passed.append(fwd)
        for a in range(n):
            copy(a, 0, sibling, me).wait_recv()
            for j, chip in enumerate(chips):
                copy(a, 4 + j, (*chip, 1 - c), me).wait_recv()
        for cp in first + passed:
            cp.wait_send()
        for cp in mine:
            cp.wait()

    any_spec = pl.BlockSpec(memory_space=pl.ANY)
    outs = _pcall(
        body,
        name=name,
        out_shape=[jax.ShapeDtypeStruct((N_DEV,) + a.shape, a.dtype) for a in arrs],
        in_specs=[any_spec] * n,
        out_specs=[any_spec] * n,
        scratch_shapes=[
            pltpu.SemaphoreType.DMA((7 * n,)),
            pltpu.SemaphoreType.DMA((7 * n,)),
            pltpu.SemaphoreType.DMA((n,)),
        ],
    )(*arrs)
    return list(outs)


_HBM = pl.BlockSpec(memory_space=pltpu.HBM)
_SEM = pl.BlockSpec(memory_space=pltpu.SEMAPHORE)
_EFFECT = pltpu.SideEffectType.DATAFLOW_SIDE_EFFECTING


def _descriptors(copies, send_sems, recv_sems):
    descs = []
    for i, (src, dst, dev) in enumerate(copies):
        if dev is None:
            descs.append(pltpu.make_async_copy(src, dst, recv_sems.at[i]))
        else:
            descs.append(pltpu.make_async_remote_copy(src_ref=src, dst_ref=dst, send_sem=send_sems.at[i],
                                                      recv_sem=recv_sems.at[i], device_id=dev, device_id_type=MESH_ID))
    return descs


def _split_start(name, arrays, copies_fn, n_copies):
    n = len(arrays)

    def body(*refs):
        send_sems, recv_sems = refs[n], refs[n + 1]
        token = refs[2 * n + 2]
        for dsc in _descriptors(copies_fn(refs[:n]), send_sems, recv_sems):
            dsc.start()
        token[...] = jnp.zeros_like(token)

    outs = _pcall(
        body,
        name=name,
        out_shape=(pltpu.SemaphoreType.DMA((n_copies,)), pltpu.SemaphoreType.DMA((n_copies,)),
                   *[pltpu.HBM(a.shape, a.dtype) for a in arrays], jax.ShapeDtypeStruct((8, LANE), F32)),
        in_specs=[_HBM] * n,
        out_specs=(_SEM, _SEM, *[_HBM] * n, pl.BlockSpec(memory_space=pltpu.VMEM)),
        input_output_aliases={i: 2 + i for i in range(n)},
        compiler_params=pltpu.CompilerParams(has_side_effects=_EFFECT),
    )(*[pltpu.with_memory_space_constraint(a, pltpu.HBM) for a in arrays])
    return outs[0], outs[1], list(outs[2:2 + n]), outs[2 + n]


def _split_wait(name, send_sems, recv_sems, arrays, copies_fn, after):
    n = len(arrays)

    def body(*refs):
        for dsc, (_, _, dev) in zip(_descriptors(copies_fn(refs[:n]), refs[n], refs[n + 1]), copies_fn(refs[:n])):
            if dev is None:
                dsc.wait()
            else:
                dsc.wait_send()
                dsc.wait_recv()

    outs = _pcall(
        body,
        name=name,
        out_shape=tuple(pltpu.HBM(a.shape, a.dtype) for a in arrays),
        in_specs=[_HBM] * n + [_SEM, _SEM, pl.BlockSpec(memory_space=pl.ANY)],
        out_specs=tuple([_HBM] * n),
        input_output_aliases={i: i for i in range(n)},
        compiler_params=pltpu.CompilerParams(has_side_effects=_EFFECT),
    )(*arrays, send_sems, recv_sems, after)
    return list(outs)


def _mesh_pos():
    x, y, c = lax.axis_index("x"), lax.axis_index("y"), lax.axis_index("c")
    return x, y, c, [(1 - x, y), (x, 1 - y), (1 - x, 1 - y)]


def _gather_ici_copies(n):
    def copies(refs):
        x, y, c, chips = _mesh_pos()
        me = 4 * x + 2 * y + c
        out = []
        for a in range(n):
            src, buf = refs[a], refs[n + a]
            out.append((src, buf.at[me], None))
            out.append((src, buf.at[me], (x, y, 1 - c)))
            out += [(src, buf.at[me], (cx, cy, c)) for cx, cy in chips[:2]]
        return out
    return copies


def _gather_pass_copies(n):
    def copies(refs):
        x, y, c, chips = _mesh_pos()
        south = c == 0
        bx, by = jnp.where(south, 1 - x, x), jnp.where(south, y, 1 - y)
        tx, ty = jnp.where(south, x, 1 - x), jnp.where(south, 1 - y, y)
        out = []
        for a in range(n):
            rows = refs[a].at[4 * bx + 2 * by + c]
            out.append((rows, rows, (tx, ty, c)))
            for cx, cy in chips[:2]:
                rows = refs[a].at[4 * cx + 2 * cy + c]
                out.append((rows, rows, (x, y, 1 - c)))
        return out
    return copies


def _gather_d2d_copies(n):
    def copies(refs):
        x, y, c, chips = _mesh_pos()
        cx, cy = chips[2]
        out = []
        for a in range(n):
            rows = refs[a].at[4 * cx + 2 * cy + c]
            out.append((rows, rows, (x, y, 1 - c)))
        return out
    return copies


def _reduce_d2d_copies(n):
    def copies(refs):
        x, y, c, _ = _mesh_pos()
        out = []
        for a in range(n):
            for k in range(4):
                out.append((refs[a].at[2 * k + (1 - c)], refs[n + a].at[k], (x, y, 1 - c)))
        return out
    return copies


def _reduce_ici_copies(n):
    def copies(refs):
        x, y, c, chips = _mesh_pos()
        mine = 2 * x + y
        out = []
        for a in range(n):
            src, land = refs[a], refs[n + a]
            out.append((src.at[mine], land.at[mine], None))
            out += [(src.at[2 * cx + cy], land.at[mine], (cx, cy, c)) for cx, cy in chips]
        return out
    return copies


def _pair_sum(send, land, c_idx, name):
    _, r, cols = send.shape
    rb = _tile(r, max(8, (1 << 22) // (send.dtype.itemsize * cols) // 8 * 8), 8)
    dt = send.dtype

    def body(c_ref, s_ref, l_ref, o_ref):
        o_ref[...] = (s_ref[...].astype(F32) + l_ref[...].astype(F32)).astype(dt)

    return pl.pallas_call(
        body,
        name=name,
        out_shape=jax.ShapeDtypeStruct((4, r, cols), dt),
        grid_spec=pltpu.PrefetchScalarGridSpec(
            num_scalar_prefetch=1,
            grid=(4, r // rb),
            in_specs=[pl.BlockSpec((None, rb, cols), lambda k, i, c_ref: (2 * k + c_ref[0], i, 0)),
                      pl.BlockSpec((None, rb, cols), lambda k, i, c_ref: (k, i, 0))],
            out_specs=pl.BlockSpec((None, rb, cols), lambda k, i, c_ref: (k, i, 0)),
        ),
        compiler_params=_cparams(("parallel", "parallel")),
    )(c_idx, send, land)


_DIMS = {
    "nn": (((1,), (0,)), ((), ())),
    "nt": (((1,), (1,)), ((), ())),
    "tn": (((0,), (0,)), ((), ())),
}


def _mm_call(a, b, *, mode, grid, a_spec, b_spec, o_spec, out_shape, acc_shape, name):
    nk = grid[2]
    out_dtype = out_shape.dtype

    def body(a_ref, b_ref, o_ref, *scratch):
        p = lax.dot_general(a_ref[...].astype(BF16), b_ref[...].astype(BF16), _DIMS[mode],
                            preferred_element_type=F32)
        if nk == 1:
            o_ref[...] = p.astype(out_dtype)
        else:
            acc = scratch[0]
            k = pl.program_id(2)

            @pl.when(k == 0)
            def _():
                acc[...] = p

            @pl.when(k > 0)
            def _():
                acc[...] += p

            @pl.when(k == nk - 1)
            def _():
                o_ref[...] = acc[...].astype(out_dtype)

    return _pcall(
        body,
        name=name,
        out_shape=out_shape,
        grid=grid,
        in_specs=[a_spec, b_spec],
        out_specs=o_spec,
        scratch_shapes=[pltpu.VMEM(acc_shape, F32)] if nk > 1 else [],
        compiler_params=_cparams(("parallel", "parallel", "arbitrary")),
    )(a, b)


def _mm(a, b, mode, out_dtype, name, tm=512, tn=512, tk=2432, a_row_off=0, rows=None):
    if mode == "nn":
        (m, k), (k2, n) = a.shape, b.shape
    elif mode == "nt":
        (m, k), (n, k2) = a.shape, b.shape
    else:
        (k, m), (k2, n) = a.shape, b.shape
        if rows is not None:
            k = k2 = rows
    assert k == k2, (a.shape, b.shape, mode)
    if mode != "tn":
        m = (m if rows is None else rows + a_row_off) - a_row_off
    tm, tn, tk = _tile(m, tm, 8), _tile(n, tn), _tile(k, tk, 8 if mode == "tn" else LANE)
    assert a_row_off % tm == 0
    ro = a_row_off // tm
    grid = (m // tm, n // tn, k // tk)
    if mode == "tn":
        a_spec = pl.BlockSpec((tk, tm), lambda i, j, kk: (kk, i))
    else:
        a_spec = pl.BlockSpec((tm, tk), lambda i, j, kk: (i + ro, kk))
    if mode == "nt":
        b_spec = pl.BlockSpec((tn, tk), lambda i, j, kk: (j, kk))
    else:
        b_spec = pl.BlockSpec((tk, tn), lambda i, j, kk: (kk, j))
    o_spec = pl.BlockSpec((tm, tn), lambda i, j, kk: (i, j))
    return _mm_call(a, b, mode=mode, grid=grid, a_spec=a_spec, b_spec=b_spec, o_spec=o_spec,
                    out_shape=jax.ShapeDtypeStruct((m, n), out_dtype), acc_shape=(tm, tn), name=name)


def _mm_cat_nt(pieces, out_dtype, name, tm=1024, tn=1024, tk=2048, rows=None):
    m = pieces[0][0].shape[0] if rows is None else rows
    n = pieces[0][1].shape[0]
    tm, tn = _tile(m, tm, 8), _tile(n, tn)
    steps, starts, s = [], [], 0
    for a, b, off in pieces:
        kp = a.shape[1]
        tkp = _tile(kp, tk)
        assert off % tkp == 0 and b.shape[0] == n
        steps.append((tkp, kp // tkp, off // tkp))
        starts.append(s)
        s += kp // tkp
    nk = s
    npc = len(pieces)

    def body(*refs):
        o_ref, acc = refs[2 * npc], refs[2 * npc + 1]
        kk = pl.program_id(2)

        @pl.when(kk == 0)
        def _():
            acc[...] = jnp.zeros_like(acc)

        for p in range(npc):
            @pl.when((kk >= starts[p]) & (kk < starts[p] + steps[p][1]))
            def _(p=p):
                acc[...] += lax.dot_general(refs[2 * p][...].astype(BF16), refs[2 * p + 1][...].astype(BF16), _DIMS["nt"],
                                            preferred_element_type=F32)

        @pl.when(kk == nk - 1)
        def _():
            o_ref[...] = acc[...].astype(out_dtype)

    in_specs, args = [], []
    for p, (a, b, off) in enumerate(pieces):
        tkp, np_, ob = steps[p]

        def rel(kk, p=p, np_=np_):
            return jnp.clip(kk - starts[p], 0, np_ - 1)

        in_specs.append(pl.BlockSpec((tm, tkp), lambda i, j, kk, rel=rel: (i, rel(kk))))
        in_specs.append(pl.BlockSpec((tn, tkp), lambda i, j, kk, rel=rel, ob=ob: (j, ob + rel(kk))))
        args += [a, b]
    return _pcall(
        body,
        name=name,
        out_shape=jax.ShapeDtypeStruct((m, n), out_dtype),
        grid=(m // tm, n // tn, nk),
        in_specs=in_specs,
        out_specs=pl.BlockSpec((tm, tn), lambda i, j, kk: (i, j)),
        scratch_shapes=[pltpu.VMEM((tm, tn), F32)],
        compiler_params=_cparams(("parallel", "parallel", "arbitrary")),
    )(*args)


def _mm_cat_tn(a, pieces, out_dtype, name, tm=1024, tn=1024, rows=None):
    k = a.shape[0] if rows is None else rows
    m = a.shape[1]
    tm = _tile(m, tm)
    starts, s = [], 0
    for b in pieces:
        assert b.shape[1] % tn == 0
        starts.append(s)
        s += b.shape[1] // tn
    nj = s
    npc = len(pieces)

    def body(*refs):
        a_ref, o_ref = refs[0], refs[1 + npc]
        j = pl.program_id(1)
        for p in range(npc):
            @pl.when((j >= starts[p]) & (j < starts[p] + pieces[p].shape[1] // tn))
            def _(p=p):
                o_ref[...] = lax.dot_general(a_ref[...].astype(BF16), refs[1 + p][...].astype(BF16), _DIMS["tn"],
                                             preferred_element_type=F32).astype(out_dtype)

    in_specs = [pl.BlockSpec((k, tm), lambda i, j: (0, i))]
    for p, b in enumerate(pieces):
        np_ = b.shape[1] // tn
        in_specs.append(pl.BlockSpec((k, tn), lambda i, j, p=p, np_=np_: (0, jnp.clip(j - starts[p], 0, np_ - 1))))
    return _pcall(
        body,
        name=name,
        out_shape=jax.ShapeDtypeStruct((m, nj * tn), out_dtype),
        grid=(m // tm, nj),
        in_specs=in_specs,
        out_specs=pl.BlockSpec((tm, tn), lambda i, j: (i, j)),
        compiler_params=_cparams(("parallel", "arbitrary")),
    )(a, *pieces)


def _mm_up_fwd(z2, w3, name, tm=1024):
    t, d = z2.shape
    nsh, _, c = w3.shape
    tm = _tile(t, tm, 8)
    return _mm_call(z2, w3, mode="nn", grid=(t // tm, nsh, 1),
                    a_spec=pl.BlockSpec((tm, d), lambda i, j, kk: (i, 0)),
                    b_spec=pl.BlockSpec((None, d, c), lambda i, j, kk: (j, 0, 0)),
                    o_spec=pl.BlockSpec((tm, c), lambda i, j, kk: (i, j)),
                    out_shape=jax.ShapeDtypeStruct((t, nsh * c), BF16), acc_shape=(tm, c), name=name)


def _mm_up_dz(du3, w3, name, tm=512, tn=1024):
    _, t, f = du3.shape
    nsh, d, c = w3.shape
    half = nsh // 2
    assert f == half * c
    tm, tn = _tile(t, tm, 8), _tile(d, tn)

    def body(a_ref, b_ref, o_ref, acc):
        kk = pl.program_id(2)
        p = None
        for s in range(half):
            q = lax.dot_general(a_ref[:, s * c:(s + 1) * c], b_ref[s], _DIMS["nt"], preferred_element_type=F32)
            p = q if p is None else p + q

        @pl.when(kk == 0)
        def _():
            acc[...] = p

        @pl.when(kk == 1)
        def _():
            o_ref[...] = (acc[...] + p).astype(BF16)

    return _pcall(
        body,
        name=name,
        out_shape=jax.ShapeDtypeStruct((t, d), BF16),
        grid=(t // tm, d // tn, 2),
        in_specs=[pl.BlockSpec((None, tm, f), lambda i, j, kk: (kk, i, 0)),
                  pl.BlockSpec((half, tn, c), lambda i, j, kk: (kk, j, 0))],
        out_specs=pl.BlockSpec((tm, tn), lambda i, j, kk: (i, j)),
        scratch_shapes=[pltpu.VMEM((tm, tn), F32)],
        compiler_params=_cparams(("parallel", "parallel", "arbitrary")),
    )(du3, w3)


def _mm_sum_nn(pieces, out_dtype, name, tm=512, tn=512, rows=None):
    m = pieces[0][0].shape[0] if rows is None else rows
    n = pieces[0][2].shape[1]
    tm, tn = _tile(m, tm, 8), _tile(n, tn)
    npc = len(pieces)

    def body(*refs):
        p = None
        for s in range(npc):
            q = jnp.dot(refs[2 * s][...].astype(BF16), refs[2 * s + 1][...].astype(BF16), preferred_element_type=F32)
            p = q if p is None else p + q
        refs[2 * npc][...] = p.astype(out_dtype)

    in_specs, args = [], []
    for a, ao, b, bo, kp in pieces:
        assert ao % kp == 0 and bo % kp == 0 and b.shape[1] == n
        in_specs.append(pl.BlockSpec((tm, kp), lambda i, j, ab=ao // kp: (i, ab)))
        in_specs.append(pl.BlockSpec((kp, tn), lambda i, j, bb=bo // kp: (bb, j)))
        args += [a, b]
    return _pcall(
        body,
        name=name,
        out_shape=jax.ShapeDtypeStruct((m, n), out_dtype),
        grid=(m // tm, n // tn),
        in_specs=in_specs,
        out_specs=pl.BlockSpec((tm, tn), lambda i, j: (i, j)),
        compiler_params=_cparams(("parallel", "parallel")),
    )(*args)


def _mm_rows_tn(pieces, b, out_dtype, name, tm=1024, tn=1024, rows=None):
    k = b.shape[0] if rows is None else rows
    n = b.shape[1]
    tn = _tile(n, tn)
    starts, s = [], 0
    for a in pieces:
        assert a.shape[1] % tm == 0
        starts.append(s)
        s += a.shape[1] // tm
    ni = s
    npc = len(pieces)

    def body(*refs):
        b_ref, o_ref = refs[npc], refs[npc + 1]
        i = pl.program_id(0)
        for p in range(npc):
            @pl.when((i >= starts[p]) & (i < starts[p] + pieces[p].shape[1] // tm))
            def _(p=p):
                o_ref[...] = lax.dot_general(refs[p][...].astype(BF16), b_ref[...].astype(BF16), _DIMS["tn"],
                                             preferred_element_type=F32).astype(out_dtype)

    in_specs = []
    for p, a in enumerate(pieces):
        np_ = a.shape[1] // tm
        in_specs.append(pl.BlockSpec((k, tm), lambda i, j, p=p, np_=np_: (0, jnp.clip(i - starts[p], 0, np_ - 1))))
    in_specs.append(pl.BlockSpec((k, tn), lambda i, j: (0, j)))
    return _pcall(
        body,
        name=name,
        out_shape=jax.ShapeDtypeStruct((ni * tm, n), out_dtype),
        grid=(ni, n // tn),
        in_specs=in_specs,
        out_specs=pl.BlockSpec((tm, tn), lambda i, j: (i, j)),
        compiler_params=_cparams(("parallel", "parallel")),
    )(*pieces, b)


def _mm_up_gw(z2, du3, nsh, name, tm=1024):
    t, d = z2.shape
    f = du3.shape[2]
    half = nsh // 2
    c = f // half
    tm = _tile(d, tm)
    return _mm_call(z2, du3, mode="tn", grid=(d // tm, nsh, 1),
                    a_spec=pl.BlockSpec((t, tm), lambda i, j, kk: (0, i)),
                    b_spec=pl.BlockSpec((None, t, c), lambda i, j, kk: (j // half, 0, j % half)),
                    o_spec=pl.BlockSpec((None, tm, c), lambda i, j, kk: (j, i, 0)),
                    out_shape=jax.ShapeDtypeStruct((nsh, d, c), BF16), acc_shape=(tm, c), name=name)


def _rms(x):
    r = lax.rsqrt(jnp.mean(x * x, axis=-1, keepdims=True) + NORM_EPS)
    return x * r, r


def _rms_bwd(dxh, xh, r):
    return r * (dxh - xh * jnp.mean(dxh * xh, axis=-1, keepdims=True))


def _colsum(v):
    return jnp.sum(v, axis=0, keepdims=True)


def _rope(v, c, s1, s2, q):
    w = v.shape[-1]
    return v * c + pltpu.roll(v, w - q, 1) * s1 + pltpu.roll(v, q, 1) * s2


def _rope_t(d, c, s1, s2, q):
    w = d.shape[-1]
    return d * c + pltpu.roll(d * s1, q, 1) + pltpu.roll(d * s2, w - q, 1)


def _norm_mod_fwd(ctx, x, gain, mods):
    tc, d = ctx.shape
    t = x.shape[0]
    rb = min(ROW_BLOCK, tc)
    nbl = t // rb

    def body(ctx_ref, x_ref, g_ref, mod_ref, z_ref):
        i = pl.program_id(0)

        def emit(src, sh, sc):
            xh, _ = _rms(src[...])
            z_ref[...] = ((xh * g_ref[...]) * (1.0 + sc) + sh).astype(BF16)

        @pl.when(i >= nbl)
        def _():
            emit(ctx_ref, mod_ref[2:3, :], mod_ref[3:4, :])

        @pl.when(i < nbl)
        def _():
            emit(x_ref, mod_ref[0:1, :], mod_ref[1:2, :])

    return _pcall(
        body,
        name="norm1_mod_fwd",
        out_shape=jax.ShapeDtypeStruct((tc + t, d), BF16),
        grid=((tc + t) // rb,),
        in_specs=[
            pl.BlockSpec((rb, d), lambda i: (jnp.maximum(i - nbl, 0), 0)),
            pl.BlockSpec((rb, d), lambda i: (jnp.minimum(i, nbl - 1), 0)),
            pl.BlockSpec((1, d), lambda i: (0, 0)),
            pl.BlockSpec((8, d), lambda i: (0, 0)),
        ],
        out_specs=pl.BlockSpec((rb, d), lambda i: (i, 0)),
        compiler_params=_cparams(("arbitrary",)),
    )(ctx, x, gain, mods)


def _norm1_bwd(ctx, x, gain, mods, dz_ctx, dz_lat, dx1):
    tc, d = ctx.shape
    t = x.shape[0]
    rb = min(ROW_BLOCK, tc)
    nbl = t // rb

    def body(ctx_ref, x_ref, g_ref, mod_ref, dzc_ref, dzl_ref, dx1_ref, gx_ref, st_ref):
        i = pl.program_id(0)

        @pl.when(i == 0)
        def _():
            st_ref[...] = jnp.zeros_like(st_ref)

        def common(src, dz, sc, row_sh, row_sc):
            xh, r = _rms(src[...])
            g = g_ref[...]
            dxn = dz * (1.0 + sc)
            st_ref[row_sh:row_sh + 1, :] += _colsum(dz)
            st_ref[row_sc:row_sc + 1, :] += _colsum(dz * (xh * g))
            st_ref[2:3, :] += _colsum(dxn * xh)
            return _rms_bwd(dxn * g, xh, r)

        @pl.when(i >= nbl)
        def _():
            common(ctx_ref, dzc_ref[...], mod_ref[3:4, :], 3, 4)

        @pl.when(i < nbl)
        def _():
            gx_ref[...] = dx1_ref[...] + common(x_ref, dzl_ref[...], mod_ref[1:2, :], 0, 1)

    lat = lambda i: (jnp.minimum(i, nbl - 1), 0)
    cix = lambda i: (jnp.maximum(i - nbl, 0), 0)
    return _pcall(
        body,
        name="norm1_mod_bwd",
        out_shape=[jax.ShapeDtypeStruct((t, d), F32), jax.ShapeDtypeStruct((8, d), F32)],
        grid=((tc + t) // rb,),
        in_specs=[
            pl.BlockSpec((rb, d), cix),
            pl.BlockSpec((rb, d), lat),
            pl.BlockSpec((1, d), lambda i: (0, 0)),
            pl.BlockSpec((8, d), lambda i: (0, 0)),
            pl.BlockSpec((rb, d), cix),
            pl.BlockSpec((rb, d), lat),
            pl.BlockSpec((rb, d), lat),
        ],
        out_specs=[pl.BlockSpec((rb, d), lat), pl.BlockSpec((8, d), lambda i: (0, 0))],
        compiler_params=_cparams(("arbitrary",)),
    )(ctx, x, gain, mods, dz_ctx, dz_lat, dx1)


def _key_prep_fwd(kv, kv_gain, kb_gain, tabs):
    ta, wkv = kv.shape
    kvl = MLA_KV_LORA
    nb = GQA_KV_HEADS * GQA_HEAD_DIM
    rb = ROW_BLOCK if ta % ROW_BLOCK == 0 else LANE
    hd = GQA_HEAD_DIM

    def body(kv_ref, g_ref, gb_ref, ca, s1a, s2a, cb, s1b, s2b, kin_ref, kb_ref, vb_ref):
        xh, _ = _rms(kv_ref[:, 0:kvl])
        kin_ref[:, 0:kvl] = (xh * g_ref[...]).astype(BF16)
        kpe = kv_ref[:, kvl + 2 * nb:kvl + 2 * nb + LANE]
        kin_ref[:, kvl:kvl + LANE] = _rope(kpe, ca[...], s1a[...], s2a[...], MLA_ROPE // 4).astype(BF16)
        for h in range(GQA_KV_HEADS):
            nh, _ = _rms(kv_ref[:, kvl + h * hd:kvl + (h + 1) * hd])
            kb_ref[:, h * hd:(h + 1) * hd] = _rope(nh * gb_ref[...], cb[...], s1b[...], s2b[...], hd // 4).astype(BF16)
        vb_ref[...] = kv_ref[:, kvl + nb:kvl + 2 * nb].astype(BF16)

    row = lambda w: pl.BlockSpec((rb, w), lambda i: (i, 0))
    fix = lambda w: pl.BlockSpec((1, w), lambda i: (0, 0))
    return _pcall(
        body,
        name="key_prep_fwd",
        out_shape=[jax.ShapeDtypeStruct((ta, kvl + LANE), BF16), jax.ShapeDtypeStruct((ta, nb), BF16),
                   jax.ShapeDtypeStruct((ta, nb), BF16)],
        grid=(ta // rb,),
        in_specs=[row(wkv), fix(kvl), fix(hd)] + [row(LANE)] * 3 + [row(hd)] * 3,
        out_specs=[row(kvl + LANE), row(nb), row(nb)],
        compiler_params=_cparams(("parallel",)),
    )(kv, kv_gain, kb_gain, *tabs)


def _key_prep_bwd(kv, kv_gain, kb_gain, tabs, dkin, dkb, dvb):
    ta, wkv = kv.shape
    kvl = MLA_KV_LORA
    nb = GQA_KV_HEADS * GQA_HEAD_DIM
    rb = ROW_BLOCK if ta % ROW_BLOCK == 0 else LANE
    hd = GQA_HEAD_DIM

    def body(kv_ref, g_ref, gb_ref, ca, s1a, s2a, cb, s1b, s2b, dkin_ref, dkb_ref, dvb_ref, dkv_ref, st_ref, stb_ref):
        @pl.when(pl.program_id(0) == 0)
        def _():
            st_ref[...] = jnp.zeros_like(st_ref)
            stb_ref[...] = jnp.zeros_like(stb_ref)

        xh, r = _rms(kv_ref[:, 0:kvl])
        dn = dkin_ref[:, 0:kvl]
        st_ref[0:1, :] += _colsum(dn * xh)
        dkv_ref[:, 0:kvl] = _rms_bwd(dn * g_ref[...], xh, r).astype(BF16)
        dpe = _rope_t(dkin_ref[:, kvl:kvl + LANE], ca[...], s1a[...], s2a[...], MLA_ROPE // 4)
        dkv_ref[:, kvl + 2 * nb:kvl + 2 * nb + LANE] = dpe.astype(BF16)
        for h in range(GQA_KV_HEADS):
            nh, rh = _rms(kv_ref[:, kvl + h * hd:kvl + (h + 1) * hd])
            dn_h = _rope_t(dkb_ref[:, h * hd:(h + 1) * hd], cb[...], s1b[...], s2b[...], hd // 4)
            stb_ref[0:1, :] += _colsum(dn_h * nh)
            dkv_ref[:, kvl + h * hd:kvl + (h + 1) * hd] = _rms_bwd(dn_h * gb_ref[...], nh, rh).astype(BF16)
        dkv_ref[:, kvl + nb:kvl + 2 * nb] = dvb_ref[...].astype(BF16)

    row = lambda w: pl.BlockSpec((rb, w), lambda i: (i, 0))
    fix = lambda w: pl.BlockSpec((1, w), lambda i: (0, 0))
    return _pcall(
        body,
        name="key_prep_bwd",
        out_shape=[jax.ShapeDtypeStruct((ta, wkv), BF16), jax.ShapeDtypeStruct((8, kvl), F32),
                   jax.ShapeDtypeStruct((8, hd), F32)],
        grid=(ta // rb,),
        in_specs=[row(wkv), fix(kvl), fix(hd)] + [row(LANE)] * 3 + [row(hd)] * 3 + [row(kvl + LANE), row(nb), row(nb)],
        out_specs=[row(wkv), pl.BlockSpec((8, kvl), lambda i: (0, 0)), pl.BlockSpec((8, hd), lambda i: (0, 0))],
        compiler_params=_cparams(("arbitrary",)),
    )(kv, kv_gain, kb_gain, *tabs, dkin, dkb, dvb)


def _q_prep_fwd(qg, q_gain, qb_gain, tabs, qscale):
    t = qg.shape[0]
    ql = MLA_Q_LORA
    hd = GQA_HEAD_DIM
    hb = GQA_HEADS * hd
    rb = min(ROW_BLOCK, t)

    def body(q_ref, g_ref, gb_ref, cb, s1b, s2b, cqn_ref, qb_ref):
        xh, _ = _rms(q_ref[:, 0:ql])
        cqn_ref[...] = (xh * g_ref[...]).astype(BF16)
        for h in range(GQA_HEADS):
            nh, _ = _rms(q_ref[:, ql + h * hd:ql + (h + 1) * hd])
            qh = _rope(nh * gb_ref[...], cb[...], s1b[...], s2b[...], hd // 4)
            qb_ref[:, h * hd:(h + 1) * hd] = (qh * qscale).astype(BF16)

    row = lambda w: pl.BlockSpec((rb, w), lambda i: (i, 0))
    fix = lambda w: pl.BlockSpec((1, w), lambda i: (0, 0))
    return _pcall(
        body,
        name="q_prep_fwd",
        out_shape=[jax.ShapeDtypeStruct((t, ql), BF16), jax.ShapeDtypeStruct((t, hb), BF16)],
        grid=(t // rb,),
        in_specs=[row(ql + hb), fix(ql), fix(hd)] + [row(hd)] * 3,
        out_specs=[row(ql), row(hb)],
        compiler_params=_cparams(("parallel",)),
    )(qg, q_gain, qb_gain, *tabs)


def _q_prep_bwd(qg, q_gain, qb_gain, tabs, dcqn, dqb, wpad, qscale):
    t = qg.shape[0]
    ql = MLA_Q_LORA
    hd = GQA_HEAD_DIM
    hb = GQA_HEADS * hd
    rb = min(ROW_BLOCK, t)

    def body(q_ref, g_ref, gb_ref, cb, s1b, s2b, dcqn_ref, dqb_ref, dq_ref, st_ref, stb_ref):
        @pl.when(pl.program_id(0) == 0)
        def _():
            st_ref[...] = jnp.zeros_like(st_ref)
            stb_ref[...] = jnp.zeros_like(stb_ref)

        xh, r = _rms(q_ref[:, 0:ql])
        dn = dcqn_ref[...]
        st_ref[0:1, :] += _colsum(dn * xh)
        dq_ref[:, 0:ql] = _rms_bwd(dn * g_ref[...], xh, r).astype(BF16)
        for h in range(GQA_HEADS):
            nh, rh = _rms(q_ref[:, ql + h * hd:ql + (h + 1) * hd])
            dn_h = _rope_t(dqb_ref[:, h * hd:(h + 1) * hd] * qscale, cb[...], s1b[...], s2b[...], hd // 4)
            stb_ref[0:1, :] += _colsum(dn_h * nh)
            dq_ref[:, ql + h * hd:ql + (h + 1) * hd] = _rms_bwd(dn_h * gb_ref[...], nh, rh).astype(BF16)
        if wpad:
            dq_ref[:, ql + hb:ql + hb + wpad] = jnp.zeros((rb, wpad), BF16)

    row = lambda w: pl.BlockSpec((rb, w), lambda i: (i, 0))
    fix = lambda w: pl.BlockSpec((1, w), lambda i: (0, 0))
    return _pcall(
        body,
        name="q_prep_bwd",
        out_shape=[jax.ShapeDtypeStruct((t, ql + hb + wpad), BF16), jax.ShapeDtypeStruct((8, ql), F32),
                   jax.ShapeDtypeStruct((8, hd), F32)],
        grid=(t // rb,),
        in_specs=[row(ql + hb), fix(ql), fix(hd)] + [row(hd)] * 3 + [row(ql), row(hb)],
        out_specs=[row(ql + hb + wpad), pl.BlockSpec((8, ql), lambda i: (0, 0)), pl.BlockSpec((8, hd), lambda i: (0, 0))],
        compiler_params=_cparams(("arbitrary",)),
    )(qg, q_gain, qb_gain, *tabs, dcqn, dqb)


def _rope_a(v, tabs, transpose, out_dtype, name, qscale):
    t, w = v.shape
    rb = min(ROW_BLOCK, t)
    fn = _rope_t if transpose else _rope

    def body(v_ref, c, s1, s2, o_ref):
        for h in range(w // MLA_SLOT):
            sl = slice(h * MLA_SLOT, (h + 1) * MLA_SLOT)
            o_ref[:, sl] = (fn(v_ref[:, sl].astype(F32), c[...], s1[...], s2[...], MLA_ROPE // 4) * qscale).astype(out_dtype)

    row = lambda ww: pl.BlockSpec((rb, ww), lambda i: (i, 0))
    return _pcall(
        body,
        name=name,
        out_shape=jax.ShapeDtypeStruct((t, w), out_dtype),
        grid=(t // rb,),
        in_specs=[row(w)] + [row(MLA_SLOT)] * 3,
        out_specs=row(w),
        compiler_params=_cparams(("parallel",)),
    )(v, *tabs)


def _merge_fwd(pa, pb, qg, gate_blk):
    t, d = pa.shape
    rb = min(ROW_BLOCK, t)

    def body(pa_ref, pb_ref, ga_ref, gb_ref, o_ref):
        o_ref[...] = (jax.nn.sigmoid(ga_ref[...]) * pa_ref[...].astype(F32)
                      + jax.nn.sigmoid(gb_ref[...]) * pb_ref[...].astype(F32)).astype(BF16)

    row = pl.BlockSpec((rb, d), lambda i: (i, 0))
    return _pcall(
        body,
        name="merge_fwd",
        out_shape=jax.ShapeDtypeStruct((t, d), BF16),
        grid=(t // rb,),
        in_specs=[row, row, pl.BlockSpec((rb, d), lambda i: (i, gate_blk)), pl.BlockSpec((rb, d), lambda i: (i, gate_blk + 1))],
        out_specs=row,
        compiler_params=_cparams(("parallel",)),
    )(pa, pb, qg, qg)


def _merge_bwd(dm, pa, pb, qg, gate_blk):
    t, d = pa.shape
    rb = min(ROW_BLOCK, t)

    def body(dm_ref, pa_ref, pb_ref, ga_ref, gb_ref, dpa_ref, dpb_ref, dg_ref):
        dmv = dm_ref[...].astype(F32)
        sa = jax.nn.sigmoid(ga_ref[...])
        sb = jax.nn.sigmoid(gb_ref[...])
        dpa_ref[...] = (dmv * sa).astype(BF16)
        dpb_ref[...] = (dmv * sb).astype(BF16)
        dg_ref[:, 0:d] = (dmv * pa_ref[...].astype(F32) * (sa * (1.0 - sa))).astype(BF16)
        dg_ref[:, d:2 * d] = (dmv * pb_ref[...].astype(F32) * (sb * (1.0 - sb))).astype(BF16)

    row = pl.BlockSpec((rb, d), lambda i: (i, 0))
    return _pcall(
        body,
        name="merge_bwd",
        out_shape=[jax.ShapeDtypeStruct((t, d), BF16), jax.ShapeDtypeStruct((t, d), BF16),
                   jax.ShapeDtypeStruct((t, 2 * d), BF16)],
        grid=(t // rb,),
        in_specs=[row, row, row, pl.BlockSpec((rb, d), lambda i: (i, gate_blk)), pl.BlockSpec((rb, d), lambda i: (i, gate_blk + 1))],
        out_specs=[row, row, pl.BlockSpec((rb, 2 * d), lambda i: (i, 0))],
        compiler_params=_cparams(("parallel",)),
    )(dm, pa, pb, qg, qg)


def _resid_norm_mod(x, branch, gain, mods, name):
    t, d = x.shape
    rb = min(ROW_BLOCK, t)

    def body(x_ref, b_ref, g_ref, mod_ref, x1_ref, z_ref):
        x1 = x_ref[...] + mod_ref[0:1, :] * b_ref[...]
        x1_ref[...] = x1
        xh, _ = _rms(x1)
        z_ref[...] = ((xh * g_ref[...]) * (1.0 + mod_ref[2:3, :]) + mod_ref[1:2, :]).astype(BF16)

    row = pl.BlockSpec((rb, d), lambda i: (i, 0))
    return _pcall(
        body,
        name=name,
        out_shape=[jax.ShapeDtypeStruct((t, d), F32), jax.ShapeDtypeStruct((t, d), BF16)],
        grid=(t // rb,),
        in_specs=[row, row, pl.BlockSpec((1, d), lambda i: (0, 0)), pl.BlockSpec((8, d), lambda i: (0, 0))],
        out_specs=[row, row],
        compiler_params=_cparams(("parallel",)),
    )(x, branch, gain, mods)


def _norm2_bwd(x1, attn, gain, mods, dz2, dx2):
    t, d = x1.shape
    rb = min(ROW_BLOCK, t)

    def body(x1_ref, at_ref, g_ref, mod_ref, dz_ref, dx2_ref, dx1_ref, da_ref, st_ref):
        @pl.when(pl.program_id(0) == 0)
        def _():
            st_ref[...] = jnp.zeros_like(st_ref)

        xh, r = _rms(x1_ref[...])
        g = g_ref[...]
        dz = dz_ref[...].astype(F32)
        dxn = dz * (1.0 + mod_ref[1:2, :])
        st_ref[0:1, :] += _colsum(dz)
        st_ref[1:2, :] += _colsum(dz * (xh * g))
        st_ref[2:3, :] += _colsum(dxn * xh)
        dx1 = dx2_ref[...] + _rms_bwd(dxn * g, xh, r)
        dx1_ref[...] = dx1
        st_ref[3:4, :] += _colsum(dx1 * at_ref[...])
        da_ref[...] = (dx1 * mod_ref[0:1, :]).astype(BF16)

    row = pl.BlockSpec((rb, d), lambda i: (i, 0))
    return _pcall(
        body,
        name="norm2_mod_bwd",
        out_shape=[jax.ShapeDtypeStruct((t, d), F32), jax.ShapeDtypeStruct((t, d), BF16), jax.ShapeDtypeStruct((8, d), F32)],
        grid=(t // rb,),
        in_specs=[row, row, pl.BlockSpec((1, d), lambda i: (0, 0)), pl.BlockSpec((8, d), lambda i: (0, 0)), row, row],
        out_specs=[row, row, pl.BlockSpec((8, d), lambda i: (0, 0))],
        compiler_params=_cparams(("arbitrary",)),
    )(x1, attn, gain, mods, dz2, dx2)


def _final_loss(x1, ffn, gain, mods, target):
    t, d = x1.shape
    rb = min(ROW_BLOCK, t)
    nb = t // rb

    def body(x1_ref, f_ref, g_ref, mod_ref, tg_ref, dx2_ref, df_ref, st_ref):
        i = pl.program_id(0)

        @pl.when(i == 0)
        def _():
            st_ref[...] = jnp.zeros_like(st_ref)

        ffn_v = f_ref[...]
        g2 = mod_ref[0:1, :]
        x2 = x1_ref[...] + g2 * ffn_v
        xh, r = _rms(x2)
        g = g_ref[...]
        err = xh * g - tg_ref[...]
        st_ref[2:3, :] += _colsum(err * err) * (0.5 / d)
        dy = err * (1.0 / d)
        st_ref[0:1, :] += _colsum(dy * xh)
        dx2 = _rms_bwd(dy * g, xh, r)
        dx2_ref[...] = dx2
        st_ref[1:2, :] += _colsum(dx2 * ffn_v)
        df_ref[...] = (dx2 * g2).astype(BF16)

        @pl.when(i == nb - 1)
        def _():
            st_ref[3:4, :] = jnp.broadcast_to(jnp.sum(st_ref[2:3, :], axis=-1, keepdims=True), (1, d))

    row = pl.BlockSpec((rb, d), lambda i: (i, 0))
    return _pcall(
        body,
        name="final_norm_loss",
        out_shape=[jax.ShapeDtypeStruct((t, d), F32), jax.ShapeDtypeStruct((t, d), BF16), jax.ShapeDtypeStruct((8, d), F32)],
        grid=(nb,),
        in_specs=[row, row, pl.BlockSpec((1, d), lambda i: (0, 0)), pl.BlockSpec((8, d), lambda i: (0, 0)), row],
        out_specs=[row, row, pl.BlockSpec((8, d), lambda i: (0, 0))],
        compiler_params=_cparams(("arbitrary",)),
    )(x1, ffn, gain, mods, target)


def _row_ends(shape):
    rows = lax.broadcasted_iota(jnp.int32, shape, 0)
    return rows == 0, rows == shape[0] - 1


def _shift_dn(v, first):
    return jnp.where(first, 0.0, pltpu.roll(v, 1, 0))


def _shift_up(v, last):
    return jnp.where(last, 0.0, pltpu.roll(v, v.shape[0] - 1, 0))


def _conv_fwd(u, cw, cb):
    t, f2 = u.shape
    f = f2 // 2
    cbk = _tile(f, 256)
    nf = f // cbk

    def body(ua_ref, ub_ref, cwa_ref, cwb_ref, cba_ref, cbb_ref, h_ref, uc_ref):
        first, last = _row_ends((t, cbk))
        outs = []
        for u_ref, cw_ref, cb_ref in ((ua_ref, cwa_ref, cba_ref), (ub_ref, cwb_ref, cbb_ref)):
            uu, cwv = u_ref[...].astype(F32), cw_ref[...]
            outs.append(cb_ref[...] + cwv[0:1, :] * _shift_dn(uu, first) + cwv[1:2, :] * uu
                        + cwv[2:3, :] * _shift_up(uu, last))
        a, b = outs
        uc_ref[0] = a.astype(BF16)
        uc_ref[1] = b.astype(BF16)
        h_ref[...] = (a * jax.nn.sigmoid(a) * b).astype(BF16)

    ca = lambda r: pl.BlockSpec((r, cbk), lambda j: (0, j))
    cbs = lambda r: pl.BlockSpec((r, cbk), lambda j: (0, nf + j))
    return _pcall(
        body,
        name="conv_gate_fwd",
        out_shape=[jax.ShapeDtypeStruct((t, f), BF16), jax.ShapeDtypeStruct((2, t, f), BF16)],
        grid=(nf,),
        in_specs=[ca(t), cbs(t), ca(3), cbs(3), ca(1), cbs(1)],
        out_specs=[ca(t), pl.BlockSpec((2, t, cbk), lambda j: (0, 0, j))],
        compiler_params=_cparams(("parallel",)),
    )(u, u, cw, cw, cb, cb)


def _conv_bwd(u, uc, cw, dh):
    t, f2 = u.shape
    f = f2 // 2
    cbk = _tile(f, 256)
    nf = f // cbk

    def body(ua_ref, ub_ref, uc_ref, cwa_ref, cwb_ref, dh_ref, du_ref, dcw_ref, dcb_ref):
        first, last = _row_ends((t, cbk))
        a, b = uc_ref[0].astype(F32), uc_ref[1].astype(F32)
        dh_v = dh_ref[...].astype(F32)
        sg = jax.nn.sigmoid(a)
        db = dh_v * (a * sg)
        da = dh_v * b * (sg * (1.0 + a * (1.0 - sg)))
        for idx, (dv, u_ref, cw_ref) in enumerate(((da, ua_ref, cwa_ref), (db, ub_ref, cwb_ref))):
            uu, cwv = u_ref[...].astype(F32), cw_ref[...]
            up, dn = _shift_up(dv, last), _shift_dn(dv, first)
            dcb_ref[idx] = _colsum(dv)
            dcw_ref[idx, 0:1, :] = _colsum(up * uu)
            dcw_ref[idx, 1:2, :] = _colsum(dv * uu)
            dcw_ref[idx, 2:3, :] = _colsum(dn * uu)
            du_ref[idx] = (cwv[0:1, :] * up + cwv[1:2, :] * dv + cwv[2:3, :] * dn).astype(BF16)

    ca = lambda r: pl.BlockSpec((r, cbk), lambda j: (0, j))
    cbs = lambda r: pl.BlockSpec((r, cbk), lambda j: (0, nf + j))
    o3 = lambda r: pl.BlockSpec((2, r, cbk), lambda j: (0, 0, j))
    return _pcall(
        body,
        name="conv_gate_bwd",
        out_shape=[jax.ShapeDtypeStruct((2, t, f), BF16), jax.ShapeDtypeStruct((2, 3, f), F32),
                   jax.ShapeDtypeStruct((2, 1, f), F32)],
        grid=(nf,),
        in_specs=[ca(t), cbs(t), o3(t), ca(3), cbs(3), ca(t)],
        out_specs=[o3(t), o3(3), o3(1)],
        compiler_params=_cparams(("parallel",)),
    )(u, u, uc, cw, cw, dh)


def _attention_fwd(q, kk, vv, *, hq, hkv, dk, dv, k_blk0, v_blk0, name):
    t = q.shape[0]
    tk = kk.shape[0]
    g_sz = hq // hkv
    tq = min(ATT_Q_BLOCK_FWD, t)

    def body(q_ref, k_ref, v_ref, o_ref, lse_ref):
        k = k_ref[...]
        v = v_ref[...]
        for j in range(g_sz):
            s = lax.dot_general(q_ref[:, j * dk:(j + 1) * dk], k, _DIMS["nt"], preferred_element_type=F32)
            m = jnp.max(s, axis=-1, keepdims=True)
            p = jnp.exp2(s - m)
            l = jnp.sum(p, axis=-1, keepdims=True)
            o = jnp.dot(p.astype(BF16), v, preferred_element_type=F32) / l
            o_ref[:, j * dv:(j + 1) * dv] = o.astype(BF16)
            lse_ref[0, :, j:j + 1] = m + jnp.log2(l)

    return _pcall(
        body,
        name=name,
        out_shape=[jax.ShapeDtypeStruct((t, hq * dv), BF16), jax.ShapeDtypeStruct((hkv, t, g_sz), F32)],
        grid=(hkv, t // tq),
        in_specs=[
            pl.BlockSpec((tq, g_sz * dk), lambda g, i: (i, g)),
            pl.BlockSpec((tk, dk), lambda g, i: (0, k_blk0 + g)),
            pl.BlockSpec((tk, dv), lambda g, i: (0, v_blk0 + g)),
        ],
        out_specs=[
            pl.BlockSpec((tq, g_sz * dv), lambda g, i: (i, g)),
            pl.BlockSpec((1, tq, g_sz), lambda g, i: (g, i, 0)),
        ],
        compiler_params=_cparams(("parallel", "parallel")),
    )(q, kk, vv)


def _attention_bwd(q, kk, vv, do, lse, *, hq, hkv, dk, dv, k_blk0, v_blk0, name):
    t = q.shape[0]
    tk = kk.shape[0]
    g_sz = hq // hkv
    tq = min(ATT_Q_BLOCK, t)

    def body(q_ref, k_ref, v_ref, do_ref, lse_ref, dq_ref, dk_ref, dv_ref):
        @pl.when(pl.program_id(1) == 0)
        def _():
            dk_ref[...] = jnp.zeros_like(dk_ref)
            dv_ref[...] = jnp.zeros_like(dv_ref)

        k = k_ref[...]
        v = v_ref[...]
        dk_acc = dv_acc = None
        for j in range(g_sz):
            qj = q_ref[:, j * dk:(j + 1) * dk]
            doj = do_ref[:, j * dv:(j + 1) * dv]
            s = lax.dot_general(qj, k, _DIMS["nt"], preferred_element_type=F32)
            p = jnp.exp2(s - lse_ref[0, :, j:j + 1])
            dp = lax.dot_general(doj, v, _DIMS["nt"], preferred_element_type=F32)
            ds = (p * (dp - jnp.sum(p * dp, axis=-1, keepdims=True))).astype(BF16)
            dv_j = lax.dot_general(p.astype(BF16), doj, _DIMS["tn"], preferred_element_type=F32)
            dk_j = lax.dot_general(ds, qj, _DIMS["tn"], preferred_element_type=F32)
            dv_acc = dv_j if dv_acc is None else dv_acc + dv_j
            dk_acc = dk_j if dk_acc is None else dk_acc + dk_j
            dq_ref[:, j * dk:(j + 1) * dk] = jnp.dot(ds, k, preferred_element_type=F32)
        dv_ref[...] += dv_acc
        dk_ref[...] += dk_acc

        @pl.when(pl.program_id(1) == t // tq - 1)
        def _():
            dk_ref[...] *= LN2

    return _pcall(
        body,
        name=name,
        out_shape=[jax.ShapeDtypeStruct((t, hq * dk), F32), jax.ShapeDtypeStruct((tk, hkv * dk), F32),
                   jax.ShapeDtypeStruct((tk, hkv * dv), F32)],
        grid=(hkv, t // tq),
        in_specs=[
            pl.BlockSpec((tq, g_sz * dk), lambda g, i: (i, g)),
            pl.BlockSpec((tk, dk), lambda g, i: (0, k_blk0 + g)),
            pl.BlockSpec((tk, dv), lambda g, i: (0, v_blk0 + g)),
            pl.BlockSpec((tq, g_sz * dv), lambda g, i: (i, g)),
            pl.BlockSpec((1, tq, g_sz), lambda g, i: (g, i, 0)),
        ],
        out_specs=[
            pl.BlockSpec((tq, g_sz * dk), lambda g, i: (i, g)),
            pl.BlockSpec((tk, dk), lambda g, i: (0, g)),
            pl.BlockSpec((tk, dv), lambda g, i: (0, g)),
        ],
        compiler_params=_cparams(("parallel", "arbitrary")),
    )(q, kk, vv, do, lse)


def _silu(v):
    return v * jax.nn.sigmoid(v)


def _ada_fwd(conds, w_ada, b_ada_shard):
    r, d = conds.shape
    n = w_ada.shape[1]
    tn = _tile(n, 512)

    def body(c_ref, w_ref, b_ref, o_ref):
        s = _silu(c_ref[...]).astype(BF16)
        o_ref[...] = jnp.dot(s, w_ref[...].astype(BF16), preferred_element_type=F32) + b_ref[...]

    return _pcall(
        body,
        name="ada_fwd",
        out_shape=jax.ShapeDtypeStruct((r, n), F32),
        grid=(n // tn,),
        in_specs=[pl.BlockSpec((r, d), lambda j: (0, 0)), pl.BlockSpec((d, tn), lambda j: (0, j)),
                  pl.BlockSpec((1, tn), lambda j: (0, j))],
        out_specs=pl.BlockSpec((r, tn), lambda j: (0, j)),
        compiler_params=_cparams(("parallel",)),
    )(conds, w_ada, b_ada_shard)


def _cctx_partial(da16_shard, w_ada, c_ctx_row):
    d, n = w_ada.shape
    td = _tile(d, 512)

    def body(g_ref, w_ref, c_ref, o_ref):
        ds = lax.dot_general(g_ref[8:16, :].astype(BF16), w_ref[...].astype(BF16), _DIMS["nt"],
                             preferred_element_type=F32)
        cv = c_ref[...]
        sg = jax.nn.sigmoid(cv)
        o_ref[...] = ds * (sg * (1.0 + cv * (1.0 - sg)))

    return _pcall(
        body,
        name="cctx_partial",
        out_shape=jax.ShapeDtypeStruct((8, d), F32),
        grid=(d // td,),
        in_specs=[pl.BlockSpec((16, n), lambda j: (0, 0)), pl.BlockSpec((td, n), lambda j: (j, 0)),
                  pl.BlockSpec((1, td), lambda j: (0, j))],
        out_specs=pl.BlockSpec((8, td), lambda j: (0, j)),
        compiler_params=_cparams(("parallel",)),
    )(da16_shard, w_ada, c_ctx_row)


def _sum_parts(parts):
    p, _, n = parts.shape

    def body(p_ref, o_ref):
        acc = p_ref[0]
        for s in range(1, p):
            acc = acc + p_ref[s]
        o_ref[...] = acc

    return _pcall(
        body,
        name="sum_parts",
        out_shape=jax.ShapeDtypeStruct((1, n), F32),
        in_specs=[pl.BlockSpec(memory_space=pltpu.VMEM)],
        out_specs=pl.BlockSpec(memory_space=pltpu.VMEM),
    )(parts)


def _adam_math(w, g, m, v):
    m2 = ADAM_B1 * m + (1.0 - ADAM_B1) * g
    v2 = ADAM_B2 * v + (1.0 - ADAM_B2) * jnp.square(g)
    m_hat = m2 / (1.0 - ADAM_B1 ** ADAM_STEP)
    v_hat = v2 / (1.0 - ADAM_B2 ** ADAM_STEP)
    delta = -ADAM_LR * (m_hat / (jnp.sqrt(v_hat) + ADAM_EPS) + ADAM_WD * w)
    return delta, m2, v2


def _adamw(parts, w, m, v, name):
    p, r, c = parts.shape
    block_elems = 1 << 18
    rb, cb = _tile(r, max(8, block_elems // c // 8 * 8), 8), c
    if rb * c < block_elems // 4 and r * c > block_elems:
        rb, cb = r, _tile(c, max(LANE, block_elems // r // LANE * LANE))

    def body(p_ref, w_ref, m_ref, v_ref, g_ref, d_ref, m2_ref, v2_ref):
        g = p_ref[0].astype(F32)
        for s in range(1, p):
            g = g + p_ref[s].astype(F32)
        g_ref[...] = g
        d_ref[...], m2_ref[...], v2_ref[...] = _adam_math(w_ref[...], g, m_ref[...], v_ref[...])

    if w.ndim == 3:
        blk = pl.BlockSpec((None, rb, cb), lambda i, j: (0, i, j))
    else:
        blk = pl.BlockSpec((rb, cb), lambda i, j: (i, j))
    return _pcall(
        body,
        name=name,
        out_shape=[jax.ShapeDtypeStruct(w.shape, F32)] * 4,
        grid=(r // rb, c // cb),
        in_specs=[pl.BlockSpec((p, rb, cb), lambda i, j: (0, i, j)), blk, blk, blk],
        out_specs=[blk] * 4,
        compiler_params=_cparams(("parallel", "parallel")),
    )(parts, w, m, v)


def _adamw_ada(conds, da16, w, m, v):
    d, n = w.shape
    rb = _tile(d, 256, LANE)

    def body(s_ref, da_ref, w_ref, m_ref, v_ref, g_ref, d_ref, m2_ref, v2_ref):
        g = lax.dot_general(_silu(s_ref[...]).astype(BF16), da_ref[...].astype(BF16), _DIMS["tn"],
                            preferred_element_type=F32)
        g_ref[...] = g
        d_ref[...], m2_ref[...], v2_ref[...] = _adam_math(w_ref[...], g, m_ref[...], v_ref[...])

    row = pl.BlockSpec((rb, n), lambda i: (i, 0))
    return _pcall(
        body,
        name="adamw_w_ada",
        out_shape=[jax.ShapeDtypeStruct((d, n), F32)] * 4,
        grid=(d // rb,),
        in_specs=[pl.BlockSpec((16, rb), lambda i: (0, i)), pl.BlockSpec((16, n), lambda i: (0, 0)), row, row, row],
        out_specs=[row] * 4,
        compiler_params=_cparams(("parallel",)),
    )(conds, da16, w, m, v)


def _cast_bf16(a, name):
    _, r, c = a.shape
    rb, cb = _tile(r, 512, 8), c
    if rb < 64 < r:
        rb, cb = r, _tile(c, 512)

    def body(a_ref, o_ref):
        o_ref[...] = a_ref[...].astype(BF16)

    return _pcall(body, name=name, out_shape=jax.ShapeDtypeStruct((r, c), BF16), grid=(r // rb, c // cb),
                  in_specs=[pl.BlockSpec((None, rb, cb), lambda i, j: (0, i, j))],
                  out_specs=pl.BlockSpec((rb, cb), lambda i, j: (i, j)),
                  compiler_params=_cparams(("parallel", "parallel")))(a)


def _rope_tabs(t, rot):
    half, q = rot // 2, rot // 4
    n_rows = t // GRID_W
    row = jnp.repeat(jnp.arange(n_rows, dtype=F32), GRID_W)
    col = jnp.tile(jnp.arange(GRID_W, dtype=F32), n_rows)
    inv_freq = ROPE_THETA ** (-jnp.arange(0, half, 2, dtype=F32) / half)
    ang = jnp.concatenate([row[:, None] * inv_freq, col[:, None] * inv_freq], axis=-1)
    cos, sin = jnp.cos(ang), jnp.sin(ang)
    c0, c1, s0, s1 = cos[:, :q], cos[:, q:], sin[:, :q], sin[:, q:]
    z = jnp.zeros_like(s0)
    return (jnp.concatenate([c0, c0, c1, c1], -1), jnp.concatenate([-s0, z, -s1, z], -1),
            jnp.concatenate([z, s0, z, s1], -1))


def _pad_cols(a, left, total, fill=0.0):
    return jnp.pad(a, ((0, 0), (left, total - left - a.shape[1])), constant_values=fill)


def _with_ctx_rows(tab, tc, fill):
    return jnp.concatenate([tab, jnp.full((tc, tab.shape[1]), fill, F32)], axis=0)


def kernel(x, c, ctx, c_ctx, w_ada, b_ada, norm1_g, w_in, mla_q_norm_g, w_q_up, mla_kv_norm_g, w_kv_up, gqa_q_norm_g, gqa_k_norm_g, w_br_a, w_br_b, w_out, norm2_g, w_up, conv_w, conv_b, w_down, final_norm_g, loss_target, m_c_ctx, m_w_ada, m_b_ada, m_norm1_g, m_w_in, m_mla_q_norm_g, m_w_q_up, m_mla_kv_norm_g, m_w_kv_up, m_gqa_q_norm_g, m_gqa_k_norm_g, m_w_br_a, m_w_br_b, m_w_out, m_norm2_g, m_w_up, m_conv_w, m_conv_b, m_w_down, m_final_norm_g, v_c_ctx, v_w_ada, v_b_ada, v_norm1_g, v_w_in, v_mla_q_norm_g, v_w_q_up, v_mla_kv_norm_g, v_w_kv_up, v_gqa_q_norm_g, v_gqa_k_norm_g, v_w_br_a, v_w_br_b, v_w_out, v_norm2_g, v_w_up, v_conv_w, v_conv_b, v_w_down, v_final_norm_g):
    weights = dict(c_ctx=c_ctx, w_ada=w_ada, b_ada=b_ada, norm1_g=norm1_g, w_in=w_in, mla_q_norm_g=mla_q_norm_g,
                   w_q_up=w_q_up, mla_kv_norm_g=mla_kv_norm_g, w_kv_up=w_kv_up, gqa_q_norm_g=gqa_q_norm_g,
                   gqa_k_norm_g=gqa_k_norm_g, w_br_a=w_br_a, w_br_b=w_br_b, w_out=w_out, norm2_g=norm2_g, w_up=w_up,
                   conv_w=conv_w, conv_b=conv_b, w_down=w_down, final_norm_g=final_norm_g)
    mom_m = dict(c_ctx=m_c_ctx, w_ada=m_w_ada, b_ada=m_b_ada, norm1_g=m_norm1_g, w_in=m_w_in, mla_q_norm_g=m_mla_q_norm_g,
                 w_q_up=m_w_q_up, mla_kv_norm_g=m_mla_kv_norm_g, w_kv_up=m_w_kv_up, gqa_q_norm_g=m_gqa_q_norm_g,
                 gqa_k_norm_g=m_gqa_k_norm_g, w_br_a=m_w_br_a, w_br_b=m_w_br_b, w_out=m_w_out, norm2_g=m_norm2_g,
                 w_up=m_w_up, conv_w=m_conv_w, conv_b=m_conv_b, w_down=m_w_down, final_norm_g=m_final_norm_g)
    mom_v = dict(c_ctx=v_c_ctx, w_ada=v_w_ada, b_ada=v_b_ada, norm1_g=v_norm1_g, w_in=v_w_in, mla_q_norm_g=v_mla_q_norm_g,
                 w_q_up=v_w_q_up, mla_kv_norm_g=v_mla_kv_norm_g, w_kv_up=v_w_kv_up, gqa_q_norm_g=v_gqa_q_norm_g,
                 gqa_k_norm_g=v_gqa_k_norm_g, w_br_a=v_w_br_a, w_br_b=v_w_br_b, w_out=v_w_out, norm2_g=v_norm2_g,
                 w_up=v_w_up, conv_w=v_conv_w, conv_b=v_conv_b, w_down=v_w_down, final_norm_g=v_final_norm_g)
    order = list(weights)

    my_idx = 4 * lax.axis_index("x") + 2 * lax.axis_index("y") + lax.axis_index("c")
    xs, cts, tgt = x[0], ctx[0], loss_target[0]
    t, d = xs.shape
    tc = cts.shape[0]
    ta = t + tc
    kvl, ql = MLA_KV_LORA, MLA_Q_LORA
    nb = GQA_KV_HEADS * GQA_HEAD_DIM
    hb = GQA_HEADS * GQA_HEAD_DIM
    ha = MLA_HEADS
    f2 = w_up.shape[2] * N_DEV
    ff = f2 // 2

    big = ["w_in", "w_q_up", "w_kv_up", "w_br_a", "w_br_b", "w_out", "w_up", "w_down"]
    _ORDER_AFTER.clear()
    narrow = ("w_in", "w_q_up")

    def tview(a):
        return jnp.transpose(a, (0, 2, 1))

    shards = {"w_in": _cast_bf16(tview(weights["w_in"]), "cast_w_in")}
    c_idx = jnp.reshape(lax.axis_index("c"), (1,)).astype(jnp.int32)

    def gather_start(names, dep):
        shs = [shards[n] for n in names]
        land = [lax.empty((N_DEV,) + s.shape, BF16) for s in shs]
        if dep is not None:
            _after(dep)
        s, r, arrs, tok = _split_start("gather_ici_start_" + names[0], shs + land, _gather_ici_copies(len(names)),
                                       4 * len(names))
        return dict(names=names, s=s, r=r, arrs=arrs, tok=tok)

    def gather_pass(g, after):
        n = len(g["names"])
        arrs = _split_wait("gather_ici_wait_" + g["names"][0], g["s"], g["r"], g["arrs"], _gather_ici_copies(n), after)
        s, r, bufs, tok = _split_start("gather_pass_start_" + g["names"][0], arrs[n:], _gather_pass_copies(n), 3 * n)
        g.update(s2=s, r2=r, bufs=bufs)
        return tok

    def gather_relay(g, after):
        n = len(g["names"])
        bufs = _split_wait("gather_pass_wait_" + g["names"][0], g["s2"], g["r2"], g["bufs"], _gather_pass_copies(n), after)
        s, r, bufs, tok = _split_start("gather_d2d_start_" + g["names"][0], bufs, _gather_d2d_copies(n), n)
        g.update(s3=s, r3=r, bufs=bufs)
        return tok

    def gather_finish(g, after):
        n = len(g["names"])
        bufs = _split_wait("gather_d2d_wait_" + g["names"][0], g["s3"], g["r3"], g["bufs"], _gather_d2d_copies(n), after)
        return dict(zip(g["names"], bufs))

    _after(shards["w_in"])
    c_all, cw_all = _all_gather([jnp.pad(c, ((0, 7), (0, 0))), jnp.pad(conv_w[0], ((0, 5), (0, 0)))], "gather_cond")
    conv_w_f = jnp.transpose(cw_all[:, :3, :], (1, 0, 2)).reshape(3, f2)
    conds = jnp.concatenate([c_all[:, 0, :], c_ctx[None, :], jnp.zeros((7, d), F32)], axis=0)
    ncol = w_ada.shape[2]
    b_shard = lax.dynamic_slice_in_dim(b_ada, my_idx * ncol, ncol, axis=1)
    ada_shard = _ada_fwd(conds, w_ada[0], b_shard)
    (ada_all,) = _all_gather([ada_shard], "gather_ada")
    ada = jnp.transpose(ada_all, (1, 0, 2)).reshape(16, N_DEV * ncol)
    lat = lax.dynamic_slice_in_dim(ada, my_idx, 1, axis=0).reshape(6, d)
    cxt = ada[8].reshape(6, d)
    mods1 = jnp.concatenate([lat[0:2], cxt[0:2], jnp.zeros((4, d), F32)], axis=0)
    mods2 = jnp.concatenate([lat[2:3], lat[3:4], lat[4:5], jnp.zeros((5, d), F32)], axis=0)
    mods2b = jnp.concatenate([lat[2:3], lat[4:5], jnp.zeros((6, d), F32)], axis=0)
    mods3 = jnp.concatenate([lat[5:6], jnp.zeros((7, d), F32)], axis=0)

    g0 = gather_start(["w_in"], ada_all)
    for n in big[1:]:
        _after(g0["tok"])
        shards[n] = _cast_bf16(tview(weights[n]) if n in narrow else weights[n], "cast_" + n)

    ca, s1a, s2a = _rope_tabs(t, MLA_ROPE)
    cb_, s1b, s2b = _rope_tabs(t, GQA_HEAD_DIM)
    q_tabs_a = (_pad_cols(jnp.concatenate([jnp.ones((t, MLA_NOPE), F32), ca], 1), 0, MLA_SLOT),
                _pad_cols(s1a, MLA_NOPE, MLA_SLOT), _pad_cols(s2a, MLA_NOPE, MLA_SLOT))
    q_tabs_b = (cb_, s1b, s2b)
    k_tabs = (_with_ctx_rows(_pad_cols(ca, 0, LANE), tc, 1.0), _with_ctx_rows(_pad_cols(s1a, 0, LANE), tc, 0.0),
              _with_ctx_rows(_pad_cols(s2a, 0, LANE), tc, 0.0),
              _with_ctx_rows(cb_, tc, 1.0), _with_ctx_rows(s1b, tc, 0.0), _with_ctx_rows(s2b, tc, 0.0))

    def cols_full(g):
        return jnp.transpose(g, (1, 0, 2)).reshape(g.shape[1], N_DEV * g.shape[2])

    _after(*q_tabs_a, *q_tabs_b, *k_tabs, *[shards[n] for n in big[1:]])
    tok_p0 = gather_pass(g0, mods1)
    g1 = gather_start(["w_q_up", "w_kv_up", "w_br_a", "w_br_b", "w_out"], tok_p0)
    _after(g1["tok"])
    z_all = _norm_mod_fwd(cts, xs, norm1_g, mods1)
    gathered = gather_finish(g0, gather_relay(g0, z_all))
    wt_in = gathered["w_in"].reshape(-1, d)
    o_kpe, o_kb, o_vb = kvl, kvl + MLA_ROPE, kvl + MLA_ROPE + nb
    o_q = o_vb + nb
    o_g = o_q + ql + hb
    wkv_w = kvl + 2 * nb + LANE
    wt_kv_p = jnp.concatenate([wt_in[:kvl], wt_in[o_kb:o_q], wt_in[o_kpe:o_kb],
                               jnp.zeros((LANE - MLA_ROPE, d), BF16)], axis=0)
    q_w = ql + hb
    q_pad = (-q_w) % 512 if d >= 512 else (-q_w) % d
    gate_blk = (q_w + q_pad) // d
    assert (q_w + q_pad) % d == 0
    wt_qg_p = jnp.concatenate([wt_in[o_q:o_g], jnp.zeros((q_pad, d), BF16), wt_in[o_g:]], axis=0)

    kv_all = _mm(z_all, wt_kv_p, "nt", F32, "proj_kv", tm=1152, tn=wkv_w)
    qg = _mm(z_all, wt_qg_p, "nt", F32, "proj_qg", tm=1024, tn=1024, rows=t)
    tok_p1 = gather_pass(g1, qg)
    g2 = gather_start(["w_up", "w_down"], tok_p1)
    _after(g2["tok"])
    kin, k_b, v_b = _key_prep_fwd(kv_all, mla_kv_norm_g, gqa_k_norm_g, k_tabs)
    sc_a = float((MLA_NOPE + MLA_ROPE) ** -0.5) * LOG2E
    sc_b = float(GQA_HEAD_DIM ** -0.5) * LOG2E
    _after(g2["tok"])
    cqn, q_b = _q_prep_fwd(qg, mla_q_norm_g, gqa_q_norm_g, q_tabs_b, sc_b)
    _after(kin, g2["tok"])
    gathered.update(gather_finish(g1, gather_relay(g1, q_b)))

    wqt_f = gathered["w_q_up"].reshape(ha, MLA_NOPE + MLA_ROPE, ql)
    wqt_ext = jnp.pad(wqt_f, ((0, 0), (0, MLA_SLOT - MLA_NOPE - MLA_ROPE), (0, 0))).reshape(ha * MLA_SLOT, ql)
    wkv_f = cols_full(gathered["w_kv_up"]).reshape(kvl, ha, MLA_NOPE + MLA_V)
    wk_slots = jnp.pad(wkv_f[:, :, :MLA_NOPE], ((0, 0), (0, 0), (0, MLA_SLOT - MLA_NOPE))).reshape(kvl, ha * MLA_SLOT)
    wv_cols = wkv_f[:, :, MLA_NOPE:].reshape(kvl, ha * MLA_V)
    e_slot = jnp.pad(jnp.eye(MLA_ROPE, dtype=BF16),
                     ((0, LANE - MLA_ROPE), (MLA_NOPE, MLA_SLOT - MLA_NOPE - MLA_ROPE)))
    e_rows = jnp.concatenate([jnp.tile(e_slot, (1, ha)), jnp.zeros((LANE, ha * MLA_V), BF16)], axis=1)
    wkv_ext = jnp.concatenate([jnp.concatenate([wk_slots, wv_cols], axis=1), e_rows], axis=0)
    w_bra = cols_full(gathered["w_br_a"])
    w_brb = cols_full(gathered["w_br_b"])
    w_out_f = gathered["w_out"].reshape(d, d)

    kv_a = _mm(kin, wkv_ext, "nn", BF16, "kv_up", tm=1152, tn=1024)
    qa_raw = _mm(cqn, wqt_ext, "nt", F32, "q_up", tm=1024, tn=1024)
    q_a = _rope_a(qa_raw, q_tabs_a, False, BF16, "rope_q_fwd", sc_a)
    att_a = dict(hq=ha, hkv=ha, dk=MLA_SLOT, dv=MLA_V, k_blk0=0, v_blk0=ha * MLA_SLOT // MLA_V)
    att_b = dict(hq=GQA_HEADS, hkv=GQA_KV_HEADS, dk=GQA_HEAD_DIM, dv=GQA_HEAD_DIM, k_blk0=0, v_blk0=0)
    o_a, lse_a = _attention_fwd(q_a, kv_a, kv_a, name="attn_a_fwd", **att_a)
    o_b, lse_b = _attention_fwd(q_b, k_b, v_b, name="attn_b_fwd", **att_b)
    _after(o_a)
    _after(gather_pass(g2, o_b))
    pa = _mm(o_a, w_bra, "nn", BF16, "br_a", tm=1024, tn=1024)
    pb = _mm(o_b, w_brb, "nn", BF16, "br_b", tm=1024, tn=1024)
    merged = _merge_fwd(pa, pb, qg, gate_blk)
    attn = _mm(merged, w_out_f, "nn", F32, "w_out", tm=1024, tn=1024)
    x1, z2 = _resid_norm_mod(xs, attn, norm2_g, mods2, "resid_norm2_fwd")
    ffn_w = gather_finish(g2, gather_relay(g2, z2))
    w_up3 = ffn_w["w_up"]
    w_down_f = ffn_w["w_down"].reshape(ff, d)
    u = _mm_up_fwd(z2, w_up3, "w_up")
    h, uc = _conv_fwd(u, conv_w_f, conv_b)
    ffn = _mm(h, w_down_f, "nn", F32, "w_down", tm=1024, tn=1024, tk=2816)

    def to_shards(g):
        return jnp.transpose(g.reshape(g.shape[0], N_DEV, g.shape[1] // N_DEV), (1, 0, 2))

    def reduce_start(tag, names, sends):
        n = len(sends)
        land = [lax.empty((4,) + s.shape[1:], s.dtype) for s in sends]
        s, r, arrs, tok = _split_start("reduce_d2d_start_" + tag, sends + land, _reduce_d2d_copies(n), 4 * n)
        return dict(tag=tag, names=names, s=s, r=r, arrs=arrs, tok=tok)

    def reduce_relay(g, after):
        n = len(g["names"])
        arrs = _split_wait("reduce_d2d_wait_" + g["tag"], g["s"], g["r"], g["arrs"], _reduce_d2d_copies(n), after)
        sums = [_pair_sum(arrs[a], arrs[n + a], c_idx, "pair_sum_" + g["names"][a]) for a in range(n)]
        land = [lax.empty(s.shape, s.dtype) for s in sums]
        s, r, arrs2, tok = _split_start("reduce_ici_start_" + g["tag"], sums + land, _reduce_ici_copies(n), 4 * n)
        g.update(s2=s, r2=r, arrs2=arrs2)
        return tok

    def reduce_finish(g, after):
        n = len(g["names"])
        arrs2 = _split_wait("reduce_ici_wait_" + g["tag"], g["s2"], g["r2"], g["arrs2"], _reduce_ici_copies(n), after)
        return dict(zip(g["names"], arrs2[n:]))

    dx2, dffn, st_fin = _final_loss(x1, ffn, final_norm_g[None, :], mods3, tgt)
    dh = _mm(dffn, w_down_f, "nt", BF16, "d_h", tm=1024, tn=1024)
    g_w_down = _mm(h, dffn, "tn", BF16, "g_w_down", tm=512, tn=1024)
    du3, dcw, dcb = _conv_bwd(u, uc, conv_w_f, dh)
    dz2 = _mm_up_dz(du3, w_up3, "d_z2")
    g_w_up = _mm_up_gw(z2, du3, N_DEV, "g_w_up")
    g_conv_w = jnp.concatenate([dcw[0], dcw[1]], axis=1)
    r_ffn = reduce_start("ffn", ["w_down", "w_up", "conv_w"],
                         [g_w_down.reshape(N_DEV, ff // N_DEV, d), g_w_up,
                          to_shards(jnp.pad(g_conv_w, ((0, 5), (0, 0))))])
    _after(r_ffn["tok"])
    dx1, dattn, st_n2 = _norm2_bwd(x1, attn, norm2_g, mods2b, dz2, dx2)
    dmerged = _mm(dattn, w_out_f, "nt", BF16, "d_merged", tm=1024, tn=1024)
    g_w_out = _mm(merged, dattn, "tn", BF16, "g_w_out", tm=1024, tn=1024)
    dpa, dpb, dgates = _merge_bwd(dmerged, pa, pb, qg, gate_blk)
    do_a = _mm(dpa, w_bra, "nt", BF16, "d_o_a", tm=1024, tn=1024)
    do_b = _mm(dpb, w_brb, "nt", BF16, "d_o_b", tm=1024, tn=1024)
    g_w_bra = _mm(o_a, dpa, "tn", BF16, "g_w_br_a", tm=1024, tn=1024)
    g_w_brb = _mm(o_b, dpb, "tn", BF16, "g_w_br_b", tm=1024, tn=1024)
    _after(reduce_relay(r_ffn, g_w_brb))
    dq_a, dk_a, dv_a = _attention_bwd(q_a, kv_a, kv_a, do_a, lse_a, name="attn_a_bwd", **att_a)
    dq_b, dk_b, dv_b = _attention_bwd(q_b, k_b, v_b, do_b, lse_b, name="attn_b_bwd", **att_b)
    dqa_raw = _rope_a(dq_a, q_tabs_a, True, BF16, "rope_q_bwd", sc_a * LN2)
    dcqn = _mm(dqa_raw, wqt_ext, "nn", F32, "d_cqn", tm=1024, tn=ql)
    g_wqt_ext = _mm(dqa_raw, cqn, "tn", BF16, "g_w_q_up", tm=1024, tn=ql)
    dq_p, st_q, st_qb = _q_prep_bwd(qg, mla_q_norm_g, gqa_q_norm_g, q_tabs_b, dcqn, dq_b, q_pad, sc_b * LN2)
    dkin = _mm_cat_nt([(dk_a, wkv_ext, 0), (dv_a, wkv_ext, ha * MLA_SLOT)], F32, "d_kin", tm=1152, tn=kvl + LANE)
    g_wkv_ext = _mm_cat_tn(kin, [dk_a, dv_a], BF16, "g_w_kv_up", tm=kvl + LANE, tn=min(1024, ha * MLA_V))
    dkv_p, st_kv, st_kb = _key_prep_bwd(kv_all, mla_kv_norm_g, gqa_k_norm_g, k_tabs, dkin, dk_b, dv_b)
    g_wqt = g_wqt_ext.reshape(ha, MLA_SLOT, ql)[:, :MLA_NOPE + MLA_ROPE, :].reshape(N_DEV, -1, ql)
    g_wkv = jnp.concatenate([g_wkv_ext[:kvl, :ha * MLA_SLOT].reshape(kvl, ha, MLA_SLOT)[:, :, :MLA_NOPE],
                             g_wkv_ext[:kvl, ha * MLA_SLOT:].reshape(kvl, ha, MLA_V)], axis=2).reshape(kvl, ha * (MLA_NOPE + MLA_V))
    r_mid = reduce_start("mid", ["w_out", "w_br_a", "w_br_b", "w_q_up", "w_kv_up"],
                         [g_w_out.reshape(N_DEV, d // N_DEV, d), to_shards(g_w_bra), to_shards(g_w_brb), g_wqt,
                          to_shards(g_wkv)])
    _after(r_mid["tok"])
    g_wkv_p = _mm(dkv_p, z_all, "tn", BF16, "g_w_in_kv", tm=wkv_w, tn=1024)
    g_wqg_p = _mm_rows_tn([dq_p, dgates], z_all, BF16, "g_w_in_qg", tm=min(1024, d), tn=1024, rows=t)
    tok_m = reduce_relay(r_mid, g_wqg_p)
    g_wt_in = jnp.concatenate([g_wkv_p[:kvl], g_wkv_p[kvl + 2 * nb:kvl + 2 * nb + MLA_ROPE],
                               g_wkv_p[kvl:kvl + 2 * nb], g_wqg_p[:q_w], g_wqg_p[q_w + q_pad:]], axis=0)
    _after(tok_m)
    r_in = reduce_start("in", ["w_in"], [g_wt_in.reshape(N_DEV, -1, d)])
    _after(r_in["tok"])
    qw_p = q_w + q_pad
    dz_lat = _mm_sum_nn([(dq_p, 0, wt_qg_p, 0, qw_p), (dgates, 0, wt_qg_p, qw_p, d), (dgates, d, wt_qg_p, qw_p + d, d),
                         (dkv_p, 0, wt_kv_p, 0, wkv_w)], F32, "d_z_lat", rows=t)
    dz_ctx = _mm(dkv_p, wt_kv_p, "nn", F32, "d_z_ctx", tm=min(ROW_BLOCK, tc), tn=1024, a_row_off=t)
    grad_x, st_n1 = _norm1_bwd(cts, xs, norm1_g, mods1, dz_ctx, dz_lat, dx1)

    res = {}

    def upd(nm, parts):
        wv, mv, vv = weights[nm], mom_m[nm], mom_v[nm]
        if wv.ndim == 1:
            wv, mv, vv = (a.reshape(1, -1) for a in (wv, mv, vv))
        if nm in narrow:
            wv, mv, vv = tview(wv), tview(mv), tview(vv)
        outs = _adamw(parts, wv, mv, vv, "adamw_" + nm)
        if nm in narrow:
            outs = [tview(o_) for o_ in outs]
        res[nm] = [o_.reshape(weights[nm].shape) for o_ in outs]

    d_lat = jnp.concatenate([st_n1[0], st_n1[1], st_n2[3], st_n2[0], st_n2[1], st_fin[1]])
    d_cxt = jnp.concatenate([st_n1[3], st_n1[4], jnp.zeros((4 * d,), F32)])
    small = jnp.concatenate([d_lat, d_cxt, st_n1[2], st_q[0], st_kv[0], st_qb[0], st_kb[0], st_n2[2],
                             jnp.concatenate([dcb[0, 0], dcb[1, 0]]), st_fin[0], st_fin[3, :LANE]])
    n_small = small.shape[0]
    pad_small = (-n_small) % LANE
    (small_all,) = _all_gather([jnp.pad(small, (0, pad_small)).reshape(1, -1)], "gather_small")
    offs = {}
    o = 0
    for nm, ln in (("d_lat", 6 * d), ("d_cxt", 6 * d), ("norm1_g", d), ("mla_q_norm_g", ql), ("mla_kv_norm_g", kvl),
                   ("gqa_q_norm_g", GQA_HEAD_DIM), ("gqa_k_norm_g", GQA_HEAD_DIM), ("norm2_g", d), ("conv_b", f2),
                   ("final_norm_g", d), ("loss", LANE)):
        offs[nm] = (o, ln)
        o += ln

    def part(nm):
        a, ln = offs[nm]
        return small_all[:, :, a:a + ln]

    loss = _sum_parts(part("loss"))[0, 0]
    d_lat_all = part("d_lat")[:, 0, :]
    d_cxt_sum = _sum_parts(part("d_cxt"))
    da16 = jnp.concatenate([d_lat_all, d_cxt_sum, jnp.zeros((7, 6 * d), F32)], axis=0)
    da16_shard = lax.dynamic_slice_in_dim(da16, my_idx * ncol, ncol, axis=1)
    cc_part = _cctx_partial(da16_shard, w_ada[0], c_ctx[None, :])
    (cc_all,) = _all_gather([cc_part], "gather_cctx")
    cc_parts = cc_all[:, 0:1, :]
    tok_i = reduce_relay(r_in, cc_all)

    _after(tok_i)
    for nm in ("norm1_g", "mla_q_norm_g", "mla_kv_norm_g", "gqa_q_norm_g", "gqa_k_norm_g", "norm2_g", "conv_b",
               "final_norm_g"):
        upd(nm, part(nm))
    upd("c_ctx", cc_parts)
    b_parts = jnp.concatenate([d_lat_all[:, None, :], d_cxt_sum[None]], axis=0)
    upd("b_ada", b_parts)
    _after(tok_i)
    outs = _adamw_ada(conds, da16_shard, w_ada[0], m_w_ada[0], v_w_ada[0])
    res["w_ada"] = [o_[None] for o_ in outs]
    last = outs[0]
    done = [last]
    for grp in (r_ffn, r_mid, r_in):
        _after(*done)
        recv = reduce_finish(grp, last)
        for nm in grp["names"]:
            upd(nm, recv[nm][:, :3, :] if nm == "conv_w" else recv[nm])
            last = res[nm][0]
            done.append(last)

    return (loss, grad_x[None], *[res[n][0] for n in order], *[res[n][1] for n in order],
            *[res[n][2] for n in order], *[res[n][3] for n in order])
```

```python
import jax
import jax.numpy as jnp
from jax import lax
from jax.experimental import pallas as pl
from jax.experimental.pallas import tpu as pltpu

F32 = jnp.float32
BF16 = jnp.bfloat16

GRID_W = 64
ROPE_THETA = 10000.0
NORM_EPS = 1e-6
MLA_HEADS = 8
MLA_Q_LORA = 768
MLA_KV_LORA = 512
MLA_NOPE = 128
MLA_ROPE = 64
MLA_V = 128
GQA_HEADS = 8
GQA_KV_HEADS = 2
GQA_HEAD_DIM = 128
ADAM_LR = 0.001
ADAM_B1 = 0.9
ADAM_B2 = 0.999
ADAM_EPS = 1e-08
ADAM_WD = 0.01
ADAM_STEP = 10

N_DEV = 8
LANE = 128
MLA_SLOT = 2 * LANE
VMEM_LIMIT = 56 * 1024 * 1024
ROW_BLOCK = 256
ATT_Q_BLOCK = 512
ATT_Q_BLOCK_FWD = 1024
LN2 = 0.6931471805599453
LOG2E = 1.4426950408889634
MESH_ID = pl.DeviceIdType.MESH


def _tile(n, pref, align=LANE):
    if n <= pref:
        return n
    best = None
    t = align
    while t <= pref:
        if n % t == 0:
            best = t
        t += align
    assert best is not None, (n, pref, align)
    return best


def _cparams(sem=None):
    return pltpu.CompilerParams(dimension_semantics=sem, vmem_limit_bytes=VMEM_LIMIT)


_ORDER_AFTER = []


def _after(*arrays):
    _ORDER_AFTER.extend(arrays)


def _pcall(body, *, in_specs, **kw):
    deps = tuple(_ORDER_AFTER)
    _ORDER_AFTER.clear()
    if not deps:
        return pl.pallas_call(body, in_specs=in_specs, **kw)
    n_in, n_dep = len(in_specs), len(deps)

    def with_deps(*refs):
        body(*refs[:n_in], *refs[n_in + n_dep:])

    call = pl.pallas_call(with_deps, in_specs=list(in_specs) + [pl.BlockSpec(memory_space=pl.ANY)] * n_dep, **kw)
    return lambda *args: call(*args, *deps)


def _all_gather(arrs, name):
    n = len(arrs)

    def body(*refs):
        ins = refs[:n]
        outs = refs[n:2 * n]
        send_sems, recv_sems, local_sems = refs[2 * n:]
        x, y, c = lax.axis_index("x"), lax.axis_index("y"), lax.axis_index("c")
        me, sibling = (x, y, c), (x, y, 1 - c)
        chips = [(1 - x, y), (x, 1 - y), (1 - x, 1 - y)]

        def rows(a, dev):
            px, py, pc = dev
            return outs[a].at[4 * px + 2 * py + pc]

        def copy(a, k, block, to, src=None):
            return pltpu.make_async_remote_copy(
                src_ref=rows(a, block) if src is None else src,
                dst_ref=rows(a, block),
                send_sem=send_sems.at[7 * a + k],
                recv_sem=recv_sems.at[7 * a + k],
                device_id=to,
                device_id_type=MESH_ID,
            )

        mine = [pltpu.make_async_copy(ins[a], rows(a, me), local_sems.at[a]) for a in range(n)]
        for cp in mine:
            cp.start()
        first = []
        for a in range(n):
            first.append(copy(a, 0, me, sibling, src=ins[a]))
            first += [copy(a, 1 + j, me, (*chip, c), src=ins[a]) for j, chip in enumerate(chips)]
        for cp in first:
            cp.start()
        passed = []
        for j, chip in enumerate(chips):
            for a in range(n):
                copy(a, 1 + j, (*chip, c), me).wait_recv()
                fwd = copy(a, 4 + j, (*chip, c), sibling)
                fwd.start()
                passed.append(fwd)
        for a in range(n):
            copy(a, 0, sibling, me).wait_recv()
            for j, chip in enumerate(chips):
                copy(a, 4 + j, (*chip, 1 - c), me).wait_recv()
        for cp in first + passed:
            cp.wait_send()
        for cp in mine:
            cp.wait()

    any_spec = pl.BlockSpec(memory_space=pl.ANY)
    outs = _pcall(
        body,
        name=name,
        out_shape=[jax.ShapeDtypeStruct((N_DEV,) + a.shape, a.dtype) for a in arrs],
        in_specs=[any_spec] * n,
        out_specs=[any_spec] * n,
        scratch_shapes=[
            pltpu.SemaphoreType.DMA((7 * n,)),
            pltpu.SemaphoreType.DMA((7 * n,)),
            pltpu.SemaphoreType.DMA((n,)),
        ],
    )(*arrs)
    return list(outs)


_HBM = pl.BlockSpec(memory_space=pltpu.HBM)
_SEM = pl.BlockSpec(memory_space=pltpu.SEMAPHORE)
_EFFECT = pltpu.SideEffectType.DATAFLOW_SIDE_EFFECTING


def _descriptors(copies, send_sems, recv_sems):
    descs = []
    for i, (src, dst, dev) in enumerate(copies):
        if dev is None:
            descs.append(pltpu.make_async_copy(src, dst, recv_sems.at[i]))
        else:
            descs.append(pltpu.make_async_remote_copy(src_ref=src, dst_ref=dst, send_sem=send_sems.at[i],
                                                      recv_sem=recv_sems.at[i], device_id=dev, device_id_type=MESH_ID))
    return descs


def _split_start(name, arrays, copies_fn, n_copies):
    n = len(arrays)

    def body(*refs):
        send_sems, recv_sems = refs[n], refs[n + 1]
        token = refs[2 * n + 2]
        for dsc in _descriptors(copies_fn(refs[:n]), send_sems, recv_sems):
            dsc.start()
        token[...] = jnp.zeros_like(token)

    outs = _pcall(
        body,
        name=name,
        out_shape=(pltpu.SemaphoreType.DMA((n_copies,)), pltpu.SemaphoreType.DMA((n_copies,)),
                   *[pltpu.HBM(a.shape, a.dtype) for a in arrays], jax.ShapeDtypeStruct((8, LANE), F32)),
        in_specs=[_HBM] * n,
        out_specs=(_SEM, _SEM, *[_HBM] * n, pl.BlockSpec(memory_space=pltpu.VMEM)),
        input_output_aliases={i: 2 + i for i in range(n)},
        compiler_params=pltpu.CompilerParams(has_side_effects=_EFFECT),
    )(*[pltpu.with_memory_space_constraint(a, pltpu.HBM) for a in arrays])
    return outs[0], outs[1], list(outs[2:2 + n]), outs[2 + n]


def _split_wait(name, send_sems, recv_sems, arrays, copies_fn, after):
    n = len(arrays)

    def body(*refs):
        for dsc, (_, _, dev) in zip(_descriptors(copies_fn(refs[:n]), refs[n], refs[n + 1]), copies_fn(refs[:n])):
            if dev is None:
                dsc.wait()
            else:
                dsc.wait_send()
                dsc.wait_recv()

    outs = _pcall(
        body,
        name=name,
        out_shape=tuple(pltpu.HBM(a.shape, a.dtype) for a in arrays),
        in_specs=[_HBM] * n + [_SEM, _SEM, pl.BlockSpec(memory_space=pl.ANY)],
        out_specs=tuple([_HBM] * n),
        input_output_aliases={i: i for i in range(n)},
        compiler_params=pltpu.CompilerParams(has_side_effects=_EFFECT),
    )(*arrays, send_sems, recv_sems, after)
    return list(outs)


def _mesh_pos():
    x, y, c = lax.axis_index("x"), lax.axis_index("y"), lax.axis_index("c")
    return x, y, c, [(1 - x, y), (x, 1 - y), (1 - x, 1 - y)]


def _gather_ici_copies(n):
    def copies(refs):
        x, y, c, chips = _mesh_pos()
        me = 4 * x + 2 * y + c
        out = []
        for a in range(n):
            src, buf = refs[a], refs[n + a]
            out.append((src, buf.at[me], None))
            out.append((src, buf.at[me], (x, y, 1 - c)))
            out += [(src, buf.at[me], (cx, cy, c)) for cx, cy in chips[:2]]
        return out
    return copies


def _gather_pass_copies(n):
    def copies(refs):
        x, y, c, chips = _mesh_pos()
        south = c == 0
        bx, by = jnp.where(south, 1 - x, x), jnp.where(south, y, 1 - y)
        tx, ty = jnp.where(south, x, 1 - x), jnp.where(south, 1 - y, y)
        out = []
        for a in range(n):
            rows = refs[a].at[4 * bx + 2 * by + c]
            out.append((rows, rows, (tx, ty, c)))
            for cx, cy in chips[:2]:
                rows = refs[a].at[4 * cx + 2 * cy + c]
                out.append((rows, rows, (x, y, 1 - c)))
        return out
    return copies


def _gather_d2d_copies(n):
    def copies(refs):
        x, y, c, chips = _mesh_pos()
        cx, cy = chips[2]
        out = []
        for a in range(n):
            rows = refs[a].at[4 * cx + 2 * cy + c]
            out.append((rows, rows, (x, y, 1 - c)))
        return out
    return copies


def _reduce_d2d_copies(n):
    def copies(refs):
        x, y, c, _ = _mesh_pos()
        out = []
        for a in range(n):
            for k in range(4):
                out.append((refs[a].at[2 * k + (1 - c)], refs[n + a].at[k], (x, y, 1 - c)))
        return out
    return copies


def _reduce_ici_copies(n):
    def copies(refs):
        x, y, c, chips = _mesh_pos()
        mine = 2 * x + y
        out = []
        for a in range(n):
            src, land = refs[a], refs[n + a]
            out.append((src.at[mine], land.at[mine], None))
            out += [(src.at[2 * cx + cy], land.at[mine], (cx, cy, c)) for cx, cy in chips]
        return out
    return copies


def _pair_sum(send, land, c_idx, name):
    _, r, cols = send.shape
    rb = _tile(r, max(8, (1 << 22) // (send.dtype.itemsize * cols) // 8 * 8), 8)
    dt = send.dtype

    def body(c_ref, s_ref, l_ref, o_ref):
        o_ref[...] = (s_ref[...].astype(F32) + l_ref[...].astype(F32)).astype(dt)

    return pl.pallas_call(
        body,
        name=name,
        out_shape=jax.ShapeDtypeStruct((4, r, cols), dt),
        grid_spec=pltpu.PrefetchScalarGridSpec(
            num_scalar_prefetch=1,
            grid=(4, r // rb),
            in_specs=[pl.BlockSpec((None, rb, cols), lambda k, i, c_ref: (2 * k + c_ref[0], i, 0)),
                      pl.BlockSpec((None, rb, cols), lambda k, i, c_ref: (k, i, 0))],
            out_specs=pl.BlockSpec((None, rb, cols), lambda k, i, c_ref: (k, i, 0)),
        ),
        compiler_params=_cparams(("parallel", "parallel")),
    )(c_idx, send, land)


_DIMS = {
    "nn": (((1,), (0,)), ((), ())),
    "nt": (((1,), (1,)), ((), ())),
    "tn": (((0,), (0,)), ((), ())),
}


def _mm_call(a, b, *, mode, grid, a_spec, b_spec, o_spec, out_shape, acc_shape, name):
    nk = grid[2]
    out_dtype = out_shape.dtype

    def body(a_ref, b_ref, o_ref, *scratch):
        p = lax.dot_general(a_ref[...].astype(BF16), b_ref[...].astype(BF16), _DIMS[mode],
                            preferred_element_type=F32)
        if nk == 1:
            o_ref[...] = p.astype(out_dtype)
        else:
            acc = scratch[0]
            k = pl.program_id(2)

            @pl.when(k == 0)
            def _():
                acc[...] = p

            @pl.when(k > 0)
            def _():
                acc[...] += p

            @pl.when(k == nk - 1)
            def _():
                o_ref[...] = acc[...].astype(out_dtype)

    return _pcall(
        body,
        name=name,
        out_shape=out_shape,
        grid=grid,
        in_specs=[a_spec, b_spec],
        out_specs=o_spec,
        scratch_shapes=[pltpu.VMEM(acc_shape, F32)] if nk > 1 else [],
        compiler_params=_cparams(("parallel", "parallel", "arbitrary")),
    )(a, b)


def _mm(a, b, mode, out_dtype, name, tm=512, tn=512, tk=2432, a_row_off=0, rows=None):
    if mode == "nn":
        (m, k), (k2, n) = a.shape, b.shape
    elif mode == "nt":
        (m, k), (n, k2) = a.shape, b.shape
    else:
        (k, m), (k2, n) = a.shape, b.shape
        if rows is not None:
            k = k2 = rows
    assert k == k2, (a.shape, b.shape, mode)
    if mode != "tn":
        m = (m if rows is None else rows + a_row_off) - a_row_off
    tm, tn, tk = _tile(m, tm, 8), _tile(n, tn), _tile(k, tk, 8 if mode == "tn" else LANE)
    assert a_row_off % tm == 0
    ro = a_row_off // tm
    grid = (m // tm, n // tn, k // tk)
    if mode == "tn":
        a_spec = pl.BlockSpec((tk, tm), lambda i, j, kk: (kk, i))
    else:
        a_spec = pl.BlockSpec((tm, tk), lambda i, j, kk: (i + ro, kk))
    if mode == "nt":
        b_spec = pl.BlockSpec((tn, tk), lambda i, j, kk: (j, kk))
    else:
        b_spec = pl.BlockSpec((tk, tn), lambda i, j, kk: (kk, j))
    o_spec = pl.BlockSpec((tm, tn), lambda i, j, kk: (i, j))
    return _mm_call(a, b, mode=mode, grid=grid, a_spec=a_spec, b_spec=b_spec, o_spec=o_spec,
                    out_shape=jax.ShapeDtypeStruct((m, n), out_dtype), acc_shape=(tm, tn), name=name)


def _mm_cat_nt(pieces, out_dtype, name, tm=1024, tn=1024, tk=2048, rows=None):
    m = pieces[0][0].shape[0] if rows is None else rows
    n = pieces[0][1].shape[0]
    tm, tn = _tile(m, tm, 8), _tile(n, tn)
    steps, starts, s = [], [], 0
    for a, b, off in pieces:
        kp = a.shape[1]
        tkp = _tile(kp, tk)
        assert off % tkp == 0 and b.shape[0] == n
        steps.append((tkp, kp // tkp, off // tkp))
        starts.append(s)
        s += kp // tkp
    nk = s
    npc = len(pieces)

    def body(*refs):
        o_ref, acc = refs[2 * npc], refs[2 * npc + 1]
        kk = pl.program_id(2)

        @pl.when(kk == 0)
        def _():
            acc[...] = jnp.zeros_like(acc)

        for p in range(npc):
            @pl.when((kk >= starts[p]) & (kk < starts[p] + steps[p][1]))
            def _(p=p):
                acc[...] += lax.dot_general(refs[2 * p][...].astype(BF16), refs[2 * p + 1][...].astype(BF16), _DIMS["nt"],
                                            preferred_element_type=F32)

        @pl.when(kk == nk - 1)
        def _():
            o_ref[...] = acc[...].astype(out_dtype)

    in_specs, args = [], []
    for p, (a, b, off) in enumerate(pieces):
        tkp, np_, ob = steps[p]

        def rel(kk, p=p, np_=np_):
            return jnp.clip(kk - starts[p], 0, np_ - 1)

        in_specs.append(pl.BlockSpec((tm, tkp), lambda i, j, kk, rel=rel: (i, rel(kk))))
        in_specs.append(pl.BlockSpec((tn, tkp), lambda i, j, kk, rel=rel, ob=ob: (j, ob + rel(kk))))
        args += [a, b]
    return _pcall(
        body,
        name=name,
        out_shape=jax.ShapeDtypeStruct((m, n), out_dtype),
        grid=(m // tm, n // tn, nk),
        in_specs=in_specs,
        out_specs=pl.BlockSpec((tm, tn), lambda i, j, kk: (i, j)),
        scratch_shapes=[pltpu.VMEM((tm, tn), F32)],
        compiler_params=_cparams(("parallel", "parallel", "arbitrary")),
    )(*args)


def _mm_cat_tn(a, pieces, out_dtype, name, tm=1024, tn=1024, rows=None):
    k = a.shape[0] if rows is None else rows
    m = a.shape[1]
    tm = _tile(m, tm)
    starts, s = [], 0
    for b in pieces:
        assert b.shape[1] % tn == 0
        starts.append(s)
        s += b.shape[1] // tn
    nj = s
    npc = len(pieces)

    def body(*refs):
        a_ref, o_ref = refs[0], refs[1 + npc]
        j = pl.program_id(1)
        for p in range(npc):
            @pl.when((j >= starts[p]) & (j < starts[p] + pieces[p].shape[1] // tn))
            def _(p=p):
                o_ref[...] = lax.dot_general(a_ref[...].astype(BF16), refs[1 + p][...].astype(BF16), _DIMS["tn"],
                                             preferred_element_type=F32).astype(out_dtype)

    in_specs = [pl.BlockSpec((k, tm), lambda i, j: (0, i))]
    for p, b in enumerate(pieces):
        np_ = b.shape[1] // tn
        in_specs.append(pl.BlockSpec((k, tn), lambda i, j, p=p, np_=np_: (0, jnp.clip(j - starts[p], 0, np_ - 1))))
    return _pcall(
        body,
        name=name,
        out_shape=jax.ShapeDtypeStruct((m, nj * tn), out_dtype),
        grid=(m // tm, nj),
        in_specs=in_specs,
        out_specs=pl.BlockSpec((tm, tn), lambda i, j: (i, j)),
        compiler_params=_cparams(("parallel", "arbitrary")),
    )(a, *pieces)


def _mm_up_fwd(z2, w3, name, tm=1024):
    t, d = z2.shape
    nsh, _, c = w3.shape
    tm = _tile(t, tm, 8)
    return _mm_call(z2, w3, mode="nn", grid=(t // tm, nsh, 1),
                    a_spec=pl.BlockSpec((tm, d), lambda i, j, kk: (i, 0)),
                    b_spec=pl.BlockSpec((None, d, c), lambda i, j, kk: (j, 0, 0)),
                    o_spec=pl.BlockSpec((tm, c), lambda i, j, kk: (i, j)),
                    out_shape=jax.ShapeDtypeStruct((t, nsh * c), BF16), acc_shape=(tm, c), name=name)


def _mm_up_dz(du3, w3, name, tm=512, tn=1024):
    _, t, f = du3.shape
    nsh, d, c = w3.shape
    half = nsh // 2
    assert f == half * c
    tm, tn = _tile(t, tm, 8), _tile(d, tn)

    def body(a_ref, b_ref, o_ref, acc):
        kk = pl.program_id(2)
        p = None
        for s in range(half):
            q = lax.dot_general(a_ref[:, s * c:(s + 1) * c], b_ref[s], _DIMS["nt"], preferred_element_type=F32)
            p = q if p is None else p + q

        @pl.when(kk == 0)
        def _():
            acc[...] = p

        @pl.when(kk == 1)
        def _():
            o_ref[...] = (acc[...] + p).astype(BF16)

    return _pcall(
        body,
        name=name,
        out_shape=jax.ShapeDtypeStruct((t, d), BF16),
        grid=(t // tm, d // tn, 2),
        in_specs=[pl.BlockSpec((None, tm, f), lambda i, j, kk: (kk, i, 0)),
                  pl.BlockSpec((half, tn, c), lambda i, j, kk: (kk, j, 0))],
        out_specs=pl.BlockSpec((tm, tn), lambda i, j, kk: (i, j)),
        scratch_shapes=[pltpu.VMEM((tm, tn), F32)],
        compiler_params=_cparams(("parallel", "parallel", "arbitrary")),
    )(du3, w3)


def _mm_sum_nn(pieces, out_dtype, name, tm=512, tn=512, rows=None):
    m = pieces[0][0].shape[0] if rows is None else rows
    n = pieces[0][2].shape[1]
    tm, tn = _tile(m, tm, 8), _tile(n, tn)
    npc = len(pieces)

    def body(*refs):
        p = None
        for s in range(npc):
            q = jnp.dot(refs[2 * s][...].astype(BF16), refs[2 * s + 1][...].astype(BF16), preferred_element_type=F32)
            p = q if p is None else p + q
        refs[2 * npc][...] = p.astype(out_dtype)

    in_specs, args = [], []
    for a, ao, b, bo, kp in pieces:
        assert ao % kp == 0 and bo % kp == 0 and b.shape[1] == n
        in_specs.append(pl.BlockSpec((tm, kp), lambda i, j, ab=ao // kp: (i, ab)))
        in_specs.append(pl.BlockSpec((kp, tn), lambda i, j, bb=bo // kp: (bb, j)))
        args += [a, b]
    return _pcall(
        body,
        name=name,
        out_shape=jax.ShapeDtypeStruct((m, n), out_dtype),
        grid=(m // tm, n // tn),
        in_specs=in_specs,
        out_specs=pl.BlockSpec((tm, tn), lambda i, j: (i, j)),
        compiler_params=_cparams(("parallel", "parallel")),
    )(*args)


def _mm_rows_tn(pieces, b, out_dtype, name, tm=1024, tn=1024, rows=None):
    k = b.shape[0] if rows is None else rows
    n = b.shape[1]
    tn = _tile(n, tn)
    starts, s = [], 0
    for a in pieces:
        assert a.shape[1] % tm == 0
        starts.append(s)
        s += a.shape[1] // tm
    ni = s
    npc = len(pieces)

    def body(*refs):
        b_ref, o_ref = refs[npc], refs[npc + 1]
        i = pl.program_id(0)
        for p in range(npc):
            @pl.when((i >= starts[p]) & (i < starts[p] + pieces[p].shape[1] // tm))
            def _(p=p):
                o_ref[...] = lax.dot_general(refs[p][...].astype(BF16), b_ref[...].astype(BF16), _DIMS["tn"],
                                             preferred_element_type=F32).astype(out_dtype)

    in_specs = []
    for p, a in enumerate(pieces):
        np_ = a.shape[1] // tm
        in_specs.append(pl.BlockSpec((k, tm), lambda i, j, p=p, np_=np_: (0, jnp.clip(i - starts[p], 0, np_ - 1))))
    in_specs.append(pl.BlockSpec((k, tn), lambda i, j: (0, j)))
    return _pcall(
        body,
        name=name,
        out_shape=jax.ShapeDtypeStruct((ni * tm, n), out_dtype),
        grid=(ni, n // tn),
        in_specs=in_specs,
        out_specs=pl.BlockSpec((tm, tn), lambda i, j: (i, j)),
        compiler_params=_cparams(("parallel", "parallel")),
    )(*pieces, b)


def _mm_tn_shards(a, b, nsh, name):
    k, m = a.shape
    n = b.shape[1]
    c = n // nsh
    return _mm_call(a, b, mode="tn", grid=(1, nsh, 1),
                    a_spec=pl.BlockSpec((k, m), lambda i, j, kk: (0, 0)),
                    b_spec=pl.BlockSpec((k, c), lambda i, j, kk: (0, j)),
                    o_spec=pl.BlockSpec((None, m, c), lambda i, j, kk: (j, 0, 0)),
                    out_shape=jax.ShapeDtypeStruct((nsh, m, c), BF16), acc_shape=(m, c), name=name)


def _mm_up_gw(z2, du3, nsh, name, tm=1024):
    t, d = z2.shape
    f = du3.shape[2]
    half = nsh // 2
    c = f // half
    tm = _tile(d, tm)
    return _mm_call(z2, du3, mode="tn", grid=(d // tm, nsh, 1),
                    a_spec=pl.BlockSpec((t, tm), lambda i, j, kk: (0, i)),
                    b_spec=pl.BlockSpec((None, t, c), lambda i, j, kk: (j // half, 0, j % half)),
                    o_spec=pl.BlockSpec((None, tm, c), lambda i, j, kk: (j, i, 0)),
                    out_shape=jax.ShapeDtypeStruct((nsh, d, c), BF16), acc_shape=(tm, c), name=name)


def _rms(x):
    r = lax.rsqrt(jnp.mean(x * x, axis=-1, keepdims=True) + NORM_EPS)
    return x * r, r


def _rms_bwd(dxh, xh, r):
    return r * (dxh - xh * jnp.mean(dxh * xh, axis=-1, keepdims=True))


def _colsum(v):
    return jnp.sum(v, axis=0, keepdims=True)


def _rope(v, c, s1, s2, q):
    w = v.shape[-1]
    return v * c + pltpu.roll(v, w - q, 1) * s1 + pltpu.roll(v, q, 1) * s2


def _rope_t(d, c, s1, s2, q):
    w = d.shape[-1]
    return d * c + pltpu.roll(d * s1, q, 1) + pltpu.roll(d * s2, w - q, 1)


def _norm_mod_fwd(ctx, x, gain, mods):
    tc, d = ctx.shape
    t = x.shape[0]
    rb = min(ROW_BLOCK, tc)
    nbl = t // rb

    def body(ctx_ref, x_ref, g_ref, mod_ref, z_ref):
        i = pl.program_id(0)

        def emit(src, sh, sc):
            xh, _ = _rms(src[...])
            z_ref[...] = ((xh * g_ref[...]) * (1.0 + sc) + sh).astype(BF16)

        @pl.when(i >= nbl)
        def _():
            emit(ctx_ref, mod_ref[2:3, :], mod_ref[3:4, :])

        @pl.when(i < nbl)
        def _():
            emit(x_ref, mod_ref[0:1, :], mod_ref[1:2, :])

    return _pcall(
        body,
        name="norm1_mod_fwd",
        out_shape=jax.ShapeDtypeStruct((tc + t, d), BF16),
        grid=((tc + t) // rb,),
        in_specs=[
            pl.BlockSpec((rb, d), lambda i: (jnp.maximum(i - nbl, 0), 0)),
            pl.BlockSpec((rb, d), lambda i: (jnp.minimum(i, nbl - 1), 0)),
            pl.BlockSpec((1, d), lambda i: (0, 0)),
            pl.BlockSpec((8, d), lambda i: (0, 0)),
        ],
        out_specs=pl.BlockSpec((rb, d), lambda i: (i, 0)),
        compiler_params=_cparams(("arbitrary",)),
    )(ctx, x, gain, mods)


def _norm1_bwd(ctx, x, gain, mods, dz_ctx, dz_lat, dx1):
    tc, d = ctx.shape
    t = x.shape[0]
    rb = min(ROW_BLOCK, tc)
    nbl = t // rb

    def body(ctx_ref, x_ref, g_ref, mod_ref, dzc_ref, dzl_ref, dx1_ref, gx_ref, st_ref):
        i = pl.program_id(0)

        @pl.when(i == 0)
        def _():
            st_ref[...] = jnp.zeros_like(st_ref)

        def common(src, dz, sc, row_sh, row_sc):
            xh, r = _rms(src[...])
            g = g_ref[...]
            dxn = dz * (1.0 + sc)
            st_ref[row_sh:row_sh + 1, :] += _colsum(dz)
            st_ref[row_sc:row_sc + 1, :] += _colsum(dz * (xh * g))
            st_ref[2:3, :] += _colsum(dxn * xh)
            return _rms_bwd(dxn * g, xh, r)

        @pl.when(i >= nbl)
        def _():
            common(ctx_ref, dzc_ref[...], mod_ref[3:4, :], 3, 4)

        @pl.when(i < nbl)
        def _():
            gx_ref[...] = dx1_ref[...] + common(x_ref, dzl_ref[...], mod_ref[1:2, :], 0, 1)

    lat = lambda i: (jnp.minimum(i, nbl - 1), 0)
    cix = lambda i: (jnp.maximum(i - nbl, 0), 0)
    return _pcall(
        body,
        name="norm1_mod_bwd",
        out_shape=[jax.ShapeDtypeStruct((t, d), F32), jax.ShapeDtypeStruct((8, d), F32)],
        grid=((tc + t) // rb,),
        in_specs=[
            pl.BlockSpec((rb, d), cix),
            pl.BlockSpec((rb, d), lat),
            pl.BlockSpec((1, d), lambda i: (0, 0)),
            pl.BlockSpec((8, d), lambda i: (0, 0)),
            pl.BlockSpec((rb, d), cix),
            pl.BlockSpec((rb, d), lat),
            pl.BlockSpec((rb, d), lat),
        ],
        out_specs=[pl.BlockSpec((rb, d), lat), pl.BlockSpec((8, d), lambda i: (0, 0))],
        compiler_params=_cparams(("arbitrary",)),
    )(ctx, x, gain, mods, dz_ctx, dz_lat, dx1)


def _key_prep_fwd(kv, kv_gain, kb_gain, tabs):
    ta, wkv = kv.shape
    kvl = MLA_KV_LORA
    nb = GQA_KV_HEADS * GQA_HEAD_DIM
    rb = ROW_BLOCK if ta % ROW_BLOCK == 0 else LANE
    hd = GQA_HEAD_DIM

    def body(kv_ref, g_ref, gb_ref, ca, s1a, s2a, cb, s1b, s2b, kin_ref, kb_ref, vb_ref):
        xh, _ = _rms(kv_ref[:, 0:kvl])
        kin_ref[:, 0:kvl] = (xh * g_ref[...]).astype(BF16)
        kpe = kv_ref[:, kvl + 2 * nb:kvl + 2 * nb + LANE]
        kin_ref[:, kvl:kvl + LANE] = _rope(kpe, ca[...], s1a[...], s2a[...], MLA_ROPE // 4).astype(BF16)
        for h in range(GQA_KV_HEADS):
            nh, _ = _rms(kv_ref[:, kvl + h * hd:kvl + (h + 1) * hd])
            kb_ref[:, h * hd:(h + 1) * hd] = _rope(nh * gb_ref[...], cb[...], s1b[...], s2b[...], hd // 4).astype(BF16)
        vb_ref[...] = kv_ref[:, kvl + nb:kvl + 2 * nb].astype(BF16)

    row = lambda w: pl.BlockSpec((rb, w), lambda i: (i, 0))
    fix = lambda w: pl.BlockSpec((1, w), lambda i: (0, 0))
    return _pcall(
        body,
        name="key_prep_fwd",
        out_shape=[jax.ShapeDtypeStruct((ta, kvl + LANE), BF16), jax.ShapeDtypeStruct((ta, nb), BF16),
                   jax.ShapeDtypeStruct((ta, nb), BF16)],
        grid=(ta // rb,),
        in_specs=[row(wkv), fix(kvl), fix(hd)] + [row(LANE)] * 3 + [row(hd)] * 3,
        out_specs=[row(kvl + LANE), row(nb), row(nb)],
        compiler_params=_cparams(("parallel",)),
    )(kv, kv_gain, kb_gain, *tabs)


def _key_prep_bwd(kv, kv_gain, kb_gain, tabs, dkin, dkb, dvb):
    ta, wkv = kv.shape
    kvl = MLA_KV_LORA
    nb = GQA_KV_HEADS * GQA_HEAD_DIM
    rb = ROW_BLOCK if ta % ROW_BLOCK == 0 else LANE
    hd = GQA_HEAD_DIM

    def body(kv_ref, g_ref, gb_ref, ca, s1a, s2a, cb, s1b, s2b, dkin_ref, dkb_ref, dvb_ref, dkv_ref, st_ref, stb_ref):
        @pl.when(pl.program_id(0) == 0)
        def _():
            st_ref[...] = jnp.zeros_like(st_ref)
            stb_ref[...] = jnp.zeros_like(stb_ref)

        xh, r = _rms(kv_ref[:, 0:kvl])
        dn = dkin_ref[:, 0:kvl]
        st_ref[0:1, :] += _colsum(dn * xh)
        dkv_ref[:, 0:kvl] = _rms_bwd(dn * g_ref[...], xh, r).astype(BF16)
        dpe = _rope_t(dkin_ref[:, kvl:kvl + LANE], ca[...], s1a[...], s2a[...], MLA_ROPE // 4)
        dkv_ref[:, kvl + 2 * nb:kvl + 2 * nb + LANE] = dpe.astype(BF16)
        for h in range(GQA_KV_HEADS):
            nh, rh = _rms(kv_ref[:, kvl + h * hd:kvl + (h + 1) * hd])
            dn_h = _rope_t(dkb_ref[:, h * hd:(h + 1) * hd], cb[...], s1b[...], s2b[...], hd // 4)
            stb_ref[0:1, :] += _colsum(dn_h * nh)
            dkv_ref[:, kvl + h * hd:kvl + (h + 1) * hd] = _rms_bwd(dn_h * gb_ref[...], nh, rh).astype(BF16)
        dkv_ref[:, kvl + nb:kvl + 2 * nb] = dvb_ref[...].astype(BF16)

    row = lambda w: pl.BlockSpec((rb, w), lambda i: (i, 0))
    fix = lambda w: pl.BlockSpec((1, w), lambda i: (0, 0))
    return _pcall(
        body,
        name="key_prep_bwd",
        out_shape=[jax.ShapeDtypeStruct((ta, wkv), BF16), jax.ShapeDtypeStruct((8, kvl), F32),
                   jax.ShapeDtypeStruct((8, hd), F32)],
        grid=(ta // rb,),
        in_specs=[row(wkv), fix(kvl), fix(hd)] + [row(LANE)] * 3 + [row(hd)] * 3 + [row(kvl + LANE), row(nb), row(nb)],
        out_specs=[row(wkv), pl.BlockSpec((8, kvl), lambda i: (0, 0)), pl.BlockSpec((8, hd), lambda i: (0, 0))],
        compiler_params=_cparams(("arbitrary",)),
    )(kv, kv_gain, kb_gain, *tabs, dkin, dkb, dvb)


def _q_prep_fwd(qg, q_gain, qb_gain, tabs, qscale):
    t = qg.shape[0]
    ql = MLA_Q_LORA
    hd = GQA_HEAD_DIM
    hb = GQA_HEADS * hd
    rb = min(ROW_BLOCK, t)

    def body(q_ref, g_ref, gb_ref, cb, s1b, s2b, cqn_ref, qb_ref):
        xh, _ = _rms(q_ref[:, 0:ql])
        cqn_ref[...] = (xh * g_ref[...]).astype(BF16)
        for h in range(GQA_HEADS):
            nh, _ = _rms(q_ref[:, ql + h * hd:ql + (h + 1) * hd])
            qh = _rope(nh * gb_ref[...], cb[...], s1b[...], s2b[...], hd // 4)
            qb_ref[:, h * hd:(h + 1) * hd] = (qh * qscale).astype(BF16)

    row = lambda w: pl.BlockSpec((rb, w), lambda i: (i, 0))
    fix = lambda w: pl.BlockSpec((1, w), lambda i: (0, 0))
    return _pcall(
        body,
        name="q_prep_fwd",
        out_shape=[jax.ShapeDtypeStruct((t, ql), BF16), jax.ShapeDtypeStruct((t, hb), BF16)],
        grid=(t // rb,),
        in_specs=[row(ql + hb), fix(ql), fix(hd)] + [row(hd)] * 3,
        out_specs=[row(ql), row(hb)],
        compiler_params=_cparams(("parallel",)),
    )(qg, q_gain, qb_gain, *tabs)


def _q_prep_bwd(qg, q_gain, qb_gain, tabs, dcqn, dqb, wpad, qscale):
    t = qg.shape[0]
    ql = MLA_Q_LORA
    hd = GQA_HEAD_DIM
    hb = GQA_HEADS * hd
    rb = min(ROW_BLOCK, t)

    def body(q_ref, g_ref, gb_ref, cb, s1b, s2b, dcqn_ref, dqb_ref, dq_ref, st_ref, stb_ref):
        @pl.when(pl.program_id(0) == 0)
        def _():
            st_ref[...] = jnp.zeros_like(st_ref)
            stb_ref[...] = jnp.zeros_like(stb_ref)

        xh, r = _rms(q_ref[:, 0:ql])
        dn = dcqn_ref[...]
        st_ref[0:1, :] += _colsum(dn * xh)
        dq_ref[:, 0:ql] = _rms_bwd(dn * g_ref[...], xh, r).astype(BF16)
        for h in range(GQA_HEADS):
            nh, rh = _rms(q_ref[:, ql + h * hd:ql + (h + 1) * hd])
            dn_h = _rope_t(dqb_ref[:, h * hd:(h + 1) * hd] * qscale, cb[...], s1b[...], s2b[...], hd // 4)
            stb_ref[0:1, :] += _colsum(dn_h * nh)
            dq_ref[:, ql + h * hd:ql + (h + 1) * hd] = _rms_bwd(dn_h * gb_ref[...], nh, rh).astype(BF16)
        if wpad:
            dq_ref[:, ql + hb:ql + hb + wpad] = jnp.zeros((rb, wpad), BF16)

    row = lambda w: pl.BlockSpec((rb, w), lambda i: (i, 0))
    fix = lambda w: pl.BlockSpec((1, w), lambda i: (0, 0))
    return _pcall(
        body,
        name="q_prep_bwd",
        out_shape=[jax.ShapeDtypeStruct((t, ql + hb + wpad), BF16), jax.ShapeDtypeStruct((8, ql), F32),
                   jax.ShapeDtypeStruct((8, hd), F32)],
        grid=(t // rb,),
        in_specs=[row(ql + hb), fix(ql), fix(hd)] + [row(hd)] * 3 + [row(ql), row(hb)],
        out_specs=[row(ql + hb + wpad), pl.BlockSpec((8, ql), lambda i: (0, 0)), pl.BlockSpec((8, hd), lambda i: (0, 0))],
        compiler_params=_cparams(("arbitrary",)),
    )(qg, q_gain, qb_gain, *tabs, dcqn, dqb)


def _rope_a(v, tabs, transpose, out_dtype, name, qscale):
    t, w = v.shape
    rb = min(ROW_BLOCK, t)
    fn = _rope_t if transpose else _rope

    def body(v_ref, c, s1, s2, o_ref):
        for h in range(w // MLA_SLOT):
            sl = slice(h * MLA_SLOT, (h + 1) * MLA_SLOT)
            o_ref[:, sl] = (fn(v_ref[:, sl].astype(F32), c[...], s1[...], s2[...], MLA_ROPE // 4) * qscale).astype(out_dtype)

    row = lambda ww: pl.BlockSpec((rb, ww), lambda i: (i, 0))
    return _pcall(
        body,
        name=name,
        out_shape=jax.ShapeDtypeStruct((t, w), out_dtype),
        grid=(t // rb,),
        in_specs=[row(w)] + [row(MLA_SLOT)] * 3,
        out_specs=row(w),
        compiler_params=_cparams(("parallel",)),
    )(v, *tabs)


def _merge_fwd(pa, pb, qg, gate_blk):
    t, d = pa.shape
    rb = min(ROW_BLOCK, t)

    def body(pa_ref, pb_ref, ga_ref, gb_ref, o_ref):
        o_ref[...] = (jax.nn.sigmoid(ga_ref[...]) * pa_ref[...].astype(F32)
                      + jax.nn.sigmoid(gb_ref[...]) * pb_ref[...].astype(F32)).astype(BF16)

    row = pl.BlockSpec((rb, d), lambda i: (i, 0))
    return _pcall(
        body,
        name="merge_fwd",
        out_shape=jax.ShapeDtypeStruct((t, d), BF16),
        grid=(t // rb,),
        in_specs=[row, row, pl.BlockSpec((rb, d), lambda i: (i, gate_blk)), pl.BlockSpec((rb, d), lambda i: (i, gate_blk + 1))],
        out_specs=row,
        compiler_params=_cparams(("parallel",)),
    )(pa, pb, qg, qg)


def _merge_bwd(dm, pa, pb, qg, gate_blk):
    t, d = pa.shape
    rb = min(ROW_BLOCK, t)

    def body(dm_ref, pa_ref, pb_ref, ga_ref, gb_ref, dpa_ref, dpb_ref, dg_ref):
        dmv = dm_ref[...].astype(F32)
        sa = jax.nn.sigmoid(ga_ref[...])
        sb = jax.nn.sigmoid(gb_ref[...])
        dpa_ref[...] = (dmv * sa).astype(BF16)
        dpb_ref[...] = (dmv * sb).astype(BF16)
        dg_ref[:, 0:d] = (dmv * pa_ref[...].astype(F32) * (sa * (1.0 - sa))).astype(BF16)
        dg_ref[:, d:2 * d] = (dmv * pb_ref[...].astype(F32) * (sb * (1.0 - sb))).astype(BF16)

    row = pl.BlockSpec((rb, d), lambda i: (i, 0))
    return _pcall(
        body,
        name="merge_bwd",
        out_shape=[jax.ShapeDtypeStruct((t, d), BF16), jax.ShapeDtypeStruct((t, d), BF16),
                   jax.ShapeDtypeStruct((t, 2 * d), BF16)],
        grid=(t // rb,),
        in_specs=[row, row, row, pl.BlockSpec((rb, d), lambda i: (i, gate_blk)), pl.BlockSpec((rb, d), lambda i: (i, gate_blk + 1))],
        out_specs=[row, row, pl.BlockSpec((rb, 2 * d), lambda i: (i, 0))],
        compiler_params=_cparams(("parallel",)),
    )(dm, pa, pb, qg, qg)


def _resid_norm_mod(x, branch, gain, mods, name):
    t, d = x.shape
    rb = min(ROW_BLOCK, t)

    def body(x_ref, b_ref, g_ref, mod_ref, x1_ref, z_ref):
        x1 = x_ref[...] + mod_ref[0:1, :] * b_ref[...]
        x1_ref[...] = x1
        xh, _ = _rms(x1)
        z_ref[...] = ((xh * g_ref[...]) * (1.0 + mod_ref[2:3, :]) + mod_ref[1:2, :]).astype(BF16)

    row = pl.BlockSpec((rb, d), lambda i: (i, 0))
    return _pcall(
        body,
        name=name,
        out_shape=[jax.ShapeDtypeStruct((t, d), F32), jax.ShapeDtypeStruct((t, d), BF16)],
        grid=(t // rb,),
        in_specs=[row, row, pl.BlockSpec((1, d), lambda i: (0, 0)), pl.BlockSpec((8, d), lambda i: (0, 0))],
        out_specs=[row, row],
        compiler_params=_cparams(("parallel",)),
    )(x, branch, gain, mods)


def _norm2_bwd(x1, attn, gain, mods, dz2, dx2):
    t, d = x1.shape
    rb = min(ROW_BLOCK, t)

    def body(x1_ref, at_ref, g_ref, mod_ref, dz_ref, dx2_ref, dx1_ref, da_ref, st_ref):
        @pl.when(pl.program_id(0) == 0)
        def _():
            st_ref[...] = jnp.zeros_like(st_ref)

        xh, r = _rms(x1_ref[...])
        g = g_ref[...]
        dz = dz_ref[...].astype(F32)
        dxn = dz * (1.0 + mod_ref[1:2, :])
        st_ref[0:1, :] += _colsum(dz)
        st_ref[1:2, :] += _colsum(dz * (xh * g))
        st_ref[2:3, :] += _colsum(dxn * xh)
        dx1 = dx2_ref[...] + _rms_bwd(dxn * g, xh, r)
        dx1_ref[...] = dx1
        st_ref[3:4, :] += _colsum(dx1 * at_ref[...])
        da_ref[...] = (dx1 * mod_ref[0:1, :]).astype(BF16)

    row = pl.BlockSpec((rb, d), lambda i: (i, 0))
    return _pcall(
        body,
        name="norm2_mod_bwd",
        out_shape=[jax.ShapeDtypeStruct((t, d), F32), jax.ShapeDtypeStruct((t, d), BF16), jax.ShapeDtypeStruct((8, d), F32)],
        grid=(t // rb,),
        in_specs=[row, row, pl.BlockSpec((1, d), lambda i: (0, 0)), pl.BlockSpec((8, d), lambda i: (0, 0)), row, row],
        out_specs=[row, row, pl.BlockSpec((8, d), lambda i: (0, 0))],
        compiler_params=_cparams(("arbitrary",)),
    )(x1, attn, gain, mods, dz2, dx2)


def _final_loss(x1, ffn, gain, mods, target):
    t, d = x1.shape
    rb = min(ROW_BLOCK, t)
    nb = t // rb

    def body(x1_ref, f_ref, g_ref, mod_ref, tg_ref, dx2_ref, df_ref, st_ref):
        i = pl.program_id(0)

        @pl.when(i == 0)
        def _():
            st_ref[...] = jnp.zeros_like(st_ref)

        ffn_v = f_ref[...]
        g2 = mod_ref[0:1, :]
        x2 = x1_ref[...] + g2 * ffn_v
        xh, r = _rms(x2)
        g = g_ref[...]
        err = xh * g - tg_ref[...]
        st_ref[2:3, :] += _colsum(err * err) * (0.5 / d)
        dy = err * (1.0 / d)
        st_ref[0:1, :] += _colsum(dy * xh)
        dx2 = _rms_bwd(dy * g, xh, r)
        dx2_ref[...] = dx2
        st_ref[1:2, :] += _colsum(dx2 * ffn_v)
        df_ref[...] = (dx2 * g2).astype(BF16)

        @pl.when(i == nb - 1)
        def _():
            st_ref[3:4, :] = jnp.broadcast_to(jnp.sum(st_ref[2:3, :], axis=-1, keepdims=True), (1, d))

    row = pl.BlockSpec((rb, d), lambda i: (i, 0))
    return _pcall(
        body,
        name="final_norm_loss",
        out_shape=[jax.ShapeDtypeStruct((t, d), F32), jax.ShapeDtypeStruct((t, d), BF16), jax.ShapeDtypeStruct((8, d), F32)],
        grid=(nb,),
        in_specs=[row, row, pl.BlockSpec((1, d), lambda i: (0, 0)), pl.BlockSpec((8, d), lambda i: (0, 0)), row],
        out_specs=[row, row, pl.BlockSpec((8, d), lambda i: (0, 0))],
        compiler_params=_cparams(("arbitrary",)),
    )(x1, ffn, gain, mods, target)


def _row_ends(shape):
    rows = lax.broadcasted_iota(jnp.int32, shape, 0)
    return rows == 0, rows == shape[0] - 1


def _shift_dn(v, first):
    return jnp.where(first, 0.0, pltpu.roll(v, 1, 0))


def _shift_up(v, last):
    return jnp.where(last, 0.0, pltpu.roll(v, v.shape[0] - 1, 0))


def _conv_fwd(u, cw, cb):
    t, f2 = u.shape
    f = f2 // 2
    cbk = _tile(f, 256)
    nf = f // cbk

    def body(ua_ref, ub_ref, cwa_ref, cwb_ref, cba_ref, cbb_ref, h_ref, uc_ref):
        first, last = _row_ends((t, cbk))
        outs = []
        for u_ref, cw_ref, cb_ref in ((ua_ref, cwa_ref, cba_ref), (ub_ref, cwb_ref, cbb_ref)):
            uu, cwv = u_ref[...].astype(F32), cw_ref[...]
            outs.append(cb_ref[...] + cwv[0:1, :] * _shift_dn(uu, first) + cwv[1:2, :] * uu
                        + cwv[2:3, :] * _shift_up(uu, last))
        a, b = outs
        uc_ref[0] = a.astype(BF16)
        uc_ref[1] = b.astype(BF16)
        h_ref[...] = (a * jax.nn.sigmoid(a) * b).astype(BF16)

    ca = lambda r: pl.BlockSpec((r, cbk), lambda j: (0, j))
    cbs = lambda r: pl.BlockSpec((r, cbk), lambda j: (0, nf + j))
    return _pcall(
        body,
        name="conv_gate_fwd",
        out_shape=[jax.ShapeDtypeStruct((t, f), BF16), jax.ShapeDtypeStruct((2, t, f), BF16)],
        grid=(nf,),
        in_specs=[ca(t), cbs(t), ca(3), cbs(3), ca(1), cbs(1)],
        out_specs=[ca(t), pl.BlockSpec((2, t, cbk), lambda j: (0, 0, j))],
        compiler_params=_cparams(("parallel",)),
    )(u, u, cw, cw, cb, cb)


def _conv_bwd(u, uc, cw, dh):
    t, f2 = u.shape
    f = f2 // 2
    cbk = _tile(f, 256)
    nf = f // cbk

    def body(ua_ref, ub_ref, uc_ref, cwa_ref, cwb_ref, dh_ref, du_ref, dcw_ref, dcb_ref):
        first, last = _row_ends((t, cbk))
        a, b = uc_ref[0].astype(F32), uc_ref[1].astype(F32)
        dh_v = dh_ref[...].astype(F32)
        sg = jax.nn.sigmoid(a)
        db = dh_v * (a * sg)
        da = dh_v * b * (sg * (1.0 + a * (1.0 - sg)))
        for idx, (dv, u_ref, cw_ref) in enumerate(((da, ua_ref, cwa_ref), (db, ub_ref, cwb_ref))):
            uu, cwv = u_ref[...].astype(F32), cw_ref[...]
            up, dn = _shift_up(dv, last), _shift_dn(dv, first)
            dcb_ref[idx] = _colsum(dv)
            dcw_ref[idx, 0:1, :] = _colsum(up * uu)
            dcw_ref[idx, 1:2, :] = _colsum(dv * uu)
            dcw_ref[idx, 2:3, :] = _colsum(dn * uu)
            du_ref[idx] = (cwv[0:1, :] * up + cwv[1:2, :] * dv + cwv[2:3, :] * dn).astype(BF16)

    ca = lambda r: pl.BlockSpec((r, cbk), lambda j: (0, j))
    cbs = lambda r: pl.BlockSpec((r, cbk), lambda j: (0, nf + j))
    o3 = lambda r: pl.BlockSpec((2, r, cbk), lambda j: (0, 0, j))
    return _pcall(
        body,
        name="conv_gate_bwd",
        out_shape=[jax.ShapeDtypeStruct((2, t, f), BF16), jax.ShapeDtypeStruct((2, 3, f), F32),
                   jax.ShapeDtypeStruct((2, 1, f), F32)],
        grid=(nf,),
        in_specs=[ca(t), cbs(t), o3(t), ca(3), cbs(3), ca(t)],
        out_specs=[o3(t), o3(3), o3(1)],
        compiler_params=_cparams(("parallel",)),
    )(u, u, uc, cw, cw, dh)


def _attention_fwd(q, kk, vv, *, hq, hkv, dk, dv, k_blk0, v_blk0, name):
    t = q.shape[0]
    tk = kk.shape[0]
    g_sz = hq // hkv
    tq = min(ATT_Q_BLOCK_FWD, t)

    def body(q_ref, k_ref, v_ref, o_ref, lse_ref):
        k = k_ref[...]
        v = v_ref[...]
        for j in range(g_sz):
            s = lax.dot_general(q_ref[:, j * dk:(j + 1) * dk], k, _DIMS["nt"], preferred_element_type=F32)
            m = jnp.max(s, axis=-1, keepdims=True)
            p = jnp.exp2(s - m)
            l = jnp.sum(p, axis=-1, keepdims=True)
            o = jnp.dot(p.astype(BF16), v, preferred_element_type=F32) / l
            o_ref[:, j * dv:(j + 1) * dv] = o.astype(BF16)
            lse_ref[0, :, j:j + 1] = m + jnp.log2(l)

    return _pcall(
        body,
        name=name,
        out_shape=[jax.ShapeDtypeStruct((t, hq * dv), BF16), jax.ShapeDtypeStruct((hkv, t, g_sz), F32)],
        grid=(hkv, t // tq),
        in_specs=[
            pl.BlockSpec((tq, g_sz * dk), lambda g, i: (i, g)),
            pl.BlockSpec((tk, dk), lambda g, i: (0, k_blk0 + g)),
            pl.BlockSpec((tk, dv), lambda g, i: (0, v_blk0 + g)),
        ],
        out_specs=[
            pl.BlockSpec((tq, g_sz * dv), lambda g, i: (i, g)),
            pl.BlockSpec((1, tq, g_sz), lambda g, i: (g, i, 0)),
        ],
        compiler_params=_cparams(("parallel", "parallel")),
    )(q, kk, vv)


def _attention_bwd(q, kk, vv, do, lse, *, hq, hkv, dk, dv, k_blk0, v_blk0, name):
    t = q.shape[0]
    tk = kk.shape[0]
    g_sz = hq // hkv
    tq = min(ATT_Q_BLOCK, t)

    def body(q_ref, k_ref, v_ref, do_ref, lse_ref, dq_ref, dk_ref, dv_ref):
        @pl.when(pl.program_id(1) == 0)
        def _():
            dk_ref[...] = jnp.zeros_like(dk_ref)
            dv_ref[...] = jnp.zeros_like(dv_ref)

        k = k_ref[...]
        v = v_ref[...]
        dk_acc = dv_acc = None
        for j in range(g_sz):
            qj = q_ref[:, j * dk:(j + 1) * dk]
            doj = do_ref[:, j * dv:(j + 1) * dv]
            s = lax.dot_general(qj, k, _DIMS["nt"], preferred_element_type=F32)
            p = jnp.exp2(s - lse_ref[0, :, j:j + 1])
            dp = lax.dot_general(doj, v, _DIMS["nt"], preferred_element_type=F32)
            ds = (p * (dp - jnp.sum(p * dp, axis=-1, keepdims=True))).astype(BF16)
            dv_j = lax.dot_general(p.astype(BF16), doj, _DIMS["tn"], preferred_element_type=F32)
            dk_j = lax.dot_general(ds, qj, _DIMS["tn"], preferred_element_type=F32)
            dv_acc = dv_j if dv_acc is None else dv_acc + dv_j
            dk_acc = dk_j if dk_acc is None else dk_acc + dk_j
            dq_ref[:, j * dk:(j + 1) * dk] = jnp.dot(ds, k, preferred_element_type=F32)
        dv_ref[...] += dv_acc
        dk_ref[...] += dk_acc

        @pl.when(pl.program_id(1) == t // tq - 1)
        def _():
            dk_ref[...] *= LN2

    return _pcall(
        body,
        name=name,
        out_shape=[jax.ShapeDtypeStruct((t, hq * dk), F32), jax.ShapeDtypeStruct((tk, hkv * dk), F32),
                   jax.ShapeDtypeStruct((tk, hkv * dv), F32)],
        grid=(hkv, t // tq),
        in_specs=[
            pl.BlockSpec((tq, g_sz * dk), lambda g, i: (i, g)),
            pl.BlockSpec((tk, dk), lambda g, i: (0, k_blk0 + g)),
            pl.BlockSpec((tk, dv), lambda g, i: (0, v_blk0 + g)),
            pl.BlockSpec((tq, g_sz * dv), lambda g, i: (i, g)),
            pl.BlockSpec((1, tq, g_sz), lambda g, i: (g, i, 0)),
        ],
        out_specs=[
            pl.BlockSpec((tq, g_sz * dk), lambda g, i: (i, g)),
            pl.BlockSpec((tk, dk), lambda g, i: (0, g)),
            pl.BlockSpec((tk, dv), lambda g, i: (0, g)),
        ],
        compiler_params=_cparams(("parallel", "arbitrary")),
    )(q, kk, vv, do, lse)


def _silu(v):
    return v * jax.nn.sigmoid(v)


def _ada_fwd(conds, w_ada, b_ada_shard):
    r, d = conds.shape
    n = w_ada.shape[1]
    tn = _tile(n, 512)

    def body(c_ref, w_ref, b_ref, o_ref):
        s = _silu(c_ref[...]).astype(BF16)
        o_ref[...] = jnp.dot(s, w_ref[...].astype(BF16), preferred_element_type=F32) + b_ref[...]

    return _pcall(
        body,
        name="ada_fwd",
        out_shape=jax.ShapeDtypeStruct((r, n), F32),
        grid=(n // tn,),
        in_specs=[pl.BlockSpec((r, d), lambda j: (0, 0)), pl.BlockSpec((d, tn), lambda j: (0, j)),
                  pl.BlockSpec((1, tn), lambda j: (0, j))],
        out_specs=pl.BlockSpec((r, tn), lambda j: (0, j)),
        compiler_params=_cparams(("parallel",)),
    )(conds, w_ada, b_ada_shard)


def _cctx_partial(da16_shard, w_ada, c_ctx_row):
    d, n = w_ada.shape
    td = _tile(d, 512)

    def body(g_ref, w_ref, c_ref, o_ref):
        ds = lax.dot_general(g_ref[8:16, :].astype(BF16), w_ref[...].astype(BF16), _DIMS["nt"],
                             preferred_element_type=F32)
        cv = c_ref[...]
        sg = jax.nn.sigmoid(cv)
        o_ref[...] = ds * (sg * (1.0 + cv * (1.0 - sg)))

    return _pcall(
        body,
        name="cctx_partial",
        out_shape=jax.ShapeDtypeStruct((8, d), F32),
        grid=(d // td,),
        in_specs=[pl.BlockSpec((16, n), lambda j: (0, 0)), pl.BlockSpec((td, n), lambda j: (j, 0)),
                  pl.BlockSpec((1, td), lambda j: (0, j))],
        out_specs=pl.BlockSpec((8, td), lambda j: (0, j)),
        compiler_params=_cparams(("parallel",)),
    )(da16_shard, w_ada, c_ctx_row)


def _sum_parts(parts):
    p, _, n = parts.shape

    def body(p_ref, o_ref):
        acc = p_ref[0]
        for s in range(1, p):
            acc = acc + p_ref[s]
        o_ref[...] = acc

    return _pcall(
        body,
        name="sum_parts",
        out_shape=jax.ShapeDtypeStruct((1, n), F32),
        in_specs=[pl.BlockSpec(memory_space=pltpu.VMEM)],
        out_specs=pl.BlockSpec(memory_space=pltpu.VMEM),
    )(parts)


def _adam_math(w, g, m, v):
    m2 = ADAM_B1 * m + (1.0 - ADAM_B1) * g
    v2 = ADAM_B2 * v + (1.0 - ADAM_B2) * jnp.square(g)
    m_hat = m2 / (1.0 - ADAM_B1 ** ADAM_STEP)
    v_hat = v2 / (1.0 - ADAM_B2 ** ADAM_STEP)
    delta = -ADAM_LR * (m_hat / (jnp.sqrt(v_hat) + ADAM_EPS) + ADAM_WD * w)
    return delta, m2, v2


def _adamw(parts, w, m, v, name):
    p, r, c = parts.shape
    block_elems = 1 << 18
    rb, cb = _tile(r, max(8, block_elems // c // 8 * 8), 8), c
    if rb * c < block_elems // 4 and r * c > block_elems:
        rb, cb = r, _tile(c, max(LANE, block_elems // r // LANE * LANE))

    def body(p_ref, w_ref, m_ref, v_ref, g_ref, d_ref, m2_ref, v2_ref):
        g = p_ref[0].astype(F32)
        for s in range(1, p):
            g = g + p_ref[s].astype(F32)
        g_ref[...] = g
        d_ref[...], m2_ref[...], v2_ref[...] = _adam_math(w_ref[...], g, m_ref[...], v_ref[...])

    if w.ndim == 3:
        blk = pl.BlockSpec((None, rb, cb), lambda i, j: (0, i, j))
    else:
        blk = pl.BlockSpec((rb, cb), lambda i, j: (i, j))
    return _pcall(
        body,
        name=name,
        out_shape=[jax.ShapeDtypeStruct(w.shape, F32)] * 4,
        grid=(r // rb, c // cb),
        in_specs=[pl.BlockSpec((p, rb, cb), lambda i, j: (0, i, j)), blk, blk, blk],
        out_specs=[blk] * 4,
        compiler_params=_cparams(("parallel", "parallel")),
    )(parts, w, m, v)


def _adamw_ada(conds, da16, w, m, v):
    d, n = w.shape
    rb = _tile(d, 256, LANE)

    def body(s_ref, da_ref, w_ref, m_ref, v_ref, g_ref, d_ref, m2_ref, v2_ref):
        g = lax.dot_general(_silu(s_ref[...]).astype(BF16), da_ref[...].astype(BF16), _DIMS["tn"],
                            preferred_element_type=F32)
        g_ref[...] = g
        d_ref[...], m2_ref[...], v2_ref[...] = _adam_math(w_ref[...], g, m_ref[...], v_ref[...])

    row = pl.BlockSpec((rb, n), lambda i: (i, 0))
    return _pcall(
        body,
        name="adamw_w_ada",
        out_shape=[jax.ShapeDtypeStruct((d, n), F32)] * 4,
        grid=(d // rb,),
        in_specs=[pl.BlockSpec((16, rb), lambda i: (0, i)), pl.BlockSpec((16, n), lambda i: (0, 0)), row, row, row],
        out_specs=[row] * 4,
        compiler_params=_cparams(("parallel",)),
    )(conds, da16, w, m, v)


def _cast_bf16(a, name):
    _, r, c = a.shape
    rb, cb = _tile(r, 512, 8), c
    if rb < 64 < r:
        rb, cb = r, _tile(c, 512)

    def body(a_ref, o_ref):
        o_ref[...] = a_ref[...].astype(BF16)

    return _pcall(body, name=name, out_shape=jax.ShapeDtypeStruct((r, c), BF16), grid=(r // rb, c // cb),
                  in_specs=[pl.BlockSpec((None, rb, cb), lambda i, j: (0, i, j))],
                  out_specs=pl.BlockSpec((rb, cb), lambda i, j: (i, j)),
                  compiler_params=_cparams(("parallel", "parallel")))(a)


def _rope_tabs(t, rot):
    half, q = rot // 2, rot // 4
    n_rows = t // GRID_W
    row = jnp.repeat(jnp.arange(n_rows, dtype=F32), GRID_W)
    col = jnp.tile(jnp.arange(GRID_W, dtype=F32), n_rows)
    inv_freq = ROPE_THETA ** (-jnp.arange(0, half, 2, dtype=F32) / half)
    ang = jnp.concatenate([row[:, None] * inv_freq, col[:, None] * inv_freq], axis=-1)
    cos, sin = jnp.cos(ang), jnp.sin(ang)
    c0, c1, s0, s1 = cos[:, :q], cos[:, q:], sin[:, :q], sin[:, q:]
    z = jnp.zeros_like(s0)
    return (jnp.concatenate([c0, c0, c1, c1], -1), jnp.concatenate([-s0, z, -s1, z], -1),
            jnp.concatenate([z, s0, z, s1], -1))


def _pad_cols(a, left, total, fill=0.0):
    return jnp.pad(a, ((0, 0), (left, total - left - a.shape[1])), constant_values=fill)


def _with_ctx_rows(tab, tc, fill):
    return jnp.concatenate([tab, jnp.full((tc, tab.shape[1]), fill, F32)], axis=0)


def kernel(x, c, ctx, c_ctx, w_ada, b_ada, norm1_g, w_in, mla_q_norm_g, w_q_up, mla_kv_norm_g, w_kv_up, gqa_q_norm_g, gqa_k_norm_g, w_br_a, w_br_b, w_out, norm2_g, w_up, conv_w, conv_b, w_down, final_norm_g, loss_target, m_c_ctx, m_w_ada, m_b_ada, m_norm1_g, m_w_in, m_mla_q_norm_g, m_w_q_up, m_mla_kv_norm_g, m_w_kv_up, m_gqa_q_norm_g, m_gqa_k_norm_g, m_w_br_a, m_w_br_b, m_w_out, m_norm2_g, m_w_up, m_conv_w, m_conv_b, m_w_down, m_final_norm_g, v_c_ctx, v_w_ada, v_b_ada, v_norm1_g, v_w_in, v_mla_q_norm_g, v_w_q_up, v_mla_kv_norm_g, v_w_kv_up, v_gqa_q_norm_g, v_gqa_k_norm_g, v_w_br_a, v_w_br_b, v_w_out, v_norm2_g, v_w_up, v_conv_w, v_conv_b, v_w_down, v_final_norm_g):
    weights = dict(c_ctx=c_ctx, w_ada=w_ada, b_ada=b_ada, norm1_g=norm1_g, w_in=w_in, mla_q_norm_g=mla_q_norm_g,
                   w_q_up=w_q_up, mla_kv_norm_g=mla_kv_norm_g, w_kv_up=w_kv_up, gqa_q_norm_g=gqa_q_norm_g,
                   gqa_k_norm_g=gqa_k_norm_g, w_br_a=w_br_a, w_br_b=w_br_b, w_out=w_out, norm2_g=norm2_g, w_up=w_up,
                   conv_w=conv_w, conv_b=conv_b, w_down=w_down, final_norm_g=final_norm_g)
    mom_m = dict(c_ctx=m_c_ctx, w_ada=m_w_ada, b_ada=m_b_ada, norm1_g=m_norm1_g, w_in=m_w_in, mla_q_norm_g=m_mla_q_norm_g,
                 w_q_up=m_w_q_up, mla_kv_norm_g=m_mla_kv_norm_g, w_kv_up=m_w_kv_up, gqa_q_norm_g=m_gqa_q_norm_g,
                 gqa_k_norm_g=m_gqa_k_norm_g, w_br_a=m_w_br_a, w_br_b=m_w_br_b, w_out=m_w_out, norm2_g=m_norm2_g,
                 w_up=m_w_up, conv_w=m_conv_w, conv_b=m_conv_b, w_down=m_w_down, final_norm_g=m_final_norm_g)
    mom_v = dict(c_ctx=v_c_ctx, w_ada=v_w_ada, b_ada=v_b_ada, norm1_g=v_norm1_g, w_in=v_w_in, mla_q_norm_g=v_mla_q_norm_g,
                 w_q_up=v_w_q_up, mla_kv_norm_g=v_mla_kv_norm_g, w_kv_up=v_w_kv_up, gqa_q_norm_g=v_gqa_q_norm_g,
                 gqa_k_norm_g=v_gqa_k_norm_g, w_br_a=v_w_br_a, w_br_b=v_w_br_b, w_out=v_w_out, norm2_g=v_norm2_g,
                 w_up=v_w_up, conv_w=v_conv_w, conv_b=v_conv_b, w_down=v_w_down, final_norm_g=v_final_norm_g)
    order = list(weights)

    my_idx = 4 * lax.axis_index("x") + 2 * lax.axis_index("y") + lax.axis_index("c")
    xs, cts, tgt = x[0], ctx[0], loss_target[0]
    t, d = xs.shape
    tc = cts.shape[0]
    ta = t + tc
    kvl, ql = MLA_KV_LORA, MLA_Q_LORA
    nb = GQA_KV_HEADS * GQA_HEAD_DIM
    hb = GQA_HEADS * GQA_HEAD_DIM
    ha = MLA_HEADS
    f2 = w_up.shape[2] * N_DEV
    ff = f2 // 2

    big = ["w_in", "w_q_up", "w_kv_up", "w_br_a", "w_br_b", "w_out", "w_up", "w_down"]
    _ORDER_AFTER.clear()
    narrow = ("w_in", "w_q_up")

    def tview(a):
        return jnp.transpose(a, (0, 2, 1))

    shards = {"w_in": _cast_bf16(tview(weights["w_in"]), "cast_w_in")}
    c_idx = jnp.reshape(lax.axis_index("c"), (1,)).astype(jnp.int32)

    def gather_start(names, dep):
        shs = [shards[n] for n in names]
        land = [lax.empty((N_DEV,) + s.shape, BF16) for s in shs]
        if dep is not None:
            _after(dep)
        s, r, arrs, tok = _split_start("gather_ici_start_" + names[0], shs + land, _gather_ici_copies(len(names)),
                                       4 * len(names))
        return dict(names=names, s=s, r=r, arrs=arrs, tok=tok)

    def gather_pass(g, after):
        n = len(g["names"])
        arrs = _split_wait("gather_ici_wait_" + g["names"][0], g["s"], g["r"], g["arrs"], _gather_ici_copies(n), after)
        s, r, bufs, tok = _split_start("gather_pass_start_" + g["names"][0], arrs[n:], _gather_pass_copies(n), 3 * n)
        g.update(s2=s, r2=r, bufs=bufs)
        return tok

    def gather_relay(g, after):
        n = len(g["names"])
        bufs = _split_wait("gather_pass_wait_" + g["names"][0], g["s2"], g["r2"], g["bufs"], _gather_pass_copies(n), after)
        s, r, bufs, tok = _split_start("gather_d2d_start_" + g["names"][0], bufs, _gather_d2d_copies(n), n)
        g.update(s3=s, r3=r, bufs=bufs)
        return tok

    def gather_finish(g, after):
        n = len(g["names"])
        bufs = _split_wait("gather_d2d_wait_" + g["names"][0], g["s3"], g["r3"], g["bufs"], _gather_d2d_copies(n), after)
        return dict(zip(g["names"], bufs))

    _after(shards["w_in"])
    c_all, cw_all = _all_gather([jnp.pad(c, ((0, 7), (0, 0))), jnp.pad(conv_w[0], ((0, 5), (0, 0)))], "gather_cond")
    conv_w_f = jnp.transpose(cw_all[:, :3, :], (1, 0, 2)).reshape(3, f2)
    conds = jnp.concatenate([c_all[:, 0, :], c_ctx[None, :], jnp.zeros((7, d), F32)], axis=0)
    ncol = w_ada.shape[2]
    b_shard = lax.dynamic_slice_in_dim(b_ada, my_idx * ncol, ncol, axis=1)
    ada_shard = _ada_fwd(conds, w_ada[0], b_shard)
    (ada_all,) = _all_gather([ada_shard], "gather_ada")
    ada = jnp.transpose(ada_all, (1, 0, 2)).reshape(16, N_DEV * ncol)
    lat = lax.dynamic_slice_in_dim(ada, my_idx, 1, axis=0).reshape(6, d)
    cxt = ada[8].reshape(6, d)
    mods1 = jnp.concatenate([lat[0:2], cxt[0:2], jnp.zeros((4, d), F32)], axis=0)
    mods2 = jnp.concatenate([lat[2:3], lat[3:4], lat[4:5], jnp.zeros((5, d), F32)], axis=0)
    mods2b = jnp.concatenate([lat[2:3], lat[4:5], jnp.zeros((6, d), F32)], axis=0)
    mods3 = jnp.concatenate([lat[5:6], jnp.zeros((7, d), F32)], axis=0)

    g0 = gather_start(["w_in"], ada_all)
    for n in big[1:]:
        _after(g0["tok"])
        shards[n] = _cast_bf16(tview(weights[n]) if n in narrow else weights[n], "cast_" + n)

    ca, s1a, s2a = _rope_tabs(t, MLA_ROPE)
    cb_, s1b, s2b = _rope_tabs(t, GQA_HEAD_DIM)
    q_tabs_a = (_pad_cols(jnp.concatenate([jnp.ones((t, MLA_NOPE), F32), ca], 1), 0, MLA_SLOT),
                _pad_cols(s1a, MLA_NOPE, MLA_SLOT), _pad_cols(s2a, MLA_NOPE, MLA_SLOT))
    q_tabs_b = (cb_, s1b, s2b)
    k_tabs = (_with_ctx_rows(_pad_cols(ca, 0, LANE), tc, 1.0), _with_ctx_rows(_pad_cols(s1a, 0, LANE), tc, 0.0),
              _with_ctx_rows(_pad_cols(s2a, 0, LANE), tc, 0.0),
              _with_ctx_rows(cb_, tc, 1.0), _with_ctx_rows(s1b, tc, 0.0), _with_ctx_rows(s2b, tc, 0.0))

    def cols_full(g):
        return jnp.transpose(g, (1, 0, 2)).reshape(g.shape[1], N_DEV * g.shape[2])

    _after(*q_tabs_a, *q_tabs_b, *k_tabs, *[shards[n] for n in big[1:]])
    tok_p0 = gather_pass(g0, mods1)
    g1 = gather_start(["w_q_up", "w_kv_up", "w_br_a", "w_br_b", "w_out"], tok_p0)
    _after(g1["tok"])
    z_all = _norm_mod_fwd(cts, xs, norm1_g, mods1)
    gathered = gather_finish(g0, gather_relay(g0, z_all))
    wt_in = gathered["w_in"].reshape(-1, d)
    o_kpe, o_kb, o_vb = kvl, kvl + MLA_ROPE, kvl + MLA_ROPE + nb
    o_q = o_vb + nb
    o_g = o_q + ql + hb
    wkv_w = kvl + 2 * nb + LANE
    wt_kv_p = jnp.concatenate([wt_in[:kvl], wt_in[o_kb:o_q], wt_in[o_kpe:o_kb],
                               jnp.zeros((LANE - MLA_ROPE, d), BF16)], axis=0)
    q_w = ql + hb
    q_pad = (-q_w) % 512 if d >= 512 else (-q_w) % d
    gate_blk = (q_w + q_pad) // d
    assert (q_w + q_pad) % d == 0
    wt_qg_p = jnp.concatenate([wt_in[o_q:o_g], jnp.zeros((q_pad, d), BF16), wt_in[o_g:]], axis=0)

    kv_all = _mm(z_all, wt_kv_p, "nt", F32, "proj_kv", tm=1152, tn=wkv_w)
    qg = _mm(z_all, wt_qg_p, "nt", F32, "proj_qg", tm=1024, tn=1024, rows=t)
    tok_p1 = gather_pass(g1, qg)
    g2 = gather_start(["w_up", "w_down"], tok_p1)
    _after(g2["tok"])
    kin, k_b, v_b = _key_prep_fwd(kv_all, mla_kv_norm_g, gqa_k_norm_g, k_tabs)
    sc_a = float((MLA_NOPE + MLA_ROPE) ** -0.5) * LOG2E
    sc_b = float(GQA_HEAD_DIM ** -0.5) * LOG2E
    _after(g2["tok"])
    cqn, q_b = _q_prep_fwd(qg, mla_q_norm_g, gqa_q_norm_g, q_tabs_b, sc_b)
    _after(kin, g2["tok"])
    gathered.update(gather_finish(g1, gather_relay(g1, q_b)))

    wqt_f = gathered["w_q_up"].reshape(ha, MLA_NOPE + MLA_ROPE, ql)
    wqt_ext = jnp.pad(wqt_f, ((0, 0), (0, MLA_SLOT - MLA_NOPE - MLA_ROPE), (0, 0))).reshape(ha * MLA_SLOT, ql)
    wkv_f = cols_full(gathered["w_kv_up"]).reshape(kvl, ha, MLA_NOPE + MLA_V)
    wk_slots = jnp.pad(wkv_f[:, :, :MLA_NOPE], ((0, 0), (0, 0), (0, MLA_SLOT - MLA_NOPE))).reshape(kvl, ha * MLA_SLOT)
    wv_cols = wkv_f[:, :, MLA_NOPE:].reshape(kvl, ha * MLA_V)
    e_slot = jnp.pad(jnp.eye(MLA_ROPE, dtype=BF16),
                     ((0, LANE - MLA_ROPE), (MLA_NOPE, MLA_SLOT - MLA_NOPE - MLA_ROPE)))
    e_rows = jnp.concatenate([jnp.tile(e_slot, (1, ha)), jnp.zeros((LANE, ha * MLA_V), BF16)], axis=1)
    wkv_ext = jnp.concatenate([jnp.concatenate([wk_slots, wv_cols], axis=1), e_rows], axis=0)
    w_bra = cols_full(gathered["w_br_a"])
    w_brb = cols_full(gathered["w_br_b"])
    w_out_f = gathered["w_out"].reshape(d, d)

    kv_a = _mm(kin, wkv_ext, "nn", BF16, "kv_up", tm=1152, tn=1024)
    qa_raw = _mm(cqn, wqt_ext, "nt", F32, "q_up", tm=1024, tn=1024)
    q_a = _rope_a(qa_raw, q_tabs_a, False, BF16, "rope_q_fwd", sc_a)
    att_a = dict(hq=ha, hkv=ha, dk=MLA_SLOT, dv=MLA_V, k_blk0=0, v_blk0=ha * MLA_SLOT // MLA_V)
    att_b = dict(hq=GQA_HEADS, hkv=GQA_KV_HEADS, dk=GQA_HEAD_DIM, dv=GQA_HEAD_DIM, k_blk0=0, v_blk0=0)
    o_a, lse_a = _attention_fwd(q_a, kv_a, kv_a, name="attn_a_fwd", **att_a)
    o_b, lse_b = _attention_fwd(q_b, k_b, v_b, name="attn_b_fwd", **att_b)
    _after(o_a)
    _after(gather_pass(g2, o_b))
    pa = _mm(o_a, w_bra, "nn", BF16, "br_a", tm=1024, tn=1024)
    pb = _mm(o_b, w_brb, "nn", BF16, "br_b", tm=1024, tn=1024)
    merged = _merge_fwd(pa, pb, qg, gate_blk)
    attn = _mm(merged, w_out_f, "nn", F32, "w_out", tm=1024, tn=1024)
    _after(gather_relay(g2, attn))
    x1, z2 = _resid_norm_mod(xs, attn, norm2_g, mods2, "resid_norm2_fwd")
    ffn_w = gather_finish(g2, z2)
    w_up3 = ffn_w["w_up"]
    w_down_f = ffn_w["w_down"].reshape(ff, d)
    u = _mm_up_fwd(z2, w_up3, "w_up")
    h, uc = _conv_fwd(u, conv_w_f, conv_b)
    ffn = _mm(h, w_down_f, "nn", F32, "w_down", tm=1024, tn=1024, tk=2816)

    def to_shards(g):
        return jnp.transpose(g.reshape(g.shape[0], N_DEV, g.shape[1] // N_DEV), (1, 0, 2))

    def reduce_start(tag, names, sends):
        n = len(sends)
        land = [lax.empty((4,) + s.shape[1:], s.dtype) for s in sends]
        s, r, arrs, tok = _split_start("reduce_d2d_start_" + tag, sends + land, _reduce_d2d_copies(n), 4 * n)
        return dict(tag=tag, names=names, s=s, r=r, arrs=arrs, tok=tok)

    def reduce_relay(g, after):
        n = len(g["names"])
        arrs = _split_wait("reduce_d2d_wait_" + g["tag"], g["s"], g["r"], g["arrs"], _reduce_d2d_copies(n), after)
        sums = [_pair_sum(arrs[a], arrs[n + a], c_idx, "pair_sum_" + g["names"][a]) for a in range(n)]
        land = [lax.empty(s.shape, s.dtype) for s in sums]
        s, r, arrs2, tok = _split_start("reduce_ici_start_" + g["tag"], sums + land, _reduce_ici_copies(n), 4 * n)
        g.update(s2=s, r2=r, arrs2=arrs2)
        return tok

    def reduce_finish(g, after):
        n = len(g["names"])
        arrs2 = _split_wait("reduce_ici_wait_" + g["tag"], g["s2"], g["r2"], g["arrs2"], _reduce_ici_copies(n), after)
        return dict(zip(g["names"], arrs2[n:]))

    dx2, dffn, st_fin = _final_loss(x1, ffn, final_norm_g[None, :], mods3, tgt)
    dh = _mm(dffn, w_down_f, "nt", BF16, "d_h", tm=1024, tn=1024)
    g_w_down = _mm(h, dffn, "tn", BF16, "g_w_down", tm=512, tn=1024)
    du3, dcw, dcb = _conv_bwd(u, uc, conv_w_f, dh)
    dz2 = _mm_up_dz(du3, w_up3, "d_z2")
    g_w_up = _mm_up_gw(z2, du3, N_DEV, "g_w_up")
    g_conv_w = jnp.concatenate([dcw[0], dcw[1]], axis=1)
    r_ffn = reduce_start("ffn", ["w_down", "w_up", "conv_w"],
                         [g_w_down.reshape(N_DEV, ff // N_DEV, d), g_w_up,
                          to_shards(jnp.pad(g_conv_w, ((0, 5), (0, 0))))])
    _after(r_ffn["tok"])
    dx1, dattn, st_n2 = _norm2_bwd(x1, attn, norm2_g, mods2b, dz2, dx2)
    dmerged = _mm(dattn, w_out_f, "nt", BF16, "d_merged", tm=1024, tn=1024)
    g_w_out = _mm(merged, dattn, "tn", BF16, "g_w_out", tm=1024, tn=1024)
    dpa, dpb, dgates = _merge_bwd(dmerged, pa, pb, qg, gate_blk)
    do_a = _mm(dpa, w_bra, "nt", BF16, "d_o_a", tm=1024, tn=1024)
    do_b = _mm(dpb, w_brb, "nt", BF16, "d_o_b", tm=1024, tn=1024)
    g_w_bra = _mm_tn_shards(o_a, dpa, N_DEV, "g_w_br_a")
    g_w_brb = _mm_tn_shards(o_b, dpb, N_DEV, "g_w_br_b")
    _after(reduce_relay(r_ffn, g_w_brb))
    dq_a, dk_a, dv_a = _attention_bwd(q_a, kv_a, kv_a, do_a, lse_a, name="attn_a_bwd", **att_a)
    dq_b, dk_b, dv_b = _attention_bwd(q_b, k_b, v_b, do_b, lse_b, name="attn_b_bwd", **att_b)
    dqa_raw = _rope_a(dq_a, q_tabs_a, True, BF16, "rope_q_bwd", sc_a * LN2)
    dcqn = _mm(dqa_raw, wqt_ext, "nn", F32, "d_cqn", tm=1024, tn=ql)
    g_wqt_ext = _mm(dqa_raw, cqn, "tn", BF16, "g_w_q_up", tm=1024, tn=ql)
    dq_p, st_q, st_qb = _q_prep_bwd(qg, mla_q_norm_g, gqa_q_norm_g, q_tabs_b, dcqn, dq_b, q_pad, sc_b * LN2)
    dkin = _mm_cat_nt([(dk_a, wkv_ext, 0), (dv_a, wkv_ext, ha * MLA_SLOT)], F32, "d_kin", tm=1152, tn=kvl + LANE)
    g_wkv_ext = _mm_cat_tn(kin, [dk_a, dv_a], BF16, "g_w_kv_up", tm=kvl + LANE, tn=min(1024, ha * MLA_V))
    dkv_p, st_kv, st_kb = _key_prep_bwd(kv_all, mla_kv_norm_g, gqa_k_norm_g, k_tabs, dkin, dk_b, dv_b)
    g_wqt = g_wqt_ext.reshape(ha, MLA_SLOT, ql)[:, :MLA_NOPE + MLA_ROPE, :].reshape(N_DEV, -1, ql)
    g_wkv = jnp.concatenate([g_wkv_ext[:kvl, :ha * MLA_SLOT].reshape(kvl, ha, MLA_SLOT)[:, :, :MLA_NOPE],
                             g_wkv_ext[:kvl, ha * MLA_SLOT:].reshape(kvl, ha, MLA_V)], axis=2).reshape(kvl, ha * (MLA_NOPE + MLA_V))
    r_mid = reduce_start("mid", ["w_out", "w_br_a", "w_br_b", "w_q_up", "w_kv_up"],
                         [g_w_out.reshape(N_DEV, d // N_DEV, d), g_w_bra, g_w_brb, g_wqt,
                          to_shards(g_wkv)])
    _after(r_mid["tok"])
    g_wkv_p = _mm(dkv_p, z_all, "tn", BF16, "g_w_in_kv", tm=wkv_w, tn=1024)
    g_wqg_p = _mm_rows_tn([dq_p, dgates], z_all, BF16, "g_w_in_qg", tm=min(1024, d), tn=1024, rows=t)
    tok_m = reduce_relay(r_mid, g_wqg_p)
    g_wt_in = jnp.concatenate([g_wkv_p[:kvl], g_wkv_p[kvl + 2 * nb:kvl + 2 * nb + MLA_ROPE],
                               g_wkv_p[kvl:kvl + 2 * nb], g_wqg_p[:q_w], g_wqg_p[q_w + q_pad:]], axis=0)
    _after(tok_m)
    r_in = reduce_start("in", ["w_in"], [g_wt_in.reshape(N_DEV, -1, d)])
    _after(r_in["tok"])
    qw_p = q_w + q_pad
    dz_lat = _mm_sum_nn([(dq_p, 0, wt_qg_p, 0, qw_p), (dgates, 0, wt_qg_p, qw_p, d), (dgates, d, wt_qg_p, qw_p + d, d),
                         (dkv_p, 0, wt_kv_p, 0, wkv_w)], F32, "d_z_lat", rows=t)
    dz_ctx = _mm(dkv_p, wt_kv_p, "nn", F32, "d_z_ctx", tm=min(ROW_BLOCK, tc), tn=1024, a_row_off=t)
    grad_x, st_n1 = _norm1_bwd(cts, xs, norm1_g, mods1, dz_ctx, dz_lat, dx1)

    res = {}

    def upd(nm, parts):
        wv, mv, vv = weights[nm], mom_m[nm], mom_v[nm]
        if wv.ndim == 1:
            wv, mv, vv = (a.reshape(1, -1) for a in (wv, mv, vv))
        if nm in narrow:
            wv, mv, vv = tview(wv), tview(mv), tview(vv)
        outs = _adamw(parts, wv, mv, vv, "adamw_" + nm)
        if nm in narrow:
            outs = [tview(o_) for o_ in outs]
        res[nm] = [o_.reshape(weights[nm].shape) for o_ in outs]

    d_lat = jnp.concatenate([st_n1[0], st_n1[1], st_n2[3], st_n2[0], st_n2[1], st_fin[1]])
    d_cxt = jnp.concatenate([st_n1[3], st_n1[4], jnp.zeros((4 * d,), F32)])
    small = jnp.concatenate([d_lat, d_cxt, st_n1[2], st_q[0], st_kv[0], st_qb[0], st_kb[0], st_n2[2],
                             jnp.concatenate([dcb[0, 0], dcb[1, 0]]), st_fin[0], st_fin[3, :LANE]])
    n_small = small.shape[0]
    pad_small = (-n_small) % LANE
    (small_all,) = _all_gather([jnp.pad(small, (0, pad_small)).reshape(1, -1)], "gather_small")
    offs = {}
    o = 0
    for nm, ln in (("d_lat", 6 * d), ("d_cxt", 6 * d), ("norm1_g", d), ("mla_q_norm_g", ql), ("mla_kv_norm_g", kvl),
                   ("gqa_q_norm_g", GQA_HEAD_DIM), ("gqa_k_norm_g", GQA_HEAD_DIM), ("norm2_g", d), ("conv_b", f2),
                   ("final_norm_g", d), ("loss", LANE)):
        offs[nm] = (o, ln)
        o += ln

    def part(nm):
        a, ln = offs[nm]
        return small_all[:, :, a:a + ln]

    loss = _sum_parts(part("loss"))[0, 0]
    d_lat_all = part("d_lat")[:, 0, :]
    d_cxt_sum = _sum_parts(part("d_cxt"))
    da16 = jnp.concatenate([d_lat_all, d_cxt_sum, jnp.zeros((7, 6 * d), F32)], axis=0)
    da16_shard = lax.dynamic_slice_in_dim(da16, my_idx * ncol, ncol, axis=1)
    cc_part = _cctx_partial(da16_shard, w_ada[0], c_ctx[None, :])
    (cc_all,) = _all_gather([cc_part], "gather_cctx")
    cc_parts = cc_all[:, 0:1, :]
    tok_i = reduce_relay(r_in, cc_all)

    _after(tok_i)
    for nm in ("norm1_g", "mla_q_norm_g", "mla_kv_norm_g", "gqa_q_norm_g", "gqa_k_norm_g", "norm2_g", "conv_b",
               "final_norm_g"):
        upd(nm, part(nm))
    upd("c_ctx", cc_parts)
    b_parts = jnp.concatenate([d_lat_all[:, None, :], d_cxt_sum[None]], axis=0)
    upd("b_ada", b_parts)
    _after(tok_i)
    outs = _adamw_ada(conds, da16_shard, w_ada[0], m_w_ada[0], v_w_ada[0])
    res["w_ada"] = [o_[None] for o_ in outs]
    last = outs[0]
    done = [last]
    for grp in (r_ffn, r_mid, r_in):
        _after(*done)
        recv = reduce_finish(grp, last)
        for nm in grp["names"]:
            upd(nm, recv[nm][:, :3, :] if nm == "conv_w" else recv[nm])
            last = res[nm][0]
            done.append(last)

    return (loss, grad_x[None], *[res[n][0] for n in order], *[res[n][1] for n in order],
            *[res[n][2] for n in order], *[res[n][3] for n in order])
```

```python
import jax
import jax.numpy as jnp
from jax import lax
from jax.experimental import pallas as pl
from jax.experimental.pallas import tpu as pltpu

F32 = jnp.float32
BF16 = jnp.bfloat16

GRID_W = 64
ROPE_THETA = 10000.0
NORM_EPS = 1e-6
MLA_HEADS = 8
MLA_Q_LORA = 768
MLA_KV_LORA = 512
MLA_NOPE = 128
MLA_ROPE = 64
MLA_V = 128
GQA_HEADS = 8
GQA_KV_HEADS = 2
GQA_HEAD_DIM = 128
ADAM_LR = 0.001
ADAM_B1 = 0.9
ADAM_B2 = 0.999
ADAM_EPS = 1e-08
ADAM_WD = 0.01
ADAM_STEP = 10

N_DEV = 8
LANE = 128
MLA_SLOT = 2 * LANE
VMEM_LIMIT = 56 * 1024 * 1024
ROW_BLOCK = 256
ATT_Q_BLOCK = 512
ATT_Q_BLOCK_FWD = 1024
LN2 = 0.6931471805599453
LOG2E = 1.4426950408889634
MESH_ID = pl.DeviceIdType.MESH


def _tile(n, pref, align=LANE):
    if n <= pref:
        return n
    best = None
    t = align
    while t <= pref:
        if n % t == 0:
            best = t
        t += align
    assert best is not None, (n, pref, align)
    return best


def _cparams(sem=None):
    return pltpu.CompilerParams(dimension_semantics=sem, vmem_limit_bytes=VMEM_LIMIT)


_ORDER_AFTER = []


def _after(*arrays):
    _ORDER_AFTER.extend(arrays)


def _pcall(body, *, in_specs, **kw):
    deps = tuple(_ORDER_AFTER)
    _ORDER_AFTER.clear()
    if not deps:
        return pl.pallas_call(body, in_specs=in_specs, **kw)
    n_in, n_dep = len(in_specs), len(deps)

    def with_deps(*refs):
        body(*refs[:n_in], *refs[n_in + n_dep:])

    call = pl.pallas_call(with_deps, in_specs=list(in_specs) + [pl.BlockSpec(memory_space=pl.ANY)] * n_dep, **kw)
    return lambda *args: call(*args, *deps)


def _all_gather(arrs, name):
    n = len(arrs)

    def body(*refs):
        ins = refs[:n]
        outs = refs[n:2 * n]
        send_sems, recv_sems, local_sems = refs[2 * n:]
        x, y, c = lax.axis_index("x"), lax.axis_index("y"), lax.axis_index("c")
        me, sibling = (x, y, c), (x, y, 1 - c)
        chips = [(1 - x, y), (x, 1 - y), (1 - x, 1 - y)]

        def rows(a, dev):
            px, py, pc = dev
            return outs[a].at[4 * px + 2 * py + pc]

        def copy(a, k, block, to, src=None):
            return pltpu.make_async_remote_copy(
                src_ref=rows(a, block) if src is None else src,
                dst_ref=rows(a, block),
                send_sem=send_sems.at[7 * a + k],
                recv_sem=recv_sems.at[7 * a + k],
                device_id=to,
                device_id_type=MESH_ID,
            )

        mine = [pltpu.make_async_copy(ins[a], rows(a, me), local_sems.at[a]) for a in range(n)]
        for cp in mine:
            cp.start()
        first = []
        for a in range(n):
            first.append(copy(a, 0, me, sibling, src=ins[a]))
            first += [copy(a, 1 + j, me, (*chip, c), src=ins[a]) for j, chip in enumerate(chips)]
        for cp in first:
            cp.start()
        passed = []
        for j, chip in enumerate(chips):
            for a in range(n):
                copy(a, 1 + j, (*chip, c), me).wait_recv()
                fwd = copy(a, 4 + j, (*chip, c), sibling)
                fwd.start()
                passed.append(fwd)
        for a in range(n):
            copy(a, 0, sibling, me).wait_recv()
            for j, chip in enumerate(chips):
                copy(a, 4 + j, (*chip, 1 - c), me).wait_recv()
        for cp in first + passed:
            cp.wait_send()
        for cp in mine:
            cp.wait()

    any_spec = pl.BlockSpec(memory_space=pl.ANY)
    outs = _pcall(
        body,
        name=name,
        out_shape=[jax.ShapeDtypeStruct((N_DEV,) + a.shape, a.dtype) for a in arrs],
        in_specs=[any_spec] * n,
        out_specs=[any_spec] * n,
        scratch_shapes=[
            pltpu.SemaphoreType.DMA((7 * n,)),
            pltpu.SemaphoreType.DMA((7 * n,)),
            pltpu.SemaphoreType.DMA((n,)),
        ],
    )(*arrs)
    return list(outs)


_HBM = pl.BlockSpec(memory_space=pltpu.HBM)
_SEM = pl.BlockSpec(memory_space=pltpu.SEMAPHORE)
_EFFECT = pltpu.SideEffectType.DATAFLOW_SIDE_EFFECTING


def _descriptors(copies, send_sems, recv_sems):
    descs = []
    for i, (src, dst, dev) in enumerate(copies):
        if dev is None:
            descs.append(pltpu.make_async_copy(src, dst, recv_sems.at[i]))
        else:
            descs.append(pltpu.make_async_remote_copy(src_ref=src, dst_ref=dst, send_sem=send_sems.at[i],
                                                      recv_sem=recv_sems.at[i], device_id=dev, device_id_type=MESH_ID))
    return descs


def _split_start(name, arrays, copies_fn, n_copies):
    n = len(arrays)

    def body(*refs):
        send_sems, recv_sems = refs[n], refs[n + 1]
        token = refs[2 * n + 2]
        for dsc in _descriptors(copies_fn(refs[:n]), send_sems, recv_sems):
            dsc.start()
        token[...] = jnp.zeros_like(token)

    outs = _pcall(
        body,
        name=name,
        out_shape=(pltpu.SemaphoreType.DMA((n_copies,)), pltpu.SemaphoreType.DMA((n_copies,)),
                   *[pltpu.HBM(a.shape, a.dtype) for a in arrays], jax.ShapeDtypeStruct((8, LANE), F32)),
        in_specs=[_HBM] * n,
        out_specs=(_SEM, _SEM, *[_HBM] * n, pl.BlockSpec(memory_space=pltpu.VMEM)),
        input_output_aliases={i: 2 + i for i in range(n)},
        compiler_params=pltpu.CompilerParams(has_side_effects=_EFFECT),
    )(*[pltpu.with_memory_space_constraint(a, pltpu.HBM) for a in arrays])
    return outs[0], outs[1], list(outs[2:2 + n]), outs[2 + n]


def _split_wait(name, send_sems, recv_sems, arrays, copies_fn, after):
    n = len(arrays)

    def body(*refs):
        for dsc, (_, _, dev) in zip(_descriptors(copies_fn(refs[:n]), refs[n], refs[n + 1]), copies_fn(refs[:n])):
            if dev is None:
                dsc.wait()
            else:
                dsc.wait_send()
                dsc.wait_recv()

    outs = _pcall(
        body,
        name=name,
        out_shape=tuple(pltpu.HBM(a.shape, a.dtype) for a in arrays),
        in_specs=[_HBM] * n + [_SEM, _SEM, pl.BlockSpec(memory_space=pl.ANY)],
        out_specs=tuple([_HBM] * n),
        input_output_aliases={i: i for i in range(n)},
        compiler_params=pltpu.CompilerParams(has_side_effects=_EFFECT),
    )(*arrays, send_sems, recv_sems, after)
    return list(outs)


def _mesh_pos():
    x, y, c = lax.axis_index("x"), lax.axis_index("y"), lax.axis_index("c")
    return x, y, c, [(1 - x, y), (x, 1 - y), (1 - x, 1 - y)]


def _gather_ici_copies(n):
    def copies(refs):
        x, y, c, chips = _mesh_pos()
        me = 4 * x + 2 * y + c
        out = []
        for a in range(n):
            src, buf = refs[a], refs[n + a]
            out.append((src, buf.at[me], None))
            out.append((src, buf.at[me], (x, y, 1 - c)))
            out += [(src, buf.at[me], (cx, cy, c)) for cx, cy in chips[:2]]
        return out
    return copies


def _gather_pass_copies(n):
    def copies(refs):
        x, y, c, chips = _mesh_pos()
        south = c == 0
        bx, by = jnp.where(south, 1 - x, x), jnp.where(south, y, 1 - y)
        tx, ty = jnp.where(south, x, 1 - x), jnp.where(south, 1 - y, y)
        out = []
        for a in range(n):
            rows = refs[a].at[4 * bx + 2 * by + c]
            out.append((rows, rows, (tx, ty, c)))
            for cx, cy in chips[:2]:
                rows = refs[a].at[4 * cx + 2 * cy + c]
                out.append((rows, rows, (x, y, 1 - c)))
        return out
    return copies


def _gather_d2d_copies(n):
    def copies(refs):
        x, y, c, chips = _mesh_pos()
        cx, cy = chips[2]
        out = []
        for a in range(n):
            rows = refs[a].at[4 * cx + 2 * cy + c]
            out.append((rows, rows, (x, y, 1 - c)))
        return out
    return copies


def _reduce_d2d_copies(n):
    def copies(refs):
        x, y, c, _ = _mesh_pos()
        out = []
        for a in range(n):
            for k in range(4):
                out.append((refs[a].at[2 * k + (1 - c)], refs[n + a].at[k], (x, y, 1 - c)))
        return out
    return copies


def _reduce_ici_copies(n):
    def copies(refs):
        x, y, c, chips = _mesh_pos()
        mine = 2 * x + y
        out = []
        for a in range(n):
            src, land = refs[a], refs[n + a]
            out.append((src.at[mine], land.at[mine], None))
            out += [(src.at[2 * cx + cy], land.at[mine], (cx, cy, c)) for cx, cy in chips]
        return out
    return copies


def _pair_sum(send, land, c_idx, name):
    _, r, cols = send.shape
    rb = _tile(r, max(8, (1 << 22) // (send.dtype.itemsize * cols) // 8 * 8), 8)
    dt = send.dtype

    def body(c_ref, s_ref, l_ref, o_ref):
        o_ref[...] = (s_ref[...].astype(F32) + l_ref[...].astype(F32)).astype(dt)

    return pl.pallas_call(
        body,
        name=name,
        out_shape=jax.ShapeDtypeStruct((4, r, cols), dt),
        grid_spec=pltpu.PrefetchScalarGridSpec(
            num_scalar_prefetch=1,
            grid=(4, r // rb),
            in_specs=[pl.BlockSpec((None, rb, cols), lambda k, i, c_ref: (2 * k + c_ref[0], i, 0)),
                      pl.BlockSpec((None, rb, cols), lambda k, i, c_ref: (k, i, 0))],
            out_specs=pl.BlockSpec((None, rb, cols), lambda k, i, c_ref: (k, i, 0)),
        ),
        compiler_params=_cparams(("parallel", "parallel")),
    )(c_idx, send, land)


_DIMS = {
    "nn": (((1,), (0,)), ((), ())),
    "nt": (((1,), (1,)), ((), ())),
    "tn": (((0,), (0,)), ((), ())),
}


def _mm_call(a, b, *, mode, grid, a_spec, b_spec, o_spec, out_shape, acc_shape, name):
    nk = grid[2]
    out_dtype = out_shape.dtype

    def body(a_ref, b_ref, o_ref, *scratch):
        p = lax.dot_general(a_ref[...].astype(BF16), b_ref[...].astype(BF16), _DIMS[mode],
                            preferred_element_type=F32)
        if nk == 1:
            o_ref[...] = p.astype(out_dtype)
        else:
            acc = scratch[0]
            k = pl.program_id(2)

            @pl.when(k == 0)
            def _():
                acc[...] = p

            @pl.when(k > 0)
            def _():
                acc[...] += p

            @pl.when(k == nk - 1)
            def _():
                o_ref[...] = acc[...].astype(out_dtype)

    return _pcall(
        body,
        name=name,
        out_shape=out_shape,
        grid=grid,
        in_specs=[a_spec, b_spec],
        out_specs=o_spec,
        scratch_shapes=[pltpu.VMEM(acc_shape, F32)] if nk > 1 else [],
        compiler_params=_cparams(("parallel", "parallel", "arbitrary")),
    )(a, b)


def _mm(a, b, mode, out_dtype, name, tm=512, tn=512, tk=2432, a_row_off=0, rows=None):
    if mode == "nn":
        (m, k), (k2, n) = a.shape, b.shape
    elif mode == "nt":
        (m, k), (n, k2) = a.shape, b.shape
    else:
        (k, m), (k2, n) = a.shape, b.shape
        if rows is not None:
            k = k2 = rows
    assert k == k2, (a.shape, b.shape, mode)
    if mode != "tn":
        m = (m if rows is None else rows + a_row_off) - a_row_off
    tm, tn, tk = _tile(m, tm, 8), _tile(n, tn), _tile(k, tk, 8 if mode == "tn" else LANE)
    assert a_row_off % tm == 0
    ro = a_row_off // tm
    grid = (m // tm, n // tn, k // tk)
    if mode == "tn":
        a_spec = pl.BlockSpec((tk, tm), lambda i, j, kk: (kk, i))
    else:
        a_spec = pl.BlockSpec((tm, tk), lambda i, j, kk: (i + ro, kk))
    if mode == "nt":
        b_spec = pl.BlockSpec((tn, tk), lambda i, j, kk: (j, kk))
    else:
        b_spec = pl.BlockSpec((tk, tn), lambda i, j, kk: (kk, j))
    o_spec = pl.BlockSpec((tm, tn), lambda i, j, kk: (i, j))
    return _mm_call(a, b, mode=mode, grid=grid, a_spec=a_spec, b_spec=b_spec, o_spec=o_spec,
                    out_shape=jax.ShapeDtypeStruct((m, n), out_dtype), acc_shape=(tm, tn), name=name)


def _mm_cat_nt(pieces, out_dtype, name, tm=1024, tn=1024, tk=2048, rows=None):
    m = pieces[0][0].shape[0] if rows is None else rows
    n = pieces[0][1].shape[0]
    tm, tn = _tile(m, tm, 8), _tile(n, tn)
    steps, starts, s = [], [], 0
    for a, b, off in pieces:
        kp = a.shape[1]
        tkp = _tile(kp, tk)
        assert off % tkp == 0 and b.shape[0] == n
        steps.append((tkp, kp // tkp, off // tkp))
        starts.append(s)
        s += kp // tkp
    nk = s
    npc = len(pieces)

    def body(*refs):
        o_ref, acc = refs[2 * npc], refs[2 * npc + 1]
        kk = pl.program_id(2)

        @pl.when(kk == 0)
        def _():
            acc[...] = jnp.zeros_like(acc)

        for p in range(npc):
            @pl.when((kk >= starts[p]) & (kk < starts[p] + steps[p][1]))
            def _(p=p):
                acc[...] += lax.dot_general(refs[2 * p][...].astype(BF16), refs[2 * p + 1][...].astype(BF16), _DIMS["nt"],
                                            preferred_element_type=F32)

        @pl.when(kk == nk - 1)
        def _():
            o_ref[...] = acc[...].astype(out_dtype)

    in_specs, args = [], []
    for p, (a, b, off) in enumerate(pieces):
        tkp, np_, ob = steps[p]

        def rel(kk, p=p, np_=np_):
            return jnp.clip(kk - starts[p], 0, np_ - 1)

        in_specs.append(pl.BlockSpec((tm, tkp), lambda i, j, kk, rel=rel: (i, rel(kk))))
        in_specs.append(pl.BlockSpec((tn, tkp), lambda i, j, kk, rel=rel, ob=ob: (j, ob + rel(kk))))
        args += [a, b]
    return _pcall(
        body,
        name=name,
        out_shape=jax.ShapeDtypeStruct((m, n), out_dtype),
        grid=(m // tm, n // tn, nk),
        in_specs=in_specs,
        out_specs=pl.BlockSpec((tm, tn), lambda i, j, kk: (i, j)),
        scratch_shapes=[pltpu.VMEM((tm, tn), F32)],
        compiler_params=_cparams(("parallel", "parallel", "arbitrary")),
    )(*args)


def _mm_cat_tn(a, pieces, out_dtype, name, tm=1024, tn=1024, rows=None):
    k = a.shape[0] if rows is None else rows
    m = a.shape[1]
    tm = _tile(m, tm)
    starts, s = [], 0
    for b in pieces:
        assert b.shape[1] % tn == 0
        starts.append(s)
        s += b.shape[1] // tn
    nj = s
    npc = len(pieces)

    def body(*refs):
        a_ref, o_ref = refs[0], refs[1 + npc]
        j = pl.program_id(1)
        for p in range(npc):
            @pl.when((j >= starts[p]) & (j < starts[p] + pieces[p].shape[1] // tn))
            def _(p=p):
                o_ref[...] = lax.dot_general(a_ref[...].astype(BF16), refs[1 + p][...].astype(BF16), _DIMS["tn"],
                                             preferred_element_type=F32).astype(out_dtype)

    in_specs = [pl.BlockSpec((k, tm), lambda i, j: (0, i))]
    for p, b in enumerate(pieces):
        np_ = b.shape[1] // tn
        in_specs.append(pl.BlockSpec((k, tn), lambda i, j, p=p, np_=np_: (0, jnp.clip(j - starts[p], 0, np_ - 1))))
    return _pcall(
        body,
        name=name,
        out_shape=jax.ShapeDtypeStruct((m, nj * tn), out_dtype),
        grid=(m // tm, nj),
        in_specs=in_specs,
        out_specs=pl.BlockSpec((tm, tn), lambda i, j: (i, j)),
        compiler_params=_cparams(("parallel", "arbitrary")),
    )(a, *pieces)


def _mm_up_fwd(z2, w3, name, tm=1024):
    t, d = z2.shape
    nsh, _, c = w3.shape
    tm = _tile(t, tm, 8)
    return _mm_call(z2, w3, mode="nn", grid=(t // tm, nsh, 1),
                    a_spec=pl.BlockSpec((tm, d), lambda i, j, kk: (i, 0)),
                    b_spec=pl.BlockSpec((None, d, c), lambda i, j, kk: (j, 0, 0)),
                    o_spec=pl.BlockSpec((tm, c), lambda i, j, kk: (i, j)),
                    out_shape=jax.ShapeDtypeStruct((t, nsh * c), BF16), acc_shape=(tm, c), name=name)


def _mm_up_dz(du3, w3, name, tm=512, tn=1024):
    _, t, f = du3.shape
    nsh, d, c = w3.shape
    half = nsh // 2
    assert f == half * c
    tm, tn = _tile(t, tm, 8), _tile(d, tn)

    def body(a_ref, b_ref, o_ref, acc):
        kk = pl.program_id(2)
        p = None
        for s in range(half):
            q = lax.dot_general(a_ref[:, s * c:(s + 1) * c], b_ref[s], _DIMS["nt"], preferred_element_type=F32)
            p = q if p is None else p + q

        @pl.when(kk == 0)
        def _():
            acc[...] = p

        @pl.when(kk == 1)
        def _():
            o_ref[...] = (acc[...] + p).astype(BF16)

    return _pcall(
        body,
        name=name,
        out_shape=jax.ShapeDtypeStruct((t, d), BF16),
        grid=(t // tm, d // tn, 2),
        in_specs=[pl.BlockSpec((None, tm, f), lambda i, j, kk: (kk, i, 0)),
                  pl.BlockSpec((half, tn, c), lambda i, j, kk: (kk, j, 0))],
        out_specs=pl.BlockSpec((tm, tn), lambda i, j, kk: (i, j)),
        scratch_shapes=[pltpu.VMEM((tm, tn), F32)],
        compiler_params=_cparams(("parallel", "parallel", "arbitrary")),
    )(du3, w3)


def _mm_sum_nn(pieces, out_dtype, name, tm=512, tn=512, rows=None):
    m = pieces[0][0].shape[0] if rows is None else rows
    n = pieces[0][2].shape[1]
    tm, tn = _tile(m, tm, 8), _tile(n, tn)
    npc = len(pieces)

    def body(*refs):
        p = None
        for s in range(npc):
            q = jnp.dot(refs[2 * s][...].astype(BF16), refs[2 * s + 1][...].astype(BF16), preferred_element_type=F32)
            p = q if p is None else p + q
        refs[2 * npc][...] = p.astype(out_dtype)

    in_specs, args = [], []
    for a, ao, b, bo, kp in pieces:
        assert ao % kp == 0 and bo % kp == 0 and b.shape[1] == n
        in_specs.append(pl.BlockSpec((tm, kp), lambda i, j, ab=ao // kp: (i, ab)))
        in_specs.append(pl.BlockSpec((kp, tn), lambda i, j, bb=bo // kp: (bb, j)))
        args += [a, b]
    return _pcall(
        body,
        name=name,
        out_shape=jax.ShapeDtypeStruct((m, n), out_dtype),
        grid=(m // tm, n // tn),
        in_specs=in_specs,
        out_specs=pl.BlockSpec((tm, tn), lambda i, j: (i, j)),
        compiler_params=_cparams(("parallel", "parallel")),
    )(*args)


def _mm_rows_tn(pieces, b, out_dtype, name, tm=1024, tn=1024, rows=None):
    k = b.shape[0] if rows is None else rows
    n = b.shape[1]
    tn = _tile(n, tn)
    starts, s = [], 0
    for a in pieces:
        assert a.shape[1] % tm == 0
        starts.append(s)
        s += a.shape[1] // tm
    ni = s
    npc = len(pieces)

    def body(*refs):
        b_ref, o_ref = refs[npc], refs[npc + 1]
        i = pl.program_id(0)
        for p in range(npc):
            @pl.when((i >= starts[p]) & (i < starts[p] + pieces[p].shape[1] // tm))
            def _(p=p):
                o_ref[...] = lax.dot_general(refs[p][...].astype(BF16), b_ref[...].astype(BF16), _DIMS["tn"],
                                             preferred_element_type=F32).astype(out_dtype)

    in_specs = []
    for p, a in enumerate(pieces):
        np_ = a.shape[1] // tm
        in_specs.append(pl.BlockSpec((k, tm), lambda i, j, p=p, np_=np_: (0, jnp.clip(i - starts[p], 0, np_ - 1))))
    in_specs.append(pl.BlockSpec((k, tn), lambda i, j: (0, j)))
    return _pcall(
        body,
        name=name,
        out_shape=jax.ShapeDtypeStruct((ni * tm, n), out_dtype),
        grid=(ni, n // tn),
        in_specs=in_specs,
        out_specs=pl.BlockSpec((tm, tn), lambda i, j: (i, j)),
        compiler_params=_cparams(("parallel", "parallel")),
    )(*pieces, b)


def _mm_tn_shards(a, b, nsh, name):
    k, m = a.shape
    n = b.shape[1]
    c = n // nsh
    return _mm_call(a, b, mode="tn", grid=(1, nsh, 1),
                    a_spec=pl.BlockSpec((k, m), lambda i, j, kk: (0, 0)),
                    b_spec=pl.BlockSpec((k, c), lambda i, j, kk: (0, j)),
                    o_spec=pl.BlockSpec((None, m, c), lambda i, j, kk: (j, 0, 0)),
                    out_shape=jax.ShapeDtypeStruct((nsh, m, c), BF16), acc_shape=(m, c), name=name)


def _mm_up_gw(z2, du3, nsh, name, tm=1024):
    t, d = z2.shape
    f = du3.shape[2]
    half = nsh // 2
    c = f // half
    tm = _tile(d, tm)
    return _mm_call(z2, du3, mode="tn", grid=(d // tm, nsh, 1),
                    a_spec=pl.BlockSpec((t, tm), lambda i, j, kk: (0, i)),
                    b_spec=pl.BlockSpec((None, t, c), lambda i, j, kk: (j // half, 0, j % half)),
                    o_spec=pl.BlockSpec((None, tm, c), lambda i, j, kk: (j, i, 0)),
                    out_shape=jax.ShapeDtypeStruct((nsh, d, c), BF16), acc_shape=(tm, c), name=name)


def _rms(x):
    r = lax.rsqrt(jnp.mean(x * x, axis=-1, keepdims=True) + NORM_EPS)
    return x * r, r


def _rms_bwd(dxh, xh, r):
    return r * (dxh - xh * jnp.mean(dxh * xh, axis=-1, keepdims=True))


def _colsum(v):
    return jnp.sum(v, axis=0, keepdims=True)


def _rope(v, c, s1, s2, q):
    w = v.shape[-1]
    return v * c + pltpu.roll(v, w - q, 1) * s1 + pltpu.roll(v, q, 1) * s2


def _rope_t(d, c, s1, s2, q):
    w = d.shape[-1]
    return d * c + pltpu.roll(d * s1, q, 1) + pltpu.roll(d * s2, w - q, 1)


def _norm_mod_fwd(ctx, x, gain, mods):
    tc, d = ctx.shape
    t = x.shape[0]
    rb = min(ROW_BLOCK, tc)
    nbl = t // rb

    def body(ctx_ref, x_ref, g_ref, mod_ref, z_ref):
        i = pl.program_id(0)

        def emit(src, sh, sc):
            xh, _ = _rms(src[...])
            z_ref[...] = ((xh * g_ref[...]) * (1.0 + sc) + sh).astype(BF16)

        @pl.when(i >= nbl)
        def _():
            emit(ctx_ref, mod_ref[2:3, :], mod_ref[3:4, :])

        @pl.when(i < nbl)
        def _():
            emit(x_ref, mod_ref[0:1, :], mod_ref[1:2, :])

    return _pcall(
        body,
        name="norm1_mod_fwd",
        out_shape=jax.ShapeDtypeStruct((tc + t, d), BF16),
        grid=((tc + t) // rb,),
        in_specs=[
            pl.BlockSpec((rb, d), lambda i: (jnp.maximum(i - nbl, 0), 0)),
            pl.BlockSpec((rb, d), lambda i: (jnp.minimum(i, nbl - 1), 0)),
            pl.BlockSpec((1, d), lambda i: (0, 0)),
            pl.BlockSpec((8, d), lambda i: (0, 0)),
        ],
        out_specs=pl.BlockSpec((rb, d), lambda i: (i, 0)),
        compiler_params=_cparams(("arbitrary",)),
    )(ctx, x, gain, mods)


def _norm1_bwd(ctx, x, gain, mods, dz_ctx, dz_lat, dx1):
    tc, d = ctx.shape
    t = x.shape[0]
    rb = min(ROW_BLOCK, tc)
    nbl = t // rb

    def body(ctx_ref, x_ref, g_ref, mod_ref, dzc_ref, dzl_ref, dx1_ref, gx_ref, st_ref):
        i = pl.program_id(0)

        @pl.when(i == 0)
        def _():
            st_ref[...] = jnp.zeros_like(st_ref)

        def common(src, dz, sc, row_sh, row_sc):
            xh, r = _rms(src[...])
            g = g_ref[...]
            dxn = dz * (1.0 + sc)
            st_ref[row_sh:row_sh + 1, :] += _colsum(dz)
            st_ref[row_sc:row_sc + 1, :] += _colsum(dz * (xh * g))
            st_ref[2:3, :] += _colsum(dxn * xh)
            return _rms_bwd(dxn * g, xh, r)

        @pl.when(i >= nbl)
        def _():
            common(ctx_ref, dzc_ref[...], mod_ref[3:4, :], 3, 4)

        @pl.when(i < nbl)
        def _():
            gx_ref[...] = dx1_ref[...] + common(x_ref, dzl_ref[...], mod_ref[1:2, :], 0, 1)

    lat = lambda i: (jnp.minimum(i, nbl - 1), 0)
    cix = lambda i: (jnp.maximum(i - nbl, 0), 0)
    return _pcall(
        body,
        name="norm1_mod_bwd",
        out_shape=[jax.ShapeDtypeStruct((t, d), F32), jax.ShapeDtypeStruct((8, d), F32)],
        grid=((tc + t) // rb,),
        in_specs=[
            pl.BlockSpec((rb, d), cix),
            pl.BlockSpec((rb, d), lat),
            pl.BlockSpec((1, d), lambda i: (0, 0)),
            pl.BlockSpec((8, d), lambda i: (0, 0)),
            pl.BlockSpec((rb, d), cix),
            pl.BlockSpec((rb, d), lat),
            pl.BlockSpec((rb, d), lat),
        ],
        out_specs=[pl.BlockSpec((rb, d), lat), pl.BlockSpec((8, d), lambda i: (0, 0))],
        compiler_params=_cparams(("arbitrary",)),
    )(ctx, x, gain, mods, dz_ctx, dz_lat, dx1)


def _key_prep_fwd(kv, kv_gain, kb_gain, tabs):
    ta, wkv = kv.shape
    kvl = MLA_KV_LORA
    nb = GQA_KV_HEADS * GQA_HEAD_DIM
    rb = ROW_BLOCK if ta % ROW_BLOCK == 0 else LANE
    hd = GQA_HEAD_DIM

    def body(kv_ref, g_ref, gb_ref, ca, s1a, s2a, cb, s1b, s2b, kin_ref, kb_ref, vb_ref):
        xh, _ = _rms(kv_ref[:, 0:kvl])
        kin_ref[:, 0:kvl] = (xh * g_ref[...]).astype(BF16)
        kpe = kv_ref[:, kvl + 2 * nb:kvl + 2 * nb + LANE]
        kin_ref[:, kvl:kvl + LANE] = _rope(kpe, ca[...], s1a[...], s2a[...], MLA_ROPE // 4).astype(BF16)
        for h in range(GQA_KV_HEADS):
            nh, _ = _rms(kv_ref[:, kvl + h * hd:kvl + (h + 1) * hd])
            kb_ref[:, h * hd:(h + 1) * hd] = _rope(nh * gb_ref[...], cb[...], s1b[...], s2b[...], hd // 4).astype(BF16)
        vb_ref[...] = kv_ref[:, kvl + nb:kvl + 2 * nb].astype(BF16)

    row = lambda w: pl.BlockSpec((rb, w), lambda i: (i, 0))
    fix = lambda w: pl.BlockSpec((1, w), lambda i: (0, 0))
    return _pcall(
        body,
        name="key_prep_fwd",
        out_shape=[jax.ShapeDtypeStruct((ta, kvl + LANE), BF16), jax.ShapeDtypeStruct((ta, nb), BF16),
                   jax.ShapeDtypeStruct((ta, nb), BF16)],
        grid=(ta // rb,),
        in_specs=[row(wkv), fix(kvl), fix(hd)] + [row(LANE)] * 3 + [row(hd)] * 3,
        out_specs=[row(kvl + LANE), row(nb), row(nb)],
        compiler_params=_cparams(("parallel",)),
    )(kv, kv_gain, kb_gain, *tabs)


def _key_prep_bwd(kv, kv_gain, kb_gain, tabs, dkin, dkb, dvb):
    ta, wkv = kv.shape
    kvl = MLA_KV_LORA
    nb = GQA_KV_HEADS * GQA_HEAD_DIM
    rb = ROW_BLOCK if ta % ROW_BLOCK == 0 else LANE
    hd = GQA_HEAD_DIM

    def body(kv_ref, g_ref, gb_ref, ca, s1a, s2a, cb, s1b, s2b, dkin_ref, dkb_ref, dvb_ref, dkv_ref, st_ref, stb_ref):
        @pl.when(pl.program_id(0) == 0)
        def _():
            st_ref[...] = jnp.zeros_like(st_ref)
            stb_ref[...] = jnp.zeros_like(stb_ref)

        xh, r = _rms(kv_ref[:, 0:kvl])
        dn = dkin_ref[:, 0:kvl]
        st_ref[0:1, :] += _colsum(dn * xh)
        dkv_ref[:, 0:kvl] = _rms_bwd(dn * g_ref[...], xh, r).astype(BF16)
        dpe = _rope_t(dkin_ref[:, kvl:kvl + LANE], ca[...], s1a[...], s2a[...], MLA_ROPE // 4)
        dkv_ref[:, kvl + 2 * nb:kvl + 2 * nb + LANE] = dpe.astype(BF16)
        for h in range(GQA_KV_HEADS):
            nh, rh = _rms(kv_ref[:, kvl + h * hd:kvl + (h + 1) * hd])
            dn_h = _rope_t(dkb_ref[:, h * hd:(h + 1) * hd], cb[...], s1b[...], s2b[...], hd // 4)
            stb_ref[0:1, :] += _colsum(dn_h * nh)
            dkv_ref[:, kvl + h * hd:kvl + (h + 1) * hd] = _rms_bwd(dn_h * gb_ref[...], nh, rh).astype(BF16)
        dkv_ref[:, kvl + nb:kvl + 2 * nb] = dvb_ref[...].astype(BF16)

    row = lambda w: pl.BlockSpec((rb, w), lambda i: (i, 0))
    fix = lambda w: pl.BlockSpec((1, w), lambda i: (0, 0))
    return _pcall(
        body,
        name="key_prep_bwd",
        out_shape=[jax.ShapeDtypeStruct((ta, wkv), BF16), jax.ShapeDtypeStruct((8, kvl), F32),
                   jax.ShapeDtypeStruct((8, hd), F32)],
        grid=(ta // rb,),
        in_specs=[row(wkv), fix(kvl), fix(hd)] + [row(LANE)] * 3 + [row(hd)] * 3 + [row(kvl + LANE), row(nb), row(nb)],
        out_specs=[row(wkv), pl.BlockSpec((8, kvl), lambda i: (0, 0)), pl.BlockSpec((8, hd), lambda i: (0, 0))],
        compiler_params=_cparams(("arbitrary",)),
    )(kv, kv_gain, kb_gain, *tabs, dkin, dkb, dvb)


def _q_prep_fwd(qg, q_gain, qb_gain, tabs, qscale):
    t = qg.shape[0]
    ql = MLA_Q_LORA
    hd = GQA_HEAD_DIM
    hb = GQA_HEADS * hd
    rb = min(ROW_BLOCK, t)

    def body(q_ref, g_ref, gb_ref, cb, s1b, s2b, cqn_ref, qb_ref):
        xh, _ = _rms(q_ref[:, 0:ql])
        cqn_ref[...] = (xh * g_ref[...]).astype(BF16)
        for h in range(GQA_HEADS):
            nh, _ = _rms(q_ref[:, ql + h * hd:ql + (h + 1) * hd])
            qh = _rope(nh * gb_ref[...], cb[...], s1b[...], s2b[...], hd // 4)
            qb_ref[:, h * hd:(h + 1) * hd] = (qh * qscale).astype(BF16)

    row = lambda w: pl.BlockSpec((rb, w), lambda i: (i, 0))
    fix = lambda w: pl.BlockSpec((1, w), lambda i: (0, 0))
    return _pcall(
        body,
        name="q_prep_fwd",
        out_shape=[jax.ShapeDtypeStruct((t, ql), BF16), jax.ShapeDtypeStruct((t, hb), BF16)],
        grid=(t // rb,),
        in_specs=[row(ql + hb), fix(ql), fix(hd)] + [row(hd)] * 3,
        out_specs=[row(ql), row(hb)],
        compiler_params=_cparams(("parallel",)),
    )(qg, q_gain, qb_gain, *tabs)


def _q_prep_bwd(qg, q_gain, qb_gain, tabs, dcqn, dqb, wpad, qscale):
    t = qg.shape[0]
    ql = MLA_Q_LORA
    hd = GQA_HEAD_DIM
    hb = GQA_HEADS * hd
    rb = min(ROW_BLOCK, t)

    def body(q_ref, g_ref, gb_ref, cb, s1b, s2b, dcqn_ref, dqb_ref, dq_ref, st_ref, stb_ref):
        @pl.when(pl.program_id(0) == 0)
        def _():
            st_ref[...] = jnp.zeros_like(st_ref)
            stb_ref[...] = jnp.zeros_like(stb_ref)

        xh, r = _rms(q_ref[:, 0:ql])
        dn = dcqn_ref[...]
        st_ref[0:1, :] += _colsum(dn * xh)
        dq_ref[:, 0:ql] = _rms_bwd(dn * g_ref[...], xh, r).astype(BF16)
        for h in range(GQA_HEADS):
            nh, rh = _rms(q_ref[:, ql + h * hd:ql + (h + 1) * hd])
            dn_h = _rope_t(dqb_ref[:, h * hd:(h + 1) * hd] * qscale, cb[...], s1b[...], s2b[...], hd // 4)
            stb_ref[0:1, :] += _colsum(dn_h * nh)
            dq_ref[:, ql + h * hd:ql + (h + 1) * hd] = _rms_bwd(dn_h * gb_ref[...], nh, rh).astype(BF16)
        if wpad:
            dq_ref[:, ql + hb:ql + hb + wpad] = jnp.zeros((rb, wpad), BF16)

    row = lambda w: pl.BlockSpec((rb, w), lambda i: (i, 0))
    fix = lambda w: pl.BlockSpec((1, w), lambda i: (0, 0))
    return _pcall(
        body,
        name="q_prep_bwd",
        out_shape=[jax.ShapeDtypeStruct((t, ql + hb + wpad), BF16), jax.ShapeDtypeStruct((8, ql), F32),
                   jax.ShapeDtypeStruct((8, hd), F32)],
        grid=(t // rb,),
        in_specs=[row(ql + hb), fix(ql), fix(hd)] + [row(hd)] * 3 + [row(ql), row(hb)],
        out_specs=[row(ql + hb + wpad), pl.BlockSpec((8, ql), lambda i: (0, 0)), pl.BlockSpec((8, hd), lambda i: (0, 0))],
        compiler_params=_cparams(("arbitrary",)),
    )(qg, q_gain, qb_gain, *tabs, dcqn, dqb)


def _rope_a(v, tabs, transpose, out_dtype, name, qscale):
    t, w = v.shape
    rb = min(ROW_BLOCK, t)
    fn = _rope_t if transpose else _rope

    def body(v_ref, c, s1, s2, o_ref):
        for h in range(w // MLA_SLOT):
            sl = slice(h * MLA_SLOT, (h + 1) * MLA_SLOT)
            o_ref[:, sl] = (fn(v_ref[:, sl].astype(F32), c[...], s1[...], s2[...], MLA_ROPE // 4) * qscale).astype(out_dtype)

    row = lambda ww: pl.BlockSpec((rb, ww), lambda i: (i, 0))
    return _pcall(
        body,
        name=name,
        out_shape=jax.ShapeDtypeStruct((t, w), out_dtype),
        grid=(t // rb,),
        in_specs=[row(w)] + [row(MLA_SLOT)] * 3,
        out_specs=row(w),
        compiler_params=_cparams(("parallel",)),
    )(v, *tabs)


def _merge_fwd(pa, pb, qg, gate_blk):
    t, d = pa.shape
    rb = min(ROW_BLOCK, t)

    def body(pa_ref, pb_ref, ga_ref, gb_ref, o_ref):
        o_ref[...] = (jax.nn.sigmoid(ga_ref[...]) * pa_ref[...].astype(F32)
                      + jax.nn.sigmoid(gb_ref[...]) * pb_ref[...].astype(F32)).astype(BF16)

    row = pl.BlockSpec((rb, d), lambda i: (i, 0))
    return _pcall(
        body,
        name="merge_fwd",
        out_shape=jax.ShapeDtypeStruct((t, d), BF16),
        grid=(t // rb,),
        in_specs=[row, row, pl.BlockSpec((rb, d), lambda i: (i, gate_blk)), pl.BlockSpec((rb, d), lambda i: (i, gate_blk + 1))],
        out_specs=row,
        compiler_params=_cparams(("parallel",)),
    )(pa, pb, qg, qg)


def _merge_bwd(dm, pa, pb, qg, gate_blk):
    t, d = pa.shape
    rb = min(ROW_BLOCK, t)

    def body(dm_ref, pa_ref, pb_ref, ga_ref, gb_ref, dpa_ref, dpb_ref, dg_ref):
        dmv = dm_ref[...].astype(F32)
        sa = jax.nn.sigmoid(ga_ref[...])
        sb = jax.nn.sigmoid(gb_ref[...])
        dpa_ref[...] = (dmv * sa).astype(BF16)
        dpb_ref[...] = (dmv * sb).astype(BF16)
        dg_ref[:, 0:d] = (dmv * pa_ref[...].astype(F32) * (sa * (1.0 - sa))).astype(BF16)
        dg_ref[:, d:2 * d] = (dmv * pb_ref[...].astype(F32) * (sb * (1.0 - sb))).astype(BF16)

    row = pl.BlockSpec((rb, d), lambda i: (i, 0))
    return _pcall(
        body,
        name="merge_bwd",
        out_shape=[jax.ShapeDtypeStruct((t, d), BF16), jax.ShapeDtypeStruct((t, d), BF16),
                   jax.ShapeDtypeStruct((t, 2 * d), BF16)],
        grid=(t // rb,),
        in_specs=[row, row, row, pl.BlockSpec((rb, d), lambda i: (i, gate_blk)), pl.BlockSpec((rb, d), lambda i: (i, gate_blk + 1))],
        out_specs=[row, row, pl.BlockSpec((rb, 2 * d), lambda i: (i, 0))],
        compiler_params=_cparams(("parallel",)),
    )(dm, pa, pb, qg, qg)


def _resid_norm_mod(x, branch, gain, mods, name):
    t, d = x.shape
    rb = min(ROW_BLOCK, t)

    def body(x_ref, b_ref, g_ref, mod_ref, x1_ref, z_ref):
        x1 = x_ref[...] + mod_ref[0:1, :] * b_ref[...]
        x1_ref[...] = x1
        xh, _ = _rms(x1)
        z_ref[...] = ((xh * g_ref[...]) * (1.0 + mod_ref[2:3, :]) + mod_ref[1:2, :]).astype(BF16)

    row = pl.BlockSpec((rb, d), lambda i: (i, 0))
    return _pcall(
        body,
        name=name,
        out_shape=[jax.ShapeDtypeStruct((t, d), F32), jax.ShapeDtypeStruct((t, d), BF16)],
        grid=(t // rb,),
        in_specs=[row, row, pl.BlockSpec((1, d), lambda i: (0, 0)), pl.BlockSpec((8, d), lambda i: (0, 0))],
        out_specs=[row, row],
        compiler_params=_cparams(("parallel",)),
    )(x, branch, gain, mods)


def _norm2_bwd(x1, attn, gain, mods, dz2, dx2):
    t, d = x1.shape
    rb = min(ROW_BLOCK, t)

    def body(x1_ref, at_ref, g_ref, mod_ref, dz_ref, dx2_ref, dx1_ref, da_ref, st_ref):
        @pl.when(pl.program_id(0) == 0)
        def _():
            st_ref[...] = jnp.zeros_like(st_ref)

        xh, r = _rms(x1_ref[...])
        g = g_ref[...]
        dz = dz_ref[...].astype(F32)
        dxn = dz * (1.0 + mod_ref[1:2, :])
        st_ref[0:1, :] += _colsum(dz)
        st_ref[1:2, :] += _colsum(dz * (xh * g))
        st_ref[2:3, :] += _colsum(dxn * xh)
        dx1 = dx2_ref[...] + _rms_bwd(dxn * g, xh, r)
        dx1_ref[...] = dx1
        st_ref[3:4, :] += _colsum(dx1 * at_ref[...])
        da_ref[...] = (dx1 * mod_ref[0:1, :]).astype(BF16)

    row = pl.BlockSpec((rb, d), lambda i: (i, 0))
    return _pcall(
        body,
        name="norm2_mod_bwd",
        out_shape=[jax.ShapeDtypeStruct((t, d), F32), jax.ShapeDtypeStruct((t, d), BF16), jax.ShapeDtypeStruct((8, d), F32)],
        grid=(t // rb,),
        in_specs=[row, row, pl.BlockSpec((1, d), lambda i: (0, 0)), pl.BlockSpec((8, d), lambda i: (0, 0)), row, row],
        out_specs=[row, row, pl.BlockSpec((8, d), lambda i: (0, 0))],
        compiler_params=_cparams(("arbitrary",)),
    )(x1, attn, gain, mods, dz2, dx2)


def _final_loss(x1, ffn, gain, mods, target):
    t, d = x1.shape
    rb = min(ROW_BLOCK, t)
    nb = t // rb

    def body(x1_ref, f_ref, g_ref, mod_ref, tg_ref, dx2_ref, df_ref, st_ref):
        i = pl.program_id(0)

        @pl.when(i == 0)
        def _():
            st_ref[...] = jnp.zeros_like(st_ref)

        ffn_v = f_ref[...]
        g2 = mod_ref[0:1, :]
        x2 = x1_ref[...] + g2 * ffn_v
        xh, r = _rms(x2)
        g = g_ref[...]
        err = xh * g - tg_ref[...]
        st_ref[2:3, :] += _colsum(err * err) * (0.5 / d)
        dy = err * (1.0 / d)
        st_ref[0:1, :] += _colsum(dy * xh)
        dx2 = _rms_bwd(dy * g, xh, r)
        dx2_ref[...] = dx2
        st_ref[1:2, :] += _colsum(dx2 * ffn_v)
        df_ref[...] = (dx2 * g2).astype(BF16)

        @pl.when(i == nb - 1)
        def _():
            st_ref[3:4, :] = jnp.broadcast_to(jnp.sum(st_ref[2:3, :], axis=-1, keepdims=True), (1, d))

    row = pl.BlockSpec((rb, d), lambda i: (i, 0))
    return _pcall(
        body,
        name="final_norm_loss",
        out_shape=[jax.ShapeDtypeStruct((t, d), F32), jax.ShapeDtypeStruct((t, d), BF16), jax.ShapeDtypeStruct((8, d), F32)],
        grid=(nb,),
        in_specs=[row, row, pl.BlockSpec((1, d), lambda i: (0, 0)), pl.BlockSpec((8, d), lambda i: (0, 0)), row],
        out_specs=[row, row, pl.BlockSpec((8, d), lambda i: (0, 0))],
        compiler_params=_cparams(("arbitrary",)),
    )(x1, ffn, gain, mods, target)


def _row_ends(shape):
    rows = lax.broadcasted_iota(jnp.int32, shape, 0)
    return rows == 0, rows == shape[0] - 1


def _shift_dn(v, first):
    return jnp.where(first, 0.0, pltpu.roll(v, 1, 0))


def _shift_up(v, last):
    return jnp.where(last, 0.0, pltpu.roll(v, v.shape[0] - 1, 0))


def _conv_fwd(u, cw, cb):
    t, f2 = u.shape
    f = f2 // 2
    cbk = _tile(f, 256)
    nf = f // cbk

    def body(ua_ref, ub_ref, cwa_ref, cwb_ref, cba_ref, cbb_ref, h_ref, uc_ref):
        first, last = _row_ends((t, cbk))
        outs = []
        for u_ref, cw_ref, cb_ref in ((ua_ref, cwa_ref, cba_ref), (ub_ref, cwb_ref, cbb_ref)):
            uu, cwv = u_ref[...].astype(F32), cw_ref[...]
            outs.append(cb_ref[...] + cwv[0:1, :] * _shift_dn(uu, first) + cwv[1:2, :] * uu
                        + cwv[2:3, :] * _shift_up(uu, last))
        a, b = outs
        uc_ref[0] = a.astype(BF16)
        uc_ref[1] = b.astype(BF16)
        h_ref[...] = (a * jax.nn.sigmoid(a) * b).astype(BF16)

    ca = lambda r: pl.BlockSpec((r, cbk), lambda j: (0, j))
    cbs = lambda r: pl.BlockSpec((r, cbk), lambda j: (0, nf + j))
    return _pcall(
        body,
        name="conv_gate_fwd",
        out_shape=[jax.ShapeDtypeStruct((t, f), BF16), jax.ShapeDtypeStruct((2, t, f), BF16)],
        grid=(nf,),
        in_specs=[ca(t), cbs(t), ca(3), cbs(3), ca(1), cbs(1)],
        out_specs=[ca(t), pl.BlockSpec((2, t, cbk), lambda j: (0, 0, j))],
        compiler_params=_cparams(("parallel",)),
    )(u, u, cw, cw, cb, cb)


def _conv_bwd(u, uc, cw, dh):
    t, f2 = u.shape
    f = f2 // 2
    cbk = _tile(f, 256)
    nf = f // cbk

    def body(ua_ref, ub_ref, uc_ref, cwa_ref, cwb_ref, dh_ref, du_ref, dcw_ref, dcb_ref):
        first, last = _row_ends((t, cbk))
        a, b = uc_ref[0].astype(F32), uc_ref[1].astype(F32)
        dh_v = dh_ref[...].astype(F32)
        sg = jax.nn.sigmoid(a)
        db = dh_v * (a * sg)
        da = dh_v * b * (sg * (1.0 + a * (1.0 - sg)))
        for idx, (dv, u_ref, cw_ref) in enumerate(((da, ua_ref, cwa_ref), (db, ub_ref, cwb_ref))):
            uu, cwv = u_ref[...].astype(F32), cw_ref[...]
            up, dn = _shift_up(dv, last), _shift_dn(dv, first)
            dcb_ref[idx] = _colsum(dv)
            dcw_ref[idx, 0:1, :] = _colsum(up * uu)
            dcw_ref[idx, 1:2, :] = _colsum(dv * uu)
            dcw_ref[idx, 2:3, :] = _colsum(dn * uu)
            du_ref[idx] = (cwv[0:1, :] * up + cwv[1:2, :] * dv + cwv[2:3, :] * dn).astype(BF16)

    ca = lambda r: pl.BlockSpec((r, cbk), lambda j: (0, j))
    cbs = lambda r: pl.BlockSpec((r, cbk), lambda j: (0, nf + j))
    o3 = lambda r: pl.BlockSpec((2, r, cbk), lambda j: (0, 0, j))
    return _pcall(
        body,
        name="conv_gate_bwd",
        out_shape=[jax.ShapeDtypeStruct((2, t, f), BF16), jax.ShapeDtypeStruct((2, 3, f), F32),
                   jax.ShapeDtypeStruct((2, 1, f), F32)],
        grid=(nf,),
        in_specs=[ca(t), cbs(t), o3(t), ca(3), cbs(3), ca(t)],
        out_specs=[o3(t), o3(3), o3(1)],
        compiler_params=_cparams(("parallel",)),
    )(u, u, uc, cw, cw, dh)


def _attention_fwd(q, kk, vv, *, hq, hkv, dk, dv, k_blk0, v_blk0, name):
    t = q.shape[0]
    tk = kk.shape[0]
    g_sz = hq // hkv
    tq = min(ATT_Q_BLOCK_FWD, t)

    def body(q_ref, k_ref, v_ref, o_ref, lse_ref):
        k = k_ref[...]
        v = v_ref[...]
        for j in range(g_sz):
            s = lax.dot_general(q_ref[:, j * dk:(j + 1) * dk], k, _DIMS["nt"], preferred_element_type=F32)
            m = jnp.max(s, axis=-1, keepdims=True)
            p = jnp.exp2(s - m)
            l = jnp.sum(p, axis=-1, keepdims=True)
            o = jnp.dot(p.astype(BF16), v, preferred_element_type=F32) / l
            o_ref[:, j * dv:(j + 1) * dv] = o.astype(BF16)
            lse_ref[0, :, j:j + 1] = m + jnp.log2(l)

    return _pcall(
        body,
        name=name,
        out_shape=[jax.ShapeDtypeStruct((t, hq * dv), BF16), jax.ShapeDtypeStruct((hkv, t, g_sz), F32)],
        grid=(hkv, t // tq),
        in_specs=[
            pl.BlockSpec((tq, g_sz * dk), lambda g, i: (i, g)),
            pl.BlockSpec((tk, dk), lambda g, i: (0, k_blk0 + g)),
            pl.BlockSpec((tk, dv), lambda g, i: (0, v_blk0 + g)),
        ],
        out_specs=[
            pl.BlockSpec((tq, g_sz * dv), lambda g, i: (i, g)),
            pl.BlockSpec((1, tq, g_sz), lambda g, i: (g, i, 0)),
        ],
        compiler_params=_cparams(("parallel", "parallel")),
    )(q, kk, vv)


def _attention_bwd(q, kk, vv, do, lse, *, hq, hkv, dk, dv, k_blk0, v_blk0, name):
    t = q.shape[0]
    tk = kk.shape[0]
    g_sz = hq // hkv
    tq = min(ATT_Q_BLOCK, t)

    def body(q_ref, k_ref, v_ref, do_ref, lse_ref, dq_ref, dk_ref, dv_ref):
        @pl.when(pl.program_id(1) == 0)
        def _():
            dk_ref[...] = jnp.zeros_like(dk_ref)
            dv_ref[...] = jnp.zeros_like(dv_ref)

        k = k_ref[...]
        v = v_ref[...]
        dk_acc = dv_acc = None
        for j in range(g_sz):
            qj = q_ref[:, j * dk:(j + 1) * dk]
            doj = do_ref[:, j * dv:(j + 1) * dv]
            s = lax.dot_general(qj, k, _DIMS["nt"], preferred_element_type=F32)
            p = jnp.exp2(s - lse_ref[0, :, j:j + 1])
            dp = lax.dot_general(doj, v, _DIMS["nt"], preferred_element_type=F32)
            ds = (p * (dp - jnp.sum(p * dp, axis=-1, keepdims=True))).astype(BF16)
            dv_j = lax.dot_general(p.astype(BF16), doj, _DIMS["tn"], preferred_element_type=F32)
            dk_j = lax.dot_general(ds, qj, _DIMS["tn"], preferred_element_type=F32)
            dv_acc = dv_j if dv_acc is None else dv_acc + dv_j
            dk_acc = dk_j if dk_acc is None else dk_acc + dk_j
            dq_ref[:, j * dk:(j + 1) * dk] = jnp.dot(ds, k, preferred_element_type=F32)
        dv_ref[...] += dv_acc
        dk_ref[...] += dk_acc

        @pl.when(pl.program_id(1) == t // tq - 1)
        def _():
            dk_ref[...] *= LN2

    return _pcall(
        body,
        name=name,
        out_shape=[jax.ShapeDtypeStruct((t, hq * dk), F32), jax.ShapeDtypeStruct((tk, hkv * dk), F32),
                   jax.ShapeDtypeStruct((tk, hkv * dv), F32)],
        grid=(hkv, t // tq),
        in_specs=[
            pl.BlockSpec((tq, g_sz * dk), lambda g, i: (i, g)),
            pl.BlockSpec((tk, dk), lambda g, i: (0, k_blk0 + g)),
            pl.BlockSpec((tk, dv), lambda g, i: (0, v_blk0 + g)),
            pl.BlockSpec((tq, g_sz * dv), lambda g, i: (i, g)),
            pl.BlockSpec((1, tq, g_sz), lambda g, i: (g, i, 0)),
        ],
        out_specs=[
            pl.BlockSpec((tq, g_sz * dk), lambda g, i: (i, g)),
            pl.BlockSpec((tk, dk), lambda g, i: (0, g)),
            pl.BlockSpec((tk, dv), lambda g, i: (0, g)),
        ],
        compiler_params=_cparams(("parallel", "arbitrary")),
    )(q, kk, vv, do, lse)


def _silu(v):
    return v * jax.nn.sigmoid(v)


def _ada_fwd(conds, w_ada, b_ada_shard):
    r, d = conds.shape
    n = w_ada.shape[1]
    tn = _tile(n, 512)

    def body(c_ref, w_ref, b_ref, o_ref):
        s = _silu(c_ref[...]).astype(BF16)
        o_ref[...] = jnp.dot(s, w_ref[...].astype(BF16), preferred_element_type=F32) + b_ref[...]

    return _pcall(
        body,
        name="ada_fwd",
        out_shape=jax.ShapeDtypeStruct((r, n), F32),
        grid=(n // tn,),
        in_specs=[pl.BlockSpec((r, d), lambda j: (0, 0)), pl.BlockSpec((d, tn), lambda j: (0, j)),
                  pl.BlockSpec((1, tn), lambda j: (0, j))],
        out_specs=pl.BlockSpec((r, tn), lambda j: (0, j)),
        compiler_params=_cparams(("parallel",)),
    )(conds, w_ada, b_ada_shard)


def _cctx_partial(da16_shard, w_ada, c_ctx_row):
    d, n = w_ada.shape
    td = _tile(d, 512)

    def body(g_ref, w_ref, c_ref, o_ref):
        ds = lax.dot_general(g_ref[8:16, :].astype(BF16), w_ref[...].astype(BF16), _DIMS["nt"],
                             preferred_element_type=F32)
        cv = c_ref[...]
        sg = jax.nn.sigmoid(cv)
        o_ref[...] = ds * (sg * (1.0 + cv * (1.0 - sg)))

    return _pcall(
        body,
        name="cctx_partial",
        out_shape=jax.ShapeDtypeStruct((8, d), F32),
        grid=(d // td,),
        in_specs=[pl.BlockSpec((16, n), lambda j: (0, 0)), pl.BlockSpec((td, n), lambda j: (j, 0)),
                  pl.BlockSpec((1, td), lambda j: (0, j))],
        out_specs=pl.BlockSpec((8, td), lambda j: (0, j)),
        compiler_params=_cparams(("parallel",)),
    )(da16_shard, w_ada, c_ctx_row)


def _sum_parts(parts):
    p, _, n = parts.shape

    def body(p_ref, o_ref):
        acc = p_ref[0]
        for s in range(1, p):
            acc = acc + p_ref[s]
        o_ref[...] = acc

    return _pcall(
        body,
        name="sum_parts",
        out_shape=jax.ShapeDtypeStruct((1, n), F32),
        in_specs=[pl.BlockSpec(memory_space=pltpu.VMEM)],
        out_specs=pl.BlockSpec(memory_space=pltpu.VMEM),
    )(parts)


def _adam_math(w, g, m, v):
    m2 = ADAM_B1 * m + (1.0 - ADAM_B1) * g
    v2 = ADAM_B2 * v + (1.0 - ADAM_B2) * jnp.square(g)
    m_hat = m2 / (1.0 - ADAM_B1 ** ADAM_STEP)
    v_hat = v2 / (1.0 - ADAM_B2 ** ADAM_STEP)
    delta = -ADAM_LR * (m_hat / (jnp.sqrt(v_hat) + ADAM_EPS) + ADAM_WD * w)
    return delta, m2, v2


def _adamw(parts, w, m, v, name):
    p, r, c = parts.shape
    block_elems = 1 << 18
    rb, cb = _tile(r, max(8, block_elems // c // 8 * 8), 8), c
    if rb * c < block_elems // 4 and r * c > block_elems:
        rb, cb = r, _tile(c, max(LANE, block_elems // r // LANE * LANE))

    def body(p_ref, w_ref, m_ref, v_ref, g_ref, d_ref, m2_ref, v2_ref):
        g = p_ref[0].astype(F32)
        for s in range(1, p):
            g = g + p_ref[s].astype(F32)
        g_ref[...] = g
        d_ref[...], m2_ref[...], v2_ref[...] = _adam_math(w_ref[...], g, m_ref[...], v_ref[...])

    if w.ndim == 3:
        blk = pl.BlockSpec((None, rb, cb), lambda i, j: (0, i, j))
    else:
        blk = pl.BlockSpec((rb, cb), lambda i, j: (i, j))
    return _pcall(
        body,
        name=name,
        out_shape=[jax.ShapeDtypeStruct(w.shape, F32)] * 4,
        grid=(r // rb, c // cb),
        in_specs=[pl.BlockSpec((p, rb, cb), lambda i, j: (0, i, j)), blk, blk, blk],
        out_specs=[blk] * 4,
        compiler_params=_cparams(("parallel", "parallel")),
    )(parts, w, m, v)


def _adamw_ada(conds, da16, w, m, v):
    d, n = w.shape
    rb = _tile(d, 256, LANE)

    def body(s_ref, da_ref, w_ref, m_ref, v_ref, g_ref, d_ref, m2_ref, v2_ref):
        g = lax.dot_general(_silu(s_ref[...]).astype(BF16), da_ref[...].astype(BF16), _DIMS["tn"],
                            preferred_element_type=F32)
        g_ref[...] = g
        d_ref[...], m2_ref[...], v2_ref[...] = _adam_math(w_ref[...], g, m_ref[...], v_ref[...])

    row = pl.BlockSpec((rb, n), lambda i: (i, 0))
    return _pcall(
        body,
        name="adamw_w_ada",
        out_shape=[jax.ShapeDtypeStruct((d, n), F32)] * 4,
        grid=(d // rb,),
        in_specs=[pl.BlockSpec((16, rb), lambda i: (0, i)), pl.BlockSpec((16, n), lambda i: (0, 0)), row, row, row],
        out_specs=[row] * 4,
        compiler_params=_cparams(("parallel",)),
    )(conds, da16, w, m, v)


def _cast_bf16(a, name):
    _, r, c = a.shape
    rb, cb = _tile(r, 512, 8), c
    if rb < 64 < r:
        rb, cb = r, _tile(c, 512)

    def body(a_ref, o_ref):
        o_ref[...] = a_ref[...].astype(BF16)

    return _pcall(body, name=name, out_shape=jax.ShapeDtypeStruct((r, c), BF16), grid=(r // rb, c // cb),
                  in_specs=[pl.BlockSpec((None, rb, cb), lambda i, j: (0, i, j))],
                  out_specs=pl.BlockSpec((rb, cb), lambda i, j: (i, j)),
                  compiler_params=_cparams(("parallel", "parallel")))(a)


def _rope_tabs(t, rot):
    half, q = rot // 2, rot // 4
    n_rows = t // GRID_W
    row = jnp.repeat(jnp.arange(n_rows, dtype=F32), GRID_W)
    col = jnp.tile(jnp.arange(GRID_W, dtype=F32), n_rows)
    inv_freq = ROPE_THETA ** (-jnp.arange(0, half, 2, dtype=F32) / half)
    ang = jnp.concatenate([row[:, None] * inv_freq, col[:, None] * inv_freq], axis=-1)
    cos, sin = jnp.cos(ang), jnp.sin(ang)
    c0, c1, s0, s1 = cos[:, :q], cos[:, q:], sin[:, :q], sin[:, q:]
    z = jnp.zeros_like(s0)
    return (jnp.concatenate([c0, c0, c1, c1], -1), jnp.concatenate([-s0, z, -s1, z], -1),
            jnp.concatenate([z, s0, z, s1], -1))


def _pad_cols(a, left, total, fill=0.0):
    return jnp.pad(a, ((0, 0), (left, total - left - a.shape[1])), constant_values=fill)


def _with_ctx_rows(tab, tc, fill):
    return jnp.concatenate([tab, jnp.full((tc, tab.shape[1]), fill, F32)], axis=0)


def kernel(x, c, ctx, c_ctx, w_ada, b_ada, norm1_g, w_in, mla_q_norm_g, w_q_up, mla_kv_norm_g, w_kv_up, gqa_q_norm_g, gqa_k_norm_g, w_br_a, w_br_b, w_out, norm2_g, w_up, conv_w, conv_b, w_down, final_norm_g, loss_target, m_c_ctx, m_w_ada, m_b_ada, m_norm1_g, m_w_in, m_mla_q_norm_g, m_w_q_up, m_mla_kv_norm_g, m_w_kv_up, m_gqa_q_norm_g, m_gqa_k_norm_g, m_w_br_a, m_w_br_b, m_w_out, m_norm2_g, m_w_up, m_conv_w, m_conv_b, m_w_down, m_final_norm_g, v_c_ctx, v_w_ada, v_b_ada, v_norm1_g, v_w_in, v_mla_q_norm_g, v_w_q_up, v_mla_kv_norm_g, v_w_kv_up, v_gqa_q_norm_g, v_gqa_k_norm_g, v_w_br_a, v_w_br_b, v_w_out, v_norm2_g, v_w_up, v_conv_w, v_conv_b, v_w_down, v_final_norm_g):
    weights = dict(c_ctx=c_ctx, w_ada=w_ada, b_ada=b_ada, norm1_g=norm1_g, w_in=w_in, mla_q_norm_g=mla_q_norm_g,
                   w_q_up=w_q_up, mla_kv_norm_g=mla_kv_norm_g, w_kv_up=w_kv_up, gqa_q_norm_g=gqa_q_norm_g,
                   gqa_k_norm_g=gqa_k_norm_g, w_br_a=w_br_a, w_br_b=w_br_b, w_out=w_out, norm2_g=norm2_g, w_up=w_up,
                   conv_w=conv_w, conv_b=conv_b, w_down=w_down, final_norm_g=final_norm_g)
    mom_m = dict(c_ctx=m_c_ctx, w_ada=m_w_ada, b_ada=m_b_ada, norm1_g=m_norm1_g, w_in=m_w_in, mla_q_norm_g=m_mla_q_norm_g,
                 w_q_up=m_w_q_up, mla_kv_norm_g=m_mla_kv_norm_g, w_kv_up=m_w_kv_up, gqa_q_norm_g=m_gqa_q_norm_g,
                 gqa_k_norm_g=m_gqa_k_norm_g, w_br_a=m_w_br_a, w_br_b=m_w_br_b, w_out=m_w_out, norm2_g=m_norm2_g,
                 w_up=m_w_up, conv_w=m_conv_w, conv_b=m_conv_b, w_down=m_w_down, final_norm_g=m_final_norm_g)
    mom_v = dict(c_ctx=v_c_ctx, w_ada=v_w_ada, b_ada=v_b_ada, norm1_g=v_norm1_g, w_in=v_w_in, mla_q_norm_g=v_mla_q_norm_g,
                 w_q_up=v_w_q_up, mla_kv_norm_g=v_mla_kv_norm_g, w_kv_up=v_w_kv_up, gqa_q_norm_g=v_gqa_q_norm_g,
                 gqa_k_norm_g=v_gqa_k_norm_g, w_br_a=v_w_br_a, w_br_b=v_w_br_b, w_out=v_w_out, norm2_g=v_norm2_g,
                 w_up=v_w_up, conv_w=v_conv_w, conv_b=v_conv_b, w_down=v_w_down, final_norm_g=v_final_norm_g)
    order = list(weights)

    my_idx = 4 * lax.axis_index("x") + 2 * lax.axis_index("y") + lax.axis_index("c")
    xs, cts, tgt = x[0], ctx[0], loss_target[0]
    t, d = xs.shape
    tc = cts.shape[0]
    ta = t + tc
    kvl, ql = MLA_KV_LORA, MLA_Q_LORA
    nb = GQA_KV_HEADS * GQA_HEAD_DIM
    hb = GQA_HEADS * GQA_HEAD_DIM
    ha = MLA_HEADS
    f2 = w_up.shape[2] * N_DEV
    ff = f2 // 2

    big = ["w_in", "w_q_up", "w_kv_up", "w_br_a", "w_br_b", "w_out", "w_up", "w_down"]
    _ORDER_AFTER.clear()
    narrow = ("w_in", "w_q_up")

    def tview(a):
        return jnp.transpose(a, (0, 2, 1))

    shards = {"w_in": _cast_bf16(tview(weights["w_in"]), "cast_w_in")}
    c_idx = jnp.reshape(lax.axis_index("c"), (1,)).astype(jnp.int32)

    def gather_start(names, dep):
        shs = [shards[n] for n in names]
        land = [lax.empty((N_DEV,) + s.shape, BF16) for s in shs]
        if dep is not None:
            _after(dep)
        s, r, arrs, tok = _split_start("gather_ici_start_" + names[0], shs + land, _gather_ici_copies(len(names)),
                                       4 * len(names))
        return dict(names=names, s=s, r=r, arrs=arrs, tok=tok)

    def gather_pass(g, after):
        n = len(g["names"])
        arrs = _split_wait("gather_ici_wait_" + g["names"][0], g["s"], g["r"], g["arrs"], _gather_ici_copies(n), after)
        s, r, bufs, tok = _split_start("gather_pass_start_" + g["names"][0], arrs[n:], _gather_pass_copies(n), 3 * n)
        g.update(s2=s, r2=r, bufs=bufs)
        return tok

    def gather_relay(g, after):
        n = len(g["names"])
        bufs = _split_wait("gather_pass_wait_" + g["names"][0], g["s2"], g["r2"], g["bufs"], _gather_pass_copies(n), after)
        s, r, bufs, tok = _split_start("gather_d2d_start_" + g["names"][0], bufs, _gather_d2d_copies(n), n)
        g.update(s3=s, r3=r, bufs=bufs)
        return tok

    def gather_finish(g, after):
        n = len(g["names"])
        bufs = _split_wait("gather_d2d_wait_" + g["names"][0], g["s3"], g["r3"], g["bufs"], _gather_d2d_copies(n), after)
        return dict(zip(g["names"], bufs))

    _after(shards["w_in"])
    c_all, cw_all = _all_gather([jnp.pad(c, ((0, 7), (0, 0))), jnp.pad(conv_w[0], ((0, 5), (0, 0)))], "gather_cond")
    conv_w_f = jnp.transpose(cw_all[:, :3, :], (1, 0, 2)).reshape(3, f2)
    conds = jnp.concatenate([c_all[:, 0, :], c_ctx[None, :], jnp.zeros((7, d), F32)], axis=0)
    ncol = w_ada.shape[2]
    b_shard = lax.dynamic_slice_in_dim(b_ada, my_idx * ncol, ncol, axis=1)
    ada_shard = _ada_fwd(conds, w_ada[0], b_shard)
    (ada_all,) = _all_gather([ada_shard], "gather_ada")
    ada = jnp.transpose(ada_all, (1, 0, 2)).reshape(16, N_DEV * ncol)
    lat = lax.dynamic_slice_in_dim(ada, my_idx, 1, axis=0).reshape(6, d)
    cxt = ada[8].reshape(6, d)
    mods1 = jnp.concatenate([lat[0:2], cxt[0:2], jnp.zeros((4, d), F32)], axis=0)
    mods2 = jnp.concatenate([lat[2:3], lat[3:4], lat[4:5], jnp.zeros((5, d), F32)], axis=0)
    mods2b = jnp.concatenate([lat[2:3], lat[4:5], jnp.zeros((6, d), F32)], axis=0)
    mods3 = jnp.concatenate([lat[5:6], jnp.zeros((7, d), F32)], axis=0)

    g0 = gather_start(["w_in"], ada_all)
    for n in big[1:]:
        _after(g0["tok"])
        shards[n] = _cast_bf16(tview(weights[n]) if n in narrow else weights[n], "cast_" + n)

    ca, s1a, s2a = _rope_tabs(t, MLA_ROPE)
    cb_, s1b, s2b = _rope_tabs(t, GQA_HEAD_DIM)
    q_tabs_a = (_pad_cols(jnp.concatenate([jnp.ones((t, MLA_NOPE), F32), ca], 1), 0, MLA_SLOT),
                _pad_cols(s1a, MLA_NOPE, MLA_SLOT), _pad_cols(s2a, MLA_NOPE, MLA_SLOT))
    q_tabs_b = (cb_, s1b, s2b)
    k_tabs = (_with_ctx_rows(_pad_cols(ca, 0, LANE), tc, 1.0), _with_ctx_rows(_pad_cols(s1a, 0, LANE), tc, 0.0),
              _with_ctx_rows(_pad_cols(s2a, 0, LANE), tc, 0.0),
              _with_ctx_rows(cb_, tc, 1.0), _with_ctx_rows(s1b, tc, 0.0), _with_ctx_rows(s2b, tc, 0.0))

    def cols_full(g):
        return jnp.transpose(g, (1, 0, 2)).reshape(g.shape[1], N_DEV * g.shape[2])

    _after(*q_tabs_a, *q_tabs_b, *k_tabs, *[shards[n] for n in big[1:]])
    tok_p0 = gather_pass(g0, mods1)
    g1 = gather_start(["w_q_up", "w_kv_up", "w_br_a", "w_br_b", "w_out"], tok_p0)
    _after(g1["tok"])
    z_all = _norm_mod_fwd(cts, xs, norm1_g, mods1)
    gathered = gather_finish(g0, gather_relay(g0, z_all))
    wt_in = gathered["w_in"].reshape(-1, d)
    o_kpe, o_kb, o_vb = kvl, kvl + MLA_ROPE, kvl + MLA_ROPE + nb
    o_q = o_vb + nb
    o_g = o_q + ql + hb
    wkv_w = kvl + 2 * nb + LANE
    wt_kv_p = jnp.concatenate([wt_in[:kvl], wt_in[o_kb:o_q], wt_in[o_kpe:o_kb],
                               jnp.zeros((LANE - MLA_ROPE, d), BF16)], axis=0)
    q_w = ql + hb
    q_pad = (-q_w) % 512 if d >= 512 else (-q_w) % d
    qw_p = q_w + q_pad
    wt_q_p = jnp.concatenate([wt_in[o_q:o_g], jnp.zeros((q_pad, d), BF16)], axis=0)
    wt_g = wt_in[o_g:]

    kv_all = _mm(z_all, wt_kv_p, "nt", F32, "proj_kv", tm=1152, tn=wkv_w)
    qq = _mm(z_all, wt_q_p, "nt", F32, "proj_q", tm=1024, tn=1024, rows=t)
    gates = _mm(z_all, wt_g, "nt", F32, "proj_gates", tm=1024, tn=1024, rows=t)
    tok_p1 = gather_pass(g1, gates)
    g2 = gather_start(["w_up", "w_down"], tok_p1)
    _after(g2["tok"])
    kin, k_b, v_b = _key_prep_fwd(kv_all, mla_kv_norm_g, gqa_k_norm_g, k_tabs)
    sc_a = float((MLA_NOPE + MLA_ROPE) ** -0.5) * LOG2E
    sc_b = float(GQA_HEAD_DIM ** -0.5) * LOG2E
    _after(g2["tok"])
    cqn, q_b = _q_prep_fwd(qq, mla_q_norm_g, gqa_q_norm_g, q_tabs_b, sc_b)
    _after(kin, g2["tok"])
    gathered.update(gather_finish(g1, gather_relay(g1, q_b)))

    wqt_f = gathered["w_q_up"].reshape(ha, MLA_NOPE + MLA_ROPE, ql)
    wqt_ext = jnp.pad(wqt_f, ((0, 0), (0, MLA_SLOT - MLA_NOPE - MLA_ROPE), (0, 0))).reshape(ha * MLA_SLOT, ql)
    wkv_f = cols_full(gathered["w_kv_up"]).reshape(kvl, ha, MLA_NOPE + MLA_V)
    wk_slots = jnp.pad(wkv_f[:, :, :MLA_NOPE], ((0, 0), (0, 0), (0, MLA_SLOT - MLA_NOPE))).reshape(kvl, ha * MLA_SLOT)
    wv_cols = wkv_f[:, :, MLA_NOPE:].reshape(kvl, ha * MLA_V)
    e_slot = jnp.pad(jnp.eye(MLA_ROPE, dtype=BF16),
                     ((0, LANE - MLA_ROPE), (MLA_NOPE, MLA_SLOT - MLA_NOPE - MLA_ROPE)))
    e_rows = jnp.concatenate([jnp.tile(e_slot, (1, ha)), jnp.zeros((LANE, ha * MLA_V), BF16)], axis=1)
    wkv_ext = jnp.concatenate([jnp.concatenate([wk_slots, wv_cols], axis=1), e_rows], axis=0)
    w_bra = cols_full(gathered["w_br_a"])
    w_brb = cols_full(gathered["w_br_b"])
    w_out_f = gathered["w_out"].reshape(d, d)

    kv_a = _mm(kin, wkv_ext, "nn", BF16, "kv_up", tm=1152, tn=1024)
    qa_raw = _mm(cqn, wqt_ext, "nt", F32, "q_up", tm=1024, tn=1024)
    q_a = _rope_a(qa_raw, q_tabs_a, False, BF16, "rope_q_fwd", sc_a)
    att_a = dict(hq=ha, hkv=ha, dk=MLA_SLOT, dv=MLA_V, k_blk0=0, v_blk0=ha * MLA_SLOT // MLA_V)
    att_b = dict(hq=GQA_HEADS, hkv=GQA_KV_HEADS, dk=GQA_HEAD_DIM, dv=GQA_HEAD_DIM, k_blk0=0, v_blk0=0)
    o_a, lse_a = _attention_fwd(q_a, kv_a, kv_a, name="attn_a_fwd", **att_a)
    o_b, lse_b = _attention_fwd(q_b, k_b, v_b, name="attn_b_fwd", **att_b)
    _after(o_a)
    _after(gather_pass(g2, o_b))
    pa = _mm(o_a, w_bra, "nn", BF16, "br_a", tm=1024, tn=1024)
    pb = _mm(o_b, w_brb, "nn", BF16, "br_b", tm=1024, tn=1024)
    merged = _merge_fwd(pa, pb, gates, 0)
    attn = _mm(merged, w_out_f, "nn", F32, "w_out", tm=1024, tn=1024)
    _after(gather_relay(g2, attn))
    x1, z2 = _resid_norm_mod(xs, attn, norm2_g, mods2, "resid_norm2_fwd")
    ffn_w = gather_finish(g2, z2)
    w_up3 = ffn_w["w_up"]
    w_down_f = ffn_w["w_down"].reshape(ff, d)
    u = _mm_up_fwd(z2, w_up3, "w_up")
    h, uc = _conv_fwd(u, conv_w_f, conv_b)
    ffn = _mm(h, w_down_f, "nn", F32, "w_down", tm=1024, tn=1024, tk=2816)

    def to_shards(g):
        return jnp.transpose(g.reshape(g.shape[0], N_DEV, g.shape[1] // N_DEV), (1, 0, 2))

    def reduce_start(tag, names, sends):
        n = len(sends)
        land = [lax.empty((4,) + s.shape[1:], s.dtype) for s in sends]
        s, r, arrs, tok = _split_start("reduce_d2d_start_" + tag, sends + land, _reduce_d2d_copies(n), 4 * n)
        return dict(tag=tag, names=names, s=s, r=r, arrs=arrs, tok=tok)

    def reduce_relay(g, after):
        n = len(g["names"])
        arrs = _split_wait("reduce_d2d_wait_" + g["tag"], g["s"], g["r"], g["arrs"], _reduce_d2d_copies(n), after)
        sums = [_pair_sum(arrs[a], arrs[n + a], c_idx, "pair_sum_" + g["names"][a]) for a in range(n)]
        land = [lax.empty(s.shape, s.dtype) for s in sums]
        s, r, arrs2, tok = _split_start("reduce_ici_start_" + g["tag"], sums + land, _reduce_ici_copies(n), 4 * n)
        g.update(s2=s, r2=r, arrs2=arrs2)
        return tok

    def reduce_finish(g, after):
        n = len(g["names"])
        arrs2 = _split_wait("reduce_ici_wait_" + g["tag"], g["s2"], g["r2"], g["arrs2"], _reduce_ici_copies(n), after)
        return dict(zip(g["names"], arrs2[n:]))

    dx2, dffn, st_fin = _final_loss(x1, ffn, final_norm_g[None, :], mods3, tgt)
    dh = _mm(dffn, w_down_f, "nt", BF16, "d_h", tm=1024, tn=1024)
    g_w_down = _mm(h, dffn, "tn", BF16, "g_w_down", tm=512, tn=1024)
    du3, dcw, dcb = _conv_bwd(u, uc, conv_w_f, dh)
    dz2 = _mm_up_dz(du3, w_up3, "d_z2")
    g_w_up = _mm_up_gw(z2, du3, N_DEV, "g_w_up")
    g_conv_w = jnp.concatenate([dcw[0], dcw[1]], axis=1)
    r_ffn = reduce_start("ffn", ["w_down", "w_up", "conv_w"],
                         [g_w_down.reshape(N_DEV, ff // N_DEV, d), g_w_up,
                          to_shards(jnp.pad(g_conv_w, ((0, 5), (0, 0))))])
    _after(r_ffn["tok"])
    dx1, dattn, st_n2 = _norm2_bwd(x1, attn, norm2_g, mods2b, dz2, dx2)
    dmerged = _mm(dattn, w_out_f, "nt", BF16, "d_merged", tm=1024, tn=1024)
    g_w_out = _mm(merged, dattn, "tn", BF16, "g_w_out", tm=1024, tn=1024)
    dpa, dpb, dgates = _merge_bwd(dmerged, pa, pb, gates, 0)
    do_a = _mm(dpa, w_bra, "nt", BF16, "d_o_a", tm=1024, tn=1024)
    do_b = _mm(dpb, w_brb, "nt", BF16, "d_o_b", tm=1024, tn=1024)
    g_w_bra = _mm_tn_shards(o_a, dpa, N_DEV, "g_w_br_a")
    g_w_brb = _mm_tn_shards(o_b, dpb, N_DEV, "g_w_br_b")
    _after(reduce_relay(r_ffn, g_w_brb))
    dq_a, dk_a, dv_a = _attention_bwd(q_a, kv_a, kv_a, do_a, lse_a, name="attn_a_bwd", **att_a)
    dq_b, dk_b, dv_b = _attention_bwd(q_b, k_b, v_b, do_b, lse_b, name="attn_b_bwd", **att_b)
    dqa_raw = _rope_a(dq_a, q_tabs_a, True, BF16, "rope_q_bwd", sc_a * LN2)
    dcqn = _mm(dqa_raw, wqt_ext, "nn", F32, "d_cqn", tm=1024, tn=ql)
    g_wqt_ext = _mm(dqa_raw, cqn, "tn", BF16, "g_w_q_up", tm=1024, tn=ql)
    dq_p, st_q, st_qb = _q_prep_bwd(qq, mla_q_norm_g, gqa_q_norm_g, q_tabs_b, dcqn, dq_b, q_pad, sc_b * LN2)
    dkin = _mm_cat_nt([(dk_a, wkv_ext, 0), (dv_a, wkv_ext, ha * MLA_SLOT)], F32, "d_kin", tm=1152, tn=kvl + LANE)
    g_wkv_ext = _mm_cat_tn(kin, [dk_a, dv_a], BF16, "g_w_kv_up", tm=kvl + LANE, tn=min(1024, ha * MLA_V))
    dkv_p, st_kv, st_kb = _key_prep_bwd(kv_all, mla_kv_norm_g, gqa_k_norm_g, k_tabs, dkin, dk_b, dv_b)
    g_wqt = g_wqt_ext.reshape(ha, MLA_SLOT, ql)[:, :MLA_NOPE + MLA_ROPE, :].reshape(N_DEV, -1, ql)
    g_wkv = jnp.concatenate([g_wkv_ext[:kvl, :ha * MLA_SLOT].reshape(kvl, ha, MLA_SLOT)[:, :, :MLA_NOPE],
                             g_wkv_ext[:kvl, ha * MLA_SLOT:].reshape(kvl, ha, MLA_V)], axis=2).reshape(kvl, ha * (MLA_NOPE + MLA_V))
    r_mid = reduce_start("mid", ["w_out", "w_br_a", "w_br_b", "w_q_up", "w_kv_up"],
                         [g_w_out.reshape(N_DEV, d // N_DEV, d), g_w_bra, g_w_brb, g_wqt,
                          to_shards(g_wkv)])
    _after(r_mid["tok"])
    g_wkv_p = _mm(dkv_p, z_all, "tn", BF16, "g_w_in_kv", tm=wkv_w, tn=1024)
    g_wqg_p = _mm_rows_tn([dq_p, dgates], z_all, BF16, "g_w_in_qg", tm=min(1024, d), tn=1024, rows=t)
    tok_m = reduce_relay(r_mid, g_wqg_p)
    g_wt_in = jnp.concatenate([g_wkv_p[:kvl], g_wkv_p[kvl + 2 * nb:kvl + 2 * nb + MLA_ROPE],
                               g_wkv_p[kvl:kvl + 2 * nb], g_wqg_p[:q_w], g_wqg_p[q_w + q_pad:]], axis=0)
    _after(tok_m)
    r_in = reduce_start("in", ["w_in"], [g_wt_in.reshape(N_DEV, -1, d)])
    _after(r_in["tok"])
    dz_lat = _mm_sum_nn([(dq_p, 0, wt_q_p, 0, qw_p), (dgates, 0, wt_g, 0, d), (dgates, d, wt_g, d, d),
                         (dkv_p, 0, wt_kv_p, 0, wkv_w)], F32, "d_z_lat", rows=t)
    dz_ctx = _mm(dkv_p, wt_kv_p, "nn", F32, "d_z_ctx", tm=min(ROW_BLOCK, tc), tn=1024, a_row_off=t)
    grad_x, st_n1 = _norm1_bwd(cts, xs, norm1_g, mods1, dz_ctx, dz_lat, dx1)

    res = {}

    def upd(nm, parts):
        wv, mv, vv = weights[nm], mom_m[nm], mom_v[nm]
        if wv.ndim == 1:
            wv, mv, vv = (a.reshape(1, -1) for a in (wv, mv, vv))
        if nm in narrow:
            wv, mv, vv = tview(wv), tview(mv), tview(vv)
        outs = _adamw(parts, wv, mv, vv, "adamw_" + nm)
        if nm in narrow:
            outs = [tview(o_) for o_ in outs]
        res[nm] = [o_.reshape(weights[nm].shape) for o_ in outs]

    d_lat = jnp.concatenate([st_n1[0], st_n1[1], st_n2[3], st_n2[0], st_n2[1], st_fin[1]])
    d_cxt = jnp.concatenate([st_n1[3], st_n1[4], jnp.zeros((4 * d,), F32)])
    small = jnp.concatenate([d_lat, d_cxt, st_n1[2], st_q[0], st_kv[0], st_qb[0], st_kb[0], st_n2[2],
                             jnp.concatenate([dcb[0, 0], dcb[1, 0]]), st_fin[0], st_fin[3, :LANE]])
    n_small = small.shape[0]
    pad_small = (-n_small) % LANE
    (small_all,) = _all_gather([jnp.pad(small, (0, pad_small)).reshape(1, -1)], "gather_small")
    offs = {}
    o = 0
    for nm, ln in (("d_lat", 6 * d), ("d_cxt", 6 * d), ("norm1_g", d), ("mla_q_norm_g", ql), ("mla_kv_norm_g", kvl),
                   ("gqa_q_norm_g", GQA_HEAD_DIM), ("gqa_k_norm_g", GQA_HEAD_DIM), ("norm2_g", d), ("conv_b", f2),
                   ("final_norm_g", d), ("loss", LANE)):
        offs[nm] = (o, ln)
        o += ln

    def part(nm):
        a, ln = offs[nm]
        return small_all[:, :, a:a + ln]

    loss = _sum_parts(part("loss"))[0, 0]
    d_lat_all = part("d_lat")[:, 0, :]
    d_cxt_sum = _sum_parts(part("d_cxt"))
    da16 = jnp.concatenate([d_lat_all, d_cxt_sum, jnp.zeros((7, 6 * d), F32)], axis=0)
    da16_shard = lax.dynamic_slice_in_dim(da16, my_idx * ncol, ncol, axis=1)
    cc_part = _cctx_partial(da16_shard, w_ada[0], c_ctx[None, :])
    (cc_all,) = _all_gather([cc_part], "gather_cctx")
    cc_parts = cc_all[:, 0:1, :]
    tok_i = reduce_relay(r_in, cc_all)

    _after(tok_i)
    for nm in ("norm1_g", "mla_q_norm_g", "mla_kv_norm_g", "gqa_q_norm_g", "gqa_k_norm_g", "norm2_g", "conv_b",
               "final_norm_g"):
        upd(nm, part(nm))
    upd("c_ctx", cc_parts)
    b_parts = jnp.concatenate([d_lat_all[:, None, :], d_cxt_sum[None]], axis=0)
    upd("b_ada", b_parts)
    _after(tok_i)
    outs = _adamw_ada(conds, da16_shard, w_ada[0], m_w_ada[0], v_w_ada[0])
    res["w_ada"] = [o_[None] for o_ in outs]
    last = outs[0]
    done = [last]
    for grp in (r_ffn, r_mid, r_in):
        _after(*done)
        recv = reduce_finish(grp, last)
        for nm in grp["names"]:
            upd(nm, recv[nm][:, :3, :] if nm == "conv_w" else recv[nm])
            last = res[nm][0]
            done.append(last)

    return (loss, grad_x[None], *[res[n][0] for n in order], *[res[n][1] for n in order],
            *[res[n][2] for n in order], *[res[n][3] for n in order])
```

```python
import jax
import jax.numpy as jnp
from jax import lax
from jax.experimental import pallas as pl
from jax.experimental.pallas import tpu as pltpu

F32 = jnp.float32
BF16 = jnp.bfloat16

GRID_W = 64
ROPE_THETA = 10000.0
NORM_EPS = 1e-6
MLA_HEADS = 8
MLA_Q_LORA = 768
MLA_KV_LORA = 512
MLA_NOPE = 128
MLA_ROPE = 64
MLA_V = 128
GQA_HEADS = 8
GQA_KV_HEADS = 2
GQA_HEAD_DIM = 128
ADAM_LR = 0.001
ADAM_B1 = 0.9
ADAM_B2 = 0.999
ADAM_EPS = 1e-08
ADAM_WD = 0.01
ADAM_STEP = 10

N_DEV = 8
LANE = 128
MLA_SLOT = 2 * LANE
VMEM_LIMIT = 56 * 1024 * 1024
ROW_BLOCK = 256
ATT_Q_BLOCK = 512
ATT_Q_BLOCK_FWD = 1024
LN2 = 0.6931471805599453
LOG2E = 1.4426950408889634
MESH_ID = pl.DeviceIdType.MESH


def _tile(n, pref, align=LANE):
    if n <= pref:
        return n
    best = None
    t = align
    while t <= pref:
        if n % t == 0:
            best = t
        t += align
    assert best is not None, (n, pref, align)
    return best


def _cparams(sem=None):
    return pltpu.CompilerParams(dimension_semantics=sem, vmem_limit_bytes=VMEM_LIMIT)


_ORDER_AFTER = []


def _after(*arrays):
    _ORDER_AFTER.extend(arrays)


def _pcall(body, *, in_specs, **kw):
    deps = tuple(_ORDER_AFTER)
    _ORDER_AFTER.clear()
    if not deps:
        return pl.pallas_call(body, in_specs=in_specs, **kw)
    n_in, n_dep = len(in_specs), len(deps)

    def with_deps(*refs):
        body(*refs[:n_in], *refs[n_in + n_dep:])

    call = pl.pallas_call(with_deps, in_specs=list(in_specs) + [pl.BlockSpec(memory_space=pl.ANY)] * n_dep, **kw)
    return lambda *args: call(*args, *deps)


def _all_gather(arrs, name):
    n = len(arrs)

    def body(*refs):
        ins = refs[:n]
        outs = refs[n:2 * n]
        send_sems, recv_sems, local_sems = refs[2 * n:]
        x, y, c = lax.axis_index("x"), lax.axis_index("y"), lax.axis_index("c")
        me, sibling = (x, y, c), (x, y, 1 - c)
        chips = [(1 - x, y), (x, 1 - y), (1 - x, 1 - y)]

        def rows(a, dev):
            px, py, pc = dev
            return outs[a].at[4 * px + 2 * py + pc]

        def copy(a, k, block, to, src=None):
            return pltpu.make_async_remote_copy(
                src_ref=rows(a, block) if src is None else src,
                dst_ref=rows(a, block),
                send_sem=send_sems.at[7 * a + k],
                recv_sem=recv_sems.at[7 * a + k],
                device_id=to,
                device_id_type=MESH_ID,
            )

        mine = [pltpu.make_async_copy(ins[a], rows(a, me), local_sems.at[a]) for a in range(n)]
        for cp in mine:
            cp.start()
        first = []
        for a in range(n):
            first.append(copy(a, 0, me, sibling, src=ins[a]))
            first += [copy(a, 1 + j, me, (*chip, c), src=ins[a]) for j, chip in enumerate(chips)]
        for cp in first:
            cp.start()
        passed = []
        for j, chip in enumerate(chips):
            for a in range(n):
                copy(a, 1 + j, (*chip, c), me).wait_recv()
                fwd = copy(a, 4 + j, (*chip, c), sibling)
                fwd.start()
                passed.append(fwd)
        for a in range(n):
            copy(a, 0, sibling, me).wait_recv()
            for j, chip in enumerate(chips):
                copy(a, 4 + j, (*chip, 1 - c), me).wait_recv()
        for cp in first + passed:
            cp.wait_send()
        for cp in mine:
            cp.wait()

    any_spec = pl.BlockSpec(memory_space=pl.ANY)
    outs = _pcall(
        body,
        name=name,
        out_shape=[jax.ShapeDtypeStruct((N_DEV,) + a.shape, a.dtype) for a in arrs],
        in_specs=[any_spec] * n,
        out_specs=[any_spec] * n,
        scratch_shapes=[
            pltpu.SemaphoreType.DMA((7 * n,)),
            pltpu.SemaphoreType.DMA((7 * n,)),
            pltpu.SemaphoreType.DMA((n,)),
        ],
    )(*arrs)
    return list(outs)


_HBM = pl.BlockSpec(memory_space=pltpu.HBM)
_SEM = pl.BlockSpec(memory_space=pltpu.SEMAPHORE)
_EFFECT = pltpu.SideEffectType.DATAFLOW_SIDE_EFFECTING


def _descriptors(copies, send_sems, recv_sems):
    descs = []
    for i, (src, dst, dev) in enumerate(copies):
        if dev is None:
            descs.append(pltpu.make_async_copy(src, dst, recv_sems.at[i]))
        else:
            descs.append(pltpu.make_async_remote_copy(src_ref=src, dst_ref=dst, send_sem=send_sems.at[i],
                                                      recv_sem=recv_sems.at[i], device_id=dev, device_id_type=MESH_ID))
    return descs


def _split_start(name, arrays, copies_fn, n_copies):
    n = len(arrays)

    def body(*refs):
        send_sems, recv_sems = refs[n], refs[n + 1]
        token = refs[2 * n + 2]
        for dsc in _descriptors(copies_fn(refs[:n]), send_sems, recv_sems):
            dsc.start()
        token[...] = jnp.zeros_like(token)

    outs = _pcall(
        body,
        name=name,
        out_shape=(pltpu.SemaphoreType.DMA((n_copies,)), pltpu.SemaphoreType.DMA((n_copies,)),
                   *[pltpu.HBM(a.shape, a.dtype) for a in arrays], jax.ShapeDtypeStruct((8, LANE), F32)),
        in_specs=[_HBM] * n,
        out_specs=(_SEM, _SEM, *[_HBM] * n, pl.BlockSpec(memory_space=pltpu.VMEM)),
        input_output_aliases={i: 2 + i for i in range(n)},
        compiler_params=pltpu.CompilerParams(has_side_effects=_EFFECT),
    )(*[pltpu.with_memory_space_constraint(a, pltpu.HBM) for a in arrays])
    return outs[0], outs[1], list(outs[2:2 + n]), outs[2 + n]


def _split_wait(name, send_sems, recv_sems, arrays, copies_fn, after):
    n = len(arrays)

    def body(*refs):
        for dsc, (_, _, dev) in zip(_descriptors(copies_fn(refs[:n]), refs[n], refs[n + 1]), copies_fn(refs[:n])):
            if dev is None:
                dsc.wait()
            else:
                dsc.wait_send()
                dsc.wait_recv()

    outs = _pcall(
        body,
        name=name,
        out_shape=tuple(pltpu.HBM(a.shape, a.dtype) for a in arrays),
        in_specs=[_HBM] * n + [_SEM, _SEM, pl.BlockSpec(memory_space=pl.ANY)],
        out_specs=tuple([_HBM] * n),
        input_output_aliases={i: i for i in range(n)},
        compiler_params=pltpu.CompilerParams(has_side_effects=_EFFECT),
    )(*arrays, send_sems, recv_sems, after)
    return list(outs)


def _mesh_pos():
    x, y, c = lax.axis_index("x"), lax.axis_index("y"), lax.axis_index("c")
    return x, y, c, [(1 - x, y), (x, 1 - y), (1 - x, 1 - y)]


def _gather_ici_copies(n):
    def copies(refs):
        x, y, c, chips = _mesh_pos()
        me = 4 * x + 2 * y + c
        out = []
        for a in range(n):
            src, buf = refs[a], refs[n + a]
            out.append((src, buf.at[me], None))
            out.append((src, buf.at[me], (x, y, 1 - c)))
            out += [(src, buf.at[me], (cx, cy, c)) for cx, cy in chips[:2]]
        return out
    return copies


def _gather_pass_copies(n):
    def copies(refs):
        x, y, c, chips = _mesh_pos()
        south = c == 0
        bx, by = jnp.where(south, 1 - x, x), jnp.where(south, y, 1 - y)
        tx, ty = jnp.where(south, x, 1 - x), jnp.where(south, 1 - y, y)
        out = []
        for a in range(n):
            rows = refs[a].at[4 * bx + 2 * by + c]
            out.append((rows, rows, (tx, ty, c)))
            for cx, cy in chips[:2]:
                rows = refs[a].at[4 * cx + 2 * cy + c]
                out.append((rows, rows, (x, y, 1 - c)))
        return out
    return copies


def _gather_d2d_copies(n):
    def copies(refs):
        x, y, c, chips = _mesh_pos()
        cx, cy = chips[2]
        out = []
        for a in range(n):
            rows = refs[a].at[4 * cx + 2 * cy + c]
            out.append((rows, rows, (x, y, 1 - c)))
        return out
    return copies


def _reduce_d2d_copies(n):
    def copies(refs):
        x, y, c, _ = _mesh_pos()
        out = []
        for a in range(n):
            for k in range(4):
                out.append((refs[a].at[2 * k + (1 - c)], refs[n + a].at[k], (x, y, 1 - c)))
        return out
    return copies


def _reduce_ici_copies(n):
    def copies(refs):
        x, y, c, chips = _mesh_pos()
        mine = 2 * x + y
        out = []
        for a in range(n):
            src, land = refs[a], refs[n + a]
            out.append((src.at[mine], land.at[mine], None))
            out += [(src.at[2 * cx + cy], land.at[mine], (cx, cy, c)) for cx, cy in chips]
        return out
    return copies


def _pair_sum(send, land, c_idx, name):
    _, r, cols = send.shape
    rb = _tile(r, max(8, (1 << 22) // (send.dtype.itemsize * cols) // 8 * 8), 8)
    dt = send.dtype

    def body(c_ref, s_ref, l_ref, o_ref):
        o_ref[...] = (s_ref[...].astype(F32) + l_ref[...].astype(F32)).astype(dt)

    return pl.pallas_call(
        body,
        name=name,
        out_shape=jax.ShapeDtypeStruct((4, r, cols), dt),
        grid_spec=pltpu.PrefetchScalarGridSpec(
            num_scalar_prefetch=1,
            grid=(4, r // rb),
            in_specs=[pl.BlockSpec((None, rb, cols), lambda k, i, c_ref: (2 * k + c_ref[0], i, 0)),
                      pl.BlockSpec((None, rb, cols), lambda k, i, c_ref: (k, i, 0))],
            out_specs=pl.BlockSpec((None, rb, cols), lambda k, i, c_ref: (k, i, 0)),
        ),
        compiler_params=_cparams(("parallel", "parallel")),
    )(c_idx, send, land)


_DIMS = {
    "nn": (((1,), (0,)), ((), ())),
    "nt": (((1,), (1,)), ((), ())),
    "tn": (((0,), (0,)), ((), ())),
}


def _mm_call(a, b, *, mode, grid, a_spec, b_spec, o_spec, out_shape, acc_shape, name):
    nk = grid[2]
    out_dtype = out_shape.dtype

    def body(a_ref, b_ref, o_ref, *scratch):
        p = lax.dot_general(a_ref[...].astype(BF16), b_ref[...].astype(BF16), _DIMS[mode],
                            preferred_element_type=F32)
        if nk == 1:
            o_ref[...] = p.astype(out_dtype)
        else:
            acc = scratch[0]
            k = pl.program_id(2)

            @pl.when(k == 0)
            def _():
                acc[...] = p

            @pl.when(k > 0)
            def _():
                acc[...] += p

            @pl.when(k == nk - 1)
            def _():
                o_ref[...] = acc[...].astype(out_dtype)

    return _pcall(
        body,
        name=name,
        out_shape=out_shape,
        grid=grid,
        in_specs=[a_spec, b_spec],
        out_specs=o_spec,
        scratch_shapes=[pltpu.VMEM(acc_shape, F32)] if nk > 1 else [],
        compiler_params=_cparams(("parallel", "parallel", "arbitrary")),
    )(a, b)


def _mm(a, b, mode, out_dtype, name, tm=512, tn=512, tk=2432, a_row_off=0, rows=None):
    if mode == "nn":
        (m, k), (k2, n) = a.shape, b.shape
    elif mode == "nt":
        (m, k), (n, k2) = a.shape, b.shape
    else:
        (k, m), (k2, n) = a.shape, b.shape
        if rows is not None:
            k = k2 = rows
    assert k == k2, (a.shape, b.shape, mode)
    if mode != "tn":
        m = (m if rows is None else rows + a_row_off) - a_row_off
    tm, tn, tk = _tile(m, tm, 8), _tile(n, tn), _tile(k, tk, 8 if mode == "tn" else LANE)
    assert a_row_off % tm == 0
    ro = a_row_off // tm
    grid = (m // tm, n // tn, k // tk)
    if mode == "tn":
        a_spec = pl.BlockSpec((tk, tm), lambda i, j, kk: (kk, i))
    else:
        a_spec = pl.BlockSpec((tm, tk), lambda i, j, kk: (i + ro, kk))
    if mode == "nt":
        b_spec = pl.BlockSpec((tn, tk), lambda i, j, kk: (j, kk))
    else:
        b_spec = pl.BlockSpec((tk, tn), lambda i, j, kk: (kk, j))
    o_spec = pl.BlockSpec((tm, tn), lambda i, j, kk: (i, j))
    return _mm_call(a, b, mode=mode, grid=grid, a_spec=a_spec, b_spec=b_spec, o_spec=o_spec,
                    out_shape=jax.ShapeDtypeStruct((m, n), out_dtype), acc_shape=(tm, tn), name=name)


def _mm_cat_nt(pieces, out_dtype, name, tm=1024, tn=1024, tk=2048, rows=None):
    m = pieces[0][0].shape[0] if rows is None else rows
    n = pieces[0][1].shape[0]
    tm, tn = _tile(m, tm, 8), _tile(n, tn)
    steps, starts, s = [], [], 0
    for a, b, off in pieces:
        kp = a.shape[1]
        tkp = _tile(kp, tk)
        assert off % tkp == 0 and b.shape[0] == n
        steps.append((tkp, kp // tkp, off // tkp))
        starts.append(s)
        s += kp // tkp
    nk = s
    npc = len(pieces)

    def body(*refs):
        o_ref, acc = refs[2 * npc], refs[2 * npc + 1]
        kk = pl.program_id(2)

        @pl.when(kk == 0)
        def _():
            acc[...] = jnp.zeros_like(acc)

        for p in range(npc):
            @pl.when((kk >= starts[p]) & (kk < starts[p] + steps[p][1]))
            def _(p=p):
                acc[...] += lax.dot_general(refs[2 * p][...].astype(BF16), refs[2 * p + 1][...].astype(BF16), _DIMS["nt"],
                                            preferred_element_type=F32)

        @pl.when(kk == nk - 1)
        def _():
            o_ref[...] = acc[...].astype(out_dtype)

    in_specs, args = [], []
    for p, (a, b, off) in enumerate(pieces):
        tkp, np_, ob = steps[p]

        def rel(kk, p=p, np_=np_):
            return jnp.clip(kk - starts[p], 0, np_ - 1)

        in_specs.append(pl.BlockSpec((tm, tkp), lambda i, j, kk, rel=rel: (i, rel(kk))))
        in_specs.append(pl.BlockSpec((tn, tkp), lambda i, j, kk, rel=rel, ob=ob: (j, ob + rel(kk))))
        args += [a, b]
    return _pcall(
        body,
        name=name,
        out_shape=jax.ShapeDtypeStruct((m, n), out_dtype),
        grid=(m // tm, n // tn, nk),
        in_specs=in_specs,
        out_specs=pl.BlockSpec((tm, tn), lambda i, j, kk: (i, j)),
        scratch_shapes=[pltpu.VMEM((tm, tn), F32)],
        compiler_params=_cparams(("parallel", "parallel", "arbitrary")),
    )(*args)


def _mm_cat_tn(a, pieces, out_dtype, name, tm=1024, tn=1024, rows=None):
    k = a.shape[0] if rows is None else rows
    m = a.shape[1]
    tm = _tile(m, tm)
    starts, s = [], 0
    for b in pieces:
        assert b.shape[1] % tn == 0
        starts.append(s)
        s += b.shape[1] // tn
    nj = s
    npc = len(pieces)

    def body(*refs):
        a_ref, o_ref = refs[0], refs[1 + npc]
        j = pl.program_id(1)
        for p in range(npc):
            @pl.when((j >= starts[p]) & (j < starts[p] + pieces[p].shape[1] // tn))
            def _(p=p):
                o_ref[...] = lax.dot_general(a_ref[...].astype(BF16), refs[1 + p][...].astype(BF16), _DIMS["tn"],
                                             preferred_element_type=F32).astype(out_dtype)

    in_specs = [pl.BlockSpec((k, tm), lambda i, j: (0, i))]
    for p, b in enumerate(pieces):
        np_ = b.shape[1] // tn
        in_specs.append(pl.BlockSpec((k, tn), lambda i, j, p=p, np_=np_: (0, jnp.clip(j - starts[p], 0, np_ - 1))))
    return _pcall(
        body,
        name=name,
        out_shape=jax.ShapeDtypeStruct((m, nj * tn), out_dtype),
        grid=(m // tm, nj),
        in_specs=in_specs,
        out_specs=pl.BlockSpec((tm, tn), lambda i, j: (i, j)),
        compiler_params=_cparams(("parallel", "arbitrary")),
    )(a, *pieces)


def _mm_up_fwd(z2, w3, name, tm=1024):
    t, d = z2.shape
    nsh, _, c = w3.shape
    tm = _tile(t, tm, 8)
    return _mm_call(z2, w3, mode="nn", grid=(t // tm, nsh, 1),
                    a_spec=pl.BlockSpec((tm, d), lambda i, j, kk: (i, 0)),
                    b_spec=pl.BlockSpec((None, d, c), lambda i, j, kk: (j, 0, 0)),
                    o_spec=pl.BlockSpec((tm, c), lambda i, j, kk: (i, j)),
                    out_shape=jax.ShapeDtypeStruct((t, nsh * c), BF16), acc_shape=(tm, c), name=name)


def _mm_up_dz(du3, w3, name, tm=512, tn=1024):
    _, t, f = du3.shape
    nsh, d, c = w3.shape
    half = nsh // 2
    assert f == half * c
    tm, tn = _tile(t, tm, 8), _tile(d, tn)

    def body(a_ref, b_ref, o_ref, acc):
        kk = pl.program_id(2)
        p = None
        for s in range(half):
            q = lax.dot_general(a_ref[:, s * c:(s + 1) * c], b_ref[s], _DIMS["nt"], preferred_element_type=F32)
            p = q if p is None else p + q

        @pl.when(kk == 0)
        def _():
            acc[...] = p

        @pl.when(kk == 1)
        def _():
            o_ref[...] = (acc[...] + p).astype(BF16)

    return _pcall(
        body,
        name=name,
        out_shape=jax.ShapeDtypeStruct((t, d), BF16),
        grid=(t // tm, d // tn, 2),
        in_specs=[pl.BlockSpec((None, tm, f), lambda i, j, kk: (kk, i, 0)),
                  pl.BlockSpec((half, tn, c), lambda i, j, kk: (kk, j, 0))],
        out_specs=pl.BlockSpec((tm, tn), lambda i, j, kk: (i, j)),
        scratch_shapes=[pltpu.VMEM((tm, tn), F32)],
        compiler_params=_cparams(("parallel", "parallel", "arbitrary")),
    )(du3, w3)


def _mm_sum_nn(pieces, out_dtype, name, tm=512, tn=512, rows=None):
    m = pieces[0][0].shape[0] if rows is None else rows
    n = pieces[0][2].shape[1]
    tm, tn = _tile(m, tm, 8), _tile(n, tn)
    npc = len(pieces)

    def body(*refs):
        p = None
        for s in range(npc):
            q = jnp.dot(refs[2 * s][...].astype(BF16), refs[2 * s + 1][...].astype(BF16), preferred_element_type=F32)
            p = q if p is None else p + q
        refs[2 * npc][...] = p.astype(out_dtype)

    in_specs, args = [], []
    for a, ao, b, bo, kp in pieces:
        assert ao % kp == 0 and bo % kp == 0 and b.shape[1] == n
        in_specs.append(pl.BlockSpec((tm, kp), lambda i, j, ab=ao // kp: (i, ab)))
        in_specs.append(pl.BlockSpec((kp, tn), lambda i, j, bb=bo // kp: (bb, j)))
        args += [a, b]
    return _pcall(
        body,
        name=name,
        out_shape=jax.ShapeDtypeStruct((m, n), out_dtype),
        grid=(m // tm, n // tn),
        in_specs=in_specs,
        out_specs=pl.BlockSpec((tm, tn), lambda i, j: (i, j)),
        compiler_params=_cparams(("parallel", "parallel")),
    )(*args)


def _mm_rows_tn(pieces, b, out_dtype, name, tm=1024, tn=1024, rows=None):
    k = b.shape[0] if rows is None else rows
    n = b.shape[1]
    tn = _tile(n, tn)
    starts, s = [], 0
    for a in pieces:
        assert a.shape[1] % tm == 0
        starts.append(s)
        s += a.shape[1] // tm
    ni = s
    npc = len(pieces)

    def body(*refs):
        b_ref, o_ref = refs[npc], refs[npc + 1]
        i = pl.program_id(0)
        for p in range(npc):
            @pl.when((i >= starts[p]) & (i < starts[p] + pieces[p].shape[1] // tm))
            def _(p=p):
                o_ref[...] = lax.dot_general(refs[p][...].astype(BF16), b_ref[...].astype(BF16), _DIMS["tn"],
                                             preferred_element_type=F32).astype(out_dtype)

    in_specs = []
    for p, a in enumerate(pieces):
        np_ = a.shape[1] // tm
        in_specs.append(pl.BlockSpec((k, tm), lambda i, j, p=p, np_=np_: (0, jnp.clip(i - starts[p], 0, np_ - 1))))
    in_specs.append(pl.BlockSpec((k, tn), lambda i, j: (0, j)))
    return _pcall(
        body,
        name=name,
        out_shape=jax.ShapeDtypeStruct((ni * tm, n), out_dtype),
        grid=(ni, n // tn),
        in_specs=in_specs,
        out_specs=pl.BlockSpec((tm, tn), lambda i, j: (i, j)),
        compiler_params=_cparams(("parallel", "parallel")),
    )(*pieces, b)


def _mm_tn_shards(a, b, nsh, name):
    k, m = a.shape
    n = b.shape[1]
    c = n // nsh
    return _mm_call(a, b, mode="tn", grid=(1, nsh, 1),
                    a_spec=pl.BlockSpec((k, m), lambda i, j, kk: (0, 0)),
                    b_spec=pl.BlockSpec((k, c), lambda i, j, kk: (0, j)),
                    o_spec=pl.BlockSpec((None, m, c), lambda i, j, kk: (j, 0, 0)),
                    out_shape=jax.ShapeDtypeStruct((nsh, m, c), BF16), acc_shape=(m, c), name=name)


def _mm_up_gw(z2, du3, nsh, name, tm=1024):
    t, d = z2.shape
    f = du3.shape[2]
    half = nsh // 2
    c = f // half
    tm = _tile(d, tm)
    return _mm_call(z2, du3, mode="tn", grid=(d // tm, nsh, 1),
                    a_spec=pl.BlockSpec((t, tm), lambda i, j, kk: (0, i)),
                    b_spec=pl.BlockSpec((None, t, c), lambda i, j, kk: (j // half, 0, j % half)),
                    o_spec=pl.BlockSpec((None, tm, c), lambda i, j, kk: (j, i, 0)),
                    out_shape=jax.ShapeDtypeStruct((nsh, d, c), BF16), acc_shape=(tm, c), name=name)


def _rms(x):
    r = lax.rsqrt(jnp.mean(x * x, axis=-1, keepdims=True) + NORM_EPS)
    return x * r, r


def _rms_bwd(dxh, xh, r):
    return r * (dxh - xh * jnp.mean(dxh * xh, axis=-1, keepdims=True))


def _colsum(v):
    return jnp.sum(v, axis=0, keepdims=True)


def _rope(v, c, s1, s2, q):
    w = v.shape[-1]
    return v * c + pltpu.roll(v, w - q, 1) * s1 + pltpu.roll(v, q, 1) * s2


def _rope_t(d, c, s1, s2, q):
    w = d.shape[-1]
    return d * c + pltpu.roll(d * s1, q, 1) + pltpu.roll(d * s2, w - q, 1)


def _norm_mod_fwd(ctx, x, gain, mods):
    tc, d = ctx.shape
    t = x.shape[0]
    rb = min(ROW_BLOCK, tc)
    nbl = t // rb

    def body(ctx_ref, x_ref, g_ref, mod_ref, z_ref):
        i = pl.program_id(0)

        def emit(src, sh, sc):
            xh, _ = _rms(src[...])
            z_ref[...] = ((xh * g_ref[...]) * (1.0 + sc) + sh).astype(BF16)

        @pl.when(i >= nbl)
        def _():
            emit(ctx_ref, mod_ref[2:3, :], mod_ref[3:4, :])

        @pl.when(i < nbl)
        def _():
            emit(x_ref, mod_ref[0:1, :], mod_ref[1:2, :])

    return _pcall(
        body,
        name="norm1_mod_fwd",
        out_shape=jax.ShapeDtypeStruct((tc + t, d), BF16),
        grid=((tc + t) // rb,),
        in_specs=[
            pl.BlockSpec((rb, d), lambda i: (jnp.maximum(i - nbl, 0), 0)),
            pl.BlockSpec((rb, d), lambda i: (jnp.minimum(i, nbl - 1), 0)),
            pl.BlockSpec((1, d), lambda i: (0, 0)),
            pl.BlockSpec((8, d), lambda i: (0, 0)),
        ],
        out_specs=pl.BlockSpec((rb, d), lambda i: (i, 0)),
        compiler_params=_cparams(("arbitrary",)),
    )(ctx, x, gain, mods)


def _norm1_bwd(ctx, x, gain, mods, dz_ctx, dz_lat, dx1):
    tc, d = ctx.shape
    t = x.shape[0]
    rb = min(ROW_BLOCK, tc)
    nbl = t // rb

    def body(ctx_ref, x_ref, g_ref, mod_ref, dzc_ref, dzl_ref, dx1_ref, gx_ref, st_ref):
        i = pl.program_id(0)

        @pl.when(i == 0)
        def _():
            st_ref[...] = jnp.zeros_like(st_ref)

        def common(src, dz, sc, row_sh, row_sc):
            xh, r = _rms(src[...])
            g = g_ref[...]
            dxn = dz * (1.0 + sc)
            st_ref[row_sh:row_sh + 1, :] += _colsum(dz)
            st_ref[row_sc:row_sc + 1, :] += _colsum(dz * (xh * g))
            st_ref[2:3, :] += _colsum(dxn * xh)
            return _rms_bwd(dxn * g, xh, r)

        @pl.when(i >= nbl)
        def _():
            common(ctx_ref, dzc_ref[...], mod_ref[3:4, :], 3, 4)

        @pl.when(i < nbl)
        def _():
            gx_ref[...] = dx1_ref[...] + common(x_ref, dzl_ref[...], mod_ref[1:2, :], 0, 1)

    lat = lambda i: (jnp.minimum(i, nbl - 1), 0)
    cix = lambda i: (jnp.maximum(i - nbl, 0), 0)
    return _pcall(
        body,
        name="norm1_mod_bwd",
        out_shape=[jax.ShapeDtypeStruct((t, d), F32), jax.ShapeDtypeStruct((8, d), F32)],
        grid=((tc + t) // rb,),
        in_specs=[
            pl.BlockSpec((rb, d), cix),
            pl.BlockSpec((rb, d), lat),
            pl.BlockSpec((1, d), lambda i: (0, 0)),
            pl.BlockSpec((8, d), lambda i: (0, 0)),
            pl.BlockSpec((rb, d), cix),
            pl.BlockSpec((rb, d), lat),
            pl.BlockSpec((rb, d), lat),
        ],
        out_specs=[pl.BlockSpec((rb, d), lat), pl.BlockSpec((8, d), lambda i: (0, 0))],
        compiler_params=_cparams(("arbitrary",)),
    )(ctx, x, gain, mods, dz_ctx, dz_lat, dx1)


def _key_prep_fwd(kv, kv_gain, kb_gain, tabs):
    ta, wkv = kv.shape
    kvl = MLA_KV_LORA
    nb = GQA_KV_HEADS * GQA_HEAD_DIM
    rb = ROW_BLOCK if ta % ROW_BLOCK == 0 else LANE
    hd = GQA_HEAD_DIM

    def body(kv_ref, g_ref, gb_ref, ca, s1a, s2a, cb, s1b, s2b, kin_ref, kb_ref, vb_ref):
        xh, _ = _rms(kv_ref[:, 0:kvl])
        kin_ref[:, 0:kvl] = (xh * g_ref[...]).astype(BF16)
        kpe = kv_ref[:, kvl + 2 * nb:kvl + 2 * nb + LANE]
        kin_ref[:, kvl:kvl + LANE] = _rope(kpe, ca[...], s1a[...], s2a[...], MLA_ROPE // 4).astype(BF16)
        for h in range(GQA_KV_HEADS):
            nh, _ = _rms(kv_ref[:, kvl + h * hd:kvl + (h + 1) * hd])
            kb_ref[:, h * hd:(h + 1) * hd] = _rope(nh * gb_ref[...], cb[...], s1b[...], s2b[...], hd // 4).astype(BF16)
        vb_ref[...] = kv_ref[:, kvl + nb:kvl + 2 * nb].astype(BF16)

    row = lambda w: pl.BlockSpec((rb, w), lambda i: (i, 0))
    fix = lambda w: pl.BlockSpec((1, w), lambda i: (0, 0))
    return _pcall(
        body,
        name="key_prep_fwd",
        out_shape=[jax.ShapeDtypeStruct((ta, kvl + LANE), BF16), jax.ShapeDtypeStruct((ta, nb), BF16),
                   jax.ShapeDtypeStruct((ta, nb), BF16)],
        grid=(ta // rb,),
        in_specs=[row(wkv), fix(kvl), fix(hd)] + [row(LANE)] * 3 + [row(hd)] * 3,
        out_specs=[row(kvl + LANE), row(nb), row(nb)],
        compiler_params=_cparams(("parallel",)),
    )(kv, kv_gain, kb_gain, *tabs)


def _key_prep_bwd(kv, kv_gain, kb_gain, tabs, dkin, dkb, dvb):
    ta, wkv = kv.shape
    kvl = MLA_KV_LORA
    nb = GQA_KV_HEADS * GQA_HEAD_DIM
    rb = ROW_BLOCK if ta % ROW_BLOCK == 0 else LANE
    hd = GQA_HEAD_DIM

    def body(kv_ref, g_ref, gb_ref, ca, s1a, s2a, cb, s1b, s2b, dkin_ref, dkb_ref, dvb_ref, dkv_ref, st_ref, stb_ref):
        @pl.when(pl.program_id(0) == 0)
        def _():
            st_ref[...] = jnp.zeros_like(st_ref)
            stb_ref[...] = jnp.zeros_like(stb_ref)

        xh, r = _rms(kv_ref[:, 0:kvl])
        dn = dkin_ref[:, 0:kvl]
        st_ref[0:1, :] += _colsum(dn * xh)
        dkv_ref[:, 0:kvl] = _rms_bwd(dn * g_ref[...], xh, r).astype(BF16)
        dpe = _rope_t(dkin_ref[:, kvl:kvl + LANE], ca[...], s1a[...], s2a[...], MLA_ROPE // 4)
        dkv_ref[:, kvl + 2 * nb:kvl + 2 * nb + LANE] = dpe.astype(BF16)
        for h in range(GQA_KV_HEADS):
            nh, rh = _rms(kv_ref[:, kvl + h * hd:kvl + (h + 1) * hd])
            dn_h = _rope_t(dkb_ref[:, h * hd:(h + 1) * hd], cb[...], s1b[...], s2b[...], hd // 4)
            stb_ref[0:1, :] += _colsum(dn_h * nh)
            dkv_ref[:, kvl + h * hd:kvl + (h + 1) * hd] = _rms_bwd(dn_h * gb_ref[...], nh, rh).astype(BF16)
        dkv_ref[:, kvl + nb:kvl + 2 * nb] = dvb_ref[...].astype(BF16)

    row = lambda w: pl.BlockSpec((rb, w), lambda i: (i, 0))
    fix = lambda w: pl.BlockSpec((1, w), lambda i: (0, 0))
    return _pcall(
        body,
        name="key_prep_bwd",
        out_shape=[jax.ShapeDtypeStruct((ta, wkv), BF16), jax.ShapeDtypeStruct((8, kvl), F32),
                   jax.ShapeDtypeStruct((8, hd), F32)],
        grid=(ta // rb,),
        in_specs=[row(wkv), fix(kvl), fix(hd)] + [row(LANE)] * 3 + [row(hd)] * 3 + [row(kvl + LANE), row(nb), row(nb)],
        out_specs=[row(wkv), pl.BlockSpec((8, kvl), lambda i: (0, 0)), pl.BlockSpec((8, hd), lambda i: (0, 0))],
        compiler_params=_cparams(("arbitrary",)),
    )(kv, kv_gain, kb_gain, *tabs, dkin, dkb, dvb)


def _q_prep_fwd(qg, q_gain, qb_gain, tabs, qscale):
    t = qg.shape[0]
    ql = MLA_Q_LORA
    hd = GQA_HEAD_DIM
    hb = GQA_HEADS * hd
    rb = min(ROW_BLOCK, t)

    def body(q_ref, g_ref, gb_ref, cb, s1b, s2b, cqn_ref, qb_ref):
        xh, _ = _rms(q_ref[:, 0:ql])
        cqn_ref[...] = (xh * g_ref[...]).astype(BF16)
        for h in range(GQA_HEADS):
            nh, _ = _rms(q_ref[:, ql + h * hd:ql + (h + 1) * hd])
            qh = _rope(nh * gb_ref[...], cb[...], s1b[...], s2b[...], hd // 4)
            qb_ref[:, h * hd:(h + 1) * hd] = (qh * qscale).astype(BF16)

    row = lambda w: pl.BlockSpec((rb, w), lambda i: (i, 0))
    fix = lambda w: pl.BlockSpec((1, w), lambda i: (0, 0))
    return _pcall(
        body,
        name="q_prep_fwd",
        out_shape=[jax.ShapeDtypeStruct((t, ql), BF16), jax.ShapeDtypeStruct((t, hb), BF16)],
        grid=(t // rb,),
        in_specs=[row(ql + hb), fix(ql), fix(hd)] + [row(hd)] * 3,
        out_specs=[row(ql), row(hb)],
        compiler_params=_cparams(("parallel",)),
    )(qg, q_gain, qb_gain, *tabs)


def _q_prep_bwd(qg, q_gain, qb_gain, tabs, dcqn, dqb, wpad, qscale):
    t = qg.shape[0]
    ql = MLA_Q_LORA
    hd = GQA_HEAD_DIM
    hb = GQA_HEADS * hd
    rb = min(ROW_BLOCK, t)

    def body(q_ref, g_ref, gb_ref, cb, s1b, s2b, dcqn_ref, dqb_ref, dq_ref, st_ref, stb_ref):
        @pl.when(pl.program_id(0) == 0)
        def _():
            st_ref[...] = jnp.zeros_like(st_ref)
            stb_ref[...] = jnp.zeros_like(stb_ref)

        xh, r = _rms(q_ref[:, 0:ql])
        dn = dcqn_ref[...]
        st_ref[0:1, :] += _colsum(dn * xh)
        dq_ref[:, 0:ql] = _rms_bwd(dn * g_ref[...], xh, r).astype(BF16)
        for h in range(GQA_HEADS):
            nh, rh = _rms(q_ref[:, ql + h * hd:ql + (h + 1) * hd])
            dn_h = _rope_t(dqb_ref[:, h * hd:(h + 1) * hd] * qscale, cb[...], s1b[...], s2b[...], hd // 4)
            stb_ref[0:1, :] += _colsum(dn_h * nh)
            dq_ref[:, ql + h * hd:ql + (h + 1) * hd] = _rms_bwd(dn_h * gb_ref[...], nh, rh).astype(BF16)
        if wpad:
            dq_ref[:, ql + hb:ql + hb + wpad] = jnp.zeros((rb, wpad), BF16)

    row = lambda w: pl.BlockSpec((rb, w), lambda i: (i, 0))
    fix = lambda w: pl.BlockSpec((1, w), lambda i: (0, 0))
    return _pcall(
        body,
        name="q_prep_bwd",
        out_shape=[jax.ShapeDtypeStruct((t, ql + hb + wpad), BF16), jax.ShapeDtypeStruct((8, ql), F32),
                   jax.ShapeDtypeStruct((8, hd), F32)],
        grid=(t // rb,),
        in_specs=[row(ql + hb), fix(ql), fix(hd)] + [row(hd)] * 3 + [row(ql), row(hb)],
        out_specs=[row(ql + hb + wpad), pl.BlockSpec((8, ql), lambda i: (0, 0)), pl.BlockSpec((8, hd), lambda i: (0, 0))],
        compiler_params=_cparams(("arbitrary",)),
    )(qg, q_gain, qb_gain, *tabs, dcqn, dqb)


def _rope_a(v, tabs, transpose, out_dtype, name, qscale):
    t, w = v.shape
    rb = min(ROW_BLOCK, t)
    fn = _rope_t if transpose else _rope

    def body(v_ref, c, s1, s2, o_ref):
        for h in range(w // MLA_SLOT):
            sl = slice(h * MLA_SLOT, (h + 1) * MLA_SLOT)
            o_ref[:, sl] = (fn(v_ref[:, sl].astype(F32), c[...], s1[...], s2[...], MLA_ROPE // 4) * qscale).astype(out_dtype)

    row = lambda ww: pl.BlockSpec((rb, ww), lambda i: (i, 0))
    return _pcall(
        body,
        name=name,
        out_shape=jax.ShapeDtypeStruct((t, w), out_dtype),
        grid=(t // rb,),
        in_specs=[row(w)] + [row(MLA_SLOT)] * 3,
        out_specs=row(w),
        compiler_params=_cparams(("parallel",)),
    )(v, *tabs)


def _merge_fwd(pa, pb, qg, gate_blk):
    t, d = pa.shape
    rb = min(ROW_BLOCK, t)

    def body(pa_ref, pb_ref, ga_ref, gb_ref, o_ref):
        o_ref[...] = (jax.nn.sigmoid(ga_ref[...]) * pa_ref[...].astype(F32)
                      + jax.nn.sigmoid(gb_ref[...]) * pb_ref[...].astype(F32)).astype(BF16)

    row = pl.BlockSpec((rb, d), lambda i: (i, 0))
    return _pcall(
        body,
        name="merge_fwd",
        out_shape=jax.ShapeDtypeStruct((t, d), BF16),
        grid=(t // rb,),
        in_specs=[row, row, pl.BlockSpec((rb, d), lambda i: (i, gate_blk)), pl.BlockSpec((rb, d), lambda i: (i, gate_blk + 1))],
        out_specs=row,
        compiler_params=_cparams(("parallel",)),
    )(pa, pb, qg, qg)


def _merge_bwd(dm, pa, pb, qg, gate_blk):
    t, d = pa.shape
    rb = min(ROW_BLOCK, t)

    def body(dm_ref, pa_ref, pb_ref, ga_ref, gb_ref, dpa_ref, dpb_ref, dg_ref):
        dmv = dm_ref[...].astype(F32)
        sa = jax.nn.sigmoid(ga_ref[...])
        sb = jax.nn.sigmoid(gb_ref[...])
        dpa_ref[...] = (dmv * sa).astype(BF16)
        dpb_ref[...] = (dmv * sb).astype(BF16)
        dg_ref[:, 0:d] = (dmv * pa_ref[...].astype(F32) * (sa * (1.0 - sa))).astype(BF16)
        dg_ref[:, d:2 * d] = (dmv * pb_ref[...].astype(F32) * (sb * (1.0 - sb))).astype(BF16)

    row = pl.BlockSpec((rb, d), lambda i: (i, 0))
    return _pcall(
        body,
        name="merge_bwd",
        out_shape=[jax.ShapeDtypeStruct((t, d), BF16), jax.ShapeDtypeStruct((t, d), BF16),
                   jax.ShapeDtypeStruct((t, 2 * d), BF16)],
        grid=(t // rb,),
        in_specs=[row, row, row, pl.BlockSpec((rb, d), lambda i: (i, gate_blk)), pl.BlockSpec((rb, d), lambda i: (i, gate_blk + 1))],
        out_specs=[row, row, pl.BlockSpec((rb, 2 * d), lambda i: (i, 0))],
        compiler_params=_cparams(("parallel",)),
    )(dm, pa, pb, qg, qg)


def _resid_norm_mod(x, branch, gain, mods, name):
    t, d = x.shape
    rb = min(ROW_BLOCK, t)

    def body(x_ref, b_ref, g_ref, mod_ref, x1_ref, z_ref):
        x1 = x_ref[...] + mod_ref[0:1, :] * b_ref[...]
        x1_ref[...] = x1
        xh, _ = _rms(x1)
        z_ref[...] = ((xh * g_ref[...]) * (1.0 + mod_ref[2:3, :]) + mod_ref[1:2, :]).astype(BF16)

    row = pl.BlockSpec((rb, d), lambda i: (i, 0))
    return _pcall(
        body,
        name=name,
        out_shape=[jax.ShapeDtypeStruct((t, d), F32), jax.ShapeDtypeStruct((t, d), BF16)],
        grid=(t // rb,),
        in_specs=[row, row, pl.BlockSpec((1, d), lambda i: (0, 0)), pl.BlockSpec((8, d), lambda i: (0, 0))],
        out_specs=[row, row],
        compiler_params=_cparams(("parallel",)),
    )(x, branch, gain, mods)


def _norm2_bwd(x1, attn, gain, mods, dz2, dx2):
    t, d = x1.shape
    rb = min(ROW_BLOCK, t)

    def body(x1_ref, at_ref, g_ref, mod_ref, dz_ref, dx2_ref, dx1_ref, da_ref, st_ref):
        @pl.when(pl.program_id(0) == 0)
        def _():
            st_ref[...] = jnp.zeros_like(st_ref)

        xh, r = _rms(x1_ref[...])
        g = g_ref[...]
        dz = dz_ref[...].astype(F32)
        dxn = dz * (1.0 + mod_ref[1:2, :])
        st_ref[0:1, :] += _colsum(dz)
        st_ref[1:2, :] += _colsum(dz * (xh * g))
        st_ref[2:3, :] += _colsum(dxn * xh)
        dx1 = dx2_ref[...] + _rms_bwd(dxn * g, xh, r)
        dx1_ref[...] = dx1
        st_ref[3:4, :] += _colsum(dx1 * at_ref[...])
        da_ref[...] = (dx1 * mod_ref[0:1, :]).astype(BF16)

    row = pl.BlockSpec((rb, d), lambda i: (i, 0))
    return _pcall(
        body,
        name="norm2_mod_bwd",
        out_shape=[jax.ShapeDtypeStruct((t, d), F32), jax.ShapeDtypeStruct((t, d), BF16), jax.ShapeDtypeStruct((8, d), F32)],
        grid=(t // rb,),
        in_specs=[row, row, pl.BlockSpec((1, d), lambda i: (0, 0)), pl.BlockSpec((8, d), lambda i: (0, 0)), row, row],
        out_specs=[row, row, pl.BlockSpec((8, d), lambda i: (0, 0))],
        compiler_params=_cparams(("arbitrary",)),
    )(x1, attn, gain, mods, dz2, dx2)


def _final_loss(x1, ffn, gain, mods, target):
    t, d = x1.shape
    rb = min(ROW_BLOCK, t)
    nb = t // rb

    def body(x1_ref, f_ref, g_ref, mod_ref, tg_ref, dx2_ref, df_ref, st_ref):
        i = pl.program_id(0)

        @pl.when(i == 0)
        def _():
            st_ref[...] = jnp.zeros_like(st_ref)

        ffn_v = f_ref[...]
        g2 = mod_ref[0:1, :]
        x2 = x1_ref[...] + g2 * ffn_v
        xh, r = _rms(x2)
        g = g_ref[...]
        err = xh * g - tg_ref[...]
        st_ref[2:3, :] += _colsum(err * err) * (0.5 / d)
        dy = err * (1.0 / d)
        st_ref[0:1, :] += _colsum(dy * xh)
        dx2 = _rms_bwd(dy * g, xh, r)
        dx2_ref[...] = dx2
        st_ref[1:2, :] += _colsum(dx2 * ffn_v)
        df_ref[...] = (dx2 * g2).astype(BF16)

        @pl.when(i == nb - 1)
        def _():
            st_ref[3:4, :] = jnp.broadcast_to(jnp.sum(st_ref[2:3, :], axis=-1, keepdims=True), (1, d))

    row = pl.BlockSpec((rb, d), lambda i: (i, 0))
    return _pcall(
        body,
        name="final_norm_loss",
        out_shape=[jax.ShapeDtypeStruct((t, d), F32), jax.ShapeDtypeStruct((t, d), BF16), jax.ShapeDtypeStruct((8, d), F32)],
        grid=(nb,),
        in_specs=[row, row, pl.BlockSpec((1, d), lambda i: (0, 0)), pl.BlockSpec((8, d), lambda i: (0, 0)), row],
        out_specs=[row, row, pl.BlockSpec((8, d), lambda i: (0, 0))],
        compiler_params=_cparams(("arbitrary",)),
    )(x1, ffn, gain, mods, target)


def _row_ends(shape):
    rows = lax.broadcasted_iota(jnp.int32, shape, 0)
    return rows == 0, rows == shape[0] - 1


def _shift_dn(v, first):
    return jnp.where(first, 0.0, pltpu.roll(v, 1, 0))


def _shift_up(v, last):
    return jnp.where(last, 0.0, pltpu.roll(v, v.shape[0] - 1, 0))


def _conv_fwd(u, cw, cb):
    t, f2 = u.shape
    f = f2 // 2
    cbk = _tile(f, 256)
    nf = f // cbk

    def body(ua_ref, ub_ref, cwa_ref, cwb_ref, cba_ref, cbb_ref, h_ref, uc_ref):
        first, last = _row_ends((t, cbk))
        outs = []
        for u_ref, cw_ref, cb_ref in ((ua_ref, cwa_ref, cba_ref), (ub_ref, cwb_ref, cbb_ref)):
            uu, cwv = u_ref[...].astype(F32), cw_ref[...]
            outs.append(cb_ref[...] + cwv[0:1, :] * _shift_dn(uu, first) + cwv[1:2, :] * uu
                        + cwv[2:3, :] * _shift_up(uu, last))
        a, b = outs
        uc_ref[0] = a.astype(BF16)
        uc_ref[1] = b.astype(BF16)
        h_ref[...] = (a * jax.nn.sigmoid(a) * b).astype(BF16)

    ca = lambda r: pl.BlockSpec((r, cbk), lambda j: (0, j))
    cbs = lambda r: pl.BlockSpec((r, cbk), lambda j: (0, nf + j))
    return _pcall(
        body,
        name="conv_gate_fwd",
        out_shape=[jax.ShapeDtypeStruct((t, f), BF16), jax.ShapeDtypeStruct((2, t, f), BF16)],
        grid=(nf,),
        in_specs=[ca(t), cbs(t), ca(3), cbs(3), ca(1), cbs(1)],
        out_specs=[ca(t), pl.BlockSpec((2, t, cbk), lambda j: (0, 0, j))],
        compiler_params=_cparams(("parallel",)),
    )(u, u, cw, cw, cb, cb)


def _conv_bwd(u, uc, cw, dh):
    t, f2 = u.shape
    f = f2 // 2
    cbk = _tile(f, 256)
    nf = f // cbk

    def body(ua_ref, ub_ref, uc_ref, cwa_ref, cwb_ref, dh_ref, du_ref, dcw_ref, dcb_ref):
        first, last = _row_ends((t, cbk))
        a, b = uc_ref[0].astype(F32), uc_ref[1].astype(F32)
        dh_v = dh_ref[...].astype(F32)
        sg = jax.nn.sigmoid(a)
        db = dh_v * (a * sg)
        da = dh_v * b * (sg * (1.0 + a * (1.0 - sg)))
        for idx, (dv, u_ref, cw_ref) in enumerate(((da, ua_ref, cwa_ref), (db, ub_ref, cwb_ref))):
            uu, cwv = u_ref[...].astype(F32), cw_ref[...]
            up, dn = _shift_up(dv, last), _shift_dn(dv, first)
            dcb_ref[idx] = _colsum(dv)
            dcw_ref[idx, 0:1, :] = _colsum(up * uu)
            dcw_ref[idx, 1:2, :] = _colsum(dv * uu)
            dcw_ref[idx, 2:3, :] = _colsum(dn * uu)
            du_ref[idx] = (cwv[0:1, :] * up + cwv[1:2, :] * dv + cwv[2:3, :] * dn).astype(BF16)

    ca = lambda r: pl.BlockSpec((r, cbk), lambda j: (0, j))
    cbs = lambda r: pl.BlockSpec((r, cbk), lambda j: (0, nf + j))
    o3 = lambda r: pl.BlockSpec((2, r, cbk), lambda j: (0, 0, j))
    return _pcall(
        body,
        name="conv_gate_bwd",
        out_shape=[jax.ShapeDtypeStruct((2, t, f), BF16), jax.ShapeDtypeStruct((2, 3, f), F32),
                   jax.ShapeDtypeStruct((2, 1, f), F32)],
        grid=(nf,),
        in_specs=[ca(t), cbs(t), o3(t), ca(3), cbs(3), ca(t)],
        out_specs=[o3(t), o3(3), o3(1)],
        compiler_params=_cparams(("parallel",)),
    )(u, u, uc, cw, cw, dh)


def _attention_fwd(q, kk, vv, *, hq, hkv, dk, dv, k_blk0, v_blk0, name):
    t = q.shape[0]
    tk = kk.shape[0]
    g_sz = hq // hkv
    tq = min(ATT_Q_BLOCK_FWD, t)

    def body(q_ref, k_ref, v_ref, o_ref, lse_ref):
        k = k_ref[...]
        v = v_ref[...]
        for j in range(g_sz):
            s = lax.dot_general(q_ref[:, j * dk:(j + 1) * dk], k, _DIMS["nt"], preferred_element_type=F32)
            m = jnp.max(s, axis=-1, keepdims=True)
            p = jnp.exp2(s - m)
            l = jnp.sum(p, axis=-1, keepdims=True)
            o = jnp.dot(p.astype(BF16), v, preferred_element_type=F32) / l
            o_ref[:, j * dv:(j + 1) * dv] = o.astype(BF16)
            lse_ref[0, :, j:j + 1] = m + jnp.log2(l)

    return _pcall(
        body,
        name=name,
        out_shape=[jax.ShapeDtypeStruct((t, hq * dv), BF16), jax.ShapeDtypeStruct((hkv, t, g_sz), F32)],
        grid=(hkv, t // tq),
        in_specs=[
            pl.BlockSpec((tq, g_sz * dk), lambda g, i: (i, g)),
            pl.BlockSpec((tk, dk), lambda g, i: (0, k_blk0 + g)),
            pl.BlockSpec((tk, dv), lambda g, i: (0, v_blk0 + g)),
        ],
        out_specs=[
            pl.BlockSpec((tq, g_sz * dv), lambda g, i: (i, g)),
            pl.BlockSpec((1, tq, g_sz), lambda g, i: (g, i, 0)),
        ],
        compiler_params=_cparams(("parallel", "parallel")),
    )(q, kk, vv)


def _attention_bwd(q, kk, vv, do, lse, *, hq, hkv, dk, dv, k_blk0, v_blk0, name):
    t = q.shape[0]
    tk = kk.shape[0]
    g_sz = hq // hkv
    tq = min(ATT_Q_BLOCK, t)

    def body(q_ref, k_ref, v_ref, do_ref, lse_ref, dq_ref, dk_ref, dv_ref):
        @pl.when(pl.program_id(1) == 0)
        def _():
            dk_ref[...] = jnp.zeros_like(dk_ref)
            dv_ref[...] = jnp.zeros_like(dv_ref)

        k = k_ref[...]
        v = v_ref[...]
        dk_acc = dv_acc = None
        for j in range(g_sz):
            qj = q_ref[:, j * dk:(j + 1) * dk]
            doj = do_ref[:, j * dv:(j + 1) * dv]
            s = lax.dot_general(qj, k, _DIMS["nt"], preferred_element_type=F32)
            p = jnp.exp2(s - lse_ref[0, :, j:j + 1])
            dp = lax.dot_general(doj, v, _DIMS["nt"], preferred_element_type=F32)
            ds = (p * (dp - jnp.sum(p * dp, axis=-1, keepdims=True))).astype(BF16)
            dv_j = lax.dot_general(p.astype(BF16), doj, _DIMS["tn"], preferred_element_type=F32)
            dk_j = lax.dot_general(ds, qj, _DIMS["tn"], preferred_element_type=F32)
            dv_acc = dv_j if dv_acc is None else dv_acc + dv_j
            dk_acc = dk_j if dk_acc is None else dk_acc + dk_j
            dq_ref[:, j * dk:(j + 1) * dk] = jnp.dot(ds, k, preferred_element_type=F32)
        dv_ref[...] += dv_acc
        dk_ref[...] += dk_acc

        @pl.when(pl.program_id(1) == t // tq - 1)
        def _():
            dk_ref[...] *= LN2

    return _pcall(
        body,
        name=name,
        out_shape=[jax.ShapeDtypeStruct((t, hq * dk), F32), jax.ShapeDtypeStruct((tk, hkv * dk), F32),
                   jax.ShapeDtypeStruct((tk, hkv * dv), F32)],
        grid=(hkv, t // tq),
        in_specs=[
            pl.BlockSpec((tq, g_sz * dk), lambda g, i: (i, g)),
            pl.BlockSpec((tk, dk), lambda g, i: (0, k_blk0 + g)),
            pl.BlockSpec((tk, dv), lambda g, i: (0, v_blk0 + g)),
            pl.BlockSpec((tq, g_sz * dv), lambda g, i: (i, g)),
            pl.BlockSpec((1, tq, g_sz), lambda g, i: (g, i, 0)),
        ],
        out_specs=[
            pl.BlockSpec((tq, g_sz * dk), lambda g, i: (i, g)),
            pl.BlockSpec((tk, dk), lambda g, i: (0, g)),
            pl.BlockSpec((tk, dv), lambda g, i: (0, g)),
        ],
        compiler_params=_cparams(("parallel", "arbitrary")),
    )(q, kk, vv, do, lse)


def _silu(v):
    return v * jax.nn.sigmoid(v)


def _ada_fwd(conds, w_ada, b_ada_shard):
    r, d = conds.shape
    n = w_ada.shape[1]
    tn = _tile(n, 512)

    def body(c_ref, w_ref, b_ref, o_ref):
        s = _silu(c_ref[...]).astype(BF16)
        o_ref[...] = jnp.dot(s, w_ref[...].astype(BF16), preferred_element_type=F32) + b_ref[...]

    return _pcall(
        body,
        name="ada_fwd",
        out_shape=jax.ShapeDtypeStruct((r, n), F32),
        grid=(n // tn,),
        in_specs=[pl.BlockSpec((r, d), lambda j: (0, 0)), pl.BlockSpec((d, tn), lambda j: (0, j)),
                  pl.BlockSpec((1, tn), lambda j: (0, j))],
        out_specs=pl.BlockSpec((r, tn), lambda j: (0, j)),
        compiler_params=_cparams(("parallel",)),
    )(conds, w_ada, b_ada_shard)


def _cctx_partial(da16_shard, w_ada, c_ctx_row):
    d, n = w_ada.shape
    td = _tile(d, 512)

    def body(g_ref, w_ref, c_ref, o_ref):
        ds = lax.dot_general(g_ref[8:16, :].astype(BF16), w_ref[...].astype(BF16), _DIMS["nt"],
                             preferred_element_type=F32)
        cv = c_ref[...]
        sg = jax.nn.sigmoid(cv)
        o_ref[...] = ds * (sg * (1.0 + cv * (1.0 - sg)))

    return _pcall(
        body,
        name="cctx_partial",
        out_shape=jax.ShapeDtypeStruct((8, d), F32),
        grid=(d // td,),
        in_specs=[pl.BlockSpec((16, n), lambda j: (0, 0)), pl.BlockSpec((td, n), lambda j: (j, 0)),
                  pl.BlockSpec((1, td), lambda j: (0, j))],
        out_specs=pl.BlockSpec((8, td), lambda j: (0, j)),
        compiler_params=_cparams(("parallel",)),
    )(da16_shard, w_ada, c_ctx_row)


def _sum_parts(parts):
    p, _, n = parts.shape

    def body(p_ref, o_ref):
        acc = p_ref[0]
        for s in range(1, p):
            acc = acc + p_ref[s]
        o_ref[...] = acc

    return _pcall(
        body,
        name="sum_parts",
        out_shape=jax.ShapeDtypeStruct((1, n), F32),
        in_specs=[pl.BlockSpec(memory_space=pltpu.VMEM)],
        out_specs=pl.BlockSpec(memory_space=pltpu.VMEM),
    )(parts)


def _adam_math(w, g, m, v):
    m2 = ADAM_B1 * m + (1.0 - ADAM_B1) * g
    v2 = ADAM_B2 * v + (1.0 - ADAM_B2) * jnp.square(g)
    m_hat = m2 / (1.0 - ADAM_B1 ** ADAM_STEP)
    v_hat = v2 / (1.0 - ADAM_B2 ** ADAM_STEP)
    delta = -ADAM_LR * (m_hat / (jnp.sqrt(v_hat) + ADAM_EPS) + ADAM_WD * w)
    return delta, m2, v2


def _adamw(parts, w, m, v, name):
    p, r, c = parts.shape
    block_elems = 1 << 18
    rb, cb = _tile(r, max(8, block_elems // c // 8 * 8), 8), c
    if rb * c < block_elems // 4 and r * c > block_elems:
        rb, cb = r, _tile(c, max(LANE, block_elems // r // LANE * LANE))

    def body(p_ref, w_ref, m_ref, v_ref, g_ref, d_ref, m2_ref, v2_ref):
        g = p_ref[0].astype(F32)
        for s in range(1, p):
            g = g + p_ref[s].astype(F32)
        g_ref[...] = g
        d_ref[...], m2_ref[...], v2_ref[...] = _adam_math(w_ref[...], g, m_ref[...], v_ref[...])

    if w.ndim == 3:
        blk = pl.BlockSpec((None, rb, cb), lambda i, j: (0, i, j))
    else:
        blk = pl.BlockSpec((rb, cb), lambda i, j: (i, j))
    return _pcall(
        body,
        name=name,
        out_shape=[jax.ShapeDtypeStruct(w.shape, F32)] * 4,
        grid=(r // rb, c // cb),
        in_specs=[pl.BlockSpec((p, rb, cb), lambda i, j: (0, i, j)), blk, blk, blk],
        out_specs=[blk] * 4,
        compiler_params=_cparams(("parallel", "parallel")),
    )(parts, w, m, v)


def _adamw_ada(conds, da16, w, m, v):
    d, n = w.shape
    rb = _tile(d, 256, LANE)

    def body(s_ref, da_ref, w_ref, m_ref, v_ref, g_ref, d_ref, m2_ref, v2_ref):
        g = lax.dot_general(_silu(s_ref[...]).astype(BF16), da_ref[...].astype(BF16), _DIMS["tn"],
                            preferred_element_type=F32)
        g_ref[...] = g
        d_ref[...], m2_ref[...], v2_ref[...] = _adam_math(w_ref[...], g, m_ref[...], v_ref[...])

    row = pl.BlockSpec((rb, n), lambda i: (i, 0))
    return _pcall(
        body,
        name="adamw_w_ada",
        out_shape=[jax.ShapeDtypeStruct((d, n), F32)] * 4,
        grid=(d // rb,),
        in_specs=[pl.BlockSpec((16, rb), lambda i: (0, i)), pl.BlockSpec((16, n), lambda i: (0, 0)), row, row, row],
        out_specs=[row] * 4,
        compiler_params=_cparams(("parallel",)),
    )(conds, da16, w, m, v)


def _cast_bf16(a, name):
    _, r, c = a.shape
    rb, cb = _tile(r, 512, 8), c
    if rb < 64 < r:
        rb, cb = r, _tile(c, 512)

    def body(a_ref, o_ref):
        o_ref[...] = a_ref[...].astype(BF16)

    return _pcall(body, name=name, out_shape=jax.ShapeDtypeStruct((r, c), BF16), grid=(r // rb, c // cb),
                  in_specs=[pl.BlockSpec((None, rb, cb), lambda i, j: (0, i, j))],
                  out_specs=pl.BlockSpec((rb, cb), lambda i, j: (i, j)),
                  compiler_params=_cparams(("parallel", "parallel")))(a)


def _rope_tabs(t, rot):
    half, q = rot // 2, rot // 4
    n_rows = t // GRID_W
    row = jnp.repeat(jnp.arange(n_rows, dtype=F32), GRID_W)
    col = jnp.tile(jnp.arange(GRID_W, dtype=F32), n_rows)
    inv_freq = ROPE_THETA ** (-jnp.arange(0, half, 2, dtype=F32) / half)
    ang = jnp.concatenate([row[:, None] * inv_freq, col[:, None] * inv_freq], axis=-1)
    cos, sin = jnp.cos(ang), jnp.sin(ang)
    c0, c1, s0, s1 = cos[:, :q], cos[:, q:], sin[:, :q], sin[:, q:]
    z = jnp.zeros_like(s0)
    return (jnp.concatenate([c0, c0, c1, c1], -1), jnp.concatenate([-s0, z, -s1, z], -1),
            jnp.concatenate([z, s0, z, s1], -1))


def _pad_cols(a, left, total, fill=0.0):
    return jnp.pad(a, ((0, 0), (left, total - left - a.shape[1])), constant_values=fill)


def _with_ctx_rows(tab, tc, fill):
    return jnp.concatenate([tab, jnp.full((tc, tab.shape[1]), fill, F32)], axis=0)


def kernel(x, c, ctx, c_ctx, w_ada, b_ada, norm1_g, w_in, mla_q_norm_g, w_q_up, mla_kv_norm_g, w_kv_up, gqa_q_norm_g, gqa_k_norm_g, w_br_a, w_br_b, w_out, norm2_g, w_up, conv_w, conv_b, w_down, final_norm_g, loss_target, m_c_ctx, m_w_ada, m_b_ada, m_norm1_g, m_w_in, m_mla_q_norm_g, m_w_q_up, m_mla_kv_norm_g, m_w_kv_up, m_gqa_q_norm_g, m_gqa_k_norm_g, m_w_br_a, m_w_br_b, m_w_out, m_norm2_g, m_w_up, m_conv_w, m_conv_b, m_w_down, m_final_norm_g, v_c_ctx, v_w_ada, v_b_ada, v_norm1_g, v_w_in, v_mla_q_norm_g, v_w_q_up, v_mla_kv_norm_g, v_w_kv_up, v_gqa_q_norm_g, v_gqa_k_norm_g, v_w_br_a, v_w_br_b, v_w_out, v_norm2_g, v_w_up, v_conv_w, v_conv_b, v_w_down, v_final_norm_g):
    weights = dict(c_ctx=c_ctx, w_ada=w_ada, b_ada=b_ada, norm1_g=norm1_g, w_in=w_in, mla_q_norm_g=mla_q_norm_g,
                   w_q_up=w_q_up, mla_kv_norm_g=mla_kv_norm_g, w_kv_up=w_kv_up, gqa_q_norm_g=gqa_q_norm_g,
                   gqa_k_norm_g=gqa_k_norm_g, w_br_a=w_br_a, w_br_b=w_br_b, w_out=w_out, norm2_g=norm2_g, w_up=w_up,
                   conv_w=conv_w, conv_b=conv_b, w_down=w_down, final_norm_g=final_norm_g)
    mom_m = dict(c_ctx=m_c_ctx, w_ada=m_w_ada, b_ada=m_b_ada, norm1_g=m_norm1_g, w_in=m_w_in, mla_q_norm_g=m_mla_q_norm_g,
                 w_q_up=m_w_q_up, mla_kv_norm_g=m_mla_kv_norm_g, w_kv_up=m_w_kv_up, gqa_q_norm_g=m_gqa_q_norm_g,
                 gqa_k_norm_g=m_gqa_k_norm_g, w_br_a=m_w_br_a, w_br_b=m_w_br_b, w_out=m_w_out, norm2_g=m_norm2_g,
                 w_up=m_w_up, conv_w=m_conv_w, conv_b=m_conv_b, w_down=m_w_down, final_norm_g=m_final_norm_g)
    mom_v = dict(c_ctx=v_c_ctx, w_ada=v_w_ada, b_ada=v_b_ada, norm1_g=v_norm1_g, w_in=v_w_in, mla_q_norm_g=v_mla_q_norm_g,
                 w_q_up=v_w_q_up, mla_kv_norm_g=v_mla_kv_norm_g, w_kv_up=v_w_kv_up, gqa_q_norm_g=v_gqa_q_norm_g,
                 gqa_k_norm_g=v_gqa_k_norm_g, w_br_a=v_w_br_a, w_br_b=v_w_br_b, w_out=v_w_out, norm2_g=v_norm2_g,
                 w_up=v_w_up, conv_w=v_conv_w, conv_b=v_conv_b, w_down=v_w_down, final_norm_g=v_final_norm_g)
    order = list(weights)

    my_idx = 4 * lax.axis_index("x") + 2 * lax.axis_index("y") + lax.axis_index("c")
    xs, cts, tgt = x[0], ctx[0], loss_target[0]
    t, d = xs.shape
    tc = cts.shape[0]
    ta = t + tc
    kvl, ql = MLA_KV_LORA, MLA_Q_LORA
    nb = GQA_KV_HEADS * GQA_HEAD_DIM
    hb = GQA_HEADS * GQA_HEAD_DIM
    ha = MLA_HEADS
    f2 = w_up.shape[2] * N_DEV
    ff = f2 // 2

    big = ["w_in", "w_q_up", "w_kv_up", "w_br_a", "w_br_b", "w_out", "w_up", "w_down"]
    _ORDER_AFTER.clear()
    narrow = ("w_in", "w_q_up")

    def tview(a):
        return jnp.transpose(a, (0, 2, 1))

    shards = {"w_in": _cast_bf16(tview(weights["w_in"]), "cast_w_in")}
    c_idx = jnp.reshape(lax.axis_index("c"), (1,)).astype(jnp.int32)

    def gather_start(names, dep):
        shs = [shards[n] for n in names]
        land = [lax.empty((N_DEV,) + s.shape, BF16) for s in shs]
        if dep is not None:
            _after(dep)
        s, r, arrs, tok = _split_start("gather_ici_start_" + names[0], shs + land, _gather_ici_copies(len(names)),
                                       4 * len(names))
        return dict(names=names, s=s, r=r, arrs=arrs, tok=tok)

    def gather_pass(g, after):
        n = len(g["names"])
        arrs = _split_wait("gather_ici_wait_" + g["names"][0], g["s"], g["r"], g["arrs"], _gather_ici_copies(n), after)
        s, r, bufs, tok = _split_start("gather_pass_start_" + g["names"][0], arrs[n:], _gather_pass_copies(n), 3 * n)
        g.update(s2=s, r2=r, bufs=bufs)
        return tok

    def gather_relay(g, after):
        n = len(g["names"])
        bufs = _split_wait("gather_pass_wait_" + g["names"][0], g["s2"], g["r2"], g["bufs"], _gather_pass_copies(n), after)
        s, r, bufs, tok = _split_start("gather_d2d_start_" + g["names"][0], bufs, _gather_d2d_copies(n), n)
        g.update(s3=s, r3=r, bufs=bufs)
        return tok

    def gather_finish(g, after):
        n = len(g["names"])
        bufs = _split_wait("gather_d2d_wait_" + g["names"][0], g["s3"], g["r3"], g["bufs"], _gather_d2d_copies(n), after)
        return dict(zip(g["names"], bufs))

    _after(shards["w_in"])
    c_all, cw_all = _all_gather([jnp.pad(c, ((0, 7), (0, 0))), jnp.pad(conv_w[0], ((0, 5), (0, 0)))], "gather_cond")
    conv_w_f = jnp.transpose(cw_all[:, :3, :], (1, 0, 2)).reshape(3, f2)
    conds = jnp.concatenate([c_all[:, 0, :], c_ctx[None, :], jnp.zeros((7, d), F32)], axis=0)
    ncol = w_ada.shape[2]
    b_shard = lax.dynamic_slice_in_dim(b_ada, my_idx * ncol, ncol, axis=1)
    ada_shard = _ada_fwd(conds, w_ada[0], b_shard)
    (ada_all,) = _all_gather([ada_shard], "gather_ada")
    ada = jnp.transpose(ada_all, (1, 0, 2)).reshape(16, N_DEV * ncol)
    lat = lax.dynamic_slice_in_dim(ada, my_idx, 1, axis=0).reshape(6, d)
    cxt = ada[8].reshape(6, d)
    mods1 = jnp.concatenate([lat[0:2], cxt[0:2], jnp.zeros((4, d), F32)], axis=0)
    mods2 = jnp.concatenate([lat[2:3], lat[3:4], lat[4:5], jnp.zeros((5, d), F32)], axis=0)
    mods2b = jnp.concatenate([lat[2:3], lat[4:5], jnp.zeros((6, d), F32)], axis=0)
    mods3 = jnp.concatenate([lat[5:6], jnp.zeros((7, d), F32)], axis=0)

    g0 = gather_start(["w_in"], ada_all)
    for n in big[1:]:
        _after(g0["tok"])
        shards[n] = _cast_bf16(tview(weights[n]) if n in narrow else weights[n], "cast_" + n)

    ca, s1a, s2a = _rope_tabs(t, MLA_ROPE)
    cb_, s1b, s2b = _rope_tabs(t, GQA_HEAD_DIM)
    q_tabs_a = (_pad_cols(jnp.concatenate([jnp.ones((t, MLA_NOPE), F32), ca], 1), 0, MLA_SLOT),
                _pad_cols(s1a, MLA_NOPE, MLA_SLOT), _pad_cols(s2a, MLA_NOPE, MLA_SLOT))
    q_tabs_b = (cb_, s1b, s2b)
    k_tabs = (_with_ctx_rows(_pad_cols(ca, 0, LANE), tc, 1.0), _with_ctx_rows(_pad_cols(s1a, 0, LANE), tc, 0.0),
              _with_ctx_rows(_pad_cols(s2a, 0, LANE), tc, 0.0),
              _with_ctx_rows(cb_, tc, 1.0), _with_ctx_rows(s1b, tc, 0.0), _with_ctx_rows(s2b, tc, 0.0))

    def cols_full(g):
        return jnp.transpose(g, (1, 0, 2)).reshape(g.shape[1], N_DEV * g.shape[2])

    _after(*q_tabs_a, *q_tabs_b, *k_tabs, *[shards[n] for n in big[1:]])
    tok_p0 = gather_pass(g0, mods1)
    g1 = gather_start(["w_q_up", "w_kv_up", "w_br_a", "w_br_b", "w_out"], tok_p0)
    _after(g1["tok"])
    z_all = _norm_mod_fwd(cts, xs, norm1_g, mods1)
    gathered = gather_finish(g0, gather_relay(g0, z_all))
    wt_in = gathered["w_in"].reshape(-1, d)
    o_kpe, o_kb, o_vb = kvl, kvl + MLA_ROPE, kvl + MLA_ROPE + nb
    o_q = o_vb + nb
    o_g = o_q + ql + hb
    wkv_w = kvl + 2 * nb + LANE
    wt_kv_p = jnp.concatenate([wt_in[:kvl], wt_in[o_kb:o_q], wt_in[o_kpe:o_kb],
                               jnp.zeros((LANE - MLA_ROPE, d), BF16)], axis=0)
    q_w = ql + hb
    q_pad = (-q_w) % 512 if d >= 512 else (-q_w) % d
    qw_p = q_w + q_pad
    wt_q_p = jnp.concatenate([wt_in[o_q:o_g], jnp.zeros((q_pad, d), BF16)], axis=0)
    wt_g = wt_in[o_g:]

    kv_all = _mm(z_all, wt_kv_p, "nt", F32, "proj_kv", tm=1152, tn=wkv_w)
    tok_p1 = gather_pass(g1, kv_all)
    g2 = gather_start(["w_up", "w_down"], tok_p1)
    _after(g2["tok"])
    qq = _mm(z_all, wt_q_p, "nt", F32, "proj_q", tm=1024, tn=1024, rows=t)
    _after(g2["tok"])
    gates = _mm(z_all, wt_g, "nt", F32, "proj_gates", tm=1024, tn=1024, rows=t)
    _after(g2["tok"])
    kin, k_b, v_b = _key_prep_fwd(kv_all, mla_kv_norm_g, gqa_k_norm_g, k_tabs)
    sc_a = float((MLA_NOPE + MLA_ROPE) ** -0.5) * LOG2E
    sc_b = float(GQA_HEAD_DIM ** -0.5) * LOG2E
    _after(g2["tok"])
    cqn, q_b = _q_prep_fwd(qq, mla_q_norm_g, gqa_q_norm_g, q_tabs_b, sc_b)
    _after(kin, gates, g2["tok"])
    gathered.update(gather_finish(g1, gather_relay(g1, q_b)))

    wqt_f = gathered["w_q_up"].reshape(ha, MLA_NOPE + MLA_ROPE, ql)
    wqt_ext = jnp.pad(wqt_f, ((0, 0), (0, MLA_SLOT - MLA_NOPE - MLA_ROPE), (0, 0))).reshape(ha * MLA_SLOT, ql)
    wkv_f = cols_full(gathered["w_kv_up"]).reshape(kvl, ha, MLA_NOPE + MLA_V)
    wk_slots = jnp.pad(wkv_f[:, :, :MLA_NOPE], ((0, 0), (0, 0), (0, MLA_SLOT - MLA_NOPE))).reshape(kvl, ha * MLA_SLOT)
    wv_cols = wkv_f[:, :, MLA_NOPE:].reshape(kvl, ha * MLA_V)
    e_slot = jnp.pad(jnp.eye(MLA_ROPE, dtype=BF16),
                     ((0, LANE - MLA_ROPE), (MLA_NOPE, MLA_SLOT - MLA_NOPE - MLA_ROPE)))
    e_rows = jnp.concatenate([jnp.tile(e_slot, (1, ha)), jnp.zeros((LANE, ha * MLA_V), BF16)], axis=1)
    wkv_ext = jnp.concatenate([jnp.concatenate([wk_slots, wv_cols], axis=1), e_rows], axis=0)
    w_bra = cols_full(gathered["w_br_a"])
    w_brb = cols_full(gathered["w_br_b"])
    w_out_f = gathered["w_out"].reshape(d, d)

    kv_a = _mm(kin, wkv_ext, "nn", BF16, "kv_up", tm=1152, tn=1024)
    qa_raw = _mm(cqn, wqt_ext, "nt", F32, "q_up", tm=1024, tn=1024)
    q_a = _rope_a(qa_raw, q_tabs_a, False, BF16, "rope_q_fwd", sc_a)
    att_a = dict(hq=ha, hkv=ha, dk=MLA_SLOT, dv=MLA_V, k_blk0=0, v_blk0=ha * MLA_SLOT // MLA_V)
    att_b = dict(hq=GQA_HEADS, hkv=GQA_KV_HEADS, dk=GQA_HEAD_DIM, dv=GQA_HEAD_DIM, k_blk0=0, v_blk0=0)
    o_a, lse_a = _attention_fwd(q_a, kv_a, kv_a, name="attn_a_fwd", **att_a)
    o_b, lse_b = _attention_fwd(q_b, k_b, v_b, name="attn_b_fwd", **att_b)
    _after(o_a)
    _after(gather_pass(g2, o_b))
    pa = _mm(o_a, w_bra, "nn", BF16, "br_a", tm=1024, tn=1024)
    pb = _mm(o_b, w_brb, "nn", BF16, "br_b", tm=1024, tn=1024)
    merged = _merge_fwd(pa, pb, gates, 0)
    attn = _mm(merged, w_out_f, "nn", F32, "w_out", tm=1024, tn=1024)
    _after(gather_relay(g2, attn))
    x1, z2 = _resid_norm_mod(xs, attn, norm2_g, mods2, "resid_norm2_fwd")
    ffn_w = gather_finish(g2, z2)
    w_up3 = ffn_w["w_up"]
    w_down_f = ffn_w["w_down"].reshape(ff, d)
    u = _mm_up_fwd(z2, w_up3, "w_up")
    h, uc = _conv_fwd(u, conv_w_f, conv_b)
    ffn = _mm(h, w_down_f, "nn", F32, "w_down", tm=1024, tn=1024, tk=2816)

    def to_shards(g):
        return jnp.transpose(g.reshape(g.shape[0], N_DEV, g.shape[1] // N_DEV), (1, 0, 2))

    def reduce_start(tag, names, sends):
        n = len(sends)
        land = [lax.empty((4,) + s.shape[1:], s.dtype) for s in sends]
        s, r, arrs, tok = _split_start("reduce_d2d_start_" + tag, sends + land, _reduce_d2d_copies(n), 4 * n)
        return dict(tag=tag, names=names, s=s, r=r, arrs=arrs, tok=tok)

    def reduce_relay(g, after):
        n = len(g["names"])
        arrs = _split_wait("reduce_d2d_wait_" + g["tag"], g["s"], g["r"], g["arrs"], _reduce_d2d_copies(n), after)
        sums = [_pair_sum(arrs[a], arrs[n + a], c_idx, "pair_sum_" + g["names"][a]) for a in range(n)]
        land = [lax.empty(s.shape, s.dtype) for s in sums]
        s, r, arrs2, tok = _split_start("reduce_ici_start_" + g["tag"], sums + land, _reduce_ici_copies(n), 4 * n)
        g.update(s2=s, r2=r, arrs2=arrs2)
        return tok

    def reduce_finish(g, after):
        n = len(g["names"])
        arrs2 = _split_wait("reduce_ici_wait_" + g["tag"], g["s2"], g["r2"], g["arrs2"], _reduce_ici_copies(n), after)
        return dict(zip(g["names"], arrs2[n:]))

    dx2, dffn, st_fin = _final_loss(x1, ffn, final_norm_g[None, :], mods3, tgt)
    dh = _mm(dffn, w_down_f, "nt", BF16, "d_h", tm=1024, tn=1024)
    g_w_down = _mm(h, dffn, "tn", BF16, "g_w_down", tm=512, tn=1024)
    du3, dcw, dcb = _conv_bwd(u, uc, conv_w_f, dh)
    dz2 = _mm_up_dz(du3, w_up3, "d_z2")
    g_w_up = _mm_up_gw(z2, du3, N_DEV, "g_w_up")
    g_conv_w = jnp.concatenate([dcw[0], dcw[1]], axis=1)
    r_ffn = reduce_start("ffn", ["w_down", "w_up", "conv_w"],
                         [g_w_down.reshape(N_DEV, ff // N_DEV, d), g_w_up,
                          to_shards(jnp.pad(g_conv_w, ((0, 5), (0, 0))))])
    _after(r_ffn["tok"])
    dx1, dattn, st_n2 = _norm2_bwd(x1, attn, norm2_g, mods2b, dz2, dx2)
    dmerged = _mm(dattn, w_out_f, "nt", BF16, "d_merged", tm=1024, tn=1024)
    g_w_out = _mm(merged, dattn, "tn", BF16, "g_w_out", tm=1024, tn=1024)
    dpa, dpb, dgates = _merge_bwd(dmerged, pa, pb, gates, 0)
    do_a = _mm(dpa, w_bra, "nt", BF16, "d_o_a", tm=1024, tn=1024)
    do_b = _mm(dpb, w_brb, "nt", BF16, "d_o_b", tm=1024, tn=1024)
    g_w_bra = _mm_tn_shards(o_a, dpa, N_DEV, "g_w_br_a")
    g_w_brb = _mm_tn_shards(o_b, dpb, N_DEV, "g_w_br_b")
    _after(reduce_relay(r_ffn, g_w_brb))
    dq_a, dk_a, dv_a = _attention_bwd(q_a, kv_a, kv_a, do_a, lse_a, name="attn_a_bwd", **att_a)
    dq_b, dk_b, dv_b = _attention_bwd(q_b, k_b, v_b, do_b, lse_b, name="attn_b_bwd", **att_b)
    dqa_raw = _rope_a(dq_a, q_tabs_a, True, BF16, "rope_q_bwd", sc_a * LN2)
    dcqn = _mm(dqa_raw, wqt_ext, "nn", F32, "d_cqn", tm=1024, tn=ql)
    g_wqt_ext = _mm(dqa_raw, cqn, "tn", BF16, "g_w_q_up", tm=1024, tn=ql)
    dq_p, st_q, st_qb = _q_prep_bwd(qq, mla_q_norm_g, gqa_q_norm_g, q_tabs_b, dcqn, dq_b, q_pad, sc_b * LN2)
    dkin = _mm_cat_nt([(dk_a, wkv_ext, 0), (dv_a, wkv_ext, ha * MLA_SLOT)], F32, "d_kin", tm=1152, tn=kvl + LANE)
    g_wkv_ext = _mm_cat_tn(kin, [dk_a, dv_a], BF16, "g_w_kv_up", tm=kvl + LANE, tn=min(1024, ha * MLA_V))
    dkv_p, st_kv, st_kb = _key_prep_bwd(kv_all, mla_kv_norm_g, gqa_k_norm_g, k_tabs, dkin, dk_b, dv_b)
    g_wqt = g_wqt_ext.reshape(ha, MLA_SLOT, ql)[:, :MLA_NOPE + MLA_ROPE, :].reshape(N_DEV, -1, ql)
    g_wkv = jnp.concatenate([g_wkv_ext[:kvl, :ha * MLA_SLOT].reshape(kvl, ha, MLA_SLOT)[:, :, :MLA_NOPE],
                             g_wkv_ext[:kvl, ha * MLA_SLOT:].reshape(kvl, ha, MLA_V)], axis=2).reshape(kvl, ha * (MLA_NOPE + MLA_V))
    r_mid = reduce_start("mid", ["w_out", "w_br_a", "w_br_b", "w_q_up", "w_kv_up"],
                         [g_w_out.reshape(N_DEV, d // N_DEV, d), g_w_bra, g_w_brb, g_wqt,
                          to_shards(g_wkv)])
    _after(r_mid["tok"])
    g_wkv_p = _mm(dkv_p, z_all, "tn", BF16, "g_w_in_kv", tm=wkv_w, tn=1024)
    g_wqg_p = _mm_rows_tn([dq_p, dgates], z_all, BF16, "g_w_in_qg", tm=min(1024, d), tn=1024, rows=t)
    tok_m = reduce_relay(r_mid, g_wqg_p)
    g_wt_in = jnp.concatenate([g_wkv_p[:kvl], g_wkv_p[kvl + 2 * nb:kvl + 2 * nb + MLA_ROPE],
                               g_wkv_p[kvl:kvl + 2 * nb], g_wqg_p[:q_w], g_wqg_p[q_w + q_pad:]], axis=0)
    _after(tok_m)
    r_in = reduce_start("in", ["w_in"], [g_wt_in.reshape(N_DEV, -1, d)])
    _after(r_in["tok"])
    dz_lat = _mm_sum_nn([(dq_p, 0, wt_q_p, 0, qw_p), (dgates, 0, wt_g, 0, d), (dgates, d, wt_g, d, d),
                         (dkv_p, 0, wt_kv_p, 0, wkv_w)], F32, "d_z_lat", rows=t)
    dz_ctx = _mm(dkv_p, wt_kv_p, "nn", F32, "d_z_ctx", tm=min(ROW_BLOCK, tc), tn=1024, a_row_off=t)
    grad_x, st_n1 = _norm1_bwd(cts, xs, norm1_g, mods1, dz_ctx, dz_lat, dx1)

    res = {}

    def upd(nm, parts):
        wv, mv, vv = weights[nm], mom_m[nm], mom_v[nm]
        if wv.ndim == 1:
            wv, mv, vv = (a.reshape(1, -1) for a in (wv, mv, vv))
        if nm in narrow:
            wv, mv, vv = tview(wv), tview(mv), tview(vv)
        outs = _adamw(parts, wv, mv, vv, "adamw_" + nm)
        if nm in narrow:
            outs = [tview(o_) for o_ in outs]
        res[nm] = [o_.reshape(weights[nm].shape) for o_ in outs]

    d_lat = jnp.concatenate([st_n1[0], st_n1[1], st_n2[3], st_n2[0], st_n2[1], st_fin[1]])
    d_cxt = jnp.concatenate([st_n1[3], st_n1[4], jnp.zeros((4 * d,), F32)])
    small = jnp.concatenate([d_lat, d_cxt, st_n1[2], st_q[0], st_kv[0], st_qb[0], st_kb[0], st_n2[2],
                             jnp.concatenate([dcb[0, 0], dcb[1, 0]]), st_fin[0], st_fin[3, :LANE]])
    n_small = small.shape[0]
    pad_small = (-n_small) % LANE
    (small_all,) = _all_gather([jnp.pad(small, (0, pad_small)).reshape(1, -1)], "gather_small")
    offs = {}
    o = 0
    for nm, ln in (("d_lat", 6 * d), ("d_cxt", 6 * d), ("norm1_g", d), ("mla_q_norm_g", ql), ("mla_kv_norm_g", kvl),
                   ("gqa_q_norm_g", GQA_HEAD_DIM), ("gqa_k_norm_g", GQA_HEAD_DIM), ("norm2_g", d), ("conv_b", f2),
                   ("final_norm_g", d), ("loss", LANE)):
        offs[nm] = (o, ln)
        o += ln

    def part(nm):
        a, ln = offs[nm]
        return small_all[:, :, a:a + ln]

    loss = _sum_parts(part("loss"))[0, 0]
    d_lat_all = part("d_lat")[:, 0, :]
    d_cxt_sum = _sum_parts(part("d_cxt"))
    da16 = jnp.concatenate([d_lat_all, d_cxt_sum, jnp.zeros((7, 6 * d), F32)], axis=0)
    da16_shard = lax.dynamic_slice_in_dim(da16, my_idx * ncol, ncol, axis=1)
    cc_part = _cctx_partial(da16_shard, w_ada[0], c_ctx[None, :])
    (cc_all,) = _all_gather([cc_part], "gather_cctx")
    cc_parts = cc_all[:, 0:1, :]
    tok_i = reduce_relay(r_in, cc_all)

    _after(tok_i)
    for nm in ("norm1_g", "mla_q_norm_g", "mla_kv_norm_g", "gqa_q_norm_g", "gqa_k_norm_g", "norm2_g", "conv_b",
               "final_norm_g"):
        upd(nm, part(nm))
    upd("c_ctx", cc_parts)
    b_parts = jnp.concatenate([d_lat_all[:, None, :], d_cxt_sum[None]], axis=0)
    upd("b_ada", b_parts)
    _after(tok_i)
    outs = _adamw_ada(conds, da16_shard, w_ada[0], m_w_ada[0], v_w_ada[0])
    res["w_ada"] = [o_[None] for o_ in outs]
    last = outs[0]
    done = [last]
    for grp in (r_ffn, r_mid, r_in):
        _after(*done)
        recv = reduce_finish(grp, last)
        for nm in grp["names"]:
            upd(nm, recv[nm][:, :3, :] if nm == "conv_w" else recv[nm])
            last = res[nm][0]
            done.append(last)

    return (loss, grad_x[None], *[res[n][0] for n in order], *[res[n][1] for n in order],
            *[res[n][2] for n in order], *[res[n][3] for n in order])
```

```python
import jax
import jax.numpy as jnp
from jax import lax
from jax.experimental import pallas as pl
from jax.experimental.pallas import tpu as pltpu

F32 = jnp.float32
BF16 = jnp.bfloat16

GRID_W = 64
ROPE_THETA = 10000.0
NORM_EPS = 1e-6
MLA_HEADS = 8
MLA_Q_LORA = 768
MLA_KV_LORA = 512
MLA_NOPE = 128
MLA_ROPE = 64
MLA_V = 128
GQA_HEADS = 8
GQA_KV_HEADS = 2
GQA_HEAD_DIM = 128
ADAM_LR = 0.001
ADAM_B1 = 0.9
ADAM_B2 = 0.999
ADAM_EPS = 1e-08
ADAM_WD = 0.01
ADAM_STEP = 10

N_DEV = 8
LANE = 128
MLA_SLOT = 2 * LANE
VMEM_LIMIT = 56 * 1024 * 1024
ROW_BLOCK = 256
ATT_Q_BLOCK = 512
ATT_Q_BLOCK_FWD = 1024
LN2 = 0.6931471805599453
LOG2E = 1.4426950408889634
MESH_ID = pl.DeviceIdType.MESH


def _tile(n, pref, align=LANE):
    if n <= pref:
        return n
    best = None
    t = align
    while t <= pref:
        if n % t == 0:
            best = t
        t += align
    assert best is not None, (n, pref, align)
    return best


def _cparams(sem=None):
    return pltpu.CompilerParams(dimension_semantics=sem, vmem_limit_bytes=VMEM_LIMIT)


_ORDER_AFTER = []


def _after(*arrays):
    _ORDER_AFTER.extend(arrays)


def _pcall(body, *, in_specs, **kw):
    deps = tuple(_ORDER_AFTER)
    _ORDER_AFTER.clear()
    if not deps:
        return pl.pallas_call(body, in_specs=in_specs, **kw)
    n_in, n_dep = len(in_specs), len(deps)

    def with_deps(*refs):
        body(*refs[:n_in], *refs[n_in + n_dep:])

    call = pl.pallas_call(with_deps, in_specs=list(in_specs) + [pl.BlockSpec(memory_space=pl.ANY)] * n_dep, **kw)
    return lambda *args: call(*args, *deps)


def _all_gather(arrs, name):
    n = len(arrs)

    def body(*refs):
        ins = refs[:n]
        outs = refs[n:2 * n]
        send_sems, recv_sems, local_sems = refs[2 * n:]
        x, y, c = lax.axis_index("x"), lax.axis_index("y"), lax.axis_index("c")
        me, sibling = (x, y, c), (x, y, 1 - c)
        chips = [(1 - x, y), (x, 1 - y), (1 - x, 1 - y)]

        def rows(a, dev):
            px, py, pc = dev
            return outs[a].at[4 * px + 2 * py + pc]

        def copy(a, k, block, to, src=None):
            return pltpu.make_async_remote_copy(
                src_ref=rows(a, block) if src is None else src,
                dst_ref=rows(a, block),
                send_sem=send_sems.at[7 * a + k],
                recv_sem=recv_sems.at[7 * a + k],
                device_id=to,
                device_id_type=MESH_ID,
            )

        mine = [pltpu.make_async_copy(ins[a], rows(a, me), local_sems.at[a]) for a in range(n)]
        for cp in mine:
            cp.start()
        first = []
        for a in range(n):
            first.append(copy(a, 0, me, sibling, src=ins[a]))
            first += [copy(a, 1 + j, me, (*chip, c), src=ins[a]) for j, chip in enumerate(chips)]
        for cp in first:
            cp.start()
        passed = []
        for j, chip in enumerate(chips):
            for a in range(n):
                copy(a, 1 + j, (*chip, c), me).wait_recv()
                fwd = copy(a, 4 + j, (*chip, c), sibling)
                fwd.start()
                passed.append(fwd)
        for a in range(n):
            copy(a, 0, sibling, me).wait_recv()
            for j, chip in enumerate(chips):
                copy(a, 4 + j, (*chip, 1 - c), me).wait_recv()
        for cp in first + passed:
            cp.wait_send()
        for cp in mine:
            cp.wait()

    any_spec = pl.BlockSpec(memory_space=pl.ANY)
    outs = _pcall(
        body,
        name=name,
        out_shape=[jax.ShapeDtypeStruct((N_DEV,) + a.shape, a.dtype) for a in arrs],
        in_specs=[any_spec] * n,
        out_specs=[any_spec] * n,
        scratch_shapes=[
            pltpu.SemaphoreType.DMA((7 * n,)),
            pltpu.SemaphoreType.DMA((7 * n,)),
            pltpu.SemaphoreType.DMA((n,)),
        ],
    )(*arrs)
    return list(outs)


_HBM = pl.BlockSpec(memory_space=pltpu.HBM)
_SEM = pl.BlockSpec(memory_space=pltpu.SEMAPHORE)
_EFFECT = pltpu.SideEffectType.DATAFLOW_SIDE_EFFECTING


def _descriptors(copies, send_sems, recv_sems):
    descs = []
    for i, (src, dst, dev) in enumerate(copies):
        if dev is None:
            descs.append(pltpu.make_async_copy(src, dst, recv_sems.at[i]))
        else:
            descs.append(pltpu.make_async_remote_copy(src_ref=src, dst_ref=dst, send_sem=send_sems.at[i],
                                                      recv_sem=recv_sems.at[i], device_id=dev, device_id_type=MESH_ID))
    return descs


def _split_start(name, arrays, copies_fn, n_copies):
    n = len(arrays)

    def body(*refs):
        send_sems, recv_sems = refs[n], refs[n + 1]
        token = refs[2 * n + 2]
        for dsc in _descriptors(copies_fn(refs[:n]), send_sems, recv_sems):
            dsc.start()
        token[...] = jnp.zeros_like(token)

    outs = _pcall(
        body,
        name=name,
        out_shape=(pltpu.SemaphoreType.DMA((n_copies,)), pltpu.SemaphoreType.DMA((n_copies,)),
                   *[pltpu.HBM(a.shape, a.dtype) for a in arrays], jax.ShapeDtypeStruct((8, LANE), F32)),
        in_specs=[_HBM] * n,
        out_specs=(_SEM, _SEM, *[_HBM] * n, pl.BlockSpec(memory_space=pltpu.VMEM)),
        input_output_aliases={i: 2 + i for i in range(n)},
        compiler_params=pltpu.CompilerParams(has_side_effects=_EFFECT),
    )(*[pltpu.with_memory_space_constraint(a, pltpu.HBM) for a in arrays])
    return outs[0], outs[1], list(outs[2:2 + n]), outs[2 + n]


def _split_wait(name, send_sems, recv_sems, arrays, copies_fn, after):
    n = len(arrays)

    def body(*refs):
        for dsc, (_, _, dev) in zip(_descriptors(copies_fn(refs[:n]), refs[n], refs[n + 1]), copies_fn(refs[:n])):
            if dev is None:
                dsc.wait()
            else:
                dsc.wait_send()
                dsc.wait_recv()

    outs = _pcall(
        body,
        name=name,
        out_shape=tuple(pltpu.HBM(a.shape, a.dtype) for a in arrays),
        in_specs=[_HBM] * n + [_SEM, _SEM, pl.BlockSpec(memory_space=pl.ANY)],
        out_specs=tuple([_HBM] * n),
        input_output_aliases={i: i for i in range(n)},
        compiler_params=pltpu.CompilerParams(has_side_effects=_EFFECT),
    )(*arrays, send_sems, recv_sems, after)
    return list(outs)


def _mesh_pos():
    x, y, c = lax.axis_index("x"), lax.axis_index("y"), lax.axis_index("c")
    return x, y, c, [(1 - x, y), (x, 1 - y), (1 - x, 1 - y)]


def _gather_ici_copies(n):
    def copies(refs):
        x, y, c, chips = _mesh_pos()
        me = 4 * x + 2 * y + c
        out = []
        for a in range(n):
            src, buf = refs[a], refs[n + a]
            out.append((src, buf.at[me], None))
            out.append((src, buf.at[me], (x, y, 1 - c)))
            out += [(src, buf.at[me], (cx, cy, c)) for cx, cy in chips[:2]]
        return out
    return copies


def _gather_pass_copies(n):
    def copies(refs):
        x, y, c, chips = _mesh_pos()
        south = c == 0
        bx, by = jnp.where(south, 1 - x, x), jnp.where(south, y, 1 - y)
        tx, ty = jnp.where(south, x, 1 - x), jnp.where(south, 1 - y, y)
        out = []
        for a in range(n):
            rows = refs[a].at[4 * bx + 2 * by + c]
            out.append((rows, rows, (tx, ty, c)))
            for cx, cy in chips[:2]:
                rows = refs[a].at[4 * cx + 2 * cy + c]
                out.append((rows, rows, (x, y, 1 - c)))
        return out
    return copies


def _gather_d2d_copies(n):
    def copies(refs):
        x, y, c, chips = _mesh_pos()
        cx, cy = chips[2]
        out = []
        for a in range(n):
            rows = refs[a].at[4 * cx + 2 * cy + c]
            out.append((rows, rows, (x, y, 1 - c)))
        return out
    return copies


def _reduce_d2d_copies(n):
    def copies(refs):
        x, y, c, _ = _mesh_pos()
        out = []
        for a in range(n):
            for k in range(4):
                out.append((refs[a].at[2 * k + (1 - c)], refs[n + a].at[k], (x, y, 1 - c)))
        return out
    return copies


def _reduce_ici_copies(n):
    def copies(refs):
        x, y, c, chips = _mesh_pos()
        mine = 2 * x + y
        out = []
        for a in range(n):
            src, land = refs[a], refs[n + a]
            out.append((src.at[mine], land.at[mine], None))
            out += [(src.at[2 * cx + cy], land.at[mine], (cx, cy, c)) for cx, cy in chips]
        return out
    return copies


def _pair_sum(send, land, c_idx, name):
    _, r, cols = send.shape
    rb = _tile(r, max(8, (1 << 22) // (send.dtype.itemsize * cols) // 8 * 8), 8)
    dt = send.dtype

    def body(c_ref, s_ref, l_ref, o_ref):
        o_ref[...] = (s_ref[...].astype(F32) + l_ref[...].astype(F32)).astype(dt)

    return pl.pallas_call(
        body,
        name=name,
        out_shape=jax.ShapeDtypeStruct((4, r, cols), dt),
        grid_spec=pltpu.PrefetchScalarGridSpec(
            num_scalar_prefetch=1,
            grid=(4, r // rb),
            in_specs=[pl.BlockSpec((None, rb, cols), lambda k, i, c_ref: (2 * k + c_ref[0], i, 0)),
                      pl.BlockSpec((None, rb, cols), lambda k, i, c_ref: (k, i, 0))],
            out_specs=pl.BlockSpec((None, rb, cols), lambda k, i, c_ref: (k, i, 0)),
        ),
        compiler_params=_cparams(("parallel", "parallel")),
    )(c_idx, send, land)


_DIMS = {
    "nn": (((1,), (0,)), ((), ())),
    "nt": (((1,), (1,)), ((), ())),
    "tn": (((0,), (0,)), ((), ())),
}


def _mm_call(a, b, *, mode, grid, a_spec, b_spec, o_spec, out_shape, acc_shape, name):
    nk = grid[2]
    out_dtype = out_shape.dtype

    def body(a_ref, b_ref, o_ref, *scratch):
        p = lax.dot_general(a_ref[...].astype(BF16), b_ref[...].astype(BF16), _DIMS[mode],
                            preferred_element_type=F32)
        if nk == 1:
            o_ref[...] = p.astype(out_dtype)
        else:
            acc = scratch[0]
            k = pl.program_id(2)

            @pl.when(k == 0)
            def _():
                acc[...] = p

            @pl.when(k > 0)
            def _():
                acc[...] += p

            @pl.when(k == nk - 1)
            def _():
                o_ref[...] = acc[...].astype(out_dtype)

    return _pcall(
        body,
        name=name,
        out_shape=out_shape,
        grid=grid,
        in_specs=[a_spec, b_spec],
        out_specs=o_spec,
        scratch_shapes=[pltpu.VMEM(acc_shape, F32)] if nk > 1 else [],
        compiler_params=_cparams(("parallel", "parallel", "arbitrary")),
    )(a, b)


def _mm(a, b, mode, out_dtype, name, tm=512, tn=512, tk=2432, a_row_off=0, rows=None):
    if mode == "nn":
        (m, k), (k2, n) = a.shape, b.shape
    elif mode == "nt":
        (m, k), (n, k2) = a.shape, b.shape
    else:
        (k, m), (k2, n) = a.shape, b.shape
        if rows is not None:
            k = k2 = rows
    assert k == k2, (a.shape, b.shape, mode)
    if mode != "tn":
        m = (m if rows is None else rows + a_row_off) - a_row_off
    tm, tn, tk = _tile(m, tm, 8), _tile(n, tn), _tile(k, tk, 8 if mode == "tn" else LANE)
    assert a_row_off % tm == 0
    ro = a_row_off // tm
    grid = (m // tm, n // tn, k // tk)
    if mode == "tn":
        a_spec = pl.BlockSpec((tk, tm), lambda i, j, kk: (kk, i))
    else:
        a_spec = pl.BlockSpec((tm, tk), lambda i, j, kk: (i + ro, kk))
    if mode == "nt":
        b_spec = pl.BlockSpec((tn, tk), lambda i, j, kk: (j, kk))
    else:
        b_spec = pl.BlockSpec((tk, tn), lambda i, j, kk: (kk, j))
    o_spec = pl.BlockSpec((tm, tn), lambda i, j, kk: (i, j))
    return _mm_call(a, b, mode=mode, grid=grid, a_spec=a_spec, b_spec=b_spec, o_spec=o_spec,
                    out_shape=jax.ShapeDtypeStruct((m, n), out_dtype), acc_shape=(tm, tn), name=name)


def _mm_cat_nt(pieces, out_dtype, name, tm=1024, tn=1024, tk=2048, rows=None):
    m = pieces[0][0].shape[0] if rows is None else rows
    n = pieces[0][1].shape[0]
    tm, tn = _tile(m, tm, 8), _tile(n, tn)
    steps, starts, s = [], [], 0
    for a, b, off in pieces:
        kp = a.shape[1]
        tkp = _tile(kp, tk)
        assert off % tkp == 0 and b.shape[0] == n
        steps.append((tkp, kp // tkp, off // tkp))
        starts.append(s)
        s += kp // tkp
    nk = s
    npc = len(pieces)

    def body(*refs):
        o_ref, acc = refs[2 * npc], refs[2 * npc + 1]
        kk = pl.program_id(2)

        @pl.when(kk == 0)
        def _():
            acc[...] = jnp.zeros_like(acc)

        for p in range(npc):
            @pl.when((kk >= starts[p]) & (kk < starts[p] + steps[p][1]))
            def _(p=p):
                acc[...] += lax.dot_general(refs[2 * p][...].astype(BF16), refs[2 * p + 1][...].astype(BF16), _DIMS["nt"],
                                            preferred_element_type=F32)

        @pl.when(kk == nk - 1)
        def _():
            o_ref[...] = acc[...].astype(out_dtype)

    in_specs, args = [], []
    for p, (a, b, off) in enumerate(pieces):
        tkp, np_, ob = steps[p]

        def rel(kk, p=p, np_=np_):
            return jnp.clip(kk - starts[p], 0, np_ - 1)

        in_specs.append(pl.BlockSpec((tm, tkp), lambda i, j, kk, rel=rel: (i, rel(kk))))
        in_specs.append(pl.BlockSpec((tn, tkp), lambda i, j, kk, rel=rel, ob=ob: (j, ob + rel(kk))))
        args += [a, b]
    return _pcall(
        body,
        name=name,
        out_shape=jax.ShapeDtypeStruct((m, n), out_dtype),
        grid=(m // tm, n // tn, nk),
        in_specs=in_specs,
        out_specs=pl.BlockSpec((tm, tn), lambda i, j, kk: (i, j)),
        scratch_shapes=[pltpu.VMEM((tm, tn), F32)],
        compiler_params=_cparams(("parallel", "parallel", "arbitrary")),
    )(*args)


def _mm_cat_tn(a, pieces, out_dtype, name, tm=1024, tn=1024, rows=None):
    k = a.shape[0] if rows is None else rows
    m = a.shape[1]
    tm = _tile(m, tm)
    starts, s = [], 0
    for b in pieces:
        assert b.shape[1] % tn == 0
        starts.append(s)
        s += b.shape[1] // tn
    nj = s
    npc = len(pieces)

    def body(*refs):
        a_ref, o_ref = refs[0], refs[1 + npc]
        j = pl.program_id(1)
        for p in range(npc):
            @pl.when((j >= starts[p]) & (j < starts[p] + pieces[p].shape[1] // tn))
            def _(p=p):
                o_ref[...] = lax.dot_general(a_ref[...].astype(BF16), refs[1 + p][...].astype(BF16), _DIMS["tn"],
                                             preferred_element_type=F32).astype(out_dtype)

    in_specs = [pl.BlockSpec((k, tm), lambda i, j: (0, i))]
    for p, b in enumerate(pieces):
        np_ = b.shape[1] // tn
        in_specs.append(pl.BlockSpec((k, tn), lambda i, j, p=p, np_=np_: (0, jnp.clip(j - starts[p], 0, np_ - 1))))
    return _pcall(
        body,
        name=name,
        out_shape=jax.ShapeDtypeStruct((m, nj * tn), out_dtype),
        grid=(m // tm, nj),
        in_specs=in_specs,
        out_specs=pl.BlockSpec((tm, tn), lambda i, j: (i, j)),
        compiler_params=_cparams(("parallel", "arbitrary")),
    )(a, *pieces)


def _mm_up_fwd(z2, w3, name, tm=1024):
    t, d = z2.shape
    nsh, _, c = w3.shape
    tm = _tile(t, tm, 8)
    return _mm_call(z2, w3, mode="nn", grid=(t // tm, nsh, 1),
                    a_spec=pl.BlockSpec((tm, d), lambda i, j, kk: (i, 0)),
                    b_spec=pl.BlockSpec((None, d, c), lambda i, j, kk: (j, 0, 0)),
                    o_spec=pl.BlockSpec((tm, c), lambda i, j, kk: (i, j)),
                    out_shape=jax.ShapeDtypeStruct((t, nsh * c), BF16), acc_shape=(tm, c), name=name)


def _mm_up_dz(du3, w3, name, tm=512, tn=1024):
    _, t, f = du3.shape
    nsh, d, c = w3.shape
    half = nsh // 2
    assert f == half * c
    tm, tn = _tile(t, tm, 8), _tile(d, tn)

    def body(a_ref, b_ref, o_ref, acc):
        kk = pl.program_id(2)
        p = None
        for s in range(half):
            q = lax.dot_general(a_ref[:, s * c:(s + 1) * c], b_ref[s], _DIMS["nt"], preferred_element_type=F32)
            p = q if p is None else p + q

        @pl.when(kk == 0)
        def _():
            acc[...] = p

        @pl.when(kk == 1)
        def _():
            o_ref[...] = (acc[...] + p).astype(BF16)

    return _pcall(
        body,
        name=name,
        out_shape=jax.ShapeDtypeStruct((t, d), BF16),
        grid=(t // tm, d // tn, 2),
        in_specs=[pl.BlockSpec((None, tm, f), lambda i, j, kk: (kk, i, 0)),
                  pl.BlockSpec((half, tn, c), lambda i, j, kk: (kk, j, 0))],
        out_specs=pl.BlockSpec((tm, tn), lambda i, j, kk: (i, j)),
        scratch_shapes=[pltpu.VMEM((tm, tn), F32)],
        compiler_params=_cparams(("parallel", "parallel", "arbitrary")),
    )(du3, w3)


def _mm_sum_nn(pieces, out_dtype, name, tm=512, tn=512, rows=None):
    m = pieces[0][0].shape[0] if rows is None else rows
    n = pieces[0][2].shape[1]
    tm, tn = _tile(m, tm, 8), _tile(n, tn)
    npc = len(pieces)

    def body(*refs):
        p = None
        for s in range(npc):
            q = jnp.dot(refs[2 * s][...].astype(BF16), refs[2 * s + 1][...].astype(BF16), preferred_element_type=F32)
            p = q if p is None else p + q
        refs[2 * npc][...] = p.astype(out_dtype)

    in_specs, args = [], []
    for a, ao, b, bo, kp in pieces:
        assert ao % kp == 0 and bo % kp == 0 and b.shape[1] == n
        in_specs.append(pl.BlockSpec((tm, kp), lambda i, j, ab=ao // kp: (i, ab)))
        in_specs.append(pl.BlockSpec((kp, tn), lambda i, j, bb=bo // kp: (bb, j)))
        args += [a, b]
    return _pcall(
        body,
        name=name,
        out_shape=jax.ShapeDtypeStruct((m, n), out_dtype),
        grid=(m // tm, n // tn),
        in_specs=in_specs,
        out_specs=pl.BlockSpec((tm, tn), lambda i, j: (i, j)),
        compiler_params=_cparams(("parallel", "parallel")),
    )(*args)


def _mm_rows_tn(pieces, b, out_dtype, name, tm=1024, tn=1024, rows=None):
    k = b.shape[0] if rows is None else rows
    n = b.shape[1]
    tn = _tile(n, tn)
    starts, s = [], 0
    for a in pieces:
        assert a.shape[1] % tm == 0
        starts.append(s)
        s += a.shape[1] // tm
    ni = s
    npc = len(pieces)

    def body(*refs):
        b_ref, o_ref = refs[npc], refs[npc + 1]
        i = pl.program_id(0)
        for p in range(npc):
            @pl.when((i >= starts[p]) & (i < starts[p] + pieces[p].shape[1] // tm))
            def _(p=p):
                o_ref[...] = lax.dot_general(refs[p][...].astype(BF16), b_ref[...].astype(BF16), _DIMS["tn"],
                                             preferred_element_type=F32).astype(out_dtype)

    in_specs = []
    for p, a in enumerate(pieces):
        np_ = a.shape[1] // tm
        in_specs.append(pl.BlockSpec((k, tm), lambda i, j, p=p, np_=np_: (0, jnp.clip(i - starts[p], 0, np_ - 1))))
    in_specs.append(pl.BlockSpec((k, tn), lambda i, j: (0, j)))
    return _pcall(
        body,
        name=name,
        out_shape=jax.ShapeDtypeStruct((ni * tm, n), out_dtype),
        grid=(ni, n // tn),
        in_specs=in_specs,
        out_specs=pl.BlockSpec((tm, tn), lambda i, j: (i, j)),
        compiler_params=_cparams(("parallel", "parallel")),
    )(*pieces, b)


def _mm_tn_shards(a, b, nsh, name):
    k, m = a.shape
    n = b.shape[1]
    c = n // nsh
    return _mm_call(a, b, mode="tn", grid=(1, nsh, 1),
                    a_spec=pl.BlockSpec((k, m), lambda i, j, kk: (0, 0)),
                    b_spec=pl.BlockSpec((k, c), lambda i, j, kk: (0, j)),
                    o_spec=pl.BlockSpec((None, m, c), lambda i, j, kk: (j, 0, 0)),
                    out_shape=jax.ShapeDtypeStruct((nsh, m, c), BF16), acc_shape=(m, c), name=name)


def _mm_up_gw(z2, du3, nsh, name, tm=1024):
    t, d = z2.shape
    f = du3.shape[2]
    half = nsh // 2
    c = f // half
    tm = _tile(d, tm)
    return _mm_call(z2, du3, mode="tn", grid=(d // tm, nsh, 1),
                    a_spec=pl.BlockSpec((t, tm), lambda i, j, kk: (0, i)),
                    b_spec=pl.BlockSpec((None, t, c), lambda i, j, kk: (j // half, 0, j % half)),
                    o_spec=pl.BlockSpec((None, tm, c), lambda i, j, kk: (j, i, 0)),
                    out_shape=jax.ShapeDtypeStruct((nsh, d, c), BF16), acc_shape=(tm, c), name=name)


def _rms(x):
    r = lax.rsqrt(jnp.mean(x * x, axis=-1, keepdims=True) + NORM_EPS)
    return x * r, r


def _rms_bwd(dxh, xh, r):
    return r * (dxh - xh * jnp.mean(dxh * xh, axis=-1, keepdims=True))


def _colsum(v):
    return jnp.sum(v, axis=0, keepdims=True)


def _rope(v, c, s1, s2, q):
    w = v.shape[-1]
    return v * c + pltpu.roll(v, w - q, 1) * s1 + pltpu.roll(v, q, 1) * s2


def _rope_t(d, c, s1, s2, q):
    w = d.shape[-1]
    return d * c + pltpu.roll(d * s1, q, 1) + pltpu.roll(d * s2, w - q, 1)


def _norm_mod_fwd(ctx, x, gain, mods):
    tc, d = ctx.shape
    t = x.shape[0]
    rb = min(ROW_BLOCK, tc)
    nbl = t // rb

    def body(ctx_ref, x_ref, g_ref, mod_ref, z_ref):
        i = pl.program_id(0)

        def emit(src, sh, sc):
            xh, _ = _rms(src[...])
            z_ref[...] = ((xh * g_ref[...]) * (1.0 + sc) + sh).astype(BF16)

        @pl.when(i >= nbl)
        def _():
            emit(ctx_ref, mod_ref[2:3, :], mod_ref[3:4, :])

        @pl.when(i < nbl)
        def _():
            emit(x_ref, mod_ref[0:1, :], mod_ref[1:2, :])

    return _pcall(
        body,
        name="norm1_mod_fwd",
        out_shape=jax.ShapeDtypeStruct((tc + t, d), BF16),
        grid=((tc + t) // rb,),
        in_specs=[
            pl.BlockSpec((rb, d), lambda i: (jnp.maximum(i - nbl, 0), 0)),
            pl.BlockSpec((rb, d), lambda i: (jnp.minimum(i, nbl - 1), 0)),
            pl.BlockSpec((1, d), lambda i: (0, 0)),
            pl.BlockSpec((8, d), lambda i: (0, 0)),
        ],
        out_specs=pl.BlockSpec((rb, d), lambda i: (i, 0)),
        compiler_params=_cparams(("arbitrary",)),
    )(ctx, x, gain, mods)


def _norm1_bwd(ctx, x, gain, mods, dz_ctx, dz_lat, dx1):
    tc, d = ctx.shape
    t = x.shape[0]
    rb = min(ROW_BLOCK, tc)
    nbl = t // rb

    def body(ctx_ref, x_ref, g_ref, mod_ref, dzc_ref, dzl_ref, dx1_ref, gx_ref, st_ref):
        i = pl.program_id(0)

        @pl.when(i == 0)
        def _():
            st_ref[...] = jnp.zeros_like(st_ref)

        def common(src, dz, sc, row_sh, row_sc):
            xh, r = _rms(src[...])
            g = g_ref[...]
            dxn = dz * (1.0 + sc)
            st_ref[row_sh:row_sh + 1, :] += _colsum(dz)
            st_ref[row_sc:row_sc + 1, :] += _colsum(dz * (xh * g))
            st_ref[2:3, :] += _colsum(dxn * xh)
            return _rms_bwd(dxn * g, xh, r)

        @pl.when(i >= nbl)
        def _():
            common(ctx_ref, dzc_ref[...], mod_ref[3:4, :], 3, 4)

        @pl.when(i < nbl)
        def _():
            gx_ref[...] = dx1_ref[...] + common(x_ref, dzl_ref[...], mod_ref[1:2, :], 0, 1)

    lat = lambda i: (jnp.minimum(i, nbl - 1), 0)
    cix = lambda i: (jnp.maximum(i - nbl, 0), 0)
    return _pcall(
        body,
        name="norm1_mod_bwd",
        out_shape=[jax.ShapeDtypeStruct((t, d), F32), jax.ShapeDtypeStruct((8, d), F32)],
        grid=((tc + t) // rb,),
        in_specs=[
            pl.BlockSpec((rb, d), cix),
            pl.BlockSpec((rb, d), lat),
            pl.BlockSpec((1, d), lambda i: (0, 0)),
            pl.BlockSpec((8, d), lambda i: (0, 0)),
            pl.BlockSpec((rb, d), cix),
            pl.BlockSpec((rb, d), lat),
            pl.BlockSpec((rb, d), lat),
        ],
        out_specs=[pl.BlockSpec((rb, d), lat), pl.BlockSpec((8, d), lambda i: (0, 0))],
        compiler_params=_cparams(("arbitrary",)),
    )(ctx, x, gain, mods, dz_ctx, dz_lat, dx1)


def _key_prep_fwd(kv, kv_gain, kb_gain, tabs):
    ta, wkv = kv.shape
    kvl = MLA_KV_LORA
    nb = GQA_KV_HEADS * GQA_HEAD_DIM
    rb = ROW_BLOCK if ta % ROW_BLOCK == 0 else LANE
    hd = GQA_HEAD_DIM

    def body(kv_ref, g_ref, gb_ref, ca, s1a, s2a, cb, s1b, s2b, kin_ref, kb_ref, vb_ref):
        xh, _ = _rms(kv_ref[:, 0:kvl])
        kin_ref[:, 0:kvl] = (xh * g_ref[...]).astype(BF16)
        kpe = kv_ref[:, kvl + 2 * nb:kvl + 2 * nb + LANE]
        kin_ref[:, kvl:kvl + LANE] = _rope(kpe, ca[...], s1a[...], s2a[...], MLA_ROPE // 4).astype(BF16)
        for h in range(GQA_KV_HEADS):
            nh, _ = _rms(kv_ref[:, kvl + h * hd:kvl + (h + 1) * hd])
            kb_ref[:, h * hd:(h + 1) * hd] = _rope(nh * gb_ref[...], cb[...], s1b[...], s2b[...], hd // 4).astype(BF16)
        vb_ref[...] = kv_ref[:, kvl + nb:kvl + 2 * nb].astype(BF16)

    row = lambda w: pl.BlockSpec((rb, w), lambda i: (i, 0))
    fix = lambda w: pl.BlockSpec((1, w), lambda i: (0, 0))
    return _pcall(
        body,
        name="key_prep_fwd",
        out_shape=[jax.ShapeDtypeStruct((ta, kvl + LANE), BF16), jax.ShapeDtypeStruct((ta, nb), BF16),
                   jax.ShapeDtypeStruct((ta, nb), BF16)],
        grid=(ta // rb,),
        in_specs=[row(wkv), fix(kvl), fix(hd)] + [row(LANE)] * 3 + [row(hd)] * 3,
        out_specs=[row(kvl + LANE), row(nb), row(nb)],
        compiler_params=_cparams(("parallel",)),
    )(kv, kv_gain, kb_gain, *tabs)


def _key_prep_bwd(kv, kv_gain, kb_gain, tabs, dkin, dkb, dvb):
    ta, wkv = kv.shape
    kvl = MLA_KV_LORA
    nb = GQA_KV_HEADS * GQA_HEAD_DIM
    rb = ROW_BLOCK if ta % ROW_BLOCK == 0 else LANE
    hd = GQA_HEAD_DIM

    def body(kv_ref, g_ref, gb_ref, ca, s1a, s2a, cb, s1b, s2b, dkin_ref, dkb_ref, dvb_ref, dkv_ref, st_ref, stb_ref):
        @pl.when(pl.program_id(0) == 0)
        def _():
            st_ref[...] = jnp.zeros_like(st_ref)
            stb_ref[...] = jnp.zeros_like(stb_ref)

        xh, r = _rms(kv_ref[:, 0:kvl])
        dn = dkin_ref[:, 0:kvl]
        st_ref[0:1, :] += _colsum(dn * xh)
        dkv_ref[:, 0:kvl] = _rms_bwd(dn * g_ref[...], xh, r).astype(BF16)
        dpe = _rope_t(dkin_ref[:, kvl:kvl + LANE], ca[...], s1a[...], s2a[...], MLA_ROPE // 4)
        dkv_ref[:, kvl + 2 * nb:kvl + 2 * nb + LANE] = dpe.astype(BF16)
        for h in range(GQA_KV_HEADS):
            nh, rh = _rms(kv_ref[:, kvl + h * hd:kvl + (h + 1) * hd])
            dn_h = _rope_t(dkb_ref[:, h * hd:(h + 1) * hd], cb[...], s1b[...], s2b[...], hd // 4)
            stb_ref[0:1, :] += _colsum(dn_h * nh)
            dkv_ref[:, kvl + h * hd:kvl + (h + 1) * hd] = _rms_bwd(dn_h * gb_ref[...], nh, rh).astype(BF16)
        dkv_ref[:, kvl + nb:kvl + 2 * nb] = dvb_ref[...].astype(BF16)

    row = lambda w: pl.BlockSpec((rb, w), lambda i: (i, 0))
    fix = lambda w: pl.BlockSpec((1, w), lambda i: (0, 0))
    return _pcall(
        body,
        name="key_prep_bwd",
        out_shape=[jax.ShapeDtypeStruct((ta, wkv), BF16), jax.ShapeDtypeStruct((8, kvl), F32),
                   jax.ShapeDtypeStruct((8, hd), F32)],
        grid=(ta // rb,),
        in_specs=[row(wkv), fix(kvl), fix(hd)] + [row(LANE)] * 3 + [row(hd)] * 3 + [row(kvl + LANE), row(nb), row(nb)],
        out_specs=[row(wkv), pl.BlockSpec((8, kvl), lambda i: (0, 0)), pl.BlockSpec((8, hd), lambda i: (0, 0))],
        compiler_params=_cparams(("arbitrary",)),
    )(kv, kv_gain, kb_gain, *tabs, dkin, dkb, dvb)


def _q_prep_fwd(qg, q_gain, qb_gain, tabs, qscale):
    t = qg.shape[0]
    ql = MLA_Q_LORA
    hd = GQA_HEAD_DIM
    hb = GQA_HEADS * hd
    rb = min(ROW_BLOCK, t)

    def body(q_ref, g_ref, gb_ref, cb, s1b, s2b, cqn_ref, qb_ref):
        xh, _ = _rms(q_ref[:, 0:ql])
        cqn_ref[...] = (xh * g_ref[...]).astype(BF16)
        for h in range(GQA_HEADS):
            nh, _ = _rms(q_ref[:, ql + h * hd:ql + (h + 1) * hd])
            qh = _rope(nh * gb_ref[...], cb[...], s1b[...], s2b[...], hd // 4)
            qb_ref[:, h * hd:(h + 1) * hd] = (qh * qscale).astype(BF16)

    row = lambda w: pl.BlockSpec((rb, w), lambda i: (i, 0))
    fix = lambda w: pl.BlockSpec((1, w), lambda i: (0, 0))
    return _pcall(
        body,
        name="q_prep_fwd",
        out_shape=[jax.ShapeDtypeStruct((t, ql), BF16), jax.ShapeDtypeStruct((t, hb), BF16)],
        grid=(t // rb,),
        in_specs=[row(ql + hb), fix(ql), fix(hd)] + [row(hd)] * 3,
        out_specs=[row(ql), row(hb)],
        compiler_params=_cparams(("parallel",)),
    )(qg, q_gain, qb_gain, *tabs)


def _q_prep_bwd(qg, q_gain, qb_gain, tabs, dcqn, dqb, wpad, qscale):
    t = qg.shape[0]
    ql = MLA_Q_LORA
    hd = GQA_HEAD_DIM
    hb = GQA_HEADS * hd
    rb = min(ROW_BLOCK, t)

    def body(q_ref, g_ref, gb_ref, cb, s1b, s2b, dcqn_ref, dqb_ref, dq_ref, st_ref, stb_ref):
        @pl.when(pl.program_id(0) == 0)
        def _():
            st_ref[...] = jnp.zeros_like(st_ref)
            stb_ref[...] = jnp.zeros_like(stb_ref)

        xh, r = _rms(q_ref[:, 0:ql])
        dn = dcqn_ref[...]
        st_ref[0:1, :] += _colsum(dn * xh)
        dq_ref[:, 0:ql] = _rms_bwd(dn * g_ref[...], xh, r).astype(BF16)
        for h in range(GQA_HEADS):
            nh, rh = _rms(q_ref[:, ql + h * hd:ql + (h + 1) * hd])
            dn_h = _rope_t(dqb_ref[:, h * hd:(h + 1) * hd] * qscale, cb[...], s1b[...], s2b[...], hd // 4)
            stb_ref[0:1, :] += _colsum(dn_h * nh)
            dq_ref[:, ql + h * hd:ql + (h + 1) * hd] = _rms_bwd(dn_h * gb_ref[...], nh, rh).astype(BF16)
        if wpad:
            dq_ref[:, ql + hb:ql + hb + wpad] = jnp.zeros((rb, wpad), BF16)

    row = lambda w: pl.BlockSpec((rb, w), lambda i: (i, 0))
    fix = lambda w: pl.BlockSpec((1, w), lambda i: (0, 0))
    return _pcall(
        body,
        name="q_prep_bwd",
        out_shape=[jax.ShapeDtypeStruct((t, ql + hb + wpad), BF16), jax.ShapeDtypeStruct((8, ql), F32),
                   jax.ShapeDtypeStruct((8, hd), F32)],
        grid=(t // rb,),
        in_specs=[row(ql + hb), fix(ql), fix(hd)] + [row(hd)] * 3 + [row(ql), row(hb)],
        out_specs=[row(ql + hb + wpad), pl.BlockSpec((8, ql), lambda i: (0, 0)), pl.BlockSpec((8, hd), lambda i: (0, 0))],
        compiler_params=_cparams(("arbitrary",)),
    )(qg, q_gain, qb_gain, *tabs, dcqn, dqb)


def _rope_a(v, tabs, transpose, out_dtype, name, qscale):
    t, w = v.shape
    rb = min(ROW_BLOCK, t)
    fn = _rope_t if transpose else _rope

    def body(v_ref, c, s1, s2, o_ref):
        for h in range(w // MLA_SLOT):
            sl = slice(h * MLA_SLOT, (h + 1) * MLA_SLOT)
            o_ref[:, sl] = (fn(v_ref[:, sl].astype(F32), c[...], s1[...], s2[...], MLA_ROPE // 4) * qscale).astype(out_dtype)

    row = lambda ww: pl.BlockSpec((rb, ww), lambda i: (i, 0))
    return _pcall(
        body,
        name=name,
        out_shape=jax.ShapeDtypeStruct((t, w), out_dtype),
        grid=(t // rb,),
        in_specs=[row(w)] + [row(MLA_SLOT)] * 3,
        out_specs=row(w),
        compiler_params=_cparams(("parallel",)),
    )(v, *tabs)


def _merge_fwd(pa, pb, qg, gate_blk):
    t, d = pa.shape
    rb = min(ROW_BLOCK, t)

    def body(pa_ref, pb_ref, ga_ref, gb_ref, o_ref):
        o_ref[...] = (jax.nn.sigmoid(ga_ref[...]) * pa_ref[...].astype(F32)
                      + jax.nn.sigmoid(gb_ref[...]) * pb_ref[...].astype(F32)).astype(BF16)

    row = pl.BlockSpec((rb, d), lambda i: (i, 0))
    return _pcall(
        body,
        name="merge_fwd",
        out_shape=jax.ShapeDtypeStruct((t, d), BF16),
        grid=(t // rb,),
        in_specs=[row, row, pl.BlockSpec((rb, d), lambda i: (i, gate_blk)), pl.BlockSpec((rb, d), lambda i: (i, gate_blk + 1))],
        out_specs=row,
        compiler_params=_cparams(("parallel",)),
    )(pa, pb, qg, qg)


def _merge_bwd(dm, pa, pb, qg, gate_blk):
    t, d = pa.shape
    rb = min(ROW_BLOCK, t)

    def body(dm_ref, pa_ref, pb_ref, ga_ref, gb_ref, dpa_ref, dpb_ref, dg_ref):
        dmv = dm_ref[...].astype(F32)
        sa = jax.nn.sigmoid(ga_ref[...])
        sb = jax.nn.sigmoid(gb_ref[...])
        dpa_ref[...] = (dmv * sa).astype(BF16)
        dpb_ref[...] = (dmv * sb).astype(BF16)
        dg_ref[:, 0:d] = (dmv * pa_ref[...].astype(F32) * (sa * (1.0 - sa))).astype(BF16)
        dg_ref[:, d:2 * d] = (dmv * pb_ref[...].astype(F32) * (sb * (1.0 - sb))).astype(BF16)

    row = pl.BlockSpec((rb, d), lambda i: (i, 0))
    return _pcall(
        body,
        name="merge_bwd",
        out_shape=[jax.ShapeDtypeStruct((t, d), BF16), jax.ShapeDtypeStruct((t, d), BF16),
                   jax.ShapeDtypeStruct((t, 2 * d), BF16)],
        grid=(t // rb,),
        in_specs=[row, row, row, pl.BlockSpec((rb, d), lambda i: (i, gate_blk)), pl.BlockSpec((rb, d), lambda i: (i, gate_blk + 1))],
        out_specs=[row, row, pl.BlockSpec((rb, 2 * d), lambda i: (i, 0))],
        compiler_params=_cparams(("parallel",)),
    )(dm, pa, pb, qg, qg)


def _resid_norm_mod(x, branch, gain, mods, name):
    t, d = x.shape
    rb = min(ROW_BLOCK, t)

    def body(x_ref, b_ref, g_ref, mod_ref, x1_ref, z_ref):
        x1 = x_ref[...] + mod_ref[0:1, :] * b_ref[...]
        x1_ref[...] = x1
        xh, _ = _rms(x1)
        z_ref[...] = ((xh * g_ref[...]) * (1.0 + mod_ref[2:3, :]) + mod_ref[1:2, :]).astype(BF16)

    row = pl.BlockSpec((rb, d), lambda i: (i, 0))
    return _pcall(
        body,
        name=name,
        out_shape=[jax.ShapeDtypeStruct((t, d), F32), jax.ShapeDtypeStruct((t, d), BF16)],
        grid=(t // rb,),
        in_specs=[row, row, pl.BlockSpec((1, d), lambda i: (0, 0)), pl.BlockSpec((8, d), lambda i: (0, 0))],
        out_specs=[row, row],
        compiler_params=_cparams(("parallel",)),
    )(x, branch, gain, mods)


def _norm2_bwd(x1, attn, gain, mods, dz2, dx2):
    t, d = x1.shape
    rb = min(ROW_BLOCK, t)

    def body(x1_ref, at_ref, g_ref, mod_ref, dz_ref, dx2_ref, dx1_ref, da_ref, st_ref):
        @pl.when(pl.program_id(0) == 0)
        def _():
            st_ref[...] = jnp.zeros_like(st_ref)

        xh, r = _rms(x1_ref[...])
        g = g_ref[...]
        dz = dz_ref[...].astype(F32)
        dxn = dz * (1.0 + mod_ref[1:2, :])
        st_ref[0:1, :] += _colsum(dz)
        st_ref[1:2, :] += _colsum(dz * (xh * g))
        st_ref[2:3, :] += _colsum(dxn * xh)
        dx1 = dx2_ref[...] + _rms_bwd(dxn * g, xh, r)
        dx1_ref[...] = dx1
        st_ref[3:4, :] += _colsum(dx1 * at_ref[...])
        da_ref[...] = (dx1 * mod_ref[0:1, :]).astype(BF16)

    row = pl.BlockSpec((rb, d), lambda i: (i, 0))
    return _pcall(
        body,
        name="norm2_mod_bwd",
        out_shape=[jax.ShapeDtypeStruct((t, d), F32), jax.ShapeDtypeStruct((t, d), BF16), jax.ShapeDtypeStruct((8, d), F32)],
        grid=(t // rb,),
        in_specs=[row, row, pl.BlockSpec((1, d), lambda i: (0, 0)), pl.BlockSpec((8, d), lambda i: (0, 0)), row, row],
        out_specs=[row, row, pl.BlockSpec((8, d), lambda i: (0, 0))],
        compiler_params=_cparams(("arbitrary",)),
    )(x1, attn, gain, mods, dz2, dx2)


def _final_loss(x1, ffn, gain, mods, target):
    t, d = x1.shape
    rb = min(ROW_BLOCK, t)
    nb = t // rb

    def body(x1_ref, f_ref, g_ref, mod_ref, tg_ref, dx2_ref, df_ref, st_ref):
        i = pl.program_id(0)

        @pl.when(i == 0)
        def _():
            st_ref[...] = jnp.zeros_like(st_ref)

        ffn_v = f_ref[...]
        g2 = mod_ref[0:1, :]
        x2 = x1_ref[...] + g2 * ffn_v
        xh, r = _rms(x2)
        g = g_ref[...]
        err = xh * g - tg_ref[...]
        st_ref[2:3, :] += _colsum(err * err) * (0.5 / d)
        dy = err * (1.0 / d)
        st_ref[0:1, :] += _colsum(dy * xh)
        dx2 = _rms_bwd(dy * g, xh, r)
        dx2_ref[...] = dx2
        st_ref[1:2, :] += _colsum(dx2 * ffn_v)
        df_ref[...] = (dx2 * g2).astype(BF16)

        @pl.when(i == nb - 1)
        def _():
            st_ref[3:4, :] = jnp.broadcast_to(jnp.sum(st_ref[2:3, :], axis=-1, keepdims=True), (1, d))

    row = pl.BlockSpec((rb, d), lambda i: (i, 0))
    return _pcall(
        body,
        name="final_norm_loss",
        out_shape=[jax.ShapeDtypeStruct((t, d), F32), jax.ShapeDtypeStruct((t, d), BF16), jax.ShapeDtypeStruct((8, d), F32)],
        grid=(nb,),
        in_specs=[row, row, pl.BlockSpec((1, d), lambda i: (0, 0)), pl.BlockSpec((8, d), lambda i: (0, 0)), row],
        out_specs=[row, row, pl.BlockSpec((8, d), lambda i: (0, 0))],
        compiler_params=_cparams(("arbitrary",)),
    )(x1, ffn, gain, mods, target)


def _row_ends(shape):
    rows = lax.broadcasted_iota(jnp.int32, shape, 0)
    return rows == 0, rows == shape[0] - 1


def _shift_dn(v, first):
    return jnp.where(first, 0.0, pltpu.roll(v, 1, 0))


def _shift_up(v, last):
    return jnp.where(last, 0.0, pltpu.roll(v, v.shape[0] - 1, 0))


def _conv_fwd(u, cw, cb):
    t, f2 = u.shape
    f = f2 // 2
    cbk = _tile(f, 256)
    nf = f // cbk

    def body(ua_ref, ub_ref, cwa_ref, cwb_ref, cba_ref, cbb_ref, h_ref, uc_ref):
        first, last = _row_ends((t, cbk))
        outs = []
        for u_ref, cw_ref, cb_ref in ((ua_ref, cwa_ref, cba_ref), (ub_ref, cwb_ref, cbb_ref)):
            uu, cwv = u_ref[...].astype(F32), cw_ref[...]
            outs.append(cb_ref[...] + cwv[0:1, :] * _shift_dn(uu, first) + cwv[1:2, :] * uu
                        + cwv[2:3, :] * _shift_up(uu, last))
        a, b = outs
        uc_ref[0] = a.astype(BF16)
        uc_ref[1] = b.astype(BF16)
        h_ref[...] = (a * jax.nn.sigmoid(a) * b).astype(BF16)

    ca = lambda r: pl.BlockSpec((r, cbk), lambda j: (0, j))
    cbs = lambda r: pl.BlockSpec((r, cbk), lambda j: (0, nf + j))
    return _pcall(
        body,
        name="conv_gate_fwd",
        out_shape=[jax.ShapeDtypeStruct((t, f), BF16), jax.ShapeDtypeStruct((2, t, f), BF16)],
        grid=(nf,),
        in_specs=[ca(t), cbs(t), ca(3), cbs(3), ca(1), cbs(1)],
        out_specs=[ca(t), pl.BlockSpec((2, t, cbk), lambda j: (0, 0, j))],
        compiler_params=_cparams(("parallel",)),
    )(u, u, cw, cw, cb, cb)


def _conv_bwd(u, uc, cw, dh):
    t, f2 = u.shape
    f = f2 // 2
    cbk = _tile(f, 256)
    nf = f // cbk

    def body(ua_ref, ub_ref, uc_ref, cwa_ref, cwb_ref, dh_ref, du_ref, dcw_ref, dcb_ref):
        first, last = _row_ends((t, cbk))
        a, b = uc_ref[0].astype(F32), uc_ref[1].astype(F32)
        dh_v = dh_ref[...].astype(F32)
        sg = jax.nn.sigmoid(a)
        db = dh_v * (a * sg)
        da = dh_v * b * (sg * (1.0 + a * (1.0 - sg)))
        for idx, (dv, u_ref, cw_ref) in enumerate(((da, ua_ref, cwa_ref), (db, ub_ref, cwb_ref))):
            uu, cwv = u_ref[...].astype(F32), cw_ref[...]
            up, dn = _shift_up(dv, last), _shift_dn(dv, first)
            dcb_ref[idx] = _colsum(dv)
            dcw_ref[idx, 0:1, :] = _colsum(up * uu)
            dcw_ref[idx, 1:2, :] = _colsum(dv * uu)
            dcw_ref[idx, 2:3, :] = _colsum(dn * uu)
            du_ref[idx] = (cwv[0:1, :] * up + cwv[1:2, :] * dv + cwv[2:3, :] * dn).astype(BF16)

    ca = lambda r: pl.BlockSpec((r, cbk), lambda j: (0, j))
    cbs = lambda r: pl.BlockSpec((r, cbk), lambda j: (0, nf + j))
    o3 = lambda r: pl.BlockSpec((2, r, cbk), lambda j: (0, 0, j))
    return _pcall(
        body,
        name="conv_gate_bwd",
        out_shape=[jax.ShapeDtypeStruct((2, t, f), BF16), jax.ShapeDtypeStruct((2, 3, f), F32),
                   jax.ShapeDtypeStruct((2, 1, f), F32)],
        grid=(nf,),
        in_specs=[ca(t), cbs(t), o3(t), ca(3), cbs(3), ca(t)],
        out_specs=[o3(t), o3(3), o3(1)],
        compiler_params=_cparams(("parallel",)),
    )(u, u, uc, cw, cw, dh)


def _attention_fwd(q, kk, vv, *, hq, hkv, dk, dv, k_blk0, v_blk0, name):
    t = q.shape[0]
    tk = kk.shape[0]
    g_sz = hq // hkv
    tq = min(ATT_Q_BLOCK_FWD, t)

    def body(q_ref, k_ref, v_ref, o_ref, lse_ref):
        k = k_ref[...]
        v = v_ref[...]
        for j in range(g_sz):
            s = lax.dot_general(q_ref[:, j * dk:(j + 1) * dk], k, _DIMS["nt"], preferred_element_type=F32)
            m = jnp.max(s, axis=-1, keepdims=True)
            p = jnp.exp2(s - m)
            l = jnp.sum(p, axis=-1, keepdims=True)
            o = jnp.dot(p.astype(BF16), v, preferred_element_type=F32) / l
            o_ref[:, j * dv:(j + 1) * dv] = o.astype(BF16)
            lse_ref[0, :, j:j + 1] = m + jnp.log2(l)

    return _pcall(
        body,
        name=name,
        out_shape=[jax.ShapeDtypeStruct((t, hq * dv), BF16), jax.ShapeDtypeStruct((hkv, t, g_sz), F32)],
        grid=(hkv, t // tq),
        in_specs=[
            pl.BlockSpec((tq, g_sz * dk), lambda g, i: (i, g)),
            pl.BlockSpec((tk, dk), lambda g, i: (0, k_blk0 + g)),
            pl.BlockSpec((tk, dv), lambda g, i: (0, v_blk0 + g)),
        ],
        out_specs=[
            pl.BlockSpec((tq, g_sz * dv), lambda g, i: (i, g)),
            pl.BlockSpec((1, tq, g_sz), lambda g, i: (g, i, 0)),
        ],
        compiler_params=_cparams(("parallel", "parallel")),
    )(q, kk, vv)


def _attention_bwd(q, kk, vv, do, lse, *, hq, hkv, dk, dv, k_blk0, v_blk0, name):
    t = q.shape[0]
    tk = kk.shape[0]
    g_sz = hq // hkv
    tq = min(ATT_Q_BLOCK, t)

    def body(q_ref, k_ref, v_ref, do_ref, lse_ref, dq_ref, dk_ref, dv_ref):
        @pl.when(pl.program_id(1) == 0)
        def _():
            dk_ref[...] = jnp.zeros_like(dk_ref)
            dv_ref[...] = jnp.zeros_like(dv_ref)

        k = k_ref[...]
        v = v_ref[...]
        dk_acc = dv_acc = None
        for j in range(g_sz):
            qj = q_ref[:, j * dk:(j + 1) * dk]
            doj = do_ref[:, j * dv:(j + 1) * dv]
            s = lax.dot_general(qj, k, _DIMS["nt"], preferred_element_type=F32)
            p = jnp.exp2(s - lse_ref[0, :, j:j + 1])
            dp = lax.dot_general(doj, v, _DIMS["nt"], preferred_element_type=F32)
            ds = (p * (dp - jnp.sum(p * dp, axis=-1, keepdims=True))).astype(BF16)
            dv_j = lax.dot_general(p.astype(BF16), doj, _DIMS["tn"], preferred_element_type=F32)
            dk_j = lax.dot_general(ds, qj, _DIMS["tn"], preferred_element_type=F32)
            dv_acc = dv_j if dv_acc is None else dv_acc + dv_j
            dk_acc = dk_j if dk_acc is None else dk_acc + dk_j
            dq_ref[:, j * dk:(j + 1) * dk] = jnp.dot(ds, k, preferred_element_type=F32)
        dv_ref[...] += dv_acc
        dk_ref[...] += dk_acc

        @pl.when(pl.program_id(1) == t // tq - 1)
        def _():
            dk_ref[...] *= LN2

    return _pcall(
        body,
        name=name,
        out_shape=[jax.ShapeDtypeStruct((t, hq * dk), F32), jax.ShapeDtypeStruct((tk, hkv * dk), F32),
                   jax.ShapeDtypeStruct((tk, hkv * dv), F32)],
        grid=(hkv, t // tq),
        in_specs=[
            pl.BlockSpec((tq, g_sz * dk), lambda g, i: (i, g)),
            pl.BlockSpec((tk, dk), lambda g, i: (0, k_blk0 + g)),
            pl.BlockSpec((tk, dv), lambda g, i: (0, v_blk0 + g)),
            pl.BlockSpec((tq, g_sz * dv), lambda g, i: (i, g)),
            pl.BlockSpec((1, tq, g_sz), lambda g, i: (g, i, 0)),
        ],
        out_specs=[
            pl.BlockSpec((tq, g_sz * dk), lambda g, i: (i, g)),
            pl.BlockSpec((tk, dk), lambda g, i: (0, g)),
            pl.BlockSpec((tk, dv), lambda g, i: (0, g)),
        ],
        compiler_params=_cparams(("parallel", "arbitrary")),
    )(q, kk, vv, do, lse)


def _silu(v):
    return v * jax.nn.sigmoid(v)


def _ada_fwd(conds, w_ada, b_ada_shard):
    r, d = conds.shape
    n = w_ada.shape[1]
    tn = _tile(n, 512)

    def body(c_ref, w_ref, b_ref, o_ref):
        s = _silu(c_ref[...]).astype(BF16)
        o_ref[...] = jnp.dot(s, w_ref[...].astype(BF16), preferred_element_type=F32) + b_ref[...]

    return _pcall(
        body,
        name="ada_fwd",
        out_shape=jax.ShapeDtypeStruct((r, n), F32),
        grid=(n // tn,),
        in_specs=[pl.BlockSpec((r, d), lambda j: (0, 0)), pl.BlockSpec((d, tn), lambda j: (0, j)),
                  pl.BlockSpec((1, tn), lambda j: (0, j))],
        out_specs=pl.BlockSpec((r, tn), lambda j: (0, j)),
        compiler_params=_cparams(("parallel",)),
    )(conds, w_ada, b_ada_shard)


def _cctx_partial(da16_shard, w_ada, c_ctx_row):
    d, n = w_ada.shape
    td = _tile(d, 512)

    def body(g_ref, w_ref, c_ref, o_ref):
        ds = lax.dot_general(g_ref[8:16, :].astype(BF16), w_ref[...].astype(BF16), _DIMS["nt"],
                             preferred_element_type=F32)
        cv = c_ref[...]
        sg = jax.nn.sigmoid(cv)
        o_ref[...] = ds * (sg * (1.0 + cv * (1.0 - sg)))

    return _pcall(
        body,
        name="cctx_partial",
        out_shape=jax.ShapeDtypeStruct((8, d), F32),
        grid=(d // td,),
        in_specs=[pl.BlockSpec((16, n), lambda j: (0, 0)), pl.BlockSpec((td, n), lambda j: (j, 0)),
                  pl.BlockSpec((1, td), lambda j: (0, j))],
        out_specs=pl.BlockSpec((8, td), lambda j: (0, j)),
        compiler_params=_cparams(("parallel",)),
    )(da16_shard, w_ada, c_ctx_row)


def _sum_parts(parts):
    p, _, n = parts.shape

    def body(p_ref, o_ref):
        acc = p_ref[0]
        for s in range(1, p):
            acc = acc + p_ref[s]
        o_ref[...] = acc

    return _pcall(
        body,
        name="sum_parts",
        out_shape=jax.ShapeDtypeStruct((1, n), F32),
        in_specs=[pl.BlockSpec(memory_space=pltpu.VMEM)],
        out_specs=pl.BlockSpec(memory_space=pltpu.VMEM),
    )(parts)


def _adam_math(w, g, m, v):
    m2 = ADAM_B1 * m + (1.0 - ADAM_B1) * g
    v2 = ADAM_B2 * v + (1.0 - ADAM_B2) * jnp.square(g)
    m_hat = m2 / (1.0 - ADAM_B1 ** ADAM_STEP)
    v_hat = v2 / (1.0 - ADAM_B2 ** ADAM_STEP)
    delta = -ADAM_LR * (m_hat / (jnp.sqrt(v_hat) + ADAM_EPS) + ADAM_WD * w)
    return delta, m2, v2


def _adamw(parts, w, m, v, name):
    p, r, c = parts.shape
    block_elems = 1 << 19
    rb, cb = _tile(r, max(8, block_elems // c // 8 * 8), 8), c
    if rb * c < block_elems // 4 and r * c > block_elems:
        rb, cb = r, _tile(c, max(LANE, block_elems // r // LANE * LANE))

    def body(p_ref, w_ref, m_ref, v_ref, g_ref, d_ref, m2_ref, v2_ref):
        g = p_ref[0].astype(F32)
        for s in range(1, p):
            g = g + p_ref[s].astype(F32)
        g_ref[...] = g
        d_ref[...], m2_ref[...], v2_ref[...] = _adam_math(w_ref[...], g, m_ref[...], v_ref[...])

    if w.ndim == 3:
        blk = pl.BlockSpec((None, rb, cb), lambda i, j: (0, i, j))
    else:
        blk = pl.BlockSpec((rb, cb), lambda i, j: (i, j))
    return _pcall(
        body,
        name=name,
        out_shape=[jax.ShapeDtypeStruct(w.shape, F32)] * 4,
        grid=(r // rb, c // cb),
        in_specs=[pl.BlockSpec((p, rb, cb), lambda i, j: (0, i, j)), blk, blk, blk],
        out_specs=[blk] * 4,
        compiler_params=_cparams(("parallel", "parallel")),
    )(parts, w, m, v)


def _adamw_ada(conds, da16, w, m, v):
    d, n = w.shape
    rb = _tile(d, 256, LANE)

    def body(s_ref, da_ref, w_ref, m_ref, v_ref, g_ref, d_ref, m2_ref, v2_ref):
        g = lax.dot_general(_silu(s_ref[...]).astype(BF16), da_ref[...].astype(BF16), _DIMS["tn"],
                            preferred_element_type=F32)
        g_ref[...] = g
        d_ref[...], m2_ref[...], v2_ref[...] = _adam_math(w_ref[...], g, m_ref[...], v_ref[...])

    row = pl.BlockSpec((rb, n), lambda i: (i, 0))
    return _pcall(
        body,
        name="adamw_w_ada",
        out_shape=[jax.ShapeDtypeStruct((d, n), F32)] * 4,
        grid=(d // rb,),
        in_specs=[pl.BlockSpec((16, rb), lambda i: (0, i)), pl.BlockSpec((16, n), lambda i: (0, 0)), row, row, row],
        out_specs=[row] * 4,
        compiler_params=_cparams(("parallel",)),
    )(conds, da16, w, m, v)


def _cast_bf16(a, name):
    _, r, c = a.shape
    rb, cb = _tile(r, 512, 8), c
    if rb < 64 < r:
        rb, cb = r, _tile(c, 512)

    def body(a_ref, o_ref):
        o_ref[...] = a_ref[...].astype(BF16)

    return _pcall(body, name=name, out_shape=jax.ShapeDtypeStruct((r, c), BF16), grid=(r // rb, c // cb),
                  in_specs=[pl.BlockSpec((None, rb, cb), lambda i, j: (0, i, j))],
                  out_specs=pl.BlockSpec((rb, cb), lambda i, j: (i, j)),
                  compiler_params=_cparams(("parallel", "parallel")))(a)


def _rope_tabs(t, rot):
    half, q = rot // 2, rot // 4
    n_rows = t // GRID_W
    row = jnp.repeat(jnp.arange(n_rows, dtype=F32), GRID_W)
    col = jnp.tile(jnp.arange(GRID_W, dtype=F32), n_rows)
    inv_freq = ROPE_THETA ** (-jnp.arange(0, half, 2, dtype=F32) / half)
    ang = jnp.concatenate([row[:, None] * inv_freq, col[:, None] * inv_freq], axis=-1)
    cos, sin = jnp.cos(ang), jnp.sin(ang)
    c0, c1, s0, s1 = cos[:, :q], cos[:, q:], sin[:, :q], sin[:, q:]
    z = jnp.zeros_like(s0)
    return (jnp.concatenate([c0, c0, c1, c1], -1), jnp.concatenate([-s0, z, -s1, z], -1),
            jnp.concatenate([z, s0, z, s1], -1))


def _pad_cols(a, left, total, fill=0.0):
    return jnp.pad(a, ((0, 0), (left, total - left - a.shape[1])), constant_values=fill)


def _with_ctx_rows(tab, tc, fill):
    return jnp.concatenate([tab, jnp.full((tc, tab.shape[1]), fill, F32)], axis=0)


def kernel(x, c, ctx, c_ctx, w_ada, b_ada, norm1_g, w_in, mla_q_norm_g, w_q_up, mla_kv_norm_g, w_kv_up, gqa_q_norm_g, gqa_k_norm_g, w_br_a, w_br_b, w_out, norm2_g, w_up, conv_w, conv_b, w_down, final_norm_g, loss_target, m_c_ctx, m_w_ada, m_b_ada, m_norm1_g, m_w_in, m_mla_q_norm_g, m_w_q_up, m_mla_kv_norm_g, m_w_kv_up, m_gqa_q_norm_g, m_gqa_k_norm_g, m_w_br_a, m_w_br_b, m_w_out, m_norm2_g, m_w_up, m_conv_w, m_conv_b, m_w_down, m_final_norm_g, v_c_ctx, v_w_ada, v_b_ada, v_norm1_g, v_w_in, v_mla_q_norm_g, v_w_q_up, v_mla_kv_norm_g, v_w_kv_up, v_gqa_q_norm_g, v_gqa_k_norm_g, v_w_br_a, v_w_br_b, v_w_out, v_norm2_g, v_w_up, v_conv_w, v_conv_b, v_w_down, v_final_norm_g):
    weights = dict(c_ctx=c_ctx, w_ada=w_ada, b_ada=b_ada, norm1_g=norm1_g, w_in=w_in, mla_q_norm_g=mla_q_norm_g,
                   w_q_up=w_q_up, mla_kv_norm_g=mla_kv_norm_g, w_kv_up=w_kv_up, gqa_q_norm_g=gqa_q_norm_g,
                   gqa_k_norm_g=gqa_k_norm_g, w_br_a=w_br_a, w_br_b=w_br_b, w_out=w_out, norm2_g=norm2_g, w_up=w_up,
                   conv_w=conv_w, conv_b=conv_b, w_down=w_down, final_norm_g=final_norm_g)
    mom_m = dict(c_ctx=m_c_ctx, w_ada=m_w_ada, b_ada=m_b_ada, norm1_g=m_norm1_g, w_in=m_w_in, mla_q_norm_g=m_mla_q_norm_g,
                 w_q_up=m_w_q_up, mla_kv_norm_g=m_mla_kv_norm_g, w_kv_up=m_w_kv_up, gqa_q_norm_g=m_gqa_q_norm_g,
                 gqa_k_norm_g=m_gqa_k_norm_g, w_br_a=m_w_br_a, w_br_b=m_w_br_b, w_out=m_w_out, norm2_g=m_norm2_g,
                 w_up=m_w_up, conv_w=m_conv_w, conv_b=m_conv_b, w_down=m_w_down, final_norm_g=m_final_norm_g)
    mom_v = dict(c_ctx=v_c_ctx, w_ada=v_w_ada, b_ada=v_b_ada, norm1_g=v_norm1_g, w_in=v_w_in, mla_q_norm_g=v_mla_q_norm_g,
                 w_q_up=v_w_q_up, mla_kv_norm_g=v_mla_kv_norm_g, w_kv_up=v_w_kv_up, gqa_q_norm_g=v_gqa_q_norm_g,
                 gqa_k_norm_g=v_gqa_k_norm_g, w_br_a=v_w_br_a, w_br_b=v_w_br_b, w_out=v_w_out, norm2_g=v_norm2_g,
                 w_up=v_w_up, conv_w=v_conv_w, conv_b=v_conv_b, w_down=v_w_down, final_norm_g=v_final_norm_g)
    order = list(weights)

    my_idx = 4 * lax.axis_index("x") + 2 * lax.axis_index("y") + lax.axis_index("c")
    xs, cts, tgt = x[0], ctx[0], loss_target[0]
    t, d = xs.shape
    tc = cts.shape[0]
    ta = t + tc
    kvl, ql = MLA_KV_LORA, MLA_Q_LORA
    nb = GQA_KV_HEADS * GQA_HEAD_DIM
    hb = GQA_HEADS * GQA_HEAD_DIM
    ha = MLA_HEADS
    f2 = w_up.shape[2] * N_DEV
    ff = f2 // 2

    big = ["w_in", "w_q_up", "w_kv_up", "w_br_a", "w_br_b", "w_out", "w_up", "w_down"]
    _ORDER_AFTER.clear()
    narrow = ("w_in", "w_q_up")

    def tview(a):
        return jnp.transpose(a, (0, 2, 1))

    shards = {"w_in": _cast_bf16(tview(weights["w_in"]), "cast_w_in")}
    c_idx = jnp.reshape(lax.axis_index("c"), (1,)).astype(jnp.int32)

    def gather_start(names, dep):
        shs = [shards[n] for n in names]
        land = [lax.empty((N_DEV,) + s.shape, BF16) for s in shs]
        if dep is not None:
            _after(dep)
        s, r, arrs, tok = _split_start("gather_ici_start_" + names[0], shs + land, _gather_ici_copies(len(names)),
                                       4 * len(names))
        return dict(names=names, s=s, r=r, arrs=arrs, tok=tok)

    def gather_pass(g, after):
        n = len(g["names"])
        arrs = _split_wait("gather_ici_wait_" + g["names"][0], g["s"], g["r"], g["arrs"], _gather_ici_copies(n), after)
        s, r, bufs, tok = _split_start("gather_pass_start_" + g["names"][0], arrs[n:], _gather_pass_copies(n), 3 * n)
        g.update(s2=s, r2=r, bufs=bufs)
        return tok

    def gather_relay(g, after):
        n = len(g["names"])
        bufs = _split_wait("gather_pass_wait_" + g["names"][0], g["s2"], g["r2"], g["bufs"], _gather_pass_copies(n), after)
        s, r, bufs, tok = _split_start("gather_d2d_start_" + g["names"][0], bufs, _gather_d2d_copies(n), n)
        g.update(s3=s, r3=r, bufs=bufs)
        return tok

    def gather_finish(g, after):
        n = len(g["names"])
        bufs = _split_wait("gather_d2d_wait_" + g["names"][0], g["s3"], g["r3"], g["bufs"], _gather_d2d_copies(n), after)
        return dict(zip(g["names"], bufs))

    _after(shards["w_in"])
    c_all, cw_all = _all_gather([jnp.pad(c, ((0, 7), (0, 0))), jnp.pad(conv_w[0], ((0, 5), (0, 0)))], "gather_cond")
    conv_w_f = jnp.transpose(cw_all[:, :3, :], (1, 0, 2)).reshape(3, f2)
    conds = jnp.concatenate([c_all[:, 0, :], c_ctx[None, :], jnp.zeros((7, d), F32)], axis=0)
    ncol = w_ada.shape[2]
    b_shard = lax.dynamic_slice_in_dim(b_ada, my_idx * ncol, ncol, axis=1)
    ada_shard = _ada_fwd(conds, w_ada[0], b_shard)
    (ada_all,) = _all_gather([ada_shard], "gather_ada")
    ada = jnp.transpose(ada_all, (1, 0, 2)).reshape(16, N_DEV * ncol)
    lat = lax.dynamic_slice_in_dim(ada, my_idx, 1, axis=0).reshape(6, d)
    cxt = ada[8].reshape(6, d)
    mods1 = jnp.concatenate([lat[0:2], cxt[0:2], jnp.zeros((4, d), F32)], axis=0)
    mods2 = jnp.concatenate([lat[2:3], lat[3:4], lat[4:5], jnp.zeros((5, d), F32)], axis=0)
    mods2b = jnp.concatenate([lat[2:3], lat[4:5], jnp.zeros((6, d), F32)], axis=0)
    mods3 = jnp.concatenate([lat[5:6], jnp.zeros((7, d), F32)], axis=0)

    g0 = gather_start(["w_in"], ada_all)
    for n in big[1:]:
        _after(g0["tok"])
        shards[n] = _cast_bf16(tview(weights[n]) if n in narrow else weights[n], "cast_" + n)

    ca, s1a, s2a = _rope_tabs(t, MLA_ROPE)
    cb_, s1b, s2b = _rope_tabs(t, GQA_HEAD_DIM)
    q_tabs_a = (_pad_cols(jnp.concatenate([jnp.ones((t, MLA_NOPE), F32), ca], 1), 0, MLA_SLOT),
                _pad_cols(s1a, MLA_NOPE, MLA_SLOT), _pad_cols(s2a, MLA_NOPE, MLA_SLOT))
    q_tabs_b = (cb_, s1b, s2b)
    k_tabs = (_with_ctx_rows(_pad_cols(ca, 0, LANE), tc, 1.0), _with_ctx_rows(_pad_cols(s1a, 0, LANE), tc, 0.0),
              _with_ctx_rows(_pad_cols(s2a, 0, LANE), tc, 0.0),
              _with_ctx_rows(cb_, tc, 1.0), _with_ctx_rows(s1b, tc, 0.0), _with_ctx_rows(s2b, tc, 0.0))

    def cols_full(g):
        return jnp.transpose(g, (1, 0, 2)).reshape(g.shape[1], N_DEV * g.shape[2])

    _after(*q_tabs_a, *q_tabs_b, *k_tabs, *[shards[n] for n in big[1:]])
    tok_p0 = gather_pass(g0, mods1)
    g1 = gather_start(["w_q_up", "w_kv_up", "w_br_a", "w_br_b", "w_out"], tok_p0)
    _after(g1["tok"])
    z_all = _norm_mod_fwd(cts, xs, norm1_g, mods1)
    gathered = gather_finish(g0, gather_relay(g0, z_all))
    wt_in = gathered["w_in"].reshape(-1, d)
    o_kpe, o_kb, o_vb = kvl, kvl + MLA_ROPE, kvl + MLA_ROPE + nb
    o_q = o_vb + nb
    o_g = o_q + ql + hb
    wkv_w = kvl + 2 * nb + LANE
    wt_kv_p = jnp.concatenate([wt_in[:kvl], wt_in[o_kb:o_q], wt_in[o_kpe:o_kb],
                               jnp.zeros((LANE - MLA_ROPE, d), BF16)], axis=0)
    q_w = ql + hb
    q_pad = (-q_w) % 512 if d >= 512 else (-q_w) % d
    qw_p = q_w + q_pad
    wt_q_p = jnp.concatenate([wt_in[o_q:o_g], jnp.zeros((q_pad, d), BF16)], axis=0)
    wt_g = wt_in[o_g:]

    kv_all = _mm(z_all, wt_kv_p, "nt", F32, "proj_kv", tm=1152, tn=wkv_w)
    tok_p1 = gather_pass(g1, kv_all)
    g2 = gather_start(["w_up", "w_down"], tok_p1)
    _after(g2["tok"])
    qq = _mm(z_all, wt_q_p, "nt", F32, "proj_q", tm=1024, tn=1024, rows=t)
    _after(g2["tok"])
    gates = _mm(z_all, wt_g, "nt", F32, "proj_gates", tm=1024, tn=1024, rows=t)
    _after(g2["tok"])
    kin, k_b, v_b = _key_prep_fwd(kv_all, mla_kv_norm_g, gqa_k_norm_g, k_tabs)
    sc_a = float((MLA_NOPE + MLA_ROPE) ** -0.5) * LOG2E
    sc_b = float(GQA_HEAD_DIM ** -0.5) * LOG2E
    _after(g2["tok"])
    cqn, q_b = _q_prep_fwd(qq, mla_q_norm_g, gqa_q_norm_g, q_tabs_b, sc_b)
    _after(kin, gates, g2["tok"])
    gathered.update(gather_finish(g1, gather_relay(g1, q_b)))

    wqt_f = gathered["w_q_up"].reshape(ha, MLA_NOPE + MLA_ROPE, ql)
    wqt_ext = jnp.pad(wqt_f, ((0, 0), (0, MLA_SLOT - MLA_NOPE - MLA_ROPE), (0, 0))).reshape(ha * MLA_SLOT, ql)
    wkv_f = cols_full(gathered["w_kv_up"]).reshape(kvl, ha, MLA_NOPE + MLA_V)
    wk_slots = jnp.pad(wkv_f[:, :, :MLA_NOPE], ((0, 0), (0, 0), (0, MLA_SLOT - MLA_NOPE))).reshape(kvl, ha * MLA_SLOT)
    wv_cols = wkv_f[:, :, MLA_NOPE:].reshape(kvl, ha * MLA_V)
    e_slot = jnp.pad(jnp.eye(MLA_ROPE, dtype=BF16),
                     ((0, LANE - MLA_ROPE), (MLA_NOPE, MLA_SLOT - MLA_NOPE - MLA_ROPE)))
    e_rows = jnp.concatenate([jnp.tile(e_slot, (1, ha)), jnp.zeros((LANE, ha * MLA_V), BF16)], axis=1)
    wkv_ext = jnp.concatenate([jnp.concatenate([wk_slots, wv_cols], axis=1), e_rows], axis=0)
    w_bra = cols_full(gathered["w_br_a"])
    w_brb = cols_full(gathered["w_br_b"])
    w_out_f = gathered["w_out"].reshape(d, d)

    kv_a = _mm(kin, wkv_ext, "nn", BF16, "kv_up", tm=1152, tn=1024)
    qa_raw = _mm(cqn, wqt_ext, "nt", F32, "q_up", tm=1024, tn=1024)
    q_a = _rope_a(qa_raw, q_tabs_a, False, BF16, "rope_q_fwd", sc_a)
    att_a = dict(hq=ha, hkv=ha, dk=MLA_SLOT, dv=MLA_V, k_blk0=0, v_blk0=ha * MLA_SLOT // MLA_V)
    att_b = dict(hq=GQA_HEADS, hkv=GQA_KV_HEADS, dk=GQA_HEAD_DIM, dv=GQA_HEAD_DIM, k_blk0=0, v_blk0=0)
    o_a, lse_a = _attention_fwd(q_a, kv_a, kv_a, name="attn_a_fwd", **att_a)
    o_b, lse_b = _attention_fwd(q_b, k_b, v_b, name="attn_b_fwd", **att_b)
    _after(o_a)
    _after(gather_pass(g2, o_b))
    pa = _mm(o_a, w_bra, "nn", BF16, "br_a", tm=1024, tn=1024)
    pb = _mm(o_b, w_brb, "nn", BF16, "br_b", tm=1024, tn=1024)
    merged = _merge_fwd(pa, pb, gates, 0)
    attn = _mm(merged, w_out_f, "nn", F32, "w_out", tm=1024, tn=1024)
    _after(gather_relay(g2, attn))
    x1, z2 = _resid_norm_mod(xs, attn, norm2_g, mods2, "resid_norm2_fwd")
    ffn_w = gather_finish(g2, z2)
    w_up3 = ffn_w["w_up"]
    w_down_f = ffn_w["w_down"].reshape(ff, d)
    u = _mm_up_fwd(z2, w_up3, "w_up")
    h, uc = _conv_fwd(u, conv_w_f, conv_b)
    ffn = _mm(h, w_down_f, "nn", F32, "w_down", tm=1024, tn=1024, tk=2816)

    def to_shards(g):
        return jnp.transpose(g.reshape(g.shape[0], N_DEV, g.shape[1] // N_DEV), (1, 0, 2))

    def reduce_start(tag, names, sends):
        n = len(sends)
        land = [lax.empty((4,) + s.shape[1:], s.dtype) for s in sends]
        s, r, arrs, tok = _split_start("reduce_d2d_start_" + tag, sends + land, _reduce_d2d_copies(n), 4 * n)
        return dict(tag=tag, names=names, s=s, r=r, arrs=arrs, tok=tok)

    def reduce_relay(g, after):
        n = len(g["names"])
        arrs = _split_wait("reduce_d2d_wait_" + g["tag"], g["s"], g["r"], g["arrs"], _reduce_d2d_copies(n), after)
        sums = [_pair_sum(arrs[a], arrs[n + a], c_idx, "pair_sum_" + g["names"][a]) for a in range(n)]
        land = [lax.empty(s.shape, s.dtype) for s in sums]
        s, r, arrs2, tok = _split_start("reduce_ici_start_" + g["tag"], sums + land, _reduce_ici_copies(n), 4 * n)
        g.update(s2=s, r2=r, arrs2=arrs2)
        return tok

    def reduce_finish(g, after):
        n = len(g["names"])
        arrs2 = _split_wait("reduce_ici_wait_" + g["tag"], g["s2"], g["r2"], g["arrs2"], _reduce_ici_copies(n), after)
        return dict(zip(g["names"], arrs2[n:]))

    dx2, dffn, st_fin = _final_loss(x1, ffn, final_norm_g[None, :], mods3, tgt)
    dh = _mm(dffn, w_down_f, "nt", BF16, "d_h", tm=1024, tn=1024)
    g_w_down = _mm(h, dffn, "tn", BF16, "g_w_down", tm=512, tn=1024)
    du3, dcw, dcb = _conv_bwd(u, uc, conv_w_f, dh)
    dz2 = _mm_up_dz(du3, w_up3, "d_z2")
    g_w_up = _mm_up_gw(z2, du3, N_DEV, "g_w_up")
    g_conv_w = jnp.concatenate([dcw[0], dcw[1]], axis=1)
    r_ffn = reduce_start("ffn", ["w_down", "w_up", "conv_w"],
                         [g_w_down.reshape(N_DEV, ff // N_DEV, d), g_w_up,
                          to_shards(jnp.pad(g_conv_w, ((0, 5), (0, 0))))])
    _after(r_ffn["tok"])
    dx1, dattn, st_n2 = _norm2_bwd(x1, attn, norm2_g, mods2b, dz2, dx2)
    dmerged = _mm(dattn, w_out_f, "nt", BF16, "d_merged", tm=1024, tn=1024)
    g_w_out = _mm(merged, dattn, "tn", BF16, "g_w_out", tm=1024, tn=1024)
    dpa, dpb, dgates = _merge_bwd(dmerged, pa, pb, gates, 0)
    do_a = _mm(dpa, w_bra, "nt", BF16, "d_o_a", tm=1024, tn=1024)
    do_b = _mm(dpb, w_brb, "nt", BF16, "d_o_b", tm=1024, tn=1024)
    g_w_bra = _mm_tn_shards(o_a, dpa, N_DEV, "g_w_br_a")
    g_w_brb = _mm_tn_shards(o_b, dpb, N_DEV, "g_w_br_b")
    _after(reduce_relay(r_ffn, g_w_brb))
    dq_a, dk_a, dv_a = _attention_bwd(q_a, kv_a, kv_a, do_a, lse_a, name="attn_a_bwd", **att_a)
    dq_b, dk_b, dv_b = _attention_bwd(q_b, k_b, v_b, do_b, lse_b, name="attn_b_bwd", **att_b)
    dqa_raw = _rope_a(dq_a, q_tabs_a, True, BF16, "rope_q_bwd", sc_a * LN2)
    dcqn = _mm(dqa_raw, wqt_ext, "nn", F32, "d_cqn", tm=1024, tn=ql)
    g_wqt_ext = _mm(dqa_raw, cqn, "tn", BF16, "g_w_q_up", tm=1024, tn=ql)
    dq_p, st_q, st_qb = _q_prep_bwd(qq, mla_q_norm_g, gqa_q_norm_g, q_tabs_b, dcqn, dq_b, q_pad, sc_b * LN2)
    dkin = _mm_cat_nt([(dk_a, wkv_ext, 0), (dv_a, wkv_ext, ha * MLA_SLOT)], F32, "d_kin", tm=1152, tn=kvl + LANE)
    g_wkv_ext = _mm_cat_tn(kin, [dk_a, dv_a], BF16, "g_w_kv_up", tm=kvl + LANE, tn=min(1024, ha * MLA_V))
    dkv_p, st_kv, st_kb = _key_prep_bwd(kv_all, mla_kv_norm_g, gqa_k_norm_g, k_tabs, dkin, dk_b, dv_b)
    g_wqt = g_wqt_ext.reshape(ha, MLA_SLOT, ql)[:, :MLA_NOPE + MLA_ROPE, :].reshape(N_DEV, -1, ql)
    g_wkv = jnp.concatenate([g_wkv_ext[:kvl, :ha * MLA_SLOT].reshape(kvl, ha, MLA_SLOT)[:, :, :MLA_NOPE],
                             g_wkv_ext[:kvl, ha * MLA_SLOT:].reshape(kvl, ha, MLA_V)], axis=2).reshape(kvl, ha * (MLA_NOPE + MLA_V))
    r_mid = reduce_start("mid", ["w_out", "w_br_a", "w_br_b", "w_q_up", "w_kv_up"],
                         [g_w_out.reshape(N_DEV, d // N_DEV, d), g_w_bra, g_w_brb, g_wqt,
                          to_shards(g_wkv)])
    _after(r_mid["tok"])
    g_wkv_p = _mm(dkv_p, z_all, "tn", BF16, "g_w_in_kv", tm=wkv_w, tn=1024)
    g_wqg_p = _mm_rows_tn([dq_p, dgates], z_all, BF16, "g_w_in_qg", tm=min(1024, d), tn=1024, rows=t)
    tok_m = reduce_relay(r_mid, g_wqg_p)
    g_wt_in = jnp.concatenate([g_wkv_p[:kvl], g_wkv_p[kvl + 2 * nb:kvl + 2 * nb + MLA_ROPE],
                               g_wkv_p[kvl:kvl + 2 * nb], g_wqg_p[:q_w], g_wqg_p[q_w + q_pad:]], axis=0)
    _after(tok_m)
    r_in = reduce_start("in", ["w_in"], [g_wt_in.reshape(N_DEV, -1, d)])
    _after(r_in["tok"])
    dz_lat = _mm_sum_nn([(dq_p, 0, wt_q_p, 0, qw_p), (dgates, 0, wt_g, 0, d), (dgates, d, wt_g, d, d),
                         (dkv_p, 0, wt_kv_p, 0, wkv_w)], F32, "d_z_lat", rows=t)
    dz_ctx = _mm(dkv_p, wt_kv_p, "nn", F32, "d_z_ctx", tm=min(ROW_BLOCK, tc), tn=1024, a_row_off=t)
    grad_x, st_n1 = _norm1_bwd(cts, xs, norm1_g, mods1, dz_ctx, dz_lat, dx1)

    res = {}

    def upd(nm, parts):
        wv, mv, vv = weights[nm], mom_m[nm], mom_v[nm]
        if wv.ndim == 1:
            wv, mv, vv = (a.reshape(1, -1) for a in (wv, mv, vv))
        if nm in narrow:
            wv, mv, vv = tview(wv), tview(mv), tview(vv)
        outs = _adamw(parts, wv, mv, vv, "adamw_" + nm)
        if nm in narrow:
            outs = [tview(o_) for o_ in outs]
        res[nm] = [o_.reshape(weights[nm].shape) for o_ in outs]

    d_lat = jnp.concatenate([st_n1[0], st_n1[1], st_n2[3], st_n2[0], st_n2[1], st_fin[1]])
    d_cxt = jnp.concatenate([st_n1[3], st_n1[4], jnp.zeros((4 * d,), F32)])
    small = jnp.concatenate([d_lat, d_cxt, st_n1[2], st_q[0], st_kv[0], st_qb[0], st_kb[0], st_n2[2],
                             jnp.concatenate([dcb[0, 0], dcb[1, 0]]), st_fin[0], st_fin[3, :LANE]])
    n_small = small.shape[0]
    pad_small = (-n_small) % LANE
    (small_all,) = _all_gather([jnp.pad(small, (0, pad_small)).reshape(1, -1)], "gather_small")
    offs = {}
    o = 0
    for nm, ln in (("d_lat", 6 * d), ("d_cxt", 6 * d), ("norm1_g", d), ("mla_q_norm_g", ql), ("mla_kv_norm_g", kvl),
                   ("gqa_q_norm_g", GQA_HEAD_DIM), ("gqa_k_norm_g", GQA_HEAD_DIM), ("norm2_g", d), ("conv_b", f2),
                   ("final_norm_g", d), ("loss", LANE)):
        offs[nm] = (o, ln)
        o += ln

    def part(nm):
        a, ln = offs[nm]
        return small_all[:, :, a:a + ln]

    loss = _sum_parts(part("loss"))[0, 0]
    d_lat_all = part("d_lat")[:, 0, :]
    d_cxt_sum = _sum_parts(part("d_cxt"))
    da16 = jnp.concatenate([d_lat_all, d_cxt_sum, jnp.zeros((7, 6 * d), F32)], axis=0)
    da16_shard = lax.dynamic_slice_in_dim(da16, my_idx * ncol, ncol, axis=1)
    cc_part = _cctx_partial(da16_shard, w_ada[0], c_ctx[None, :])
    (cc_all,) = _all_gather([cc_part], "gather_cctx")
    cc_parts = cc_all[:, 0:1, :]
    tok_i = reduce_relay(r_in, cc_all)

    _after(tok_i)
    for nm in ("norm1_g", "mla_q_norm_g", "mla_kv_norm_g", "gqa_q_norm_g", "gqa_k_norm_g", "norm2_g", "conv_b",
               "final_norm_g"):
        upd(nm, part(nm))
    upd("c_ctx", cc_parts)
    b_parts = jnp.concatenate([d_lat_all[:, None, :], d_cxt_sum[None]], axis=0)
    upd("b_ada", b_parts)
    _after(tok_i)
    outs = _adamw_ada(conds, da16_shard, w_ada[0], m_w_ada[0], v_w_ada[0])
    res["w_ada"] = [o_[None] for o_ in outs]
    last = outs[0]
    done = [last]
    for grp in (r_ffn, r_mid, r_in):
        _after(*done)
        recv = reduce_finish(grp, last)
        for nm in grp["names"]:
            upd(nm, recv[nm][:, :3, :] if nm == "conv_w" else recv[nm])
            last = res[nm][0]
            done.append(last)

    return (loss, grad_x[None], *[res[n][0] for n in order], *[res[n][1] for n in order],
            *[res[n][2] for n in order], *[res[n][3] for n in order])
```

```python
import jax
import jax.numpy as jnp
from jax import lax
from jax.experimental import pallas as pl
from jax.experimental.pallas import tpu as pltpu

F32 = jnp.float32
BF16 = jnp.bfloat16

GRID_W = 64
ROPE_THETA = 10000.0
NORM_EPS = 1e-6
MLA_HEADS = 8
MLA_Q_LORA = 768
MLA_KV_LORA = 512
MLA_NOPE = 128
MLA_ROPE = 64
MLA_V = 128
GQA_HEADS = 8
GQA_KV_HEADS = 2
GQA_HEAD_DIM = 128
ADAM_LR = 0.001
ADAM_B1 = 0.9
ADAM_B2 = 0.999
ADAM_EPS = 1e-08
ADAM_WD = 0.01
ADAM_STEP = 10

N_DEV = 8
LANE = 128
MLA_SLOT = 2 * LANE
VMEM_LIMIT = 56 * 1024 * 1024
ROW_BLOCK = 256
ATT_Q_BLOCK = 512
ATT_Q_BLOCK_FWD = 1024
LN2 = 0.6931471805599453
LOG2E = 1.4426950408889634
MESH_ID = pl.DeviceIdType.MESH


def _tile(n, pref, align=LANE):
    if n <= pref:
        return n
    best = None
    t = align
    while t <= pref:
        if n % t == 0:
            best = t
        t += align
    assert best is not None, (n, pref, align)
    return best


def _cparams(sem=None):
    return pltpu.CompilerParams(dimension_semantics=sem, vmem_limit_bytes=VMEM_LIMIT)


_ORDER_AFTER = []


def _after(*arrays):
    _ORDER_AFTER.extend(arrays)


def _pcall(body, *, in_specs, **kw):
    deps = tuple(_ORDER_AFTER)
    _ORDER_AFTER.clear()
    if not deps:
        return pl.pallas_call(body, in_specs=in_specs, **kw)
    n_in, n_dep = len(in_specs), len(deps)

    def with_deps(*refs):
        body(*refs[:n_in], *refs[n_in + n_dep:])

    call = pl.pallas_call(with_deps, in_specs=list(in_specs) + [pl.BlockSpec(memory_space=pl.ANY)] * n_dep, **kw)
    return lambda *args: call(*args, *deps)


def _all_gather(arrs, name):
    n = len(arrs)

    def body(*refs):
        ins = refs[:n]
        outs = refs[n:2 * n]
        send_sems, recv_sems, local_sems = refs[2 * n:]
        x, y, c = lax.axis_index("x"), lax.axis_index("y"), lax.axis_index("c")
        me, sibling = (x, y, c), (x, y, 1 - c)
        chips = [(1 - x, y), (x, 1 - y), (1 - x, 1 - y)]

        def rows(a, dev):
            px, py, pc = dev
            return outs[a].at[4 * px + 2 * py + pc]

        def copy(a, k, block, to, src=None):
            return pltpu.make_async_remote_copy(
                src_ref=rows(a, block) if src is None else src,
                dst_ref=rows(a, block),
                send_sem=send_sems.at[7 * a + k],
                recv_sem=recv_sems.at[7 * a + k],
                device_id=to,
                device_id_type=MESH_ID,
            )

        mine = [pltpu.make_async_copy(ins[a], rows(a, me), local_sems.at[a]) for a in range(n)]
        for cp in mine:
            cp.start()
        first = []
        for a in range(n):
            first.append(copy(a, 0, me, sibling, src=ins[a]))
            first += [copy(a, 1 + j, me, (*chip, c), src=ins[a]) for j, chip in enumerate(chips)]
        for cp in first:
            cp.start()
        passed = []
        for j, chip in enumerate(chips):
            for a in range(n):
                copy(a, 1 + j, (*chip, c), me).wait_recv()
                fwd = copy(a, 4 + j, (*chip, c), sibling)
                fwd.start()
                passed.append(fwd)
        for a in range(n):
            copy(a, 0, sibling, me).wait_recv()
            for j, chip in enumerate(chips):
                copy(a, 4 + j, (*chip, 1 - c), me).wait_recv()
        for cp in first + passed:
            cp.wait_send()
        for cp in mine:
            cp.wait()

    any_spec = pl.BlockSpec(memory_space=pl.ANY)
    outs = _pcall(
        body,
        name=name,
        out_shape=[jax.ShapeDtypeStruct((N_DEV,) + a.shape, a.dtype) for a in arrs],
        in_specs=[any_spec] * n,
        out_specs=[any_spec] * n,
        scratch_shapes=[
            pltpu.SemaphoreType.DMA((7 * n,)),
            pltpu.SemaphoreType.DMA((7 * n,)),
            pltpu.SemaphoreType.DMA((n,)),
        ],
    )(*arrs)
    return list(outs)


_HBM = pl.BlockSpec(memory_space=pltpu.HBM)
_SEM = pl.BlockSpec(memory_space=pltpu.SEMAPHORE)
_EFFECT = pltpu.SideEffectType.DATAFLOW_SIDE_EFFECTING


def _descriptors(copies, send_sems, recv_sems):
    descs = []
    for i, (src, dst, dev) in enumerate(copies):
        if dev is None:
            descs.append(pltpu.make_async_copy(src, dst, recv_sems.at[i]))
        else:
            descs.append(pltpu.make_async_remote_copy(src_ref=src, dst_ref=dst, send_sem=send_sems.at[i],
                                                      recv_sem=recv_sems.at[i], device_id=dev, device_id_type=MESH_ID))
    return descs


def _split_start(name, arrays, copies_fn, n_copies):
    n = len(arrays)

    def body(*refs):
        send_sems, recv_sems = refs[n], refs[n + 1]
        token = refs[2 * n + 2]
        for dsc in _descriptors(copies_fn(refs[:n]), send_sems, recv_sems):
            dsc.start()
        token[...] = jnp.zeros_like(token)

    outs = _pcall(
        body,
        name=name,
        out_shape=(pltpu.SemaphoreType.DMA((n_copies,)), pltpu.SemaphoreType.DMA((n_copies,)),
                   *[pltpu.HBM(a.shape, a.dtype) for a in arrays], jax.ShapeDtypeStruct((8, LANE), F32)),
        in_specs=[_HBM] * n,
        out_specs=(_SEM, _SEM, *[_HBM] * n, pl.BlockSpec(memory_space=pltpu.VMEM)),
        input_output_aliases={i: 2 + i for i in range(n)},
        compiler_params=pltpu.CompilerParams(has_side_effects=_EFFECT),
    )(*[pltpu.with_memory_space_constraint(a, pltpu.HBM) for a in arrays])
    return outs[0], outs[1], list(outs[2:2 + n]), outs[2 + n]


def _split_wait(name, send_sems, recv_sems, arrays, copies_fn, after):
    n = len(arrays)

    def body(*refs):
        for dsc, (_, _, dev) in zip(_descriptors(copies_fn(refs[:n]), refs[n], refs[n + 1]), copies_fn(refs[:n])):
            if dev is None:
                dsc.wait()
            else:
                dsc.wait_send()
                dsc.wait_recv()

    outs = _pcall(
        body,
        name=name,
        out_shape=tuple(pltpu.HBM(a.shape, a.dtype) for a in arrays),
        in_specs=[_HBM] * n + [_SEM, _SEM, pl.BlockSpec(memory_space=pl.ANY)],
        out_specs=tuple([_HBM] * n),
        input_output_aliases={i: i for i in range(n)},
        compiler_params=pltpu.CompilerParams(has_side_effects=_EFFECT),
    )(*arrays, send_sems, recv_sems, after)
    return list(outs)


def _mesh_pos():
    x, y, c = lax.axis_index("x"), lax.axis_index("y"), lax.axis_index("c")
    return x, y, c, [(1 - x, y), (x, 1 - y), (1 - x, 1 - y)]


def _gather_ici_copies(n):
    def copies(refs):
        x, y, c, chips = _mesh_pos()
        me = 4 * x + 2 * y + c
        out = []
        for a in range(n):
            src, buf = refs[a], refs[n + a]
            out.append((src, buf.at[me], None))
            out.append((src, buf.at[me], (x, y, 1 - c)))
            out += [(src, buf.at[me], (cx, cy, c)) for cx, cy in chips[:2]]
        return out
    return copies


def _gather_pass_copies(n):
    def copies(refs):
        x, y, c, chips = _mesh_pos()
        south = c == 0
        bx, by = jnp.where(south, 1 - x, x), jnp.where(south, y, 1 - y)
        tx, ty = jnp.where(south, x, 1 - x), jnp.where(south, 1 - y, y)
        out = []
        for a in range(n):
            rows = refs[a].at[4 * bx + 2 * by + c]
            out.append((rows, rows, (tx, ty, c)))
            for cx, cy in chips[:2]:
                rows = refs[a].at[4 * cx + 2 * cy + c]
                out.append((rows, rows, (x, y, 1 - c)))
        return out
    return copies


def _gather_d2d_copies(n):
    def copies(refs):
        x, y, c, chips = _mesh_pos()
        cx, cy = chips[2]
        out = []
        for a in range(n):
            rows = refs[a].at[4 * cx + 2 * cy + c]
            out.append((rows, rows, (x, y, 1 - c)))
        return out
    return copies


def _reduce_d2d_copies(n):
    def copies(refs):
        x, y, c, _ = _mesh_pos()
        out = []
        for a in range(n):
            for k in range(4):
                out.append((refs[a].at[2 * k + (1 - c)], refs[n + a].at[k], (x, y, 1 - c)))
        return out
    return copies


def _reduce_ici_copies(n):
    def copies(refs):
        x, y, c, chips = _mesh_pos()
        mine = 2 * x + y
        out = []
        for a in range(n):
            src, land = refs[a], refs[n + a]
            out.append((src.at[mine], land.at[mine], None))
            out += [(src.at[2 * cx + cy], land.at[mine], (cx, cy, c)) for cx, cy in chips]
        return out
    return copies


def _pair_sum(send, land, c_idx, name):
    _, r, cols = send.shape
    rb = _tile(r, max(8, (1 << 22) // (send.dtype.itemsize * cols) // 8 * 8), 8)
    dt = send.dtype

    def body(c_ref, s_ref, l_ref, o_ref):
        o_ref[...] = (s_ref[...].astype(F32) + l_ref[...].astype(F32)).astype(dt)

    return pl.pallas_call(
        body,
        name=name,
        out_shape=jax.ShapeDtypeStruct((4, r, cols), dt),
        grid_spec=pltpu.PrefetchScalarGridSpec(
            num_scalar_prefetch=1,
            grid=(4, r // rb),
            in_specs=[pl.BlockSpec((None, rb, cols), lambda k, i, c_ref: (2 * k + c_ref[0], i, 0)),
                      pl.BlockSpec((None, rb, cols), lambda k, i, c_ref: (k, i, 0))],
            out_specs=pl.BlockSpec((None, rb, cols), lambda k, i, c_ref: (k, i, 0)),
        ),
        compiler_params=_cparams(("parallel", "parallel")),
    )(c_idx, send, land)


_DIMS = {
    "nn": (((1,), (0,)), ((), ())),
    "nt": (((1,), (1,)), ((), ())),
    "tn": (((0,), (0,)), ((), ())),
}


def _mm_call(a, b, *, mode, grid, a_spec, b_spec, o_spec, out_shape, acc_shape, name):
    nk = grid[2]
    out_dtype = out_shape.dtype

    def body(a_ref, b_ref, o_ref, *scratch):
        p = lax.dot_general(a_ref[...].astype(BF16), b_ref[...].astype(BF16), _DIMS[mode],
                            preferred_element_type=F32)
        if nk == 1:
            o_ref[...] = p.astype(out_dtype)
        else:
            acc = scratch[0]
            k = pl.program_id(2)

            @pl.when(k == 0)
            def _():
                acc[...] = p

            @pl.when(k > 0)
            def _():
                acc[...] += p

            @pl.when(k == nk - 1)
            def _():
                o_ref[...] = acc[...].astype(out_dtype)

    return _pcall(
        body,
        name=name,
        out_shape=out_shape,
        grid=grid,
        in_specs=[a_spec, b_spec],
        out_specs=o_spec,
        scratch_shapes=[pltpu.VMEM(acc_shape, F32)] if nk > 1 else [],
        compiler_params=_cparams(("parallel", "parallel", "arbitrary")),
    )(a, b)


def _mm(a, b, mode, out_dtype, name, tm=512, tn=512, tk=2432, a_row_off=0, rows=None):
    if mode == "nn":
        (m, k), (k2, n) = a.shape, b.shape
    elif mode == "nt":
        (m, k), (n, k2) = a.shape, b.shape
    else:
        (k, m), (k2, n) = a.shape, b.shape
        if rows is not None:
            k = k2 = rows
    assert k == k2, (a.shape, b.shape, mode)
    if mode != "tn":
        m = (m if rows is None else rows + a_row_off) - a_row_off
    tm, tn, tk = _tile(m, tm, 8), _tile(n, tn), _tile(k, tk, 8 if mode == "tn" else LANE)
    assert a_row_off % tm == 0
    ro = a_row_off // tm
    grid = (m // tm, n // tn, k // tk)
    if mode == "tn":
        a_spec = pl.BlockSpec((tk, tm), lambda i, j, kk: (kk, i))
    else:
        a_spec = pl.BlockSpec((tm, tk), lambda i, j, kk: (i + ro, kk))
    if mode == "nt":
        b_spec = pl.BlockSpec((tn, tk), lambda i, j, kk: (j, kk))
    else:
        b_spec = pl.BlockSpec((tk, tn), lambda i, j, kk: (kk, j))
    o_spec = pl.BlockSpec((tm, tn), lambda i, j, kk: (i, j))
    return _mm_call(a, b, mode=mode, grid=grid, a_spec=a_spec, b_spec=b_spec, o_spec=o_spec,
                    out_shape=jax.ShapeDtypeStruct((m, n), out_dtype), acc_shape=(tm, tn), name=name)


def _mm_cat_nt(pieces, out_dtype, name, tm=1024, tn=1024, tk=2048, rows=None):
    m = pieces[0][0].shape[0] if rows is None else rows
    n = pieces[0][1].shape[0]
    tm, tn = _tile(m, tm, 8), _tile(n, tn)
    steps, starts, s = [], [], 0
    for a, b, off in pieces:
        kp = a.shape[1]
        tkp = _tile(kp, tk)
        assert off % tkp == 0 and b.shape[0] == n
        steps.append((tkp, kp // tkp, off // tkp))
        starts.append(s)
        s += kp // tkp
    nk = s
    npc = len(pieces)

    def body(*refs):
        o_ref, acc = refs[2 * npc], refs[2 * npc + 1]
        kk = pl.program_id(2)

        @pl.when(kk == 0)
        def _():
            acc[...] = jnp.zeros_like(acc)

        for p in range(npc):
            @pl.when((kk >= starts[p]) & (kk < starts[p] + steps[p][1]))
            def _(p=p):
                acc[...] += lax.dot_general(refs[2 * p][...].astype(BF16), refs[2 * p + 1][...].astype(BF16), _DIMS["nt"],
                                            preferred_element_type=F32)

        @pl.when(kk == nk - 1)
        def _():
            o_ref[...] = acc[...].astype(out_dtype)

    in_specs, args = [], []
    for p, (a, b, off) in enumerate(pieces):
        tkp, np_, ob = steps[p]

        def rel(kk, p=p, np_=np_):
            return jnp.clip(kk - starts[p], 0, np_ - 1)

        in_specs.append(pl.BlockSpec((tm, tkp), lambda i, j, kk, rel=rel: (i, rel(kk))))
        in_specs.append(pl.BlockSpec((tn, tkp), lambda i, j, kk, rel=rel, ob=ob: (j, ob + rel(kk))))
        args += [a, b]
    return _pcall(
        body,
        name=name,
        out_shape=jax.ShapeDtypeStruct((m, n), out_dtype),
        grid=(m // tm, n // tn, nk),
        in_specs=in_specs,
        out_specs=pl.BlockSpec((tm, tn), lambda i, j, kk: (i, j)),
        scratch_shapes=[pltpu.VMEM((tm, tn), F32)],
        compiler_params=_cparams(("parallel", "parallel", "arbitrary")),
    )(*args)


def _mm_cat_tn(a, pieces, out_dtype, name, tm=1024, tn=1024, rows=None):
    k = a.shape[0] if rows is None else rows
    m = a.shape[1]
    tm = _tile(m, tm)
    starts, s = [], 0
    for b in pieces:
        assert b.shape[1] % tn == 0
        starts.append(s)
        s += b.shape[1] // tn
    nj = s
    npc = len(pieces)

    def body(*refs):
        a_ref, o_ref = refs[0], refs[1 + npc]
        j = pl.program_id(1)
        for p in range(npc):
            @pl.when((j >= starts[p]) & (j < starts[p] + pieces[p].shape[1] // tn))
            def _(p=p):
                o_ref[...] = lax.dot_general(a_ref[...].astype(BF16), refs[1 + p][...].astype(BF16), _DIMS["tn"],
                                             preferred_element_type=F32).astype(out_dtype)

    in_specs = [pl.BlockSpec((k, tm), lambda i, j: (0, i))]
    for p, b in enumerate(pieces):
        np_ = b.shape[1] // tn
        in_specs.append(pl.BlockSpec((k, tn), lambda i, j, p=p, np_=np_: (0, jnp.clip(j - starts[p], 0, np_ - 1))))
    return _pcall(
        body,
        name=name,
        out_shape=jax.ShapeDtypeStruct((m, nj * tn), out_dtype),
        grid=(m // tm, nj),
        in_specs=in_specs,
        out_specs=pl.BlockSpec((tm, tn), lambda i, j: (i, j)),
        compiler_params=_cparams(("parallel", "arbitrary")),
    )(a, *pieces)


def _mm_up_fwd(z2, w3, name, tm=1024):
    t, d = z2.shape
    nsh, _, c = w3.shape
    tm = _tile(t, tm, 8)
    return _mm_call(z2, w3, mode="nn", grid=(t // tm, nsh, 1),
                    a_spec=pl.BlockSpec((tm, d), lambda i, j, kk: (i, 0)),
                    b_spec=pl.BlockSpec((None, d, c), lambda i, j, kk: (j, 0, 0)),
                    o_spec=pl.BlockSpec((tm, c), lambda i, j, kk: (i, j)),
                    out_shape=jax.ShapeDtypeStruct((t, nsh * c), BF16), acc_shape=(tm, c), name=name)


def _mm_up_dz(du3, w3, name, tm=512, tn=1024):
    _, t, f = du3.shape
    nsh, d, c = w3.shape
    half = nsh // 2
    assert f == half * c
    tm, tn = _tile(t, tm, 8), _tile(d, tn)

    def body(a_ref, b_ref, o_ref, acc):
        kk = pl.program_id(2)
        p = None
        for s in range(half):
            q = lax.dot_general(a_ref[:, s * c:(s + 1) * c], b_ref[s], _DIMS["nt"], preferred_element_type=F32)
            p = q if p is None else p + q

        @pl.when(kk == 0)
        def _():
            acc[...] = p

        @pl.when(kk == 1)
        def _():
            o_ref[...] = (acc[...] + p).astype(BF16)

    return _pcall(
        body,
        name=name,
        out_shape=jax.ShapeDtypeStruct((t, d), BF16),
        grid=(t // tm, d // tn, 2),
        in_specs=[pl.BlockSpec((None, tm, f), lambda i, j, kk: (kk, i, 0)),
                  pl.BlockSpec((half, tn, c), lambda i, j, kk: (kk, j, 0))],
        out_specs=pl.BlockSpec((tm, tn), lambda i, j, kk: (i, j)),
        scratch_shapes=[pltpu.VMEM((tm, tn), F32)],
        compiler_params=_cparams(("parallel", "parallel", "arbitrary")),
    )(du3, w3)


def _mm_sum_nn(pieces, out_dtype, name, tm=512, tn=512, rows=None):
    m = pieces[0][0].shape[0] if rows is None else rows
    n = pieces[0][2].shape[1]
    tm, tn = _tile(m, tm, 8), _tile(n, tn)
    npc = len(pieces)

    def body(*refs):
        p = None
        for s in range(npc):
            q = jnp.dot(refs[2 * s][...].astype(BF16), refs[2 * s + 1][...].astype(BF16), preferred_element_type=F32)
            p = q if p is None else p + q
        refs[2 * npc][...] = p.astype(out_dtype)

    in_specs, args = [], []
    for a, ao, b, bo, kp in pieces:
        assert ao % kp == 0 and bo % kp == 0 and b.shape[1] == n
        in_specs.append(pl.BlockSpec((tm, kp), lambda i, j, ab=ao // kp: (i, ab)))
        in_specs.append(pl.BlockSpec((kp, tn), lambda i, j, bb=bo // kp: (bb, j)))
        args += [a, b]
    return _pcall(
        body,
        name=name,
        out_shape=jax.ShapeDtypeStruct((m, n), out_dtype),
        grid=(m // tm, n // tn),
        in_specs=in_specs,
        out_specs=pl.BlockSpec((tm, tn), lambda i, j: (i, j)),
        compiler_params=_cparams(("parallel", "parallel")),
    )(*args)


def _mm_rows_tn(pieces, b, out_dtype, name, tm=1024, tn=1024, rows=None):
    k = b.shape[0] if rows is None else rows
    n = b.shape[1]
    tn = _tile(n, tn)
    starts, s = [], 0
    for a in pieces:
        assert a.shape[1] % tm == 0
        starts.append(s)
        s += a.shape[1] // tm
    ni = s
    npc = len(pieces)

    def body(*refs):
        b_ref, o_ref = refs[npc], refs[npc + 1]
        i = pl.program_id(0)
        for p in range(npc):
            @pl.when((i >= starts[p]) & (i < starts[p] + pieces[p].shape[1] // tm))
            def _(p=p):
                o_ref[...] = lax.dot_general(refs[p][...].astype(BF16), b_ref[...].astype(BF16), _DIMS["tn"],
                                             preferred_element_type=F32).astype(out_dtype)

    in_specs = []
    for p, a in enumerate(pieces):
        np_ = a.shape[1] // tm
        in_specs.append(pl.BlockSpec((k, tm), lambda i, j, p=p, np_=np_: (0, jnp.clip(i - starts[p], 0, np_ - 1))))
    in_specs.append(pl.BlockSpec((k, tn), lambda i, j: (0, j)))
    return _pcall(
        body,
        name=name,
        out_shape=jax.ShapeDtypeStruct((ni * tm, n), out_dtype),
        grid=(ni, n // tn),
        in_specs=in_specs,
        out_specs=pl.BlockSpec((tm, tn), lambda i, j: (i, j)),
        compiler_params=_cparams(("parallel", "parallel")),
    )(*pieces, b)


def _mm_tn_shards(a, b, nsh, name):
    k, m = a.shape
    n = b.shape[1]
    c = n // nsh
    return _mm_call(a, b, mode="tn", grid=(1, nsh, 1),
                    a_spec=pl.BlockSpec((k, m), lambda i, j, kk: (0, 0)),
                    b_spec=pl.BlockSpec((k, c), lambda i, j, kk: (0, j)),
                    o_spec=pl.BlockSpec((None, m, c), lambda i, j, kk: (j, 0, 0)),
                    out_shape=jax.ShapeDtypeStruct((nsh, m, c), BF16), acc_shape=(m, c), name=name)


def _mm_up_gw(z2, du3, nsh, name, tm=1024):
    t, d = z2.shape
    f = du3.shape[2]
    half = nsh // 2
    c = f // half
    tm = _tile(d, tm)
    return _mm_call(z2, du3, mode="tn", grid=(d // tm, nsh, 1),
                    a_spec=pl.BlockSpec((t, tm), lambda i, j, kk: (0, i)),
                    b_spec=pl.BlockSpec((None, t, c), lambda i, j, kk: (j // half, 0, j % half)),
                    o_spec=pl.BlockSpec((None, tm, c), lambda i, j, kk: (j, i, 0)),
                    out_shape=jax.ShapeDtypeStruct((nsh, d, c), BF16), acc_shape=(tm, c), name=name)


def _rms(x):
    r = lax.rsqrt(jnp.mean(x * x, axis=-1, keepdims=True) + NORM_EPS)
    return x * r, r


def _rms_bwd(dxh, xh, r):
    return r * (dxh - xh * jnp.mean(dxh * xh, axis=-1, keepdims=True))


def _colsum(v):
    return jnp.sum(v, axis=0, keepdims=True)


def _rope(v, c, s1, s2, q):
    w = v.shape[-1]
    return v * c + pltpu.roll(v, w - q, 1) * s1 + pltpu.roll(v, q, 1) * s2


def _rope_t(d, c, s1, s2, q):
    w = d.shape[-1]
    return d * c + pltpu.roll(d * s1, q, 1) + pltpu.roll(d * s2, w - q, 1)


def _norm_mod_fwd(ctx, x, gain, mods):
    tc, d = ctx.shape
    t = x.shape[0]
    rb = min(ROW_BLOCK, tc)
    nbl = t // rb

    def body(ctx_ref, x_ref, g_ref, mod_ref, z_ref):
        i = pl.program_id(0)

        def emit(src, sh, sc):
            xh, _ = _rms(src[...])
            z_ref[...] = ((xh * g_ref[...]) * (1.0 + sc) + sh).astype(BF16)

        @pl.when(i >= nbl)
        def _():
            emit(ctx_ref, mod_ref[2:3, :], mod_ref[3:4, :])

        @pl.when(i < nbl)
        def _():
            emit(x_ref, mod_ref[0:1, :], mod_ref[1:2, :])

    return _pcall(
        body,
        name="norm1_mod_fwd",
        out_shape=jax.ShapeDtypeStruct((tc + t, d), BF16),
        grid=((tc + t) // rb,),
        in_specs=[
            pl.BlockSpec((rb, d), lambda i: (jnp.maximum(i - nbl, 0), 0)),
            pl.BlockSpec((rb, d), lambda i: (jnp.minimum(i, nbl - 1), 0)),
            pl.BlockSpec((1, d), lambda i: (0, 0)),
            pl.BlockSpec((8, d), lambda i: (0, 0)),
        ],
        out_specs=pl.BlockSpec((rb, d), lambda i: (i, 0)),
        compiler_params=_cparams(("arbitrary",)),
    )(ctx, x, gain, mods)


def _norm1_bwd(ctx, x, gain, mods, dz_ctx, dz_lat, dx1):
    tc, d = ctx.shape
    t = x.shape[0]
    rb = min(ROW_BLOCK, tc)
    nbl = t // rb

    def body(ctx_ref, x_ref, g_ref, mod_ref, dzc_ref, dzl_ref, dx1_ref, gx_ref, st_ref):
        i = pl.program_id(0)

        @pl.when(i == 0)
        def _():
            st_ref[...] = jnp.zeros_like(st_ref)

        def common(src, dz, sc, row_sh, row_sc):
            xh, r = _rms(src[...])
            g = g_ref[...]
            dxn = dz * (1.0 + sc)
            st_ref[row_sh:row_sh + 1, :] += _colsum(dz)
            st_ref[row_sc:row_sc + 1, :] += _colsum(dz * (xh * g))
            st_ref[2:3, :] += _colsum(dxn * xh)
            return _rms_bwd(dxn * g, xh, r)

        @pl.when(i >= nbl)
        def _():
            common(ctx_ref, dzc_ref[...], mod_ref[3:4, :], 3, 4)

        @pl.when(i < nbl)
        def _():
            gx_ref[...] = dx1_ref[...] + common(x_ref, dzl_ref[...], mod_ref[1:2, :], 0, 1)

    lat = lambda i: (jnp.minimum(i, nbl - 1), 0)
    cix = lambda i: (jnp.maximum(i - nbl, 0), 0)
    return _pcall(
        body,
        name="norm1_mod_bwd",
        out_shape=[jax.ShapeDtypeStruct((t, d), F32), jax.ShapeDtypeStruct((8, d), F32)],
        grid=((tc + t) // rb,),
        in_specs=[
            pl.BlockSpec((rb, d), cix),
            pl.BlockSpec((rb, d), lat),
            pl.BlockSpec((1, d), lambda i: (0, 0)),
            pl.BlockSpec((8, d), lambda i: (0, 0)),
            pl.BlockSpec((rb, d), cix),
            pl.BlockSpec((rb, d), lat),
            pl.BlockSpec((rb, d), lat),
        ],
        out_specs=[pl.BlockSpec((rb, d), lat), pl.BlockSpec((8, d), lambda i: (0, 0))],
        compiler_params=_cparams(("arbitrary",)),
    )(ctx, x, gain, mods, dz_ctx, dz_lat, dx1)


def _key_prep_fwd(kv, kv_gain, kb_gain, tabs):
    ta, wkv = kv.shape
    kvl = MLA_KV_LORA
    nb = GQA_KV_HEADS * GQA_HEAD_DIM
    rb = ROW_BLOCK if ta % ROW_BLOCK == 0 else LANE
    hd = GQA_HEAD_DIM

    def body(kv_ref, g_ref, gb_ref, ca, s1a, s2a, cb, s1b, s2b, kin_ref, kb_ref, vb_ref):
        xh, _ = _rms(kv_ref[:, 0:kvl])
        kin_ref[:, 0:kvl] = (xh * g_ref[...]).astype(BF16)
        kpe = kv_ref[:, kvl + 2 * nb:kvl + 2 * nb + LANE]
        kin_ref[:, kvl:kvl + LANE] = _rope(kpe, ca[...], s1a[...], s2a[...], MLA_ROPE // 4).astype(BF16)
        for h in range(GQA_KV_HEADS):
            nh, _ = _rms(kv_ref[:, kvl + h * hd:kvl + (h + 1) * hd])
            kb_ref[:, h * hd:(h + 1) * hd] = _rope(nh * gb_ref[...], cb[...], s1b[...], s2b[...], hd // 4).astype(BF16)
        vb_ref[...] = kv_ref[:, kvl + nb:kvl + 2 * nb].astype(BF16)

    row = lambda w: pl.BlockSpec((rb, w), lambda i: (i, 0))
    fix = lambda w: pl.BlockSpec((1, w), lambda i: (0, 0))
    return _pcall(
        body,
        name="key_prep_fwd",
        out_shape=[jax.ShapeDtypeStruct((ta, kvl + LANE), BF16), jax.ShapeDtypeStruct((ta, nb), BF16),
                   jax.ShapeDtypeStruct((ta, nb), BF16)],
        grid=(ta // rb,),
        in_specs=[row(wkv), fix(kvl), fix(hd)] + [row(LANE)] * 3 + [row(hd)] * 3,
        out_specs=[row(kvl + LANE), row(nb), row(nb)],
        compiler_params=_cparams(("parallel",)),
    )(kv, kv_gain, kb_gain, *tabs)


def _key_prep_bwd(kv, kv_gain, kb_gain, tabs, dkin, dkb, dvb):
    ta, wkv = kv.shape
    kvl = MLA_KV_LORA
    nb = GQA_KV_HEADS * GQA_HEAD_DIM
    rb = ROW_BLOCK if ta % ROW_BLOCK == 0 else LANE
    hd = GQA_HEAD_DIM

    def body(kv_ref, g_ref, gb_ref, ca, s1a, s2a, cb, s1b, s2b, dkin_ref, dkb_ref, dvb_ref, dkv_ref, st_ref, stb_ref):
        @pl.when(pl.program_id(0) == 0)
        def _():
            st_ref[...] = jnp.zeros_like(st_ref)
            stb_ref[...] = jnp.zeros_like(stb_ref)

        xh, r = _rms(kv_ref[:, 0:kvl])
        dn = dkin_ref[:, 0:kvl]
        st_ref[0:1, :] += _colsum(dn * xh)
        dkv_ref[:, 0:kvl] = _rms_bwd(dn * g_ref[...], xh, r).astype(BF16)
        dpe = _rope_t(dkin_ref[:, kvl:kvl + LANE], ca[...], s1a[...], s2a[...], MLA_ROPE // 4)
        dkv_ref[:, kvl + 2 * nb:kvl + 2 * nb + LANE] = dpe.astype(BF16)
        for h in range(GQA_KV_HEADS):
            nh, rh = _rms(kv_ref[:, kvl + h * hd:kvl + (h + 1) * hd])
            dn_h = _rope_t(dkb_ref[:, h * hd:(h + 1) * hd], cb[...], s1b[...], s2b[...], hd // 4)
            stb_ref[0:1, :] += _colsum(dn_h * nh)
            dkv_ref[:, kvl + h * hd:kvl + (h + 1) * hd] = _rms_bwd(dn_h * gb_ref[...], nh, rh).astype(BF16)
        dkv_ref[:, kvl + nb:kvl + 2 * nb] = dvb_ref[...].astype(BF16)

    row = lambda w: pl.BlockSpec((rb, w), lambda i: (i, 0))
    fix = lambda w: pl.BlockSpec((1, w), lambda i: (0, 0))
    return _pcall(
        body,
        name="key_prep_bwd",
        out_shape=[jax.ShapeDtypeStruct((ta, wkv), BF16), jax.ShapeDtypeStruct((8, kvl), F32),
                   jax.ShapeDtypeStruct((8, hd), F32)],
        grid=(ta // rb,),
        in_specs=[row(wkv), fix(kvl), fix(hd)] + [row(LANE)] * 3 + [row(hd)] * 3 + [row(kvl + LANE), row(nb), row(nb)],
        out_specs=[row(wkv), pl.BlockSpec((8, kvl), lambda i: (0, 0)), pl.BlockSpec((8, hd), lambda i: (0, 0))],
        compiler_params=_cparams(("arbitrary",)),
    )(kv, kv_gain, kb_gain, *tabs, dkin, dkb, dvb)


def _q_prep_fwd(qg, q_gain, qb_gain, tabs, qscale):
    t = qg.shape[0]
    ql = MLA_Q_LORA
    hd = GQA_HEAD_DIM
    hb = GQA_HEADS * hd
    rb = min(ROW_BLOCK, t)

    def body(q_ref, g_ref, gb_ref, cb, s1b, s2b, cqn_ref, qb_ref):
        xh, _ = _rms(q_ref[:, 0:ql])
        cqn_ref[...] = (xh * g_ref[...]).astype(BF16)
        for h in range(GQA_HEADS):
            nh, _ = _rms(q_ref[:, ql + h * hd:ql + (h + 1) * hd])
            qh = _rope(nh * gb_ref[...], cb[...], s1b[...], s2b[...], hd // 4)
            qb_ref[:, h * hd:(h + 1) * hd] = (qh * qscale).astype(BF16)

    row = lambda w: pl.BlockSpec((rb, w), lambda i: (i, 0))
    fix = lambda w: pl.BlockSpec((1, w), lambda i: (0, 0))
    return _pcall(
        body,
        name="q_prep_fwd",
        out_shape=[jax.ShapeDtypeStruct((t, ql), BF16), jax.ShapeDtypeStruct((t, hb), BF16)],
        grid=(t // rb,),
        in_specs=[row(ql + hb), fix(ql), fix(hd)] + [row(hd)] * 3,
        out_specs=[row(ql), row(hb)],
        compiler_params=_cparams(("parallel",)),
    )(qg, q_gain, qb_gain, *tabs)


def _q_prep_bwd(qg, q_gain, qb_gain, tabs, dcqn, dqb, wpad, qscale):
    t = qg.shape[0]
    ql = MLA_Q_LORA
    hd = GQA_HEAD_DIM
    hb = GQA_HEADS * hd
    rb = min(ROW_BLOCK, t)

    def body(q_ref, g_ref, gb_ref, cb, s1b, s2b, dcqn_ref, dqb_ref, dq_ref, st_ref, stb_ref):
        @pl.when(pl.program_id(0) == 0)
        def _():
            st_ref[...] = jnp.zeros_like(st_ref)
            stb_ref[...] = jnp.zeros_like(stb_ref)

        xh, r = _rms(q_ref[:, 0:ql])
        dn = dcqn_ref[...]
        st_ref[0:1, :] += _colsum(dn * xh)
        dq_ref[:, 0:ql] = _rms_bwd(dn * g_ref[...], xh, r).astype(BF16)
        for h in range(GQA_HEADS):
            nh, rh = _rms(q_ref[:, ql + h * hd:ql + (h + 1) * hd])
            dn_h = _rope_t(dqb_ref[:, h * hd:(h + 1) * hd] * qscale, cb[...], s1b[...], s2b[...], hd // 4)
            stb_ref[0:1, :] += _colsum(dn_h * nh)
            dq_ref[:, ql + h * hd:ql + (h + 1) * hd] = _rms_bwd(dn_h * gb_ref[...], nh, rh).astype(BF16)
        if wpad:
            dq_ref[:, ql + hb:ql + hb + wpad] = jnp.zeros((rb, wpad), BF16)

    row = lambda w: pl.BlockSpec((rb, w), lambda i: (i, 0))
    fix = lambda w: pl.BlockSpec((1, w), lambda i: (0, 0))
    return _pcall(
        body,
        name="q_prep_bwd",
        out_shape=[jax.ShapeDtypeStruct((t, ql + hb + wpad), BF16), jax.ShapeDtypeStruct((8, ql), F32),
                   jax.ShapeDtypeStruct((8, hd), F32)],
        grid=(t // rb,),
        in_specs=[row(ql + hb), fix(ql), fix(hd)] + [row(hd)] * 3 + [row(ql), row(hb)],
        out_specs=[row(ql + hb + wpad), pl.BlockSpec((8, ql), lambda i: (0, 0)), pl.BlockSpec((8, hd), lambda i: (0, 0))],
        compiler_params=_cparams(("arbitrary",)),
    )(qg, q_gain, qb_gain, *tabs, dcqn, dqb)


def _rope_a(v, tabs, transpose, out_dtype, name, qscale):
    t, w = v.shape
    rb = min(ROW_BLOCK, t)
    fn = _rope_t if transpose else _rope

    def body(v_ref, c, s1, s2, o_ref):
        for h in range(w // MLA_SLOT):
            sl = slice(h * MLA_SLOT, (h + 1) * MLA_SLOT)
            o_ref[:, sl] = (fn(v_ref[:, sl].astype(F32), c[...], s1[...], s2[...], MLA_ROPE // 4) * qscale).astype(out_dtype)

    row = lambda ww: pl.BlockSpec((rb, ww), lambda i: (i, 0))
    return _pcall(
        body,
        name=name,
        out_shape=jax.ShapeDtypeStruct((t, w), out_dtype),
        grid=(t // rb,),
        in_specs=[row(w)] + [row(MLA_SLOT)] * 3,
        out_specs=row(w),
        compiler_params=_cparams(("parallel",)),
    )(v, *tabs)


def _merge_fwd(pa, pb, qg, gate_blk):
    t, d = pa.shape
    rb = min(ROW_BLOCK, t)

    def body(pa_ref, pb_ref, ga_ref, gb_ref, o_ref):
        o_ref[...] = (jax.nn.sigmoid(ga_ref[...]) * pa_ref[...].astype(F32)
                      + jax.nn.sigmoid(gb_ref[...]) * pb_ref[...].astype(F32)).astype(BF16)

    row = pl.BlockSpec((rb, d), lambda i: (i, 0))
    return _pcall(
        body,
        name="merge_fwd",
        out_shape=jax.ShapeDtypeStruct((t, d), BF16),
        grid=(t // rb,),
        in_specs=[row, row, pl.BlockSpec((rb, d), lambda i: (i, gate_blk)), pl.BlockSpec((rb, d), lambda i: (i, gate_blk + 1))],
        out_specs=row,
        compiler_params=_cparams(("parallel",)),
    )(pa, pb, qg, qg)


def _merge_bwd(dm, pa, pb, qg, gate_blk):
    t, d = pa.shape
    rb = min(ROW_BLOCK, t)

    def body(dm_ref, pa_ref, pb_ref, ga_ref, gb_ref, dpa_ref, dpb_ref, dg_ref):
        dmv = dm_ref[...].astype(F32)
        sa = jax.nn.sigmoid(ga_ref[...])
        sb = jax.nn.sigmoid(gb_ref[...])
        dpa_ref[...] = (dmv * sa).astype(BF16)
        dpb_ref[...] = (dmv * sb).astype(BF16)
        dg_ref[:, 0:d] = (dmv * pa_ref[...].astype(F32) * (sa * (1.0 - sa))).astype(BF16)
        dg_ref[:, d:2 * d] = (dmv * pb_ref[...].astype(F32) * (sb * (1.0 - sb))).astype(BF16)

    row = pl.BlockSpec((rb, d), lambda i: (i, 0))
    return _pcall(
        body,
        name="merge_bwd",
        out_shape=[jax.ShapeDtypeStruct((t, d), BF16), jax.ShapeDtypeStruct((t, d), BF16),
                   jax.ShapeDtypeStruct((t, 2 * d), BF16)],
        grid=(t // rb,),
        in_specs=[row, row, row, pl.BlockSpec((rb, d), lambda i: (i, gate_blk)), pl.BlockSpec((rb, d), lambda i: (i, gate_blk + 1))],
        out_specs=[row, row, pl.BlockSpec((rb, 2 * d), lambda i: (i, 0))],
        compiler_params=_cparams(("parallel",)),
    )(dm, pa, pb, qg, qg)


def _resid_norm_mod(x, branch, gain, mods, name):
    t, d = x.shape
    rb = min(ROW_BLOCK, t)

    def body(x_ref, b_ref, g_ref, mod_ref, x1_ref, z_ref):
        x1 = x_ref[...] + mod_ref[0:1, :] * b_ref[...]
        x1_ref[...] = x1
        xh, _ = _rms(x1)
        z_ref[...] = ((xh * g_ref[...]) * (1.0 + mod_ref[2:3, :]) + mod_ref[1:2, :]).astype(BF16)

    row = pl.BlockSpec((rb, d), lambda i: (i, 0))
    return _pcall(
        body,
        name=name,
        out_shape=[jax.ShapeDtypeStruct((t, d), F32), jax.ShapeDtypeStruct((t, d), BF16)],
        grid=(t // rb,),
        in_specs=[row, row, pl.BlockSpec((1, d), lambda i: (0, 0)), pl.BlockSpec((8, d), lambda i: (0, 0))],
        out_specs=[row, row],
        compiler_params=_cparams(("parallel",)),
    )(x, branch, gain, mods)


def _norm2_bwd(x1, attn, gain, mods, dz2, dx2):
    t, d = x1.shape
    rb = min(ROW_BLOCK, t)

    ch = 16

    def body(x1_ref, at_ref, g_ref, mod_ref, dz_ref, dx2_ref, dx1_ref, da_ref, st_ref, acc_ref):
        @pl.when(pl.program_id(0) == 0)
        def _():
            st_ref[...] = jnp.zeros_like(st_ref)

        acc_ref[...] = jnp.zeros_like(acc_ref)
        g = g_ref[...]
        sc = 1.0 + mod_ref[1:2, :]
        gate = mod_ref[0:1, :]

        def chunk(i, carry):
            rows = pl.ds(pl.multiple_of(i * ch, ch), ch)
            xh, r = _rms(x1_ref[rows, :])
            dz = dz_ref[rows, :].astype(F32)
            dxn = dz * sc
            dx1 = dx2_ref[rows, :] + _rms_bwd(dxn * g, xh, r)
            dx1_ref[rows, :] = dx1
            da_ref[rows, :] = (dx1 * gate).astype(BF16)
            acc_ref[0] += dz
            acc_ref[1] += dz * (xh * g)
            acc_ref[2] += dxn * xh
            acc_ref[3] += dx1 * at_ref[rows, :]
            return carry

        lax.fori_loop(0, rb // ch, chunk, 0)
        for k in range(4):
            st_ref[k:k + 1, :] += _colsum(acc_ref[k])

    row = pl.BlockSpec((rb, d), lambda i: (i, 0))
    return _pcall(
        body,
        name="norm2_mod_bwd",
        out_shape=[jax.ShapeDtypeStruct((t, d), F32), jax.ShapeDtypeStruct((t, d), BF16), jax.ShapeDtypeStruct((8, d), F32)],
        grid=(t // rb,),
        in_specs=[row, row, pl.BlockSpec((1, d), lambda i: (0, 0)), pl.BlockSpec((8, d), lambda i: (0, 0)), row, row],
        out_specs=[row, row, pl.BlockSpec((8, d), lambda i: (0, 0))],
        scratch_shapes=[pltpu.VMEM((4, ch, d), F32)],
        compiler_params=_cparams(("arbitrary",)),
    )(x1, attn, gain, mods, dz2, dx2)


def _final_loss(x1, ffn, gain, mods, target):
    t, d = x1.shape
    rb = min(ROW_BLOCK, t)
    nb = t // rb

    def body(x1_ref, f_ref, g_ref, mod_ref, tg_ref, dx2_ref, df_ref, st_ref):
        i = pl.program_id(0)

        @pl.when(i == 0)
        def _():
            st_ref[...] = jnp.zeros_like(st_ref)

        ffn_v = f_ref[...]
        g2 = mod_ref[0:1, :]
        x2 = x1_ref[...] + g2 * ffn_v
        xh, r = _rms(x2)
        g = g_ref[...]
        err = xh * g - tg_ref[...]
        st_ref[2:3, :] += _colsum(err * err) * (0.5 / d)
        dy = err * (1.0 / d)
        st_ref[0:1, :] += _colsum(dy * xh)
        dx2 = _rms_bwd(dy * g, xh, r)
        dx2_ref[...] = dx2
        st_ref[1:2, :] += _colsum(dx2 * ffn_v)
        df_ref[...] = (dx2 * g2).astype(BF16)

        @pl.when(i == nb - 1)
        def _():
            st_ref[3:4, :] = jnp.broadcast_to(jnp.sum(st_ref[2:3, :], axis=-1, keepdims=True), (1, d))

    row = pl.BlockSpec((rb, d), lambda i: (i, 0))
    return _pcall(
        body,
        name="final_norm_loss",
        out_shape=[jax.ShapeDtypeStruct((t, d), F32), jax.ShapeDtypeStruct((t, d), BF16), jax.ShapeDtypeStruct((8, d), F32)],
        grid=(nb,),
        in_specs=[row, row, pl.BlockSpec((1, d), lambda i: (0, 0)), pl.BlockSpec((8, d), lambda i: (0, 0)), row],
        out_specs=[row, row, pl.BlockSpec((8, d), lambda i: (0, 0))],
        compiler_params=_cparams(("arbitrary",)),
    )(x1, ffn, gain, mods, target)


def _row_ends(shape):
    rows = lax.broadcasted_iota(jnp.int32, shape, 0)
    return rows == 0, rows == shape[0] - 1


def _shift_dn(v, first):
    return jnp.where(first, 0.0, pltpu.roll(v, 1, 0))


def _shift_up(v, last):
    return jnp.where(last, 0.0, pltpu.roll(v, v.shape[0] - 1, 0))


def _conv_fwd(u, cw, cb):
    t, f2 = u.shape
    f = f2 // 2
    cbk = _tile(f, 256)
    nf = f // cbk

    def body(ua_ref, ub_ref, cwa_ref, cwb_ref, cba_ref, cbb_ref, h_ref, uc_ref):
        first, last = _row_ends((t, cbk))
        outs = []
        for u_ref, cw_ref, cb_ref in ((ua_ref, cwa_ref, cba_ref), (ub_ref, cwb_ref, cbb_ref)):
            uu, cwv = u_ref[...].astype(F32), cw_ref[...]
            outs.append(cb_ref[...] + cwv[0:1, :] * _shift_dn(uu, first) + cwv[1:2, :] * uu
                        + cwv[2:3, :] * _shift_up(uu, last))
        a, b = outs
        uc_ref[0] = a.astype(BF16)
        uc_ref[1] = b.astype(BF16)
        h_ref[...] = (a * jax.nn.sigmoid(a) * b).astype(BF16)

    ca = lambda r: pl.BlockSpec((r, cbk), lambda j: (0, j))
    cbs = lambda r: pl.BlockSpec((r, cbk), lambda j: (0, nf + j))
    return _pcall(
        body,
        name="conv_gate_fwd",
        out_shape=[jax.ShapeDtypeStruct((t, f), BF16), jax.ShapeDtypeStruct((2, t, f), BF16)],
        grid=(nf,),
        in_specs=[ca(t), cbs(t), ca(3), cbs(3), ca(1), cbs(1)],
        out_specs=[ca(t), pl.BlockSpec((2, t, cbk), lambda j: (0, 0, j))],
        compiler_params=_cparams(("parallel",)),
    )(u, u, cw, cw, cb, cb)


def _conv_bwd(u, uc, cw, dh):
    t, f2 = u.shape
    f = f2 // 2
    cbk = _tile(f, 256)
    nf = f // cbk

    def body(ua_ref, ub_ref, uc_ref, cwa_ref, cwb_ref, dh_ref, du_ref, dcw_ref, dcb_ref):
        first, last = _row_ends((t, cbk))
        a, b = uc_ref[0].astype(F32), uc_ref[1].astype(F32)
        dh_v = dh_ref[...].astype(F32)
        sg = jax.nn.sigmoid(a)
        db = dh_v * (a * sg)
        da = dh_v * b * (sg * (1.0 + a * (1.0 - sg)))
        for idx, (dv, u_ref, cw_ref) in enumerate(((da, ua_ref, cwa_ref), (db, ub_ref, cwb_ref))):
            uu, cwv = u_ref[...].astype(F32), cw_ref[...]
            up, dn = _shift_up(dv, last), _shift_dn(dv, first)
            dcb_ref[idx] = _colsum(dv)
            dcw_ref[idx, 0:1, :] = _colsum(up * uu)
            dcw_ref[idx, 1:2, :] = _colsum(dv * uu)
            dcw_ref[idx, 2:3, :] = _colsum(dn * uu)
            du_ref[idx] = (cwv[0:1, :] * up + cwv[1:2, :] * dv + cwv[2:3, :] * dn).astype(BF16)

    ca = lambda r: pl.BlockSpec((r, cbk), lambda j: (0, j))
    cbs = lambda r: pl.BlockSpec((r, cbk), lambda j: (0, nf + j))
    o3 = lambda r: pl.BlockSpec((2, r, cbk), lambda j: (0, 0, j))
    return _pcall(
        body,
        name="conv_gate_bwd",
        out_shape=[jax.ShapeDtypeStruct((2, t, f), BF16), jax.ShapeDtypeStruct((2, 3, f), F32),
                   jax.ShapeDtypeStruct((2, 1, f), F32)],
        grid=(nf,),
        in_specs=[ca(t), cbs(t), o3(t), ca(3), cbs(3), ca(t)],
        out_specs=[o3(t), o3(3), o3(1)],
        compiler_params=_cparams(("parallel",)),
    )(u, u, uc, cw, cw, dh)


def _attention_fwd(q, kk, vv, *, hq, hkv, dk, dv, k_blk0, v_blk0, name):
    t = q.shape[0]
    tk = kk.shape[0]
    g_sz = hq // hkv
    tq = min(ATT_Q_BLOCK_FWD, t)

    def body(q_ref, k_ref, v_ref, o_ref, lse_ref):
        k = k_ref[...]
        v = v_ref[...]
        for j in range(g_sz):
            s = lax.dot_general(q_ref[:, j * dk:(j + 1) * dk], k, _DIMS["nt"], preferred_element_type=F32)
            m = jnp.max(s, axis=-1, keepdims=True)
            p = jnp.exp2(s - m)
            l = jnp.sum(p, axis=-1, keepdims=True)
            o = jnp.dot(p.astype(BF16), v, preferred_element_type=F32) / l
            o_ref[:, j * dv:(j + 1) * dv] = o.astype(BF16)
            lse_ref[0, :, j:j + 1] = m + jnp.log2(l)

    return _pcall(
        body,
        name=name,
        out_shape=[jax.ShapeDtypeStruct((t, hq * dv), BF16), jax.ShapeDtypeStruct((hkv, t, g_sz), F32)],
        grid=(hkv, t // tq),
        in_specs=[
            pl.BlockSpec((tq, g_sz * dk), lambda g, i: (i, g)),
            pl.BlockSpec((tk, dk), lambda g, i: (0, k_blk0 + g)),
            pl.BlockSpec((tk, dv), lambda g, i: (0, v_blk0 + g)),
        ],
        out_specs=[
            pl.BlockSpec((tq, g_sz * dv), lambda g, i: (i, g)),
            pl.BlockSpec((1, tq, g_sz), lambda g, i: (g, i, 0)),
        ],
        compiler_params=_cparams(("parallel", "parallel")),
    )(q, kk, vv)


def _attention_bwd(q, kk, vv, do, lse, *, hq, hkv, dk, dv, k_blk0, v_blk0, name):
    t = q.shape[0]
    tk = kk.shape[0]
    g_sz = hq // hkv
    tq = min(ATT_Q_BLOCK, t)

    def body(q_ref, k_ref, v_ref, do_ref, lse_ref, dq_ref, dk_ref, dv_ref):
        @pl.when(pl.program_id(1) == 0)
        def _():
            dk_ref[...] = jnp.zeros_like(dk_ref)
            dv_ref[...] = jnp.zeros_like(dv_ref)

        k = k_ref[...]
        v = v_ref[...]
        dk_acc = dv_acc = None
        for j in range(g_sz):
            qj = q_ref[:, j * dk:(j + 1) * dk]
            doj = do_ref[:, j * dv:(j + 1) * dv]
            s = lax.dot_general(qj, k, _DIMS["nt"], preferred_element_type=F32)
            p = jnp.exp2(s - lse_ref[0, :, j:j + 1])
            dp = lax.dot_general(doj, v, _DIMS["nt"], preferred_element_type=F32)
            ds = (p * (dp - jnp.sum(p * dp, axis=-1, keepdims=True))).astype(BF16)
            dv_j = lax.dot_general(p.astype(BF16), doj, _DIMS["tn"], preferred_element_type=F32)
            dk_j = lax.dot_general(ds, qj, _DIMS["tn"], preferred_element_type=F32)
            dv_acc = dv_j if dv_acc is None else dv_acc + dv_j
            dk_acc = dk_j if dk_acc is None else dk_acc + dk_j
            dq_ref[:, j * dk:(j + 1) * dk] = jnp.dot(ds, k, preferred_element_type=F32)
        dv_ref[...] += dv_acc
        dk_ref[...] += dk_acc

        @pl.when(pl.program_id(1) == t // tq - 1)
        def _():
            dk_ref[...] *= LN2

    return _pcall(
        body,
        name=name,
        out_shape=[jax.ShapeDtypeStruct((t, hq * dk), F32), jax.ShapeDtypeStruct((tk, hkv * dk), F32),
                   jax.ShapeDtypeStruct((tk, hkv * dv), F32)],
        grid=(hkv, t // tq),
        in_specs=[
            pl.BlockSpec((tq, g_sz * dk), lambda g, i: (i, g)),
            pl.BlockSpec((tk, dk), lambda g, i: (0, k_blk0 + g)),
            pl.BlockSpec((tk, dv), lambda g, i: (0, v_blk0 + g)),
            pl.BlockSpec((tq, g_sz * dv), lambda g, i: (i, g)),
            pl.BlockSpec((1, tq, g_sz), lambda g, i: (g, i, 0)),
        ],
        out_specs=[
            pl.BlockSpec((tq, g_sz * dk), lambda g, i: (i, g)),
            pl.BlockSpec((tk, dk), lambda g, i: (0, g)),
            pl.BlockSpec((tk, dv), lambda g, i: (0, g)),
        ],
        compiler_params=_cparams(("parallel", "arbitrary")),
    )(q, kk, vv, do, lse)


def _silu(v):
    return v * jax.nn.sigmoid(v)


def _ada_fwd(conds, w_ada, b_ada_shard):
    r, d = conds.shape
    n = w_ada.shape[1]
    tn = _tile(n, 512)

    def body(c_ref, w_ref, b_ref, o_ref):
        s = _silu(c_ref[...]).astype(BF16)
        o_ref[...] = jnp.dot(s, w_ref[...].astype(BF16), preferred_element_type=F32) + b_ref[...]

    return _pcall(
        body,
        name="ada_fwd",
        out_shape=jax.ShapeDtypeStruct((r, n), F32),
        grid=(n // tn,),
        in_specs=[pl.BlockSpec((r, d), lambda j: (0, 0)), pl.BlockSpec((d, tn), lambda j: (0, j)),
                  pl.BlockSpec((1, tn), lambda j: (0, j))],
        out_specs=pl.BlockSpec((r, tn), lambda j: (0, j)),
        compiler_params=_cparams(("parallel",)),
    )(conds, w_ada, b_ada_shard)


def _cctx_partial(da16_shard, w_ada, c_ctx_row):
    d, n = w_ada.shape
    td = _tile(d, 512)

    def body(g_ref, w_ref, c_ref, o_ref):
        ds = lax.dot_general(g_ref[8:16, :].astype(BF16), w_ref[...].astype(BF16), _DIMS["nt"],
                             preferred_element_type=F32)
        cv = c_ref[...]
        sg = jax.nn.sigmoid(cv)
        o_ref[...] = ds * (sg * (1.0 + cv * (1.0 - sg)))

    return _pcall(
        body,
        name="cctx_partial",
        out_shape=jax.ShapeDtypeStruct((8, d), F32),
        grid=(d // td,),
        in_specs=[pl.BlockSpec((16, n), lambda j: (0, 0)), pl.BlockSpec((td, n), lambda j: (j, 0)),
                  pl.BlockSpec((1, td), lambda j: (0, j))],
        out_specs=pl.BlockSpec((8, td), lambda j: (0, j)),
        compiler_params=_cparams(("parallel",)),
    )(da16_shard, w_ada, c_ctx_row)


def _sum_parts(parts):
    p, _, n = parts.shape

    def body(p_ref, o_ref):
        acc = p_ref[0]
        for s in range(1, p):
            acc = acc + p_ref[s]
        o_ref[...] = acc

    return _pcall(
        body,
        name="sum_parts",
        out_shape=jax.ShapeDtypeStruct((1, n), F32),
        in_specs=[pl.BlockSpec(memory_space=pltpu.VMEM)],
        out_specs=pl.BlockSpec(memory_space=pltpu.VMEM),
    )(parts)


def _adam_math(w, g, m, v):
    m2 = ADAM_B1 * m + (1.0 - ADAM_B1) * g
    v2 = ADAM_B2 * v + (1.0 - ADAM_B2) * jnp.square(g)
    m_hat = m2 / (1.0 - ADAM_B1 ** ADAM_STEP)
    v_hat = v2 / (1.0 - ADAM_B2 ** ADAM_STEP)
    delta = -ADAM_LR * (m_hat / (jnp.sqrt(v_hat) + ADAM_EPS) + ADAM_WD * w)
    return delta, m2, v2


def _adamw(parts, w, m, v, name):
    p, r, c = parts.shape
    block_elems = 1 << 18
    rb, cb = _tile(r, max(8, block_elems // c // 8 * 8), 8), c
    if rb * c < block_elems // 4 and r * c > block_elems:
        rb, cb = r, _tile(c, max(LANE, block_elems // r // LANE * LANE))

    def body(p_ref, w_ref, m_ref, v_ref, g_ref, d_ref, m2_ref, v2_ref):
        g = p_ref[0].astype(F32)
        for s in range(1, p):
            g = g + p_ref[s].astype(F32)
        g_ref[...] = g
        d_ref[...], m2_ref[...], v2_ref[...] = _adam_math(w_ref[...], g, m_ref[...], v_ref[...])

    if w.ndim == 3:
        blk = pl.BlockSpec((None, rb, cb), lambda i, j: (0, i, j))
    else:
        blk = pl.BlockSpec((rb, cb), lambda i, j: (i, j))
    return _pcall(
        body,
        name=name,
        out_shape=[jax.ShapeDtypeStruct(w.shape, F32)] * 4,
        grid=(r // rb, c // cb),
        in_specs=[pl.BlockSpec((p, rb, cb), lambda i, j: (0, i, j)), blk, blk, blk],
        out_specs=[blk] * 4,
        compiler_params=_cparams(("parallel", "parallel")),
    )(parts, w, m, v)


def _adamw_ada(conds, da16, w, m, v):
    d, n = w.shape
    rb = _tile(d, 256, LANE)

    def body(s_ref, da_ref, w_ref, m_ref, v_ref, g_ref, d_ref, m2_ref, v2_ref):
        g = lax.dot_general(_silu(s_ref[...]).astype(BF16), da_ref[...].astype(BF16), _DIMS["tn"],
                            preferred_element_type=F32)
        g_ref[...] = g
        d_ref[...], m2_ref[...], v2_ref[...] = _adam_math(w_ref[...], g, m_ref[...], v_ref[...])

    row = pl.BlockSpec((rb, n), lambda i: (i, 0))
    return _pcall(
        body,
        name="adamw_w_ada",
        out_shape=[jax.ShapeDtypeStruct((d, n), F32)] * 4,
        grid=(d // rb,),
        in_specs=[pl.BlockSpec((16, rb), lambda i: (0, i)), pl.BlockSpec((16, n), lambda i: (0, 0)), row, row, row],
        out_specs=[row] * 4,
        compiler_params=_cparams(("parallel",)),
    )(conds, da16, w, m, v)


def _cast_bf16(a, name):
    _, r, c = a.shape
    rb, cb = _tile(r, 512, 8), c
    if rb < 64 < r:
        rb, cb = r, _tile(c, 512)

    def body(a_ref, o_ref):
        o_ref[...] = a_ref[...].astype(BF16)

    return _pcall(body, name=name, out_shape=jax.ShapeDtypeStruct((r, c), BF16), grid=(r // rb, c // cb),
                  in_specs=[pl.BlockSpec((None, rb, cb), lambda i, j: (0, i, j))],
                  out_specs=pl.BlockSpec((rb, cb), lambda i, j: (i, j)),
                  compiler_params=_cparams(("parallel", "parallel")))(a)


def _rope_tabs(t, rot):
    half, q = rot // 2, rot // 4
    n_rows = t // GRID_W
    row = jnp.repeat(jnp.arange(n_rows, dtype=F32), GRID_W)
    col = jnp.tile(jnp.arange(GRID_W, dtype=F32), n_rows)
    inv_freq = ROPE_THETA ** (-jnp.arange(0, half, 2, dtype=F32) / half)
    ang = jnp.concatenate([row[:, None] * inv_freq, col[:, None] * inv_freq], axis=-1)
    cos, sin = jnp.cos(ang), jnp.sin(ang)
    c0, c1, s0, s1 = cos[:, :q], cos[:, q:], sin[:, :q], sin[:, q:]
    z = jnp.zeros_like(s0)
    return (jnp.concatenate([c0, c0, c1, c1], -1), jnp.concatenate([-s0, z, -s1, z], -1),
            jnp.concatenate([z, s0, z, s1], -1))


def _pad_cols(a, left, total, fill=0.0):
    return jnp.pad(a, ((0, 0), (left, total - left - a.shape[1])), constant_values=fill)


def _with_ctx_rows(tab, tc, fill):
    return jnp.concatenate([tab, jnp.full((tc, tab.shape[1]), fill, F32)], axis=0)


def kernel(x, c, ctx, c_ctx, w_ada, b_ada, norm1_g, w_in, mla_q_norm_g, w_q_up, mla_kv_norm_g, w_kv_up, gqa_q_norm_g, gqa_k_norm_g, w_br_a, w_br_b, w_out, norm2_g, w_up, conv_w, conv_b, w_down, final_norm_g, loss_target, m_c_ctx, m_w_ada, m_b_ada, m_norm1_g, m_w_in, m_mla_q_norm_g, m_w_q_up, m_mla_kv_norm_g, m_w_kv_up, m_gqa_q_norm_g, m_gqa_k_norm_g, m_w_br_a, m_w_br_b, m_w_out, m_norm2_g, m_w_up, m_conv_w, m_conv_b, m_w_down, m_final_norm_g, v_c_ctx, v_w_ada, v_b_ada, v_norm1_g, v_w_in, v_mla_q_norm_g, v_w_q_up, v_mla_kv_norm_g, v_w_kv_up, v_gqa_q_norm_g, v_gqa_k_norm_g, v_w_br_a, v_w_br_b, v_w_out, v_norm2_g, v_w_up, v_conv_w, v_conv_b, v_w_down, v_final_norm_g):
    weights = dict(c_ctx=c_ctx, w_ada=w_ada, b_ada=b_ada, norm1_g=norm1_g, w_in=w_in, mla_q_norm_g=mla_q_norm_g,
                   w_q_up=w_q_up, mla_kv_norm_g=mla_kv_norm_g, w_kv_up=w_kv_up, gqa_q_norm_g=gqa_q_norm_g,
                   gqa_k_norm_g=gqa_k_norm_g, w_br_a=w_br_a, w_br_b=w_br_b, w_out=w_out, norm2_g=norm2_g, w_up=w_up,
                   conv_w=conv_w, conv_b=conv_b, w_down=w_down, final_norm_g=final_norm_g)
    mom_m = dict(c_ctx=m_c_ctx, w_ada=m_w_ada, b_ada=m_b_ada, norm1_g=m_norm1_g, w_in=m_w_in, mla_q_norm_g=m_mla_q_norm_g,
                 w_q_up=m_w_q_up, mla_kv_norm_g=m_mla_kv_norm_g, w_kv_up=m_w_kv_up, gqa_q_norm_g=m_gqa_q_norm_g,
                 gqa_k_norm_g=m_gqa_k_norm_g, w_br_a=m_w_br_a, w_br_b=m_w_br_b, w_out=m_w_out, norm2_g=m_norm2_g,
                 w_up=m_w_up, conv_w=m_conv_w, conv_b=m_conv_b, w_down=m_w_down, final_norm_g=m_final_norm_g)
    mom_v = dict(c_ctx=v_c_ctx, w_ada=v_w_ada, b_ada=v_b_ada, norm1_g=v_norm1_g, w_in=v_w_in, mla_q_norm_g=v_mla_q_norm_g,
                 w_q_up=v_w_q_up, mla_kv_norm_g=v_mla_kv_norm_g, w_kv_up=v_w_kv_up, gqa_q_norm_g=v_gqa_q_norm_g,
                 gqa_k_norm_g=v_gqa_k_norm_g, w_br_a=v_w_br_a, w_br_b=v_w_br_b, w_out=v_w_out, norm2_g=v_norm2_g,
                 w_up=v_w_up, conv_w=v_conv_w, conv_b=v_conv_b, w_down=v_w_down, final_norm_g=v_final_norm_g)
    order = list(weights)

    my_idx = 4 * lax.axis_index("x") + 2 * lax.axis_index("y") + lax.axis_index("c")
    xs, cts, tgt = x[0], ctx[0], loss_target[0]
    t, d = xs.shape
    tc = cts.shape[0]
    ta = t + tc
    kvl, ql = MLA_KV_LORA, MLA_Q_LORA
    nb = GQA_KV_HEADS * GQA_HEAD_DIM
    hb = GQA_HEADS * GQA_HEAD_DIM
    ha = MLA_HEADS
    f2 = w_up.shape[2] * N_DEV
    ff = f2 // 2

    big = ["w_in", "w_q_up", "w_kv_up", "w_br_a", "w_br_b", "w_out", "w_up", "w_down"]
    _ORDER_AFTER.clear()
    narrow = ("w_in", "w_q_up")

    def tview(a):
        return jnp.transpose(a, (0, 2, 1))

    shards = {"w_in": _cast_bf16(tview(weights["w_in"]), "cast_w_in")}
    c_idx = jnp.reshape(lax.axis_index("c"), (1,)).astype(jnp.int32)

    def gather_start(names, dep):
        shs = [shards[n] for n in names]
        land = [lax.empty((N_DEV,) + s.shape, BF16) for s in shs]
        if dep is not None:
            _after(dep)
        s, r, arrs, tok = _split_start("gather_ici_start_" + names[0], shs + land, _gather_ici_copies(len(names)),
                                       4 * len(names))
        return dict(names=names, s=s, r=r, arrs=arrs, tok=tok)

    def gather_pass(g, after):
        n = len(g["names"])
        arrs = _split_wait("gather_ici_wait_" + g["names"][0], g["s"], g["r"], g["arrs"], _gather_ici_copies(n), after)
        s, r, bufs, tok = _split_start("gather_pass_start_" + g["names"][0], arrs[n:], _gather_pass_copies(n), 3 * n)
        g.update(s2=s, r2=r, bufs=bufs)
        return tok

    def gather_relay(g, after):
        n = len(g["names"])
        bufs = _split_wait("gather_pass_wait_" + g["names"][0], g["s2"], g["r2"], g["bufs"], _gather_pass_copies(n), after)
        s, r, bufs, tok = _split_start("gather_d2d_start_" + g["names"][0], bufs, _gather_d2d_copies(n), n)
        g.update(s3=s, r3=r, bufs=bufs)
        return tok

    def gather_finish(g, after):
        n = len(g["names"])
        bufs = _split_wait("gather_d2d_wait_" + g["names"][0], g["s3"], g["r3"], g["bufs"], _gather_d2d_copies(n), after)
        return dict(zip(g["names"], bufs))

    _after(shards["w_in"])
    c_all, cw_all = _all_gather([jnp.pad(c, ((0, 7), (0, 0))), jnp.pad(conv_w[0], ((0, 5), (0, 0)))], "gather_cond")
    conv_w_f = jnp.transpose(cw_all[:, :3, :], (1, 0, 2)).reshape(3, f2)
    conds = jnp.concatenate([c_all[:, 0, :], c_ctx[None, :], jnp.zeros((7, d), F32)], axis=0)
    ncol = w_ada.shape[2]
    b_shard = lax.dynamic_slice_in_dim(b_ada, my_idx * ncol, ncol, axis=1)
    ada_shard = _ada_fwd(conds, w_ada[0], b_shard)
    (ada_all,) = _all_gather([ada_shard], "gather_ada")
    ada = jnp.transpose(ada_all, (1, 0, 2)).reshape(16, N_DEV * ncol)
    lat = lax.dynamic_slice_in_dim(ada, my_idx, 1, axis=0).reshape(6, d)
    cxt = ada[8].reshape(6, d)
    mods1 = jnp.concatenate([lat[0:2], cxt[0:2], jnp.zeros((4, d), F32)], axis=0)
    mods2 = jnp.concatenate([lat[2:3], lat[3:4], lat[4:5], jnp.zeros((5, d), F32)], axis=0)
    mods2b = jnp.concatenate([lat[2:3], lat[4:5], jnp.zeros((6, d), F32)], axis=0)
    mods3 = jnp.concatenate([lat[5:6], jnp.zeros((7, d), F32)], axis=0)

    g0 = gather_start(["w_in"], ada_all)
    for n in big[1:]:
        _after(g0["tok"])
        shards[n] = _cast_bf16(tview(weights[n]) if n in narrow else weights[n], "cast_" + n)

    ca, s1a, s2a = _rope_tabs(t, MLA_ROPE)
    cb_, s1b, s2b = _rope_tabs(t, GQA_HEAD_DIM)
    q_tabs_a = (_pad_cols(jnp.concatenate([jnp.ones((t, MLA_NOPE), F32), ca], 1), 0, MLA_SLOT),
                _pad_cols(s1a, MLA_NOPE, MLA_SLOT), _pad_cols(s2a, MLA_NOPE, MLA_SLOT))
    q_tabs_b = (cb_, s1b, s2b)
    k_tabs = (_with_ctx_rows(_pad_cols(ca, 0, LANE), tc, 1.0), _with_ctx_rows(_pad_cols(s1a, 0, LANE), tc, 0.0),
              _with_ctx_rows(_pad_cols(s2a, 0, LANE), tc, 0.0),
              _with_ctx_rows(cb_, tc, 1.0), _with_ctx_rows(s1b, tc, 0.0), _with_ctx_rows(s2b, tc, 0.0))

    def cols_full(g):
        return jnp.transpose(g, (1, 0, 2)).reshape(g.shape[1], N_DEV * g.shape[2])

    _after(*q_tabs_a, *q_tabs_b, *k_tabs, *[shards[n] for n in big[1:]])
    tok_p0 = gather_pass(g0, mods1)
    g1 = gather_start(["w_q_up", "w_kv_up", "w_br_a", "w_br_b", "w_out"], tok_p0)
    _after(g1["tok"])
    z_all = _norm_mod_fwd(cts, xs, norm1_g, mods1)
    gathered = gather_finish(g0, gather_relay(g0, z_all))
    wt_in = gathered["w_in"].reshape(-1, d)
    o_kpe, o_kb, o_vb = kvl, kvl + MLA_ROPE, kvl + MLA_ROPE + nb
    o_q = o_vb + nb
    o_g = o_q + ql + hb
    wkv_w = kvl + 2 * nb + LANE
    wt_kv_p = jnp.concatenate([wt_in[:kvl], wt_in[o_kb:o_q], wt_in[o_kpe:o_kb],
                               jnp.zeros((LANE - MLA_ROPE, d), BF16)], axis=0)
    q_w = ql + hb
    q_pad = (-q_w) % 512 if d >= 512 else (-q_w) % d
    qw_p = q_w + q_pad
    wt_q_p = jnp.concatenate([wt_in[o_q:o_g], jnp.zeros((q_pad, d), BF16)], axis=0)
    wt_g = wt_in[o_g:]

    kv_all = _mm(z_all, wt_kv_p, "nt", F32, "proj_kv", tm=1152, tn=wkv_w)
    tok_p1 = gather_pass(g1, kv_all)
    g2 = gather_start(["w_up", "w_down"], tok_p1)
    _after(g2["tok"])
    qq = _mm(z_all, wt_q_p, "nt", F32, "proj_q", tm=1024, tn=1024, rows=t)
    _after(g2["tok"])
    gates = _mm(z_all, wt_g, "nt", F32, "proj_gates", tm=1024, tn=1024, rows=t)
    _after(g2["tok"])
    kin, k_b, v_b = _key_prep_fwd(kv_all, mla_kv_norm_g, gqa_k_norm_g, k_tabs)
    sc_a = float((MLA_NOPE + MLA_ROPE) ** -0.5) * LOG2E
    sc_b = float(GQA_HEAD_DIM ** -0.5) * LOG2E
    _after(g2["tok"])
    cqn, q_b = _q_prep_fwd(qq, mla_q_norm_g, gqa_q_norm_g, q_tabs_b, sc_b)
    _after(kin, gates, g2["tok"])
    gathered.update(gather_finish(g1, gather_relay(g1, q_b)))

    wqt_f = gathered["w_q_up"].reshape(ha, MLA_NOPE + MLA_ROPE, ql)
    wqt_ext = jnp.pad(wqt_f, ((0, 0), (0, MLA_SLOT - MLA_NOPE - MLA_ROPE), (0, 0))).reshape(ha * MLA_SLOT, ql)
    wkv_f = cols_full(gathered["w_kv_up"]).reshape(kvl, ha, MLA_NOPE + MLA_V)
    wk_slots = jnp.pad(wkv_f[:, :, :MLA_NOPE], ((0, 0), (0, 0), (0, MLA_SLOT - MLA_NOPE))).reshape(kvl, ha * MLA_SLOT)
    wv_cols = wkv_f[:, :, MLA_NOPE:].reshape(kvl, ha * MLA_V)
    e_slot = jnp.pad(jnp.eye(MLA_ROPE, dtype=BF16),
                     ((0, LANE - MLA_ROPE), (MLA_NOPE, MLA_SLOT - MLA_NOPE - MLA_ROPE)))
    e_rows = jnp.concatenate([jnp.tile(e_slot, (1, ha)), jnp.zeros((LANE, ha * MLA_V), BF16)], axis=1)
    wkv_ext = jnp.concatenate([jnp.concatenate([wk_slots, wv_cols], axis=1), e_rows], axis=0)
    w_bra = cols_full(gathered["w_br_a"])
    w_brb = cols_full(gathered["w_br_b"])
    w_out_f = gathered["w_out"].reshape(d, d)

    kv_a = _mm(kin, wkv_ext, "nn", BF16, "kv_up", tm=1152, tn=1024)
    qa_raw = _mm(cqn, wqt_ext, "nt", F32, "q_up", tm=1024, tn=1024)
    q_a = _rope_a(qa_raw, q_tabs_a, False, BF16, "rope_q_fwd", sc_a)
    att_a = dict(hq=ha, hkv=ha, dk=MLA_SLOT, dv=MLA_V, k_blk0=0, v_blk0=ha * MLA_SLOT // MLA_V)
    att_b = dict(hq=GQA_HEADS, hkv=GQA_KV_HEADS, dk=GQA_HEAD_DIM, dv=GQA_HEAD_DIM, k_blk0=0, v_blk0=0)
    o_a, lse_a = _attention_fwd(q_a, kv_a, kv_a, name="attn_a_fwd", **att_a)
    o_b, lse_b = _attention_fwd(q_b, k_b, v_b, name="attn_b_fwd", **att_b)
    _after(o_a)
    _after(gather_pass(g2, o_b))
    pa = _mm(o_a, w_bra, "nn", BF16, "br_a", tm=1024, tn=1024)
    pb = _mm(o_b, w_brb, "nn", BF16, "br_b", tm=1024, tn=1024)
    merged = _merge_fwd(pa, pb, gates, 0)
    attn = _mm(merged, w_out_f, "nn", F32, "w_out", tm=1024, tn=1024)
    _after(gather_relay(g2, attn))
    x1, z2 = _resid_norm_mod(xs, attn, norm2_g, mods2, "resid_norm2_fwd")
    ffn_w = gather_finish(g2, z2)
    w_up3 = ffn_w["w_up"]
    w_down_f = ffn_w["w_down"].reshape(ff, d)
    u = _mm_up_fwd(z2, w_up3, "w_up")
    h, uc = _conv_fwd(u, conv_w_f, conv_b)
    ffn = _mm(h, w_down_f, "nn", F32, "w_down", tm=1024, tn=1024, tk=2816)

    def to_shards(g):
        return jnp.transpose(g.reshape(g.shape[0], N_DEV, g.shape[1] // N_DEV), (1, 0, 2))

    def reduce_start(tag, names, sends):
        n = len(sends)
        land = [lax.empty((4,) + s.shape[1:], s.dtype) for s in sends]
        s, r, arrs, tok = _split_start("reduce_d2d_start_" + tag, sends + land, _reduce_d2d_copies(n), 4 * n)
        return dict(tag=tag, names=names, s=s, r=r, arrs=arrs, tok=tok)

    def reduce_relay(g, after):
        n = len(g["names"])
        arrs = _split_wait("reduce_d2d_wait_" + g["tag"], g["s"], g["r"], g["arrs"], _reduce_d2d_copies(n), after)
        sums = [_pair_sum(arrs[a], arrs[n + a], c_idx, "pair_sum_" + g["names"][a]) for a in range(n)]
        land = [lax.empty(s.shape, s.dtype) for s in sums]
        s, r, arrs2, tok = _split_start("reduce_ici_start_" + g["tag"], sums + land, _reduce_ici_copies(n), 4 * n)
        g.update(s2=s, r2=r, arrs2=arrs2)
        return tok

    def reduce_finish(g, after):
        n = len(g["names"])
        arrs2 = _split_wait("reduce_ici_wait_" + g["tag"], g["s2"], g["r2"], g["arrs2"], _reduce_ici_copies(n), after)
        return dict(zip(g["names"], arrs2[n:]))

    dx2, dffn, st_fin = _final_loss(x1, ffn, final_norm_g[None, :], mods3, tgt)
    dh = _mm(dffn, w_down_f, "nt", BF16, "d_h", tm=1024, tn=1024)
    g_w_down = _mm(h, dffn, "tn", BF16, "g_w_down", tm=512, tn=1024)
    du3, dcw, dcb = _conv_bwd(u, uc, conv_w_f, dh)
    dz2 = _mm_up_dz(du3, w_up3, "d_z2")
    g_w_up = _mm_up_gw(z2, du3, N_DEV, "g_w_up")
    g_conv_w = jnp.concatenate([dcw[0], dcw[1]], axis=1)
    r_ffn = reduce_start("ffn", ["w_down", "w_up", "conv_w"],
                         [g_w_down.reshape(N_DEV, ff // N_DEV, d), g_w_up,
                          to_shards(jnp.pad(g_conv_w, ((0, 5), (0, 0))))])
    _after(r_ffn["tok"])
    dx1, dattn, st_n2 = _norm2_bwd(x1, attn, norm2_g, mods2b, dz2, dx2)
    dmerged = _mm(dattn, w_out_f, "nt", BF16, "d_merged", tm=1024, tn=1024)
    g_w_out = _mm(merged, dattn, "tn", BF16, "g_w_out", tm=1024, tn=1024)
    dpa, dpb, dgates = _merge_bwd(dmerged, pa, pb, gates, 0)
    do_a = _mm(dpa, w_bra, "nt", BF16, "d_o_a", tm=1024, tn=1024)
    do_b = _mm(dpb, w_brb, "nt", BF16, "d_o_b", tm=1024, tn=1024)
    g_w_bra = _mm_tn_shards(o_a, dpa, N_DEV, "g_w_br_a")
    g_w_brb = _mm_tn_shards(o_b, dpb, N_DEV, "g_w_br_b")
    _after(reduce_relay(r_ffn, g_w_brb))
    dq_a, dk_a, dv_a = _attention_bwd(q_a, kv_a, kv_a, do_a, lse_a, name="attn_a_bwd", **att_a)
    dq_b, dk_b, dv_b = _attention_bwd(q_b, k_b, v_b, do_b, lse_b, name="attn_b_bwd", **att_b)
    dqa_raw = _rope_a(dq_a, q_tabs_a, True, BF16, "rope_q_bwd", sc_a * LN2)
    dcqn = _mm(dqa_raw, wqt_ext, "nn", F32, "d_cqn", tm=1024, tn=ql)
    g_wqt_ext = _mm(dqa_raw, cqn, "tn", BF16, "g_w_q_up", tm=1024, tn=ql)
    dq_p, st_q, st_qb = _q_prep_bwd(qq, mla_q_norm_g, gqa_q_norm_g, q_tabs_b, dcqn, dq_b, q_pad, sc_b * LN2)
    dkin = _mm_cat_nt([(dk_a, wkv_ext, 0), (dv_a, wkv_ext, ha * MLA_SLOT)], F32, "d_kin", tm=1152, tn=kvl + LANE)
    g_wkv_ext = _mm_cat_tn(kin, [dk_a, dv_a], BF16, "g_w_kv_up", tm=kvl + LANE, tn=min(1024, ha * MLA_V))
    dkv_p, st_kv, st_kb = _key_prep_bwd(kv_all, mla_kv_norm_g, gqa_k_norm_g, k_tabs, dkin, dk_b, dv_b)
    g_wqt = g_wqt_ext.reshape(ha, MLA_SLOT, ql)[:, :MLA_NOPE + MLA_ROPE, :].reshape(N_DEV, -1, ql)
    g_wkv = jnp.concatenate([g_wkv_ext[:kvl, :ha * MLA_SLOT].reshape(kvl, ha, MLA_SLOT)[:, :, :MLA_NOPE],
                             g_wkv_ext[:kvl, ha * MLA_SLOT:].reshape(kvl, ha, MLA_V)], axis=2).reshape(kvl, ha * (MLA_NOPE + MLA_V))
    r_mid = reduce_start("mid", ["w_out", "w_br_a", "w_br_b", "w_q_up", "w_kv_up"],
                         [g_w_out.reshape(N_DEV, d // N_DEV, d), g_w_bra, g_w_brb, g_wqt,
                          to_shards(g_wkv)])
    _after(r_mid["tok"])
    g_wkv_p = _mm(dkv_p, z_all, "tn", BF16, "g_w_in_kv", tm=wkv_w, tn=1024)
    g_wqg_p = _mm_rows_tn([dq_p, dgates], z_all, BF16, "g_w_in_qg", tm=min(1024, d), tn=1024, rows=t)
    tok_m = reduce_relay(r_mid, g_wqg_p)
    g_wt_in = jnp.concatenate([g_wkv_p[:kvl], g_wkv_p[kvl + 2 * nb:kvl + 2 * nb + MLA_ROPE],
                               g_wkv_p[kvl:kvl + 2 * nb], g_wqg_p[:q_w], g_wqg_p[q_w + q_pad:]], axis=0)
    _after(tok_m)
    r_in = reduce_start("in", ["w_in"], [g_wt_in.reshape(N_DEV, -1, d)])
    _after(r_in["tok"])
    dz_lat = _mm_sum_nn([(dq_p, 0, wt_q_p, 0, qw_p), (dgates, 0, wt_g, 0, d), (dgates, d, wt_g, d, d),
                         (dkv_p, 0, wt_kv_p, 0, wkv_w)], F32, "d_z_lat", rows=t)
    dz_ctx = _mm(dkv_p, wt_kv_p, "nn", F32, "d_z_ctx", tm=min(ROW_BLOCK, tc), tn=1024, a_row_off=t)
    grad_x, st_n1 = _norm1_bwd(cts, xs, norm1_g, mods1, dz_ctx, dz_lat, dx1)

    res = {}

    def upd(nm, parts):
        wv, mv, vv = weights[nm], mom_m[nm], mom_v[nm]
        if wv.ndim == 1:
            wv, mv, vv = (a.reshape(1, -1) for a in (wv, mv, vv))
        if nm in narrow:
            wv, mv, vv = tview(wv), tview(mv), tview(vv)
        outs = _adamw(parts, wv, mv, vv, "adamw_" + nm)
        if nm in narrow:
            outs = [tview(o_) for o_ in outs]
        res[nm] = [o_.reshape(weights[nm].shape) for o_ in outs]

    d_lat = jnp.concatenate([st_n1[0], st_n1[1], st_n2[3], st_n2[0], st_n2[1], st_fin[1]])
    d_cxt = jnp.concatenate([st_n1[3], st_n1[4], jnp.zeros((4 * d,), F32)])
    small = jnp.concatenate([d_lat, d_cxt, st_n1[2], st_q[0], st_kv[0], st_qb[0], st_kb[0], st_n2[2],
                             jnp.concatenate([dcb[0, 0], dcb[1, 0]]), st_fin[0], st_fin[3, :LANE]])
    n_small = small.shape[0]
    pad_small = (-n_small) % LANE
    (small_all,) = _all_gather([jnp.pad(small, (0, pad_small)).reshape(1, -1)], "gather_small")
    offs = {}
    o = 0
    for nm, ln in (("d_lat", 6 * d), ("d_cxt", 6 * d), ("norm1_g", d), ("mla_q_norm_g", ql), ("mla_kv_norm_g", kvl),
                   ("gqa_q_norm_g", GQA_HEAD_DIM), ("gqa_k_norm_g", GQA_HEAD_DIM), ("norm2_g", d), ("conv_b", f2),
                   ("final_norm_g", d), ("loss", LANE)):
        offs[nm] = (o, ln)
        o += ln

    def part(nm):
        a, ln = offs[nm]
        return small_all[:, :, a:a + ln]

    loss = _sum_parts(part("loss"))[0, 0]
    d_lat_all = part("d_lat")[:, 0, :]
    d_cxt_sum = _sum_parts(part("d_cxt"))
    da16 = jnp.concatenate([d_lat_all, d_cxt_sum, jnp.zeros((7, 6 * d), F32)], axis=0)
    da16_shard = lax.dynamic_slice_in_dim(da16, my_idx * ncol, ncol, axis=1)
    cc_part = _cctx_partial(da16_shard, w_ada[0], c_ctx[None, :])
    (cc_all,) = _all_gather([cc_part], "gather_cctx")
    cc_parts = cc_all[:, 0:1, :]
    tok_i = reduce_relay(r_in, cc_all)

    _after(tok_i)
    for nm in ("norm1_g", "mla_q_norm_g", "mla_kv_norm_g", "gqa_q_norm_g", "gqa_k_norm_g", "norm2_g", "conv_b",
               "final_norm_g"):
        upd(nm, part(nm))
    upd("c_ctx", cc_parts)
    b_parts = jnp.concatenate([d_lat_all[:, None, :], d_cxt_sum[None]], axis=0)
    upd("b_ada", b_parts)
    _after(tok_i)
    outs = _adamw_ada(conds, da16_shard, w_ada[0], m_w_ada[0], v_w_ada[0])
    res["w_ada"] = [o_[None] for o_ in outs]
    last = outs[0]
    done = [last]
    for grp in (r_ffn, r_mid, r_in):
        _after(*done)
        recv = reduce_finish(grp, last)
        for nm in grp["names"]:
            upd(nm, recv[nm][:, :3, :] if nm == "conv_w" else recv[nm])
            last = res[nm][0]
            done.append(last)

    return (loss, grad_x[None], *[res[n][0] for n in order], *[res[n][1] for n in order],
            *[res[n][2] for n in order], *[res[n][3] for n in order])
```
